```python
import jax
import jax.numpy as jnp
from jax import lax
import numpy as np

D_MODEL = 1024
BATCH = 8
SEQ = 2048
DEPTH = 2

N_META = 16
RMS_EPS = 1e-6
LN_EPS = 1e-5
D_A = D_MODEL // 2
D_B = D_MODEL // 2
CONV_A_WIDTH = 31
CONV_B_WIDTH = 3
EVEN_COLS = 2 * D_A + 3 * D_B
HEAD_DIM = 64
N_Q_HEADS = 8
N_KV_HEADS = 2
GQA_GROUP = N_Q_HEADS // N_KV_HEADS
D_ATT = N_Q_HEADS * HEAD_DIM
D_KV = N_KV_HEADS * HEAD_DIM
WINDOW = 128
BLOCK = 128
ROPE_THETA = 10000.0
RWKV_HEAD = 64
D_R = D_MODEL // 2
N_R_HEADS = D_R // RWKV_HEAD
LORA_W = 64
LORA_A = 64
LORA_G = 128
RWKV_GN_EPS = 64e-5
ATT_COLS = D_ATT + 2 * D_KV
RWKV_COLS = 3 * D_R + LORA_W + LORA_A + LORA_G
ODD_COLS = ATT_COLS + RWKV_COLS
D_FF = 2816
FF_CONV_WIDTH = 3
NEG_INF = -1e30

kernel_name = 'hybrid_conv_swa_rwkv7_block'


def rms_norm(x, g):
    xf = x.astype(jnp.float32)
    y = xf * lax.rsqrt(jnp.mean(xf * xf, axis=-1, keepdims=True) + RMS_EPS)
    return (y * g.astype(jnp.float32)).astype(x.dtype)


def layer_norm(x, g, b):
    xf = x.astype(jnp.float32)
    mu = jnp.mean(xf, axis=-1, keepdims=True)
    var = jnp.mean(jnp.square(xf - mu), axis=-1, keepdims=True)
    y = (xf - mu) * lax.rsqrt(var + LN_EPS)
    return (y * g.astype(jnp.float32) + b.astype(jnp.float32)).astype(x.dtype)


def causal_dwconv(x, w):
    k_width, ch = w.shape
    return lax.conv_general_dilated(
        x, w[:, None, :].astype(x.dtype), window_strides=(1,),
        padding=[(k_width - 1, 0)], dimension_numbers=('NWC', 'WIO', 'NWC'),
        feature_group_count=ch)


def rope(x, pos):
    half = x.shape[-1] // 2
    inv = ROPE_THETA ** (-jnp.arange(half, dtype=jnp.float32) / half)
    ang = pos.astype(jnp.float32)[:, None] * inv[None, :]
    cos = jnp.cos(ang)[None, :, None, :]
    sin = jnp.sin(ang)[None, :, None, :]
    xf = x.astype(jnp.float32)
    x1, x2 = xf[..., :half], xf[..., half:]
    return jnp.concatenate([x1 * cos - x2 * sin, x2 * cos + x1 * sin], axis=-1).astype(x.dtype)


def conformer_conv_group(a_val, a_gate, conv_w, ln_g, ln_b):
    u = a_val * jax.nn.sigmoid(a_gate)
    u = causal_dwconv(u, conv_w)
    return jax.nn.silu(layer_norm(u, ln_g, ln_b))


def short_conv_group(gate_b, gate_c, x_in, conv_w):
    return gate_b * causal_dwconv(gate_c * x_in, conv_w)


def even_mixer(h, w_in, conv_a, ln_a_g, ln_a_b, conv_b, w_out):
    p = h @ w_in
    a_val, a_gate, g_b, g_c, x_in = jnp.split(
        p, [D_A, 2 * D_A, 2 * D_A + D_B, 2 * D_A + 2 * D_B], axis=-1)
    y = jnp.concatenate([conformer_conv_group(a_val, a_gate, conv_a, ln_a_g, ln_a_b),
                         short_conv_group(g_b, g_c, x_in, conv_b)], axis=-1)
    return y @ w_out


def swa_sink_attention(q, k, v, sinks):
    bsz, t_len = q.shape[:2]
    pad = BLOCK - N_META
    t_pad = t_len + pad
    nb = t_pad // BLOCK
    padt = lambda z: jnp.pad(z, ((0, 0), (pad, 0), (0, 0), (0, 0)))
    qb = padt(q).reshape(bsz, nb, BLOCK, N_KV_HEADS, GQA_GROUP, HEAD_DIM)
    kb = padt(k).reshape(bsz, nb, BLOCK, N_KV_HEADS, HEAD_DIM)
    vb = padt(v).reshape(bsz, nb, BLOCK, N_KV_HEADS, HEAD_DIM)

    def band(z):
        prev = jnp.pad(z, ((0, 0), (1, 0), (0, 0), (0, 0), (0, 0)))[:, :-1]
        return jnp.concatenate([prev, z], axis=2)

    k_band, v_band = band(kb), band(vb)
    k_meta, v_meta = k[:, :N_META], v[:, :N_META]
    scale = HEAD_DIM ** -0.5
    s_band = jnp.einsum('bnqkgd,bnskd->bnkgqs', qb, k_band).astype(jnp.float32) * scale
    s_meta = jnp.einsum('bnqkgd,bmkd->bnkgqm', qb, k_meta).astype(jnp.float32) * scale

    blk0 = jnp.arange(nb)[:, None] * BLOCK
    t_pos = blk0 + jnp.arange(BLOCK)[None, :]
    s_pos = blk0 - BLOCK + jnp.arange(2 * BLOCK)[None, :]
    dist = t_pos[:, :, None] - s_pos[:, None, :]
    band_ok = (s_pos[:, None, :] >= BLOCK) & (dist >= 0) & (dist < WINDOW)
    meta_ok = (pad + jnp.arange(N_META))[None, None, :] <= t_pos[:, :, None]
    s_band = jnp.where(band_ok[None, :, None, None], s_band, NEG_INF)
    s_meta = jnp.where(meta_ok[None, :, None, None], s_meta, NEG_INF)
    s_sink = jnp.broadcast_to(
        sinks.astype(jnp.float32).reshape(1, 1, N_KV_HEADS, GQA_GROUP, 1, 1),
        s_band.shape[:-1] + (1,))
    prob = jax.nn.softmax(jnp.concatenate([s_band, s_meta, s_sink], axis=-1), axis=-1)
    p_band = prob[..., :2 * BLOCK].astype(v.dtype)
    p_meta = prob[..., 2 * BLOCK:2 * BLOCK + N_META].astype(v.dtype)
    out = (jnp.einsum('bnkgqs,bnskd->bnqkgd', p_band, v_band)
           + jnp.einsum('bnkgqm,bmkd->bnqkgd', p_meta, v_meta))
    return out.reshape(bsz, t_pad, D_ATT)[:, pad:]


def wkv7_scan(r, w, k, v, a, b):
    bsz, _, nh, n = r.shape

    def step(s, inp):
        r_t, w_t, k_t, v_t, a_t, b_t = inp
        sa = jnp.einsum('bhij,bhj->bhi', s, a_t)
        s = s * w_t[:, :, None, :] + sa[..., None] * b_t[:, :, None, :] + v_t[..., None] * k_t[:, :, None, :]
        return s, jnp.einsum('bhij,bhj->bhi', s, r_t)

    xs = tuple(jnp.moveaxis(z, 1, 0) for z in (r, w, k, v, a, b))
    s0 = jnp.zeros((bsz, nh, n, n), jnp.float32)
    _, y = lax.scan(step, s0, xs)
    return jnp.moveaxis(y, 0, 1)


def rwkv7_group(pr, mu, w0, w2, a0, a2, g2, k_k, k_a, r_k, lnx_g, lnx_b):
    f32 = jnp.float32
    bsz, t_len, _ = pr.shape
    pr = pr.astype(f32)
    prev = jnp.pad(pr, ((0, 0), (1, 0), (0, 0)))[:, :-1]
    pr = pr + (prev - pr) * mu.astype(f32)
    r, k, v, wd, ad, gd = jnp.split(
        pr, [D_R, 2 * D_R, 3 * D_R, 3 * D_R + LORA_W, 3 * D_R + LORA_W + LORA_A], axis=-1)
    w_log = -jax.nn.softplus(-(w0.astype(f32) + jnp.tanh(wd) @ w2.astype(f32))) - 0.5
    decay = jnp.exp(-jnp.exp(w_log))
    alpha = jax.nn.sigmoid(a0.astype(f32) + ad @ a2.astype(f32))
    g = jax.nn.sigmoid(gd) @ g2.astype(f32)
    heads = lambda z: z.reshape(bsz, t_len, N_R_HEADS, RWKV_HEAD)
    kk = heads(k * k_k.astype(f32))
    kk = kk / jnp.maximum(jnp.sqrt(jnp.sum(kk * kk, axis=-1, keepdims=True)), 1e-12)
    k = k * (1.0 + (alpha - 1.0) * k_a.astype(f32))
    r_h, k_h, v_h, a_h = heads(r), heads(k), heads(v), heads(alpha)
    y = wkv7_scan(r_h, heads(decay), k_h, v_h, -kk, kk * a_h)
    mean = jnp.mean(y, axis=-1, keepdims=True)
    var = jnp.mean(jnp.square(y - mean), axis=-1, keepdims=True)
    y = ((y - mean) * lax.rsqrt(var + RWKV_GN_EPS)).reshape(bsz, t_len, D_R)
    y = y * lnx_g.astype(f32) + lnx_b.astype(f32)
    bonus = jnp.sum(r_h * k_h * r_k.astype(f32), axis=-1, keepdims=True) * v_h
    y = y + bonus.reshape(bsz, t_len, D_R)
    return y * g


def odd_mixer(h, w_in, sinks, mu, w0, w2, a0, a2, g2, k_k, k_a, r_k, lnx_g, lnx_b, w_out):
    bsz, t_len, _ = h.shape
    p = h @ w_in
    q = p[..., :D_ATT].reshape(bsz, t_len, N_Q_HEADS, HEAD_DIM)
    k = p[..., D_ATT:D_ATT + D_KV].reshape(bsz, t_len, N_KV_HEADS, HEAD_DIM)
    v = p[..., D_ATT + D_KV:ATT_COLS].reshape(bsz, t_len, N_KV_HEADS, HEAD_DIM)
    pos = jnp.arange(t_len)
    y_att = swa_sink_attention(rope(q, pos), rope(k, pos), v, sinks)
    y_rwkv = rwkv7_group(p[..., ATT_COLS:], mu, w0, w2, a0, a2, g2, k_k, k_a, r_k, lnx_g, lnx_b)
    y = jnp.concatenate([y_att.astype(h.dtype), y_rwkv.astype(h.dtype)], axis=-1)
    return y @ w_out


def conv_glu(h, w_up, conv_w, conv_b, w_down):
    u = h @ w_up
    gate, val = u[..., :D_FF], u[..., D_FF:]
    gate = causal_dwconv(gate, conv_w) + conv_b.astype(h.dtype)
    return (jax.nn.silu(gate) * val) @ w_down


def _fwd_setup_inputs(seed: int = 0) -> dict:
    key = jax.random.key(seed)
    ks = iter(jax.random.split(key, 32))
    f32 = jnp.float32
    nrm = lambda shape, s: jax.random.normal(next(ks), shape, f32) * s
    uni = lambda shape, lo, hi: jax.random.uniform(next(ks), shape, f32, lo, hi)
    ne = (DEPTH + 1) // 2
    no = DEPTH // 2
    return {
        'x': nrm((BATCH, SEQ, D_MODEL), 1.0),
        'meta_tokens': nrm((N_META, D_MODEL), 1.0),
        'norm_mix': 1.0 + nrm((DEPTH, D_MODEL), 0.02),
        'norm_ffn': 1.0 + nrm((DEPTH, D_MODEL), 0.02),
        'norm_final': 1.0 + nrm((D_MODEL,), 0.02),
        'ev_w_in': nrm((ne, D_MODEL, EVEN_COLS), D_MODEL ** -0.5),
        'ev_conv_a': nrm((ne, CONV_A_WIDTH, D_A), CONV_A_WIDTH ** -0.5),
        'ev_ln_a_g': 1.0 + nrm((ne, D_A), 0.02),
        'ev_ln_a_b': nrm((ne, D_A), 0.02),
        'ev_conv_b': nrm((ne, CONV_B_WIDTH, D_B), CONV_B_WIDTH ** -0.5),
        'ev_w_out': nrm((ne, D_A + D_B, D_MODEL), (D_A + D_B) ** -0.5),
        'od_w_in': nrm((no, D_MODEL, ODD_COLS), D_MODEL ** -0.5),
        'od_sinks': nrm((no, N_Q_HEADS), 0.5),
        'od_mu': uni((no, RWKV_COLS), 0.0, 1.0),
        'od_w0': uni((no, D_R), -6.0, -1.0),
        'od_w2': nrm((no, LORA_W, D_R), 0.1),
        'od_a0': nrm((no, D_R), 0.1),
        'od_a2': nrm((no, LORA_A, D_R), 0.1),
        'od_g2': nrm((no, LORA_G, D_R), LORA_G ** -0.5),
        'od_k_k': 0.85 + nrm((no, D_R), 0.02),
        'od_k_a': 1.0 + nrm((no, D_R), 0.02),
        'od_r_k': nrm((no, N_R_HEADS, RWKV_HEAD), 0.1),
        'od_lnx_g': 1.0 + nrm((no, D_R), 0.02),
        'od_lnx_b': nrm((no, D_R), 0.02),
        'od_w_out': nrm((no, D_ATT + D_R, D_MODEL), (D_ATT + D_R) ** -0.5),
        'ff_w_up': nrm((DEPTH, D_MODEL, 2 * D_FF), D_MODEL ** -0.5),
        'ff_conv': nrm((DEPTH, FF_CONV_WIDTH, D_FF), FF_CONV_WIDTH ** -0.5),
        'ff_conv_b': nrm((DEPTH, D_FF), 0.02),
        'ff_w_down': nrm((DEPTH, D_FF, D_MODEL), D_FF ** -0.5),
    }


def _fwd_reference(x, meta_tokens, norm_mix, norm_ffn, norm_final,
              ev_w_in, ev_conv_a, ev_ln_a_g, ev_ln_a_b, ev_conv_b, ev_w_out,
              od_w_in, od_sinks, od_mu, od_w0, od_w2, od_a0, od_a2, od_g2,
              od_k_k, od_k_a, od_r_k, od_lnx_g, od_lnx_b, od_w_out,
              ff_w_up, ff_conv, ff_conv_b, ff_w_down):
    bsz = x.shape[0]
    meta = jnp.broadcast_to(meta_tokens[None].astype(x.dtype), (bsz, N_META, D_MODEL))
    h = jnp.concatenate([meta, x], axis=1)
    for i in range(DEPTH):
        hn = rms_norm(h, norm_mix[i])
        j = i // 2
        if i % 2 == 0:
            h = h + even_mixer(hn, ev_w_in[j], ev_conv_a[j], ev_ln_a_g[j], ev_ln_a_b[j],
                               ev_conv_b[j], ev_w_out[j])
        else:
            h = h + odd_mixer(hn, od_w_in[j], od_sinks[j], od_mu[j], od_w0[j], od_w2[j],
                              od_a0[j], od_a2[j], od_g2[j], od_k_k[j], od_k_a[j], od_r_k[j],
                              od_lnx_g[j], od_lnx_b[j], od_w_out[j])
        h = h + conv_glu(rms_norm(h, norm_ffn[i]), ff_w_up[i], ff_conv[i], ff_conv_b[i], ff_w_down[i])
    return rms_norm(h, norm_final)[:, N_META:]


import jax as _jax
import jax.numpy as _jnp

TWIN_FORMAT = 'train_step'
FWD_PARAMS = ['x', 'meta_tokens', 'norm_mix', 'norm_ffn', 'norm_final', 'ev_w_in', 'ev_conv_a', 'ev_ln_a_g', 'ev_ln_a_b', 'ev_conv_b', 'ev_w_out', 'od_w_in', 'od_sinks', 'od_mu', 'od_w0', 'od_w2', 'od_a0', 'od_a2', 'od_g2', 'od_k_k', 'od_k_a', 'od_r_k', 'od_lnx_g', 'od_lnx_b', 'od_w_out', 'ff_w_up', 'ff_conv', 'ff_conv_b', 'ff_w_down']
TWIN_WEIGHTS = ['meta_tokens', 'norm_mix', 'norm_ffn', 'norm_final', 'ev_w_in', 'ev_conv_a', 'ev_ln_a_g', 'ev_ln_a_b', 'ev_conv_b', 'ev_w_out', 'od_w_in', 'od_sinks', 'od_mu', 'od_w0', 'od_w2', 'od_a0', 'od_a2', 'od_g2', 'od_k_k', 'od_k_a', 'od_r_k', 'od_lnx_g', 'od_lnx_b', 'od_w_out', 'ff_w_up', 'ff_conv', 'ff_conv_b', 'ff_w_down']
TWIN_DIFF_INPUT = 'x'
TWIN_INPUTS = ['x', 'meta_tokens', 'norm_mix', 'norm_ffn', 'norm_final', 'ev_w_in', 'ev_conv_a', 'ev_ln_a_g', 'ev_ln_a_b', 'ev_conv_b', 'ev_w_out', 'od_w_in', 'od_sinks', 'od_mu', 'od_w0', 'od_w2', 'od_a0', 'od_a2', 'od_g2', 'od_k_k', 'od_k_a', 'od_r_k', 'od_lnx_g', 'od_lnx_b', 'od_w_out', 'ff_w_up', 'ff_conv', 'ff_conv_b', 'ff_w_down', 'loss_target', 'm_meta_tokens', 'm_norm_mix', 'm_norm_ffn', 'm_norm_final', 'm_ev_w_in', 'm_ev_conv_a', 'm_ev_ln_a_g', 'm_ev_ln_a_b', 'm_ev_conv_b', 'm_ev_w_out', 'm_od_w_in', 'm_od_sinks', 'm_od_mu', 'm_od_w0', 'm_od_w2', 'm_od_a0', 'm_od_a2', 'm_od_g2', 'm_od_k_k', 'm_od_k_a', 'm_od_r_k', 'm_od_lnx_g', 'm_od_lnx_b', 'm_od_w_out', 'm_ff_w_up', 'm_ff_conv', 'm_ff_conv_b', 'm_ff_w_down', 'v_meta_tokens', 'v_norm_mix', 'v_norm_ffn', 'v_norm_final', 'v_ev_w_in', 'v_ev_conv_a', 'v_ev_ln_a_g', 'v_ev_ln_a_b', 'v_ev_conv_b', 'v_ev_w_out', 'v_od_w_in', 'v_od_sinks', 'v_od_mu', 'v_od_w0', 'v_od_w2', 'v_od_a0', 'v_od_a2', 'v_od_g2', 'v_od_k_k', 'v_od_k_a', 'v_od_r_k', 'v_od_lnx_g', 'v_od_lnx_b', 'v_od_w_out', 'v_ff_w_up', 'v_ff_conv', 'v_ff_conv_b', 'v_ff_w_down']
TWIN_OUTPUTS = ['loss', 'grad_x', 'grad_meta_tokens', 'grad_norm_mix', 'grad_norm_ffn', 'grad_norm_final', 'grad_ev_w_in', 'grad_ev_conv_a', 'grad_ev_ln_a_g', 'grad_ev_ln_a_b', 'grad_ev_conv_b', 'grad_ev_w_out', 'grad_od_w_in', 'grad_od_sinks', 'grad_od_mu', 'grad_od_w0', 'grad_od_w2', 'grad_od_a0', 'grad_od_a2', 'grad_od_g2', 'grad_od_k_k', 'grad_od_k_a', 'grad_od_r_k', 'grad_od_lnx_g', 'grad_od_lnx_b', 'grad_od_w_out', 'grad_ff_w_up', 'grad_ff_conv', 'grad_ff_conv_b', 'grad_ff_w_down', 'delta_meta_tokens', 'delta_norm_mix', 'delta_norm_ffn', 'delta_norm_final', 'delta_ev_w_in', 'delta_ev_conv_a', 'delta_ev_ln_a_g', 'delta_ev_ln_a_b', 'delta_ev_conv_b', 'delta_ev_w_out', 'delta_od_w_in', 'delta_od_sinks', 'delta_od_mu', 'delta_od_w0', 'delta_od_w2', 'delta_od_a0', 'delta_od_a2', 'delta_od_g2', 'delta_od_k_k', 'delta_od_k_a', 'delta_od_r_k', 'delta_od_lnx_g', 'delta_od_lnx_b', 'delta_od_w_out', 'delta_ff_w_up', 'delta_ff_conv', 'delta_ff_conv_b', 'delta_ff_w_down', 'new_m_meta_tokens', 'new_m_norm_mix', 'new_m_norm_ffn', 'new_m_norm_final', 'new_m_ev_w_in', 'new_m_ev_conv_a', 'new_m_ev_ln_a_g', 'new_m_ev_ln_a_b', 'new_m_ev_conv_b', 'new_m_ev_w_out', 'new_m_od_w_in', 'new_m_od_sinks', 'new_m_od_mu', 'new_m_od_w0', 'new_m_od_w2', 'new_m_od_a0', 'new_m_od_a2', 'new_m_od_g2', 'new_m_od_k_k', 'new_m_od_k_a', 'new_m_od_r_k', 'new_m_od_lnx_g', 'new_m_od_lnx_b', 'new_m_od_w_out', 'new_m_ff_w_up', 'new_m_ff_conv', 'new_m_ff_conv_b', 'new_m_ff_w_down', 'new_v_meta_tokens', 'new_v_norm_mix', 'new_v_norm_ffn', 'new_v_norm_final', 'new_v_ev_w_in', 'new_v_ev_conv_a', 'new_v_ev_ln_a_g', 'new_v_ev_ln_a_b', 'new_v_ev_conv_b', 'new_v_ev_w_out', 'new_v_od_w_in', 'new_v_od_sinks', 'new_v_od_mu', 'new_v_od_w0', 'new_v_od_w2', 'new_v_od_a0', 'new_v_od_a2', 'new_v_od_g2', 'new_v_od_k_k', 'new_v_od_k_a', 'new_v_od_r_k', 'new_v_od_lnx_g', 'new_v_od_lnx_b', 'new_v_od_w_out', 'new_v_ff_w_up', 'new_v_ff_conv', 'new_v_ff_conv_b', 'new_v_ff_w_down']
TWIN_LEAF_KINDS = {'loss': 'loss', 'grad_x': 'grad_x', 'grad_meta_tokens': 'grad_w', 'grad_norm_mix': 'grad_w', 'grad_norm_ffn': 'grad_w', 'grad_norm_final': 'grad_w', 'grad_ev_w_in': 'grad_w', 'grad_ev_conv_a': 'grad_w', 'grad_ev_ln_a_g': 'grad_w', 'grad_ev_ln_a_b': 'grad_w', 'grad_ev_conv_b': 'grad_w', 'grad_ev_w_out': 'grad_w', 'grad_od_w_in': 'grad_w', 'grad_od_sinks': 'grad_w', 'grad_od_mu': 'grad_w', 'grad_od_w0': 'grad_w', 'grad_od_w2': 'grad_w', 'grad_od_a0': 'grad_w', 'grad_od_a2': 'grad_w', 'grad_od_g2': 'grad_w', 'grad_od_k_k': 'grad_w', 'grad_od_k_a': 'grad_w', 'grad_od_r_k': 'grad_w', 'grad_od_lnx_g': 'grad_w', 'grad_od_lnx_b': 'grad_w', 'grad_od_w_out': 'grad_w', 'grad_ff_w_up': 'grad_w', 'grad_ff_conv': 'grad_w', 'grad_ff_conv_b': 'grad_w', 'grad_ff_w_down': 'grad_w', 'delta_meta_tokens': 'delta_w', 'delta_norm_mix': 'delta_w', 'delta_norm_ffn': 'delta_w', 'delta_norm_final': 'delta_w', 'delta_ev_w_in': 'delta_w', 'delta_ev_conv_a': 'delta_w', 'delta_ev_ln_a_g': 'delta_w', 'delta_ev_ln_a_b': 'delta_w', 'delta_ev_conv_b': 'delta_w', 'delta_ev_w_out': 'delta_w', 'delta_od_w_in': 'delta_w', 'delta_od_sinks': 'delta_w', 'delta_od_mu': 'delta_w', 'delta_od_w0': 'delta_w', 'delta_od_w2': 'delta_w', 'delta_od_a0': 'delta_w', 'delta_od_a2': 'delta_w', 'delta_od_g2': 'delta_w', 'delta_od_k_k': 'delta_w', 'delta_od_k_a': 'delta_w', 'delta_od_r_k': 'delta_w', 'delta_od_lnx_g': 'delta_w', 'delta_od_lnx_b': 'delta_w', 'delta_od_w_out': 'delta_w', 'delta_ff_w_up': 'delta_w', 'delta_ff_conv': 'delta_w', 'delta_ff_conv_b': 'delta_w', 'delta_ff_w_down': 'delta_w', 'new_m_meta_tokens': 'new_m', 'new_m_norm_mix': 'new_m', 'new_m_norm_ffn': 'new_m', 'new_m_norm_final': 'new_m', 'new_m_ev_w_in': 'new_m', 'new_m_ev_conv_a': 'new_m', 'new_m_ev_ln_a_g': 'new_m', 'new_m_ev_ln_a_b': 'new_m', 'new_m_ev_conv_b': 'new_m', 'new_m_ev_w_out': 'new_m', 'new_m_od_w_in': 'new_m', 'new_m_od_sinks': 'new_m', 'new_m_od_mu': 'new_m', 'new_m_od_w0': 'new_m', 'new_m_od_w2': 'new_m', 'new_m_od_a0': 'new_m', 'new_m_od_a2': 'new_m', 'new_m_od_g2': 'new_m', 'new_m_od_k_k': 'new_m', 'new_m_od_k_a': 'new_m', 'new_m_od_r_k': 'new_m', 'new_m_od_lnx_g': 'new_m', 'new_m_od_lnx_b': 'new_m', 'new_m_od_w_out': 'new_m', 'new_m_ff_w_up': 'new_m', 'new_m_ff_conv': 'new_m', 'new_m_ff_conv_b': 'new_m', 'new_m_ff_w_down': 'new_m', 'new_v_meta_tokens': 'new_v', 'new_v_norm_mix': 'new_v', 'new_v_norm_ffn': 'new_v', 'new_v_norm_final': 'new_v', 'new_v_ev_w_in': 'new_v', 'new_v_ev_conv_a': 'new_v', 'new_v_ev_ln_a_g': 'new_v', 'new_v_ev_ln_a_b': 'new_v', 'new_v_ev_conv_b': 'new_v', 'new_v_ev_w_out': 'new_v', 'new_v_od_w_in': 'new_v', 'new_v_od_sinks': 'new_v', 'new_v_od_mu': 'new_v', 'new_v_od_w0': 'new_v', 'new_v_od_w2': 'new_v', 'new_v_od_a0': 'new_v', 'new_v_od_a2': 'new_v', 'new_v_od_g2': 'new_v', 'new_v_od_k_k': 'new_v', 'new_v_od_k_a': 'new_v', 'new_v_od_r_k': 'new_v', 'new_v_od_lnx_g': 'new_v', 'new_v_od_lnx_b': 'new_v', 'new_v_od_w_out': 'new_v', 'new_v_ff_w_up': 'new_v', 'new_v_ff_conv': 'new_v', 'new_v_ff_conv_b': 'new_v', 'new_v_ff_w_down': 'new_v'}


def _forward(args):
    return _fwd_reference(*[args[k] for k in FWD_PARAMS])


def _output_shape():
    out = _jax.eval_shape(lambda: _forward(_fwd_setup_inputs(0)))
    return out.shape, out.dtype

N_MICROBATCH = 1
ADAM_LR = 0.001
ADAM_B1 = 0.9
ADAM_B2 = 0.999
ADAM_EPS = 1e-08
ADAM_WD = 0.01
ADAM_STEP = 10
PER_EXAMPLE_BATCH_AXIS = {'x': 0, 'loss_target': 0}
SHARED_INPUTS = []
_WEIGHT_DTYPES = {'meta_tokens': _jnp.float32, 'norm_mix': _jnp.float32, 'norm_ffn': _jnp.float32, 'norm_final': _jnp.float32, 'ev_w_in': _jnp.float32, 'ev_conv_a': _jnp.float32, 'ev_ln_a_g': _jnp.float32, 'ev_ln_a_b': _jnp.float32, 'ev_conv_b': _jnp.float32, 'ev_w_out': _jnp.float32, 'od_w_in': _jnp.float32, 'od_sinks': _jnp.float32, 'od_mu': _jnp.float32, 'od_w0': _jnp.float32, 'od_w2': _jnp.float32, 'od_a0': _jnp.float32, 'od_a2': _jnp.float32, 'od_g2': _jnp.float32, 'od_k_k': _jnp.float32, 'od_k_a': _jnp.float32, 'od_r_k': _jnp.float32, 'od_lnx_g': _jnp.float32, 'od_lnx_b': _jnp.float32, 'od_w_out': _jnp.float32, 'ff_w_up': _jnp.float32, 'ff_conv': _jnp.float32, 'ff_conv_b': _jnp.float32, 'ff_w_down': _jnp.float32}
MOMENT_SCALE = {'meta_tokens': 6.987653e-03, 'norm_mix': 1.274188e-01, 'norm_ffn': 8.058007e-02, 'norm_final': 1.599969e+01, 'ev_w_in': 1.030468e-01, 'ev_conv_a': 7.674258e-02, 'ev_ln_a_g': 8.989143e-02, 'ev_ln_a_b': 8.636350e-02, 'ev_conv_b': 1.225057e-01, 'ev_w_out': 1.028611e-01, 'od_w_in': 4.331485e-02, 'od_sinks': 1.488315e-03, 'od_mu': 8.395579e-02, 'od_w0': 1.924947e-02, 'od_w2': 2.853426e-03, 'od_a0': 1.917158e-02, 'od_a2': 1.825882e-02, 'od_g2': 5.318744e-02, 'od_k_k': 5.503805e-02, 'od_k_a': 5.287721e-02, 'od_r_k': 1.232454e-01, 'od_lnx_g': 5.704682e-02, 'od_lnx_b': 5.377463e-02, 'od_w_out': 3.745293e-02, 'ff_w_up': 3.381125e-02, 'ff_conv': 3.463263e-02, 'ff_conv_b': 3.289040e-02, 'ff_w_down': 5.523010e-02}


def _to_microbatches(a, axis):
    t = _jnp.moveaxis(a, axis, 0)
    t = t.reshape((N_MICROBATCH, t.shape[0] // N_MICROBATCH) + t.shape[1:])
    return _jnp.moveaxis(t, 1, axis + 1)


def setup_inputs(seed: int = 0) -> dict:
    inp = _fwd_setup_inputs(seed)
    key = _jax.random.fold_in(_jax.random.key(seed), 7919)
    shape, _ = _output_shape()
    out = dict(inp)
    out["loss_target"] = _jax.random.normal(_jax.random.fold_in(key, 0), shape, _jnp.float32)
    for i, name in enumerate(TWIN_WEIGHTS):
        w = inp[name].astype(_jnp.float32)
        if MOMENT_SCALE is None:
            s = _jnp.sqrt(_jnp.mean(_jnp.square(w)) + 1e-30)
        else:
            s = MOMENT_SCALE[name]
        km, kv = _jax.random.split(_jax.random.fold_in(key, i + 1))
        out[name] = w
        out["m_" + name] = s * _jax.random.normal(km, w.shape, _jnp.float32)
        out["v_" + name] = (s * s) * _jax.random.uniform(kv, w.shape, _jnp.float32, 0.5, 1.5)
    if N_MICROBATCH > 1:
        for name, axis in PER_EXAMPLE_BATCH_AXIS.items():
            out[name] = _to_microbatches(out[name], axis)
    return {'x': out['x'], 'meta_tokens': out['meta_tokens'], 'norm_mix': out['norm_mix'], 'norm_ffn': out['norm_ffn'], 'norm_final': out['norm_final'], 'ev_w_in': out['ev_w_in'], 'ev_conv_a': out['ev_conv_a'], 'ev_ln_a_g': out['ev_ln_a_g'], 'ev_ln_a_b': out['ev_ln_a_b'], 'ev_conv_b': out['ev_conv_b'], 'ev_w_out': out['ev_w_out'], 'od_w_in': out['od_w_in'], 'od_sinks': out['od_sinks'], 'od_mu': out['od_mu'], 'od_w0': out['od_w0'], 'od_w2': out['od_w2'], 'od_a0': out['od_a0'], 'od_a2': out['od_a2'], 'od_g2': out['od_g2'], 'od_k_k': out['od_k_k'], 'od_k_a': out['od_k_a'], 'od_r_k': out['od_r_k'], 'od_lnx_g': out['od_lnx_g'], 'od_lnx_b': out['od_lnx_b'], 'od_w_out': out['od_w_out'], 'ff_w_up': out['ff_w_up'], 'ff_conv': out['ff_conv'], 'ff_conv_b': out['ff_conv_b'], 'ff_w_down': out['ff_w_down'], 'loss_target': out['loss_target'], 'm_meta_tokens': out['m_meta_tokens'], 'm_norm_mix': out['m_norm_mix'], 'm_norm_ffn': out['m_norm_ffn'], 'm_norm_final': out['m_norm_final'], 'm_ev_w_in': out['m_ev_w_in'], 'm_ev_conv_a': out['m_ev_conv_a'], 'm_ev_ln_a_g': out['m_ev_ln_a_g'], 'm_ev_ln_a_b': out['m_ev_ln_a_b'], 'm_ev_conv_b': out['m_ev_conv_b'], 'm_ev_w_out': out['m_ev_w_out'], 'm_od_w_in': out['m_od_w_in'], 'm_od_sinks': out['m_od_sinks'], 'm_od_mu': out['m_od_mu'], 'm_od_w0': out['m_od_w0'], 'm_od_w2': out['m_od_w2'], 'm_od_a0': out['m_od_a0'], 'm_od_a2': out['m_od_a2'], 'm_od_g2': out['m_od_g2'], 'm_od_k_k': out['m_od_k_k'], 'm_od_k_a': out['m_od_k_a'], 'm_od_r_k': out['m_od_r_k'], 'm_od_lnx_g': out['m_od_lnx_g'], 'm_od_lnx_b': out['m_od_lnx_b'], 'm_od_w_out': out['m_od_w_out'], 'm_ff_w_up': out['m_ff_w_up'], 'm_ff_conv': out['m_ff_conv'], 'm_ff_conv_b': out['m_ff_conv_b'], 'm_ff_w_down': out['m_ff_w_down'], 'v_meta_tokens': out['v_meta_tokens'], 'v_norm_mix': out['v_norm_mix'], 'v_norm_ffn': out['v_norm_ffn'], 'v_norm_final': out['v_norm_final'], 'v_ev_w_in': out['v_ev_w_in'], 'v_ev_conv_a': out['v_ev_conv_a'], 'v_ev_ln_a_g': out['v_ev_ln_a_g'], 'v_ev_ln_a_b': out['v_ev_ln_a_b'], 'v_ev_conv_b': out['v_ev_conv_b'], 'v_ev_w_out': out['v_ev_w_out'], 'v_od_w_in': out['v_od_w_in'], 'v_od_sinks': out['v_od_sinks'], 'v_od_mu': out['v_od_mu'], 'v_od_w0': out['v_od_w0'], 'v_od_w2': out['v_od_w2'], 'v_od_a0': out['v_od_a0'], 'v_od_a2': out['v_od_a2'], 'v_od_g2': out['v_od_g2'], 'v_od_k_k': out['v_od_k_k'], 'v_od_k_a': out['v_od_k_a'], 'v_od_r_k': out['v_od_r_k'], 'v_od_lnx_g': out['v_od_lnx_g'], 'v_od_lnx_b': out['v_od_lnx_b'], 'v_od_w_out': out['v_od_w_out'], 'v_ff_w_up': out['v_ff_w_up'], 'v_ff_conv': out['v_ff_conv'], 'v_ff_conv_b': out['v_ff_conv_b'], 'v_ff_w_down': out['v_ff_w_down']}


def _loss(weights, diff, rest, loss_target):
    with _jax.named_scope("forward"):
        args = {**rest, TWIN_DIFF_INPUT: diff, **{k: w.astype(_WEIGHT_DTYPES[k]) for k, w in weights.items()}}
        y = _forward(args)
    with _jax.named_scope("loss_head"):
        err = _jnp.square(y.astype(_jnp.float32) - loss_target)
        return 0.5 * _jnp.sum(_jnp.mean(err, axis=-1)) if err.ndim else 0.5 * err


def _adamw(w, g, m, v):
    m = ADAM_B1 * m + (1.0 - ADAM_B1) * g
    v = ADAM_B2 * v + (1.0 - ADAM_B2) * _jnp.square(g)
    m_hat = m / (1.0 - ADAM_B1 ** ADAM_STEP)
    v_hat = v / (1.0 - ADAM_B2 ** ADAM_STEP)
    delta = -ADAM_LR * (m_hat / (_jnp.sqrt(v_hat) + ADAM_EPS) + ADAM_WD * w)
    return delta, m, v


def reference(x, meta_tokens, norm_mix, norm_ffn, norm_final, ev_w_in, ev_conv_a, ev_ln_a_g, ev_ln_a_b, ev_conv_b, ev_w_out, od_w_in, od_sinks, od_mu, od_w0, od_w2, od_a0, od_a2, od_g2, od_k_k, od_k_a, od_r_k, od_lnx_g, od_lnx_b, od_w_out, ff_w_up, ff_conv, ff_conv_b, ff_w_down, loss_target, m_meta_tokens, m_norm_mix, m_norm_ffn, m_norm_final, m_ev_w_in, m_ev_conv_a, m_ev_ln_a_g, m_ev_ln_a_b, m_ev_conv_b, m_ev_w_out, m_od_w_in, m_od_sinks, m_od_mu, m_od_w0, m_od_w2, m_od_a0, m_od_a2, m_od_g2, m_od_k_k, m_od_k_a, m_od_r_k, m_od_lnx_g, m_od_lnx_b, m_od_w_out, m_ff_w_up, m_ff_conv, m_ff_conv_b, m_ff_w_down, v_meta_tokens, v_norm_mix, v_norm_ffn, v_norm_final, v_ev_w_in, v_ev_conv_a, v_ev_ln_a_g, v_ev_ln_a_b, v_ev_conv_b, v_ev_w_out, v_od_w_in, v_od_sinks, v_od_mu, v_od_w0, v_od_w2, v_od_a0, v_od_a2, v_od_g2, v_od_k_k, v_od_k_a, v_od_r_k, v_od_lnx_g, v_od_lnx_b, v_od_w_out, v_ff_w_up, v_ff_conv, v_ff_conv_b, v_ff_w_down):
    given = dict(x=x, meta_tokens=meta_tokens, norm_mix=norm_mix, norm_ffn=norm_ffn, norm_final=norm_final, ev_w_in=ev_w_in, ev_conv_a=ev_conv_a, ev_ln_a_g=ev_ln_a_g, ev_ln_a_b=ev_ln_a_b, ev_conv_b=ev_conv_b, ev_w_out=ev_w_out, od_w_in=od_w_in, od_sinks=od_sinks, od_mu=od_mu, od_w0=od_w0, od_w2=od_w2, od_a0=od_a0, od_a2=od_a2, od_g2=od_g2, od_k_k=od_k_k, od_k_a=od_k_a, od_r_k=od_r_k, od_lnx_g=od_lnx_g, od_lnx_b=od_lnx_b, od_w_out=od_w_out, ff_w_up=ff_w_up, ff_conv=ff_conv, ff_conv_b=ff_conv_b, ff_w_down=ff_w_down, loss_target=loss_target, m_meta_tokens=m_meta_tokens, m_norm_mix=m_norm_mix, m_norm_ffn=m_norm_ffn, m_norm_final=m_norm_final, m_ev_w_in=m_ev_w_in, m_ev_conv_a=m_ev_conv_a, m_ev_ln_a_g=m_ev_ln_a_g, m_ev_ln_a_b=m_ev_ln_a_b, m_ev_conv_b=m_ev_conv_b, m_ev_w_out=m_ev_w_out, m_od_w_in=m_od_w_in, m_od_sinks=m_od_sinks, m_od_mu=m_od_mu, m_od_w0=m_od_w0, m_od_w2=m_od_w2, m_od_a0=m_od_a0, m_od_a2=m_od_a2, m_od_g2=m_od_g2, m_od_k_k=m_od_k_k, m_od_k_a=m_od_k_a, m_od_r_k=m_od_r_k, m_od_lnx_g=m_od_lnx_g, m_od_lnx_b=m_od_lnx_b, m_od_w_out=m_od_w_out, m_ff_w_up=m_ff_w_up, m_ff_conv=m_ff_conv, m_ff_conv_b=m_ff_conv_b, m_ff_w_down=m_ff_w_down, v_meta_tokens=v_meta_tokens, v_norm_mix=v_norm_mix, v_norm_ffn=v_norm_ffn, v_norm_final=v_norm_final, v_ev_w_in=v_ev_w_in, v_ev_conv_a=v_ev_conv_a, v_ev_ln_a_g=v_ev_ln_a_g, v_ev_ln_a_b=v_ev_ln_a_b, v_ev_conv_b=v_ev_conv_b, v_ev_w_out=v_ev_w_out, v_od_w_in=v_od_w_in, v_od_sinks=v_od_sinks, v_od_mu=v_od_mu, v_od_w0=v_od_w0, v_od_w2=v_od_w2, v_od_a0=v_od_a0, v_od_a2=v_od_a2, v_od_g2=v_od_g2, v_od_k_k=v_od_k_k, v_od_k_a=v_od_k_a, v_od_r_k=v_od_r_k, v_od_lnx_g=v_od_lnx_g, v_od_lnx_b=v_od_lnx_b, v_od_w_out=v_od_w_out, v_ff_w_up=v_ff_w_up, v_ff_conv=v_ff_conv, v_ff_conv_b=v_ff_conv_b, v_ff_w_down=v_ff_w_down)
    weights = {n: given[n] for n in TWIN_WEIGHTS}
    shared = {n: given[n] for n in SHARED_INPUTS}
    per_example = {n: given[n] for n in ['x']}
    grad_fn = _jax.value_and_grad(_loss, argnums=(0, 1))

    def one_microbatch(ex, loss_target):
        ex = dict(ex)
        diff = ex.pop(TWIN_DIFF_INPUT)
        return grad_fn(weights, diff, {**shared, **ex}, loss_target)

    if N_MICROBATCH == 1:
        loss, (grad_w, grad_x) = one_microbatch(per_example, given["loss_target"])
    else:
        def body(carry, xs):
            loss_sum, grad_sum = carry
            l_k, (gw_k, gx_k) = one_microbatch(xs[0], xs[1])
            with _jax.named_scope("update"):
                return (loss_sum + l_k, _jax.tree.map(_jnp.add, grad_sum, gw_k)), gx_k

        init = (_jnp.zeros((), _jnp.float32), _jax.tree.map(_jnp.zeros_like, weights))
        (loss, grad_w), grad_x = _jax.lax.scan(body, init, (per_example, given["loss_target"]))
    with _jax.named_scope("update"):
        delta_w, new_m, new_v = {}, {}, {}
        for n in TWIN_WEIGHTS:
            delta_w[n], new_m[n], new_v[n] = _adamw(weights[n], grad_w[n], given["m_" + n], given["v_" + n])
    return (loss, grad_x, *[grad_w[n] for n in TWIN_WEIGHTS], *[delta_w[n] for n in TWIN_WEIGHTS],
            *[new_m[n] for n in TWIN_WEIGHTS], *[new_v[n] for n in TWIN_WEIGHTS])
```

```python
import jax
import jax.numpy as jnp
from jax import lax
from jax.experimental import pallas as pl
from jax.experimental.pallas import tpu as pltpu

f32, bf16 = jnp.float32, jnp.bfloat16

D_MODEL = 1024
N_META = 16
RMS_EPS = 1e-6
LN_EPS = 1e-5
D_A = 512
CONV_A_WIDTH = 31
CONV_B_WIDTH = 3
HEAD_DIM = 64
N_Q_HEADS = 8
N_KV_HEADS = 2
GQA_GROUP = 4
D_ATT = 512
D_KV = 128
BLOCK = 128
ROPE_THETA = 10000.0
D_R = 512
LORA_W, LORA_A, LORA_G = 64, 64, 128
RWKV_GN_EPS = 64e-5
ATT_COLS = D_ATT + 2 * D_KV
RWKV_COLS = 3 * D_R + LORA_W + LORA_A + LORA_G
D_FF = 2816
FF_CONV_WIDTH = 3
NEG_INF = -1e30
ATT_PAD = BLOCK - N_META
ATT_SCALE = HEAD_DIM ** -0.5

ADAM_LR, ADAM_B1, ADAM_B2, ADAM_EPS, ADAM_WD, ADAM_STEP = 0.001, 0.9, 0.999, 1e-08, 0.01, 10

N_DEV = 8
LANES = 128
SUBLANES = 8
SCAN_CHUNK = 48
PAIR_ROWS = 4 * HEAD_DIM
V7X_VMEM_LIMIT = 56 * 1024 * 1024
ADAMW_BLOCK_ELEMS = 400 * 1024
MESH = pl.DeviceIdType.MESH
S = jax.ShapeDtypeStruct
HIGHEST = lax.Precision.HIGHEST


def _pc(body, **kw):
    return pl.pallas_call(body, **kw)


def _cparams(sem=None):
    return pltpu.CompilerParams(dimension_semantics=sem, vmem_limit_bytes=V7X_VMEM_LIMIT)


def _divisor_block(t, unit, limit):
    best = unit
    for rb in range(unit, limit + 1, unit):
        if t % rb == 0:
            best = rb
    assert t % best == 0, (t, unit)
    return best


def _row_block(t):
    return _divisor_block(t, 16, 704)


def _row_block8(t):
    return _divisor_block(t, 8, 344)


def _col_tile(n):
    for t in (512, 256, 128):
        if n % t == 0:
            return t
    return n


def _full(shape):
    nd = len(shape)
    return pl.BlockSpec(shape, lambda *_: (0,) * nd)


def _sigmoid(x):
    return jax.nn.sigmoid(x)


_DIMS = {"nn": (((1,), (0,)), ((), ())), "nt": (((1,), (1,)), ((), ())), "tn": (((0,), (0,)), ((), ()))}
MM_MAX_K = 2816


def _mm(a, b, mode, name, out_dtype=f32, res=None):
    if mode == "nn":
        (m, k), (k2, n) = a.shape, b.shape
    elif mode == "nt":
        (m, k), (n, k2) = a.shape, b.shape
    else:
        (k, m), (k2, n) = a.shape, b.shape
    assert k == k2, (a.shape, b.shape, mode)
    tm = _row_block(m) if m % LANES else _col_tile(m)
    tn = _col_tile(n)
    nk = 1 if (mode == "tn" or k <= MM_MAX_K) else k // MM_MAX_K
    tk = k // nk
    assert tk * nk == k
    dims = _DIMS[mode]

    def body(a_ref, b_ref, *rest):
        part = lax.dot_general(a_ref[...].astype(bf16), b_ref[...].astype(bf16), dims, preferred_element_type=f32)
        if nk == 1:
            o_ref = rest[-1]
            if res is not None:
                part = part + rest[0][...]
            o_ref[...] = part.astype(out_dtype)
            return
        o_ref, acc_ref = rest[-2], rest[-1]
        kk = pl.program_id(2)

        @pl.when(kk == 0)
        def _():
            acc_ref[...] = part

        @pl.when(kk > 0)
        def _():
            acc_ref[...] += part

        @pl.when(kk == nk - 1)
        def _():
            acc = acc_ref[...]
            if res is not None:
                acc = acc + rest[0][...]
            o_ref[...] = acc.astype(out_dtype)

    if mode == "tn":
        a_spec = pl.BlockSpec((k, tm), lambda i, j, kk: (0, i))
    else:
        a_spec = pl.BlockSpec((tm, tk), lambda i, j, kk: (i, kk))
    if mode == "nt":
        b_spec = pl.BlockSpec((tn, tk), lambda i, j, kk: (j, kk))
    else:
        b_spec = pl.BlockSpec((tk, tn), lambda i, j, kk: (kk, j))
    o_spec = pl.BlockSpec((tm, tn), lambda i, j, kk: (i, j))
    ins, specs = [a, b], [a_spec, b_spec]
    if res is not None:
        ins.append(res)
        specs.append(o_spec)
    scratch = [pltpu.VMEM((tm, tn), f32)] if nk > 1 else []
    return _pc(body, name=name, grid=(m // tm, n // tn, nk), in_specs=specs, out_specs=o_spec,
               out_shape=S((m, n), out_dtype), scratch_shapes=scratch,
               compiler_params=_cparams(("arbitrary", "arbitrary", "arbitrary")))(*ins)


def _rms_fwd(x, g, name):
    t, d = x.shape
    rb = _row_block(t)

    def body(x_ref, g_ref, o_ref):
        xv = x_ref[...]
        rstd = lax.rsqrt(jnp.mean(xv * xv, axis=-1, keepdims=True) + RMS_EPS)
        o_ref[...] = (xv * rstd * g_ref[...]).astype(bf16)

    row = pl.BlockSpec((rb, d), lambda i: (i, 0))
    return _pc(body, name=name, grid=(t // rb,), in_specs=[row, _full((1, d))], out_specs=row,
               out_shape=S((t, d), bf16), compiler_params=_cparams(("arbitrary",)))(x, g.reshape(1, d))


def _rms_bwd(dy, x, g, dres, name):
    t, d = x.shape
    rb = _row_block8(t)

    def body(dy_ref, x_ref, g_ref, dres_ref, dx_ref, dg_ref):
        @pl.when(pl.program_id(0) == 0)
        def _():
            dg_ref[...] = jnp.zeros_like(dg_ref)
        xv, dyv = x_ref[...], dy_ref[...]
        rstd = lax.rsqrt(jnp.mean(xv * xv, axis=-1, keepdims=True) + RMS_EPS)
        xn = xv * rstd
        dg_ref[...] += jnp.sum(dyv * xn, axis=0, keepdims=True)
        dxh = dyv * g_ref[...]
        dx_ref[...] = dres_ref[...] + rstd * (dxh - xn * jnp.mean(dxh * xn, axis=-1, keepdims=True))

    row = pl.BlockSpec((rb, d), lambda i: (i, 0))
    return _pc(body, name=name, grid=(t // rb,), in_specs=[row, row, _full((1, d)), row],
               out_specs=(row, _full((1, d))), out_shape=(S((t, d), f32), S((1, d), f32)),
               compiler_params=_cparams(("arbitrary",)))(dy, x, g.reshape(1, d), dres)


def _final_loss(h, g, target_padded):
    t, d = h.shape
    rb = _row_block8(t)

    def body(x_ref, g_ref, t_ref, loss_ref, dx_ref, dg_ref):
        i = pl.program_id(0)

        @pl.when(i == 0)
        def _():
            dg_ref[...] = jnp.zeros_like(dg_ref)
            loss_ref[...] = jnp.zeros_like(loss_ref)
        xv = x_ref[...]
        rstd = lax.rsqrt(jnp.mean(xv * xv, axis=-1, keepdims=True) + RMS_EPS)
        xn = xv * rstd
        gv = g_ref[...]
        row = i * rb + lax.broadcasted_iota(jnp.int32, (rb, 1), 0)
        diff = jnp.where(row >= N_META, xn * gv - t_ref[...], 0.0)
        loss_ref[...] += 0.5 * jnp.sum(jnp.mean(diff * diff, axis=-1, keepdims=True))
        dout = diff * (1.0 / d)
        dg_ref[...] += jnp.sum(dout * xn, axis=0, keepdims=True)
        dxh = dout * gv
        dx_ref[...] = rstd * (dxh - xn * jnp.mean(dxh * xn, axis=-1, keepdims=True))

    row = pl.BlockSpec((rb, d), lambda i: (i, 0))
    return _pc(body, name="final_loss", grid=(t // rb,), in_specs=[row, _full((1, d)), row],
               out_specs=(_full((SUBLANES, LANES)), row, _full((1, d))),
               out_shape=(S((SUBLANES, LANES), f32), S((t, d), f32), S((1, d), f32)),
               compiler_params=_cparams(("arbitrary",)))(h, g.reshape(1, d), target_padded)


CONV_LEAD = 32


def _fill_front_padded(pad_ref, x, t):
    pad_ref[0:CONV_LEAD, :] = jnp.zeros((CONV_LEAD, x.shape[1]), f32)
    pad_ref[CONV_LEAD:CONV_LEAD + t, :] = x


def _fill_back_padded(pad_ref, x, t):
    pad_ref[0:t, :] = x
    pad_ref[t:t + CONV_LEAD, :] = jnp.zeros((CONV_LEAD, x.shape[1]), f32)


def _conv_rows(pad_ref, w_ref, kw, r0, nr):
    acc = None
    for j in range(kw):
        lo = CONV_LEAD + r0 - (kw - 1) + j
        term = w_ref[j:j + 1, :] * pad_ref[lo:lo + nr, :]
        acc = term if acc is None else acc + term
    return acc


def _conv_t_rows(padb_ref, w_ref, kw, r0, nr):
    acc = None
    for j in range(kw):
        lo = r0 + (kw - 1) - j
        term = w_ref[j:j + 1, :] * padb_ref[lo:lo + nr, :]
        acc = term if acc is None else acc + term
    return acc


def _conv_dw_rows(dy_blk, pad_ref, kw, r0, nr):
    out = []
    for j in range(kw):
        lo = CONV_LEAD + r0 - (kw - 1) + j
        out.append(jnp.sum(dy_blk * pad_ref[lo:lo + nr, :], axis=0, keepdims=True))
    return out


def _acc_list(a, b):
    return b if a is None else [x + y for x, y in zip(a, b)]


def _ev_a_conv(p, conv_a):
    t = p.shape[0]
    cr = _row_block8(t)
    nb = D_A // LANES

    def body(av_ref, ag_ref, w_ref, o_ref, pad_ref):
        _fill_front_padded(pad_ref, av_ref[...] * _sigmoid(ag_ref[...]), t)
        for r in range(t // cr):
            o_ref[r * cr:(r + 1) * cr, :] = _conv_rows(pad_ref, w_ref, CONV_A_WIDTH, r * cr, cr)

    col = lambda off: pl.BlockSpec((t, LANES), lambda j: (0, j + off))
    return _pc(body, name="ev_a_conv", grid=(nb,),
               in_specs=[col(0), col(nb), pl.BlockSpec((CONV_A_WIDTH, LANES), lambda j: (0, j))],
               out_specs=col(0), out_shape=S((t, D_A), f32),
               scratch_shapes=[pltpu.VMEM((t + CONV_LEAD, LANES), f32)],
               compiler_params=_cparams(("arbitrary",)))(p, p, conv_a)


def _ln_silu(uc, g, b):
    mu = jnp.mean(uc, axis=-1, keepdims=True)
    xc = uc - mu
    var = jnp.mean(xc * xc, axis=-1, keepdims=True)
    y = xc * lax.rsqrt(var + LN_EPS) * g + b
    return y * _sigmoid(y)


def _ev_a_norm(uc, g, b):
    t, d = uc.shape
    rb = _row_block(t)

    def body(u_ref, g_ref, b_ref, o_ref):
        o_ref[...] = _ln_silu(u_ref[...], g_ref[...], b_ref[...]).astype(bf16)

    row = pl.BlockSpec((rb, d), lambda i: (i, 0))
    return _pc(body, name="ev_a_norm", grid=(t // rb,), in_specs=[row, _full((1, d)), _full((1, d))],
               out_specs=row, out_shape=S((t, d), bf16), compiler_params=_cparams(("arbitrary",)))(uc, g, b)


def _ev_a_norm_bwd(dy, uc, g, b):
    t, d = uc.shape
    rb = _row_block8(t)

    def body(dy_ref, u_ref, g_ref, b_ref, du_ref, dg_ref, db_ref):
        @pl.when(pl.program_id(0) == 0)
        def _():
            dg_ref[...] = jnp.zeros_like(dg_ref)
            db_ref[...] = jnp.zeros_like(db_ref)
        _, vjp = jax.vjp(_ln_silu, u_ref[...], g_ref[...], b_ref[...])
        du, dg, db = vjp(dy_ref[...])
        du_ref[...] = du
        dg_ref[...] += dg
        db_ref[...] += db

    row = pl.BlockSpec((rb, d), lambda i: (i, 0))
    return _pc(body, name="ev_a_norm_bwd", grid=(t // rb,), in_specs=[row, row, _full((1, d)), _full((1, d))],
               out_specs=(row, _full((1, d)), _full((1, d))),
               out_shape=(S((t, d), f32), S((1, d), f32), S((1, d), f32)),
               compiler_params=_cparams(("arbitrary",)))(dy, uc, g, b)


def _ev_a_conv_bwd(duc, p, conv_a):
    t = p.shape[0]
    cr = _row_block8(t)
    nb = D_A // LANES

    def body(dy_ref, av_ref, ag_ref, w_ref, dav_ref, dag_ref, dw_ref, pad_ref, padb_ref):
        _fill_front_padded(pad_ref, av_ref[...] * _sigmoid(ag_ref[...]), t)
        _fill_back_padded(padb_ref, dy_ref[...], t)
        dw = None
        for r in range(t // cr):
            rows = slice(r * cr, (r + 1) * cr)
            du = _conv_t_rows(padb_ref, w_ref, CONV_A_WIDTH, r * cr, cr)
            avr = av_ref[rows, :]
            sgr = _sigmoid(ag_ref[rows, :])
            dav_ref[rows, :] = du * sgr
            dag_ref[rows, :] = du * avr * sgr * (1.0 - sgr)
            dw = _acc_list(dw, _conv_dw_rows(dy_ref[rows, :], pad_ref, CONV_A_WIDTH, r * cr, cr))
        for j in range(CONV_A_WIDTH):
            dw_ref[j:j + 1, :] = dw[j]

    col = lambda off: pl.BlockSpec((t, LANES), lambda j: (0, j + off))
    wsp = pl.BlockSpec((CONV_A_WIDTH, LANES), lambda j: (0, j))
    return _pc(body, name="ev_a_conv_bwd", grid=(nb,), in_specs=[col(0), col(0), col(nb), wsp],
               out_specs=(col(0), col(0), wsp),
               out_shape=(S((t, D_A), f32), S((t, D_A), f32), S((CONV_A_WIDTH, D_A), f32)),
               scratch_shapes=[pltpu.VMEM((t + CONV_LEAD, LANES), f32), pltpu.VMEM((t + CONV_LEAD, LANES), f32)],
               compiler_params=_cparams(("arbitrary",)))(duc, p, p, conv_a)


def _ev_b(p, conv_b):
    t = p.shape[0]
    cr = _row_block8(t)
    nb = D_A // LANES

    def body(gb_ref, gc_ref, xi_ref, w_ref, o_ref, pad_ref, stage_ref):
        _fill_front_padded(pad_ref, gc_ref[...] * xi_ref[...], t)
        for r in range(t // cr):
            rows = slice(r * cr, (r + 1) * cr)
            stage_ref[rows, :] = gb_ref[rows, :] * _conv_rows(pad_ref, w_ref, CONV_B_WIDTH, r * cr, cr)
        o_ref[...] = stage_ref[...].astype(bf16)

    col = lambda off: pl.BlockSpec((t, LANES), lambda j: (0, j + off))
    return _pc(body, name="ev_b", grid=(nb,),
               in_specs=[col(2 * nb), col(3 * nb), col(4 * nb), pl.BlockSpec((CONV_B_WIDTH, LANES), lambda j: (0, j))],
               out_specs=col(0), out_shape=S((t, D_A), bf16),
               scratch_shapes=[pltpu.VMEM((t + CONV_LEAD, LANES), f32), pltpu.VMEM((t, LANES), f32)],
               compiler_params=_cparams(("arbitrary",)))(p, p, p, conv_b)


def _ev_b_bwd(dy, p, conv_b):
    t = p.shape[0]
    cr = _row_block8(t)
    nb = D_A // LANES

    def body(dy_ref, gb_ref, gc_ref, xi_ref, w_ref, dgb_ref, dgc_ref, dxi_ref, dw_ref, pad_ref, padb_ref):
        _fill_front_padded(pad_ref, gc_ref[...] * xi_ref[...], t)
        _fill_back_padded(padb_ref, dy_ref[...] * gb_ref[...], t)
        dw = None
        for r in range(t // cr):
            rows = slice(r * cr, (r + 1) * cr)
            dgb_ref[rows, :] = dy_ref[rows, :] * _conv_rows(pad_ref, w_ref, CONV_B_WIDTH, r * cr, cr)
            dcx = _conv_t_rows(padb_ref, w_ref, CONV_B_WIDTH, r * cr, cr)
            dgc_ref[rows, :] = dcx * xi_ref[rows, :]
            dxi_ref[rows, :] = dcx * gc_ref[rows, :]
            dw = _acc_list(dw, _conv_dw_rows(padb_ref[rows, :], pad_ref, CONV_B_WIDTH, r * cr, cr))
        for j in range(CONV_B_WIDTH):
            dw_ref[j:j + 1, :] = dw[j]

    col = lambda off: pl.BlockSpec((t, LANES), lambda j: (0, j + off))
    wsp = pl.BlockSpec((CONV_B_WIDTH, LANES), lambda j: (0, j))
    return _pc(body, name="ev_b_bwd", grid=(nb,), in_specs=[col(nb), col(2 * nb), col(3 * nb), col(4 * nb), wsp],
               out_specs=(col(0), col(0), col(0), wsp),
               out_shape=(S((t, D_A), f32), S((t, D_A), f32), S((t, D_A), f32), S((CONV_B_WIDTH, D_A), f32)),
               scratch_shapes=[pltpu.VMEM((t + CONV_LEAD, LANES), f32), pltpu.VMEM((t + CONV_LEAD, LANES), f32)],
               compiler_params=_cparams(("arbitrary",)))(dy, p, p, p, conv_b)


def _ffn_mid(u, conv_w, conv_b, name):
    t = u.shape[0]
    cr = _row_block8(t)
    nb = D_FF // LANES

    def body(gt_ref, vl_ref, w_ref, b_ref, o_ref, pad_ref, stage_ref):
        _fill_front_padded(pad_ref, gt_ref[...], t)
        for r in range(t // cr):
            rows = slice(r * cr, (r + 1) * cr)
            gc = _conv_rows(pad_ref, w_ref, FF_CONV_WIDTH, r * cr, cr) + b_ref[...]
            stage_ref[rows, :] = gc * _sigmoid(gc) * vl_ref[rows, :]
        o_ref[...] = stage_ref[...].astype(bf16)

    col = lambda off: pl.BlockSpec((t, LANES), lambda j: (0, j + off))
    return _pc(body, name=name, grid=(nb,),
               in_specs=[col(0), col(nb), pl.BlockSpec((FF_CONV_WIDTH, LANES), lambda j: (0, j)),
                         pl.BlockSpec((1, LANES), lambda j: (0, j))],
               out_specs=col(0), out_shape=S((t, D_FF), bf16),
               scratch_shapes=[pltpu.VMEM((t + CONV_LEAD, LANES), f32), pltpu.VMEM((t, LANES), f32)],
               compiler_params=_cparams(("arbitrary",)))(u, u, conv_w, conv_b.reshape(1, D_FF))


def _ffn_mid_bwd(dz, u, conv_w, conv_b, name):
    t = u.shape[0]
    cr = _row_block8(t)
    nb = D_FF // LANES

    def body(dz_ref, gt_ref, vl_ref, w_ref, b_ref, du_ref, dw_ref, db_ref, pad_ref, padb_ref):
        s = pl.program_id(1)
        _fill_front_padded(pad_ref, gt_ref[...], t)

        @pl.when(s == 0)
        def _():
            for r in range(t // cr):
                rows = slice(r * cr, (r + 1) * cr)
                gc = _conv_rows(pad_ref, w_ref, FF_CONV_WIDTH, r * cr, cr) + b_ref[...]
                sg = _sigmoid(gc)
                padb_ref[rows, :] = dz_ref[rows, :] * vl_ref[rows, :] * sg * (1.0 + gc * (1.0 - sg))
            padb_ref[t:t + CONV_LEAD, :] = jnp.zeros((CONV_LEAD, LANES), f32)
            dw, db = None, None
            for r in range(t // cr):
                rows = slice(r * cr, (r + 1) * cr)
                du_ref[rows, :] = _conv_t_rows(padb_ref, w_ref, FF_CONV_WIDTH, r * cr, cr)
                dgc = padb_ref[rows, :]
                dw = _acc_list(dw, _conv_dw_rows(dgc, pad_ref, FF_CONV_WIDTH, r * cr, cr))
                pb = jnp.sum(dgc, axis=0, keepdims=True)
                db = pb if db is None else db + pb
            for j in range(FF_CONV_WIDTH):
                dw_ref[j:j + 1, :] = dw[j]
            db_ref[...] = db

        @pl.when(s == 1)
        def _():
            for r in range(t // cr):
                rows = slice(r * cr, (r + 1) * cr)
                gc = _conv_rows(pad_ref, w_ref, FF_CONV_WIDTH, r * cr, cr) + b_ref[...]
                du_ref[rows, :] = dz_ref[rows, :] * gc * _sigmoid(gc)

    col = lambda off: pl.BlockSpec((t, LANES), lambda j, s: (0, j + off))
    wsp = pl.BlockSpec((FF_CONV_WIDTH, LANES), lambda j, s: (0, j))
    bsp = pl.BlockSpec((1, LANES), lambda j, s: (0, j))
    return _pc(body, name=name, grid=(nb, 2), in_specs=[col(0), col(0), col(nb), wsp, bsp],
               out_specs=(pl.BlockSpec((t, LANES), lambda j, s: (0, s * nb + j)), wsp, bsp),
               out_shape=(S((t, 2 * D_FF), f32), S((FF_CONV_WIDTH, D_FF), f32), S((1, D_FF), f32)),
               scratch_shapes=[pltpu.VMEM((t + CONV_LEAD, LANES), f32), pltpu.VMEM((t + CONV_LEAD, LANES), f32)],
               compiler_params=_cparams(("arbitrary", "arbitrary")))(dz, u, u, conv_w, conv_b.reshape(1, D_FF))


def _swap_halves(x):
    w = x.shape[1]
    lane = lax.broadcasted_iota(jnp.int32, x.shape, 1) % HEAD_DIM
    return jnp.where(lane < HEAD_DIM // 2, pltpu.roll(x, w - HEAD_DIM // 2, axis=1), pltpu.roll(x, HEAD_DIM // 2, axis=1))


def _rope_pack(patt, c64, s64):
    t = patt.shape[0]
    tp = t + ATT_PAD

    def body(p_ref, c_ref, s_ref, q_ref, k_ref, v_ref):
        c, s = c_ref[...], s_ref[...]

        def rope(x, nh):
            cc = jnp.concatenate([c] * nh, axis=1)
            ss = jnp.concatenate([s] * nh, axis=1)
            return x * cc + _swap_halves(x) * ss

        for ref, val in ((q_ref, rope(p_ref[:, 0:D_ATT], N_Q_HEADS)),
                         (k_ref, rope(p_ref[:, D_ATT:D_ATT + D_KV], N_KV_HEADS)),
                         (v_ref, p_ref[:, D_ATT + D_KV:ATT_COLS])):
            ref[0:ATT_PAD, :] = jnp.zeros((ATT_PAD, val.shape[1]), bf16)
            ref[ATT_PAD:tp, :] = val.astype(bf16)

    return _pc(body, name="rope_pack", in_specs=[_full((t, ATT_COLS)), _full((t, HEAD_DIM)), _full((t, HEAD_DIM))],
               out_specs=(_full((tp, D_ATT)), _full((tp, D_KV)), _full((tp, D_KV))), grid=(1,),
               out_shape=(S((tp, D_ATT), bf16), S((tp, D_KV), bf16), S((tp, D_KV), bf16)),
               compiler_params=_cparams(("arbitrary",)))(patt, c64, s64)


def _rope_bwd(dqp, dkp, dvp, c64, s64):
    tp = dqp.shape[0]
    t = tp - ATT_PAD

    def body(dq_ref, dk_ref, dv_ref, c_ref, s_ref, o_ref):
        c, s = c_ref[...], s_ref[...]

        def unrope(dy, nh):
            cc = jnp.concatenate([c] * nh, axis=1)
            ss = jnp.concatenate([s] * nh, axis=1)
            return dy * cc + _swap_halves(dy * ss)

        o_ref[:, 0:D_ATT] = unrope(dq_ref[ATT_PAD:tp, :], N_Q_HEADS)
        o_ref[:, D_ATT:D_ATT + D_KV] = unrope(dk_ref[ATT_PAD:tp, :], N_KV_HEADS)
        o_ref[:, D_ATT + D_KV:ATT_COLS] = dv_ref[ATT_PAD:tp, :]

    return _pc(body, name="rope_bwd", grid=(1,),
               in_specs=[_full((tp, D_ATT)), _full((tp, D_KV)), _full((tp, D_KV)), _full((t, HEAD_DIM)), _full((t, HEAD_DIM))],
               out_specs=_full((t, ATT_COLS)), out_shape=S((t, ATT_COLS), f32),
               compiler_params=_cparams(("arbitrary",)))(dqp, dkp, dvp, c64, s64)


def _attn_masks(n):
    rows = GQA_GROUP * BLOCK
    ri = lax.broadcasted_iota(jnp.int32, (rows, BLOCK), 0) % BLOCK
    ci = lax.broadcasted_iota(jnp.int32, (rows, BLOCK), 1)
    m_cur = (ci <= ri) & (ci >= jnp.where(n >= 1, 0, ATT_PAD))
    m_prev = ci > ri + jnp.where(n >= 2, 0, BLOCK)
    m_meta = ci >= jnp.where(n >= 1, ATT_PAD, BLOCK)
    return m_cur, m_prev, m_meta


def _attn_probs(qg, kc, kp, km, masks, skv):
    def scores(k, m):
        s = lax.dot_general(qg, k, _DIMS["nt"], preferred_element_type=f32) * ATT_SCALE
        return jnp.where(m, s, NEG_INF)
    s_c, s_p, s_m = scores(kc, masks[0]), scores(kp, masks[1]), scores(km, masks[2])
    mx = jnp.maximum(jnp.maximum(jnp.max(s_c, axis=-1, keepdims=True), jnp.max(s_p, axis=-1, keepdims=True)),
                     jnp.maximum(jnp.max(s_m, axis=-1, keepdims=True), skv))
    e_c, e_p, e_m, e_s = jnp.exp(s_c - mx), jnp.exp(s_p - mx), jnp.exp(s_m - mx), jnp.exp(skv - mx)
    den = (jnp.sum(e_c, axis=-1, keepdims=True) + jnp.sum(e_p, axis=-1, keepdims=True)
           + jnp.sum(e_m, axis=-1, keepdims=True) + e_s)
    inv = 1.0 / den
    return e_c * inv, e_p * inv, e_m * inv, e_s * inv


def _sink_rows(sk_ref, g):
    hrow = lax.broadcasted_iota(jnp.int32, (GQA_GROUP * BLOCK, 1), 0) // BLOCK
    skv = jnp.zeros((GQA_GROUP * BLOCK, 1), f32)
    for hh in range(GQA_GROUP):
        skv = jnp.where(hrow == hh, sk_ref[0, GQA_GROUP * g + hh], skv)
    return skv, hrow


def _stack_heads(ref, g):
    return jnp.concatenate([ref[:, (GQA_GROUP * g + hh) * HEAD_DIM:(GQA_GROUP * g + hh + 1) * HEAD_DIM]
                            for hh in range(GQA_GROUP)], axis=0)


def _attn_specs():
    blk = lambda w: pl.BlockSpec((BLOCK, w), lambda n: (n, 0))
    prev = pl.BlockSpec((BLOCK, D_KV), lambda n: (jnp.maximum(n - 1, 0), 0))
    meta = pl.BlockSpec((BLOCK, D_KV), lambda n: (0, 0))
    return blk, prev, meta


def _attn_fwd(qp, kp, vp, sinks):
    tp = qp.shape[0]
    blk, prev, meta = _attn_specs()

    def body(sk_ref, q_ref, kc_ref, kp_ref, km_ref, vc_ref, vp_ref, vm_ref, o_ref):
        masks = _attn_masks(pl.program_id(0))
        for g in range(N_KV_HEADS):
            sl = slice(g * HEAD_DIM, (g + 1) * HEAD_DIM)
            skv, _ = _sink_rows(sk_ref, g)
            p_c, p_p, p_m, _ = _attn_probs(_stack_heads(q_ref, g), kc_ref[:, sl], kp_ref[:, sl], km_ref[:, sl], masks, skv)
            o = (jnp.dot(p_c.astype(bf16), vc_ref[:, sl], preferred_element_type=f32)
                 + jnp.dot(p_p.astype(bf16), vp_ref[:, sl], preferred_element_type=f32)
                 + jnp.dot(p_m.astype(bf16), vm_ref[:, sl], preferred_element_type=f32))
            for hh in range(GQA_GROUP):
                h = GQA_GROUP * g + hh
                o_ref[:, h * HEAD_DIM:(h + 1) * HEAD_DIM] = o[hh * BLOCK:(hh + 1) * BLOCK].astype(bf16)

    return _pc(body, name="attn_fwd", grid=(tp // BLOCK,),
               in_specs=[pl.BlockSpec(memory_space=pltpu.SMEM), blk(D_ATT), blk(D_KV), prev, meta, blk(D_KV), prev, meta],
               out_specs=blk(D_ATT), out_shape=S((tp, D_ATT), bf16),
               compiler_params=_cparams(("arbitrary",)))(sinks, qp, kp, kp, kp, vp, vp, vp)


def _attn_bwd(qp, kp, vp, sinks, dop):
    tp = qp.shape[0]
    blk, prev, meta = _attn_specs()

    def body(sk_ref, q_ref, kc_ref, kp_ref, km_ref, vc_ref, vp_ref, vm_ref, do_ref, dq_ref, dk_ref, dv_ref, dsk_ref):
        n = pl.program_id(0)

        @pl.when(n == 0)
        def _():
            dk_ref[...] = jnp.zeros_like(dk_ref)
            dv_ref[...] = jnp.zeros_like(dv_ref)
            dsk_ref[...] = jnp.zeros_like(dsk_ref)
        masks = _attn_masks(n)
        cur = pl.ds(pl.multiple_of(n * BLOCK, BLOCK), BLOCK)
        prv = pl.ds(pl.multiple_of(jnp.maximum(n - 1, 0) * BLOCK, BLOCK), BLOCK)
        lane = lax.broadcasted_iota(jnp.int32, (1, LANES), 1)
        dsk = jnp.zeros((1, LANES), f32)
        for g in range(N_KV_HEADS):
            sl = slice(g * HEAD_DIM, (g + 1) * HEAD_DIM)
            skv, hrow = _sink_rows(sk_ref, g)
            qg = _stack_heads(q_ref, g)
            dog = _stack_heads(do_ref, g)
            ks = (kc_ref[:, sl], kp_ref[:, sl], km_ref[:, sl])
            vs = (vc_ref[:, sl], vp_ref[:, sl], vm_ref[:, sl])
            probs = _attn_probs(qg, ks[0], ks[1], ks[2], masks, skv)
            dps = [lax.dot_general(dog, v, _DIMS["nt"], preferred_element_type=f32) for v in vs]
            delta = sum(jnp.sum(p * dp, axis=-1, keepdims=True) for p, dp in zip(probs[:3], dps))
            dss = [(p * (dp - delta) * ATT_SCALE).astype(bf16) for p, dp in zip(probs[:3], dps)]
            dq = sum(jnp.dot(ds, k, preferred_element_type=f32) for ds, k in zip(dss, ks))
            for hh in range(GQA_GROUP):
                h = GQA_GROUP * g + hh
                dq_ref[:, h * HEAD_DIM:(h + 1) * HEAD_DIM] = dq[hh * BLOCK:(hh + 1) * BLOCK]
                dsk = dsk + jnp.where(lane == h, -jnp.sum(jnp.where(hrow == hh, probs[3] * delta, 0.0)), 0.0)
            for rows, p, ds in zip((cur, prv, slice(0, BLOCK)), probs[:3], dss):
                dv_ref[rows, sl] += lax.dot_general(p.astype(bf16), dog, _DIMS["tn"], preferred_element_type=f32)
                dk_ref[rows, sl] += lax.dot_general(ds, qg, _DIMS["tn"], preferred_element_type=f32)
        dsk_ref[...] += dsk

    return _pc(body, name="attn_bwd", grid=(tp // BLOCK,),
               in_specs=[pl.BlockSpec(memory_space=pltpu.SMEM), blk(D_ATT), blk(D_KV), prev, meta, blk(D_KV), prev, meta,
                         blk(D_ATT)],
               out_specs=(blk(D_ATT), _full((tp, D_KV)), _full((tp, D_KV)), _full((1, LANES))),
               out_shape=(S((tp, D_ATT), f32), S((tp, D_KV), f32), S((tp, D_KV), f32), S((1, LANES), f32)),
               compiler_params=_cparams(("arbitrary",)))(sinks, qp, kp, kp, kp, vp, vp, vp, dop)


def _seg(x, bm):
    return jnp.dot(x, bm, precision=HIGHEST, preferred_element_type=f32)


def _softplus(y):
    return jnp.maximum(y, 0.0) + jnp.log(1.0 + jnp.exp(-jnp.abs(y)))


def _prep_fn(xr, xk, xwd, xad, xgd, w0, w2, a0, a2, g2, k_k, k_a, bm):
    xw = w0 + jnp.dot(jnp.tanh(xwd), w2, preferred_element_type=f32)
    decay = jnp.exp(-jnp.exp(-_softplus(-xw) - 0.5))
    alpha = _sigmoid(a0 + jnp.dot(xad, a2, preferred_element_type=f32))
    g = jnp.dot(_sigmoid(xgd), g2, preferred_element_type=f32)
    kk = xk * k_k
    kkn = kk / jnp.maximum(jnp.sqrt(_seg(kk * kk, bm)), 1e-12)
    k2 = xk * (1.0 + (alpha - 1.0) * k_a)
    return decay, k2, -kkn, kkn * alpha, g


def _split_cols(x):
    o1, o2, o3 = 3 * D_R, 3 * D_R + LORA_W, 3 * D_R + LORA_W + LORA_A
    return x[:, 0:D_R], x[:, D_R:2 * D_R], x[:, 2 * D_R:o1], x[:, o1:o2], x[:, o2:o3], x[:, o3:RWKV_COLS]


def _shifted(sh_ref, x, halo, first, rb):
    sh_ref[0:SUBLANES, :] = jnp.where(first, 0.0, halo)
    sh_ref[SUBLANES:SUBLANES + rb, :] = x
    return sh_ref[SUBLANES - 1:SUBLANES - 1 + rb, :]


_PREP_PARAMS = ("od_w0", "od_w2", "od_a0", "od_a2", "od_g2", "od_k_k", "od_k_a")


def _rwkv_prep(pr, mu, params, bm):
    t = pr.shape[0]
    rb = _row_block8(t)
    hb = rb // SUBLANES

    def body(pr_ref, halo_ref, mu_ref, w0, w2, a0, a2, g2, kk_ref, ka_ref, bm_ref, *outs_sh):
        outs, sh_ref = outs_sh[:-1], outs_sh[-1]
        x = pr_ref[...]
        prev = _shifted(sh_ref, x, halo_ref[...], pl.program_id(0) == 0, rb)
        xr, xk, xv, xwd, xad, xgd = _split_cols(x + (prev - x) * mu_ref[...])
        bmv = bm_ref[...]
        decay, k2, a_s, b_s, g = _prep_fn(xr, xk, xwd, xad, xgd, w0[...], w2[...], a0[...], a2[...], g2[...],
                                          kk_ref[...], ka_ref[...], bmv)
        vals = (xr, xv, decay, k2, a_s, b_s, decay * xr, _seg(b_s * xr, bmv), _seg(k2 * xr, bmv), g)
        for ref, val in zip(outs, vals):
            ref[...] = val

    row = pl.BlockSpec((rb, RWKV_COLS), lambda i: (i, 0))
    halo = pl.BlockSpec((SUBLANES, RWKV_COLS), lambda i: (jnp.maximum(i * hb - 1, 0), 0))
    orow = pl.BlockSpec((rb, D_R), lambda i: (i, 0))
    return _pc(body, name="rwkv_prep", grid=(t // rb,),
               in_specs=[row, halo, _full((1, RWKV_COLS))] + [_full(p.shape) for p in params] + [_full(bm.shape)],
               out_specs=(orow,) * 10, out_shape=(S((t, D_R), f32),) * 10,
               scratch_shapes=[pltpu.VMEM((rb + SUBLANES, RWKV_COLS), f32)],
               compiler_params=_cparams(("arbitrary",)))(pr, pr, mu, *params, bm)


def _rwkv_prep_bwd(pr, mu, params, bm, cts):
    t = pr.shape[0]
    rb = _row_block8(t)
    hb = rb // SUBLANES
    counts = [len(c) for c in cts]
    flat = [a for c in cts for a in c]

    def body(pr_ref, halo_ref, mu_ref, w0, w2, a0, a2, g2, kk_ref, ka_ref, bm_ref, *rest):
        ct_refs, rest = rest[:len(flat)], rest[len(flat):]
        dx_ref, dmu_ref = rest[0], rest[1]
        dpar_refs, sh_ref = rest[2:9], rest[9]

        @pl.when(pl.program_id(0) == 0)
        def _():
            dmu_ref[...] = jnp.zeros_like(dmu_ref)
            for r in dpar_refs:
                r[...] = jnp.zeros_like(r)
        sums, pos = [], 0
        for c in counts:
            sums.append(sum(r[...] for r in ct_refs[pos:pos + c]))
            pos += c
        x = pr_ref[...]
        prev = _shifted(sh_ref, x, halo_ref[...], pl.program_id(0) == 0, rb)
        xr, xk, xv, xwd, xad, xgd = _split_cols(x + (prev - x) * mu_ref[...])
        bmv = bm_ref[...]
        _, vjp = jax.vjp(lambda *a: _prep_fn(*a, bmv), xr, xk, xwd, xad, xgd, w0[...], w2[...], a0[...], a2[...],
                         g2[...], kk_ref[...], ka_ref[...])
        grads = vjp(tuple(sums[:5]))
        dxr, dxk, dxwd, dxad, dxgd = grads[:5]
        o1, o2, o3 = 3 * D_R, 3 * D_R + LORA_W, 3 * D_R + LORA_W + LORA_A
        dx_ref[:, 0:D_R] = dxr + sums[5]
        dx_ref[:, D_R:2 * D_R] = dxk
        dx_ref[:, 2 * D_R:o1] = sums[6]
        dx_ref[:, o1:o2] = dxwd
        dx_ref[:, o2:o3] = dxad
        dx_ref[:, o3:RWKV_COLS] = dxgd
        dmu_ref[...] += jnp.sum(dx_ref[...] * (prev - x), axis=0, keepdims=True)
        for r, gval in zip(dpar_refs, grads[5:]):
            r[...] += gval

    row = pl.BlockSpec((rb, RWKV_COLS), lambda i: (i, 0))
    halo = pl.BlockSpec((SUBLANES, RWKV_COLS), lambda i: (jnp.maximum(i * hb - 1, 0), 0))
    crow = pl.BlockSpec((rb, D_R), lambda i: (i, 0))
    return _pc(body, name="rwkv_prep_bwd", grid=(t // rb,),
               in_specs=[row, halo, _full((1, RWKV_COLS))] + [_full(p.shape) for p in params] + [_full(bm.shape)]
               + [crow] * len(flat),
               out_specs=(row, _full((1, RWKV_COLS))) + tuple(_full(p.shape) for p in params),
               out_shape=(S((t, RWKV_COLS), f32), S((1, RWKV_COLS), f32)) + tuple(S(p.shape, f32) for p in params),
               scratch_shapes=[pltpu.VMEM((rb + SUBLANES, RWKV_COLS), f32)],
               compiler_params=_cparams(("arbitrary",)))(pr, pr, mu, *params, bm, *flat)


def _shift_bwd(dxs, mu):
    t = dxs.shape[0]
    rb = _row_block8(t)
    hb = rb // SUBLANES
    nblk = t // rb

    def body(dx_ref, halo_ref, mu_ref, o_ref, sh_ref):
        dx = dx_ref[...]
        sh_ref[0:rb, :] = dx
        sh_ref[rb:rb + SUBLANES, :] = jnp.where(pl.program_id(0) == nblk - 1, 0.0, halo_ref[...])
        m = mu_ref[...]
        o_ref[...] = dx * (1.0 - m) + sh_ref[1:1 + rb, :] * m

    row = pl.BlockSpec((rb, RWKV_COLS), lambda i: (i, 0))
    halo = pl.BlockSpec((SUBLANES, RWKV_COLS), lambda i: (jnp.minimum((i + 1) * hb, t // SUBLANES - 1), 0))
    return _pc(body, name="rwkv_shift_bwd", grid=(nblk,), in_specs=[row, halo, _full((1, RWKV_COLS))],
               out_specs=row, out_shape=S((t, RWKV_COLS), f32),
               scratch_shapes=[pltpu.VMEM((rb + SUBLANES, RWKV_COLS), f32)],
               compiler_params=_cparams(("arbitrary",)))(dxs, dxs, mu)


def _post_fn(y, xr, k2, xv, g, lg, lb, rk, bm):
    inv_n = 1.0 / HEAD_DIM
    yc = y - _seg(y, bm) * inv_n
    var = _seg(yc * yc, bm) * inv_n
    yn = yc * lax.rsqrt(var + RWKV_GN_EPS) * lg + lb
    return (yn + _seg(xr * k2 * rk, bm) * xv) * g


def _rwkv_post(y, xr, k2, xv, g, lg, lb, rk, bm):
    t = y.shape[0]
    rb = _row_block8(t)

    def body(y_ref, xr_ref, k2_ref, xv_ref, g_ref, lg_ref, lb_ref, rk_ref, bm_ref, o_ref):
        o_ref[...] = _post_fn(y_ref[...], xr_ref[...], k2_ref[...], xv_ref[...], g_ref[...], lg_ref[...], lb_ref[...],
                              rk_ref[...], bm_ref[...])

    row = pl.BlockSpec((rb, D_R), lambda i: (i, 0))
    vec = _full((1, D_R))
    return _pc(body, name="rwkv_post", grid=(t // rb,), in_specs=[row] * 5 + [vec] * 3 + [_full(bm.shape)],
               out_specs=row, out_shape=S((t, D_R), f32),
               compiler_params=_cparams(("arbitrary",)))(y, xr, k2, xv, g, lg, lb, rk, bm)


def _rwkv_post_bwd(dy1, y, xr, k2, xv, g, lg, lb, rk, bm):
    t = y.shape[0]
    rb = _row_block8(t)

    def body(dy_ref, y_ref, xr_ref, k2_ref, xv_ref, g_ref, lg_ref, lb_ref, rk_ref, bm_ref, *outs):
        @pl.when(pl.program_id(0) == 0)
        def _():
            for r in outs[5:]:
                r[...] = jnp.zeros_like(r)
        bmv = bm_ref[...]
        _, vjp = jax.vjp(lambda *a: _post_fn(*a, bmv), y_ref[...], xr_ref[...], k2_ref[...], xv_ref[...], g_ref[...],
                         lg_ref[...], lb_ref[...], rk_ref[...])
        grads = vjp(dy_ref[...])
        for r, gval in zip(outs[:5], grads[:5]):
            r[...] = gval
        for r, gval in zip(outs[5:], grads[5:]):
            r[...] += gval

    row = pl.BlockSpec((rb, D_R), lambda i: (i, 0))
    vec = _full((1, D_R))
    return _pc(body, name="rwkv_post_bwd", grid=(t // rb,),
               in_specs=[pl.BlockSpec((rb, D_R), lambda i: (i, 1))] + [row] * 5 + [vec] * 3 + [_full(bm.shape)],
               out_specs=(row,) * 5 + (vec,) * 3, out_shape=(S((t, D_R), f32),) * 5 + (S((1, D_R), f32),) * 3,
               compiler_params=_cparams(("arbitrary",)))(dy1, y, xr, k2, xv, g, lg, lb, rk, bm)


def _seg2(x, bb):
    hi = x.astype(bf16)
    lo = (x - hi.astype(f32)).astype(bf16)
    return jnp.dot(jnp.concatenate([hi, lo], axis=1), bb, preferred_element_type=f32)


def _row4(rows, j):
    return jnp.concatenate([jnp.broadcast_to(rows[j:j + 1, p * LANES:(p + 1) * LANES], (HEAD_DIM, LANES))
                            for p in range(4)], axis=0)


def _scan_consts():
    lane_group = jnp.arange(LANES) // HEAD_DIM
    b128 = (lane_group[:, None] == lane_group[None, :]).astype(bf16)
    bb = jnp.concatenate([b128, b128], axis=0)
    qsel = (jnp.arange(PAIR_ROWS)[:, None] % HEAD_DIM == jnp.arange(LANES)[None, :] % HEAD_DIM).astype(f32)
    return bb, qsel


def _store_cols(acc_ref, o_ref, tc):
    for p in range(4):
        blk = acc_ref[p * HEAD_DIM:(p + 1) * HEAD_DIM, :].T
        o_ref[:, (2 * p) * HEAD_DIM:(2 * p + 1) * HEAD_DIM] = blk[0:tc]
        o_ref[:, (2 * p + 1) * HEAD_DIM:(2 * p + 2) * HEAD_DIM] = blk[HEAD_DIM:HEAD_DIM + tc]


def _wkv_fwd(w, k, v, a, b, wr, br, kr):
    t = w.shape[0]
    tc = SCAN_CHUNK
    bb, qsel = _scan_consts()

    def body(w_ref, k_ref, v_ref, a_ref, b_ref, wr_ref, br_ref, kr_ref, bb_ref, q_ref, y_ref, st_ref, s_scr, yacc):
        @pl.when(pl.program_id(0) == 0)
        def _():
            s_scr[...] = jnp.zeros_like(s_scr)
        bbv, qv = bb_ref[...], q_ref[...]
        lane64 = lax.broadcasted_iota(jnp.int32, (PAIR_ROWS, LANES), 1) % HEAD_DIM

        def group(gi, s):
            base = pl.multiple_of(gi * SUBLANES, SUBLANES)
            w8, k8, v8, a8, b8, wr8, br8, kr8 = (ref[pl.ds(base, SUBLANES), :] for ref in
                                                 (w_ref, k_ref, v_ref, a_ref, b_ref, wr_ref, br_ref, kr_ref))
            for j in range(SUBLANES):
                tt = base + j
                st_ref[tt] = s
                x = jnp.concatenate([s * _row4(a8, j), s * _row4(wr8, j), qv * _row4(v8, j)], axis=0)
                r = _seg2(x, bbv)
                sa, z, vb = r[0:PAIR_ROWS], r[PAIR_ROWS:2 * PAIR_ROWS], r[2 * PAIR_ROWS:3 * PAIR_ROWS]
                ynew = z + sa * _row4(br8, j) + vb * _row4(kr8, j)
                yacc[...] = jnp.where(lane64 == tt, ynew, yacc[...])
                s = s * _row4(w8, j) + sa * _row4(b8, j) + vb * _row4(k8, j)
            return s

        s_scr[...] = lax.fori_loop(0, tc // SUBLANES, group, s_scr[...])
        _store_cols(yacc, y_ref, tc)

    row = pl.BlockSpec((tc, D_R), lambda c: (c, 0))
    return _pc(body, name="wkv_fwd", grid=(t // tc,), in_specs=[row] * 8 + [_full(bb.shape), _full(qsel.shape)],
               out_specs=(row, pl.BlockSpec((tc, PAIR_ROWS, LANES), lambda c: (c, 0, 0))),
               out_shape=(S((t, D_R), f32), S((t, PAIR_ROWS, LANES), f32)),
               scratch_shapes=[pltpu.VMEM((PAIR_ROWS, LANES), f32), pltpu.VMEM((PAIR_ROWS, LANES), f32)],
               compiler_params=_cparams(("arbitrary",)))(w, k, v, a, b, wr, br, kr, bb, qsel)


def _wkv_bwd(sprev, w, k, v, a, b, r, dy):
    t = w.shape[0]
    tc = SCAN_CHUNK
    nc = t // tc
    bb, qsel = _scan_consts()

    def body(st_ref, w_ref, k_ref, v_ref, a_ref, b_ref, r_ref, dy_ref, bb_ref, q_ref,
             dr_ref, dw_ref, dk_ref, dv_ref, da_ref, db_ref, g_scr, dvacc, rows_scr):
        @pl.when(pl.program_id(0) == 0)
        def _():
            g_scr[...] = jnp.zeros_like(g_scr)
        bbv, qv = bb_ref[...], q_ref[...]
        lane64 = lax.broadcasted_iota(jnp.int32, (PAIR_ROWS, LANES), 1) % HEAD_DIM
        outs = (dr_ref, dw_ref, db_ref, dk_ref, da_ref)

        def colsums(slot, j, x):
            for p in range(4):
                rows_scr[slot, j:j + 1, p * LANES:(p + 1) * LANES] = jnp.sum(x[p * HEAD_DIM:(p + 1) * HEAD_DIM], axis=0,
                                                                           keepdims=True)

        def group(i, g):
            base = pl.multiple_of((tc // SUBLANES - 1 - i) * SUBLANES, SUBLANES)
            w8, k8, v8, a8, b8, r8, dy8 = (ref[pl.ds(base, SUBLANES), :] for ref in
                                           (w_ref, k_ref, v_ref, a_ref, b_ref, r_ref, dy_ref))
            for j in reversed(range(SUBLANES)):
                tt = base + j
                sp = st_ref[tt]
                a4, b4, w4, k4 = _row4(a8, j), _row4(b8, j), _row4(w8, j), _row4(k8, j)
                x = jnp.concatenate([sp * a4, qv * _row4(v8, j), qv * _row4(dy8, j)], axis=0)
                rr = _seg2(x, bbv)
                u, vb, dyb = rr[0:PAIR_ROWS], rr[PAIR_ROWS:2 * PAIR_ROWS], rr[2 * PAIR_ROWS:3 * PAIR_ROWS]
                s_t = sp * w4 + u * b4 + vb * k4
                g = g + dyb * _row4(r8, j)
                rr2 = _seg2(jnp.concatenate([g * b4, g * k4], axis=0), bbv)
                du, dvb = rr2[0:PAIR_ROWS], rr2[PAIR_ROWS:2 * PAIR_ROWS]
                for slot, val in enumerate((s_t * dyb, g * sp, g * u, g * vb, sp * du)):
                    colsums(slot, j, val)
                dvacc[...] = jnp.where(lane64 == tt, dvb, dvacc[...])
                g = g * w4 + du * a4
            for slot, ref in enumerate(outs):
                ref[pl.ds(base, SUBLANES), :] = rows_scr[slot]
            return g

        g_scr[...] = lax.fori_loop(0, tc // SUBLANES, group, g_scr[...])
        _store_cols(dvacc, dv_ref, tc)

    row = pl.BlockSpec((tc, D_R), lambda c: (nc - 1 - c, 0))
    return _pc(body, name="wkv_bwd", grid=(nc,),
               in_specs=[pl.BlockSpec((tc, PAIR_ROWS, LANES), lambda c: (nc - 1 - c, 0, 0))] + [row] * 7
               + [_full(bb.shape), _full(qsel.shape)],
               out_specs=(row,) * 6, out_shape=(S((t, D_R), f32),) * 6,
               scratch_shapes=[pltpu.VMEM((PAIR_ROWS, LANES), f32), pltpu.VMEM((PAIR_ROWS, LANES), f32),
                               pltpu.VMEM((5, SUBLANES, D_R), f32)],
               compiler_params=_cparams(("arbitrary",)))(sprev, w, k, v, a, b, r, dy, bb, qsel)


def _rope_tables(t):
    half = HEAD_DIM // 2
    inv = ROPE_THETA ** (-jnp.arange(half, dtype=f32) / half)
    ang = jnp.arange(t, dtype=f32)[:, None] * inv[None, :]
    cos, sin = jnp.cos(ang), jnp.sin(ang)
    return jnp.concatenate([cos, cos], axis=1), jnp.concatenate([-sin, sin], axis=1)


def _head_matrix():
    grp = jnp.arange(D_R) // HEAD_DIM
    return (grp[:, None] == grp[None, :]).astype(f32)


def _ffn_fwd(h, g, w_up_t, conv_w, conv_b, w_down, i):
    hf = _rms_fwd(h, g, f"ffn{i}_norm")
    u = _mm(hf, w_up_t, "nt", f"ffn{i}_up")
    z = _ffn_mid(u, conv_w, conv_b, f"ffn{i}_mid")
    return _mm(z, w_down, "nn", f"ffn{i}_down", res=h), (hf, u, z)


def _ffn_bwd(dh, h, saved, g, w_up_t, conv_w, conv_b, w_down, i):
    hf, u, z = saved
    dz = _mm(dh, w_down, "nt", f"ffn{i}_dz")
    g_down = _mm(z, dh, "tn", f"ffn{i}_gdown")
    du, g_conv, g_convb = _ffn_mid_bwd(dz, u, conv_w, conv_b, f"ffn{i}_mid_bwd")
    g_up_t = _mm(du, hf, "tn", f"ffn{i}_gup")
    dhf = _mm(du, w_up_t, "nn", f"ffn{i}_dhf")
    dh_in, g_norm = _rms_bwd(dhf, h, g, dh, f"ffn{i}_norm_bwd")
    return dh_in, dict(up_t=g_up_t, down=g_down, conv=g_conv, conv_b=g_convb, norm=g_norm)


def _local_step(x, target, W):
    t = N_META + x.shape[0]
    c64, s64 = _rope_tables(t)
    bm = _head_matrix()
    h0 = jnp.concatenate([W["meta_tokens"], x], axis=0)

    hn0 = _rms_fwd(h0, W["norm_mix"][0], "mix0_norm")
    p0 = _mm(hn0, W["ev_w_in_t"], "nt", "ev_in")
    uc = _ev_a_conv(p0, W["ev_conv_a"])
    y0 = jnp.concatenate([_ev_a_norm(uc, W["ev_ln_a_g"], W["ev_ln_a_b"]), _ev_b(p0, W["ev_conv_b"])], axis=1)
    h1 = _mm(y0, W["ev_w_out"], "nn", "ev_out", res=h0)
    h2, ffn0 = _ffn_fwd(h1, W["norm_ffn"][0], W["ff_w_up_t"][0], W["ff_conv"][0], W["ff_conv_b"][0], W["ff_w_down"][0], 0)

    hn1 = _rms_fwd(h2, W["norm_mix"][1], "mix1_norm")
    p1 = _mm(hn1, W["od_w_in_t"], "nt", "od_in")
    pr = p1[:, ATT_COLS:]
    qp, kp, vp = _rope_pack(p1[:, :ATT_COLS], c64, s64)
    op = _attn_fwd(qp, kp, vp, W["od_sinks"])
    prep_params = [W[n] for n in _PREP_PARAMS]
    xr, xv, decay, k2, a_s, b_s, wr, br, kr, gate = _rwkv_prep(pr, W["od_mu"], prep_params, bm)
    ysc, sprev = _wkv_fwd(decay, k2, xv, a_s, b_s, wr, br, kr)
    rk = W["od_r_k"].reshape(1, D_R)
    yr = _rwkv_post(ysc, xr, k2, xv, gate, W["od_lnx_g"], W["od_lnx_b"], rk, bm)
    y1 = jnp.concatenate([op[ATT_PAD:], yr.astype(bf16)], axis=1)
    h3 = _mm(y1, W["od_w_out"], "nn", "od_out", res=h2)
    h4, ffn1 = _ffn_fwd(h3, W["norm_ffn"][1], W["ff_w_up_t"][1], W["ff_conv"][1], W["ff_conv_b"][1], W["ff_w_down"][1], 1)

    tgt = jnp.concatenate([jnp.zeros((N_META, D_MODEL), f32), target], axis=0)
    loss, dh4, g_norm_final = _final_loss(h4, W["norm_final"], tgt)

    dh3, gf1 = _ffn_bwd(dh4, h3, ffn1, W["norm_ffn"][1], W["ff_w_up_t"][1], W["ff_conv"][1], W["ff_conv_b"][1],
                        W["ff_w_down"][1], 1)
    dy1 = _mm(dh3, W["od_w_out"], "nt", "od_dy")
    g_od_w_out = _mm(y1, dh3, "tn", "od_gout")
    dysc, dxr_p, dk2_p, dxv_p, dgate, g_lnx_g, g_lnx_b, g_rk = _rwkv_post_bwd(
        dy1, ysc, xr, k2, xv, gate, W["od_lnx_g"], W["od_lnx_b"], rk, bm)
    dr, dw, dk, dv, da, db = _wkv_bwd(sprev, decay, k2, xv, a_s, b_s, xr, dysc)
    prep_grads = _rwkv_prep_bwd(pr, W["od_mu"], prep_params, bm,
                                [[dw], [dk, dk2_p], [da], [db], [dgate], [dr, dxr_p], [dv, dxv_p]])
    dxs, g_mu = prep_grads[0], prep_grads[1]
    dpr = _shift_bwd(dxs, W["od_mu"])
    dop = jnp.concatenate([jnp.zeros((ATT_PAD, D_ATT), f32), dy1[:, :D_ATT]], axis=0).astype(bf16)
    dqp, dkp, dvp, dsk = _attn_bwd(qp, kp, vp, W["od_sinks"], dop)
    dp1 = jnp.concatenate([_rope_bwd(dqp, dkp, dvp, c64, s64), dpr], axis=1)
    g_od_w_in_t = _mm(dp1, hn1, "tn", "od_gin")
    dhn1 = _mm(dp1, W["od_w_in_t"], "nn", "od_dhn")
    dh2, g_norm_mix1 = _rms_bwd(dhn1, h2, W["norm_mix"][1], dh3, "mix1_norm_bwd")

    dh1, gf0 = _ffn_bwd(dh2, h1, ffn0, W["norm_ffn"][0], W["ff_w_up_t"][0], W["ff_conv"][0], W["ff_conv_b"][0],
                        W["ff_w_down"][0], 0)
    dy0 = _mm(dh1, W["ev_w_out"], "nt", "ev_dy")
    g_ev_w_out = _mm(y0, dh1, "tn", "ev_gout")
    duc, g_ln_g, g_ln_b = _ev_a_norm_bwd(dy0, uc, W["ev_ln_a_g"], W["ev_ln_a_b"])
    dav, dag, g_conv_a = _ev_a_conv_bwd(duc, p0, W["ev_conv_a"])
    dgb, dgc, dxi, g_conv_b = _ev_b_bwd(dy0, p0, W["ev_conv_b"])
    dp0 = jnp.concatenate([dav, dag, dgb, dgc, dxi], axis=1)
    g_ev_w_in_t = _mm(dp0, hn0, "tn", "ev_gin")
    dhn0 = _mm(dp0, W["ev_w_in_t"], "nn", "ev_dhn")
    dh0, g_norm_mix0 = _rms_bwd(dhn0, h0, W["norm_mix"][0], dh1, "mix0_norm_bwd")

    G = dict(
        meta_tokens=dh0[:N_META], norm_mix=jnp.concatenate([g_norm_mix0, g_norm_mix1], axis=0),
        norm_ffn=jnp.concatenate([gf0["norm"], gf1["norm"]], axis=0), norm_final=g_norm_final.reshape(D_MODEL),
        ev_w_in_t=g_ev_w_in_t, ev_conv_a=g_conv_a, ev_ln_a_g=g_ln_g, ev_ln_a_b=g_ln_b, ev_conv_b=g_conv_b,
        ev_w_out=g_ev_w_out, od_w_in_t=g_od_w_in_t, od_sinks=dsk[:, :N_Q_HEADS], od_mu=g_mu,
        od_lnx_g=g_lnx_g, od_lnx_b=g_lnx_b, od_r_k=g_rk.reshape(N_Q_HEADS, HEAD_DIM), od_w_out=g_od_w_out,
        ff_w_up_t=[gf0["up_t"], gf1["up_t"]], ff_w_down=[gf0["down"], gf1["down"]],
        ff_conv=jnp.stack([gf0["conv"], gf1["conv"]]), ff_conv_b=jnp.concatenate([gf0["conv_b"], gf1["conv_b"]], axis=0),
    )
    for name, gval in zip(_PREP_PARAMS, prep_grads[2:]):
        G[name] = gval
    return loss, dh0[N_META:], G


HBM = pl.BlockSpec(memory_space=pl.ANY)


def _mesh_pos():
    return lax.axis_index("x"), lax.axis_index("y"), lax.axis_index("c")


def _dev(px, py, pc):
    return 4 * px + 2 * py + pc


def _all_gather(xs, name):
    n = len(xs)

    def body(*refs):
        x_refs, o_refs = refs[:n], refs[n:2 * n]
        send_sems, recv_sems, local_sems = refs[2 * n:]
        x, y, c = _mesh_pos()
        me, sibling = (x, y, c), (x, y, 1 - c)
        chips = [(1 - x, y), (x, 1 - y), (1 - x, 1 - y)]

        def copy(i, k, block, to, from_input=False):
            dst = o_refs[i].at[_dev(*block)]
            return pltpu.make_async_remote_copy(src_ref=x_refs[i] if from_input else dst, dst_ref=dst,
                                                send_sem=send_sems.at[i, k], recv_sem=recv_sems.at[i, k],
                                                device_id=to, device_id_type=MESH)

        mine = [pltpu.make_async_copy(x_refs[i], o_refs[i].at[_dev(*me)], local_sems.at[i]) for i in range(n)]
        for cp in mine:
            cp.start()
        first = []
        for i in range(n):
            first.append(copy(i, 0, me, sibling, True))
            first += [copy(i, 1 + j, me, (*chip, c), True) for j, chip in enumerate(chips)]
        for cp in first:
            cp.start()
        passed = []
        for j, chip in enumerate(chips):
            for i in range(n):
                copy(i, 1 + j, (*chip, c), me).wait_recv()
                fwd = copy(i, 4 + j, (*chip, c), sibling)
                fwd.start()
                passed.append(fwd)
        for i in range(n):
            copy(i, 0, sibling, me).wait_recv()
            for j, chip in enumerate(chips):
                copy(i, 4 + j, (*chip, 1 - c), me).wait_recv()
        for cp in first + passed:
            cp.wait_send()
        for cp in mine:
            cp.wait()

    return _pc(body, name=name, in_specs=[HBM] * n, out_specs=tuple([HBM] * n),
               out_shape=tuple(S((N_DEV,) + x.shape, x.dtype) for x in xs),
               scratch_shapes=[pltpu.SemaphoreType.DMA((n, 7)), pltpu.SemaphoreType.DMA((n, 7)),
                               pltpu.SemaphoreType.DMA((n,))])(*xs)


def _rs_d2d(gs, name):
    n = len(gs)

    def body(*refs):
        g_refs, o_refs = refs[:n], refs[n:2 * n]
        send_sems, recv_sems = refs[2 * n:]
        x, y, c = _mesh_pos()
        copies = []
        for i in range(n):
            for q in range(4):
                cp = pltpu.make_async_remote_copy(src_ref=g_refs[i].at[2 * q + (1 - c)], dst_ref=o_refs[i].at[q],
                                                  send_sem=send_sems.at[i, q], recv_sem=recv_sems.at[i, q],
                                                  device_id=(x, y, 1 - c), device_id_type=MESH)
                cp.start()
                copies.append(cp)
        for cp in copies:
            cp.wait()

    return _pc(body, name=name, in_specs=[HBM] * n, out_specs=tuple([HBM] * n),
               out_shape=tuple(S((4,) + g.shape[1:], g.dtype) for g in gs),
               scratch_shapes=[pltpu.SemaphoreType.DMA((n, 4)), pltpu.SemaphoreType.DMA((n, 4))])(*gs)


def _rs_add(g, r1, c_vec, name):
    _, r, cols = g.shape
    tr = _divisor_block(r, 8, min(r, 352))

    def body(c_ref, g_ref, r_ref, o_ref):
        o_ref[...] = g_ref[...] + r_ref[...]

    blk = lambda f: pl.BlockSpec((1, tr, cols), f)
    grid_spec = pltpu.PrefetchScalarGridSpec(
        num_scalar_prefetch=1, grid=(4, r // tr),
        in_specs=[blk(lambda q, i, c_ref: (2 * q + c_ref[0], i, 0)), blk(lambda q, i, c_ref: (q, i, 0))],
        out_specs=blk(lambda q, i, c_ref: (q, i, 0)))
    return _pc(body, name=name, grid_spec=grid_spec, out_shape=S((4, r, cols), g.dtype),
               compiler_params=_cparams(("arbitrary", "arbitrary")))(c_vec, g, r1)


def _rs_ici(ps, name):
    n = len(ps)

    def body(*refs):
        p_refs, o_refs = refs[:n], refs[n:2 * n]
        send_sems, recv_sems = refs[2 * n:]
        x, y, c = _mesh_pos()
        chips = [(1 - x, y), (x, 1 - y), (1 - x, 1 - y)]
        copies = []
        for i in range(n):
            for j, (qx, qy) in enumerate(chips):
                cp = pltpu.make_async_remote_copy(src_ref=p_refs[i].at[2 * qx + qy], dst_ref=o_refs[i].at[j],
                                                  send_sem=send_sems.at[i, j], recv_sem=recv_sems.at[i, j],
                                                  device_id=(qx, qy, c), device_id_type=MESH)
                cp.start()
                copies.append(cp)
        for cp in copies:
            cp.wait()

    return _pc(body, name=name, in_specs=[HBM] * n, out_specs=tuple([HBM] * n),
               out_shape=tuple(S((3,) + p.shape[1:], p.dtype) for p in ps),
               scratch_shapes=[pltpu.SemaphoreType.DMA((n, 3)), pltpu.SemaphoreType.DMA((n, 3))])(*ps)


def _rs_final(p, r2, q_vec, name):
    _, r, cols = p.shape
    tr = _divisor_block(r, 8, min(r, 352))

    def body(q_ref, p_ref, a_ref, b_ref, c_ref, o_ref):
        o_ref[...] = ((p_ref[0] + a_ref[0]) + b_ref[0]) + c_ref[0]

    blk = lambda f: pl.BlockSpec((1, tr, cols), f)
    grid_spec = pltpu.PrefetchScalarGridSpec(
        num_scalar_prefetch=1, grid=(r // tr,),
        in_specs=[blk(lambda i, q_ref: (q_ref[0], i, 0))] + [blk(lambda i, q_ref, j=j: (j, i, 0)) for j in range(3)],
        out_specs=pl.BlockSpec((tr, cols), lambda i, q_ref: (i, 0)))
    return _pc(body, name=name, grid_spec=grid_spec, out_shape=S((r, cols), p.dtype),
               compiler_params=_cparams(("arbitrary",)))(q_vec, p, r2, r2, r2)


def _sum_devices(a):
    def body(a_ref, o_ref):
        acc = a_ref[0]
        for d in range(1, N_DEV):
            acc = acc + a_ref[d]
        o_ref[...] = acc

    return _pc(body, name="sum_small_grads", grid=(1,), in_specs=[_full(a.shape)], out_specs=_full(a.shape[1:]),
               out_shape=S(a.shape[1:], a.dtype), compiler_params=_cparams(("arbitrary",)))(a)


def _adamw(w, m, v, g, name):
    shape = w.shape
    w2, m2, v2, g2 = (a.reshape(-1, shape[-1]) for a in (w, m, v, g))
    rows, cols = w2.shape
    tr = rows if rows % SUBLANES else _divisor_block(rows, SUBLANES, max(SUBLANES, min(rows, ADAMW_BLOCK_ELEMS // cols)))
    c1, c2 = 1.0 - ADAM_B1 ** ADAM_STEP, 1.0 - ADAM_B2 ** ADAM_STEP

    def body(w_ref, m_ref, v_ref, g_ref, d_ref, nm_ref, nv_ref):
        gv = g_ref[...]
        nm = ADAM_B1 * m_ref[...] + (1.0 - ADAM_B1) * gv
        nv = ADAM_B2 * v_ref[...] + (1.0 - ADAM_B2) * (gv * gv)
        d_ref[...] = -ADAM_LR * ((nm / c1) / (jnp.sqrt(nv / c2) + ADAM_EPS) + ADAM_WD * w_ref[...])
        nm_ref[...] = nm
        nv_ref[...] = nv

    blk = pl.BlockSpec((tr, cols), lambda i: (i, 0))
    outs = _pc(body, name=name, grid=(rows // tr,), in_specs=[blk] * 4, out_specs=(blk,) * 3,
               out_shape=(S((rows, cols), f32),) * 3, compiler_params=_cparams(("arbitrary",)))(w2, m2, v2, g2)
    return tuple(o.reshape(shape) for o in outs)


_WEIGHTS = ("meta_tokens", "norm_mix", "norm_ffn", "norm_final", "ev_w_in", "ev_conv_a", "ev_ln_a_g", "ev_ln_a_b",
            "ev_conv_b", "ev_w_out", "od_w_in", "od_sinks", "od_mu", "od_w0", "od_w2", "od_a0", "od_a2", "od_g2",
            "od_k_k", "od_k_a", "od_r_k", "od_lnx_g", "od_lnx_b", "od_w_out", "ff_w_up", "ff_conv", "ff_conv_b", "ff_w_down")
_SMALL_SHARDED = (("meta_tokens", 1), ("ev_conv_a", 2), ("ev_conv_b", 2), ("od_mu", 1), ("od_w0", 1), ("od_w2", 2),
                  ("od_a0", 1), ("od_a2", 2), ("od_g2", 2), ("od_k_k", 1), ("od_k_a", 1), ("od_lnx_g", 1),
                  ("od_lnx_b", 1), ("ff_conv", 2))
_SMALL_REPLICATED = ("norm_mix", "norm_ffn", "norm_final", "ev_ln_a_g", "ev_ln_a_b", "od_sinks", "od_r_k", "ff_conv_b")
SLAB_UNIT = SUBLANES * LANES


def _pack(arrs):
    flat = jnp.concatenate([a.reshape(-1).astype(f32) for a in arrs])
    pad = (-flat.shape[0]) % SLAB_UNIT
    return jnp.pad(flat, (0, pad)).reshape(-1, LANES)


def _unpack(flat, shapes):
    out, off = [], 0
    for shp in shapes:
        size = 1
        for s in shp:
            size *= s
        out.append(flat[..., off:off + size].reshape(flat.shape[:-1] + tuple(shp)))
        off += size
    return out


def _full_shape(shape, axis):
    return tuple(N_DEV * s if i == axis else s for i, s in enumerate(shape))


def kernel(x, meta_tokens, norm_mix, norm_ffn, norm_final, ev_w_in, ev_conv_a, ev_ln_a_g, ev_ln_a_b, ev_conv_b, ev_w_out, od_w_in, od_sinks, od_mu, od_w0, od_w2, od_a0, od_a2, od_g2, od_k_k, od_k_a, od_r_k, od_lnx_g, od_lnx_b, od_w_out, ff_w_up, ff_conv, ff_conv_b, ff_w_down, loss_target, m_meta_tokens, m_norm_mix, m_norm_ffn, m_norm_final, m_ev_w_in, m_ev_conv_a, m_ev_ln_a_g, m_ev_ln_a_b, m_ev_conv_b, m_ev_w_out, m_od_w_in, m_od_sinks, m_od_mu, m_od_w0, m_od_w2, m_od_a0, m_od_a2, m_od_g2, m_od_k_k, m_od_k_a, m_od_r_k, m_od_lnx_g, m_od_lnx_b, m_od_w_out, m_ff_w_up, m_ff_conv, m_ff_conv_b, m_ff_w_down, v_meta_tokens, v_norm_mix, v_norm_ffn, v_norm_final, v_ev_w_in, v_ev_conv_a, v_ev_ln_a_g, v_ev_ln_a_b, v_ev_conv_b, v_ev_w_out, v_od_w_in, v_od_sinks, v_od_mu, v_od_w0, v_od_w2, v_od_a0, v_od_a2, v_od_g2, v_od_k_k, v_od_k_a, v_od_r_k, v_od_lnx_g, v_od_lnx_b, v_od_w_out, v_ff_w_up, v_ff_conv, v_ff_conv_b, v_ff_w_down):
    A = dict(locals())
    px, py, pc = _mesh_pos()
    me = _dev(px, py, pc)
    c_vec = jnp.reshape(pc, (1,)).astype(jnp.int32)
    q_vec = jnp.reshape(2 * px + py, (1,)).astype(jnp.int32)

    big = [ev_w_in[0].T, ev_w_out[0], od_w_in[0].T, od_w_out[0], ff_w_up[0].T, ff_w_up[1].T, ff_w_down[0], ff_w_down[1]]
    small_shapes = [A[n].shape for n, _ in _SMALL_SHARDED]
    gathered = _all_gather([b.astype(bf16) for b in big] + [_pack([A[n] for n, _ in _SMALL_SHARDED])], "gather_params")
    fb = [g.reshape(N_DEV * g.shape[1], g.shape[2]) for g in gathered[:-1]]
    W = dict(ev_w_in_t=fb[0], ev_w_out=fb[1], od_w_in_t=fb[2], od_w_out=fb[3], ff_w_up_t=[fb[4], fb[5]], ff_w_down=[fb[6], fb[7]])
    for (n, ax), seg in zip(_SMALL_SHARDED, _unpack(gathered[-1].reshape(N_DEV, -1), small_shapes)):
        W[n] = jnp.moveaxis(seg, 0, ax).reshape(_full_shape(A[n].shape, ax))
    for n in ("ev_conv_a", "ev_conv_b", "od_w2", "od_a2", "od_g2"):
        W[n] = W[n][0]
    for n in _SMALL_REPLICATED:
        W[n] = A[n]
    W["od_r_k"] = od_r_k[0]

    loss_tile, grad_x, G = _local_step(x[0], loss_target[0], W)

    gbig = [G["ev_w_in_t"], G["ev_w_out"], G["od_w_in_t"], G["od_w_out"], G["ff_w_up_t"][0], G["ff_w_up_t"][1],
            G["ff_w_down"][0], G["ff_w_down"][1]]
    gbig = [g.reshape(N_DEV, g.shape[0] // N_DEV, g.shape[1]) for g in gbig]
    r1 = _rs_d2d(gbig, "rs_sibling")
    ps = [_rs_add(g, r, c_vec, f"rs_add{i}") for i, (g, r) in enumerate(zip(gbig, r1))]
    r2 = _rs_ici(ps, "rs_chips")
    gsh = [_rs_final(p, r, q_vec, f"rs_final{i}") for i, (p, r) in enumerate(zip(ps, r2))]
    grads = dict(ev_w_in=gsh[0].T[None], ev_w_out=gsh[1][None], od_w_in=gsh[2].T[None], od_w_out=gsh[3][None],
                 ff_w_up=jnp.stack([gsh[4].T, gsh[5].T]), ff_w_down=jnp.stack([gsh[6], gsh[7]]))

    small_names = [n for n, _ in _SMALL_SHARDED] + list(_SMALL_REPLICATED)
    small_full_shapes = [_full_shape(A[n].shape, ax) for n, ax in _SMALL_SHARDED] + [A[n].shape for n in _SMALL_REPLICATED]
    (gsm,) = _all_gather([_pack([G[n] for n in small_names])], "gather_small_grads")
    summed = _unpack(_sum_devices(gsm).reshape(-1), small_full_shapes)
    for n, full in zip(small_names, summed):
        grads[n] = full
    for n, ax in _SMALL_SHARDED:
        size = A[n].shape[ax]
        grads[n] = lax.dynamic_slice_in_dim(grads[n], me * size, size, axis=ax)

    delta, new_m, new_v = {}, {}, {}
    for n in _WEIGHTS:
        delta[n], new_m[n], new_v[n] = _adamw(A[n], A["m_" + n], A["v_" + n], grads[n], "adamw_" + n)

    loss = lax.psum(loss_tile[0, 0], ("x", "y", "c"))
    return (loss, grad_x[None], *[grads[n] for n in _WEIGHTS], *[delta[n] for n in _WEIGHTS],
            *[new_m[n] for n in _WEIGHTS], *[new_v[n] for n in _WEIGHTS])
```

```python
import jax
import jax.numpy as jnp
from jax import lax
from jax.experimental import pallas as pl
from jax.experimental.pallas import tpu as pltpu

f32, bf16 = jnp.float32, jnp.bfloat16

D_MODEL = 1024
N_META = 16
RMS_EPS = 1e-6
LN_EPS = 1e-5
D_A = 512
CONV_A_WIDTH = 31
CONV_B_WIDTH = 3
HEAD_DIM = 64
N_Q_HEADS = 8
N_KV_HEADS = 2
GQA_GROUP = 4
D_ATT = 512
D_KV = 128
BLOCK = 128
ROPE_THETA = 10000.0
D_R = 512
LORA_W, LORA_A, LORA_G = 64, 64, 128
RWKV_GN_EPS = 64e-5
ATT_COLS = D_ATT + 2 * D_KV
RWKV_COLS = 3 * D_R + LORA_W + LORA_A + LORA_G
D_FF = 2816
FF_CONV_WIDTH = 3
NEG_INF = -1e30
ATT_PAD = BLOCK - N_META
ATT_SCALE = HEAD_DIM ** -0.5

ADAM_LR, ADAM_B1, ADAM_B2, ADAM_EPS, ADAM_WD, ADAM_STEP = 0.001, 0.9, 0.999, 1e-08, 0.01, 10

N_DEV = 8
LANES = 128
SUBLANES = 8
SCAN_CHUNK = 48
PAIR_ROWS = 4 * HEAD_DIM
V7X_VMEM_LIMIT = 56 * 1024 * 1024
ADAMW_BLOCK_ELEMS = 400 * 1024
GRAD_WIRE_DTYPE = bf16
MESH = pl.DeviceIdType.MESH
S = jax.ShapeDtypeStruct
HIGHEST = lax.Precision.HIGHEST


def _pc(body, **kw):
    return pl.pallas_call(body, **kw)


def _cparams(sem=None):
    return pltpu.CompilerParams(dimension_semantics=sem, vmem_limit_bytes=V7X_VMEM_LIMIT)


def _divisor_block(t, unit, limit):
    best = unit
    for rb in range(unit, limit + 1, unit):
        if t % rb == 0:
            best = rb
    assert t % best == 0, (t, unit)
    return best


def _row_block(t):
    return _divisor_block(t, 16, 704)


def _row_block8(t):
    return _divisor_block(t, 8, 344)


def _col_tile(n):
    for t in (512, 256, 128):
        if n % t == 0:
            return t
    return n


def _full(shape):
    nd = len(shape)
    return pl.BlockSpec(shape, lambda *_: (0,) * nd)


def _sigmoid(x):
    return jax.nn.sigmoid(x)


_DIMS = {"nn": (((1,), (0,)), ((), ())), "nt": (((1,), (1,)), ((), ())), "tn": (((0,), (0,)), ((), ()))}
MM_MAX_K = 2816


def _mm(a, b, mode, name, out_dtype=f32, res=None):
    if mode == "nn":
        (m, k), (k2, n) = a.shape, b.shape
    elif mode == "nt":
        (m, k), (n, k2) = a.shape, b.shape
    else:
        (k, m), (k2, n) = a.shape, b.shape
    assert k == k2, (a.shape, b.shape, mode)
    tm = _row_block(m) if m % LANES else _col_tile(m)
    tn = _col_tile(n)
    nk = 1 if (mode == "tn" or k <= MM_MAX_K) else k // MM_MAX_K
    tk = k // nk
    assert tk * nk == k
    dims = _DIMS[mode]

    def body(a_ref, b_ref, *rest):
        part = lax.dot_general(a_ref[...].astype(bf16), b_ref[...].astype(bf16), dims, preferred_element_type=f32)
        if nk == 1:
            o_ref = rest[-1]
            if res is not None:
                part = part + rest[0][...]
            o_ref[...] = part.astype(out_dtype)
            return
        o_ref, acc_ref = rest[-2], rest[-1]
        kk = pl.program_id(2)

        @pl.when(kk == 0)
        def _():
            acc_ref[...] = part

        @pl.when(kk > 0)
        def _():
            acc_ref[...] += part

        @pl.when(kk == nk - 1)
        def _():
            acc = acc_ref[...]
            if res is not None:
                acc = acc + rest[0][...]
            o_ref[...] = acc.astype(out_dtype)

    if mode == "tn":
        a_spec = pl.BlockSpec((k, tm), lambda i, j, kk: (0, i))
    else:
        a_spec = pl.BlockSpec((tm, tk), lambda i, j, kk: (i, kk))
    if mode == "nt":
        b_spec = pl.BlockSpec((tn, tk), lambda i, j, kk: (j, kk))
    else:
        b_spec = pl.BlockSpec((tk, tn), lambda i, j, kk: (kk, j))
    o_spec = pl.BlockSpec((tm, tn), lambda i, j, kk: (i, j))
    ins, specs = [a, b], [a_spec, b_spec]
    if res is not None:
        ins.append(res)
        specs.append(o_spec)
    scratch = [pltpu.VMEM((tm, tn), f32)] if nk > 1 else []
    return _pc(body, name=name, grid=(m // tm, n // tn, nk), in_specs=specs, out_specs=o_spec,
               out_shape=S((m, n), out_dtype), scratch_shapes=scratch,
               compiler_params=_cparams(("arbitrary", "arbitrary", "arbitrary")))(*ins)


def _rms_fwd(x, g, name):
    t, d = x.shape
    rb = _row_block(t)

    def body(x_ref, g_ref, o_ref):
        xv = x_ref[...]
        rstd = lax.rsqrt(jnp.mean(xv * xv, axis=-1, keepdims=True) + RMS_EPS)
        o_ref[...] = (xv * rstd * g_ref[...]).astype(bf16)

    row = pl.BlockSpec((rb, d), lambda i: (i, 0))
    return _pc(body, name=name, grid=(t // rb,), in_specs=[row, _full((1, d))], out_specs=row,
               out_shape=S((t, d), bf16), compiler_params=_cparams(("arbitrary",)))(x, g.reshape(1, d))


def _rms_bwd(dy, x, g, dres, name):
    t, d = x.shape
    rb = _row_block8(t)

    def body(dy_ref, x_ref, g_ref, dres_ref, dx_ref, dg_ref):
        @pl.when(pl.program_id(0) == 0)
        def _():
            dg_ref[...] = jnp.zeros_like(dg_ref)
        xv, dyv = x_ref[...], dy_ref[...]
        rstd = lax.rsqrt(jnp.mean(xv * xv, axis=-1, keepdims=True) + RMS_EPS)
        xn = xv * rstd
        dg_ref[...] += jnp.sum(dyv * xn, axis=0, keepdims=True)
        dxh = dyv * g_ref[...]
        dx_ref[...] = dres_ref[...] + rstd * (dxh - xn * jnp.mean(dxh * xn, axis=-1, keepdims=True))

    row = pl.BlockSpec((rb, d), lambda i: (i, 0))
    return _pc(body, name=name, grid=(t // rb,), in_specs=[row, row, _full((1, d)), row],
               out_specs=(row, _full((1, d))), out_shape=(S((t, d), f32), S((1, d), f32)),
               compiler_params=_cparams(("arbitrary",)))(dy, x, g.reshape(1, d), dres)


def _final_loss(h, g, target_padded):
    t, d = h.shape
    rb = _row_block8(t)

    def body(x_ref, g_ref, t_ref, loss_ref, dx_ref, dg_ref):
        i = pl.program_id(0)

        @pl.when(i == 0)
        def _():
            dg_ref[...] = jnp.zeros_like(dg_ref)
            loss_ref[...] = jnp.zeros_like(loss_ref)
        xv = x_ref[...]
        rstd = lax.rsqrt(jnp.mean(xv * xv, axis=-1, keepdims=True) + RMS_EPS)
        xn = xv * rstd
        gv = g_ref[...]
        row = i * rb + lax.broadcasted_iota(jnp.int32, (rb, 1), 0)
        diff = jnp.where(row >= N_META, xn * gv - t_ref[...], 0.0)
        loss_ref[...] += 0.5 * jnp.sum(jnp.mean(diff * diff, axis=-1, keepdims=True))
        dout = diff * (1.0 / d)
        dg_ref[...] += jnp.sum(dout * xn, axis=0, keepdims=True)
        dxh = dout * gv
        dx_ref[...] = rstd * (dxh - xn * jnp.mean(dxh * xn, axis=-1, keepdims=True))

    row = pl.BlockSpec((rb, d), lambda i: (i, 0))
    return _pc(body, name="final_loss", grid=(t // rb,), in_specs=[row, _full((1, d)), row],
               out_specs=(_full((SUBLANES, LANES)), row, _full((1, d))),
               out_shape=(S((SUBLANES, LANES), f32), S((t, d), f32), S((1, d), f32)),
               compiler_params=_cparams(("arbitrary",)))(h, g.reshape(1, d), target_padded)


CONV_LEAD = 32


def _fill_front_padded(pad_ref, x, t):
    pad_ref[0:CONV_LEAD, :] = jnp.zeros((CONV_LEAD, x.shape[1]), f32)
    pad_ref[CONV_LEAD:CONV_LEAD + t, :] = x


def _fill_back_padded(pad_ref, x, t):
    pad_ref[0:t, :] = x
    pad_ref[t:t + CONV_LEAD, :] = jnp.zeros((CONV_LEAD, x.shape[1]), f32)


def _conv_rows(pad_ref, w_ref, kw, r0, nr):
    acc = None
    for j in range(kw):
        lo = CONV_LEAD + r0 - (kw - 1) + j
        term = w_ref[j:j + 1, :] * pad_ref[lo:lo + nr, :]
        acc = term if acc is None else acc + term
    return acc


def _conv_t_rows(padb_ref, w_ref, kw, r0, nr):
    acc = None
    for j in range(kw):
        lo = r0 + (kw - 1) - j
        term = w_ref[j:j + 1, :] * padb_ref[lo:lo + nr, :]
        acc = term if acc is None else acc + term
    return acc


def _conv_dw_rows(dy_blk, pad_ref, kw, r0, nr):
    out = []
    for j in range(kw):
        lo = CONV_LEAD + r0 - (kw - 1) + j
        out.append(jnp.sum(dy_blk * pad_ref[lo:lo + nr, :], axis=0, keepdims=True))
    return out


def _acc_list(a, b):
    return b if a is None else [x + y for x, y in zip(a, b)]


def _ev_a_conv(p, conv_a):
    t = p.shape[0]
    cr = _row_block8(t)
    nb = D_A // LANES

    def body(av_ref, ag_ref, w_ref, o_ref, pad_ref):
        _fill_front_padded(pad_ref, av_ref[...] * _sigmoid(ag_ref[...]), t)
        for r in range(t // cr):
            o_ref[r * cr:(r + 1) * cr, :] = _conv_rows(pad_ref, w_ref, CONV_A_WIDTH, r * cr, cr)

    col = lambda off: pl.BlockSpec((t, LANES), lambda j: (0, j + off))
    return _pc(body, name="ev_a_conv", grid=(nb,),
               in_specs=[col(0), col(nb), pl.BlockSpec((CONV_A_WIDTH, LANES), lambda j: (0, j))],
               out_specs=col(0), out_shape=S((t, D_A), f32),
               scratch_shapes=[pltpu.VMEM((t + CONV_LEAD, LANES), f32)],
               compiler_params=_cparams(("arbitrary",)))(p, p, conv_a)


def _ln_silu(uc, g, b):
    mu = jnp.mean(uc, axis=-1, keepdims=True)
    xc = uc - mu
    var = jnp.mean(xc * xc, axis=-1, keepdims=True)
    y = xc * lax.rsqrt(var + LN_EPS) * g + b
    return y * _sigmoid(y)


def _ev_a_norm(uc, g, b):
    t, d = uc.shape
    rb = _row_block(t)

    def body(u_ref, g_ref, b_ref, o_ref):
        o_ref[...] = _ln_silu(u_ref[...], g_ref[...], b_ref[...]).astype(bf16)

    row = pl.BlockSpec((rb, d), lambda i: (i, 0))
    return _pc(body, name="ev_a_norm", grid=(t // rb,), in_specs=[row, _full((1, d)), _full((1, d))],
               out_specs=row, out_shape=S((t, d), bf16), compiler_params=_cparams(("arbitrary",)))(uc, g, b)


def _ev_a_norm_bwd(dy, uc, g, b):
    t, d = uc.shape
    rb = _row_block8(t)

    def body(dy_ref, u_ref, g_ref, b_ref, du_ref, dg_ref, db_ref):
        @pl.when(pl.program_id(0) == 0)
        def _():
            dg_ref[...] = jnp.zeros_like(dg_ref)
            db_ref[...] = jnp.zeros_like(db_ref)
        _, vjp = jax.vjp(_ln_silu, u_ref[...], g_ref[...], b_ref[...])
        du, dg, db = vjp(dy_ref[...])
        du_ref[...] = du
        dg_ref[...] += dg
        db_ref[...] += db

    row = pl.BlockSpec((rb, d), lambda i: (i, 0))
    return _pc(body, name="ev_a_norm_bwd", grid=(t // rb,), in_specs=[row, row, _full((1, d)), _full((1, d))],
               out_specs=(row, _full((1, d)), _full((1, d))),
               out_shape=(S((t, d), f32), S((1, d), f32), S((1, d), f32)),
               compiler_params=_cparams(("arbitrary",)))(dy, uc, g, b)


def _ev_a_conv_bwd(duc, p, conv_a):
    t = p.shape[0]
    cr = _row_block8(t)
    nb = D_A // LANES

    def body(dy_ref, av_ref, ag_ref, w_ref, dav_ref, dag_ref, dw_ref, pad_ref, padb_ref):
        _fill_front_padded(pad_ref, av_ref[...] * _sigmoid(ag_ref[...]), t)
        _fill_back_padded(padb_ref, dy_ref[...], t)
        dw = None
        for r in range(t // cr):
            rows = slice(r * cr, (r + 1) * cr)
            du = _conv_t_rows(padb_ref, w_ref, CONV_A_WIDTH, r * cr, cr)
            avr = av_ref[rows, :]
            sgr = _sigmoid(ag_ref[rows, :])
            dav_ref[rows, :] = du * sgr
            dag_ref[rows, :] = du * avr * sgr * (1.0 - sgr)
            dw = _acc_list(dw, _conv_dw_rows(dy_ref[rows, :], pad_ref, CONV_A_WIDTH, r * cr, cr))
        for j in range(CONV_A_WIDTH):
            dw_ref[j:j + 1, :] = dw[j]

    col = lambda off: pl.BlockSpec((t, LANES), lambda j: (0, j + off))
    wsp = pl.BlockSpec((CONV_A_WIDTH, LANES), lambda j: (0, j))
    return _pc(body, name="ev_a_conv_bwd", grid=(nb,), in_specs=[col(0), col(0), col(nb), wsp],
               out_specs=(col(0), col(0), wsp),
               out_shape=(S((t, D_A), f32), S((t, D_A), f32), S((CONV_A_WIDTH, D_A), f32)),
               scratch_shapes=[pltpu.VMEM((t + CONV_LEAD, LANES), f32), pltpu.VMEM((t + CONV_LEAD, LANES), f32)],
               compiler_params=_cparams(("arbitrary",)))(duc, p, p, conv_a)


def _ev_b(p, conv_b):
    t = p.shape[0]
    cr = _row_block8(t)
    nb = D_A // LANES

    def body(gb_ref, gc_ref, xi_ref, w_ref, o_ref, pad_ref, stage_ref):
        _fill_front_padded(pad_ref, gc_ref[...] * xi_ref[...], t)
        for r in range(t // cr):
            rows = slice(r * cr, (r + 1) * cr)
            stage_ref[rows, :] = gb_ref[rows, :] * _conv_rows(pad_ref, w_ref, CONV_B_WIDTH, r * cr, cr)
        o_ref[...] = stage_ref[...].astype(bf16)

    col = lambda off: pl.BlockSpec((t, LANES), lambda j: (0, j + off))
    return _pc(body, name="ev_b", grid=(nb,),
               in_specs=[col(2 * nb), col(3 * nb), col(4 * nb), pl.BlockSpec((CONV_B_WIDTH, LANES), lambda j: (0, j))],
               out_specs=col(0), out_shape=S((t, D_A), bf16),
               scratch_shapes=[pltpu.VMEM((t + CONV_LEAD, LANES), f32), pltpu.VMEM((t, LANES), f32)],
               compiler_params=_cparams(("arbitrary",)))(p, p, p, conv_b)


def _ev_b_bwd(dy, p, conv_b):
    t = p.shape[0]
    cr = _row_block8(t)
    nb = D_A // LANES

    def body(dy_ref, gb_ref, gc_ref, xi_ref, w_ref, dgb_ref, dgc_ref, dxi_ref, dw_ref, pad_ref, padb_ref):
        _fill_front_padded(pad_ref, gc_ref[...] * xi_ref[...], t)
        _fill_back_padded(padb_ref, dy_ref[...] * gb_ref[...], t)
        dw = None
        for r in range(t // cr):
            rows = slice(r * cr, (r + 1) * cr)
            dgb_ref[rows, :] = dy_ref[rows, :] * _conv_rows(pad_ref, w_ref, CONV_B_WIDTH, r * cr, cr)
            dcx = _conv_t_rows(padb_ref, w_ref, CONV_B_WIDTH, r * cr, cr)
            dgc_ref[rows, :] = dcx * xi_ref[rows, :]
            dxi_ref[rows, :] = dcx * gc_ref[rows, :]
            dw = _acc_list(dw, _conv_dw_rows(padb_ref[rows, :], pad_ref, CONV_B_WIDTH, r * cr, cr))
        for j in range(CONV_B_WIDTH):
            dw_ref[j:j + 1, :] = dw[j]

    col = lambda off: pl.BlockSpec((t, LANES), lambda j: (0, j + off))
    wsp = pl.BlockSpec((CONV_B_WIDTH, LANES), lambda j: (0, j))
    return _pc(body, name="ev_b_bwd", grid=(nb,), in_specs=[col(nb), col(2 * nb), col(3 * nb), col(4 * nb), wsp],
               out_specs=(col(0), col(0), col(0), wsp),
               out_shape=(S((t, D_A), f32), S((t, D_A), f32), S((t, D_A), f32), S((CONV_B_WIDTH, D_A), f32)),
               scratch_shapes=[pltpu.VMEM((t + CONV_LEAD, LANES), f32), pltpu.VMEM((t + CONV_LEAD, LANES), f32)],
               compiler_params=_cparams(("arbitrary",)))(dy, p, p, p, conv_b)


def _ffn_mid(u, conv_w, conv_b, name):
    t = u.shape[0]
    cr = _row_block8(t)
    nb = D_FF // LANES

    def body(gt_ref, vl_ref, w_ref, b_ref, o_ref, pad_ref, stage_ref):
        _fill_front_padded(pad_ref, gt_ref[...], t)
        for r in range(t // cr):
            rows = slice(r * cr, (r + 1) * cr)
            gc = _conv_rows(pad_ref, w_ref, FF_CONV_WIDTH, r * cr, cr) + b_ref[...]
            stage_ref[rows, :] = gc * _sigmoid(gc) * vl_ref[rows, :]
        o_ref[...] = stage_ref[...].astype(bf16)

    col = lambda off: pl.BlockSpec((t, LANES), lambda j: (0, j + off))
    return _pc(body, name=name, grid=(nb,),
               in_specs=[col(0), col(nb), pl.BlockSpec((FF_CONV_WIDTH, LANES), lambda j: (0, j)),
                         pl.BlockSpec((1, LANES), lambda j: (0, j))],
               out_specs=col(0), out_shape=S((t, D_FF), bf16),
               scratch_shapes=[pltpu.VMEM((t + CONV_LEAD, LANES), f32), pltpu.VMEM((t, LANES), f32)],
               compiler_params=_cparams(("arbitrary",)))(u, u, conv_w, conv_b.reshape(1, D_FF))


def _ffn_mid_bwd(dz, u, conv_w, conv_b, name):
    t = u.shape[0]
    cr = _row_block8(t)
    nb = D_FF // LANES

    def body(dz_ref, gt_ref, vl_ref, w_ref, b_ref, du_ref, dw_ref, db_ref, pad_ref, padb_ref):
        s = pl.program_id(1)
        _fill_front_padded(pad_ref, gt_ref[...], t)

        @pl.when(s == 0)
        def _():
            for r in range(t // cr):
                rows = slice(r * cr, (r + 1) * cr)
                gc = _conv_rows(pad_ref, w_ref, FF_CONV_WIDTH, r * cr, cr) + b_ref[...]
                sg = _sigmoid(gc)
                padb_ref[rows, :] = dz_ref[rows, :] * vl_ref[rows, :] * sg * (1.0 + gc * (1.0 - sg))
            padb_ref[t:t + CONV_LEAD, :] = jnp.zeros((CONV_LEAD, LANES), f32)
            dw, db = None, None
            for r in range(t // cr):
                rows = slice(r * cr, (r + 1) * cr)
                du_ref[rows, :] = _conv_t_rows(padb_ref, w_ref, FF_CONV_WIDTH, r * cr, cr)
                dgc = padb_ref[rows, :]
                dw = _acc_list(dw, _conv_dw_rows(dgc, pad_ref, FF_CONV_WIDTH, r * cr, cr))
                pb = jnp.sum(dgc, axis=0, keepdims=True)
                db = pb if db is None else db + pb
            for j in range(FF_CONV_WIDTH):
                dw_ref[j:j + 1, :] = dw[j]
            db_ref[...] = db

        @pl.when(s == 1)
        def _():
            for r in range(t // cr):
                rows = slice(r * cr, (r + 1) * cr)
                gc = _conv_rows(pad_ref, w_ref, FF_CONV_WIDTH, r * cr, cr) + b_ref[...]
                du_ref[rows, :] = dz_ref[rows, :] * gc * _sigmoid(gc)

    col = lambda off: pl.BlockSpec((t, LANES), lambda j, s: (0, j + off))
    wsp = pl.BlockSpec((FF_CONV_WIDTH, LANES), lambda j, s: (0, j))
    bsp = pl.BlockSpec((1, LANES), lambda j, s: (0, j))
    return _pc(body, name=name, grid=(nb, 2), in_specs=[col(0), col(0), col(nb), wsp, bsp],
               out_specs=(pl.BlockSpec((t, LANES), lambda j, s: (0, s * nb + j)), wsp, bsp),
               out_shape=(S((t, 2 * D_FF), f32), S((FF_CONV_WIDTH, D_FF), f32), S((1, D_FF), f32)),
               scratch_shapes=[pltpu.VMEM((t + CONV_LEAD, LANES), f32), pltpu.VMEM((t + CONV_LEAD, LANES), f32)],
               compiler_params=_cparams(("arbitrary", "arbitrary")))(dz, u, u, conv_w, conv_b.reshape(1, D_FF))


def _swap_halves(x):
    w = x.shape[1]
    lane = lax.broadcasted_iota(jnp.int32, x.shape, 1) % HEAD_DIM
    return jnp.where(lane < HEAD_DIM // 2, pltpu.roll(x, w - HEAD_DIM // 2, axis=1), pltpu.roll(x, HEAD_DIM // 2, axis=1))


def _rope_pack(patt, c64, s64):
    t = patt.shape[0]
    tp = t + ATT_PAD

    def body(p_ref, c_ref, s_ref, q_ref, k_ref, v_ref):
        c, s = c_ref[...], s_ref[...]

        def rope(x, nh):
            cc = jnp.concatenate([c] * nh, axis=1)
            ss = jnp.concatenate([s] * nh, axis=1)
            return x * cc + _swap_halves(x) * ss

        for ref, val in ((q_ref, rope(p_ref[:, 0:D_ATT], N_Q_HEADS)),
                         (k_ref, rope(p_ref[:, D_ATT:D_ATT + D_KV], N_KV_HEADS)),
                         (v_ref, p_ref[:, D_ATT + D_KV:ATT_COLS])):
            ref[0:ATT_PAD, :] = jnp.zeros((ATT_PAD, val.shape[1]), bf16)
            ref[ATT_PAD:tp, :] = val.astype(bf16)

    return _pc(body, name="rope_pack", in_specs=[_full((t, ATT_COLS)), _full((t, HEAD_DIM)), _full((t, HEAD_DIM))],
               out_specs=(_full((tp, D_ATT)), _full((tp, D_KV)), _full((tp, D_KV))), grid=(1,),
               out_shape=(S((tp, D_ATT), bf16), S((tp, D_KV), bf16), S((tp, D_KV), bf16)),
               compiler_params=_cparams(("arbitrary",)))(patt, c64, s64)


def _rope_bwd(dqp, dkp, dvp, c64, s64):
    tp = dqp.shape[0]
    t = tp - ATT_PAD

    def body(dq_ref, dk_ref, dv_ref, c_ref, s_ref, o_ref):
        c, s = c_ref[...], s_ref[...]

        def unrope(dy, nh):
            cc = jnp.concatenate([c] * nh, axis=1)
            ss = jnp.concatenate([s] * nh, axis=1)
            return dy * cc + _swap_halves(dy * ss)

        o_ref[:, 0:D_ATT] = unrope(dq_ref[ATT_PAD:tp, :], N_Q_HEADS)
        o_ref[:, D_ATT:D_ATT + D_KV] = unrope(dk_ref[ATT_PAD:tp, :], N_KV_HEADS)
        o_ref[:, D_ATT + D_KV:ATT_COLS] = dv_ref[ATT_PAD:tp, :]

    return _pc(body, name="rope_bwd", grid=(1,),
               in_specs=[_full((tp, D_ATT)), _full((tp, D_KV)), _full((tp, D_KV)), _full((t, HEAD_DIM)), _full((t, HEAD_DIM))],
               out_specs=_full((t, ATT_COLS)), out_shape=S((t, ATT_COLS), f32),
               compiler_params=_cparams(("arbitrary",)))(dqp, dkp, dvp, c64, s64)


def _attn_masks(n):
    rows = GQA_GROUP * BLOCK
    ri = lax.broadcasted_iota(jnp.int32, (rows, BLOCK), 0) % BLOCK
    ci = lax.broadcasted_iota(jnp.int32, (rows, BLOCK), 1)
    m_cur = (ci <= ri) & (ci >= jnp.where(n >= 1, 0, ATT_PAD))
    m_prev = ci > ri + jnp.where(n >= 2, 0, BLOCK)
    m_meta = ci >= jnp.where(n >= 1, ATT_PAD, BLOCK)
    return m_cur, m_prev, m_meta


def _attn_probs(qg, kc, kp, km, masks, skv):
    def scores(k, m):
        s = lax.dot_general(qg, k, _DIMS["nt"], preferred_element_type=f32) * ATT_SCALE
        return jnp.where(m, s, NEG_INF)
    s_c, s_p, s_m = scores(kc, masks[0]), scores(kp, masks[1]), scores(km, masks[2])
    mx = jnp.maximum(jnp.maximum(jnp.max(s_c, axis=-1, keepdims=True), jnp.max(s_p, axis=-1, keepdims=True)),
                     jnp.maximum(jnp.max(s_m, axis=-1, keepdims=True), skv))
    e_c, e_p, e_m, e_s = jnp.exp(s_c - mx), jnp.exp(s_p - mx), jnp.exp(s_m - mx), jnp.exp(skv - mx)
    den = (jnp.sum(e_c, axis=-1, keepdims=True) + jnp.sum(e_p, axis=-1, keepdims=True)
           + jnp.sum(e_m, axis=-1, keepdims=True) + e_s)
    inv = 1.0 / den
    return e_c * inv, e_p * inv, e_m * inv, e_s * inv


def _sink_rows(sk_ref, g):
    hrow = lax.broadcasted_iota(jnp.int32, (GQA_GROUP * BLOCK, 1), 0) // BLOCK
    skv = jnp.zeros((GQA_GROUP * BLOCK, 1), f32)
    for hh in range(GQA_GROUP):
        skv = jnp.where(hrow == hh, sk_ref[0, GQA_GROUP * g + hh], skv)
    return skv, hrow


def _stack_heads(ref, g):
    return jnp.concatenate([ref[:, (GQA_GROUP * g + hh) * HEAD_DIM:(GQA_GROUP * g + hh + 1) * HEAD_DIM]
                            for hh in range(GQA_GROUP)], axis=0)


def _attn_specs():
    blk = lambda w: pl.BlockSpec((BLOCK, w), lambda n: (n, 0))
    prev = pl.BlockSpec((BLOCK, D_KV), lambda n: (jnp.maximum(n - 1, 0), 0))
    meta = pl.BlockSpec((BLOCK, D_KV), lambda n: (0, 0))
    return blk, prev, meta


def _attn_fwd(qp, kp, vp, sinks):
    tp = qp.shape[0]
    blk, prev, meta = _attn_specs()

    def body(sk_ref, q_ref, kc_ref, kp_ref, km_ref, vc_ref, vp_ref, vm_ref, o_ref):
        masks = _attn_masks(pl.program_id(0))
        for g in range(N_KV_HEADS):
            sl = slice(g * HEAD_DIM, (g + 1) * HEAD_DIM)
            skv, _ = _sink_rows(sk_ref, g)
            p_c, p_p, p_m, _ = _attn_probs(_stack_heads(q_ref, g), kc_ref[:, sl], kp_ref[:, sl], km_ref[:, sl], masks, skv)
            o = (jnp.dot(p_c.astype(bf16), vc_ref[:, sl], preferred_element_type=f32)
                 + jnp.dot(p_p.astype(bf16), vp_ref[:, sl], preferred_element_type=f32)
                 + jnp.dot(p_m.astype(bf16), vm_ref[:, sl], preferred_element_type=f32))
            for hh in range(GQA_GROUP):
                h = GQA_GROUP * g + hh
                o_ref[:, h * HEAD_DIM:(h + 1) * HEAD_DIM] = o[hh * BLOCK:(hh + 1) * BLOCK].astype(bf16)

    return _pc(body, name="attn_fwd", grid=(tp // BLOCK,),
               in_specs=[pl.BlockSpec(memory_space=pltpu.SMEM), blk(D_ATT), blk(D_KV), prev, meta, blk(D_KV), prev, meta],
               out_specs=blk(D_ATT), out_shape=S((tp, D_ATT), bf16),
               compiler_params=_cparams(("arbitrary",)))(sinks, qp, kp, kp, kp, vp, vp, vp)


def _attn_bwd(qp, kp, vp, sinks, dop):
    tp = qp.shape[0]
    blk, prev, meta = _attn_specs()

    def body(sk_ref, q_ref, kc_ref, kp_ref, km_ref, vc_ref, vp_ref, vm_ref, do_ref, dq_ref, dk_ref, dv_ref, dsk_ref):
        n = pl.program_id(0)

        @pl.when(n == 0)
        def _():
            dk_ref[...] = jnp.zeros_like(dk_ref)
            dv_ref[...] = jnp.zeros_like(dv_ref)
            dsk_ref[...] = jnp.zeros_like(dsk_ref)
        masks = _attn_masks(n)
        cur = pl.ds(pl.multiple_of(n * BLOCK, BLOCK), BLOCK)
        prv = pl.ds(pl.multiple_of(jnp.maximum(n - 1, 0) * BLOCK, BLOCK), BLOCK)
        lane = lax.broadcasted_iota(jnp.int32, (1, LANES), 1)
        dsk = jnp.zeros((1, LANES), f32)
        for g in range(N_KV_HEADS):
            sl = slice(g * HEAD_DIM, (g + 1) * HEAD_DIM)
            skv, hrow = _sink_rows(sk_ref, g)
            qg = _stack_heads(q_ref, g)
            dog = _stack_heads(do_ref, g)
            ks = (kc_ref[:, sl], kp_ref[:, sl], km_ref[:, sl])
            vs = (vc_ref[:, sl], vp_ref[:, sl], vm_ref[:, sl])
            probs = _attn_probs(qg, ks[0], ks[1], ks[2], masks, skv)
            dps = [lax.dot_general(dog, v, _DIMS["nt"], preferred_element_type=f32) for v in vs]
            delta = sum(jnp.sum(p * dp, axis=-1, keepdims=True) for p, dp in zip(probs[:3], dps))
            dss = [(p * (dp - delta) * ATT_SCALE).astype(bf16) for p, dp in zip(probs[:3], dps)]
            dq = sum(jnp.dot(ds, k, preferred_element_type=f32) for ds, k in zip(dss, ks))
            for hh in range(GQA_GROUP):
                h = GQA_GROUP * g + hh
                dq_ref[:, h * HEAD_DIM:(h + 1) * HEAD_DIM] = dq[hh * BLOCK:(hh + 1) * BLOCK]
                dsk = dsk + jnp.where(lane == h, -jnp.sum(jnp.where(hrow == hh, probs[3] * delta, 0.0)), 0.0)
            for rows, p, ds in zip((cur, prv, slice(0, BLOCK)), probs[:3], dss):
                dv_ref[rows, sl] += lax.dot_general(p.astype(bf16), dog, _DIMS["tn"], preferred_element_type=f32)
                dk_ref[rows, sl] += lax.dot_general(ds, qg, _DIMS["tn"], preferred_element_type=f32)
        dsk_ref[...] += dsk

    return _pc(body, name="attn_bwd", grid=(tp // BLOCK,),
               in_specs=[pl.BlockSpec(memory_space=pltpu.SMEM), blk(D_ATT), blk(D_KV), prev, meta, blk(D_KV), prev, meta,
                         blk(D_ATT)],
               out_specs=(blk(D_ATT), _full((tp, D_KV)), _full((tp, D_KV)), _full((1, LANES))),
               out_shape=(S((tp, D_ATT), f32), S((tp, D_KV), f32), S((tp, D_KV), f32), S((1, LANES), f32)),
               compiler_params=_cparams(("arbitrary",)))(sinks, qp, kp, kp, kp, vp, vp, vp, dop)


def _seg(x, bm):
    return jnp.dot(x, bm, precision=HIGHEST, preferred_element_type=f32)


def _softplus(y):
    return jnp.maximum(y, 0.0) + jnp.log(1.0 + jnp.exp(-jnp.abs(y)))


def _prep_fn(xr, xk, xwd, xad, xgd, w0, w2, a0, a2, g2, k_k, k_a, bm):
    xw = w0 + jnp.dot(jnp.tanh(xwd), w2, preferred_element_type=f32)
    decay = jnp.exp(-jnp.exp(-_softplus(-xw) - 0.5))
    alpha = _sigmoid(a0 + jnp.dot(xad, a2, preferred_element_type=f32))
    g = jnp.dot(_sigmoid(xgd), g2, preferred_element_type=f32)
    kk = xk * k_k
    kkn = kk / jnp.maximum(jnp.sqrt(_seg(kk * kk, bm)), 1e-12)
    k2 = xk * (1.0 + (alpha - 1.0) * k_a)
    return decay, k2, -kkn, kkn * alpha, g


def _split_cols(x):
    o1, o2, o3 = 3 * D_R, 3 * D_R + LORA_W, 3 * D_R + LORA_W + LORA_A
    return x[:, 0:D_R], x[:, D_R:2 * D_R], x[:, 2 * D_R:o1], x[:, o1:o2], x[:, o2:o3], x[:, o3:RWKV_COLS]


def _shifted(sh_ref, x, halo, first, rb):
    sh_ref[0:SUBLANES, :] = jnp.where(first, 0.0, halo)
    sh_ref[SUBLANES:SUBLANES + rb, :] = x
    return sh_ref[SUBLANES - 1:SUBLANES - 1 + rb, :]


_PREP_PARAMS = ("od_w0", "od_w2", "od_a0", "od_a2", "od_g2", "od_k_k", "od_k_a")


def _rwkv_prep(pr, mu, params, bm):
    t = pr.shape[0]
    rb = _row_block8(t)
    hb = rb // SUBLANES

    def body(pr_ref, halo_ref, mu_ref, w0, w2, a0, a2, g2, kk_ref, ka_ref, bm_ref, *outs_sh):
        outs, sh_ref = outs_sh[:-1], outs_sh[-1]
        x = pr_ref[...]
        prev = _shifted(sh_ref, x, halo_ref[...], pl.program_id(0) == 0, rb)
        xr, xk, xv, xwd, xad, xgd = _split_cols(x + (prev - x) * mu_ref[...])
        bmv = bm_ref[...]
        decay, k2, a_s, b_s, g = _prep_fn(xr, xk, xwd, xad, xgd, w0[...], w2[...], a0[...], a2[...], g2[...],
                                          kk_ref[...], ka_ref[...], bmv)
        vals = (xr, xv, decay, k2, a_s, b_s, decay * xr, _seg(b_s * xr, bmv), _seg(k2 * xr, bmv), g)
        for ref, val in zip(outs, vals):
            ref[...] = val

    row = pl.BlockSpec((rb, RWKV_COLS), lambda i: (i, 0))
    halo = pl.BlockSpec((SUBLANES, RWKV_COLS), lambda i: (jnp.maximum(i * hb - 1, 0), 0))
    orow = pl.BlockSpec((rb, D_R), lambda i: (i, 0))
    return _pc(body, name="rwkv_prep", grid=(t // rb,),
               in_specs=[row, halo, _full((1, RWKV_COLS))] + [_full(p.shape) for p in params] + [_full(bm.shape)],
               out_specs=(orow,) * 10, out_shape=(S((t, D_R), f32),) * 10,
               scratch_shapes=[pltpu.VMEM((rb + SUBLANES, RWKV_COLS), f32)],
               compiler_params=_cparams(("arbitrary",)))(pr, pr, mu, *params, bm)


def _rwkv_prep_bwd(pr, mu, params, bm, cts):
    t = pr.shape[0]
    rb = _row_block8(t)
    hb = rb // SUBLANES
    counts = [len(c) for c in cts]
    flat = [a for c in cts for a in c]

    def body(pr_ref, halo_ref, mu_ref, w0, w2, a0, a2, g2, kk_ref, ka_ref, bm_ref, *rest):
        ct_refs, rest = rest[:len(flat)], rest[len(flat):]
        dx_ref, dmu_ref = rest[0], rest[1]
        dpar_refs, sh_ref = rest[2:9], rest[9]

        @pl.when(pl.program_id(0) == 0)
        def _():
            dmu_ref[...] = jnp.zeros_like(dmu_ref)
            for r in dpar_refs:
                r[...] = jnp.zeros_like(r)
        sums, pos = [], 0
        for c in counts:
            sums.append(sum(r[...] for r in ct_refs[pos:pos + c]))
            pos += c
        x = pr_ref[...]
        prev = _shifted(sh_ref, x, halo_ref[...], pl.program_id(0) == 0, rb)
        xr, xk, xv, xwd, xad, xgd = _split_cols(x + (prev - x) * mu_ref[...])
        bmv = bm_ref[...]
        _, vjp = jax.vjp(lambda *a: _prep_fn(*a, bmv), xr, xk, xwd, xad, xgd, w0[...], w2[...], a0[...], a2[...],
                         g2[...], kk_ref[...], ka_ref[...])
        grads = vjp(tuple(sums[:5]))
        dxr, dxk, dxwd, dxad, dxgd = grads[:5]
        o1, o2, o3 = 3 * D_R, 3 * D_R + LORA_W, 3 * D_R + LORA_W + LORA_A
        dx_ref[:, 0:D_R] = dxr + sums[5]
        dx_ref[:, D_R:2 * D_R] = dxk
        dx_ref[:, 2 * D_R:o1] = sums[6]
        dx_ref[:, o1:o2] = dxwd
        dx_ref[:, o2:o3] = dxad
        dx_ref[:, o3:RWKV_COLS] = dxgd
        dmu_ref[...] += jnp.sum(dx_ref[...] * (prev - x), axis=0, keepdims=True)
        for r, gval in zip(dpar_refs, grads[5:]):
            r[...] += gval

    row = pl.BlockSpec((rb, RWKV_COLS), lambda i: (i, 0))
    halo = pl.BlockSpec((SUBLANES, RWKV_COLS), lambda i: (jnp.maximum(i * hb - 1, 0), 0))
    crow = pl.BlockSpec((rb, D_R), lambda i: (i, 0))
    return _pc(body, name="rwkv_prep_bwd", grid=(t // rb,),
               in_specs=[row, halo, _full((1, RWKV_COLS))] + [_full(p.shape) for p in params] + [_full(bm.shape)]
               + [crow] * len(flat),
               out_specs=(row, _full((1, RWKV_COLS))) + tuple(_full(p.shape) for p in params),
               out_shape=(S((t, RWKV_COLS), f32), S((1, RWKV_COLS), f32)) + tuple(S(p.shape, f32) for p in params),
               scratch_shapes=[pltpu.VMEM((rb + SUBLANES, RWKV_COLS), f32)],
               compiler_params=_cparams(("arbitrary",)))(pr, pr, mu, *params, bm, *flat)


def _shift_bwd(dxs, mu):
    t = dxs.shape[0]
    rb = _row_block8(t)
    hb = rb // SUBLANES
    nblk = t // rb

    def body(dx_ref, halo_ref, mu_ref, o_ref, sh_ref):
        dx = dx_ref[...]
        sh_ref[0:rb, :] = dx
        sh_ref[rb:rb + SUBLANES, :] = jnp.where(pl.program_id(0) == nblk - 1, 0.0, halo_ref[...])
        m = mu_ref[...]
        o_ref[...] = dx * (1.0 - m) + sh_ref[1:1 + rb, :] * m

    row = pl.BlockSpec((rb, RWKV_COLS), lambda i: (i, 0))
    halo = pl.BlockSpec((SUBLANES, RWKV_COLS), lambda i: (jnp.minimum((i + 1) * hb, t // SUBLANES - 1), 0))
    return _pc(body, name="rwkv_shift_bwd", grid=(nblk,), in_specs=[row, halo, _full((1, RWKV_COLS))],
               out_specs=row, out_shape=S((t, RWKV_COLS), f32),
               scratch_shapes=[pltpu.VMEM((rb + SUBLANES, RWKV_COLS), f32)],
               compiler_params=_cparams(("arbitrary",)))(dxs, dxs, mu)


def _post_fn(y, xr, k2, xv, g, lg, lb, rk, bm):
    inv_n = 1.0 / HEAD_DIM
    yc = y - _seg(y, bm) * inv_n
    var = _seg(yc * yc, bm) * inv_n
    yn = yc * lax.rsqrt(var + RWKV_GN_EPS) * lg + lb
    return (yn + _seg(xr * k2 * rk, bm) * xv) * g


def _rwkv_post(y, xr, k2, xv, g, lg, lb, rk, bm):
    t = y.shape[0]
    rb = _row_block8(t)

    def body(y_ref, xr_ref, k2_ref, xv_ref, g_ref, lg_ref, lb_ref, rk_ref, bm_ref, o_ref):
        o_ref[...] = _post_fn(y_ref[...], xr_ref[...], k2_ref[...], xv_ref[...], g_ref[...], lg_ref[...], lb_ref[...],
                              rk_ref[...], bm_ref[...])

    row = pl.BlockSpec((rb, D_R), lambda i: (i, 0))
    vec = _full((1, D_R))
    return _pc(body, name="rwkv_post", grid=(t // rb,), in_specs=[row] * 5 + [vec] * 3 + [_full(bm.shape)],
               out_specs=row, out_shape=S((t, D_R), f32),
               compiler_params=_cparams(("arbitrary",)))(y, xr, k2, xv, g, lg, lb, rk, bm)


def _rwkv_post_bwd(dy1, y, xr, k2, xv, g, lg, lb, rk, bm):
    t = y.shape[0]
    rb = _row_block8(t)

    def body(dy_ref, y_ref, xr_ref, k2_ref, xv_ref, g_ref, lg_ref, lb_ref, rk_ref, bm_ref, *outs):
        @pl.when(pl.program_id(0) == 0)
        def _():
            for r in outs[5:]:
                r[...] = jnp.zeros_like(r)
        bmv = bm_ref[...]
        _, vjp = jax.vjp(lambda *a: _post_fn(*a, bmv), y_ref[...], xr_ref[...], k2_ref[...], xv_ref[...], g_ref[...],
                         lg_ref[...], lb_ref[...], rk_ref[...])
        grads = vjp(dy_ref[...])
        for r, gval in zip(outs[:5], grads[:5]):
            r[...] = gval
        for r, gval in zip(outs[5:], grads[5:]):
            r[...] += gval

    row = pl.BlockSpec((rb, D_R), lambda i: (i, 0))
    vec = _full((1, D_R))
    return _pc(body, name="rwkv_post_bwd", grid=(t // rb,),
               in_specs=[pl.BlockSpec((rb, D_R), lambda i: (i, 1))] + [row] * 5 + [vec] * 3 + [_full(bm.shape)],
               out_specs=(row,) * 5 + (vec,) * 3, out_shape=(S((t, D_R), f32),) * 5 + (S((1, D_R), f32),) * 3,
               compiler_params=_cparams(("arbitrary",)))(dy1, y, xr, k2, xv, g, lg, lb, rk, bm)


def _seg2(x, bb):
    hi = x.astype(bf16)
    lo = (x - hi.astype(f32)).astype(bf16)
    return jnp.dot(jnp.concatenate([hi, lo], axis=1), bb, preferred_element_type=f32)


def _row4(rows, j):
    return jnp.concatenate([jnp.broadcast_to(rows[j:j + 1, p * LANES:(p + 1) * LANES], (HEAD_DIM, LANES))
                            for p in range(4)], axis=0)


def _scan_consts():
    lane_group = jnp.arange(LANES) // HEAD_DIM
    b128 = (lane_group[:, None] == lane_group[None, :]).astype(bf16)
    bb = jnp.concatenate([b128, b128], axis=0)
    qsel = (jnp.arange(PAIR_ROWS)[:, None] % HEAD_DIM == jnp.arange(LANES)[None, :] % HEAD_DIM).astype(f32)
    return bb, qsel


def _store_cols(acc_ref, o_ref, tc):
    for p in range(4):
        blk = acc_ref[p * HEAD_DIM:(p + 1) * HEAD_DIM, :].T
        o_ref[:, (2 * p) * HEAD_DIM:(2 * p + 1) * HEAD_DIM] = blk[0:tc]
        o_ref[:, (2 * p + 1) * HEAD_DIM:(2 * p + 2) * HEAD_DIM] = blk[HEAD_DIM:HEAD_DIM + tc]


def _wkv_fwd(w, k, v, a, b, wr, br, kr):
    t = w.shape[0]
    tc = SCAN_CHUNK
    bb, qsel = _scan_consts()

    def body(w_ref, k_ref, v_ref, a_ref, b_ref, wr_ref, br_ref, kr_ref, bb_ref, q_ref, y_ref, st_ref, s_scr, yacc):
        @pl.when(pl.program_id(0) == 0)
        def _():
            s_scr[...] = jnp.zeros_like(s_scr)
        bbv, qv = bb_ref[...], q_ref[...]
        lane64 = lax.broadcasted_iota(jnp.int32, (PAIR_ROWS, LANES), 1) % HEAD_DIM

        def group(gi, s):
            base = pl.multiple_of(gi * SUBLANES, SUBLANES)
            w8, k8, v8, a8, b8, wr8, br8, kr8 = (ref[pl.ds(base, SUBLANES), :] for ref in
                                                 (w_ref, k_ref, v_ref, a_ref, b_ref, wr_ref, br_ref, kr_ref))
            for j in range(SUBLANES):
                tt = base + j
                st_ref[tt] = s
                x = jnp.concatenate([s * _row4(a8, j), s * _row4(wr8, j), qv * _row4(v8, j)], axis=0)
                r = _seg2(x, bbv)
                sa, z, vb = r[0:PAIR_ROWS], r[PAIR_ROWS:2 * PAIR_ROWS], r[2 * PAIR_ROWS:3 * PAIR_ROWS]
                ynew = z + sa * _row4(br8, j) + vb * _row4(kr8, j)
                yacc[...] = jnp.where(lane64 == tt, ynew, yacc[...])
                s = s * _row4(w8, j) + sa * _row4(b8, j) + vb * _row4(k8, j)
            return s

        s_scr[...] = lax.fori_loop(0, tc // SUBLANES, group, s_scr[...])
        _store_cols(yacc, y_ref, tc)

    row = pl.BlockSpec((tc, D_R), lambda c: (c, 0))
    return _pc(body, name="wkv_fwd", grid=(t // tc,), in_specs=[row] * 8 + [_full(bb.shape), _full(qsel.shape)],
               out_specs=(row, pl.BlockSpec((tc, PAIR_ROWS, LANES), lambda c: (c, 0, 0))),
               out_shape=(S((t, D_R), f32), S((t, PAIR_ROWS, LANES), f32)),
               scratch_shapes=[pltpu.VMEM((PAIR_ROWS, LANES), f32), pltpu.VMEM((PAIR_ROWS, LANES), f32)],
               compiler_params=_cparams(("arbitrary",)))(w, k, v, a, b, wr, br, kr, bb, qsel)


def _wkv_bwd(sprev, w, k, v, a, b, r, dy):
    t = w.shape[0]
    tc = SCAN_CHUNK
    nc = t // tc
    bb, qsel = _scan_consts()

    def body(st_ref, w_ref, k_ref, v_ref, a_ref, b_ref, r_ref, dy_ref, bb_ref, q_ref,
             dr_ref, dw_ref, dk_ref, dv_ref, da_ref, db_ref, g_scr, dvacc, rows_scr):
        @pl.when(pl.program_id(0) == 0)
        def _():
            g_scr[...] = jnp.zeros_like(g_scr)
        bbv, qv = bb_ref[...], q_ref[...]
        lane64 = lax.broadcasted_iota(jnp.int32, (PAIR_ROWS, LANES), 1) % HEAD_DIM
        outs = (dr_ref, dw_ref, db_ref, dk_ref, da_ref)

        def colsums(slot, j, x):
            for p in range(4):
                rows_scr[slot, j:j + 1, p * LANES:(p + 1) * LANES] = jnp.sum(x[p * HEAD_DIM:(p + 1) * HEAD_DIM], axis=0,
                                                                           keepdims=True)

        def group(i, g):
            base = pl.multiple_of((tc // SUBLANES - 1 - i) * SUBLANES, SUBLANES)
            w8, k8, v8, a8, b8, r8, dy8 = (ref[pl.ds(base, SUBLANES), :] for ref in
                                           (w_ref, k_ref, v_ref, a_ref, b_ref, r_ref, dy_ref))
            for j in reversed(range(SUBLANES)):
                tt = base + j
                sp = st_ref[tt]
                a4, b4, w4, k4 = _row4(a8, j), _row4(b8, j), _row4(w8, j), _row4(k8, j)
                x = jnp.concatenate([sp * a4, qv * _row4(v8, j), qv * _row4(dy8, j)], axis=0)
                rr = _seg2(x, bbv)
                u, vb, dyb = rr[0:PAIR_ROWS], rr[PAIR_ROWS:2 * PAIR_ROWS], rr[2 * PAIR_ROWS:3 * PAIR_ROWS]
                s_t = sp * w4 + u * b4 + vb * k4
                g = g + dyb * _row4(r8, j)
                rr2 = _seg2(jnp.concatenate([g * b4, g * k4], axis=0), bbv)
                du, dvb = rr2[0:PAIR_ROWS], rr2[PAIR_ROWS:2 * PAIR_ROWS]
                for slot, val in enumerate((s_t * dyb, g * sp, g * u, g * vb, sp * du)):
                    colsums(slot, j, val)
                dvacc[...] = jnp.where(lane64 == tt, dvb, dvacc[...])
                g = g * w4 + du * a4
            for slot, ref in enumerate(outs):
                ref[pl.ds(base, SUBLANES), :] = rows_scr[slot]
            return g

        g_scr[...] = lax.fori_loop(0, tc // SUBLANES, group, g_scr[...])
        _store_cols(dvacc, dv_ref, tc)

    row = pl.BlockSpec((tc, D_R), lambda c: (nc - 1 - c, 0))
    return _pc(body, name="wkv_bwd", grid=(nc,),
               in_specs=[pl.BlockSpec((tc, PAIR_ROWS, LANES), lambda c: (nc - 1 - c, 0, 0))] + [row] * 7
               + [_full(bb.shape), _full(qsel.shape)],
               out_specs=(row,) * 6, out_shape=(S((t, D_R), f32),) * 6,
               scratch_shapes=[pltpu.VMEM((PAIR_ROWS, LANES), f32), pltpu.VMEM((PAIR_ROWS, LANES), f32),
                               pltpu.VMEM((5, SUBLANES, D_R), f32)],
               compiler_params=_cparams(("arbitrary",)))(sprev, w, k, v, a, b, r, dy, bb, qsel)


def _rope_tables(t):
    half = HEAD_DIM // 2
    inv = ROPE_THETA ** (-jnp.arange(half, dtype=f32) / half)
    ang = jnp.arange(t, dtype=f32)[:, None] * inv[None, :]
    cos, sin = jnp.cos(ang), jnp.sin(ang)
    return jnp.concatenate([cos, cos], axis=1), jnp.concatenate([-sin, sin], axis=1)


def _head_matrix():
    grp = jnp.arange(D_R) // HEAD_DIM
    return (grp[:, None] == grp[None, :]).astype(f32)


def _ffn_fwd(h, g, w_up_t, conv_w, conv_b, w_down, i):
    hf = _rms_fwd(h, g, f"ffn{i}_norm")
    u = _mm(hf, w_up_t, "nt", f"ffn{i}_up")
    z = _ffn_mid(u, conv_w, conv_b, f"ffn{i}_mid")
    return _mm(z, w_down, "nn", f"ffn{i}_down", res=h), (hf, u, z)


def _ffn_bwd(dh, h, saved, g, w_up_t, conv_w, conv_b, w_down, i):
    hf, u, z = saved
    dz = _mm(dh, w_down, "nt", f"ffn{i}_dz")
    g_down = _mm(z, dh, "tn", f"ffn{i}_gdown", out_dtype=GRAD_WIRE_DTYPE)
    du, g_conv, g_convb = _ffn_mid_bwd(dz, u, conv_w, conv_b, f"ffn{i}_mid_bwd")
    g_up_t = _mm(du, hf, "tn", f"ffn{i}_gup", out_dtype=GRAD_WIRE_DTYPE)
    dhf = _mm(du, w_up_t, "nn", f"ffn{i}_dhf")
    dh_in, g_norm = _rms_bwd(dhf, h, g, dh, f"ffn{i}_norm_bwd")
    return dh_in, dict(up_t=g_up_t, down=g_down, conv=g_conv, conv_b=g_convb, norm=g_norm)


def _local_step(x, target, W):
    t = N_META + x.shape[0]
    c64, s64 = _rope_tables(t)
    bm = _head_matrix()
    h0 = jnp.concatenate([W["meta_tokens"], x], axis=0)

    hn0 = _rms_fwd(h0, W["norm_mix"][0], "mix0_norm")
    p0 = _mm(hn0, W["ev_w_in_t"], "nt", "ev_in")
    uc = _ev_a_conv(p0, W["ev_conv_a"])
    y0 = jnp.concatenate([_ev_a_norm(uc, W["ev_ln_a_g"], W["ev_ln_a_b"]), _ev_b(p0, W["ev_conv_b"])], axis=1)
    h1 = _mm(y0, W["ev_w_out"], "nn", "ev_out", res=h0)
    h2, ffn0 = _ffn_fwd(h1, W["norm_ffn"][0], W["ff_w_up_t"][0], W["ff_conv"][0], W["ff_conv_b"][0], W["ff_w_down"][0], 0)

    hn1 = _rms_fwd(h2, W["norm_mix"][1], "mix1_norm")
    p1 = _mm(hn1, W["od_w_in_t"], "nt", "od_in")
    pr = p1[:, ATT_COLS:]
    qp, kp, vp = _rope_pack(p1[:, :ATT_COLS], c64, s64)
    op = _attn_fwd(qp, kp, vp, W["od_sinks"])
    prep_params = [W[n] for n in _PREP_PARAMS]
    xr, xv, decay, k2, a_s, b_s, wr, br, kr, gate = _rwkv_prep(pr, W["od_mu"], prep_params, bm)
    ysc, sprev = _wkv_fwd(decay, k2, xv, a_s, b_s, wr, br, kr)
    rk = W["od_r_k"].reshape(1, D_R)
    yr = _rwkv_post(ysc, xr, k2, xv, gate, W["od_lnx_g"], W["od_lnx_b"], rk, bm)
    y1 = jnp.concatenate([op[ATT_PAD:], yr.astype(bf16)], axis=1)
    h3 = _mm(y1, W["od_w_out"], "nn", "od_out", res=h2)
    h4, ffn1 = _ffn_fwd(h3, W["norm_ffn"][1], W["ff_w_up_t"][1], W["ff_conv"][1], W["ff_conv_b"][1], W["ff_w_down"][1], 1)

    tgt = jnp.concatenate([jnp.zeros((N_META, D_MODEL), f32), target], axis=0)
    loss, dh4, g_norm_final = _final_loss(h4, W["norm_final"], tgt)

    dh3, gf1 = _ffn_bwd(dh4, h3, ffn1, W["norm_ffn"][1], W["ff_w_up_t"][1], W["ff_conv"][1], W["ff_conv_b"][1],
                        W["ff_w_down"][1], 1)
    dy1 = _mm(dh3, W["od_w_out"], "nt", "od_dy")
    g_od_w_out = _mm(y1, dh3, "tn", "od_gout", out_dtype=GRAD_WIRE_DTYPE)
    dysc, dxr_p, dk2_p, dxv_p, dgate, g_lnx_g, g_lnx_b, g_rk = _rwkv_post_bwd(
        dy1, ysc, xr, k2, xv, gate, W["od_lnx_g"], W["od_lnx_b"], rk, bm)
    dr, dw, dk, dv, da, db = _wkv_bwd(sprev, decay, k2, xv, a_s, b_s, xr, dysc)
    prep_grads = _rwkv_prep_bwd(pr, W["od_mu"], prep_params, bm,
                                [[dw], [dk, dk2_p], [da], [db], [dgate], [dr, dxr_p], [dv, dxv_p]])
    dxs, g_mu = prep_grads[0], prep_grads[1]
    dpr = _shift_bwd(dxs, W["od_mu"])
    dop = jnp.concatenate([jnp.zeros((ATT_PAD, D_ATT), f32), dy1[:, :D_ATT]], axis=0).astype(bf16)
    dqp, dkp, dvp, dsk = _attn_bwd(qp, kp, vp, W["od_sinks"], dop)
    dp1 = jnp.concatenate([_rope_bwd(dqp, dkp, dvp, c64, s64), dpr], axis=1)
    g_od_w_in_t = _mm(dp1, hn1, "tn", "od_gin", out_dtype=GRAD_WIRE_DTYPE)
    dhn1 = _mm(dp1, W["od_w_in_t"], "nn", "od_dhn")
    dh2, g_norm_mix1 = _rms_bwd(dhn1, h2, W["norm_mix"][1], dh3, "mix1_norm_bwd")

    dh1, gf0 = _ffn_bwd(dh2, h1, ffn0, W["norm_ffn"][0], W["ff_w_up_t"][0], W["ff_conv"][0], W["ff_conv_b"][0],
                        W["ff_w_down"][0], 0)
    dy0 = _mm(dh1, W["ev_w_out"], "nt", "ev_dy")
    g_ev_w_out = _mm(y0, dh1, "tn", "ev_gout", out_dtype=GRAD_WIRE_DTYPE)
    duc, g_ln_g, g_ln_b = _ev_a_norm_bwd(dy0, uc, W["ev_ln_a_g"], W["ev_ln_a_b"])
    dav, dag, g_conv_a = _ev_a_conv_bwd(duc, p0, W["ev_conv_a"])
    dgb, dgc, dxi, g_conv_b = _ev_b_bwd(dy0, p0, W["ev_conv_b"])
    dp0 = jnp.concatenate([dav, dag, dgb, dgc, dxi], axis=1)
    g_ev_w_in_t = _mm(dp0, hn0, "tn", "ev_gin", out_dtype=GRAD_WIRE_DTYPE)
    dhn0 = _mm(dp0, W["ev_w_in_t"], "nn", "ev_dhn")
    dh0, g_norm_mix0 = _rms_bwd(dhn0, h0, W["norm_mix"][0], dh1, "mix0_norm_bwd")

    G = dict(
        meta_tokens=dh0[:N_META], norm_mix=jnp.concatenate([g_norm_mix0, g_norm_mix1], axis=0),
        norm_ffn=jnp.concatenate([gf0["norm"], gf1["norm"]], axis=0), norm_final=g_norm_final.reshape(D_MODEL),
        ev_w_in_t=g_ev_w_in_t, ev_conv_a=g_conv_a, ev_ln_a_g=g_ln_g, ev_ln_a_b=g_ln_b, ev_conv_b=g_conv_b,
        ev_w_out=g_ev_w_out, od_w_in_t=g_od_w_in_t, od_sinks=dsk[:, :N_Q_HEADS], od_mu=g_mu,
        od_lnx_g=g_lnx_g, od_lnx_b=g_lnx_b, od_r_k=g_rk.reshape(N_Q_HEADS, HEAD_DIM), od_w_out=g_od_w_out,
        ff_w_up_t=[gf0["up_t"], gf1["up_t"]], ff_w_down=[gf0["down"], gf1["down"]],
        ff_conv=jnp.stack([gf0["conv"], gf1["conv"]]), ff_conv_b=jnp.concatenate([gf0["conv_b"], gf1["conv_b"]], axis=0),
    )
    for name, gval in zip(_PREP_PARAMS, prep_grads[2:]):
        G[name] = gval
    return loss, dh0[N_META:], G


HBM = pl.BlockSpec(memory_space=pl.ANY)


def _mesh_pos():
    return lax.axis_index("x"), lax.axis_index("y"), lax.axis_index("c")


def _dev(px, py, pc):
    return 4 * px + 2 * py + pc


def _all_gather(xs, name):
    n = len(xs)

    def body(*refs):
        x_refs, o_refs = refs[:n], refs[n:2 * n]
        send_sems, recv_sems, local_sems = refs[2 * n:]
        x, y, c = _mesh_pos()
        me, sibling = (x, y, c), (x, y, 1 - c)
        chips = [(1 - x, y), (x, 1 - y), (1 - x, 1 - y)]

        def copy(i, k, block, to, from_input=False):
            dst = o_refs[i].at[_dev(*block)]
            return pltpu.make_async_remote_copy(src_ref=x_refs[i] if from_input else dst, dst_ref=dst,
                                                send_sem=send_sems.at[i, k], recv_sem=recv_sems.at[i, k],
                                                device_id=to, device_id_type=MESH)

        mine = [pltpu.make_async_copy(x_refs[i], o_refs[i].at[_dev(*me)], local_sems.at[i]) for i in range(n)]
        for cp in mine:
            cp.start()
        first = []
        for i in range(n):
            first.append(copy(i, 0, me, sibling, True))
            first += [copy(i, 1 + j, me, (*chip, c), True) for j, chip in enumerate(chips)]
        for cp in first:
            cp.start()
        passed = []
        for j, chip in enumerate(chips):
            for i in range(n):
                copy(i, 1 + j, (*chip, c), me).wait_recv()
                fwd = copy(i, 4 + j, (*chip, c), sibling)
                fwd.start()
                passed.append(fwd)
        for i in range(n):
            copy(i, 0, sibling, me).wait_recv()
            for j, chip in enumerate(chips):
                copy(i, 4 + j, (*chip, 1 - c), me).wait_recv()
        for cp in first + passed:
            cp.wait_send()
        for cp in mine:
            cp.wait()

    return _pc(body, name=name, in_specs=[HBM] * n, out_specs=tuple([HBM] * n),
               out_shape=tuple(S((N_DEV,) + x.shape, x.dtype) for x in xs),
               scratch_shapes=[pltpu.SemaphoreType.DMA((n, 7)), pltpu.SemaphoreType.DMA((n, 7)),
                               pltpu.SemaphoreType.DMA((n,))])(*xs)


def _rs_d2d(gs, name):
    n = len(gs)

    def body(*refs):
        g_refs, o_refs = refs[:n], refs[n:2 * n]
        send_sems, recv_sems = refs[2 * n:]
        x, y, c = _mesh_pos()
        copies = []
        for i in range(n):
            for q in range(4):
                cp = pltpu.make_async_remote_copy(src_ref=g_refs[i].at[2 * q + (1 - c)], dst_ref=o_refs[i].at[q],
                                                  send_sem=send_sems.at[i, q], recv_sem=recv_sems.at[i, q],
                                                  device_id=(x, y, 1 - c), device_id_type=MESH)
                cp.start()
                copies.append(cp)
        for cp in copies:
            cp.wait()

    return _pc(body, name=name, in_specs=[HBM] * n, out_specs=tuple([HBM] * n),
               out_shape=tuple(S((4,) + g.shape[1:], g.dtype) for g in gs),
               scratch_shapes=[pltpu.SemaphoreType.DMA((n, 4)), pltpu.SemaphoreType.DMA((n, 4))])(*gs)


def _rs_add(g, r1, c_vec, name):
    _, r, cols = g.shape
    tr = _divisor_block(r, 16, min(r, 352))

    def body(c_ref, g_ref, r_ref, o_ref):
        o_ref[...] = (g_ref[...].astype(f32) + r_ref[...].astype(f32)).astype(o_ref.dtype)

    blk = lambda f: pl.BlockSpec((1, tr, cols), f)
    grid_spec = pltpu.PrefetchScalarGridSpec(
        num_scalar_prefetch=1, grid=(4, r // tr),
        in_specs=[blk(lambda q, i, c_ref: (2 * q + c_ref[0], i, 0)), blk(lambda q, i, c_ref: (q, i, 0))],
        out_specs=blk(lambda q, i, c_ref: (q, i, 0)))
    return _pc(body, name=name, grid_spec=grid_spec, out_shape=S((4, r, cols), g.dtype),
               compiler_params=_cparams(("arbitrary", "arbitrary")))(c_vec, g, r1)


def _rs_ici(ps, name):
    n = len(ps)

    def body(*refs):
        p_refs, o_refs = refs[:n], refs[n:2 * n]
        send_sems, recv_sems = refs[2 * n:]
        x, y, c = _mesh_pos()
        chips = [(1 - x, y), (x, 1 - y), (1 - x, 1 - y)]
        copies = []
        for i in range(n):
            for j, (qx, qy) in enumerate(chips):
                cp = pltpu.make_async_remote_copy(src_ref=p_refs[i].at[2 * qx + qy], dst_ref=o_refs[i].at[j],
                                                  send_sem=send_sems.at[i, j], recv_sem=recv_sems.at[i, j],
                                                  device_id=(qx, qy, c), device_id_type=MESH)
                cp.start()
                copies.append(cp)
        for cp in copies:
            cp.wait()

    return _pc(body, name=name, in_specs=[HBM] * n, out_specs=tuple([HBM] * n),
               out_shape=tuple(S((3,) + p.shape[1:], p.dtype) for p in ps),
               scratch_shapes=[pltpu.SemaphoreType.DMA((n, 3)), pltpu.SemaphoreType.DMA((n, 3))])(*ps)


def _rs_final(p, r2, q_vec, name):
    _, r, cols = p.shape
    tr = _divisor_block(r, 16, min(r, 352))

    def body(q_ref, p_ref, a_ref, b_ref, c_ref, o_ref):
        o_ref[...] = ((p_ref[0].astype(f32) + a_ref[0].astype(f32)) + b_ref[0].astype(f32)) + c_ref[0].astype(f32)

    blk = lambda f: pl.BlockSpec((1, tr, cols), f)
    grid_spec = pltpu.PrefetchScalarGridSpec(
        num_scalar_prefetch=1, grid=(r // tr,),
        in_specs=[blk(lambda i, q_ref: (q_ref[0], i, 0))] + [blk(lambda i, q_ref, j=j: (j, i, 0)) for j in range(3)],
        out_specs=pl.BlockSpec((tr, cols), lambda i, q_ref: (i, 0)))
    return _pc(body, name=name, grid_spec=grid_spec, out_shape=S((r, cols), f32),
               compiler_params=_cparams(("arbitrary",)))(q_vec, p, r2, r2, r2)


def _sum_devices(a):
    def body(a_ref, o_ref):
        acc = a_ref[0]
        for d in range(1, N_DEV):
            acc = acc + a_ref[d]
        o_ref[...] = acc

    return _pc(body, name="sum_small_grads", grid=(1,), in_specs=[_full(a.shape)], out_specs=_full(a.shape[1:]),
               out_shape=S(a.shape[1:], a.dtype), compiler_params=_cparams(("arbitrary",)))(a)


def _adamw(w, m, v, g, name):
    shape = w.shape
    w2, m2, v2, g2 = (a.reshape(-1, shape[-1]) for a in (w, m, v, g))
    rows, cols = w2.shape
    tr = rows if rows % SUBLANES else _divisor_block(rows, SUBLANES, max(SUBLANES, min(rows, ADAMW_BLOCK_ELEMS // cols)))
    c1, c2 = 1.0 - ADAM_B1 ** ADAM_STEP, 1.0 - ADAM_B2 ** ADAM_STEP

    def body(w_ref, m_ref, v_ref, g_ref, d_ref, nm_ref, nv_ref):
        gv = g_ref[...]
        nm = ADAM_B1 * m_ref[...] + (1.0 - ADAM_B1) * gv
        nv = ADAM_B2 * v_ref[...] + (1.0 - ADAM_B2) * (gv * gv)
        d_ref[...] = -ADAM_LR * ((nm / c1) / (jnp.sqrt(nv / c2) + ADAM_EPS) + ADAM_WD * w_ref[...])
        nm_ref[...] = nm
        nv_ref[...] = nv

    blk = pl.BlockSpec((tr, cols), lambda i: (i, 0))
    outs = _pc(body, name=name, grid=(rows // tr,), in_specs=[blk] * 4, out_specs=(blk,) * 3,
               out_shape=(S((rows, cols), f32),) * 3, compiler_params=_cparams(("arbitrary",)))(w2, m2, v2, g2)
    return tuple(o.reshape(shape) for o in outs)


_WEIGHTS = ("meta_tokens", "norm_mix", "norm_ffn", "norm_final", "ev_w_in", "ev_conv_a", "ev_ln_a_g", "ev_ln_a_b",
            "ev_conv_b", "ev_w_out", "od_w_in", "od_sinks", "od_mu", "od_w0", "od_w2", "od_a0", "od_a2", "od_g2",
            "od_k_k", "od_k_a", "od_r_k", "od_lnx_g", "od_lnx_b", "od_w_out", "ff_w_up", "ff_conv", "ff_conv_b", "ff_w_down")
_SMALL_SHARDED = (("meta_tokens", 1), ("ev_conv_a", 2), ("ev_conv_b", 2), ("od_mu", 1), ("od_w0", 1), ("od_w2", 2),
                  ("od_a0", 1), ("od_a2", 2), ("od_g2", 2), ("od_k_k", 1), ("od_k_a", 1), ("od_lnx_g", 1),
                  ("od_lnx_b", 1), ("ff_conv", 2))
_SMALL_REPLICATED = ("norm_mix", "norm_ffn", "norm_final", "ev_ln_a_g", "ev_ln_a_b", "od_sinks", "od_r_k", "ff_conv_b")
SLAB_UNIT = SUBLANES * LANES


def _pack(arrs):
    flat = jnp.concatenate([a.reshape(-1).astype(f32) for a in arrs])
    pad = (-flat.shape[0]) % SLAB_UNIT
    return jnp.pad(flat, (0, pad)).reshape(-1, LANES)


def _unpack(flat, shapes):
    out, off = [], 0
    for shp in shapes:
        size = 1
        for s in shp:
            size *= s
        out.append(flat[..., off:off + size].reshape(flat.shape[:-1] + tuple(shp)))
        off += size
    return out


def _full_shape(shape, axis):
    return tuple(N_DEV * s if i == axis else s for i, s in enumerate(shape))


def kernel(x, meta_tokens, norm_mix, norm_ffn, norm_final, ev_w_in, ev_conv_a, ev_ln_a_g, ev_ln_a_b, ev_conv_b, ev_w_out, od_w_in, od_sinks, od_mu, od_w0, od_w2, od_a0, od_a2, od_g2, od_k_k, od_k_a, od_r_k, od_lnx_g, od_lnx_b, od_w_out, ff_w_up, ff_conv, ff_conv_b, ff_w_down, loss_target, m_meta_tokens, m_norm_mix, m_norm_ffn, m_norm_final, m_ev_w_in, m_ev_conv_a, m_ev_ln_a_g, m_ev_ln_a_b, m_ev_conv_b, m_ev_w_out, m_od_w_in, m_od_sinks, m_od_mu, m_od_w0, m_od_w2, m_od_a0, m_od_a2, m_od_g2, m_od_k_k, m_od_k_a, m_od_r_k, m_od_lnx_g, m_od_lnx_b, m_od_w_out, m_ff_w_up, m_ff_conv, m_ff_conv_b, m_ff_w_down, v_meta_tokens, v_norm_mix, v_norm_ffn, v_norm_final, v_ev_w_in, v_ev_conv_a, v_ev_ln_a_g, v_ev_ln_a_b, v_ev_conv_b, v_ev_w_out, v_od_w_in, v_od_sinks, v_od_mu, v_od_w0, v_od_w2, v_od_a0, v_od_a2, v_od_g2, v_od_k_k, v_od_k_a, v_od_r_k, v_od_lnx_g, v_od_lnx_b, v_od_w_out, v_ff_w_up, v_ff_conv, v_ff_conv_b, v_ff_w_down):
    A = dict(locals())
    px, py, pc = _mesh_pos()
    me = _dev(px, py, pc)
    c_vec = jnp.reshape(pc, (1,)).astype(jnp.int32)
    q_vec = jnp.reshape(2 * px + py, (1,)).astype(jnp.int32)

    big = [ev_w_in[0].T, ev_w_out[0], od_w_in[0].T, od_w_out[0], ff_w_up[0].T, ff_w_up[1].T, ff_w_down[0], ff_w_down[1]]
    small_shapes = [A[n].shape for n, _ in _SMALL_SHARDED]
    gathered = _all_gather([b.astype(bf16) for b in big] + [_pack([A[n] for n, _ in _SMALL_SHARDED])], "gather_params")
    fb = [g.reshape(N_DEV * g.shape[1], g.shape[2]) for g in gathered[:-1]]
    W = dict(ev_w_in_t=fb[0], ev_w_out=fb[1], od_w_in_t=fb[2], od_w_out=fb[3], ff_w_up_t=[fb[4], fb[5]], ff_w_down=[fb[6], fb[7]])
    for (n, ax), seg in zip(_SMALL_SHARDED, _unpack(gathered[-1].reshape(N_DEV, -1), small_shapes)):
        W[n] = jnp.moveaxis(seg, 0, ax).reshape(_full_shape(A[n].shape, ax))
    for n in ("ev_conv_a", "ev_conv_b", "od_w2", "od_a2", "od_g2"):
        W[n] = W[n][0]
    for n in _SMALL_REPLICATED:
        W[n] = A[n]
    W["od_r_k"] = od_r_k[0]

    loss_tile, grad_x, G = _local_step(x[0], loss_target[0], W)

    gbig = [G["ev_w_in_t"], G["ev_w_out"], G["od_w_in_t"], G["od_w_out"], G["ff_w_up_t"][0], G["ff_w_up_t"][1],
            G["ff_w_down"][0], G["ff_w_down"][1]]
    gbig = [g.reshape(N_DEV, g.shape[0] // N_DEV, g.shape[1]) for g in gbig]
    r1 = _rs_d2d(gbig, "rs_sibling")
    ps = [_rs_add(g, r, c_vec, f"rs_add{i}") for i, (g, r) in enumerate(zip(gbig, r1))]
    r2 = _rs_ici(ps, "rs_chips")
    gsh = [_rs_final(p, r, q_vec, f"rs_final{i}") for i, (p, r) in enumerate(zip(ps, r2))]
    grads = dict(ev_w_in=gsh[0].T[None], ev_w_out=gsh[1][None], od_w_in=gsh[2].T[None], od_w_out=gsh[3][None],
                 ff_w_up=jnp.stack([gsh[4].T, gsh[5].T]), ff_w_down=jnp.stack([gsh[6], gsh[7]]))

    small_names = [n for n, _ in _SMALL_SHARDED] + list(_SMALL_REPLICATED)
    small_full_shapes = [_full_shape(A[n].shape, ax) for n, ax in _SMALL_SHARDED] + [A[n].shape for n in _SMALL_REPLICATED]
    (gsm,) = _all_gather([_pack([G[n] for n in small_names])], "gather_small_grads")
    summed = _unpack(_sum_devices(gsm).reshape(-1), small_full_shapes)
    for n, full in zip(small_names, summed):
        grads[n] = full
    for n, ax in _SMALL_SHARDED:
        size = A[n].shape[ax]
        grads[n] = lax.dynamic_slice_in_dim(grads[n], me * size, size, axis=ax)

    delta, new_m, new_v = {}, {}, {}
    for n in _WEIGHTS:
        delta[n], new_m[n], new_v[n] = _adamw(A[n], A["m_" + n], A["v_" + n], grads[n], "adamw_" + n)

    loss = lax.psum(loss_tile[0, 0], ("x", "y", "c"))
    return (loss, grad_x[None], *[grads[n] for n in _WEIGHTS], *[delta[n] for n in _WEIGHTS],
            *[new_m[n] for n in _WEIGHTS], *[new_v[n] for n in _WEIGHTS])
```

```python
import jax
import jax.numpy as jnp
from jax import lax
from jax.experimental import pallas as pl
from jax.experimental.pallas import tpu as pltpu

f32, bf16 = jnp.float32, jnp.bfloat16

D_MODEL = 1024
N_META = 16
RMS_EPS = 1e-6
LN_EPS = 1e-5
D_A = 512
CONV_A_WIDTH = 31
CONV_B_WIDTH = 3
HEAD_DIM = 64
N_Q_HEADS = 8
N_KV_HEADS = 2
GQA_GROUP = 4
D_ATT = 512
D_KV = 128
BLOCK = 128
ROPE_THETA = 10000.0
D_R = 512
LORA_W, LORA_A, LORA_G = 64, 64, 128
RWKV_GN_EPS = 64e-5
ATT_COLS = D_ATT + 2 * D_KV
RWKV_COLS = 3 * D_R + LORA_W + LORA_A + LORA_G
D_FF = 2816
FF_CONV_WIDTH = 3
NEG_INF = -1e30
ATT_PAD = BLOCK - N_META
ATT_SCALE = HEAD_DIM ** -0.5

ADAM_LR, ADAM_B1, ADAM_B2, ADAM_EPS, ADAM_WD, ADAM_STEP = 0.001, 0.9, 0.999, 1e-08, 0.01, 10

N_DEV = 8
LANES = 128
SUBLANES = 8
SCAN_CHUNK = 48
PAIR_ROWS = 4 * HEAD_DIM
V7X_VMEM_LIMIT = 56 * 1024 * 1024
ADAMW_BLOCK_ELEMS = 400 * 1024
GRAD_WIRE_DTYPE = bf16
MESH = pl.DeviceIdType.MESH
S = jax.ShapeDtypeStruct
HIGHEST = lax.Precision.HIGHEST


def _pc(body, **kw):
    return pl.pallas_call(body, **kw)


def _cparams(sem=None):
    return pltpu.CompilerParams(dimension_semantics=sem, vmem_limit_bytes=V7X_VMEM_LIMIT)


def _divisor_block(t, unit, limit):
    best = unit
    for rb in range(unit, limit + 1, unit):
        if t % rb == 0:
            best = rb
    assert t % best == 0, (t, unit)
    return best


def _row_block(t):
    return _divisor_block(t, 16, 704)


def _row_block8(t):
    return _divisor_block(t, 8, 344)


def _col_tile(n):
    for t in (512, 256, 128):
        if n % t == 0:
            return t
    return n


def _full(shape):
    nd = len(shape)
    return pl.BlockSpec(shape, lambda *_: (0,) * nd)


def _sigmoid(x):
    return jax.nn.sigmoid(x)


_DIMS = {"nn": (((1,), (0,)), ((), ())), "nt": (((1,), (1,)), ((), ())), "tn": (((0,), (0,)), ((), ()))}
MM_MAX_K = 2816


def _mm(a, b, mode, name, out_dtype=f32, res=None):
    if mode == "nn":
        (m, k), (k2, n) = a.shape, b.shape
    elif mode == "nt":
        (m, k), (n, k2) = a.shape, b.shape
    else:
        (k, m), (k2, n) = a.shape, b.shape
    assert k == k2, (a.shape, b.shape, mode)
    tm = _row_block(m) if m % LANES else _col_tile(m)
    tn = _col_tile(n)
    nk = 1 if (mode == "tn" or k <= MM_MAX_K) else k // MM_MAX_K
    tk = k // nk
    assert tk * nk == k
    dims = _DIMS[mode]

    def body(a_ref, b_ref, *rest):
        part = lax.dot_general(a_ref[...].astype(bf16), b_ref[...].astype(bf16), dims, preferred_element_type=f32)
        if nk == 1:
            o_ref = rest[-1]
            if res is not None:
                part = part + rest[0][...]
            o_ref[...] = part.astype(out_dtype)
            return
        o_ref, acc_ref = rest[-2], rest[-1]
        kk = pl.program_id(2)

        @pl.when(kk == 0)
        def _():
            acc_ref[...] = part

        @pl.when(kk > 0)
        def _():
            acc_ref[...] += part

        @pl.when(kk == nk - 1)
        def _():
            acc = acc_ref[...]
            if res is not None:
                acc = acc + rest[0][...]
            o_ref[...] = acc.astype(out_dtype)

    if mode == "tn":
        a_spec = pl.BlockSpec((k, tm), lambda i, j, kk: (0, i))
    else:
        a_spec = pl.BlockSpec((tm, tk), lambda i, j, kk: (i, kk))
    if mode == "nt":
        b_spec = pl.BlockSpec((tn, tk), lambda i, j, kk: (j, kk))
    else:
        b_spec = pl.BlockSpec((tk, tn), lambda i, j, kk: (kk, j))
    o_spec = pl.BlockSpec((tm, tn), lambda i, j, kk: (i, j))
    ins, specs = [a, b], [a_spec, b_spec]
    if res is not None:
        ins.append(res)
        specs.append(o_spec)
    scratch = [pltpu.VMEM((tm, tn), f32)] if nk > 1 else []
    return _pc(body, name=name, grid=(m // tm, n // tn, nk), in_specs=specs, out_specs=o_spec,
               out_shape=S((m, n), out_dtype), scratch_shapes=scratch,
               compiler_params=_cparams(("arbitrary", "arbitrary", "arbitrary")))(*ins)


def _rms_fwd(x, g, name):
    t, d = x.shape
    rb = _row_block(t)

    def body(x_ref, g_ref, o_ref):
        xv = x_ref[...]
        rstd = lax.rsqrt(jnp.mean(xv * xv, axis=-1, keepdims=True) + RMS_EPS)
        o_ref[...] = (xv * rstd * g_ref[...]).astype(bf16)

    row = pl.BlockSpec((rb, d), lambda i: (i, 0))
    return _pc(body, name=name, grid=(t // rb,), in_specs=[row, _full((1, d))], out_specs=row,
               out_shape=S((t, d), bf16), compiler_params=_cparams(("arbitrary",)))(x, g.reshape(1, d))


def _rms_bwd(dy, x, g, dres, name):
    t, d = x.shape
    rb = _row_block8(t)

    def body(dy_ref, x_ref, g_ref, dres_ref, dx_ref, dg_ref):
        @pl.when(pl.program_id(0) == 0)
        def _():
            dg_ref[...] = jnp.zeros_like(dg_ref)
        xv, dyv = x_ref[...], dy_ref[...]
        rstd = lax.rsqrt(jnp.mean(xv * xv, axis=-1, keepdims=True) + RMS_EPS)
        xn = xv * rstd
        dg_ref[...] += jnp.sum(dyv * xn, axis=0, keepdims=True)
        dxh = dyv * g_ref[...]
        dx_ref[...] = dres_ref[...] + rstd * (dxh - xn * jnp.mean(dxh * xn, axis=-1, keepdims=True))

    row = pl.BlockSpec((rb, d), lambda i: (i, 0))
    return _pc(body, name=name, grid=(t // rb,), in_specs=[row, row, _full((1, d)), row],
               out_specs=(row, _full((1, d))), out_shape=(S((t, d), f32), S((1, d), f32)),
               compiler_params=_cparams(("arbitrary",)))(dy, x, g.reshape(1, d), dres)


def _final_loss(h, g, target_padded):
    t, d = h.shape
    rb = _row_block8(t)

    def body(x_ref, g_ref, t_ref, loss_ref, dx_ref, dg_ref):
        i = pl.program_id(0)

        @pl.when(i == 0)
        def _():
            dg_ref[...] = jnp.zeros_like(dg_ref)
            loss_ref[...] = jnp.zeros_like(loss_ref)
        xv = x_ref[...]
        rstd = lax.rsqrt(jnp.mean(xv * xv, axis=-1, keepdims=True) + RMS_EPS)
        xn = xv * rstd
        gv = g_ref[...]
        row = i * rb + lax.broadcasted_iota(jnp.int32, (rb, 1), 0)
        diff = jnp.where(row >= N_META, xn * gv - t_ref[...], 0.0)
        loss_ref[...] += 0.5 * jnp.sum(jnp.mean(diff * diff, axis=-1, keepdims=True))
        dout = diff * (1.0 / d)
        dg_ref[...] += jnp.sum(dout * xn, axis=0, keepdims=True)
        dxh = dout * gv
        dx_ref[...] = rstd * (dxh - xn * jnp.mean(dxh * xn, axis=-1, keepdims=True))

    row = pl.BlockSpec((rb, d), lambda i: (i, 0))
    return _pc(body, name="final_loss", grid=(t // rb,), in_specs=[row, _full((1, d)), row],
               out_specs=(_full((SUBLANES, LANES)), row, _full((1, d))),
               out_shape=(S((SUBLANES, LANES), f32), S((t, d), f32), S((1, d), f32)),
               compiler_params=_cparams(("arbitrary",)))(h, g.reshape(1, d), target_padded)


CONV_LEAD = 32


def _fill_front_padded(pad_ref, x, t):
    pad_ref[0:CONV_LEAD, :] = jnp.zeros((CONV_LEAD, x.shape[1]), f32)
    pad_ref[CONV_LEAD:CONV_LEAD + t, :] = x


def _fill_back_padded(pad_ref, x, t):
    pad_ref[0:t, :] = x
    pad_ref[t:t + CONV_LEAD, :] = jnp.zeros((CONV_LEAD, x.shape[1]), f32)


def _conv_rows(pad_ref, w_ref, kw, r0, nr):
    acc = None
    for j in range(kw):
        lo = CONV_LEAD + r0 - (kw - 1) + j
        term = w_ref[j:j + 1, :] * pad_ref[lo:lo + nr, :]
        acc = term if acc is None else acc + term
    return acc


def _conv_t_rows(padb_ref, w_ref, kw, r0, nr):
    acc = None
    for j in range(kw):
        lo = r0 + (kw - 1) - j
        term = w_ref[j:j + 1, :] * padb_ref[lo:lo + nr, :]
        acc = term if acc is None else acc + term
    return acc


def _conv_dw_rows(dy_blk, pad_ref, kw, r0, nr):
    out = []
    for j in range(kw):
        lo = CONV_LEAD + r0 - (kw - 1) + j
        out.append(jnp.sum(dy_blk * pad_ref[lo:lo + nr, :], axis=0, keepdims=True))
    return out


def _acc_list(a, b):
    return b if a is None else [x + y for x, y in zip(a, b)]


def _ev_a_conv(p, conv_a):
    t = p.shape[0]
    cr = _row_block8(t)
    nb = D_A // LANES

    def body(av_ref, ag_ref, w_ref, o_ref, pad_ref):
        _fill_front_padded(pad_ref, av_ref[...] * _sigmoid(ag_ref[...]), t)
        for r in range(t // cr):
            o_ref[r * cr:(r + 1) * cr, :] = _conv_rows(pad_ref, w_ref, CONV_A_WIDTH, r * cr, cr)

    col = lambda off: pl.BlockSpec((t, LANES), lambda j: (0, j + off))
    return _pc(body, name="ev_a_conv", grid=(nb,),
               in_specs=[col(0), col(nb), pl.BlockSpec((CONV_A_WIDTH, LANES), lambda j: (0, j))],
               out_specs=col(0), out_shape=S((t, D_A), f32),
               scratch_shapes=[pltpu.VMEM((t + CONV_LEAD, LANES), f32)],
               compiler_params=_cparams(("arbitrary",)))(p, p, conv_a)


def _ln_silu(uc, g, b):
    mu = jnp.mean(uc, axis=-1, keepdims=True)
    xc = uc - mu
    var = jnp.mean(xc * xc, axis=-1, keepdims=True)
    y = xc * lax.rsqrt(var + LN_EPS) * g + b
    return y * _sigmoid(y)


def _ev_a_norm(uc, g, b):
    t, d = uc.shape
    rb = _row_block(t)

    def body(u_ref, g_ref, b_ref, o_ref):
        o_ref[...] = _ln_silu(u_ref[...], g_ref[...], b_ref[...]).astype(bf16)

    row = pl.BlockSpec((rb, d), lambda i: (i, 0))
    return _pc(body, name="ev_a_norm", grid=(t // rb,), in_specs=[row, _full((1, d)), _full((1, d))],
               out_specs=row, out_shape=S((t, d), bf16), compiler_params=_cparams(("arbitrary",)))(uc, g, b)


def _ev_a_norm_bwd(dy, uc, g, b):
    t, d = uc.shape
    rb = _row_block8(t)

    def body(dy_ref, u_ref, g_ref, b_ref, du_ref, dg_ref, db_ref):
        @pl.when(pl.program_id(0) == 0)
        def _():
            dg_ref[...] = jnp.zeros_like(dg_ref)
            db_ref[...] = jnp.zeros_like(db_ref)
        _, vjp = jax.vjp(_ln_silu, u_ref[...], g_ref[...], b_ref[...])
        du, dg, db = vjp(dy_ref[...])
        du_ref[...] = du
        dg_ref[...] += dg
        db_ref[...] += db

    row = pl.BlockSpec((rb, d), lambda i: (i, 0))
    return _pc(body, name="ev_a_norm_bwd", grid=(t // rb,), in_specs=[row, row, _full((1, d)), _full((1, d))],
               out_specs=(row, _full((1, d)), _full((1, d))),
               out_shape=(S((t, d), f32), S((1, d), f32), S((1, d), f32)),
               compiler_params=_cparams(("arbitrary",)))(dy, uc, g, b)


def _ev_a_conv_bwd(duc, p, conv_a):
    t = p.shape[0]
    cr = _row_block8(t)
    nb = D_A // LANES

    def body(dy_ref, av_ref, ag_ref, w_ref, dav_ref, dag_ref, dw_ref, pad_ref, padb_ref):
        _fill_front_padded(pad_ref, av_ref[...] * _sigmoid(ag_ref[...]), t)
        _fill_back_padded(padb_ref, dy_ref[...], t)
        dw = None
        for r in range(t // cr):
            rows = slice(r * cr, (r + 1) * cr)
            du = _conv_t_rows(padb_ref, w_ref, CONV_A_WIDTH, r * cr, cr)
            avr = av_ref[rows, :]
            sgr = _sigmoid(ag_ref[rows, :])
            dav_ref[rows, :] = du * sgr
            dag_ref[rows, :] = du * avr * sgr * (1.0 - sgr)
            dw = _acc_list(dw, _conv_dw_rows(dy_ref[rows, :], pad_ref, CONV_A_WIDTH, r * cr, cr))
        for j in range(CONV_A_WIDTH):
            dw_ref[j:j + 1, :] = dw[j]

    col = lambda off: pl.BlockSpec((t, LANES), lambda j: (0, j + off))
    wsp = pl.BlockSpec((CONV_A_WIDTH, LANES), lambda j: (0, j))
    return _pc(body, name="ev_a_conv_bwd", grid=(nb,), in_specs=[col(0), col(0), col(nb), wsp],
               out_specs=(col(0), col(0), wsp),
               out_shape=(S((t, D_A), f32), S((t, D_A), f32), S((CONV_A_WIDTH, D_A), f32)),
               scratch_shapes=[pltpu.VMEM((t + CONV_LEAD, LANES), f32), pltpu.VMEM((t + CONV_LEAD, LANES), f32)],
               compiler_params=_cparams(("arbitrary",)))(duc, p, p, conv_a)


def _ev_b(p, conv_b):
    t = p.shape[0]
    cr = _row_block8(t)
    nb = D_A // LANES

    def body(gb_ref, gc_ref, xi_ref, w_ref, o_ref, pad_ref, stage_ref):
        _fill_front_padded(pad_ref, gc_ref[...] * xi_ref[...], t)
        for r in range(t // cr):
            rows = slice(r * cr, (r + 1) * cr)
            stage_ref[rows, :] = gb_ref[rows, :] * _conv_rows(pad_ref, w_ref, CONV_B_WIDTH, r * cr, cr)
        o_ref[...] = stage_ref[...].astype(bf16)

    col = lambda off: pl.BlockSpec((t, LANES), lambda j: (0, j + off))
    return _pc(body, name="ev_b", grid=(nb,),
               in_specs=[col(2 * nb), col(3 * nb), col(4 * nb), pl.BlockSpec((CONV_B_WIDTH, LANES), lambda j: (0, j))],
               out_specs=col(0), out_shape=S((t, D_A), bf16),
               scratch_shapes=[pltpu.VMEM((t + CONV_LEAD, LANES), f32), pltpu.VMEM((t, LANES), f32)],
               compiler_params=_cparams(("arbitrary",)))(p, p, p, conv_b)


def _ev_b_bwd(dy, p, conv_b):
    t = p.shape[0]
    cr = _row_block8(t)
    nb = D_A // LANES

    def body(dy_ref, gb_ref, gc_ref, xi_ref, w_ref, dgb_ref, dgc_ref, dxi_ref, dw_ref, pad_ref, padb_ref):
        _fill_front_padded(pad_ref, gc_ref[...] * xi_ref[...], t)
        _fill_back_padded(padb_ref, dy_ref[...] * gb_ref[...], t)
        dw = None
        for r in range(t // cr):
            rows = slice(r * cr, (r + 1) * cr)
            dgb_ref[rows, :] = dy_ref[rows, :] * _conv_rows(pad_ref, w_ref, CONV_B_WIDTH, r * cr, cr)
            dcx = _conv_t_rows(padb_ref, w_ref, CONV_B_WIDTH, r * cr, cr)
            dgc_ref[rows, :] = dcx * xi_ref[rows, :]
            dxi_ref[rows, :] = dcx * gc_ref[rows, :]
            dw = _acc_list(dw, _conv_dw_rows(padb_ref[rows, :], pad_ref, CONV_B_WIDTH, r * cr, cr))
        for j in range(CONV_B_WIDTH):
            dw_ref[j:j + 1, :] = dw[j]

    col = lambda off: pl.BlockSpec((t, LANES), lambda j: (0, j + off))
    wsp = pl.BlockSpec((CONV_B_WIDTH, LANES), lambda j: (0, j))
    return _pc(body, name="ev_b_bwd", grid=(nb,), in_specs=[col(nb), col(2 * nb), col(3 * nb), col(4 * nb), wsp],
               out_specs=(col(0), col(0), col(0), wsp),
               out_shape=(S((t, D_A), f32), S((t, D_A), f32), S((t, D_A), f32), S((CONV_B_WIDTH, D_A), f32)),
               scratch_shapes=[pltpu.VMEM((t + CONV_LEAD, LANES), f32), pltpu.VMEM((t + CONV_LEAD, LANES), f32)],
               compiler_params=_cparams(("arbitrary",)))(dy, p, p, p, conv_b)


def _ffn_mid(u, conv_w, conv_b, name):
    t = u.shape[0]
    cr = _row_block8(t)
    nb = D_FF // LANES

    def body(gt_ref, vl_ref, w_ref, b_ref, o_ref, pad_ref, stage_ref):
        _fill_front_padded(pad_ref, gt_ref[...], t)
        for r in range(t // cr):
            rows = slice(r * cr, (r + 1) * cr)
            gc = _conv_rows(pad_ref, w_ref, FF_CONV_WIDTH, r * cr, cr) + b_ref[...]
            stage_ref[rows, :] = gc * _sigmoid(gc) * vl_ref[rows, :]
        o_ref[...] = stage_ref[...].astype(bf16)

    col = lambda off: pl.BlockSpec((t, LANES), lambda j: (0, j + off))
    return _pc(body, name=name, grid=(nb,),
               in_specs=[col(0), col(nb), pl.BlockSpec((FF_CONV_WIDTH, LANES), lambda j: (0, j)),
                         pl.BlockSpec((1, LANES), lambda j: (0, j))],
               out_specs=col(0), out_shape=S((t, D_FF), bf16),
               scratch_shapes=[pltpu.VMEM((t + CONV_LEAD, LANES), f32), pltpu.VMEM((t, LANES), f32)],
               compiler_params=_cparams(("arbitrary",)))(u, u, conv_w, conv_b.reshape(1, D_FF))


def _ffn_mid_bwd(dz, u, conv_w, conv_b, name):
    t = u.shape[0]
    cr = _row_block8(t)
    nb = D_FF // LANES

    def body(dz_ref, gt_ref, vl_ref, w_ref, b_ref, du_ref, dw_ref, db_ref, pad_ref, padb_ref):
        s = pl.program_id(1)
        _fill_front_padded(pad_ref, gt_ref[...], t)

        @pl.when(s == 0)
        def _():
            for r in range(t // cr):
                rows = slice(r * cr, (r + 1) * cr)
                gc = _conv_rows(pad_ref, w_ref, FF_CONV_WIDTH, r * cr, cr) + b_ref[...]
                sg = _sigmoid(gc)
                padb_ref[rows, :] = dz_ref[rows, :] * vl_ref[rows, :] * sg * (1.0 + gc * (1.0 - sg))
            padb_ref[t:t + CONV_LEAD, :] = jnp.zeros((CONV_LEAD, LANES), f32)
            dw, db = None, None
            for r in range(t // cr):
                rows = slice(r * cr, (r + 1) * cr)
                du_ref[rows, :] = _conv_t_rows(padb_ref, w_ref, FF_CONV_WIDTH, r * cr, cr)
                dgc = padb_ref[rows, :]
                dw = _acc_list(dw, _conv_dw_rows(dgc, pad_ref, FF_CONV_WIDTH, r * cr, cr))
                pb = jnp.sum(dgc, axis=0, keepdims=True)
                db = pb if db is None else db + pb
            for j in range(FF_CONV_WIDTH):
                dw_ref[j:j + 1, :] = dw[j]
            db_ref[...] = db

        @pl.when(s == 1)
        def _():
            for r in range(t // cr):
                rows = slice(r * cr, (r + 1) * cr)
                gc = _conv_rows(pad_ref, w_ref, FF_CONV_WIDTH, r * cr, cr) + b_ref[...]
                du_ref[rows, :] = dz_ref[rows, :] * gc * _sigmoid(gc)

    col = lambda off: pl.BlockSpec((t, LANES), lambda j, s: (0, j + off))
    wsp = pl.BlockSpec((FF_CONV_WIDTH, LANES), lambda j, s: (0, j))
    bsp = pl.BlockSpec((1, LANES), lambda j, s: (0, j))
    return _pc(body, name=name, grid=(nb, 2), in_specs=[col(0), col(0), col(nb), wsp, bsp],
               out_specs=(pl.BlockSpec((t, LANES), lambda j, s: (0, s * nb + j)), wsp, bsp),
               out_shape=(S((t, 2 * D_FF), f32), S((FF_CONV_WIDTH, D_FF), f32), S((1, D_FF), f32)),
               scratch_shapes=[pltpu.VMEM((t + CONV_LEAD, LANES), f32), pltpu.VMEM((t + CONV_LEAD, LANES), f32)],
               compiler_params=_cparams(("arbitrary", "arbitrary")))(dz, u, u, conv_w, conv_b.reshape(1, D_FF))


def _swap_halves(x):
    w = x.shape[1]
    lane = lax.broadcasted_iota(jnp.int32, x.shape, 1) % HEAD_DIM
    return jnp.where(lane < HEAD_DIM // 2, pltpu.roll(x, w - HEAD_DIM // 2, axis=1), pltpu.roll(x, HEAD_DIM // 2, axis=1))


def _rope_pack(patt, c64, s64):
    t = patt.shape[0]
    tp = t + ATT_PAD

    def body(p_ref, c_ref, s_ref, q_ref, k_ref, v_ref):
        c, s = c_ref[...], s_ref[...]

        def rope(x, nh):
            cc = jnp.concatenate([c] * nh, axis=1)
            ss = jnp.concatenate([s] * nh, axis=1)
            return x * cc + _swap_halves(x) * ss

        for ref, val in ((q_ref, rope(p_ref[:, 0:D_ATT], N_Q_HEADS)),
                         (k_ref, rope(p_ref[:, D_ATT:D_ATT + D_KV], N_KV_HEADS)),
                         (v_ref, p_ref[:, D_ATT + D_KV:ATT_COLS])):
            ref[0:ATT_PAD, :] = jnp.zeros((ATT_PAD, val.shape[1]), bf16)
            ref[ATT_PAD:tp, :] = val.astype(bf16)

    return _pc(body, name="rope_pack", in_specs=[_full((t, ATT_COLS)), _full((t, HEAD_DIM)), _full((t, HEAD_DIM))],
               out_specs=(_full((tp, D_ATT)), _full((tp, D_KV)), _full((tp, D_KV))), grid=(1,),
               out_shape=(S((tp, D_ATT), bf16), S((tp, D_KV), bf16), S((tp, D_KV), bf16)),
               compiler_params=_cparams(("arbitrary",)))(patt, c64, s64)


def _rope_bwd(dqp, dkp, dvp, c64, s64):
    tp = dqp.shape[0]
    t = tp - ATT_PAD

    def body(dq_ref, dk_ref, dv_ref, c_ref, s_ref, o_ref):
        c, s = c_ref[...], s_ref[...]

        def unrope(dy, nh):
            cc = jnp.concatenate([c] * nh, axis=1)
            ss = jnp.concatenate([s] * nh, axis=1)
            return dy * cc + _swap_halves(dy * ss)

        o_ref[:, 0:D_ATT] = unrope(dq_ref[ATT_PAD:tp, :], N_Q_HEADS)
        o_ref[:, D_ATT:D_ATT + D_KV] = unrope(dk_ref[ATT_PAD:tp, :], N_KV_HEADS)
        o_ref[:, D_ATT + D_KV:ATT_COLS] = dv_ref[ATT_PAD:tp, :]

    return _pc(body, name="rope_bwd", grid=(1,),
               in_specs=[_full((tp, D_ATT)), _full((tp, D_KV)), _full((tp, D_KV)), _full((t, HEAD_DIM)), _full((t, HEAD_DIM))],
               out_specs=_full((t, ATT_COLS)), out_shape=S((t, ATT_COLS), f32),
               compiler_params=_cparams(("arbitrary",)))(dqp, dkp, dvp, c64, s64)


def _attn_masks(n):
    rows = GQA_GROUP * BLOCK
    ri = lax.broadcasted_iota(jnp.int32, (rows, BLOCK), 0) % BLOCK
    ci = lax.broadcasted_iota(jnp.int32, (rows, BLOCK), 1)
    m_cur = (ci <= ri) & (ci >= jnp.where(n >= 1, 0, ATT_PAD))
    m_prev = ci > ri + jnp.where(n >= 2, 0, BLOCK)
    m_meta = ci >= jnp.where(n >= 1, ATT_PAD, BLOCK)
    return m_cur, m_prev, m_meta


def _attn_probs(qg, kc, kp, km, masks, skv):
    def scores(k, m):
        s = lax.dot_general(qg, k, _DIMS["nt"], preferred_element_type=f32) * ATT_SCALE
        return jnp.where(m, s, NEG_INF)
    s_c, s_p, s_m = scores(kc, masks[0]), scores(kp, masks[1]), scores(km, masks[2])
    mx = jnp.maximum(jnp.maximum(jnp.max(s_c, axis=-1, keepdims=True), jnp.max(s_p, axis=-1, keepdims=True)),
                     jnp.maximum(jnp.max(s_m, axis=-1, keepdims=True), skv))
    e_c, e_p, e_m, e_s = jnp.exp(s_c - mx), jnp.exp(s_p - mx), jnp.exp(s_m - mx), jnp.exp(skv - mx)
    den = (jnp.sum(e_c, axis=-1, keepdims=True) + jnp.sum(e_p, axis=-1, keepdims=True)
           + jnp.sum(e_m, axis=-1, keepdims=True) + e_s)
    inv = 1.0 / den
    return e_c * inv, e_p * inv, e_m * inv, e_s * inv


def _sink_rows(sk_ref, g):
    hrow = lax.broadcasted_iota(jnp.int32, (GQA_GROUP * BLOCK, 1), 0) // BLOCK
    skv = jnp.zeros((GQA_GROUP * BLOCK, 1), f32)
    for hh in range(GQA_GROUP):
        skv = jnp.where(hrow == hh, sk_ref[0, GQA_GROUP * g + hh], skv)
    return skv, hrow


def _stack_heads(ref, g):
    return jnp.concatenate([ref[:, (GQA_GROUP * g + hh) * HEAD_DIM:(GQA_GROUP * g + hh + 1) * HEAD_DIM]
                            for hh in range(GQA_GROUP)], axis=0)


def _attn_specs():
    blk = lambda w: pl.BlockSpec((BLOCK, w), lambda n: (n, 0))
    prev = pl.BlockSpec((BLOCK, D_KV), lambda n: (jnp.maximum(n - 1, 0), 0))
    meta = pl.BlockSpec((BLOCK, D_KV), lambda n: (0, 0))
    return blk, prev, meta


def _attn_fwd(qp, kp, vp, sinks):
    tp = qp.shape[0]
    blk, prev, meta = _attn_specs()

    def body(sk_ref, q_ref, kc_ref, kp_ref, km_ref, vc_ref, vp_ref, vm_ref, o_ref):
        masks = _attn_masks(pl.program_id(0))
        for g in range(N_KV_HEADS):
            sl = slice(g * HEAD_DIM, (g + 1) * HEAD_DIM)
            skv, _ = _sink_rows(sk_ref, g)
            p_c, p_p, p_m, _ = _attn_probs(_stack_heads(q_ref, g), kc_ref[:, sl], kp_ref[:, sl], km_ref[:, sl], masks, skv)
            o = (jnp.dot(p_c.astype(bf16), vc_ref[:, sl], preferred_element_type=f32)
                 + jnp.dot(p_p.astype(bf16), vp_ref[:, sl], preferred_element_type=f32)
                 + jnp.dot(p_m.astype(bf16), vm_ref[:, sl], preferred_element_type=f32))
            for hh in range(GQA_GROUP):
                h = GQA_GROUP * g + hh
                o_ref[:, h * HEAD_DIM:(h + 1) * HEAD_DIM] = o[hh * BLOCK:(hh + 1) * BLOCK].astype(bf16)

    return _pc(body, name="attn_fwd", grid=(tp // BLOCK,),
               in_specs=[pl.BlockSpec(memory_space=pltpu.SMEM), blk(D_ATT), blk(D_KV), prev, meta, blk(D_KV), prev, meta],
               out_specs=blk(D_ATT), out_shape=S((tp, D_ATT), bf16),
               compiler_params=_cparams(("arbitrary",)))(sinks, qp, kp, kp, kp, vp, vp, vp)


def _attn_bwd(qp, kp, vp, sinks, dop):
    tp = qp.shape[0]
    blk, prev, meta = _attn_specs()

    def body(sk_ref, q_ref, kc_ref, kp_ref, km_ref, vc_ref, vp_ref, vm_ref, do_ref, dq_ref, dk_ref, dv_ref, dsk_ref):
        n = pl.program_id(0)

        @pl.when(n == 0)
        def _():
            dk_ref[...] = jnp.zeros_like(dk_ref)
            dv_ref[...] = jnp.zeros_like(dv_ref)
            dsk_ref[...] = jnp.zeros_like(dsk_ref)
        masks = _attn_masks(n)
        cur = pl.ds(pl.multiple_of(n * BLOCK, BLOCK), BLOCK)
        prv = pl.ds(pl.multiple_of(jnp.maximum(n - 1, 0) * BLOCK, BLOCK), BLOCK)
        lane = lax.broadcasted_iota(jnp.int32, (1, LANES), 1)
        dsk = jnp.zeros((1, LANES), f32)
        for g in range(N_KV_HEADS):
            sl = slice(g * HEAD_DIM, (g + 1) * HEAD_DIM)
            skv, hrow = _sink_rows(sk_ref, g)
            qg = _stack_heads(q_ref, g)
            dog = _stack_heads(do_ref, g)
            ks = (kc_ref[:, sl], kp_ref[:, sl], km_ref[:, sl])
            vs = (vc_ref[:, sl], vp_ref[:, sl], vm_ref[:, sl])
            probs = _attn_probs(qg, ks[0], ks[1], ks[2], masks, skv)
            dps = [lax.dot_general(dog, v, _DIMS["nt"], preferred_element_type=f32) for v in vs]
            delta = sum(jnp.sum(p * dp, axis=-1, keepdims=True) for p, dp in zip(probs[:3], dps))
            dss = [(p * (dp - delta) * ATT_SCALE).astype(bf16) for p, dp in zip(probs[:3], dps)]
            dq = sum(jnp.dot(ds, k, preferred_element_type=f32) for ds, k in zip(dss, ks))
            for hh in range(GQA_GROUP):
                h = GQA_GROUP * g + hh
                dq_ref[:, h * HEAD_DIM:(h + 1) * HEAD_DIM] = dq[hh * BLOCK:(hh + 1) * BLOCK]
                dsk = dsk + jnp.where(lane == h, -jnp.sum(jnp.where(hrow == hh, probs[3] * delta, 0.0)), 0.0)
            for rows, p, ds in zip((cur, prv, slice(0, BLOCK)), probs[:3], dss):
                dv_ref[rows, sl] += lax.dot_general(p.astype(bf16), dog, _DIMS["tn"], preferred_element_type=f32)
                dk_ref[rows, sl] += lax.dot_general(ds, qg, _DIMS["tn"], preferred_element_type=f32)
        dsk_ref[...] += dsk

    return _pc(body, name="attn_bwd", grid=(tp // BLOCK,),
               in_specs=[pl.BlockSpec(memory_space=pltpu.SMEM), blk(D_ATT), blk(D_KV), prev, meta, blk(D_KV), prev, meta,
                         blk(D_ATT)],
               out_specs=(blk(D_ATT), _full((tp, D_KV)), _full((tp, D_KV)), _full((1, LANES))),
               out_shape=(S((tp, D_ATT), f32), S((tp, D_KV), f32), S((tp, D_KV), f32), S((1, LANES), f32)),
               compiler_params=_cparams(("arbitrary",)))(sinks, qp, kp, kp, kp, vp, vp, vp, dop)


def _seg(x, bm):
    return jnp.dot(x, bm, precision=HIGHEST, preferred_element_type=f32)


def _softplus(y):
    return jnp.maximum(y, 0.0) + jnp.log(1.0 + jnp.exp(-jnp.abs(y)))


def _prep_fn(xr, xk, xwd, xad, xgd, w0, w2, a0, a2, g2, k_k, k_a, bm):
    xw = w0 + jnp.dot(jnp.tanh(xwd), w2, preferred_element_type=f32)
    decay = jnp.exp(-jnp.exp(-_softplus(-xw) - 0.5))
    alpha = _sigmoid(a0 + jnp.dot(xad, a2, preferred_element_type=f32))
    g = jnp.dot(_sigmoid(xgd), g2, preferred_element_type=f32)
    kk = xk * k_k
    kkn = kk / jnp.maximum(jnp.sqrt(_seg(kk * kk, bm)), 1e-12)
    k2 = xk * (1.0 + (alpha - 1.0) * k_a)
    return decay, k2, -kkn, kkn * alpha, g


def _split_cols(x):
    o1, o2, o3 = 3 * D_R, 3 * D_R + LORA_W, 3 * D_R + LORA_W + LORA_A
    return x[:, 0:D_R], x[:, D_R:2 * D_R], x[:, 2 * D_R:o1], x[:, o1:o2], x[:, o2:o3], x[:, o3:RWKV_COLS]


def _shifted(sh_ref, x, halo, first, rb):
    sh_ref[0:SUBLANES, :] = jnp.where(first, 0.0, halo)
    sh_ref[SUBLANES:SUBLANES + rb, :] = x
    return sh_ref[SUBLANES - 1:SUBLANES - 1 + rb, :]


_PREP_PARAMS = ("od_w0", "od_w2", "od_a0", "od_a2", "od_g2", "od_k_k", "od_k_a")


def _rwkv_prep(pr, mu, params, bm):
    t = pr.shape[0]
    rb = _row_block8(t)
    hb = rb // SUBLANES

    def body(pr_ref, halo_ref, mu_ref, w0, w2, a0, a2, g2, kk_ref, ka_ref, bm_ref, *outs_sh):
        outs, sh_ref = outs_sh[:-1], outs_sh[-1]
        x = pr_ref[...]
        prev = _shifted(sh_ref, x, halo_ref[...], pl.program_id(0) == 0, rb)
        xr, xk, xv, xwd, xad, xgd = _split_cols(x + (prev - x) * mu_ref[...])
        bmv = bm_ref[...]
        decay, k2, a_s, b_s, g = _prep_fn(xr, xk, xwd, xad, xgd, w0[...], w2[...], a0[...], a2[...], g2[...],
                                          kk_ref[...], ka_ref[...], bmv)
        vals = (xr, xv, decay, k2, a_s, b_s, decay * xr, _seg(b_s * xr, bmv), _seg(k2 * xr, bmv), g)
        for ref, val in zip(outs, vals):
            ref[...] = val

    row = pl.BlockSpec((rb, RWKV_COLS), lambda i: (i, 0))
    halo = pl.BlockSpec((SUBLANES, RWKV_COLS), lambda i: (jnp.maximum(i * hb - 1, 0), 0))
    orow = pl.BlockSpec((rb, D_R), lambda i: (i, 0))
    return _pc(body, name="rwkv_prep", grid=(t // rb,),
               in_specs=[row, halo, _full((1, RWKV_COLS))] + [_full(p.shape) for p in params] + [_full(bm.shape)],
               out_specs=(orow,) * 10, out_shape=(S((t, D_R), f32),) * 10,
               scratch_shapes=[pltpu.VMEM((rb + SUBLANES, RWKV_COLS), f32)],
               compiler_params=_cparams(("arbitrary",)))(pr, pr, mu, *params, bm)


def _rwkv_prep_bwd(pr, mu, params, bm, cts):
    t = pr.shape[0]
    rb = _row_block8(t)
    hb = rb // SUBLANES
    counts = [len(c) for c in cts]
    flat = [a for c in cts for a in c]

    def body(pr_ref, halo_ref, mu_ref, w0, w2, a0, a2, g2, kk_ref, ka_ref, bm_ref, *rest):
        ct_refs, rest = rest[:len(flat)], rest[len(flat):]
        dx_ref, dmu_ref = rest[0], rest[1]
        dpar_refs, sh_ref = rest[2:9], rest[9]

        @pl.when(pl.program_id(0) == 0)
        def _():
            dmu_ref[...] = jnp.zeros_like(dmu_ref)
            for r in dpar_refs:
                r[...] = jnp.zeros_like(r)
        sums, pos = [], 0
        for c in counts:
            sums.append(sum(r[...] for r in ct_refs[pos:pos + c]))
            pos += c
        x = pr_ref[...]
        prev = _shifted(sh_ref, x, halo_ref[...], pl.program_id(0) == 0, rb)
        xr, xk, xv, xwd, xad, xgd = _split_cols(x + (prev - x) * mu_ref[...])
        bmv = bm_ref[...]
        _, vjp = jax.vjp(lambda *a: _prep_fn(*a, bmv), xr, xk, xwd, xad, xgd, w0[...], w2[...], a0[...], a2[...],
                         g2[...], kk_ref[...], ka_ref[...])
        grads = vjp(tuple(sums[:5]))
        dxr, dxk, dxwd, dxad, dxgd = grads[:5]
        o1, o2, o3 = 3 * D_R, 3 * D_R + LORA_W, 3 * D_R + LORA_W + LORA_A
        dx_ref[:, 0:D_R] = dxr + sums[5]
        dx_ref[:, D_R:2 * D_R] = dxk
        dx_ref[:, 2 * D_R:o1] = sums[6]
        dx_ref[:, o1:o2] = dxwd
        dx_ref[:, o2:o3] = dxad
        dx_ref[:, o3:RWKV_COLS] = dxgd
        dmu_ref[...] += jnp.sum(dx_ref[...] * (prev - x), axis=0, keepdims=True)
        for r, gval in zip(dpar_refs, grads[5:]):
            r[...] += gval

    row = pl.BlockSpec((rb, RWKV_COLS), lambda i: (i, 0))
    halo = pl.BlockSpec((SUBLANES, RWKV_COLS), lambda i: (jnp.maximum(i * hb - 1, 0), 0))
    crow = pl.BlockSpec((rb, D_R), lambda i: (i, 0))
    return _pc(body, name="rwkv_prep_bwd", grid=(t // rb,),
               in_specs=[row, halo, _full((1, RWKV_COLS))] + [_full(p.shape) for p in params] + [_full(bm.shape)]
               + [crow] * len(flat),
               out_specs=(row, _full((1, RWKV_COLS))) + tuple(_full(p.shape) for p in params),
               out_shape=(S((t, RWKV_COLS), f32), S((1, RWKV_COLS), f32)) + tuple(S(p.shape, f32) for p in params),
               scratch_shapes=[pltpu.VMEM((rb + SUBLANES, RWKV_COLS), f32)],
               compiler_params=_cparams(("arbitrary",)))(pr, pr, mu, *params, bm, *flat)


def _shift_bwd(dxs, mu):
    t = dxs.shape[0]
    rb = _row_block8(t)
    hb = rb // SUBLANES
    nblk = t // rb

    def body(dx_ref, halo_ref, mu_ref, o_ref, sh_ref):
        dx = dx_ref[...]
        sh_ref[0:rb, :] = dx
        sh_ref[rb:rb + SUBLANES, :] = jnp.where(pl.program_id(0) == nblk - 1, 0.0, halo_ref[...])
        m = mu_ref[...]
        o_ref[...] = dx * (1.0 - m) + sh_ref[1:1 + rb, :] * m

    row = pl.BlockSpec((rb, RWKV_COLS), lambda i: (i, 0))
    halo = pl.BlockSpec((SUBLANES, RWKV_COLS), lambda i: (jnp.minimum((i + 1) * hb, t // SUBLANES - 1), 0))
    return _pc(body, name="rwkv_shift_bwd", grid=(nblk,), in_specs=[row, halo, _full((1, RWKV_COLS))],
               out_specs=row, out_shape=S((t, RWKV_COLS), f32),
               scratch_shapes=[pltpu.VMEM((rb + SUBLANES, RWKV_COLS), f32)],
               compiler_params=_cparams(("arbitrary",)))(dxs, dxs, mu)


def _post_fn(y, xr, k2, xv, g, lg, lb, rk, bm):
    inv_n = 1.0 / HEAD_DIM
    yc = y - _seg(y, bm) * inv_n
    var = _seg(yc * yc, bm) * inv_n
    yn = yc * lax.rsqrt(var + RWKV_GN_EPS) * lg + lb
    return (yn + _seg(xr * k2 * rk, bm) * xv) * g


def _rwkv_post(y, xr, k2, xv, g, lg, lb, rk, bm):
    t = y.shape[0]
    rb = _row_block8(t)

    def body(y_ref, xr_ref, k2_ref, xv_ref, g_ref, lg_ref, lb_ref, rk_ref, bm_ref, o_ref):
        o_ref[...] = _post_fn(y_ref[...], xr_ref[...], k2_ref[...], xv_ref[...], g_ref[...], lg_ref[...], lb_ref[...],
                              rk_ref[...], bm_ref[...])

    row = pl.BlockSpec((rb, D_R), lambda i: (i, 0))
    vec = _full((1, D_R))
    return _pc(body, name="rwkv_post", grid=(t // rb,), in_specs=[row] * 5 + [vec] * 3 + [_full(bm.shape)],
               out_specs=row, out_shape=S((t, D_R), f32),
               compiler_params=_cparams(("arbitrary",)))(y, xr, k2, xv, g, lg, lb, rk, bm)


def _rwkv_post_bwd(dy1, y, xr, k2, xv, g, lg, lb, rk, bm):
    t = y.shape[0]
    rb = _row_block8(t)

    def body(dy_ref, y_ref, xr_ref, k2_ref, xv_ref, g_ref, lg_ref, lb_ref, rk_ref, bm_ref, *outs):
        @pl.when(pl.program_id(0) == 0)
        def _():
            for r in outs[5:]:
                r[...] = jnp.zeros_like(r)
        bmv = bm_ref[...]
        _, vjp = jax.vjp(lambda *a: _post_fn(*a, bmv), y_ref[...], xr_ref[...], k2_ref[...], xv_ref[...], g_ref[...],
                         lg_ref[...], lb_ref[...], rk_ref[...])
        grads = vjp(dy_ref[...])
        for r, gval in zip(outs[:5], grads[:5]):
            r[...] = gval
        for r, gval in zip(outs[5:], grads[5:]):
            r[...] += gval

    row = pl.BlockSpec((rb, D_R), lambda i: (i, 0))
    vec = _full((1, D_R))
    return _pc(body, name="rwkv_post_bwd", grid=(t // rb,),
               in_specs=[pl.BlockSpec((rb, D_R), lambda i: (i, 1))] + [row] * 5 + [vec] * 3 + [_full(bm.shape)],
               out_specs=(row,) * 5 + (vec,) * 3, out_shape=(S((t, D_R), f32),) * 5 + (S((1, D_R), f32),) * 3,
               compiler_params=_cparams(("arbitrary",)))(dy1, y, xr, k2, xv, g, lg, lb, rk, bm)


def _seg2(x, bb):
    hi = x.astype(bf16)
    lo = (x - hi.astype(f32)).astype(bf16)
    return jnp.dot(jnp.concatenate([hi, lo], axis=1), bb, preferred_element_type=f32)


def _row4(rows, j):
    return jnp.concatenate([jnp.broadcast_to(rows[j:j + 1, p * LANES:(p + 1) * LANES], (HEAD_DIM, LANES))
                            for p in range(4)], axis=0)


def _scan_consts():
    lane_group = jnp.arange(LANES) // HEAD_DIM
    b128 = (lane_group[:, None] == lane_group[None, :]).astype(bf16)
    bb = jnp.concatenate([b128, b128], axis=0)
    qsel = (jnp.arange(PAIR_ROWS)[:, None] % HEAD_DIM == jnp.arange(LANES)[None, :] % HEAD_DIM).astype(f32)
    return bb, qsel


def _store_cols(acc_ref, o_ref, tc):
    for p in range(4):
        blk = acc_ref[p * HEAD_DIM:(p + 1) * HEAD_DIM, :].T
        o_ref[:, (2 * p) * HEAD_DIM:(2 * p + 1) * HEAD_DIM] = blk[0:tc]
        o_ref[:, (2 * p + 1) * HEAD_DIM:(2 * p + 2) * HEAD_DIM] = blk[HEAD_DIM:HEAD_DIM + tc]


def _wkv_fwd(w, k, v, a, b, wr, br, kr):
    t = w.shape[0]
    tc = SCAN_CHUNK
    bb, qsel = _scan_consts()

    def body(w_ref, k_ref, v_ref, a_ref, b_ref, wr_ref, br_ref, kr_ref, bb_ref, q_ref, y_ref, st_ref, sa_ref, vb_ref,
             s_scr, yacc):
        @pl.when(pl.program_id(0) == 0)
        def _():
            s_scr[...] = jnp.zeros_like(s_scr)
        bbv, qv = bb_ref[...], q_ref[...]
        lane64 = lax.broadcasted_iota(jnp.int32, (PAIR_ROWS, LANES), 1) % HEAD_DIM

        def group(gi, s):
            base = pl.multiple_of(gi * SUBLANES, SUBLANES)
            w8, k8, v8, a8, b8, wr8, br8, kr8 = (ref[pl.ds(base, SUBLANES), :] for ref in
                                                 (w_ref, k_ref, v_ref, a_ref, b_ref, wr_ref, br_ref, kr_ref))
            for j in range(SUBLANES):
                tt = base + j
                st_ref[tt] = s
                x = jnp.concatenate([s * _row4(a8, j), s * _row4(wr8, j), qv * _row4(v8, j)], axis=0)
                r = _seg2(x, bbv)
                sa, z, vb = r[0:PAIR_ROWS], r[PAIR_ROWS:2 * PAIR_ROWS], r[2 * PAIR_ROWS:3 * PAIR_ROWS]
                sa_ref[tt] = sa
                vb_ref[tt] = vb
                ynew = z + sa * _row4(br8, j) + vb * _row4(kr8, j)
                yacc[...] = jnp.where(lane64 == tt, ynew, yacc[...])
                s = s * _row4(w8, j) + sa * _row4(b8, j) + vb * _row4(k8, j)
            return s

        s_scr[...] = lax.fori_loop(0, tc // SUBLANES, group, s_scr[...])
        _store_cols(yacc, y_ref, tc)

    row = pl.BlockSpec((tc, D_R), lambda c: (c, 0))
    tiles = pl.BlockSpec((tc, PAIR_ROWS, LANES), lambda c: (c, 0, 0))
    return _pc(body, name="wkv_fwd", grid=(t // tc,), in_specs=[row] * 8 + [_full(bb.shape), _full(qsel.shape)],
               out_specs=(row, tiles, tiles, tiles),
               out_shape=(S((t, D_R), f32),) + (S((t, PAIR_ROWS, LANES), f32),) * 3,
               scratch_shapes=[pltpu.VMEM((PAIR_ROWS, LANES), f32), pltpu.VMEM((PAIR_ROWS, LANES), f32)],
               compiler_params=_cparams(("arbitrary",)))(w, k, v, a, b, wr, br, kr, bb, qsel)


def _wkv_bwd(sprev, sab, vbb, w, k, a, b, r, dy):
    t = w.shape[0]
    tc = SCAN_CHUNK
    nc = t // tc
    bb, qsel = _scan_consts()

    def body(st_ref, sa_ref, vb_ref, w_ref, k_ref, a_ref, b_ref, r_ref, dy_ref, bb_ref, q_ref,
             dr_ref, dw_ref, dk_ref, dv_ref, da_ref, db_ref, g_scr, dvacc, rows_scr):
        @pl.when(pl.program_id(0) == 0)
        def _():
            g_scr[...] = jnp.zeros_like(g_scr)
        bbv, qv = bb_ref[...], q_ref[...]
        lane64 = lax.broadcasted_iota(jnp.int32, (PAIR_ROWS, LANES), 1) % HEAD_DIM
        outs = (dr_ref, dw_ref, db_ref, dk_ref, da_ref)

        def colsums(slot, j, x):
            for p in range(4):
                rows_scr[slot, j:j + 1, p * LANES:(p + 1) * LANES] = jnp.sum(x[p * HEAD_DIM:(p + 1) * HEAD_DIM], axis=0,
                                                                           keepdims=True)

        def group(i, g):
            base = pl.multiple_of((tc // SUBLANES - 1 - i) * SUBLANES, SUBLANES)
            w8, k8, a8, b8, r8, dy8 = (ref[pl.ds(base, SUBLANES), :] for ref in (w_ref, k_ref, a_ref, b_ref, r_ref, dy_ref))
            for j in reversed(range(SUBLANES)):
                tt = base + j
                sp, u, vb = st_ref[tt], sa_ref[tt], vb_ref[tt]
                a4, b4, w4, k4 = _row4(a8, j), _row4(b8, j), _row4(w8, j), _row4(k8, j)
                dyb = _seg2(qv * _row4(dy8, j), bbv)
                s_t = sp * w4 + u * b4 + vb * k4
                g = g + dyb * _row4(r8, j)
                rr2 = _seg2(jnp.concatenate([g * b4, g * k4], axis=0), bbv)
                du, dvb = rr2[0:PAIR_ROWS], rr2[PAIR_ROWS:2 * PAIR_ROWS]
                for slot, val in enumerate((s_t * dyb, g * sp, g * u, g * vb, sp * du)):
                    colsums(slot, j, val)
                dvacc[...] = jnp.where(lane64 == tt, dvb, dvacc[...])
                g = g * w4 + du * a4
            for slot, ref in enumerate(outs):
                ref[pl.ds(base, SUBLANES), :] = rows_scr[slot]
            return g

        g_scr[...] = lax.fori_loop(0, tc // SUBLANES, group, g_scr[...])
        _store_cols(dvacc, dv_ref, tc)

    row = pl.BlockSpec((tc, D_R), lambda c: (nc - 1 - c, 0))
    tiles = pl.BlockSpec((tc, PAIR_ROWS, LANES), lambda c: (nc - 1 - c, 0, 0))
    return _pc(body, name="wkv_bwd", grid=(nc,),
               in_specs=[tiles] * 3 + [row] * 6 + [_full(bb.shape), _full(qsel.shape)],
               out_specs=(row,) * 6, out_shape=(S((t, D_R), f32),) * 6,
               scratch_shapes=[pltpu.VMEM((PAIR_ROWS, LANES), f32), pltpu.VMEM((PAIR_ROWS, LANES), f32),
                               pltpu.VMEM((5, SUBLANES, D_R), f32)],
               compiler_params=_cparams(("arbitrary",)))(sprev, sab, vbb, w, k, a, b, r, dy, bb, qsel)


def _rope_tables(t):
    half = HEAD_DIM // 2
    inv = ROPE_THETA ** (-jnp.arange(half, dtype=f32) / half)
    ang = jnp.arange(t, dtype=f32)[:, None] * inv[None, :]
    cos, sin = jnp.cos(ang), jnp.sin(ang)
    return jnp.concatenate([cos, cos], axis=1), jnp.concatenate([-sin, sin], axis=1)


def _head_matrix():
    grp = jnp.arange(D_R) // HEAD_DIM
    return (grp[:, None] == grp[None, :]).astype(f32)


def _ffn_fwd(h, g, w_up_t, conv_w, conv_b, w_down, i):
    hf = _rms_fwd(h, g, f"ffn{i}_norm")
    u = _mm(hf, w_up_t, "nt", f"ffn{i}_up")
    z = _ffn_mid(u, conv_w, conv_b, f"ffn{i}_mid")
    return _mm(z, w_down, "nn", f"ffn{i}_down", res=h), (hf, u, z)


def _ffn_bwd(dh, h, saved, g, w_up_t, conv_w, conv_b, w_down, i):
    hf, u, z = saved
    dz = _mm(dh, w_down, "nt", f"ffn{i}_dz")
    g_down = _mm(z, dh, "tn", f"ffn{i}_gdown", out_dtype=GRAD_WIRE_DTYPE)
    du, g_conv, g_convb = _ffn_mid_bwd(dz, u, conv_w, conv_b, f"ffn{i}_mid_bwd")
    g_up_t = _mm(du, hf, "tn", f"ffn{i}_gup", out_dtype=GRAD_WIRE_DTYPE)
    dhf = _mm(du, w_up_t, "nn", f"ffn{i}_dhf")
    dh_in, g_norm = _rms_bwd(dhf, h, g, dh, f"ffn{i}_norm_bwd")
    return dh_in, dict(up_t=g_up_t, down=g_down, conv=g_conv, conv_b=g_convb, norm=g_norm)


def _local_step(x, target, W):
    t = N_META + x.shape[0]
    c64, s64 = _rope_tables(t)
    bm = _head_matrix()
    h0 = jnp.concatenate([W["meta_tokens"], x], axis=0)

    hn0 = _rms_fwd(h0, W["norm_mix"][0], "mix0_norm")
    p0 = _mm(hn0, W["ev_w_in_t"], "nt", "ev_in")
    uc = _ev_a_conv(p0, W["ev_conv_a"])
    y0 = jnp.concatenate([_ev_a_norm(uc, W["ev_ln_a_g"], W["ev_ln_a_b"]), _ev_b(p0, W["ev_conv_b"])], axis=1)
    h1 = _mm(y0, W["ev_w_out"], "nn", "ev_out", res=h0)
    h2, ffn0 = _ffn_fwd(h1, W["norm_ffn"][0], W["ff_w_up_t"][0], W["ff_conv"][0], W["ff_conv_b"][0], W["ff_w_down"][0], 0)

    hn1 = _rms_fwd(h2, W["norm_mix"][1], "mix1_norm")
    p1 = _mm(hn1, W["od_w_in_t"], "nt", "od_in")
    pr = p1[:, ATT_COLS:]
    qp, kp, vp = _rope_pack(p1[:, :ATT_COLS], c64, s64)
    op = _attn_fwd(qp, kp, vp, W["od_sinks"])
    prep_params = [W[n] for n in _PREP_PARAMS]
    xr, xv, decay, k2, a_s, b_s, wr, br, kr, gate = _rwkv_prep(pr, W["od_mu"], prep_params, bm)
    ysc, sprev, sab, vbb = _wkv_fwd(decay, k2, xv, a_s, b_s, wr, br, kr)
    rk = W["od_r_k"].reshape(1, D_R)
    yr = _rwkv_post(ysc, xr, k2, xv, gate, W["od_lnx_g"], W["od_lnx_b"], rk, bm)
    y1 = jnp.concatenate([op[ATT_PAD:], yr.astype(bf16)], axis=1)
    h3 = _mm(y1, W["od_w_out"], "nn", "od_out", res=h2)
    h4, ffn1 = _ffn_fwd(h3, W["norm_ffn"][1], W["ff_w_up_t"][1], W["ff_conv"][1], W["ff_conv_b"][1], W["ff_w_down"][1], 1)

    tgt = jnp.concatenate([jnp.zeros((N_META, D_MODEL), f32), target], axis=0)
    loss, dh4, g_norm_final = _final_loss(h4, W["norm_final"], tgt)

    dh3, gf1 = _ffn_bwd(dh4, h3, ffn1, W["norm_ffn"][1], W["ff_w_up_t"][1], W["ff_conv"][1], W["ff_conv_b"][1],
                        W["ff_w_down"][1], 1)
    dy1 = _mm(dh3, W["od_w_out"], "nt", "od_dy")
    g_od_w_out = _mm(y1, dh3, "tn", "od_gout", out_dtype=GRAD_WIRE_DTYPE)
    dysc, dxr_p, dk2_p, dxv_p, dgate, g_lnx_g, g_lnx_b, g_rk = _rwkv_post_bwd(
        dy1, ysc, xr, k2, xv, gate, W["od_lnx_g"], W["od_lnx_b"], rk, bm)
    dr, dw, dk, dv, da, db = _wkv_bwd(sprev, sab, vbb, decay, k2, a_s, b_s, xr, dysc)
    prep_grads = _rwkv_prep_bwd(pr, W["od_mu"], prep_params, bm,
                                [[dw], [dk, dk2_p], [da], [db], [dgate], [dr, dxr_p], [dv, dxv_p]])
    dxs, g_mu = prep_grads[0], prep_grads[1]
    dpr = _shift_bwd(dxs, W["od_mu"])
    dop = jnp.concatenate([jnp.zeros((ATT_PAD, D_ATT), f32), dy1[:, :D_ATT]], axis=0).astype(bf16)
    dqp, dkp, dvp, dsk = _attn_bwd(qp, kp, vp, W["od_sinks"], dop)
    dp1 = jnp.concatenate([_rope_bwd(dqp, dkp, dvp, c64, s64), dpr], axis=1)
    g_od_w_in_t = _mm(dp1, hn1, "tn", "od_gin", out_dtype=GRAD_WIRE_DTYPE)
    dhn1 = _mm(dp1, W["od_w_in_t"], "nn", "od_dhn")
    dh2, g_norm_mix1 = _rms_bwd(dhn1, h2, W["norm_mix"][1], dh3, "mix1_norm_bwd")

    dh1, gf0 = _ffn_bwd(dh2, h1, ffn0, W["norm_ffn"][0], W["ff_w_up_t"][0], W["ff_conv"][0], W["ff_conv_b"][0],
                        W["ff_w_down"][0], 0)
    dy0 = _mm(dh1, W["ev_w_out"], "nt", "ev_dy")
    g_ev_w_out = _mm(y0, dh1, "tn", "ev_gout", out_dtype=GRAD_WIRE_DTYPE)
    duc, g_ln_g, g_ln_b = _ev_a_norm_bwd(dy0, uc, W["ev_ln_a_g"], W["ev_ln_a_b"])
    dav, dag, g_conv_a = _ev_a_conv_bwd(duc, p0, W["ev_conv_a"])
    dgb, dgc, dxi, g_conv_b = _ev_b_bwd(dy0, p0, W["ev_conv_b"])
    dp0 = jnp.concatenate([dav, dag, dgb, dgc, dxi], axis=1)
    g_ev_w_in_t = _mm(dp0, hn0, "tn", "ev_gin", out_dtype=GRAD_WIRE_DTYPE)
    dhn0 = _mm(dp0, W["ev_w_in_t"], "nn", "ev_dhn")
    dh0, g_norm_mix0 = _rms_bwd(dhn0, h0, W["norm_mix"][0], dh1, "mix0_norm_bwd")

    G = dict(
        meta_tokens=dh0[:N_META], norm_mix=jnp.concatenate([g_norm_mix0, g_norm_mix1], axis=0),
        norm_ffn=jnp.concatenate([gf0["norm"], gf1["norm"]], axis=0), norm_final=g_norm_final.reshape(D_MODEL),
        ev_w_in_t=g_ev_w_in_t, ev_conv_a=g_conv_a, ev_ln_a_g=g_ln_g, ev_ln_a_b=g_ln_b, ev_conv_b=g_conv_b,
        ev_w_out=g_ev_w_out, od_w_in_t=g_od_w_in_t, od_sinks=dsk[:, :N_Q_HEADS], od_mu=g_mu,
        od_lnx_g=g_lnx_g, od_lnx_b=g_lnx_b, od_r_k=g_rk.reshape(N_Q_HEADS, HEAD_DIM), od_w_out=g_od_w_out,
        ff_w_up_t=[gf0["up_t"], gf1["up_t"]], ff_w_down=[gf0["down"], gf1["down"]],
        ff_conv=jnp.stack([gf0["conv"], gf1["conv"]]), ff_conv_b=jnp.concatenate([gf0["conv_b"], gf1["conv_b"]], axis=0),
    )
    for name, gval in zip(_PREP_PARAMS, prep_grads[2:]):
        G[name] = gval
    return loss, dh0[N_META:], G


HBM = pl.BlockSpec(memory_space=pl.ANY)


def _mesh_pos():
    return lax.axis_index("x"), lax.axis_index("y"), lax.axis_index("c")


def _dev(px, py, pc):
    return 4 * px + 2 * py + pc


def _all_gather(xs, name):
    n = len(xs)

    def body(*refs):
        x_refs, o_refs = refs[:n], refs[n:2 * n]
        send_sems, recv_sems, local_sems = refs[2 * n:]
        x, y, c = _mesh_pos()
        me, sibling = (x, y, c), (x, y, 1 - c)
        chips = [(1 - x, y), (x, 1 - y), (1 - x, 1 - y)]

        def copy(i, k, block, to, from_input=False):
            dst = o_refs[i].at[_dev(*block)]
            return pltpu.make_async_remote_copy(src_ref=x_refs[i] if from_input else dst, dst_ref=dst,
                                                send_sem=send_sems.at[i, k], recv_sem=recv_sems.at[i, k],
                                                device_id=to, device_id_type=MESH)

        mine = [pltpu.make_async_copy(x_refs[i], o_refs[i].at[_dev(*me)], local_sems.at[i]) for i in range(n)]
        for cp in mine:
            cp.start()
        first = []
        for i in range(n):
            first.append(copy(i, 0, me, sibling, True))
            first += [copy(i, 1 + j, me, (*chip, c), True) for j, chip in enumerate(chips)]
        for cp in first:
            cp.start()
        passed = []
        for j, chip in enumerate(chips):
            for i in range(n):
                copy(i, 1 + j, (*chip, c), me).wait_recv()
                fwd = copy(i, 4 + j, (*chip, c), sibling)
                fwd.start()
                passed.append(fwd)
        for i in range(n):
            copy(i, 0, sibling, me).wait_recv()
            for j, chip in enumerate(chips):
                copy(i, 4 + j, (*chip, 1 - c), me).wait_recv()
        for cp in first + passed:
            cp.wait_send()
        for cp in mine:
            cp.wait()

    return _pc(body, name=name, in_specs=[HBM] * n, out_specs=tuple([HBM] * n),
               out_shape=tuple(S((N_DEV,) + x.shape, x.dtype) for x in xs),
               scratch_shapes=[pltpu.SemaphoreType.DMA((n, 7)), pltpu.SemaphoreType.DMA((n, 7)),
                               pltpu.SemaphoreType.DMA((n,))])(*xs)


def _rs_d2d(gs, name):
    n = len(gs)

    def body(*refs):
        g_refs, o_refs = refs[:n], refs[n:2 * n]
        send_sems, recv_sems = refs[2 * n:]
        x, y, c = _mesh_pos()
        copies = []
        for i in range(n):
            for q in range(4):
                cp = pltpu.make_async_remote_copy(src_ref=g_refs[i].at[2 * q + (1 - c)], dst_ref=o_refs[i].at[q],
                                                  send_sem=send_sems.at[i, q], recv_sem=recv_sems.at[i, q],
                                                  device_id=(x, y, 1 - c), device_id_type=MESH)
                cp.start()
                copies.append(cp)
        for cp in copies:
            cp.wait()

    return _pc(body, name=name, in_specs=[HBM] * n, out_specs=tuple([HBM] * n),
               out_shape=tuple(S((4,) + g.shape[1:], g.dtype) for g in gs),
               scratch_shapes=[pltpu.SemaphoreType.DMA((n, 4)), pltpu.SemaphoreType.DMA((n, 4))])(*gs)


def _rs_add(g, r1, c_vec, name):
    _, r, cols = g.shape
    tr = _divisor_block(r, 16, min(r, 352))

    def body(c_ref, g_ref, r_ref, o_ref):
        o_ref[...] = (g_ref[...].astype(f32) + r_ref[...].astype(f32)).astype(o_ref.dtype)

    blk = lambda f: pl.BlockSpec((1, tr, cols), f)
    grid_spec = pltpu.PrefetchScalarGridSpec(
        num_scalar_prefetch=1, grid=(4, r // tr),
        in_specs=[blk(lambda q, i, c_ref: (2 * q + c_ref[0], i, 0)), blk(lambda q, i, c_ref: (q, i, 0))],
        out_specs=blk(lambda q, i, c_ref: (q, i, 0)))
    return _pc(body, name=name, grid_spec=grid_spec, out_shape=S((4, r, cols), g.dtype),
               compiler_params=_cparams(("arbitrary", "arbitrary")))(c_vec, g, r1)


def _rs_ici(ps, name):
    n = len(ps)

    def body(*refs):
        p_refs, o_refs = refs[:n], refs[n:2 * n]
        send_sems, recv_sems = refs[2 * n:]
        x, y, c = _mesh_pos()
        chips = [(1 - x, y), (x, 1 - y), (1 - x, 1 - y)]
        copies = []
        for i in range(n):
            for j, (qx, qy) in enumerate(chips):
                cp = pltpu.make_async_remote_copy(src_ref=p_refs[i].at[2 * qx + qy], dst_ref=o_refs[i].at[j],
                                                  send_sem=send_sems.at[i, j], recv_sem=recv_sems.at[i, j],
                                                  device_id=(qx, qy, c), device_id_type=MESH)
                cp.start()
                copies.append(cp)
        for cp in copies:
            cp.wait()

    return _pc(body, name=name, in_specs=[HBM] * n, out_specs=tuple([HBM] * n),
               out_shape=tuple(S((3,) + p.shape[1:], p.dtype) for p in ps),
               scratch_shapes=[pltpu.SemaphoreType.DMA((n, 3)), pltpu.SemaphoreType.DMA((n, 3))])(*ps)


def _rs_final(p, r2, q_vec, name):
    _, r, cols = p.shape
    tr = _divisor_block(r, 16, min(r, 352))

    def body(q_ref, p_ref, a_ref, b_ref, c_ref, o_ref):
        o_ref[...] = ((p_ref[0].astype(f32) + a_ref[0].astype(f32)) + b_ref[0].astype(f32)) + c_ref[0].astype(f32)

    blk = lambda f: pl.BlockSpec((1, tr, cols), f)
    grid_spec = pltpu.PrefetchScalarGridSpec(
        num_scalar_prefetch=1, grid=(r // tr,),
        in_specs=[blk(lambda i, q_ref: (q_ref[0], i, 0))] + [blk(lambda i, q_ref, j=j: (j, i, 0)) for j in range(3)],
        out_specs=pl.BlockSpec((tr, cols), lambda i, q_ref: (i, 0)))
    return _pc(body, name=name, grid_spec=grid_spec, out_shape=S((r, cols), f32),
               compiler_params=_cparams(("arbitrary",)))(q_vec, p, r2, r2, r2)


def _sum_devices(a):
    def body(a_ref, o_ref):
        acc = a_ref[0]
        for d in range(1, N_DEV):
            acc = acc + a_ref[d]
        o_ref[...] = acc

    return _pc(body, name="sum_small_grads", grid=(1,), in_specs=[_full(a.shape)], out_specs=_full(a.shape[1:]),
               out_shape=S(a.shape[1:], a.dtype), compiler_params=_cparams(("arbitrary",)))(a)


def _adamw(w, m, v, g, name):
    shape = w.shape
    w2, m2, v2, g2 = (a.reshape(-1, shape[-1]) for a in (w, m, v, g))
    rows, cols = w2.shape
    tr = rows if rows % SUBLANES else _divisor_block(rows, SUBLANES, max(SUBLANES, min(rows, ADAMW_BLOCK_ELEMS // cols)))
    c1, c2 = 1.0 - ADAM_B1 ** ADAM_STEP, 1.0 - ADAM_B2 ** ADAM_STEP

    def body(w_ref, m_ref, v_ref, g_ref, d_ref, nm_ref, nv_ref):
        gv = g_ref[...]
        nm = ADAM_B1 * m_ref[...] + (1.0 - ADAM_B1) * gv
        nv = ADAM_B2 * v_ref[...] + (1.0 - ADAM_B2) * (gv * gv)
        d_ref[...] = -ADAM_LR * ((nm / c1) / (jnp.sqrt(nv / c2) + ADAM_EPS) + ADAM_WD * w_ref[...])
        nm_ref[...] = nm
        nv_ref[...] = nv

    blk = pl.BlockSpec((tr, cols), lambda i: (i, 0))
    outs = _pc(body, name=name, grid=(rows // tr,), in_specs=[blk] * 4, out_specs=(blk,) * 3,
               out_shape=(S((rows, cols), f32),) * 3, compiler_params=_cparams(("arbitrary",)))(w2, m2, v2, g2)
    return tuple(o.reshape(shape) for o in outs)


_WEIGHTS = ("meta_tokens", "norm_mix", "norm_ffn", "norm_final", "ev_w_in", "ev_conv_a", "ev_ln_a_g", "ev_ln_a_b",
            "ev_conv_b", "ev_w_out", "od_w_in", "od_sinks", "od_mu", "od_w0", "od_w2", "od_a0", "od_a2", "od_g2",
            "od_k_k", "od_k_a", "od_r_k", "od_lnx_g", "od_lnx_b", "od_w_out", "ff_w_up", "ff_conv", "ff_conv_b", "ff_w_down")
_SMALL_SHARDED = (("meta_tokens", 1), ("ev_conv_a", 2), ("ev_conv_b", 2), ("od_mu", 1), ("od_w0", 1), ("od_w2", 2),
                  ("od_a0", 1), ("od_a2", 2), ("od_g2", 2), ("od_k_k", 1), ("od_k_a", 1), ("od_lnx_g", 1),
                  ("od_lnx_b", 1), ("ff_conv", 2))
_SMALL_REPLICATED = ("norm_mix", "norm_ffn", "norm_final", "ev_ln_a_g", "ev_ln_a_b", "od_sinks", "od_r_k", "ff_conv_b")
SLAB_UNIT = SUBLANES * LANES


def _pack(arrs):
    flat = jnp.concatenate([a.reshape(-1).astype(f32) for a in arrs])
    pad = (-flat.shape[0]) % SLAB_UNIT
    return jnp.pad(flat, (0, pad)).reshape(-1, LANES)


def _unpack(flat, shapes):
    out, off = [], 0
    for shp in shapes:
        size = 1
        for s in shp:
            size *= s
        out.append(flat[..., off:off + size].reshape(flat.shape[:-1] + tuple(shp)))
        off += size
    return out


def _full_shape(shape, axis):
    return tuple(N_DEV * s if i == axis else s for i, s in enumerate(shape))


def kernel(x, meta_tokens, norm_mix, norm_ffn, norm_final, ev_w_in, ev_conv_a, ev_ln_a_g, ev_ln_a_b, ev_conv_b, ev_w_out, od_w_in, od_sinks, od_mu, od_w0, od_w2, od_a0, od_a2, od_g2, od_k_k, od_k_a, od_r_k, od_lnx_g, od_lnx_b, od_w_out, ff_w_up, ff_conv, ff_conv_b, ff_w_down, loss_target, m_meta_tokens, m_norm_mix, m_norm_ffn, m_norm_final, m_ev_w_in, m_ev_conv_a, m_ev_ln_a_g, m_ev_ln_a_b, m_ev_conv_b, m_ev_w_out, m_od_w_in, m_od_sinks, m_od_mu, m_od_w0, m_od_w2, m_od_a0, m_od_a2, m_od_g2, m_od_k_k, m_od_k_a, m_od_r_k, m_od_lnx_g, m_od_lnx_b, m_od_w_out, m_ff_w_up, m_ff_conv, m_ff_conv_b, m_ff_w_down, v_meta_tokens, v_norm_mix, v_norm_ffn, v_norm_final, v_ev_w_in, v_ev_conv_a, v_ev_ln_a_g, v_ev_ln_a_b, v_ev_conv_b, v_ev_w_out, v_od_w_in, v_od_sinks, v_od_mu, v_od_w0, v_od_w2, v_od_a0, v_od_a2, v_od_g2, v_od_k_k, v_od_k_a, v_od_r_k, v_od_lnx_g, v_od_lnx_b, v_od_w_out, v_ff_w_up, v_ff_conv, v_ff_conv_b, v_ff_w_down):
    A = dict(locals())
    px, py, pc = _mesh_pos()
    me = _dev(px, py, pc)
    c_vec = jnp.reshape(pc, (1,)).astype(jnp.int32)
    q_vec = jnp.reshape(2 * px + py, (1,)).astype(jnp.int32)

    big = [ev_w_in[0].T, ev_w_out[0], od_w_in[0].T, od_w_out[0], ff_w_up[0].T, ff_w_up[1].T, ff_w_down[0], ff_w_down[1]]
    small_shapes = [A[n].shape for n, _ in _SMALL_SHARDED]
    gathered = _all_gather([b.astype(bf16) for b in big] + [_pack([A[n] for n, _ in _SMALL_SHARDED])], "gather_params")
    fb = [g.reshape(N_DEV * g.shape[1], g.shape[2]) for g in gathered[:-1]]
    W = dict(ev_w_in_t=fb[0], ev_w_out=fb[1], od_w_in_t=fb[2], od_w_out=fb[3], ff_w_up_t=[fb[4], fb[5]], ff_w_down=[fb[6], fb[7]])
    for (n, ax), seg in zip(_SMALL_SHARDED, _unpack(gathered[-1].reshape(N_DEV, -1), small_shapes)):
        W[n] = jnp.moveaxis(seg, 0, ax).reshape(_full_shape(A[n].shape, ax))
    for n in ("ev_conv_a", "ev_conv_b", "od_w2", "od_a2", "od_g2"):
        W[n] = W[n][0]
    for n in _SMALL_REPLICATED:
        W[n] = A[n]
    W["od_r_k"] = od_r_k[0]

    loss_tile, grad_x, G = _local_step(x[0], loss_target[0], W)

    gbig = [G["ev_w_in_t"], G["ev_w_out"], G["od_w_in_t"], G["od_w_out"], G["ff_w_up_t"][0], G["ff_w_up_t"][1],
            G["ff_w_down"][0], G["ff_w_down"][1]]
    gbig = [g.reshape(N_DEV, g.shape[0] // N_DEV, g.shape[1]) for g in gbig]
    r1 = _rs_d2d(gbig, "rs_sibling")
    ps = [_rs_add(g, r, c_vec, f"rs_add{i}") for i, (g, r) in enumerate(zip(gbig, r1))]
    r2 = _rs_ici(ps, "rs_chips")
    gsh = [_rs_final(p, r, q_vec, f"rs_final{i}") for i, (p, r) in enumerate(zip(ps, r2))]
    grads = dict(ev_w_in=gsh[0].T[None], ev_w_out=gsh[1][None], od_w_in=gsh[2].T[None], od_w_out=gsh[3][None],
                 ff_w_up=jnp.stack([gsh[4].T, gsh[5].T]), ff_w_down=jnp.stack([gsh[6], gsh[7]]))

    small_names = [n for n, _ in _SMALL_SHARDED] + list(_SMALL_REPLICATED)
    small_full_shapes = [_full_shape(A[n].shape, ax) for n, ax in _SMALL_SHARDED] + [A[n].shape for n in _SMALL_REPLICATED]
    (gsm,) = _all_gather([_pack([G[n] for n in small_names])], "gather_small_grads")
    summed = _unpack(_sum_devices(gsm).reshape(-1), small_full_shapes)
    for n, full in zip(small_names, summed):
        grads[n] = full
    for n, ax in _SMALL_SHARDED:
        size = A[n].shape[ax]
        grads[n] = lax.dynamic_slice_in_dim(grads[n], me * size, size, axis=ax)

    delta, new_m, new_v = {}, {}, {}
    for n in _WEIGHTS:
        delta[n], new_m[n], new_v[n] = _adamw(A[n], A["m_" + n], A["v_" + n], grads[n], "adamw_" + n)

    loss = lax.psum(loss_tile[0, 0], ("x", "y", "c"))
    return (loss, grad_x[None], *[grads[n] for n in _WEIGHTS], *[delta[n] for n in _WEIGHTS],
            *[new_m[n] for n in _WEIGHTS], *[new_v[n] for n in _WEIGHTS])
```

```python
import jax
import jax.numpy as jnp
from jax import lax
from jax.experimental import pallas as pl
from jax.experimental.pallas import tpu as pltpu

f32, bf16 = jnp.float32, jnp.bfloat16

D_MODEL = 1024
N_META = 16
RMS_EPS = 1e-6
LN_EPS = 1e-5
D_A = 512
CONV_A_WIDTH = 31
CONV_B_WIDTH = 3
HEAD_DIM = 64
N_Q_HEADS = 8
N_KV_HEADS = 2
GQA_GROUP = 4
D_ATT = 512
D_KV = 128
BLOCK = 128
ROPE_THETA = 10000.0
D_R = 512
LORA_W, LORA_A, LORA_G = 64, 64, 128
RWKV_GN_EPS = 64e-5
ATT_COLS = D_ATT + 2 * D_KV
RWKV_COLS = 3 * D_R + LORA_W + LORA_A + LORA_G
D_FF = 2816
FF_CONV_WIDTH = 3
NEG_INF = -1e30
ATT_PAD = BLOCK - N_META
ATT_SCALE = HEAD_DIM ** -0.5

ADAM_LR, ADAM_B1, ADAM_B2, ADAM_EPS, ADAM_WD, ADAM_STEP = 0.001, 0.9, 0.999, 1e-08, 0.01, 10

N_DEV = 8
LANES = 128
SUBLANES = 8
SCAN_CHUNK = 48
PAIR_ROWS = 4 * HEAD_DIM
V7X_VMEM_LIMIT = 56 * 1024 * 1024
ADAMW_BLOCK_ELEMS = 400 * 1024
GRAD_WIRE_DTYPE = bf16
MESH = pl.DeviceIdType.MESH
S = jax.ShapeDtypeStruct
HIGHEST = lax.Precision.HIGHEST


def _pc(body, **kw):
    return pl.pallas_call(body, **kw)


def _cparams(sem=None):
    return pltpu.CompilerParams(dimension_semantics=sem, vmem_limit_bytes=V7X_VMEM_LIMIT)


def _divisor_block(t, unit, limit):
    best = unit
    for rb in range(unit, limit + 1, unit):
        if t % rb == 0:
            best = rb
    assert t % best == 0, (t, unit)
    return best


def _row_block(t):
    return _divisor_block(t, 16, 704)


def _row_block8(t):
    return _divisor_block(t, 8, 344)


def _col_tile(n):
    for t in (512, 256, 128):
        if n % t == 0:
            return t
    return n


def _full(shape):
    nd = len(shape)
    return pl.BlockSpec(shape, lambda *_: (0,) * nd)


def _sigmoid(x):
    return jax.nn.sigmoid(x)


_DIMS = {"nn": (((1,), (0,)), ((), ())), "nt": (((1,), (1,)), ((), ())), "tn": (((0,), (0,)), ((), ()))}
MM_MAX_K = 2816


def _mm(a, b, mode, name, out_dtype=f32, res=None):
    if mode == "nn":
        (m, k), (k2, n) = a.shape, b.shape
    elif mode == "nt":
        (m, k), (n, k2) = a.shape, b.shape
    else:
        (k, m), (k2, n) = a.shape, b.shape
    assert k == k2, (a.shape, b.shape, mode)
    tm = _row_block(m) if m % LANES else _col_tile(m)
    tn = _col_tile(n)
    nk = 1 if (mode == "tn" or k <= MM_MAX_K) else k // MM_MAX_K
    tk = k // nk
    assert tk * nk == k
    dims = _DIMS[mode]

    def body(a_ref, b_ref, *rest):
        part = lax.dot_general(a_ref[...].astype(bf16), b_ref[...].astype(bf16), dims, preferred_element_type=f32)
        if nk == 1:
            o_ref = rest[-1]
            if res is not None:
                part = part + rest[0][...]
            o_ref[...] = part.astype(out_dtype)
            return
        o_ref, acc_ref = rest[-2], rest[-1]
        kk = pl.program_id(2)

        @pl.when(kk == 0)
        def _():
            acc_ref[...] = part

        @pl.when(kk > 0)
        def _():
            acc_ref[...] += part

        @pl.when(kk == nk - 1)
        def _():
            acc = acc_ref[...]
            if res is not None:
                acc = acc + rest[0][...]
            o_ref[...] = acc.astype(out_dtype)

    if mode == "tn":
        a_spec = pl.BlockSpec((k, tm), lambda i, j, kk: (0, i))
    else:
        a_spec = pl.BlockSpec((tm, tk), lambda i, j, kk: (i, kk))
    if mode == "nt":
        b_spec = pl.BlockSpec((tn, tk), lambda i, j, kk: (j, kk))
    else:
        b_spec = pl.BlockSpec((tk, tn), lambda i, j, kk: (kk, j))
    o_spec = pl.BlockSpec((tm, tn), lambda i, j, kk: (i, j))
    ins, specs = [a, b], [a_spec, b_spec]
    if res is not None:
        ins.append(res)
        specs.append(o_spec)
    scratch = [pltpu.VMEM((tm, tn), f32)] if nk > 1 else []
    return _pc(body, name=name, grid=(m // tm, n // tn, nk), in_specs=specs, out_specs=o_spec,
               out_shape=S((m, n), out_dtype), scratch_shapes=scratch,
               compiler_params=_cparams(("arbitrary", "arbitrary", "arbitrary")))(*ins)


def _rms_fwd(x, g, name):
    t, d = x.shape
    rb = _row_block(t)

    def body(x_ref, g_ref, o_ref):
        xv = x_ref[...]
        rstd = lax.rsqrt(jnp.mean(xv * xv, axis=-1, keepdims=True) + RMS_EPS)
        o_ref[...] = (xv * rstd * g_ref[...]).astype(bf16)

    row = pl.BlockSpec((rb, d), lambda i: (i, 0))
    return _pc(body, name=name, grid=(t // rb,), in_specs=[row, _full((1, d))], out_specs=row,
               out_shape=S((t, d), bf16), compiler_params=_cparams(("arbitrary",)))(x, g.reshape(1, d))


def _rms_bwd(dy, x, g, dres, name):
    t, d = x.shape
    rb = _row_block8(t)

    def body(dy_ref, x_ref, g_ref, dres_ref, dx_ref, dg_ref):
        @pl.when(pl.program_id(0) == 0)
        def _():
            dg_ref[...] = jnp.zeros_like(dg_ref)
        xv, dyv = x_ref[...], dy_ref[...]
        rstd = lax.rsqrt(jnp.mean(xv * xv, axis=-1, keepdims=True) + RMS_EPS)
        xn = xv * rstd
        dg_ref[...] += jnp.sum(dyv * xn, axis=0, keepdims=True)
        dxh = dyv * g_ref[...]
        dx_ref[...] = dres_ref[...] + rstd * (dxh - xn * jnp.mean(dxh * xn, axis=-1, keepdims=True))

    row = pl.BlockSpec((rb, d), lambda i: (i, 0))
    return _pc(body, name=name, grid=(t // rb,), in_specs=[row, row, _full((1, d)), row],
               out_specs=(row, _full((1, d))), out_shape=(S((t, d), f32), S((1, d), f32)),
               compiler_params=_cparams(("arbitrary",)))(dy, x, g.reshape(1, d), dres)


def _final_loss(h, g, target_padded):
    t, d = h.shape
    rb = _row_block8(t)

    def body(x_ref, g_ref, t_ref, loss_ref, dx_ref, dg_ref):
        i = pl.program_id(0)

        @pl.when(i == 0)
        def _():
            dg_ref[...] = jnp.zeros_like(dg_ref)
            loss_ref[...] = jnp.zeros_like(loss_ref)
        xv = x_ref[...]
        rstd = lax.rsqrt(jnp.mean(xv * xv, axis=-1, keepdims=True) + RMS_EPS)
        xn = xv * rstd
        gv = g_ref[...]
        row = i * rb + lax.broadcasted_iota(jnp.int32, (rb, 1), 0)
        diff = jnp.where(row >= N_META, xn * gv - t_ref[...], 0.0)
        loss_ref[...] += 0.5 * jnp.sum(jnp.mean(diff * diff, axis=-1, keepdims=True))
        dout = diff * (1.0 / d)
        dg_ref[...] += jnp.sum(dout * xn, axis=0, keepdims=True)
        dxh = dout * gv
        dx_ref[...] = rstd * (dxh - xn * jnp.mean(dxh * xn, axis=-1, keepdims=True))

    row = pl.BlockSpec((rb, d), lambda i: (i, 0))
    return _pc(body, name="final_loss", grid=(t // rb,), in_specs=[row, _full((1, d)), row],
               out_specs=(_full((SUBLANES, LANES)), row, _full((1, d))),
               out_shape=(S((SUBLANES, LANES), f32), S((t, d), f32), S((1, d), f32)),
               compiler_params=_cparams(("arbitrary",)))(h, g.reshape(1, d), target_padded)


CONV_LEAD = 32


def _fill_front_padded(pad_ref, x, t):
    pad_ref[0:CONV_LEAD, :] = jnp.zeros((CONV_LEAD, x.shape[1]), f32)
    pad_ref[CONV_LEAD:CONV_LEAD + t, :] = x


def _fill_back_padded(pad_ref, x, t):
    pad_ref[0:t, :] = x
    pad_ref[t:t + CONV_LEAD, :] = jnp.zeros((CONV_LEAD, x.shape[1]), f32)


def _conv_rows(pad_ref, w_ref, kw, r0, nr):
    acc = None
    for j in range(kw):
        lo = CONV_LEAD + r0 - (kw - 1) + j
        term = w_ref[j:j + 1, :] * pad_ref[lo:lo + nr, :]
        acc = term if acc is None else acc + term
    return acc


def _conv_t_rows(padb_ref, w_ref, kw, r0, nr):
    acc = None
    for j in range(kw):
        lo = r0 + (kw - 1) - j
        term = w_ref[j:j + 1, :] * padb_ref[lo:lo + nr, :]
        acc = term if acc is None else acc + term
    return acc


def _conv_dw_rows(dy_blk, pad_ref, kw, r0, nr):
    out = []
    for j in range(kw):
        lo = CONV_LEAD + r0 - (kw - 1) + j
        out.append(jnp.sum(dy_blk * pad_ref[lo:lo + nr, :], axis=0, keepdims=True))
    return out


def _acc_list(a, b):
    return b if a is None else [x + y for x, y in zip(a, b)]


def _ev_a_conv(p, conv_a):
    t = p.shape[0]
    cr = _row_block8(t)
    nb = D_A // LANES

    def body(av_ref, ag_ref, w_ref, o_ref, pad_ref):
        _fill_front_padded(pad_ref, av_ref[...] * _sigmoid(ag_ref[...]), t)
        for r in range(t // cr):
            o_ref[r * cr:(r + 1) * cr, :] = _conv_rows(pad_ref, w_ref, CONV_A_WIDTH, r * cr, cr)

    col = lambda off: pl.BlockSpec((t, LANES), lambda j: (0, j + off))
    return _pc(body, name="ev_a_conv", grid=(nb,),
               in_specs=[col(0), col(nb), pl.BlockSpec((CONV_A_WIDTH, LANES), lambda j: (0, j))],
               out_specs=col(0), out_shape=S((t, D_A), f32),
               scratch_shapes=[pltpu.VMEM((t + CONV_LEAD, LANES), f32)],
               compiler_params=_cparams(("arbitrary",)))(p, p, conv_a)


def _ln_silu(uc, g, b):
    mu = jnp.mean(uc, axis=-1, keepdims=True)
    xc = uc - mu
    var = jnp.mean(xc * xc, axis=-1, keepdims=True)
    y = xc * lax.rsqrt(var + LN_EPS) * g + b
    return y * _sigmoid(y)


def _ev_a_norm(uc, g, b):
    t, d = uc.shape
    rb = _row_block(t)

    def body(u_ref, g_ref, b_ref, o_ref):
        o_ref[...] = _ln_silu(u_ref[...], g_ref[...], b_ref[...]).astype(bf16)

    row = pl.BlockSpec((rb, d), lambda i: (i, 0))
    return _pc(body, name="ev_a_norm", grid=(t // rb,), in_specs=[row, _full((1, d)), _full((1, d))],
               out_specs=row, out_shape=S((t, d), bf16), compiler_params=_cparams(("arbitrary",)))(uc, g, b)


def _ev_a_norm_bwd(dy, uc, g, b):
    t, d = uc.shape
    rb = _row_block8(t)

    def body(dy_ref, u_ref, g_ref, b_ref, du_ref, dg_ref, db_ref):
        @pl.when(pl.program_id(0) == 0)
        def _():
            dg_ref[...] = jnp.zeros_like(dg_ref)
            db_ref[...] = jnp.zeros_like(db_ref)
        _, vjp = jax.vjp(_ln_silu, u_ref[...], g_ref[...], b_ref[...])
        du, dg, db = vjp(dy_ref[...])
        du_ref[...] = du
        dg_ref[...] += dg
        db_ref[...] += db

    row = pl.BlockSpec((rb, d), lambda i: (i, 0))
    return _pc(body, name="ev_a_norm_bwd", grid=(t // rb,), in_specs=[row, row, _full((1, d)), _full((1, d))],
               out_specs=(row, _full((1, d)), _full((1, d))),
               out_shape=(S((t, d), f32), S((1, d), f32), S((1, d), f32)),
               compiler_params=_cparams(("arbitrary",)))(dy, uc, g, b)


def _ev_a_conv_bwd(duc, p, conv_a):
    t = p.shape[0]
    cr = _row_block8(t)
    nb = D_A // LANES

    def body(dy_ref, av_ref, ag_ref, w_ref, dav_ref, dag_ref, dw_ref, pad_ref, padb_ref):
        _fill_front_padded(pad_ref, av_ref[...] * _sigmoid(ag_ref[...]), t)
        _fill_back_padded(padb_ref, dy_ref[...], t)
        dw = None
        for r in range(t // cr):
            rows = slice(r * cr, (r + 1) * cr)
            du = _conv_t_rows(padb_ref, w_ref, CONV_A_WIDTH, r * cr, cr)
            avr = av_ref[rows, :]
            sgr = _sigmoid(ag_ref[rows, :])
            dav_ref[rows, :] = du * sgr
            dag_ref[rows, :] = du * avr * sgr * (1.0 - sgr)
            dw = _acc_list(dw, _conv_dw_rows(dy_ref[rows, :], pad_ref, CONV_A_WIDTH, r * cr, cr))
        for j in range(CONV_A_WIDTH):
            dw_ref[j:j + 1, :] = dw[j]

    col = lambda off: pl.BlockSpec((t, LANES), lambda j: (0, j + off))
    wsp = pl.BlockSpec((CONV_A_WIDTH, LANES), lambda j: (0, j))
    return _pc(body, name="ev_a_conv_bwd", grid=(nb,), in_specs=[col(0), col(0), col(nb), wsp],
               out_specs=(col(0), col(0), wsp),
               out_shape=(S((t, D_A), f32), S((t, D_A), f32), S((CONV_A_WIDTH, D_A), f32)),
               scratch_shapes=[pltpu.VMEM((t + CONV_LEAD, LANES), f32), pltpu.VMEM((t + CONV_LEAD, LANES), f32)],
               compiler_params=_cparams(("arbitrary",)))(duc, p, p, conv_a)


def _ev_b(p, conv_b):
    t = p.shape[0]
    cr = _row_block8(t)
    nb = D_A // LANES

    def body(gb_ref, gc_ref, xi_ref, w_ref, o_ref, pad_ref, stage_ref):
        _fill_front_padded(pad_ref, gc_ref[...] * xi_ref[...], t)
        for r in range(t // cr):
            rows = slice(r * cr, (r + 1) * cr)
            stage_ref[rows, :] = gb_ref[rows, :] * _conv_rows(pad_ref, w_ref, CONV_B_WIDTH, r * cr, cr)
        o_ref[...] = stage_ref[...].astype(bf16)

    col = lambda off: pl.BlockSpec((t, LANES), lambda j: (0, j + off))
    return _pc(body, name="ev_b", grid=(nb,),
               in_specs=[col(2 * nb), col(3 * nb), col(4 * nb), pl.BlockSpec((CONV_B_WIDTH, LANES), lambda j: (0, j))],
               out_specs=col(0), out_shape=S((t, D_A), bf16),
               scratch_shapes=[pltpu.VMEM((t + CONV_LEAD, LANES), f32), pltpu.VMEM((t, LANES), f32)],
               compiler_params=_cparams(("arbitrary",)))(p, p, p, conv_b)


def _ev_b_bwd(dy, p, conv_b):
    t = p.shape[0]
    cr = _row_block8(t)
    nb = D_A // LANES

    def body(dy_ref, gb_ref, gc_ref, xi_ref, w_ref, dgb_ref, dgc_ref, dxi_ref, dw_ref, pad_ref, padb_ref):
        _fill_front_padded(pad_ref, gc_ref[...] * xi_ref[...], t)
        _fill_back_padded(padb_ref, dy_ref[...] * gb_ref[...], t)
        dw = None
        for r in range(t // cr):
            rows = slice(r * cr, (r + 1) * cr)
            dgb_ref[rows, :] = dy_ref[rows, :] * _conv_rows(pad_ref, w_ref, CONV_B_WIDTH, r * cr, cr)
            dcx = _conv_t_rows(padb_ref, w_ref, CONV_B_WIDTH, r * cr, cr)
            dgc_ref[rows, :] = dcx * xi_ref[rows, :]
            dxi_ref[rows, :] = dcx * gc_ref[rows, :]
            dw = _acc_list(dw, _conv_dw_rows(padb_ref[rows, :], pad_ref, CONV_B_WIDTH, r * cr, cr))
        for j in range(CONV_B_WIDTH):
            dw_ref[j:j + 1, :] = dw[j]

    col = lambda off: pl.BlockSpec((t, LANES), lambda j: (0, j + off))
    wsp = pl.BlockSpec((CONV_B_WIDTH, LANES), lambda j: (0, j))
    return _pc(body, name="ev_b_bwd", grid=(nb,), in_specs=[col(nb), col(2 * nb), col(3 * nb), col(4 * nb), wsp],
               out_specs=(col(0), col(0), col(0), wsp),
               out_shape=(S((t, D_A), f32), S((t, D_A), f32), S((t, D_A), f32), S((CONV_B_WIDTH, D_A), f32)),
               scratch_shapes=[pltpu.VMEM((t + CONV_LEAD, LANES), f32), pltpu.VMEM((t + CONV_LEAD, LANES), f32)],
               compiler_params=_cparams(("arbitrary",)))(dy, p, p, p, conv_b)


def _ffn_mid(u, conv_w, conv_b, name):
    t = u.shape[0]
    cr = _row_block8(t)
    nb = D_FF // LANES

    def body(gt_ref, vl_ref, w_ref, b_ref, o_ref, pad_ref, stage_ref):
        _fill_front_padded(pad_ref, gt_ref[...], t)
        for r in range(t // cr):
            rows = slice(r * cr, (r + 1) * cr)
            gc = _conv_rows(pad_ref, w_ref, FF_CONV_WIDTH, r * cr, cr) + b_ref[...]
            stage_ref[rows, :] = gc * _sigmoid(gc) * vl_ref[rows, :]
        o_ref[...] = stage_ref[...].astype(bf16)

    col = lambda off: pl.BlockSpec((t, LANES), lambda j: (0, j + off))
    return _pc(body, name=name, grid=(nb,),
               in_specs=[col(0), col(nb), pl.BlockSpec((FF_CONV_WIDTH, LANES), lambda j: (0, j)),
                         pl.BlockSpec((1, LANES), lambda j: (0, j))],
               out_specs=col(0), out_shape=S((t, D_FF), bf16),
               scratch_shapes=[pltpu.VMEM((t + CONV_LEAD, LANES), f32), pltpu.VMEM((t, LANES), f32)],
               compiler_params=_cparams(("arbitrary",)))(u, u, conv_w, conv_b.reshape(1, D_FF))


def _ffn_mid_bwd(dz, u, conv_w, conv_b, name):
    t = u.shape[0]
    cr = _row_block8(t)
    nb = D_FF // LANES

    def body(dz_ref, gt_ref, vl_ref, w_ref, b_ref, du_ref, dw_ref, db_ref, pad_ref, padb_ref):
        s = pl.program_id(1)
        _fill_front_padded(pad_ref, gt_ref[...], t)

        @pl.when(s == 0)
        def _():
            for r in range(t // cr):
                rows = slice(r * cr, (r + 1) * cr)
                gc = _conv_rows(pad_ref, w_ref, FF_CONV_WIDTH, r * cr, cr) + b_ref[...]
                sg = _sigmoid(gc)
                padb_ref[rows, :] = dz_ref[rows, :] * vl_ref[rows, :] * sg * (1.0 + gc * (1.0 - sg))
            padb_ref[t:t + CONV_LEAD, :] = jnp.zeros((CONV_LEAD, LANES), f32)
            dw, db = None, None
            for r in range(t // cr):
                rows = slice(r * cr, (r + 1) * cr)
                du_ref[rows, :] = _conv_t_rows(padb_ref, w_ref, FF_CONV_WIDTH, r * cr, cr)
                dgc = padb_ref[rows, :]
                dw = _acc_list(dw, _conv_dw_rows(dgc, pad_ref, FF_CONV_WIDTH, r * cr, cr))
                pb = jnp.sum(dgc, axis=0, keepdims=True)
                db = pb if db is None else db + pb
            for j in range(FF_CONV_WIDTH):
                dw_ref[j:j + 1, :] = dw[j]
            db_ref[...] = db

        @pl.when(s == 1)
        def _():
            for r in range(t // cr):
                rows = slice(r * cr, (r + 1) * cr)
                gc = _conv_rows(pad_ref, w_ref, FF_CONV_WIDTH, r * cr, cr) + b_ref[...]
                du_ref[rows, :] = dz_ref[rows, :] * gc * _sigmoid(gc)

    col = lambda off: pl.BlockSpec((t, LANES), lambda j, s: (0, j + off))
    wsp = pl.BlockSpec((FF_CONV_WIDTH, LANES), lambda j, s: (0, j))
    bsp = pl.BlockSpec((1, LANES), lambda j, s: (0, j))
    return _pc(body, name=name, grid=(nb, 2), in_specs=[col(0), col(0), col(nb), wsp, bsp],
               out_specs=(pl.BlockSpec((t, LANES), lambda j, s: (0, s * nb + j)), wsp, bsp),
               out_shape=(S((t, 2 * D_FF), f32), S((FF_CONV_WIDTH, D_FF), f32), S((1, D_FF), f32)),
               scratch_shapes=[pltpu.VMEM((t + CONV_LEAD, LANES), f32), pltpu.VMEM((t + CONV_LEAD, LANES), f32)],
               compiler_params=_cparams(("arbitrary", "arbitrary")))(dz, u, u, conv_w, conv_b.reshape(1, D_FF))


def _swap_halves(x):
    w = x.shape[1]
    lane = lax.broadcasted_iota(jnp.int32, x.shape, 1) % HEAD_DIM
    return jnp.where(lane < HEAD_DIM // 2, pltpu.roll(x, w - HEAD_DIM // 2, axis=1), pltpu.roll(x, HEAD_DIM // 2, axis=1))


def _rope_pack(patt, c64, s64):
    t = patt.shape[0]
    tp = t + ATT_PAD

    def body(p_ref, c_ref, s_ref, q_ref, k_ref, v_ref):
        c, s = c_ref[...], s_ref[...]

        def rope(x, nh):
            cc = jnp.concatenate([c] * nh, axis=1)
            ss = jnp.concatenate([s] * nh, axis=1)
            return x * cc + _swap_halves(x) * ss

        for ref, val in ((q_ref, rope(p_ref[:, 0:D_ATT], N_Q_HEADS)),
                         (k_ref, rope(p_ref[:, D_ATT:D_ATT + D_KV], N_KV_HEADS)),
                         (v_ref, p_ref[:, D_ATT + D_KV:ATT_COLS])):
            ref[0:ATT_PAD, :] = jnp.zeros((ATT_PAD, val.shape[1]), bf16)
            ref[ATT_PAD:tp, :] = val.astype(bf16)

    return _pc(body, name="rope_pack", in_specs=[_full((t, ATT_COLS)), _full((t, HEAD_DIM)), _full((t, HEAD_DIM))],
               out_specs=(_full((tp, D_ATT)), _full((tp, D_KV)), _full((tp, D_KV))), grid=(1,),
               out_shape=(S((tp, D_ATT), bf16), S((tp, D_KV), bf16), S((tp, D_KV), bf16)),
               compiler_params=_cparams(("arbitrary",)))(patt, c64, s64)


def _rope_bwd(dqp, dkp, dvp, c64, s64):
    tp = dqp.shape[0]
    t = tp - ATT_PAD

    def body(dq_ref, dk_ref, dv_ref, c_ref, s_ref, o_ref):
        c, s = c_ref[...], s_ref[...]

        def unrope(dy, nh):
            cc = jnp.concatenate([c] * nh, axis=1)
            ss = jnp.concatenate([s] * nh, axis=1)
            return dy * cc + _swap_halves(dy * ss)

        o_ref[:, 0:D_ATT] = unrope(dq_ref[ATT_PAD:tp, :], N_Q_HEADS)
        o_ref[:, D_ATT:D_ATT + D_KV] = unrope(dk_ref[ATT_PAD:tp, :], N_KV_HEADS)
        o_ref[:, D_ATT + D_KV:ATT_COLS] = dv_ref[ATT_PAD:tp, :]

    return _pc(body, name="rope_bwd", grid=(1,),
               in_specs=[_full((tp, D_ATT)), _full((tp, D_KV)), _full((tp, D_KV)), _full((t, HEAD_DIM)), _full((t, HEAD_DIM))],
               out_specs=_full((t, ATT_COLS)), out_shape=S((t, ATT_COLS), f32),
               compiler_params=_cparams(("arbitrary",)))(dqp, dkp, dvp, c64, s64)


def _attn_masks(n):
    rows = GQA_GROUP * BLOCK
    ri = lax.broadcasted_iota(jnp.int32, (rows, BLOCK), 0) % BLOCK
    ci = lax.broadcasted_iota(jnp.int32, (rows, BLOCK), 1)
    m_cur = (ci <= ri) & (ci >= jnp.where(n >= 1, 0, ATT_PAD))
    m_prev = ci > ri + jnp.where(n >= 2, 0, BLOCK)
    m_meta = ci >= jnp.where(n >= 1, ATT_PAD, BLOCK)
    return m_cur, m_prev, m_meta


def _attn_probs(qg, kc, kp, km, masks, skv):
    def scores(k, m):
        s = lax.dot_general(qg, k, _DIMS["nt"], preferred_element_type=f32) * ATT_SCALE
        return jnp.where(m, s, NEG_INF)
    s_c, s_p, s_m = scores(kc, masks[0]), scores(kp, masks[1]), scores(km, masks[2])
    mx = jnp.maximum(jnp.maximum(jnp.max(s_c, axis=-1, keepdims=True), jnp.max(s_p, axis=-1, keepdims=True)),
                     jnp.maximum(jnp.max(s_m, axis=-1, keepdims=True), skv))
    e_c, e_p, e_m, e_s = jnp.exp(s_c - mx), jnp.exp(s_p - mx), jnp.exp(s_m - mx), jnp.exp(skv - mx)
    den = (jnp.sum(e_c, axis=-1, keepdims=True) + jnp.sum(e_p, axis=-1, keepdims=True)
           + jnp.sum(e_m, axis=-1, keepdims=True) + e_s)
    inv = 1.0 / den
    return e_c * inv, e_p * inv, e_m * inv, e_s * inv


def _sink_rows(sk_ref, g):
    hrow = lax.broadcasted_iota(jnp.int32, (GQA_GROUP * BLOCK, 1), 0) // BLOCK
    skv = jnp.zeros((GQA_GROUP * BLOCK, 1), f32)
    for hh in range(GQA_GROUP):
        skv = jnp.where(hrow == hh, sk_ref[0, GQA_GROUP * g + hh], skv)
    return skv, hrow


def _stack_heads(ref, g):
    return jnp.concatenate([ref[:, (GQA_GROUP * g + hh) * HEAD_DIM:(GQA_GROUP * g + hh + 1) * HEAD_DIM]
                            for hh in range(GQA_GROUP)], axis=0)


def _attn_specs():
    blk = lambda w: pl.BlockSpec((BLOCK, w), lambda n: (n, 0))
    prev = pl.BlockSpec((BLOCK, D_KV), lambda n: (jnp.maximum(n - 1, 0), 0))
    meta = pl.BlockSpec((BLOCK, D_KV), lambda n: (0, 0))
    return blk, prev, meta


def _attn_fwd(qp, kp, vp, sinks):
    tp = qp.shape[0]
    blk, prev, meta = _attn_specs()

    def body(sk_ref, q_ref, kc_ref, kp_ref, km_ref, vc_ref, vp_ref, vm_ref, o_ref):
        masks = _attn_masks(pl.program_id(0))
        for g in range(N_KV_HEADS):
            sl = slice(g * HEAD_DIM, (g + 1) * HEAD_DIM)
            skv, _ = _sink_rows(sk_ref, g)
            p_c, p_p, p_m, _ = _attn_probs(_stack_heads(q_ref, g), kc_ref[:, sl], kp_ref[:, sl], km_ref[:, sl], masks, skv)
            o = (jnp.dot(p_c.astype(bf16), vc_ref[:, sl], preferred_element_type=f32)
                 + jnp.dot(p_p.astype(bf16), vp_ref[:, sl], preferred_element_type=f32)
                 + jnp.dot(p_m.astype(bf16), vm_ref[:, sl], preferred_element_type=f32))
            for hh in range(GQA_GROUP):
                h = GQA_GROUP * g + hh
                o_ref[:, h * HEAD_DIM:(h + 1) * HEAD_DIM] = o[hh * BLOCK:(hh + 1) * BLOCK].astype(bf16)

    return _pc(body, name="attn_fwd", grid=(tp // BLOCK,),
               in_specs=[pl.BlockSpec(memory_space=pltpu.SMEM), blk(D_ATT), blk(D_KV), prev, meta, blk(D_KV), prev, meta],
               out_specs=blk(D_ATT), out_shape=S((tp, D_ATT), bf16),
               compiler_params=_cparams(("arbitrary",)))(sinks, qp, kp, kp, kp, vp, vp, vp)


def _attn_bwd(qp, kp, vp, sinks, dop):
    tp = qp.shape[0]
    blk, prev, meta = _attn_specs()

    def body(sk_ref, q_ref, kc_ref, kp_ref, km_ref, vc_ref, vp_ref, vm_ref, do_ref, dq_ref, dk_ref, dv_ref, dsk_ref):
        n = pl.program_id(0)

        @pl.when(n == 0)
        def _():
            dk_ref[...] = jnp.zeros_like(dk_ref)
            dv_ref[...] = jnp.zeros_like(dv_ref)
            dsk_ref[...] = jnp.zeros_like(dsk_ref)
        masks = _attn_masks(n)
        cur = pl.ds(pl.multiple_of(n * BLOCK, BLOCK), BLOCK)
        prv = pl.ds(pl.multiple_of(jnp.maximum(n - 1, 0) * BLOCK, BLOCK), BLOCK)
        lane = lax.broadcasted_iota(jnp.int32, (1, LANES), 1)
        dsk = jnp.zeros((1, LANES), f32)
        for g in range(N_KV_HEADS):
            sl = slice(g * HEAD_DIM, (g + 1) * HEAD_DIM)
            skv, hrow = _sink_rows(sk_ref, g)
            qg = _stack_heads(q_ref, g)
            dog = _stack_heads(do_ref, g)
            ks = (kc_ref[:, sl], kp_ref[:, sl], km_ref[:, sl])
            vs = (vc_ref[:, sl], vp_ref[:, sl], vm_ref[:, sl])
            probs = _attn_probs(qg, ks[0], ks[1], ks[2], masks, skv)
            dps = [lax.dot_general(dog, v, _DIMS["nt"], preferred_element_type=f32) for v in vs]
            delta = sum(jnp.sum(p * dp, axis=-1, keepdims=True) for p, dp in zip(probs[:3], dps))
            dss = [(p * (dp - delta) * ATT_SCALE).astype(bf16) for p, dp in zip(probs[:3], dps)]
            dq = sum(jnp.dot(ds, k, preferred_element_type=f32) for ds, k in zip(dss, ks))
            for hh in range(GQA_GROUP):
                h = GQA_GROUP * g + hh
                dq_ref[:, h * HEAD_DIM:(h + 1) * HEAD_DIM] = dq[hh * BLOCK:(hh + 1) * BLOCK]
                dsk = dsk + jnp.where(lane == h, -jnp.sum(jnp.where(hrow == hh, probs[3] * delta, 0.0)), 0.0)
            for rows, p, ds in zip((cur, prv, slice(0, BLOCK)), probs[:3], dss):
                dv_ref[rows, sl] += lax.dot_general(p.astype(bf16), dog, _DIMS["tn"], preferred_element_type=f32)
                dk_ref[rows, sl] += lax.dot_general(ds, qg, _DIMS["tn"], preferred_element_type=f32)
        dsk_ref[...] += dsk

    return _pc(body, name="attn_bwd", grid=(tp // BLOCK,),
               in_specs=[pl.BlockSpec(memory_space=pltpu.SMEM), blk(D_ATT), blk(D_KV), prev, meta, blk(D_KV), prev, meta,
                         blk(D_ATT)],
               out_specs=(blk(D_ATT), _full((tp, D_KV)), _full((tp, D_KV)), _full((1, LANES))),
               out_shape=(S((tp, D_ATT), f32), S((tp, D_KV), f32), S((tp, D_KV), f32), S((1, LANES), f32)),
               compiler_params=_cparams(("arbitrary",)))(sinks, qp, kp, kp, kp, vp, vp, vp, dop)


def _seg(x, bm):
    return jnp.dot(x, bm, precision=HIGHEST, preferred_element_type=f32)


def _softplus(y):
    return jnp.maximum(y, 0.0) + jnp.log(1.0 + jnp.exp(-jnp.abs(y)))


def _prep_fn(xr, xk, xwd, xad, xgd, w0, w2, a0, a2, g2, k_k, k_a, bm):
    xw = w0 + jnp.dot(jnp.tanh(xwd), w2, preferred_element_type=f32)
    decay = jnp.exp(-jnp.exp(-_softplus(-xw) - 0.5))
    alpha = _sigmoid(a0 + jnp.dot(xad, a2, preferred_element_type=f32))
    g = jnp.dot(_sigmoid(xgd), g2, preferred_element_type=f32)
    kk = xk * k_k
    kkn = kk / jnp.maximum(jnp.sqrt(_seg(kk * kk, bm)), 1e-12)
    k2 = xk * (1.0 + (alpha - 1.0) * k_a)
    return decay, k2, -kkn, kkn * alpha, g


def _split_cols(x):
    o1, o2, o3 = 3 * D_R, 3 * D_R + LORA_W, 3 * D_R + LORA_W + LORA_A
    return x[:, 0:D_R], x[:, D_R:2 * D_R], x[:, 2 * D_R:o1], x[:, o1:o2], x[:, o2:o3], x[:, o3:RWKV_COLS]


def _shifted(sh_ref, x, halo, first, rb):
    sh_ref[0:SUBLANES, :] = jnp.where(first, 0.0, halo)
    sh_ref[SUBLANES:SUBLANES + rb, :] = x
    return sh_ref[SUBLANES - 1:SUBLANES - 1 + rb, :]


_PREP_PARAMS = ("od_w0", "od_w2", "od_a0", "od_a2", "od_g2", "od_k_k", "od_k_a")


def _rwkv_prep(pr, mu, params, bm):
    t = pr.shape[0]
    rb = _row_block8(t)
    hb = rb // SUBLANES

    def body(pr_ref, halo_ref, mu_ref, w0, w2, a0, a2, g2, kk_ref, ka_ref, bm_ref, *outs_sh):
        outs, sh_ref = outs_sh[:-1], outs_sh[-1]
        x = pr_ref[...]
        prev = _shifted(sh_ref, x, halo_ref[...], pl.program_id(0) == 0, rb)
        xr, xk, xv, xwd, xad, xgd = _split_cols(x + (prev - x) * mu_ref[...])
        bmv = bm_ref[...]
        decay, k2, a_s, b_s, g = _prep_fn(xr, xk, xwd, xad, xgd, w0[...], w2[...], a0[...], a2[...], g2[...],
                                          kk_ref[...], ka_ref[...], bmv)
        vals = (xr, xv, decay, k2, a_s, b_s, decay * xr, _seg(b_s * xr, bmv), _seg(k2 * xr, bmv), g)
        for ref, val in zip(outs, vals):
            ref[...] = val

    row = pl.BlockSpec((rb, RWKV_COLS), lambda i: (i, 0))
    halo = pl.BlockSpec((SUBLANES, RWKV_COLS), lambda i: (jnp.maximum(i * hb - 1, 0), 0))
    orow = pl.BlockSpec((rb, D_R), lambda i: (i, 0))
    return _pc(body, name="rwkv_prep", grid=(t // rb,),
               in_specs=[row, halo, _full((1, RWKV_COLS))] + [_full(p.shape) for p in params] + [_full(bm.shape)],
               out_specs=(orow,) * 10, out_shape=(S((t, D_R), f32),) * 10,
               scratch_shapes=[pltpu.VMEM((rb + SUBLANES, RWKV_COLS), f32)],
               compiler_params=_cparams(("arbitrary",)))(pr, pr, mu, *params, bm)


def _rwkv_prep_bwd(pr, mu, params, bm, cts):
    t = pr.shape[0]
    rb = _row_block8(t)
    hb = rb // SUBLANES
    counts = [len(c) for c in cts]
    flat = [a for c in cts for a in c]

    def body(pr_ref, halo_ref, mu_ref, w0, w2, a0, a2, g2, kk_ref, ka_ref, bm_ref, *rest):
        ct_refs, rest = rest[:len(flat)], rest[len(flat):]
        dx_ref, dmu_ref = rest[0], rest[1]
        dpar_refs, sh_ref = rest[2:9], rest[9]

        @pl.when(pl.program_id(0) == 0)
        def _():
            dmu_ref[...] = jnp.zeros_like(dmu_ref)
            for r in dpar_refs:
                r[...] = jnp.zeros_like(r)
        sums, pos = [], 0
        for c in counts:
            sums.append(sum(r[...] for r in ct_refs[pos:pos + c]))
            pos += c
        x = pr_ref[...]
        prev = _shifted(sh_ref, x, halo_ref[...], pl.program_id(0) == 0, rb)
        xr, xk, xv, xwd, xad, xgd = _split_cols(x + (prev - x) * mu_ref[...])
        bmv = bm_ref[...]
        _, vjp = jax.vjp(lambda *a: _prep_fn(*a, bmv), xr, xk, xwd, xad, xgd, w0[...], w2[...], a0[...], a2[...],
                         g2[...], kk_ref[...], ka_ref[...])
        grads = vjp(tuple(sums[:5]))
        dxr, dxk, dxwd, dxad, dxgd = grads[:5]
        o1, o2, o3 = 3 * D_R, 3 * D_R + LORA_W, 3 * D_R + LORA_W + LORA_A
        dx_ref[:, 0:D_R] = dxr + sums[5]
        dx_ref[:, D_R:2 * D_R] = dxk
        dx_ref[:, 2 * D_R:o1] = sums[6]
        dx_ref[:, o1:o2] = dxwd
        dx_ref[:, o2:o3] = dxad
        dx_ref[:, o3:RWKV_COLS] = dxgd
        dmu_ref[...] += jnp.sum(dx_ref[...] * (prev - x), axis=0, keepdims=True)
        for r, gval in zip(dpar_refs, grads[5:]):
            r[...] += gval

    row = pl.BlockSpec((rb, RWKV_COLS), lambda i: (i, 0))
    halo = pl.BlockSpec((SUBLANES, RWKV_COLS), lambda i: (jnp.maximum(i * hb - 1, 0), 0))
    crow = pl.BlockSpec((rb, D_R), lambda i: (i, 0))
    return _pc(body, name="rwkv_prep_bwd", grid=(t // rb,),
               in_specs=[row, halo, _full((1, RWKV_COLS))] + [_full(p.shape) for p in params] + [_full(bm.shape)]
               + [crow] * len(flat),
               out_specs=(row, _full((1, RWKV_COLS))) + tuple(_full(p.shape) for p in params),
               out_shape=(S((t, RWKV_COLS), f32), S((1, RWKV_COLS), f32)) + tuple(S(p.shape, f32) for p in params),
               scratch_shapes=[pltpu.VMEM((rb + SUBLANES, RWKV_COLS), f32)],
               compiler_params=_cparams(("arbitrary",)))(pr, pr, mu, *params, bm, *flat)


def _shift_bwd(dxs, mu):
    t = dxs.shape[0]
    rb = _row_block8(t)
    hb = rb // SUBLANES
    nblk = t // rb

    def body(dx_ref, halo_ref, mu_ref, o_ref, sh_ref):
        dx = dx_ref[...]
        sh_ref[0:rb, :] = dx
        sh_ref[rb:rb + SUBLANES, :] = jnp.where(pl.program_id(0) == nblk - 1, 0.0, halo_ref[...])
        m = mu_ref[...]
        o_ref[...] = dx * (1.0 - m) + sh_ref[1:1 + rb, :] * m

    row = pl.BlockSpec((rb, RWKV_COLS), lambda i: (i, 0))
    halo = pl.BlockSpec((SUBLANES, RWKV_COLS), lambda i: (jnp.minimum((i + 1) * hb, t // SUBLANES - 1), 0))
    return _pc(body, name="rwkv_shift_bwd", grid=(nblk,), in_specs=[row, halo, _full((1, RWKV_COLS))],
               out_specs=row, out_shape=S((t, RWKV_COLS), f32),
               scratch_shapes=[pltpu.VMEM((rb + SUBLANES, RWKV_COLS), f32)],
               compiler_params=_cparams(("arbitrary",)))(dxs, dxs, mu)


def _post_fn(y, xr, k2, xv, g, lg, lb, rk, bm):
    inv_n = 1.0 / HEAD_DIM
    yc = y - _seg(y, bm) * inv_n
    var = _seg(yc * yc, bm) * inv_n
    yn = yc * lax.rsqrt(var + RWKV_GN_EPS) * lg + lb
    return (yn + _seg(xr * k2 * rk, bm) * xv) * g


def _rwkv_post(y, xr, k2, xv, g, lg, lb, rk, bm):
    t = y.shape[0]
    rb = _row_block8(t)

    def body(y_ref, xr_ref, k2_ref, xv_ref, g_ref, lg_ref, lb_ref, rk_ref, bm_ref, o_ref):
        o_ref[...] = _post_fn(y_ref[...], xr_ref[...], k2_ref[...], xv_ref[...], g_ref[...], lg_ref[...], lb_ref[...],
                              rk_ref[...], bm_ref[...])

    row = pl.BlockSpec((rb, D_R), lambda i: (i, 0))
    vec = _full((1, D_R))
    return _pc(body, name="rwkv_post", grid=(t // rb,), in_specs=[row] * 5 + [vec] * 3 + [_full(bm.shape)],
               out_specs=row, out_shape=S((t, D_R), f32),
               compiler_params=_cparams(("arbitrary",)))(y, xr, k2, xv, g, lg, lb, rk, bm)


def _rwkv_post_bwd(dy1, y, xr, k2, xv, g, lg, lb, rk, bm):
    t = y.shape[0]
    rb = _row_block8(t)

    def body(dy_ref, y_ref, xr_ref, k2_ref, xv_ref, g_ref, lg_ref, lb_ref, rk_ref, bm_ref, *outs):
        @pl.when(pl.program_id(0) == 0)
        def _():
            for r in outs[5:]:
                r[...] = jnp.zeros_like(r)
        bmv = bm_ref[...]
        _, vjp = jax.vjp(lambda *a: _post_fn(*a, bmv), y_ref[...], xr_ref[...], k2_ref[...], xv_ref[...], g_ref[...],
                         lg_ref[...], lb_ref[...], rk_ref[...])
        grads = vjp(dy_ref[...])
        for r, gval in zip(outs[:5], grads[:5]):
            r[...] = gval
        for r, gval in zip(outs[5:], grads[5:]):
            r[...] += gval

    row = pl.BlockSpec((rb, D_R), lambda i: (i, 0))
    vec = _full((1, D_R))
    return _pc(body, name="rwkv_post_bwd", grid=(t // rb,),
               in_specs=[pl.BlockSpec((rb, D_R), lambda i: (i, 1))] + [row] * 5 + [vec] * 3 + [_full(bm.shape)],
               out_specs=(row,) * 5 + (vec,) * 3, out_shape=(S((t, D_R), f32),) * 5 + (S((1, D_R), f32),) * 3,
               compiler_params=_cparams(("arbitrary",)))(dy1, y, xr, k2, xv, g, lg, lb, rk, bm)


def _seg2(x, bb):
    hi = x.astype(bf16)
    lo = (x - hi.astype(f32)).astype(bf16)
    return jnp.dot(jnp.concatenate([hi, lo], axis=1), bb, preferred_element_type=f32)


def _row4(rows, j):
    return jnp.concatenate([jnp.broadcast_to(rows[j:j + 1, p * LANES:(p + 1) * LANES], (HEAD_DIM, LANES))
                            for p in range(4)], axis=0)


def _scan_consts():
    lane_group = jnp.arange(LANES) // HEAD_DIM
    b128 = (lane_group[:, None] == lane_group[None, :]).astype(bf16)
    bb = jnp.concatenate([b128, b128], axis=0)
    qsel = (jnp.arange(PAIR_ROWS)[:, None] % HEAD_DIM == jnp.arange(LANES)[None, :] % HEAD_DIM).astype(f32)
    return bb, qsel


def _store_cols(acc_ref, o_ref, tc):
    for p in range(4):
        blk = acc_ref[p * HEAD_DIM:(p + 1) * HEAD_DIM, :].T
        o_ref[:, (2 * p) * HEAD_DIM:(2 * p + 1) * HEAD_DIM] = blk[0:tc]
        o_ref[:, (2 * p + 1) * HEAD_DIM:(2 * p + 2) * HEAD_DIM] = blk[HEAD_DIM:HEAD_DIM + tc]


PAIR_GROUP = 2 * SUBLANES


def _rwkv_pairs(w, a, b, k, wr, bm):
    t = w.shape[0]
    rb = _row_block8(t)

    def body(w_ref, a_ref, b_ref, k_ref, wr_ref, bm_ref, *outs_sh):
        outs, sh_ref = outs_sh[:-1], outs_sh[-1]

        def second(ref):
            sh_ref[0:rb, :] = ref[...]
            sh_ref[rb:rb + SUBLANES, :] = jnp.zeros((SUBLANES, D_R), f32)
            return sh_ref[1:1 + rb, :]

        w1, b1, k1 = w_ref[...], b_ref[...], k_ref[...]
        w2, a2, wr2 = second(w_ref), second(a_ref), second(wr_ref)
        bmv = bm_ref[...]
        vals = (w1 * a2, w1 * wr2, w1 * w2, b1 * w2, k1 * w2, _seg(b1 * a2, bmv), _seg(k1 * a2, bmv),
                _seg(b1 * wr2, bmv), _seg(k1 * wr2, bmv))
        for ref, val in zip(outs, vals):
            ref[...] = val

    row = pl.BlockSpec((rb, D_R), lambda i: (i, 0))
    return _pc(body, name="rwkv_pairs", grid=(t // rb,), in_specs=[row] * 5 + [_full(bm.shape)],
               out_specs=(row,) * 9, out_shape=(S((t, D_R), f32),) * 9,
               scratch_shapes=[pltpu.VMEM((rb + SUBLANES, D_R), f32)],
               compiler_params=_cparams(("arbitrary",)))(w, a, b, k, wr, bm)


def _wkv_fwd(w, k, v, a, b, wr, br, kr, pairs):
    t = w.shape[0]
    tc = SCAN_CHUNK
    bb, qsel = _scan_consts()

    def body(*refs):
        step_refs, pair_refs = refs[0:8], refs[8:17]
        bb_ref, q_ref, y_ref, st_ref, sa_ref, vb_ref, s_scr, yacc = refs[17:]

        @pl.when(pl.program_id(0) == 0)
        def _():
            s_scr[...] = jnp.zeros_like(s_scr)
        bbv, qv = bb_ref[...], q_ref[...]
        lane64 = lax.broadcasted_iota(jnp.int32, (PAIR_ROWS, LANES), 1) % HEAD_DIM

        def halves(x):
            hi = x.astype(bf16)
            return jnp.concatenate([hi, (x - hi.astype(f32)).astype(bf16)], axis=1)

        def group(gi, s):
            base = pl.multiple_of(gi * PAIR_GROUP, PAIR_GROUP)
            w16, k16, v16, a16, b16, wr16, br16, kr16 = (
                (ref[pl.ds(base, SUBLANES), :], ref[pl.ds(base + SUBLANES, SUBLANES), :]) for ref in step_refs)
            a2p, r2p, w12p, b1wp, k1wp, betap, kappap, bwrp, kwrp = (
                (ref[pl.ds(base, SUBLANES), :], ref[pl.ds(base + SUBLANES, SUBLANES), :]) for ref in pair_refs)
            vh16 = tuple(x.astype(bf16).astype(f32) for x in v16)
            vl16 = tuple(x - h for x, h in zip(v16, vh16))
            step = lambda arr, j: _row4(arr[j // SUBLANES], j % SUBLANES)
            for q in range(SUBLANES):
                j1, j2 = 2 * q, 2 * q + 1
                t1 = base + j1
                lhs = [halves(jnp.concatenate([s * step(a16, j1), s * step(a2p, j1), s * step(wr16, j1), s * step(r2p, j1)],
                                              axis=0))]
                for j in (j1, j2):
                    lhs.append(jnp.concatenate([(qv * step(vh16, j)).astype(bf16), (qv * step(vl16, j)).astype(bf16)], axis=1))
                r = jnp.dot(jnp.concatenate(lhs, axis=0), bbv, preferred_element_type=f32)
                sa1, p2, z1, z2, vb1, vb2 = (r[n * PAIR_ROWS:(n + 1) * PAIR_ROWS] for n in range(6))
                sa2 = p2 + sa1 * step(betap, j1) + vb1 * step(kappap, j1)
                y1 = z1 + sa1 * step(br16, j1) + vb1 * step(kr16, j1)
                y2 = (z2 + sa1 * step(bwrp, j1) + vb1 * step(kwrp, j1)) + (sa2 * step(br16, j2) + vb2 * step(kr16, j2))
                yacc[...] = jnp.where(lane64 == t1, y1, jnp.where(lane64 == t1 + 1, y2, yacc[...]))
                st_ref[t1] = s
                st_ref[t1 + 1] = s * step(w16, j1) + sa1 * step(b16, j1) + vb1 * step(k16, j1)
                sa_ref[t1] = sa1
                sa_ref[t1 + 1] = sa2
                vb_ref[t1] = vb1
                vb_ref[t1 + 1] = vb2
                s = ((s * step(w12p, j1) + sa1 * step(b1wp, j1)) + vb1 * step(k1wp, j1)) + (sa2 * step(b16, j2) + vb2 * step(k16, j2))
            return s

        s_scr[...] = lax.fori_loop(0, tc // PAIR_GROUP, group, s_scr[...])
        _store_cols(yacc, y_ref, tc)

    row = pl.BlockSpec((tc, D_R), lambda c: (c, 0))
    tiles = pl.BlockSpec((tc, PAIR_ROWS, LANES), lambda c: (c, 0, 0))
    return _pc(body, name="wkv_fwd", grid=(t // tc,),
               in_specs=[row] * 17 + [_full(bb.shape), _full(qsel.shape)],
               out_specs=(row, tiles, tiles, tiles),
               out_shape=(S((t, D_R), f32),) + (S((t, PAIR_ROWS, LANES), f32),) * 3,
               scratch_shapes=[pltpu.VMEM((PAIR_ROWS, LANES), f32), pltpu.VMEM((PAIR_ROWS, LANES), f32)],
               compiler_params=_cparams(("arbitrary",)))(w, k, v, a, b, wr, br, kr, *pairs, bb, qsel)


def _wkv_bwd(sprev, sab, vbb, w, k, a, b, r, dy):
    t = w.shape[0]
    tc = SCAN_CHUNK
    nc = t // tc
    bb, qsel = _scan_consts()

    def body(st_ref, sa_ref, vb_ref, w_ref, k_ref, a_ref, b_ref, r_ref, dy_ref, bb_ref, q_ref,
             dr_ref, dw_ref, dk_ref, dv_ref, da_ref, db_ref, g_scr, dvacc, rows_scr):
        @pl.when(pl.program_id(0) == 0)
        def _():
            g_scr[...] = jnp.zeros_like(g_scr)
        bbv, qv = bb_ref[...], q_ref[...]
        lane64 = lax.broadcasted_iota(jnp.int32, (PAIR_ROWS, LANES), 1) % HEAD_DIM
        outs = (dr_ref, dw_ref, db_ref, dk_ref, da_ref)

        def colsums(slot, j, x):
            for p in range(4):
                rows_scr[slot, j:j + 1, p * LANES:(p + 1) * LANES] = jnp.sum(x[p * HEAD_DIM:(p + 1) * HEAD_DIM], axis=0,
                                                                           keepdims=True)

        def group(i, g):
            base = pl.multiple_of((tc // SUBLANES - 1 - i) * SUBLANES, SUBLANES)
            w8, k8, a8, b8, r8, dy8 = (ref[pl.ds(base, SUBLANES), :] for ref in (w_ref, k_ref, a_ref, b_ref, r_ref, dy_ref))
            for j in reversed(range(SUBLANES)):
                tt = base + j
                sp, u, vb = st_ref[tt], sa_ref[tt], vb_ref[tt]
                a4, b4, w4, k4 = _row4(a8, j), _row4(b8, j), _row4(w8, j), _row4(k8, j)
                dyb = _seg2(qv * _row4(dy8, j), bbv)
                s_t = sp * w4 + u * b4 + vb * k4
                g = g + dyb * _row4(r8, j)
                rr2 = _seg2(jnp.concatenate([g * b4, g * k4], axis=0), bbv)
                du, dvb = rr2[0:PAIR_ROWS], rr2[PAIR_ROWS:2 * PAIR_ROWS]
                for slot, val in enumerate((s_t * dyb, g * sp, g * u, g * vb, sp * du)):
                    colsums(slot, j, val)
                dvacc[...] = jnp.where(lane64 == tt, dvb, dvacc[...])
                g = g * w4 + du * a4
            for slot, ref in enumerate(outs):
                ref[pl.ds(base, SUBLANES), :] = rows_scr[slot]
            return g

        g_scr[...] = lax.fori_loop(0, tc // SUBLANES, group, g_scr[...])
        _store_cols(dvacc, dv_ref, tc)

    row = pl.BlockSpec((tc, D_R), lambda c: (nc - 1 - c, 0))
    tiles = pl.BlockSpec((tc, PAIR_ROWS, LANES), lambda c: (nc - 1 - c, 0, 0))
    return _pc(body, name="wkv_bwd", grid=(nc,),
               in_specs=[tiles] * 3 + [row] * 6 + [_full(bb.shape), _full(qsel.shape)],
               out_specs=(row,) * 6, out_shape=(S((t, D_R), f32),) * 6,
               scratch_shapes=[pltpu.VMEM((PAIR_ROWS, LANES), f32), pltpu.VMEM((PAIR_ROWS, LANES), f32),
                               pltpu.VMEM((5, SUBLANES, D_R), f32)],
               compiler_params=_cparams(("arbitrary",)))(sprev, sab, vbb, w, k, a, b, r, dy, bb, qsel)


def _rope_tables(t):
    half = HEAD_DIM // 2
    inv = ROPE_THETA ** (-jnp.arange(half, dtype=f32) / half)
    ang = jnp.arange(t, dtype=f32)[:, None] * inv[None, :]
    cos, sin = jnp.cos(ang), jnp.sin(ang)
    return jnp.concatenate([cos, cos], axis=1), jnp.concatenate([-sin, sin], axis=1)


def _head_matrix():
    grp = jnp.arange(D_R) // HEAD_DIM
    return (grp[:, None] == grp[None, :]).astype(f32)


def _ffn_fwd(h, g, w_up_t, conv_w, conv_b, w_down, i):
    hf = _rms_fwd(h, g, f"ffn{i}_norm")
    u = _mm(hf, w_up_t, "nt", f"ffn{i}_up")
    z = _ffn_mid(u, conv_w, conv_b, f"ffn{i}_mid")
    return _mm(z, w_down, "nn", f"ffn{i}_down", res=h), (hf, u, z)


def _ffn_bwd(dh, h, saved, g, w_up_t, conv_w, conv_b, w_down, i):
    hf, u, z = saved
    dz = _mm(dh, w_down, "nt", f"ffn{i}_dz")
    g_down = _mm(z, dh, "tn", f"ffn{i}_gdown", out_dtype=GRAD_WIRE_DTYPE)
    du, g_conv, g_convb = _ffn_mid_bwd(dz, u, conv_w, conv_b, f"ffn{i}_mid_bwd")
    g_up_t = _mm(du, hf, "tn", f"ffn{i}_gup", out_dtype=GRAD_WIRE_DTYPE)
    dhf = _mm(du, w_up_t, "nn", f"ffn{i}_dhf")
    dh_in, g_norm = _rms_bwd(dhf, h, g, dh, f"ffn{i}_norm_bwd")
    return dh_in, dict(up_t=g_up_t, down=g_down, conv=g_conv, conv_b=g_convb, norm=g_norm)


def _local_step(x, target, W):
    t = N_META + x.shape[0]
    c64, s64 = _rope_tables(t)
    bm = _head_matrix()
    h0 = jnp.concatenate([W["meta_tokens"], x], axis=0)

    hn0 = _rms_fwd(h0, W["norm_mix"][0], "mix0_norm")
    p0 = _mm(hn0, W["ev_w_in_t"], "nt", "ev_in")
    uc = _ev_a_conv(p0, W["ev_conv_a"])
    y0 = jnp.concatenate([_ev_a_norm(uc, W["ev_ln_a_g"], W["ev_ln_a_b"]), _ev_b(p0, W["ev_conv_b"])], axis=1)
    h1 = _mm(y0, W["ev_w_out"], "nn", "ev_out", res=h0)
    h2, ffn0 = _ffn_fwd(h1, W["norm_ffn"][0], W["ff_w_up_t"][0], W["ff_conv"][0], W["ff_conv_b"][0], W["ff_w_down"][0], 0)

    hn1 = _rms_fwd(h2, W["norm_mix"][1], "mix1_norm")
    p1 = _mm(hn1, W["od_w_in_t"], "nt", "od_in")
    pr = p1[:, ATT_COLS:]
    qp, kp, vp = _rope_pack(p1[:, :ATT_COLS], c64, s64)
    op = _attn_fwd(qp, kp, vp, W["od_sinks"])
    prep_params = [W[n] for n in _PREP_PARAMS]
    xr, xv, decay, k2, a_s, b_s, wr, br, kr, gate = _rwkv_prep(pr, W["od_mu"], prep_params, bm)
    pairs = _rwkv_pairs(decay, a_s, b_s, k2, wr, bm)
    ysc, sprev, sab, vbb = _wkv_fwd(decay, k2, xv, a_s, b_s, wr, br, kr, pairs)
    rk = W["od_r_k"].reshape(1, D_R)
    yr = _rwkv_post(ysc, xr, k2, xv, gate, W["od_lnx_g"], W["od_lnx_b"], rk, bm)
    y1 = jnp.concatenate([op[ATT_PAD:], yr.astype(bf16)], axis=1)
    h3 = _mm(y1, W["od_w_out"], "nn", "od_out", res=h2)
    h4, ffn1 = _ffn_fwd(h3, W["norm_ffn"][1], W["ff_w_up_t"][1], W["ff_conv"][1], W["ff_conv_b"][1], W["ff_w_down"][1], 1)

    tgt = jnp.concatenate([jnp.zeros((N_META, D_MODEL), f32), target], axis=0)
    loss, dh4, g_norm_final = _final_loss(h4, W["norm_final"], tgt)

    dh3, gf1 = _ffn_bwd(dh4, h3, ffn1, W["norm_ffn"][1], W["ff_w_up_t"][1], W["ff_conv"][1], W["ff_conv_b"][1],
                        W["ff_w_down"][1], 1)
    dy1 = _mm(dh3, W["od_w_out"], "nt", "od_dy")
    g_od_w_out = _mm(y1, dh3, "tn", "od_gout", out_dtype=GRAD_WIRE_DTYPE)
    dysc, dxr_p, dk2_p, dxv_p, dgate, g_lnx_g, g_lnx_b, g_rk = _rwkv_post_bwd(
        dy1, ysc, xr, k2, xv, gate, W["od_lnx_g"], W["od_lnx_b"], rk, bm)
    dr, dw, dk, dv, da, db = _wkv_bwd(sprev, sab, vbb, decay, k2, a_s, b_s, xr, dysc)
    prep_grads = _rwkv_prep_bwd(pr, W["od_mu"], prep_params, bm,
                                [[dw], [dk, dk2_p], [da], [db], [dgate], [dr, dxr_p], [dv, dxv_p]])
    dxs, g_mu = prep_grads[0], prep_grads[1]
    dpr = _shift_bwd(dxs, W["od_mu"])
    dop = jnp.concatenate([jnp.zeros((ATT_PAD, D_ATT), f32), dy1[:, :D_ATT]], axis=0).astype(bf16)
    dqp, dkp, dvp, dsk = _attn_bwd(qp, kp, vp, W["od_sinks"], dop)
    dp1 = jnp.concatenate([_rope_bwd(dqp, dkp, dvp, c64, s64), dpr], axis=1)
    g_od_w_in_t = _mm(dp1, hn1, "tn", "od_gin", out_dtype=GRAD_WIRE_DTYPE)
    dhn1 = _mm(dp1, W["od_w_in_t"], "nn", "od_dhn")
    dh2, g_norm_mix1 = _rms_bwd(dhn1, h2, W["norm_mix"][1], dh3, "mix1_norm_bwd")

    dh1, gf0 = _ffn_bwd(dh2, h1, ffn0, W["norm_ffn"][0], W["ff_w_up_t"][0], W["ff_conv"][0], W["ff_conv_b"][0],
                        W["ff_w_down"][0], 0)
    dy0 = _mm(dh1, W["ev_w_out"], "nt", "ev_dy")
    g_ev_w_out = _mm(y0, dh1, "tn", "ev_gout", out_dtype=GRAD_WIRE_DTYPE)
    duc, g_ln_g, g_ln_b = _ev_a_norm_bwd(dy0, uc, W["ev_ln_a_g"], W["ev_ln_a_b"])
    dav, dag, g_conv_a = _ev_a_conv_bwd(duc, p0, W["ev_conv_a"])
    dgb, dgc, dxi, g_conv_b = _ev_b_bwd(dy0, p0, W["ev_conv_b"])
    dp0 = jnp.concatenate([dav, dag, dgb, dgc, dxi], axis=1)
    g_ev_w_in_t = _mm(dp0, hn0, "tn", "ev_gin", out_dtype=GRAD_WIRE_DTYPE)
    dhn0 = _mm(dp0, W["ev_w_in_t"], "nn", "ev_dhn")
    dh0, g_norm_mix0 = _rms_bwd(dhn0, h0, W["norm_mix"][0], dh1, "mix0_norm_bwd")

    G = dict(
        meta_tokens=dh0[:N_META], norm_mix=jnp.concatenate([g_norm_mix0, g_norm_mix1], axis=0),
        norm_ffn=jnp.concatenate([gf0["norm"], gf1["norm"]], axis=0), norm_final=g_norm_final.reshape(D_MODEL),
        ev_w_in_t=g_ev_w_in_t, ev_conv_a=g_conv_a, ev_ln_a_g=g_ln_g, ev_ln_a_b=g_ln_b, ev_conv_b=g_conv_b,
        ev_w_out=g_ev_w_out, od_w_in_t=g_od_w_in_t, od_sinks=dsk[:, :N_Q_HEADS], od_mu=g_mu,
        od_lnx_g=g_lnx_g, od_lnx_b=g_lnx_b, od_r_k=g_rk.reshape(N_Q_HEADS, HEAD_DIM), od_w_out=g_od_w_out,
        ff_w_up_t=[gf0["up_t"], gf1["up_t"]], ff_w_down=[gf0["down"], gf1["down"]],
        ff_conv=jnp.stack([gf0["conv"], gf1["conv"]]), ff_conv_b=jnp.concatenate([gf0["conv_b"], gf1["conv_b"]], axis=0),
    )
    for name, gval in zip(_PREP_PARAMS, prep_grads[2:]):
        G[name] = gval
    return loss, dh0[N_META:], G


HBM = pl.BlockSpec(memory_space=pl.ANY)


def _mesh_pos():
    return lax.axis_index("x"), lax.axis_index("y"), lax.axis_index("c")


def _dev(px, py, pc):
    return 4 * px + 2 * py + pc


def _all_gather(xs, name):
    n = len(xs)

    def body(*refs):
        x_refs, o_refs = refs[:n], refs[n:2 * n]
        send_sems, recv_sems, local_sems = refs[2 * n:]
        x, y, c = _mesh_pos()
        me, sibling = (x, y, c), (x, y, 1 - c)
        chips = [(1 - x, y), (x, 1 - y), (1 - x, 1 - y)]

        def copy(i, k, block, to, from_input=False):
            dst = o_refs[i].at[_dev(*block)]
            return pltpu.make_async_remote_copy(src_ref=x_refs[i] if from_input else dst, dst_ref=dst,
                                                send_sem=send_sems.at[i, k], recv_sem=recv_sems.at[i, k],
                                                device_id=to, device_id_type=MESH)

        mine = [pltpu.make_async_copy(x_refs[i], o_refs[i].at[_dev(*me)], local_sems.at[i]) for i in range(n)]
        for cp in mine:
            cp.start()
        first = []
        for i in range(n):
            first.append(copy(i, 0, me, sibling, True))
            first += [copy(i, 1 + j, me, (*chip, c), True) for j, chip in enumerate(chips)]
        for cp in first:
            cp.start()
        passed = []
        for j, chip in enumerate(chips):
            for i in range(n):
                copy(i, 1 + j, (*chip, c), me).wait_recv()
                fwd = copy(i, 4 + j, (*chip, c), sibling)
                fwd.start()
                passed.append(fwd)
        for i in range(n):
            copy(i, 0, sibling, me).wait_recv()
            for j, chip in enumerate(chips):
                copy(i, 4 + j, (*chip, 1 - c), me).wait_recv()
        for cp in first + passed:
            cp.wait_send()
        for cp in mine:
            cp.wait()

    return _pc(body, name=name, in_specs=[HBM] * n, out_specs=tuple([HBM] * n),
               out_shape=tuple(S((N_DEV,) + x.shape, x.dtype) for x in xs),
               scratch_shapes=[pltpu.SemaphoreType.DMA((n, 7)), pltpu.SemaphoreType.DMA((n, 7)),
                               pltpu.SemaphoreType.DMA((n,))])(*xs)


def _rs_d2d(gs, name):
    n = len(gs)

    def body(*refs):
        g_refs, o_refs = refs[:n], refs[n:2 * n]
        send_sems, recv_sems = refs[2 * n:]
        x, y, c = _mesh_pos()
        copies = []
        for i in range(n):
            for q in range(4):
                cp = pltpu.make_async_remote_copy(src_ref=g_refs[i].at[2 * q + (1 - c)], dst_ref=o_refs[i].at[q],
                                                  send_sem=send_sems.at[i, q], recv_sem=recv_sems.at[i, q],
                                                  device_id=(x, y, 1 - c), device_id_type=MESH)
                cp.start()
                copies.append(cp)
        for cp in copies:
            cp.wait()

    return _pc(body, name=name, in_specs=[HBM] * n, out_specs=tuple([HBM] * n),
               out_shape=tuple(S((4,) + g.shape[1:], g.dtype) for g in gs),
               scratch_shapes=[pltpu.SemaphoreType.DMA((n, 4)), pltpu.SemaphoreType.DMA((n, 4))])(*gs)


def _rs_add(g, r1, c_vec, name):
    _, r, cols = g.shape
    tr = _divisor_block(r, 16, min(r, 352))

    def body(c_ref, g_ref, r_ref, o_ref):
        o_ref[...] = (g_ref[...].astype(f32) + r_ref[...].astype(f32)).astype(o_ref.dtype)

    blk = lambda f: pl.BlockSpec((1, tr, cols), f)
    grid_spec = pltpu.PrefetchScalarGridSpec(
        num_scalar_prefetch=1, grid=(4, r // tr),
        in_specs=[blk(lambda q, i, c_ref: (2 * q + c_ref[0], i, 0)), blk(lambda q, i, c_ref: (q, i, 0))],
        out_specs=blk(lambda q, i, c_ref: (q, i, 0)))
    return _pc(body, name=name, grid_spec=grid_spec, out_shape=S((4, r, cols), g.dtype),
               compiler_params=_cparams(("arbitrary", "arbitrary")))(c_vec, g, r1)


def _rs_ici(ps, name):
    n = len(ps)

    def body(*refs):
        p_refs, o_refs = refs[:n], refs[n:2 * n]
        send_sems, recv_sems = refs[2 * n:]
        x, y, c = _mesh_pos()
        chips = [(1 - x, y), (x, 1 - y), (1 - x, 1 - y)]
        copies = []
        for i in range(n):
            for j, (qx, qy) in enumerate(chips):
                cp = pltpu.make_async_remote_copy(src_ref=p_refs[i].at[2 * qx + qy], dst_ref=o_refs[i].at[j],
                                                  send_sem=send_sems.at[i, j], recv_sem=recv_sems.at[i, j],
                                                  device_id=(qx, qy, c), device_id_type=MESH)
                cp.start()
                copies.append(cp)
        for cp in copies:
            cp.wait()

    return _pc(body, name=name, in_specs=[HBM] * n, out_specs=tuple([HBM] * n),
               out_shape=tuple(S((3,) + p.shape[1:], p.dtype) for p in ps),
               scratch_shapes=[pltpu.SemaphoreType.DMA((n, 3)), pltpu.SemaphoreType.DMA((n, 3))])(*ps)


def _rs_final(p, r2, q_vec, name):
    _, r, cols = p.shape
    tr = _divisor_block(r, 16, min(r, 352))

    def body(q_ref, p_ref, a_ref, b_ref, c_ref, o_ref):
        o_ref[...] = ((p_ref[0].astype(f32) + a_ref[0].astype(f32)) + b_ref[0].astype(f32)) + c_ref[0].astype(f32)

    blk = lambda f: pl.BlockSpec((1, tr, cols), f)
    grid_spec = pltpu.PrefetchScalarGridSpec(
        num_scalar_prefetch=1, grid=(r // tr,),
        in_specs=[blk(lambda i, q_ref: (q_ref[0], i, 0))] + [blk(lambda i, q_ref, j=j: (j, i, 0)) for j in range(3)],
        out_specs=pl.BlockSpec((tr, cols), lambda i, q_ref: (i, 0)))
    return _pc(body, name=name, grid_spec=grid_spec, out_shape=S((r, cols), f32),
               compiler_params=_cparams(("arbitrary",)))(q_vec, p, r2, r2, r2)


def _sum_devices(a):
    def body(a_ref, o_ref):
        acc = a_ref[0]
        for d in range(1, N_DEV):
            acc = acc + a_ref[d]
        o_ref[...] = acc

    return _pc(body, name="sum_small_grads", grid=(1,), in_specs=[_full(a.shape)], out_specs=_full(a.shape[1:]),
               out_shape=S(a.shape[1:], a.dtype), compiler_params=_cparams(("arbitrary",)))(a)


def _adamw(w, m, v, g, name):
    shape = w.shape
    w2, m2, v2, g2 = (a.reshape(-1, shape[-1]) for a in (w, m, v, g))
    rows, cols = w2.shape
    tr = rows if rows % SUBLANES else _divisor_block(rows, SUBLANES, max(SUBLANES, min(rows, ADAMW_BLOCK_ELEMS // cols)))
    c1, c2 = 1.0 - ADAM_B1 ** ADAM_STEP, 1.0 - ADAM_B2 ** ADAM_STEP

    def body(w_ref, m_ref, v_ref, g_ref, d_ref, nm_ref, nv_ref):
        gv = g_ref[...]
        nm = ADAM_B1 * m_ref[...] + (1.0 - ADAM_B1) * gv
        nv = ADAM_B2 * v_ref[...] + (1.0 - ADAM_B2) * (gv * gv)
        d_ref[...] = -ADAM_LR * ((nm / c1) / (jnp.sqrt(nv / c2) + ADAM_EPS) + ADAM_WD * w_ref[...])
        nm_ref[...] = nm
        nv_ref[...] = nv

    blk = pl.BlockSpec((tr, cols), lambda i: (i, 0))
    outs = _pc(body, name=name, grid=(rows // tr,), in_specs=[blk] * 4, out_specs=(blk,) * 3,
               out_shape=(S((rows, cols), f32),) * 3, compiler_params=_cparams(("arbitrary",)))(w2, m2, v2, g2)
    return tuple(o.reshape(shape) for o in outs)


_WEIGHTS = ("meta_tokens", "norm_mix", "norm_ffn", "norm_final", "ev_w_in", "ev_conv_a", "ev_ln_a_g", "ev_ln_a_b",
            "ev_conv_b", "ev_w_out", "od_w_in", "od_sinks", "od_mu", "od_w0", "od_w2", "od_a0", "od_a2", "od_g2",
            "od_k_k", "od_k_a", "od_r_k", "od_lnx_g", "od_lnx_b", "od_w_out", "ff_w_up", "ff_conv", "ff_conv_b", "ff_w_down")
_SMALL_SHARDED = (("meta_tokens", 1), ("ev_conv_a", 2), ("ev_conv_b", 2), ("od_mu", 1), ("od_w0", 1), ("od_w2", 2),
                  ("od_a0", 1), ("od_a2", 2), ("od_g2", 2), ("od_k_k", 1), ("od_k_a", 1), ("od_lnx_g", 1),
                  ("od_lnx_b", 1), ("ff_conv", 2))
_SMALL_REPLICATED = ("norm_mix", "norm_ffn", "norm_final", "ev_ln_a_g", "ev_ln_a_b", "od_sinks", "od_r_k", "ff_conv_b")
SLAB_UNIT = SUBLANES * LANES


def _pack(arrs):
    flat = jnp.concatenate([a.reshape(-1).astype(f32) for a in arrs])
    pad = (-flat.shape[0]) % SLAB_UNIT
    return jnp.pad(flat, (0, pad)).reshape(-1, LANES)


def _unpack(flat, shapes):
    out, off = [], 0
    for shp in shapes:
        size = 1
        for s in shp:
            size *= s
        out.append(flat[..., off:off + size].reshape(flat.shape[:-1] + tuple(shp)))
        off += size
    return out


def _full_shape(shape, axis):
    return tuple(N_DEV * s if i == axis else s for i, s in enumerate(shape))


def kernel(x, meta_tokens, norm_mix, norm_ffn, norm_final, ev_w_in, ev_conv_a, ev_ln_a_g, ev_ln_a_b, ev_conv_b, ev_w_out, od_w_in, od_sinks, od_mu, od_w0, od_w2, od_a0, od_a2, od_g2, od_k_k, od_k_a, od_r_k, od_lnx_g, od_lnx_b, od_w_out, ff_w_up, ff_conv, ff_conv_b, ff_w_down, loss_target, m_meta_tokens, m_norm_mix, m_norm_ffn, m_norm_final, m_ev_w_in, m_ev_conv_a, m_ev_ln_a_g, m_ev_ln_a_b, m_ev_conv_b, m_ev_w_out, m_od_w_in, m_od_sinks, m_od_mu, m_od_w0, m_od_w2, m_od_a0, m_od_a2, m_od_g2, m_od_k_k, m_od_k_a, m_od_r_k, m_od_lnx_g, m_od_lnx_b, m_od_w_out, m_ff_w_up, m_ff_conv, m_ff_conv_b, m_ff_w_down, v_meta_tokens, v_norm_mix, v_norm_ffn, v_norm_final, v_ev_w_in, v_ev_conv_a, v_ev_ln_a_g, v_ev_ln_a_b, v_ev_conv_b, v_ev_w_out, v_od_w_in, v_od_sinks, v_od_mu, v_od_w0, v_od_w2, v_od_a0, v_od_a2, v_od_g2, v_od_k_k, v_od_k_a, v_od_r_k, v_od_lnx_g, v_od_lnx_b, v_od_w_out, v_ff_w_up, v_ff_conv, v_ff_conv_b, v_ff_w_down):
    A = dict(locals())
    px, py, pc = _mesh_pos()
    me = _dev(px, py, pc)
    c_vec = jnp.reshape(pc, (1,)).astype(jnp.int32)
    q_vec = jnp.reshape(2 * px + py, (1,)).astype(jnp.int32)

    big = [ev_w_in[0].T, ev_w_out[0], od_w_in[0].T, od_w_out[0], ff_w_up[0].T, ff_w_up[1].T, ff_w_down[0], ff_w_down[1]]
    small_shapes = [A[n].shape for n, _ in _SMALL_SHARDED]
    gathered = _all_gather([b.astype(bf16) for b in big] + [_pack([A[n] for n, _ in _SMALL_SHARDED])], "gather_params")
    fb = [g.reshape(N_DEV * g.shape[1], g.shape[2]) for g in gathered[:-1]]
    W = dict(ev_w_in_t=fb[0], ev_w_out=fb[1], od_w_in_t=fb[2], od_w_out=fb[3], ff_w_up_t=[fb[4], fb[5]], ff_w_down=[fb[6], fb[7]])
    for (n, ax), seg in zip(_SMALL_SHARDED, _unpack(gathered[-1].reshape(N_DEV, -1), small_shapes)):
        W[n] = jnp.moveaxis(seg, 0, ax).reshape(_full_shape(A[n].shape, ax))
    for n in ("ev_conv_a", "ev_conv_b", "od_w2", "od_a2", "od_g2"):
        W[n] = W[n][0]
    for n in _SMALL_REPLICATED:
        W[n] = A[n]
    W["od_r_k"] = od_r_k[0]

    loss_tile, grad_x, G = _local_step(x[0], loss_target[0], W)

    gbig = [G["ev_w_in_t"], G["ev_w_out"], G["od_w_in_t"], G["od_w_out"], G["ff_w_up_t"][0], G["ff_w_up_t"][1],
            G["ff_w_down"][0], G["ff_w_down"][1]]
    gbig = [g.reshape(N_DEV, g.shape[0] // N_DEV, g.shape[1]) for g in gbig]
    r1 = _rs_d2d(gbig, "rs_sibling")
    ps = [_rs_add(g, r, c_vec, f"rs_add{i}") for i, (g, r) in enumerate(zip(gbig, r1))]
    r2 = _rs_ici(ps, "rs_chips")
    gsh = [_rs_final(p, r, q_vec, f"rs_final{i}") for i, (p, r) in enumerate(zip(ps, r2))]
    grads = dict(ev_w_in=gsh[0].T[None], ev_w_out=gsh[1][None], od_w_in=gsh[2].T[None], od_w_out=gsh[3][None],
                 ff_w_up=jnp.stack([gsh[4].T, gsh[5].T]), ff_w_down=jnp.stack([gsh[6], gsh[7]]))

    small_names = [n for n, _ in _SMALL_SHARDED] + list(_SMALL_REPLICATED)
    small_full_shapes = [_full_shape(A[n].shape, ax) for n, ax in _SMALL_SHARDED] + [A[n].shape for n in _SMALL_REPLICATED]
    (gsm,) = _all_gather([_pack([G[n] for n in small_names])], "gather_small_grads")
    summed = _unpack(_sum_devices(gsm).reshape(-1), small_full_shapes)
    for n, full in zip(small_names, summed):
        grads[n] = full
    for n, ax in _SMALL_SHARDED:
        size = A[n].shape[ax]
        grads[n] = lax.dynamic_slice_in_dim(grads[n], me * size, size, axis=ax)

    delta, new_m, new_v = {}, {}, {}
    for n in _WEIGHTS:
        delta[n], new_m[n], new_v[n] = _adamw(A[n], A["m_" + n], A["v_" + n], grads[n], "adamw_" + n)

    loss = lax.psum(loss_tile[0, 0], ("x", "y", "c"))
    return (loss, grad_x[None], *[grads[n] for n in _WEIGHTS], *[delta[n] for n in _WEIGHTS],
            *[new_m[n] for n in _WEIGHTS], *[new_v[n] for n in _WEIGHTS])
```

```python
import jax
import jax.numpy as jnp
from jax import lax
from jax.experimental import pallas as pl
from jax.experimental.pallas import tpu as pltpu

f32, bf16 = jnp.float32, jnp.bfloat16

D_MODEL = 1024
N_META = 16
RMS_EPS = 1e-6
LN_EPS = 1e-5
D_A = 512
CONV_A_WIDTH = 31
CONV_B_WIDTH = 3
HEAD_DIM = 64
N_Q_HEADS = 8
N_KV_HEADS = 2
GQA_GROUP = 4
D_ATT = 512
D_KV = 128
BLOCK = 128
ROPE_THETA = 10000.0
D_R = 512
LORA_W, LORA_A, LORA_G = 64, 64, 128
RWKV_GN_EPS = 64e-5
ATT_COLS = D_ATT + 2 * D_KV
RWKV_COLS = 3 * D_R + LORA_W + LORA_A + LORA_G
D_FF = 2816
FF_CONV_WIDTH = 3
NEG_INF = -1e30
ATT_PAD = BLOCK - N_META
ATT_SCALE = HEAD_DIM ** -0.5

ADAM_LR, ADAM_B1, ADAM_B2, ADAM_EPS, ADAM_WD, ADAM_STEP = 0.001, 0.9, 0.999, 1e-08, 0.01, 10

N_DEV = 8
LANES = 128
SUBLANES = 8
SCAN_CHUNK = 48
PAIR_ROWS = 4 * HEAD_DIM
V7X_VMEM_LIMIT = 56 * 1024 * 1024
ADAMW_BLOCK_ELEMS = 400 * 1024
GRAD_WIRE_DTYPE = bf16
MESH = pl.DeviceIdType.MESH
S = jax.ShapeDtypeStruct
HIGHEST = lax.Precision.HIGHEST


def _pc(body, **kw):
    return pl.pallas_call(body, **kw)


def _cparams(sem=None):
    return pltpu.CompilerParams(dimension_semantics=sem, vmem_limit_bytes=V7X_VMEM_LIMIT)


def _divisor_block(t, unit, limit):
    best = unit
    for rb in range(unit, limit + 1, unit):
        if t % rb == 0:
            best = rb
    assert t % best == 0, (t, unit)
    return best


def _row_block(t):
    return _divisor_block(t, 16, 704)


def _row_block8(t):
    return _divisor_block(t, 8, 344)


def _col_tile(n):
    for t in (512, 256, 128):
        if n % t == 0:
            return t
    return n


def _full(shape):
    nd = len(shape)
    return pl.BlockSpec(shape, lambda *_: (0,) * nd)


def _sigmoid(x):
    return jax.nn.sigmoid(x)


_DIMS = {"nn": (((1,), (0,)), ((), ())), "nt": (((1,), (1,)), ((), ())), "tn": (((0,), (0,)), ((), ()))}
MM_MAX_K = 2816


def _mm(a, b, mode, name, out_dtype=f32, res=None):
    if mode == "nn":
        (m, k), (k2, n) = a.shape, b.shape
    elif mode == "nt":
        (m, k), (n, k2) = a.shape, b.shape
    else:
        (k, m), (k2, n) = a.shape, b.shape
    assert k == k2, (a.shape, b.shape, mode)
    tm = _row_block(m) if m % LANES else _col_tile(m)
    tn = _col_tile(n)
    nk = 1 if (mode == "tn" or k <= MM_MAX_K) else k // MM_MAX_K
    tk = k // nk
    assert tk * nk == k
    dims = _DIMS[mode]

    def body(a_ref, b_ref, *rest):
        part = lax.dot_general(a_ref[...].astype(bf16), b_ref[...].astype(bf16), dims, preferred_element_type=f32)
        if nk == 1:
            o_ref = rest[-1]
            if res is not None:
                part = part + rest[0][...]
            o_ref[...] = part.astype(out_dtype)
            return
        o_ref, acc_ref = rest[-2], rest[-1]
        kk = pl.program_id(2)

        @pl.when(kk == 0)
        def _():
            acc_ref[...] = part

        @pl.when(kk > 0)
        def _():
            acc_ref[...] += part

        @pl.when(kk == nk - 1)
        def _():
            acc = acc_ref[...]
            if res is not None:
                acc = acc + rest[0][...]
            o_ref[...] = acc.astype(out_dtype)

    if mode == "tn":
        a_spec = pl.BlockSpec((k, tm), lambda i, j, kk: (0, i))
    else:
        a_spec = pl.BlockSpec((tm, tk), lambda i, j, kk: (i, kk))
    if mode == "nt":
        b_spec = pl.BlockSpec((tn, tk), lambda i, j, kk: (j, kk))
    else:
        b_spec = pl.BlockSpec((tk, tn), lambda i, j, kk: (kk, j))
    o_spec = pl.BlockSpec((tm, tn), lambda i, j, kk: (i, j))
    ins, specs = [a, b], [a_spec, b_spec]
    if res is not None:
        ins.append(res)
        specs.append(o_spec)
    scratch = [pltpu.VMEM((tm, tn), f32)] if nk > 1 else []
    return _pc(body, name=name, grid=(m // tm, n // tn, nk), in_specs=specs, out_specs=o_spec,
               out_shape=S((m, n), out_dtype), scratch_shapes=scratch,
               compiler_params=_cparams(("arbitrary", "arbitrary", "arbitrary")))(*ins)


def _rms_fwd(x, g, name):
    t, d = x.shape
    rb = _row_block(t)

    def body(x_ref, g_ref, o_ref):
        xv = x_ref[...]
        rstd = lax.rsqrt(jnp.mean(xv * xv, axis=-1, keepdims=True) + RMS_EPS)
        o_ref[...] = (xv * rstd * g_ref[...]).astype(bf16)

    row = pl.BlockSpec((rb, d), lambda i: (i, 0))
    return _pc(body, name=name, grid=(t // rb,), in_specs=[row, _full((1, d))], out_specs=row,
               out_shape=S((t, d), bf16), compiler_params=_cparams(("arbitrary",)))(x, g.reshape(1, d))


def _rms_bwd(dy, x, g, dres, name):
    t, d = x.shape
    rb = _row_block8(t)

    def body(dy_ref, x_ref, g_ref, dres_ref, dx_ref, dg_ref):
        @pl.when(pl.program_id(0) == 0)
        def _():
            dg_ref[...] = jnp.zeros_like(dg_ref)
        xv, dyv = x_ref[...], dy_ref[...]
        rstd = lax.rsqrt(jnp.mean(xv * xv, axis=-1, keepdims=True) + RMS_EPS)
        xn = xv * rstd
        dg_ref[...] += jnp.sum(dyv * xn, axis=0, keepdims=True)
        dxh = dyv * g_ref[...]
        dx_ref[...] = dres_ref[...] + rstd * (dxh - xn * jnp.mean(dxh * xn, axis=-1, keepdims=True))

    row = pl.BlockSpec((rb, d), lambda i: (i, 0))
    return _pc(body, name=name, grid=(t // rb,), in_specs=[row, row, _full((1, d)), row],
               out_specs=(row, _full((1, d))), out_shape=(S((t, d), f32), S((1, d), f32)),
               compiler_params=_cparams(("arbitrary",)))(dy, x, g.reshape(1, d), dres)


def _final_loss(h, g, target_padded):
    t, d = h.shape
    rb = _row_block8(t)

    def body(x_ref, g_ref, t_ref, loss_ref, dx_ref, dg_ref):
        i = pl.program_id(0)

        @pl.when(i == 0)
        def _():
            dg_ref[...] = jnp.zeros_like(dg_ref)
            loss_ref[...] = jnp.zeros_like(loss_ref)
        xv = x_ref[...]
        rstd = lax.rsqrt(jnp.mean(xv * xv, axis=-1, keepdims=True) + RMS_EPS)
        xn = xv * rstd
        gv = g_ref[...]
        row = i * rb + lax.broadcasted_iota(jnp.int32, (rb, 1), 0)
        diff = jnp.where(row >= N_META, xn * gv - t_ref[...], 0.0)
        loss_ref[...] += 0.5 * jnp.sum(jnp.mean(diff * diff, axis=-1, keepdims=True))
        dout = diff * (1.0 / d)
        dg_ref[...] += jnp.sum(dout * xn, axis=0, keepdims=True)
        dxh = dout * gv
        dx_ref[...] = rstd * (dxh - xn * jnp.mean(dxh * xn, axis=-1, keepdims=True))

    row = pl.BlockSpec((rb, d), lambda i: (i, 0))
    return _pc(body, name="final_loss", grid=(t // rb,), in_specs=[row, _full((1, d)), row],
               out_specs=(_full((SUBLANES, LANES)), row, _full((1, d))),
               out_shape=(S((SUBLANES, LANES), f32), S((t, d), f32), S((1, d), f32)),
               compiler_params=_cparams(("arbitrary",)))(h, g.reshape(1, d), target_padded)


CONV_LEAD = 32


def _fill_front_padded(pad_ref, x, t):
    pad_ref[0:CONV_LEAD, :] = jnp.zeros((CONV_LEAD, x.shape[1]), f32)
    pad_ref[CONV_LEAD:CONV_LEAD + t, :] = x


def _fill_back_padded(pad_ref, x, t):
    pad_ref[0:t, :] = x
    pad_ref[t:t + CONV_LEAD, :] = jnp.zeros((CONV_LEAD, x.shape[1]), f32)


def _conv_rows(pad_ref, w_ref, kw, r0, nr):
    acc = None
    for j in range(kw):
        lo = CONV_LEAD + r0 - (kw - 1) + j
        term = w_ref[j:j + 1, :] * pad_ref[lo:lo + nr, :]
        acc = term if acc is None else acc + term
    return acc


def _conv_t_rows(padb_ref, w_ref, kw, r0, nr):
    acc = None
    for j in range(kw):
        lo = r0 + (kw - 1) - j
        term = w_ref[j:j + 1, :] * padb_ref[lo:lo + nr, :]
        acc = term if acc is None else acc + term
    return acc


def _conv_dw_rows(dy_blk, pad_ref, kw, r0, nr):
    out = []
    for j in range(kw):
        lo = CONV_LEAD + r0 - (kw - 1) + j
        out.append(jnp.sum(dy_blk * pad_ref[lo:lo + nr, :], axis=0, keepdims=True))
    return out


def _acc_list(a, b):
    return b if a is None else [x + y for x, y in zip(a, b)]


def _ev_a_conv(p, conv_a):
    t = p.shape[0]
    cr = _row_block8(t)
    nb = D_A // LANES

    def body(av_ref, ag_ref, w_ref, o_ref, pad_ref):
        _fill_front_padded(pad_ref, av_ref[...] * _sigmoid(ag_ref[...]), t)
        for r in range(t // cr):
            o_ref[r * cr:(r + 1) * cr, :] = _conv_rows(pad_ref, w_ref, CONV_A_WIDTH, r * cr, cr)

    col = lambda off: pl.BlockSpec((t, LANES), lambda j: (0, j + off))
    return _pc(body, name="ev_a_conv", grid=(nb,),
               in_specs=[col(0), col(nb), pl.BlockSpec((CONV_A_WIDTH, LANES), lambda j: (0, j))],
               out_specs=col(0), out_shape=S((t, D_A), f32),
               scratch_shapes=[pltpu.VMEM((t + CONV_LEAD, LANES), f32)],
               compiler_params=_cparams(("arbitrary",)))(p, p, conv_a)


def _ln_silu(uc, g, b):
    mu = jnp.mean(uc, axis=-1, keepdims=True)
    xc = uc - mu
    var = jnp.mean(xc * xc, axis=-1, keepdims=True)
    y = xc * lax.rsqrt(var + LN_EPS) * g + b
    return y * _sigmoid(y)


def _ev_a_norm(uc, g, b):
    t, d = uc.shape
    rb = _row_block(t)

    def body(u_ref, g_ref, b_ref, o_ref):
        o_ref[...] = _ln_silu(u_ref[...], g_ref[...], b_ref[...]).astype(bf16)

    row = pl.BlockSpec((rb, d), lambda i: (i, 0))
    return _pc(body, name="ev_a_norm", grid=(t // rb,), in_specs=[row, _full((1, d)), _full((1, d))],
               out_specs=row, out_shape=S((t, d), bf16), compiler_params=_cparams(("arbitrary",)))(uc, g, b)


def _ev_a_norm_bwd(dy, uc, g, b):
    t, d = uc.shape
    rb = _row_block8(t)

    def body(dy_ref, u_ref, g_ref, b_ref, du_ref, dg_ref, db_ref):
        @pl.when(pl.program_id(0) == 0)
        def _():
            dg_ref[...] = jnp.zeros_like(dg_ref)
            db_ref[...] = jnp.zeros_like(db_ref)
        _, vjp = jax.vjp(_ln_silu, u_ref[...], g_ref[...], b_ref[...])
        du, dg, db = vjp(dy_ref[...])
        du_ref[...] = du
        dg_ref[...] += dg
        db_ref[...] += db

    row = pl.BlockSpec((rb, d), lambda i: (i, 0))
    return _pc(body, name="ev_a_norm_bwd", grid=(t // rb,), in_specs=[row, row, _full((1, d)), _full((1, d))],
               out_specs=(row, _full((1, d)), _full((1, d))),
               out_shape=(S((t, d), f32), S((1, d), f32), S((1, d), f32)),
               compiler_params=_cparams(("arbitrary",)))(dy, uc, g, b)


def _ev_a_conv_bwd(duc, p, conv_a):
    t = p.shape[0]
    cr = _row_block8(t)
    nb = D_A // LANES

    def body(dy_ref, av_ref, ag_ref, w_ref, dav_ref, dag_ref, dw_ref, pad_ref, padb_ref):
        _fill_front_padded(pad_ref, av_ref[...] * _sigmoid(ag_ref[...]), t)
        _fill_back_padded(padb_ref, dy_ref[...], t)
        dw = None
        for r in range(t // cr):
            rows = slice(r * cr, (r + 1) * cr)
            du = _conv_t_rows(padb_ref, w_ref, CONV_A_WIDTH, r * cr, cr)
            avr = av_ref[rows, :]
            sgr = _sigmoid(ag_ref[rows, :])
            dav_ref[rows, :] = du * sgr
            dag_ref[rows, :] = du * avr * sgr * (1.0 - sgr)
            dw = _acc_list(dw, _conv_dw_rows(dy_ref[rows, :], pad_ref, CONV_A_WIDTH, r * cr, cr))
        for j in range(CONV_A_WIDTH):
            dw_ref[j:j + 1, :] = dw[j]

    col = lambda off: pl.BlockSpec((t, LANES), lambda j: (0, j + off))
    wsp = pl.BlockSpec((CONV_A_WIDTH, LANES), lambda j: (0, j))
    return _pc(body, name="ev_a_conv_bwd", grid=(nb,), in_specs=[col(0), col(0), col(nb), wsp],
               out_specs=(col(0), col(0), wsp),
               out_shape=(S((t, D_A), f32), S((t, D_A), f32), S((CONV_A_WIDTH, D_A), f32)),
               scratch_shapes=[pltpu.VMEM((t + CONV_LEAD, LANES), f32), pltpu.VMEM((t + CONV_LEAD, LANES), f32)],
               compiler_params=_cparams(("arbitrary",)))(duc, p, p, conv_a)


def _ev_b(p, conv_b):
    t = p.shape[0]
    cr = _row_block8(t)
    nb = D_A // LANES

    def body(gb_ref, gc_ref, xi_ref, w_ref, o_ref, pad_ref, stage_ref):
        _fill_front_padded(pad_ref, gc_ref[...] * xi_ref[...], t)
        for r in range(t // cr):
            rows = slice(r * cr, (r + 1) * cr)
            stage_ref[rows, :] = gb_ref[rows, :] * _conv_rows(pad_ref, w_ref, CONV_B_WIDTH, r * cr, cr)
        o_ref[...] = stage_ref[...].astype(bf16)

    col = lambda off: pl.BlockSpec((t, LANES), lambda j: (0, j + off))
    return _pc(body, name="ev_b", grid=(nb,),
               in_specs=[col(2 * nb), col(3 * nb), col(4 * nb), pl.BlockSpec((CONV_B_WIDTH, LANES), lambda j: (0, j))],
               out_specs=col(0), out_shape=S((t, D_A), bf16),
               scratch_shapes=[pltpu.VMEM((t + CONV_LEAD, LANES), f32), pltpu.VMEM((t, LANES), f32)],
               compiler_params=_cparams(("arbitrary",)))(p, p, p, conv_b)


def _ev_b_bwd(dy, p, conv_b):
    t = p.shape[0]
    cr = _row_block8(t)
    nb = D_A // LANES

    def body(dy_ref, gb_ref, gc_ref, xi_ref, w_ref, dgb_ref, dgc_ref, dxi_ref, dw_ref, pad_ref, padb_ref):
        _fill_front_padded(pad_ref, gc_ref[...] * xi_ref[...], t)
        _fill_back_padded(padb_ref, dy_ref[...] * gb_ref[...], t)
        dw = None
        for r in range(t // cr):
            rows = slice(r * cr, (r + 1) * cr)
            dgb_ref[rows, :] = dy_ref[rows, :] * _conv_rows(pad_ref, w_ref, CONV_B_WIDTH, r * cr, cr)
            dcx = _conv_t_rows(padb_ref, w_ref, CONV_B_WIDTH, r * cr, cr)
            dgc_ref[rows, :] = dcx * xi_ref[rows, :]
            dxi_ref[rows, :] = dcx * gc_ref[rows, :]
            dw = _acc_list(dw, _conv_dw_rows(padb_ref[rows, :], pad_ref, CONV_B_WIDTH, r * cr, cr))
        for j in range(CONV_B_WIDTH):
            dw_ref[j:j + 1, :] = dw[j]

    col = lambda off: pl.BlockSpec((t, LANES), lambda j: (0, j + off))
    wsp = pl.BlockSpec((CONV_B_WIDTH, LANES), lambda j: (0, j))
    return _pc(body, name="ev_b_bwd", grid=(nb,), in_specs=[col(nb), col(2 * nb), col(3 * nb), col(4 * nb), wsp],
               out_specs=(col(0), col(0), col(0), wsp),
               out_shape=(S((t, D_A), f32), S((t, D_A), f32), S((t, D_A), f32), S((CONV_B_WIDTH, D_A), f32)),
               scratch_shapes=[pltpu.VMEM((t + CONV_LEAD, LANES), f32), pltpu.VMEM((t + CONV_LEAD, LANES), f32)],
               compiler_params=_cparams(("arbitrary",)))(dy, p, p, p, conv_b)


def _ffn_mid(u, conv_w, conv_b, name):
    t = u.shape[0]
    cr = _row_block8(t)
    nb = D_FF // LANES

    def body(gt_ref, vl_ref, w_ref, b_ref, o_ref, pad_ref, stage_ref):
        _fill_front_padded(pad_ref, gt_ref[...], t)
        for r in range(t // cr):
            rows = slice(r * cr, (r + 1) * cr)
            gc = _conv_rows(pad_ref, w_ref, FF_CONV_WIDTH, r * cr, cr) + b_ref[...]
            stage_ref[rows, :] = gc * _sigmoid(gc) * vl_ref[rows, :]
        o_ref[...] = stage_ref[...].astype(bf16)

    col = lambda off: pl.BlockSpec((t, LANES), lambda j: (0, j + off))
    return _pc(body, name=name, grid=(nb,),
               in_specs=[col(0), col(nb), pl.BlockSpec((FF_CONV_WIDTH, LANES), lambda j: (0, j)),
                         pl.BlockSpec((1, LANES), lambda j: (0, j))],
               out_specs=col(0), out_shape=S((t, D_FF), bf16),
               scratch_shapes=[pltpu.VMEM((t + CONV_LEAD, LANES), f32), pltpu.VMEM((t, LANES), f32)],
               compiler_params=_cparams(("arbitrary",)))(u, u, conv_w, conv_b.reshape(1, D_FF))


def _ffn_mid_bwd(dz, u, conv_w, conv_b, name):
    t = u.shape[0]
    cr = _row_block8(t)
    nb = D_FF // LANES

    def body(dz_ref, gt_ref, vl_ref, w_ref, b_ref, du_ref, dw_ref, db_ref, pad_ref, padb_ref):
        s = pl.program_id(1)
        _fill_front_padded(pad_ref, gt_ref[...], t)

        @pl.when(s == 0)
        def _():
            for r in range(t // cr):
                rows = slice(r * cr, (r + 1) * cr)
                gc = _conv_rows(pad_ref, w_ref, FF_CONV_WIDTH, r * cr, cr) + b_ref[...]
                sg = _sigmoid(gc)
                padb_ref[rows, :] = dz_ref[rows, :] * vl_ref[rows, :] * sg * (1.0 + gc * (1.0 - sg))
            padb_ref[t:t + CONV_LEAD, :] = jnp.zeros((CONV_LEAD, LANES), f32)
            dw, db = None, None
            for r in range(t // cr):
                rows = slice(r * cr, (r + 1) * cr)
                du_ref[rows, :] = _conv_t_rows(padb_ref, w_ref, FF_CONV_WIDTH, r * cr, cr)
                dgc = padb_ref[rows, :]
                dw = _acc_list(dw, _conv_dw_rows(dgc, pad_ref, FF_CONV_WIDTH, r * cr, cr))
                pb = jnp.sum(dgc, axis=0, keepdims=True)
                db = pb if db is None else db + pb
            for j in range(FF_CONV_WIDTH):
                dw_ref[j:j + 1, :] = dw[j]
            db_ref[...] = db

        @pl.when(s == 1)
        def _():
            for r in range(t // cr):
                rows = slice(r * cr, (r + 1) * cr)
                gc = _conv_rows(pad_ref, w_ref, FF_CONV_WIDTH, r * cr, cr) + b_ref[...]
                du_ref[rows, :] = dz_ref[rows, :] * gc * _sigmoid(gc)

    col = lambda off: pl.BlockSpec((t, LANES), lambda j, s: (0, j + off))
    wsp = pl.BlockSpec((FF_CONV_WIDTH, LANES), lambda j, s: (0, j))
    bsp = pl.BlockSpec((1, LANES), lambda j, s: (0, j))
    return _pc(body, name=name, grid=(nb, 2), in_specs=[col(0), col(0), col(nb), wsp, bsp],
               out_specs=(pl.BlockSpec((t, LANES), lambda j, s: (0, s * nb + j)), wsp, bsp),
               out_shape=(S((t, 2 * D_FF), f32), S((FF_CONV_WIDTH, D_FF), f32), S((1, D_FF), f32)),
               scratch_shapes=[pltpu.VMEM((t + CONV_LEAD, LANES), f32), pltpu.VMEM((t + CONV_LEAD, LANES), f32)],
               compiler_params=_cparams(("arbitrary", "arbitrary")))(dz, u, u, conv_w, conv_b.reshape(1, D_FF))


def _swap_halves(x):
    w = x.shape[1]
    lane = lax.broadcasted_iota(jnp.int32, x.shape, 1) % HEAD_DIM
    return jnp.where(lane < HEAD_DIM // 2, pltpu.roll(x, w - HEAD_DIM // 2, axis=1), pltpu.roll(x, HEAD_DIM // 2, axis=1))


def _rope_pack(patt, c64, s64):
    t = patt.shape[0]
    tp = t + ATT_PAD

    def body(p_ref, c_ref, s_ref, q_ref, k_ref, v_ref):
        c, s = c_ref[...], s_ref[...]

        def rope(x, nh):
            cc = jnp.concatenate([c] * nh, axis=1)
            ss = jnp.concatenate([s] * nh, axis=1)
            return x * cc + _swap_halves(x) * ss

        for ref, val in ((q_ref, rope(p_ref[:, 0:D_ATT], N_Q_HEADS)),
                         (k_ref, rope(p_ref[:, D_ATT:D_ATT + D_KV], N_KV_HEADS)),
                         (v_ref, p_ref[:, D_ATT + D_KV:ATT_COLS])):
            ref[0:ATT_PAD, :] = jnp.zeros((ATT_PAD, val.shape[1]), bf16)
            ref[ATT_PAD:tp, :] = val.astype(bf16)

    return _pc(body, name="rope_pack", in_specs=[_full((t, ATT_COLS)), _full((t, HEAD_DIM)), _full((t, HEAD_DIM))],
               out_specs=(_full((tp, D_ATT)), _full((tp, D_KV)), _full((tp, D_KV))), grid=(1,),
               out_shape=(S((tp, D_ATT), bf16), S((tp, D_KV), bf16), S((tp, D_KV), bf16)),
               compiler_params=_cparams(("arbitrary",)))(patt, c64, s64)


def _rope_bwd(dqp, dkp, dvp, c64, s64):
    tp = dqp.shape[0]
    t = tp - ATT_PAD

    def body(dq_ref, dk_ref, dv_ref, c_ref, s_ref, o_ref):
        c, s = c_ref[...], s_ref[...]

        def unrope(dy, nh):
            cc = jnp.concatenate([c] * nh, axis=1)
            ss = jnp.concatenate([s] * nh, axis=1)
            return dy * cc + _swap_halves(dy * ss)

        o_ref[:, 0:D_ATT] = unrope(dq_ref[ATT_PAD:tp, :], N_Q_HEADS)
        o_ref[:, D_ATT:D_ATT + D_KV] = unrope(dk_ref[ATT_PAD:tp, :], N_KV_HEADS)
        o_ref[:, D_ATT + D_KV:ATT_COLS] = dv_ref[ATT_PAD:tp, :]

    return _pc(body, name="rope_bwd", grid=(1,),
               in_specs=[_full((tp, D_ATT)), _full((tp, D_KV)), _full((tp, D_KV)), _full((t, HEAD_DIM)), _full((t, HEAD_DIM))],
               out_specs=_full((t, ATT_COLS)), out_shape=S((t, ATT_COLS), f32),
               compiler_params=_cparams(("arbitrary",)))(dqp, dkp, dvp, c64, s64)


def _attn_masks(n):
    rows = GQA_GROUP * BLOCK
    ri = lax.broadcasted_iota(jnp.int32, (rows, BLOCK), 0) % BLOCK
    ci = lax.broadcasted_iota(jnp.int32, (rows, BLOCK), 1)
    m_cur = (ci <= ri) & (ci >= jnp.where(n >= 1, 0, ATT_PAD))
    m_prev = ci > ri + jnp.where(n >= 2, 0, BLOCK)
    m_meta = ci >= jnp.where(n >= 1, ATT_PAD, BLOCK)
    return m_cur, m_prev, m_meta


def _attn_probs(qg, kc, kp, km, masks, skv):
    def scores(k, m):
        s = lax.dot_general(qg, k, _DIMS["nt"], preferred_element_type=f32) * ATT_SCALE
        return jnp.where(m, s, NEG_INF)
    s_c, s_p, s_m = scores(kc, masks[0]), scores(kp, masks[1]), scores(km, masks[2])
    mx = jnp.maximum(jnp.maximum(jnp.max(s_c, axis=-1, keepdims=True), jnp.max(s_p, axis=-1, keepdims=True)),
                     jnp.maximum(jnp.max(s_m, axis=-1, keepdims=True), skv))
    e_c, e_p, e_m, e_s = jnp.exp(s_c - mx), jnp.exp(s_p - mx), jnp.exp(s_m - mx), jnp.exp(skv - mx)
    den = (jnp.sum(e_c, axis=-1, keepdims=True) + jnp.sum(e_p, axis=-1, keepdims=True)
           + jnp.sum(e_m, axis=-1, keepdims=True) + e_s)
    inv = 1.0 / den
    return e_c * inv, e_p * inv, e_m * inv, e_s * inv


def _sink_rows(sk_ref, g):
    hrow = lax.broadcasted_iota(jnp.int32, (GQA_GROUP * BLOCK, 1), 0) // BLOCK
    skv = jnp.zeros((GQA_GROUP * BLOCK, 1), f32)
    for hh in range(GQA_GROUP):
        skv = jnp.where(hrow == hh, sk_ref[0, GQA_GROUP * g + hh], skv)
    return skv, hrow


def _stack_heads(ref, g):
    return jnp.concatenate([ref[:, (GQA_GROUP * g + hh) * HEAD_DIM:(GQA_GROUP * g + hh + 1) * HEAD_DIM]
                            for hh in range(GQA_GROUP)], axis=0)


def _attn_specs():
    blk = lambda w: pl.BlockSpec((BLOCK, w), lambda n: (n, 0))
    prev = pl.BlockSpec((BLOCK, D_KV), lambda n: (jnp.maximum(n - 1, 0), 0))
    meta = pl.BlockSpec((BLOCK, D_KV), lambda n: (0, 0))
    return blk, prev, meta


def _attn_fwd(qp, kp, vp, sinks):
    tp = qp.shape[0]
    blk, prev, meta = _attn_specs()

    def body(sk_ref, q_ref, kc_ref, kp_ref, km_ref, vc_ref, vp_ref, vm_ref, o_ref):
        masks = _attn_masks(pl.program_id(0))
        for g in range(N_KV_HEADS):
            sl = slice(g * HEAD_DIM, (g + 1) * HEAD_DIM)
            skv, _ = _sink_rows(sk_ref, g)
            p_c, p_p, p_m, _ = _attn_probs(_stack_heads(q_ref, g), kc_ref[:, sl], kp_ref[:, sl], km_ref[:, sl], masks, skv)
            o = (jnp.dot(p_c.astype(bf16), vc_ref[:, sl], preferred_element_type=f32)
                 + jnp.dot(p_p.astype(bf16), vp_ref[:, sl], preferred_element_type=f32)
                 + jnp.dot(p_m.astype(bf16), vm_ref[:, sl], preferred_element_type=f32))
            for hh in range(GQA_GROUP):
                h = GQA_GROUP * g + hh
                o_ref[:, h * HEAD_DIM:(h + 1) * HEAD_DIM] = o[hh * BLOCK:(hh + 1) * BLOCK].astype(bf16)

    return _pc(body, name="attn_fwd", grid=(tp // BLOCK,),
               in_specs=[pl.BlockSpec(memory_space=pltpu.SMEM), blk(D_ATT), blk(D_KV), prev, meta, blk(D_KV), prev, meta],
               out_specs=blk(D_ATT), out_shape=S((tp, D_ATT), bf16),
               compiler_params=_cparams(("arbitrary",)))(sinks, qp, kp, kp, kp, vp, vp, vp)


def _attn_bwd(qp, kp, vp, sinks, dop):
    tp = qp.shape[0]
    blk, prev, meta = _attn_specs()

    def body(sk_ref, q_ref, kc_ref, kp_ref, km_ref, vc_ref, vp_ref, vm_ref, do_ref, dq_ref, dk_ref, dv_ref, dsk_ref):
        n = pl.program_id(0)

        @pl.when(n == 0)
        def _():
            dk_ref[...] = jnp.zeros_like(dk_ref)
            dv_ref[...] = jnp.zeros_like(dv_ref)
            dsk_ref[...] = jnp.zeros_like(dsk_ref)
        masks = _attn_masks(n)
        cur = pl.ds(pl.multiple_of(n * BLOCK, BLOCK), BLOCK)
        prv = pl.ds(pl.multiple_of(jnp.maximum(n - 1, 0) * BLOCK, BLOCK), BLOCK)
        lane = lax.broadcasted_iota(jnp.int32, (1, LANES), 1)
        dsk = jnp.zeros((1, LANES), f32)
        for g in range(N_KV_HEADS):
            sl = slice(g * HEAD_DIM, (g + 1) * HEAD_DIM)
            skv, hrow = _sink_rows(sk_ref, g)
            qg = _stack_heads(q_ref, g)
            dog = _stack_heads(do_ref, g)
            ks = (kc_ref[:, sl], kp_ref[:, sl], km_ref[:, sl])
            vs = (vc_ref[:, sl], vp_ref[:, sl], vm_ref[:, sl])
            probs = _attn_probs(qg, ks[0], ks[1], ks[2], masks, skv)
            dps = [lax.dot_general(dog, v, _DIMS["nt"], preferred_element_type=f32) for v in vs]
            delta = sum(jnp.sum(p * dp, axis=-1, keepdims=True) for p, dp in zip(probs[:3], dps))
            dss = [(p * (dp - delta) * ATT_SCALE).astype(bf16) for p, dp in zip(probs[:3], dps)]
            dq = sum(jnp.dot(ds, k, preferred_element_type=f32) for ds, k in zip(dss, ks))
            for hh in range(GQA_GROUP):
                h = GQA_GROUP * g + hh
                dq_ref[:, h * HEAD_DIM:(h + 1) * HEAD_DIM] = dq[hh * BLOCK:(hh + 1) * BLOCK]
                dsk = dsk + jnp.where(lane == h, -jnp.sum(jnp.where(hrow == hh, probs[3] * delta, 0.0)), 0.0)
            for rows, p, ds in zip((cur, prv, slice(0, BLOCK)), probs[:3], dss):
                dv_ref[rows, sl] += lax.dot_general(p.astype(bf16), dog, _DIMS["tn"], preferred_element_type=f32)
                dk_ref[rows, sl] += lax.dot_general(ds, qg, _DIMS["tn"], preferred_element_type=f32)
        dsk_ref[...] += dsk

    return _pc(body, name="attn_bwd", grid=(tp // BLOCK,),
               in_specs=[pl.BlockSpec(memory_space=pltpu.SMEM), blk(D_ATT), blk(D_KV), prev, meta, blk(D_KV), prev, meta,
                         blk(D_ATT)],
               out_specs=(blk(D_ATT), _full((tp, D_KV)), _full((tp, D_KV)), _full((1, LANES))),
               out_shape=(S((tp, D_ATT), f32), S((tp, D_KV), f32), S((tp, D_KV), f32), S((1, LANES), f32)),
               compiler_params=_cparams(("arbitrary",)))(sinks, qp, kp, kp, kp, vp, vp, vp, dop)


def _seg(x, bm):
    return jnp.dot(x, bm, precision=HIGHEST, preferred_element_type=f32)


def _softplus(y):
    return jnp.maximum(y, 0.0) + jnp.log(1.0 + jnp.exp(-jnp.abs(y)))


def _prep_fn(xr, xk, xwd, xad, xgd, w0, w2, a0, a2, g2, k_k, k_a, bm):
    xw = w0 + jnp.dot(jnp.tanh(xwd), w2, preferred_element_type=f32)
    decay = jnp.exp(-jnp.exp(-_softplus(-xw) - 0.5))
    alpha = _sigmoid(a0 + jnp.dot(xad, a2, preferred_element_type=f32))
    g = jnp.dot(_sigmoid(xgd), g2, preferred_element_type=f32)
    kk = xk * k_k
    kkn = kk / jnp.maximum(jnp.sqrt(_seg(kk * kk, bm)), 1e-12)
    k2 = xk * (1.0 + (alpha - 1.0) * k_a)
    return decay, k2, -kkn, kkn * alpha, g


def _split_cols(x):
    o1, o2, o3 = 3 * D_R, 3 * D_R + LORA_W, 3 * D_R + LORA_W + LORA_A
    return x[:, 0:D_R], x[:, D_R:2 * D_R], x[:, 2 * D_R:o1], x[:, o1:o2], x[:, o2:o3], x[:, o3:RWKV_COLS]


def _shifted(sh_ref, x, halo, first, rb):
    sh_ref[0:SUBLANES, :] = jnp.where(first, 0.0, halo)
    sh_ref[SUBLANES:SUBLANES + rb, :] = x
    return sh_ref[SUBLANES - 1:SUBLANES - 1 + rb, :]


_PREP_PARAMS = ("od_w0", "od_w2", "od_a0", "od_a2", "od_g2", "od_k_k", "od_k_a")


def _rwkv_prep(pr, mu, params, bm):
    t = pr.shape[0]
    rb = _row_block8(t)
    hb = rb // SUBLANES

    def body(pr_ref, halo_ref, mu_ref, w0, w2, a0, a2, g2, kk_ref, ka_ref, bm_ref, *outs_sh):
        outs, sh_ref = outs_sh[:-1], outs_sh[-1]
        x = pr_ref[...]
        prev = _shifted(sh_ref, x, halo_ref[...], pl.program_id(0) == 0, rb)
        xr, xk, xv, xwd, xad, xgd = _split_cols(x + (prev - x) * mu_ref[...])
        bmv = bm_ref[...]
        decay, k2, a_s, b_s, g = _prep_fn(xr, xk, xwd, xad, xgd, w0[...], w2[...], a0[...], a2[...], g2[...],
                                          kk_ref[...], ka_ref[...], bmv)
        vals = (xr, xv, decay, k2, a_s, b_s, decay * xr, _seg(b_s * xr, bmv), _seg(k2 * xr, bmv), g)
        for ref, val in zip(outs, vals):
            ref[...] = val

    row = pl.BlockSpec((rb, RWKV_COLS), lambda i: (i, 0))
    halo = pl.BlockSpec((SUBLANES, RWKV_COLS), lambda i: (jnp.maximum(i * hb - 1, 0), 0))
    orow = pl.BlockSpec((rb, D_R), lambda i: (i, 0))
    return _pc(body, name="rwkv_prep", grid=(t // rb,),
               in_specs=[row, halo, _full((1, RWKV_COLS))] + [_full(p.shape) for p in params] + [_full(bm.shape)],
               out_specs=(orow,) * 10, out_shape=(S((t, D_R), f32),) * 10,
               scratch_shapes=[pltpu.VMEM((rb + SUBLANES, RWKV_COLS), f32)],
               compiler_params=_cparams(("arbitrary",)))(pr, pr, mu, *params, bm)


def _rwkv_prep_bwd(pr, mu, params, bm, cts):
    t = pr.shape[0]
    rb = _row_block8(t)
    hb = rb // SUBLANES
    counts = [len(c) for c in cts]
    flat = [a for c in cts for a in c]

    def body(pr_ref, halo_ref, mu_ref, w0, w2, a0, a2, g2, kk_ref, ka_ref, bm_ref, *rest):
        ct_refs, rest = rest[:len(flat)], rest[len(flat):]
        dx_ref, dmu_ref = rest[0], rest[1]
        dpar_refs, sh_ref = rest[2:9], rest[9]

        @pl.when(pl.program_id(0) == 0)
        def _():
            dmu_ref[...] = jnp.zeros_like(dmu_ref)
            for r in dpar_refs:
                r[...] = jnp.zeros_like(r)
        sums, pos = [], 0
        for c in counts:
            sums.append(sum(r[...] for r in ct_refs[pos:pos + c]))
            pos += c
        x = pr_ref[...]
        prev = _shifted(sh_ref, x, halo_ref[...], pl.program_id(0) == 0, rb)
        xr, xk, xv, xwd, xad, xgd = _split_cols(x + (prev - x) * mu_ref[...])
        bmv = bm_ref[...]
        _, vjp = jax.vjp(lambda *a: _prep_fn(*a, bmv), xr, xk, xwd, xad, xgd, w0[...], w2[...], a0[...], a2[...],
                         g2[...], kk_ref[...], ka_ref[...])
        grads = vjp(tuple(sums[:5]))
        dxr, dxk, dxwd, dxad, dxgd = grads[:5]
        o1, o2, o3 = 3 * D_R, 3 * D_R + LORA_W, 3 * D_R + LORA_W + LORA_A
        dx_ref[:, 0:D_R] = dxr + sums[5]
        dx_ref[:, D_R:2 * D_R] = dxk
        dx_ref[:, 2 * D_R:o1] = sums[6]
        dx_ref[:, o1:o2] = dxwd
        dx_ref[:, o2:o3] = dxad
        dx_ref[:, o3:RWKV_COLS] = dxgd
        dmu_ref[...] += jnp.sum(dx_ref[...] * (prev - x), axis=0, keepdims=True)
        for r, gval in zip(dpar_refs, grads[5:]):
            r[...] += gval

    row = pl.BlockSpec((rb, RWKV_COLS), lambda i: (i, 0))
    halo = pl.BlockSpec((SUBLANES, RWKV_COLS), lambda i: (jnp.maximum(i * hb - 1, 0), 0))
    crow = pl.BlockSpec((rb, D_R), lambda i: (i, 0))
    return _pc(body, name="rwkv_prep_bwd", grid=(t // rb,),
               in_specs=[row, halo, _full((1, RWKV_COLS))] + [_full(p.shape) for p in params] + [_full(bm.shape)]
               + [crow] * len(flat),
               out_specs=(row, _full((1, RWKV_COLS))) + tuple(_full(p.shape) for p in params),
               out_shape=(S((t, RWKV_COLS), f32), S((1, RWKV_COLS), f32)) + tuple(S(p.shape, f32) for p in params),
               scratch_shapes=[pltpu.VMEM((rb + SUBLANES, RWKV_COLS), f32)],
               compiler_params=_cparams(("arbitrary",)))(pr, pr, mu, *params, bm, *flat)


def _shift_bwd(dxs, mu):
    t = dxs.shape[0]
    rb = _row_block8(t)
    hb = rb // SUBLANES
    nblk = t // rb

    def body(dx_ref, halo_ref, mu_ref, o_ref, sh_ref):
        dx = dx_ref[...]
        sh_ref[0:rb, :] = dx
        sh_ref[rb:rb + SUBLANES, :] = jnp.where(pl.program_id(0) == nblk - 1, 0.0, halo_ref[...])
        m = mu_ref[...]
        o_ref[...] = dx * (1.0 - m) + sh_ref[1:1 + rb, :] * m

    row = pl.BlockSpec((rb, RWKV_COLS), lambda i: (i, 0))
    halo = pl.BlockSpec((SUBLANES, RWKV_COLS), lambda i: (jnp.minimum((i + 1) * hb, t // SUBLANES - 1), 0))
    return _pc(body, name="rwkv_shift_bwd", grid=(nblk,), in_specs=[row, halo, _full((1, RWKV_COLS))],
               out_specs=row, out_shape=S((t, RWKV_COLS), f32),
               scratch_shapes=[pltpu.VMEM((rb + SUBLANES, RWKV_COLS), f32)],
               compiler_params=_cparams(("arbitrary",)))(dxs, dxs, mu)


def _post_fn(y, xr, k2, xv, g, lg, lb, rk, bm):
    inv_n = 1.0 / HEAD_DIM
    yc = y - _seg(y, bm) * inv_n
    var = _seg(yc * yc, bm) * inv_n
    yn = yc * lax.rsqrt(var + RWKV_GN_EPS) * lg + lb
    return (yn + _seg(xr * k2 * rk, bm) * xv) * g


def _rwkv_post(y, xr, k2, xv, g, lg, lb, rk, bm):
    t = y.shape[0]
    rb = _row_block8(t)

    def body(y_ref, xr_ref, k2_ref, xv_ref, g_ref, lg_ref, lb_ref, rk_ref, bm_ref, o_ref):
        o_ref[...] = _post_fn(y_ref[...], xr_ref[...], k2_ref[...], xv_ref[...], g_ref[...], lg_ref[...], lb_ref[...],
                              rk_ref[...], bm_ref[...])

    row = pl.BlockSpec((rb, D_R), lambda i: (i, 0))
    vec = _full((1, D_R))
    return _pc(body, name="rwkv_post", grid=(t // rb,), in_specs=[row] * 5 + [vec] * 3 + [_full(bm.shape)],
               out_specs=row, out_shape=S((t, D_R), f32),
               compiler_params=_cparams(("arbitrary",)))(y, xr, k2, xv, g, lg, lb, rk, bm)


def _rwkv_post_bwd(dy1, y, xr, k2, xv, g, lg, lb, rk, bm):
    t = y.shape[0]
    rb = _row_block8(t)

    def body(dy_ref, y_ref, xr_ref, k2_ref, xv_ref, g_ref, lg_ref, lb_ref, rk_ref, bm_ref, *outs):
        @pl.when(pl.program_id(0) == 0)
        def _():
            for r in outs[5:]:
                r[...] = jnp.zeros_like(r)
        bmv = bm_ref[...]
        _, vjp = jax.vjp(lambda *a: _post_fn(*a, bmv), y_ref[...], xr_ref[...], k2_ref[...], xv_ref[...], g_ref[...],
                         lg_ref[...], lb_ref[...], rk_ref[...])
        grads = vjp(dy_ref[...])
        for r, gval in zip(outs[:5], grads[:5]):
            r[...] = gval
        for r, gval in zip(outs[5:], grads[5:]):
            r[...] += gval

    row = pl.BlockSpec((rb, D_R), lambda i: (i, 0))
    vec = _full((1, D_R))
    return _pc(body, name="rwkv_post_bwd", grid=(t // rb,),
               in_specs=[pl.BlockSpec((rb, D_R), lambda i: (i, 1))] + [row] * 5 + [vec] * 3 + [_full(bm.shape)],
               out_specs=(row,) * 5 + (vec,) * 3, out_shape=(S((t, D_R), f32),) * 5 + (S((1, D_R), f32),) * 3,
               compiler_params=_cparams(("arbitrary",)))(dy1, y, xr, k2, xv, g, lg, lb, rk, bm)


def _seg2(x, bb):
    hi = x.astype(bf16)
    lo = (x - hi.astype(f32)).astype(bf16)
    return jnp.dot(jnp.concatenate([hi, lo], axis=1), bb, preferred_element_type=f32)


def _row4(rows, j):
    return jnp.concatenate([jnp.broadcast_to(rows[j:j + 1, p * LANES:(p + 1) * LANES], (HEAD_DIM, LANES))
                            for p in range(4)], axis=0)


def _scan_consts():
    lane_group = jnp.arange(LANES) // HEAD_DIM
    b128 = (lane_group[:, None] == lane_group[None, :]).astype(bf16)
    bb = jnp.concatenate([b128, b128], axis=0)
    qsel = (jnp.arange(PAIR_ROWS)[:, None] % HEAD_DIM == jnp.arange(LANES)[None, :] % HEAD_DIM).astype(f32)
    return bb, qsel


def _store_cols(acc_ref, o_ref, tc):
    for p in range(4):
        blk = acc_ref[p * HEAD_DIM:(p + 1) * HEAD_DIM, :].T
        o_ref[:, (2 * p) * HEAD_DIM:(2 * p + 1) * HEAD_DIM] = blk[0:tc]
        o_ref[:, (2 * p + 1) * HEAD_DIM:(2 * p + 2) * HEAD_DIM] = blk[HEAD_DIM:HEAD_DIM + tc]


PAIR_GROUP = 2 * SUBLANES


def _rwkv_pairs(w, a, b, k, wr, bm):
    t = w.shape[0]
    rb = _row_block8(t)

    def body(w_ref, a_ref, b_ref, k_ref, wr_ref, bm_ref, *outs_sh):
        outs, sh_ref = outs_sh[:-1], outs_sh[-1]

        def second(ref):
            sh_ref[0:rb, :] = ref[...]
            sh_ref[rb:rb + SUBLANES, :] = jnp.zeros((SUBLANES, D_R), f32)
            return sh_ref[1:1 + rb, :]

        w1, b1, k1 = w_ref[...], b_ref[...], k_ref[...]
        w2, a2, wr2 = second(w_ref), second(a_ref), second(wr_ref)
        bmv = bm_ref[...]
        vals = (w1 * a2, w1 * wr2, w1 * w2, b1 * w2, k1 * w2, _seg(b1 * a2, bmv), _seg(k1 * a2, bmv),
                _seg(b1 * wr2, bmv), _seg(k1 * wr2, bmv))
        for ref, val in zip(outs, vals):
            ref[...] = val

    row = pl.BlockSpec((rb, D_R), lambda i: (i, 0))
    return _pc(body, name="rwkv_pairs", grid=(t // rb,), in_specs=[row] * 5 + [_full(bm.shape)],
               out_specs=(row,) * 9, out_shape=(S((t, D_R), f32),) * 9,
               scratch_shapes=[pltpu.VMEM((rb + SUBLANES, D_R), f32)],
               compiler_params=_cparams(("arbitrary",)))(w, a, b, k, wr, bm)


def _wkv_fwd(w, k, v, a, b, wr, br, kr, pairs):
    t = w.shape[0]
    tc = SCAN_CHUNK
    bb, qsel = _scan_consts()

    def body(*refs):
        step_refs, pair_refs = refs[0:8], refs[8:17]
        bb_ref, q_ref, y_ref, st_ref, sa_ref, vb_ref, s_scr, yacc = refs[17:]

        @pl.when(pl.program_id(0) == 0)
        def _():
            s_scr[...] = jnp.zeros_like(s_scr)
        bbv, qv = bb_ref[...], q_ref[...]
        lane64 = lax.broadcasted_iota(jnp.int32, (PAIR_ROWS, LANES), 1) % HEAD_DIM

        def halves(x):
            hi = x.astype(bf16)
            return jnp.concatenate([hi, (x - hi.astype(f32)).astype(bf16)], axis=1)

        def group(gi, s):
            base = pl.multiple_of(gi * PAIR_GROUP, PAIR_GROUP)
            w16, k16, v16, a16, b16, wr16, br16, kr16 = (
                (ref[pl.ds(base, SUBLANES), :], ref[pl.ds(base + SUBLANES, SUBLANES), :]) for ref in step_refs)
            a2p, r2p, w12p, b1wp, k1wp, betap, kappap, bwrp, kwrp = (
                (ref[pl.ds(base, SUBLANES), :], ref[pl.ds(base + SUBLANES, SUBLANES), :]) for ref in pair_refs)
            vh16 = tuple(x.astype(bf16).astype(f32) for x in v16)
            vl16 = tuple(x - h for x, h in zip(v16, vh16))
            step = lambda arr, j: _row4(arr[j // SUBLANES], j % SUBLANES)
            for q in range(SUBLANES):
                j1, j2 = 2 * q, 2 * q + 1
                t1 = base + j1
                lhs = [halves(jnp.concatenate([s * step(a16, j1), s * step(a2p, j1), s * step(wr16, j1), s * step(r2p, j1)],
                                              axis=0))]
                for j in (j1, j2):
                    lhs.append(jnp.concatenate([(qv * step(vh16, j)).astype(bf16), (qv * step(vl16, j)).astype(bf16)], axis=1))
                r = jnp.dot(jnp.concatenate(lhs, axis=0), bbv, preferred_element_type=f32)
                sa1, p2, z1, z2, vb1, vb2 = (r[n * PAIR_ROWS:(n + 1) * PAIR_ROWS] for n in range(6))
                sa2 = p2 + sa1 * step(betap, j1) + vb1 * step(kappap, j1)
                y1 = z1 + sa1 * step(br16, j1) + vb1 * step(kr16, j1)
                y2 = (z2 + sa1 * step(bwrp, j1) + vb1 * step(kwrp, j1)) + (sa2 * step(br16, j2) + vb2 * step(kr16, j2))
                yacc[...] = jnp.where(lane64 == t1, y1, jnp.where(lane64 == t1 + 1, y2, yacc[...]))
                st_ref[t1] = s
                st_ref[t1 + 1] = s * step(w16, j1) + sa1 * step(b16, j1) + vb1 * step(k16, j1)
                sa_ref[t1] = sa1
                sa_ref[t1 + 1] = sa2
                vb_ref[t1] = vb1
                vb_ref[t1 + 1] = vb2
                s = ((s * step(w12p, j1) + sa1 * step(b1wp, j1)) + vb1 * step(k1wp, j1)) + (sa2 * step(b16, j2) + vb2 * step(k16, j2))
            return s

        s_scr[...] = lax.fori_loop(0, tc // PAIR_GROUP, group, s_scr[...])
        _store_cols(yacc, y_ref, tc)

    row = pl.BlockSpec((tc, D_R), lambda c: (c, 0))
    tiles = pl.BlockSpec((tc, PAIR_ROWS, LANES), lambda c: (c, 0, 0))
    return _pc(body, name="wkv_fwd", grid=(t // tc,),
               in_specs=[row] * 17 + [_full(bb.shape), _full(qsel.shape)],
               out_specs=(row, tiles, tiles, tiles),
               out_shape=(S((t, D_R), f32),) + (S((t, PAIR_ROWS, LANES), f32),) * 3,
               scratch_shapes=[pltpu.VMEM((PAIR_ROWS, LANES), f32), pltpu.VMEM((PAIR_ROWS, LANES), f32)],
               compiler_params=_cparams(("arbitrary",)))(w, k, v, a, b, wr, br, kr, *pairs, bb, qsel)


def _wkv_bwd(sprev, sab, vbb, w, k, a, b, r, dy):
    t = w.shape[0]
    tc = SCAN_CHUNK
    nc = t // tc
    bb, qsel = _scan_consts()

    def body(st_ref, sa_ref, vb_ref, w_ref, k_ref, a_ref, b_ref, r_ref, dy_ref, bb_ref, q_ref,
             dr_ref, dw_ref, dk_ref, dv_ref, da_ref, db_ref, g_scr, dvacc, rows_scr):
        @pl.when(pl.program_id(0) == 0)
        def _():
            g_scr[...] = jnp.zeros_like(g_scr)
        bbv, qv = bb_ref[...], q_ref[...]
        lane64 = lax.broadcasted_iota(jnp.int32, (PAIR_ROWS, LANES), 1) % HEAD_DIM
        outs = (dr_ref, dw_ref, db_ref, dk_ref, da_ref)

        def colsums(slot, j, x):
            for p in range(4):
                rows_scr[slot, j:j + 1, p * LANES:(p + 1) * LANES] = jnp.sum(x[p * HEAD_DIM:(p + 1) * HEAD_DIM], axis=0,
                                                                           keepdims=True)

        def group(i, g):
            base = pl.multiple_of((tc // SUBLANES - 1 - i) * SUBLANES, SUBLANES)
            w8, k8, a8, b8, r8, dy8 = (ref[pl.ds(base, SUBLANES), :] for ref in (w_ref, k_ref, a_ref, b_ref, r_ref, dy_ref))
            for j in reversed(range(SUBLANES)):
                tt = base + j
                sp, u, vb = st_ref[tt], sa_ref[tt], vb_ref[tt]
                a4, b4, w4, k4 = _row4(a8, j), _row4(b8, j), _row4(w8, j), _row4(k8, j)
                dyb = _seg2(qv * _row4(dy8, j), bbv)
                s_t = sp * w4 + u * b4 + vb * k4
                g = g + dyb * _row4(r8, j)
                rr2 = _seg2(jnp.concatenate([g * b4, g * k4], axis=0), bbv)
                du, dvb = rr2[0:PAIR_ROWS], rr2[PAIR_ROWS:2 * PAIR_ROWS]
                for slot, val in enumerate((s_t * dyb, g * sp, g * u, g * vb, sp * du)):
                    colsums(slot, j, val)
                dvacc[...] = jnp.where(lane64 == tt, dvb, dvacc[...])
                g = g * w4 + du * a4
            for slot, ref in enumerate(outs):
                ref[pl.ds(base, SUBLANES), :] = rows_scr[slot]
            return g

        g_scr[...] = lax.fori_loop(0, tc // SUBLANES, group, g_scr[...])
        _store_cols(dvacc, dv_ref, tc)

    row = pl.BlockSpec((tc, D_R), lambda c: (nc - 1 - c, 0))
    tiles = pl.BlockSpec((tc, PAIR_ROWS, LANES), lambda c: (nc - 1 - c, 0, 0))
    return _pc(body, name="wkv_bwd", grid=(nc,),
               in_specs=[tiles] * 3 + [row] * 6 + [_full(bb.shape), _full(qsel.shape)],
               out_specs=(row,) * 6, out_shape=(S((t, D_R), f32),) * 6,
               scratch_shapes=[pltpu.VMEM((PAIR_ROWS, LANES), f32), pltpu.VMEM((PAIR_ROWS, LANES), f32),
                               pltpu.VMEM((5, SUBLANES, D_R), f32)],
               compiler_params=_cparams(("arbitrary",)))(sprev, sab, vbb, w, k, a, b, r, dy, bb, qsel)


def _rope_tables(t):
    half = HEAD_DIM // 2
    inv = ROPE_THETA ** (-jnp.arange(half, dtype=f32) / half)
    ang = jnp.arange(t, dtype=f32)[:, None] * inv[None, :]
    cos, sin = jnp.cos(ang), jnp.sin(ang)
    return jnp.concatenate([cos, cos], axis=1), jnp.concatenate([-sin, sin], axis=1)


def _head_matrix():
    grp = jnp.arange(D_R) // HEAD_DIM
    return (grp[:, None] == grp[None, :]).astype(f32)


def _ffn_fwd(h, g, w_up_t, conv_w, conv_b, w_down, i):
    hf = _rms_fwd(h, g, f"ffn{i}_norm")
    u = _mm(hf, w_up_t, "nt", f"ffn{i}_up")
    z = _ffn_mid(u, conv_w, conv_b, f"ffn{i}_mid")
    return _mm(z, w_down, "nn", f"ffn{i}_down", res=h), (hf, u, z)


def _ffn_bwd(dh, h, saved, g, w_up_t, conv_w, conv_b, w_down, i, put_g):
    hf, u, z = saved
    dz = _mm(dh, w_down, "nt", f"ffn{i}_dz")
    g_down = _mm(z, dh, "tn", f"ffn{i}_gdown", out_dtype=GRAD_WIRE_DTYPE)
    du, g_conv, g_convb = _ffn_mid_bwd(dz, u, conv_w, conv_b, f"ffn{i}_mid_bwd")
    g_up_t = _mm(du, hf, "tn", f"ffn{i}_gup", out_dtype=GRAD_WIRE_DTYPE)
    tok = put_g(f"ff{i}", [g_up_t, g_down])
    dhf = _mm(du, w_up_t, "nn", f"ffn{i}_dhf")
    dh_in, g_norm = _rms_bwd(dhf, h, g + tok, dh, f"ffn{i}_norm_bwd")
    return dh_in, dict(conv=g_conv, conv_b=g_convb, norm=g_norm)


def _local_step(x, target, W, get_w, put_g, tok0):
    t = N_META + x.shape[0]
    c64, s64 = _rope_tables(t)
    bm = _head_matrix()
    h0 = jnp.concatenate([W["meta_tokens"], x], axis=0)

    ev_w_in_t, ev_w_out = get_w("ev", None)
    hn0 = _rms_fwd(h0, W["norm_mix"][0] + tok0, "mix0_norm")
    p0 = _mm(hn0, ev_w_in_t, "nt", "ev_in")
    uc = _ev_a_conv(p0, W["ev_conv_a"])
    y0 = jnp.concatenate([_ev_a_norm(uc, W["ev_ln_a_g"], W["ev_ln_a_b"]), _ev_b(p0, W["ev_conv_b"])], axis=1)
    h1 = _mm(y0, ev_w_out, "nn", "ev_out", res=h0)
    ff0_up_t, ff0_down = get_w("ff0", y0)
    h2, ffn0 = _ffn_fwd(h1, W["norm_ffn"][0], ff0_up_t, W["ff_conv"][0], W["ff_conv_b"][0], ff0_down, 0)
    od_w_in_t, od_w_out = get_w("od", ffn0[2])

    hn1 = _rms_fwd(h2, W["norm_mix"][1], "mix1_norm")
    p1 = _mm(hn1, od_w_in_t, "nt", "od_in")
    pr = p1[:, ATT_COLS:]
    qp, kp, vp = _rope_pack(p1[:, :ATT_COLS], c64, s64)
    op = _attn_fwd(qp, kp, vp, W["od_sinks"])
    prep_params = [W[n] for n in _PREP_PARAMS]
    xr, xv, decay, k2, a_s, b_s, wr, br, kr, gate = _rwkv_prep(pr, W["od_mu"], prep_params, bm)
    pairs = _rwkv_pairs(decay, a_s, b_s, k2, wr, bm)
    ysc, sprev, sab, vbb = _wkv_fwd(decay, k2, xv, a_s, b_s, wr, br, kr, pairs)
    rk = W["od_r_k"].reshape(1, D_R)
    yr = _rwkv_post(ysc, xr, k2, xv, gate, W["od_lnx_g"], W["od_lnx_b"], rk, bm)
    y1 = jnp.concatenate([op[ATT_PAD:], yr.astype(bf16)], axis=1)
    h3 = _mm(y1, od_w_out, "nn", "od_out", res=h2)
    ff1_up_t, ff1_down = get_w("ff1", y1)
    h4, ffn1 = _ffn_fwd(h3, W["norm_ffn"][1], ff1_up_t, W["ff_conv"][1], W["ff_conv_b"][1], ff1_down, 1)

    tgt = jnp.concatenate([jnp.zeros((N_META, D_MODEL), f32), target], axis=0)
    loss, dh4, g_norm_final = _final_loss(h4, W["norm_final"], tgt)

    dh3, gf1 = _ffn_bwd(dh4, h3, ffn1, W["norm_ffn"][1], ff1_up_t, W["ff_conv"][1], W["ff_conv_b"][1], ff1_down, 1, put_g)
    dy1 = _mm(dh3, od_w_out, "nt", "od_dy")
    g_od_w_out = _mm(y1, dh3, "tn", "od_gout", out_dtype=GRAD_WIRE_DTYPE)
    dysc, dxr_p, dk2_p, dxv_p, dgate, g_lnx_g, g_lnx_b, g_rk = _rwkv_post_bwd(
        dy1, ysc, xr, k2, xv, gate, W["od_lnx_g"], W["od_lnx_b"], rk, bm)
    dr, dw, dk, dv, da, db = _wkv_bwd(sprev, sab, vbb, decay, k2, a_s, b_s, xr, dysc)
    prep_grads = _rwkv_prep_bwd(pr, W["od_mu"], prep_params, bm,
                                [[dw], [dk, dk2_p], [da], [db], [dgate], [dr, dxr_p], [dv, dxv_p]])
    dxs, g_mu = prep_grads[0], prep_grads[1]
    dpr = _shift_bwd(dxs, W["od_mu"])
    dop = jnp.concatenate([jnp.zeros((ATT_PAD, D_ATT), f32), dy1[:, :D_ATT]], axis=0).astype(bf16)
    dqp, dkp, dvp, dsk = _attn_bwd(qp, kp, vp, W["od_sinks"], dop)
    dp1 = jnp.concatenate([_rope_bwd(dqp, dkp, dvp, c64, s64), dpr], axis=1)
    g_od_w_in_t = _mm(dp1, hn1, "tn", "od_gin", out_dtype=GRAD_WIRE_DTYPE)
    tok = put_g("od", [g_od_w_in_t, g_od_w_out])
    dhn1 = _mm(dp1, od_w_in_t, "nn", "od_dhn")
    dh2, g_norm_mix1 = _rms_bwd(dhn1, h2, W["norm_mix"][1] + tok, dh3, "mix1_norm_bwd")

    dh1, gf0 = _ffn_bwd(dh2, h1, ffn0, W["norm_ffn"][0], ff0_up_t, W["ff_conv"][0], W["ff_conv_b"][0], ff0_down, 0, put_g)
    dy0 = _mm(dh1, ev_w_out, "nt", "ev_dy")
    g_ev_w_out = _mm(y0, dh1, "tn", "ev_gout", out_dtype=GRAD_WIRE_DTYPE)
    duc, g_ln_g, g_ln_b = _ev_a_norm_bwd(dy0, uc, W["ev_ln_a_g"], W["ev_ln_a_b"])
    dav, dag, g_conv_a = _ev_a_conv_bwd(duc, p0, W["ev_conv_a"])
    dgb, dgc, dxi, g_conv_b = _ev_b_bwd(dy0, p0, W["ev_conv_b"])
    dp0 = jnp.concatenate([dav, dag, dgb, dgc, dxi], axis=1)
    g_ev_w_in_t = _mm(dp0, hn0, "tn", "ev_gin", out_dtype=GRAD_WIRE_DTYPE)
    tok = put_g("ev", [g_ev_w_in_t, g_ev_w_out])
    dhn0 = _mm(dp0, ev_w_in_t, "nn", "ev_dhn")
    dh0, g_norm_mix0 = _rms_bwd(dhn0, h0, W["norm_mix"][0] + tok, dh1, "mix0_norm_bwd")

    G = dict(
        meta_tokens=dh0[:N_META], norm_mix=jnp.concatenate([g_norm_mix0, g_norm_mix1], axis=0),
        norm_ffn=jnp.concatenate([gf0["norm"], gf1["norm"]], axis=0), norm_final=g_norm_final.reshape(D_MODEL),
        ev_conv_a=g_conv_a, ev_ln_a_g=g_ln_g, ev_ln_a_b=g_ln_b, ev_conv_b=g_conv_b,
        od_sinks=dsk[:, :N_Q_HEADS], od_mu=g_mu,
        od_lnx_g=g_lnx_g, od_lnx_b=g_lnx_b, od_r_k=g_rk.reshape(N_Q_HEADS, HEAD_DIM),
        ff_conv=jnp.stack([gf0["conv"], gf1["conv"]]), ff_conv_b=jnp.concatenate([gf0["conv_b"], gf1["conv_b"]], axis=0),
    )
    for name, gval in zip(_PREP_PARAMS, prep_grads[2:]):
        G[name] = gval
    return loss, dh0[N_META:], G


HBM = pl.BlockSpec(memory_space=pl.ANY)


def _mesh_pos():
    return lax.axis_index("x"), lax.axis_index("y"), lax.axis_index("c")


def _dev(px, py, pc):
    return 4 * px + 2 * py + pc


def _all_gather(xs, name):
    n = len(xs)

    def body(*refs):
        x_refs, o_refs = refs[:n], refs[n:2 * n]
        send_sems, recv_sems, local_sems = refs[2 * n:]
        x, y, c = _mesh_pos()
        me, sibling = (x, y, c), (x, y, 1 - c)
        chips = [(1 - x, y), (x, 1 - y), (1 - x, 1 - y)]

        def copy(i, k, block, to, from_input=False):
            dst = o_refs[i].at[_dev(*block)]
            return pltpu.make_async_remote_copy(src_ref=x_refs[i] if from_input else dst, dst_ref=dst,
                                                send_sem=send_sems.at[i, k], recv_sem=recv_sems.at[i, k],
                                                device_id=to, device_id_type=MESH)

        mine = [pltpu.make_async_copy(x_refs[i], o_refs[i].at[_dev(*me)], local_sems.at[i]) for i in range(n)]
        for cp in mine:
            cp.start()
        first = []
        for i in range(n):
            first.append(copy(i, 0, me, sibling, True))
            first += [copy(i, 1 + j, me, (*chip, c), True) for j, chip in enumerate(chips)]
        for cp in first:
            cp.start()
        passed = []
        for j, chip in enumerate(chips):
            for i in range(n):
                copy(i, 1 + j, (*chip, c), me).wait_recv()
                fwd = copy(i, 4 + j, (*chip, c), sibling)
                fwd.start()
                passed.append(fwd)
        for i in range(n):
            copy(i, 0, sibling, me).wait_recv()
            for j, chip in enumerate(chips):
                copy(i, 4 + j, (*chip, 1 - c), me).wait_recv()
        for cp in first + passed:
            cp.wait_send()
        for cp in mine:
            cp.wait()

    return _pc(body, name=name, in_specs=[HBM] * n, out_specs=tuple([HBM] * n),
               out_shape=tuple(S((N_DEV,) + x.shape, x.dtype) for x in xs),
               scratch_shapes=[pltpu.SemaphoreType.DMA((n, 7)), pltpu.SemaphoreType.DMA((n, 7)),
                               pltpu.SemaphoreType.DMA((n,))])(*xs)


HBM_SPEC = pl.BlockSpec(memory_space=pltpu.HBM)
SEM_SPEC = pl.BlockSpec(memory_space=pltpu.SEMAPHORE)
DATAFLOW = pltpu.SideEffectType.DATAFLOW_SIDE_EFFECTING
_PEER_FLIPS = ((1, 0, 0), (0, 1, 0), (1, 1, 0), (1, 0, 1), (0, 1, 1), (1, 1, 1), (0, 0, 1))
N_PEERS = len(_PEER_FLIPS)


def _peers(x, y, c):
    return [((1 - x) if fx else x, (1 - y) if fy else y, (1 - c) if fc else c) for fx, fy, fc in _PEER_FLIPS]


def _xchg_start(srcs, lands, scatter, name):
    n = len(srcs)

    def body(*refs):
        src_refs, land_refs = refs[:n], refs[n:2 * n]
        send_sems, recv_sems, token = refs[2 * n], refs[2 * n + 1], refs[-1]
        x, y, c = _mesh_pos()
        me = _dev(x, y, c)
        for i in range(n):
            for k, peer in enumerate(_peers(x, y, c)):
                pltpu.make_async_remote_copy(src_ref=src_refs[i].at[_dev(*peer)] if scatter else src_refs[i],
                                             dst_ref=land_refs[i].at[me], send_sem=send_sems.at[i * N_PEERS + k],
                                             recv_sem=recv_sems.at[i * N_PEERS + k], device_id=peer, device_id_type=MESH).start()
        token[...] = jnp.zeros_like(token)

    arrs = list(srcs) + list(lands)
    outs = _pc(body, name=name,
               out_shape=(pltpu.SemaphoreType.DMA((n * N_PEERS,)), pltpu.SemaphoreType.DMA((n * N_PEERS,)),
                          *[pltpu.HBM(a.shape, a.dtype) for a in arrs], S((SUBLANES, LANES), f32)),
               in_specs=[HBM_SPEC] * (2 * n),
               out_specs=(SEM_SPEC, SEM_SPEC, *[HBM_SPEC] * (2 * n), pl.BlockSpec(memory_space=pltpu.VMEM)),
               input_output_aliases={i: 2 + i for i in range(2 * n)},
               compiler_params=pltpu.CompilerParams(has_side_effects=DATAFLOW))(
        *[pltpu.with_memory_space_constraint(a, pltpu.HBM) for a in arrs])
    return (outs[0], outs[1], list(outs[2:2 + n]), list(outs[2 + n:2 + 2 * n]), scatter), outs[-1]


def _xchg_wait(handle, after, name):
    send_sems, recv_sems, srcs, lands, scatter = handle
    n = len(srcs)

    def body(*refs):
        src_refs, land_refs = refs[:n], refs[n:2 * n]
        send, recv = refs[2 * n], refs[2 * n + 1]
        x, y, c = _mesh_pos()
        for i in range(n):
            for k in range(N_PEERS):
                cp = pltpu.make_async_remote_copy(src_ref=src_refs[i].at[0] if scatter else src_refs[i],
                                                  dst_ref=land_refs[i].at[0], send_sem=send.at[i * N_PEERS + k],
                                                  recv_sem=recv.at[i * N_PEERS + k],
                                                  device_id=(x, y, c), device_id_type=MESH)
                cp.wait_send()
                cp.wait_recv()

    arrs = srcs + lands
    outs = _pc(body, name=name, out_shape=tuple(pltpu.HBM(a.shape, a.dtype) for a in arrs),
               in_specs=[HBM_SPEC] * (2 * n) + [SEM_SPEC, SEM_SPEC, pl.BlockSpec(memory_space=pl.ANY)],
               out_specs=tuple([HBM_SPEC] * (2 * n)), input_output_aliases={i: i for i in range(2 * n)},
               compiler_params=pltpu.CompilerParams(has_side_effects=DATAFLOW))(*arrs, send_sems, recv_sems, after)
    return list(outs[:n]), list(outs[n:])


def _rs_sum(g, land, me_vec, name):
    _, r, cols = g.shape
    tr = _divisor_block(r, 16, min(r, 352))

    def body(me_ref, g_ref, *rest):
        o_ref = rest[-1]
        acc = g_ref[0].astype(f32)
        for l_ref in rest[:-1]:
            acc = acc + l_ref[0].astype(f32)
        o_ref[...] = acc

    blk = lambda f: pl.BlockSpec((1, tr, cols), f)
    grid_spec = pltpu.PrefetchScalarGridSpec(
        num_scalar_prefetch=1, grid=(r // tr,),
        in_specs=[blk(lambda i, me_ref: (me_ref[0], i, 0))]
        + [blk(lambda i, me_ref, k=k: ((me_ref[0] + k) % N_DEV, i, 0)) for k in range(1, N_DEV)],
        out_specs=pl.BlockSpec((tr, cols), lambda i, me_ref: (i, 0)))
    return _pc(body, name=name, grid_spec=grid_spec, out_shape=S((r, cols), f32),
               compiler_params=_cparams(("arbitrary",)))(me_vec, g, *([land] * (N_DEV - 1)))


def _sum_devices(a):
    def body(a_ref, o_ref):
        acc = a_ref[0]
        for d in range(1, N_DEV):
            acc = acc + a_ref[d]
        o_ref[...] = acc

    return _pc(body, name="sum_small_grads", grid=(1,), in_specs=[_full(a.shape)], out_specs=_full(a.shape[1:]),
               out_shape=S(a.shape[1:], a.dtype), compiler_params=_cparams(("arbitrary",)))(a)


def _adamw(w, m, v, g, name):
    shape = w.shape
    w2, m2, v2, g2 = (a.reshape(-1, shape[-1]) for a in (w, m, v, g))
    rows, cols = w2.shape
    tr = rows if rows % SUBLANES else _divisor_block(rows, SUBLANES, max(SUBLANES, min(rows, ADAMW_BLOCK_ELEMS // cols)))
    c1, c2 = 1.0 - ADAM_B1 ** ADAM_STEP, 1.0 - ADAM_B2 ** ADAM_STEP

    def body(w_ref, m_ref, v_ref, g_ref, d_ref, nm_ref, nv_ref):
        gv = g_ref[...]
        nm = ADAM_B1 * m_ref[...] + (1.0 - ADAM_B1) * gv
        nv = ADAM_B2 * v_ref[...] + (1.0 - ADAM_B2) * (gv * gv)
        d_ref[...] = -ADAM_LR * ((nm / c1) / (jnp.sqrt(nv / c2) + ADAM_EPS) + ADAM_WD * w_ref[...])
        nm_ref[...] = nm
        nv_ref[...] = nv

    blk = pl.BlockSpec((tr, cols), lambda i: (i, 0))
    outs = _pc(body, name=name, grid=(rows // tr,), in_specs=[blk] * 4, out_specs=(blk,) * 3,
               out_shape=(S((rows, cols), f32),) * 3, compiler_params=_cparams(("arbitrary",)))(w2, m2, v2, g2)
    return tuple(o.reshape(shape) for o in outs)


_WEIGHTS = ("meta_tokens", "norm_mix", "norm_ffn", "norm_final", "ev_w_in", "ev_conv_a", "ev_ln_a_g", "ev_ln_a_b",
            "ev_conv_b", "ev_w_out", "od_w_in", "od_sinks", "od_mu", "od_w0", "od_w2", "od_a0", "od_a2", "od_g2",
            "od_k_k", "od_k_a", "od_r_k", "od_lnx_g", "od_lnx_b", "od_w_out", "ff_w_up", "ff_conv", "ff_conv_b", "ff_w_down")
_SMALL_SHARDED = (("meta_tokens", 1), ("ev_conv_a", 2), ("ev_conv_b", 2), ("od_mu", 1), ("od_w0", 1), ("od_w2", 2),
                  ("od_a0", 1), ("od_a2", 2), ("od_g2", 2), ("od_k_k", 1), ("od_k_a", 1), ("od_lnx_g", 1),
                  ("od_lnx_b", 1), ("ff_conv", 2))
_SMALL_REPLICATED = ("norm_mix", "norm_ffn", "norm_final", "ev_ln_a_g", "ev_ln_a_b", "od_sinks", "od_r_k", "ff_conv_b")
SLAB_UNIT = SUBLANES * LANES


def _pack(arrs):
    flat = jnp.concatenate([a.reshape(-1).astype(f32) for a in arrs])
    pad = (-flat.shape[0]) % SLAB_UNIT
    return jnp.pad(flat, (0, pad)).reshape(-1, LANES)


def _unpack(flat, shapes):
    out, off = [], 0
    for shp in shapes:
        size = 1
        for s in shp:
            size *= s
        out.append(flat[..., off:off + size].reshape(flat.shape[:-1] + tuple(shp)))
        off += size
    return out


def _full_shape(shape, axis):
    return tuple(N_DEV * s if i == axis else s for i, s in enumerate(shape))


def kernel(x, meta_tokens, norm_mix, norm_ffn, norm_final, ev_w_in, ev_conv_a, ev_ln_a_g, ev_ln_a_b, ev_conv_b, ev_w_out, od_w_in, od_sinks, od_mu, od_w0, od_w2, od_a0, od_a2, od_g2, od_k_k, od_k_a, od_r_k, od_lnx_g, od_lnx_b, od_w_out, ff_w_up, ff_conv, ff_conv_b, ff_w_down, loss_target, m_meta_tokens, m_norm_mix, m_norm_ffn, m_norm_final, m_ev_w_in, m_ev_conv_a, m_ev_ln_a_g, m_ev_ln_a_b, m_ev_conv_b, m_ev_w_out, m_od_w_in, m_od_sinks, m_od_mu, m_od_w0, m_od_w2, m_od_a0, m_od_a2, m_od_g2, m_od_k_k, m_od_k_a, m_od_r_k, m_od_lnx_g, m_od_lnx_b, m_od_w_out, m_ff_w_up, m_ff_conv, m_ff_conv_b, m_ff_w_down, v_meta_tokens, v_norm_mix, v_norm_ffn, v_norm_final, v_ev_w_in, v_ev_conv_a, v_ev_ln_a_g, v_ev_ln_a_b, v_ev_conv_b, v_ev_w_out, v_od_w_in, v_od_sinks, v_od_mu, v_od_w0, v_od_w2, v_od_a0, v_od_a2, v_od_g2, v_od_k_k, v_od_k_a, v_od_r_k, v_od_lnx_g, v_od_lnx_b, v_od_w_out, v_ff_w_up, v_ff_conv, v_ff_conv_b, v_ff_w_down):
    A = dict(locals())
    px, py, pc = _mesh_pos()
    me = _dev(px, py, pc)
    me_vec = jnp.reshape(me, (1,)).astype(jnp.int32)
    rows = lambda a: a.reshape(N_DEV * a.shape[1], a.shape[2])
    blocks = lambda a: a.reshape(N_DEV, a.shape[0] // N_DEV, a.shape[1])

    shards = dict(ev=[ev_w_in[0].T, ev_w_out[0]], ff0=[ff_w_up[0].T, ff_w_down[0]], od=[od_w_in[0].T, od_w_out[0]],
                  ff1=[ff_w_up[1].T, ff_w_down[1]])
    shards = {grp: [b.astype(bf16) for b in bs] for grp, bs in shards.items()}
    small_shapes = [A[n].shape for n, _ in _SMALL_SHARDED]
    gathered = _all_gather(shards["ev"] + [_pack([A[n] for n, _ in _SMALL_SHARDED])], "gather_first")
    gathered, shards = lax.optimization_barrier((gathered, shards))
    fetch, tok0 = {}, 0.0
    for grp in ("ff0", "od", "ff1"):
        lands = [lax.dynamic_update_slice(lax.empty((N_DEV,) + b.shape, b.dtype), b[None], (me, 0, 0)) for b in shards[grp]]
        fetch[grp], token = _xchg_start(shards[grp], lands, False, f"gather_{grp}_start")
        tok0 = tok0 + token[0, 0]

    def get_w(grp, after):
        if grp == "ev":
            return [rows(g) for g in gathered[:2]]
        return [rows(land) for land in _xchg_wait(fetch[grp], after, f"gather_{grp}_wait")[1]]

    W = {}
    for (n, ax), seg in zip(_SMALL_SHARDED, _unpack(gathered[-1].reshape(N_DEV, -1), small_shapes)):
        W[n] = jnp.moveaxis(seg, 0, ax).reshape(_full_shape(A[n].shape, ax))
    for n in ("ev_conv_a", "ev_conv_b", "od_w2", "od_a2", "od_g2"):
        W[n] = W[n][0]
    for n in _SMALL_REPLICATED:
        W[n] = A[n]
    W["od_r_k"] = od_r_k[0]

    sent = {}

    def put_g(grp, gs):
        g8 = [blocks(g) for g in gs]
        sent[grp], token = _xchg_start(g8, [lax.empty(g.shape, g.dtype) for g in g8], True, f"reduce_{grp}_start")
        return token[0, 0]

    loss_tile, grad_x, G = _local_step(x[0], loss_target[0], W, get_w, put_g, tok0)

    gsh = {}
    for grp in ("ff1", "od", "ff0", "ev"):
        srcs, lands = _xchg_wait(sent[grp], grad_x, f"reduce_{grp}_wait")
        gsh[grp] = [_rs_sum(g, land, me_vec, f"reduce_{grp}_sum{i}") for i, (g, land) in enumerate(zip(srcs, lands))]
    grads = dict(ev_w_in=gsh["ev"][0].T[None], ev_w_out=gsh["ev"][1][None], od_w_in=gsh["od"][0].T[None],
                 od_w_out=gsh["od"][1][None], ff_w_up=jnp.stack([gsh["ff0"][0].T, gsh["ff1"][0].T]),
                 ff_w_down=jnp.stack([gsh["ff0"][1], gsh["ff1"][1]]))

    small_names = [n for n, _ in _SMALL_SHARDED] + list(_SMALL_REPLICATED)
    small_full_shapes = [_full_shape(A[n].shape, ax) for n, ax in _SMALL_SHARDED] + [A[n].shape for n in _SMALL_REPLICATED]
    (gsm,) = _all_gather([_pack([G[n] for n in small_names])], "gather_small_grads")
    summed = _unpack(_sum_devices(gsm).reshape(-1), small_full_shapes)
    for n, full in zip(small_names, summed):
        grads[n] = full
    for n, ax in _SMALL_SHARDED:
        size = A[n].shape[ax]
        grads[n] = lax.dynamic_slice_in_dim(grads[n], me * size, size, axis=ax)

    delta, new_m, new_v = {}, {}, {}
    for n in _WEIGHTS:
        delta[n], new_m[n], new_v[n] = _adamw(A[n], A["m_" + n], A["v_" + n], grads[n], "adamw_" + n)

    loss = lax.psum(loss_tile[0, 0], ("x", "y", "c"))
    return (loss, grad_x[None], *[grads[n] for n in _WEIGHTS], *[delta[n] for n in _WEIGHTS],
            *[new_m[n] for n in _WEIGHTS], *[new_v[n] for n in _WEIGHTS])
```

```python
import jax
import jax.numpy as jnp
from jax import lax
from jax.experimental import pallas as pl
from jax.experimental.pallas import tpu as pltpu

f32, bf16 = jnp.float32, jnp.bfloat16

D_MODEL = 1024
N_META = 16
RMS_EPS = 1e-6
LN_EPS = 1e-5
D_A = 512
CONV_A_WIDTH = 31
CONV_B_WIDTH = 3
HEAD_DIM = 64
N_Q_HEADS = 8
N_KV_HEADS = 2
GQA_GROUP = 4
D_ATT = 512
D_KV = 128
BLOCK = 128
ROPE_THETA = 10000.0
D_R = 512
LORA_W, LORA_A, LORA_G = 64, 64, 128
RWKV_GN_EPS = 64e-5
ATT_COLS = D_ATT + 2 * D_KV
RWKV_COLS = 3 * D_R + LORA_W + LORA_A + LORA_G
D_FF = 2816
FF_CONV_WIDTH = 3
NEG_INF = -1e30
ATT_PAD = BLOCK - N_META
ATT_SCALE = HEAD_DIM ** -0.5

ADAM_LR, ADAM_B1, ADAM_B2, ADAM_EPS, ADAM_WD, ADAM_STEP = 0.001, 0.9, 0.999, 1e-08, 0.01, 10

N_DEV = 8
LANES = 128
SUBLANES = 8
SCAN_CHUNK = 48
PAIR_ROWS = 4 * HEAD_DIM
V7X_VMEM_LIMIT = 56 * 1024 * 1024
ADAMW_BLOCK_ELEMS = 400 * 1024
GRAD_WIRE_DTYPE = bf16
MESH = pl.DeviceIdType.MESH
S = jax.ShapeDtypeStruct
HIGHEST = lax.Precision.HIGHEST


def _pc(body, **kw):
    return pl.pallas_call(body, **kw)


def _cparams(sem=None):
    return pltpu.CompilerParams(dimension_semantics=sem, vmem_limit_bytes=V7X_VMEM_LIMIT)


def _divisor_block(t, unit, limit):
    best = unit
    for rb in range(unit, limit + 1, unit):
        if t % rb == 0:
            best = rb
    assert t % best == 0, (t, unit)
    return best


def _row_block(t):
    return _divisor_block(t, 16, 704)


def _row_block8(t):
    return _divisor_block(t, 8, 344)


def _col_tile(n, cap):
    return _divisor_block(n, LANES, min(n, cap)) if n % LANES == 0 else n


def _full(shape):
    nd = len(shape)
    return pl.BlockSpec(shape, lambda *_: (0,) * nd)


def _sigmoid(x):
    return jax.nn.sigmoid(x)


_DIMS = {"nn": (((1,), (0,)), ((), ())), "nt": (((1,), (1,)), ((), ())), "tn": (((0,), (0,)), ((), ()))}
MM_MAX_K = 2816
MM_MAX_TM = 704
MM_MAX_TN = 1408


def _mm(a, b, mode, name, out_dtype=f32, res=None):
    if mode == "nn":
        (m, k), (k2, n) = a.shape, b.shape
    elif mode == "nt":
        (m, k), (n, k2) = a.shape, b.shape
    else:
        (k, m), (k2, n) = a.shape, b.shape
    assert k == k2, (a.shape, b.shape, mode)
    tm = _row_block(m) if m % LANES else _col_tile(m, MM_MAX_TM)
    tn = _col_tile(n, MM_MAX_TN)
    nk = 1 if (mode == "tn" or k <= MM_MAX_K) else k // MM_MAX_K
    tk = k // nk
    assert tk * nk == k
    dims = _DIMS[mode]

    def body(a_ref, b_ref, *rest):
        part = lax.dot_general(a_ref[...].astype(bf16), b_ref[...].astype(bf16), dims, preferred_element_type=f32)
        if nk == 1:
            o_ref = rest[-1]
            if res is not None:
                part = part + rest[0][...]
            o_ref[...] = part.astype(out_dtype)
            return
        o_ref, acc_ref = rest[-2], rest[-1]
        kk = pl.program_id(2)

        @pl.when(kk == 0)
        def _():
            acc_ref[...] = part

        @pl.when(kk > 0)
        def _():
            acc_ref[...] += part

        @pl.when(kk == nk - 1)
        def _():
            acc = acc_ref[...]
            if res is not None:
                acc = acc + rest[0][...]
            o_ref[...] = acc.astype(out_dtype)

    if mode == "tn":
        a_spec = pl.BlockSpec((k, tm), lambda i, j, kk: (0, i))
    else:
        a_spec = pl.BlockSpec((tm, tk), lambda i, j, kk: (i, kk))
    if mode == "nt":
        b_spec = pl.BlockSpec((tn, tk), lambda i, j, kk: (j, kk))
    else:
        b_spec = pl.BlockSpec((tk, tn), lambda i, j, kk: (kk, j))
    o_spec = pl.BlockSpec((tm, tn), lambda i, j, kk: (i, j))
    ins, specs = [a, b], [a_spec, b_spec]
    if res is not None:
        ins.append(res)
        specs.append(o_spec)
    scratch = [pltpu.VMEM((tm, tn), f32)] if nk > 1 else []
    return _pc(body, name=name, grid=(m // tm, n // tn, nk), in_specs=specs, out_specs=o_spec,
               out_shape=S((m, n), out_dtype), scratch_shapes=scratch,
               compiler_params=_cparams(("arbitrary", "arbitrary", "arbitrary")))(*ins)


def _rms_fwd(x, g, name):
    t, d = x.shape
    rb = _row_block(t)

    def body(x_ref, g_ref, o_ref):
        xv = x_ref[...]
        rstd = lax.rsqrt(jnp.mean(xv * xv, axis=-1, keepdims=True) + RMS_EPS)
        o_ref[...] = (xv * rstd * g_ref[...]).astype(bf16)

    row = pl.BlockSpec((rb, d), lambda i: (i, 0))
    return _pc(body, name=name, grid=(t // rb,), in_specs=[row, _full((1, d))], out_specs=row,
               out_shape=S((t, d), bf16), compiler_params=_cparams(("arbitrary",)))(x, g.reshape(1, d))


def _rms_bwd(dy, x, g, dres, name):
    t, d = x.shape
    rb = _row_block8(t)

    def body(dy_ref, x_ref, g_ref, dres_ref, dx_ref, dg_ref):
        @pl.when(pl.program_id(0) == 0)
        def _():
            dg_ref[...] = jnp.zeros_like(dg_ref)
        xv, dyv = x_ref[...], dy_ref[...]
        rstd = lax.rsqrt(jnp.mean(xv * xv, axis=-1, keepdims=True) + RMS_EPS)
        xn = xv * rstd
        dg_ref[...] += jnp.sum(dyv * xn, axis=0, keepdims=True)
        dxh = dyv * g_ref[...]
        dx_ref[...] = dres_ref[...] + rstd * (dxh - xn * jnp.mean(dxh * xn, axis=-1, keepdims=True))

    row = pl.BlockSpec((rb, d), lambda i: (i, 0))
    return _pc(body, name=name, grid=(t // rb,), in_specs=[row, row, _full((1, d)), row],
               out_specs=(row, _full((1, d))), out_shape=(S((t, d), f32), S((1, d), f32)),
               compiler_params=_cparams(("arbitrary",)))(dy, x, g.reshape(1, d), dres)


def _final_loss(h, g, target_padded):
    t, d = h.shape
    rb = _row_block8(t)

    def body(x_ref, g_ref, t_ref, loss_ref, dx_ref, dg_ref):
        i = pl.program_id(0)

        @pl.when(i == 0)
        def _():
            dg_ref[...] = jnp.zeros_like(dg_ref)
            loss_ref[...] = jnp.zeros_like(loss_ref)
        xv = x_ref[...]
        rstd = lax.rsqrt(jnp.mean(xv * xv, axis=-1, keepdims=True) + RMS_EPS)
        xn = xv * rstd
        gv = g_ref[...]
        row = i * rb + lax.broadcasted_iota(jnp.int32, (rb, 1), 0)
        diff = jnp.where(row >= N_META, xn * gv - t_ref[...], 0.0)
        loss_ref[...] += 0.5 * jnp.sum(jnp.mean(diff * diff, axis=-1, keepdims=True))
        dout = diff * (1.0 / d)
        dg_ref[...] += jnp.sum(dout * xn, axis=0, keepdims=True)
        dxh = dout * gv
        dx_ref[...] = rstd * (dxh - xn * jnp.mean(dxh * xn, axis=-1, keepdims=True))

    row = pl.BlockSpec((rb, d), lambda i: (i, 0))
    return _pc(body, name="final_loss", grid=(t // rb,), in_specs=[row, _full((1, d)), row],
               out_specs=(_full((SUBLANES, LANES)), row, _full((1, d))),
               out_shape=(S((SUBLANES, LANES), f32), S((t, d), f32), S((1, d), f32)),
               compiler_params=_cparams(("arbitrary",)))(h, g.reshape(1, d), target_padded)


CONV_LEAD = 32


def _fill_front_padded(pad_ref, x, t):
    pad_ref[0:CONV_LEAD, :] = jnp.zeros((CONV_LEAD, x.shape[1]), f32)
    pad_ref[CONV_LEAD:CONV_LEAD + t, :] = x


def _fill_back_padded(pad_ref, x, t):
    pad_ref[0:t, :] = x
    pad_ref[t:t + CONV_LEAD, :] = jnp.zeros((CONV_LEAD, x.shape[1]), f32)


def _conv_rows(pad_ref, w_ref, kw, r0, nr):
    acc = None
    for j in range(kw):
        lo = CONV_LEAD + r0 - (kw - 1) + j
        term = w_ref[j:j + 1, :] * pad_ref[lo:lo + nr, :]
        acc = term if acc is None else acc + term
    return acc


def _conv_t_rows(padb_ref, w_ref, kw, r0, nr):
    acc = None
    for j in range(kw):
        lo = r0 + (kw - 1) - j
        term = w_ref[j:j + 1, :] * padb_ref[lo:lo + nr, :]
        acc = term if acc is None else acc + term
    return acc


def _conv_dw_rows(dy_blk, pad_ref, kw, r0, nr):
    out = []
    for j in range(kw):
        lo = CONV_LEAD + r0 - (kw - 1) + j
        out.append(jnp.sum(dy_blk * pad_ref[lo:lo + nr, :], axis=0, keepdims=True))
    return out


def _acc_list(a, b):
    return b if a is None else [x + y for x, y in zip(a, b)]


def _ev_a_conv(p, conv_a):
    t = p.shape[0]
    cr = _row_block8(t)
    nb = D_A // LANES

    def body(av_ref, ag_ref, w_ref, o_ref, pad_ref):
        _fill_front_padded(pad_ref, av_ref[...] * _sigmoid(ag_ref[...]), t)
        for r in range(t // cr):
            o_ref[r * cr:(r + 1) * cr, :] = _conv_rows(pad_ref, w_ref, CONV_A_WIDTH, r * cr, cr)

    col = lambda off: pl.BlockSpec((t, LANES), lambda j: (0, j + off))
    return _pc(body, name="ev_a_conv", grid=(nb,),
               in_specs=[col(0), col(nb), pl.BlockSpec((CONV_A_WIDTH, LANES), lambda j: (0, j))],
               out_specs=col(0), out_shape=S((t, D_A), f32),
               scratch_shapes=[pltpu.VMEM((t + CONV_LEAD, LANES), f32)],
               compiler_params=_cparams(("arbitrary",)))(p, p, conv_a)


def _ln_silu(uc, g, b):
    mu = jnp.mean(uc, axis=-1, keepdims=True)
    xc = uc - mu
    var = jnp.mean(xc * xc, axis=-1, keepdims=True)
    y = xc * lax.rsqrt(var + LN_EPS) * g + b
    return y * _sigmoid(y)


def _ev_a_norm(uc, g, b):
    t, d = uc.shape
    rb = _row_block(t)

    def body(u_ref, g_ref, b_ref, o_ref):
        o_ref[...] = _ln_silu(u_ref[...], g_ref[...], b_ref[...]).astype(bf16)

    row = pl.BlockSpec((rb, d), lambda i: (i, 0))
    return _pc(body, name="ev_a_norm", grid=(t // rb,), in_specs=[row, _full((1, d)), _full((1, d))],
               out_specs=row, out_shape=S((t, d), bf16), compiler_params=_cparams(("arbitrary",)))(uc, g, b)


def _ev_a_norm_bwd(dy, uc, g, b):
    t, d = uc.shape
    rb = _row_block8(t)

    def body(dy_ref, u_ref, g_ref, b_ref, du_ref, dg_ref, db_ref):
        @pl.when(pl.program_id(0) == 0)
        def _():
            dg_ref[...] = jnp.zeros_like(dg_ref)
            db_ref[...] = jnp.zeros_like(db_ref)
        _, vjp = jax.vjp(_ln_silu, u_ref[...], g_ref[...], b_ref[...])
        du, dg, db = vjp(dy_ref[...])
        du_ref[...] = du
        dg_ref[...] += dg
        db_ref[...] += db

    row = pl.BlockSpec((rb, d), lambda i: (i, 0))
    return _pc(body, name="ev_a_norm_bwd", grid=(t // rb,), in_specs=[row, row, _full((1, d)), _full((1, d))],
               out_specs=(row, _full((1, d)), _full((1, d))),
               out_shape=(S((t, d), f32), S((1, d), f32), S((1, d), f32)),
               compiler_params=_cparams(("arbitrary",)))(dy, uc, g, b)


def _ev_a_conv_bwd(duc, p, conv_a):
    t = p.shape[0]
    cr = _row_block8(t)
    nb = D_A // LANES

    def body(dy_ref, av_ref, ag_ref, w_ref, dav_ref, dag_ref, dw_ref, pad_ref, padb_ref):
        _fill_front_padded(pad_ref, av_ref[...] * _sigmoid(ag_ref[...]), t)
        _fill_back_padded(padb_ref, dy_ref[...], t)
        dw = None
        for r in range(t // cr):
            rows = slice(r * cr, (r + 1) * cr)
            du = _conv_t_rows(padb_ref, w_ref, CONV_A_WIDTH, r * cr, cr)
            avr = av_ref[rows, :]
            sgr = _sigmoid(ag_ref[rows, :])
            dav_ref[rows, :] = du * sgr
            dag_ref[rows, :] = du * avr * sgr * (1.0 - sgr)
            dw = _acc_list(dw, _conv_dw_rows(dy_ref[rows, :], pad_ref, CONV_A_WIDTH, r * cr, cr))
        for j in range(CONV_A_WIDTH):
            dw_ref[j:j + 1, :] = dw[j]

    col = lambda off: pl.BlockSpec((t, LANES), lambda j: (0, j + off))
    wsp = pl.BlockSpec((CONV_A_WIDTH, LANES), lambda j: (0, j))
    return _pc(body, name="ev_a_conv_bwd", grid=(nb,), in_specs=[col(0), col(0), col(nb), wsp],
               out_specs=(col(0), col(0), wsp),
               out_shape=(S((t, D_A), f32), S((t, D_A), f32), S((CONV_A_WIDTH, D_A), f32)),
               scratch_shapes=[pltpu.VMEM((t + CONV_LEAD, LANES), f32), pltpu.VMEM((t + CONV_LEAD, LANES), f32)],
               compiler_params=_cparams(("arbitrary",)))(duc, p, p, conv_a)


def _ev_b(p, conv_b):
    t = p.shape[0]
    cr = _row_block8(t)
    nb = D_A // LANES

    def body(gb_ref, gc_ref, xi_ref, w_ref, o_ref, pad_ref, stage_ref):
        _fill_front_padded(pad_ref, gc_ref[...] * xi_ref[...], t)
        for r in range(t // cr):
            rows = slice(r * cr, (r + 1) * cr)
            stage_ref[rows, :] = gb_ref[rows, :] * _conv_rows(pad_ref, w_ref, CONV_B_WIDTH, r * cr, cr)
        o_ref[...] = stage_ref[...].astype(bf16)

    col = lambda off: pl.BlockSpec((t, LANES), lambda j: (0, j + off))
    return _pc(body, name="ev_b", grid=(nb,),
               in_specs=[col(2 * nb), col(3 * nb), col(4 * nb), pl.BlockSpec((CONV_B_WIDTH, LANES), lambda j: (0, j))],
               out_specs=col(0), out_shape=S((t, D_A), bf16),
               scratch_shapes=[pltpu.VMEM((t + CONV_LEAD, LANES), f32), pltpu.VMEM((t, LANES), f32)],
               compiler_params=_cparams(("arbitrary",)))(p, p, p, conv_b)


def _ev_b_bwd(dy, p, conv_b):
    t = p.shape[0]
    cr = _row_block8(t)
    nb = D_A // LANES

    def body(dy_ref, gb_ref, gc_ref, xi_ref, w_ref, dgb_ref, dgc_ref, dxi_ref, dw_ref, pad_ref, padb_ref):
        _fill_front_padded(pad_ref, gc_ref[...] * xi_ref[...], t)
        _fill_back_padded(padb_ref, dy_ref[...] * gb_ref[...], t)
        dw = None
        for r in range(t // cr):
            rows = slice(r * cr, (r + 1) * cr)
            dgb_ref[rows, :] = dy_ref[rows, :] * _conv_rows(pad_ref, w_ref, CONV_B_WIDTH, r * cr, cr)
            dcx = _conv_t_rows(padb_ref, w_ref, CONV_B_WIDTH, r * cr, cr)
            dgc_ref[rows, :] = dcx * xi_ref[rows, :]
            dxi_ref[rows, :] = dcx * gc_ref[rows, :]
            dw = _acc_list(dw, _conv_dw_rows(padb_ref[rows, :], pad_ref, CONV_B_WIDTH, r * cr, cr))
        for j in range(CONV_B_WIDTH):
            dw_ref[j:j + 1, :] = dw[j]

    col = lambda off: pl.BlockSpec((t, LANES), lambda j: (0, j + off))
    wsp = pl.BlockSpec((CONV_B_WIDTH, LANES), lambda j: (0, j))
    return _pc(body, name="ev_b_bwd", grid=(nb,), in_specs=[col(nb), col(2 * nb), col(3 * nb), col(4 * nb), wsp],
               out_specs=(col(0), col(0), col(0), wsp),
               out_shape=(S((t, D_A), f32), S((t, D_A), f32), S((t, D_A), f32), S((CONV_B_WIDTH, D_A), f32)),
               scratch_shapes=[pltpu.VMEM((t + CONV_LEAD, LANES), f32), pltpu.VMEM((t + CONV_LEAD, LANES), f32)],
               compiler_params=_cparams(("arbitrary",)))(dy, p, p, p, conv_b)


def _ffn_mid(u, conv_w, conv_b, name):
    t = u.shape[0]
    cr = _row_block8(t)
    nb = D_FF // LANES

    def body(gt_ref, vl_ref, w_ref, b_ref, o_ref, pad_ref, stage_ref):
        _fill_front_padded(pad_ref, gt_ref[...], t)
        for r in range(t // cr):
            rows = slice(r * cr, (r + 1) * cr)
            gc = _conv_rows(pad_ref, w_ref, FF_CONV_WIDTH, r * cr, cr) + b_ref[...]
            stage_ref[rows, :] = gc * _sigmoid(gc) * vl_ref[rows, :]
        o_ref[...] = stage_ref[...].astype(bf16)

    col = lambda off: pl.BlockSpec((t, LANES), lambda j: (0, j + off))
    return _pc(body, name=name, grid=(nb,),
               in_specs=[col(0), col(nb), pl.BlockSpec((FF_CONV_WIDTH, LANES), lambda j: (0, j)),
                         pl.BlockSpec((1, LANES), lambda j: (0, j))],
               out_specs=col(0), out_shape=S((t, D_FF), bf16),
               scratch_shapes=[pltpu.VMEM((t + CONV_LEAD, LANES), f32), pltpu.VMEM((t, LANES), f32)],
               compiler_params=_cparams(("arbitrary",)))(u, u, conv_w, conv_b.reshape(1, D_FF))


def _ffn_mid_bwd(dz, u, conv_w, conv_b, name):
    t = u.shape[0]
    cr = _row_block8(t)
    nb = D_FF // LANES

    def body(dz_ref, gt_ref, vl_ref, w_ref, b_ref, du_ref, dw_ref, db_ref, pad_ref, padb_ref):
        s = pl.program_id(1)
        _fill_front_padded(pad_ref, gt_ref[...], t)

        @pl.when(s == 0)
        def _():
            for r in range(t // cr):
                rows = slice(r * cr, (r + 1) * cr)
                gc = _conv_rows(pad_ref, w_ref, FF_CONV_WIDTH, r * cr, cr) + b_ref[...]
                sg = _sigmoid(gc)
                padb_ref[rows, :] = dz_ref[rows, :] * vl_ref[rows, :] * sg * (1.0 + gc * (1.0 - sg))
            padb_ref[t:t + CONV_LEAD, :] = jnp.zeros((CONV_LEAD, LANES), f32)
            dw, db = None, None
            for r in range(t // cr):
                rows = slice(r * cr, (r + 1) * cr)
                du_ref[rows, :] = _conv_t_rows(padb_ref, w_ref, FF_CONV_WIDTH, r * cr, cr)
                dgc = padb_ref[rows, :]
                dw = _acc_list(dw, _conv_dw_rows(dgc, pad_ref, FF_CONV_WIDTH, r * cr, cr))
                pb = jnp.sum(dgc, axis=0, keepdims=True)
                db = pb if db is None else db + pb
            for j in range(FF_CONV_WIDTH):
                dw_ref[j:j + 1, :] = dw[j]
            db_ref[...] = db

        @pl.when(s == 1)
        def _():
            for r in range(t // cr):
                rows = slice(r * cr, (r + 1) * cr)
                gc = _conv_rows(pad_ref, w_ref, FF_CONV_WIDTH, r * cr, cr) + b_ref[...]
                du_ref[rows, :] = dz_ref[rows, :] * gc * _sigmoid(gc)

    col = lambda off: pl.BlockSpec((t, LANES), lambda j, s: (0, j + off))
    wsp = pl.BlockSpec((FF_CONV_WIDTH, LANES), lambda j, s: (0, j))
    bsp = pl.BlockSpec((1, LANES), lambda j, s: (0, j))
    return _pc(body, name=name, grid=(nb, 2), in_specs=[col(0), col(0), col(nb), wsp, bsp],
               out_specs=(pl.BlockSpec((t, LANES), lambda j, s: (0, s * nb + j)), wsp, bsp),
               out_shape=(S((t, 2 * D_FF), f32), S((FF_CONV_WIDTH, D_FF), f32), S((1, D_FF), f32)),
               scratch_shapes=[pltpu.VMEM((t + CONV_LEAD, LANES), f32), pltpu.VMEM((t + CONV_LEAD, LANES), f32)],
               compiler_params=_cparams(("arbitrary", "arbitrary")))(dz, u, u, conv_w, conv_b.reshape(1, D_FF))


def _swap_halves(x):
    w = x.shape[1]
    lane = lax.broadcasted_iota(jnp.int32, x.shape, 1) % HEAD_DIM
    return jnp.where(lane < HEAD_DIM // 2, pltpu.roll(x, w - HEAD_DIM // 2, axis=1), pltpu.roll(x, HEAD_DIM // 2, axis=1))


def _rope_pack(patt, c64, s64):
    t = patt.shape[0]
    tp = t + ATT_PAD

    def body(p_ref, c_ref, s_ref, q_ref, k_ref, v_ref):
        c, s = c_ref[...], s_ref[...]

        def rope(x, nh):
            cc = jnp.concatenate([c] * nh, axis=1)
            ss = jnp.concatenate([s] * nh, axis=1)
            return x * cc + _swap_halves(x) * ss

        for ref, val in ((q_ref, rope(p_ref[:, 0:D_ATT], N_Q_HEADS)),
                         (k_ref, rope(p_ref[:, D_ATT:D_ATT + D_KV], N_KV_HEADS)),
                         (v_ref, p_ref[:, D_ATT + D_KV:ATT_COLS])):
            ref[0:ATT_PAD, :] = jnp.zeros((ATT_PAD, val.shape[1]), bf16)
            ref[ATT_PAD:tp, :] = val.astype(bf16)

    return _pc(body, name="rope_pack", in_specs=[_full((t, ATT_COLS)), _full((t, HEAD_DIM)), _full((t, HEAD_DIM))],
               out_specs=(_full((tp, D_ATT)), _full((tp, D_KV)), _full((tp, D_KV))), grid=(1,),
               out_shape=(S((tp, D_ATT), bf16), S((tp, D_KV), bf16), S((tp, D_KV), bf16)),
               compiler_params=_cparams(("arbitrary",)))(patt, c64, s64)


def _rope_bwd(dqp, dkp, dvp, c64, s64):
    tp = dqp.shape[0]
    t = tp - ATT_PAD

    def body(dq_ref, dk_ref, dv_ref, c_ref, s_ref, o_ref):
        c, s = c_ref[...], s_ref[...]

        def unrope(dy, nh):
            cc = jnp.concatenate([c] * nh, axis=1)
            ss = jnp.concatenate([s] * nh, axis=1)
            return dy * cc + _swap_halves(dy * ss)

        o_ref[:, 0:D_ATT] = unrope(dq_ref[ATT_PAD:tp, :], N_Q_HEADS)
        o_ref[:, D_ATT:D_ATT + D_KV] = unrope(dk_ref[ATT_PAD:tp, :], N_KV_HEADS)
        o_ref[:, D_ATT + D_KV:ATT_COLS] = dv_ref[ATT_PAD:tp, :]

    return _pc(body, name="rope_bwd", grid=(1,),
               in_specs=[_full((tp, D_ATT)), _full((tp, D_KV)), _full((tp, D_KV)), _full((t, HEAD_DIM)), _full((t, HEAD_DIM))],
               out_specs=_full((t, ATT_COLS)), out_shape=S((t, ATT_COLS), f32),
               compiler_params=_cparams(("arbitrary",)))(dqp, dkp, dvp, c64, s64)


def _attn_masks(n):
    rows = GQA_GROUP * BLOCK
    ri = lax.broadcasted_iota(jnp.int32, (rows, BLOCK), 0) % BLOCK
    ci = lax.broadcasted_iota(jnp.int32, (rows, BLOCK), 1)
    m_cur = (ci <= ri) & (ci >= jnp.where(n >= 1, 0, ATT_PAD))
    m_prev = ci > ri + jnp.where(n >= 2, 0, BLOCK)
    m_meta = ci >= jnp.where(n >= 1, ATT_PAD, BLOCK)
    return m_cur, m_prev, m_meta


def _attn_probs(qg, kc, kp, km, masks, skv):
    def scores(k, m):
        s = lax.dot_general(qg, k, _DIMS["nt"], preferred_element_type=f32) * ATT_SCALE
        return jnp.where(m, s, NEG_INF)
    s_c, s_p, s_m = scores(kc, masks[0]), scores(kp, masks[1]), scores(km, masks[2])
    mx = jnp.maximum(jnp.maximum(jnp.max(s_c, axis=-1, keepdims=True), jnp.max(s_p, axis=-1, keepdims=True)),
                     jnp.maximum(jnp.max(s_m, axis=-1, keepdims=True), skv))
    e_c, e_p, e_m, e_s = jnp.exp(s_c - mx), jnp.exp(s_p - mx), jnp.exp(s_m - mx), jnp.exp(skv - mx)
    den = (jnp.sum(e_c, axis=-1, keepdims=True) + jnp.sum(e_p, axis=-1, keepdims=True)
           + jnp.sum(e_m, axis=-1, keepdims=True) + e_s)
    inv = 1.0 / den
    return e_c * inv, e_p * inv, e_m * inv, e_s * inv


def _sink_rows(sk_ref, g):
    hrow = lax.broadcasted_iota(jnp.int32, (GQA_GROUP * BLOCK, 1), 0) // BLOCK
    skv = jnp.zeros((GQA_GROUP * BLOCK, 1), f32)
    for hh in range(GQA_GROUP):
        skv = jnp.where(hrow == hh, sk_ref[0, GQA_GROUP * g + hh], skv)
    return skv, hrow


def _stack_heads(ref, g):
    return jnp.concatenate([ref[:, (GQA_GROUP * g + hh) * HEAD_DIM:(GQA_GROUP * g + hh + 1) * HEAD_DIM]
                            for hh in range(GQA_GROUP)], axis=0)


def _attn_specs():
    blk = lambda w: pl.BlockSpec((BLOCK, w), lambda n: (n, 0))
    prev = pl.BlockSpec((BLOCK, D_KV), lambda n: (jnp.maximum(n - 1, 0), 0))
    meta = pl.BlockSpec((BLOCK, D_KV), lambda n: (0, 0))
    return blk, prev, meta


def _attn_fwd(qp, kp, vp, sinks):
    tp = qp.shape[0]
    blk, prev, meta = _attn_specs()

    def body(sk_ref, q_ref, kc_ref, kp_ref, km_ref, vc_ref, vp_ref, vm_ref, o_ref):
        masks = _attn_masks(pl.program_id(0))
        for g in range(N_KV_HEADS):
            sl = slice(g * HEAD_DIM, (g + 1) * HEAD_DIM)
            skv, _ = _sink_rows(sk_ref, g)
            p_c, p_p, p_m, _ = _attn_probs(_stack_heads(q_ref, g), kc_ref[:, sl], kp_ref[:, sl], km_ref[:, sl], masks, skv)
            o = (jnp.dot(p_c.astype(bf16), vc_ref[:, sl], preferred_element_type=f32)
                 + jnp.dot(p_p.astype(bf16), vp_ref[:, sl], preferred_element_type=f32)
                 + jnp.dot(p_m.astype(bf16), vm_ref[:, sl], preferred_element_type=f32))
            for hh in range(GQA_GROUP):
                h = GQA_GROUP * g + hh
                o_ref[:, h * HEAD_DIM:(h + 1) * HEAD_DIM] = o[hh * BLOCK:(hh + 1) * BLOCK].astype(bf16)

    return _pc(body, name="attn_fwd", grid=(tp // BLOCK,),
               in_specs=[pl.BlockSpec(memory_space=pltpu.SMEM), blk(D_ATT), blk(D_KV), prev, meta, blk(D_KV), prev, meta],
               out_specs=blk(D_ATT), out_shape=S((tp, D_ATT), bf16),
               compiler_params=_cparams(("arbitrary",)))(sinks, qp, kp, kp, kp, vp, vp, vp)


def _attn_bwd(qp, kp, vp, sinks, dop):
    tp = qp.shape[0]
    blk, prev, meta = _attn_specs()

    def body(sk_ref, q_ref, kc_ref, kp_ref, km_ref, vc_ref, vp_ref, vm_ref, do_ref, dq_ref, dk_ref, dv_ref, dsk_ref):
        n = pl.program_id(0)

        @pl.when(n == 0)
        def _():
            dk_ref[...] = jnp.zeros_like(dk_ref)
            dv_ref[...] = jnp.zeros_like(dv_ref)
            dsk_ref[...] = jnp.zeros_like(dsk_ref)
        masks = _attn_masks(n)
        cur = pl.ds(pl.multiple_of(n * BLOCK, BLOCK), BLOCK)
        prv = pl.ds(pl.multiple_of(jnp.maximum(n - 1, 0) * BLOCK, BLOCK), BLOCK)
        lane = lax.broadcasted_iota(jnp.int32, (1, LANES), 1)
        dsk = jnp.zeros((1, LANES), f32)
        for g in range(N_KV_HEADS):
            sl = slice(g * HEAD_DIM, (g + 1) * HEAD_DIM)
            skv, hrow = _sink_rows(sk_ref, g)
            qg = _stack_heads(q_ref, g)
            dog = _stack_heads(do_ref, g)
            ks = (kc_ref[:, sl], kp_ref[:, sl], km_ref[:, sl])
            vs = (vc_ref[:, sl], vp_ref[:, sl], vm_ref[:, sl])
            probs = _attn_probs(qg, ks[0], ks[1], ks[2], masks, skv)
            dps = [lax.dot_general(dog, v, _DIMS["nt"], preferred_element_type=f32) for v in vs]
            delta = sum(jnp.sum(p * dp, axis=-1, keepdims=True) for p, dp in zip(probs[:3], dps))
            dss = [(p * (dp - delta) * ATT_SCALE).astype(bf16) for p, dp in zip(probs[:3], dps)]
            dq = sum(jnp.dot(ds, k, preferred_element_type=f32) for ds, k in zip(dss, ks))
            for hh in range(GQA_GROUP):
                h = GQA_GROUP * g + hh
                dq_ref[:, h * HEAD_DIM:(h + 1) * HEAD_DIM] = dq[hh * BLOCK:(hh + 1) * BLOCK]
                dsk = dsk + jnp.where(lane == h, -jnp.sum(jnp.where(hrow == hh, probs[3] * delta, 0.0)), 0.0)
            for rows, p, ds in zip((cur, prv, slice(0, BLOCK)), probs[:3], dss):
                dv_ref[rows, sl] += lax.dot_general(p.astype(bf16), dog, _DIMS["tn"], preferred_element_type=f32)
                dk_ref[rows, sl] += lax.dot_general(ds, qg, _DIMS["tn"], preferred_element_type=f32)
        dsk_ref[...] += dsk

    return _pc(body, name="attn_bwd", grid=(tp // BLOCK,),
               in_specs=[pl.BlockSpec(memory_space=pltpu.SMEM), blk(D_ATT), blk(D_KV), prev, meta, blk(D_KV), prev, meta,
                         blk(D_ATT)],
               out_specs=(blk(D_ATT), _full((tp, D_KV)), _full((tp, D_KV)), _full((1, LANES))),
               out_shape=(S((tp, D_ATT), f32), S((tp, D_KV), f32), S((tp, D_KV), f32), S((1, LANES), f32)),
               compiler_params=_cparams(("arbitrary",)))(sinks, qp, kp, kp, kp, vp, vp, vp, dop)


def _seg(x, bm):
    return jnp.dot(x, bm, precision=HIGHEST, preferred_element_type=f32)


def _softplus(y):
    return jnp.maximum(y, 0.0) + jnp.log(1.0 + jnp.exp(-jnp.abs(y)))


def _prep_fn(xr, xk, xwd, xad, xgd, w0, w2, a0, a2, g2, k_k, k_a, bm):
    xw = w0 + jnp.dot(jnp.tanh(xwd), w2, preferred_element_type=f32)
    decay = jnp.exp(-jnp.exp(-_softplus(-xw) - 0.5))
    alpha = _sigmoid(a0 + jnp.dot(xad, a2, preferred_element_type=f32))
    g = jnp.dot(_sigmoid(xgd), g2, preferred_element_type=f32)
    kk = xk * k_k
    kkn = kk / jnp.maximum(jnp.sqrt(_seg(kk * kk, bm)), 1e-12)
    k2 = xk * (1.0 + (alpha - 1.0) * k_a)
    return decay, k2, -kkn, kkn * alpha, g


def _split_cols(x):
    o1, o2, o3 = 3 * D_R, 3 * D_R + LORA_W, 3 * D_R + LORA_W + LORA_A
    return x[:, 0:D_R], x[:, D_R:2 * D_R], x[:, 2 * D_R:o1], x[:, o1:o2], x[:, o2:o3], x[:, o3:RWKV_COLS]


def _shifted(sh_ref, x, halo, first, rb):
    sh_ref[0:SUBLANES, :] = jnp.where(first, 0.0, halo)
    sh_ref[SUBLANES:SUBLANES + rb, :] = x
    return sh_ref[SUBLANES - 1:SUBLANES - 1 + rb, :]


_PREP_PARAMS = ("od_w0", "od_w2", "od_a0", "od_a2", "od_g2", "od_k_k", "od_k_a")


def _rwkv_prep(pr, mu, params, bm):
    t = pr.shape[0]
    rb = _row_block8(t)
    hb = rb // SUBLANES

    def body(pr_ref, halo_ref, mu_ref, w0, w2, a0, a2, g2, kk_ref, ka_ref, bm_ref, *outs_sh):
        outs, sh_ref = outs_sh[:-1], outs_sh[-1]
        x = pr_ref[...]
        prev = _shifted(sh_ref, x, halo_ref[...], pl.program_id(0) == 0, rb)
        xr, xk, xv, xwd, xad, xgd = _split_cols(x + (prev - x) * mu_ref[...])
        bmv = bm_ref[...]
        decay, k2, a_s, b_s, g = _prep_fn(xr, xk, xwd, xad, xgd, w0[...], w2[...], a0[...], a2[...], g2[...],
                                          kk_ref[...], ka_ref[...], bmv)
        vals = (xr, xv, decay, k2, a_s, b_s, decay * xr, _seg(b_s * xr, bmv), _seg(k2 * xr, bmv), g)
        for ref, val in zip(outs, vals):
            ref[...] = val

    row = pl.BlockSpec((rb, RWKV_COLS), lambda i: (i, 0))
    halo = pl.BlockSpec((SUBLANES, RWKV_COLS), lambda i: (jnp.maximum(i * hb - 1, 0), 0))
    orow = pl.BlockSpec((rb, D_R), lambda i: (i, 0))
    return _pc(body, name="rwkv_prep", grid=(t // rb,),
               in_specs=[row, halo, _full((1, RWKV_COLS))] + [_full(p.shape) for p in params] + [_full(bm.shape)],
               out_specs=(orow,) * 10, out_shape=(S((t, D_R), f32),) * 10,
               scratch_shapes=[pltpu.VMEM((rb + SUBLANES, RWKV_COLS), f32)],
               compiler_params=_cparams(("arbitrary",)))(pr, pr, mu, *params, bm)


def _rwkv_prep_bwd(pr, mu, params, bm, cts):
    t = pr.shape[0]
    rb = _row_block8(t)
    hb = rb // SUBLANES
    counts = [len(c) for c in cts]
    flat = [a for c in cts for a in c]

    def body(pr_ref, halo_ref, mu_ref, w0, w2, a0, a2, g2, kk_ref, ka_ref, bm_ref, *rest):
        ct_refs, rest = rest[:len(flat)], rest[len(flat):]
        dx_ref, dmu_ref = rest[0], rest[1]
        dpar_refs, sh_ref = rest[2:9], rest[9]

        @pl.when(pl.program_id(0) == 0)
        def _():
            dmu_ref[...] = jnp.zeros_like(dmu_ref)
            for r in dpar_refs:
                r[...] = jnp.zeros_like(r)
        sums, pos = [], 0
        for c in counts:
            sums.append(sum(r[...] for r in ct_refs[pos:pos + c]))
            pos += c
        x = pr_ref[...]
        prev = _shifted(sh_ref, x, halo_ref[...], pl.program_id(0) == 0, rb)
        xr, xk, xv, xwd, xad, xgd = _split_cols(x + (prev - x) * mu_ref[...])
        bmv = bm_ref[...]
        _, vjp = jax.vjp(lambda *a: _prep_fn(*a, bmv), xr, xk, xwd, xad, xgd, w0[...], w2[...], a0[...], a2[...],
                         g2[...], kk_ref[...], ka_ref[...])
        grads = vjp(tuple(sums[:5]))
        dxr, dxk, dxwd, dxad, dxgd = grads[:5]
        o1, o2, o3 = 3 * D_R, 3 * D_R + LORA_W, 3 * D_R + LORA_W + LORA_A
        dx_ref[:, 0:D_R] = dxr + sums[5]
        dx_ref[:, D_R:2 * D_R] = dxk
        dx_ref[:, 2 * D_R:o1] = sums[6]
        dx_ref[:, o1:o2] = dxwd
        dx_ref[:, o2:o3] = dxad
        dx_ref[:, o3:RWKV_COLS] = dxgd
        dmu_ref[...] += jnp.sum(dx_ref[...] * (prev - x), axis=0, keepdims=True)
        for r, gval in zip(dpar_refs, grads[5:]):
            r[...] += gval

    row = pl.BlockSpec((rb, RWKV_COLS), lambda i: (i, 0))
    halo = pl.BlockSpec((SUBLANES, RWKV_COLS), lambda i: (jnp.maximum(i * hb - 1, 0), 0))
    crow = pl.BlockSpec((rb, D_R), lambda i: (i, 0))
    return _pc(body, name="rwkv_prep_bwd", grid=(t // rb,),
               in_specs=[row, halo, _full((1, RWKV_COLS))] + [_full(p.shape) for p in params] + [_full(bm.shape)]
               + [crow] * len(flat),
               out_specs=(row, _full((1, RWKV_COLS))) + tuple(_full(p.shape) for p in params),
               out_shape=(S((t, RWKV_COLS), f32), S((1, RWKV_COLS), f32)) + tuple(S(p.shape, f32) for p in params),
               scratch_shapes=[pltpu.VMEM((rb + SUBLANES, RWKV_COLS), f32)],
               compiler_params=_cparams(("arbitrary",)))(pr, pr, mu, *params, bm, *flat)


def _shift_bwd(dxs, mu):
    t = dxs.shape[0]
    rb = _row_block8(t)
    hb = rb // SUBLANES
    nblk = t // rb

    def body(dx_ref, halo_ref, mu_ref, o_ref, sh_ref):
        dx = dx_ref[...]
        sh_ref[0:rb, :] = dx
        sh_ref[rb:rb + SUBLANES, :] = jnp.where(pl.program_id(0) == nblk - 1, 0.0, halo_ref[...])
        m = mu_ref[...]
        o_ref[...] = dx * (1.0 - m) + sh_ref[1:1 + rb, :] * m

    row = pl.BlockSpec((rb, RWKV_COLS), lambda i: (i, 0))
    halo = pl.BlockSpec((SUBLANES, RWKV_COLS), lambda i: (jnp.minimum((i + 1) * hb, t // SUBLANES - 1), 0))
    return _pc(body, name="rwkv_shift_bwd", grid=(nblk,), in_specs=[row, halo, _full((1, RWKV_COLS))],
               out_specs=row, out_shape=S((t, RWKV_COLS), f32),
               scratch_shapes=[pltpu.VMEM((rb + SUBLANES, RWKV_COLS), f32)],
               compiler_params=_cparams(("arbitrary",)))(dxs, dxs, mu)


def _post_fn(y, xr, k2, xv, g, lg, lb, rk, bm):
    inv_n = 1.0 / HEAD_DIM
    yc = y - _seg(y, bm) * inv_n
    var = _seg(yc * yc, bm) * inv_n
    yn = yc * lax.rsqrt(var + RWKV_GN_EPS) * lg + lb
    return (yn + _seg(xr * k2 * rk, bm) * xv) * g


def _rwkv_post(y, xr, k2, xv, g, lg, lb, rk, bm):
    t = y.shape[0]
    rb = _row_block8(t)

    def body(y_ref, xr_ref, k2_ref, xv_ref, g_ref, lg_ref, lb_ref, rk_ref, bm_ref, o_ref):
        o_ref[...] = _post_fn(y_ref[...], xr_ref[...], k2_ref[...], xv_ref[...], g_ref[...], lg_ref[...], lb_ref[...],
                              rk_ref[...], bm_ref[...])

    row = pl.BlockSpec((rb, D_R), lambda i: (i, 0))
    vec = _full((1, D_R))
    return _pc(body, name="rwkv_post", grid=(t // rb,), in_specs=[row] * 5 + [vec] * 3 + [_full(bm.shape)],
               out_specs=row, out_shape=S((t, D_R), f32),
               compiler_params=_cparams(("arbitrary",)))(y, xr, k2, xv, g, lg, lb, rk, bm)


def _rwkv_post_bwd(dy1, y, xr, k2, xv, g, lg, lb, rk, bm):
    t = y.shape[0]
    rb = _row_block8(t)

    def body(dy_ref, y_ref, xr_ref, k2_ref, xv_ref, g_ref, lg_ref, lb_ref, rk_ref, bm_ref, *outs):
        @pl.when(pl.program_id(0) == 0)
        def _():
            for r in outs[5:]:
                r[...] = jnp.zeros_like(r)
        bmv = bm_ref[...]
        _, vjp = jax.vjp(lambda *a: _post_fn(*a, bmv), y_ref[...], xr_ref[...], k2_ref[...], xv_ref[...], g_ref[...],
                         lg_ref[...], lb_ref[...], rk_ref[...])
        grads = vjp(dy_ref[...])
        for r, gval in zip(outs[:5], grads[:5]):
            r[...] = gval
        for r, gval in zip(outs[5:], grads[5:]):
            r[...] += gval

    row = pl.BlockSpec((rb, D_R), lambda i: (i, 0))
    vec = _full((1, D_R))
    return _pc(body, name="rwkv_post_bwd", grid=(t // rb,),
               in_specs=[pl.BlockSpec((rb, D_R), lambda i: (i, 1))] + [row] * 5 + [vec] * 3 + [_full(bm.shape)],
               out_specs=(row,) * 5 + (vec,) * 3, out_shape=(S((t, D_R), f32),) * 5 + (S((1, D_R), f32),) * 3,
               compiler_params=_cparams(("arbitrary",)))(dy1, y, xr, k2, xv, g, lg, lb, rk, bm)


def _seg2(x, bb):
    hi = x.astype(bf16)
    lo = (x - hi.astype(f32)).astype(bf16)
    return jnp.dot(jnp.concatenate([hi, lo], axis=1), bb, preferred_element_type=f32)


def _row4(rows, j):
    return jnp.concatenate([jnp.broadcast_to(rows[j:j + 1, p * LANES:(p + 1) * LANES], (HEAD_DIM, LANES))
                            for p in range(4)], axis=0)


def _scan_consts():
    lane_group = jnp.arange(LANES) // HEAD_DIM
    b128 = (lane_group[:, None] == lane_group[None, :]).astype(bf16)
    bb = jnp.concatenate([b128, b128], axis=0)
    qsel = (jnp.arange(PAIR_ROWS)[:, None] % HEAD_DIM == jnp.arange(LANES)[None, :] % HEAD_DIM).astype(f32)
    return bb, qsel


def _store_cols(acc_ref, o_ref, tc):
    for p in range(4):
        blk = acc_ref[p * HEAD_DIM:(p + 1) * HEAD_DIM, :].T
        o_ref[:, (2 * p) * HEAD_DIM:(2 * p + 1) * HEAD_DIM] = blk[0:tc]
        o_ref[:, (2 * p + 1) * HEAD_DIM:(2 * p + 2) * HEAD_DIM] = blk[HEAD_DIM:HEAD_DIM + tc]


PAIR_GROUP = 2 * SUBLANES


def _rwkv_pairs(w, a, b, k, wr, bm):
    t = w.shape[0]
    rb = _row_block8(t)

    def body(w_ref, a_ref, b_ref, k_ref, wr_ref, bm_ref, *outs_sh):
        outs, sh_ref = outs_sh[:-1], outs_sh[-1]

        def second(ref):
            sh_ref[0:rb, :] = ref[...]
            sh_ref[rb:rb + SUBLANES, :] = jnp.zeros((SUBLANES, D_R), f32)
            return sh_ref[1:1 + rb, :]

        w1, b1, k1 = w_ref[...], b_ref[...], k_ref[...]
        w2, a2, wr2 = second(w_ref), second(a_ref), second(wr_ref)
        bmv = bm_ref[...]
        vals = (w1 * a2, w1 * wr2, w1 * w2, b1 * w2, k1 * w2, _seg(b1 * a2, bmv), _seg(k1 * a2, bmv),
                _seg(b1 * wr2, bmv), _seg(k1 * wr2, bmv))
        for ref, val in zip(outs, vals):
            ref[...] = val

    row = pl.BlockSpec((rb, D_R), lambda i: (i, 0))
    return _pc(body, name="rwkv_pairs", grid=(t // rb,), in_specs=[row] * 5 + [_full(bm.shape)],
               out_specs=(row,) * 9, out_shape=(S((t, D_R), f32),) * 9,
               scratch_shapes=[pltpu.VMEM((rb + SUBLANES, D_R), f32)],
               compiler_params=_cparams(("arbitrary",)))(w, a, b, k, wr, bm)


def _wkv_fwd(w, k, v, a, b, wr, br, kr, pairs):
    t = w.shape[0]
    tc = SCAN_CHUNK
    bb, qsel = _scan_consts()

    def body(*refs):
        step_refs, pair_refs = refs[0:8], refs[8:17]
        bb_ref, q_ref, y_ref, st_ref, sa_ref, vb_ref, s_scr, yacc = refs[17:]

        @pl.when(pl.program_id(0) == 0)
        def _():
            s_scr[...] = jnp.zeros_like(s_scr)
        bbv, qv = bb_ref[...], q_ref[...]
        lane64 = lax.broadcasted_iota(jnp.int32, (PAIR_ROWS, LANES), 1) % HEAD_DIM

        def halves(x):
            hi = x.astype(bf16)
            return jnp.concatenate([hi, (x - hi.astype(f32)).astype(bf16)], axis=1)

        def group(gi, s):
            base = pl.multiple_of(gi * PAIR_GROUP, PAIR_GROUP)
            w16, k16, v16, a16, b16, wr16, br16, kr16 = (
                (ref[pl.ds(base, SUBLANES), :], ref[pl.ds(base + SUBLANES, SUBLANES), :]) for ref in step_refs)
            a2p, r2p, w12p, b1wp, k1wp, betap, kappap, bwrp, kwrp = (
                (ref[pl.ds(base, SUBLANES), :], ref[pl.ds(base + SUBLANES, SUBLANES), :]) for ref in pair_refs)
            vh16 = tuple(x.astype(bf16).astype(f32) for x in v16)
            vl16 = tuple(x - h for x, h in zip(v16, vh16))
            step = lambda arr, j: _row4(arr[j // SUBLANES], j % SUBLANES)
            for q in range(SUBLANES):
                j1, j2 = 2 * q, 2 * q + 1
                t1 = base + j1
                lhs = [halves(jnp.concatenate([s * step(a16, j1), s * step(a2p, j1), s * step(wr16, j1), s * step(r2p, j1)],
                                              axis=0))]
                for j in (j1, j2):
                    lhs.append(jnp.concatenate([(qv * step(vh16, j)).astype(bf16), (qv * step(vl16, j)).astype(bf16)], axis=1))
                r = jnp.dot(jnp.concatenate(lhs, axis=0), bbv, preferred_element_type=f32)
                sa1, p2, z1, z2, vb1, vb2 = (r[n * PAIR_ROWS:(n + 1) * PAIR_ROWS] for n in range(6))
                sa2 = p2 + sa1 * step(betap, j1) + vb1 * step(kappap, j1)
                y1 = z1 + sa1 * step(br16, j1) + vb1 * step(kr16, j1)
                y2 = (z2 + sa1 * step(bwrp, j1) + vb1 * step(kwrp, j1)) + (sa2 * step(br16, j2) + vb2 * step(kr16, j2))
                yacc[...] = jnp.where(lane64 == t1, y1, jnp.where(lane64 == t1 + 1, y2, yacc[...]))
                st_ref[t1] = s
                st_ref[t1 + 1] = s * step(w16, j1) + sa1 * step(b16, j1) + vb1 * step(k16, j1)
                sa_ref[t1] = sa1
                sa_ref[t1 + 1] = sa2
                vb_ref[t1] = vb1
                vb_ref[t1 + 1] = vb2
                s = ((s * step(w12p, j1) + sa1 * step(b1wp, j1)) + vb1 * step(k1wp, j1)) + (sa2 * step(b16, j2) + vb2 * step(k16, j2))
            return s

        s_scr[...] = lax.fori_loop(0, tc // PAIR_GROUP, group, s_scr[...])
        _store_cols(yacc, y_ref, tc)

    row = pl.BlockSpec((tc, D_R), lambda c: (c, 0))
    tiles = pl.BlockSpec((tc, PAIR_ROWS, LANES), lambda c: (c, 0, 0))
    return _pc(body, name="wkv_fwd", grid=(t // tc,),
               in_specs=[row] * 17 + [_full(bb.shape), _full(qsel.shape)],
               out_specs=(row, tiles, tiles, tiles),
               out_shape=(S((t, D_R), f32),) + (S((t, PAIR_ROWS, LANES), f32),) * 3,
               scratch_shapes=[pltpu.VMEM((PAIR_ROWS, LANES), f32), pltpu.VMEM((PAIR_ROWS, LANES), f32)],
               compiler_params=_cparams(("arbitrary",)))(w, k, v, a, b, wr, br, kr, *pairs, bb, qsel)


def _wkv_bwd(sprev, sab, vbb, w, k, a, b, r, dy):
    t = w.shape[0]
    tc = SCAN_CHUNK
    nc = t // tc
    bb, qsel = _scan_consts()

    def body(st_ref, sa_ref, vb_ref, w_ref, k_ref, a_ref, b_ref, r_ref, dy_ref, bb_ref, q_ref,
             dr_ref, dw_ref, dk_ref, dv_ref, da_ref, db_ref, g_scr, dvacc, rows_scr):
        @pl.when(pl.program_id(0) == 0)
        def _():
            g_scr[...] = jnp.zeros_like(g_scr)
        bbv, qv = bb_ref[...], q_ref[...]
        lane64 = lax.broadcasted_iota(jnp.int32, (PAIR_ROWS, LANES), 1) % HEAD_DIM
        outs = (dr_ref, dw_ref, db_ref, dk_ref, da_ref)

        def colsums(slot, j, x):
            for p in range(4):
                rows_scr[slot, j:j + 1, p * LANES:(p + 1) * LANES] = jnp.sum(x[p * HEAD_DIM:(p + 1) * HEAD_DIM], axis=0,
                                                                           keepdims=True)

        def group(i, g):
            base = pl.multiple_of((tc // SUBLANES - 1 - i) * SUBLANES, SUBLANES)
            w8, k8, a8, b8, r8, dy8 = (ref[pl.ds(base, SUBLANES), :] for ref in (w_ref, k_ref, a_ref, b_ref, r_ref, dy_ref))
            for j in reversed(range(SUBLANES)):
                tt = base + j
                sp, u, vb = st_ref[tt], sa_ref[tt], vb_ref[tt]
                a4, b4, w4, k4 = _row4(a8, j), _row4(b8, j), _row4(w8, j), _row4(k8, j)
                dyb = _seg2(qv * _row4(dy8, j), bbv)
                s_t = sp * w4 + u * b4 + vb * k4
                g = g + dyb * _row4(r8, j)
                rr2 = _seg2(jnp.concatenate([g * b4, g * k4], axis=0), bbv)
                du, dvb = rr2[0:PAIR_ROWS], rr2[PAIR_ROWS:2 * PAIR_ROWS]
                for slot, val in enumerate((s_t * dyb, g * sp, g * u, g * vb, sp * du)):
                    colsums(slot, j, val)
                dvacc[...] = jnp.where(lane64 == tt, dvb, dvacc[...])
                g = g * w4 + du * a4
            for slot, ref in enumerate(outs):
                ref[pl.ds(base, SUBLANES), :] = rows_scr[slot]
            return g

        g_scr[...] = lax.fori_loop(0, tc // SUBLANES, group, g_scr[...])
        _store_cols(dvacc, dv_ref, tc)

    row = pl.BlockSpec((tc, D_R), lambda c: (nc - 1 - c, 0))
    tiles = pl.BlockSpec((tc, PAIR_ROWS, LANES), lambda c: (nc - 1 - c, 0, 0))
    return _pc(body, name="wkv_bwd", grid=(nc,),
               in_specs=[tiles] * 3 + [row] * 6 + [_full(bb.shape), _full(qsel.shape)],
               out_specs=(row,) * 6, out_shape=(S((t, D_R), f32),) * 6,
               scratch_shapes=[pltpu.VMEM((PAIR_ROWS, LANES), f32), pltpu.VMEM((PAIR_ROWS, LANES), f32),
                               pltpu.VMEM((5, SUBLANES, D_R), f32)],
               compiler_params=_cparams(("arbitrary",)))(sprev, sab, vbb, w, k, a, b, r, dy, bb, qsel)


def _rope_tables(t):
    half = HEAD_DIM // 2
    inv = ROPE_THETA ** (-jnp.arange(half, dtype=f32) / half)
    ang = jnp.arange(t, dtype=f32)[:, None] * inv[None, :]
    cos, sin = jnp.cos(ang), jnp.sin(ang)
    return jnp.concatenate([cos, cos], axis=1), jnp.concatenate([-sin, sin], axis=1)


def _head_matrix():
    grp = jnp.arange(D_R) // HEAD_DIM
    return (grp[:, None] == grp[None, :]).astype(f32)


def _ffn_fwd(h, g, get_w, conv_w, conv_b, i):
    hf = _rms_fwd(h, g, f"ffn{i}_norm")
    w_up_t = get_w(f"ff{i}_up", hf)
    u = _mm(hf, w_up_t, "nt", f"ffn{i}_up")
    z = _ffn_mid(u, conv_w, conv_b, f"ffn{i}_mid")
    w_down = get_w(f"ff{i}_down", z)
    return _mm(z, w_down, "nn", f"ffn{i}_down", res=h), (hf, u, z), w_up_t, w_down


def _ffn_bwd(dh, h, saved, g, w_up_t, conv_w, conv_b, w_down, i, put_g):
    hf, u, z = saved
    dz = _mm(dh, w_down, "nt", f"ffn{i}_dz")
    g_down = _mm(z, dh, "tn", f"ffn{i}_gdown", out_dtype=GRAD_WIRE_DTYPE)
    du, g_conv, g_convb = _ffn_mid_bwd(dz, u, conv_w, conv_b, f"ffn{i}_mid_bwd")
    g_up_t = _mm(du, hf, "tn", f"ffn{i}_gup", out_dtype=GRAD_WIRE_DTYPE)
    tok = put_g(f"ff{i}", [g_up_t, g_down])
    dhf = _mm(du, w_up_t, "nn", f"ffn{i}_dhf")
    dh_in, g_norm = _rms_bwd(dhf, h, g + tok, dh, f"ffn{i}_norm_bwd")
    return dh_in, dict(conv=g_conv, conv_b=g_convb, norm=g_norm)


def _local_step(x, target, W, get_w, put_g, tok0):
    t = N_META + x.shape[0]
    c64, s64 = _rope_tables(t)
    bm = _head_matrix()
    h0 = jnp.concatenate([W["meta_tokens"], x], axis=0)

    ev_w_in_t, ev_w_out = get_w("ev_in", None), get_w("ev_out", None)
    hn0 = _rms_fwd(h0, W["norm_mix"][0] + tok0, "mix0_norm")
    p0 = _mm(hn0, ev_w_in_t, "nt", "ev_in")
    uc = _ev_a_conv(p0, W["ev_conv_a"])
    y0 = jnp.concatenate([_ev_a_norm(uc, W["ev_ln_a_g"], W["ev_ln_a_b"]), _ev_b(p0, W["ev_conv_b"])], axis=1)
    h1 = _mm(y0, ev_w_out, "nn", "ev_out", res=h0)
    h2, ffn0, ff0_up_t, ff0_down = _ffn_fwd(h1, W["norm_ffn"][0], get_w, W["ff_conv"][0], W["ff_conv_b"][0], 0)

    hn1 = _rms_fwd(h2, W["norm_mix"][1], "mix1_norm")
    od_w_in_t = get_w("od_in", hn1)
    p1 = _mm(hn1, od_w_in_t, "nt", "od_in")
    pr = p1[:, ATT_COLS:]
    qp, kp, vp = _rope_pack(p1[:, :ATT_COLS], c64, s64)
    op = _attn_fwd(qp, kp, vp, W["od_sinks"])
    prep_params = [W[n] for n in _PREP_PARAMS]
    xr, xv, decay, k2, a_s, b_s, wr, br, kr, gate = _rwkv_prep(pr, W["od_mu"], prep_params, bm)
    pairs = _rwkv_pairs(decay, a_s, b_s, k2, wr, bm)
    ysc, sprev, sab, vbb = _wkv_fwd(decay, k2, xv, a_s, b_s, wr, br, kr, pairs)
    rk = W["od_r_k"].reshape(1, D_R)
    yr = _rwkv_post(ysc, xr, k2, xv, gate, W["od_lnx_g"], W["od_lnx_b"], rk, bm)
    y1 = jnp.concatenate([op[ATT_PAD:], yr.astype(bf16)], axis=1)
    od_w_out = get_w("od_out", y1)
    h3 = _mm(y1, od_w_out, "nn", "od_out", res=h2)
    h4, ffn1, ff1_up_t, ff1_down = _ffn_fwd(h3, W["norm_ffn"][1], get_w, W["ff_conv"][1], W["ff_conv_b"][1], 1)

    tgt = jnp.concatenate([jnp.zeros((N_META, D_MODEL), f32), target], axis=0)
    loss, dh4, g_norm_final = _final_loss(h4, W["norm_final"], tgt)

    dh3, gf1 = _ffn_bwd(dh4, h3, ffn1, W["norm_ffn"][1], ff1_up_t, W["ff_conv"][1], W["ff_conv_b"][1], ff1_down, 1, put_g)
    dy1 = _mm(dh3, od_w_out, "nt", "od_dy")
    g_od_w_out = _mm(y1, dh3, "tn", "od_gout", out_dtype=GRAD_WIRE_DTYPE)
    dysc, dxr_p, dk2_p, dxv_p, dgate, g_lnx_g, g_lnx_b, g_rk = _rwkv_post_bwd(
        dy1, ysc, xr, k2, xv, gate, W["od_lnx_g"], W["od_lnx_b"], rk, bm)
    dr, dw, dk, dv, da, db = _wkv_bwd(sprev, sab, vbb, decay, k2, a_s, b_s, xr, dysc)
    prep_grads = _rwkv_prep_bwd(pr, W["od_mu"], prep_params, bm,
                                [[dw], [dk, dk2_p], [da], [db], [dgate], [dr, dxr_p], [dv, dxv_p]])
    dxs, g_mu = prep_grads[0], prep_grads[1]
    dpr = _shift_bwd(dxs, W["od_mu"])
    dop = jnp.concatenate([jnp.zeros((ATT_PAD, D_ATT), f32), dy1[:, :D_ATT]], axis=0).astype(bf16)
    dqp, dkp, dvp, dsk = _attn_bwd(qp, kp, vp, W["od_sinks"], dop)
    dp1 = jnp.concatenate([_rope_bwd(dqp, dkp, dvp, c64, s64), dpr], axis=1)
    g_od_w_in_t = _mm(dp1, hn1, "tn", "od_gin", out_dtype=GRAD_WIRE_DTYPE)
    tok = put_g("od", [g_od_w_in_t, g_od_w_out])
    dhn1 = _mm(dp1, od_w_in_t, "nn", "od_dhn")
    dh2, g_norm_mix1 = _rms_bwd(dhn1, h2, W["norm_mix"][1] + tok, dh3, "mix1_norm_bwd")

    dh1, gf0 = _ffn_bwd(dh2, h1, ffn0, W["norm_ffn"][0], ff0_up_t, W["ff_conv"][0], W["ff_conv_b"][0], ff0_down, 0, put_g)
    dy0 = _mm(dh1, ev_w_out, "nt", "ev_dy")
    g_ev_w_out = _mm(y0, dh1, "tn", "ev_gout", out_dtype=GRAD_WIRE_DTYPE)
    duc, g_ln_g, g_ln_b = _ev_a_norm_bwd(dy0, uc, W["ev_ln_a_g"], W["ev_ln_a_b"])
    dav, dag, g_conv_a = _ev_a_conv_bwd(duc, p0, W["ev_conv_a"])
    dgb, dgc, dxi, g_conv_b = _ev_b_bwd(dy0, p0, W["ev_conv_b"])
    dp0 = jnp.concatenate([dav, dag, dgb, dgc, dxi], axis=1)
    g_ev_w_in_t = _mm(dp0, hn0, "tn", "ev_gin", out_dtype=GRAD_WIRE_DTYPE)
    tok = put_g("ev", [g_ev_w_in_t, g_ev_w_out])
    dhn0 = _mm(dp0, ev_w_in_t, "nn", "ev_dhn")
    dh0, g_norm_mix0 = _rms_bwd(dhn0, h0, W["norm_mix"][0] + tok, dh1, "mix0_norm_bwd")

    G = dict(
        meta_tokens=dh0[:N_META], norm_mix=jnp.concatenate([g_norm_mix0, g_norm_mix1], axis=0),
        norm_ffn=jnp.concatenate([gf0["norm"], gf1["norm"]], axis=0), norm_final=g_norm_final.reshape(D_MODEL),
        ev_conv_a=g_conv_a, ev_ln_a_g=g_ln_g, ev_ln_a_b=g_ln_b, ev_conv_b=g_conv_b,
        od_sinks=dsk[:, :N_Q_HEADS], od_mu=g_mu,
        od_lnx_g=g_lnx_g, od_lnx_b=g_lnx_b, od_r_k=g_rk.reshape(N_Q_HEADS, HEAD_DIM),
        ff_conv=jnp.stack([gf0["conv"], gf1["conv"]]), ff_conv_b=jnp.concatenate([gf0["conv_b"], gf1["conv_b"]], axis=0),
    )
    for name, gval in zip(_PREP_PARAMS, prep_grads[2:]):
        G[name] = gval
    return loss, dh0[N_META:], G


HBM = pl.BlockSpec(memory_space=pl.ANY)


def _mesh_pos():
    return lax.axis_index("x"), lax.axis_index("y"), lax.axis_index("c")


def _dev(px, py, pc):
    return 4 * px + 2 * py + pc


def _all_gather(xs, name):
    n = len(xs)

    def body(*refs):
        x_refs, o_refs = refs[:n], refs[n:2 * n]
        send_sems, recv_sems, local_sems = refs[2 * n:]
        x, y, c = _mesh_pos()
        me, sibling = (x, y, c), (x, y, 1 - c)
        chips = [(1 - x, y), (x, 1 - y), (1 - x, 1 - y)]

        def copy(i, k, block, to, from_input=False):
            dst = o_refs[i].at[_dev(*block)]
            return pltpu.make_async_remote_copy(src_ref=x_refs[i] if from_input else dst, dst_ref=dst,
                                                send_sem=send_sems.at[i, k], recv_sem=recv_sems.at[i, k],
                                                device_id=to, device_id_type=MESH)

        mine = [pltpu.make_async_copy(x_refs[i], o_refs[i].at[_dev(*me)], local_sems.at[i]) for i in range(n)]
        for cp in mine:
            cp.start()
        first = []
        for i in range(n):
            first.append(copy(i, 0, me, sibling, True))
            first += [copy(i, 1 + j, me, (*chip, c), True) for j, chip in enumerate(chips)]
        for cp in first:
            cp.start()
        passed = []
        for j, chip in enumerate(chips):
            for i in range(n):
                copy(i, 1 + j, (*chip, c), me).wait_recv()
                fwd = copy(i, 4 + j, (*chip, c), sibling)
                fwd.start()
                passed.append(fwd)
        for i in range(n):
            copy(i, 0, sibling, me).wait_recv()
            for j, chip in enumerate(chips):
                copy(i, 4 + j, (*chip, 1 - c), me).wait_recv()
        for cp in first + passed:
            cp.wait_send()
        for cp in mine:
            cp.wait()

    return _pc(body, name=name, in_specs=[HBM] * n, out_specs=tuple([HBM] * n),
               out_shape=tuple(S((N_DEV,) + x.shape, x.dtype) for x in xs),
               scratch_shapes=[pltpu.SemaphoreType.DMA((n, 7)), pltpu.SemaphoreType.DMA((n, 7)),
                               pltpu.SemaphoreType.DMA((n,))])(*xs)


HBM_SPEC = pl.BlockSpec(memory_space=pltpu.HBM)
SEM_SPEC = pl.BlockSpec(memory_space=pltpu.SEMAPHORE)
DATAFLOW = pltpu.SideEffectType.DATAFLOW_SIDE_EFFECTING
_PEER_FLIPS = ((1, 0, 0), (0, 1, 0), (1, 1, 0), (1, 0, 1), (0, 1, 1), (1, 1, 1), (0, 0, 1))
N_PEERS = len(_PEER_FLIPS)


def _peers(x, y, c):
    return [((1 - x) if fx else x, (1 - y) if fy else y, (1 - c) if fc else c) for fx, fy, fc in _PEER_FLIPS]


def _xchg_start(srcs, lands, scatter, name):
    n = len(srcs)

    def body(*refs):
        src_refs, land_refs = refs[:n], refs[n:2 * n]
        send_sems, recv_sems, token = refs[2 * n], refs[2 * n + 1], refs[-1]
        x, y, c = _mesh_pos()
        me = _dev(x, y, c)
        for i in range(n):
            for k, peer in enumerate(_peers(x, y, c)):
                pltpu.make_async_remote_copy(src_ref=src_refs[i].at[_dev(*peer)] if scatter else src_refs[i],
                                             dst_ref=land_refs[i].at[me], send_sem=send_sems.at[i * N_PEERS + k],
                                             recv_sem=recv_sems.at[i * N_PEERS + k], device_id=peer, device_id_type=MESH).start()
        token[...] = jnp.zeros_like(token)

    arrs = list(srcs) + list(lands)
    outs = _pc(body, name=name,
               out_shape=(pltpu.SemaphoreType.DMA((n * N_PEERS,)), pltpu.SemaphoreType.DMA((n * N_PEERS,)),
                          *[pltpu.HBM(a.shape, a.dtype) for a in arrs], S((SUBLANES, LANES), f32)),
               in_specs=[HBM_SPEC] * (2 * n),
               out_specs=(SEM_SPEC, SEM_SPEC, *[HBM_SPEC] * (2 * n), pl.BlockSpec(memory_space=pltpu.VMEM)),
               input_output_aliases={i: 2 + i for i in range(2 * n)},
               compiler_params=pltpu.CompilerParams(has_side_effects=DATAFLOW))(
        *[pltpu.with_memory_space_constraint(a, pltpu.HBM) for a in arrs])
    return (outs[0], outs[1], list(outs[2:2 + n]), list(outs[2 + n:2 + 2 * n]), scatter), outs[-1]


def _xchg_wait(handle, after, name):
    send_sems, recv_sems, srcs, lands, scatter = handle
    n = len(srcs)

    def body(*refs):
        src_refs, land_refs = refs[:n], refs[n:2 * n]
        send, recv = refs[2 * n], refs[2 * n + 1]
        x, y, c = _mesh_pos()
        for i in range(n):
            for k in range(N_PEERS):
                cp = pltpu.make_async_remote_copy(src_ref=src_refs[i].at[0] if scatter else src_refs[i],
                                                  dst_ref=land_refs[i].at[0], send_sem=send.at[i * N_PEERS + k],
                                                  recv_sem=recv.at[i * N_PEERS + k],
                                                  device_id=(x, y, c), device_id_type=MESH)
                cp.wait_send()
                cp.wait_recv()

    arrs = srcs + lands
    outs = _pc(body, name=name, out_shape=tuple(pltpu.HBM(a.shape, a.dtype) for a in arrs),
               in_specs=[HBM_SPEC] * (2 * n) + [SEM_SPEC, SEM_SPEC, pl.BlockSpec(memory_space=pl.ANY)],
               out_specs=tuple([HBM_SPEC] * (2 * n)), input_output_aliases={i: i for i in range(2 * n)},
               compiler_params=pltpu.CompilerParams(has_side_effects=DATAFLOW))(*arrs, send_sems, recv_sems, after)
    return list(outs[:n]), list(outs[n:])


def _rs_sum(g, land, me_vec, name):
    _, r, cols = g.shape
    tr = _divisor_block(r, 16, min(r, 352))

    def body(me_ref, g_ref, *rest):
        o_ref = rest[-1]
        acc = g_ref[0].astype(f32)
        for l_ref in rest[:-1]:
            acc = acc + l_ref[0].astype(f32)
        o_ref[...] = acc

    blk = lambda f: pl.BlockSpec((1, tr, cols), f)
    grid_spec = pltpu.PrefetchScalarGridSpec(
        num_scalar_prefetch=1, grid=(r // tr,),
        in_specs=[blk(lambda i, me_ref: (me_ref[0], i, 0))]
        + [blk(lambda i, me_ref, k=k: ((me_ref[0] + k) % N_DEV, i, 0)) for k in range(1, N_DEV)],
        out_specs=pl.BlockSpec((tr, cols), lambda i, me_ref: (i, 0)))
    return _pc(body, name=name, grid_spec=grid_spec, out_shape=S((r, cols), f32),
               compiler_params=_cparams(("arbitrary",)))(me_vec, g, *([land] * (N_DEV - 1)))


def _sum_devices(a):
    def body(a_ref, o_ref):
        acc = a_ref[0]
        for d in range(1, N_DEV):
            acc = acc + a_ref[d]
        o_ref[...] = acc

    return _pc(body, name="sum_small_grads", grid=(1,), in_specs=[_full(a.shape)], out_specs=_full(a.shape[1:]),
               out_shape=S(a.shape[1:], a.dtype), compiler_params=_cparams(("arbitrary",)))(a)


def _adamw(w, m, v, g, name):
    shape = w.shape
    w2, m2, v2, g2 = (a.reshape(-1, shape[-1]) for a in (w, m, v, g))
    rows, cols = w2.shape
    tr = rows if rows % SUBLANES else _divisor_block(rows, SUBLANES, max(SUBLANES, min(rows, ADAMW_BLOCK_ELEMS // cols)))
    c1, c2 = 1.0 - ADAM_B1 ** ADAM_STEP, 1.0 - ADAM_B2 ** ADAM_STEP

    def body(w_ref, m_ref, v_ref, g_ref, d_ref, nm_ref, nv_ref):
        gv = g_ref[...]
        nm = ADAM_B1 * m_ref[...] + (1.0 - ADAM_B1) * gv
        nv = ADAM_B2 * v_ref[...] + (1.0 - ADAM_B2) * (gv * gv)
        d_ref[...] = -ADAM_LR * ((nm / c1) / (jnp.sqrt(nv / c2) + ADAM_EPS) + ADAM_WD * w_ref[...])
        nm_ref[...] = nm
        nv_ref[...] = nv

    blk = pl.BlockSpec((tr, cols), lambda i: (i, 0))
    outs = _pc(body, name=name, grid=(rows // tr,), in_specs=[blk] * 4, out_specs=(blk,) * 3,
               out_shape=(S((rows, cols), f32),) * 3, compiler_params=_cparams(("arbitrary",)))(w2, m2, v2, g2)
    return tuple(o.reshape(shape) for o in outs)


_WEIGHTS = ("meta_tokens", "norm_mix", "norm_ffn", "norm_final", "ev_w_in", "ev_conv_a", "ev_ln_a_g", "ev_ln_a_b",
            "ev_conv_b", "ev_w_out", "od_w_in", "od_sinks", "od_mu", "od_w0", "od_w2", "od_a0", "od_a2", "od_g2",
            "od_k_k", "od_k_a", "od_r_k", "od_lnx_g", "od_lnx_b", "od_w_out", "ff_w_up", "ff_conv", "ff_conv_b", "ff_w_down")
_SMALL_SHARDED = (("meta_tokens", 1), ("ev_conv_a", 2), ("ev_conv_b", 2), ("od_mu", 1), ("od_w0", 1), ("od_w2", 2),
                  ("od_a0", 1), ("od_a2", 2), ("od_g2", 2), ("od_k_k", 1), ("od_k_a", 1), ("od_lnx_g", 1),
                  ("od_lnx_b", 1), ("ff_conv", 2))
_SMALL_REPLICATED = ("norm_mix", "norm_ffn", "norm_final", "ev_ln_a_g", "ev_ln_a_b", "od_sinks", "od_r_k", "ff_conv_b")
SLAB_UNIT = SUBLANES * LANES


def _pack(arrs):
    flat = jnp.concatenate([a.reshape(-1).astype(f32) for a in arrs])
    pad = (-flat.shape[0]) % SLAB_UNIT
    return jnp.pad(flat, (0, pad)).reshape(-1, LANES)


def _unpack(flat, shapes):
    out, off = [], 0
    for shp in shapes:
        size = 1
        for s in shp:
            size *= s
        out.append(flat[..., off:off + size].reshape(flat.shape[:-1] + tuple(shp)))
        off += size
    return out


def _full_shape(shape, axis):
    return tuple(N_DEV * s if i == axis else s for i, s in enumerate(shape))


def kernel(x, meta_tokens, norm_mix, norm_ffn, norm_final, ev_w_in, ev_conv_a, ev_ln_a_g, ev_ln_a_b, ev_conv_b, ev_w_out, od_w_in, od_sinks, od_mu, od_w0, od_w2, od_a0, od_a2, od_g2, od_k_k, od_k_a, od_r_k, od_lnx_g, od_lnx_b, od_w_out, ff_w_up, ff_conv, ff_conv_b, ff_w_down, loss_target, m_meta_tokens, m_norm_mix, m_norm_ffn, m_norm_final, m_ev_w_in, m_ev_conv_a, m_ev_ln_a_g, m_ev_ln_a_b, m_ev_conv_b, m_ev_w_out, m_od_w_in, m_od_sinks, m_od_mu, m_od_w0, m_od_w2, m_od_a0, m_od_a2, m_od_g2, m_od_k_k, m_od_k_a, m_od_r_k, m_od_lnx_g, m_od_lnx_b, m_od_w_out, m_ff_w_up, m_ff_conv, m_ff_conv_b, m_ff_w_down, v_meta_tokens, v_norm_mix, v_norm_ffn, v_norm_final, v_ev_w_in, v_ev_conv_a, v_ev_ln_a_g, v_ev_ln_a_b, v_ev_conv_b, v_ev_w_out, v_od_w_in, v_od_sinks, v_od_mu, v_od_w0, v_od_w2, v_od_a0, v_od_a2, v_od_g2, v_od_k_k, v_od_k_a, v_od_r_k, v_od_lnx_g, v_od_lnx_b, v_od_w_out, v_ff_w_up, v_ff_conv, v_ff_conv_b, v_ff_w_down):
    A = dict(locals())
    px, py, pc = _mesh_pos()
    me = _dev(px, py, pc)
    me_vec = jnp.reshape(me, (1,)).astype(jnp.int32)
    rows = lambda a: a.reshape(N_DEV * a.shape[1], a.shape[2])
    blocks = lambda a: a.reshape(N_DEV, a.shape[0] // N_DEV, a.shape[1])

    shards = dict(ev_in=ev_w_in[0].T, ev_out=ev_w_out[0], ff0_up=ff_w_up[0].T, ff0_down=ff_w_down[0], od_in=od_w_in[0].T,
                  od_out=od_w_out[0], ff1_up=ff_w_up[1].T, ff1_down=ff_w_down[1])
    shards = {n: b.astype(bf16) for n, b in shards.items()}
    small_shapes = [A[n].shape for n, _ in _SMALL_SHARDED]
    gathered = _all_gather([shards["ev_in"], shards["ev_out"], _pack([A[n] for n, _ in _SMALL_SHARDED])], "gather_first")
    gathered, shards = lax.optimization_barrier((gathered, shards))
    fetch, tok0 = {}, jnp.zeros((), f32)
    for n in ("ff0_up", "ff0_down", "od_in", "od_out", "ff1_up", "ff1_down"):
        shard, tok0 = lax.optimization_barrier((shards[n], tok0))
        land = lax.dynamic_update_slice(lax.empty((N_DEV,) + shard.shape, bf16), shard[None], (me, 0, 0))
        fetch[n], token = _xchg_start([shard], [land], False, f"gather_{n}_start")
        tok0 = tok0 + token[0, 0]

    def get_w(n, after):
        if n in ("ev_in", "ev_out"):
            return rows(gathered[("ev_in", "ev_out").index(n)])
        return rows(_xchg_wait(fetch[n], after, f"gather_{n}_wait")[1][0])

    W = {}
    for (n, ax), seg in zip(_SMALL_SHARDED, _unpack(gathered[-1].reshape(N_DEV, -1), small_shapes)):
        W[n] = jnp.moveaxis(seg, 0, ax).reshape(_full_shape(A[n].shape, ax))
    for n in ("ev_conv_a", "ev_conv_b", "od_w2", "od_a2", "od_g2"):
        W[n] = W[n][0]
    for n in _SMALL_REPLICATED:
        W[n] = A[n]
    W["od_r_k"] = od_r_k[0]

    sent = {}

    def put_g(grp, gs):
        g8 = [blocks(g) for g in gs]
        sent[grp], token = _xchg_start(g8, [lax.empty(g.shape, g.dtype) for g in g8], True, f"reduce_{grp}_start")
        return token[0, 0]

    loss_tile, grad_x, G = _local_step(x[0], loss_target[0], W, get_w, put_g, tok0)

    small_names = [n for n, _ in _SMALL_SHARDED] + list(_SMALL_REPLICATED)
    small_full_shapes = [_full_shape(A[n].shape, ax) for n, ax in _SMALL_SHARDED] + [A[n].shape for n in _SMALL_REPLICATED]
    slab = _pack([G[n] for n in small_names])
    small_sent, small_tok = _xchg_start(
        [slab], [lax.dynamic_update_slice(lax.empty((N_DEV,) + slab.shape, f32), slab[None], (me, 0, 0))], False,
        "gather_small_grads_start")
    gsh = {}
    for grp in ("ff1", "od", "ff0", "ev"):
        srcs, lands = _xchg_wait(sent[grp], small_tok, f"reduce_{grp}_wait")
        gsh[grp] = [_rs_sum(g, land, me_vec, f"reduce_{grp}_sum{i}") for i, (g, land) in enumerate(zip(srcs, lands))]
    grads = dict(ev_w_in=gsh["ev"][0].T[None], ev_w_out=gsh["ev"][1][None], od_w_in=gsh["od"][0].T[None],
                 od_w_out=gsh["od"][1][None], ff_w_up=jnp.stack([gsh["ff0"][0].T, gsh["ff1"][0].T]),
                 ff_w_down=jnp.stack([gsh["ff0"][1], gsh["ff1"][1]]))

    delta, new_m, new_v = {}, {}, {}
    for n in ("ff_w_up", "ff_w_down", "od_w_in", "od_w_out", "ev_w_in", "ev_w_out"):
        delta[n], new_m[n], new_v[n] = _adamw(A[n], A["m_" + n], A["v_" + n], grads[n], "adamw_" + n)
    gsm = _xchg_wait(small_sent, delta["ev_w_out"], "gather_small_grads_wait")[1][0]
    for n, full in zip(small_names, _unpack(_sum_devices(gsm).reshape(-1), small_full_shapes)):
        grads[n] = full
    for n, ax in _SMALL_SHARDED:
        size = A[n].shape[ax]
        grads[n] = lax.dynamic_slice_in_dim(grads[n], me * size, size, axis=ax)
    for n in small_names:
        delta[n], new_m[n], new_v[n] = _adamw(A[n], A["m_" + n], A["v_" + n], grads[n], "adamw_" + n)

    loss = lax.psum(loss_tile[0, 0], ("x", "y", "c"))
    return (loss, grad_x[None], *[grads[n] for n in _WEIGHTS], *[delta[n] for n in _WEIGHTS],
            *[new_m[n] for n in _WEIGHTS], *[new_v[n] for n in _WEIGHTS])
```

```python
import jax
import jax.numpy as jnp
from jax import lax
from jax.experimental import pallas as pl
from jax.experimental.pallas import tpu as pltpu

f32, bf16 = jnp.float32, jnp.bfloat16

D_MODEL = 1024
N_META = 16
RMS_EPS = 1e-6
LN_EPS = 1e-5
D_A = 512
CONV_A_WIDTH = 31
CONV_B_WIDTH = 3
HEAD_DIM = 64
N_Q_HEADS = 8
N_KV_HEADS = 2
GQA_GROUP = 4
D_ATT = 512
D_KV = 128
BLOCK = 128
ROPE_THETA = 10000.0
D_R = 512
LORA_W, LORA_A, LORA_G = 64, 64, 128
RWKV_GN_EPS = 64e-5
ATT_COLS = D_ATT + 2 * D_KV
RWKV_COLS = 3 * D_R + LORA_W + LORA_A + LORA_G
D_FF = 2816
FF_CONV_WIDTH = 3
NEG_INF = -1e30
ATT_PAD = BLOCK - N_META
ATT_SCALE = HEAD_DIM ** -0.5

ADAM_LR, ADAM_B1, ADAM_B2, ADAM_EPS, ADAM_WD, ADAM_STEP = 0.001, 0.9, 0.999, 1e-08, 0.01, 10

N_DEV = 8
LANES = 128
SUBLANES = 8
SCAN_CHUNK = 48
PAIR_ROWS = 4 * HEAD_DIM
V7X_VMEM_LIMIT = 56 * 1024 * 1024
ADAMW_BLOCK_ELEMS = 400 * 1024
GRAD_WIRE_DTYPE = bf16
MESH = pl.DeviceIdType.MESH
S = jax.ShapeDtypeStruct
HIGHEST = lax.Precision.HIGHEST


def _pc(body, **kw):
    return pl.pallas_call(body, **kw)


def _cparams(sem=None):
    return pltpu.CompilerParams(dimension_semantics=sem, vmem_limit_bytes=V7X_VMEM_LIMIT)


def _divisor_block(t, unit, limit):
    best = unit
    for rb in range(unit, limit + 1, unit):
        if t % rb == 0:
            best = rb
    assert t % best == 0, (t, unit)
    return best


def _row_block(t):
    return _divisor_block(t, 16, 704)


def _row_block8(t):
    return _divisor_block(t, 8, 344)


def _col_tile(n, cap):
    return _divisor_block(n, LANES, min(n, cap)) if n % LANES == 0 else n


def _full(shape):
    nd = len(shape)
    return pl.BlockSpec(shape, lambda *_: (0,) * nd)


def _sigmoid(x):
    return jax.nn.sigmoid(x)


_DIMS = {"nn": (((1,), (0,)), ((), ())), "nt": (((1,), (1,)), ((), ())), "tn": (((0,), (0,)), ((), ()))}
MM_MAX_K = 2816
MM_MAX_TM = 704
MM_MAX_TN = 1408


def _mm(a, b, mode, name, out_dtype=f32, res=None):
    if mode == "nn":
        (m, k), (k2, n) = a.shape, b.shape
    elif mode == "nt":
        (m, k), (n, k2) = a.shape, b.shape
    else:
        (k, m), (k2, n) = a.shape, b.shape
    assert k == k2, (a.shape, b.shape, mode)
    tm = _row_block(m) if m % LANES else _col_tile(m, MM_MAX_TM)
    tn = _col_tile(n, MM_MAX_TN)
    nk = 1 if (mode == "tn" or k <= MM_MAX_K) else k // MM_MAX_K
    tk = k // nk
    assert tk * nk == k
    dims = _DIMS[mode]

    def body(a_ref, b_ref, *rest):
        part = lax.dot_general(a_ref[...].astype(bf16), b_ref[...].astype(bf16), dims, preferred_element_type=f32)
        if nk == 1:
            o_ref = rest[-1]
            if res is not None:
                part = part + rest[0][...]
            o_ref[...] = part.astype(out_dtype)
            return
        o_ref, acc_ref = rest[-2], rest[-1]
        kk = pl.program_id(2)

        @pl.when(kk == 0)
        def _():
            acc_ref[...] = part

        @pl.when(kk > 0)
        def _():
            acc_ref[...] += part

        @pl.when(kk == nk - 1)
        def _():
            acc = acc_ref[...]
            if res is not None:
                acc = acc + rest[0][...]
            o_ref[...] = acc.astype(out_dtype)

    if mode == "tn":
        a_spec = pl.BlockSpec((k, tm), lambda i, j, kk: (0, i))
    else:
        a_spec = pl.BlockSpec((tm, tk), lambda i, j, kk: (i, kk))
    if mode == "nt":
        b_spec = pl.BlockSpec((tn, tk), lambda i, j, kk: (j, kk))
    else:
        b_spec = pl.BlockSpec((tk, tn), lambda i, j, kk: (kk, j))
    o_spec = pl.BlockSpec((tm, tn), lambda i, j, kk: (i, j))
    ins, specs = [a, b], [a_spec, b_spec]
    if res is not None:
        ins.append(res)
        specs.append(o_spec)
    scratch = [pltpu.VMEM((tm, tn), f32)] if nk > 1 else []
    return _pc(body, name=name, grid=(m // tm, n // tn, nk), in_specs=specs, out_specs=o_spec,
               out_shape=S((m, n), out_dtype), scratch_shapes=scratch,
               compiler_params=_cparams(("arbitrary", "arbitrary", "arbitrary")))(*ins)


def _rms_fwd(x, g, name):
    t, d = x.shape
    rb = _row_block(t)

    def body(x_ref, g_ref, o_ref):
        xv = x_ref[...]
        rstd = lax.rsqrt(jnp.mean(xv * xv, axis=-1, keepdims=True) + RMS_EPS)
        o_ref[...] = (xv * rstd * g_ref[...]).astype(bf16)

    row = pl.BlockSpec((rb, d), lambda i: (i, 0))
    return _pc(body, name=name, grid=(t // rb,), in_specs=[row, _full((1, d))], out_specs=row,
               out_shape=S((t, d), bf16), compiler_params=_cparams(("arbitrary",)))(x, g.reshape(1, d))


def _rms_bwd(dy, x, g, dres, name):
    t, d = x.shape
    rb = _row_block8(t)

    def body(dy_ref, x_ref, g_ref, dres_ref, dx_ref, dg_ref):
        @pl.when(pl.program_id(0) == 0)
        def _():
            dg_ref[...] = jnp.zeros_like(dg_ref)
        xv, dyv = x_ref[...], dy_ref[...]
        rstd = lax.rsqrt(jnp.mean(xv * xv, axis=-1, keepdims=True) + RMS_EPS)
        xn = xv * rstd
        dg_ref[...] += jnp.sum(dyv * xn, axis=0, keepdims=True)
        dxh = dyv * g_ref[...]
        dx_ref[...] = dres_ref[...] + rstd * (dxh - xn * jnp.mean(dxh * xn, axis=-1, keepdims=True))

    row = pl.BlockSpec((rb, d), lambda i: (i, 0))
    return _pc(body, name=name, grid=(t // rb,), in_specs=[row, row, _full((1, d)), row],
               out_specs=(row, _full((1, d))), out_shape=(S((t, d), f32), S((1, d), f32)),
               compiler_params=_cparams(("arbitrary",)))(dy, x, g.reshape(1, d), dres)


def _final_loss(h, g, target_padded):
    t, d = h.shape
    rb = _row_block8(t)

    def body(x_ref, g_ref, t_ref, loss_ref, dx_ref, dg_ref):
        i = pl.program_id(0)

        @pl.when(i == 0)
        def _():
            dg_ref[...] = jnp.zeros_like(dg_ref)
            loss_ref[...] = jnp.zeros_like(loss_ref)
        xv = x_ref[...]
        rstd = lax.rsqrt(jnp.mean(xv * xv, axis=-1, keepdims=True) + RMS_EPS)
        xn = xv * rstd
        gv = g_ref[...]
        row = i * rb + lax.broadcasted_iota(jnp.int32, (rb, 1), 0)
        diff = jnp.where(row >= N_META, xn * gv - t_ref[...], 0.0)
        loss_ref[...] += 0.5 * jnp.sum(jnp.mean(diff * diff, axis=-1, keepdims=True))
        dout = diff * (1.0 / d)
        dg_ref[...] += jnp.sum(dout * xn, axis=0, keepdims=True)
        dxh = dout * gv
        dx_ref[...] = rstd * (dxh - xn * jnp.mean(dxh * xn, axis=-1, keepdims=True))

    row = pl.BlockSpec((rb, d), lambda i: (i, 0))
    return _pc(body, name="final_loss", grid=(t // rb,), in_specs=[row, _full((1, d)), row],
               out_specs=(_full((SUBLANES, LANES)), row, _full((1, d))),
               out_shape=(S((SUBLANES, LANES), f32), S((t, d), f32), S((1, d), f32)),
               compiler_params=_cparams(("arbitrary",)))(h, g.reshape(1, d), target_padded)


CONV_LEAD = 32


def _fill_front_padded(pad_ref, x, t):
    pad_ref[0:CONV_LEAD, :] = jnp.zeros((CONV_LEAD, x.shape[1]), f32)
    pad_ref[CONV_LEAD:CONV_LEAD + t, :] = x


def _fill_back_padded(pad_ref, x, t):
    pad_ref[0:t, :] = x
    pad_ref[t:t + CONV_LEAD, :] = jnp.zeros((CONV_LEAD, x.shape[1]), f32)


def _conv_rows(pad_ref, w_ref, kw, r0, nr):
    acc = None
    for j in range(kw):
        lo = CONV_LEAD + r0 - (kw - 1) + j
        term = w_ref[j:j + 1, :] * pad_ref[lo:lo + nr, :]
        acc = term if acc is None else acc + term
    return acc


def _conv_t_rows(padb_ref, w_ref, kw, r0, nr):
    acc = None
    for j in range(kw):
        lo = r0 + (kw - 1) - j
        term = w_ref[j:j + 1, :] * padb_ref[lo:lo + nr, :]
        acc = term if acc is None else acc + term
    return acc


def _conv_dw_rows(dy_blk, pad_ref, kw, r0, nr):
    out = []
    for j in range(kw):
        lo = CONV_LEAD + r0 - (kw - 1) + j
        out.append(jnp.sum(dy_blk * pad_ref[lo:lo + nr, :], axis=0, keepdims=True))
    return out


def _acc_list(a, b):
    return b if a is None else [x + y for x, y in zip(a, b)]


def _ev_a_conv(p, conv_a):
    t = p.shape[0]
    cr = _row_block8(t)
    nb = D_A // LANES

    def body(av_ref, ag_ref, w_ref, o_ref, pad_ref):
        _fill_front_padded(pad_ref, av_ref[...] * _sigmoid(ag_ref[...]), t)
        for r in range(t // cr):
            o_ref[r * cr:(r + 1) * cr, :] = _conv_rows(pad_ref, w_ref, CONV_A_WIDTH, r * cr, cr)

    col = lambda off: pl.BlockSpec((t, LANES), lambda j: (0, j + off))
    return _pc(body, name="ev_a_conv", grid=(nb,),
               in_specs=[col(0), col(nb), pl.BlockSpec((CONV_A_WIDTH, LANES), lambda j: (0, j))],
               out_specs=col(0), out_shape=S((t, D_A), f32),
               scratch_shapes=[pltpu.VMEM((t + CONV_LEAD, LANES), f32)],
               compiler_params=_cparams(("arbitrary",)))(p, p, conv_a)


def _ln_silu(uc, g, b):
    mu = jnp.mean(uc, axis=-1, keepdims=True)
    xc = uc - mu
    var = jnp.mean(xc * xc, axis=-1, keepdims=True)
    y = xc * lax.rsqrt(var + LN_EPS) * g + b
    return y * _sigmoid(y)


def _ev_a_norm(uc, g, b):
    t, d = uc.shape
    rb = _row_block(t)

    def body(u_ref, g_ref, b_ref, o_ref):
        o_ref[...] = _ln_silu(u_ref[...], g_ref[...], b_ref[...]).astype(bf16)

    row = pl.BlockSpec((rb, d), lambda i: (i, 0))
    return _pc(body, name="ev_a_norm", grid=(t // rb,), in_specs=[row, _full((1, d)), _full((1, d))],
               out_specs=row, out_shape=S((t, d), bf16), compiler_params=_cparams(("arbitrary",)))(uc, g, b)


def _ev_a_norm_bwd(dy, uc, g, b):
    t, d = uc.shape
    rb = _row_block8(t)

    def body(dy_ref, u_ref, g_ref, b_ref, du_ref, dg_ref, db_ref):
        @pl.when(pl.program_id(0) == 0)
        def _():
            dg_ref[...] = jnp.zeros_like(dg_ref)
            db_ref[...] = jnp.zeros_like(db_ref)
        _, vjp = jax.vjp(_ln_silu, u_ref[...], g_ref[...], b_ref[...])
        du, dg, db = vjp(dy_ref[...])
        du_ref[...] = du
        dg_ref[...] += dg
        db_ref[...] += db

    row = pl.BlockSpec((rb, d), lambda i: (i, 0))
    return _pc(body, name="ev_a_norm_bwd", grid=(t // rb,), in_specs=[row, row, _full((1, d)), _full((1, d))],
               out_specs=(row, _full((1, d)), _full((1, d))),
               out_shape=(S((t, d), f32), S((1, d), f32), S((1, d), f32)),
               compiler_params=_cparams(("arbitrary",)))(dy, uc, g, b)


def _ev_a_conv_bwd(duc, p, conv_a):
    t = p.shape[0]
    cr = _row_block8(t)
    nb = D_A // LANES

    def body(dy_ref, av_ref, ag_ref, w_ref, dav_ref, dag_ref, dw_ref, pad_ref, padb_ref):
        _fill_front_padded(pad_ref, av_ref[...] * _sigmoid(ag_ref[...]), t)
        _fill_back_padded(padb_ref, dy_ref[...], t)
        dw = None
        for r in range(t // cr):
            rows = slice(r * cr, (r + 1) * cr)
            du = _conv_t_rows(padb_ref, w_ref, CONV_A_WIDTH, r * cr, cr)
            avr = av_ref[rows, :]
            sgr = _sigmoid(ag_ref[rows, :])
            dav_ref[rows, :] = du * sgr
            dag_ref[rows, :] = du * avr * sgr * (1.0 - sgr)
            dw = _acc_list(dw, _conv_dw_rows(dy_ref[rows, :], pad_ref, CONV_A_WIDTH, r * cr, cr))
        for j in range(CONV_A_WIDTH):
            dw_ref[j:j + 1, :] = dw[j]

    col = lambda off: pl.BlockSpec((t, LANES), lambda j: (0, j + off))
    wsp = pl.BlockSpec((CONV_A_WIDTH, LANES), lambda j: (0, j))
    return _pc(body, name="ev_a_conv_bwd", grid=(nb,), in_specs=[col(0), col(0), col(nb), wsp],
               out_specs=(col(0), col(0), wsp),
               out_shape=(S((t, D_A), f32), S((t, D_A), f32), S((CONV_A_WIDTH, D_A), f32)),
               scratch_shapes=[pltpu.VMEM((t + CONV_LEAD, LANES), f32), pltpu.VMEM((t + CONV_LEAD, LANES), f32)],
               compiler_params=_cparams(("arbitrary",)))(duc, p, p, conv_a)


def _ev_b(p, conv_b):
    t = p.shape[0]
    cr = _row_block8(t)
    nb = D_A // LANES

    def body(gb_ref, gc_ref, xi_ref, w_ref, o_ref, pad_ref, stage_ref):
        _fill_front_padded(pad_ref, gc_ref[...] * xi_ref[...], t)
        for r in range(t // cr):
            rows = slice(r * cr, (r + 1) * cr)
            stage_ref[rows, :] = gb_ref[rows, :] * _conv_rows(pad_ref, w_ref, CONV_B_WIDTH, r * cr, cr)
        o_ref[...] = stage_ref[...].astype(bf16)

    col = lambda off: pl.BlockSpec((t, LANES), lambda j: (0, j + off))
    return _pc(body, name="ev_b", grid=(nb,),
               in_specs=[col(2 * nb), col(3 * nb), col(4 * nb), pl.BlockSpec((CONV_B_WIDTH, LANES), lambda j: (0, j))],
               out_specs=col(0), out_shape=S((t, D_A), bf16),
               scratch_shapes=[pltpu.VMEM((t + CONV_LEAD, LANES), f32), pltpu.VMEM((t, LANES), f32)],
               compiler_params=_cparams(("arbitrary",)))(p, p, p, conv_b)


def _ev_b_bwd(dy, p, conv_b):
    t = p.shape[0]
    cr = _row_block8(t)
    nb = D_A // LANES

    def body(dy_ref, gb_ref, gc_ref, xi_ref, w_ref, dgb_ref, dgc_ref, dxi_ref, dw_ref, pad_ref, padb_ref):
        _fill_front_padded(pad_ref, gc_ref[...] * xi_ref[...], t)
        _fill_back_padded(padb_ref, dy_ref[...] * gb_ref[...], t)
        dw = None
        for r in range(t // cr):
            rows = slice(r * cr, (r + 1) * cr)
            dgb_ref[rows, :] = dy_ref[rows, :] * _conv_rows(pad_ref, w_ref, CONV_B_WIDTH, r * cr, cr)
            dcx = _conv_t_rows(padb_ref, w_ref, CONV_B_WIDTH, r * cr, cr)
            dgc_ref[rows, :] = dcx * xi_ref[rows, :]
            dxi_ref[rows, :] = dcx * gc_ref[rows, :]
            dw = _acc_list(dw, _conv_dw_rows(padb_ref[rows, :], pad_ref, CONV_B_WIDTH, r * cr, cr))
        for j in range(CONV_B_WIDTH):
            dw_ref[j:j + 1, :] = dw[j]

    col = lambda off: pl.BlockSpec((t, LANES), lambda j: (0, j + off))
    wsp = pl.BlockSpec((CONV_B_WIDTH, LANES), lambda j: (0, j))
    return _pc(body, name="ev_b_bwd", grid=(nb,), in_specs=[col(nb), col(2 * nb), col(3 * nb), col(4 * nb), wsp],
               out_specs=(col(0), col(0), col(0), wsp),
               out_shape=(S((t, D_A), f32), S((t, D_A), f32), S((t, D_A), f32), S((CONV_B_WIDTH, D_A), f32)),
               scratch_shapes=[pltpu.VMEM((t + CONV_LEAD, LANES), f32), pltpu.VMEM((t + CONV_LEAD, LANES), f32)],
               compiler_params=_cparams(("arbitrary",)))(dy, p, p, p, conv_b)


def _ffn_mid(u, conv_w, conv_b, name):
    t = u.shape[0]
    cr = _row_block8(t)
    nb = D_FF // LANES

    def body(gt_ref, vl_ref, w_ref, b_ref, o_ref, pad_ref, stage_ref):
        _fill_front_padded(pad_ref, gt_ref[...], t)
        for r in range(t // cr):
            rows = slice(r * cr, (r + 1) * cr)
            gc = _conv_rows(pad_ref, w_ref, FF_CONV_WIDTH, r * cr, cr) + b_ref[...]
            stage_ref[rows, :] = gc * _sigmoid(gc) * vl_ref[rows, :]
        o_ref[...] = stage_ref[...].astype(bf16)

    col = lambda off: pl.BlockSpec((t, LANES), lambda j: (0, j + off))
    return _pc(body, name=name, grid=(nb,),
               in_specs=[col(0), col(nb), pl.BlockSpec((FF_CONV_WIDTH, LANES), lambda j: (0, j)),
                         pl.BlockSpec((1, LANES), lambda j: (0, j))],
               out_specs=col(0), out_shape=S((t, D_FF), bf16),
               scratch_shapes=[pltpu.VMEM((t + CONV_LEAD, LANES), f32), pltpu.VMEM((t, LANES), f32)],
               compiler_params=_cparams(("arbitrary",)))(u, u, conv_w, conv_b.reshape(1, D_FF))


def _ffn_mid_bwd(dz, u, conv_w, conv_b, name):
    t = u.shape[0]
    cr = _row_block8(t)
    nb = D_FF // LANES

    def body(dz_ref, gt_ref, vl_ref, w_ref, b_ref, du_ref, dw_ref, db_ref, pad_ref, padb_ref):
        s = pl.program_id(1)
        _fill_front_padded(pad_ref, gt_ref[...], t)

        @pl.when(s == 0)
        def _():
            for r in range(t // cr):
                rows = slice(r * cr, (r + 1) * cr)
                gc = _conv_rows(pad_ref, w_ref, FF_CONV_WIDTH, r * cr, cr) + b_ref[...]
                sg = _sigmoid(gc)
                padb_ref[rows, :] = dz_ref[rows, :] * vl_ref[rows, :] * sg * (1.0 + gc * (1.0 - sg))
            padb_ref[t:t + CONV_LEAD, :] = jnp.zeros((CONV_LEAD, LANES), f32)
            dw, db = None, None
            for r in range(t // cr):
                rows = slice(r * cr, (r + 1) * cr)
                du_ref[rows, :] = _conv_t_rows(padb_ref, w_ref, FF_CONV_WIDTH, r * cr, cr)
                dgc = padb_ref[rows, :]
                dw = _acc_list(dw, _conv_dw_rows(dgc, pad_ref, FF_CONV_WIDTH, r * cr, cr))
                pb = jnp.sum(dgc, axis=0, keepdims=True)
                db = pb if db is None else db + pb
            for j in range(FF_CONV_WIDTH):
                dw_ref[j:j + 1, :] = dw[j]
            db_ref[...] = db

        @pl.when(s == 1)
        def _():
            for r in range(t // cr):
                rows = slice(r * cr, (r + 1) * cr)
                gc = _conv_rows(pad_ref, w_ref, FF_CONV_WIDTH, r * cr, cr) + b_ref[...]
                du_ref[rows, :] = dz_ref[rows, :] * gc * _sigmoid(gc)

    col = lambda off: pl.BlockSpec((t, LANES), lambda j, s: (0, j + off))
    wsp = pl.BlockSpec((FF_CONV_WIDTH, LANES), lambda j, s: (0, j))
    bsp = pl.BlockSpec((1, LANES), lambda j, s: (0, j))
    return _pc(body, name=name, grid=(nb, 2), in_specs=[col(0), col(0), col(nb), wsp, bsp],
               out_specs=(pl.BlockSpec((t, LANES), lambda j, s: (0, s * nb + j)), wsp, bsp),
               out_shape=(S((t, 2 * D_FF), f32), S((FF_CONV_WIDTH, D_FF), f32), S((1, D_FF), f32)),
               scratch_shapes=[pltpu.VMEM((t + CONV_LEAD, LANES), f32), pltpu.VMEM((t + CONV_LEAD, LANES), f32)],
               compiler_params=_cparams(("arbitrary", "arbitrary")))(dz, u, u, conv_w, conv_b.reshape(1, D_FF))


def _swap_halves(x):
    w = x.shape[1]
    lane = lax.broadcasted_iota(jnp.int32, x.shape, 1) % HEAD_DIM
    return jnp.where(lane < HEAD_DIM // 2, pltpu.roll(x, w - HEAD_DIM // 2, axis=1), pltpu.roll(x, HEAD_DIM // 2, axis=1))


def _rope_pack(patt, c64, s64):
    t = patt.shape[0]
    tp = t + ATT_PAD

    def body(p_ref, c_ref, s_ref, q_ref, k_ref, v_ref):
        c, s = c_ref[...], s_ref[...]

        def rope(x, nh):
            cc = jnp.concatenate([c] * nh, axis=1)
            ss = jnp.concatenate([s] * nh, axis=1)
            return x * cc + _swap_halves(x) * ss

        for ref, val in ((q_ref, rope(p_ref[:, 0:D_ATT], N_Q_HEADS)),
                         (k_ref, rope(p_ref[:, D_ATT:D_ATT + D_KV], N_KV_HEADS)),
                         (v_ref, p_ref[:, D_ATT + D_KV:ATT_COLS])):
            ref[0:ATT_PAD, :] = jnp.zeros((ATT_PAD, val.shape[1]), bf16)
            ref[ATT_PAD:tp, :] = val.astype(bf16)

    return _pc(body, name="rope_pack", in_specs=[_full((t, ATT_COLS)), _full((t, HEAD_DIM)), _full((t, HEAD_DIM))],
               out_specs=(_full((tp, D_ATT)), _full((tp, D_KV)), _full((tp, D_KV))), grid=(1,),
               out_shape=(S((tp, D_ATT), bf16), S((tp, D_KV), bf16), S((tp, D_KV), bf16)),
               compiler_params=_cparams(("arbitrary",)))(patt, c64, s64)


def _rope_bwd(dqp, dkp, dvp, c64, s64):
    tp = dqp.shape[0]
    t = tp - ATT_PAD

    def body(dq_ref, dk_ref, dv_ref, c_ref, s_ref, o_ref):
        c, s = c_ref[...], s_ref[...]

        def unrope(dy, nh):
            cc = jnp.concatenate([c] * nh, axis=1)
            ss = jnp.concatenate([s] * nh, axis=1)
            return dy * cc + _swap_halves(dy * ss)

        o_ref[:, 0:D_ATT] = unrope(dq_ref[ATT_PAD:tp, :], N_Q_HEADS)
        o_ref[:, D_ATT:D_ATT + D_KV] = unrope(dk_ref[ATT_PAD:tp, :], N_KV_HEADS)
        o_ref[:, D_ATT + D_KV:ATT_COLS] = dv_ref[ATT_PAD:tp, :]

    return _pc(body, name="rope_bwd", grid=(1,),
               in_specs=[_full((tp, D_ATT)), _full((tp, D_KV)), _full((tp, D_KV)), _full((t, HEAD_DIM)), _full((t, HEAD_DIM))],
               out_specs=_full((t, ATT_COLS)), out_shape=S((t, ATT_COLS), f32),
               compiler_params=_cparams(("arbitrary",)))(dqp, dkp, dvp, c64, s64)


def _attn_masks(n):
    rows = GQA_GROUP * BLOCK
    ri = lax.broadcasted_iota(jnp.int32, (rows, BLOCK), 0) % BLOCK
    ci = lax.broadcasted_iota(jnp.int32, (rows, BLOCK), 1)
    m_cur = (ci <= ri) & (ci >= jnp.where(n >= 1, 0, ATT_PAD))
    m_prev = ci > ri + jnp.where(n >= 2, 0, BLOCK)
    m_meta = ci >= jnp.where(n >= 1, ATT_PAD, BLOCK)
    return m_cur, m_prev, m_meta


def _attn_probs(qg, kc, kp, km, masks, skv):
    def scores(k, m):
        s = lax.dot_general(qg, k, _DIMS["nt"], preferred_element_type=f32) * ATT_SCALE
        return jnp.where(m, s, NEG_INF)
    s_c, s_p, s_m = scores(kc, masks[0]), scores(kp, masks[1]), scores(km, masks[2])
    mx = jnp.maximum(jnp.maximum(jnp.max(s_c, axis=-1, keepdims=True), jnp.max(s_p, axis=-1, keepdims=True)),
                     jnp.maximum(jnp.max(s_m, axis=-1, keepdims=True), skv))
    e_c, e_p, e_m, e_s = jnp.exp(s_c - mx), jnp.exp(s_p - mx), jnp.exp(s_m - mx), jnp.exp(skv - mx)
    den = (jnp.sum(e_c, axis=-1, keepdims=True) + jnp.sum(e_p, axis=-1, keepdims=True)
           + jnp.sum(e_m, axis=-1, keepdims=True) + e_s)
    inv = 1.0 / den
    return e_c * inv, e_p * inv, e_m * inv, e_s * inv


def _sink_rows(sk_ref, g):
    hrow = lax.broadcasted_iota(jnp.int32, (GQA_GROUP * BLOCK, 1), 0) // BLOCK
    skv = jnp.zeros((GQA_GROUP * BLOCK, 1), f32)
    for hh in range(GQA_GROUP):
        skv = jnp.where(hrow == hh, sk_ref[0, GQA_GROUP * g + hh], skv)
    return skv, hrow


def _stack_heads(ref, g):
    return jnp.concatenate([ref[:, (GQA_GROUP * g + hh) * HEAD_DIM:(GQA_GROUP * g + hh + 1) * HEAD_DIM]
                            for hh in range(GQA_GROUP)], axis=0)


def _attn_specs():
    blk = lambda w: pl.BlockSpec((BLOCK, w), lambda n: (n, 0))
    prev = pl.BlockSpec((BLOCK, D_KV), lambda n: (jnp.maximum(n - 1, 0), 0))
    meta = pl.BlockSpec((BLOCK, D_KV), lambda n: (0, 0))
    return blk, prev, meta


def _attn_fwd(qp, kp, vp, sinks):
    tp = qp.shape[0]
    blk, prev, meta = _attn_specs()

    def body(sk_ref, q_ref, kc_ref, kp_ref, km_ref, vc_ref, vp_ref, vm_ref, o_ref):
        masks = _attn_masks(pl.program_id(0))
        for g in range(N_KV_HEADS):
            sl = slice(g * HEAD_DIM, (g + 1) * HEAD_DIM)
            skv, _ = _sink_rows(sk_ref, g)
            p_c, p_p, p_m, _ = _attn_probs(_stack_heads(q_ref, g), kc_ref[:, sl], kp_ref[:, sl], km_ref[:, sl], masks, skv)
            o = (jnp.dot(p_c.astype(bf16), vc_ref[:, sl], preferred_element_type=f32)
                 + jnp.dot(p_p.astype(bf16), vp_ref[:, sl], preferred_element_type=f32)
                 + jnp.dot(p_m.astype(bf16), vm_ref[:, sl], preferred_element_type=f32))
            for hh in range(GQA_GROUP):
                h = GQA_GROUP * g + hh
                o_ref[:, h * HEAD_DIM:(h + 1) * HEAD_DIM] = o[hh * BLOCK:(hh + 1) * BLOCK].astype(bf16)

    return _pc(body, name="attn_fwd", grid=(tp // BLOCK,),
               in_specs=[pl.BlockSpec(memory_space=pltpu.SMEM), blk(D_ATT), blk(D_KV), prev, meta, blk(D_KV), prev, meta],
               out_specs=blk(D_ATT), out_shape=S((tp, D_ATT), bf16),
               compiler_params=_cparams(("arbitrary",)))(sinks, qp, kp, kp, kp, vp, vp, vp)


def _attn_bwd(qp, kp, vp, sinks, dop):
    tp = qp.shape[0]
    blk, prev, meta = _attn_specs()

    def body(sk_ref, q_ref, kc_ref, kp_ref, km_ref, vc_ref, vp_ref, vm_ref, do_ref, dq_ref, dk_ref, dv_ref, dsk_ref):
        n = pl.program_id(0)

        @pl.when(n == 0)
        def _():
            dk_ref[...] = jnp.zeros_like(dk_ref)
            dv_ref[...] = jnp.zeros_like(dv_ref)
            dsk_ref[...] = jnp.zeros_like(dsk_ref)
        masks = _attn_masks(n)
        cur = pl.ds(pl.multiple_of(n * BLOCK, BLOCK), BLOCK)
        prv = pl.ds(pl.multiple_of(jnp.maximum(n - 1, 0) * BLOCK, BLOCK), BLOCK)
        lane = lax.broadcasted_iota(jnp.int32, (1, LANES), 1)
        dsk = jnp.zeros((1, LANES), f32)
        for g in range(N_KV_HEADS):
            sl = slice(g * HEAD_DIM, (g + 1) * HEAD_DIM)
            skv, hrow = _sink_rows(sk_ref, g)
            qg = _stack_heads(q_ref, g)
            dog = _stack_heads(do_ref, g)
            ks = (kc_ref[:, sl], kp_ref[:, sl], km_ref[:, sl])
            vs = (vc_ref[:, sl], vp_ref[:, sl], vm_ref[:, sl])
            probs = _attn_probs(qg, ks[0], ks[1], ks[2], masks, skv)
            dps = [lax.dot_general(dog, v, _DIMS["nt"], preferred_element_type=f32) for v in vs]
            delta = sum(jnp.sum(p * dp, axis=-1, keepdims=True) for p, dp in zip(probs[:3], dps))
            dss = [(p * (dp - delta) * ATT_SCALE).astype(bf16) for p, dp in zip(probs[:3], dps)]
            dq = sum(jnp.dot(ds, k, preferred_element_type=f32) for ds, k in zip(dss, ks))
            for hh in range(GQA_GROUP):
                h = GQA_GROUP * g + hh
                dq_ref[:, h * HEAD_DIM:(h + 1) * HEAD_DIM] = dq[hh * BLOCK:(hh + 1) * BLOCK]
                dsk = dsk + jnp.where(lane == h, -jnp.sum(jnp.where(hrow == hh, probs[3] * delta, 0.0)), 0.0)
            for rows, p, ds in zip((cur, prv, slice(0, BLOCK)), probs[:3], dss):
                dv_ref[rows, sl] += lax.dot_general(p.astype(bf16), dog, _DIMS["tn"], preferred_element_type=f32)
                dk_ref[rows, sl] += lax.dot_general(ds, qg, _DIMS["tn"], preferred_element_type=f32)
        dsk_ref[...] += dsk

    return _pc(body, name="attn_bwd", grid=(tp // BLOCK,),
               in_specs=[pl.BlockSpec(memory_space=pltpu.SMEM), blk(D_ATT), blk(D_KV), prev, meta, blk(D_KV), prev, meta,
                         blk(D_ATT)],
               out_specs=(blk(D_ATT), _full((tp, D_KV)), _full((tp, D_KV)), _full((1, LANES))),
               out_shape=(S((tp, D_ATT), f32), S((tp, D_KV), f32), S((tp, D_KV), f32), S((1, LANES), f32)),
               compiler_params=_cparams(("arbitrary",)))(sinks, qp, kp, kp, kp, vp, vp, vp, dop)


def _seg(x, bm):
    hi = x.astype(bf16)
    lo = (x - hi.astype(f32)).astype(bf16)
    return jnp.dot(jnp.concatenate([hi, lo], axis=1), bm, preferred_element_type=f32)


@jax.custom_vjp
def _seg_linear(x, bm):
    return _seg(x, bm)


_seg_linear.defvjp(lambda x, bm: (_seg(x, bm), bm), lambda bm, ct: (_seg(ct, bm), jnp.zeros_like(bm)))


def _softplus(y):
    return jnp.maximum(y, 0.0) + jnp.log(1.0 + jnp.exp(-jnp.abs(y)))


def _prep_fn(xr, xk, xwd, xad, xgd, w0, w2, a0, a2, g2, k_k, k_a, bm, seg=_seg):
    xw = w0 + jnp.dot(jnp.tanh(xwd), w2, preferred_element_type=f32)
    decay = jnp.exp(-jnp.exp(-_softplus(-xw) - 0.5))
    alpha = _sigmoid(a0 + jnp.dot(xad, a2, preferred_element_type=f32))
    g = jnp.dot(_sigmoid(xgd), g2, preferred_element_type=f32)
    kk = xk * k_k
    kkn = kk / jnp.maximum(jnp.sqrt(seg(kk * kk, bm)), 1e-12)
    k2 = xk * (1.0 + (alpha - 1.0) * k_a)
    return decay, k2, -kkn, kkn * alpha, g


def _split_cols(x):
    o1, o2, o3 = 3 * D_R, 3 * D_R + LORA_W, 3 * D_R + LORA_W + LORA_A
    return x[:, 0:D_R], x[:, D_R:2 * D_R], x[:, 2 * D_R:o1], x[:, o1:o2], x[:, o2:o3], x[:, o3:RWKV_COLS]


def _shifted(sh_ref, x, halo, first, rb):
    sh_ref[0:SUBLANES, :] = jnp.where(first, 0.0, halo)
    sh_ref[SUBLANES:SUBLANES + rb, :] = x
    return sh_ref[SUBLANES - 1:SUBLANES - 1 + rb, :]


_PREP_PARAMS = ("od_w0", "od_w2", "od_a0", "od_a2", "od_g2", "od_k_k", "od_k_a")


def _rwkv_prep(pr, mu, params, bm):
    t = pr.shape[0]
    rb = _row_block8(t)
    hb = rb // SUBLANES

    def body(pr_ref, halo_ref, mu_ref, w0, w2, a0, a2, g2, kk_ref, ka_ref, bm_ref, *outs_sh):
        outs, sh_ref = outs_sh[:-1], outs_sh[-1]
        x = pr_ref[...]
        prev = _shifted(sh_ref, x, halo_ref[...], pl.program_id(0) == 0, rb)
        xr, xk, xv, xwd, xad, xgd = _split_cols(x + (prev - x) * mu_ref[...])
        bmv = bm_ref[...]
        decay, k2, a_s, b_s, g = _prep_fn(xr, xk, xwd, xad, xgd, w0[...], w2[...], a0[...], a2[...], g2[...],
                                          kk_ref[...], ka_ref[...], bmv)
        vals = (xr, xv, decay, k2, a_s, b_s, decay * xr, _seg(b_s * xr, bmv), _seg(k2 * xr, bmv), g)
        for ref, val in zip(outs, vals):
            ref[...] = val

    row = pl.BlockSpec((rb, RWKV_COLS), lambda i: (i, 0))
    halo = pl.BlockSpec((SUBLANES, RWKV_COLS), lambda i: (jnp.maximum(i * hb - 1, 0), 0))
    orow = pl.BlockSpec((rb, D_R), lambda i: (i, 0))
    return _pc(body, name="rwkv_prep", grid=(t // rb,),
               in_specs=[row, halo, _full((1, RWKV_COLS))] + [_full(p.shape) for p in params] + [_full(bm.shape)],
               out_specs=(orow,) * 10, out_shape=(S((t, D_R), f32),) * 10,
               scratch_shapes=[pltpu.VMEM((rb + SUBLANES, RWKV_COLS), f32)],
               compiler_params=_cparams(("arbitrary",)))(pr, pr, mu, *params, bm)


def _rwkv_prep_bwd(pr, mu, params, bm, cts):
    t = pr.shape[0]
    rb = _row_block8(t)
    hb = rb // SUBLANES
    counts = [len(c) for c in cts]
    flat = [a for c in cts for a in c]

    def body(pr_ref, halo_ref, mu_ref, w0, w2, a0, a2, g2, kk_ref, ka_ref, bm_ref, *rest):
        ct_refs, rest = rest[:len(flat)], rest[len(flat):]
        dx_ref, dmu_ref = rest[0], rest[1]
        dpar_refs, sh_ref = rest[2:9], rest[9]

        @pl.when(pl.program_id(0) == 0)
        def _():
            dmu_ref[...] = jnp.zeros_like(dmu_ref)
            for r in dpar_refs:
                r[...] = jnp.zeros_like(r)
        sums, pos = [], 0
        for c in counts:
            sums.append(sum(r[...] for r in ct_refs[pos:pos + c]))
            pos += c
        x = pr_ref[...]
        prev = _shifted(sh_ref, x, halo_ref[...], pl.program_id(0) == 0, rb)
        xr, xk, xv, xwd, xad, xgd = _split_cols(x + (prev - x) * mu_ref[...])
        bmv = bm_ref[...]
        _, vjp = jax.vjp(lambda *a: _prep_fn(*a, bmv, _seg_linear), xr, xk, xwd, xad, xgd, w0[...], w2[...], a0[...], a2[...],
                         g2[...], kk_ref[...], ka_ref[...])
        grads = vjp(tuple(sums[:5]))
        dxr, dxk, dxwd, dxad, dxgd = grads[:5]
        o1, o2, o3 = 3 * D_R, 3 * D_R + LORA_W, 3 * D_R + LORA_W + LORA_A
        dx_ref[:, 0:D_R] = dxr + sums[5]
        dx_ref[:, D_R:2 * D_R] = dxk
        dx_ref[:, 2 * D_R:o1] = sums[6]
        dx_ref[:, o1:o2] = dxwd
        dx_ref[:, o2:o3] = dxad
        dx_ref[:, o3:RWKV_COLS] = dxgd
        dmu_ref[...] += jnp.sum(dx_ref[...] * (prev - x), axis=0, keepdims=True)
        for r, gval in zip(dpar_refs, grads[5:]):
            r[...] += gval

    row = pl.BlockSpec((rb, RWKV_COLS), lambda i: (i, 0))
    halo = pl.BlockSpec((SUBLANES, RWKV_COLS), lambda i: (jnp.maximum(i * hb - 1, 0), 0))
    crow = pl.BlockSpec((rb, D_R), lambda i: (i, 0))
    return _pc(body, name="rwkv_prep_bwd", grid=(t // rb,),
               in_specs=[row, halo, _full((1, RWKV_COLS))] + [_full(p.shape) for p in params] + [_full(bm.shape)]
               + [crow] * len(flat),
               out_specs=(row, _full((1, RWKV_COLS))) + tuple(_full(p.shape) for p in params),
               out_shape=(S((t, RWKV_COLS), f32), S((1, RWKV_COLS), f32)) + tuple(S(p.shape, f32) for p in params),
               scratch_shapes=[pltpu.VMEM((rb + SUBLANES, RWKV_COLS), f32)],
               compiler_params=_cparams(("arbitrary",)))(pr, pr, mu, *params, bm, *flat)


def _shift_bwd(dxs, mu):
    t = dxs.shape[0]
    rb = _row_block8(t)
    hb = rb // SUBLANES
    nblk = t // rb

    def body(dx_ref, halo_ref, mu_ref, o_ref, sh_ref):
        dx = dx_ref[...]
        sh_ref[0:rb, :] = dx
        sh_ref[rb:rb + SUBLANES, :] = jnp.where(pl.program_id(0) == nblk - 1, 0.0, halo_ref[...])
        m = mu_ref[...]
        o_ref[...] = dx * (1.0 - m) + sh_ref[1:1 + rb, :] * m

    row = pl.BlockSpec((rb, RWKV_COLS), lambda i: (i, 0))
    halo = pl.BlockSpec((SUBLANES, RWKV_COLS), lambda i: (jnp.minimum((i + 1) * hb, t // SUBLANES - 1), 0))
    return _pc(body, name="rwkv_shift_bwd", grid=(nblk,), in_specs=[row, halo, _full((1, RWKV_COLS))],
               out_specs=row, out_shape=S((t, RWKV_COLS), f32),
               scratch_shapes=[pltpu.VMEM((rb + SUBLANES, RWKV_COLS), f32)],
               compiler_params=_cparams(("arbitrary",)))(dxs, dxs, mu)


def _post_fn(y, xr, k2, xv, g, lg, lb, rk, bm, seg=_seg):
    inv_n = 1.0 / HEAD_DIM
    yc = y - seg(y, bm) * inv_n
    var = seg(yc * yc, bm) * inv_n
    yn = yc * lax.rsqrt(var + RWKV_GN_EPS) * lg + lb
    return (yn + seg(xr * k2 * rk, bm) * xv) * g


def _rwkv_post(y, xr, k2, xv, g, lg, lb, rk, bm):
    t = y.shape[0]
    rb = _row_block8(t)

    def body(y_ref, xr_ref, k2_ref, xv_ref, g_ref, lg_ref, lb_ref, rk_ref, bm_ref, o_ref):
        o_ref[...] = _post_fn(y_ref[...], xr_ref[...], k2_ref[...], xv_ref[...], g_ref[...], lg_ref[...], lb_ref[...],
                              rk_ref[...], bm_ref[...])

    row = pl.BlockSpec((rb, D_R), lambda i: (i, 0))
    vec = _full((1, D_R))
    return _pc(body, name="rwkv_post", grid=(t // rb,), in_specs=[row] * 5 + [vec] * 3 + [_full(bm.shape)],
               out_specs=row, out_shape=S((t, D_R), f32),
               compiler_params=_cparams(("arbitrary",)))(y, xr, k2, xv, g, lg, lb, rk, bm)


def _rwkv_post_bwd(dy1, y, xr, k2, xv, g, lg, lb, rk, bm):
    t = y.shape[0]
    rb = _row_block8(t)

    def body(dy_ref, y_ref, xr_ref, k2_ref, xv_ref, g_ref, lg_ref, lb_ref, rk_ref, bm_ref, *outs):
        @pl.when(pl.program_id(0) == 0)
        def _():
            for r in outs[5:]:
                r[...] = jnp.zeros_like(r)
        bmv = bm_ref[...]
        _, vjp = jax.vjp(lambda *a: _post_fn(*a, bmv, _seg_linear), y_ref[...], xr_ref[...], k2_ref[...], xv_ref[...], g_ref[...],
                         lg_ref[...], lb_ref[...], rk_ref[...])
        grads = vjp(dy_ref[...])
        for r, gval in zip(outs[:5], grads[:5]):
            r[...] = gval
        for r, gval in zip(outs[5:], grads[5:]):
            r[...] += gval

    row = pl.BlockSpec((rb, D_R), lambda i: (i, 0))
    vec = _full((1, D_R))
    return _pc(body, name="rwkv_post_bwd", grid=(t // rb,),
               in_specs=[pl.BlockSpec((rb, D_R), lambda i: (i, 1))] + [row] * 5 + [vec] * 3 + [_full(bm.shape)],
               out_specs=(row,) * 5 + (vec,) * 3, out_shape=(S((t, D_R), f32),) * 5 + (S((1, D_R), f32),) * 3,
               compiler_params=_cparams(("arbitrary",)))(dy1, y, xr, k2, xv, g, lg, lb, rk, bm)


def _seg2(x, bb):
    hi = x.astype(bf16)
    lo = (x - hi.astype(f32)).astype(bf16)
    return jnp.dot(jnp.concatenate([hi, lo], axis=1), bb, preferred_element_type=f32)


def _row4(rows, j):
    return jnp.concatenate([jnp.broadcast_to(rows[j:j + 1, p * LANES:(p + 1) * LANES], (HEAD_DIM, LANES))
                            for p in range(4)], axis=0)


def _scan_consts():
    lane_group = jnp.arange(LANES) // HEAD_DIM
    b128 = (lane_group[:, None] == lane_group[None, :]).astype(bf16)
    bb = jnp.concatenate([b128, b128], axis=0)
    qsel = (jnp.arange(PAIR_ROWS)[:, None] % HEAD_DIM == jnp.arange(LANES)[None, :] % HEAD_DIM).astype(f32)
    return bb, qsel


def _store_cols(acc_ref, o_ref, tc):
    for p in range(4):
        blk = acc_ref[p * HEAD_DIM:(p + 1) * HEAD_DIM, :].T
        o_ref[:, (2 * p) * HEAD_DIM:(2 * p + 1) * HEAD_DIM] = blk[0:tc]
        o_ref[:, (2 * p + 1) * HEAD_DIM:(2 * p + 2) * HEAD_DIM] = blk[HEAD_DIM:HEAD_DIM + tc]


PAIR_GROUP = 2 * SUBLANES


def _rwkv_pairs(w, a, b, k, wr, bm):
    t = w.shape[0]
    rb = _row_block8(t)

    def body(w_ref, a_ref, b_ref, k_ref, wr_ref, bm_ref, *outs_sh):
        outs, sh_ref = outs_sh[:-1], outs_sh[-1]

        def second(ref):
            sh_ref[0:rb, :] = ref[...]
            sh_ref[rb:rb + SUBLANES, :] = jnp.zeros((SUBLANES, D_R), f32)
            return sh_ref[1:1 + rb, :]

        w1, b1, k1 = w_ref[...], b_ref[...], k_ref[...]
        w2, a2, wr2 = second(w_ref), second(a_ref), second(wr_ref)
        bmv = bm_ref[...]
        vals = (w1 * a2, w1 * wr2, w1 * w2, b1 * w2, k1 * w2, _seg(b1 * a2, bmv), _seg(k1 * a2, bmv),
                _seg(b1 * wr2, bmv), _seg(k1 * wr2, bmv))
        for ref, val in zip(outs, vals):
            ref[...] = val

    row = pl.BlockSpec((rb, D_R), lambda i: (i, 0))
    return _pc(body, name="rwkv_pairs", grid=(t // rb,), in_specs=[row] * 5 + [_full(bm.shape)],
               out_specs=(row,) * 9, out_shape=(S((t, D_R), f32),) * 9,
               scratch_shapes=[pltpu.VMEM((rb + SUBLANES, D_R), f32)],
               compiler_params=_cparams(("arbitrary",)))(w, a, b, k, wr, bm)


def _wkv_fwd(w, k, v, a, b, wr, br, kr, pairs):
    t = w.shape[0]
    tc = SCAN_CHUNK
    bb, qsel = _scan_consts()

    def body(*refs):
        step_refs, pair_refs = refs[0:8], refs[8:17]
        bb_ref, q_ref, y_ref, st_ref, sa_ref, vb_ref, s_scr, yacc = refs[17:]

        @pl.when(pl.program_id(0) == 0)
        def _():
            s_scr[...] = jnp.zeros_like(s_scr)
        bbv, qv = bb_ref[...], q_ref[...]
        lane64 = lax.broadcasted_iota(jnp.int32, (PAIR_ROWS, LANES), 1) % HEAD_DIM

        def halves(x):
            hi = x.astype(bf16)
            return jnp.concatenate([hi, (x - hi.astype(f32)).astype(bf16)], axis=1)

        def group(gi, s):
            base = pl.multiple_of(gi * PAIR_GROUP, PAIR_GROUP)
            w16, k16, v16, a16, b16, wr16, br16, kr16 = (
                (ref[pl.ds(base, SUBLANES), :], ref[pl.ds(base + SUBLANES, SUBLANES), :]) for ref in step_refs)
            a2p, r2p, w12p, b1wp, k1wp, betap, kappap, bwrp, kwrp = (
                (ref[pl.ds(base, SUBLANES), :], ref[pl.ds(base + SUBLANES, SUBLANES), :]) for ref in pair_refs)
            vh16 = tuple(x.astype(bf16).astype(f32) for x in v16)
            vl16 = tuple(x - h for x, h in zip(v16, vh16))
            step = lambda arr, j: _row4(arr[j // SUBLANES], j % SUBLANES)
            for q in range(SUBLANES):
                j1, j2 = 2 * q, 2 * q + 1
                t1 = base + j1
                lhs = [halves(jnp.concatenate([s * step(a16, j1), s * step(a2p, j1), s * step(wr16, j1), s * step(r2p, j1)],
                                              axis=0))]
                for j in (j1, j2):
                    lhs.append(jnp.concatenate([(qv * step(vh16, j)).astype(bf16), (qv * step(vl16, j)).astype(bf16)], axis=1))
                r = jnp.dot(jnp.concatenate(lhs, axis=0), bbv, preferred_element_type=f32)
                sa1, p2, z1, z2, vb1, vb2 = (r[n * PAIR_ROWS:(n + 1) * PAIR_ROWS] for n in range(6))
                sa2 = p2 + sa1 * step(betap, j1) + vb1 * step(kappap, j1)
                y1 = z1 + sa1 * step(br16, j1) + vb1 * step(kr16, j1)
                y2 = (z2 + sa1 * step(bwrp, j1) + vb1 * step(kwrp, j1)) + (sa2 * step(br16, j2) + vb2 * step(kr16, j2))
                yacc[...] = jnp.where(lane64 == t1, y1, jnp.where(lane64 == t1 + 1, y2, yacc[...]))
                st_ref[t1] = s
                st_ref[t1 + 1] = s * step(w16, j1) + sa1 * step(b16, j1) + vb1 * step(k16, j1)
                sa_ref[t1] = sa1
                sa_ref[t1 + 1] = sa2
                vb_ref[t1] = vb1
                vb_ref[t1 + 1] = vb2
                s = ((s * step(w12p, j1) + sa1 * step(b1wp, j1)) + vb1 * step(k1wp, j1)) + (sa2 * step(b16, j2) + vb2 * step(k16, j2))
            return s

        s_scr[...] = lax.fori_loop(0, tc // PAIR_GROUP, group, s_scr[...])
        _store_cols(yacc, y_ref, tc)

    row = pl.BlockSpec((tc, D_R), lambda c: (c, 0))
    tiles = pl.BlockSpec((tc, PAIR_ROWS, LANES), lambda c: (c, 0, 0))
    return _pc(body, name="wkv_fwd", grid=(t // tc,),
               in_specs=[row] * 17 + [_full(bb.shape), _full(qsel.shape)],
               out_specs=(row, tiles, tiles, tiles),
               out_shape=(S((t, D_R), f32),) + (S((t, PAIR_ROWS, LANES), f32),) * 3,
               scratch_shapes=[pltpu.VMEM((PAIR_ROWS, LANES), f32), pltpu.VMEM((PAIR_ROWS, LANES), f32)],
               compiler_params=_cparams(("arbitrary",)))(w, k, v, a, b, wr, br, kr, *pairs, bb, qsel)


def _wkv_bwd(sprev, sab, vbb, w, k, a, b, r, dy):
    t = w.shape[0]
    tc = SCAN_CHUNK
    nc = t // tc
    bb, qsel = _scan_consts()

    def body(st_ref, sa_ref, vb_ref, w_ref, k_ref, a_ref, b_ref, r_ref, dy_ref, bb_ref, q_ref,
             dr_ref, dw_ref, dk_ref, dv_ref, da_ref, db_ref, g_scr, dvacc, rows_scr):
        @pl.when(pl.program_id(0) == 0)
        def _():
            g_scr[...] = jnp.zeros_like(g_scr)
        bbv, qv = bb_ref[...], q_ref[...]
        lane64 = lax.broadcasted_iota(jnp.int32, (PAIR_ROWS, LANES), 1) % HEAD_DIM
        outs = (dr_ref, dw_ref, db_ref, dk_ref, da_ref)

        def colsums(slot, j, x):
            for p in range(4):
                rows_scr[slot, j:j + 1, p * LANES:(p + 1) * LANES] = jnp.sum(x[p * HEAD_DIM:(p + 1) * HEAD_DIM], axis=0,
                                                                           keepdims=True)

        def group(i, g):
            base = pl.multiple_of((tc // SUBLANES - 1 - i) * SUBLANES, SUBLANES)
            w8, k8, a8, b8, r8, dy8 = (ref[pl.ds(base, SUBLANES), :] for ref in (w_ref, k_ref, a_ref, b_ref, r_ref, dy_ref))
            for j in reversed(range(SUBLANES)):
                tt = base + j
                sp, u, vb = st_ref[tt], sa_ref[tt], vb_ref[tt]
                a4, b4, w4, k4 = _row4(a8, j), _row4(b8, j), _row4(w8, j), _row4(k8, j)
                dyb = _seg2(qv * _row4(dy8, j), bbv)
                s_t = sp * w4 + u * b4 + vb * k4
                g = g + dyb * _row4(r8, j)
                rr2 = _seg2(jnp.concatenate([g * b4, g * k4], axis=0), bbv)
                du, dvb = rr2[0:PAIR_ROWS], rr2[PAIR_ROWS:2 * PAIR_ROWS]
                for slot, val in enumerate((s_t * dyb, g * sp, g * u, g * vb, sp * du)):
                    colsums(slot, j, val)
                dvacc[...] = jnp.where(lane64 == tt, dvb, dvacc[...])
                g = g * w4 + du * a4
            for slot, ref in enumerate(outs):
                ref[pl.ds(base, SUBLANES), :] = rows_scr[slot]
            return g

        g_scr[...] = lax.fori_loop(0, tc // SUBLANES, group, g_scr[...])
        _store_cols(dvacc, dv_ref, tc)

    row = pl.BlockSpec((tc, D_R), lambda c: (nc - 1 - c, 0))
    tiles = pl.BlockSpec((tc, PAIR_ROWS, LANES), lambda c: (nc - 1 - c, 0, 0))
    return _pc(body, name="wkv_bwd", grid=(nc,),
               in_specs=[tiles] * 3 + [row] * 6 + [_full(bb.shape), _full(qsel.shape)],
               out_specs=(row,) * 6, out_shape=(S((t, D_R), f32),) * 6,
               scratch_shapes=[pltpu.VMEM((PAIR_ROWS, LANES), f32), pltpu.VMEM((PAIR_ROWS, LANES), f32),
                               pltpu.VMEM((5, SUBLANES, D_R), f32)],
               compiler_params=_cparams(("arbitrary",)))(sprev, sab, vbb, w, k, a, b, r, dy, bb, qsel)


def _rope_tables(t):
    half = HEAD_DIM // 2
    inv = ROPE_THETA ** (-jnp.arange(half, dtype=f32) / half)
    ang = jnp.arange(t, dtype=f32)[:, None] * inv[None, :]
    cos, sin = jnp.cos(ang), jnp.sin(ang)
    return jnp.concatenate([cos, cos], axis=1), jnp.concatenate([-sin, sin], axis=1)


def _head_matrix():
    grp = jnp.arange(D_R) // HEAD_DIM
    b = (grp[:, None] == grp[None, :]).astype(bf16)
    return jnp.concatenate([b, b], axis=0)


def _ffn_fwd(h, g, get_w, conv_w, conv_b, i):
    hf = _rms_fwd(h, g, f"ffn{i}_norm")
    w_up_t = get_w(f"ff{i}_up", hf)
    u = _mm(hf, w_up_t, "nt", f"ffn{i}_up")
    z = _ffn_mid(u, conv_w, conv_b, f"ffn{i}_mid")
    w_down = get_w(f"ff{i}_down", z)
    return _mm(z, w_down, "nn", f"ffn{i}_down", res=h), (hf, u, z), w_up_t, w_down


def _ffn_bwd(dh, h, saved, g, w_up_t, conv_w, conv_b, w_down, i, put_g):
    hf, u, z = saved
    dz = _mm(dh, w_down, "nt", f"ffn{i}_dz")
    g_down = _mm(z, dh, "tn", f"ffn{i}_gdown", out_dtype=GRAD_WIRE_DTYPE)
    tok = put_g(f"ff{i}_down", g_down)
    du, g_conv, g_convb = _ffn_mid_bwd(dz, u, conv_w, conv_b + tok, f"ffn{i}_mid_bwd")
    g_up_t = _mm(du, hf, "tn", f"ffn{i}_gup", out_dtype=GRAD_WIRE_DTYPE)
    tok = put_g(f"ff{i}_up", g_up_t)
    dhf = _mm(du, w_up_t, "nn", f"ffn{i}_dhf")
    dh_in, g_norm = _rms_bwd(dhf, h, g + tok, dh, f"ffn{i}_norm_bwd")
    return dh_in, dict(conv=g_conv, conv_b=g_convb, norm=g_norm)


def _local_step(x, target, W, get_w, put_g, put_small, tok0):
    t = N_META + x.shape[0]
    c64, s64 = _rope_tables(t)
    bm = _head_matrix()
    h0 = jnp.concatenate([W["meta_tokens"], x], axis=0)

    ev_w_in_t, ev_w_out = get_w("ev_in", None), get_w("ev_out", None)
    hn0 = _rms_fwd(h0, W["norm_mix"][0] + tok0, "mix0_norm")
    p0 = _mm(hn0, ev_w_in_t, "nt", "ev_in")
    uc = _ev_a_conv(p0, W["ev_conv_a"])
    y0 = jnp.concatenate([_ev_a_norm(uc, W["ev_ln_a_g"], W["ev_ln_a_b"]), _ev_b(p0, W["ev_conv_b"])], axis=1)
    h1 = _mm(y0, ev_w_out, "nn", "ev_out", res=h0)
    h2, ffn0, ff0_up_t, ff0_down = _ffn_fwd(h1, W["norm_ffn"][0], get_w, W["ff_conv"][0], W["ff_conv_b"][0], 0)

    hn1 = _rms_fwd(h2, W["norm_mix"][1], "mix1_norm")
    od_w_in_t = get_w("od_in", hn1)
    p1 = _mm(hn1, od_w_in_t, "nt", "od_in")
    pr = p1[:, ATT_COLS:]
    qp, kp, vp = _rope_pack(p1[:, :ATT_COLS], c64, s64)
    op = _attn_fwd(qp, kp, vp, W["od_sinks"])
    prep_params = [W[n] for n in _PREP_PARAMS]
    xr, xv, decay, k2, a_s, b_s, wr, br, kr, gate = _rwkv_prep(pr, W["od_mu"], prep_params, bm)
    pairs = _rwkv_pairs(decay, a_s, b_s, k2, wr, bm)
    ysc, sprev, sab, vbb = _wkv_fwd(decay, k2, xv, a_s, b_s, wr, br, kr, pairs)
    rk = W["od_r_k"].reshape(1, D_R)
    yr = _rwkv_post(ysc, xr, k2, xv, gate, W["od_lnx_g"], W["od_lnx_b"], rk, bm)
    y1 = jnp.concatenate([op[ATT_PAD:], yr.astype(bf16)], axis=1)
    od_w_out = get_w("od_out", y1)
    h3 = _mm(y1, od_w_out, "nn", "od_out", res=h2)
    h4, ffn1, ff1_up_t, ff1_down = _ffn_fwd(h3, W["norm_ffn"][1], get_w, W["ff_conv"][1], W["ff_conv_b"][1], 1)

    tgt = jnp.concatenate([jnp.zeros((N_META, D_MODEL), f32), target], axis=0)
    loss, dh4, g_norm_final = _final_loss(h4, W["norm_final"], tgt)

    dh3, gf1 = _ffn_bwd(dh4, h3, ffn1, W["norm_ffn"][1], ff1_up_t, W["ff_conv"][1], W["ff_conv_b"][1], ff1_down, 1, put_g)
    dy1 = _mm(dh3, od_w_out, "nt", "od_dy")
    g_od_w_out = _mm(y1, dh3, "tn", "od_gout", out_dtype=GRAD_WIRE_DTYPE)
    tok = put_g("od_out", g_od_w_out)
    dysc, dxr_p, dk2_p, dxv_p, dgate, g_lnx_g, g_lnx_b, g_rk = _rwkv_post_bwd(
        dy1, ysc, xr, k2, xv, gate, W["od_lnx_g"], W["od_lnx_b"] + tok, rk, bm)
    dr, dw, dk, dv, da, db = _wkv_bwd(sprev, sab, vbb, decay, k2, a_s, b_s, xr, dysc)
    prep_grads = _rwkv_prep_bwd(pr, W["od_mu"], prep_params, bm,
                                [[dw], [dk, dk2_p], [da], [db], [dgate], [dr, dxr_p], [dv, dxv_p]])
    dxs, g_mu = prep_grads[0], prep_grads[1]
    dpr = _shift_bwd(dxs, W["od_mu"])
    dop = jnp.concatenate([jnp.zeros((ATT_PAD, D_ATT), f32), dy1[:, :D_ATT]], axis=0).astype(bf16)
    dqp, dkp, dvp, dsk = _attn_bwd(qp, kp, vp, W["od_sinks"], dop)
    dp1 = jnp.concatenate([_rope_bwd(dqp, dkp, dvp, c64, s64), dpr], axis=1)
    g_od_w_in_t = _mm(dp1, hn1, "tn", "od_gin", out_dtype=GRAD_WIRE_DTYPE)
    tok = put_g("od_in", g_od_w_in_t)
    dhn1 = _mm(dp1, od_w_in_t, "nn", "od_dhn")
    dh2, g_norm_mix1 = _rms_bwd(dhn1, h2, W["norm_mix"][1] + tok, dh3, "mix1_norm_bwd")

    dh1, gf0 = _ffn_bwd(dh2, h1, ffn0, W["norm_ffn"][0], ff0_up_t, W["ff_conv"][0], W["ff_conv_b"][0], ff0_down, 0, put_g)
    early = dict(
        norm_ffn=jnp.concatenate([gf0["norm"], gf1["norm"]], axis=0), norm_final=g_norm_final.reshape(D_MODEL),
        od_sinks=dsk[:, :N_Q_HEADS], od_mu=g_mu, od_lnx_g=g_lnx_g, od_lnx_b=g_lnx_b, od_r_k=g_rk.reshape(N_Q_HEADS, HEAD_DIM),
        ff_conv=jnp.stack([gf0["conv"], gf1["conv"]]), ff_conv_b=jnp.concatenate([gf0["conv_b"], gf1["conv_b"]], axis=0),
        **dict(zip(_PREP_PARAMS, prep_grads[2:])))
    dy0 = _mm(dh1, ev_w_out, "nt", "ev_dy")
    g_ev_w_out = _mm(y0, dh1, "tn", "ev_gout", out_dtype=GRAD_WIRE_DTYPE)
    tok = put_g("ev_out", g_ev_w_out) + put_small(early)
    duc, g_ln_g, g_ln_b = _ev_a_norm_bwd(dy0, uc, W["ev_ln_a_g"], W["ev_ln_a_b"] + tok)
    dav, dag, g_conv_a = _ev_a_conv_bwd(duc, p0, W["ev_conv_a"])
    dgb, dgc, dxi, g_conv_b = _ev_b_bwd(dy0, p0, W["ev_conv_b"])
    dp0 = jnp.concatenate([dav, dag, dgb, dgc, dxi], axis=1)
    g_ev_w_in_t = _mm(dp0, hn0, "tn", "ev_gin", out_dtype=GRAD_WIRE_DTYPE)
    tok = put_g("ev_in", g_ev_w_in_t)
    dhn0 = _mm(dp0, ev_w_in_t, "nn", "ev_dhn")
    dh0, g_norm_mix0 = _rms_bwd(dhn0, h0, W["norm_mix"][0] + tok, dh1, "mix0_norm_bwd")

    late = dict(meta_tokens=dh0[:N_META], norm_mix=jnp.concatenate([g_norm_mix0, g_norm_mix1], axis=0),
                ev_conv_a=g_conv_a, ev_ln_a_g=g_ln_g, ev_ln_a_b=g_ln_b, ev_conv_b=g_conv_b)
    return loss, dh0[N_META:], late


HBM = pl.BlockSpec(memory_space=pl.ANY)


def _mesh_pos():
    return lax.axis_index("x"), lax.axis_index("y"), lax.axis_index("c")


def _dev(px, py, pc):
    return 4 * px + 2 * py + pc


def _all_gather(xs, name):
    n = len(xs)

    def body(*refs):
        x_refs, o_refs = refs[:n], refs[n:2 * n]
        send_sems, recv_sems, local_sems = refs[2 * n:]
        x, y, c = _mesh_pos()
        me, sibling = (x, y, c), (x, y, 1 - c)
        chips = [(1 - x, y), (x, 1 - y), (1 - x, 1 - y)]

        def copy(i, k, block, to, from_input=False):
            dst = o_refs[i].at[_dev(*block)]
            return pltpu.make_async_remote_copy(src_ref=x_refs[i] if from_input else dst, dst_ref=dst,
                                                send_sem=send_sems.at[i, k], recv_sem=recv_sems.at[i, k],
                                                device_id=to, device_id_type=MESH)

        mine = [pltpu.make_async_copy(x_refs[i], o_refs[i].at[_dev(*me)], local_sems.at[i]) for i in range(n)]
        for cp in mine:
            cp.start()
        first = []
        for i in range(n):
            first.append(copy(i, 0, me, sibling, True))
            first += [copy(i, 1 + j, me, (*chip, c), True) for j, chip in enumerate(chips)]
        for cp in first:
            cp.start()
        passed = []
        for j, chip in enumerate(chips):
            for i in range(n):
                copy(i, 1 + j, (*chip, c), me).wait_recv()
                fwd = copy(i, 4 + j, (*chip, c), sibling)
                fwd.start()
                passed.append(fwd)
        for i in range(n):
            copy(i, 0, sibling, me).wait_recv()
            for j, chip in enumerate(chips):
                copy(i, 4 + j, (*chip, 1 - c), me).wait_recv()
        for cp in first + passed:
            cp.wait_send()
        for cp in mine:
            cp.wait()

    return _pc(body, name=name, in_specs=[HBM] * n, out_specs=tuple([HBM] * n),
               out_shape=tuple(S((N_DEV,) + x.shape, x.dtype) for x in xs),
               scratch_shapes=[pltpu.SemaphoreType.DMA((n, 7)), pltpu.SemaphoreType.DMA((n, 7)),
                               pltpu.SemaphoreType.DMA((n,))])(*xs)


HBM_SPEC = pl.BlockSpec(memory_space=pltpu.HBM)
SEM_SPEC = pl.BlockSpec(memory_space=pltpu.SEMAPHORE)
DATAFLOW = pltpu.SideEffectType.DATAFLOW_SIDE_EFFECTING
_PEER_FLIPS = ((1, 0, 0), (0, 1, 0), (1, 1, 0), (1, 0, 1), (0, 1, 1), (1, 1, 1), (0, 0, 1))
N_PEERS = len(_PEER_FLIPS)


def _peers(x, y, c):
    return [((1 - x) if fx else x, (1 - y) if fy else y, (1 - c) if fc else c) for fx, fy, fc in _PEER_FLIPS]


def _xchg_start(srcs, lands, scatter, name):
    n = len(srcs)

    def body(*refs):
        src_refs, land_refs = refs[:n], refs[n:2 * n]
        send_sems, recv_sems, token = refs[2 * n], refs[2 * n + 1], refs[-1]
        x, y, c = _mesh_pos()
        me = _dev(x, y, c)
        for i in range(n):
            for k, peer in enumerate(_peers(x, y, c)):
                pltpu.make_async_remote_copy(src_ref=src_refs[i].at[_dev(*peer)] if scatter else src_refs[i],
                                             dst_ref=land_refs[i].at[me], send_sem=send_sems.at[i * N_PEERS + k],
                                             recv_sem=recv_sems.at[i * N_PEERS + k], device_id=peer, device_id_type=MESH).start()
        token[...] = jnp.zeros_like(token)

    arrs = list(srcs) + list(lands)
    outs = _pc(body, name=name,
               out_shape=(pltpu.SemaphoreType.DMA((n * N_PEERS,)), pltpu.SemaphoreType.DMA((n * N_PEERS,)),
                          *[pltpu.HBM(a.shape, a.dtype) for a in arrs], S((SUBLANES, LANES), f32)),
               in_specs=[HBM_SPEC] * (2 * n),
               out_specs=(SEM_SPEC, SEM_SPEC, *[HBM_SPEC] * (2 * n), pl.BlockSpec(memory_space=pltpu.VMEM)),
               input_output_aliases={i: 2 + i for i in range(2 * n)},
               compiler_params=pltpu.CompilerParams(has_side_effects=DATAFLOW))(
        *[pltpu.with_memory_space_constraint(a, pltpu.HBM) for a in arrs])
    return (outs[0], outs[1], list(outs[2:2 + n]), list(outs[2 + n:2 + 2 * n]), scatter), outs[-1]


def _xchg_wait(handle, after, name):
    send_sems, recv_sems, srcs, lands, scatter = handle
    n = len(srcs)

    def body(*refs):
        src_refs, land_refs = refs[:n], refs[n:2 * n]
        send, recv = refs[2 * n], refs[2 * n + 1]
        x, y, c = _mesh_pos()
        for i in range(n):
            for k in range(N_PEERS):
                cp = pltpu.make_async_remote_copy(src_ref=src_refs[i].at[0] if scatter else src_refs[i],
                                                  dst_ref=land_refs[i].at[0], send_sem=send.at[i * N_PEERS + k],
                                                  recv_sem=recv.at[i * N_PEERS + k],
                                                  device_id=(x, y, c), device_id_type=MESH)
                cp.wait_send()
                cp.wait_recv()

    arrs = srcs + lands
    outs = _pc(body, name=name, out_shape=tuple(pltpu.HBM(a.shape, a.dtype) for a in arrs),
               in_specs=[HBM_SPEC] * (2 * n) + [SEM_SPEC, SEM_SPEC, pl.BlockSpec(memory_space=pl.ANY)],
               out_specs=tuple([HBM_SPEC] * (2 * n)), input_output_aliases={i: i for i in range(2 * n)},
               compiler_params=pltpu.CompilerParams(has_side_effects=DATAFLOW))(*arrs, send_sems, recv_sems, after)
    return list(outs[:n]), list(outs[n:])


def _rs_sum(g, land, me_vec, name):
    _, r, cols = g.shape
    tr = _divisor_block(r, 16, min(r, 352))

    def body(me_ref, g_ref, *rest):
        o_ref = rest[-1]
        acc = g_ref[0].astype(f32)
        for l_ref in rest[:-1]:
            acc = acc + l_ref[0].astype(f32)
        o_ref[...] = acc

    blk = lambda f: pl.BlockSpec((1, tr, cols), f)
    grid_spec = pltpu.PrefetchScalarGridSpec(
        num_scalar_prefetch=1, grid=(r // tr,),
        in_specs=[blk(lambda i, me_ref: (me_ref[0], i, 0))]
        + [blk(lambda i, me_ref, k=k: ((me_ref[0] + k) % N_DEV, i, 0)) for k in range(1, N_DEV)],
        out_specs=pl.BlockSpec((tr, cols), lambda i, me_ref: (i, 0)))
    return _pc(body, name=name, grid_spec=grid_spec, out_shape=S((r, cols), f32),
               compiler_params=_cparams(("arbitrary",)))(me_vec, g, *([land] * (N_DEV - 1)))


def _sum_devices(a, name):
    def body(a_ref, o_ref):
        acc = a_ref[0]
        for d in range(1, N_DEV):
            acc = acc + a_ref[d]
        o_ref[...] = acc

    return _pc(body, name=name, grid=(1,), in_specs=[_full(a.shape)], out_specs=_full(a.shape[1:]),
               out_shape=S(a.shape[1:], a.dtype), compiler_params=_cparams(("arbitrary",)))(a)


def _adamw(w, m, v, g, name):
    shape = w.shape
    w2, m2, v2, g2 = (a.reshape(-1, shape[-1]) for a in (w, m, v, g))
    rows, cols = w2.shape
    tr = rows if rows % SUBLANES else _divisor_block(rows, SUBLANES, max(SUBLANES, min(rows, ADAMW_BLOCK_ELEMS // cols)))
    c1, c2 = 1.0 - ADAM_B1 ** ADAM_STEP, 1.0 - ADAM_B2 ** ADAM_STEP

    def body(w_ref, m_ref, v_ref, g_ref, d_ref, nm_ref, nv_ref):
        gv = g_ref[...]
        nm = ADAM_B1 * m_ref[...] + (1.0 - ADAM_B1) * gv
        nv = ADAM_B2 * v_ref[...] + (1.0 - ADAM_B2) * (gv * gv)
        d_ref[...] = -ADAM_LR * ((nm / c1) / (jnp.sqrt(nv / c2) + ADAM_EPS) + ADAM_WD * w_ref[...])
        nm_ref[...] = nm
        nv_ref[...] = nv

    blk = pl.BlockSpec((tr, cols), lambda i: (i, 0))
    outs = _pc(body, name=name, grid=(rows // tr,), in_specs=[blk] * 4, out_specs=(blk,) * 3,
               out_shape=(S((rows, cols), f32),) * 3, compiler_params=_cparams(("arbitrary",)))(w2, m2, v2, g2)
    return tuple(o.reshape(shape) for o in outs)


_WEIGHTS = ("meta_tokens", "norm_mix", "norm_ffn", "norm_final", "ev_w_in", "ev_conv_a", "ev_ln_a_g", "ev_ln_a_b",
            "ev_conv_b", "ev_w_out", "od_w_in", "od_sinks", "od_mu", "od_w0", "od_w2", "od_a0", "od_a2", "od_g2",
            "od_k_k", "od_k_a", "od_r_k", "od_lnx_g", "od_lnx_b", "od_w_out", "ff_w_up", "ff_conv", "ff_conv_b", "ff_w_down")
_SMALL_SHARDED = (("meta_tokens", 1), ("ev_conv_a", 2), ("ev_conv_b", 2), ("od_mu", 1), ("od_w0", 1), ("od_w2", 2),
                  ("od_a0", 1), ("od_a2", 2), ("od_g2", 2), ("od_k_k", 1), ("od_k_a", 1), ("od_lnx_g", 1),
                  ("od_lnx_b", 1), ("ff_conv", 2))
_SMALL_REPLICATED = ("norm_mix", "norm_ffn", "norm_final", "ev_ln_a_g", "ev_ln_a_b", "od_sinks", "od_r_k", "ff_conv_b")
SLAB_UNIT = SUBLANES * LANES


def _pack(arrs):
    flat = jnp.concatenate([a.reshape(-1).astype(f32) for a in arrs])
    pad = (-flat.shape[0]) % SLAB_UNIT
    return jnp.pad(flat, (0, pad)).reshape(-1, LANES)


def _unpack(flat, shapes):
    out, off = [], 0
    for shp in shapes:
        size = 1
        for s in shp:
            size *= s
        out.append(flat[..., off:off + size].reshape(flat.shape[:-1] + tuple(shp)))
        off += size
    return out


def _full_shape(shape, axis):
    return tuple(N_DEV * s if i == axis else s for i, s in enumerate(shape))


def kernel(x, meta_tokens, norm_mix, norm_ffn, norm_final, ev_w_in, ev_conv_a, ev_ln_a_g, ev_ln_a_b, ev_conv_b, ev_w_out, od_w_in, od_sinks, od_mu, od_w0, od_w2, od_a0, od_a2, od_g2, od_k_k, od_k_a, od_r_k, od_lnx_g, od_lnx_b, od_w_out, ff_w_up, ff_conv, ff_conv_b, ff_w_down, loss_target, m_meta_tokens, m_norm_mix, m_norm_ffn, m_norm_final, m_ev_w_in, m_ev_conv_a, m_ev_ln_a_g, m_ev_ln_a_b, m_ev_conv_b, m_ev_w_out, m_od_w_in, m_od_sinks, m_od_mu, m_od_w0, m_od_w2, m_od_a0, m_od_a2, m_od_g2, m_od_k_k, m_od_k_a, m_od_r_k, m_od_lnx_g, m_od_lnx_b, m_od_w_out, m_ff_w_up, m_ff_conv, m_ff_conv_b, m_ff_w_down, v_meta_tokens, v_norm_mix, v_norm_ffn, v_norm_final, v_ev_w_in, v_ev_conv_a, v_ev_ln_a_g, v_ev_ln_a_b, v_ev_conv_b, v_ev_w_out, v_od_w_in, v_od_sinks, v_od_mu, v_od_w0, v_od_w2, v_od_a0, v_od_a2, v_od_g2, v_od_k_k, v_od_k_a, v_od_r_k, v_od_lnx_g, v_od_lnx_b, v_od_w_out, v_ff_w_up, v_ff_conv, v_ff_conv_b, v_ff_w_down):
    A = dict(locals())
    px, py, pc = _mesh_pos()
    me = _dev(px, py, pc)
    me_vec = jnp.reshape(me, (1,)).astype(jnp.int32)
    rows = lambda a: a.reshape(N_DEV * a.shape[1], a.shape[2])
    blocks = lambda a: a.reshape(N_DEV, a.shape[0] // N_DEV, a.shape[1])

    shards = dict(ev_in=ev_w_in[0].T, ev_out=ev_w_out[0], ff0_up=ff_w_up[0].T, ff0_down=ff_w_down[0], od_in=od_w_in[0].T,
                  od_out=od_w_out[0], ff1_up=ff_w_up[1].T, ff1_down=ff_w_down[1])
    shards = {n: b.astype(bf16) for n, b in shards.items()}
    small_shapes = [A[n].shape for n, _ in _SMALL_SHARDED]
    gathered = _all_gather([shards["ev_in"], shards["ev_out"], _pack([A[n] for n, _ in _SMALL_SHARDED])], "gather_first")
    gathered, shards = lax.optimization_barrier((gathered, shards))
    fetch, tok0 = {}, jnp.zeros((), f32)
    for n in ("ff0_up", "ff0_down", "od_in", "od_out", "ff1_up", "ff1_down"):
        shard, tok0 = lax.optimization_barrier((shards[n], tok0))
        land = lax.dynamic_update_slice(lax.empty((N_DEV,) + shard.shape, bf16), shard[None], (me, 0, 0))
        fetch[n], token = _xchg_start([shard], [land], False, f"gather_{n}_start")
        tok0 = tok0 + token[0, 0]

    def get_w(n, after):
        if n in ("ev_in", "ev_out"):
            return rows(gathered[("ev_in", "ev_out").index(n)])
        return rows(_xchg_wait(fetch[n], after, f"gather_{n}_wait")[1][0])

    W = {}
    for (n, ax), seg in zip(_SMALL_SHARDED, _unpack(gathered[-1].reshape(N_DEV, -1), small_shapes)):
        W[n] = jnp.moveaxis(seg, 0, ax).reshape(_full_shape(A[n].shape, ax))
    for n in ("ev_conv_a", "ev_conv_b", "od_w2", "od_a2", "od_g2"):
        W[n] = W[n][0]
    for n in _SMALL_REPLICATED:
        W[n] = A[n]
    W["od_r_k"] = od_r_k[0]

    small_shape = {n: _full_shape(A[n].shape, ax) for n, ax in _SMALL_SHARDED}
    small_shape.update({n: A[n].shape for n in _SMALL_REPLICATED})
    sent, small_sent, small_names = {}, {}, {}

    def put_g(n, g):
        g8 = blocks(g)
        sent[n], token = _xchg_start([g8], [lax.empty(g8.shape, g8.dtype)], True, f"reduce_{n}_start")
        return token[0, 0]

    def put_small(gs, stage="early"):
        small_names[stage] = sorted(gs)
        slab = _pack([gs[n] for n in small_names[stage]])
        land = lax.dynamic_update_slice(lax.empty((N_DEV,) + slab.shape, f32), slab[None], (me, 0, 0))
        small_sent[stage], small_tok[stage] = _xchg_start([slab], [land], False, f"gather_{stage}_small_grads_start")
        return small_tok[stage][0, 0]

    small_tok = {}
    loss_tile, grad_x, late = _local_step(x[0], loss_target[0], W, get_w, put_g, put_small, tok0)
    put_small(late, "late")
    late_tok = small_tok["late"]

    gsh, prev = {}, late_tok
    for n in ("ff1_down", "ff1_up", "od_out", "od_in", "ff0_down", "ff0_up", "ev_out", "ev_in"):
        srcs, lands = _xchg_wait(sent[n], prev, f"reduce_{n}_wait")
        gsh[n] = prev = _rs_sum(srcs[0], lands[0], me_vec, f"reduce_{n}_sum")
    grads = dict(ev_w_in=gsh["ev_in"].T[None], ev_w_out=gsh["ev_out"][None], od_w_in=gsh["od_in"].T[None],
                 od_w_out=gsh["od_out"][None], ff_w_up=jnp.stack([gsh["ff0_up"].T, gsh["ff1_up"].T]),
                 ff_w_down=jnp.stack([gsh["ff0_down"], gsh["ff1_down"]]))

    delta, new_m, new_v = {}, {}, {}
    for n in ("ff_w_up", "ff_w_down", "od_w_in", "od_w_out", "ev_w_in", "ev_w_out"):
        delta[n], new_m[n], new_v[n] = _adamw(A[n], A["m_" + n], A["v_" + n], grads[n], "adamw_" + n)
    for stage in ("early", "late"):
        gsm = _xchg_wait(small_sent[stage], delta["ev_w_in"], f"gather_{stage}_small_grads_wait")[1][0]
        summed = _sum_devices(gsm, f"sum_{stage}_small_grads").reshape(-1)
        for n, full in zip(small_names[stage], _unpack(summed, [small_shape[n] for n in small_names[stage]])):
            grads[n] = full
    for n, ax in _SMALL_SHARDED:
        size = A[n].shape[ax]
        grads[n] = lax.dynamic_slice_in_dim(grads[n], me * size, size, axis=ax)
    for n in small_shape:
        delta[n], new_m[n], new_v[n] = _adamw(A[n], A["m_" + n], A["v_" + n], grads[n], "adamw_" + n)

    loss = lax.psum(loss_tile[0, 0], ("x", "y", "c"))
    return (loss, grad_x[None], *[grads[n] for n in _WEIGHTS], *[delta[n] for n in _WEIGHTS],
            *[new_m[n] for n in _WEIGHTS], *[new_v[n] for n in _WEIGHTS])
```

```python
import jax
import jax.numpy as jnp
from jax import lax
from jax.experimental import pallas as pl
from jax.experimental.pallas import tpu as pltpu

f32, bf16 = jnp.float32, jnp.bfloat16

D_MODEL = 1024
N_META = 16
RMS_EPS = 1e-6
LN_EPS = 1e-5
D_A = 512
CONV_A_WIDTH = 31
CONV_B_WIDTH = 3
HEAD_DIM = 64
N_Q_HEADS = 8
N_KV_HEADS = 2
GQA_GROUP = 4
D_ATT = 512
D_KV = 128
BLOCK = 128
ROPE_THETA = 10000.0
D_R = 512
LORA_W, LORA_A, LORA_G = 64, 64, 128
RWKV_GN_EPS = 64e-5
ATT_COLS = D_ATT + 2 * D_KV
RWKV_COLS = 3 * D_R + LORA_W + LORA_A + LORA_G
D_FF = 2816
FF_CONV_WIDTH = 3
NEG_INF = -1e30
ATT_PAD = BLOCK - N_META
ATT_SCALE = HEAD_DIM ** -0.5

ADAM_LR, ADAM_B1, ADAM_B2, ADAM_EPS, ADAM_WD, ADAM_STEP = 0.001, 0.9, 0.999, 1e-08, 0.01, 10

N_DEV = 8
LANES = 128
SUBLANES = 8
SCAN_CHUNK = 48
PAIR_ROWS = 4 * HEAD_DIM
V7X_VMEM_LIMIT = 56 * 1024 * 1024
ADAMW_BLOCK_ELEMS = 400 * 1024
GRAD_WIRE_DTYPE = bf16
MESH = pl.DeviceIdType.MESH
S = jax.ShapeDtypeStruct
HIGHEST = lax.Precision.HIGHEST


def _pc(body, **kw):
    return pl.pallas_call(body, **kw)


def _cparams(sem=None):
    return pltpu.CompilerParams(dimension_semantics=sem, vmem_limit_bytes=V7X_VMEM_LIMIT)


def _divisor_block(t, unit, limit):
    best = unit
    for rb in range(unit, limit + 1, unit):
        if t % rb == 0:
            best = rb
    assert t % best == 0, (t, unit)
    return best


def _row_block(t):
    return _divisor_block(t, 16, 704)


def _row_block8(t):
    return _divisor_block(t, 8, 344)


def _col_tile(n, cap):
    return _divisor_block(n, LANES, min(n, cap)) if n % LANES == 0 else n


def _full(shape):
    nd = len(shape)
    return pl.BlockSpec(shape, lambda *_: (0,) * nd)


def _sigmoid(x):
    return jax.nn.sigmoid(x)


_DIMS = {"nn": (((1,), (0,)), ((), ())), "nt": (((1,), (1,)), ((), ())), "tn": (((0,), (0,)), ((), ()))}
MM_MAX_K = 2816
MM_MAX_TM = 704
MM_MAX_TN = 1408


def _mm(a, b, mode, name, out_dtype=f32, res=None):
    if mode == "nn":
        (m, k), (k2, n) = a.shape, b.shape
    elif mode == "nt":
        (m, k), (n, k2) = a.shape, b.shape
    else:
        (k, m), (k2, n) = a.shape, b.shape
    assert k == k2, (a.shape, b.shape, mode)
    tm = _row_block(m) if m % LANES else _col_tile(m, MM_MAX_TM)
    tn = _col_tile(n, MM_MAX_TN)
    nk = 1 if (mode == "tn" or k <= MM_MAX_K) else k // MM_MAX_K
    tk = k // nk
    assert tk * nk == k
    dims = _DIMS[mode]

    def body(a_ref, b_ref, *rest):
        part = lax.dot_general(a_ref[...].astype(bf16), b_ref[...].astype(bf16), dims, preferred_element_type=f32)
        if nk == 1:
            o_ref = rest[-1]
            if res is not None:
                part = part + rest[0][...]
            o_ref[...] = part.astype(out_dtype)
            return
        o_ref, acc_ref = rest[-2], rest[-1]
        kk = pl.program_id(2)

        @pl.when(kk == 0)
        def _():
            acc_ref[...] = part

        @pl.when(kk > 0)
        def _():
            acc_ref[...] += part

        @pl.when(kk == nk - 1)
        def _():
            acc = acc_ref[...]
            if res is not None:
                acc = acc + rest[0][...]
            o_ref[...] = acc.astype(out_dtype)

    if mode == "tn":
        a_spec = pl.BlockSpec((k, tm), lambda i, j, kk: (0, i))
    else:
        a_spec = pl.BlockSpec((tm, tk), lambda i, j, kk: (i, kk))
    if mode == "nt":
        b_spec = pl.BlockSpec((tn, tk), lambda i, j, kk: (j, kk))
    else:
        b_spec = pl.BlockSpec((tk, tn), lambda i, j, kk: (kk, j))
    o_spec = pl.BlockSpec((tm, tn), lambda i, j, kk: (i, j))
    ins, specs = [a, b], [a_spec, b_spec]
    if res is not None:
        ins.append(res)
        specs.append(o_spec)
    scratch = [pltpu.VMEM((tm, tn), f32)] if nk > 1 else []
    return _pc(body, name=name, grid=(m // tm, n // tn, nk), in_specs=specs, out_specs=o_spec,
               out_shape=S((m, n), out_dtype), scratch_shapes=scratch,
               compiler_params=_cparams(("arbitrary", "arbitrary", "arbitrary")))(*ins)


def _rms_fwd(x, g, name):
    t, d = x.shape
    rb = _row_block(t)

    def body(x_ref, g_ref, o_ref):
        xv = x_ref[...]
        rstd = lax.rsqrt(jnp.mean(xv * xv, axis=-1, keepdims=True) + RMS_EPS)
        o_ref[...] = (xv * rstd * g_ref[...]).astype(bf16)

    row = pl.BlockSpec((rb, d), lambda i: (i, 0))
    return _pc(body, name=name, grid=(t // rb,), in_specs=[row, _full((1, d))], out_specs=row,
               out_shape=S((t, d), bf16), compiler_params=_cparams(("arbitrary",)))(x, g.reshape(1, d))


def _rms_bwd(dy, x, g, dres, name):
    t, d = x.shape
    rb = _row_block8(t)

    def body(dy_ref, x_ref, g_ref, dres_ref, dx_ref, dg_ref):
        @pl.when(pl.program_id(0) == 0)
        def _():
            dg_ref[...] = jnp.zeros_like(dg_ref)
        xv, dyv = x_ref[...], dy_ref[...]
        rstd = lax.rsqrt(jnp.mean(xv * xv, axis=-1, keepdims=True) + RMS_EPS)
        xn = xv * rstd
        dg_ref[...] += jnp.sum(dyv * xn, axis=0, keepdims=True)
        dxh = dyv * g_ref[...]
        dx_ref[...] = dres_ref[...] + rstd * (dxh - xn * jnp.mean(dxh * xn, axis=-1, keepdims=True))

    row = pl.BlockSpec((rb, d), lambda i: (i, 0))
    return _pc(body, name=name, grid=(t // rb,), in_specs=[row, row, _full((1, d)), row],
               out_specs=(row, _full((1, d))), out_shape=(S((t, d), f32), S((1, d), f32)),
               compiler_params=_cparams(("arbitrary",)))(dy, x, g.reshape(1, d), dres)


def _final_loss(h, g, target_padded):
    t, d = h.shape
    rb = _row_block8(t)

    def body(x_ref, g_ref, t_ref, loss_ref, dx_ref, dg_ref):
        i = pl.program_id(0)

        @pl.when(i == 0)
        def _():
            dg_ref[...] = jnp.zeros_like(dg_ref)
            loss_ref[...] = jnp.zeros_like(loss_ref)
        xv = x_ref[...]
        rstd = lax.rsqrt(jnp.mean(xv * xv, axis=-1, keepdims=True) + RMS_EPS)
        xn = xv * rstd
        gv = g_ref[...]
        row = i * rb + lax.broadcasted_iota(jnp.int32, (rb, 1), 0)
        diff = jnp.where(row >= N_META, xn * gv - t_ref[...], 0.0)
        loss_ref[...] += 0.5 * jnp.sum(jnp.mean(diff * diff, axis=-1, keepdims=True))
        dout = diff * (1.0 / d)
        dg_ref[...] += jnp.sum(dout * xn, axis=0, keepdims=True)
        dxh = dout * gv
        dx_ref[...] = rstd * (dxh - xn * jnp.mean(dxh * xn, axis=-1, keepdims=True))

    row = pl.BlockSpec((rb, d), lambda i: (i, 0))
    return _pc(body, name="final_loss", grid=(t // rb,), in_specs=[row, _full((1, d)), row],
               out_specs=(_full((SUBLANES, LANES)), row, _full((1, d))),
               out_shape=(S((SUBLANES, LANES), f32), S((t, d), f32), S((1, d), f32)),
               compiler_params=_cparams(("arbitrary",)))(h, g.reshape(1, d), target_padded)


CONV_LEAD = 32


def _fill_front_padded(pad_ref, x, t):
    pad_ref[0:CONV_LEAD, :] = jnp.zeros((CONV_LEAD, x.shape[1]), f32)
    pad_ref[CONV_LEAD:CONV_LEAD + t, :] = x


def _fill_back_padded(pad_ref, x, t):
    pad_ref[0:t, :] = x
    pad_ref[t:t + CONV_LEAD, :] = jnp.zeros((CONV_LEAD, x.shape[1]), f32)


def _conv_rows(pad_ref, w_ref, kw, r0, nr):
    acc = None
    for j in range(kw):
        lo = CONV_LEAD + r0 - (kw - 1) + j
        term = w_ref[j:j + 1, :] * pad_ref[lo:lo + nr, :]
        acc = term if acc is None else acc + term
    return acc


def _conv_t_rows(padb_ref, w_ref, kw, r0, nr):
    acc = None
    for j in range(kw):
        lo = r0 + (kw - 1) - j
        term = w_ref[j:j + 1, :] * padb_ref[lo:lo + nr, :]
        acc = term if acc is None else acc + term
    return acc


def _conv_dw_rows(dy_blk, pad_ref, kw, r0, nr):
    out = []
    for j in range(kw):
        lo = CONV_LEAD + r0 - (kw - 1) + j
        out.append(jnp.sum(dy_blk * pad_ref[lo:lo + nr, :], axis=0, keepdims=True))
    return out


def _acc_list(a, b):
    return b if a is None else [x + y for x, y in zip(a, b)]


def _ev_a_conv(p, conv_a):
    t = p.shape[0]
    cr = _row_block8(t)
    nb = D_A // LANES

    def body(av_ref, ag_ref, w_ref, o_ref, pad_ref):
        _fill_front_padded(pad_ref, av_ref[...] * _sigmoid(ag_ref[...]), t)
        for r in range(t // cr):
            o_ref[r * cr:(r + 1) * cr, :] = _conv_rows(pad_ref, w_ref, CONV_A_WIDTH, r * cr, cr)

    col = lambda off: pl.BlockSpec((t, LANES), lambda j: (0, j + off))
    return _pc(body, name="ev_a_conv", grid=(nb,),
               in_specs=[col(0), col(nb), pl.BlockSpec((CONV_A_WIDTH, LANES), lambda j: (0, j))],
               out_specs=col(0), out_shape=S((t, D_A), f32),
               scratch_shapes=[pltpu.VMEM((t + CONV_LEAD, LANES), f32)],
               compiler_params=_cparams(("arbitrary",)))(p, p, conv_a)


def _ln_silu(uc, g, b):
    mu = jnp.mean(uc, axis=-1, keepdims=True)
    xc = uc - mu
    var = jnp.mean(xc * xc, axis=-1, keepdims=True)
    y = xc * lax.rsqrt(var + LN_EPS) * g + b
    return y * _sigmoid(y)


def _ev_a_norm(uc, g, b):
    t, d = uc.shape
    rb = _row_block(t)

    def body(u_ref, g_ref, b_ref, o_ref):
        o_ref[...] = _ln_silu(u_ref[...], g_ref[...], b_ref[...]).astype(bf16)

    row = pl.BlockSpec((rb, d), lambda i: (i, 0))
    return _pc(body, name="ev_a_norm", grid=(t // rb,), in_specs=[row, _full((1, d)), _full((1, d))],
               out_specs=row, out_shape=S((t, 2 * d), bf16), compiler_params=_cparams(("arbitrary",)))(uc, g, b)


def _ev_a_norm_bwd(dy, uc, g, b):
    t, d = uc.shape
    rb = _row_block8(t)

    def body(dy_ref, u_ref, g_ref, b_ref, du_ref, dg_ref, db_ref):
        @pl.when(pl.program_id(0) == 0)
        def _():
            dg_ref[...] = jnp.zeros_like(dg_ref)
            db_ref[...] = jnp.zeros_like(db_ref)
        _, vjp = jax.vjp(_ln_silu, u_ref[...], g_ref[...], b_ref[...])
        du, dg, db = vjp(dy_ref[...])
        du_ref[...] = du
        dg_ref[...] += dg
        db_ref[...] += db

    row = pl.BlockSpec((rb, d), lambda i: (i, 0))
    return _pc(body, name="ev_a_norm_bwd", grid=(t // rb,), in_specs=[row, row, _full((1, d)), _full((1, d))],
               out_specs=(row, _full((1, d)), _full((1, d))),
               out_shape=(S((t, d), f32), S((1, d), f32), S((1, d), f32)),
               compiler_params=_cparams(("arbitrary",)))(dy, uc, g, b)


def _ev_a_conv_bwd(duc, p, conv_a):
    t = p.shape[0]
    cr = _row_block8(t)
    nb = D_A // LANES

    def body(dy_ref, av_ref, ag_ref, w_ref, dav_ref, dag_ref, dw_ref, pad_ref, padb_ref):
        _fill_front_padded(pad_ref, av_ref[...] * _sigmoid(ag_ref[...]), t)
        _fill_back_padded(padb_ref, dy_ref[...], t)
        dw = None
        for r in range(t // cr):
            rows = slice(r * cr, (r + 1) * cr)
            du = _conv_t_rows(padb_ref, w_ref, CONV_A_WIDTH, r * cr, cr)
            avr = av_ref[rows, :]
            sgr = _sigmoid(ag_ref[rows, :])
            dav_ref[rows, :] = du * sgr
            dag_ref[rows, :] = du * avr * sgr * (1.0 - sgr)
            dw = _acc_list(dw, _conv_dw_rows(dy_ref[rows, :], pad_ref, CONV_A_WIDTH, r * cr, cr))
        for j in range(CONV_A_WIDTH):
            dw_ref[j:j + 1, :] = dw[j]

    col = lambda off: pl.BlockSpec((t, LANES), lambda j: (0, j + off))
    wsp = pl.BlockSpec((CONV_A_WIDTH, LANES), lambda j: (0, j))
    return _pc(body, name="ev_a_conv_bwd", grid=(nb,), in_specs=[col(0), col(0), col(nb), wsp],
               out_specs=(col(0), col(0), wsp),
               out_shape=(S((t, D_A), f32), S((t, D_A), f32), S((CONV_A_WIDTH, D_A), f32)),
               scratch_shapes=[pltpu.VMEM((t + CONV_LEAD, LANES), f32), pltpu.VMEM((t + CONV_LEAD, LANES), f32)],
               compiler_params=_cparams(("arbitrary",)))(duc, p, p, conv_a)


def _ev_b(p, conv_b, y):
    t = p.shape[0]
    cr = _row_block8(t)
    nb = D_A // LANES

    def body(gb_ref, gc_ref, xi_ref, w_ref, y_ref, o_ref, pad_ref, stage_ref):
        _fill_front_padded(pad_ref, gc_ref[...] * xi_ref[...], t)
        for r in range(t // cr):
            rows = slice(r * cr, (r + 1) * cr)
            stage_ref[rows, :] = gb_ref[rows, :] * _conv_rows(pad_ref, w_ref, CONV_B_WIDTH, r * cr, cr)
        o_ref[...] = stage_ref[...].astype(bf16)

    col = lambda off: pl.BlockSpec((t, LANES), lambda j: (0, j + off))
    return _pc(body, name="ev_b", grid=(nb,),
               in_specs=[col(2 * nb), col(3 * nb), col(4 * nb), pl.BlockSpec((CONV_B_WIDTH, LANES), lambda j: (0, j)), HBM],
               out_specs=col(nb), out_shape=S(y.shape, bf16), input_output_aliases={4: 0},
               scratch_shapes=[pltpu.VMEM((t + CONV_LEAD, LANES), f32), pltpu.VMEM((t, LANES), f32)],
               compiler_params=_cparams(("arbitrary",)))(p, p, p, conv_b, y)


def _ev_b_bwd(dy, p, conv_b):
    t = p.shape[0]
    cr = _row_block8(t)
    nb = D_A // LANES

    def body(dy_ref, gb_ref, gc_ref, xi_ref, w_ref, dgb_ref, dgc_ref, dxi_ref, dw_ref, pad_ref, padb_ref):
        _fill_front_padded(pad_ref, gc_ref[...] * xi_ref[...], t)
        _fill_back_padded(padb_ref, dy_ref[...] * gb_ref[...], t)
        dw = None
        for r in range(t // cr):
            rows = slice(r * cr, (r + 1) * cr)
            dgb_ref[rows, :] = dy_ref[rows, :] * _conv_rows(pad_ref, w_ref, CONV_B_WIDTH, r * cr, cr)
            dcx = _conv_t_rows(padb_ref, w_ref, CONV_B_WIDTH, r * cr, cr)
            dgc_ref[rows, :] = dcx * xi_ref[rows, :]
            dxi_ref[rows, :] = dcx * gc_ref[rows, :]
            dw = _acc_list(dw, _conv_dw_rows(padb_ref[rows, :], pad_ref, CONV_B_WIDTH, r * cr, cr))
        for j in range(CONV_B_WIDTH):
            dw_ref[j:j + 1, :] = dw[j]

    col = lambda off: pl.BlockSpec((t, LANES), lambda j: (0, j + off))
    wsp = pl.BlockSpec((CONV_B_WIDTH, LANES), lambda j: (0, j))
    return _pc(body, name="ev_b_bwd", grid=(nb,), in_specs=[col(nb), col(2 * nb), col(3 * nb), col(4 * nb), wsp],
               out_specs=(col(0), col(0), col(0), wsp),
               out_shape=(S((t, D_A), f32), S((t, D_A), f32), S((t, D_A), f32), S((CONV_B_WIDTH, D_A), f32)),
               scratch_shapes=[pltpu.VMEM((t + CONV_LEAD, LANES), f32), pltpu.VMEM((t + CONV_LEAD, LANES), f32)],
               compiler_params=_cparams(("arbitrary",)))(dy, p, p, p, conv_b)


def _ffn_mid(u, conv_w, conv_b, name):
    t = u.shape[0]
    cr = _row_block8(t)
    nb = D_FF // LANES

    def body(gt_ref, vl_ref, w_ref, b_ref, o_ref, pad_ref, stage_ref):
        _fill_front_padded(pad_ref, gt_ref[...], t)
        for r in range(t // cr):
            rows = slice(r * cr, (r + 1) * cr)
            gc = _conv_rows(pad_ref, w_ref, FF_CONV_WIDTH, r * cr, cr) + b_ref[...]
            stage_ref[rows, :] = gc * _sigmoid(gc) * vl_ref[rows, :]
        o_ref[...] = stage_ref[...].astype(bf16)

    col = lambda off: pl.BlockSpec((t, LANES), lambda j: (0, j + off))
    return _pc(body, name=name, grid=(nb,),
               in_specs=[col(0), col(nb), pl.BlockSpec((FF_CONV_WIDTH, LANES), lambda j: (0, j)),
                         pl.BlockSpec((1, LANES), lambda j: (0, j))],
               out_specs=col(0), out_shape=S((t, D_FF), bf16),
               scratch_shapes=[pltpu.VMEM((t + CONV_LEAD, LANES), f32), pltpu.VMEM((t, LANES), f32)],
               compiler_params=_cparams(("arbitrary",)))(u, u, conv_w, conv_b.reshape(1, D_FF))


def _ffn_mid_bwd(dz, u, conv_w, conv_b, name):
    t = u.shape[0]
    cr = _row_block8(t)
    nb = D_FF // LANES

    def body(dz_ref, gt_ref, vl_ref, w_ref, b_ref, du_ref, dw_ref, db_ref, pad_ref, padb_ref, dval_ref):
        s = pl.program_id(1)

        @pl.when(s == 0)
        def _():
            _fill_front_padded(pad_ref, gt_ref[...], t)
            for r in range(t // cr):
                rows = slice(r * cr, (r + 1) * cr)
                gc = _conv_rows(pad_ref, w_ref, FF_CONV_WIDTH, r * cr, cr) + b_ref[...]
                sg = _sigmoid(gc)
                dzr = dz_ref[rows, :]
                dval_ref[rows, :] = dzr * gc * sg
                padb_ref[rows, :] = dzr * vl_ref[rows, :] * sg * (1.0 + gc * (1.0 - sg))
            padb_ref[t:t + CONV_LEAD, :] = jnp.zeros((CONV_LEAD, LANES), f32)
            dw, db = None, None
            for r in range(t // cr):
                rows = slice(r * cr, (r + 1) * cr)
                du_ref[rows, :] = _conv_t_rows(padb_ref, w_ref, FF_CONV_WIDTH, r * cr, cr)
                dgc = padb_ref[rows, :]
                dw = _acc_list(dw, _conv_dw_rows(dgc, pad_ref, FF_CONV_WIDTH, r * cr, cr))
                pb = jnp.sum(dgc, axis=0, keepdims=True)
                db = pb if db is None else db + pb
            for j in range(FF_CONV_WIDTH):
                dw_ref[j:j + 1, :] = dw[j]
            db_ref[...] = db

        @pl.when(s == 1)
        def _():
            du_ref[...] = dval_ref[...]

    col = lambda off: pl.BlockSpec((t, LANES), lambda j, s: (0, j + off))
    wsp = pl.BlockSpec((FF_CONV_WIDTH, LANES), lambda j, s: (0, j))
    bsp = pl.BlockSpec((1, LANES), lambda j, s: (0, j))
    return _pc(body, name=name, grid=(nb, 2), in_specs=[col(0), col(0), col(nb), wsp, bsp],
               out_specs=(pl.BlockSpec((t, LANES), lambda j, s: (0, s * nb + j)), wsp, bsp),
               out_shape=(S((t, 2 * D_FF), f32), S((FF_CONV_WIDTH, D_FF), f32), S((1, D_FF), f32)),
               scratch_shapes=[pltpu.VMEM((t + CONV_LEAD, LANES), f32), pltpu.VMEM((t + CONV_LEAD, LANES), f32),
                               pltpu.VMEM((t, LANES), f32)],
               compiler_params=_cparams(("arbitrary", "arbitrary")))(dz, u, u, conv_w, conv_b.reshape(1, D_FF))


def _swap_halves(x):
    w = x.shape[1]
    lane = lax.broadcasted_iota(jnp.int32, x.shape, 1) % HEAD_DIM
    return jnp.where(lane < HEAD_DIM // 2, pltpu.roll(x, w - HEAD_DIM // 2, axis=1), pltpu.roll(x, HEAD_DIM // 2, axis=1))


def _rope_pack(patt, c64, s64):
    t = patt.shape[0]
    tp = t + ATT_PAD

    def body(p_ref, c_ref, s_ref, q_ref, k_ref, v_ref):
        c, s = c_ref[...], s_ref[...]

        def rope(x, nh):
            cc = jnp.concatenate([c] * nh, axis=1)
            ss = jnp.concatenate([s] * nh, axis=1)
            return x * cc + _swap_halves(x) * ss

        for ref, val in ((q_ref, rope(p_ref[:, 0:D_ATT], N_Q_HEADS)),
                         (k_ref, rope(p_ref[:, D_ATT:D_ATT + D_KV], N_KV_HEADS)),
                         (v_ref, p_ref[:, D_ATT + D_KV:ATT_COLS])):
            ref[0:ATT_PAD, :] = jnp.zeros((ATT_PAD, val.shape[1]), bf16)
            ref[ATT_PAD:tp, :] = val.astype(bf16)

    return _pc(body, name="rope_pack", in_specs=[_full((t, ATT_COLS)), _full((t, HEAD_DIM)), _full((t, HEAD_DIM))],
               out_specs=(_full((tp, D_ATT)), _full((tp, D_KV)), _full((tp, D_KV))), grid=(1,),
               out_shape=(S((tp, D_ATT), bf16), S((tp, D_KV), bf16), S((tp, D_KV), bf16)),
               compiler_params=_cparams(("arbitrary",)))(patt, c64, s64)


def _rope_bwd(dqp, dkp, dvp, c64, s64):
    tp = dqp.shape[0]
    t = tp - ATT_PAD

    def body(dq_ref, dk_ref, dv_ref, c_ref, s_ref, o_ref):
        c, s = c_ref[...], s_ref[...]

        def unrope(dy, nh):
            cc = jnp.concatenate([c] * nh, axis=1)
            ss = jnp.concatenate([s] * nh, axis=1)
            return dy * cc + _swap_halves(dy * ss)

        o_ref[:, 0:D_ATT] = unrope(dq_ref[ATT_PAD:tp, :], N_Q_HEADS)
        o_ref[:, D_ATT:D_ATT + D_KV] = unrope(dk_ref[ATT_PAD:tp, :], N_KV_HEADS)
        o_ref[:, D_ATT + D_KV:ATT_COLS] = dv_ref[ATT_PAD:tp, :]

    return _pc(body, name="rope_bwd", grid=(1,),
               in_specs=[_full((tp, D_ATT)), _full((tp, D_KV)), _full((tp, D_KV)), _full((t, HEAD_DIM)), _full((t, HEAD_DIM))],
               out_specs=_full((t, ATT_COLS)), out_shape=S((t, ATT_COLS), f32),
               compiler_params=_cparams(("arbitrary",)))(dqp, dkp, dvp, c64, s64)


def _attn_masks(n):
    rows = GQA_GROUP * BLOCK
    ri = lax.broadcasted_iota(jnp.int32, (rows, BLOCK), 0) % BLOCK
    ci = lax.broadcasted_iota(jnp.int32, (rows, BLOCK), 1)
    m_cur = (ci <= ri) & (ci >= jnp.where(n >= 1, 0, ATT_PAD))
    m_prev = ci > ri + jnp.where(n >= 2, 0, BLOCK)
    m_meta = ci >= jnp.where(n >= 1, ATT_PAD, BLOCK)
    return m_cur, m_prev, m_meta


def _attn_probs(qg, kc, kp, km, masks, skv):
    def scores(k, m):
        s = lax.dot_general(qg, k, _DIMS["nt"], preferred_element_type=f32) * ATT_SCALE
        return jnp.where(m, s, NEG_INF)
    s_c, s_p, s_m = scores(kc, masks[0]), scores(kp, masks[1]), scores(km, masks[2])
    mx = jnp.maximum(jnp.maximum(jnp.max(s_c, axis=-1, keepdims=True), jnp.max(s_p, axis=-1, keepdims=True)),
                     jnp.maximum(jnp.max(s_m, axis=-1, keepdims=True), skv))
    e_c, e_p, e_m, e_s = jnp.exp(s_c - mx), jnp.exp(s_p - mx), jnp.exp(s_m - mx), jnp.exp(skv - mx)
    den = (jnp.sum(e_c, axis=-1, keepdims=True) + jnp.sum(e_p, axis=-1, keepdims=True)
           + jnp.sum(e_m, axis=-1, keepdims=True) + e_s)
    inv = 1.0 / den
    return e_c * inv, e_p * inv, e_m * inv, e_s * inv


def _sink_rows(sk_ref, g):
    hrow = lax.broadcasted_iota(jnp.int32, (GQA_GROUP * BLOCK, 1), 0) // BLOCK
    skv = jnp.zeros((GQA_GROUP * BLOCK, 1), f32)
    for hh in range(GQA_GROUP):
        skv = jnp.where(hrow == hh, sk_ref[0, GQA_GROUP * g + hh], skv)
    return skv, hrow


def _stack_heads(ref, g):
    return jnp.concatenate([ref[:, (GQA_GROUP * g + hh) * HEAD_DIM:(GQA_GROUP * g + hh + 1) * HEAD_DIM]
                            for hh in range(GQA_GROUP)], axis=0)


def _attn_specs():
    blk = lambda w: pl.BlockSpec((BLOCK, w), lambda n: (n, 0))
    prev = pl.BlockSpec((BLOCK, D_KV), lambda n: (jnp.maximum(n - 1, 0), 0))
    meta = pl.BlockSpec((BLOCK, D_KV), lambda n: (0, 0))
    return blk, prev, meta


def _attn_fwd(qp, kp, vp, sinks):
    tp = qp.shape[0]
    blk, prev, meta = _attn_specs()

    def body(sk_ref, q_ref, kc_ref, kp_ref, km_ref, vc_ref, vp_ref, vm_ref, o_ref):
        masks = _attn_masks(pl.program_id(0))
        for g in range(N_KV_HEADS):
            sl = slice(g * HEAD_DIM, (g + 1) * HEAD_DIM)
            skv, _ = _sink_rows(sk_ref, g)
            p_c, p_p, p_m, _ = _attn_probs(_stack_heads(q_ref, g), kc_ref[:, sl], kp_ref[:, sl], km_ref[:, sl], masks, skv)
            o = (jnp.dot(p_c.astype(bf16), vc_ref[:, sl], preferred_element_type=f32)
                 + jnp.dot(p_p.astype(bf16), vp_ref[:, sl], preferred_element_type=f32)
                 + jnp.dot(p_m.astype(bf16), vm_ref[:, sl], preferred_element_type=f32))
            for hh in range(GQA_GROUP):
                h = GQA_GROUP * g + hh
                o_ref[:, h * HEAD_DIM:(h + 1) * HEAD_DIM] = o[hh * BLOCK:(hh + 1) * BLOCK].astype(bf16)

    return _pc(body, name="attn_fwd", grid=(tp // BLOCK,),
               in_specs=[pl.BlockSpec(memory_space=pltpu.SMEM), blk(D_ATT), blk(D_KV), prev, meta, blk(D_KV), prev, meta],
               out_specs=blk(D_ATT), out_shape=S((tp, D_ATT), bf16),
               compiler_params=_cparams(("arbitrary",)))(sinks, qp, kp, kp, kp, vp, vp, vp)


def _attn_bwd(qp, kp, vp, sinks, dop):
    tp = qp.shape[0]
    blk, prev, meta = _attn_specs()

    def body(sk_ref, q_ref, kc_ref, kp_ref, km_ref, vc_ref, vp_ref, vm_ref, do_ref, dq_ref, dk_ref, dv_ref, dsk_ref):
        n = pl.program_id(0)

        @pl.when(n == 0)
        def _():
            dk_ref[...] = jnp.zeros_like(dk_ref)
            dv_ref[...] = jnp.zeros_like(dv_ref)
            dsk_ref[...] = jnp.zeros_like(dsk_ref)
        masks = _attn_masks(n)
        cur = pl.ds(pl.multiple_of(n * BLOCK, BLOCK), BLOCK)
        prv = pl.ds(pl.multiple_of(jnp.maximum(n - 1, 0) * BLOCK, BLOCK), BLOCK)
        lane = lax.broadcasted_iota(jnp.int32, (1, LANES), 1)
        dsk = jnp.zeros((1, LANES), f32)
        for g in range(N_KV_HEADS):
            sl = slice(g * HEAD_DIM, (g + 1) * HEAD_DIM)
            skv, hrow = _sink_rows(sk_ref, g)
            qg = _stack_heads(q_ref, g)
            dog = _stack_heads(do_ref, g)
            ks = (kc_ref[:, sl], kp_ref[:, sl], km_ref[:, sl])
            vs = (vc_ref[:, sl], vp_ref[:, sl], vm_ref[:, sl])
            probs = _attn_probs(qg, ks[0], ks[1], ks[2], masks, skv)
            dps = [lax.dot_general(dog, v, _DIMS["nt"], preferred_element_type=f32) for v in vs]
            delta = sum(jnp.sum(p * dp, axis=-1, keepdims=True) for p, dp in zip(probs[:3], dps))
            dss = [(p * (dp - delta) * ATT_SCALE).astype(bf16) for p, dp in zip(probs[:3], dps)]
            dq = sum(jnp.dot(ds, k, preferred_element_type=f32) for ds, k in zip(dss, ks))
            for hh in range(GQA_GROUP):
                h = GQA_GROUP * g + hh
                dq_ref[:, h * HEAD_DIM:(h + 1) * HEAD_DIM] = dq[hh * BLOCK:(hh + 1) * BLOCK]
                dsk = dsk + jnp.where(lane == h, -jnp.sum(jnp.where(hrow == hh, probs[3] * delta, 0.0)), 0.0)
            for rows, p, ds in zip((cur, prv, slice(0, BLOCK)), probs[:3], dss):
                dv_ref[rows, sl] += lax.dot_general(p.astype(bf16), dog, _DIMS["tn"], preferred_element_type=f32)
                dk_ref[rows, sl] += lax.dot_general(ds, qg, _DIMS["tn"], preferred_element_type=f32)
        dsk_ref[...] += dsk

    return _pc(body, name="attn_bwd", grid=(tp // BLOCK,),
               in_specs=[pl.BlockSpec(memory_space=pltpu.SMEM), blk(D_ATT), blk(D_KV), prev, meta, blk(D_KV), prev, meta,
                         blk(D_ATT)],
               out_specs=(blk(D_ATT), _full((tp, D_KV)), _full((tp, D_KV)), _full((1, LANES))),
               out_shape=(S((tp, D_ATT), f32), S((tp, D_KV), f32), S((tp, D_KV), f32), S((1, LANES), f32)),
               compiler_params=_cparams(("arbitrary",)))(sinks, qp, kp, kp, kp, vp, vp, vp, dop)


def _seg(x, bm):
    hi = x.astype(bf16)
    lo = (x - hi.astype(f32)).astype(bf16)
    return jnp.dot(jnp.concatenate([hi, lo], axis=1), bm, preferred_element_type=f32)


@jax.custom_vjp
def _seg_linear(x, bm):
    return _seg(x, bm)


_seg_linear.defvjp(lambda x, bm: (_seg(x, bm), bm), lambda bm, ct: (_seg(ct, bm), jnp.zeros_like(bm)))


def _softplus(y):
    return jnp.maximum(y, 0.0) + jnp.log(1.0 + jnp.exp(-jnp.abs(y)))


def _prep_fn(xr, xk, xwd, xad, xgd, w0, w2, a0, a2, g2, k_k, k_a, bm, seg=_seg):
    xw = w0 + jnp.dot(jnp.tanh(xwd), w2, preferred_element_type=f32)
    decay = jnp.exp(-jnp.exp(-_softplus(-xw) - 0.5))
    alpha = _sigmoid(a0 + jnp.dot(xad, a2, preferred_element_type=f32))
    g = jnp.dot(_sigmoid(xgd), g2, preferred_element_type=f32)
    kk = xk * k_k
    kkn = kk / jnp.maximum(jnp.sqrt(seg(kk * kk, bm)), 1e-12)
    k2 = xk * (1.0 + (alpha - 1.0) * k_a)
    return decay, k2, -kkn, kkn * alpha, g


def _split_cols(x):
    o1, o2, o3 = 3 * D_R, 3 * D_R + LORA_W, 3 * D_R + LORA_W + LORA_A
    return x[:, 0:D_R], x[:, D_R:2 * D_R], x[:, 2 * D_R:o1], x[:, o1:o2], x[:, o2:o3], x[:, o3:RWKV_COLS]


def _shifted(sh_ref, x, halo, first, rb):
    sh_ref[0:SUBLANES, :] = jnp.where(first, 0.0, halo)
    sh_ref[SUBLANES:SUBLANES + rb, :] = x
    return sh_ref[SUBLANES - 1:SUBLANES - 1 + rb, :]


_PREP_PARAMS = ("od_w0", "od_w2", "od_a0", "od_a2", "od_g2", "od_k_k", "od_k_a")


def _rwkv_prep(pr, mu, params, bm):
    t = pr.shape[0]
    rb = _row_block8(t)
    hb = rb // SUBLANES

    def body(pr_ref, halo_ref, mu_ref, w0, w2, a0, a2, g2, kk_ref, ka_ref, bm_ref, *outs_sh):
        outs, sh_ref = outs_sh[:-1], outs_sh[-1]
        x = pr_ref[...]
        prev = _shifted(sh_ref, x, halo_ref[...], pl.program_id(0) == 0, rb)
        xr, xk, xv, xwd, xad, xgd = _split_cols(x + (prev - x) * mu_ref[...])
        bmv = bm_ref[...]
        decay, k2, a_s, b_s, g = _prep_fn(xr, xk, xwd, xad, xgd, w0[...], w2[...], a0[...], a2[...], g2[...],
                                          kk_ref[...], ka_ref[...], bmv)
        vals = (xr, xv, decay, k2, a_s, b_s, decay * xr, _seg(b_s * xr, bmv), _seg(k2 * xr, bmv), g)
        for ref, val in zip(outs, vals):
            ref[...] = val

    row = pl.BlockSpec((rb, RWKV_COLS), lambda i: (i, 0))
    halo = pl.BlockSpec((SUBLANES, RWKV_COLS), lambda i: (jnp.maximum(i * hb - 1, 0), 0))
    orow = pl.BlockSpec((rb, D_R), lambda i: (i, 0))
    return _pc(body, name="rwkv_prep", grid=(t // rb,),
               in_specs=[row, halo, _full((1, RWKV_COLS))] + [_full(p.shape) for p in params] + [_full(bm.shape)],
               out_specs=(orow,) * 10, out_shape=(S((t, D_R), f32),) * 10,
               scratch_shapes=[pltpu.VMEM((rb + SUBLANES, RWKV_COLS), f32)],
               compiler_params=_cparams(("arbitrary",)))(pr, pr, mu, *params, bm)


def _rwkv_prep_bwd(pr, mu, params, bm, cts):
    t = pr.shape[0]
    rb = _row_block8(t)
    hb = rb // SUBLANES
    counts = [len(c) for c in cts]
    flat = [a for c in cts for a in c]

    def body(pr_ref, halo_ref, mu_ref, w0, w2, a0, a2, g2, kk_ref, ka_ref, bm_ref, *rest):
        ct_refs, rest = rest[:len(flat)], rest[len(flat):]
        dx_ref, dmu_ref = rest[0], rest[1]
        dpar_refs, sh_ref = rest[2:9], rest[9]

        @pl.when(pl.program_id(0) == 0)
        def _():
            dmu_ref[...] = jnp.zeros_like(dmu_ref)
            for r in dpar_refs:
                r[...] = jnp.zeros_like(r)
        sums, pos = [], 0
        for c in counts:
            sums.append(sum(r[...] for r in ct_refs[pos:pos + c]))
            pos += c
        x = pr_ref[...]
        prev = _shifted(sh_ref, x, halo_ref[...], pl.program_id(0) == 0, rb)
        xr, xk, xv, xwd, xad, xgd = _split_cols(x + (prev - x) * mu_ref[...])
        bmv = bm_ref[...]
        _, vjp = jax.vjp(lambda *a: _prep_fn(*a, bmv, _seg_linear), xr, xk, xwd, xad, xgd, w0[...], w2[...], a0[...], a2[...],
                         g2[...], kk_ref[...], ka_ref[...])
        grads = vjp(tuple(sums[:5]))
        dxr, dxk, dxwd, dxad, dxgd = grads[:5]
        o1, o2, o3 = 3 * D_R, 3 * D_R + LORA_W, 3 * D_R + LORA_W + LORA_A
        dx_ref[:, 0:D_R] = dxr + sums[5]
        dx_ref[:, D_R:2 * D_R] = dxk
        dx_ref[:, 2 * D_R:o1] = sums[6]
        dx_ref[:, o1:o2] = dxwd
        dx_ref[:, o2:o3] = dxad
        dx_ref[:, o3:RWKV_COLS] = dxgd
        dmu_ref[...] += jnp.sum(dx_ref[...] * (prev - x), axis=0, keepdims=True)
        for r, gval in zip(dpar_refs, grads[5:]):
            r[...] += gval

    row = pl.BlockSpec((rb, RWKV_COLS), lambda i: (i, 0))
    halo = pl.BlockSpec((SUBLANES, RWKV_COLS), lambda i: (jnp.maximum(i * hb - 1, 0), 0))
    crow = pl.BlockSpec((rb, D_R), lambda i: (i, 0))
    return _pc(body, name="rwkv_prep_bwd", grid=(t // rb,),
               in_specs=[row, halo, _full((1, RWKV_COLS))] + [_full(p.shape) for p in params] + [_full(bm.shape)]
               + [crow] * len(flat),
               out_specs=(row, _full((1, RWKV_COLS))) + tuple(_full(p.shape) for p in params),
               out_shape=(S((t, RWKV_COLS), f32), S((1, RWKV_COLS), f32)) + tuple(S(p.shape, f32) for p in params),
               scratch_shapes=[pltpu.VMEM((rb + SUBLANES, RWKV_COLS), f32)],
               compiler_params=_cparams(("arbitrary",)))(pr, pr, mu, *params, bm, *flat)


def _shift_bwd(dxs, mu):
    t = dxs.shape[0]
    rb = _row_block8(t)
    hb = rb // SUBLANES
    nblk = t // rb

    def body(dx_ref, halo_ref, mu_ref, o_ref, sh_ref):
        dx = dx_ref[...]
        sh_ref[0:rb, :] = dx
        sh_ref[rb:rb + SUBLANES, :] = jnp.where(pl.program_id(0) == nblk - 1, 0.0, halo_ref[...])
        m = mu_ref[...]
        o_ref[...] = dx * (1.0 - m) + sh_ref[1:1 + rb, :] * m

    row = pl.BlockSpec((rb, RWKV_COLS), lambda i: (i, 0))
    halo = pl.BlockSpec((SUBLANES, RWKV_COLS), lambda i: (jnp.minimum((i + 1) * hb, t // SUBLANES - 1), 0))
    return _pc(body, name="rwkv_shift_bwd", grid=(nblk,), in_specs=[row, halo, _full((1, RWKV_COLS))],
               out_specs=row, out_shape=S((t, RWKV_COLS), f32),
               scratch_shapes=[pltpu.VMEM((rb + SUBLANES, RWKV_COLS), f32)],
               compiler_params=_cparams(("arbitrary",)))(dxs, dxs, mu)


def _post_fn(y, xr, k2, xv, g, lg, lb, rk, bm, seg=_seg):
    inv_n = 1.0 / HEAD_DIM
    yc = y - seg(y, bm) * inv_n
    var = seg(yc * yc, bm) * inv_n
    yn = yc * lax.rsqrt(var + RWKV_GN_EPS) * lg + lb
    return (yn + seg(xr * k2 * rk, bm) * xv) * g


def _rwkv_post(y, xr, k2, xv, g, lg, lb, rk, bm):
    t = y.shape[0]
    rb = _row_block8(t)

    def body(y_ref, xr_ref, k2_ref, xv_ref, g_ref, lg_ref, lb_ref, rk_ref, bm_ref, o_ref):
        o_ref[...] = _post_fn(y_ref[...], xr_ref[...], k2_ref[...], xv_ref[...], g_ref[...], lg_ref[...], lb_ref[...],
                              rk_ref[...], bm_ref[...])

    row = pl.BlockSpec((rb, D_R), lambda i: (i, 0))
    vec = _full((1, D_R))
    return _pc(body, name="rwkv_post", grid=(t // rb,), in_specs=[row] * 5 + [vec] * 3 + [_full(bm.shape)],
               out_specs=row, out_shape=S((t, D_R), f32),
               compiler_params=_cparams(("arbitrary",)))(y, xr, k2, xv, g, lg, lb, rk, bm)


def _rwkv_post_bwd(dy1, y, xr, k2, xv, g, lg, lb, rk, bm):
    t = y.shape[0]
    rb = _row_block8(t)

    def body(dy_ref, y_ref, xr_ref, k2_ref, xv_ref, g_ref, lg_ref, lb_ref, rk_ref, bm_ref, *outs):
        @pl.when(pl.program_id(0) == 0)
        def _():
            for r in outs[5:]:
                r[...] = jnp.zeros_like(r)
        bmv = bm_ref[...]
        _, vjp = jax.vjp(lambda *a: _post_fn(*a, bmv, _seg_linear), y_ref[...], xr_ref[...], k2_ref[...], xv_ref[...], g_ref[...],
                         lg_ref[...], lb_ref[...], rk_ref[...])
        grads = vjp(dy_ref[...])
        for r, gval in zip(outs[:5], grads[:5]):
            r[...] = gval
        for r, gval in zip(outs[5:], grads[5:]):
            r[...] += gval

    row = pl.BlockSpec((rb, D_R), lambda i: (i, 0))
    vec = _full((1, D_R))
    return _pc(body, name="rwkv_post_bwd", grid=(t // rb,),
               in_specs=[pl.BlockSpec((rb, D_R), lambda i: (i, 1))] + [row] * 5 + [vec] * 3 + [_full(bm.shape)],
               out_specs=(row,) * 5 + (vec,) * 3, out_shape=(S((t, D_R), f32),) * 5 + (S((1, D_R), f32),) * 3,
               compiler_params=_cparams(("arbitrary",)))(dy1, y, xr, k2, xv, g, lg, lb, rk, bm)


def _seg2(x, bb):
    hi = x.astype(bf16)
    lo = (x - hi.astype(f32)).astype(bf16)
    return jnp.dot(jnp.concatenate([hi, lo], axis=1), bb, preferred_element_type=f32)


def _row4(rows, j):
    return jnp.concatenate([jnp.broadcast_to(rows[j:j + 1, p * LANES:(p + 1) * LANES], (HEAD_DIM, LANES))
                            for p in range(4)], axis=0)


def _scan_consts():
    lane_group = jnp.arange(LANES) // HEAD_DIM
    b128 = (lane_group[:, None] == lane_group[None, :]).astype(bf16)
    bb = jnp.concatenate([b128, b128], axis=0)
    qsel = (jnp.arange(PAIR_ROWS)[:, None] % HEAD_DIM == jnp.arange(LANES)[None, :] % HEAD_DIM).astype(f32)
    return bb, qsel


def _store_cols(acc_ref, o_ref, tc):
    for p in range(4):
        blk = acc_ref[p * HEAD_DIM:(p + 1) * HEAD_DIM, :].T
        o_ref[:, (2 * p) * HEAD_DIM:(2 * p + 1) * HEAD_DIM] = blk[0:tc]
        o_ref[:, (2 * p + 1) * HEAD_DIM:(2 * p + 2) * HEAD_DIM] = blk[HEAD_DIM:HEAD_DIM + tc]


PAIR_GROUP = 2 * SUBLANES


def _rwkv_pairs(w, a, b, k, wr, bm):
    t = w.shape[0]
    rb = _row_block8(t)

    def body(w_ref, a_ref, b_ref, k_ref, wr_ref, bm_ref, *outs_sh):
        outs, sh_ref = outs_sh[:-1], outs_sh[-1]

        def second(ref):
            sh_ref[0:rb, :] = ref[...]
            sh_ref[rb:rb + SUBLANES, :] = jnp.zeros((SUBLANES, D_R), f32)
            return sh_ref[1:1 + rb, :]

        w1, b1, k1 = w_ref[...], b_ref[...], k_ref[...]
        w2, a2, wr2 = second(w_ref), second(a_ref), second(wr_ref)
        bmv = bm_ref[...]
        vals = (w1 * a2, w1 * wr2, w1 * w2, b1 * w2, k1 * w2, _seg(b1 * a2, bmv), _seg(k1 * a2, bmv),
                _seg(b1 * wr2, bmv), _seg(k1 * wr2, bmv))
        for ref, val in zip(outs, vals):
            ref[...] = val

    row = pl.BlockSpec((rb, D_R), lambda i: (i, 0))
    return _pc(body, name="rwkv_pairs", grid=(t // rb,), in_specs=[row] * 5 + [_full(bm.shape)],
               out_specs=(row,) * 9, out_shape=(S((t, D_R), f32),) * 9,
               scratch_shapes=[pltpu.VMEM((rb + SUBLANES, D_R), f32)],
               compiler_params=_cparams(("arbitrary",)))(w, a, b, k, wr, bm)


def _wkv_fwd(w, k, v, a, b, wr, br, kr, pairs):
    t = w.shape[0]
    tc = SCAN_CHUNK
    bb, qsel = _scan_consts()

    def body(*refs):
        step_refs, pair_refs = refs[0:8], refs[8:17]
        bb_ref, q_ref, y_ref, st_ref, sa_ref, vb_ref, s_scr, yacc = refs[17:]

        @pl.when(pl.program_id(0) == 0)
        def _():
            s_scr[...] = jnp.zeros_like(s_scr)
        bbv, qv = bb_ref[...], q_ref[...]
        lane64 = lax.broadcasted_iota(jnp.int32, (PAIR_ROWS, LANES), 1) % HEAD_DIM

        def halves(x):
            hi = x.astype(bf16)
            return jnp.concatenate([hi, (x - hi.astype(f32)).astype(bf16)], axis=1)

        def group(gi, s):
            base = pl.multiple_of(gi * PAIR_GROUP, PAIR_GROUP)
            w16, k16, v16, a16, b16, wr16, br16, kr16 = (
                (ref[pl.ds(base, SUBLANES), :], ref[pl.ds(base + SUBLANES, SUBLANES), :]) for ref in step_refs)
            a2p, r2p, w12p, b1wp, k1wp, betap, kappap, bwrp, kwrp = (
                (ref[pl.ds(base, SUBLANES), :], ref[pl.ds(base + SUBLANES, SUBLANES), :]) for ref in pair_refs)
            vh16 = tuple(x.astype(bf16).astype(f32) for x in v16)
            vl16 = tuple(x - h for x, h in zip(v16, vh16))
            step = lambda arr, j: _row4(arr[j // SUBLANES], j % SUBLANES)
            for q in range(SUBLANES):
                j1, j2 = 2 * q, 2 * q + 1
                t1 = base + j1
                lhs = [halves(jnp.concatenate([s * step(a16, j1), s * step(a2p, j1), s * step(wr16, j1), s * step(r2p, j1)],
                                              axis=0))]
                for j in (j1, j2):
                    lhs.append(jnp.concatenate([(qv * step(vh16, j)).astype(bf16), (qv * step(vl16, j)).astype(bf16)], axis=1))
                r = jnp.dot(jnp.concatenate(lhs, axis=0), bbv, preferred_element_type=f32)
                sa1, p2, z1, z2, vb1, vb2 = (r[n * PAIR_ROWS:(n + 1) * PAIR_ROWS] for n in range(6))
                sa2 = p2 + sa1 * step(betap, j1) + vb1 * step(kappap, j1)
                y1 = z1 + sa1 * step(br16, j1) + vb1 * step(kr16, j1)
                y2 = (z2 + sa1 * step(bwrp, j1) + vb1 * step(kwrp, j1)) + (sa2 * step(br16, j2) + vb2 * step(kr16, j2))
                yacc[...] = jnp.where(lane64 == t1, y1, jnp.where(lane64 == t1 + 1, y2, yacc[...]))
                st_ref[base // 2 + q] = s
                sa_ref[t1] = sa1
                sa_ref[t1 + 1] = sa2
                vb_ref[t1] = vb1
                vb_ref[t1 + 1] = vb2
                s = ((s * step(w12p, j1) + sa1 * step(b1wp, j1)) + vb1 * step(k1wp, j1)) + (sa2 * step(b16, j2) + vb2 * step(k16, j2))
            return s

        s_scr[...] = lax.fori_loop(0, tc // PAIR_GROUP, group, s_scr[...])
        _store_cols(yacc, y_ref, tc)

    row = pl.BlockSpec((tc, D_R), lambda c: (c, 0))
    tiles = pl.BlockSpec((tc, PAIR_ROWS, LANES), lambda c: (c, 0, 0))
    return _pc(body, name="wkv_fwd", grid=(t // tc,),
               in_specs=[row] * 17 + [_full(bb.shape), _full(qsel.shape)],
               out_specs=(row, pl.BlockSpec((tc // 2, PAIR_ROWS, LANES), lambda c: (c, 0, 0)), tiles, tiles),
               out_shape=(S((t, D_R), f32), S((t // 2, PAIR_ROWS, LANES), f32)) + (S((t, PAIR_ROWS, LANES), f32),) * 2,
               scratch_shapes=[pltpu.VMEM((PAIR_ROWS, LANES), f32), pltpu.VMEM((PAIR_ROWS, LANES), f32)],
               compiler_params=_cparams(("arbitrary",)))(w, k, v, a, b, wr, br, kr, *pairs, bb, qsel)


def _wkv_bwd(sprev, sab, vbb, w, k, a, b, r, dy):
    t = w.shape[0]
    tc = SCAN_CHUNK
    nc = t // tc
    bb, qsel = _scan_consts()

    def body(st_ref, sa_ref, vb_ref, w_ref, k_ref, a_ref, b_ref, r_ref, dy_ref, bb_ref, q_ref,
             dr_ref, dw_ref, dk_ref, dv_ref, da_ref, db_ref, g_scr, dvacc, rows_scr):
        @pl.when(pl.program_id(0) == 0)
        def _():
            g_scr[...] = jnp.zeros_like(g_scr)
        bbv, qv = bb_ref[...], q_ref[...]
        lane64 = lax.broadcasted_iota(jnp.int32, (PAIR_ROWS, LANES), 1) % HEAD_DIM
        outs = (dr_ref, dw_ref, db_ref, dk_ref, da_ref)

        def colsums(slot, j, x):
            for p in range(4):
                rows_scr[slot, j:j + 1, p * LANES:(p + 1) * LANES] = jnp.sum(x[p * HEAD_DIM:(p + 1) * HEAD_DIM], axis=0,
                                                                           keepdims=True)

        def group(i, g):
            base = pl.multiple_of((tc // SUBLANES - 1 - i) * SUBLANES, SUBLANES)
            w8, k8, a8, b8, r8, dy8 = (ref[pl.ds(base, SUBLANES), :] for ref in (w_ref, k_ref, a_ref, b_ref, r_ref, dy_ref))
            def after_step(j, sp):
                return sp * _row4(w8, j) + sa_ref[base + j] * _row4(b8, j) + vb_ref[base + j] * _row4(k8, j)

            def back_step(j, sp, s_t, g):
                tt = base + j
                u, vb = sa_ref[tt], vb_ref[tt]
                a4, b4, w4, k4 = _row4(a8, j), _row4(b8, j), _row4(w8, j), _row4(k8, j)
                dyb = _seg2(qv * _row4(dy8, j), bbv)
                g = g + dyb * _row4(r8, j)
                rr2 = _seg2(jnp.concatenate([g * b4, g * k4], axis=0), bbv)
                du, dvb = rr2[0:PAIR_ROWS], rr2[PAIR_ROWS:2 * PAIR_ROWS]
                for slot, val in enumerate((s_t * dyb, g * sp, g * u, g * vb, sp * du)):
                    colsums(slot, j, val)
                dvacc[...] = jnp.where(lane64 == tt, dvb, dvacc[...])
                return g * w4 + du * a4

            for q in reversed(range(SUBLANES // 2)):
                s0 = st_ref[base // 2 + q]
                s1 = after_step(2 * q, s0)
                g = back_step(2 * q + 1, s1, after_step(2 * q + 1, s1), g)
                g = back_step(2 * q, s0, s1, g)
            for slot, ref in enumerate(outs):
                ref[pl.ds(base, SUBLANES), :] = rows_scr[slot]
            return g

        g_scr[...] = lax.fori_loop(0, tc // SUBLANES, group, g_scr[...])
        _store_cols(dvacc, dv_ref, tc)

    row = pl.BlockSpec((tc, D_R), lambda c: (nc - 1 - c, 0))
    tiles = pl.BlockSpec((tc, PAIR_ROWS, LANES), lambda c: (nc - 1 - c, 0, 0))
    states = pl.BlockSpec((tc // 2, PAIR_ROWS, LANES), lambda c: (nc - 1 - c, 0, 0))
    return _pc(body, name="wkv_bwd", grid=(nc,),
               in_specs=[states, tiles, tiles] + [row] * 6 + [_full(bb.shape), _full(qsel.shape)],
               out_specs=(row,) * 6, out_shape=(S((t, D_R), f32),) * 6,
               scratch_shapes=[pltpu.VMEM((PAIR_ROWS, LANES), f32), pltpu.VMEM((PAIR_ROWS, LANES), f32),
                               pltpu.VMEM((5, SUBLANES, D_R), f32)],
               compiler_params=_cparams(("arbitrary",)))(sprev, sab, vbb, w, k, a, b, r, dy, bb, qsel)


def _rope_tables(t):
    half = HEAD_DIM // 2
    inv = ROPE_THETA ** (-jnp.arange(half, dtype=f32) / half)
    ang = jnp.arange(t, dtype=f32)[:, None] * inv[None, :]
    cos, sin = jnp.cos(ang), jnp.sin(ang)
    return jnp.concatenate([cos, cos], axis=1), jnp.concatenate([-sin, sin], axis=1)


def _head_matrix():
    grp = jnp.arange(D_R) // HEAD_DIM
    b = (grp[:, None] == grp[None, :]).astype(bf16)
    return jnp.concatenate([b, b], axis=0)


def _ffn_fwd(h, g, get_w, conv_w, conv_b, i):
    hf = _rms_fwd(h, g, f"ffn{i}_norm")
    w_up_t = get_w(f"ff{i}_up", hf)
    u = _mm(hf, w_up_t, "nt", f"ffn{i}_up")
    z = _ffn_mid(u, conv_w, conv_b, f"ffn{i}_mid")
    w_down = get_w(f"ff{i}_down", z)
    return _mm(z, w_down, "nn", f"ffn{i}_down", res=h), (hf, u, z), w_up_t, w_down


def _ffn_bwd(dh, h, saved, g, w_up_t, conv_w, conv_b, w_down, i, put_g):
    hf, u, z = saved
    dz = _mm(dh, w_down, "nt", f"ffn{i}_dz")
    g_down = _mm(z, dh, "tn", f"ffn{i}_gdown", out_dtype=GRAD_WIRE_DTYPE)
    tok = put_g(f"ff{i}_down", g_down)
    du, g_conv, g_convb = _ffn_mid_bwd(dz, u, conv_w, conv_b + tok, f"ffn{i}_mid_bwd")
    g_up_t = _mm(du, hf, "tn", f"ffn{i}_gup", out_dtype=GRAD_WIRE_DTYPE)
    tok = put_g(f"ff{i}_up", g_up_t)
    dhf = _mm(du, w_up_t, "nn", f"ffn{i}_dhf")
    dh_in, g_norm = _rms_bwd(dhf, h, g + tok, dh, f"ffn{i}_norm_bwd")
    return dh_in, dict(conv=g_conv, conv_b=g_convb, norm=g_norm)


def _local_step(x, target, W, get_w, put_g, put_small, tok0):
    t = N_META + x.shape[0]
    c64, s64 = _rope_tables(t)
    bm = _head_matrix()
    h0 = jnp.concatenate([W["meta_tokens"], x], axis=0)

    ev_w_in_t, ev_w_out = get_w("ev_in", None), get_w("ev_out", None)
    hn0 = _rms_fwd(h0, W["norm_mix"][0] + tok0, "mix0_norm")
    p0 = _mm(hn0, ev_w_in_t, "nt", "ev_in")
    uc = _ev_a_conv(p0, W["ev_conv_a"])
    y0 = _ev_b(p0, W["ev_conv_b"], _ev_a_norm(uc, W["ev_ln_a_g"], W["ev_ln_a_b"]))
    h1 = _mm(y0, ev_w_out, "nn", "ev_out", res=h0)
    h2, ffn0, ff0_up_t, ff0_down = _ffn_fwd(h1, W["norm_ffn"][0], get_w, W["ff_conv"][0], W["ff_conv_b"][0], 0)

    hn1 = _rms_fwd(h2, W["norm_mix"][1], "mix1_norm")
    od_w_in_t = get_w("od_in", hn1)
    w_att, w_rwkv = od_w_in_t[:ATT_COLS], od_w_in_t[ATT_COLS:]
    pr = _mm(hn1, w_rwkv, "nt", "od_in_rwkv")
    qp, kp, vp = _rope_pack(_mm(hn1, w_att, "nt", "od_in_att"), c64, s64)
    op = _attn_fwd(qp, kp, vp, W["od_sinks"])
    prep_params = [W[n] for n in _PREP_PARAMS]
    xr, xv, decay, k2, a_s, b_s, wr, br, kr, gate = _rwkv_prep(pr, W["od_mu"], prep_params, bm)
    pairs = _rwkv_pairs(decay, a_s, b_s, k2, wr, bm)
    ysc, sprev, sab, vbb = _wkv_fwd(decay, k2, xv, a_s, b_s, wr, br, kr, pairs)
    rk = W["od_r_k"].reshape(1, D_R)
    yr = _rwkv_post(ysc, xr, k2, xv, gate, W["od_lnx_g"], W["od_lnx_b"], rk, bm)
    y1 = jnp.concatenate([op[ATT_PAD:], yr.astype(bf16)], axis=1)
    od_w_out = get_w("od_out", y1)
    h3 = _mm(y1, od_w_out, "nn", "od_out", res=h2)
    h4, ffn1, ff1_up_t, ff1_down = _ffn_fwd(h3, W["norm_ffn"][1], get_w, W["ff_conv"][1], W["ff_conv_b"][1], 1)

    tgt = jnp.concatenate([jnp.zeros((N_META, D_MODEL), f32), target], axis=0)
    loss, dh4, g_norm_final = _final_loss(h4, W["norm_final"], tgt)

    dh3, gf1 = _ffn_bwd(dh4, h3, ffn1, W["norm_ffn"][1], ff1_up_t, W["ff_conv"][1], W["ff_conv_b"][1], ff1_down, 1, put_g)
    dy1 = _mm(dh3, od_w_out, "nt", "od_dy")
    g_od_w_out = _mm(y1, dh3, "tn", "od_gout", out_dtype=GRAD_WIRE_DTYPE)
    tok = put_g("od_out", g_od_w_out)
    dysc, dxr_p, dk2_p, dxv_p, dgate, g_lnx_g, g_lnx_b, g_rk = _rwkv_post_bwd(
        dy1, ysc, xr, k2, xv, gate, W["od_lnx_g"], W["od_lnx_b"] + tok, rk, bm)
    dr, dw, dk, dv, da, db = _wkv_bwd(sprev, sab, vbb, decay, k2, a_s, b_s, xr, dysc)
    prep_grads = _rwkv_prep_bwd(pr, W["od_mu"], prep_params, bm,
                                [[dw], [dk, dk2_p], [da], [db], [dgate], [dr, dxr_p], [dv, dxv_p]])
    dxs, g_mu = prep_grads[0], prep_grads[1]
    dpr = _shift_bwd(dxs, W["od_mu"])
    dop = jnp.concatenate([jnp.zeros((ATT_PAD, D_ATT), f32), dy1[:, :D_ATT]], axis=0).astype(bf16)
    dqp, dkp, dvp, dsk = _attn_bwd(qp, kp, vp, W["od_sinks"], dop)
    dpatt = _rope_bwd(dqp, dkp, dvp, c64, s64)
    g_od_w_in_t = jnp.concatenate([_mm(dpatt, hn1, "tn", "od_gin_att", out_dtype=GRAD_WIRE_DTYPE),
                                   _mm(dpr, hn1, "tn", "od_gin_rwkv", out_dtype=GRAD_WIRE_DTYPE)], axis=0)
    tok = put_g("od_in", g_od_w_in_t)
    dhn1 = _mm(dpr, w_rwkv, "nn", "od_dhn_rwkv", res=_mm(dpatt, w_att, "nn", "od_dhn_att"))
    dh2, g_norm_mix1 = _rms_bwd(dhn1, h2, W["norm_mix"][1] + tok, dh3, "mix1_norm_bwd")

    dh1, gf0 = _ffn_bwd(dh2, h1, ffn0, W["norm_ffn"][0], ff0_up_t, W["ff_conv"][0], W["ff_conv_b"][0], ff0_down, 0, put_g)
    early = dict(
        norm_ffn=jnp.concatenate([gf0["norm"], gf1["norm"]], axis=0), norm_final=g_norm_final.reshape(D_MODEL),
        od_sinks=dsk[:, :N_Q_HEADS], od_mu=g_mu, od_lnx_g=g_lnx_g, od_lnx_b=g_lnx_b, od_r_k=g_rk.reshape(N_Q_HEADS, HEAD_DIM),
        ff_conv=jnp.stack([gf0["conv"], gf1["conv"]]), ff_conv_b=jnp.concatenate([gf0["conv_b"], gf1["conv_b"]], axis=0),
        **dict(zip(_PREP_PARAMS, prep_grads[2:])))
    dy0 = _mm(dh1, ev_w_out, "nt", "ev_dy")
    g_ev_w_out = _mm(y0, dh1, "tn", "ev_gout", out_dtype=GRAD_WIRE_DTYPE)
    tok = put_g("ev_out", g_ev_w_out) + put_small(early)
    duc, g_ln_g, g_ln_b = _ev_a_norm_bwd(dy0, uc, W["ev_ln_a_g"], W["ev_ln_a_b"] + tok)
    dav, dag, g_conv_a = _ev_a_conv_bwd(duc, p0, W["ev_conv_a"])
    dgb, dgc, dxi, g_conv_b = _ev_b_bwd(dy0, p0, W["ev_conv_b"])
    dp0 = jnp.concatenate([dav, dag, dgb, dgc, dxi], axis=1)
    g_ev_w_in_t = _mm(dp0, hn0, "tn", "ev_gin", out_dtype=GRAD_WIRE_DTYPE)
    tok = put_g("ev_in", g_ev_w_in_t)
    dhn0 = _mm(dp0, ev_w_in_t, "nn", "ev_dhn")
    dh0, g_norm_mix0 = _rms_bwd(dhn0, h0, W["norm_mix"][0] + tok, dh1, "mix0_norm_bwd")

    late = dict(meta_tokens=dh0[:N_META], norm_mix=jnp.concatenate([g_norm_mix0, g_norm_mix1], axis=0),
                ev_conv_a=g_conv_a, ev_ln_a_g=g_ln_g, ev_ln_a_b=g_ln_b, ev_conv_b=g_conv_b)
    return loss, dh0[N_META:], late


HBM = pl.BlockSpec(memory_space=pl.ANY)


def _mesh_pos():
    return lax.axis_index("x"), lax.axis_index("y"), lax.axis_index("c")


def _dev(px, py, pc):
    return 4 * px + 2 * py + pc


def _all_gather(xs, name):
    n = len(xs)

    def body(*refs):
        x_refs, o_refs = refs[:n], refs[n:2 * n]
        send_sems, recv_sems, local_sems = refs[2 * n:]
        x, y, c = _mesh_pos()
        me, sibling = (x, y, c), (x, y, 1 - c)
        chips = [(1 - x, y), (x, 1 - y), (1 - x, 1 - y)]

        def copy(i, k, block, to, from_input=False):
            dst = o_refs[i].at[_dev(*block)]
            return pltpu.make_async_remote_copy(src_ref=x_refs[i] if from_input else dst, dst_ref=dst,
                                                send_sem=send_sems.at[i, k], recv_sem=recv_sems.at[i, k],
                                                device_id=to, device_id_type=MESH)

        mine = [pltpu.make_async_copy(x_refs[i], o_refs[i].at[_dev(*me)], local_sems.at[i]) for i in range(n)]
        for cp in mine:
            cp.start()
        first = []
        for i in range(n):
            first.append(copy(i, 0, me, sibling, True))
            first += [copy(i, 1 + j, me, (*chip, c), True) for j, chip in enumerate(chips)]
        for cp in first:
            cp.start()
        passed = []
        for j, chip in enumerate(chips):
            for i in range(n):
                copy(i, 1 + j, (*chip, c), me).wait_recv()
                fwd = copy(i, 4 + j, (*chip, c), sibling)
                fwd.start()
                passed.append(fwd)
        for i in range(n):
            copy(i, 0, sibling, me).wait_recv()
            for j, chip in enumerate(chips):
                copy(i, 4 + j, (*chip, 1 - c), me).wait_recv()
        for cp in first + passed:
            cp.wait_send()
        for cp in mine:
            cp.wait()

    return _pc(body, name=name, in_specs=[HBM] * n, out_specs=tuple([HBM] * n),
               out_shape=tuple(S((N_DEV,) + x.shape, x.dtype) for x in xs),
               scratch_shapes=[pltpu.SemaphoreType.DMA((n, 7)), pltpu.SemaphoreType.DMA((n, 7)),
                               pltpu.SemaphoreType.DMA((n,))])(*xs)


HBM_SPEC = pl.BlockSpec(memory_space=pltpu.HBM)
SEM_SPEC = pl.BlockSpec(memory_space=pltpu.SEMAPHORE)
DATAFLOW = pltpu.SideEffectType.DATAFLOW_SIDE_EFFECTING
_PEER_FLIPS = ((1, 0, 0), (0, 1, 0), (1, 1, 0), (1, 0, 1), (0, 1, 1), (1, 1, 1), (0, 0, 1))
N_PEERS = len(_PEER_FLIPS)


def _peers(x, y, c):
    return [((1 - x) if fx else x, (1 - y) if fy else y, (1 - c) if fc else c) for fx, fy, fc in _PEER_FLIPS]


def _xchg_start(srcs, lands, scatter, name):
    n = len(srcs)

    def body(*refs):
        src_refs, land_refs = refs[:n], refs[n:2 * n]
        send_sems, recv_sems, token = refs[2 * n], refs[2 * n + 1], refs[-1]
        x, y, c = _mesh_pos()
        me = _dev(x, y, c)
        for i in range(n):
            for k, peer in enumerate(_peers(x, y, c)):
                pltpu.make_async_remote_copy(src_ref=src_refs[i].at[_dev(*peer)] if scatter else src_refs[i],
                                             dst_ref=land_refs[i].at[me], send_sem=send_sems.at[i * N_PEERS + k],
                                             recv_sem=recv_sems.at[i * N_PEERS + k], device_id=peer, device_id_type=MESH).start()
        token[...] = jnp.zeros_like(token)

    arrs = list(srcs) + list(lands)
    outs = _pc(body, name=name,
               out_shape=(pltpu.SemaphoreType.DMA((n * N_PEERS,)), pltpu.SemaphoreType.DMA((n * N_PEERS,)),
                          *[pltpu.HBM(a.shape, a.dtype) for a in arrs], S((SUBLANES, LANES), f32)),
               in_specs=[HBM_SPEC] * (2 * n),
               out_specs=(SEM_SPEC, SEM_SPEC, *[HBM_SPEC] * (2 * n), pl.BlockSpec(memory_space=pltpu.VMEM)),
               input_output_aliases={i: 2 + i for i in range(2 * n)},
               compiler_params=pltpu.CompilerParams(has_side_effects=DATAFLOW))(
        *[pltpu.with_memory_space_constraint(a, pltpu.HBM) for a in arrs])
    return (outs[0], outs[1], list(outs[2:2 + n]), list(outs[2 + n:2 + 2 * n]), scatter), outs[-1]


def _xchg_wait(handle, after, name):
    send_sems, recv_sems, srcs, lands, scatter = handle
    n = len(srcs)

    def body(*refs):
        src_refs, land_refs = refs[:n], refs[n:2 * n]
        send, recv = refs[2 * n], refs[2 * n + 1]
        x, y, c = _mesh_pos()
        for i in range(n):
            for k in range(N_PEERS):
                cp = pltpu.make_async_remote_copy(src_ref=src_refs[i].at[0] if scatter else src_refs[i],
                                                  dst_ref=land_refs[i].at[0], send_sem=send.at[i * N_PEERS + k],
                                                  recv_sem=recv.at[i * N_PEERS + k],
                                                  device_id=(x, y, c), device_id_type=MESH)
                cp.wait_send()
                cp.wait_recv()

    arrs = srcs + lands
    outs = _pc(body, name=name, out_shape=tuple(pltpu.HBM(a.shape, a.dtype) for a in arrs),
               in_specs=[HBM_SPEC] * (2 * n) + [SEM_SPEC, SEM_SPEC, pl.BlockSpec(memory_space=pl.ANY)],
               out_specs=tuple([HBM_SPEC] * (2 * n)), input_output_aliases={i: i for i in range(2 * n)},
               compiler_params=pltpu.CompilerParams(has_side_effects=DATAFLOW))(*arrs, send_sems, recv_sems, after)
    return list(outs[:n]), list(outs[n:])


def _rs_sum(g, land, me_vec, name):
    _, r, cols = g.shape
    tr = _divisor_block(r, 16, min(r, 352))

    def body(me_ref, g_ref, *rest):
        o_ref = rest[-1]
        acc = g_ref[0].astype(f32)
        for l_ref in rest[:-1]:
            acc = acc + l_ref[0].astype(f32)
        o_ref[...] = acc

    blk = lambda f: pl.BlockSpec((1, tr, cols), f)
    grid_spec = pltpu.PrefetchScalarGridSpec(
        num_scalar_prefetch=1, grid=(r // tr,),
        in_specs=[blk(lambda i, me_ref: (me_ref[0], i, 0))]
        + [blk(lambda i, me_ref, k=k: ((me_ref[0] + k) % N_DEV, i, 0)) for k in range(1, N_DEV)],
        out_specs=pl.BlockSpec((tr, cols), lambda i, me_ref: (i, 0)))
    return _pc(body, name=name, grid_spec=grid_spec, out_shape=S((r, cols), f32),
               compiler_params=_cparams(("arbitrary",)))(me_vec, g, *([land] * (N_DEV - 1)))


def _sum_devices(a, name):
    def body(a_ref, o_ref):
        acc = a_ref[0]
        for d in range(1, N_DEV):
            acc = acc + a_ref[d]
        o_ref[...] = acc

    return _pc(body, name=name, grid=(1,), in_specs=[_full(a.shape)], out_specs=_full(a.shape[1:]),
               out_shape=S(a.shape[1:], a.dtype), compiler_params=_cparams(("arbitrary",)))(a)


def _adamw(w, m, v, g, name):
    shape = w.shape
    w2, m2, v2, g2 = (a.reshape(-1, shape[-1]) for a in (w, m, v, g))
    rows, cols = w2.shape
    tr = rows if rows % SUBLANES else _divisor_block(rows, SUBLANES, max(SUBLANES, min(rows, ADAMW_BLOCK_ELEMS // cols)))
    c1, c2 = 1.0 - ADAM_B1 ** ADAM_STEP, 1.0 - ADAM_B2 ** ADAM_STEP

    def body(w_ref, m_ref, v_ref, g_ref, d_ref, nm_ref, nv_ref):
        gv = g_ref[...]
        nm = ADAM_B1 * m_ref[...] + (1.0 - ADAM_B1) * gv
        nv = ADAM_B2 * v_ref[...] + (1.0 - ADAM_B2) * (gv * gv)
        d_ref[...] = -ADAM_LR * ((nm / c1) / (jnp.sqrt(nv / c2) + ADAM_EPS) + ADAM_WD * w_ref[...])
        nm_ref[...] = nm
        nv_ref[...] = nv

    blk = pl.BlockSpec((tr, cols), lambda i: (i, 0))
    outs = _pc(body, name=name, grid=(rows // tr,), in_specs=[blk] * 4, out_specs=(blk,) * 3,
               out_shape=(S((rows, cols), f32),) * 3, compiler_params=_cparams(("arbitrary",)))(w2, m2, v2, g2)
    return tuple(o.reshape(shape) for o in outs)


_WEIGHTS = ("meta_tokens", "norm_mix", "norm_ffn", "norm_final", "ev_w_in", "ev_conv_a", "ev_ln_a_g", "ev_ln_a_b",
            "ev_conv_b", "ev_w_out", "od_w_in", "od_sinks", "od_mu", "od_w0", "od_w2", "od_a0", "od_a2", "od_g2",
            "od_k_k", "od_k_a", "od_r_k", "od_lnx_g", "od_lnx_b", "od_w_out", "ff_w_up", "ff_conv", "ff_conv_b", "ff_w_down")
_SMALL_SHARDED = (("meta_tokens", 1), ("ev_conv_a", 2), ("ev_conv_b", 2), ("od_mu", 1), ("od_w0", 1), ("od_w2", 2),
                  ("od_a0", 1), ("od_a2", 2), ("od_g2", 2), ("od_k_k", 1), ("od_k_a", 1), ("od_lnx_g", 1),
                  ("od_lnx_b", 1), ("ff_conv", 2))
_SMALL_REPLICATED = ("norm_mix", "norm_ffn", "norm_final", "ev_ln_a_g", "ev_ln_a_b", "od_sinks", "od_r_k", "ff_conv_b")
SLAB_UNIT = SUBLANES * LANES


def _pack(arrs):
    flat = jnp.concatenate([a.reshape(-1).astype(f32) for a in arrs])
    pad = (-flat.shape[0]) % SLAB_UNIT
    return jnp.pad(flat, (0, pad)).reshape(-1, LANES)


def _unpack(flat, shapes):
    out, off = [], 0
    for shp in shapes:
        size = 1
        for s in shp:
            size *= s
        out.append(flat[..., off:off + size].reshape(flat.shape[:-1] + tuple(shp)))
        off += size
    return out


def _full_shape(shape, axis):
    return tuple(N_DEV * s if i == axis else s for i, s in enumerate(shape))


def kernel(x, meta_tokens, norm_mix, norm_ffn, norm_final, ev_w_in, ev_conv_a, ev_ln_a_g, ev_ln_a_b, ev_conv_b, ev_w_out, od_w_in, od_sinks, od_mu, od_w0, od_w2, od_a0, od_a2, od_g2, od_k_k, od_k_a, od_r_k, od_lnx_g, od_lnx_b, od_w_out, ff_w_up, ff_conv, ff_conv_b, ff_w_down, loss_target, m_meta_tokens, m_norm_mix, m_norm_ffn, m_norm_final, m_ev_w_in, m_ev_conv_a, m_ev_ln_a_g, m_ev_ln_a_b, m_ev_conv_b, m_ev_w_out, m_od_w_in, m_od_sinks, m_od_mu, m_od_w0, m_od_w2, m_od_a0, m_od_a2, m_od_g2, m_od_k_k, m_od_k_a, m_od_r_k, m_od_lnx_g, m_od_lnx_b, m_od_w_out, m_ff_w_up, m_ff_conv, m_ff_conv_b, m_ff_w_down, v_meta_tokens, v_norm_mix, v_norm_ffn, v_norm_final, v_ev_w_in, v_ev_conv_a, v_ev_ln_a_g, v_ev_ln_a_b, v_ev_conv_b, v_ev_w_out, v_od_w_in, v_od_sinks, v_od_mu, v_od_w0, v_od_w2, v_od_a0, v_od_a2, v_od_g2, v_od_k_k, v_od_k_a, v_od_r_k, v_od_lnx_g, v_od_lnx_b, v_od_w_out, v_ff_w_up, v_ff_conv, v_ff_conv_b, v_ff_w_down):
    A = dict(locals())
    px, py, pc = _mesh_pos()
    me = _dev(px, py, pc)
    me_vec = jnp.reshape(me, (1,)).astype(jnp.int32)
    rows = lambda a: a.reshape(N_DEV * a.shape[1], a.shape[2])
    blocks = lambda a: a.reshape(N_DEV, a.shape[0] // N_DEV, a.shape[1])

    shards = dict(ev_in=ev_w_in[0].T, ev_out=ev_w_out[0], ff0_up=ff_w_up[0].T, ff0_down=ff_w_down[0], od_in=od_w_in[0].T,
                  od_out=od_w_out[0], ff1_up=ff_w_up[1].T, ff1_down=ff_w_down[1])
    shards = {n: b.astype(bf16) for n, b in shards.items()}
    small_shapes = [A[n].shape for n, _ in _SMALL_SHARDED]
    gathered = _all_gather([shards["ev_in"], shards["ev_out"], _pack([A[n] for n, _ in _SMALL_SHARDED])], "gather_first")
    gathered, shards = lax.optimization_barrier((gathered, shards))
    fetch, tok0 = {}, jnp.zeros((), f32)
    for n in ("ff0_up", "ff0_down", "od_in", "od_out", "ff1_up", "ff1_down"):
        shard, tok0 = lax.optimization_barrier((shards[n], tok0))
        land = lax.dynamic_update_slice(lax.empty((N_DEV,) + shard.shape, bf16), shard[None], (me, 0, 0))
        fetch[n], token = _xchg_start([shard], [land], False, f"gather_{n}_start")
        tok0 = tok0 + token[0, 0]

    def get_w(n, after):
        if n in ("ev_in", "ev_out"):
            return rows(gathered[("ev_in", "ev_out").index(n)])
        return rows(_xchg_wait(fetch[n], after, f"gather_{n}_wait")[1][0])

    W = {}
    for (n, ax), seg in zip(_SMALL_SHARDED, _unpack(gathered[-1].reshape(N_DEV, -1), small_shapes)):
        W[n] = jnp.moveaxis(seg, 0, ax).reshape(_full_shape(A[n].shape, ax))
    for n in ("ev_conv_a", "ev_conv_b", "od_w2", "od_a2", "od_g2"):
        W[n] = W[n][0]
    for n in _SMALL_REPLICATED:
        W[n] = A[n]
    W["od_r_k"] = od_r_k[0]

    small_shape = {n: _full_shape(A[n].shape, ax) for n, ax in _SMALL_SHARDED}
    small_shape.update({n: A[n].shape for n in _SMALL_REPLICATED})
    sent, small_sent, small_names = {}, {}, {}

    def put_g(n, g):
        g8 = blocks(g)
        sent[n], token = _xchg_start([g8], [lax.empty(g8.shape, g8.dtype)], True, f"reduce_{n}_start")
        return token[0, 0]

    def put_small(gs, stage="early"):
        small_names[stage] = sorted(gs)
        slab = _pack([gs[n] for n in small_names[stage]])
        land = lax.dynamic_update_slice(lax.empty((N_DEV,) + slab.shape, f32), slab[None], (me, 0, 0))
        small_sent[stage], small_tok[stage] = _xchg_start([slab], [land], False, f"gather_{stage}_small_grads_start")
        return small_tok[stage][0, 0]

    small_tok = {}
    loss_tile, grad_x, late = _local_step(x[0], loss_target[0], W, get_w, put_g, put_small, tok0)
    put_small(late, "late")
    late_tok = small_tok["late"]

    gsh, prev = {}, late_tok
    for n in ("ff1_down", "ff1_up", "od_out", "od_in", "ff0_down", "ff0_up", "ev_out", "ev_in"):
        srcs, lands = _xchg_wait(sent[n], prev, f"reduce_{n}_wait")
        gsh[n] = prev = _rs_sum(srcs[0], lands[0], me_vec, f"reduce_{n}_sum")
    grads = dict(ev_w_in=gsh["ev_in"].T[None], ev_w_out=gsh["ev_out"][None], od_w_in=gsh["od_in"].T[None],
                 od_w_out=gsh["od_out"][None], ff_w_up=jnp.stack([gsh["ff0_up"].T, gsh["ff1_up"].T]),
                 ff_w_down=jnp.stack([gsh["ff0_down"], gsh["ff1_down"]]))

    delta, new_m, new_v = {}, {}, {}
    for n in ("ff_w_up", "ff_w_down", "od_w_in", "od_w_out", "ev_w_in", "ev_w_out"):
        delta[n], new_m[n], new_v[n] = _adamw(A[n], A["m_" + n], A["v_" + n], grads[n], "adamw_" + n)
    for stage in ("early", "late"):
        gsm = _xchg_wait(small_sent[stage], delta["ev_w_in"], f"gather_{stage}_small_grads_wait")[1][0]
        summed = _sum_devices(gsm, f"sum_{stage}_small_grads").reshape(-1)
        for n, full in zip(small_names[stage], _unpack(summed, [small_shape[n] for n in small_names[stage]])):
            grads[n] = full
    for n, ax in _SMALL_SHARDED:
        size = A[n].shape[ax]
        grads[n] = lax.dynamic_slice_in_dim(grads[n], me * size, size, axis=ax)
    for n in small_shape:
        delta[n], new_m[n], new_v[n] = _adamw(A[n], A["m_" + n], A["v_" + n], grads[n], "adamw_" + n)

    loss = lax.psum(loss_tile[0, 0], ("x", "y", "c"))
    return (loss, grad_x[None], *[grads[n] for n in _WEIGHTS], *[delta[n] for n in _WEIGHTS],
            *[new_m[n] for n in _WEIGHTS], *[new_v[n] for n in _WEIGHTS])
```

```python
import jax
import jax.numpy as jnp
from jax import lax
from jax.experimental import pallas as pl
from jax.experimental.pallas import tpu as pltpu

f32, bf16 = jnp.float32, jnp.bfloat16

D_MODEL = 1024
N_META = 16
RMS_EPS = 1e-6
LN_EPS = 1e-5
D_A = 512
CONV_A_WIDTH = 31
CONV_B_WIDTH = 3
HEAD_DIM = 64
N_Q_HEADS = 8
N_KV_HEADS = 2
GQA_GROUP = 4
D_ATT = 512
D_KV = 128
BLOCK = 128
ROPE_THETA = 10000.0
D_R = 512
LORA_W, LORA_A, LORA_G = 64, 64, 128
RWKV_GN_EPS = 64e-5
ATT_COLS = D_ATT + 2 * D_KV
RWKV_COLS = 3 * D_R + LORA_W + LORA_A + LORA_G
D_FF = 2816
FF_CONV_WIDTH = 3
NEG_INF = -1e30
ATT_PAD = BLOCK - N_META
ATT_SCALE = HEAD_DIM ** -0.5

ADAM_LR, ADAM_B1, ADAM_B2, ADAM_EPS, ADAM_WD, ADAM_STEP = 0.001, 0.9, 0.999, 1e-08, 0.01, 10

N_DEV = 8
LANES = 128
SUBLANES = 8
SCAN_CHUNK = 48
PAIR_ROWS = 4 * HEAD_DIM
V7X_VMEM_LIMIT = 56 * 1024 * 1024
ADAMW_BLOCK_ELEMS = 400 * 1024
GRAD_WIRE_DTYPE = bf16
MESH = pl.DeviceIdType.MESH
S = jax.ShapeDtypeStruct
HIGHEST = lax.Precision.HIGHEST


def _pc(body, **kw):
    return pl.pallas_call(body, **kw)


def _cparams(sem=None):
    return pltpu.CompilerParams(dimension_semantics=sem, vmem_limit_bytes=V7X_VMEM_LIMIT)


def _divisor_block(t, unit, limit):
    best = unit
    for rb in range(unit, limit + 1, unit):
        if t % rb == 0:
            best = rb
    assert t % best == 0, (t, unit)
    return best


def _row_block(t):
    return _divisor_block(t, 16, 704)


def _row_block8(t):
    return _divisor_block(t, 8, 344)


def _col_tile(n, cap):
    return _divisor_block(n, LANES, min(n, cap)) if n % LANES == 0 else n


def _full(shape):
    nd = len(shape)
    return pl.BlockSpec(shape, lambda *_: (0,) * nd)


def _sigmoid(x):
    return jax.nn.sigmoid(x)


_DIMS = {"nn": (((1,), (0,)), ((), ())), "nt": (((1,), (1,)), ((), ())), "tn": (((0,), (0,)), ((), ()))}
MM_MAX_K = 2816
MM_MAX_TM = 704
MM_MAX_TN = 1408


def _mm(a, b, mode, name, out_dtype=f32, res=None):
    if mode == "nn":
        (m, k), (k2, n) = a.shape, b.shape
    elif mode == "nt":
        (m, k), (n, k2) = a.shape, b.shape
    else:
        (k, m), (k2, n) = a.shape, b.shape
    assert k == k2, (a.shape, b.shape, mode)
    tm = _row_block(m) if m % LANES else _col_tile(m, MM_MAX_TM)
    tn = _col_tile(n, MM_MAX_TN)
    nk = 1 if (mode == "tn" or k <= MM_MAX_K) else k // MM_MAX_K
    tk = k // nk
    assert tk * nk == k
    dims = _DIMS[mode]

    def body(a_ref, b_ref, *rest):
        part = lax.dot_general(a_ref[...].astype(bf16), b_ref[...].astype(bf16), dims, preferred_element_type=f32)
        if nk == 1:
            o_ref = rest[-1]
            if res is not None:
                part = part + rest[0][...]
            o_ref[...] = part.astype(out_dtype)
            return
        o_ref, acc_ref = rest[-2], rest[-1]
        kk = pl.program_id(2)

        @pl.when(kk == 0)
        def _():
            acc_ref[...] = part

        @pl.when(kk > 0)
        def _():
            acc_ref[...] += part

        @pl.when(kk == nk - 1)
        def _():
            acc = acc_ref[...]
            if res is not None:
                acc = acc + rest[0][...]
            o_ref[...] = acc.astype(out_dtype)

    if mode == "tn":
        a_spec = pl.BlockSpec((k, tm), lambda i, j, kk: (0, i))
    else:
        a_spec = pl.BlockSpec((tm, tk), lambda i, j, kk: (i, kk))
    if mode == "nt":
        b_spec = pl.BlockSpec((tn, tk), lambda i, j, kk: (j, kk))
    else:
        b_spec = pl.BlockSpec((tk, tn), lambda i, j, kk: (kk, j))
    o_spec = pl.BlockSpec((tm, tn), lambda i, j, kk: (i, j))
    ins, specs = [a, b], [a_spec, b_spec]
    if res is not None:
        ins.append(res)
        specs.append(o_spec)
    scratch = [pltpu.VMEM((tm, tn), f32)] if nk > 1 else []
    return _pc(body, name=name, grid=(m // tm, n // tn, nk), in_specs=specs, out_specs=o_spec,
               out_shape=S((m, n), out_dtype), scratch_shapes=scratch,
               compiler_params=_cparams(("arbitrary", "arbitrary", "arbitrary")))(*ins)


def _rms_fwd(x, g, name):
    t, d = x.shape
    rb = _row_block(t)

    def body(x_ref, g_ref, o_ref):
        xv = x_ref[...]
        rstd = lax.rsqrt(jnp.mean(xv * xv, axis=-1, keepdims=True) + RMS_EPS)
        o_ref[...] = (xv * rstd * g_ref[...]).astype(bf16)

    row = pl.BlockSpec((rb, d), lambda i: (i, 0))
    return _pc(body, name=name, grid=(t // rb,), in_specs=[row, _full((1, d))], out_specs=row,
               out_shape=S((t, d), bf16), compiler_params=_cparams(("arbitrary",)))(x, g.reshape(1, d))


def _rms_bwd(dy, x, g, dres, name):
    t, d = x.shape
    rb = _row_block8(t)

    def body(dy_ref, x_ref, g_ref, dres_ref, dx_ref, dg_ref):
        @pl.when(pl.program_id(0) == 0)
        def _():
            dg_ref[...] = jnp.zeros_like(dg_ref)
        xv, dyv = x_ref[...], dy_ref[...]
        rstd = lax.rsqrt(jnp.mean(xv * xv, axis=-1, keepdims=True) + RMS_EPS)
        xn = xv * rstd
        dg_ref[...] += jnp.sum(dyv * xn, axis=0, keepdims=True)
        dxh = dyv * g_ref[...]
        dx_ref[...] = dres_ref[...] + rstd * (dxh - xn * jnp.mean(dxh * xn, axis=-1, keepdims=True))

    row = pl.BlockSpec((rb, d), lambda i: (i, 0))
    return _pc(body, name=name, grid=(t // rb,), in_specs=[row, row, _full((1, d)), row],
               out_specs=(row, _full((1, d))), out_shape=(S((t, d), f32), S((1, d), f32)),
               compiler_params=_cparams(("arbitrary",)))(dy, x, g.reshape(1, d), dres)


def _final_loss(h, g, target_padded):
    t, d = h.shape
    rb = _row_block8(t)

    def body(x_ref, g_ref, t_ref, loss_ref, dx_ref, dg_ref):
        i = pl.program_id(0)

        @pl.when(i == 0)
        def _():
            dg_ref[...] = jnp.zeros_like(dg_ref)
            loss_ref[...] = jnp.zeros_like(loss_ref)
        xv = x_ref[...]
        rstd = lax.rsqrt(jnp.mean(xv * xv, axis=-1, keepdims=True) + RMS_EPS)
        xn = xv * rstd
        gv = g_ref[...]
        row = i * rb + lax.broadcasted_iota(jnp.int32, (rb, 1), 0)
        diff = jnp.where(row >= N_META, xn * gv - t_ref[...], 0.0)
        loss_ref[...] += 0.5 * jnp.sum(jnp.mean(diff * diff, axis=-1, keepdims=True))
        dout = diff * (1.0 / d)
        dg_ref[...] += jnp.sum(dout * xn, axis=0, keepdims=True)
        dxh = dout * gv
        dx_ref[...] = rstd * (dxh - xn * jnp.mean(dxh * xn, axis=-1, keepdims=True))

    row = pl.BlockSpec((rb, d), lambda i: (i, 0))
    return _pc(body, name="final_loss", grid=(t // rb,), in_specs=[row, _full((1, d)), row],
               out_specs=(_full((SUBLANES, LANES)), row, _full((1, d))),
               out_shape=(S((SUBLANES, LANES), f32), S((t, d), f32), S((1, d), f32)),
               compiler_params=_cparams(("arbitrary",)))(h, g.reshape(1, d), target_padded)


CONV_LEAD = 32


def _fill_front_padded(pad_ref, x, t):
    pad_ref[0:CONV_LEAD, :] = jnp.zeros((CONV_LEAD, x.shape[1]), f32)
    pad_ref[CONV_LEAD:CONV_LEAD + t, :] = x


def _fill_back_padded(pad_ref, x, t):
    pad_ref[0:t, :] = x
    pad_ref[t:t + CONV_LEAD, :] = jnp.zeros((CONV_LEAD, x.shape[1]), f32)


def _conv_rows(pad_ref, w_ref, kw, r0, nr):
    acc = None
    for j in range(kw):
        lo = CONV_LEAD + r0 - (kw - 1) + j
        term = w_ref[j:j + 1, :] * pad_ref[lo:lo + nr, :]
        acc = term if acc is None else acc + term
    return acc


def _conv_t_rows(padb_ref, w_ref, kw, r0, nr):
    acc = None
    for j in range(kw):
        lo = r0 + (kw - 1) - j
        term = w_ref[j:j + 1, :] * padb_ref[lo:lo + nr, :]
        acc = term if acc is None else acc + term
    return acc


def _conv_dw_rows(dy_blk, pad_ref, kw, r0, nr):
    out = []
    for j in range(kw):
        lo = CONV_LEAD + r0 - (kw - 1) + j
        out.append(jnp.sum(dy_blk * pad_ref[lo:lo + nr, :], axis=0, keepdims=True))
    return out


def _acc_list(a, b):
    return b if a is None else [x + y for x, y in zip(a, b)]


def _ev_a_conv(p, conv_a):
    t = p.shape[0]
    cr = _row_block8(t)
    nb = D_A // LANES

    def body(av_ref, ag_ref, w_ref, o_ref, pad_ref):
        _fill_front_padded(pad_ref, av_ref[...] * _sigmoid(ag_ref[...]), t)
        for r in range(t // cr):
            o_ref[r * cr:(r + 1) * cr, :] = _conv_rows(pad_ref, w_ref, CONV_A_WIDTH, r * cr, cr)

    col = lambda off: pl.BlockSpec((t, LANES), lambda j: (0, j + off))
    return _pc(body, name="ev_a_conv", grid=(nb,),
               in_specs=[col(0), col(nb), pl.BlockSpec((CONV_A_WIDTH, LANES), lambda j: (0, j))],
               out_specs=col(0), out_shape=S((t, D_A), f32),
               scratch_shapes=[pltpu.VMEM((t + CONV_LEAD, LANES), f32)],
               compiler_params=_cparams(("arbitrary",)))(p, p, conv_a)


def _ln_silu(uc, g, b):
    mu = jnp.mean(uc, axis=-1, keepdims=True)
    xc = uc - mu
    var = jnp.mean(xc * xc, axis=-1, keepdims=True)
    y = xc * lax.rsqrt(var + LN_EPS) * g + b
    return y * _sigmoid(y)


def _ev_a_norm(uc, g, b):
    t, d = uc.shape
    rb = _row_block(t)

    def body(u_ref, g_ref, b_ref, o_ref):
        o_ref[...] = _ln_silu(u_ref[...], g_ref[...], b_ref[...]).astype(bf16)

    row = pl.BlockSpec((rb, d), lambda i: (i, 0))
    return _pc(body, name="ev_a_norm", grid=(t // rb,), in_specs=[row, _full((1, d)), _full((1, d))],
               out_specs=row, out_shape=S((t, 2 * d), bf16), compiler_params=_cparams(("arbitrary",)))(uc, g, b)


def _ev_a_norm_bwd(dy, uc, g, b):
    t, d = uc.shape
    rb = _row_block8(t)

    def body(dy_ref, u_ref, g_ref, b_ref, du_ref, dg_ref, db_ref):
        @pl.when(pl.program_id(0) == 0)
        def _():
            dg_ref[...] = jnp.zeros_like(dg_ref)
            db_ref[...] = jnp.zeros_like(db_ref)
        _, vjp = jax.vjp(_ln_silu, u_ref[...], g_ref[...], b_ref[...])
        du, dg, db = vjp(dy_ref[...])
        du_ref[...] = du
        dg_ref[...] += dg
        db_ref[...] += db

    row = pl.BlockSpec((rb, d), lambda i: (i, 0))
    return _pc(body, name="ev_a_norm_bwd", grid=(t // rb,), in_specs=[row, row, _full((1, d)), _full((1, d))],
               out_specs=(row, _full((1, d)), _full((1, d))),
               out_shape=(S((t, d), f32), S((1, d), f32), S((1, d), f32)),
               compiler_params=_cparams(("arbitrary",)))(dy, uc, g, b)


def _ev_a_conv_bwd(duc, p, conv_a):
    t = p.shape[0]
    cr = _row_block8(t)
    nb = D_A // LANES

    def body(dy_ref, av_ref, ag_ref, w_ref, dav_ref, dag_ref, dw_ref, pad_ref, padb_ref):
        _fill_front_padded(pad_ref, av_ref[...] * _sigmoid(ag_ref[...]), t)
        _fill_back_padded(padb_ref, dy_ref[...], t)
        dw = None
        for r in range(t // cr):
            rows = slice(r * cr, (r + 1) * cr)
            du = _conv_t_rows(padb_ref, w_ref, CONV_A_WIDTH, r * cr, cr)
            avr = av_ref[rows, :]
            sgr = _sigmoid(ag_ref[rows, :])
            dav_ref[rows, :] = du * sgr
            dag_ref[rows, :] = du * avr * sgr * (1.0 - sgr)
            dw = _acc_list(dw, _conv_dw_rows(dy_ref[rows, :], pad_ref, CONV_A_WIDTH, r * cr, cr))
        for j in range(CONV_A_WIDTH):
            dw_ref[j:j + 1, :] = dw[j]

    col = lambda off: pl.BlockSpec((t, LANES), lambda j: (0, j + off))
    wsp = pl.BlockSpec((CONV_A_WIDTH, LANES), lambda j: (0, j))
    return _pc(body, name="ev_a_conv_bwd", grid=(nb,), in_specs=[col(0), col(0), col(nb), wsp],
               out_specs=(col(0), col(0), wsp),
               out_shape=(S((t, D_A), f32), S((t, D_A), f32), S((CONV_A_WIDTH, D_A), f32)),
               scratch_shapes=[pltpu.VMEM((t + CONV_LEAD, LANES), f32), pltpu.VMEM((t + CONV_LEAD, LANES), f32)],
               compiler_params=_cparams(("arbitrary",)))(duc, p, p, conv_a)


def _ev_b(p, conv_b, y):
    t = p.shape[0]
    cr = _row_block8(t)
    nb = D_A // LANES

    def body(gb_ref, gc_ref, xi_ref, w_ref, y_ref, o_ref, pad_ref, stage_ref):
        _fill_front_padded(pad_ref, gc_ref[...] * xi_ref[...], t)
        for r in range(t // cr):
            rows = slice(r * cr, (r + 1) * cr)
            stage_ref[rows, :] = gb_ref[rows, :] * _conv_rows(pad_ref, w_ref, CONV_B_WIDTH, r * cr, cr)
        o_ref[...] = stage_ref[...].astype(bf16)

    col = lambda off: pl.BlockSpec((t, LANES), lambda j: (0, j + off))
    return _pc(body, name="ev_b", grid=(nb,),
               in_specs=[col(2 * nb), col(3 * nb), col(4 * nb), pl.BlockSpec((CONV_B_WIDTH, LANES), lambda j: (0, j)), HBM],
               out_specs=col(nb), out_shape=S(y.shape, bf16), input_output_aliases={4: 0},
               scratch_shapes=[pltpu.VMEM((t + CONV_LEAD, LANES), f32), pltpu.VMEM((t, LANES), f32)],
               compiler_params=_cparams(("arbitrary",)))(p, p, p, conv_b, y)


def _ev_b_bwd(dy, p, conv_b):
    t = p.shape[0]
    cr = _row_block8(t)
    nb = D_A // LANES

    def body(dy_ref, gb_ref, gc_ref, xi_ref, w_ref, dgb_ref, dgc_ref, dxi_ref, dw_ref, pad_ref, padb_ref):
        _fill_front_padded(pad_ref, gc_ref[...] * xi_ref[...], t)
        _fill_back_padded(padb_ref, dy_ref[...] * gb_ref[...], t)
        dw = None
        for r in range(t // cr):
            rows = slice(r * cr, (r + 1) * cr)
            dgb_ref[rows, :] = dy_ref[rows, :] * _conv_rows(pad_ref, w_ref, CONV_B_WIDTH, r * cr, cr)
            dcx = _conv_t_rows(padb_ref, w_ref, CONV_B_WIDTH, r * cr, cr)
            dgc_ref[rows, :] = dcx * xi_ref[rows, :]
            dxi_ref[rows, :] = dcx * gc_ref[rows, :]
            dw = _acc_list(dw, _conv_dw_rows(padb_ref[rows, :], pad_ref, CONV_B_WIDTH, r * cr, cr))
        for j in range(CONV_B_WIDTH):
            dw_ref[j:j + 1, :] = dw[j]

    col = lambda off: pl.BlockSpec((t, LANES), lambda j: (0, j + off))
    wsp = pl.BlockSpec((CONV_B_WIDTH, LANES), lambda j: (0, j))
    return _pc(body, name="ev_b_bwd", grid=(nb,), in_specs=[col(nb), col(2 * nb), col(3 * nb), col(4 * nb), wsp],
               out_specs=(col(0), col(0), col(0), wsp),
               out_shape=(S((t, D_A), f32), S((t, D_A), f32), S((t, D_A), f32), S((CONV_B_WIDTH, D_A), f32)),
               scratch_shapes=[pltpu.VMEM((t + CONV_LEAD, LANES), f32), pltpu.VMEM((t + CONV_LEAD, LANES), f32)],
               compiler_params=_cparams(("arbitrary",)))(dy, p, p, p, conv_b)


def _ffn_mid(u, conv_w, conv_b, name):
    t = u.shape[0]
    cr = _row_block8(t)
    nb = D_FF // LANES

    def body(gt_ref, vl_ref, w_ref, b_ref, o_ref, pad_ref, stage_ref):
        _fill_front_padded(pad_ref, gt_ref[...], t)
        for r in range(t // cr):
            rows = slice(r * cr, (r + 1) * cr)
            gc = _conv_rows(pad_ref, w_ref, FF_CONV_WIDTH, r * cr, cr) + b_ref[...]
            stage_ref[rows, :] = gc * _sigmoid(gc) * vl_ref[rows, :]
        o_ref[...] = stage_ref[...].astype(bf16)

    col = lambda off: pl.BlockSpec((t, LANES), lambda j: (0, j + off))
    return _pc(body, name=name, grid=(nb,),
               in_specs=[col(0), col(nb), pl.BlockSpec((FF_CONV_WIDTH, LANES), lambda j: (0, j)),
                         pl.BlockSpec((1, LANES), lambda j: (0, j))],
               out_specs=col(0), out_shape=S((t, D_FF), bf16),
               scratch_shapes=[pltpu.VMEM((t + CONV_LEAD, LANES), f32), pltpu.VMEM((t, LANES), f32)],
               compiler_params=_cparams(("arbitrary",)))(u, u, conv_w, conv_b.reshape(1, D_FF))


def _ffn_mid_bwd(dz, u, conv_w, conv_b, name):
    t = u.shape[0]
    cr = _row_block8(t)
    nb = D_FF // LANES

    def body(dz_ref, gt_ref, vl_ref, w_ref, b_ref, du_ref, dw_ref, db_ref, pad_ref, padb_ref, dval_ref):
        s = pl.program_id(1)

        @pl.when(s == 0)
        def _():
            _fill_front_padded(pad_ref, gt_ref[...], t)
            for r in range(t // cr):
                rows = slice(r * cr, (r + 1) * cr)
                gc = _conv_rows(pad_ref, w_ref, FF_CONV_WIDTH, r * cr, cr) + b_ref[...]
                sg = _sigmoid(gc)
                dzr = dz_ref[rows, :]
                dval_ref[rows, :] = dzr * gc * sg
                padb_ref[rows, :] = dzr * vl_ref[rows, :] * sg * (1.0 + gc * (1.0 - sg))
            padb_ref[t:t + CONV_LEAD, :] = jnp.zeros((CONV_LEAD, LANES), f32)
            dw, db = None, None
            for r in range(t // cr):
                rows = slice(r * cr, (r + 1) * cr)
                du_ref[rows, :] = _conv_t_rows(padb_ref, w_ref, FF_CONV_WIDTH, r * cr, cr)
                dgc = padb_ref[rows, :]
                dw = _acc_list(dw, _conv_dw_rows(dgc, pad_ref, FF_CONV_WIDTH, r * cr, cr))
                pb = jnp.sum(dgc, axis=0, keepdims=True)
                db = pb if db is None else db + pb
            for j in range(FF_CONV_WIDTH):
                dw_ref[j:j + 1, :] = dw[j]
            db_ref[...] = db

        @pl.when(s == 1)
        def _():
            du_ref[...] = dval_ref[...]

    col = lambda off: pl.BlockSpec((t, LANES), lambda j, s: (0, j + off))
    wsp = pl.BlockSpec((FF_CONV_WIDTH, LANES), lambda j, s: (0, j))
    bsp = pl.BlockSpec((1, LANES), lambda j, s: (0, j))
    return _pc(body, name=name, grid=(nb, 2), in_specs=[col(0), col(0), col(nb), wsp, bsp],
               out_specs=(pl.BlockSpec((t, LANES), lambda j, s: (0, s * nb + j)), wsp, bsp),
               out_shape=(S((t, 2 * D_FF), f32), S((FF_CONV_WIDTH, D_FF), f32), S((1, D_FF), f32)),
               scratch_shapes=[pltpu.VMEM((t + CONV_LEAD, LANES), f32), pltpu.VMEM((t + CONV_LEAD, LANES), f32),
                               pltpu.VMEM((t, LANES), f32)],
               compiler_params=_cparams(("arbitrary", "arbitrary")))(dz, u, u, conv_w, conv_b.reshape(1, D_FF))


def _swap_halves(x):
    w = x.shape[1]
    lane = lax.broadcasted_iota(jnp.int32, x.shape, 1) % HEAD_DIM
    return jnp.where(lane < HEAD_DIM // 2, pltpu.roll(x, w - HEAD_DIM // 2, axis=1), pltpu.roll(x, HEAD_DIM // 2, axis=1))


def _rope_pack(patt, c64, s64):
    t = patt.shape[0]
    tp = t + ATT_PAD

    def body(p_ref, c_ref, s_ref, q_ref, k_ref, v_ref):
        c, s = c_ref[...], s_ref[...]

        def rope(x, nh):
            cc = jnp.concatenate([c] * nh, axis=1)
            ss = jnp.concatenate([s] * nh, axis=1)
            return x * cc + _swap_halves(x) * ss

        for ref, val in ((q_ref, rope(p_ref[:, 0:D_ATT], N_Q_HEADS)),
                         (k_ref, rope(p_ref[:, D_ATT:D_ATT + D_KV], N_KV_HEADS)),
                         (v_ref, p_ref[:, D_ATT + D_KV:ATT_COLS])):
            ref[0:ATT_PAD, :] = jnp.zeros((ATT_PAD, val.shape[1]), bf16)
            ref[ATT_PAD:tp, :] = val.astype(bf16)

    return _pc(body, name="rope_pack", in_specs=[_full((t, ATT_COLS)), _full((t, HEAD_DIM)), _full((t, HEAD_DIM))],
               out_specs=(_full((tp, D_ATT)), _full((tp, D_KV)), _full((tp, D_KV))), grid=(1,),
               out_shape=(S((tp, D_ATT), bf16), S((tp, D_KV), bf16), S((tp, D_KV), bf16)),
               compiler_params=_cparams(("arbitrary",)))(patt, c64, s64)


def _rope_bwd(dqp, dkp, dvp, c64, s64):
    tp = dqp.shape[0]
    t = tp - ATT_PAD

    def body(dq_ref, dk_ref, dv_ref, c_ref, s_ref, o_ref):
        c, s = c_ref[...], s_ref[...]

        def unrope(dy, nh):
            cc = jnp.concatenate([c] * nh, axis=1)
            ss = jnp.concatenate([s] * nh, axis=1)
            return dy * cc + _swap_halves(dy * ss)

        o_ref[:, 0:D_ATT] = unrope(dq_ref[ATT_PAD:tp, :], N_Q_HEADS)
        o_ref[:, D_ATT:D_ATT + D_KV] = unrope(dk_ref[ATT_PAD:tp, :], N_KV_HEADS)
        o_ref[:, D_ATT + D_KV:ATT_COLS] = dv_ref[ATT_PAD:tp, :]

    return _pc(body, name="rope_bwd", grid=(1,),
               in_specs=[_full((tp, D_ATT)), _full((tp, D_KV)), _full((tp, D_KV)), _full((t, HEAD_DIM)), _full((t, HEAD_DIM))],
               out_specs=_full((t, ATT_COLS)), out_shape=S((t, ATT_COLS), f32),
               compiler_params=_cparams(("arbitrary",)))(dqp, dkp, dvp, c64, s64)


def _attn_masks(n):
    rows = GQA_GROUP * BLOCK
    ri = lax.broadcasted_iota(jnp.int32, (rows, BLOCK), 0) % BLOCK
    ci = lax.broadcasted_iota(jnp.int32, (rows, BLOCK), 1)
    m_cur = (ci <= ri) & (ci >= jnp.where(n >= 1, 0, ATT_PAD))
    m_prev = ci > ri + jnp.where(n >= 2, 0, BLOCK)
    m_meta = ci >= jnp.where(n >= 1, ATT_PAD, BLOCK)
    return m_cur, m_prev, m_meta


def _attn_probs(qg, kc, kp, km, masks, skv):
    def scores(k, m):
        s = lax.dot_general(qg, k, _DIMS["nt"], preferred_element_type=f32) * ATT_SCALE
        return jnp.where(m, s, NEG_INF)
    s_c, s_p, s_m = scores(kc, masks[0]), scores(kp, masks[1]), scores(km, masks[2])
    mx = jnp.maximum(jnp.maximum(jnp.max(s_c, axis=-1, keepdims=True), jnp.max(s_p, axis=-1, keepdims=True)),
                     jnp.maximum(jnp.max(s_m, axis=-1, keepdims=True), skv))
    e_c, e_p, e_m, e_s = jnp.exp(s_c - mx), jnp.exp(s_p - mx), jnp.exp(s_m - mx), jnp.exp(skv - mx)
    den = (jnp.sum(e_c, axis=-1, keepdims=True) + jnp.sum(e_p, axis=-1, keepdims=True)
           + jnp.sum(e_m, axis=-1, keepdims=True) + e_s)
    inv = 1.0 / den
    return e_c * inv, e_p * inv, e_m * inv, e_s * inv


def _sink_rows(sk_ref, g):
    hrow = lax.broadcasted_iota(jnp.int32, (GQA_GROUP * BLOCK, 1), 0) // BLOCK
    skv = jnp.zeros((GQA_GROUP * BLOCK, 1), f32)
    for hh in range(GQA_GROUP):
        skv = jnp.where(hrow == hh, sk_ref[0, GQA_GROUP * g + hh], skv)
    return skv, hrow


def _stack_heads(ref, g):
    return jnp.concatenate([ref[:, (GQA_GROUP * g + hh) * HEAD_DIM:(GQA_GROUP * g + hh + 1) * HEAD_DIM]
                            for hh in range(GQA_GROUP)], axis=0)


def _attn_specs():
    blk = lambda w: pl.BlockSpec((BLOCK, w), lambda n: (n, 0))
    prev = pl.BlockSpec((BLOCK, D_KV), lambda n: (jnp.maximum(n - 1, 0), 0))
    meta = pl.BlockSpec((BLOCK, D_KV), lambda n: (0, 0))
    return blk, prev, meta


def _attn_fwd(qp, kp, vp, sinks):
    tp = qp.shape[0]
    blk, prev, meta = _attn_specs()

    def body(sk_ref, q_ref, kc_ref, kp_ref, km_ref, vc_ref, vp_ref, vm_ref, o_ref):
        masks = _attn_masks(pl.program_id(0))
        for g in range(N_KV_HEADS):
            sl = slice(g * HEAD_DIM, (g + 1) * HEAD_DIM)
            skv, _ = _sink_rows(sk_ref, g)
            p_c, p_p, p_m, _ = _attn_probs(_stack_heads(q_ref, g), kc_ref[:, sl], kp_ref[:, sl], km_ref[:, sl], masks, skv)
            o = (jnp.dot(p_c.astype(bf16), vc_ref[:, sl], preferred_element_type=f32)
                 + jnp.dot(p_p.astype(bf16), vp_ref[:, sl], preferred_element_type=f32)
                 + jnp.dot(p_m.astype(bf16), vm_ref[:, sl], preferred_element_type=f32))
            for hh in range(GQA_GROUP):
                h = GQA_GROUP * g + hh
                o_ref[:, h * HEAD_DIM:(h + 1) * HEAD_DIM] = o[hh * BLOCK:(hh + 1) * BLOCK].astype(bf16)

    return _pc(body, name="attn_fwd", grid=(tp // BLOCK,),
               in_specs=[pl.BlockSpec(memory_space=pltpu.SMEM), blk(D_ATT), blk(D_KV), prev, meta, blk(D_KV), prev, meta],
               out_specs=blk(D_ATT), out_shape=S((tp, D_ATT), bf16),
               compiler_params=_cparams(("arbitrary",)))(sinks, qp, kp, kp, kp, vp, vp, vp)


def _attn_bwd(qp, kp, vp, sinks, dop):
    tp = qp.shape[0]
    blk, prev, meta = _attn_specs()

    def body(sk_ref, q_ref, kc_ref, kp_ref, km_ref, vc_ref, vp_ref, vm_ref, do_ref, dq_ref, dk_ref, dv_ref, dsk_ref):
        n = pl.program_id(0)

        @pl.when(n == 0)
        def _():
            dk_ref[...] = jnp.zeros_like(dk_ref)
            dv_ref[...] = jnp.zeros_like(dv_ref)
            dsk_ref[...] = jnp.zeros_like(dsk_ref)
        masks = _attn_masks(n)
        cur = pl.ds(pl.multiple_of(n * BLOCK, BLOCK), BLOCK)
        prv = pl.ds(pl.multiple_of(jnp.maximum(n - 1, 0) * BLOCK, BLOCK), BLOCK)
        lane = lax.broadcasted_iota(jnp.int32, (1, LANES), 1)
        dsk = jnp.zeros((1, LANES), f32)
        for g in range(N_KV_HEADS):
            sl = slice(g * HEAD_DIM, (g + 1) * HEAD_DIM)
            skv, hrow = _sink_rows(sk_ref, g)
            qg = _stack_heads(q_ref, g)
            dog = _stack_heads(do_ref, g)
            ks = (kc_ref[:, sl], kp_ref[:, sl], km_ref[:, sl])
            vs = (vc_ref[:, sl], vp_ref[:, sl], vm_ref[:, sl])
            probs = _attn_probs(qg, ks[0], ks[1], ks[2], masks, skv)
            dps = [lax.dot_general(dog, v, _DIMS["nt"], preferred_element_type=f32) for v in vs]
            delta = sum(jnp.sum(p * dp, axis=-1, keepdims=True) for p, dp in zip(probs[:3], dps))
            dss = [(p * (dp - delta) * ATT_SCALE).astype(bf16) for p, dp in zip(probs[:3], dps)]
            dq = sum(jnp.dot(ds, k, preferred_element_type=f32) for ds, k in zip(dss, ks))
            for hh in range(GQA_GROUP):
                h = GQA_GROUP * g + hh
                dq_ref[:, h * HEAD_DIM:(h + 1) * HEAD_DIM] = dq[hh * BLOCK:(hh + 1) * BLOCK]
                dsk = dsk + jnp.where(lane == h, -jnp.sum(jnp.where(hrow == hh, probs[3] * delta, 0.0)), 0.0)
            for rows, p, ds in zip((cur, prv, slice(0, BLOCK)), probs[:3], dss):
                dv_ref[rows, sl] += lax.dot_general(p.astype(bf16), dog, _DIMS["tn"], preferred_element_type=f32)
                dk_ref[rows, sl] += lax.dot_general(ds, qg, _DIMS["tn"], preferred_element_type=f32)
        dsk_ref[...] += dsk

    return _pc(body, name="attn_bwd", grid=(tp // BLOCK,),
               in_specs=[pl.BlockSpec(memory_space=pltpu.SMEM), blk(D_ATT), blk(D_KV), prev, meta, blk(D_KV), prev, meta,
                         blk(D_ATT)],
               out_specs=(blk(D_ATT), _full((tp, D_KV)), _full((tp, D_KV)), _full((1, LANES))),
               out_shape=(S((tp, D_ATT), f32), S((tp, D_KV), f32), S((tp, D_KV), f32), S((1, LANES), f32)),
               compiler_params=_cparams(("arbitrary",)))(sinks, qp, kp, kp, kp, vp, vp, vp, dop)


def _seg(x, bm):
    hi = x.astype(bf16)
    lo = (x - hi.astype(f32)).astype(bf16)
    return jnp.dot(jnp.concatenate([hi, lo], axis=1), bm, preferred_element_type=f32)


@jax.custom_vjp
def _seg_linear(x, bm):
    return _seg(x, bm)


_seg_linear.defvjp(lambda x, bm: (_seg(x, bm), bm), lambda bm, ct: (_seg(ct, bm), jnp.zeros_like(bm)))


def _softplus(y):
    return jnp.maximum(y, 0.0) + jnp.log(1.0 + jnp.exp(-jnp.abs(y)))


def _prep_fn(xr, xk, xwd, xad, xgd, w0, w2, a0, a2, g2, k_k, k_a, bm, seg=_seg):
    xw = w0 + jnp.dot(jnp.tanh(xwd), w2, preferred_element_type=f32)
    decay = jnp.exp(-jnp.exp(-_softplus(-xw) - 0.5))
    alpha = _sigmoid(a0 + jnp.dot(xad, a2, preferred_element_type=f32))
    g = jnp.dot(_sigmoid(xgd), g2, preferred_element_type=f32)
    kk = xk * k_k
    kkn = kk / jnp.maximum(jnp.sqrt(seg(kk * kk, bm)), 1e-12)
    k2 = xk * (1.0 + (alpha - 1.0) * k_a)
    return decay, k2, -kkn, kkn * alpha, g


def _split_cols(x):
    o1, o2, o3 = 3 * D_R, 3 * D_R + LORA_W, 3 * D_R + LORA_W + LORA_A
    return x[:, 0:D_R], x[:, D_R:2 * D_R], x[:, 2 * D_R:o1], x[:, o1:o2], x[:, o2:o3], x[:, o3:RWKV_COLS]


def _shifted(sh_ref, x, halo, first, rb):
    sh_ref[0:SUBLANES, :] = jnp.where(first, 0.0, halo)
    sh_ref[SUBLANES:SUBLANES + rb, :] = x
    return sh_ref[SUBLANES - 1:SUBLANES - 1 + rb, :]


_PREP_PARAMS = ("od_w0", "od_w2", "od_a0", "od_a2", "od_g2", "od_k_k", "od_k_a")


def _rwkv_prep(pr, mu, params, bm):
    t = pr.shape[0]
    rb = _row_block8(t)
    hb = rb // SUBLANES

    def body(pr_ref, halo_ref, mu_ref, w0, w2, a0, a2, g2, kk_ref, ka_ref, bm_ref, *outs_sh):
        outs, sh_ref = outs_sh[:-1], outs_sh[-1]
        x = pr_ref[...]
        prev = _shifted(sh_ref, x, halo_ref[...], pl.program_id(0) == 0, rb)
        xr, xk, xv, xwd, xad, xgd = _split_cols(x + (prev - x) * mu_ref[...])
        bmv = bm_ref[...]
        decay, k2, a_s, b_s, g = _prep_fn(xr, xk, xwd, xad, xgd, w0[...], w2[...], a0[...], a2[...], g2[...],
                                          kk_ref[...], ka_ref[...], bmv)
        vals = (xr, xv, decay, k2, a_s, b_s, decay * xr, _seg(b_s * xr, bmv), _seg(k2 * xr, bmv), g)
        for ref, val in zip(outs, vals):
            ref[...] = val

    row = pl.BlockSpec((rb, RWKV_COLS), lambda i: (i, 0))
    halo = pl.BlockSpec((SUBLANES, RWKV_COLS), lambda i: (jnp.maximum(i * hb - 1, 0), 0))
    orow = pl.BlockSpec((rb, D_R), lambda i: (i, 0))
    return _pc(body, name="rwkv_prep", grid=(t // rb,),
               in_specs=[row, halo, _full((1, RWKV_COLS))] + [_full(p.shape) for p in params] + [_full(bm.shape)],
               out_specs=(orow,) * 10, out_shape=(S((t, D_R), f32),) * 10,
               scratch_shapes=[pltpu.VMEM((rb + SUBLANES, RWKV_COLS), f32)],
               compiler_params=_cparams(("arbitrary",)))(pr, pr, mu, *params, bm)


def _rwkv_prep_bwd(pr, mu, params, bm, cts):
    t = pr.shape[0]
    rb = _row_block8(t)
    hb = rb // SUBLANES
    counts = [len(c) for c in cts]
    flat = [a for c in cts for a in c]

    def body(pr_ref, halo_ref, mu_ref, w0, w2, a0, a2, g2, kk_ref, ka_ref, bm_ref, *rest):
        ct_refs, rest = rest[:len(flat)], rest[len(flat):]
        dx_ref, dmu_ref = rest[0], rest[1]
        dpar_refs, sh_ref = rest[2:9], rest[9]

        @pl.when(pl.program_id(0) == 0)
        def _():
            dmu_ref[...] = jnp.zeros_like(dmu_ref)
            for r in dpar_refs:
                r[...] = jnp.zeros_like(r)
        sums, pos = [], 0
        for c in counts:
            sums.append(sum(r[...] for r in ct_refs[pos:pos + c]))
            pos += c
        x = pr_ref[...]
        prev = _shifted(sh_ref, x, halo_ref[...], pl.program_id(0) == 0, rb)
        xr, xk, xv, xwd, xad, xgd = _split_cols(x + (prev - x) * mu_ref[...])
        bmv = bm_ref[...]
        _, vjp = jax.vjp(lambda *a: _prep_fn(*a, bmv, _seg_linear), xr, xk, xwd, xad, xgd, w0[...], w2[...], a0[...], a2[...],
                         g2[...], kk_ref[...], ka_ref[...])
        grads = vjp(tuple(sums[:5]))
        dxr, dxk, dxwd, dxad, dxgd = grads[:5]
        o1, o2, o3 = 3 * D_R, 3 * D_R + LORA_W, 3 * D_R + LORA_W + LORA_A
        dx_ref[:, 0:D_R] = dxr + sums[5]
        dx_ref[:, D_R:2 * D_R] = dxk
        dx_ref[:, 2 * D_R:o1] = sums[6]
        dx_ref[:, o1:o2] = dxwd
        dx_ref[:, o2:o3] = dxad
        dx_ref[:, o3:RWKV_COLS] = dxgd
        dmu_ref[...] += jnp.sum(dx_ref[...] * (prev - x), axis=0, keepdims=True)
        for r, gval in zip(dpar_refs, grads[5:]):
            r[...] += gval

    row = pl.BlockSpec((rb, RWKV_COLS), lambda i: (i, 0))
    halo = pl.BlockSpec((SUBLANES, RWKV_COLS), lambda i: (jnp.maximum(i * hb - 1, 0), 0))
    crow = pl.BlockSpec((rb, D_R), lambda i: (i, 0))
    return _pc(body, name="rwkv_prep_bwd", grid=(t // rb,),
               in_specs=[row, halo, _full((1, RWKV_COLS))] + [_full(p.shape) for p in params] + [_full(bm.shape)]
               + [crow] * len(flat),
               out_specs=(row, _full((1, RWKV_COLS))) + tuple(_full(p.shape) for p in params),
               out_shape=(S((t, RWKV_COLS), f32), S((1, RWKV_COLS), f32)) + tuple(S(p.shape, f32) for p in params),
               scratch_shapes=[pltpu.VMEM((rb + SUBLANES, RWKV_COLS), f32)],
               compiler_params=_cparams(("arbitrary",)))(pr, pr, mu, *params, bm, *flat)


def _shift_bwd(dxs, mu):
    t = dxs.shape[0]
    rb = _row_block8(t)
    hb = rb // SUBLANES
    nblk = t // rb

    def body(dx_ref, halo_ref, mu_ref, o_ref, sh_ref):
        dx = dx_ref[...]
        sh_ref[0:rb, :] = dx
        sh_ref[rb:rb + SUBLANES, :] = jnp.where(pl.program_id(0) == nblk - 1, 0.0, halo_ref[...])
        m = mu_ref[...]
        o_ref[...] = dx * (1.0 - m) + sh_ref[1:1 + rb, :] * m

    row = pl.BlockSpec((rb, RWKV_COLS), lambda i: (i, 0))
    halo = pl.BlockSpec((SUBLANES, RWKV_COLS), lambda i: (jnp.minimum((i + 1) * hb, t // SUBLANES - 1), 0))
    return _pc(body, name="rwkv_shift_bwd", grid=(nblk,), in_specs=[row, halo, _full((1, RWKV_COLS))],
               out_specs=row, out_shape=S((t, RWKV_COLS), f32),
               scratch_shapes=[pltpu.VMEM((rb + SUBLANES, RWKV_COLS), f32)],
               compiler_params=_cparams(("arbitrary",)))(dxs, dxs, mu)


def _post_fn(y, xr, k2, xv, g, lg, lb, rk, bm, seg=_seg):
    inv_n = 1.0 / HEAD_DIM
    yc = y - seg(y, bm) * inv_n
    var = seg(yc * yc, bm) * inv_n
    yn = yc * lax.rsqrt(var + RWKV_GN_EPS) * lg + lb
    return (yn + seg(xr * k2 * rk, bm) * xv) * g


def _rwkv_post(y, xr, k2, xv, g, lg, lb, rk, bm):
    t = y.shape[0]
    rb = _row_block8(t)

    def body(y_ref, xr_ref, k2_ref, xv_ref, g_ref, lg_ref, lb_ref, rk_ref, bm_ref, o_ref):
        o_ref[...] = _post_fn(y_ref[...], xr_ref[...], k2_ref[...], xv_ref[...], g_ref[...], lg_ref[...], lb_ref[...],
                              rk_ref[...], bm_ref[...])

    row = pl.BlockSpec((rb, D_R), lambda i: (i, 0))
    vec = _full((1, D_R))
    return _pc(body, name="rwkv_post", grid=(t // rb,), in_specs=[row] * 5 + [vec] * 3 + [_full(bm.shape)],
               out_specs=row, out_shape=S((t, D_R), f32),
               compiler_params=_cparams(("arbitrary",)))(y, xr, k2, xv, g, lg, lb, rk, bm)


def _rwkv_post_bwd(dy1, y, xr, k2, xv, g, lg, lb, rk, bm):
    t = y.shape[0]
    rb = _row_block8(t)

    def body(dy_ref, y_ref, xr_ref, k2_ref, xv_ref, g_ref, lg_ref, lb_ref, rk_ref, bm_ref, *outs):
        @pl.when(pl.program_id(0) == 0)
        def _():
            for r in outs[5:]:
                r[...] = jnp.zeros_like(r)
        bmv = bm_ref[...]
        _, vjp = jax.vjp(lambda *a: _post_fn(*a, bmv, _seg_linear), y_ref[...], xr_ref[...], k2_ref[...], xv_ref[...], g_ref[...],
                         lg_ref[...], lb_ref[...], rk_ref[...])
        grads = vjp(dy_ref[...])
        for r, gval in zip(outs[:5], grads[:5]):
            r[...] = gval
        for r, gval in zip(outs[5:], grads[5:]):
            r[...] += gval

    row = pl.BlockSpec((rb, D_R), lambda i: (i, 0))
    vec = _full((1, D_R))
    return _pc(body, name="rwkv_post_bwd", grid=(t // rb,),
               in_specs=[pl.BlockSpec((rb, D_R), lambda i: (i, 1))] + [row] * 5 + [vec] * 3 + [_full(bm.shape)],
               out_specs=(row,) * 5 + (vec,) * 3, out_shape=(S((t, D_R), f32),) * 5 + (S((1, D_R), f32),) * 3,
               compiler_params=_cparams(("arbitrary",)))(dy1, y, xr, k2, xv, g, lg, lb, rk, bm)


def _seg2(x, bb):
    hi = x.astype(bf16)
    lo = (x - hi.astype(f32)).astype(bf16)
    return jnp.dot(jnp.concatenate([hi, lo], axis=1), bb, preferred_element_type=f32)


def _row4(rows, j):
    return jnp.concatenate([jnp.broadcast_to(rows[j:j + 1, p * LANES:(p + 1) * LANES], (HEAD_DIM, LANES))
                            for p in range(4)], axis=0)


def _scan_consts():
    lane_group = jnp.arange(LANES) // HEAD_DIM
    b128 = (lane_group[:, None] == lane_group[None, :]).astype(bf16)
    bb = jnp.concatenate([b128, b128], axis=0)
    qsel = (jnp.arange(PAIR_ROWS)[:, None] % HEAD_DIM == jnp.arange(LANES)[None, :] % HEAD_DIM).astype(f32)
    return bb, qsel


def _store_cols(acc_ref, o_ref, tc):
    for p in range(4):
        blk = acc_ref[p * HEAD_DIM:(p + 1) * HEAD_DIM, :].T
        o_ref[:, (2 * p) * HEAD_DIM:(2 * p + 1) * HEAD_DIM] = blk[0:tc]
        o_ref[:, (2 * p + 1) * HEAD_DIM:(2 * p + 2) * HEAD_DIM] = blk[HEAD_DIM:HEAD_DIM + tc]


PAIR_GROUP = 2 * SUBLANES


def _rwkv_pairs(w, a, b, k, wr, bm):
    t = w.shape[0]
    rb = _row_block8(t)

    def body(w_ref, a_ref, b_ref, k_ref, wr_ref, bm_ref, *outs_sh):
        outs, sh_ref = outs_sh[:-1], outs_sh[-1]

        def second(ref):
            sh_ref[0:rb, :] = ref[...]
            sh_ref[rb:rb + SUBLANES, :] = jnp.zeros((SUBLANES, D_R), f32)
            return sh_ref[1:1 + rb, :]

        w1, b1, k1 = w_ref[...], b_ref[...], k_ref[...]
        w2, a2, wr2 = second(w_ref), second(a_ref), second(wr_ref)
        bmv = bm_ref[...]
        vals = (w1 * a2, w1 * wr2, w1 * w2, b1 * w2, k1 * w2, _seg(b1 * a2, bmv), _seg(k1 * a2, bmv),
                _seg(b1 * wr2, bmv), _seg(k1 * wr2, bmv))
        for ref, val in zip(outs, vals):
            ref[...] = val

    row = pl.BlockSpec((rb, D_R), lambda i: (i, 0))
    return _pc(body, name="rwkv_pairs", grid=(t // rb,), in_specs=[row] * 5 + [_full(bm.shape)],
               out_specs=(row,) * 9, out_shape=(S((t, D_R), f32),) * 9,
               scratch_shapes=[pltpu.VMEM((rb + SUBLANES, D_R), f32)],
               compiler_params=_cparams(("arbitrary",)))(w, a, b, k, wr, bm)


def _wkv_fwd(w, k, v, a, b, wr, br, kr, pairs):
    t = w.shape[0]
    tc = SCAN_CHUNK
    bb, qsel = _scan_consts()

    def body(*refs):
        step_refs, pair_refs = refs[0:8], refs[8:17]
        bb_ref, q_ref, y_ref, st_ref, sa_ref, vb_ref, s_scr, yacc = refs[17:]

        @pl.when(pl.program_id(0) == 0)
        def _():
            s_scr[...] = jnp.zeros_like(s_scr)
        bbv, qv = bb_ref[...], q_ref[...]
        lane64 = lax.broadcasted_iota(jnp.int32, (PAIR_ROWS, LANES), 1) % HEAD_DIM

        def halves(x):
            hi = x.astype(bf16)
            return jnp.concatenate([hi, (x - hi.astype(f32)).astype(bf16)], axis=1)

        def group(gi, s):
            base = pl.multiple_of(gi * PAIR_GROUP, PAIR_GROUP)
            w16, k16, v16, a16, b16, wr16, br16, kr16 = step_refs
            a2p, r2p, w12p, b1wp, k1wp, betap, kappap, bwrp, kwrp = pair_refs

            def rows8(ref, j):
                return ref[pl.ds(base + (j // SUBLANES) * SUBLANES, SUBLANES), :]

            def bcast(rows, j, p):
                return jnp.broadcast_to(rows[j % SUBLANES:j % SUBLANES + 1, p * LANES:(p + 1) * LANES], (HEAD_DIM, LANES))

            step = lambda ref, j, p: bcast(rows8(ref, j), j, p)
            qp = qv[0:HEAD_DIM]
            lane = lane64[0:HEAD_DIM]
            for q in range(SUBLANES):
                j1, j2 = 2 * q, 2 * q + 1
                t1 = base + j1
                nxt = []
                for p in range(4):
                    sl = slice(p * HEAD_DIM, (p + 1) * HEAD_DIM)
                    sp = s[sl]
                    lhs = [halves(jnp.concatenate([sp * step(a16, j1, p), sp * step(a2p, j1, p), sp * step(wr16, j1, p),
                                                   sp * step(r2p, j1, p)], axis=0))]
                    for j in (j1, j2):
                        v8 = rows8(v16, j)
                        vh8 = v8.astype(bf16).astype(f32)
                        lhs.append(jnp.concatenate([(qp * bcast(vh8, j, p)).astype(bf16),
                                                    (qp * bcast(v8 - vh8, j, p)).astype(bf16)], axis=1))
                    r = jnp.dot(jnp.concatenate(lhs, axis=0), bbv, preferred_element_type=f32)
                    sa1, p2, z1, z2, vb1, vb2 = (r[n * HEAD_DIM:(n + 1) * HEAD_DIM] for n in range(6))
                    sa2 = p2 + sa1 * step(betap, j1, p) + vb1 * step(kappap, j1, p)
                    y1 = z1 + sa1 * step(br16, j1, p) + vb1 * step(kr16, j1, p)
                    y2 = (z2 + sa1 * step(bwrp, j1, p) + vb1 * step(kwrp, j1, p)) + (sa2 * step(br16, j2, p)
                                                                                      + vb2 * step(kr16, j2, p))
                    yacc[sl, :] = jnp.where(lane == t1, y1, jnp.where(lane == t1 + 1, y2, yacc[sl, :]))
                    st_ref[base // 2 + q, sl, :] = sp
                    sa_ref[t1, sl, :] = sa1
                    sa_ref[t1 + 1, sl, :] = sa2
                    vb_ref[t1, sl, :] = vb1
                    vb_ref[t1 + 1, sl, :] = vb2
                    nxt.append(((sp * step(w12p, j1, p) + sa1 * step(b1wp, j1, p)) + vb1 * step(k1wp, j1, p))
                               + (sa2 * step(b16, j2, p) + vb2 * step(k16, j2, p)))
                s = jnp.concatenate(nxt, axis=0)
            return s

        s_scr[...] = lax.fori_loop(0, tc // PAIR_GROUP, group, s_scr[...])
        _store_cols(yacc, y_ref, tc)

    row = pl.BlockSpec((tc, D_R), lambda c: (c, 0))
    tiles = pl.BlockSpec((tc, PAIR_ROWS, LANES), lambda c: (c, 0, 0))
    return _pc(body, name="wkv_fwd", grid=(t // tc,),
               in_specs=[row] * 17 + [_full(bb.shape), _full(qsel.shape)],
               out_specs=(row, pl.BlockSpec((tc // 2, PAIR_ROWS, LANES), lambda c: (c, 0, 0)), tiles, tiles),
               out_shape=(S((t, D_R), f32), S((t // 2, PAIR_ROWS, LANES), f32)) + (S((t, PAIR_ROWS, LANES), f32),) * 2,
               scratch_shapes=[pltpu.VMEM((PAIR_ROWS, LANES), f32), pltpu.VMEM((PAIR_ROWS, LANES), f32)],
               compiler_params=_cparams(("arbitrary",)))(w, k, v, a, b, wr, br, kr, *pairs, bb, qsel)


def _wkv_bwd(sprev, sab, vbb, w, k, a, b, r, dy):
    t = w.shape[0]
    tc = SCAN_CHUNK
    nc = t // tc
    bb, qsel = _scan_consts()

    def body(st_ref, sa_ref, vb_ref, w_ref, k_ref, a_ref, b_ref, r_ref, dy_ref, bb_ref, q_ref,
             dr_ref, dw_ref, dk_ref, dv_ref, da_ref, db_ref, g_scr, dvacc, rows_scr):
        @pl.when(pl.program_id(0) == 0)
        def _():
            g_scr[...] = jnp.zeros_like(g_scr)
        bbv, qv = bb_ref[...], q_ref[...]
        lane64 = lax.broadcasted_iota(jnp.int32, (PAIR_ROWS, LANES), 1) % HEAD_DIM
        outs = (dr_ref, dw_ref, db_ref, dk_ref, da_ref)

        def colsums(slot, j, x):
            for p in range(4):
                rows_scr[slot, j:j + 1, p * LANES:(p + 1) * LANES] = jnp.sum(x[p * HEAD_DIM:(p + 1) * HEAD_DIM], axis=0,
                                                                           keepdims=True)

        def group(i, g):
            base = pl.multiple_of((tc // SUBLANES - 1 - i) * SUBLANES, SUBLANES)
            w8, k8, a8, b8, r8, dy8 = (ref[pl.ds(base, SUBLANES), :] for ref in (w_ref, k_ref, a_ref, b_ref, r_ref, dy_ref))

            def after_step(j, sp):
                return sp * _row4(w8, j) + sa_ref[base + j] * _row4(b8, j) + vb_ref[base + j] * _row4(k8, j)

            def back_step(j, sp, s_t, g):
                tt = base + j
                u, vb = sa_ref[tt], vb_ref[tt]
                a4, b4, w4, k4 = _row4(a8, j), _row4(b8, j), _row4(w8, j), _row4(k8, j)
                dyb = _seg2(qv * _row4(dy8, j), bbv)
                g = g + dyb * _row4(r8, j)
                rr2 = _seg2(jnp.concatenate([g * b4, g * k4], axis=0), bbv)
                du, dvb = rr2[0:PAIR_ROWS], rr2[PAIR_ROWS:2 * PAIR_ROWS]
                for slot, val in enumerate((s_t * dyb, g * sp, g * u, g * vb, sp * du)):
                    colsums(slot, j, val)
                dvacc[...] = jnp.where(lane64 == tt, dvb, dvacc[...])
                return g * w4 + du * a4

            for q in reversed(range(SUBLANES // 2)):
                s0 = st_ref[base // 2 + q]
                s1 = after_step(2 * q, s0)
                g = back_step(2 * q + 1, s1, after_step(2 * q + 1, s1), g)
                g = back_step(2 * q, s0, s1, g)
            for slot, ref in enumerate(outs):
                ref[pl.ds(base, SUBLANES), :] = rows_scr[slot]
            return g

        g_scr[...] = lax.fori_loop(0, tc // SUBLANES, group, g_scr[...])
        _store_cols(dvacc, dv_ref, tc)

    row = pl.BlockSpec((tc, D_R), lambda c: (nc - 1 - c, 0))
    tiles = pl.BlockSpec((tc, PAIR_ROWS, LANES), lambda c: (nc - 1 - c, 0, 0))
    states = pl.BlockSpec((tc // 2, PAIR_ROWS, LANES), lambda c: (nc - 1 - c, 0, 0))
    return _pc(body, name="wkv_bwd", grid=(nc,),
               in_specs=[states, tiles, tiles] + [row] * 6 + [_full(bb.shape), _full(qsel.shape)],
               out_specs=(row,) * 6, out_shape=(S((t, D_R), f32),) * 6,
               scratch_shapes=[pltpu.VMEM((PAIR_ROWS, LANES), f32), pltpu.VMEM((PAIR_ROWS, LANES), f32),
                               pltpu.VMEM((5, SUBLANES, D_R), f32)],
               compiler_params=_cparams(("arbitrary",)))(sprev, sab, vbb, w, k, a, b, r, dy, bb, qsel)


def _rope_tables(t):
    half = HEAD_DIM // 2
    inv = ROPE_THETA ** (-jnp.arange(half, dtype=f32) / half)
    ang = jnp.arange(t, dtype=f32)[:, None] * inv[None, :]
    cos, sin = jnp.cos(ang), jnp.sin(ang)
    return jnp.concatenate([cos, cos], axis=1), jnp.concatenate([-sin, sin], axis=1)


def _head_matrix():
    grp = jnp.arange(D_R) // HEAD_DIM
    b = (grp[:, None] == grp[None, :]).astype(bf16)
    return jnp.concatenate([b, b], axis=0)


def _ffn_fwd(h, g, get_w, conv_w, conv_b, i):
    hf = _rms_fwd(h, g, f"ffn{i}_norm")
    w_up_t = get_w(f"ff{i}_up", hf)
    u = _mm(hf, w_up_t, "nt", f"ffn{i}_up")
    z = _ffn_mid(u, conv_w, conv_b, f"ffn{i}_mid")
    w_down = get_w(f"ff{i}_down", z)
    return _mm(z, w_down, "nn", f"ffn{i}_down", res=h), (hf, u, z), w_up_t, w_down


def _ffn_bwd(dh, h, saved, g, w_up_t, conv_w, conv_b, w_down, i, put_g):
    hf, u, z = saved
    dz = _mm(dh, w_down, "nt", f"ffn{i}_dz")
    g_down = _mm(z, dh, "tn", f"ffn{i}_gdown", out_dtype=GRAD_WIRE_DTYPE)
    tok = put_g(f"ff{i}_down", g_down)
    du, g_conv, g_convb = _ffn_mid_bwd(dz, u, conv_w, conv_b + tok, f"ffn{i}_mid_bwd")
    g_up_t = _mm(du, hf, "tn", f"ffn{i}_gup", out_dtype=GRAD_WIRE_DTYPE)
    tok = put_g(f"ff{i}_up", g_up_t)
    dhf = _mm(du, w_up_t, "nn", f"ffn{i}_dhf")
    dh_in, g_norm = _rms_bwd(dhf, h, g + tok, dh, f"ffn{i}_norm_bwd")
    return dh_in, dict(conv=g_conv, conv_b=g_convb, norm=g_norm)


def _local_step(x, target, W, get_w, put_g, put_small, tok0):
    t = N_META + x.shape[0]
    c64, s64 = _rope_tables(t)
    bm = _head_matrix()
    h0 = jnp.concatenate([W["meta_tokens"], x], axis=0)

    ev_w_in_t, ev_w_out = get_w("ev_in", None), get_w("ev_out", None)
    hn0 = _rms_fwd(h0, W["norm_mix"][0] + tok0, "mix0_norm")
    p0 = _mm(hn0, ev_w_in_t, "nt", "ev_in")
    uc = _ev_a_conv(p0, W["ev_conv_a"])
    y0 = _ev_b(p0, W["ev_conv_b"], _ev_a_norm(uc, W["ev_ln_a_g"], W["ev_ln_a_b"]))
    h1 = _mm(y0, ev_w_out, "nn", "ev_out", res=h0)
    h2, ffn0, ff0_up_t, ff0_down = _ffn_fwd(h1, W["norm_ffn"][0], get_w, W["ff_conv"][0], W["ff_conv_b"][0], 0)

    hn1 = _rms_fwd(h2, W["norm_mix"][1], "mix1_norm")
    od_w_in_t = get_w("od_in", hn1)
    w_att, w_rwkv = od_w_in_t[:ATT_COLS], od_w_in_t[ATT_COLS:]
    pr = _mm(hn1, w_rwkv, "nt", "od_in_rwkv")
    qp, kp, vp = _rope_pack(_mm(hn1, w_att, "nt", "od_in_att"), c64, s64)
    op = _attn_fwd(qp, kp, vp, W["od_sinks"])
    prep_params = [W[n] for n in _PREP_PARAMS]
    xr, xv, decay, k2, a_s, b_s, wr, br, kr, gate = _rwkv_prep(pr, W["od_mu"], prep_params, bm)
    pairs = _rwkv_pairs(decay, a_s, b_s, k2, wr, bm)
    ysc, sprev, sab, vbb = _wkv_fwd(decay, k2, xv, a_s, b_s, wr, br, kr, pairs)
    rk = W["od_r_k"].reshape(1, D_R)
    yr = _rwkv_post(ysc, xr, k2, xv, gate, W["od_lnx_g"], W["od_lnx_b"], rk, bm)
    y1 = jnp.concatenate([op[ATT_PAD:], yr.astype(bf16)], axis=1)
    od_w_out = get_w("od_out", y1)
    h3 = _mm(y1, od_w_out, "nn", "od_out", res=h2)
    h4, ffn1, ff1_up_t, ff1_down = _ffn_fwd(h3, W["norm_ffn"][1], get_w, W["ff_conv"][1], W["ff_conv_b"][1], 1)

    tgt = jnp.concatenate([jnp.zeros((N_META, D_MODEL), f32), target], axis=0)
    loss, dh4, g_norm_final = _final_loss(h4, W["norm_final"], tgt)

    dh3, gf1 = _ffn_bwd(dh4, h3, ffn1, W["norm_ffn"][1], ff1_up_t, W["ff_conv"][1], W["ff_conv_b"][1], ff1_down, 1, put_g)
    dy1 = _mm(dh3, od_w_out, "nt", "od_dy")
    g_od_w_out = _mm(y1, dh3, "tn", "od_gout", out_dtype=GRAD_WIRE_DTYPE)
    tok = put_g("od_out", g_od_w_out)
    dysc, dxr_p, dk2_p, dxv_p, dgate, g_lnx_g, g_lnx_b, g_rk = _rwkv_post_bwd(
        dy1, ysc, xr, k2, xv, gate, W["od_lnx_g"], W["od_lnx_b"] + tok, rk, bm)
    dr, dw, dk, dv, da, db = _wkv_bwd(sprev, sab, vbb, decay, k2, a_s, b_s, xr, dysc)
    prep_grads = _rwkv_prep_bwd(pr, W["od_mu"], prep_params, bm,
                                [[dw], [dk, dk2_p], [da], [db], [dgate], [dr, dxr_p], [dv, dxv_p]])
    dxs, g_mu = prep_grads[0], prep_grads[1]
    dpr = _shift_bwd(dxs, W["od_mu"])
    dop = jnp.concatenate([jnp.zeros((ATT_PAD, D_ATT), f32), dy1[:, :D_ATT]], axis=0).astype(bf16)
    dqp, dkp, dvp, dsk = _attn_bwd(qp, kp, vp, W["od_sinks"], dop)
    dpatt = _rope_bwd(dqp, dkp, dvp, c64, s64)
    g_od_w_in_t = jnp.concatenate([_mm(dpatt, hn1, "tn", "od_gin_att", out_dtype=GRAD_WIRE_DTYPE),
                                   _mm(dpr, hn1, "tn", "od_gin_rwkv", out_dtype=GRAD_WIRE_DTYPE)], axis=0)
    tok = put_g("od_in", g_od_w_in_t)
    dhn1 = _mm(dpr, w_rwkv, "nn", "od_dhn_rwkv", res=_mm(dpatt, w_att, "nn", "od_dhn_att"))
    dh2, g_norm_mix1 = _rms_bwd(dhn1, h2, W["norm_mix"][1] + tok, dh3, "mix1_norm_bwd")

    dh1, gf0 = _ffn_bwd(dh2, h1, ffn0, W["norm_ffn"][0], ff0_up_t, W["ff_conv"][0], W["ff_conv_b"][0], ff0_down, 0, put_g)
    early = dict(
        norm_ffn=jnp.concatenate([gf0["norm"], gf1["norm"]], axis=0), norm_final=g_norm_final.reshape(D_MODEL),
        od_sinks=dsk[:, :N_Q_HEADS], od_mu=g_mu, od_lnx_g=g_lnx_g, od_lnx_b=g_lnx_b, od_r_k=g_rk.reshape(N_Q_HEADS, HEAD_DIM),
        ff_conv=jnp.stack([gf0["conv"], gf1["conv"]]), ff_conv_b=jnp.concatenate([gf0["conv_b"], gf1["conv_b"]], axis=0),
        **dict(zip(_PREP_PARAMS, prep_grads[2:])))
    dy0 = _mm(dh1, ev_w_out, "nt", "ev_dy")
    g_ev_w_out = _mm(y0, dh1, "tn", "ev_gout", out_dtype=GRAD_WIRE_DTYPE)
    tok = put_g("ev_out", g_ev_w_out) + put_small(early)
    duc, g_ln_g, g_ln_b = _ev_a_norm_bwd(dy0, uc, W["ev_ln_a_g"], W["ev_ln_a_b"] + tok)
    dav, dag, g_conv_a = _ev_a_conv_bwd(duc, p0, W["ev_conv_a"])
    dgb, dgc, dxi, g_conv_b = _ev_b_bwd(dy0, p0, W["ev_conv_b"])
    dp0 = jnp.concatenate([dav, dag, dgb, dgc, dxi], axis=1)
    g_ev_w_in_t = _mm(dp0, hn0, "tn", "ev_gin", out_dtype=GRAD_WIRE_DTYPE)
    tok = put_g("ev_in", g_ev_w_in_t)
    dhn0 = _mm(dp0, ev_w_in_t, "nn", "ev_dhn")
    dh0, g_norm_mix0 = _rms_bwd(dhn0, h0, W["norm_mix"][0] + tok, dh1, "mix0_norm_bwd")

    late = dict(meta_tokens=dh0[:N_META], norm_mix=jnp.concatenate([g_norm_mix0, g_norm_mix1], axis=0),
                ev_conv_a=g_conv_a, ev_ln_a_g=g_ln_g, ev_ln_a_b=g_ln_b, ev_conv_b=g_conv_b)
    return loss, dh0[N_META:], late


HBM = pl.BlockSpec(memory_space=pl.ANY)


def _mesh_pos():
    return lax.axis_index("x"), lax.axis_index("y"), lax.axis_index("c")


def _dev(px, py, pc):
    return 4 * px + 2 * py + pc


def _all_gather(xs, name):
    n = len(xs)

    def body(*refs):
        x_refs, o_refs = refs[:n], refs[n:2 * n]
        send_sems, recv_sems, local_sems = refs[2 * n:]
        x, y, c = _mesh_pos()
        me, sibling = (x, y, c), (x, y, 1 - c)
        chips = [(1 - x, y), (x, 1 - y), (1 - x, 1 - y)]

        def copy(i, k, block, to, from_input=False):
            dst = o_refs[i].at[_dev(*block)]
            return pltpu.make_async_remote_copy(src_ref=x_refs[i] if from_input else dst, dst_ref=dst,
                                                send_sem=send_sems.at[i, k], recv_sem=recv_sems.at[i, k],
                                                device_id=to, device_id_type=MESH)

        mine = [pltpu.make_async_copy(x_refs[i], o_refs[i].at[_dev(*me)], local_sems.at[i]) for i in range(n)]
        for cp in mine:
            cp.start()
        first = []
        for i in range(n):
            first.append(copy(i, 0, me, sibling, True))
            first += [copy(i, 1 + j, me, (*chip, c), True) for j, chip in enumerate(chips)]
        for cp in first:
            cp.start()
        passed = []
        for j, chip in enumerate(chips):
            for i in range(n):
                copy(i, 1 + j, (*chip, c), me).wait_recv()
                fwd = copy(i, 4 + j, (*chip, c), sibling)
                fwd.start()
                passed.append(fwd)
        for i in range(n):
            copy(i, 0, sibling, me).wait_recv()
            for j, chip in enumerate(chips):
                copy(i, 4 + j, (*chip, 1 - c), me).wait_recv()
        for cp in first + passed:
            cp.wait_send()
        for cp in mine:
            cp.wait()

    return _pc(body, name=name, in_specs=[HBM] * n, out_specs=tuple([HBM] * n),
               out_shape=tuple(S((N_DEV,) + x.shape, x.dtype) for x in xs),
               scratch_shapes=[pltpu.SemaphoreType.DMA((n, 7)), pltpu.SemaphoreType.DMA((n, 7)),
                               pltpu.SemaphoreType.DMA((n,))])(*xs)


HBM_SPEC = pl.BlockSpec(memory_space=pltpu.HBM)
SEM_SPEC = pl.BlockSpec(memory_space=pltpu.SEMAPHORE)
DATAFLOW = pltpu.SideEffectType.DATAFLOW_SIDE_EFFECTING
_PEER_FLIPS = ((1, 0, 0), (0, 1, 0), (1, 1, 0), (1, 0, 1), (0, 1, 1), (1, 1, 1), (0, 0, 1))
N_PEERS = len(_PEER_FLIPS)


def _peers(x, y, c):
    return [((1 - x) if fx else x, (1 - y) if fy else y, (1 - c) if fc else c) for fx, fy, fc in _PEER_FLIPS]


def _xchg_start(srcs, lands, scatter, name):
    n = len(srcs)

    def body(*refs):
        src_refs, land_refs = refs[:n], refs[n:2 * n]
        send_sems, recv_sems, token = refs[2 * n], refs[2 * n + 1], refs[-1]
        x, y, c = _mesh_pos()
        me = _dev(x, y, c)
        for i in range(n):
            for k, peer in enumerate(_peers(x, y, c)):
                pltpu.make_async_remote_copy(src_ref=src_refs[i].at[_dev(*peer)] if scatter else src_refs[i],
                                             dst_ref=land_refs[i].at[me], send_sem=send_sems.at[i * N_PEERS + k],
                                             recv_sem=recv_sems.at[i * N_PEERS + k], device_id=peer, device_id_type=MESH).start()
        token[...] = jnp.zeros_like(token)

    arrs = list(srcs) + list(lands)
    outs = _pc(body, name=name,
               out_shape=(pltpu.SemaphoreType.DMA((n * N_PEERS,)), pltpu.SemaphoreType.DMA((n * N_PEERS,)),
                          *[pltpu.HBM(a.shape, a.dtype) for a in arrs], S((SUBLANES, LANES), f32)),
               in_specs=[HBM_SPEC] * (2 * n),
               out_specs=(SEM_SPEC, SEM_SPEC, *[HBM_SPEC] * (2 * n), pl.BlockSpec(memory_space=pltpu.VMEM)),
               input_output_aliases={i: 2 + i for i in range(2 * n)},
               compiler_params=pltpu.CompilerParams(has_side_effects=DATAFLOW))(
        *[pltpu.with_memory_space_constraint(a, pltpu.HBM) for a in arrs])
    return (outs[0], outs[1], list(outs[2:2 + n]), list(outs[2 + n:2 + 2 * n]), scatter), outs[-1]


def _xchg_wait(handle, after, name):
    send_sems, recv_sems, srcs, lands, scatter = handle
    n = len(srcs)

    def body(*refs):
        src_refs, land_refs = refs[:n], refs[n:2 * n]
        send, recv = refs[2 * n], refs[2 * n + 1]
        x, y, c = _mesh_pos()
        for i in range(n):
            for k in range(N_PEERS):
                cp = pltpu.make_async_remote_copy(src_ref=src_refs[i].at[0] if scatter else src_refs[i],
                                                  dst_ref=land_refs[i].at[0], send_sem=send.at[i * N_PEERS + k],
                                                  recv_sem=recv.at[i * N_PEERS + k],
                                                  device_id=(x, y, c), device_id_type=MESH)
                cp.wait_send()
                cp.wait_recv()

    arrs = srcs + lands
    outs = _pc(body, name=name, out_shape=tuple(pltpu.HBM(a.shape, a.dtype) for a in arrs),
               in_specs=[HBM_SPEC] * (2 * n) + [SEM_SPEC, SEM_SPEC, pl.BlockSpec(memory_space=pl.ANY)],
               out_specs=tuple([HBM_SPEC] * (2 * n)), input_output_aliases={i: i for i in range(2 * n)},
               compiler_params=pltpu.CompilerParams(has_side_effects=DATAFLOW))(*arrs, send_sems, recv_sems, after)
    return list(outs[:n]), list(outs[n:])


def _rs_sum(g, land, me_vec, name):
    _, r, cols = g.shape
    tr = _divisor_block(r, 16, min(r, 352))

    def body(me_ref, g_ref, *rest):
        o_ref = rest[-1]
        acc = g_ref[0].astype(f32)
        for l_ref in rest[:-1]:
            acc = acc + l_ref[0].astype(f32)
        o_ref[...] = acc

    blk = lambda f: pl.BlockSpec((1, tr, cols), f)
    grid_spec = pltpu.PrefetchScalarGridSpec(
        num_scalar_prefetch=1, grid=(r // tr,),
        in_specs=[blk(lambda i, me_ref: (me_ref[0], i, 0))]
        + [blk(lambda i, me_ref, k=k: ((me_ref[0] + k) % N_DEV, i, 0)) for k in range(1, N_DEV)],
        out_specs=pl.BlockSpec((tr, cols), lambda i, me_ref: (i, 0)))
    return _pc(body, name=name, grid_spec=grid_spec, out_shape=S((r, cols), f32),
               compiler_params=_cparams(("arbitrary",)))(me_vec, g, *([land] * (N_DEV - 1)))


def _sum_devices(a, name):
    def body(a_ref, o_ref):
        acc = a_ref[0]
        for d in range(1, N_DEV):
            acc = acc + a_ref[d]
        o_ref[...] = acc

    return _pc(body, name=name, grid=(1,), in_specs=[_full(a.shape)], out_specs=_full(a.shape[1:]),
               out_shape=S(a.shape[1:], a.dtype), compiler_params=_cparams(("arbitrary",)))(a)


def _adamw(w, m, v, g, name):
    shape = w.shape
    w2, m2, v2, g2 = (a.reshape(-1, shape[-1]) for a in (w, m, v, g))
    rows, cols = w2.shape
    tr = rows if rows % SUBLANES else _divisor_block(rows, SUBLANES, max(SUBLANES, min(rows, ADAMW_BLOCK_ELEMS // cols)))
    c1, c2 = 1.0 - ADAM_B1 ** ADAM_STEP, 1.0 - ADAM_B2 ** ADAM_STEP

    def body(w_ref, m_ref, v_ref, g_ref, d_ref, nm_ref, nv_ref):
        gv = g_ref[...]
        nm = ADAM_B1 * m_ref[...] + (1.0 - ADAM_B1) * gv
        nv = ADAM_B2 * v_ref[...] + (1.0 - ADAM_B2) * (gv * gv)
        d_ref[...] = -ADAM_LR * ((nm / c1) / (jnp.sqrt(nv / c2) + ADAM_EPS) + ADAM_WD * w_ref[...])
        nm_ref[...] = nm
        nv_ref[...] = nv

    blk = pl.BlockSpec((tr, cols), lambda i: (i, 0))
    outs = _pc(body, name=name, grid=(rows // tr,), in_specs=[blk] * 4, out_specs=(blk,) * 3,
               out_shape=(S((rows, cols), f32),) * 3, compiler_params=_cparams(("arbitrary",)))(w2, m2, v2, g2)
    return tuple(o.reshape(shape) for o in outs)


_WEIGHTS = ("meta_tokens", "norm_mix", "norm_ffn", "norm_final", "ev_w_in", "ev_conv_a", "ev_ln_a_g", "ev_ln_a_b",
            "ev_conv_b", "ev_w_out", "od_w_in", "od_sinks", "od_mu", "od_w0", "od_w2", "od_a0", "od_a2", "od_g2",
            "od_k_k", "od_k_a", "od_r_k", "od_lnx_g", "od_lnx_b", "od_w_out", "ff_w_up", "ff_conv", "ff_conv_b", "ff_w_down")
_SMALL_SHARDED = (("meta_tokens", 1), ("ev_conv_a", 2), ("ev_conv_b", 2), ("od_mu", 1), ("od_w0", 1), ("od_w2", 2),
                  ("od_a0", 1), ("od_a2", 2), ("od_g2", 2), ("od_k_k", 1), ("od_k_a", 1), ("od_lnx_g", 1),
                  ("od_lnx_b", 1), ("ff_conv", 2))
_SMALL_REPLICATED = ("norm_mix", "norm_ffn", "norm_final", "ev_ln_a_g", "ev_ln_a_b", "od_sinks", "od_r_k", "ff_conv_b")
SLAB_UNIT = SUBLANES * LANES


def _pack(arrs):
    flat = jnp.concatenate([a.reshape(-1).astype(f32) for a in arrs])
    pad = (-flat.shape[0]) % SLAB_UNIT
    return jnp.pad(flat, (0, pad)).reshape(-1, LANES)


def _unpack(flat, shapes):
    out, off = [], 0
    for shp in shapes:
        size = 1
        for s in shp:
            size *= s
        out.append(flat[..., off:off + size].reshape(flat.shape[:-1] + tuple(shp)))
        off += size
    return out


def _full_shape(shape, axis):
    return tuple(N_DEV * s if i == axis else s for i, s in enumerate(shape))


def kernel(x, meta_tokens, norm_mix, norm_ffn, norm_final, ev_w_in, ev_conv_a, ev_ln_a_g, ev_ln_a_b, ev_conv_b, ev_w_out, od_w_in, od_sinks, od_mu, od_w0, od_w2, od_a0, od_a2, od_g2, od_k_k, od_k_a, od_r_k, od_lnx_g, od_lnx_b, od_w_out, ff_w_up, ff_conv, ff_conv_b, ff_w_down, loss_target, m_meta_tokens, m_norm_mix, m_norm_ffn, m_norm_final, m_ev_w_in, m_ev_conv_a, m_ev_ln_a_g, m_ev_ln_a_b, m_ev_conv_b, m_ev_w_out, m_od_w_in, m_od_sinks, m_od_mu, m_od_w0, m_od_w2, m_od_a0, m_od_a2, m_od_g2, m_od_k_k, m_od_k_a, m_od_r_k, m_od_lnx_g, m_od_lnx_b, m_od_w_out, m_ff_w_up, m_ff_conv, m_ff_conv_b, m_ff_w_down, v_meta_tokens, v_norm_mix, v_norm_ffn, v_norm_final, v_ev_w_in, v_ev_conv_a, v_ev_ln_a_g, v_ev_ln_a_b, v_ev_conv_b, v_ev_w_out, v_od_w_in, v_od_sinks, v_od_mu, v_od_w0, v_od_w2, v_od_a0, v_od_a2, v_od_g2, v_od_k_k, v_od_k_a, v_od_r_k, v_od_lnx_g, v_od_lnx_b, v_od_w_out, v_ff_w_up, v_ff_conv, v_ff_conv_b, v_ff_w_down):
    A = dict(locals())
    px, py, pc = _mesh_pos()
    me = _dev(px, py, pc)
    me_vec = jnp.reshape(me, (1,)).astype(jnp.int32)
    rows = lambda a: a.reshape(N_DEV * a.shape[1], a.shape[2])
    blocks = lambda a: a.reshape(N_DEV, a.shape[0] // N_DEV, a.shape[1])

    shards = dict(ev_in=ev_w_in[0].T, ev_out=ev_w_out[0], ff0_up=ff_w_up[0].T, ff0_down=ff_w_down[0], od_in=od_w_in[0].T,
                  od_out=od_w_out[0], ff1_up=ff_w_up[1].T, ff1_down=ff_w_down[1])
    shards = {n: b.astype(bf16) for n, b in shards.items()}
    small_shapes = [A[n].shape for n, _ in _SMALL_SHARDED]
    gathered = _all_gather([shards["ev_in"], shards["ev_out"], _pack([A[n] for n, _ in _SMALL_SHARDED])], "gather_first")
    gathered, shards = lax.optimization_barrier((gathered, shards))
    fetch, tok0 = {}, jnp.zeros((), f32)
    for n in ("ff0_up", "ff0_down", "od_in", "od_out", "ff1_up", "ff1_down"):
        shard, tok0 = lax.optimization_barrier((shards[n], tok0))
        land = lax.dynamic_update_slice(lax.empty((N_DEV,) + shard.shape, bf16), shard[None], (me, 0, 0))
        fetch[n], token = _xchg_start([shard], [land], False, f"gather_{n}_start")
        tok0 = tok0 + token[0, 0]

    def get_w(n, after):
        if n in ("ev_in", "ev_out"):
            return rows(gathered[("ev_in", "ev_out").index(n)])
        return rows(_xchg_wait(fetch[n], after, f"gather_{n}_wait")[1][0])

    W = {}
    for (n, ax), seg in zip(_SMALL_SHARDED, _unpack(gathered[-1].reshape(N_DEV, -1), small_shapes)):
        W[n] = jnp.moveaxis(seg, 0, ax).reshape(_full_shape(A[n].shape, ax))
    for n in ("ev_conv_a", "ev_conv_b", "od_w2", "od_a2", "od_g2"):
        W[n] = W[n][0]
    for n in _SMALL_REPLICATED:
        W[n] = A[n]
    W["od_r_k"] = od_r_k[0]

    small_shape = {n: _full_shape(A[n].shape, ax) for n, ax in _SMALL_SHARDED}
    small_shape.update({n: A[n].shape for n in _SMALL_REPLICATED})
    sent, small_sent, small_names = {}, {}, {}

    def put_g(n, g):
        g8 = blocks(g)
        sent[n], token = _xchg_start([g8], [lax.empty(g8.shape, g8.dtype)], True, f"reduce_{n}_start")
        return token[0, 0]

    def put_small(gs, stage="early"):
        small_names[stage] = sorted(gs)
        slab = _pack([gs[n] for n in small_names[stage]])
        land = lax.dynamic_update_slice(lax.empty((N_DEV,) + slab.shape, f32), slab[None], (me, 0, 0))
        small_sent[stage], small_tok[stage] = _xchg_start([slab], [land], False, f"gather_{stage}_small_grads_start")
        return small_tok[stage][0, 0]

    small_tok = {}
    loss_tile, grad_x, late = _local_step(x[0], loss_target[0], W, get_w, put_g, put_small, tok0)
    put_small(late, "late")
    late_tok = small_tok["late"]

    gsh, prev = {}, late_tok
    for n in ("ff1_down", "ff1_up", "od_out", "od_in", "ff0_down", "ff0_up", "ev_out", "ev_in"):
        srcs, lands = _xchg_wait(sent[n], prev, f"reduce_{n}_wait")
        gsh[n] = prev = _rs_sum(srcs[0], lands[0], me_vec, f"reduce_{n}_sum")
    grads = dict(ev_w_in=gsh["ev_in"].T[None], ev_w_out=gsh["ev_out"][None], od_w_in=gsh["od_in"].T[None],
                 od_w_out=gsh["od_out"][None], ff_w_up=jnp.stack([gsh["ff0_up"].T, gsh["ff1_up"].T]),
                 ff_w_down=jnp.stack([gsh["ff0_down"], gsh["ff1_down"]]))

    delta, new_m, new_v = {}, {}, {}
    for n in ("ff_w_up", "ff_w_down", "od_w_in", "od_w_out", "ev_w_in", "ev_w_out"):
        delta[n], new_m[n], new_v[n] = _adamw(A[n], A["m_" + n], A["v_" + n], grads[n], "adamw_" + n)
    for stage in ("early", "late"):
        gsm = _xchg_wait(small_sent[stage], delta["ev_w_in"], f"gather_{stage}_small_grads_wait")[1][0]
        summed = _sum_devices(gsm, f"sum_{stage}_small_grads").reshape(-1)
        for n, full in zip(small_names[stage], _unpack(summed, [small_shape[n] for n in small_names[stage]])):
            grads[n] = full
    for n, ax in _SMALL_SHARDED:
        size = A[n].shape[ax]
        grads[n] = lax.dynamic_slice_in_dim(grads[n], me * size, size, axis=ax)
    for n in small_shape:
        delta[n], new_m[n], new_v[n] = _adamw(A[n], A["m_" + n], A["v_" + n], grads[n], "adamw_" + n)

    loss = lax.psum(loss_tile[0, 0], ("x", "y", "c"))
    return (loss, grad_x[None], *[grads[n] for n in _WEIGHTS], *[delta[n] for n in _WEIGHTS],
            *[new_m[n] for n in _WEIGHTS], *[new_v[n] for n in _WEIGHTS])
```

```python
import jax
import jax.numpy as jnp
from jax import lax
from jax.experimental import pallas as pl
from jax.experimental.pallas import tpu as pltpu

f32, bf16 = jnp.float32, jnp.bfloat16

D_MODEL = 1024
N_META = 16
RMS_EPS = 1e-6
LN_EPS = 1e-5
D_A = 512
CONV_A_WIDTH = 31
CONV_B_WIDTH = 3
HEAD_DIM = 64
N_Q_HEADS = 8
N_KV_HEADS = 2
GQA_GROUP = 4
D_ATT = 512
D_KV = 128
BLOCK = 128
ROPE_THETA = 10000.0
D_R = 512
LORA_W, LORA_A, LORA_G = 64, 64, 128
RWKV_GN_EPS = 64e-5
ATT_COLS = D_ATT + 2 * D_KV
RWKV_COLS = 3 * D_R + LORA_W + LORA_A + LORA_G
D_FF = 2816
FF_CONV_WIDTH = 3
NEG_INF = -1e30
ATT_PAD = BLOCK - N_META
ATT_SCALE = HEAD_DIM ** -0.5

ADAM_LR, ADAM_B1, ADAM_B2, ADAM_EPS, ADAM_WD, ADAM_STEP = 0.001, 0.9, 0.999, 1e-08, 0.01, 10

N_DEV = 8
LANES = 128
SUBLANES = 8
SCAN_CHUNK = 48
PAIR_ROWS = 4 * HEAD_DIM
V7X_VMEM_LIMIT = 56 * 1024 * 1024
ADAMW_BLOCK_ELEMS = 400 * 1024
GRAD_WIRE_DTYPE = bf16
MESH = pl.DeviceIdType.MESH
S = jax.ShapeDtypeStruct
HIGHEST = lax.Precision.HIGHEST


def _pc(body, **kw):
    return pl.pallas_call(body, **kw)


def _cparams(sem=None):
    return pltpu.CompilerParams(dimension_semantics=sem, vmem_limit_bytes=V7X_VMEM_LIMIT)


def _divisor_block(t, unit, limit):
    best = unit
    for rb in range(unit, limit + 1, unit):
        if t % rb == 0:
            best = rb
    assert t % best == 0, (t, unit)
    return best


def _row_block(t):
    return _divisor_block(t, 16, 704)


def _row_block8(t):
    return _divisor_block(t, 8, 344)


def _col_tile(n, cap):
    return _divisor_block(n, LANES, min(n, cap)) if n % LANES == 0 else n


def _full(shape):
    nd = len(shape)
    return pl.BlockSpec(shape, lambda *_: (0,) * nd)


def _sigmoid(x):
    return jax.nn.sigmoid(x)


_DIMS = {"nn": (((1,), (0,)), ((), ())), "nt": (((1,), (1,)), ((), ())), "tn": (((0,), (0,)), ((), ()))}
MM_MAX_K = 2816
MM_MAX_TM = 704
MM_MAX_TN = 1408


def _mm(a, b, mode, name, out_dtype=f32, res=None):
    if mode == "nn":
        (m, k), (k2, n) = a.shape, b.shape
    elif mode == "nt":
        (m, k), (n, k2) = a.shape, b.shape
    else:
        (k, m), (k2, n) = a.shape, b.shape
    assert k == k2, (a.shape, b.shape, mode)
    tm = _row_block(m) if m % LANES else _col_tile(m, MM_MAX_TM)
    tn = _col_tile(n, MM_MAX_TN)
    nk = 1 if (mode == "tn" or k <= MM_MAX_K) else k // MM_MAX_K
    tk = k // nk
    assert tk * nk == k
    dims = _DIMS[mode]

    def body(a_ref, b_ref, *rest):
        part = lax.dot_general(a_ref[...].astype(bf16), b_ref[...].astype(bf16), dims, preferred_element_type=f32)
        if nk == 1:
            o_ref = rest[-1]
            if res is not None:
                part = part + rest[0][...]
            o_ref[...] = part.astype(out_dtype)
            return
        o_ref, acc_ref = rest[-2], rest[-1]
        kk = pl.program_id(2)

        @pl.when(kk == 0)
        def _():
            acc_ref[...] = part

        @pl.when(kk > 0)
        def _():
            acc_ref[...] += part

        @pl.when(kk == nk - 1)
        def _():
            acc = acc_ref[...]
            if res is not None:
                acc = acc + rest[0][...]
            o_ref[...] = acc.astype(out_dtype)

    if mode == "tn":
        a_spec = pl.BlockSpec((k, tm), lambda i, j, kk: (0, i))
    else:
        a_spec = pl.BlockSpec((tm, tk), lambda i, j, kk: (i, kk))
    if mode == "nt":
        b_spec = pl.BlockSpec((tn, tk), lambda i, j, kk: (j, kk))
    else:
        b_spec = pl.BlockSpec((tk, tn), lambda i, j, kk: (kk, j))
    o_spec = pl.BlockSpec((tm, tn), lambda i, j, kk: (i, j))
    ins, specs = [a, b], [a_spec, b_spec]
    if res is not None:
        ins.append(res)
        specs.append(o_spec)
    scratch = [pltpu.VMEM((tm, tn), f32)] if nk > 1 else []
    return _pc(body, name=name, grid=(m // tm, n // tn, nk), in_specs=specs, out_specs=o_spec,
               out_shape=S((m, n), out_dtype), scratch_shapes=scratch,
               compiler_params=_cparams(("arbitrary", "arbitrary", "arbitrary")))(*ins)


def _rms_fwd(x, g, name):
    t, d = x.shape
    rb = _row_block(t)

    def body(x_ref, g_ref, o_ref):
        xv = x_ref[...]
        rstd = lax.rsqrt(jnp.mean(xv * xv, axis=-1, keepdims=True) + RMS_EPS)
        o_ref[...] = (xv * rstd * g_ref[...]).astype(bf16)

    row = pl.BlockSpec((rb, d), lambda i: (i, 0))
    return _pc(body, name=name, grid=(t // rb,), in_specs=[row, _full((1, d))], out_specs=row,
               out_shape=S((t, d), bf16), compiler_params=_cparams(("arbitrary",)))(x, g.reshape(1, d))


def _rms_bwd(dy, x, g, dres, name):
    t, d = x.shape
    rb = _row_block8(t)

    def body(dy_ref, x_ref, g_ref, dres_ref, dx_ref, dg_ref):
        @pl.when(pl.program_id(0) == 0)
        def _():
            dg_ref[...] = jnp.zeros_like(dg_ref)
        xv, dyv = x_ref[...], dy_ref[...]
        rstd = lax.rsqrt(jnp.mean(xv * xv, axis=-1, keepdims=True) + RMS_EPS)
        xn = xv * rstd
        dg_ref[...] += jnp.sum(dyv * xn, axis=0, keepdims=True)
        dxh = dyv * g_ref[...]
        dx_ref[...] = dres_ref[...] + rstd * (dxh - xn * jnp.mean(dxh * xn, axis=-1, keepdims=True))

    row = pl.BlockSpec((rb, d), lambda i: (i, 0))
    return _pc(body, name=name, grid=(t // rb,), in_specs=[row, row, _full((1, d)), row],
               out_specs=(row, _full((1, d))), out_shape=(S((t, d), f32), S((1, d), f32)),
               compiler_params=_cparams(("arbitrary",)))(dy, x, g.reshape(1, d), dres)


def _final_loss(h, g, target_padded):
    t, d = h.shape
    rb = _row_block8(t)

    def body(x_ref, g_ref, t_ref, loss_ref, dx_ref, dg_ref):
        i = pl.program_id(0)

        @pl.when(i == 0)
        def _():
            dg_ref[...] = jnp.zeros_like(dg_ref)
            loss_ref[...] = jnp.zeros_like(loss_ref)
        xv = x_ref[...]
        rstd = lax.rsqrt(jnp.mean(xv * xv, axis=-1, keepdims=True) + RMS_EPS)
        xn = xv * rstd
        gv = g_ref[...]
        row = i * rb + lax.broadcasted_iota(jnp.int32, (rb, 1), 0)
        diff = jnp.where(row >= N_META, xn * gv - t_ref[...], 0.0)
        loss_ref[...] += 0.5 * jnp.sum(jnp.mean(diff * diff, axis=-1, keepdims=True))
        dout = diff * (1.0 / d)
        dg_ref[...] += jnp.sum(dout * xn, axis=0, keepdims=True)
        dxh = dout * gv
        dx_ref[...] = rstd * (dxh - xn * jnp.mean(dxh * xn, axis=-1, keepdims=True))

    row = pl.BlockSpec((rb, d), lambda i: (i, 0))
    return _pc(body, name="final_loss", grid=(t // rb,), in_specs=[row, _full((1, d)), row],
               out_specs=(_full((SUBLANES, LANES)), row, _full((1, d))),
               out_shape=(S((SUBLANES, LANES), f32), S((t, d), f32), S((1, d), f32)),
               compiler_params=_cparams(("arbitrary",)))(h, g.reshape(1, d), target_padded)


CONV_LEAD = 32


def _fill_front_padded(pad_ref, x, t):
    pad_ref[0:CONV_LEAD, :] = jnp.zeros((CONV_LEAD, x.shape[1]), f32)
    pad_ref[CONV_LEAD:CONV_LEAD + t, :] = x


def _fill_back_padded(pad_ref, x, t):
    pad_ref[0:t, :] = x
    pad_ref[t:t + CONV_LEAD, :] = jnp.zeros((CONV_LEAD, x.shape[1]), f32)


def _conv_rows(pad_ref, w_ref, kw, r0, nr):
    acc = None
    for j in range(kw):
        lo = CONV_LEAD + r0 - (kw - 1) + j
        term = w_ref[j:j + 1, :] * pad_ref[lo:lo + nr, :]
        acc = term if acc is None else acc + term
    return acc


def _conv_t_rows(padb_ref, w_ref, kw, r0, nr):
    acc = None
    for j in range(kw):
        lo = r0 + (kw - 1) - j
        term = w_ref[j:j + 1, :] * padb_ref[lo:lo + nr, :]
        acc = term if acc is None else acc + term
    return acc


def _conv_dw_rows(dy_blk, pad_ref, kw, r0, nr):
    out = []
    for j in range(kw):
        lo = CONV_LEAD + r0 - (kw - 1) + j
        out.append(jnp.sum(dy_blk * pad_ref[lo:lo + nr, :], axis=0, keepdims=True))
    return out


def _acc_list(a, b):
    return b if a is None else [x + y for x, y in zip(a, b)]


def _ev_a_conv(p, conv_a):
    t = p.shape[0]
    cr = _row_block8(t)
    nb = D_A // LANES

    def body(av_ref, ag_ref, w_ref, o_ref, pad_ref):
        _fill_front_padded(pad_ref, av_ref[...] * _sigmoid(ag_ref[...]), t)
        for r in range(t // cr):
            o_ref[r * cr:(r + 1) * cr, :] = _conv_rows(pad_ref, w_ref, CONV_A_WIDTH, r * cr, cr)

    col = lambda off: pl.BlockSpec((t, LANES), lambda j: (0, j + off))
    return _pc(body, name="ev_a_conv", grid=(nb,),
               in_specs=[col(0), col(nb), pl.BlockSpec((CONV_A_WIDTH, LANES), lambda j: (0, j))],
               out_specs=col(0), out_shape=S((t, D_A), f32),
               scratch_shapes=[pltpu.VMEM((t + CONV_LEAD, LANES), f32)],
               compiler_params=_cparams(("arbitrary",)))(p, p, conv_a)


def _ln_silu(uc, g, b):
    mu = jnp.mean(uc, axis=-1, keepdims=True)
    xc = uc - mu
    var = jnp.mean(xc * xc, axis=-1, keepdims=True)
    y = xc * lax.rsqrt(var + LN_EPS) * g + b
    return y * _sigmoid(y)


def _ev_a_norm(uc, g, b):
    t, d = uc.shape
    rb = _row_block(t)

    def body(u_ref, g_ref, b_ref, o_ref):
        o_ref[...] = _ln_silu(u_ref[...], g_ref[...], b_ref[...]).astype(bf16)

    row = pl.BlockSpec((rb, d), lambda i: (i, 0))
    return _pc(body, name="ev_a_norm", grid=(t // rb,), in_specs=[row, _full((1, d)), _full((1, d))],
               out_specs=row, out_shape=S((t, 2 * d), bf16), compiler_params=_cparams(("arbitrary",)))(uc, g, b)


def _ev_a_norm_bwd(dy, uc, g, b):
    t, d = uc.shape
    rb = _row_block8(t)

    def body(dy_ref, u_ref, g_ref, b_ref, du_ref, dg_ref, db_ref):
        @pl.when(pl.program_id(0) == 0)
        def _():
            dg_ref[...] = jnp.zeros_like(dg_ref)
            db_ref[...] = jnp.zeros_like(db_ref)
        _, vjp = jax.vjp(_ln_silu, u_ref[...], g_ref[...], b_ref[...])
        du, dg, db = vjp(dy_ref[...])
        du_ref[...] = du
        dg_ref[...] += dg
        db_ref[...] += db

    row = pl.BlockSpec((rb, d), lambda i: (i, 0))
    return _pc(body, name="ev_a_norm_bwd", grid=(t // rb,), in_specs=[row, row, _full((1, d)), _full((1, d))],
               out_specs=(row, _full((1, d)), _full((1, d))),
               out_shape=(S((t, d), f32), S((1, d), f32), S((1, d), f32)),
               compiler_params=_cparams(("arbitrary",)))(dy, uc, g, b)


def _ev_a_conv_bwd(duc, p, conv_a):
    t = p.shape[0]
    cr = _row_block8(t)
    nb = D_A // LANES

    def body(dy_ref, av_ref, ag_ref, w_ref, dav_ref, dag_ref, dw_ref, pad_ref, padb_ref):
        _fill_front_padded(pad_ref, av_ref[...] * _sigmoid(ag_ref[...]), t)
        _fill_back_padded(padb_ref, dy_ref[...], t)
        dw = None
        for r in range(t // cr):
            rows = slice(r * cr, (r + 1) * cr)
            du = _conv_t_rows(padb_ref, w_ref, CONV_A_WIDTH, r * cr, cr)
            avr = av_ref[rows, :]
            sgr = _sigmoid(ag_ref[rows, :])
            dav_ref[rows, :] = du * sgr
            dag_ref[rows, :] = du * avr * sgr * (1.0 - sgr)
            dw = _acc_list(dw, _conv_dw_rows(dy_ref[rows, :], pad_ref, CONV_A_WIDTH, r * cr, cr))
        for j in range(CONV_A_WIDTH):
            dw_ref[j:j + 1, :] = dw[j]

    col = lambda off: pl.BlockSpec((t, LANES), lambda j: (0, j + off))
    wsp = pl.BlockSpec((CONV_A_WIDTH, LANES), lambda j: (0, j))
    return _pc(body, name="ev_a_conv_bwd", grid=(nb,), in_specs=[col(0), col(0), col(nb), wsp],
               out_specs=(col(0), col(0), wsp),
               out_shape=(S((t, D_A), f32), S((t, D_A), f32), S((CONV_A_WIDTH, D_A), f32)),
               scratch_shapes=[pltpu.VMEM((t + CONV_LEAD, LANES), f32), pltpu.VMEM((t + CONV_LEAD, LANES), f32)],
               compiler_params=_cparams(("arbitrary",)))(duc, p, p, conv_a)


def _ev_b(p, conv_b, y):
    t = p.shape[0]
    cr = _row_block8(t)
    nb = D_A // LANES

    def body(gb_ref, gc_ref, xi_ref, w_ref, y_ref, o_ref, pad_ref, stage_ref):
        _fill_front_padded(pad_ref, gc_ref[...] * xi_ref[...], t)
        for r in range(t // cr):
            rows = slice(r * cr, (r + 1) * cr)
            stage_ref[rows, :] = gb_ref[rows, :] * _conv_rows(pad_ref, w_ref, CONV_B_WIDTH, r * cr, cr)
        o_ref[...] = stage_ref[...].astype(bf16)

    col = lambda off: pl.BlockSpec((t, LANES), lambda j: (0, j + off))
    return _pc(body, name="ev_b", grid=(nb,),
               in_specs=[col(2 * nb), col(3 * nb), col(4 * nb), pl.BlockSpec((CONV_B_WIDTH, LANES), lambda j: (0, j)), HBM],
               out_specs=col(nb), out_shape=S(y.shape, bf16), input_output_aliases={4: 0},
               scratch_shapes=[pltpu.VMEM((t + CONV_LEAD, LANES), f32), pltpu.VMEM((t, LANES), f32)],
               compiler_params=_cparams(("arbitrary",)))(p, p, p, conv_b, y)


def _ev_b_bwd(dy, p, conv_b):
    t = p.shape[0]
    cr = _row_block8(t)
    nb = D_A // LANES

    def body(dy_ref, gb_ref, gc_ref, xi_ref, w_ref, dgb_ref, dgc_ref, dxi_ref, dw_ref, pad_ref, padb_ref):
        _fill_front_padded(pad_ref, gc_ref[...] * xi_ref[...], t)
        _fill_back_padded(padb_ref, dy_ref[...] * gb_ref[...], t)
        dw = None
        for r in range(t // cr):
            rows = slice(r * cr, (r + 1) * cr)
            dgb_ref[rows, :] = dy_ref[rows, :] * _conv_rows(pad_ref, w_ref, CONV_B_WIDTH, r * cr, cr)
            dcx = _conv_t_rows(padb_ref, w_ref, CONV_B_WIDTH, r * cr, cr)
            dgc_ref[rows, :] = dcx * xi_ref[rows, :]
            dxi_ref[rows, :] = dcx * gc_ref[rows, :]
            dw = _acc_list(dw, _conv_dw_rows(padb_ref[rows, :], pad_ref, CONV_B_WIDTH, r * cr, cr))
        for j in range(CONV_B_WIDTH):
            dw_ref[j:j + 1, :] = dw[j]

    col = lambda off: pl.BlockSpec((t, LANES), lambda j: (0, j + off))
    wsp = pl.BlockSpec((CONV_B_WIDTH, LANES), lambda j: (0, j))
    return _pc(body, name="ev_b_bwd", grid=(nb,), in_specs=[col(nb), col(2 * nb), col(3 * nb), col(4 * nb), wsp],
               out_specs=(col(0), col(0), col(0), wsp),
               out_shape=(S((t, D_A), f32), S((t, D_A), f32), S((t, D_A), f32), S((CONV_B_WIDTH, D_A), f32)),
               scratch_shapes=[pltpu.VMEM((t + CONV_LEAD, LANES), f32), pltpu.VMEM((t + CONV_LEAD, LANES), f32)],
               compiler_params=_cparams(("arbitrary",)))(dy, p, p, p, conv_b)


def _ffn_mid(u, conv_w, conv_b, name):
    t = u.shape[0]
    cr = _row_block8(t)
    nb = D_FF // LANES

    def body(gt_ref, vl_ref, w_ref, b_ref, o_ref, pad_ref, stage_ref):
        _fill_front_padded(pad_ref, gt_ref[...], t)
        for r in range(t // cr):
            rows = slice(r * cr, (r + 1) * cr)
            gc = _conv_rows(pad_ref, w_ref, FF_CONV_WIDTH, r * cr, cr) + b_ref[...]
            stage_ref[rows, :] = gc * _sigmoid(gc) * vl_ref[rows, :]
        o_ref[...] = stage_ref[...].astype(bf16)

    col = lambda off: pl.BlockSpec((t, LANES), lambda j: (0, j + off))
    return _pc(body, name=name, grid=(nb,),
               in_specs=[col(0), col(nb), pl.BlockSpec((FF_CONV_WIDTH, LANES), lambda j: (0, j)),
                         pl.BlockSpec((1, LANES), lambda j: (0, j))],
               out_specs=col(0), out_shape=S((t, D_FF), bf16),
               scratch_shapes=[pltpu.VMEM((t + CONV_LEAD, LANES), f32), pltpu.VMEM((t, LANES), f32)],
               compiler_params=_cparams(("arbitrary",)))(u, u, conv_w, conv_b.reshape(1, D_FF))


def _ffn_mid_bwd(dz, u, conv_w, conv_b, name):
    t = u.shape[0]
    cr = _row_block8(t)
    nb = D_FF // LANES

    def body(dz_ref, gt_ref, vl_ref, w_ref, b_ref, du_ref, dw_ref, db_ref, pad_ref, padb_ref, dval_ref):
        s = pl.program_id(1)

        @pl.when(s == 0)
        def _():
            _fill_front_padded(pad_ref, gt_ref[...], t)
            dw, db = None, None
            for r in range(t // cr):
                rows = slice(r * cr, (r + 1) * cr)
                lo = CONV_LEAD + r * cr - (FF_CONV_WIDTH - 1)
                taps = [pad_ref[lo + j:lo + j + cr, :] for j in range(FF_CONV_WIDTH)]
                gc = sum(w_ref[j:j + 1, :] * taps[j] for j in range(FF_CONV_WIDTH)) + b_ref[...]
                sg = _sigmoid(gc)
                dzr = dz_ref[rows, :]
                dval_ref[rows, :] = dzr * gc * sg
                dgc = dzr * vl_ref[rows, :] * sg * (1.0 + gc * (1.0 - sg))
                padb_ref[rows, :] = dgc
                dw = _acc_list(dw, [jnp.sum(dgc * tap, axis=0, keepdims=True) for tap in taps])
                pb = jnp.sum(dgc, axis=0, keepdims=True)
                db = pb if db is None else db + pb
            padb_ref[t:t + CONV_LEAD, :] = jnp.zeros((CONV_LEAD, LANES), f32)
            for r in range(t // cr):
                pad_ref[r * cr:(r + 1) * cr, :] = _conv_t_rows(padb_ref, w_ref, FF_CONV_WIDTH, r * cr, cr)
            du_ref[...] = pad_ref[0:t, :].astype(du_ref.dtype)
            for j in range(FF_CONV_WIDTH):
                dw_ref[j:j + 1, :] = dw[j]
            db_ref[...] = db

        @pl.when(s == 1)
        def _():
            du_ref[...] = dval_ref[...].astype(du_ref.dtype)

    col = lambda off: pl.BlockSpec((t, LANES), lambda j, s: (0, j + off))
    wsp = pl.BlockSpec((FF_CONV_WIDTH, LANES), lambda j, s: (0, j))
    bsp = pl.BlockSpec((1, LANES), lambda j, s: (0, j))
    return _pc(body, name=name, grid=(nb, 2), in_specs=[col(0), col(0), col(nb), wsp, bsp],
               out_specs=(pl.BlockSpec((t, LANES), lambda j, s: (0, s * nb + j)), wsp, bsp),
               out_shape=(S((t, 2 * D_FF), bf16), S((FF_CONV_WIDTH, D_FF), f32), S((1, D_FF), f32)),
               scratch_shapes=[pltpu.VMEM((t + CONV_LEAD, LANES), f32), pltpu.VMEM((t + CONV_LEAD, LANES), f32),
                               pltpu.VMEM((t, LANES), f32)],
               compiler_params=_cparams(("arbitrary", "arbitrary")))(dz, u, u, conv_w, conv_b.reshape(1, D_FF))


def _swap_halves(x):
    w = x.shape[1]
    lane = lax.broadcasted_iota(jnp.int32, x.shape, 1) % HEAD_DIM
    return jnp.where(lane < HEAD_DIM // 2, pltpu.roll(x, w - HEAD_DIM // 2, axis=1), pltpu.roll(x, HEAD_DIM // 2, axis=1))


def _rope_pack(patt, c64, s64):
    t = patt.shape[0]
    tp = t + ATT_PAD

    def body(p_ref, c_ref, s_ref, q_ref, k_ref, v_ref):
        c, s = c_ref[...], s_ref[...]

        def rope(x, nh):
            cc = jnp.concatenate([c] * nh, axis=1)
            ss = jnp.concatenate([s] * nh, axis=1)
            return x * cc + _swap_halves(x) * ss

        for ref, val in ((q_ref, rope(p_ref[:, 0:D_ATT], N_Q_HEADS)),
                         (k_ref, rope(p_ref[:, D_ATT:D_ATT + D_KV], N_KV_HEADS)),
                         (v_ref, p_ref[:, D_ATT + D_KV:ATT_COLS])):
            ref[0:ATT_PAD, :] = jnp.zeros((ATT_PAD, val.shape[1]), bf16)
            ref[ATT_PAD:tp, :] = val.astype(bf16)

    return _pc(body, name="rope_pack", in_specs=[_full((t, ATT_COLS)), _full((t, HEAD_DIM)), _full((t, HEAD_DIM))],
               out_specs=(_full((tp, D_ATT)), _full((tp, D_KV)), _full((tp, D_KV))), grid=(1,),
               out_shape=(S((tp, D_ATT), bf16), S((tp, D_KV), bf16), S((tp, D_KV), bf16)),
               compiler_params=_cparams(("arbitrary",)))(patt, c64, s64)


def _rope_bwd(dqp, dkp, dvp, c64, s64):
    tp = dqp.shape[0]
    t = tp - ATT_PAD

    def body(dq_ref, dk_ref, dv_ref, c_ref, s_ref, o_ref):
        c, s = c_ref[...], s_ref[...]

        def unrope(dy, nh):
            cc = jnp.concatenate([c] * nh, axis=1)
            ss = jnp.concatenate([s] * nh, axis=1)
            return dy * cc + _swap_halves(dy * ss)

        o_ref[:, 0:D_ATT] = unrope(dq_ref[ATT_PAD:tp, :], N_Q_HEADS).astype(bf16)
        o_ref[:, D_ATT:D_ATT + D_KV] = unrope(dk_ref[ATT_PAD:tp, :], N_KV_HEADS).astype(bf16)
        o_ref[:, D_ATT + D_KV:ATT_COLS] = dv_ref[ATT_PAD:tp, :].astype(bf16)

    return _pc(body, name="rope_bwd", grid=(1,),
               in_specs=[_full((tp, D_ATT)), _full((tp, D_KV)), _full((tp, D_KV)), _full((t, HEAD_DIM)), _full((t, HEAD_DIM))],
               out_specs=_full((t, ATT_COLS)), out_shape=S((t, ATT_COLS), bf16),
               compiler_params=_cparams(("arbitrary",)))(dqp, dkp, dvp, c64, s64)


def _attn_masks(n):
    rows = GQA_GROUP * BLOCK
    ri = lax.broadcasted_iota(jnp.int32, (rows, BLOCK), 0) % BLOCK
    ci = lax.broadcasted_iota(jnp.int32, (rows, BLOCK), 1)
    m_cur = (ci <= ri) & (ci >= jnp.where(n >= 1, 0, ATT_PAD))
    m_prev = ci > ri + jnp.where(n >= 2, 0, BLOCK)
    m_meta = ci >= jnp.where(n >= 1, ATT_PAD, BLOCK)
    return m_cur, m_prev, m_meta


def _attn_probs(qg, kc, kp, km, masks, skv):
    def scores(k, m):
        s = lax.dot_general(qg, k, _DIMS["nt"], preferred_element_type=f32) * ATT_SCALE
        return jnp.where(m, s, NEG_INF)
    s_c, s_p, s_m = scores(kc, masks[0]), scores(kp, masks[1]), scores(km, masks[2])
    mx = jnp.maximum(jnp.maximum(jnp.max(s_c, axis=-1, keepdims=True), jnp.max(s_p, axis=-1, keepdims=True)),
                     jnp.maximum(jnp.max(s_m, axis=-1, keepdims=True), skv))
    e_c, e_p, e_m, e_s = jnp.exp(s_c - mx), jnp.exp(s_p - mx), jnp.exp(s_m - mx), jnp.exp(skv - mx)
    den = (jnp.sum(e_c, axis=-1, keepdims=True) + jnp.sum(e_p, axis=-1, keepdims=True)
           + jnp.sum(e_m, axis=-1, keepdims=True) + e_s)
    inv = 1.0 / den
    return e_c * inv, e_p * inv, e_m * inv, e_s * inv


def _sink_rows(sk_ref, g):
    hrow = lax.broadcasted_iota(jnp.int32, (GQA_GROUP * BLOCK, 1), 0) // BLOCK
    skv = jnp.zeros((GQA_GROUP * BLOCK, 1), f32)
    for hh in range(GQA_GROUP):
        skv = jnp.where(hrow == hh, sk_ref[0, GQA_GROUP * g + hh], skv)
    return skv, hrow


def _stack_heads(ref, g):
    return jnp.concatenate([ref[:, (GQA_GROUP * g + hh) * HEAD_DIM:(GQA_GROUP * g + hh + 1) * HEAD_DIM]
                            for hh in range(GQA_GROUP)], axis=0)


def _attn_specs():
    blk = lambda w: pl.BlockSpec((BLOCK, w), lambda n: (n, 0))
    prev = pl.BlockSpec((BLOCK, D_KV), lambda n: (jnp.maximum(n - 1, 0), 0))
    meta = pl.BlockSpec((BLOCK, D_KV), lambda n: (0, 0))
    return blk, prev, meta


def _attn_fwd(qp, kp, vp, sinks):
    tp = qp.shape[0]
    blk, prev, meta = _attn_specs()

    def body(sk_ref, q_ref, kc_ref, kp_ref, km_ref, vc_ref, vp_ref, vm_ref, o_ref):
        masks = _attn_masks(pl.program_id(0))
        for g in range(N_KV_HEADS):
            sl = slice(g * HEAD_DIM, (g + 1) * HEAD_DIM)
            skv, _ = _sink_rows(sk_ref, g)
            p_c, p_p, p_m, _ = _attn_probs(_stack_heads(q_ref, g), kc_ref[:, sl], kp_ref[:, sl], km_ref[:, sl], masks, skv)
            o = (jnp.dot(p_c.astype(bf16), vc_ref[:, sl], preferred_element_type=f32)
                 + jnp.dot(p_p.astype(bf16), vp_ref[:, sl], preferred_element_type=f32)
                 + jnp.dot(p_m.astype(bf16), vm_ref[:, sl], preferred_element_type=f32))
            for hh in range(GQA_GROUP):
                h = GQA_GROUP * g + hh
                o_ref[:, h * HEAD_DIM:(h + 1) * HEAD_DIM] = o[hh * BLOCK:(hh + 1) * BLOCK].astype(bf16)

    return _pc(body, name="attn_fwd", grid=(tp // BLOCK,),
               in_specs=[pl.BlockSpec(memory_space=pltpu.SMEM), blk(D_ATT), blk(D_KV), prev, meta, blk(D_KV), prev, meta],
               out_specs=blk(D_ATT), out_shape=S((tp, D_ATT), bf16),
               compiler_params=_cparams(("arbitrary",)))(sinks, qp, kp, kp, kp, vp, vp, vp)


def _attn_bwd(qp, kp, vp, sinks, dop):
    tp = qp.shape[0]
    blk, prev, meta = _attn_specs()

    def body(sk_ref, q_ref, kc_ref, kp_ref, km_ref, vc_ref, vp_ref, vm_ref, do_ref, dq_ref, dk_ref, dv_ref, dsk_ref):
        n = pl.program_id(0)

        @pl.when(n == 0)
        def _():
            dk_ref[...] = jnp.zeros_like(dk_ref)
            dv_ref[...] = jnp.zeros_like(dv_ref)
            dsk_ref[...] = jnp.zeros_like(dsk_ref)
        masks = _attn_masks(n)
        cur = pl.ds(pl.multiple_of(n * BLOCK, BLOCK), BLOCK)
        prv = pl.ds(pl.multiple_of(jnp.maximum(n - 1, 0) * BLOCK, BLOCK), BLOCK)
        lane = lax.broadcasted_iota(jnp.int32, (1, LANES), 1)
        dsk = jnp.zeros((1, LANES), f32)
        for g in range(N_KV_HEADS):
            sl = slice(g * HEAD_DIM, (g + 1) * HEAD_DIM)
            skv, hrow = _sink_rows(sk_ref, g)
            qg = _stack_heads(q_ref, g)
            dog = _stack_heads(do_ref, g)
            ks = (kc_ref[:, sl], kp_ref[:, sl], km_ref[:, sl])
            vs = (vc_ref[:, sl], vp_ref[:, sl], vm_ref[:, sl])
            probs = _attn_probs(qg, ks[0], ks[1], ks[2], masks, skv)
            dps = [lax.dot_general(dog, v, _DIMS["nt"], preferred_element_type=f32) for v in vs]
            delta = sum(jnp.sum(p * dp, axis=-1, keepdims=True) for p, dp in zip(probs[:3], dps))
            dss = [(p * (dp - delta) * ATT_SCALE).astype(bf16) for p, dp in zip(probs[:3], dps)]
            dq = sum(jnp.dot(ds, k, preferred_element_type=f32) for ds, k in zip(dss, ks))
            for hh in range(GQA_GROUP):
                h = GQA_GROUP * g + hh
                dq_ref[:, h * HEAD_DIM:(h + 1) * HEAD_DIM] = dq[hh * BLOCK:(hh + 1) * BLOCK]
                dsk = dsk + jnp.where(lane == h, -jnp.sum(jnp.where(hrow == hh, probs[3] * delta, 0.0)), 0.0)
            for rows, p, ds in zip((cur, prv, slice(0, BLOCK)), probs[:3], dss):
                dv_ref[rows, sl] += lax.dot_general(p.astype(bf16), dog, _DIMS["tn"], preferred_element_type=f32)
                dk_ref[rows, sl] += lax.dot_general(ds, qg, _DIMS["tn"], preferred_element_type=f32)
        dsk_ref[...] += dsk

    return _pc(body, name="attn_bwd", grid=(tp // BLOCK,),
               in_specs=[pl.BlockSpec(memory_space=pltpu.SMEM), blk(D_ATT), blk(D_KV), prev, meta, blk(D_KV), prev, meta,
                         blk(D_ATT)],
               out_specs=(blk(D_ATT), _full((tp, D_KV)), _full((tp, D_KV)), _full((1, LANES))),
               out_shape=(S((tp, D_ATT), f32), S((tp, D_KV), f32), S((tp, D_KV), f32), S((1, LANES), f32)),
               compiler_params=_cparams(("arbitrary",)))(sinks, qp, kp, kp, kp, vp, vp, vp, dop)


def _seg(x, bm):
    hi = x.astype(bf16)
    lo = (x - hi.astype(f32)).astype(bf16)
    return jnp.dot(jnp.concatenate([hi, lo], axis=1), bm, preferred_element_type=f32)


@jax.custom_vjp
def _seg_linear(x, bm):
    return _seg(x, bm)


_seg_linear.defvjp(lambda x, bm: (_seg(x, bm), bm), lambda bm, ct: (_seg(ct, bm), jnp.zeros_like(bm)))


def _softplus(y):
    return jnp.maximum(y, 0.0) + jnp.log(1.0 + jnp.exp(-jnp.abs(y)))


def _prep_fn(xr, xk, xwd, xad, xgd, w0, w2, a0, a2, g2, k_k, k_a, bm, seg=_seg):
    xw = w0 + jnp.dot(jnp.tanh(xwd), w2, preferred_element_type=f32)
    decay = jnp.exp(-jnp.exp(-_softplus(-xw) - 0.5))
    alpha = _sigmoid(a0 + jnp.dot(xad, a2, preferred_element_type=f32))
    g = jnp.dot(_sigmoid(xgd), g2, preferred_element_type=f32)
    kk = xk * k_k
    kkn = kk / jnp.maximum(jnp.sqrt(seg(kk * kk, bm)), 1e-12)
    k2 = xk * (1.0 + (alpha - 1.0) * k_a)
    return decay, k2, -kkn, kkn * alpha, g


def _split_cols(x):
    o1, o2, o3 = 3 * D_R, 3 * D_R + LORA_W, 3 * D_R + LORA_W + LORA_A
    return x[:, 0:D_R], x[:, D_R:2 * D_R], x[:, 2 * D_R:o1], x[:, o1:o2], x[:, o2:o3], x[:, o3:RWKV_COLS]


def _shifted(sh_ref, x, halo, first, rb):
    sh_ref[0:SUBLANES, :] = jnp.where(first, 0.0, halo)
    sh_ref[SUBLANES:SUBLANES + rb, :] = x
    return sh_ref[SUBLANES - 1:SUBLANES - 1 + rb, :]


_PREP_PARAMS = ("od_w0", "od_w2", "od_a0", "od_a2", "od_g2", "od_k_k", "od_k_a")


def _rwkv_prep(pr, mu, params, bm):
    t = pr.shape[0]
    rb = _row_block8(t)
    hb = rb // SUBLANES

    def body(pr_ref, halo_ref, mu_ref, w0, w2, a0, a2, g2, kk_ref, ka_ref, bm_ref, *outs_sh):
        outs, sh_ref = outs_sh[:-1], outs_sh[-1]
        x = pr_ref[...]
        prev = _shifted(sh_ref, x, halo_ref[...], pl.program_id(0) == 0, rb)
        xr, xk, xv, xwd, xad, xgd = _split_cols(x + (prev - x) * mu_ref[...])
        bmv = bm_ref[...]
        decay, k2, a_s, b_s, g = _prep_fn(xr, xk, xwd, xad, xgd, w0[...], w2[...], a0[...], a2[...], g2[...],
                                          kk_ref[...], ka_ref[...], bmv)
        vals = (xr, xv, decay, k2, a_s, b_s, decay * xr, _seg(b_s * xr, bmv), _seg(k2 * xr, bmv), g)
        for ref, val in zip(outs, vals):
            ref[...] = val

    row = pl.BlockSpec((rb, RWKV_COLS), lambda i: (i, 0))
    halo = pl.BlockSpec((SUBLANES, RWKV_COLS), lambda i: (jnp.maximum(i * hb - 1, 0), 0))
    orow = pl.BlockSpec((rb, D_R), lambda i: (i, 0))
    return _pc(body, name="rwkv_prep", grid=(t // rb,),
               in_specs=[row, halo, _full((1, RWKV_COLS))] + [_full(p.shape) for p in params] + [_full(bm.shape)],
               out_specs=(orow,) * 10, out_shape=(S((t, D_R), f32),) * 10,
               scratch_shapes=[pltpu.VMEM((rb + SUBLANES, RWKV_COLS), f32)],
               compiler_params=_cparams(("arbitrary",)))(pr, pr, mu, *params, bm)


def _rwkv_prep_bwd(pr, mu, params, bm, cts):
    t = pr.shape[0]
    rb = _row_block8(t)
    hb = rb // SUBLANES
    counts = [len(c) for c in cts]
    flat = [a for c in cts for a in c]

    def body(pr_ref, halo_ref, mu_ref, w0, w2, a0, a2, g2, kk_ref, ka_ref, bm_ref, *rest):
        ct_refs, rest = rest[:len(flat)], rest[len(flat):]
        dx_ref, dmu_ref = rest[0], rest[1]
        dpar_refs, sh_ref = rest[2:9], rest[9]

        @pl.when(pl.program_id(0) == 0)
        def _():
            dmu_ref[...] = jnp.zeros_like(dmu_ref)
            for r in dpar_refs:
                r[...] = jnp.zeros_like(r)
        sums, pos = [], 0
        for c in counts:
            sums.append(sum(r[...] for r in ct_refs[pos:pos + c]))
            pos += c
        x = pr_ref[...]
        prev = _shifted(sh_ref, x, halo_ref[...], pl.program_id(0) == 0, rb)
        xr, xk, xv, xwd, xad, xgd = _split_cols(x + (prev - x) * mu_ref[...])
        bmv = bm_ref[...]
        _, vjp = jax.vjp(lambda *a: _prep_fn(*a, bmv, _seg_linear), xr, xk, xwd, xad, xgd, w0[...], w2[...], a0[...], a2[...],
                         g2[...], kk_ref[...], ka_ref[...])
        grads = vjp(tuple(sums[:5]))
        dxr, dxk, dxwd, dxad, dxgd = grads[:5]
        o1, o2, o3 = 3 * D_R, 3 * D_R + LORA_W, 3 * D_R + LORA_W + LORA_A
        dx_ref[:, 0:D_R] = dxr + sums[5]
        dx_ref[:, D_R:2 * D_R] = dxk
        dx_ref[:, 2 * D_R:o1] = sums[6]
        dx_ref[:, o1:o2] = dxwd
        dx_ref[:, o2:o3] = dxad
        dx_ref[:, o3:RWKV_COLS] = dxgd
        dmu_ref[...] += jnp.sum(dx_ref[...] * (prev - x), axis=0, keepdims=True)
        for r, gval in zip(dpar_refs, grads[5:]):
            r[...] += gval

    row = pl.BlockSpec((rb, RWKV_COLS), lambda i: (i, 0))
    halo = pl.BlockSpec((SUBLANES, RWKV_COLS), lambda i: (jnp.maximum(i * hb - 1, 0), 0))
    crow = pl.BlockSpec((rb, D_R), lambda i: (i, 0))
    return _pc(body, name="rwkv_prep_bwd", grid=(t // rb,),
               in_specs=[row, halo, _full((1, RWKV_COLS))] + [_full(p.shape) for p in params] + [_full(bm.shape)]
               + [crow] * len(flat),
               out_specs=(row, _full((1, RWKV_COLS))) + tuple(_full(p.shape) for p in params),
               out_shape=(S((t, RWKV_COLS), f32), S((1, RWKV_COLS), f32)) + tuple(S(p.shape, f32) for p in params),
               scratch_shapes=[pltpu.VMEM((rb + SUBLANES, RWKV_COLS), f32)],
               compiler_params=_cparams(("arbitrary",)))(pr, pr, mu, *params, bm, *flat)


def _shift_bwd(dxs, mu):
    t = dxs.shape[0]
    rb = _row_block(t)
    hb = rb // SUBLANES
    nblk = t // rb

    def body(dx_ref, halo_ref, mu_ref, o_ref, sh_ref):
        dx = dx_ref[...]
        sh_ref[0:rb, :] = dx
        sh_ref[rb:rb + SUBLANES, :] = jnp.where(pl.program_id(0) == nblk - 1, 0.0, halo_ref[...])
        m = mu_ref[...]
        o_ref[...] = (dx * (1.0 - m) + sh_ref[1:1 + rb, :] * m).astype(bf16)

    row = pl.BlockSpec((rb, RWKV_COLS), lambda i: (i, 0))
    halo = pl.BlockSpec((SUBLANES, RWKV_COLS), lambda i: (jnp.minimum((i + 1) * hb, t // SUBLANES - 1), 0))
    return _pc(body, name="rwkv_shift_bwd", grid=(nblk,), in_specs=[row, halo, _full((1, RWKV_COLS))],
               out_specs=row, out_shape=S((t, RWKV_COLS), bf16),
               scratch_shapes=[pltpu.VMEM((rb + SUBLANES, RWKV_COLS), f32)],
               compiler_params=_cparams(("arbitrary",)))(dxs, dxs, mu)


def _post_fn(y, xr, k2, xv, g, lg, lb, rk, bm, seg=_seg):
    inv_n = 1.0 / HEAD_DIM
    yc = y - seg(y, bm) * inv_n
    var = seg(yc * yc, bm) * inv_n
    yn = yc * lax.rsqrt(var + RWKV_GN_EPS) * lg + lb
    return (yn + seg(xr * k2 * rk, bm) * xv) * g


def _rwkv_post(y, xr, k2, xv, g, lg, lb, rk, bm):
    t = y.shape[0]
    rb = _row_block8(t)

    def body(y_ref, xr_ref, k2_ref, xv_ref, g_ref, lg_ref, lb_ref, rk_ref, bm_ref, o_ref):
        o_ref[...] = _post_fn(y_ref[...], xr_ref[...], k2_ref[...], xv_ref[...], g_ref[...], lg_ref[...], lb_ref[...],
                              rk_ref[...], bm_ref[...])

    row = pl.BlockSpec((rb, D_R), lambda i: (i, 0))
    vec = _full((1, D_R))
    return _pc(body, name="rwkv_post", grid=(t // rb,), in_specs=[row] * 5 + [vec] * 3 + [_full(bm.shape)],
               out_specs=row, out_shape=S((t, D_R), f32),
               compiler_params=_cparams(("arbitrary",)))(y, xr, k2, xv, g, lg, lb, rk, bm)


def _rwkv_post_bwd(dy1, y, xr, k2, xv, g, lg, lb, rk, bm):
    t = y.shape[0]
    rb = _row_block8(t)

    def body(dy_ref, y_ref, xr_ref, k2_ref, xv_ref, g_ref, lg_ref, lb_ref, rk_ref, bm_ref, *outs):
        @pl.when(pl.program_id(0) == 0)
        def _():
            for r in outs[5:]:
                r[...] = jnp.zeros_like(r)
        bmv = bm_ref[...]
        _, vjp = jax.vjp(lambda *a: _post_fn(*a, bmv, _seg_linear), y_ref[...], xr_ref[...], k2_ref[...], xv_ref[...], g_ref[...],
                         lg_ref[...], lb_ref[...], rk_ref[...])
        grads = vjp(dy_ref[...])
        for r, gval in zip(outs[:5], grads[:5]):
            r[...] = gval
        for r, gval in zip(outs[5:], grads[5:]):
            r[...] += gval

    row = pl.BlockSpec((rb, D_R), lambda i: (i, 0))
    vec = _full((1, D_R))
    return _pc(body, name="rwkv_post_bwd", grid=(t // rb,),
               in_specs=[pl.BlockSpec((rb, D_R), lambda i: (i, 1))] + [row] * 5 + [vec] * 3 + [_full(bm.shape)],
               out_specs=(row,) * 5 + (vec,) * 3, out_shape=(S((t, D_R), f32),) * 5 + (S((1, D_R), f32),) * 3,
               compiler_params=_cparams(("arbitrary",)))(dy1, y, xr, k2, xv, g, lg, lb, rk, bm)


def _seg2(x, bb):
    hi = x.astype(bf16)
    lo = (x - hi.astype(f32)).astype(bf16)
    return jnp.dot(jnp.concatenate([hi, lo], axis=1), bb, preferred_element_type=f32)


def _row4(rows, j):
    return jnp.concatenate([jnp.broadcast_to(rows[j:j + 1, p * LANES:(p + 1) * LANES], (HEAD_DIM, LANES))
                            for p in range(4)], axis=0)


def _scan_consts():
    lane_group = jnp.arange(LANES) // HEAD_DIM
    b128 = (lane_group[:, None] == lane_group[None, :]).astype(bf16)
    bb = jnp.concatenate([b128, b128], axis=0)
    qsel = (jnp.arange(PAIR_ROWS)[:, None] % HEAD_DIM == jnp.arange(LANES)[None, :] % HEAD_DIM).astype(f32)
    return bb, qsel


def _store_cols(acc_ref, o_ref, tc):
    for p in range(4):
        blk = acc_ref[p * HEAD_DIM:(p + 1) * HEAD_DIM, :].T
        o_ref[:, (2 * p) * HEAD_DIM:(2 * p + 1) * HEAD_DIM] = blk[0:tc]
        o_ref[:, (2 * p + 1) * HEAD_DIM:(2 * p + 2) * HEAD_DIM] = blk[HEAD_DIM:HEAD_DIM + tc]


PAIR_GROUP = 2 * SUBLANES


def _rwkv_pairs(w, a, b, k, wr, bm):
    t = w.shape[0]
    rb = _row_block8(t)

    def body(w_ref, a_ref, b_ref, k_ref, wr_ref, bm_ref, *outs_sh):
        outs, sh_ref = outs_sh[:-1], outs_sh[-1]

        def second(ref):
            sh_ref[0:rb, :] = ref[...]
            sh_ref[rb:rb + SUBLANES, :] = jnp.zeros((SUBLANES, D_R), f32)
            return sh_ref[1:1 + rb, :]

        w1, b1, k1 = w_ref[...], b_ref[...], k_ref[...]
        w2, a2, wr2 = second(w_ref), second(a_ref), second(wr_ref)
        bmv = bm_ref[...]
        vals = (w1 * a2, w1 * wr2, w1 * w2, b1 * w2, k1 * w2, _seg(b1 * a2, bmv), _seg(k1 * a2, bmv),
                _seg(b1 * wr2, bmv), _seg(k1 * wr2, bmv))
        for ref, val in zip(outs, vals):
            ref[...] = val

    row = pl.BlockSpec((rb, D_R), lambda i: (i, 0))
    return _pc(body, name="rwkv_pairs", grid=(t // rb,), in_specs=[row] * 5 + [_full(bm.shape)],
               out_specs=(row,) * 9, out_shape=(S((t, D_R), f32),) * 9,
               scratch_shapes=[pltpu.VMEM((rb + SUBLANES, D_R), f32)],
               compiler_params=_cparams(("arbitrary",)))(w, a, b, k, wr, bm)


def _wkv_fwd(w, k, v, a, b, wr, br, kr, pairs):
    t = w.shape[0]
    tc = SCAN_CHUNK
    bb, qsel = _scan_consts()

    def body(*refs):
        step_refs, pair_refs = refs[0:8], refs[8:17]
        bb_ref, q_ref, y_ref, st_ref, sa_ref, vb_ref, s_scr, yacc = refs[17:]

        @pl.when(pl.program_id(0) == 0)
        def _():
            s_scr[...] = jnp.zeros_like(s_scr)
        bbv, qv = bb_ref[...], q_ref[...]
        lane64 = lax.broadcasted_iota(jnp.int32, (PAIR_ROWS, LANES), 1) % HEAD_DIM

        def halves(x):
            hi = x.astype(bf16)
            return jnp.concatenate([hi, (x - hi.astype(f32)).astype(bf16)], axis=1)

        def group(gi, s):
            base = pl.multiple_of(gi * PAIR_GROUP, PAIR_GROUP)
            w16, k16, v16, a16, b16, wr16, br16, kr16 = step_refs
            a2p, r2p, w12p, b1wp, k1wp, betap, kappap, bwrp, kwrp = pair_refs

            def rows8(ref, j):
                return ref[pl.ds(base + (j // SUBLANES) * SUBLANES, SUBLANES), :]

            def bcast(rows, j, p):
                return jnp.broadcast_to(rows[j % SUBLANES:j % SUBLANES + 1, p * LANES:(p + 1) * LANES], (HEAD_DIM, LANES))

            step = lambda ref, j, p: bcast(rows8(ref, j), j, p)
            qp = qv[0:HEAD_DIM]
            lane = lane64[0:HEAD_DIM]
            for q in range(SUBLANES):
                j1, j2 = 2 * q, 2 * q + 1
                t1 = base + j1
                nxt = []
                for p in range(4):
                    sl = slice(p * HEAD_DIM, (p + 1) * HEAD_DIM)
                    sp = s[sl]
                    lhs = [halves(jnp.concatenate([sp * step(a16, j1, p), sp * step(a2p, j1, p), sp * step(wr16, j1, p),
                                                   sp * step(r2p, j1, p)], axis=0))]
                    for j in (j1, j2):
                        v8 = rows8(v16, j)
                        vh8 = v8.astype(bf16).astype(f32)
                        lhs.append(jnp.concatenate([(qp * bcast(vh8, j, p)).astype(bf16),
                                                    (qp * bcast(v8 - vh8, j, p)).astype(bf16)], axis=1))
                    r = jnp.dot(jnp.concatenate(lhs, axis=0), bbv, preferred_element_type=f32)
                    sa1, p2, z1, z2, vb1, vb2 = (r[n * HEAD_DIM:(n + 1) * HEAD_DIM] for n in range(6))
                    sa2 = p2 + sa1 * step(betap, j1, p) + vb1 * step(kappap, j1, p)
                    y1 = z1 + sa1 * step(br16, j1, p) + vb1 * step(kr16, j1, p)
                    y2 = (z2 + sa1 * step(bwrp, j1, p) + vb1 * step(kwrp, j1, p)) + (sa2 * step(br16, j2, p)
                                                                                      + vb2 * step(kr16, j2, p))
                    yacc[sl, :] = jnp.where(lane == t1, y1, jnp.where(lane == t1 + 1, y2, yacc[sl, :]))
                    st_ref[base // 2 + q, sl, :] = sp
                    sa_ref[t1, sl, :] = sa1
                    sa_ref[t1 + 1, sl, :] = sa2
                    vb_ref[t1, sl, :] = vb1
                    vb_ref[t1 + 1, sl, :] = vb2
                    nxt.append(((sp * step(w12p, j1, p) + sa1 * step(b1wp, j1, p)) + vb1 * step(k1wp, j1, p))
                               + (sa2 * step(b16, j2, p) + vb2 * step(k16, j2, p)))
                s = jnp.concatenate(nxt, axis=0)
            return s

        s_scr[...] = lax.fori_loop(0, tc // PAIR_GROUP, group, s_scr[...])
        _store_cols(yacc, y_ref, tc)

    row = pl.BlockSpec((tc, D_R), lambda c: (c, 0))
    tiles = pl.BlockSpec((tc, PAIR_ROWS, LANES), lambda c: (c, 0, 0))
    return _pc(body, name="wkv_fwd", grid=(t // tc,),
               in_specs=[row] * 17 + [_full(bb.shape), _full(qsel.shape)],
               out_specs=(row, pl.BlockSpec((tc // 2, PAIR_ROWS, LANES), lambda c: (c, 0, 0)), tiles, tiles),
               out_shape=(S((t, D_R), f32), S((t // 2, PAIR_ROWS, LANES), f32)) + (S((t, PAIR_ROWS, LANES), f32),) * 2,
               scratch_shapes=[pltpu.VMEM((PAIR_ROWS, LANES), f32), pltpu.VMEM((PAIR_ROWS, LANES), f32)],
               compiler_params=_cparams(("arbitrary",)))(w, k, v, a, b, wr, br, kr, *pairs, bb, qsel)


def _wkv_bwd(sprev, sab, vbb, w, k, a, b, r, dy):
    t = w.shape[0]
    tc = SCAN_CHUNK
    nc = t // tc
    bb, qsel = _scan_consts()

    def body(st_ref, sa_ref, vb_ref, w_ref, k_ref, a_ref, b_ref, r_ref, dy_ref, bb_ref, q_ref,
             dr_ref, dw_ref, dk_ref, dv_ref, da_ref, db_ref, g_scr, dvacc, rows_scr):
        @pl.when(pl.program_id(0) == 0)
        def _():
            g_scr[...] = jnp.zeros_like(g_scr)
        bbv, qv = bb_ref[...], q_ref[...]
        lane64 = lax.broadcasted_iota(jnp.int32, (PAIR_ROWS, LANES), 1) % HEAD_DIM
        outs = (dr_ref, dw_ref, db_ref, dk_ref, da_ref)

        def colsums(slot, j, x):
            for p in range(4):
                rows_scr[slot, j:j + 1, p * LANES:(p + 1) * LANES] = jnp.sum(x[p * HEAD_DIM:(p + 1) * HEAD_DIM], axis=0,
                                                                           keepdims=True)

        def group(i, g):
            base = pl.multiple_of((tc // SUBLANES - 1 - i) * SUBLANES, SUBLANES)
            w8, k8, a8, b8, r8, dy8 = (ref[pl.ds(base, SUBLANES), :] for ref in (w_ref, k_ref, a_ref, b_ref, r_ref, dy_ref))

            def after_step(j, sp):
                return sp * _row4(w8, j) + sa_ref[base + j] * _row4(b8, j) + vb_ref[base + j] * _row4(k8, j)

            def back_step(j, sp, s_t, g):
                tt = base + j
                u, vb = sa_ref[tt], vb_ref[tt]
                a4, b4, w4, k4 = _row4(a8, j), _row4(b8, j), _row4(w8, j), _row4(k8, j)
                dyb = _seg2(qv * _row4(dy8, j), bbv)
                g = g + dyb * _row4(r8, j)
                rr2 = _seg2(jnp.concatenate([g * b4, g * k4], axis=0), bbv)
                du, dvb = rr2[0:PAIR_ROWS], rr2[PAIR_ROWS:2 * PAIR_ROWS]
                for slot, val in enumerate((s_t * dyb, g * sp, g * u, g * vb, sp * du)):
                    colsums(slot, j, val)
                dvacc[...] = jnp.where(lane64 == tt, dvb, dvacc[...])
                return g * w4 + du * a4

            for q in reversed(range(SUBLANES // 2)):
                s0 = st_ref[base // 2 + q]
                s1 = after_step(2 * q, s0)
                g = back_step(2 * q + 1, s1, after_step(2 * q + 1, s1), g)
                g = back_step(2 * q, s0, s1, g)
            for slot, ref in enumerate(outs):
                ref[pl.ds(base, SUBLANES), :] = rows_scr[slot]
            return g

        g_scr[...] = lax.fori_loop(0, tc // SUBLANES, group, g_scr[...])
        _store_cols(dvacc, dv_ref, tc)

    row = pl.BlockSpec((tc, D_R), lambda c: (nc - 1 - c, 0))
    tiles = pl.BlockSpec((tc, PAIR_ROWS, LANES), lambda c: (nc - 1 - c, 0, 0))
    states = pl.BlockSpec((tc // 2, PAIR_ROWS, LANES), lambda c: (nc - 1 - c, 0, 0))
    return _pc(body, name="wkv_bwd", grid=(nc,),
               in_specs=[states, tiles, tiles] + [row] * 6 + [_full(bb.shape), _full(qsel.shape)],
               out_specs=(row,) * 6, out_shape=(S((t, D_R), f32),) * 6,
               scratch_shapes=[pltpu.VMEM((PAIR_ROWS, LANES), f32), pltpu.VMEM((PAIR_ROWS, LANES), f32),
                               pltpu.VMEM((5, SUBLANES, D_R), f32)],
               compiler_params=_cparams(("arbitrary",)))(sprev, sab, vbb, w, k, a, b, r, dy, bb, qsel)


def _rope_tables(t):
    half = HEAD_DIM // 2
    inv = ROPE_THETA ** (-jnp.arange(half, dtype=f32) / half)
    ang = jnp.arange(t, dtype=f32)[:, None] * inv[None, :]
    cos, sin = jnp.cos(ang), jnp.sin(ang)
    return jnp.concatenate([cos, cos], axis=1), jnp.concatenate([-sin, sin], axis=1)


def _head_matrix():
    grp = jnp.arange(D_R) // HEAD_DIM
    b = (grp[:, None] == grp[None, :]).astype(bf16)
    return jnp.concatenate([b, b], axis=0)


def _ffn_fwd(h, g, get_w, conv_w, conv_b, i):
    hf = _rms_fwd(h, g, f"ffn{i}_norm")
    w_up_t = get_w(f"ff{i}_up", hf)
    u = _mm(hf, w_up_t, "nt", f"ffn{i}_up")
    z = _ffn_mid(u, conv_w, conv_b, f"ffn{i}_mid")
    w_down = get_w(f"ff{i}_down", z)
    return _mm(z, w_down, "nn", f"ffn{i}_down", res=h), (hf, u, z), w_up_t, w_down


def _ffn_bwd(dh, h, saved, g, w_up_t, conv_w, conv_b, w_down, i, put_g):
    hf, u, z = saved
    dz = _mm(dh, w_down, "nt", f"ffn{i}_dz")
    g_down = _mm(z, dh, "tn", f"ffn{i}_gdown", out_dtype=GRAD_WIRE_DTYPE)
    tok = put_g(f"ff{i}_down", g_down)
    du, g_conv, g_convb = _ffn_mid_bwd(dz, u, conv_w, conv_b + tok, f"ffn{i}_mid_bwd")
    g_up_t = _mm(du, hf, "tn", f"ffn{i}_gup", out_dtype=GRAD_WIRE_DTYPE)
    tok = put_g(f"ff{i}_up", g_up_t)
    dhf = _mm(du, w_up_t, "nn", f"ffn{i}_dhf")
    dh_in, g_norm = _rms_bwd(dhf, h, g + tok, dh, f"ffn{i}_norm_bwd")
    return dh_in, dict(conv=g_conv, conv_b=g_convb, norm=g_norm)


def _local_step(x, target, W, get_w, put_g, put_small, tok0):
    t = N_META + x.shape[0]
    c64, s64 = _rope_tables(t)
    bm = _head_matrix()
    h0 = jnp.concatenate([W["meta_tokens"], x], axis=0)

    ev_w_in_t, ev_w_out = get_w("ev_in", None), get_w("ev_out", None)
    hn0 = _rms_fwd(h0, W["norm_mix"][0] + tok0, "mix0_norm")
    p0 = _mm(hn0, ev_w_in_t, "nt", "ev_in")
    uc = _ev_a_conv(p0, W["ev_conv_a"])
    y0 = _ev_b(p0, W["ev_conv_b"], _ev_a_norm(uc, W["ev_ln_a_g"], W["ev_ln_a_b"]))
    h1 = _mm(y0, ev_w_out, "nn", "ev_out", res=h0)
    h2, ffn0, ff0_up_t, ff0_down = _ffn_fwd(h1, W["norm_ffn"][0], get_w, W["ff_conv"][0], W["ff_conv_b"][0], 0)

    hn1 = _rms_fwd(h2, W["norm_mix"][1], "mix1_norm")
    od_w_in_t = get_w("od_in", hn1)
    w_att, w_rwkv = od_w_in_t[:ATT_COLS], od_w_in_t[ATT_COLS:]
    pr = _mm(hn1, w_rwkv, "nt", "od_in_rwkv")
    qp, kp, vp = _rope_pack(_mm(hn1, w_att, "nt", "od_in_att"), c64, s64)
    op = _attn_fwd(qp, kp, vp, W["od_sinks"])
    prep_params = [W[n] for n in _PREP_PARAMS]
    xr, xv, decay, k2, a_s, b_s, wr, br, kr, gate = _rwkv_prep(pr, W["od_mu"], prep_params, bm)
    pairs = _rwkv_pairs(decay, a_s, b_s, k2, wr, bm)
    ysc, sprev, sab, vbb = _wkv_fwd(decay, k2, xv, a_s, b_s, wr, br, kr, pairs)
    rk = W["od_r_k"].reshape(1, D_R)
    yr = _rwkv_post(ysc, xr, k2, xv, gate, W["od_lnx_g"], W["od_lnx_b"], rk, bm)
    y1 = jnp.concatenate([op[ATT_PAD:], yr.astype(bf16)], axis=1)
    od_w_out = get_w("od_out", y1)
    h3 = _mm(y1, od_w_out, "nn", "od_out", res=h2)
    h4, ffn1, ff1_up_t, ff1_down = _ffn_fwd(h3, W["norm_ffn"][1], get_w, W["ff_conv"][1], W["ff_conv_b"][1], 1)

    tgt = jnp.concatenate([jnp.zeros((N_META, D_MODEL), f32), target], axis=0)
    loss, dh4, g_norm_final = _final_loss(h4, W["norm_final"], tgt)

    dh3, gf1 = _ffn_bwd(dh4, h3, ffn1, W["norm_ffn"][1], ff1_up_t, W["ff_conv"][1], W["ff_conv_b"][1], ff1_down, 1, put_g)
    dy1 = _mm(dh3, od_w_out, "nt", "od_dy")
    g_od_w_out = _mm(y1, dh3, "tn", "od_gout", out_dtype=GRAD_WIRE_DTYPE)
    tok = put_g("od_out", g_od_w_out)
    dysc, dxr_p, dk2_p, dxv_p, dgate, g_lnx_g, g_lnx_b, g_rk = _rwkv_post_bwd(
        dy1, ysc, xr, k2, xv, gate, W["od_lnx_g"], W["od_lnx_b"] + tok, rk, bm)
    dr, dw, dk, dv, da, db = _wkv_bwd(sprev, sab, vbb, decay, k2, a_s, b_s, xr, dysc)
    prep_grads = _rwkv_prep_bwd(pr, W["od_mu"], prep_params, bm,
                                [[dw], [dk, dk2_p], [da], [db], [dgate], [dr, dxr_p], [dv, dxv_p]])
    dxs, g_mu = prep_grads[0], prep_grads[1]
    dpr = _shift_bwd(dxs, W["od_mu"])
    dop = jnp.concatenate([jnp.zeros((ATT_PAD, D_ATT), f32), dy1[:, :D_ATT]], axis=0).astype(bf16)
    dqp, dkp, dvp, dsk = _attn_bwd(qp, kp, vp, W["od_sinks"], dop)
    dpatt = _rope_bwd(dqp, dkp, dvp, c64, s64)
    g_od_w_in_t = jnp.concatenate([_mm(dpatt, hn1, "tn", "od_gin_att", out_dtype=GRAD_WIRE_DTYPE),
                                   _mm(dpr, hn1, "tn", "od_gin_rwkv", out_dtype=GRAD_WIRE_DTYPE)], axis=0)
    tok = put_g("od_in", g_od_w_in_t)
    dhn1 = _mm(dpr, w_rwkv, "nn", "od_dhn_rwkv", res=_mm(dpatt, w_att, "nn", "od_dhn_att"))
    dh2, g_norm_mix1 = _rms_bwd(dhn1, h2, W["norm_mix"][1] + tok, dh3, "mix1_norm_bwd")

    dh1, gf0 = _ffn_bwd(dh2, h1, ffn0, W["norm_ffn"][0], ff0_up_t, W["ff_conv"][0], W["ff_conv_b"][0], ff0_down, 0, put_g)
    early = dict(
        norm_ffn=jnp.concatenate([gf0["norm"], gf1["norm"]], axis=0), norm_final=g_norm_final.reshape(D_MODEL),
        od_sinks=dsk[:, :N_Q_HEADS], od_mu=g_mu, od_lnx_g=g_lnx_g, od_lnx_b=g_lnx_b, od_r_k=g_rk.reshape(N_Q_HEADS, HEAD_DIM),
        ff_conv=jnp.stack([gf0["conv"], gf1["conv"]]), ff_conv_b=jnp.concatenate([gf0["conv_b"], gf1["conv_b"]], axis=0),
        **dict(zip(_PREP_PARAMS, prep_grads[2:])))
    dy0 = _mm(dh1, ev_w_out, "nt", "ev_dy")
    g_ev_w_out = _mm(y0, dh1, "tn", "ev_gout", out_dtype=GRAD_WIRE_DTYPE)
    tok = put_g("ev_out", g_ev_w_out) + put_small(early)
    duc, g_ln_g, g_ln_b = _ev_a_norm_bwd(dy0, uc, W["ev_ln_a_g"], W["ev_ln_a_b"] + tok)
    dav, dag, g_conv_a = _ev_a_conv_bwd(duc, p0, W["ev_conv_a"])
    dgb, dgc, dxi, g_conv_b = _ev_b_bwd(dy0, p0, W["ev_conv_b"])
    dp0 = jnp.concatenate([dav, dag, dgb, dgc, dxi], axis=1)
    g_ev_w_in_t = _mm(dp0, hn0, "tn", "ev_gin", out_dtype=GRAD_WIRE_DTYPE)
    tok = put_g("ev_in", g_ev_w_in_t)
    dhn0 = _mm(dp0, ev_w_in_t, "nn", "ev_dhn")
    dh0, g_norm_mix0 = _rms_bwd(dhn0, h0, W["norm_mix"][0] + tok, dh1, "mix0_norm_bwd")

    late = dict(meta_tokens=dh0[:N_META], norm_mix=jnp.concatenate([g_norm_mix0, g_norm_mix1], axis=0),
                ev_conv_a=g_conv_a, ev_ln_a_g=g_ln_g, ev_ln_a_b=g_ln_b, ev_conv_b=g_conv_b)
    return loss, dh0[N_META:], late


HBM = pl.BlockSpec(memory_space=pl.ANY)


def _mesh_pos():
    return lax.axis_index("x"), lax.axis_index("y"), lax.axis_index("c")


def _dev(px, py, pc):
    return 4 * px + 2 * py + pc


def _all_gather(xs, name):
    n = len(xs)

    def body(*refs):
        x_refs, o_refs = refs[:n], refs[n:2 * n]
        send_sems, recv_sems, local_sems = refs[2 * n:]
        x, y, c = _mesh_pos()
        me, sibling = (x, y, c), (x, y, 1 - c)
        chips = [(1 - x, y), (x, 1 - y), (1 - x, 1 - y)]

        def copy(i, k, block, to, from_input=False):
            dst = o_refs[i].at[_dev(*block)]
            return pltpu.make_async_remote_copy(src_ref=x_refs[i] if from_input else dst, dst_ref=dst,
                                                send_sem=send_sems.at[i, k], recv_sem=recv_sems.at[i, k],
                                                device_id=to, device_id_type=MESH)

        mine = [pltpu.make_async_copy(x_refs[i], o_refs[i].at[_dev(*me)], local_sems.at[i]) for i in range(n)]
        for cp in mine:
            cp.start()
        first = []
        for i in range(n):
            first.append(copy(i, 0, me, sibling, True))
            first += [copy(i, 1 + j, me, (*chip, c), True) for j, chip in enumerate(chips)]
        for cp in first:
            cp.start()
        passed = []
        for j, chip in enumerate(chips):
            for i in range(n):
                copy(i, 1 + j, (*chip, c), me).wait_recv()
                fwd = copy(i, 4 + j, (*chip, c), sibling)
                fwd.start()
                passed.append(fwd)
        for i in range(n):
            copy(i, 0, sibling, me).wait_recv()
            for j, chip in enumerate(chips):
                copy(i, 4 + j, (*chip, 1 - c), me).wait_recv()
        for cp in first + passed:
            cp.wait_send()
        for cp in mine:
            cp.wait()

    return _pc(body, name=name, in_specs=[HBM] * n, out_specs=tuple([HBM] * n),
               out_shape=tuple(S((N_DEV,) + x.shape, x.dtype) for x in xs),
               scratch_shapes=[pltpu.SemaphoreType.DMA((n, 7)), pltpu.SemaphoreType.DMA((n, 7)),
                               pltpu.SemaphoreType.DMA((n,))])(*xs)


HBM_SPEC = pl.BlockSpec(memory_space=pltpu.HBM)
SEM_SPEC = pl.BlockSpec(memory_space=pltpu.SEMAPHORE)
DATAFLOW = pltpu.SideEffectType.DATAFLOW_SIDE_EFFECTING
_PEER_FLIPS = ((1, 0, 0), (0, 1, 0), (1, 1, 0), (1, 0, 1), (0, 1, 1), (1, 1, 1), (0, 0, 1))
N_PEERS = len(_PEER_FLIPS)


def _peers(x, y, c):
    return [((1 - x) if fx else x, (1 - y) if fy else y, (1 - c) if fc else c) for fx, fy, fc in _PEER_FLIPS]


def _xchg_start(srcs, lands, scatter, name):
    n = len(srcs)

    def body(*refs):
        src_refs, land_refs = refs[:n], refs[n:2 * n]
        send_sems, recv_sems, token = refs[2 * n], refs[2 * n + 1], refs[-1]
        x, y, c = _mesh_pos()
        me = _dev(x, y, c)
        for i in range(n):
            for k, peer in enumerate(_peers(x, y, c)):
                pltpu.make_async_remote_copy(src_ref=src_refs[i].at[_dev(*peer)] if scatter else src_refs[i],
                                             dst_ref=land_refs[i].at[me], send_sem=send_sems.at[i * N_PEERS + k],
                                             recv_sem=recv_sems.at[i * N_PEERS + k], device_id=peer, device_id_type=MESH).start()
        token[...] = jnp.zeros_like(token)

    arrs = list(srcs) + list(lands)
    outs = _pc(body, name=name,
               out_shape=(pltpu.SemaphoreType.DMA((n * N_PEERS,)), pltpu.SemaphoreType.DMA((n * N_PEERS,)),
                          *[pltpu.HBM(a.shape, a.dtype) for a in arrs], S((SUBLANES, LANES), f32)),
               in_specs=[HBM_SPEC] * (2 * n),
               out_specs=(SEM_SPEC, SEM_SPEC, *[HBM_SPEC] * (2 * n), pl.BlockSpec(memory_space=pltpu.VMEM)),
               input_output_aliases={i: 2 + i for i in range(2 * n)},
               compiler_params=pltpu.CompilerParams(has_side_effects=DATAFLOW))(
        *[pltpu.with_memory_space_constraint(a, pltpu.HBM) for a in arrs])
    return (outs[0], outs[1], list(outs[2:2 + n]), list(outs[2 + n:2 + 2 * n]), scatter), outs[-1]


def _xchg_wait(handle, after, name):
    send_sems, recv_sems, srcs, lands, scatter = handle
    n = len(srcs)

    def body(*refs):
        src_refs, land_refs = refs[:n], refs[n:2 * n]
        send, recv = refs[2 * n], refs[2 * n + 1]
        x, y, c = _mesh_pos()
        for i in range(n):
            for k in range(N_PEERS):
                cp = pltpu.make_async_remote_copy(src_ref=src_refs[i].at[0] if scatter else src_refs[i],
                                                  dst_ref=land_refs[i].at[0], send_sem=send.at[i * N_PEERS + k],
                                                  recv_sem=recv.at[i * N_PEERS + k],
                                                  device_id=(x, y, c), device_id_type=MESH)
                cp.wait_send()
                cp.wait_recv()

    arrs = srcs + lands
    outs = _pc(body, name=name, out_shape=tuple(pltpu.HBM(a.shape, a.dtype) for a in arrs),
               in_specs=[HBM_SPEC] * (2 * n) + [SEM_SPEC, SEM_SPEC, pl.BlockSpec(memory_space=pl.ANY)],
               out_specs=tuple([HBM_SPEC] * (2 * n)), input_output_aliases={i: i for i in range(2 * n)},
               compiler_params=pltpu.CompilerParams(has_side_effects=DATAFLOW))(*arrs, send_sems, recv_sems, after)
    return list(outs[:n]), list(outs[n:])


def _rs_sum(g, land, me_vec, name):
    _, r, cols = g.shape
    tr = _divisor_block(r, 16, min(r, 352))

    def body(me_ref, g_ref, *rest):
        o_ref = rest[-1]
        acc = g_ref[0].astype(f32)
        for l_ref in rest[:-1]:
            acc = acc + l_ref[0].astype(f32)
        o_ref[...] = acc

    blk = lambda f: pl.BlockSpec((1, tr, cols), f)
    grid_spec = pltpu.PrefetchScalarGridSpec(
        num_scalar_prefetch=1, grid=(r // tr,),
        in_specs=[blk(lambda i, me_ref: (me_ref[0], i, 0))]
        + [blk(lambda i, me_ref, k=k: ((me_ref[0] + k) % N_DEV, i, 0)) for k in range(1, N_DEV)],
        out_specs=pl.BlockSpec((tr, cols), lambda i, me_ref: (i, 0)))
    return _pc(body, name=name, grid_spec=grid_spec, out_shape=S((r, cols), f32),
               compiler_params=_cparams(("arbitrary",)))(me_vec, g, *([land] * (N_DEV - 1)))


def _sum_devices(a, name):
    def body(a_ref, o_ref):
        acc = a_ref[0]
        for d in range(1, N_DEV):
            acc = acc + a_ref[d]
        o_ref[...] = acc

    return _pc(body, name=name, grid=(1,), in_specs=[_full(a.shape)], out_specs=_full(a.shape[1:]),
               out_shape=S(a.shape[1:], a.dtype), compiler_params=_cparams(("arbitrary",)))(a)


def _adamw(w, m, v, g, name):
    shape = w.shape
    w2, m2, v2, g2 = (a.reshape(-1, shape[-1]) for a in (w, m, v, g))
    rows, cols = w2.shape
    tr = rows if rows % SUBLANES else _divisor_block(rows, SUBLANES, max(SUBLANES, min(rows, ADAMW_BLOCK_ELEMS // cols)))
    c1, c2 = 1.0 - ADAM_B1 ** ADAM_STEP, 1.0 - ADAM_B2 ** ADAM_STEP

    def body(w_ref, m_ref, v_ref, g_ref, d_ref, nm_ref, nv_ref):
        gv = g_ref[...]
        nm = ADAM_B1 * m_ref[...] + (1.0 - ADAM_B1) * gv
        nv = ADAM_B2 * v_ref[...] + (1.0 - ADAM_B2) * (gv * gv)
        d_ref[...] = -ADAM_LR * ((nm / c1) / (jnp.sqrt(nv / c2) + ADAM_EPS) + ADAM_WD * w_ref[...])
        nm_ref[...] = nm
        nv_ref[...] = nv

    blk = pl.BlockSpec((tr, cols), lambda i: (i, 0))
    outs = _pc(body, name=name, grid=(rows // tr,), in_specs=[blk] * 4, out_specs=(blk,) * 3,
               out_shape=(S((rows, cols), f32),) * 3, compiler_params=_cparams(("arbitrary",)))(w2, m2, v2, g2)
    return tuple(o.reshape(shape) for o in outs)


_WEIGHTS = ("meta_tokens", "norm_mix", "norm_ffn", "norm_final", "ev_w_in", "ev_conv_a", "ev_ln_a_g", "ev_ln_a_b",
            "ev_conv_b", "ev_w_out", "od_w_in", "od_sinks", "od_mu", "od_w0", "od_w2", "od_a0", "od_a2", "od_g2",
            "od_k_k", "od_k_a", "od_r_k", "od_lnx_g", "od_lnx_b", "od_w_out", "ff_w_up", "ff_conv", "ff_conv_b", "ff_w_down")
_SMALL_SHARDED = (("meta_tokens", 1), ("ev_conv_a", 2), ("ev_conv_b", 2), ("od_mu", 1), ("od_w0", 1), ("od_w2", 2),
                  ("od_a0", 1), ("od_a2", 2), ("od_g2", 2), ("od_k_k", 1), ("od_k_a", 1), ("od_lnx_g", 1),
                  ("od_lnx_b", 1), ("ff_conv", 2))
_SMALL_REPLICATED = ("norm_mix", "norm_ffn", "norm_final", "ev_ln_a_g", "ev_ln_a_b", "od_sinks", "od_r_k", "ff_conv_b")
SLAB_UNIT = SUBLANES * LANES


def _pack(arrs):
    flat = jnp.concatenate([a.reshape(-1).astype(f32) for a in arrs])
    pad = (-flat.shape[0]) % SLAB_UNIT
    return jnp.pad(flat, (0, pad)).reshape(-1, LANES)


def _unpack(flat, shapes):
    out, off = [], 0
    for shp in shapes:
        size = 1
        for s in shp:
            size *= s
        out.append(flat[..., off:off + size].reshape(flat.shape[:-1] + tuple(shp)))
        off += size
    return out


def _full_shape(shape, axis):
    return tuple(N_DEV * s if i == axis else s for i, s in enumerate(shape))


def kernel(x, meta_tokens, norm_mix, norm_ffn, norm_final, ev_w_in, ev_conv_a, ev_ln_a_g, ev_ln_a_b, ev_conv_b, ev_w_out, od_w_in, od_sinks, od_mu, od_w0, od_w2, od_a0, od_a2, od_g2, od_k_k, od_k_a, od_r_k, od_lnx_g, od_lnx_b, od_w_out, ff_w_up, ff_conv, ff_conv_b, ff_w_down, loss_target, m_meta_tokens, m_norm_mix, m_norm_ffn, m_norm_final, m_ev_w_in, m_ev_conv_a, m_ev_ln_a_g, m_ev_ln_a_b, m_ev_conv_b, m_ev_w_out, m_od_w_in, m_od_sinks, m_od_mu, m_od_w0, m_od_w2, m_od_a0, m_od_a2, m_od_g2, m_od_k_k, m_od_k_a, m_od_r_k, m_od_lnx_g, m_od_lnx_b, m_od_w_out, m_ff_w_up, m_ff_conv, m_ff_conv_b, m_ff_w_down, v_meta_tokens, v_norm_mix, v_norm_ffn, v_norm_final, v_ev_w_in, v_ev_conv_a, v_ev_ln_a_g, v_ev_ln_a_b, v_ev_conv_b, v_ev_w_out, v_od_w_in, v_od_sinks, v_od_mu, v_od_w0, v_od_w2, v_od_a0, v_od_a2, v_od_g2, v_od_k_k, v_od_k_a, v_od_r_k, v_od_lnx_g, v_od_lnx_b, v_od_w_out, v_ff_w_up, v_ff_conv, v_ff_conv_b, v_ff_w_down):
    A = dict(locals())
    px, py, pc = _mesh_pos()
    me = _dev(px, py, pc)
    me_vec = jnp.reshape(me, (1,)).astype(jnp.int32)
    rows = lambda a: a.reshape(N_DEV * a.shape[1], a.shape[2])
    blocks = lambda a: a.reshape(N_DEV, a.shape[0] // N_DEV, a.shape[1])

    shards = dict(ev_in=ev_w_in[0].T, ev_out=ev_w_out[0], ff0_up=ff_w_up[0].T, ff0_down=ff_w_down[0], od_in=od_w_in[0].T,
                  od_out=od_w_out[0], ff1_up=ff_w_up[1].T, ff1_down=ff_w_down[1])
    shards = {n: b.astype(bf16) for n, b in shards.items()}
    small_shapes = [A[n].shape for n, _ in _SMALL_SHARDED]
    gathered = _all_gather([shards["ev_in"], shards["ev_out"], _pack([A[n] for n, _ in _SMALL_SHARDED])], "gather_first")
    gathered, shards = lax.optimization_barrier((gathered, shards))
    fetch, tok0 = {}, jnp.zeros((), f32)
    for n in ("ff0_up", "ff0_down", "od_in", "od_out", "ff1_up", "ff1_down"):
        shard, tok0 = lax.optimization_barrier((shards[n], tok0))
        land = lax.dynamic_update_slice(lax.empty((N_DEV,) + shard.shape, bf16), shard[None], (me, 0, 0))
        fetch[n], token = _xchg_start([shard], [land], False, f"gather_{n}_start")
        tok0 = tok0 + token[0, 0]

    def get_w(n, after):
        if n in ("ev_in", "ev_out"):
            return rows(gathered[("ev_in", "ev_out").index(n)])
        return rows(_xchg_wait(fetch[n], after, f"gather_{n}_wait")[1][0])

    W = {}
    for (n, ax), seg in zip(_SMALL_SHARDED, _unpack(gathered[-1].reshape(N_DEV, -1), small_shapes)):
        W[n] = jnp.moveaxis(seg, 0, ax).reshape(_full_shape(A[n].shape, ax))
    for n in ("ev_conv_a", "ev_conv_b", "od_w2", "od_a2", "od_g2"):
        W[n] = W[n][0]
    for n in _SMALL_REPLICATED:
        W[n] = A[n]
    W["od_r_k"] = od_r_k[0]

    small_shape = {n: _full_shape(A[n].shape, ax) for n, ax in _SMALL_SHARDED}
    small_shape.update({n: A[n].shape for n in _SMALL_REPLICATED})
    sent, small_sent, small_names = {}, {}, {}

    def put_g(n, g):
        g8 = blocks(g)
        sent[n], token = _xchg_start([g8], [lax.empty(g8.shape, g8.dtype)], True, f"reduce_{n}_start")
        return token[0, 0]

    def put_small(gs, stage="early"):
        small_names[stage] = sorted(gs)
        slab = _pack([gs[n] for n in small_names[stage]])
        land = lax.dynamic_update_slice(lax.empty((N_DEV,) + slab.shape, f32), slab[None], (me, 0, 0))
        small_sent[stage], small_tok[stage] = _xchg_start([slab], [land], False, f"gather_{stage}_small_grads_start")
        return small_tok[stage][0, 0]

    small_tok = {}
    loss_tile, grad_x, late = _local_step(x[0], loss_target[0], W, get_w, put_g, put_small, tok0)
    put_small(late, "late")
    late_tok = small_tok["late"]

    gsh, prev = {}, late_tok
    for n in ("ff1_down", "ff1_up", "od_out", "od_in", "ff0_down", "ff0_up", "ev_out", "ev_in"):
        srcs, lands = _xchg_wait(sent[n], prev, f"reduce_{n}_wait")
        gsh[n] = prev = _rs_sum(srcs[0], lands[0], me_vec, f"reduce_{n}_sum")
    grads = dict(ev_w_in=gsh["ev_in"].T[None], ev_w_out=gsh["ev_out"][None], od_w_in=gsh["od_in"].T[None],
                 od_w_out=gsh["od_out"][None], ff_w_up=jnp.stack([gsh["ff0_up"].T, gsh["ff1_up"].T]),
                 ff_w_down=jnp.stack([gsh["ff0_down"], gsh["ff1_down"]]))

    delta, new_m, new_v = {}, {}, {}
    for n in ("ff_w_up", "ff_w_down", "od_w_in", "od_w_out", "ev_w_in", "ev_w_out"):
        delta[n], new_m[n], new_v[n] = _adamw(A[n], A["m_" + n], A["v_" + n], grads[n], "adamw_" + n)
    for stage in ("early", "late"):
        gsm = _xchg_wait(small_sent[stage], delta["ev_w_in"], f"gather_{stage}_small_grads_wait")[1][0]
        summed = _sum_devices(gsm, f"sum_{stage}_small_grads").reshape(-1)
        for n, full in zip(small_names[stage], _unpack(summed, [small_shape[n] for n in small_names[stage]])):
            grads[n] = full
    for n, ax in _SMALL_SHARDED:
        size = A[n].shape[ax]
        grads[n] = lax.dynamic_slice_in_dim(grads[n], me * size, size, axis=ax)
    for n in small_shape:
        delta[n], new_m[n], new_v[n] = _adamw(A[n], A["m_" + n], A["v_" + n], grads[n], "adamw_" + n)

    loss = lax.psum(loss_tile[0, 0], ("x", "y", "c"))
    return (loss, grad_x[None], *[grads[n] for n in _WEIGHTS], *[delta[n] for n in _WEIGHTS],
            *[new_m[n] for n in _WEIGHTS], *[new_v[n] for n in _WEIGHTS])
```

```python
import jax
import jax.numpy as jnp
from jax import lax
from jax.experimental import pallas as pl
from jax.experimental.pallas import tpu as pltpu

f32, bf16 = jnp.float32, jnp.bfloat16

D_MODEL = 1024
N_META = 16
RMS_EPS = 1e-6
LN_EPS = 1e-5
D_A = 512
CONV_A_WIDTH = 31
CONV_B_WIDTH = 3
HEAD_DIM = 64
N_Q_HEADS = 8
N_KV_HEADS = 2
GQA_GROUP = 4
D_ATT = 512
D_KV = 128
BLOCK = 128
ROPE_THETA = 10000.0
D_R = 512
LORA_W, LORA_A, LORA_G = 64, 64, 128
RWKV_GN_EPS = 64e-5
ATT_COLS = D_ATT + 2 * D_KV
RWKV_COLS = 3 * D_R + LORA_W + LORA_A + LORA_G
D_FF = 2816
FF_CONV_WIDTH = 3
FF_BLOCK = 256
NEG_INF = -1e30
ATT_PAD = BLOCK - N_META
ATT_SCALE = HEAD_DIM ** -0.5

ADAM_LR, ADAM_B1, ADAM_B2, ADAM_EPS, ADAM_WD, ADAM_STEP = 0.001, 0.9, 0.999, 1e-08, 0.01, 10

N_DEV = 8
LANES = 128
SUBLANES = 8
SCAN_CHUNK = 48
PAIR_ROWS = 4 * HEAD_DIM
V7X_VMEM_LIMIT = 56 * 1024 * 1024
ADAMW_BLOCK_ELEMS = 400 * 1024
GRAD_WIRE_DTYPE = bf16
MESH = pl.DeviceIdType.MESH
S = jax.ShapeDtypeStruct
HIGHEST = lax.Precision.HIGHEST


def _pc(body, **kw):
    return pl.pallas_call(body, **kw)


def _cparams(sem=None):
    return pltpu.CompilerParams(dimension_semantics=sem, vmem_limit_bytes=V7X_VMEM_LIMIT)


def _divisor_block(t, unit, limit):
    best = unit
    for rb in range(unit, limit + 1, unit):
        if t % rb == 0:
            best = rb
    assert t % best == 0, (t, unit)
    return best


def _row_block(t):
    return _divisor_block(t, 16, 704)


def _row_block8(t):
    return _divisor_block(t, 8, 344)


def _col_tile(n, cap):
    return _divisor_block(n, LANES, min(n, cap)) if n % LANES == 0 else n


def _full(shape):
    nd = len(shape)
    return pl.BlockSpec(shape, lambda *_: (0,) * nd)


def _sigmoid(x):
    return jax.nn.sigmoid(x)


_DIMS = {"nn": (((1,), (0,)), ((), ())), "nt": (((1,), (1,)), ((), ())), "tn": (((0,), (0,)), ((), ()))}
MM_MAX_K = 2816
MM_MAX_TM = 704
MM_MAX_TN = 1408


def _mm(a, b, mode, name, out_dtype=f32, res=None):
    if mode == "nn":
        (m, k), (k2, n) = a.shape, b.shape
    elif mode == "nt":
        (m, k), (n, k2) = a.shape, b.shape
    else:
        (k, m), (k2, n) = a.shape, b.shape
    assert k == k2, (a.shape, b.shape, mode)
    tm = _row_block(m) if m % LANES else _col_tile(m, MM_MAX_TM)
    tn = _col_tile(n, MM_MAX_TN)
    nk = 1 if (mode == "tn" or k <= MM_MAX_K) else k // MM_MAX_K
    tk = k // nk
    assert tk * nk == k
    dims = _DIMS[mode]

    def body(a_ref, b_ref, *rest):
        part = lax.dot_general(a_ref[...].astype(bf16), b_ref[...].astype(bf16), dims, preferred_element_type=f32)
        if nk == 1:
            o_ref = rest[-1]
            if res is not None:
                part = part + rest[0][...]
            o_ref[...] = part.astype(out_dtype)
            return
        o_ref, acc_ref = rest[-2], rest[-1]
        kk = pl.program_id(2)

        @pl.when(kk == 0)
        def _():
            acc_ref[...] = part

        @pl.when(kk > 0)
        def _():
            acc_ref[...] += part

        @pl.when(kk == nk - 1)
        def _():
            acc = acc_ref[...]
            if res is not None:
                acc = acc + rest[0][...]
            o_ref[...] = acc.astype(out_dtype)

    if mode == "tn":
        a_spec = pl.BlockSpec((k, tm), lambda i, j, kk: (0, i))
    else:
        a_spec = pl.BlockSpec((tm, tk), lambda i, j, kk: (i, kk))
    if mode == "nt":
        b_spec = pl.BlockSpec((tn, tk), lambda i, j, kk: (j, kk))
    else:
        b_spec = pl.BlockSpec((tk, tn), lambda i, j, kk: (kk, j))
    o_spec = pl.BlockSpec((tm, tn), lambda i, j, kk: (i, j))
    ins, specs = [a, b], [a_spec, b_spec]
    if res is not None:
        ins.append(res)
        specs.append(o_spec)
    scratch = [pltpu.VMEM((tm, tn), f32)] if nk > 1 else []
    return _pc(body, name=name, grid=(m // tm, n // tn, nk), in_specs=specs, out_specs=o_spec,
               out_shape=S((m, n), out_dtype), scratch_shapes=scratch,
               compiler_params=_cparams(("arbitrary", "arbitrary", "arbitrary")))(*ins)


def _rms_fwd(x, g, name):
    t, d = x.shape
    rb = _row_block(t)

    def body(x_ref, g_ref, o_ref):
        xv = x_ref[...]
        rstd = lax.rsqrt(jnp.mean(xv * xv, axis=-1, keepdims=True) + RMS_EPS)
        o_ref[...] = (xv * rstd * g_ref[...]).astype(bf16)

    row = pl.BlockSpec((rb, d), lambda i: (i, 0))
    return _pc(body, name=name, grid=(t // rb,), in_specs=[row, _full((1, d))], out_specs=row,
               out_shape=S((t, d), bf16), compiler_params=_cparams(("arbitrary",)))(x, g.reshape(1, d))


def _rms_bwd(dy, x, g, dres, name):
    t, d = x.shape
    rb = _row_block8(t)

    def body(dy_ref, x_ref, g_ref, dres_ref, dx_ref, dg_ref):
        @pl.when(pl.program_id(0) == 0)
        def _():
            dg_ref[...] = jnp.zeros_like(dg_ref)
        xv, dyv = x_ref[...], dy_ref[...]
        rstd = lax.rsqrt(jnp.mean(xv * xv, axis=-1, keepdims=True) + RMS_EPS)
        xn = xv * rstd
        dg_ref[...] += jnp.sum(dyv * xn, axis=0, keepdims=True)
        dxh = dyv * g_ref[...]
        dx_ref[...] = dres_ref[...] + rstd * (dxh - xn * jnp.mean(dxh * xn, axis=-1, keepdims=True))

    row = pl.BlockSpec((rb, d), lambda i: (i, 0))
    return _pc(body, name=name, grid=(t // rb,), in_specs=[row, row, _full((1, d)), row],
               out_specs=(row, _full((1, d))), out_shape=(S((t, d), f32), S((1, d), f32)),
               compiler_params=_cparams(("arbitrary",)))(dy, x, g.reshape(1, d), dres)


def _final_loss(h, g, target_padded):
    t, d = h.shape
    rb = _row_block8(t)

    def body(x_ref, g_ref, t_ref, loss_ref, dx_ref, dg_ref):
        i = pl.program_id(0)

        @pl.when(i == 0)
        def _():
            dg_ref[...] = jnp.zeros_like(dg_ref)
            loss_ref[...] = jnp.zeros_like(loss_ref)
        xv = x_ref[...]
        rstd = lax.rsqrt(jnp.mean(xv * xv, axis=-1, keepdims=True) + RMS_EPS)
        xn = xv * rstd
        gv = g_ref[...]
        row = i * rb + lax.broadcasted_iota(jnp.int32, (rb, 1), 0)
        diff = jnp.where(row >= N_META, xn * gv - t_ref[...], 0.0)
        loss_ref[...] += 0.5 * jnp.sum(jnp.mean(diff * diff, axis=-1, keepdims=True))
        dout = diff * (1.0 / d)
        dg_ref[...] += jnp.sum(dout * xn, axis=0, keepdims=True)
        dxh = dout * gv
        dx_ref[...] = rstd * (dxh - xn * jnp.mean(dxh * xn, axis=-1, keepdims=True))

    row = pl.BlockSpec((rb, d), lambda i: (i, 0))
    return _pc(body, name="final_loss", grid=(t // rb,), in_specs=[row, _full((1, d)), row],
               out_specs=(_full((SUBLANES, LANES)), row, _full((1, d))),
               out_shape=(S((SUBLANES, LANES), f32), S((t, d), f32), S((1, d), f32)),
               compiler_params=_cparams(("arbitrary",)))(h, g.reshape(1, d), target_padded)


CONV_LEAD = 32


def _fill_front_padded(pad_ref, x, t):
    pad_ref[0:CONV_LEAD, :] = jnp.zeros((CONV_LEAD, x.shape[1]), f32)
    pad_ref[CONV_LEAD:CONV_LEAD + t, :] = x


def _fill_back_padded(pad_ref, x, t):
    pad_ref[0:t, :] = x
    pad_ref[t:t + CONV_LEAD, :] = jnp.zeros((CONV_LEAD, x.shape[1]), f32)


def _conv_rows(pad_ref, w_ref, kw, r0, nr):
    acc = None
    for j in range(kw):
        lo = CONV_LEAD + r0 - (kw - 1) + j
        term = w_ref[j:j + 1, :] * pad_ref[lo:lo + nr, :]
        acc = term if acc is None else acc + term
    return acc


def _conv_t_rows(padb_ref, w_ref, kw, r0, nr):
    acc = None
    for j in range(kw):
        lo = r0 + (kw - 1) - j
        term = w_ref[j:j + 1, :] * padb_ref[lo:lo + nr, :]
        acc = term if acc is None else acc + term
    return acc


def _conv_dw_rows(dy_blk, pad_ref, kw, r0, nr):
    out = []
    for j in range(kw):
        lo = CONV_LEAD + r0 - (kw - 1) + j
        out.append(jnp.sum(dy_blk * pad_ref[lo:lo + nr, :], axis=0, keepdims=True))
    return out


def _acc_list(a, b):
    return b if a is None else [x + y for x, y in zip(a, b)]


def _ev_a_conv(p, conv_a):
    t = p.shape[0]
    cr = _row_block8(t)
    nb = D_A // LANES

    def body(av_ref, ag_ref, w_ref, o_ref, pad_ref):
        _fill_front_padded(pad_ref, av_ref[...] * _sigmoid(ag_ref[...]), t)
        for r in range(t // cr):
            o_ref[r * cr:(r + 1) * cr, :] = _conv_rows(pad_ref, w_ref, CONV_A_WIDTH, r * cr, cr)

    col = lambda off: pl.BlockSpec((t, LANES), lambda j: (0, j + off))
    return _pc(body, name="ev_a_conv", grid=(nb,),
               in_specs=[col(0), col(nb), pl.BlockSpec((CONV_A_WIDTH, LANES), lambda j: (0, j))],
               out_specs=col(0), out_shape=S((t, D_A), f32),
               scratch_shapes=[pltpu.VMEM((t + CONV_LEAD, LANES), f32)],
               compiler_params=_cparams(("arbitrary",)))(p, p, conv_a)


def _ln_silu(uc, g, b):
    mu = jnp.mean(uc, axis=-1, keepdims=True)
    xc = uc - mu
    var = jnp.mean(xc * xc, axis=-1, keepdims=True)
    y = xc * lax.rsqrt(var + LN_EPS) * g + b
    return y * _sigmoid(y)


def _ev_a_norm(uc, g, b):
    t, d = uc.shape
    rb = _row_block(t)

    def body(u_ref, g_ref, b_ref, o_ref):
        o_ref[...] = _ln_silu(u_ref[...], g_ref[...], b_ref[...]).astype(bf16)

    row = pl.BlockSpec((rb, d), lambda i: (i, 0))
    return _pc(body, name="ev_a_norm", grid=(t // rb,), in_specs=[row, _full((1, d)), _full((1, d))],
               out_specs=row, out_shape=S((t, 2 * d), bf16), compiler_params=_cparams(("arbitrary",)))(uc, g, b)


def _ev_a_norm_bwd(dy, uc, g, b):
    t, d = uc.shape
    rb = _row_block8(t)

    def body(dy_ref, u_ref, g_ref, b_ref, du_ref, dg_ref, db_ref):
        @pl.when(pl.program_id(0) == 0)
        def _():
            dg_ref[...] = jnp.zeros_like(dg_ref)
            db_ref[...] = jnp.zeros_like(db_ref)
        _, vjp = jax.vjp(_ln_silu, u_ref[...], g_ref[...], b_ref[...])
        du, dg, db = vjp(dy_ref[...])
        du_ref[...] = du
        dg_ref[...] += dg
        db_ref[...] += db

    row = pl.BlockSpec((rb, d), lambda i: (i, 0))
    return _pc(body, name="ev_a_norm_bwd", grid=(t // rb,), in_specs=[row, row, _full((1, d)), _full((1, d))],
               out_specs=(row, _full((1, d)), _full((1, d))),
               out_shape=(S((t, d), f32), S((1, d), f32), S((1, d), f32)),
               compiler_params=_cparams(("arbitrary",)))(dy, uc, g, b)


def _ev_a_conv_bwd(duc, p, conv_a):
    t = p.shape[0]
    cr = _row_block8(t)
    nb = D_A // LANES

    def body(dy_ref, av_ref, ag_ref, w_ref, dav_ref, dag_ref, dw_ref, pad_ref, padb_ref):
        _fill_front_padded(pad_ref, av_ref[...] * _sigmoid(ag_ref[...]), t)
        _fill_back_padded(padb_ref, dy_ref[...], t)
        dw = None
        for r in range(t // cr):
            rows = slice(r * cr, (r + 1) * cr)
            du = _conv_t_rows(padb_ref, w_ref, CONV_A_WIDTH, r * cr, cr)
            avr = av_ref[rows, :]
            sgr = _sigmoid(ag_ref[rows, :])
            dav_ref[rows, :] = du * sgr
            dag_ref[rows, :] = du * avr * sgr * (1.0 - sgr)
            dw = _acc_list(dw, _conv_dw_rows(dy_ref[rows, :], pad_ref, CONV_A_WIDTH, r * cr, cr))
        for j in range(CONV_A_WIDTH):
            dw_ref[j:j + 1, :] = dw[j]

    col = lambda off: pl.BlockSpec((t, LANES), lambda j: (0, j + off))
    wsp = pl.BlockSpec((CONV_A_WIDTH, LANES), lambda j: (0, j))
    return _pc(body, name="ev_a_conv_bwd", grid=(nb,), in_specs=[col(0), col(0), col(nb), wsp],
               out_specs=(col(0), col(0), wsp),
               out_shape=(S((t, D_A), f32), S((t, D_A), f32), S((CONV_A_WIDTH, D_A), f32)),
               scratch_shapes=[pltpu.VMEM((t + CONV_LEAD, LANES), f32), pltpu.VMEM((t + CONV_LEAD, LANES), f32)],
               compiler_params=_cparams(("arbitrary",)))(duc, p, p, conv_a)


def _ev_b(p, conv_b, y):
    t = p.shape[0]
    cr = _row_block8(t)
    nb = D_A // LANES

    def body(gb_ref, gc_ref, xi_ref, w_ref, y_ref, o_ref, pad_ref, stage_ref):
        _fill_front_padded(pad_ref, gc_ref[...] * xi_ref[...], t)
        for r in range(t // cr):
            rows = slice(r * cr, (r + 1) * cr)
            stage_ref[rows, :] = gb_ref[rows, :] * _conv_rows(pad_ref, w_ref, CONV_B_WIDTH, r * cr, cr)
        o_ref[...] = stage_ref[...].astype(bf16)

    col = lambda off: pl.BlockSpec((t, LANES), lambda j: (0, j + off))
    return _pc(body, name="ev_b", grid=(nb,),
               in_specs=[col(2 * nb), col(3 * nb), col(4 * nb), pl.BlockSpec((CONV_B_WIDTH, LANES), lambda j: (0, j)), HBM],
               out_specs=col(nb), out_shape=S(y.shape, bf16), input_output_aliases={4: 0},
               scratch_shapes=[pltpu.VMEM((t + CONV_LEAD, LANES), f32), pltpu.VMEM((t, LANES), f32)],
               compiler_params=_cparams(("arbitrary",)))(p, p, p, conv_b, y)


def _ev_b_bwd(dy, p, conv_b):
    t = p.shape[0]
    cr = _row_block8(t)
    nb = D_A // LANES

    def body(dy_ref, gb_ref, gc_ref, xi_ref, w_ref, dgb_ref, dgc_ref, dxi_ref, dw_ref, pad_ref, padb_ref):
        _fill_front_padded(pad_ref, gc_ref[...] * xi_ref[...], t)
        _fill_back_padded(padb_ref, dy_ref[...] * gb_ref[...], t)
        dw = None
        for r in range(t // cr):
            rows = slice(r * cr, (r + 1) * cr)
            dgb_ref[rows, :] = dy_ref[rows, :] * _conv_rows(pad_ref, w_ref, CONV_B_WIDTH, r * cr, cr)
            dcx = _conv_t_rows(padb_ref, w_ref, CONV_B_WIDTH, r * cr, cr)
            dgc_ref[rows, :] = dcx * xi_ref[rows, :]
            dxi_ref[rows, :] = dcx * gc_ref[rows, :]
            dw = _acc_list(dw, _conv_dw_rows(padb_ref[rows, :], pad_ref, CONV_B_WIDTH, r * cr, cr))
        for j in range(CONV_B_WIDTH):
            dw_ref[j:j + 1, :] = dw[j]

    col = lambda off: pl.BlockSpec((t, LANES), lambda j: (0, j + off))
    wsp = pl.BlockSpec((CONV_B_WIDTH, LANES), lambda j: (0, j))
    return _pc(body, name="ev_b_bwd", grid=(nb,), in_specs=[col(nb), col(2 * nb), col(3 * nb), col(4 * nb), wsp],
               out_specs=(col(0), col(0), col(0), wsp),
               out_shape=(S((t, D_A), f32), S((t, D_A), f32), S((t, D_A), f32), S((CONV_B_WIDTH, D_A), f32)),
               scratch_shapes=[pltpu.VMEM((t + CONV_LEAD, LANES), f32), pltpu.VMEM((t + CONV_LEAD, LANES), f32)],
               compiler_params=_cparams(("arbitrary",)))(dy, p, p, p, conv_b)


def _ffn_mid(u, conv_w, conv_b, name):
    t = u.shape[0]
    cr = _row_block8(t)
    nb = D_FF // FF_BLOCK

    def one(gt_ref, vl_ref, w_ref, b_ref, o_ref, pad_ref, stage_ref):
        _fill_front_padded(pad_ref, gt_ref[...], t)
        for r in range(t // cr):
            rows = slice(r * cr, (r + 1) * cr)
            gc = _conv_rows(pad_ref, w_ref, FF_CONV_WIDTH, r * cr, cr) + b_ref[...]
            stage_ref[rows, :] = gc * _sigmoid(gc) * vl_ref[rows, :]
        o_ref[...] = stage_ref[...].astype(bf16)

    def body(*refs):
        for h in range(FF_BLOCK // LANES):
            one(*[r.at[:, pl.ds(h * LANES, LANES)] for r in refs[:5]], *refs[5:])

    col = lambda off: pl.BlockSpec((t, FF_BLOCK), lambda j: (0, j + off))
    return _pc(body, name=name, grid=(nb,),
               in_specs=[col(0), col(nb), pl.BlockSpec((FF_CONV_WIDTH, FF_BLOCK), lambda j: (0, j)),
                         pl.BlockSpec((1, FF_BLOCK), lambda j: (0, j))],
               out_specs=col(0), out_shape=S((t, D_FF), bf16),
               scratch_shapes=[pltpu.VMEM((t + CONV_LEAD, LANES), f32), pltpu.VMEM((t, LANES), f32)],
               compiler_params=_cparams(("arbitrary",)))(u, u, conv_w, conv_b.reshape(1, D_FF))


def _ffn_mid_bwd(dz, u, conv_w, conv_b, name):
    t = u.shape[0]
    cr = _row_block8(t)
    nb = D_FF // FF_BLOCK
    nh = FF_BLOCK // LANES

    def body(*refs):
        for h in range(nh):
            one(*[r.at[:, pl.ds(h * LANES, LANES)] for r in refs[:8]], refs[8], refs[9], refs[10].at[h])

    def one(dz_ref, gt_ref, vl_ref, w_ref, b_ref, du_ref, dw_ref, db_ref, pad_ref, padb_ref, dval_ref):
        s = pl.program_id(1)

        @pl.when(s == 0)
        def _():
            _fill_front_padded(pad_ref, gt_ref[...], t)
            dw, db = None, None
            for r in range(t // cr):
                rows = slice(r * cr, (r + 1) * cr)
                lo = CONV_LEAD + r * cr - (FF_CONV_WIDTH - 1)
                taps = [pad_ref[lo + j:lo + j + cr, :] for j in range(FF_CONV_WIDTH)]
                gc = sum(w_ref[j:j + 1, :] * taps[j] for j in range(FF_CONV_WIDTH)) + b_ref[...]
                sg = _sigmoid(gc)
                dzr = dz_ref[rows, :]
                dval_ref[rows, :] = dzr * gc * sg
                dgc = dzr * vl_ref[rows, :] * sg * (1.0 + gc * (1.0 - sg))
                padb_ref[rows, :] = dgc
                dw = _acc_list(dw, [jnp.sum(dgc * tap, axis=0, keepdims=True) for tap in taps])
                pb = jnp.sum(dgc, axis=0, keepdims=True)
                db = pb if db is None else db + pb
            padb_ref[t:t + CONV_LEAD, :] = jnp.zeros((CONV_LEAD, LANES), f32)
            for r in range(t // cr):
                pad_ref[r * cr:(r + 1) * cr, :] = _conv_t_rows(padb_ref, w_ref, FF_CONV_WIDTH, r * cr, cr)
            du_ref[...] = pad_ref[0:t, :].astype(du_ref.dtype)
            for j in range(FF_CONV_WIDTH):
                dw_ref[j:j + 1, :] = dw[j]
            db_ref[...] = db

        @pl.when(s == 1)
        def _():
            du_ref[...] = dval_ref[...].astype(du_ref.dtype)

    col = lambda off: pl.BlockSpec((t, FF_BLOCK), lambda j, s: (0, j + off))
    wsp = pl.BlockSpec((FF_CONV_WIDTH, FF_BLOCK), lambda j, s: (0, j))
    bsp = pl.BlockSpec((1, FF_BLOCK), lambda j, s: (0, j))
    return _pc(body, name=name, grid=(nb, 2), in_specs=[col(0), col(0), col(nb), wsp, bsp],
               out_specs=(pl.BlockSpec((t, FF_BLOCK), lambda j, s: (0, s * nb + j)), wsp, bsp),
               out_shape=(S((t, 2 * D_FF), bf16), S((FF_CONV_WIDTH, D_FF), f32), S((1, D_FF), f32)),
               scratch_shapes=[pltpu.VMEM((t + CONV_LEAD, LANES), f32), pltpu.VMEM((t + CONV_LEAD, LANES), f32),
                               pltpu.VMEM((nh, t, LANES), f32)],
               compiler_params=_cparams(("arbitrary", "arbitrary")))(dz, u, u, conv_w, conv_b.reshape(1, D_FF))


def _swap_halves(x):
    w = x.shape[1]
    lane = lax.broadcasted_iota(jnp.int32, x.shape, 1) % HEAD_DIM
    return jnp.where(lane < HEAD_DIM // 2, pltpu.roll(x, w - HEAD_DIM // 2, axis=1), pltpu.roll(x, HEAD_DIM // 2, axis=1))


def _rope_pack(patt, c64, s64):
    t = patt.shape[0]
    tp = t + ATT_PAD

    def body(p_ref, c_ref, s_ref, q_ref, k_ref, v_ref):
        c, s = c_ref[...], s_ref[...]

        def rope(x, nh):
            cc = jnp.concatenate([c] * nh, axis=1)
            ss = jnp.concatenate([s] * nh, axis=1)
            return x * cc + _swap_halves(x) * ss

        for ref, val in ((q_ref, rope(p_ref[:, 0:D_ATT], N_Q_HEADS)),
                         (k_ref, rope(p_ref[:, D_ATT:D_ATT + D_KV], N_KV_HEADS)),
                         (v_ref, p_ref[:, D_ATT + D_KV:ATT_COLS])):
            ref[0:ATT_PAD, :] = jnp.zeros((ATT_PAD, val.shape[1]), bf16)
            ref[ATT_PAD:tp, :] = val.astype(bf16)

    return _pc(body, name="rope_pack", in_specs=[_full((t, ATT_COLS)), _full((t, HEAD_DIM)), _full((t, HEAD_DIM))],
               out_specs=(_full((tp, D_ATT)), _full((tp, D_KV)), _full((tp, D_KV))), grid=(1,),
               out_shape=(S((tp, D_ATT), bf16), S((tp, D_KV), bf16), S((tp, D_KV), bf16)),
               compiler_params=_cparams(("arbitrary",)))(patt, c64, s64)


def _rope_bwd(dqp, dkp, dvp, c64, s64):
    tp = dqp.shape[0]
    t = tp - ATT_PAD

    def body(dq_ref, dk_ref, dv_ref, c_ref, s_ref, o_ref):
        c, s = c_ref[...], s_ref[...]

        def unrope(dy, nh):
            cc = jnp.concatenate([c] * nh, axis=1)
            ss = jnp.concatenate([s] * nh, axis=1)
            return dy * cc + _swap_halves(dy * ss)

        o_ref[:, 0:D_ATT] = unrope(dq_ref[ATT_PAD:tp, :], N_Q_HEADS).astype(bf16)
        o_ref[:, D_ATT:D_ATT + D_KV] = unrope(dk_ref[ATT_PAD:tp, :], N_KV_HEADS).astype(bf16)
        o_ref[:, D_ATT + D_KV:ATT_COLS] = dv_ref[ATT_PAD:tp, :].astype(bf16)

    return _pc(body, name="rope_bwd", grid=(1,),
               in_specs=[_full((tp, D_ATT)), _full((tp, D_KV)), _full((tp, D_KV)), _full((t, HEAD_DIM)), _full((t, HEAD_DIM))],
               out_specs=_full((t, ATT_COLS)), out_shape=S((t, ATT_COLS), bf16),
               compiler_params=_cparams(("arbitrary",)))(dqp, dkp, dvp, c64, s64)


def _attn_masks(n):
    rows = GQA_GROUP * BLOCK
    ri = lax.broadcasted_iota(jnp.int32, (rows, BLOCK), 0) % BLOCK
    ci = lax.broadcasted_iota(jnp.int32, (rows, BLOCK), 1)
    m_cur = (ci <= ri) & (ci >= jnp.where(n >= 1, 0, ATT_PAD))
    m_prev = ci > ri + jnp.where(n >= 2, 0, BLOCK)
    m_meta = ci >= jnp.where(n >= 1, ATT_PAD, BLOCK)
    return m_cur, m_prev, m_meta


def _attn_probs(qg, kc, kp, km, masks, skv):
    def scores(k, m):
        s = lax.dot_general(qg, k, _DIMS["nt"], preferred_element_type=f32) * ATT_SCALE
        return jnp.where(m, s, NEG_INF)
    s_c, s_p, s_m = scores(kc, masks[0]), scores(kp, masks[1]), scores(km, masks[2])
    mx = jnp.maximum(jnp.maximum(jnp.max(s_c, axis=-1, keepdims=True), jnp.max(s_p, axis=-1, keepdims=True)),
                     jnp.maximum(jnp.max(s_m, axis=-1, keepdims=True), skv))
    e_c, e_p, e_m, e_s = jnp.exp(s_c - mx), jnp.exp(s_p - mx), jnp.exp(s_m - mx), jnp.exp(skv - mx)
    den = (jnp.sum(e_c, axis=-1, keepdims=True) + jnp.sum(e_p, axis=-1, keepdims=True)
           + jnp.sum(e_m, axis=-1, keepdims=True) + e_s)
    inv = 1.0 / den
    return e_c * inv, e_p * inv, e_m * inv, e_s * inv


def _sink_rows(sk_ref, g):
    hrow = lax.broadcasted_iota(jnp.int32, (GQA_GROUP * BLOCK, 1), 0) // BLOCK
    skv = jnp.zeros((GQA_GROUP * BLOCK, 1), f32)
    for hh in range(GQA_GROUP):
        skv = jnp.where(hrow == hh, sk_ref[0, GQA_GROUP * g + hh], skv)
    return skv, hrow


def _stack_heads(ref, g):
    return jnp.concatenate([ref[:, (GQA_GROUP * g + hh) * HEAD_DIM:(GQA_GROUP * g + hh + 1) * HEAD_DIM]
                            for hh in range(GQA_GROUP)], axis=0)


def _attn_specs():
    blk = lambda w: pl.BlockSpec((BLOCK, w), lambda n: (n, 0))
    prev = pl.BlockSpec((BLOCK, D_KV), lambda n: (jnp.maximum(n - 1, 0), 0))
    meta = pl.BlockSpec((BLOCK, D_KV), lambda n: (0, 0))
    return blk, prev, meta


def _attn_fwd(qp, kp, vp, sinks):
    tp = qp.shape[0]
    blk, prev, meta = _attn_specs()

    def body(sk_ref, q_ref, kc_ref, kp_ref, km_ref, vc_ref, vp_ref, vm_ref, o_ref):
        masks = _attn_masks(pl.program_id(0))
        for g in range(N_KV_HEADS):
            sl = slice(g * HEAD_DIM, (g + 1) * HEAD_DIM)
            skv, _ = _sink_rows(sk_ref, g)
            p_c, p_p, p_m, _ = _attn_probs(_stack_heads(q_ref, g), kc_ref[:, sl], kp_ref[:, sl], km_ref[:, sl], masks, skv)
            o = (jnp.dot(p_c.astype(bf16), vc_ref[:, sl], preferred_element_type=f32)
                 + jnp.dot(p_p.astype(bf16), vp_ref[:, sl], preferred_element_type=f32)
                 + jnp.dot(p_m.astype(bf16), vm_ref[:, sl], preferred_element_type=f32))
            for hh in range(GQA_GROUP):
                h = GQA_GROUP * g + hh
                o_ref[:, h * HEAD_DIM:(h + 1) * HEAD_DIM] = o[hh * BLOCK:(hh + 1) * BLOCK].astype(bf16)

    return _pc(body, name="attn_fwd", grid=(tp // BLOCK,),
               in_specs=[pl.BlockSpec(memory_space=pltpu.SMEM), blk(D_ATT), blk(D_KV), prev, meta, blk(D_KV), prev, meta],
               out_specs=blk(D_ATT), out_shape=S((tp, D_ATT), bf16),
               compiler_params=_cparams(("arbitrary",)))(sinks, qp, kp, kp, kp, vp, vp, vp)


def _attn_bwd(qp, kp, vp, sinks, dop):
    tp = qp.shape[0]
    blk, prev, meta = _attn_specs()

    def body(sk_ref, q_ref, kc_ref, kp_ref, km_ref, vc_ref, vp_ref, vm_ref, do_ref, dq_ref, dk_ref, dv_ref, dsk_ref):
        n = pl.program_id(0)

        @pl.when(n == 0)
        def _():
            dk_ref[...] = jnp.zeros_like(dk_ref)
            dv_ref[...] = jnp.zeros_like(dv_ref)
            dsk_ref[...] = jnp.zeros_like(dsk_ref)
        masks = _attn_masks(n)
        cur = pl.ds(pl.multiple_of(n * BLOCK, BLOCK), BLOCK)
        prv = pl.ds(pl.multiple_of(jnp.maximum(n - 1, 0) * BLOCK, BLOCK), BLOCK)
        lane = lax.broadcasted_iota(jnp.int32, (1, LANES), 1)
        dsk = jnp.zeros((1, LANES), f32)
        for g in range(N_KV_HEADS):
            sl = slice(g * HEAD_DIM, (g + 1) * HEAD_DIM)
            skv, hrow = _sink_rows(sk_ref, g)
            qg = _stack_heads(q_ref, g)
            dog = _stack_heads(do_ref, g)
            ks = (kc_ref[:, sl], kp_ref[:, sl], km_ref[:, sl])
            vs = (vc_ref[:, sl], vp_ref[:, sl], vm_ref[:, sl])
            probs = _attn_probs(qg, ks[0], ks[1], ks[2], masks, skv)
            dps = [lax.dot_general(dog, v, _DIMS["nt"], preferred_element_type=f32) for v in vs]
            delta = sum(jnp.sum(p * dp, axis=-1, keepdims=True) for p, dp in zip(probs[:3], dps))
            dss = [(p * (dp - delta) * ATT_SCALE).astype(bf16) for p, dp in zip(probs[:3], dps)]
            dq = sum(jnp.dot(ds, k, preferred_element_type=f32) for ds, k in zip(dss, ks))
            for hh in range(GQA_GROUP):
                h = GQA_GROUP * g + hh
                dq_ref[:, h * HEAD_DIM:(h + 1) * HEAD_DIM] = dq[hh * BLOCK:(hh + 1) * BLOCK]
                dsk = dsk + jnp.where(lane == h, -jnp.sum(jnp.where(hrow == hh, probs[3] * delta, 0.0)), 0.0)
            for rows, p, ds in zip((cur, prv, slice(0, BLOCK)), probs[:3], dss):
                dv_ref[rows, sl] += lax.dot_general(p.astype(bf16), dog, _DIMS["tn"], preferred_element_type=f32)
                dk_ref[rows, sl] += lax.dot_general(ds, qg, _DIMS["tn"], preferred_element_type=f32)
        dsk_ref[...] += dsk

    return _pc(body, name="attn_bwd", grid=(tp // BLOCK,),
               in_specs=[pl.BlockSpec(memory_space=pltpu.SMEM), blk(D_ATT), blk(D_KV), prev, meta, blk(D_KV), prev, meta,
                         blk(D_ATT)],
               out_specs=(blk(D_ATT), _full((tp, D_KV)), _full((tp, D_KV)), _full((1, LANES))),
               out_shape=(S((tp, D_ATT), f32), S((tp, D_KV), f32), S((tp, D_KV), f32), S((1, LANES), f32)),
               compiler_params=_cparams(("arbitrary",)))(sinks, qp, kp, kp, kp, vp, vp, vp, dop)


def _seg(x, bm):
    hi = x.astype(bf16)
    lo = (x - hi.astype(f32)).astype(bf16)
    return jnp.dot(jnp.concatenate([hi, lo], axis=1), bm, preferred_element_type=f32)


@jax.custom_vjp
def _seg_linear(x, bm):
    return _seg(x, bm)


_seg_linear.defvjp(lambda x, bm: (_seg(x, bm), bm), lambda bm, ct: (_seg(ct, bm), jnp.zeros_like(bm)))


def _softplus(y):
    return jnp.maximum(y, 0.0) + jnp.log(1.0 + jnp.exp(-jnp.abs(y)))


def _prep_fn(xr, xk, xwd, xad, xgd, w0, w2, a0, a2, g2, k_k, k_a, bm, seg=_seg):
    xw = w0 + jnp.dot(jnp.tanh(xwd), w2, preferred_element_type=f32)
    decay = jnp.exp(-jnp.exp(-_softplus(-xw) - 0.5))
    alpha = _sigmoid(a0 + jnp.dot(xad, a2, preferred_element_type=f32))
    g = jnp.dot(_sigmoid(xgd), g2, preferred_element_type=f32)
    kk = xk * k_k
    kkn = kk / jnp.maximum(jnp.sqrt(seg(kk * kk, bm)), 1e-12)
    k2 = xk * (1.0 + (alpha - 1.0) * k_a)
    return decay, k2, -kkn, kkn * alpha, g


def _split_cols(x):
    o1, o2, o3 = 3 * D_R, 3 * D_R + LORA_W, 3 * D_R + LORA_W + LORA_A
    return x[:, 0:D_R], x[:, D_R:2 * D_R], x[:, 2 * D_R:o1], x[:, o1:o2], x[:, o2:o3], x[:, o3:RWKV_COLS]


def _shifted(sh_ref, x, halo, first, rb):
    sh_ref[0:SUBLANES, :] = jnp.where(first, 0.0, halo)
    sh_ref[SUBLANES:SUBLANES + rb, :] = x
    return sh_ref[SUBLANES - 1:SUBLANES - 1 + rb, :]


_PREP_PARAMS = ("od_w0", "od_w2", "od_a0", "od_a2", "od_g2", "od_k_k", "od_k_a")


def _rwkv_prep(pr, mu, params, bm):
    t = pr.shape[0]
    rb = _row_block8(t)
    hb = rb // SUBLANES

    def body(pr_ref, halo_ref, mu_ref, w0, w2, a0, a2, g2, kk_ref, ka_ref, bm_ref, *outs_sh):
        outs, sh_ref = outs_sh[:-1], outs_sh[-1]
        x = pr_ref[...]
        prev = _shifted(sh_ref, x, halo_ref[...], pl.program_id(0) == 0, rb)
        xr, xk, xv, xwd, xad, xgd = _split_cols(x + (prev - x) * mu_ref[...])
        bmv = bm_ref[...]
        decay, k2, a_s, b_s, g = _prep_fn(xr, xk, xwd, xad, xgd, w0[...], w2[...], a0[...], a2[...], g2[...],
                                          kk_ref[...], ka_ref[...], bmv)
        vals = (xr, xv, decay, k2, a_s, b_s, decay * xr, _seg(b_s * xr, bmv), _seg(k2 * xr, bmv), g)
        for ref, val in zip(outs, vals):
            ref[...] = val

    row = pl.BlockSpec((rb, RWKV_COLS), lambda i: (i, 0))
    halo = pl.BlockSpec((SUBLANES, RWKV_COLS), lambda i: (jnp.maximum(i * hb - 1, 0), 0))
    orow = pl.BlockSpec((rb, D_R), lambda i: (i, 0))
    return _pc(body, name="rwkv_prep", grid=(t // rb,),
               in_specs=[row, halo, _full((1, RWKV_COLS))] + [_full(p.shape) for p in params] + [_full(bm.shape)],
               out_specs=(orow,) * 10, out_shape=(S((t, D_R), f32),) * 10,
               scratch_shapes=[pltpu.VMEM((rb + SUBLANES, RWKV_COLS), f32)],
               compiler_params=_cparams(("arbitrary",)))(pr, pr, mu, *params, bm)


def _rwkv_prep_bwd(pr, mu, params, bm, cts):
    t = pr.shape[0]
    rb = _row_block8(t)
    hb = rb // SUBLANES
    counts = [len(c) for c in cts]
    flat = [a for c in cts for a in c]

    def body(pr_ref, halo_ref, mu_ref, w0, w2, a0, a2, g2, kk_ref, ka_ref, bm_ref, *rest):
        ct_refs, rest = rest[:len(flat)], rest[len(flat):]
        dx_ref, dmu_ref = rest[0], rest[1]
        dpar_refs, sh_ref = rest[2:9], rest[9]

        @pl.when(pl.program_id(0) == 0)
        def _():
            dmu_ref[...] = jnp.zeros_like(dmu_ref)
            for r in dpar_refs:
                r[...] = jnp.zeros_like(r)
        sums, pos = [], 0
        for c in counts:
            sums.append(sum(r[...] for r in ct_refs[pos:pos + c]))
            pos += c
        x = pr_ref[...]
        prev = _shifted(sh_ref, x, halo_ref[...], pl.program_id(0) == 0, rb)
        xr, xk, xv, xwd, xad, xgd = _split_cols(x + (prev - x) * mu_ref[...])
        bmv = bm_ref[...]
        _, vjp = jax.vjp(lambda *a: _prep_fn(*a, bmv, _seg_linear), xr, xk, xwd, xad, xgd, w0[...], w2[...], a0[...], a2[...],
                         g2[...], kk_ref[...], ka_ref[...])
        grads = vjp(tuple(sums[:5]))
        dxr, dxk, dxwd, dxad, dxgd = grads[:5]
        o1, o2, o3 = 3 * D_R, 3 * D_R + LORA_W, 3 * D_R + LORA_W + LORA_A
        dx_ref[:, 0:D_R] = dxr + sums[5]
        dx_ref[:, D_R:2 * D_R] = dxk
        dx_ref[:, 2 * D_R:o1] = sums[6]
        dx_ref[:, o1:o2] = dxwd
        dx_ref[:, o2:o3] = dxad
        dx_ref[:, o3:RWKV_COLS] = dxgd
        dmu_ref[...] += jnp.sum(dx_ref[...] * (prev - x), axis=0, keepdims=True)
        for r, gval in zip(dpar_refs, grads[5:]):
            r[...] += gval

    row = pl.BlockSpec((rb, RWKV_COLS), lambda i: (i, 0))
    halo = pl.BlockSpec((SUBLANES, RWKV_COLS), lambda i: (jnp.maximum(i * hb - 1, 0), 0))
    crow = pl.BlockSpec((rb, D_R), lambda i: (i, 0))
    return _pc(body, name="rwkv_prep_bwd", grid=(t // rb,),
               in_specs=[row, halo, _full((1, RWKV_COLS))] + [_full(p.shape) for p in params] + [_full(bm.shape)]
               + [crow] * len(flat),
               out_specs=(row, _full((1, RWKV_COLS))) + tuple(_full(p.shape) for p in params),
               out_shape=(S((t, RWKV_COLS), f32), S((1, RWKV_COLS), f32)) + tuple(S(p.shape, f32) for p in params),
               scratch_shapes=[pltpu.VMEM((rb + SUBLANES, RWKV_COLS), f32)],
               compiler_params=_cparams(("arbitrary",)))(pr, pr, mu, *params, bm, *flat)


def _shift_bwd(dxs, mu):
    t = dxs.shape[0]
    rb = _row_block(t)
    hb = rb // SUBLANES
    nblk = t // rb

    def body(dx_ref, halo_ref, mu_ref, o_ref, sh_ref):
        dx = dx_ref[...]
        sh_ref[0:rb, :] = dx
        sh_ref[rb:rb + SUBLANES, :] = jnp.where(pl.program_id(0) == nblk - 1, 0.0, halo_ref[...])
        m = mu_ref[...]
        o_ref[...] = (dx * (1.0 - m) + sh_ref[1:1 + rb, :] * m).astype(bf16)

    row = pl.BlockSpec((rb, RWKV_COLS), lambda i: (i, 0))
    halo = pl.BlockSpec((SUBLANES, RWKV_COLS), lambda i: (jnp.minimum((i + 1) * hb, t // SUBLANES - 1), 0))
    return _pc(body, name="rwkv_shift_bwd", grid=(nblk,), in_specs=[row, halo, _full((1, RWKV_COLS))],
               out_specs=row, out_shape=S((t, RWKV_COLS), bf16),
               scratch_shapes=[pltpu.VMEM((rb + SUBLANES, RWKV_COLS), f32)],
               compiler_params=_cparams(("arbitrary",)))(dxs, dxs, mu)


def _post_fn(y, xr, k2, xv, g, lg, lb, rk, bm, seg=_seg):
    inv_n = 1.0 / HEAD_DIM
    yc = y - seg(y, bm) * inv_n
    var = seg(yc * yc, bm) * inv_n
    yn = yc * lax.rsqrt(var + RWKV_GN_EPS) * lg + lb
    return (yn + seg(xr * k2 * rk, bm) * xv) * g


def _rwkv_post(y, xr, k2, xv, g, lg, lb, rk, bm):
    t = y.shape[0]
    rb = _row_block8(t)

    def body(y_ref, xr_ref, k2_ref, xv_ref, g_ref, lg_ref, lb_ref, rk_ref, bm_ref, o_ref):
        o_ref[...] = _post_fn(y_ref[...], xr_ref[...], k2_ref[...], xv_ref[...], g_ref[...], lg_ref[...], lb_ref[...],
                              rk_ref[...], bm_ref[...])

    row = pl.BlockSpec((rb, D_R), lambda i: (i, 0))
    vec = _full((1, D_R))
    return _pc(body, name="rwkv_post", grid=(t // rb,), in_specs=[row] * 5 + [vec] * 3 + [_full(bm.shape)],
               out_specs=row, out_shape=S((t, D_R), f32),
               compiler_params=_cparams(("arbitrary",)))(y, xr, k2, xv, g, lg, lb, rk, bm)


def _rwkv_post_bwd(dy1, y, xr, k2, xv, g, lg, lb, rk, bm):
    t = y.shape[0]
    rb = _row_block8(t)

    def body(dy_ref, y_ref, xr_ref, k2_ref, xv_ref, g_ref, lg_ref, lb_ref, rk_ref, bm_ref, *outs):
        @pl.when(pl.program_id(0) == 0)
        def _():
            for r in outs[5:]:
                r[...] = jnp.zeros_like(r)
        bmv = bm_ref[...]
        _, vjp = jax.vjp(lambda *a: _post_fn(*a, bmv, _seg_linear), y_ref[...], xr_ref[...], k2_ref[...], xv_ref[...], g_ref[...],
                         lg_ref[...], lb_ref[...], rk_ref[...])
        grads = vjp(dy_ref[...])
        for r, gval in zip(outs[:5], grads[:5]):
            r[...] = gval
        for r, gval in zip(outs[5:], grads[5:]):
            r[...] += gval

    row = pl.BlockSpec((rb, D_R), lambda i: (i, 0))
    vec = _full((1, D_R))
    return _pc(body, name="rwkv_post_bwd", grid=(t // rb,),
               in_specs=[pl.BlockSpec((rb, D_R), lambda i: (i, 1))] + [row] * 5 + [vec] * 3 + [_full(bm.shape)],
               out_specs=(row,) * 5 + (vec,) * 3, out_shape=(S((t, D_R), f32),) * 5 + (S((1, D_R), f32),) * 3,
               compiler_params=_cparams(("arbitrary",)))(dy1, y, xr, k2, xv, g, lg, lb, rk, bm)


def _seg2(x, bb):
    hi = x.astype(bf16)
    lo = (x - hi.astype(f32)).astype(bf16)
    return jnp.dot(jnp.concatenate([hi, lo], axis=1), bb, preferred_element_type=f32)


def _row4(rows, j):
    return jnp.concatenate([jnp.broadcast_to(rows[j:j + 1, p * LANES:(p + 1) * LANES], (HEAD_DIM, LANES))
                            for p in range(4)], axis=0)


def _scan_consts():
    lane_group = jnp.arange(LANES) // HEAD_DIM
    b128 = (lane_group[:, None] == lane_group[None, :]).astype(bf16)
    bb = jnp.concatenate([b128, b128], axis=0)
    qsel = (jnp.arange(PAIR_ROWS)[:, None] % HEAD_DIM == jnp.arange(LANES)[None, :] % HEAD_DIM).astype(f32)
    return bb, qsel


def _store_cols(acc_ref, o_ref, tc):
    for p in range(4):
        blk = acc_ref[p * HEAD_DIM:(p + 1) * HEAD_DIM, :].T
        o_ref[:, (2 * p) * HEAD_DIM:(2 * p + 1) * HEAD_DIM] = blk[0:tc]
        o_ref[:, (2 * p + 1) * HEAD_DIM:(2 * p + 2) * HEAD_DIM] = blk[HEAD_DIM:HEAD_DIM + tc]


PAIR_GROUP = 2 * SUBLANES


def _rwkv_pairs(w, a, b, k, wr, bm):
    t = w.shape[0]
    rb = _row_block8(t)

    def body(w_ref, a_ref, b_ref, k_ref, wr_ref, bm_ref, *outs_sh):
        outs, sh_ref = outs_sh[:-1], outs_sh[-1]

        def second(ref):
            sh_ref[0:rb, :] = ref[...]
            sh_ref[rb:rb + SUBLANES, :] = jnp.zeros((SUBLANES, D_R), f32)
            return sh_ref[1:1 + rb, :]

        w1, b1, k1 = w_ref[...], b_ref[...], k_ref[...]
        w2, a2, wr2 = second(w_ref), second(a_ref), second(wr_ref)
        bmv = bm_ref[...]
        vals = (w1 * a2, w1 * wr2, w1 * w2, b1 * w2, k1 * w2, _seg(b1 * a2, bmv), _seg(k1 * a2, bmv),
                _seg(b1 * wr2, bmv), _seg(k1 * wr2, bmv))
        for ref, val in zip(outs, vals):
            ref[...] = val

    row = pl.BlockSpec((rb, D_R), lambda i: (i, 0))
    return _pc(body, name="rwkv_pairs", grid=(t // rb,), in_specs=[row] * 5 + [_full(bm.shape)],
               out_specs=(row,) * 9, out_shape=(S((t, D_R), f32),) * 9,
               scratch_shapes=[pltpu.VMEM((rb + SUBLANES, D_R), f32)],
               compiler_params=_cparams(("arbitrary",)))(w, a, b, k, wr, bm)


def _wkv_fwd(w, k, v, a, b, wr, br, kr, pairs):
    t = w.shape[0]
    tc = SCAN_CHUNK
    bb, qsel = _scan_consts()

    def body(*refs):
        step_refs, pair_refs = refs[0:8], refs[8:17]
        bb_ref, q_ref, y_ref, st_ref, sa_ref, vb_ref, s_scr, yacc = refs[17:]

        @pl.when(pl.program_id(0) == 0)
        def _():
            s_scr[...] = jnp.zeros_like(s_scr)
        bbv, qv = bb_ref[...], q_ref[...]
        lane64 = lax.broadcasted_iota(jnp.int32, (PAIR_ROWS, LANES), 1) % HEAD_DIM

        def halves(x):
            hi = x.astype(bf16)
            return jnp.concatenate([hi, (x - hi.astype(f32)).astype(bf16)], axis=1)

        def group(gi, s):
            base = pl.multiple_of(gi * PAIR_GROUP, PAIR_GROUP)
            w16, k16, v16, a16, b16, wr16, br16, kr16 = step_refs
            a2p, r2p, w12p, b1wp, k1wp, betap, kappap, bwrp, kwrp = pair_refs

            def rows8(ref, j):
                return ref[pl.ds(base + (j // SUBLANES) * SUBLANES, SUBLANES), :]

            def bcast(rows, j, p):
                return jnp.broadcast_to(rows[j % SUBLANES:j % SUBLANES + 1, p * LANES:(p + 1) * LANES], (HEAD_DIM, LANES))

            step = lambda ref, j, p: bcast(rows8(ref, j), j, p)
            qp = qv[0:HEAD_DIM]
            lane = lane64[0:HEAD_DIM]
            for q in range(SUBLANES):
                j1, j2 = 2 * q, 2 * q + 1
                t1 = base + j1
                nxt = []
                for p in range(4):
                    sl = slice(p * HEAD_DIM, (p + 1) * HEAD_DIM)
                    sp = s[sl]
                    lhs = [halves(jnp.concatenate([sp * step(a16, j1, p), sp * step(a2p, j1, p), sp * step(wr16, j1, p),
                                                   sp * step(r2p, j1, p)], axis=0))]
                    for j in (j1, j2):
                        v8 = rows8(v16, j)
                        vh8 = v8.astype(bf16).astype(f32)
                        lhs.append(jnp.concatenate([(qp * bcast(vh8, j, p)).astype(bf16),
                                                    (qp * bcast(v8 - vh8, j, p)).astype(bf16)], axis=1))
                    r = jnp.dot(jnp.concatenate(lhs, axis=0), bbv, preferred_element_type=f32)
                    sa1, p2, z1, z2, vb1, vb2 = (r[n * HEAD_DIM:(n + 1) * HEAD_DIM] for n in range(6))
                    sa2 = p2 + sa1 * step(betap, j1, p) + vb1 * step(kappap, j1, p)
                    y1 = z1 + sa1 * step(br16, j1, p) + vb1 * step(kr16, j1, p)
                    y2 = (z2 + sa1 * step(bwrp, j1, p) + vb1 * step(kwrp, j1, p)) + (sa2 * step(br16, j2, p)
                                                                                      + vb2 * step(kr16, j2, p))
                    yacc[sl, :] = jnp.where(lane == t1, y1, jnp.where(lane == t1 + 1, y2, yacc[sl, :]))
                    st_ref[base // 2 + q, sl, :] = sp
                    sa_ref[t1, sl, :] = sa1
                    sa_ref[t1 + 1, sl, :] = sa2
                    vb_ref[t1, sl, :] = vb1
                    vb_ref[t1 + 1, sl, :] = vb2
                    nxt.append(((sp * step(w12p, j1, p) + sa1 * step(b1wp, j1, p)) + vb1 * step(k1wp, j1, p))
                               + (sa2 * step(b16, j2, p) + vb2 * step(k16, j2, p)))
                s = jnp.concatenate(nxt, axis=0)
            return s

        s_scr[...] = lax.fori_loop(0, tc // PAIR_GROUP, group, s_scr[...])
        _store_cols(yacc, y_ref, tc)

    row = pl.BlockSpec((tc, D_R), lambda c: (c, 0))
    tiles = pl.BlockSpec((tc, PAIR_ROWS, LANES), lambda c: (c, 0, 0))
    return _pc(body, name="wkv_fwd", grid=(t // tc,),
               in_specs=[row] * 17 + [_full(bb.shape), _full(qsel.shape)],
               out_specs=(row, pl.BlockSpec((tc // 2, PAIR_ROWS, LANES), lambda c: (c, 0, 0)), tiles, tiles),
               out_shape=(S((t, D_R), f32), S((t // 2, PAIR_ROWS, LANES), f32)) + (S((t, PAIR_ROWS, LANES), f32),) * 2,
               scratch_shapes=[pltpu.VMEM((PAIR_ROWS, LANES), f32), pltpu.VMEM((PAIR_ROWS, LANES), f32)],
               compiler_params=_cparams(("arbitrary",)))(w, k, v, a, b, wr, br, kr, *pairs, bb, qsel)


def _wkv_bwd(sprev, sab, vbb, w, k, a, b, r, dy):
    t = w.shape[0]
    tc = SCAN_CHUNK
    nc = t // tc
    bb, qsel = _scan_consts()

    def body(st_ref, sa_ref, vb_ref, w_ref, k_ref, a_ref, b_ref, r_ref, dy_ref, bb_ref, q_ref,
             dr_ref, dw_ref, dk_ref, dv_ref, da_ref, db_ref, g_scr, dvacc, rows_scr):
        @pl.when(pl.program_id(0) == 0)
        def _():
            g_scr[...] = jnp.zeros_like(g_scr)
        bbv, qv = bb_ref[...], q_ref[...]
        lane64 = lax.broadcasted_iota(jnp.int32, (PAIR_ROWS, LANES), 1) % HEAD_DIM
        outs = (dr_ref, dw_ref, db_ref, dk_ref, da_ref)

        def colsums(slot, j, x):
            for p in range(4):
                rows_scr[slot, j:j + 1, p * LANES:(p + 1) * LANES] = jnp.sum(x[p * HEAD_DIM:(p + 1) * HEAD_DIM], axis=0,
                                                                           keepdims=True)

        def group(i, g):
            base = pl.multiple_of((tc // SUBLANES - 1 - i) * SUBLANES, SUBLANES)
            w8, k8, a8, b8, r8, dy8 = (ref[pl.ds(base, SUBLANES), :] for ref in (w_ref, k_ref, a_ref, b_ref, r_ref, dy_ref))

            def after_step(j, sp):
                return sp * _row4(w8, j) + sa_ref[base + j] * _row4(b8, j) + vb_ref[base + j] * _row4(k8, j)

            def back_step(j, sp, s_t, g):
                tt = base + j
                u, vb = sa_ref[tt], vb_ref[tt]
                a4, b4, w4, k4 = _row4(a8, j), _row4(b8, j), _row4(w8, j), _row4(k8, j)
                dyb = _seg2(qv * _row4(dy8, j), bbv)
                g = g + dyb * _row4(r8, j)
                rr2 = _seg2(jnp.concatenate([g * b4, g * k4], axis=0), bbv)
                du, dvb = rr2[0:PAIR_ROWS], rr2[PAIR_ROWS:2 * PAIR_ROWS]
                for slot, val in enumerate((s_t * dyb, g * sp, g * u, g * vb, sp * du)):
                    colsums(slot, j, val)
                dvacc[...] = jnp.where(lane64 == tt, dvb, dvacc[...])
                return g * w4 + du * a4

            for q in reversed(range(SUBLANES // 2)):
                s0 = st_ref[base // 2 + q]
                s1 = after_step(2 * q, s0)
                g = back_step(2 * q + 1, s1, after_step(2 * q + 1, s1), g)
                g = back_step(2 * q, s0, s1, g)
            for slot, ref in enumerate(outs):
                ref[pl.ds(base, SUBLANES), :] = rows_scr[slot]
            return g

        g_scr[...] = lax.fori_loop(0, tc // SUBLANES, group, g_scr[...])
        _store_cols(dvacc, dv_ref, tc)

    row = pl.BlockSpec((tc, D_R), lambda c: (nc - 1 - c, 0))
    tiles = pl.BlockSpec((tc, PAIR_ROWS, LANES), lambda c: (nc - 1 - c, 0, 0))
    states = pl.BlockSpec((tc // 2, PAIR_ROWS, LANES), lambda c: (nc - 1 - c, 0, 0))
    return _pc(body, name="wkv_bwd", grid=(nc,),
               in_specs=[states, tiles, tiles] + [row] * 6 + [_full(bb.shape), _full(qsel.shape)],
               out_specs=(row,) * 6, out_shape=(S((t, D_R), f32),) * 6,
               scratch_shapes=[pltpu.VMEM((PAIR_ROWS, LANES), f32), pltpu.VMEM((PAIR_ROWS, LANES), f32),
                               pltpu.VMEM((5, SUBLANES, D_R), f32)],
               compiler_params=_cparams(("arbitrary",)))(sprev, sab, vbb, w, k, a, b, r, dy, bb, qsel)


def _rope_tables(t):
    half = HEAD_DIM // 2
    inv = ROPE_THETA ** (-jnp.arange(half, dtype=f32) / half)
    ang = jnp.arange(t, dtype=f32)[:, None] * inv[None, :]
    cos, sin = jnp.cos(ang), jnp.sin(ang)
    return jnp.concatenate([cos, cos], axis=1), jnp.concatenate([-sin, sin], axis=1)


def _head_matrix():
    grp = jnp.arange(D_R) // HEAD_DIM
    b = (grp[:, None] == grp[None, :]).astype(bf16)
    return jnp.concatenate([b, b], axis=0)


def _ffn_fwd(h, g, get_w, conv_w, conv_b, i):
    hf = _rms_fwd(h, g, f"ffn{i}_norm")
    w_up_t = get_w(f"ff{i}_up", hf)
    u = _mm(hf, w_up_t, "nt", f"ffn{i}_up")
    z = _ffn_mid(u, conv_w, conv_b, f"ffn{i}_mid")
    w_down = get_w(f"ff{i}_down", z)
    return _mm(z, w_down, "nn", f"ffn{i}_down", res=h), (hf, u, z), w_up_t, w_down


def _ffn_bwd(dh, h, saved, g, w_up_t, conv_w, conv_b, w_down, i, put_g):
    hf, u, z = saved
    dz = _mm(dh, w_down, "nt", f"ffn{i}_dz")
    g_down = _mm(z, dh, "tn", f"ffn{i}_gdown", out_dtype=GRAD_WIRE_DTYPE)
    tok = put_g(f"ff{i}_down", g_down)
    du, g_conv, g_convb = _ffn_mid_bwd(dz, u, conv_w, conv_b + tok, f"ffn{i}_mid_bwd")
    g_up_t = _mm(du, hf, "tn", f"ffn{i}_gup", out_dtype=GRAD_WIRE_DTYPE)
    tok = put_g(f"ff{i}_up", g_up_t)
    dhf = _mm(du, w_up_t, "nn", f"ffn{i}_dhf")
    dh_in, g_norm = _rms_bwd(dhf, h, g + tok, dh, f"ffn{i}_norm_bwd")
    return dh_in, dict(conv=g_conv, conv_b=g_convb, norm=g_norm)


def _local_step(x, target, W, get_w, put_g, put_small, tok0):
    t = N_META + x.shape[0]
    c64, s64 = _rope_tables(t)
    bm = _head_matrix()
    h0 = jnp.concatenate([W["meta_tokens"], x], axis=0)

    ev_w_in_t, ev_w_out = get_w("ev_in", None), get_w("ev_out", None)
    hn0 = _rms_fwd(h0, W["norm_mix"][0] + tok0, "mix0_norm")
    p0 = _mm(hn0, ev_w_in_t, "nt", "ev_in")
    uc = _ev_a_conv(p0, W["ev_conv_a"])
    y0 = _ev_b(p0, W["ev_conv_b"], _ev_a_norm(uc, W["ev_ln_a_g"], W["ev_ln_a_b"]))
    h1 = _mm(y0, ev_w_out, "nn", "ev_out", res=h0)
    h2, ffn0, ff0_up_t, ff0_down = _ffn_fwd(h1, W["norm_ffn"][0], get_w, W["ff_conv"][0], W["ff_conv_b"][0], 0)

    hn1 = _rms_fwd(h2, W["norm_mix"][1], "mix1_norm")
    od_w_in_t = get_w("od_in", hn1)
    w_att, w_rwkv = od_w_in_t[:ATT_COLS], od_w_in_t[ATT_COLS:]
    pr = _mm(hn1, w_rwkv, "nt", "od_in_rwkv")
    qp, kp, vp = _rope_pack(_mm(hn1, w_att, "nt", "od_in_att"), c64, s64)
    op = _attn_fwd(qp, kp, vp, W["od_sinks"])
    prep_params = [W[n] for n in _PREP_PARAMS]
    xr, xv, decay, k2, a_s, b_s, wr, br, kr, gate = _rwkv_prep(pr, W["od_mu"], prep_params, bm)
    pairs = _rwkv_pairs(decay, a_s, b_s, k2, wr, bm)
    ysc, sprev, sab, vbb = _wkv_fwd(decay, k2, xv, a_s, b_s, wr, br, kr, pairs)
    rk = W["od_r_k"].reshape(1, D_R)
    yr = _rwkv_post(ysc, xr, k2, xv, gate, W["od_lnx_g"], W["od_lnx_b"], rk, bm)
    y1 = jnp.concatenate([op[ATT_PAD:], yr.astype(bf16)], axis=1)
    od_w_out = get_w("od_out", y1)
    h3 = _mm(y1, od_w_out, "nn", "od_out", res=h2)
    h4, ffn1, ff1_up_t, ff1_down = _ffn_fwd(h3, W["norm_ffn"][1], get_w, W["ff_conv"][1], W["ff_conv_b"][1], 1)

    tgt = jnp.concatenate([jnp.zeros((N_META, D_MODEL), f32), target], axis=0)
    loss, dh4, g_norm_final = _final_loss(h4, W["norm_final"], tgt)

    dh3, gf1 = _ffn_bwd(dh4, h3, ffn1, W["norm_ffn"][1], ff1_up_t, W["ff_conv"][1], W["ff_conv_b"][1], ff1_down, 1, put_g)
    dy1 = _mm(dh3, od_w_out, "nt", "od_dy")
    g_od_w_out = _mm(y1, dh3, "tn", "od_gout", out_dtype=GRAD_WIRE_DTYPE)
    tok = put_g("od_out", g_od_w_out)
    dysc, dxr_p, dk2_p, dxv_p, dgate, g_lnx_g, g_lnx_b, g_rk = _rwkv_post_bwd(
        dy1, ysc, xr, k2, xv, gate, W["od_lnx_g"], W["od_lnx_b"] + tok, rk, bm)
    dr, dw, dk, dv, da, db = _wkv_bwd(sprev, sab, vbb, decay, k2, a_s, b_s, xr, dysc)
    prep_grads = _rwkv_prep_bwd(pr, W["od_mu"], prep_params, bm,
                                [[dw], [dk, dk2_p], [da], [db], [dgate], [dr, dxr_p], [dv, dxv_p]])
    dxs, g_mu = prep_grads[0], prep_grads[1]
    dpr = _shift_bwd(dxs, W["od_mu"])
    dop = jnp.concatenate([jnp.zeros((ATT_PAD, D_ATT), f32), dy1[:, :D_ATT]], axis=0).astype(bf16)
    dqp, dkp, dvp, dsk = _attn_bwd(qp, kp, vp, W["od_sinks"], dop)
    dpatt = _rope_bwd(dqp, dkp, dvp, c64, s64)
    g_od_w_in_t = jnp.concatenate([_mm(dpatt, hn1, "tn", "od_gin_att", out_dtype=GRAD_WIRE_DTYPE),
                                   _mm(dpr, hn1, "tn", "od_gin_rwkv", out_dtype=GRAD_WIRE_DTYPE)], axis=0)
    tok = put_g("od_in", g_od_w_in_t)
    dhn1 = _mm(dpr, w_rwkv, "nn", "od_dhn_rwkv", res=_mm(dpatt, w_att, "nn", "od_dhn_att"))
    dh2, g_norm_mix1 = _rms_bwd(dhn1, h2, W["norm_mix"][1] + tok, dh3, "mix1_norm_bwd")

    dh1, gf0 = _ffn_bwd(dh2, h1, ffn0, W["norm_ffn"][0], ff0_up_t, W["ff_conv"][0], W["ff_conv_b"][0], ff0_down, 0, put_g)
    early = dict(
        norm_ffn=jnp.concatenate([gf0["norm"], gf1["norm"]], axis=0), norm_final=g_norm_final.reshape(D_MODEL),
        od_sinks=dsk[:, :N_Q_HEADS], od_mu=g_mu, od_lnx_g=g_lnx_g, od_lnx_b=g_lnx_b, od_r_k=g_rk.reshape(N_Q_HEADS, HEAD_DIM),
        ff_conv=jnp.stack([gf0["conv"], gf1["conv"]]), ff_conv_b=jnp.concatenate([gf0["conv_b"], gf1["conv_b"]], axis=0),
        **dict(zip(_PREP_PARAMS, prep_grads[2:])))
    dy0 = _mm(dh1, ev_w_out, "nt", "ev_dy")
    g_ev_w_out = _mm(y0, dh1, "tn", "ev_gout", out_dtype=GRAD_WIRE_DTYPE)
    tok = put_g("ev_out", g_ev_w_out) + put_small(early)
    duc, g_ln_g, g_ln_b = _ev_a_norm_bwd(dy0, uc, W["ev_ln_a_g"], W["ev_ln_a_b"] + tok)
    dav, dag, g_conv_a = _ev_a_conv_bwd(duc, p0, W["ev_conv_a"])
    dgb, dgc, dxi, g_conv_b = _ev_b_bwd(dy0, p0, W["ev_conv_b"])
    dp0 = jnp.concatenate([dav, dag, dgb, dgc, dxi], axis=1)
    g_ev_w_in_t = _mm(dp0, hn0, "tn", "ev_gin", out_dtype=GRAD_WIRE_DTYPE)
    tok = put_g("ev_in", g_ev_w_in_t)
    dhn0 = _mm(dp0, ev_w_in_t, "nn", "ev_dhn")
    dh0, g_norm_mix0 = _rms_bwd(dhn0, h0, W["norm_mix"][0] + tok, dh1, "mix0_norm_bwd")

    late = dict(meta_tokens=dh0[:N_META], norm_mix=jnp.concatenate([g_norm_mix0, g_norm_mix1], axis=0),
                ev_conv_a=g_conv_a, ev_ln_a_g=g_ln_g, ev_ln_a_b=g_ln_b, ev_conv_b=g_conv_b)
    return loss, dh0[N_META:], late


HBM = pl.BlockSpec(memory_space=pl.ANY)


def _mesh_pos():
    return lax.axis_index("x"), lax.axis_index("y"), lax.axis_index("c")


def _dev(px, py, pc):
    return 4 * px + 2 * py + pc


def _all_gather(xs, name):
    n = len(xs)

    def body(*refs):
        x_refs, o_refs = refs[:n], refs[n:2 * n]
        send_sems, recv_sems, local_sems = refs[2 * n:]
        x, y, c = _mesh_pos()
        me, sibling = (x, y, c), (x, y, 1 - c)
        chips = [(1 - x, y), (x, 1 - y), (1 - x, 1 - y)]

        def copy(i, k, block, to, from_input=False):
            dst = o_refs[i].at[_dev(*block)]
            return pltpu.make_async_remote_copy(src_ref=x_refs[i] if from_input else dst, dst_ref=dst,
                                                send_sem=send_sems.at[i, k], recv_sem=recv_sems.at[i, k],
                                                device_id=to, device_id_type=MESH)

        mine = [pltpu.make_async_copy(x_refs[i], o_refs[i].at[_dev(*me)], local_sems.at[i]) for i in range(n)]
        for cp in mine:
            cp.start()
        first = []
        for i in range(n):
            first.append(copy(i, 0, me, sibling, True))
            first += [copy(i, 1 + j, me, (*chip, c), True) for j, chip in enumerate(chips)]
        for cp in first:
            cp.start()
        passed = []
        for j, chip in enumerate(chips):
            for i in range(n):
                copy(i, 1 + j, (*chip, c), me).wait_recv()
                fwd = copy(i, 4 + j, (*chip, c), sibling)
                fwd.start()
                passed.append(fwd)
        for i in range(n):
            copy(i, 0, sibling, me).wait_recv()
            for j, chip in enumerate(chips):
                copy(i, 4 + j, (*chip, 1 - c), me).wait_recv()
        for cp in first + passed:
            cp.wait_send()
        for cp in mine:
            cp.wait()

    return _pc(body, name=name, in_specs=[HBM] * n, out_specs=tuple([HBM] * n),
               out_shape=tuple(S((N_DEV,) + x.shape, x.dtype) for x in xs),
               scratch_shapes=[pltpu.SemaphoreType.DMA((n, 7)), pltpu.SemaphoreType.DMA((n, 7)),
                               pltpu.SemaphoreType.DMA((n,))])(*xs)


HBM_SPEC = pl.BlockSpec(memory_space=pltpu.HBM)
SEM_SPEC = pl.BlockSpec(memory_space=pltpu.SEMAPHORE)
DATAFLOW = pltpu.SideEffectType.DATAFLOW_SIDE_EFFECTING
_PEER_FLIPS = ((1, 0, 0), (0, 1, 0), (1, 1, 0), (1, 0, 1), (0, 1, 1), (1, 1, 1), (0, 0, 1))
N_PEERS = len(_PEER_FLIPS)


def _peers(x, y, c):
    return [((1 - x) if fx else x, (1 - y) if fy else y, (1 - c) if fc else c) for fx, fy, fc in _PEER_FLIPS]


def _xchg_start(srcs, lands, scatter, name):
    n = len(srcs)

    def body(*refs):
        src_refs, land_refs = refs[:n], refs[n:2 * n]
        send_sems, recv_sems, token = refs[2 * n], refs[2 * n + 1], refs[-1]
        x, y, c = _mesh_pos()
        me = _dev(x, y, c)
        for i in range(n):
            for k, peer in enumerate(_peers(x, y, c)):
                pltpu.make_async_remote_copy(src_ref=src_refs[i].at[_dev(*peer)] if scatter else src_refs[i],
                                             dst_ref=land_refs[i].at[me], send_sem=send_sems.at[i * N_PEERS + k],
                                             recv_sem=recv_sems.at[i * N_PEERS + k], device_id=peer, device_id_type=MESH).start()
        token[...] = jnp.zeros_like(token)

    arrs = list(srcs) + list(lands)
    outs = _pc(body, name=name,
               out_shape=(pltpu.SemaphoreType.DMA((n * N_PEERS,)), pltpu.SemaphoreType.DMA((n * N_PEERS,)),
                          *[pltpu.HBM(a.shape, a.dtype) for a in arrs], S((SUBLANES, LANES), f32)),
               in_specs=[HBM_SPEC] * (2 * n),
               out_specs=(SEM_SPEC, SEM_SPEC, *[HBM_SPEC] * (2 * n), pl.BlockSpec(memory_space=pltpu.VMEM)),
               input_output_aliases={i: 2 + i for i in range(2 * n)},
               compiler_params=pltpu.CompilerParams(has_side_effects=DATAFLOW))(
        *[pltpu.with_memory_space_constraint(a, pltpu.HBM) for a in arrs])
    return (outs[0], outs[1], list(outs[2:2 + n]), list(outs[2 + n:2 + 2 * n]), scatter), outs[-1]


def _xchg_wait(handle, after, name):
    send_sems, recv_sems, srcs, lands, scatter = handle
    n = len(srcs)

    def body(*refs):
        src_refs, land_refs = refs[:n], refs[n:2 * n]
        send, recv = refs[2 * n], refs[2 * n + 1]
        x, y, c = _mesh_pos()
        for i in range(n):
            for k in range(N_PEERS):
                cp = pltpu.make_async_remote_copy(src_ref=src_refs[i].at[0] if scatter else src_refs[i],
                                                  dst_ref=land_refs[i].at[0], send_sem=send.at[i * N_PEERS + k],
                                                  recv_sem=recv.at[i * N_PEERS + k],
                                                  device_id=(x, y, c), device_id_type=MESH)
                cp.wait_send()
                cp.wait_recv()

    arrs = srcs + lands
    outs = _pc(body, name=name, out_shape=tuple(pltpu.HBM(a.shape, a.dtype) for a in arrs),
               in_specs=[HBM_SPEC] * (2 * n) + [SEM_SPEC, SEM_SPEC, pl.BlockSpec(memory_space=pl.ANY)],
               out_specs=tuple([HBM_SPEC] * (2 * n)), input_output_aliases={i: i for i in range(2 * n)},
               compiler_params=pltpu.CompilerParams(has_side_effects=DATAFLOW))(*arrs, send_sems, recv_sems, after)
    return list(outs[:n]), list(outs[n:])


def _rs_sum(g, land, me_vec, name):
    _, r, cols = g.shape
    tr = _divisor_block(r, 16, min(r, 352))

    def body(me_ref, g_ref, *rest):
        o_ref = rest[-1]
        acc = g_ref[0].astype(f32)
        for l_ref in rest[:-1]:
            acc = acc + l_ref[0].astype(f32)
        o_ref[...] = acc

    blk = lambda f: pl.BlockSpec((1, tr, cols), f)
    grid_spec = pltpu.PrefetchScalarGridSpec(
        num_scalar_prefetch=1, grid=(r // tr,),
        in_specs=[blk(lambda i, me_ref: (me_ref[0], i, 0))]
        + [blk(lambda i, me_ref, k=k: ((me_ref[0] + k) % N_DEV, i, 0)) for k in range(1, N_DEV)],
        out_specs=pl.BlockSpec((tr, cols), lambda i, me_ref: (i, 0)))
    return _pc(body, name=name, grid_spec=grid_spec, out_shape=S((r, cols), f32),
               compiler_params=_cparams(("arbitrary",)))(me_vec, g, *([land] * (N_DEV - 1)))


def _sum_devices(a, name):
    def body(a_ref, o_ref):
        acc = a_ref[0]
        for d in range(1, N_DEV):
            acc = acc + a_ref[d]
        o_ref[...] = acc

    return _pc(body, name=name, grid=(1,), in_specs=[_full(a.shape)], out_specs=_full(a.shape[1:]),
               out_shape=S(a.shape[1:], a.dtype), compiler_params=_cparams(("arbitrary",)))(a)


def _adamw(w, m, v, g, name):
    shape = w.shape
    w2, m2, v2, g2 = (a.reshape(-1, shape[-1]) for a in (w, m, v, g))
    rows, cols = w2.shape
    tr = rows if rows % SUBLANES else _divisor_block(rows, SUBLANES, max(SUBLANES, min(rows, ADAMW_BLOCK_ELEMS // cols)))
    c1, c2 = 1.0 - ADAM_B1 ** ADAM_STEP, 1.0 - ADAM_B2 ** ADAM_STEP

    def body(w_ref, m_ref, v_ref, g_ref, d_ref, nm_ref, nv_ref):
        gv = g_ref[...]
        nm = ADAM_B1 * m_ref[...] + (1.0 - ADAM_B1) * gv
        nv = ADAM_B2 * v_ref[...] + (1.0 - ADAM_B2) * (gv * gv)
        d_ref[...] = -ADAM_LR * ((nm / c1) / (jnp.sqrt(nv / c2) + ADAM_EPS) + ADAM_WD * w_ref[...])
        nm_ref[...] = nm
        nv_ref[...] = nv

    blk = pl.BlockSpec((tr, cols), lambda i: (i, 0))
    outs = _pc(body, name=name, grid=(rows // tr,), in_specs=[blk] * 4, out_specs=(blk,) * 3,
               out_shape=(S((rows, cols), f32),) * 3, compiler_params=_cparams(("arbitrary",)))(w2, m2, v2, g2)
    return tuple(o.reshape(shape) for o in outs)


_WEIGHTS = ("meta_tokens", "norm_mix", "norm_ffn", "norm_final", "ev_w_in", "ev_conv_a", "ev_ln_a_g", "ev_ln_a_b",
            "ev_conv_b", "ev_w_out", "od_w_in", "od_sinks", "od_mu", "od_w0", "od_w2", "od_a0", "od_a2", "od_g2",
            "od_k_k", "od_k_a", "od_r_k", "od_lnx_g", "od_lnx_b", "od_w_out", "ff_w_up", "ff_conv", "ff_conv_b", "ff_w_down")
_SMALL_SHARDED = (("meta_tokens", 1), ("ev_conv_a", 2), ("ev_conv_b", 2), ("od_mu", 1), ("od_w0", 1), ("od_w2", 2),
                  ("od_a0", 1), ("od_a2", 2), ("od_g2", 2), ("od_k_k", 1), ("od_k_a", 1), ("od_lnx_g", 1),
                  ("od_lnx_b", 1), ("ff_conv", 2))
_SMALL_REPLICATED = ("norm_mix", "norm_ffn", "norm_final", "ev_ln_a_g", "ev_ln_a_b", "od_sinks", "od_r_k", "ff_conv_b")
SLAB_UNIT = SUBLANES * LANES


def _pack(arrs):
    flat = jnp.concatenate([a.reshape(-1).astype(f32) for a in arrs])
    pad = (-flat.shape[0]) % SLAB_UNIT
    return jnp.pad(flat, (0, pad)).reshape(-1, LANES)


def _unpack(flat, shapes):
    out, off = [], 0
    for shp in shapes:
        size = 1
        for s in shp:
            size *= s
        out.append(flat[..., off:off + size].reshape(flat.shape[:-1] + tuple(shp)))
        off += size
    return out


def _full_shape(shape, axis):
    return tuple(N_DEV * s if i == axis else s for i, s in enumerate(shape))


def kernel(x, meta_tokens, norm_mix, norm_ffn, norm_final, ev_w_in, ev_conv_a, ev_ln_a_g, ev_ln_a_b, ev_conv_b, ev_w_out, od_w_in, od_sinks, od_mu, od_w0, od_w2, od_a0, od_a2, od_g2, od_k_k, od_k_a, od_r_k, od_lnx_g, od_lnx_b, od_w_out, ff_w_up, ff_conv, ff_conv_b, ff_w_down, loss_target, m_meta_tokens, m_norm_mix, m_norm_ffn, m_norm_final, m_ev_w_in, m_ev_conv_a, m_ev_ln_a_g, m_ev_ln_a_b, m_ev_conv_b, m_ev_w_out, m_od_w_in, m_od_sinks, m_od_mu, m_od_w0, m_od_w2, m_od_a0, m_od_a2, m_od_g2, m_od_k_k, m_od_k_a, m_od_r_k, m_od_lnx_g, m_od_lnx_b, m_od_w_out, m_ff_w_up, m_ff_conv, m_ff_conv_b, m_ff_w_down, v_meta_tokens, v_norm_mix, v_norm_ffn, v_norm_final, v_ev_w_in, v_ev_conv_a, v_ev_ln_a_g, v_ev_ln_a_b, v_ev_conv_b, v_ev_w_out, v_od_w_in, v_od_sinks, v_od_mu, v_od_w0, v_od_w2, v_od_a0, v_od_a2, v_od_g2, v_od_k_k, v_od_k_a, v_od_r_k, v_od_lnx_g, v_od_lnx_b, v_od_w_out, v_ff_w_up, v_ff_conv, v_ff_conv_b, v_ff_w_down):
    A = dict(locals())
    px, py, pc = _mesh_pos()
    me = _dev(px, py, pc)
    me_vec = jnp.reshape(me, (1,)).astype(jnp.int32)
    rows = lambda a: a.reshape(N_DEV * a.shape[1], a.shape[2])
    blocks = lambda a: a.reshape(N_DEV, a.shape[0] // N_DEV, a.shape[1])

    shards = dict(ev_in=ev_w_in[0].T, ev_out=ev_w_out[0], ff0_up=ff_w_up[0].T, ff0_down=ff_w_down[0], od_in=od_w_in[0].T,
                  od_out=od_w_out[0], ff1_up=ff_w_up[1].T, ff1_down=ff_w_down[1])
    shards = {n: b.astype(bf16) for n, b in shards.items()}
    small_shapes = [A[n].shape for n, _ in _SMALL_SHARDED]
    gathered = _all_gather([shards["ev_in"], shards["ev_out"], _pack([A[n] for n, _ in _SMALL_SHARDED])], "gather_first")
    gathered, shards = lax.optimization_barrier((gathered, shards))
    fetch, tok0 = {}, jnp.zeros((), f32)
    for n in ("ff0_up", "ff0_down", "od_in", "od_out", "ff1_up", "ff1_down"):
        shard, tok0 = lax.optimization_barrier((shards[n], tok0))
        land = lax.dynamic_update_slice(lax.empty((N_DEV,) + shard.shape, bf16), shard[None], (me, 0, 0))
        fetch[n], token = _xchg_start([shard], [land], False, f"gather_{n}_start")
        tok0 = tok0 + token[0, 0]

    def get_w(n, after):
        if n in ("ev_in", "ev_out"):
            return rows(gathered[("ev_in", "ev_out").index(n)])
        return rows(_xchg_wait(fetch[n], after, f"gather_{n}_wait")[1][0])

    W = {}
    for (n, ax), seg in zip(_SMALL_SHARDED, _unpack(gathered[-1].reshape(N_DEV, -1), small_shapes)):
        W[n] = jnp.moveaxis(seg, 0, ax).reshape(_full_shape(A[n].shape, ax))
    for n in ("ev_conv_a", "ev_conv_b", "od_w2", "od_a2", "od_g2"):
        W[n] = W[n][0]
    for n in _SMALL_REPLICATED:
        W[n] = A[n]
    W["od_r_k"] = od_r_k[0]

    small_shape = {n: _full_shape(A[n].shape, ax) for n, ax in _SMALL_SHARDED}
    small_shape.update({n: A[n].shape for n in _SMALL_REPLICATED})
    sent, small_sent, small_names = {}, {}, {}

    def put_g(n, g):
        g8 = blocks(g)
        sent[n], token = _xchg_start([g8], [lax.empty(g8.shape, g8.dtype)], True, f"reduce_{n}_start")
        return token[0, 0]

    def put_small(gs, stage="early"):
        small_names[stage] = sorted(gs)
        slab = _pack([gs[n] for n in small_names[stage]])
        land = lax.dynamic_update_slice(lax.empty((N_DEV,) + slab.shape, f32), slab[None], (me, 0, 0))
        small_sent[stage], small_tok[stage] = _xchg_start([slab], [land], False, f"gather_{stage}_small_grads_start")
        return small_tok[stage][0, 0]

    small_tok = {}
    loss_tile, grad_x, late = _local_step(x[0], loss_target[0], W, get_w, put_g, put_small, tok0)
    put_small(late, "late")
    late_tok = small_tok["late"]

    gsh, prev = {}, late_tok
    for n in ("ff1_down", "ff1_up", "od_out", "od_in", "ff0_down", "ff0_up", "ev_out", "ev_in"):
        srcs, lands = _xchg_wait(sent[n], prev, f"reduce_{n}_wait")
        gsh[n] = prev = _rs_sum(srcs[0], lands[0], me_vec, f"reduce_{n}_sum")
    grads = dict(ev_w_in=gsh["ev_in"].T[None], ev_w_out=gsh["ev_out"][None], od_w_in=gsh["od_in"].T[None],
                 od_w_out=gsh["od_out"][None], ff_w_up=jnp.stack([gsh["ff0_up"].T, gsh["ff1_up"].T]),
                 ff_w_down=jnp.stack([gsh["ff0_down"], gsh["ff1_down"]]))

    delta, new_m, new_v = {}, {}, {}
    for n in ("ff_w_up", "ff_w_down", "od_w_in", "od_w_out", "ev_w_in", "ev_w_out"):
        delta[n], new_m[n], new_v[n] = _adamw(A[n], A["m_" + n], A["v_" + n], grads[n], "adamw_" + n)
    for stage in ("early", "late"):
        gsm = _xchg_wait(small_sent[stage], delta["ev_w_in"], f"gather_{stage}_small_grads_wait")[1][0]
        summed = _sum_devices(gsm, f"sum_{stage}_small_grads").reshape(-1)
        for n, full in zip(small_names[stage], _unpack(summed, [small_shape[n] for n in small_names[stage]])):
            grads[n] = full
    for n, ax in _SMALL_SHARDED:
        size = A[n].shape[ax]
        grads[n] = lax.dynamic_slice_in_dim(grads[n], me * size, size, axis=ax)
    for n in small_shape:
        delta[n], new_m[n], new_v[n] = _adamw(A[n], A["m_" + n], A["v_" + n], grads[n], "adamw_" + n)

    loss = lax.psum(loss_tile[0, 0], ("x", "y", "c"))
    return (loss, grad_x[None], *[grads[n] for n in _WEIGHTS], *[delta[n] for n in _WEIGHTS],
            *[new_m[n] for n in _WEIGHTS], *[new_v[n] for n in _WEIGHTS])
```

```python
import jax
import jax.numpy as jnp
from jax import lax
from jax.experimental import pallas as pl
from jax.experimental.pallas import tpu as pltpu

f32, bf16 = jnp.float32, jnp.bfloat16

D_MODEL = 1024
N_META = 16
RMS_EPS = 1e-6
LN_EPS = 1e-5
D_A = 512
CONV_A_WIDTH = 31
CONV_B_WIDTH = 3
HEAD_DIM = 64
N_Q_HEADS = 8
N_KV_HEADS = 2
GQA_GROUP = 4
D_ATT = 512
D_KV = 128
BLOCK = 128
ROPE_THETA = 10000.0
D_R = 512
LORA_W, LORA_A, LORA_G = 64, 64, 128
RWKV_GN_EPS = 64e-5
ATT_COLS = D_ATT + 2 * D_KV
RWKV_COLS = 3 * D_R + LORA_W + LORA_A + LORA_G
D_FF = 2816
FF_CONV_WIDTH = 3
FF_BLOCK = 256
NEG_INF = -1e30
ATT_PAD = BLOCK - N_META
ATT_SCALE = HEAD_DIM ** -0.5

ADAM_LR, ADAM_B1, ADAM_B2, ADAM_EPS, ADAM_WD, ADAM_STEP = 0.001, 0.9, 0.999, 1e-08, 0.01, 10

N_DEV = 8
LANES = 128
SUBLANES = 8
SCAN_CHUNK = 48
PAIR_ROWS = 4 * HEAD_DIM
V7X_VMEM_LIMIT = 56 * 1024 * 1024
ADAMW_BLOCK_ELEMS = 400 * 1024
GRAD_WIRE_DTYPE = bf16
MESH = pl.DeviceIdType.MESH
S = jax.ShapeDtypeStruct
HIGHEST = lax.Precision.HIGHEST


def _pc(body, **kw):
    return pl.pallas_call(body, **kw)


def _cparams(sem=None):
    return pltpu.CompilerParams(dimension_semantics=sem, vmem_limit_bytes=V7X_VMEM_LIMIT)


def _divisor_block(t, unit, limit):
    best = unit
    for rb in range(unit, limit + 1, unit):
        if t % rb == 0:
            best = rb
    assert t % best == 0, (t, unit)
    return best


def _row_block(t):
    return _divisor_block(t, 16, 704)


def _row_block8(t):
    return _divisor_block(t, 8, 344)


def _col_tile(n, cap):
    return _divisor_block(n, LANES, min(n, cap)) if n % LANES == 0 else n


def _full(shape):
    nd = len(shape)
    return pl.BlockSpec(shape, lambda *_: (0,) * nd)


def _sigmoid(x):
    return jax.nn.sigmoid(x)


_DIMS = {"nn": (((1,), (0,)), ((), ())), "nt": (((1,), (1,)), ((), ())), "tn": (((0,), (0,)), ((), ()))}
MM_MAX_K = 2816
MM_MAX_TM = 704
MM_MAX_TN = 1408


def _mm(a, b, mode, name, out_dtype=f32, res=None, b_row0=0):
    if mode == "nn":
        (m, k), n, k2 = a.shape, b.shape[1], a.shape[1]
        assert b_row0 % k == 0 and b_row0 + k <= b.shape[0], (a.shape, b.shape, b_row0)
    elif mode == "nt":
        (m, k), (n, k2) = a.shape, b.shape
    else:
        (k, m), (k2, n) = a.shape, b.shape
    assert k == k2, (a.shape, b.shape, mode)
    tm = _row_block(m) if m % LANES else _col_tile(m, MM_MAX_TM)
    tn = _col_tile(n, MM_MAX_TN)
    nk = 1 if (mode == "tn" or k <= MM_MAX_K) else k // MM_MAX_K
    tk = k // nk
    assert tk * nk == k
    dims = _DIMS[mode]

    def body(a_ref, b_ref, *rest):
        part = lax.dot_general(a_ref[...].astype(bf16), b_ref[...].astype(bf16), dims, preferred_element_type=f32)
        if nk == 1:
            o_ref = rest[-1]
            if res is not None:
                part = part + rest[0][...]
            o_ref[...] = part.astype(out_dtype)
            return
        o_ref, acc_ref = rest[-2], rest[-1]
        kk = pl.program_id(2)

        @pl.when(kk == 0)
        def _():
            acc_ref[...] = part

        @pl.when(kk > 0)
        def _():
            acc_ref[...] += part

        @pl.when(kk == nk - 1)
        def _():
            acc = acc_ref[...]
            if res is not None:
                acc = acc + rest[0][...]
            o_ref[...] = acc.astype(out_dtype)

    if mode == "tn":
        a_spec = pl.BlockSpec((k, tm), lambda i, j, kk: (0, i))
    else:
        a_spec = pl.BlockSpec((tm, tk), lambda i, j, kk: (i, kk))
    if mode == "nt":
        b_spec = pl.BlockSpec((tn, tk), lambda i, j, kk: (j, kk))
    else:
        b_spec = pl.BlockSpec((tk, tn), lambda i, j, kk: (kk + b_row0 // tk, j))
    o_spec = pl.BlockSpec((tm, tn), lambda i, j, kk: (i, j))
    ins, specs = [a, b], [a_spec, b_spec]
    if res is not None:
        ins.append(res)
        specs.append(o_spec)
    scratch = [pltpu.VMEM((tm, tn), f32)] if nk > 1 else []
    return _pc(body, name=name, grid=(m // tm, n // tn, nk), in_specs=specs, out_specs=o_spec,
               out_shape=S((m, n), out_dtype), scratch_shapes=scratch,
               compiler_params=_cparams(("arbitrary", "arbitrary", "arbitrary")))(*ins)


def _rms_fwd(x, g, name):
    t, d = x.shape
    rb = _row_block(t)

    def body(x_ref, g_ref, o_ref):
        xv = x_ref[...]
        rstd = lax.rsqrt(jnp.mean(xv * xv, axis=-1, keepdims=True) + RMS_EPS)
        o_ref[...] = (xv * rstd * g_ref[...]).astype(bf16)

    row = pl.BlockSpec((rb, d), lambda i: (i, 0))
    return _pc(body, name=name, grid=(t // rb,), in_specs=[row, _full((1, d))], out_specs=row,
               out_shape=S((t, d), bf16), compiler_params=_cparams(("arbitrary",)))(x, g.reshape(1, d))


def _rms_bwd(dy, x, g, dres, name):
    t, d = x.shape
    rb = _row_block8(t)

    def body(dy_ref, x_ref, g_ref, dres_ref, dx_ref, dg_ref):
        @pl.when(pl.program_id(0) == 0)
        def _():
            dg_ref[...] = jnp.zeros_like(dg_ref)
        xv, dyv = x_ref[...], dy_ref[...]
        rstd = lax.rsqrt(jnp.mean(xv * xv, axis=-1, keepdims=True) + RMS_EPS)
        xn = xv * rstd
        dg_ref[...] += jnp.sum(dyv * xn, axis=0, keepdims=True)
        dxh = dyv * g_ref[...]
        dx_ref[...] = dres_ref[...] + rstd * (dxh - xn * jnp.mean(dxh * xn, axis=-1, keepdims=True))

    row = pl.BlockSpec((rb, d), lambda i: (i, 0))
    return _pc(body, name=name, grid=(t // rb,), in_specs=[row, row, _full((1, d)), row],
               out_specs=(row, _full((1, d))), out_shape=(S((t, d), f32), S((1, d), f32)),
               compiler_params=_cparams(("arbitrary",)))(dy, x, g.reshape(1, d), dres)


def _final_loss(h, g, target_padded):
    t, d = h.shape
    rb = _row_block8(t)

    def body(x_ref, g_ref, t_ref, loss_ref, dx_ref, dg_ref):
        i = pl.program_id(0)

        @pl.when(i == 0)
        def _():
            dg_ref[...] = jnp.zeros_like(dg_ref)
            loss_ref[...] = jnp.zeros_like(loss_ref)
        xv = x_ref[...]
        rstd = lax.rsqrt(jnp.mean(xv * xv, axis=-1, keepdims=True) + RMS_EPS)
        xn = xv * rstd
        gv = g_ref[...]
        row = i * rb + lax.broadcasted_iota(jnp.int32, (rb, 1), 0)
        diff = jnp.where(row >= N_META, xn * gv - t_ref[...], 0.0)
        loss_ref[...] += 0.5 * jnp.sum(jnp.mean(diff * diff, axis=-1, keepdims=True))
        dout = diff * (1.0 / d)
        dg_ref[...] += jnp.sum(dout * xn, axis=0, keepdims=True)
        dxh = dout * gv
        dx_ref[...] = rstd * (dxh - xn * jnp.mean(dxh * xn, axis=-1, keepdims=True))

    row = pl.BlockSpec((rb, d), lambda i: (i, 0))
    return _pc(body, name="final_loss", grid=(t // rb,), in_specs=[row, _full((1, d)), row],
               out_specs=(_full((SUBLANES, LANES)), row, _full((1, d))),
               out_shape=(S((SUBLANES, LANES), f32), S((t, d), f32), S((1, d), f32)),
               compiler_params=_cparams(("arbitrary",)))(h, g.reshape(1, d), target_padded)


CONV_LEAD = 32


def _fill_front_padded(pad_ref, x, t):
    pad_ref[0:CONV_LEAD, :] = jnp.zeros((CONV_LEAD, x.shape[1]), f32)
    pad_ref[CONV_LEAD:CONV_LEAD + t, :] = x


def _fill_back_padded(pad_ref, x, t):
    pad_ref[0:t, :] = x
    pad_ref[t:t + CONV_LEAD, :] = jnp.zeros((CONV_LEAD, x.shape[1]), f32)


def _conv_rows(pad_ref, w_ref, kw, r0, nr):
    acc = None
    for j in range(kw):
        lo = CONV_LEAD + r0 - (kw - 1) + j
        term = w_ref[j:j + 1, :] * pad_ref[lo:lo + nr, :]
        acc = term if acc is None else acc + term
    return acc


def _conv_t_rows(padb_ref, w_ref, kw, r0, nr):
    acc = None
    for j in range(kw):
        lo = r0 + (kw - 1) - j
        term = w_ref[j:j + 1, :] * padb_ref[lo:lo + nr, :]
        acc = term if acc is None else acc + term
    return acc


def _conv_dw_rows(dy_blk, pad_ref, kw, r0, nr):
    out = []
    for j in range(kw):
        lo = CONV_LEAD + r0 - (kw - 1) + j
        out.append(jnp.sum(dy_blk * pad_ref[lo:lo + nr, :], axis=0, keepdims=True))
    return out


def _acc_list(a, b):
    return b if a is None else [x + y for x, y in zip(a, b)]


def _ev_a_conv(p, conv_a):
    t = p.shape[0]
    cr = _row_block8(t)
    nb = D_A // LANES

    def body(av_ref, ag_ref, w_ref, o_ref, pad_ref):
        _fill_front_padded(pad_ref, av_ref[...] * _sigmoid(ag_ref[...]), t)
        for r in range(t // cr):
            o_ref[r * cr:(r + 1) * cr, :] = _conv_rows(pad_ref, w_ref, CONV_A_WIDTH, r * cr, cr)

    col = lambda off: pl.BlockSpec((t, LANES), lambda j: (0, j + off))
    return _pc(body, name="ev_a_conv", grid=(nb,),
               in_specs=[col(0), col(nb), pl.BlockSpec((CONV_A_WIDTH, LANES), lambda j: (0, j))],
               out_specs=col(0), out_shape=S((t, D_A), f32),
               scratch_shapes=[pltpu.VMEM((t + CONV_LEAD, LANES), f32)],
               compiler_params=_cparams(("arbitrary",)))(p, p, conv_a)


def _ln_silu(uc, g, b):
    mu = jnp.mean(uc, axis=-1, keepdims=True)
    xc = uc - mu
    var = jnp.mean(xc * xc, axis=-1, keepdims=True)
    y = xc * lax.rsqrt(var + LN_EPS) * g + b
    return y * _sigmoid(y)


def _ev_a_norm(uc, g, b):
    t, d = uc.shape
    rb = _row_block(t)

    def body(u_ref, g_ref, b_ref, o_ref):
        o_ref[...] = _ln_silu(u_ref[...], g_ref[...], b_ref[...]).astype(bf16)

    row = pl.BlockSpec((rb, d), lambda i: (i, 0))
    return _pc(body, name="ev_a_norm", grid=(t // rb,), in_specs=[row, _full((1, d)), _full((1, d))],
               out_specs=row, out_shape=S((t, 2 * d), bf16), compiler_params=_cparams(("arbitrary",)))(uc, g, b)


def _ev_a_norm_bwd(dy, uc, g, b):
    t, d = uc.shape
    rb = _row_block8(t)

    def body(dy_ref, u_ref, g_ref, b_ref, du_ref, dg_ref, db_ref):
        @pl.when(pl.program_id(0) == 0)
        def _():
            dg_ref[...] = jnp.zeros_like(dg_ref)
            db_ref[...] = jnp.zeros_like(db_ref)
        _, vjp = jax.vjp(_ln_silu, u_ref[...], g_ref[...], b_ref[...])
        du, dg, db = vjp(dy_ref[...])
        du_ref[...] = du
        dg_ref[...] += dg
        db_ref[...] += db

    row = pl.BlockSpec((rb, d), lambda i: (i, 0))
    return _pc(body, name="ev_a_norm_bwd", grid=(t // rb,), in_specs=[row, row, _full((1, d)), _full((1, d))],
               out_specs=(row, _full((1, d)), _full((1, d))),
               out_shape=(S((t, d), f32), S((1, d), f32), S((1, d), f32)),
               compiler_params=_cparams(("arbitrary",)))(dy, uc, g, b)


def _ev_a_conv_bwd(duc, p, conv_a):
    t = p.shape[0]
    cr = _row_block8(t)
    nb = D_A // LANES

    def body(dy_ref, av_ref, ag_ref, w_ref, dav_ref, dag_ref, dw_ref, pad_ref, padb_ref):
        _fill_front_padded(pad_ref, av_ref[...] * _sigmoid(ag_ref[...]), t)
        _fill_back_padded(padb_ref, dy_ref[...], t)
        dw = None
        for r in range(t // cr):
            rows = slice(r * cr, (r + 1) * cr)
            du = _conv_t_rows(padb_ref, w_ref, CONV_A_WIDTH, r * cr, cr)
            avr = av_ref[rows, :]
            sgr = _sigmoid(ag_ref[rows, :])
            dav_ref[rows, :] = du * sgr
            dag_ref[rows, :] = du * avr * sgr * (1.0 - sgr)
            dw = _acc_list(dw, _conv_dw_rows(dy_ref[rows, :], pad_ref, CONV_A_WIDTH, r * cr, cr))
        for j in range(CONV_A_WIDTH):
            dw_ref[j:j + 1, :] = dw[j]

    col = lambda off: pl.BlockSpec((t, LANES), lambda j: (0, j + off))
    wsp = pl.BlockSpec((CONV_A_WIDTH, LANES), lambda j: (0, j))
    return _pc(body, name="ev_a_conv_bwd", grid=(nb,), in_specs=[col(0), col(0), col(nb), wsp],
               out_specs=(col(0), col(0), wsp),
               out_shape=(S((t, D_A), f32), S((t, D_A), f32), S((CONV_A_WIDTH, D_A), f32)),
               scratch_shapes=[pltpu.VMEM((t + CONV_LEAD, LANES), f32), pltpu.VMEM((t + CONV_LEAD, LANES), f32)],
               compiler_params=_cparams(("arbitrary",)))(duc, p, p, conv_a)


def _ev_b(p, conv_b, y):
    t = p.shape[0]
    cr = _row_block8(t)
    nb = D_A // LANES

    def body(gb_ref, gc_ref, xi_ref, w_ref, y_ref, o_ref, pad_ref, stage_ref):
        _fill_front_padded(pad_ref, gc_ref[...] * xi_ref[...], t)
        for r in range(t // cr):
            rows = slice(r * cr, (r + 1) * cr)
            stage_ref[rows, :] = gb_ref[rows, :] * _conv_rows(pad_ref, w_ref, CONV_B_WIDTH, r * cr, cr)
        o_ref[...] = stage_ref[...].astype(bf16)

    col = lambda off: pl.BlockSpec((t, LANES), lambda j: (0, j + off))
    return _pc(body, name="ev_b", grid=(nb,),
               in_specs=[col(2 * nb), col(3 * nb), col(4 * nb), pl.BlockSpec((CONV_B_WIDTH, LANES), lambda j: (0, j)), HBM],
               out_specs=col(nb), out_shape=S(y.shape, bf16), input_output_aliases={4: 0},
               scratch_shapes=[pltpu.VMEM((t + CONV_LEAD, LANES), f32), pltpu.VMEM((t, LANES), f32)],
               compiler_params=_cparams(("arbitrary",)))(p, p, p, conv_b, y)


def _ev_b_bwd(dy, p, conv_b):
    t = p.shape[0]
    cr = _row_block8(t)
    nb = D_A // LANES

    def body(dy_ref, gb_ref, gc_ref, xi_ref, w_ref, dgb_ref, dgc_ref, dxi_ref, dw_ref, pad_ref, padb_ref):
        _fill_front_padded(pad_ref, gc_ref[...] * xi_ref[...], t)
        _fill_back_padded(padb_ref, dy_ref[...] * gb_ref[...], t)
        dw = None
        for r in range(t // cr):
            rows = slice(r * cr, (r + 1) * cr)
            dgb_ref[rows, :] = dy_ref[rows, :] * _conv_rows(pad_ref, w_ref, CONV_B_WIDTH, r * cr, cr)
            dcx = _conv_t_rows(padb_ref, w_ref, CONV_B_WIDTH, r * cr, cr)
            dgc_ref[rows, :] = dcx * xi_ref[rows, :]
            dxi_ref[rows, :] = dcx * gc_ref[rows, :]
            dw = _acc_list(dw, _conv_dw_rows(padb_ref[rows, :], pad_ref, CONV_B_WIDTH, r * cr, cr))
        for j in range(CONV_B_WIDTH):
            dw_ref[j:j + 1, :] = dw[j]

    col = lambda off: pl.BlockSpec((t, LANES), lambda j: (0, j + off))
    wsp = pl.BlockSpec((CONV_B_WIDTH, LANES), lambda j: (0, j))
    return _pc(body, name="ev_b_bwd", grid=(nb,), in_specs=[col(nb), col(2 * nb), col(3 * nb), col(4 * nb), wsp],
               out_specs=(col(0), col(0), col(0), wsp),
               out_shape=(S((t, D_A), f32), S((t, D_A), f32), S((t, D_A), f32), S((CONV_B_WIDTH, D_A), f32)),
               scratch_shapes=[pltpu.VMEM((t + CONV_LEAD, LANES), f32), pltpu.VMEM((t + CONV_LEAD, LANES), f32)],
               compiler_params=_cparams(("arbitrary",)))(dy, p, p, p, conv_b)


def _ffn_mid(u, conv_w, conv_b, name):
    t = u.shape[0]
    cr = _row_block8(t)
    nb = D_FF // FF_BLOCK

    def one(gt_ref, vl_ref, w_ref, b_ref, o_ref, pad_ref, stage_ref):
        _fill_front_padded(pad_ref, gt_ref[...], t)
        for r in range(t // cr):
            rows = slice(r * cr, (r + 1) * cr)
            gc = _conv_rows(pad_ref, w_ref, FF_CONV_WIDTH, r * cr, cr) + b_ref[...]
            stage_ref[rows, :] = gc * _sigmoid(gc) * vl_ref[rows, :]
        o_ref[...] = stage_ref[...].astype(bf16)

    def body(*refs):
        for h in range(FF_BLOCK // LANES):
            one(*[r.at[:, pl.ds(h * LANES, LANES)] for r in refs[:5]], *refs[5:])

    col = lambda off: pl.BlockSpec((t, FF_BLOCK), lambda j: (0, j + off))
    return _pc(body, name=name, grid=(nb,),
               in_specs=[col(0), col(nb), pl.BlockSpec((FF_CONV_WIDTH, FF_BLOCK), lambda j: (0, j)),
                         pl.BlockSpec((1, FF_BLOCK), lambda j: (0, j))],
               out_specs=col(0), out_shape=S((t, D_FF), bf16),
               scratch_shapes=[pltpu.VMEM((t + CONV_LEAD, LANES), f32), pltpu.VMEM((t, LANES), f32)],
               compiler_params=_cparams(("arbitrary",)))(u, u, conv_w, conv_b.reshape(1, D_FF))


def _ffn_mid_bwd(dz, u, conv_w, conv_b, name):
    t = u.shape[0]
    cr = _row_block8(t)
    nb = D_FF // FF_BLOCK
    nh = FF_BLOCK // LANES

    def body(*refs):
        for h in range(nh):
            one(*[r.at[:, pl.ds(h * LANES, LANES)] for r in refs[:9]], *refs[9:])

    def one(dz_ref, gt_ref, vl_ref, w_ref, b_ref, du_ref, dv_ref, dw_ref, db_ref, pad_ref, padb_ref, stage_ref):
        _fill_front_padded(pad_ref, gt_ref[...], t)
        dw, db = None, None
        for r in range(t // cr):
            rows = slice(r * cr, (r + 1) * cr)
            lo = CONV_LEAD + r * cr - (FF_CONV_WIDTH - 1)
            taps = [pad_ref[lo + j:lo + j + cr, :] for j in range(FF_CONV_WIDTH)]
            gc = sum(w_ref[j:j + 1, :] * taps[j] for j in range(FF_CONV_WIDTH)) + b_ref[...]
            sg = _sigmoid(gc)
            dzr = dz_ref[rows, :]
            stage_ref[rows, :] = dzr * gc * sg
            dgc = dzr * vl_ref[rows, :] * sg * (1.0 + gc * (1.0 - sg))
            padb_ref[rows, :] = dgc
            dw = _acc_list(dw, [jnp.sum(dgc * tap, axis=0, keepdims=True) for tap in taps])
            pb = jnp.sum(dgc, axis=0, keepdims=True)
            db = pb if db is None else db + pb
        padb_ref[t:t + CONV_LEAD, :] = jnp.zeros((CONV_LEAD, LANES), f32)
        for r in range(t // cr):
            pad_ref[r * cr:(r + 1) * cr, :] = _conv_t_rows(padb_ref, w_ref, FF_CONV_WIDTH, r * cr, cr)
        du_ref[...] = pad_ref[0:t, :].astype(du_ref.dtype)
        dv_ref[...] = stage_ref[...].astype(dv_ref.dtype)
        for j in range(FF_CONV_WIDTH):
            dw_ref[j:j + 1, :] = dw[j]
        db_ref[...] = db

    col = lambda off: pl.BlockSpec((t, FF_BLOCK), lambda j: (0, j + off))
    wsp = pl.BlockSpec((FF_CONV_WIDTH, FF_BLOCK), lambda j: (0, j))
    bsp = pl.BlockSpec((1, FF_BLOCK), lambda j: (0, j))
    return _pc(body, name=name, grid=(nb,), in_specs=[col(0), col(0), col(nb), wsp, bsp],
               out_specs=(col(0), col(0), wsp, bsp),
               out_shape=(S((t, D_FF), bf16), S((t, D_FF), bf16), S((FF_CONV_WIDTH, D_FF), f32), S((1, D_FF), f32)),
               scratch_shapes=[pltpu.VMEM((t + CONV_LEAD, LANES), f32), pltpu.VMEM((t + CONV_LEAD, LANES), f32),
                               pltpu.VMEM((t, LANES), f32)],
               compiler_params=_cparams(("arbitrary",)))(dz, u, u, conv_w, conv_b.reshape(1, D_FF))


def _swap_halves(x):
    w = x.shape[1]
    lane = lax.broadcasted_iota(jnp.int32, x.shape, 1) % HEAD_DIM
    return jnp.where(lane < HEAD_DIM // 2, pltpu.roll(x, w - HEAD_DIM // 2, axis=1), pltpu.roll(x, HEAD_DIM // 2, axis=1))


def _rope_pack(patt, c64, s64):
    t = patt.shape[0]
    tp = t + ATT_PAD

    def body(p_ref, c_ref, s_ref, q_ref, k_ref, v_ref):
        c, s = c_ref[...], s_ref[...]

        def rope(x, nh):
            cc = jnp.concatenate([c] * nh, axis=1)
            ss = jnp.concatenate([s] * nh, axis=1)
            return x * cc + _swap_halves(x) * ss

        for ref, val in ((q_ref, rope(p_ref[:, 0:D_ATT], N_Q_HEADS)),
                         (k_ref, rope(p_ref[:, D_ATT:D_ATT + D_KV], N_KV_HEADS)),
                         (v_ref, p_ref[:, D_ATT + D_KV:ATT_COLS])):
            ref[0:ATT_PAD, :] = jnp.zeros((ATT_PAD, val.shape[1]), bf16)
            ref[ATT_PAD:tp, :] = val.astype(bf16)

    return _pc(body, name="rope_pack", in_specs=[_full((t, ATT_COLS)), _full((t, HEAD_DIM)), _full((t, HEAD_DIM))],
               out_specs=(_full((tp, D_ATT)), _full((tp, D_KV)), _full((tp, D_KV))), grid=(1,),
               out_shape=(S((tp, D_ATT), bf16), S((tp, D_KV), bf16), S((tp, D_KV), bf16)),
               compiler_params=_cparams(("arbitrary",)))(patt, c64, s64)


def _rope_bwd(dqp, dkp, dvp, c64, s64):
    tp = dqp.shape[0]
    t = tp - ATT_PAD

    def body(dq_ref, dk_ref, dv_ref, c_ref, s_ref, o_ref):
        c, s = c_ref[...], s_ref[...]

        def unrope(dy, nh):
            cc = jnp.concatenate([c] * nh, axis=1)
            ss = jnp.concatenate([s] * nh, axis=1)
            return dy * cc + _swap_halves(dy * ss)

        o_ref[:, 0:D_ATT] = unrope(dq_ref[ATT_PAD:tp, :], N_Q_HEADS).astype(bf16)
        o_ref[:, D_ATT:D_ATT + D_KV] = unrope(dk_ref[ATT_PAD:tp, :], N_KV_HEADS).astype(bf16)
        o_ref[:, D_ATT + D_KV:ATT_COLS] = dv_ref[ATT_PAD:tp, :].astype(bf16)

    return _pc(body, name="rope_bwd", grid=(1,),
               in_specs=[_full((tp, D_ATT)), _full((tp, D_KV)), _full((tp, D_KV)), _full((t, HEAD_DIM)), _full((t, HEAD_DIM))],
               out_specs=_full((t, ATT_COLS)), out_shape=S((t, ATT_COLS), bf16),
               compiler_params=_cparams(("arbitrary",)))(dqp, dkp, dvp, c64, s64)


def _attn_masks(n):
    rows = GQA_GROUP * BLOCK
    ri = lax.broadcasted_iota(jnp.int32, (rows, BLOCK), 0) % BLOCK
    ci = lax.broadcasted_iota(jnp.int32, (rows, BLOCK), 1)
    m_cur = (ci <= ri) & (ci >= jnp.where(n >= 1, 0, ATT_PAD))
    m_prev = ci > ri + jnp.where(n >= 2, 0, BLOCK)
    m_meta = ci >= jnp.where(n >= 1, ATT_PAD, BLOCK)
    return m_cur, m_prev, m_meta


def _attn_probs(qg, kc, kp, km, masks, skv):
    def scores(k, m):
        s = lax.dot_general(qg, k, _DIMS["nt"], preferred_element_type=f32) * ATT_SCALE
        return jnp.where(m, s, NEG_INF)
    s_c, s_p, s_m = scores(kc, masks[0]), scores(kp, masks[1]), scores(km, masks[2])
    mx = jnp.maximum(jnp.maximum(jnp.max(s_c, axis=-1, keepdims=True), jnp.max(s_p, axis=-1, keepdims=True)),
                     jnp.maximum(jnp.max(s_m, axis=-1, keepdims=True), skv))
    e_c, e_p, e_m, e_s = jnp.exp(s_c - mx), jnp.exp(s_p - mx), jnp.exp(s_m - mx), jnp.exp(skv - mx)
    den = (jnp.sum(e_c, axis=-1, keepdims=True) + jnp.sum(e_p, axis=-1, keepdims=True)
           + jnp.sum(e_m, axis=-1, keepdims=True) + e_s)
    inv = 1.0 / den
    return e_c * inv, e_p * inv, e_m * inv, e_s * inv


def _sink_rows(sk_ref, g):
    hrow = lax.broadcasted_iota(jnp.int32, (GQA_GROUP * BLOCK, 1), 0) // BLOCK
    skv = jnp.zeros((GQA_GROUP * BLOCK, 1), f32)
    for hh in range(GQA_GROUP):
        skv = jnp.where(hrow == hh, sk_ref[0, GQA_GROUP * g + hh], skv)
    return skv, hrow


def _stack_heads(ref, g):
    return jnp.concatenate([ref[:, (GQA_GROUP * g + hh) * HEAD_DIM:(GQA_GROUP * g + hh + 1) * HEAD_DIM]
                            for hh in range(GQA_GROUP)], axis=0)


def _attn_specs():
    blk = lambda w: pl.BlockSpec((BLOCK, w), lambda n: (n, 0))
    prev = pl.BlockSpec((BLOCK, D_KV), lambda n: (jnp.maximum(n - 1, 0), 0))
    meta = pl.BlockSpec((BLOCK, D_KV), lambda n: (0, 0))
    return blk, prev, meta


def _attn_fwd(qp, kp, vp, sinks):
    tp = qp.shape[0]
    blk, prev, meta = _attn_specs()

    def body(sk_ref, q_ref, kc_ref, kp_ref, km_ref, vc_ref, vp_ref, vm_ref, o_ref):
        masks = _attn_masks(pl.program_id(0))
        for g in range(N_KV_HEADS):
            sl = slice(g * HEAD_DIM, (g + 1) * HEAD_DIM)
            skv, _ = _sink_rows(sk_ref, g)
            p_c, p_p, p_m, _ = _attn_probs(_stack_heads(q_ref, g), kc_ref[:, sl], kp_ref[:, sl], km_ref[:, sl], masks, skv)
            o = (jnp.dot(p_c.astype(bf16), vc_ref[:, sl], preferred_element_type=f32)
                 + jnp.dot(p_p.astype(bf16), vp_ref[:, sl], preferred_element_type=f32)
                 + jnp.dot(p_m.astype(bf16), vm_ref[:, sl], preferred_element_type=f32))
            for hh in range(GQA_GROUP):
                h = GQA_GROUP * g + hh
                o_ref[:, h * HEAD_DIM:(h + 1) * HEAD_DIM] = o[hh * BLOCK:(hh + 1) * BLOCK].astype(bf16)

    return _pc(body, name="attn_fwd", grid=(tp // BLOCK,),
               in_specs=[pl.BlockSpec(memory_space=pltpu.SMEM), blk(D_ATT), blk(D_KV), prev, meta, blk(D_KV), prev, meta],
               out_specs=blk(D_ATT), out_shape=S((tp, D_ATT), bf16),
               compiler_params=_cparams(("arbitrary",)))(sinks, qp, kp, kp, kp, vp, vp, vp)


def _attn_bwd(qp, kp, vp, sinks, dop):
    tp = qp.shape[0]
    blk, prev, meta = _attn_specs()

    def body(sk_ref, q_ref, kc_ref, kp_ref, km_ref, vc_ref, vp_ref, vm_ref, do_ref, dq_ref, dk_ref, dv_ref, dsk_ref):
        n = pl.program_id(0)

        @pl.when(n == 0)
        def _():
            dk_ref[...] = jnp.zeros_like(dk_ref)
            dv_ref[...] = jnp.zeros_like(dv_ref)
            dsk_ref[...] = jnp.zeros_like(dsk_ref)
        masks = _attn_masks(n)
        cur = pl.ds(pl.multiple_of(n * BLOCK, BLOCK), BLOCK)
        prv = pl.ds(pl.multiple_of(jnp.maximum(n - 1, 0) * BLOCK, BLOCK), BLOCK)
        lane = lax.broadcasted_iota(jnp.int32, (1, LANES), 1)
        dsk = jnp.zeros((1, LANES), f32)
        for g in range(N_KV_HEADS):
            sl = slice(g * HEAD_DIM, (g + 1) * HEAD_DIM)
            skv, hrow = _sink_rows(sk_ref, g)
            qg = _stack_heads(q_ref, g)
            dog = _stack_heads(do_ref, g)
            ks = (kc_ref[:, sl], kp_ref[:, sl], km_ref[:, sl])
            vs = (vc_ref[:, sl], vp_ref[:, sl], vm_ref[:, sl])
            probs = _attn_probs(qg, ks[0], ks[1], ks[2], masks, skv)
            dps = [lax.dot_general(dog, v, _DIMS["nt"], preferred_element_type=f32) for v in vs]
            delta = sum(jnp.sum(p * dp, axis=-1, keepdims=True) for p, dp in zip(probs[:3], dps))
            dss = [(p * (dp - delta) * ATT_SCALE).astype(bf16) for p, dp in zip(probs[:3], dps)]
            dq = sum(jnp.dot(ds, k, preferred_element_type=f32) for ds, k in zip(dss, ks))
            for hh in range(GQA_GROUP):
                h = GQA_GROUP * g + hh
                dq_ref[:, h * HEAD_DIM:(h + 1) * HEAD_DIM] = dq[hh * BLOCK:(hh + 1) * BLOCK]
                dsk = dsk + jnp.where(lane == h, -jnp.sum(jnp.where(hrow == hh, probs[3] * delta, 0.0)), 0.0)
            for rows, p, ds in zip((cur, prv, slice(0, BLOCK)), probs[:3], dss):
                dv_ref[rows, sl] += lax.dot_general(p.astype(bf16), dog, _DIMS["tn"], preferred_element_type=f32)
                dk_ref[rows, sl] += lax.dot_general(ds, qg, _DIMS["tn"], preferred_element_type=f32)
        dsk_ref[...] += dsk

    return _pc(body, name="attn_bwd", grid=(tp // BLOCK,),
               in_specs=[pl.BlockSpec(memory_space=pltpu.SMEM), blk(D_ATT), blk(D_KV), prev, meta, blk(D_KV), prev, meta,
                         blk(D_ATT)],
               out_specs=(blk(D_ATT), _full((tp, D_KV)), _full((tp, D_KV)), _full((1, LANES))),
               out_shape=(S((tp, D_ATT), f32), S((tp, D_KV), f32), S((tp, D_KV), f32), S((1, LANES), f32)),
               compiler_params=_cparams(("arbitrary",)))(sinks, qp, kp, kp, kp, vp, vp, vp, dop)


def _seg(x, bm):
    hi = x.astype(bf16)
    lo = (x - hi.astype(f32)).astype(bf16)
    return jnp.dot(jnp.concatenate([hi, lo], axis=1), bm, preferred_element_type=f32)


@jax.custom_vjp
def _seg_linear(x, bm):
    return _seg(x, bm)


_seg_linear.defvjp(lambda x, bm: (_seg(x, bm), bm), lambda bm, ct: (_seg(ct, bm), jnp.zeros_like(bm)))


def _softplus(y):
    return jnp.maximum(y, 0.0) + jnp.log(1.0 + jnp.exp(-jnp.abs(y)))


def _prep_fn(xr, xk, xwd, xad, xgd, w0, w2, a0, a2, g2, k_k, k_a, bm, seg=_seg):
    xw = w0 + jnp.dot(jnp.tanh(xwd), w2, preferred_element_type=f32)
    decay = jnp.exp(-jnp.exp(-_softplus(-xw) - 0.5))
    alpha = _sigmoid(a0 + jnp.dot(xad, a2, preferred_element_type=f32))
    g = jnp.dot(_sigmoid(xgd), g2, preferred_element_type=f32)
    kk = xk * k_k
    kkn = kk / jnp.maximum(jnp.sqrt(seg(kk * kk, bm)), 1e-12)
    k2 = xk * (1.0 + (alpha - 1.0) * k_a)
    return decay, k2, -kkn, kkn * alpha, g


def _split_cols(x):
    o1, o2, o3 = 3 * D_R, 3 * D_R + LORA_W, 3 * D_R + LORA_W + LORA_A
    return x[:, 0:D_R], x[:, D_R:2 * D_R], x[:, 2 * D_R:o1], x[:, o1:o2], x[:, o2:o3], x[:, o3:RWKV_COLS]


def _shifted(sh_ref, x, halo, first, rb):
    sh_ref[0:SUBLANES, :] = jnp.where(first, 0.0, halo)
    sh_ref[SUBLANES:SUBLANES + rb, :] = x
    return sh_ref[SUBLANES - 1:SUBLANES - 1 + rb, :]


_PREP_PARAMS = ("od_w0", "od_w2", "od_a0", "od_a2", "od_g2", "od_k_k", "od_k_a")


def _rwkv_prep(pr, mu, params, bm):
    t = pr.shape[0]
    rb = _row_block8(t)
    hb = rb // SUBLANES

    def body(pr_ref, halo_ref, mu_ref, w0, w2, a0, a2, g2, kk_ref, ka_ref, bm_ref, *outs_sh):
        outs, sh_ref = outs_sh[:-1], outs_sh[-1]
        x = pr_ref[...]
        prev = _shifted(sh_ref, x, halo_ref[...], pl.program_id(0) == 0, rb)
        xr, xk, xv, xwd, xad, xgd = _split_cols(x + (prev - x) * mu_ref[...])
        bmv = bm_ref[...]
        decay, k2, a_s, b_s, g = _prep_fn(xr, xk, xwd, xad, xgd, w0[...], w2[...], a0[...], a2[...], g2[...],
                                          kk_ref[...], ka_ref[...], bmv)
        vals = (xr, xv, decay, k2, a_s, b_s, decay * xr, _seg(b_s * xr, bmv), _seg(k2 * xr, bmv), g)
        for ref, val in zip(outs, vals):
            ref[...] = val

    row = pl.BlockSpec((rb, RWKV_COLS), lambda i: (i, 0))
    halo = pl.BlockSpec((SUBLANES, RWKV_COLS), lambda i: (jnp.maximum(i * hb - 1, 0), 0))
    orow = pl.BlockSpec((rb, D_R), lambda i: (i, 0))
    return _pc(body, name="rwkv_prep", grid=(t // rb,),
               in_specs=[row, halo, _full((1, RWKV_COLS))] + [_full(p.shape) for p in params] + [_full(bm.shape)],
               out_specs=(orow,) * 10, out_shape=(S((t, D_R), f32),) * 10,
               scratch_shapes=[pltpu.VMEM((rb + SUBLANES, RWKV_COLS), f32)],
               compiler_params=_cparams(("arbitrary",)))(pr, pr, mu, *params, bm)


def _rwkv_prep_bwd(pr, mu, params, bm, cts):
    t = pr.shape[0]
    rb = _row_block8(t)
    hb = rb // SUBLANES
    counts = [len(c) for c in cts]
    flat = [a for c in cts for a in c]

    def body(pr_ref, halo_ref, mu_ref, w0, w2, a0, a2, g2, kk_ref, ka_ref, bm_ref, *rest):
        ct_refs, rest = rest[:len(flat)], rest[len(flat):]
        dx_ref, dmu_ref = rest[0], rest[1]
        dpar_refs, sh_ref = rest[2:9], rest[9]

        @pl.when(pl.program_id(0) == 0)
        def _():
            dmu_ref[...] = jnp.zeros_like(dmu_ref)
            for r in dpar_refs:
                r[...] = jnp.zeros_like(r)
        sums, pos = [], 0
        for c in counts:
            sums.append(sum(r[...] for r in ct_refs[pos:pos + c]))
            pos += c
        x = pr_ref[...]
        prev = _shifted(sh_ref, x, halo_ref[...], pl.program_id(0) == 0, rb)
        xr, xk, xv, xwd, xad, xgd = _split_cols(x + (prev - x) * mu_ref[...])
        bmv = bm_ref[...]
        _, vjp = jax.vjp(lambda *a: _prep_fn(*a, bmv, _seg_linear), xr, xk, xwd, xad, xgd, w0[...], w2[...], a0[...], a2[...],
                         g2[...], kk_ref[...], ka_ref[...])
        grads = vjp(tuple(sums[:5]))
        dxr, dxk, dxwd, dxad, dxgd = grads[:5]
        o1, o2, o3 = 3 * D_R, 3 * D_R + LORA_W, 3 * D_R + LORA_W + LORA_A
        dx_ref[:, 0:D_R] = dxr + sums[5]
        dx_ref[:, D_R:2 * D_R] = dxk
        dx_ref[:, 2 * D_R:o1] = sums[6]
        dx_ref[:, o1:o2] = dxwd
        dx_ref[:, o2:o3] = dxad
        dx_ref[:, o3:RWKV_COLS] = dxgd
        dmu_ref[...] += jnp.sum(dx_ref[...] * (prev - x), axis=0, keepdims=True)
        for r, gval in zip(dpar_refs, grads[5:]):
            r[...] += gval

    row = pl.BlockSpec((rb, RWKV_COLS), lambda i: (i, 0))
    halo = pl.BlockSpec((SUBLANES, RWKV_COLS), lambda i: (jnp.maximum(i * hb - 1, 0), 0))
    crow = pl.BlockSpec((rb, D_R), lambda i: (i, 0))
    return _pc(body, name="rwkv_prep_bwd", grid=(t // rb,),
               in_specs=[row, halo, _full((1, RWKV_COLS))] + [_full(p.shape) for p in params] + [_full(bm.shape)]
               + [crow] * len(flat),
               out_specs=(row, _full((1, RWKV_COLS))) + tuple(_full(p.shape) for p in params),
               out_shape=(S((t, RWKV_COLS), f32), S((1, RWKV_COLS), f32)) + tuple(S(p.shape, f32) for p in params),
               scratch_shapes=[pltpu.VMEM((rb + SUBLANES, RWKV_COLS), f32)],
               compiler_params=_cparams(("arbitrary",)))(pr, pr, mu, *params, bm, *flat)


def _shift_bwd(dxs, mu):
    t = dxs.shape[0]
    rb = _row_block(t)
    hb = rb // SUBLANES
    nblk = t // rb

    def body(dx_ref, halo_ref, mu_ref, o_ref, sh_ref):
        dx = dx_ref[...]
        sh_ref[0:rb, :] = dx
        sh_ref[rb:rb + SUBLANES, :] = jnp.where(pl.program_id(0) == nblk - 1, 0.0, halo_ref[...])
        m = mu_ref[...]
        o_ref[...] = (dx * (1.0 - m) + sh_ref[1:1 + rb, :] * m).astype(bf16)

    row = pl.BlockSpec((rb, RWKV_COLS), lambda i: (i, 0))
    halo = pl.BlockSpec((SUBLANES, RWKV_COLS), lambda i: (jnp.minimum((i + 1) * hb, t // SUBLANES - 1), 0))
    return _pc(body, name="rwkv_shift_bwd", grid=(nblk,), in_specs=[row, halo, _full((1, RWKV_COLS))],
               out_specs=row, out_shape=S((t, RWKV_COLS), bf16),
               scratch_shapes=[pltpu.VMEM((rb + SUBLANES, RWKV_COLS), f32)],
               compiler_params=_cparams(("arbitrary",)))(dxs, dxs, mu)


def _post_fn(y, xr, k2, xv, g, lg, lb, rk, bm, seg=_seg):
    inv_n = 1.0 / HEAD_DIM
    yc = y - seg(y, bm) * inv_n
    var = seg(yc * yc, bm) * inv_n
    yn = yc * lax.rsqrt(var + RWKV_GN_EPS) * lg + lb
    return (yn + seg(xr * k2 * rk, bm) * xv) * g


def _rwkv_post(y, xr, k2, xv, g, lg, lb, rk, bm):
    t = y.shape[0]
    rb = _row_block8(t)

    def body(y_ref, xr_ref, k2_ref, xv_ref, g_ref, lg_ref, lb_ref, rk_ref, bm_ref, o_ref):
        o_ref[...] = _post_fn(y_ref[...], xr_ref[...], k2_ref[...], xv_ref[...], g_ref[...], lg_ref[...], lb_ref[...],
                              rk_ref[...], bm_ref[...])

    row = pl.BlockSpec((rb, D_R), lambda i: (i, 0))
    vec = _full((1, D_R))
    return _pc(body, name="rwkv_post", grid=(t // rb,), in_specs=[row] * 5 + [vec] * 3 + [_full(bm.shape)],
               out_specs=row, out_shape=S((t, D_R), f32),
               compiler_params=_cparams(("arbitrary",)))(y, xr, k2, xv, g, lg, lb, rk, bm)


def _rwkv_post_bwd(dy1, y, xr, k2, xv, g, lg, lb, rk, bm):
    t = y.shape[0]
    rb = _row_block8(t)

    def body(dy_ref, y_ref, xr_ref, k2_ref, xv_ref, g_ref, lg_ref, lb_ref, rk_ref, bm_ref, *outs):
        @pl.when(pl.program_id(0) == 0)
        def _():
            for r in outs[5:]:
                r[...] = jnp.zeros_like(r)
        bmv = bm_ref[...]
        _, vjp = jax.vjp(lambda *a: _post_fn(*a, bmv, _seg_linear), y_ref[...], xr_ref[...], k2_ref[...], xv_ref[...], g_ref[...],
                         lg_ref[...], lb_ref[...], rk_ref[...])
        grads = vjp(dy_ref[...])
        for r, gval in zip(outs[:5], grads[:5]):
            r[...] = gval
        for r, gval in zip(outs[5:], grads[5:]):
            r[...] += gval

    row = pl.BlockSpec((rb, D_R), lambda i: (i, 0))
    vec = _full((1, D_R))
    return _pc(body, name="rwkv_post_bwd", grid=(t // rb,),
               in_specs=[pl.BlockSpec((rb, D_R), lambda i: (i, 1))] + [row] * 5 + [vec] * 3 + [_full(bm.shape)],
               out_specs=(row,) * 5 + (vec,) * 3, out_shape=(S((t, D_R), f32),) * 5 + (S((1, D_R), f32),) * 3,
               compiler_params=_cparams(("arbitrary",)))(dy1, y, xr, k2, xv, g, lg, lb, rk, bm)


def _seg2(x, bb):
    hi = x.astype(bf16)
    lo = (x - hi.astype(f32)).astype(bf16)
    return jnp.dot(jnp.concatenate([hi, lo], axis=1), bb, preferred_element_type=f32)


def _row4(rows, j):
    return jnp.concatenate([jnp.broadcast_to(rows[j:j + 1, p * LANES:(p + 1) * LANES], (HEAD_DIM, LANES))
                            for p in range(4)], axis=0)


def _scan_consts():
    lane_group = jnp.arange(LANES) // HEAD_DIM
    b128 = (lane_group[:, None] == lane_group[None, :]).astype(bf16)
    bb = jnp.concatenate([b128, b128], axis=0)
    qsel = (jnp.arange(PAIR_ROWS)[:, None] % HEAD_DIM == jnp.arange(LANES)[None, :] % HEAD_DIM).astype(f32)
    return bb, qsel


def _store_cols(acc_ref, o_ref, tc):
    for p in range(4):
        blk = acc_ref[p * HEAD_DIM:(p + 1) * HEAD_DIM, :].T
        o_ref[:, (2 * p) * HEAD_DIM:(2 * p + 1) * HEAD_DIM] = blk[0:tc]
        o_ref[:, (2 * p + 1) * HEAD_DIM:(2 * p + 2) * HEAD_DIM] = blk[HEAD_DIM:HEAD_DIM + tc]


PAIR_GROUP = 2 * SUBLANES


def _rwkv_pairs(w, a, b, k, wr, bm):
    t = w.shape[0]
    rb = _row_block8(t)

    def body(w_ref, a_ref, b_ref, k_ref, wr_ref, bm_ref, *outs_sh):
        outs, sh_ref = outs_sh[:-1], outs_sh[-1]

        def second(ref):
            sh_ref[0:rb, :] = ref[...]
            sh_ref[rb:rb + SUBLANES, :] = jnp.zeros((SUBLANES, D_R), f32)
            return sh_ref[1:1 + rb, :]

        w1, b1, k1 = w_ref[...], b_ref[...], k_ref[...]
        w2, a2, wr2 = second(w_ref), second(a_ref), second(wr_ref)
        bmv = bm_ref[...]
        vals = (w1 * a2, w1 * wr2, w1 * w2, b1 * w2, k1 * w2, _seg(b1 * a2, bmv), _seg(k1 * a2, bmv),
                _seg(b1 * wr2, bmv), _seg(k1 * wr2, bmv))
        for ref, val in zip(outs, vals):
            ref[...] = val

    row = pl.BlockSpec((rb, D_R), lambda i: (i, 0))
    return _pc(body, name="rwkv_pairs", grid=(t // rb,), in_specs=[row] * 5 + [_full(bm.shape)],
               out_specs=(row,) * 9, out_shape=(S((t, D_R), f32),) * 9,
               scratch_shapes=[pltpu.VMEM((rb + SUBLANES, D_R), f32)],
               compiler_params=_cparams(("arbitrary",)))(w, a, b, k, wr, bm)


def _wkv_fwd(w, k, v, a, b, wr, br, kr, pairs):
    t = w.shape[0]
    tc = SCAN_CHUNK
    bb, qsel = _scan_consts()

    def body(*refs):
        step_refs, pair_refs = refs[0:8], refs[8:17]
        bb_ref, q_ref, y_ref, st_ref, sa_ref, vb_ref, s_scr, yacc = refs[17:]

        @pl.when(pl.program_id(0) == 0)
        def _():
            s_scr[...] = jnp.zeros_like(s_scr)
        bbv, qv = bb_ref[...], q_ref[...]
        lane64 = lax.broadcasted_iota(jnp.int32, (PAIR_ROWS, LANES), 1) % HEAD_DIM

        def halves(x):
            hi = x.astype(bf16)
            return jnp.concatenate([hi, (x - hi.astype(f32)).astype(bf16)], axis=1)

        def group(gi, s):
            base = pl.multiple_of(gi * PAIR_GROUP, PAIR_GROUP)
            w16, k16, v16, a16, b16, wr16, br16, kr16 = step_refs
            a2p, r2p, w12p, b1wp, k1wp, betap, kappap, bwrp, kwrp = pair_refs

            def rows8(ref, j):
                return ref[pl.ds(base + (j // SUBLANES) * SUBLANES, SUBLANES), :]

            def bcast(rows, j, p):
                return jnp.broadcast_to(rows[j % SUBLANES:j % SUBLANES + 1, p * LANES:(p + 1) * LANES], (HEAD_DIM, LANES))

            step = lambda ref, j, p: bcast(rows8(ref, j), j, p)
            qp = qv[0:HEAD_DIM]
            lane = lane64[0:HEAD_DIM]
            for q in range(SUBLANES):
                j1, j2 = 2 * q, 2 * q + 1
                t1 = base + j1
                nxt = []
                for p in range(4):
                    sl = slice(p * HEAD_DIM, (p + 1) * HEAD_DIM)
                    sp = s[sl]
                    lhs = [halves(jnp.concatenate([sp * step(a16, j1, p), sp * step(a2p, j1, p), sp * step(wr16, j1, p),
                                                   sp * step(r2p, j1, p)], axis=0))]
                    for j in (j1, j2):
                        v8 = rows8(v16, j)
                        vh8 = v8.astype(bf16).astype(f32)
                        lhs.append(jnp.concatenate([(qp * bcast(vh8, j, p)).astype(bf16),
                                                    (qp * bcast(v8 - vh8, j, p)).astype(bf16)], axis=1))
                    r = jnp.dot(jnp.concatenate(lhs, axis=0), bbv, preferred_element_type=f32)
                    sa1, p2, z1, z2, vb1, vb2 = (r[n * HEAD_DIM:(n + 1) * HEAD_DIM] for n in range(6))
                    sa2 = p2 + sa1 * step(betap, j1, p) + vb1 * step(kappap, j1, p)
                    y1 = z1 + sa1 * step(br16, j1, p) + vb1 * step(kr16, j1, p)
                    y2 = (z2 + sa1 * step(bwrp, j1, p) + vb1 * step(kwrp, j1, p)) + (sa2 * step(br16, j2, p)
                                                                                      + vb2 * step(kr16, j2, p))
                    yacc[sl, :] = jnp.where(lane == t1, y1, jnp.where(lane == t1 + 1, y2, yacc[sl, :]))
                    st_ref[base // 2 + q, sl, :] = sp
                    sa_ref[t1, sl, :] = sa1
                    sa_ref[t1 + 1, sl, :] = sa2
                    vb_ref[t1, sl, :] = vb1
                    vb_ref[t1 + 1, sl, :] = vb2
                    nxt.append(((sp * step(w12p, j1, p) + sa1 * step(b1wp, j1, p)) + vb1 * step(k1wp, j1, p))
                               + (sa2 * step(b16, j2, p) + vb2 * step(k16, j2, p)))
                s = jnp.concatenate(nxt, axis=0)
            return s

        s_scr[...] = lax.fori_loop(0, tc // PAIR_GROUP, group, s_scr[...])
        _store_cols(yacc, y_ref, tc)

    row = pl.BlockSpec((tc, D_R), lambda c: (c, 0))
    tiles = pl.BlockSpec((tc, PAIR_ROWS, LANES), lambda c: (c, 0, 0))
    return _pc(body, name="wkv_fwd", grid=(t // tc,),
               in_specs=[row] * 17 + [_full(bb.shape), _full(qsel.shape)],
               out_specs=(row, pl.BlockSpec((tc // 2, PAIR_ROWS, LANES), lambda c: (c, 0, 0)), tiles, tiles),
               out_shape=(S((t, D_R), f32), S((t // 2, PAIR_ROWS, LANES), f32)) + (S((t, PAIR_ROWS, LANES), f32),) * 2,
               scratch_shapes=[pltpu.VMEM((PAIR_ROWS, LANES), f32), pltpu.VMEM((PAIR_ROWS, LANES), f32)],
               compiler_params=_cparams(("arbitrary",)))(w, k, v, a, b, wr, br, kr, *pairs, bb, qsel)


def _wkv_bwd(sprev, sab, vbb, w, k, a, b, r, dy):
    t = w.shape[0]
    tc = SCAN_CHUNK
    nc = t // tc
    bb, qsel = _scan_consts()

    def body(st_ref, sa_ref, vb_ref, w_ref, k_ref, a_ref, b_ref, r_ref, dy_ref, bb_ref, q_ref,
             dr_ref, dw_ref, dk_ref, dv_ref, da_ref, db_ref, g_scr, dvacc, rows_scr):
        @pl.when(pl.program_id(0) == 0)
        def _():
            g_scr[...] = jnp.zeros_like(g_scr)
        bbv, qv = bb_ref[...], q_ref[...]
        lane64 = lax.broadcasted_iota(jnp.int32, (PAIR_ROWS, LANES), 1) % HEAD_DIM
        outs = (dr_ref, dw_ref, db_ref, dk_ref, da_ref)

        def colsums(slot, j, x):
            for p in range(4):
                rows_scr[slot, j:j + 1, p * LANES:(p + 1) * LANES] = jnp.sum(x[p * HEAD_DIM:(p + 1) * HEAD_DIM], axis=0,
                                                                           keepdims=True)

        def group(i, g):
            base = pl.multiple_of((tc // SUBLANES - 1 - i) * SUBLANES, SUBLANES)
            w8, k8, a8, b8, r8, dy8 = (ref[pl.ds(base, SUBLANES), :] for ref in (w_ref, k_ref, a_ref, b_ref, r_ref, dy_ref))

            def after_step(j, sp):
                return sp * _row4(w8, j) + sa_ref[base + j] * _row4(b8, j) + vb_ref[base + j] * _row4(k8, j)

            def back_step(j, sp, s_t, g):
                tt = base + j
                u, vb = sa_ref[tt], vb_ref[tt]
                a4, b4, w4, k4 = _row4(a8, j), _row4(b8, j), _row4(w8, j), _row4(k8, j)
                dyb = _seg2(qv * _row4(dy8, j), bbv)
                g = g + dyb * _row4(r8, j)
                rr2 = _seg2(jnp.concatenate([g * b4, g * k4], axis=0), bbv)
                du, dvb = rr2[0:PAIR_ROWS], rr2[PAIR_ROWS:2 * PAIR_ROWS]
                for slot, val in enumerate((s_t * dyb, g * sp, g * u, g * vb, sp * du)):
                    colsums(slot, j, val)
                dvacc[...] = jnp.where(lane64 == tt, dvb, dvacc[...])
                return g * w4 + du * a4

            for q in reversed(range(SUBLANES // 2)):
                s0 = st_ref[base // 2 + q]
                s1 = after_step(2 * q, s0)
                g = back_step(2 * q + 1, s1, after_step(2 * q + 1, s1), g)
                g = back_step(2 * q, s0, s1, g)
            for slot, ref in enumerate(outs):
                ref[pl.ds(base, SUBLANES), :] = rows_scr[slot]
            return g

        g_scr[...] = lax.fori_loop(0, tc // SUBLANES, group, g_scr[...])
        _store_cols(dvacc, dv_ref, tc)

    row = pl.BlockSpec((tc, D_R), lambda c: (nc - 1 - c, 0))
    tiles = pl.BlockSpec((tc, PAIR_ROWS, LANES), lambda c: (nc - 1 - c, 0, 0))
    states = pl.BlockSpec((tc // 2, PAIR_ROWS, LANES), lambda c: (nc - 1 - c, 0, 0))
    return _pc(body, name="wkv_bwd", grid=(nc,),
               in_specs=[states, tiles, tiles] + [row] * 6 + [_full(bb.shape), _full(qsel.shape)],
               out_specs=(row,) * 6, out_shape=(S((t, D_R), f32),) * 6,
               scratch_shapes=[pltpu.VMEM((PAIR_ROWS, LANES), f32), pltpu.VMEM((PAIR_ROWS, LANES), f32),
                               pltpu.VMEM((5, SUBLANES, D_R), f32)],
               compiler_params=_cparams(("arbitrary",)))(sprev, sab, vbb, w, k, a, b, r, dy, bb, qsel)


def _rope_tables(t):
    half = HEAD_DIM // 2
    inv = ROPE_THETA ** (-jnp.arange(half, dtype=f32) / half)
    ang = jnp.arange(t, dtype=f32)[:, None] * inv[None, :]
    cos, sin = jnp.cos(ang), jnp.sin(ang)
    return jnp.concatenate([cos, cos], axis=1), jnp.concatenate([-sin, sin], axis=1)


def _head_matrix():
    grp = jnp.arange(D_R) // HEAD_DIM
    b = (grp[:, None] == grp[None, :]).astype(bf16)
    return jnp.concatenate([b, b], axis=0)


def _ffn_fwd(h, g, get_w, conv_w, conv_b, i):
    hf = _rms_fwd(h, g, f"ffn{i}_norm")
    w_up_t = get_w(f"ff{i}_up", hf)
    u = _mm(hf, w_up_t, "nt", f"ffn{i}_up")
    z = _ffn_mid(u, conv_w, conv_b, f"ffn{i}_mid")
    w_down = get_w(f"ff{i}_down", z)
    return _mm(z, w_down, "nn", f"ffn{i}_down", res=h), (hf, u, z), w_up_t, w_down


def _ffn_bwd(dh, h, saved, g, w_up_t, conv_w, conv_b, w_down, i, put_g):
    hf, u, z = saved
    dz = _mm(dh, w_down, "nt", f"ffn{i}_dz")
    g_down = _mm(z, dh, "tn", f"ffn{i}_gdown", out_dtype=GRAD_WIRE_DTYPE)
    tok = put_g(f"ff{i}_down", g_down)
    dgate, dval, g_conv, g_convb = _ffn_mid_bwd(dz, u, conv_w, conv_b + tok, f"ffn{i}_mid_bwd")
    g_up_t = jnp.concatenate([_mm(dgate, hf, "tn", f"ffn{i}_gup_gate", out_dtype=GRAD_WIRE_DTYPE),
                              _mm(dval, hf, "tn", f"ffn{i}_gup_val", out_dtype=GRAD_WIRE_DTYPE)], axis=0)
    tok = put_g(f"ff{i}_up", g_up_t)
    dhf = _mm(dval, w_up_t, "nn", f"ffn{i}_dhf_val", b_row0=D_FF, res=_mm(dgate, w_up_t, "nn", f"ffn{i}_dhf_gate"))
    dh_in, g_norm = _rms_bwd(dhf, h, g + tok, dh, f"ffn{i}_norm_bwd")
    return dh_in, dict(conv=g_conv, conv_b=g_convb, norm=g_norm)


def _local_step(x, target, W, get_w, put_g, put_small, tok0):
    t = N_META + x.shape[0]
    c64, s64 = _rope_tables(t)
    bm = _head_matrix()
    h0 = jnp.concatenate([W["meta_tokens"], x], axis=0)

    ev_w_in_t, ev_w_out = get_w("ev_in", None), get_w("ev_out", None)
    hn0 = _rms_fwd(h0, W["norm_mix"][0] + tok0, "mix0_norm")
    p0 = _mm(hn0, ev_w_in_t, "nt", "ev_in")
    uc = _ev_a_conv(p0, W["ev_conv_a"])
    y0 = _ev_b(p0, W["ev_conv_b"], _ev_a_norm(uc, W["ev_ln_a_g"], W["ev_ln_a_b"]))
    h1 = _mm(y0, ev_w_out, "nn", "ev_out", res=h0)
    h2, ffn0, ff0_up_t, ff0_down = _ffn_fwd(h1, W["norm_ffn"][0], get_w, W["ff_conv"][0], W["ff_conv_b"][0], 0)

    hn1 = _rms_fwd(h2, W["norm_mix"][1], "mix1_norm")
    od_w_in_t = get_w("od_in", hn1)
    w_att, w_rwkv = od_w_in_t[:ATT_COLS], od_w_in_t[ATT_COLS:]
    pr = _mm(hn1, w_rwkv, "nt", "od_in_rwkv")
    qp, kp, vp = _rope_pack(_mm(hn1, w_att, "nt", "od_in_att"), c64, s64)
    op = _attn_fwd(qp, kp, vp, W["od_sinks"])
    prep_params = [W[n] for n in _PREP_PARAMS]
    xr, xv, decay, k2, a_s, b_s, wr, br, kr, gate = _rwkv_prep(pr, W["od_mu"], prep_params, bm)
    pairs = _rwkv_pairs(decay, a_s, b_s, k2, wr, bm)
    ysc, sprev, sab, vbb = _wkv_fwd(decay, k2, xv, a_s, b_s, wr, br, kr, pairs)
    rk = W["od_r_k"].reshape(1, D_R)
    yr = _rwkv_post(ysc, xr, k2, xv, gate, W["od_lnx_g"], W["od_lnx_b"], rk, bm)
    y1 = jnp.concatenate([op[ATT_PAD:], yr.astype(bf16)], axis=1)
    od_w_out = get_w("od_out", y1)
    h3 = _mm(y1, od_w_out, "nn", "od_out", res=h2)
    h4, ffn1, ff1_up_t, ff1_down = _ffn_fwd(h3, W["norm_ffn"][1], get_w, W["ff_conv"][1], W["ff_conv_b"][1], 1)

    tgt = jnp.concatenate([jnp.zeros((N_META, D_MODEL), f32), target], axis=0)
    loss, dh4, g_norm_final = _final_loss(h4, W["norm_final"], tgt)

    dh3, gf1 = _ffn_bwd(dh4, h3, ffn1, W["norm_ffn"][1], ff1_up_t, W["ff_conv"][1], W["ff_conv_b"][1], ff1_down, 1, put_g)
    dy1 = _mm(dh3, od_w_out, "nt", "od_dy")
    g_od_w_out = _mm(y1, dh3, "tn", "od_gout", out_dtype=GRAD_WIRE_DTYPE)
    tok = put_g("od_out", g_od_w_out)
    dysc, dxr_p, dk2_p, dxv_p, dgate, g_lnx_g, g_lnx_b, g_rk = _rwkv_post_bwd(
        dy1, ysc, xr, k2, xv, gate, W["od_lnx_g"], W["od_lnx_b"] + tok, rk, bm)
    dr, dw, dk, dv, da, db = _wkv_bwd(sprev, sab, vbb, decay, k2, a_s, b_s, xr, dysc)
    prep_grads = _rwkv_prep_bwd(pr, W["od_mu"], prep_params, bm,
                                [[dw], [dk, dk2_p], [da], [db], [dgate], [dr, dxr_p], [dv, dxv_p]])
    dxs, g_mu = prep_grads[0], prep_grads[1]
    dpr = _shift_bwd(dxs, W["od_mu"])
    dop = jnp.concatenate([jnp.zeros((ATT_PAD, D_ATT), f32), dy1[:, :D_ATT]], axis=0).astype(bf16)
    dqp, dkp, dvp, dsk = _attn_bwd(qp, kp, vp, W["od_sinks"], dop)
    dpatt = _rope_bwd(dqp, dkp, dvp, c64, s64)
    g_od_w_in_t = jnp.concatenate([_mm(dpatt, hn1, "tn", "od_gin_att", out_dtype=GRAD_WIRE_DTYPE),
                                   _mm(dpr, hn1, "tn", "od_gin_rwkv", out_dtype=GRAD_WIRE_DTYPE)], axis=0)
    tok = put_g("od_in", g_od_w_in_t)
    dhn1 = _mm(dpr, w_rwkv, "nn", "od_dhn_rwkv", res=_mm(dpatt, w_att, "nn", "od_dhn_att"))
    dh2, g_norm_mix1 = _rms_bwd(dhn1, h2, W["norm_mix"][1] + tok, dh3, "mix1_norm_bwd")

    dh1, gf0 = _ffn_bwd(dh2, h1, ffn0, W["norm_ffn"][0], ff0_up_t, W["ff_conv"][0], W["ff_conv_b"][0], ff0_down, 0, put_g)
    early = dict(
        norm_ffn=jnp.concatenate([gf0["norm"], gf1["norm"]], axis=0), norm_final=g_norm_final.reshape(D_MODEL),
        od_sinks=dsk[:, :N_Q_HEADS], od_mu=g_mu, od_lnx_g=g_lnx_g, od_lnx_b=g_lnx_b, od_r_k=g_rk.reshape(N_Q_HEADS, HEAD_DIM),
        ff_conv=jnp.stack([gf0["conv"], gf1["conv"]]), ff_conv_b=jnp.concatenate([gf0["conv_b"], gf1["conv_b"]], axis=0),
        **dict(zip(_PREP_PARAMS, prep_grads[2:])))
    dy0 = _mm(dh1, ev_w_out, "nt", "ev_dy")
    g_ev_w_out = _mm(y0, dh1, "tn", "ev_gout", out_dtype=GRAD_WIRE_DTYPE)
    tok = put_g("ev_out", g_ev_w_out) + put_small(early)
    duc, g_ln_g, g_ln_b = _ev_a_norm_bwd(dy0, uc, W["ev_ln_a_g"], W["ev_ln_a_b"] + tok)
    dav, dag, g_conv_a = _ev_a_conv_bwd(duc, p0, W["ev_conv_a"])
    dgb, dgc, dxi, g_conv_b = _ev_b_bwd(dy0, p0, W["ev_conv_b"])
    dp0 = jnp.concatenate([dav, dag, dgb, dgc, dxi], axis=1)
    g_ev_w_in_t = _mm(dp0, hn0, "tn", "ev_gin", out_dtype=GRAD_WIRE_DTYPE)
    tok = put_g("ev_in", g_ev_w_in_t)
    dhn0 = _mm(dp0, ev_w_in_t, "nn", "ev_dhn")
    dh0, g_norm_mix0 = _rms_bwd(dhn0, h0, W["norm_mix"][0] + tok, dh1, "mix0_norm_bwd")

    late = dict(meta_tokens=dh0[:N_META], norm_mix=jnp.concatenate([g_norm_mix0, g_norm_mix1], axis=0),
                ev_conv_a=g_conv_a, ev_ln_a_g=g_ln_g, ev_ln_a_b=g_ln_b, ev_conv_b=g_conv_b)
    return loss, dh0[N_META:], late


HBM = pl.BlockSpec(memory_space=pl.ANY)


def _mesh_pos():
    return lax.axis_index("x"), lax.axis_index("y"), lax.axis_index("c")


def _dev(px, py, pc):
    return 4 * px + 2 * py + pc


def _all_gather(xs, name):
    n = len(xs)

    def body(*refs):
        x_refs, o_refs = refs[:n], refs[n:2 * n]
        send_sems, recv_sems, local_sems = refs[2 * n:]
        x, y, c = _mesh_pos()
        me, sibling = (x, y, c), (x, y, 1 - c)
        chips = [(1 - x, y), (x, 1 - y), (1 - x, 1 - y)]

        def copy(i, k, block, to, from_input=False):
            dst = o_refs[i].at[_dev(*block)]
            return pltpu.make_async_remote_copy(src_ref=x_refs[i] if from_input else dst, dst_ref=dst,
                                                send_sem=send_sems.at[i, k], recv_sem=recv_sems.at[i, k],
                                                device_id=to, device_id_type=MESH)

        mine = [pltpu.make_async_copy(x_refs[i], o_refs[i].at[_dev(*me)], local_sems.at[i]) for i in range(n)]
        for cp in mine:
            cp.start()
        first = []
        for i in range(n):
            first.append(copy(i, 0, me, sibling, True))
            first += [copy(i, 1 + j, me, (*chip, c), True) for j, chip in enumerate(chips)]
        for cp in first:
            cp.start()
        passed = []
        for j, chip in enumerate(chips):
            for i in range(n):
                copy(i, 1 + j, (*chip, c), me).wait_recv()
                fwd = copy(i, 4 + j, (*chip, c), sibling)
                fwd.start()
                passed.append(fwd)
        for i in range(n):
            copy(i, 0, sibling, me).wait_recv()
            for j, chip in enumerate(chips):
                copy(i, 4 + j, (*chip, 1 - c), me).wait_recv()
        for cp in first + passed:
            cp.wait_send()
        for cp in mine:
            cp.wait()

    return _pc(body, name=name, in_specs=[HBM] * n, out_specs=tuple([HBM] * n),
               out_shape=tuple(S((N_DEV,) + x.shape, x.dtype) for x in xs),
               scratch_shapes=[pltpu.SemaphoreType.DMA((n, 7)), pltpu.SemaphoreType.DMA((n, 7)),
                               pltpu.SemaphoreType.DMA((n,))])(*xs)


HBM_SPEC = pl.BlockSpec(memory_space=pltpu.HBM)
SEM_SPEC = pl.BlockSpec(memory_space=pltpu.SEMAPHORE)
DATAFLOW = pltpu.SideEffectType.DATAFLOW_SIDE_EFFECTING
_PEER_FLIPS = ((1, 0, 0), (0, 1, 0), (1, 1, 0), (1, 0, 1), (0, 1, 1), (1, 1, 1), (0, 0, 1))
N_PEERS = len(_PEER_FLIPS)


def _peers(x, y, c):
    return [((1 - x) if fx else x, (1 - y) if fy else y, (1 - c) if fc else c) for fx, fy, fc in _PEER_FLIPS]


def _xchg_start(srcs, lands, scatter, name):
    n = len(srcs)

    def body(*refs):
        src_refs, land_refs = refs[:n], refs[n:2 * n]
        send_sems, recv_sems, token = refs[2 * n], refs[2 * n + 1], refs[-1]
        x, y, c = _mesh_pos()
        me = _dev(x, y, c)
        for i in range(n):
            for k, peer in enumerate(_peers(x, y, c)):
                pltpu.make_async_remote_copy(src_ref=src_refs[i].at[_dev(*peer)] if scatter else src_refs[i],
                                             dst_ref=land_refs[i].at[me], send_sem=send_sems.at[i * N_PEERS + k],
                                             recv_sem=recv_sems.at[i * N_PEERS + k], device_id=peer, device_id_type=MESH).start()
        token[...] = jnp.zeros_like(token)

    arrs = list(srcs) + list(lands)
    outs = _pc(body, name=name,
               out_shape=(pltpu.SemaphoreType.DMA((n * N_PEERS,)), pltpu.SemaphoreType.DMA((n * N_PEERS,)),
                          *[pltpu.HBM(a.shape, a.dtype) for a in arrs], S((SUBLANES, LANES), f32)),
               in_specs=[HBM_SPEC] * (2 * n),
               out_specs=(SEM_SPEC, SEM_SPEC, *[HBM_SPEC] * (2 * n), pl.BlockSpec(memory_space=pltpu.VMEM)),
               input_output_aliases={i: 2 + i for i in range(2 * n)},
               compiler_params=pltpu.CompilerParams(has_side_effects=DATAFLOW))(
        *[pltpu.with_memory_space_constraint(a, pltpu.HBM) for a in arrs])
    return (outs[0], outs[1], list(outs[2:2 + n]), list(outs[2 + n:2 + 2 * n]), scatter), outs[-1]


def _xchg_wait(handle, after, name):
    send_sems, recv_sems, srcs, lands, scatter = handle
    n = len(srcs)

    def body(*refs):
        src_refs, land_refs = refs[:n], refs[n:2 * n]
        send, recv = refs[2 * n], refs[2 * n + 1]
        x, y, c = _mesh_pos()
        for i in range(n):
            for k in range(N_PEERS):
                cp = pltpu.make_async_remote_copy(src_ref=src_refs[i].at[0] if scatter else src_refs[i],
                                                  dst_ref=land_refs[i].at[0], send_sem=send.at[i * N_PEERS + k],
                                                  recv_sem=recv.at[i * N_PEERS + k],
                                                  device_id=(x, y, c), device_id_type=MESH)
                cp.wait_send()
                cp.wait_recv()

    arrs = srcs + lands
    outs = _pc(body, name=name, out_shape=tuple(pltpu.HBM(a.shape, a.dtype) for a in arrs),
               in_specs=[HBM_SPEC] * (2 * n) + [SEM_SPEC, SEM_SPEC, pl.BlockSpec(memory_space=pl.ANY)],
               out_specs=tuple([HBM_SPEC] * (2 * n)), input_output_aliases={i: i for i in range(2 * n)},
               compiler_params=pltpu.CompilerParams(has_side_effects=DATAFLOW))(*arrs, send_sems, recv_sems, after)
    return list(outs[:n]), list(outs[n:])


def _rs_sum(g, land, me_vec, name):
    _, r, cols = g.shape
    tr = _divisor_block(r, 16, min(r, 352))

    def body(me_ref, g_ref, *rest):
        o_ref = rest[-1]
        acc = g_ref[0].astype(f32)
        for l_ref in rest[:-1]:
            acc = acc + l_ref[0].astype(f32)
        o_ref[...] = acc

    blk = lambda f: pl.BlockSpec((1, tr, cols), f)
    grid_spec = pltpu.PrefetchScalarGridSpec(
        num_scalar_prefetch=1, grid=(r // tr,),
        in_specs=[blk(lambda i, me_ref: (me_ref[0], i, 0))]
        + [blk(lambda i, me_ref, k=k: ((me_ref[0] + k) % N_DEV, i, 0)) for k in range(1, N_DEV)],
        out_specs=pl.BlockSpec((tr, cols), lambda i, me_ref: (i, 0)))
    return _pc(body, name=name, grid_spec=grid_spec, out_shape=S((r, cols), f32),
               compiler_params=_cparams(("arbitrary",)))(me_vec, g, *([land] * (N_DEV - 1)))


def _sum_devices(a, name):
    def body(a_ref, o_ref):
        acc = a_ref[0]
        for d in range(1, N_DEV):
            acc = acc + a_ref[d]
        o_ref[...] = acc

    return _pc(body, name=name, grid=(1,), in_specs=[_full(a.shape)], out_specs=_full(a.shape[1:]),
               out_shape=S(a.shape[1:], a.dtype), compiler_params=_cparams(("arbitrary",)))(a)


def _adamw(w, m, v, g, name):
    shape = w.shape
    w2, m2, v2, g2 = (a.reshape(-1, shape[-1]) for a in (w, m, v, g))
    rows, cols = w2.shape
    tr = rows if rows % SUBLANES else _divisor_block(rows, SUBLANES, max(SUBLANES, min(rows, ADAMW_BLOCK_ELEMS // cols)))
    c1, c2 = 1.0 - ADAM_B1 ** ADAM_STEP, 1.0 - ADAM_B2 ** ADAM_STEP

    def body(w_ref, m_ref, v_ref, g_ref, d_ref, nm_ref, nv_ref):
        gv = g_ref[...]
        nm = ADAM_B1 * m_ref[...] + (1.0 - ADAM_B1) * gv
        nv = ADAM_B2 * v_ref[...] + (1.0 - ADAM_B2) * (gv * gv)
        d_ref[...] = -ADAM_LR * ((nm / c1) / (jnp.sqrt(nv / c2) + ADAM_EPS) + ADAM_WD * w_ref[...])
        nm_ref[...] = nm
        nv_ref[...] = nv

    blk = pl.BlockSpec((tr, cols), lambda i: (i, 0))
    outs = _pc(body, name=name, grid=(rows // tr,), in_specs=[blk] * 4, out_specs=(blk,) * 3,
               out_shape=(S((rows, cols), f32),) * 3, compiler_params=_cparams(("arbitrary",)))(w2, m2, v2, g2)
    return tuple(o.reshape(shape) for o in outs)


_WEIGHTS = ("meta_tokens", "norm_mix", "norm_ffn", "norm_final", "ev_w_in", "ev_conv_a", "ev_ln_a_g", "ev_ln_a_b",
            "ev_conv_b", "ev_w_out", "od_w_in", "od_sinks", "od_mu", "od_w0", "od_w2", "od_a0", "od_a2", "od_g2",
            "od_k_k", "od_k_a", "od_r_k", "od_lnx_g", "od_lnx_b", "od_w_out", "ff_w_up", "ff_conv", "ff_conv_b", "ff_w_down")
_SMALL_SHARDED = (("meta_tokens", 1), ("ev_conv_a", 2), ("ev_conv_b", 2), ("od_mu", 1), ("od_w0", 1), ("od_w2", 2),
                  ("od_a0", 1), ("od_a2", 2), ("od_g2", 2), ("od_k_k", 1), ("od_k_a", 1), ("od_lnx_g", 1),
                  ("od_lnx_b", 1), ("ff_conv", 2))
_SMALL_REPLICATED = ("norm_mix", "norm_ffn", "norm_final", "ev_ln_a_g", "ev_ln_a_b", "od_sinks", "od_r_k", "ff_conv_b")
SLAB_UNIT = SUBLANES * LANES


def _pack(arrs):
    flat = jnp.concatenate([a.reshape(-1).astype(f32) for a in arrs])
    pad = (-flat.shape[0]) % SLAB_UNIT
    return jnp.pad(flat, (0, pad)).reshape(-1, LANES)


def _unpack(flat, shapes):
    out, off = [], 0
    for shp in shapes:
        size = 1
        for s in shp:
            size *= s
        out.append(flat[..., off:off + size].reshape(flat.shape[:-1] + tuple(shp)))
        off += size
    return out


def _full_shape(shape, axis):
    return tuple(N_DEV * s if i == axis else s for i, s in enumerate(shape))


def kernel(x, meta_tokens, norm_mix, norm_ffn, norm_final, ev_w_in, ev_conv_a, ev_ln_a_g, ev_ln_a_b, ev_conv_b, ev_w_out, od_w_in, od_sinks, od_mu, od_w0, od_w2, od_a0, od_a2, od_g2, od_k_k, od_k_a, od_r_k, od_lnx_g, od_lnx_b, od_w_out, ff_w_up, ff_conv, ff_conv_b, ff_w_down, loss_target, m_meta_tokens, m_norm_mix, m_norm_ffn, m_norm_final, m_ev_w_in, m_ev_conv_a, m_ev_ln_a_g, m_ev_ln_a_b, m_ev_conv_b, m_ev_w_out, m_od_w_in, m_od_sinks, m_od_mu, m_od_w0, m_od_w2, m_od_a0, m_od_a2, m_od_g2, m_od_k_k, m_od_k_a, m_od_r_k, m_od_lnx_g, m_od_lnx_b, m_od_w_out, m_ff_w_up, m_ff_conv, m_ff_conv_b, m_ff_w_down, v_meta_tokens, v_norm_mix, v_norm_ffn, v_norm_final, v_ev_w_in, v_ev_conv_a, v_ev_ln_a_g, v_ev_ln_a_b, v_ev_conv_b, v_ev_w_out, v_od_w_in, v_od_sinks, v_od_mu, v_od_w0, v_od_w2, v_od_a0, v_od_a2, v_od_g2, v_od_k_k, v_od_k_a, v_od_r_k, v_od_lnx_g, v_od_lnx_b, v_od_w_out, v_ff_w_up, v_ff_conv, v_ff_conv_b, v_ff_w_down):
    A = dict(locals())
    px, py, pc = _mesh_pos()
    me = _dev(px, py, pc)
    me_vec = jnp.reshape(me, (1,)).astype(jnp.int32)
    rows = lambda a: a.reshape(N_DEV * a.shape[1], a.shape[2])
    blocks = lambda a: a.reshape(N_DEV, a.shape[0] // N_DEV, a.shape[1])

    shards = dict(ev_in=ev_w_in[0].T, ev_out=ev_w_out[0], ff0_up=ff_w_up[0].T, ff0_down=ff_w_down[0], od_in=od_w_in[0].T,
                  od_out=od_w_out[0], ff1_up=ff_w_up[1].T, ff1_down=ff_w_down[1])
    shards = {n: b.astype(bf16) for n, b in shards.items()}
    small_shapes = [A[n].shape for n, _ in _SMALL_SHARDED]
    gathered = _all_gather([shards["ev_in"], shards["ev_out"], _pack([A[n] for n, _ in _SMALL_SHARDED])], "gather_first")
    gathered, shards = lax.optimization_barrier((gathered, shards))
    fetch, tok0 = {}, jnp.zeros((), f32)
    for n in ("ff0_up", "ff0_down", "od_in", "od_out", "ff1_up", "ff1_down"):
        shard, tok0 = lax.optimization_barrier((shards[n], tok0))
        land = lax.dynamic_update_slice(lax.empty((N_DEV,) + shard.shape, bf16), shard[None], (me, 0, 0))
        fetch[n], token = _xchg_start([shard], [land], False, f"gather_{n}_start")
        tok0 = tok0 + token[0, 0]

    def get_w(n, after):
        if n in ("ev_in", "ev_out"):
            return rows(gathered[("ev_in", "ev_out").index(n)])
        return rows(_xchg_wait(fetch[n], after, f"gather_{n}_wait")[1][0])

    W = {}
    for (n, ax), seg in zip(_SMALL_SHARDED, _unpack(gathered[-1].reshape(N_DEV, -1), small_shapes)):
        W[n] = jnp.moveaxis(seg, 0, ax).reshape(_full_shape(A[n].shape, ax))
    for n in ("ev_conv_a", "ev_conv_b", "od_w2", "od_a2", "od_g2"):
        W[n] = W[n][0]
    for n in _SMALL_REPLICATED:
        W[n] = A[n]
    W["od_r_k"] = od_r_k[0]

    small_shape = {n: _full_shape(A[n].shape, ax) for n, ax in _SMALL_SHARDED}
    small_shape.update({n: A[n].shape for n in _SMALL_REPLICATED})
    sent, small_sent, small_names = {}, {}, {}

    def put_g(n, g):
        g8 = blocks(g)
        sent[n], token = _xchg_start([g8], [lax.empty(g8.shape, g8.dtype)], True, f"reduce_{n}_start")
        return token[0, 0]

    def put_small(gs, stage="early"):
        small_names[stage] = sorted(gs)
        slab = _pack([gs[n] for n in small_names[stage]])
        land = lax.dynamic_update_slice(lax.empty((N_DEV,) + slab.shape, f32), slab[None], (me, 0, 0))
        small_sent[stage], small_tok[stage] = _xchg_start([slab], [land], False, f"gather_{stage}_small_grads_start")
        return small_tok[stage][0, 0]

    small_tok = {}
    loss_tile, grad_x, late = _local_step(x[0], loss_target[0], W, get_w, put_g, put_small, tok0)
    put_small(late, "late")
    late_tok = small_tok["late"]

    gsh, prev = {}, late_tok
    for n in ("ff1_down", "ff1_up", "od_out", "od_in", "ff0_down", "ff0_up", "ev_out", "ev_in"):
        srcs, lands = _xchg_wait(sent[n], prev, f"reduce_{n}_wait")
        gsh[n] = prev = _rs_sum(srcs[0], lands[0], me_vec, f"reduce_{n}_sum")
    grads = dict(ev_w_in=gsh["ev_in"].T[None], ev_w_out=gsh["ev_out"][None], od_w_in=gsh["od_in"].T[None],
                 od_w_out=gsh["od_out"][None], ff_w_up=jnp.stack([gsh["ff0_up"].T, gsh["ff1_up"].T]),
                 ff_w_down=jnp.stack([gsh["ff0_down"], gsh["ff1_down"]]))

    delta, new_m, new_v = {}, {}, {}
    for n in ("ff_w_up", "ff_w_down", "od_w_in", "od_w_out", "ev_w_in", "ev_w_out"):
        delta[n], new_m[n], new_v[n] = _adamw(A[n], A["m_" + n], A["v_" + n], grads[n], "adamw_" + n)
    for stage in ("early", "late"):
        gsm = _xchg_wait(small_sent[stage], delta["ev_w_in"], f"gather_{stage}_small_grads_wait")[1][0]
        summed = _sum_devices(gsm, f"sum_{stage}_small_grads").reshape(-1)
        for n, full in zip(small_names[stage], _unpack(summed, [small_shape[n] for n in small_names[stage]])):
            grads[n] = full
    for n, ax in _SMALL_SHARDED:
        size = A[n].shape[ax]
        grads[n] = lax.dynamic_slice_in_dim(grads[n], me * size, size, axis=ax)
    for n in small_shape:
        delta[n], new_m[n], new_v[n] = _adamw(A[n], A["m_" + n], A["v_" + n], grads[n], "adamw_" + n)

    loss = lax.psum(loss_tile[0, 0], ("x", "y", "c"))
    return (loss, grad_x[None], *[grads[n] for n in _WEIGHTS], *[delta[n] for n in _WEIGHTS],
            *[new_m[n] for n in _WEIGHTS], *[new_v[n] for n in _WEIGHTS])
```

```python
import jax
import jax.numpy as jnp
from jax import lax
from jax.experimental import pallas as pl
from jax.experimental.pallas import tpu as pltpu

f32, bf16 = jnp.float32, jnp.bfloat16

D_MODEL = 1024
N_META = 16
RMS_EPS = 1e-6
LN_EPS = 1e-5
D_A = 512
CONV_A_WIDTH = 31
CONV_B_WIDTH = 3
HEAD_DIM = 64
N_Q_HEADS = 8
N_KV_HEADS = 2
GQA_GROUP = 4
D_ATT = 512
D_KV = 128
BLOCK = 128
ROPE_THETA = 10000.0
D_R = 512
LORA_W, LORA_A, LORA_G = 64, 64, 128
RWKV_GN_EPS = 64e-5
ATT_COLS = D_ATT + 2 * D_KV
RWKV_COLS = 3 * D_R + LORA_W + LORA_A + LORA_G
D_FF = 2816
FF_CONV_WIDTH = 3
FF_BLOCK = 256
NEG_INF = -1e30
ATT_PAD = BLOCK - N_META
ATT_SCALE = HEAD_DIM ** -0.5

ADAM_LR, ADAM_B1, ADAM_B2, ADAM_EPS, ADAM_WD, ADAM_STEP = 0.001, 0.9, 0.999, 1e-08, 0.01, 10

N_DEV = 8
LANES = 128
SUBLANES = 8
SCAN_CHUNK = 48
PAIR_ROWS = 4 * HEAD_DIM
V7X_VMEM_LIMIT = 56 * 1024 * 1024
ADAMW_BLOCK_ELEMS = 400 * 1024
GRAD_WIRE_DTYPE = bf16
MESH = pl.DeviceIdType.MESH
S = jax.ShapeDtypeStruct
HIGHEST = lax.Precision.HIGHEST


def _pc(body, **kw):
    return pl.pallas_call(body, **kw)


def _cparams(sem=None):
    return pltpu.CompilerParams(dimension_semantics=sem, vmem_limit_bytes=V7X_VMEM_LIMIT)


def _divisor_block(t, unit, limit):
    best = unit
    for rb in range(unit, limit + 1, unit):
        if t % rb == 0:
            best = rb
    assert t % best == 0, (t, unit)
    return best


def _row_block(t):
    return _divisor_block(t, 16, 704)


def _row_block8(t):
    return _divisor_block(t, 8, 344)


def _col_tile(n, cap):
    return _divisor_block(n, LANES, min(n, cap)) if n % LANES == 0 else n


def _full(shape):
    nd = len(shape)
    return pl.BlockSpec(shape, lambda *_: (0,) * nd)


def _sigmoid(x):
    return jax.nn.sigmoid(x)


_DIMS = {"nn": (((1,), (0,)), ((), ())), "nt": (((1,), (1,)), ((), ())), "tn": (((0,), (0,)), ((), ()))}
MM_MAX_K = 2816
MM_MAX_TM = 704
MM_MAX_TN = 1408


def _mm(a, b, mode, name, out_dtype=f32, res=None, b_row0=0, out_rows=None, out_row0=0, into=None):
    if mode == "nn":
        (m, k), n, k2 = a.shape, b.shape[1], a.shape[1]
        assert b_row0 % k == 0 and b_row0 + k <= b.shape[0], (a.shape, b.shape, b_row0)
    elif mode == "nt":
        (m, k), (n, k2) = a.shape, b.shape
    else:
        (k, m), (k2, n) = a.shape, b.shape
    assert k == k2, (a.shape, b.shape, mode)
    tm = _row_block(m) if m % LANES else _col_tile(m, MM_MAX_TM)
    tn = _col_tile(n, MM_MAX_TN)
    nk = 1 if (mode == "tn" or k <= MM_MAX_K) else k // MM_MAX_K
    tk = k // nk
    assert tk * nk == k
    dims = _DIMS[mode]

    def body(a_ref, b_ref, *rest):
        part = lax.dot_general(a_ref[...].astype(bf16), b_ref[...].astype(bf16), dims, preferred_element_type=f32)
        if nk == 1:
            o_ref = rest[-1]
            if res is not None:
                part = part + rest[0][...]
            o_ref[...] = part.astype(out_dtype)
            return
        o_ref, acc_ref = rest[-2], rest[-1]
        kk = pl.program_id(2)

        @pl.when(kk == 0)
        def _():
            acc_ref[...] = part

        @pl.when(kk > 0)
        def _():
            acc_ref[...] += part

        @pl.when(kk == nk - 1)
        def _():
            acc = acc_ref[...]
            if res is not None:
                acc = acc + rest[0][...]
            o_ref[...] = acc.astype(out_dtype)

    if mode == "tn":
        a_spec = pl.BlockSpec((k, tm), lambda i, j, kk: (0, i))
    else:
        a_spec = pl.BlockSpec((tm, tk), lambda i, j, kk: (i, kk))
    if mode == "nt":
        b_spec = pl.BlockSpec((tn, tk), lambda i, j, kk: (j, kk))
    else:
        b_spec = pl.BlockSpec((tk, tn), lambda i, j, kk: (kk + b_row0 // tk, j))
    assert out_row0 % tm == 0 and res is None or out_row0 == 0
    o_spec = pl.BlockSpec((tm, tn), lambda i, j, kk: (i + out_row0 // tm, j))
    ins, specs, aliases = [a, b], [a_spec, b_spec], {}
    if res is not None:
        ins.append(res)
        specs.append(o_spec)
    if into is not None:
        assert into.shape == (out_rows, n) and into.dtype == out_dtype
        aliases = {len(ins): 0}
        ins.append(into)
        specs.append(pl.BlockSpec(memory_space=pl.ANY))
    scratch = [pltpu.VMEM((tm, tn), f32)] if nk > 1 else []
    return _pc(body, name=name, grid=(m // tm, n // tn, nk), in_specs=specs, out_specs=o_spec,
               out_shape=S((out_rows or m, n), out_dtype), scratch_shapes=scratch, input_output_aliases=aliases,
               compiler_params=_cparams(("arbitrary", "arbitrary", "arbitrary")))(*ins)


def _rms_fwd(x, g, name):
    t, d = x.shape
    rb = _row_block(t)

    def body(x_ref, g_ref, o_ref):
        xv = x_ref[...]
        rstd = lax.rsqrt(jnp.mean(xv * xv, axis=-1, keepdims=True) + RMS_EPS)
        o_ref[...] = (xv * rstd * g_ref[...]).astype(bf16)

    row = pl.BlockSpec((rb, d), lambda i: (i, 0))
    return _pc(body, name=name, grid=(t // rb,), in_specs=[row, _full((1, d))], out_specs=row,
               out_shape=S((t, d), bf16), compiler_params=_cparams(("arbitrary",)))(x, g.reshape(1, d))


def _rms_bwd(dy, x, g, dres, name):
    t, d = x.shape
    rb = _row_block8(t)

    def body(dy_ref, x_ref, g_ref, dres_ref, dx_ref, dg_ref):
        @pl.when(pl.program_id(0) == 0)
        def _():
            dg_ref[...] = jnp.zeros_like(dg_ref)
        xv, dyv = x_ref[...], dy_ref[...]
        rstd = lax.rsqrt(jnp.mean(xv * xv, axis=-1, keepdims=True) + RMS_EPS)
        xn = xv * rstd
        dg_ref[...] += jnp.sum(dyv * xn, axis=0, keepdims=True)
        dxh = dyv * g_ref[...]
        dx_ref[...] = dres_ref[...] + rstd * (dxh - xn * jnp.mean(dxh * xn, axis=-1, keepdims=True))

    row = pl.BlockSpec((rb, d), lambda i: (i, 0))
    return _pc(body, name=name, grid=(t // rb,), in_specs=[row, row, _full((1, d)), row],
               out_specs=(row, _full((1, d))), out_shape=(S((t, d), f32), S((1, d), f32)),
               compiler_params=_cparams(("arbitrary",)))(dy, x, g.reshape(1, d), dres)


def _final_loss(h, g, target_padded):
    t, d = h.shape
    rb = _row_block8(t)

    def body(x_ref, g_ref, t_ref, loss_ref, dx_ref, dg_ref):
        i = pl.program_id(0)

        @pl.when(i == 0)
        def _():
            dg_ref[...] = jnp.zeros_like(dg_ref)
            loss_ref[...] = jnp.zeros_like(loss_ref)
        xv = x_ref[...]
        rstd = lax.rsqrt(jnp.mean(xv * xv, axis=-1, keepdims=True) + RMS_EPS)
        xn = xv * rstd
        gv = g_ref[...]
        row = i * rb + lax.broadcasted_iota(jnp.int32, (rb, 1), 0)
        diff = jnp.where(row >= N_META, xn * gv - t_ref[...], 0.0)
        loss_ref[...] += 0.5 * jnp.sum(jnp.mean(diff * diff, axis=-1, keepdims=True))
        dout = diff * (1.0 / d)
        dg_ref[...] += jnp.sum(dout * xn, axis=0, keepdims=True)
        dxh = dout * gv
        dx_ref[...] = rstd * (dxh - xn * jnp.mean(dxh * xn, axis=-1, keepdims=True))

    row = pl.BlockSpec((rb, d), lambda i: (i, 0))
    return _pc(body, name="final_loss", grid=(t // rb,), in_specs=[row, _full((1, d)), row],
               out_specs=(_full((SUBLANES, LANES)), row, _full((1, d))),
               out_shape=(S((SUBLANES, LANES), f32), S((t, d), f32), S((1, d), f32)),
               compiler_params=_cparams(("arbitrary",)))(h, g.reshape(1, d), target_padded)


CONV_LEAD = 32


def _fill_front_padded(pad_ref, x, t):
    pad_ref[0:CONV_LEAD, :] = jnp.zeros((CONV_LEAD, x.shape[1]), f32)
    pad_ref[CONV_LEAD:CONV_LEAD + t, :] = x


def _fill_back_padded(pad_ref, x, t):
    pad_ref[0:t, :] = x
    pad_ref[t:t + CONV_LEAD, :] = jnp.zeros((CONV_LEAD, x.shape[1]), f32)


def _conv_rows(pad_ref, w_ref, kw, r0, nr):
    acc = None
    for j in range(kw):
        lo = CONV_LEAD + r0 - (kw - 1) + j
        term = w_ref[j:j + 1, :] * pad_ref[lo:lo + nr, :]
        acc = term if acc is None else acc + term
    return acc


def _conv_t_rows(padb_ref, w_ref, kw, r0, nr):
    acc = None
    for j in range(kw):
        lo = r0 + (kw - 1) - j
        term = w_ref[j:j + 1, :] * padb_ref[lo:lo + nr, :]
        acc = term if acc is None else acc + term
    return acc


def _conv_dw_rows(dy_blk, pad_ref, kw, r0, nr):
    out = []
    for j in range(kw):
        lo = CONV_LEAD + r0 - (kw - 1) + j
        out.append(jnp.sum(dy_blk * pad_ref[lo:lo + nr, :], axis=0, keepdims=True))
    return out


def _acc_list(a, b):
    return b if a is None else [x + y for x, y in zip(a, b)]


def _ev_a_conv(p, conv_a):
    t = p.shape[0]
    cr = _row_block8(t)
    nb = D_A // LANES

    def body(av_ref, ag_ref, w_ref, o_ref, pad_ref):
        _fill_front_padded(pad_ref, av_ref[...] * _sigmoid(ag_ref[...]), t)
        for r in range(t // cr):
            o_ref[r * cr:(r + 1) * cr, :] = _conv_rows(pad_ref, w_ref, CONV_A_WIDTH, r * cr, cr)

    col = lambda off: pl.BlockSpec((t, LANES), lambda j: (0, j + off))
    return _pc(body, name="ev_a_conv", grid=(nb,),
               in_specs=[col(0), col(nb), pl.BlockSpec((CONV_A_WIDTH, LANES), lambda j: (0, j))],
               out_specs=col(0), out_shape=S((t, D_A), f32),
               scratch_shapes=[pltpu.VMEM((t + CONV_LEAD, LANES), f32)],
               compiler_params=_cparams(("arbitrary",)))(p, p, conv_a)


def _ln_silu(uc, g, b):
    mu = jnp.mean(uc, axis=-1, keepdims=True)
    xc = uc - mu
    var = jnp.mean(xc * xc, axis=-1, keepdims=True)
    y = xc * lax.rsqrt(var + LN_EPS) * g + b
    return y * _sigmoid(y)


def _ev_a_norm(uc, g, b):
    t, d = uc.shape
    rb = _row_block(t)

    def body(u_ref, g_ref, b_ref, o_ref):
        o_ref[...] = _ln_silu(u_ref[...], g_ref[...], b_ref[...]).astype(bf16)

    row = pl.BlockSpec((rb, d), lambda i: (i, 0))
    return _pc(body, name="ev_a_norm", grid=(t // rb,), in_specs=[row, _full((1, d)), _full((1, d))],
               out_specs=row, out_shape=S((t, 2 * d), bf16), compiler_params=_cparams(("arbitrary",)))(uc, g, b)


def _ev_a_norm_bwd(dy, uc, g, b):
    t, d = uc.shape
    rb = _row_block8(t)

    def body(dy_ref, u_ref, g_ref, b_ref, du_ref, dg_ref, db_ref):
        @pl.when(pl.program_id(0) == 0)
        def _():
            dg_ref[...] = jnp.zeros_like(dg_ref)
            db_ref[...] = jnp.zeros_like(db_ref)
        _, vjp = jax.vjp(_ln_silu, u_ref[...], g_ref[...], b_ref[...])
        du, dg, db = vjp(dy_ref[...])
        du_ref[...] = du
        dg_ref[...] += dg
        db_ref[...] += db

    row = pl.BlockSpec((rb, d), lambda i: (i, 0))
    return _pc(body, name="ev_a_norm_bwd", grid=(t // rb,), in_specs=[row, row, _full((1, d)), _full((1, d))],
               out_specs=(row, _full((1, d)), _full((1, d))),
               out_shape=(S((t, d), f32), S((1, d), f32), S((1, d), f32)),
               compiler_params=_cparams(("arbitrary",)))(dy, uc, g, b)


def _ev_a_conv_bwd(duc, p, conv_a):
    t = p.shape[0]
    cr = _row_block8(t)
    nb = D_A // LANES

    def body(dy_ref, av_ref, ag_ref, w_ref, dav_ref, dag_ref, dw_ref, pad_ref, padb_ref):
        _fill_front_padded(pad_ref, av_ref[...] * _sigmoid(ag_ref[...]), t)
        _fill_back_padded(padb_ref, dy_ref[...], t)
        dw = None
        for r in range(t // cr):
            rows = slice(r * cr, (r + 1) * cr)
            du = _conv_t_rows(padb_ref, w_ref, CONV_A_WIDTH, r * cr, cr)
            avr = av_ref[rows, :]
            sgr = _sigmoid(ag_ref[rows, :])
            dav_ref[rows, :] = du * sgr
            dag_ref[rows, :] = du * avr * sgr * (1.0 - sgr)
            dw = _acc_list(dw, _conv_dw_rows(dy_ref[rows, :], pad_ref, CONV_A_WIDTH, r * cr, cr))
        for j in range(CONV_A_WIDTH):
            dw_ref[j:j + 1, :] = dw[j]

    col = lambda off: pl.BlockSpec((t, LANES), lambda j: (0, j + off))
    wsp = pl.BlockSpec((CONV_A_WIDTH, LANES), lambda j: (0, j))
    return _pc(body, name="ev_a_conv_bwd", grid=(nb,), in_specs=[col(0), col(0), col(nb), wsp],
               out_specs=(col(0), col(0), wsp),
               out_shape=(S((t, D_A), f32), S((t, D_A), f32), S((CONV_A_WIDTH, D_A), f32)),
               scratch_shapes=[pltpu.VMEM((t + CONV_LEAD, LANES), f32), pltpu.VMEM((t + CONV_LEAD, LANES), f32)],
               compiler_params=_cparams(("arbitrary",)))(duc, p, p, conv_a)


def _ev_b(p, conv_b, y):
    t = p.shape[0]
    cr = _row_block8(t)
    nb = D_A // LANES

    def body(gb_ref, gc_ref, xi_ref, w_ref, y_ref, o_ref, pad_ref, stage_ref):
        _fill_front_padded(pad_ref, gc_ref[...] * xi_ref[...], t)
        for r in range(t // cr):
            rows = slice(r * cr, (r + 1) * cr)
            stage_ref[rows, :] = gb_ref[rows, :] * _conv_rows(pad_ref, w_ref, CONV_B_WIDTH, r * cr, cr)
        o_ref[...] = stage_ref[...].astype(bf16)

    col = lambda off: pl.BlockSpec((t, LANES), lambda j: (0, j + off))
    return _pc(body, name="ev_b", grid=(nb,),
               in_specs=[col(2 * nb), col(3 * nb), col(4 * nb), pl.BlockSpec((CONV_B_WIDTH, LANES), lambda j: (0, j)), HBM],
               out_specs=col(nb), out_shape=S(y.shape, bf16), input_output_aliases={4: 0},
               scratch_shapes=[pltpu.VMEM((t + CONV_LEAD, LANES), f32), pltpu.VMEM((t, LANES), f32)],
               compiler_params=_cparams(("arbitrary",)))(p, p, p, conv_b, y)


def _ev_b_bwd(dy, p, conv_b):
    t = p.shape[0]
    cr = _row_block8(t)
    nb = D_A // LANES

    def body(dy_ref, gb_ref, gc_ref, xi_ref, w_ref, dgb_ref, dgc_ref, dxi_ref, dw_ref, pad_ref, padb_ref):
        _fill_front_padded(pad_ref, gc_ref[...] * xi_ref[...], t)
        _fill_back_padded(padb_ref, dy_ref[...] * gb_ref[...], t)
        dw = None
        for r in range(t // cr):
            rows = slice(r * cr, (r + 1) * cr)
            dgb_ref[rows, :] = dy_ref[rows, :] * _conv_rows(pad_ref, w_ref, CONV_B_WIDTH, r * cr, cr)
            dcx = _conv_t_rows(padb_ref, w_ref, CONV_B_WIDTH, r * cr, cr)
            dgc_ref[rows, :] = dcx * xi_ref[rows, :]
            dxi_ref[rows, :] = dcx * gc_ref[rows, :]
            dw = _acc_list(dw, _conv_dw_rows(padb_ref[rows, :], pad_ref, CONV_B_WIDTH, r * cr, cr))
        for j in range(CONV_B_WIDTH):
            dw_ref[j:j + 1, :] = dw[j]

    col = lambda off: pl.BlockSpec((t, LANES), lambda j: (0, j + off))
    wsp = pl.BlockSpec((CONV_B_WIDTH, LANES), lambda j: (0, j))
    return _pc(body, name="ev_b_bwd", grid=(nb,), in_specs=[col(nb), col(2 * nb), col(3 * nb), col(4 * nb), wsp],
               out_specs=(col(0), col(0), col(0), wsp),
               out_shape=(S((t, D_A), f32), S((t, D_A), f32), S((t, D_A), f32), S((CONV_B_WIDTH, D_A), f32)),
               scratch_shapes=[pltpu.VMEM((t + CONV_LEAD, LANES), f32), pltpu.VMEM((t + CONV_LEAD, LANES), f32)],
               compiler_params=_cparams(("arbitrary",)))(dy, p, p, p, conv_b)


def _ffn_mid(u, conv_w, conv_b, name):
    t = u.shape[0]
    cr = _row_block8(t)
    nb = D_FF // FF_BLOCK

    def one(gt_ref, vl_ref, w_ref, b_ref, o_ref, pad_ref, stage_ref):
        _fill_front_padded(pad_ref, gt_ref[...], t)
        for r in range(t // cr):
            rows = slice(r * cr, (r + 1) * cr)
            gc = _conv_rows(pad_ref, w_ref, FF_CONV_WIDTH, r * cr, cr) + b_ref[...]
            stage_ref[rows, :] = gc * _sigmoid(gc) * vl_ref[rows, :]
        o_ref[...] = stage_ref[...].astype(bf16)

    def body(*refs):
        for h in range(FF_BLOCK // LANES):
            one(*[r.at[:, pl.ds(h * LANES, LANES)] for r in refs[:5]], *refs[5:])

    col = lambda off: pl.BlockSpec((t, FF_BLOCK), lambda j: (0, j + off))
    return _pc(body, name=name, grid=(nb,),
               in_specs=[col(0), col(nb), pl.BlockSpec((FF_CONV_WIDTH, FF_BLOCK), lambda j: (0, j)),
                         pl.BlockSpec((1, FF_BLOCK), lambda j: (0, j))],
               out_specs=col(0), out_shape=S((t, D_FF), bf16),
               scratch_shapes=[pltpu.VMEM((t + CONV_LEAD, LANES), f32), pltpu.VMEM((t, LANES), f32)],
               compiler_params=_cparams(("arbitrary",)))(u, u, conv_w, conv_b.reshape(1, D_FF))


def _ffn_mid_bwd(dz, u, conv_w, conv_b, name):
    t = u.shape[0]
    cr = _row_block8(t)
    nb = D_FF // FF_BLOCK
    nh = FF_BLOCK // LANES

    def body(*refs):
        for h in range(nh):
            one(*[r.at[:, pl.ds(h * LANES, LANES)] for r in refs[:9]], *refs[9:])

    def one(dz_ref, gt_ref, vl_ref, w_ref, b_ref, du_ref, dv_ref, dw_ref, db_ref, pad_ref, padb_ref, stage_ref):
        _fill_front_padded(pad_ref, gt_ref[...], t)
        dw, db = None, None
        for r in range(t // cr):
            rows = slice(r * cr, (r + 1) * cr)
            lo = CONV_LEAD + r * cr - (FF_CONV_WIDTH - 1)
            taps = [pad_ref[lo + j:lo + j + cr, :] for j in range(FF_CONV_WIDTH)]
            gc = sum(w_ref[j:j + 1, :] * taps[j] for j in range(FF_CONV_WIDTH)) + b_ref[...]
            sg = _sigmoid(gc)
            dzr = dz_ref[rows, :]
            stage_ref[rows, :] = dzr * gc * sg
            dgc = dzr * vl_ref[rows, :] * sg * (1.0 + gc * (1.0 - sg))
            padb_ref[rows, :] = dgc
            dw = _acc_list(dw, [jnp.sum(dgc * tap, axis=0, keepdims=True) for tap in taps])
            pb = jnp.sum(dgc, axis=0, keepdims=True)
            db = pb if db is None else db + pb
        padb_ref[t:t + CONV_LEAD, :] = jnp.zeros((CONV_LEAD, LANES), f32)
        for r in range(t // cr):
            pad_ref[r * cr:(r + 1) * cr, :] = _conv_t_rows(padb_ref, w_ref, FF_CONV_WIDTH, r * cr, cr)
        du_ref[...] = pad_ref[0:t, :].astype(du_ref.dtype)
        dv_ref[...] = stage_ref[...].astype(dv_ref.dtype)
        for j in range(FF_CONV_WIDTH):
            dw_ref[j:j + 1, :] = dw[j]
        db_ref[...] = db

    col = lambda off: pl.BlockSpec((t, FF_BLOCK), lambda j: (0, j + off))
    wsp = pl.BlockSpec((FF_CONV_WIDTH, FF_BLOCK), lambda j: (0, j))
    bsp = pl.BlockSpec((1, FF_BLOCK), lambda j: (0, j))
    return _pc(body, name=name, grid=(nb,), in_specs=[col(0), col(0), col(nb), wsp, bsp],
               out_specs=(col(0), col(0), wsp, bsp),
               out_shape=(S((t, D_FF), bf16), S((t, D_FF), bf16), S((FF_CONV_WIDTH, D_FF), f32), S((1, D_FF), f32)),
               scratch_shapes=[pltpu.VMEM((t + CONV_LEAD, LANES), f32), pltpu.VMEM((t + CONV_LEAD, LANES), f32),
                               pltpu.VMEM((t, LANES), f32)],
               compiler_params=_cparams(("arbitrary",)))(dz, u, u, conv_w, conv_b.reshape(1, D_FF))


def _swap_halves(x):
    w = x.shape[1]
    lane = lax.broadcasted_iota(jnp.int32, x.shape, 1) % HEAD_DIM
    return jnp.where(lane < HEAD_DIM // 2, pltpu.roll(x, w - HEAD_DIM // 2, axis=1), pltpu.roll(x, HEAD_DIM // 2, axis=1))


def _rope_pack(patt, c64, s64):
    t = patt.shape[0]
    tp = t + ATT_PAD

    def body(p_ref, c_ref, s_ref, q_ref, k_ref, v_ref):
        c, s = c_ref[...], s_ref[...]

        def rope(x, nh):
            cc = jnp.concatenate([c] * nh, axis=1)
            ss = jnp.concatenate([s] * nh, axis=1)
            return x * cc + _swap_halves(x) * ss

        for ref, val in ((q_ref, rope(p_ref[:, 0:D_ATT], N_Q_HEADS)),
                         (k_ref, rope(p_ref[:, D_ATT:D_ATT + D_KV], N_KV_HEADS)),
                         (v_ref, p_ref[:, D_ATT + D_KV:ATT_COLS])):
            ref[0:ATT_PAD, :] = jnp.zeros((ATT_PAD, val.shape[1]), bf16)
            ref[ATT_PAD:tp, :] = val.astype(bf16)

    return _pc(body, name="rope_pack", in_specs=[_full((t, ATT_COLS)), _full((t, HEAD_DIM)), _full((t, HEAD_DIM))],
               out_specs=(_full((tp, D_ATT)), _full((tp, D_KV)), _full((tp, D_KV))), grid=(1,),
               out_shape=(S((tp, D_ATT), bf16), S((tp, D_KV), bf16), S((tp, D_KV), bf16)),
               compiler_params=_cparams(("arbitrary",)))(patt, c64, s64)


def _rope_bwd(dqp, dkp, dvp, c64, s64):
    tp = dqp.shape[0]
    t = tp - ATT_PAD

    def body(dq_ref, dk_ref, dv_ref, c_ref, s_ref, o_ref):
        c, s = c_ref[...], s_ref[...]

        def unrope(dy, nh):
            cc = jnp.concatenate([c] * nh, axis=1)
            ss = jnp.concatenate([s] * nh, axis=1)
            return dy * cc + _swap_halves(dy * ss)

        o_ref[:, 0:D_ATT] = unrope(dq_ref[ATT_PAD:tp, :], N_Q_HEADS).astype(bf16)
        o_ref[:, D_ATT:D_ATT + D_KV] = unrope(dk_ref[ATT_PAD:tp, :], N_KV_HEADS).astype(bf16)
        o_ref[:, D_ATT + D_KV:ATT_COLS] = dv_ref[ATT_PAD:tp, :].astype(bf16)

    return _pc(body, name="rope_bwd", grid=(1,),
               in_specs=[_full((tp, D_ATT)), _full((tp, D_KV)), _full((tp, D_KV)), _full((t, HEAD_DIM)), _full((t, HEAD_DIM))],
               out_specs=_full((t, ATT_COLS)), out_shape=S((t, ATT_COLS), bf16),
               compiler_params=_cparams(("arbitrary",)))(dqp, dkp, dvp, c64, s64)


def _attn_masks(n):
    rows = GQA_GROUP * BLOCK
    ri = lax.broadcasted_iota(jnp.int32, (rows, BLOCK), 0) % BLOCK
    ci = lax.broadcasted_iota(jnp.int32, (rows, BLOCK), 1)
    m_cur = (ci <= ri) & (ci >= jnp.where(n >= 1, 0, ATT_PAD))
    m_prev = ci > ri + jnp.where(n >= 2, 0, BLOCK)
    m_meta = ci >= jnp.where(n >= 1, ATT_PAD, BLOCK)
    return m_cur, m_prev, m_meta


def _attn_probs(qg, kc, kp, km, masks, skv):
    def scores(k, m):
        s = lax.dot_general(qg, k, _DIMS["nt"], preferred_element_type=f32) * ATT_SCALE
        return jnp.where(m, s, NEG_INF)
    s_c, s_p, s_m = scores(kc, masks[0]), scores(kp, masks[1]), scores(km, masks[2])
    mx = jnp.maximum(jnp.maximum(jnp.max(s_c, axis=-1, keepdims=True), jnp.max(s_p, axis=-1, keepdims=True)),
                     jnp.maximum(jnp.max(s_m, axis=-1, keepdims=True), skv))
    e_c, e_p, e_m, e_s = jnp.exp(s_c - mx), jnp.exp(s_p - mx), jnp.exp(s_m - mx), jnp.exp(skv - mx)
    den = (jnp.sum(e_c, axis=-1, keepdims=True) + jnp.sum(e_p, axis=-1, keepdims=True)
           + jnp.sum(e_m, axis=-1, keepdims=True) + e_s)
    inv = 1.0 / den
    return e_c * inv, e_p * inv, e_m * inv, e_s * inv


def _sink_rows(sk_ref, g):
    hrow = lax.broadcasted_iota(jnp.int32, (GQA_GROUP * BLOCK, 1), 0) // BLOCK
    skv = jnp.zeros((GQA_GROUP * BLOCK, 1), f32)
    for hh in range(GQA_GROUP):
        skv = jnp.where(hrow == hh, sk_ref[0, GQA_GROUP * g + hh], skv)
    return skv, hrow


def _stack_heads(ref, g):
    return jnp.concatenate([ref[:, (GQA_GROUP * g + hh) * HEAD_DIM:(GQA_GROUP * g + hh + 1) * HEAD_DIM]
                            for hh in range(GQA_GROUP)], axis=0)


def _attn_specs():
    blk = lambda w: pl.BlockSpec((BLOCK, w), lambda n: (n, 0))
    prev = pl.BlockSpec((BLOCK, D_KV), lambda n: (jnp.maximum(n - 1, 0), 0))
    meta = pl.BlockSpec((BLOCK, D_KV), lambda n: (0, 0))
    return blk, prev, meta


def _attn_fwd(qp, kp, vp, sinks):
    tp = qp.shape[0]
    blk, prev, meta = _attn_specs()

    def body(sk_ref, q_ref, kc_ref, kp_ref, km_ref, vc_ref, vp_ref, vm_ref, o_ref):
        masks = _attn_masks(pl.program_id(0))
        for g in range(N_KV_HEADS):
            sl = slice(g * HEAD_DIM, (g + 1) * HEAD_DIM)
            skv, _ = _sink_rows(sk_ref, g)
            p_c, p_p, p_m, _ = _attn_probs(_stack_heads(q_ref, g), kc_ref[:, sl], kp_ref[:, sl], km_ref[:, sl], masks, skv)
            o = (jnp.dot(p_c.astype(bf16), vc_ref[:, sl], preferred_element_type=f32)
                 + jnp.dot(p_p.astype(bf16), vp_ref[:, sl], preferred_element_type=f32)
                 + jnp.dot(p_m.astype(bf16), vm_ref[:, sl], preferred_element_type=f32))
            for hh in range(GQA_GROUP):
                h = GQA_GROUP * g + hh
                o_ref[:, h * HEAD_DIM:(h + 1) * HEAD_DIM] = o[hh * BLOCK:(hh + 1) * BLOCK].astype(bf16)

    return _pc(body, name="attn_fwd", grid=(tp // BLOCK,),
               in_specs=[pl.BlockSpec(memory_space=pltpu.SMEM), blk(D_ATT), blk(D_KV), prev, meta, blk(D_KV), prev, meta],
               out_specs=blk(D_ATT), out_shape=S((tp, D_ATT), bf16),
               compiler_params=_cparams(("arbitrary",)))(sinks, qp, kp, kp, kp, vp, vp, vp)


def _attn_bwd(qp, kp, vp, sinks, dop):
    tp = qp.shape[0]
    blk, prev, meta = _attn_specs()

    def body(sk_ref, q_ref, kc_ref, kp_ref, km_ref, vc_ref, vp_ref, vm_ref, do_ref, dq_ref, dk_ref, dv_ref, dsk_ref):
        n = pl.program_id(0)

        @pl.when(n == 0)
        def _():
            dk_ref[...] = jnp.zeros_like(dk_ref)
            dv_ref[...] = jnp.zeros_like(dv_ref)
            dsk_ref[...] = jnp.zeros_like(dsk_ref)
        masks = _attn_masks(n)
        cur = pl.ds(pl.multiple_of(n * BLOCK, BLOCK), BLOCK)
        prv = pl.ds(pl.multiple_of(jnp.maximum(n - 1, 0) * BLOCK, BLOCK), BLOCK)
        lane = lax.broadcasted_iota(jnp.int32, (1, LANES), 1)
        dsk = jnp.zeros((1, LANES), f32)
        for g in range(N_KV_HEADS):
            sl = slice(g * HEAD_DIM, (g + 1) * HEAD_DIM)
            skv, hrow = _sink_rows(sk_ref, g)
            qg = _stack_heads(q_ref, g)
            dog = _stack_heads(do_ref, g)
            ks = (kc_ref[:, sl], kp_ref[:, sl], km_ref[:, sl])
            vs = (vc_ref[:, sl], vp_ref[:, sl], vm_ref[:, sl])
            probs = _attn_probs(qg, ks[0], ks[1], ks[2], masks, skv)
            dps = [lax.dot_general(dog, v, _DIMS["nt"], preferred_element_type=f32) for v in vs]
            delta = sum(jnp.sum(p * dp, axis=-1, keepdims=True) for p, dp in zip(probs[:3], dps))
            dss = [(p * (dp - delta) * ATT_SCALE).astype(bf16) for p, dp in zip(probs[:3], dps)]
            dq = sum(jnp.dot(ds, k, preferred_element_type=f32) for ds, k in zip(dss, ks))
            for hh in range(GQA_GROUP):
                h = GQA_GROUP * g + hh
                dq_ref[:, h * HEAD_DIM:(h + 1) * HEAD_DIM] = dq[hh * BLOCK:(hh + 1) * BLOCK]
                dsk = dsk + jnp.where(lane == h, -jnp.sum(jnp.where(hrow == hh, probs[3] * delta, 0.0)), 0.0)
            for rows, p, ds in zip((cur, prv, slice(0, BLOCK)), probs[:3], dss):
                dv_ref[rows, sl] += lax.dot_general(p.astype(bf16), dog, _DIMS["tn"], preferred_element_type=f32)
                dk_ref[rows, sl] += lax.dot_general(ds, qg, _DIMS["tn"], preferred_element_type=f32)
        dsk_ref[...] += dsk

    return _pc(body, name="attn_bwd", grid=(tp // BLOCK,),
               in_specs=[pl.BlockSpec(memory_space=pltpu.SMEM), blk(D_ATT), blk(D_KV), prev, meta, blk(D_KV), prev, meta,
                         blk(D_ATT)],
               out_specs=(blk(D_ATT), _full((tp, D_KV)), _full((tp, D_KV)), _full((1, LANES))),
               out_shape=(S((tp, D_ATT), f32), S((tp, D_KV), f32), S((tp, D_KV), f32), S((1, LANES), f32)),
               compiler_params=_cparams(("arbitrary",)))(sinks, qp, kp, kp, kp, vp, vp, vp, dop)


def _seg(x, bm):
    hi = x.astype(bf16)
    lo = (x - hi.astype(f32)).astype(bf16)
    return jnp.dot(jnp.concatenate([hi, lo], axis=1), bm, preferred_element_type=f32)


@jax.custom_vjp
def _seg_linear(x, bm):
    return _seg(x, bm)


_seg_linear.defvjp(lambda x, bm: (_seg(x, bm), bm), lambda bm, ct: (_seg(ct, bm), jnp.zeros_like(bm)))


def _softplus(y):
    return jnp.maximum(y, 0.0) + jnp.log(1.0 + jnp.exp(-jnp.abs(y)))


def _prep_fn(xr, xk, xwd, xad, xgd, w0, w2, a0, a2, g2, k_k, k_a, bm, seg=_seg):
    xw = w0 + jnp.dot(jnp.tanh(xwd), w2, preferred_element_type=f32)
    decay = jnp.exp(-jnp.exp(-_softplus(-xw) - 0.5))
    alpha = _sigmoid(a0 + jnp.dot(xad, a2, preferred_element_type=f32))
    g = jnp.dot(_sigmoid(xgd), g2, preferred_element_type=f32)
    kk = xk * k_k
    kkn = kk / jnp.maximum(jnp.sqrt(seg(kk * kk, bm)), 1e-12)
    k2 = xk * (1.0 + (alpha - 1.0) * k_a)
    return decay, k2, -kkn, kkn * alpha, g


def _split_cols(x):
    o1, o2, o3 = 3 * D_R, 3 * D_R + LORA_W, 3 * D_R + LORA_W + LORA_A
    return x[:, 0:D_R], x[:, D_R:2 * D_R], x[:, 2 * D_R:o1], x[:, o1:o2], x[:, o2:o3], x[:, o3:RWKV_COLS]


def _shifted(sh_ref, x, halo, first, rb):
    sh_ref[0:SUBLANES, :] = jnp.where(first, 0.0, halo)
    sh_ref[SUBLANES:SUBLANES + rb, :] = x
    return sh_ref[SUBLANES - 1:SUBLANES - 1 + rb, :]


_PREP_PARAMS = ("od_w0", "od_w2", "od_a0", "od_a2", "od_g2", "od_k_k", "od_k_a")


def _rwkv_prep(pr, mu, params, bm):
    t = pr.shape[0]
    rb = _row_block8(t)
    hb = rb // SUBLANES

    def body(pr_ref, halo_ref, mu_ref, w0, w2, a0, a2, g2, kk_ref, ka_ref, bm_ref, *outs_sh):
        outs, sh_ref = outs_sh[:-1], outs_sh[-1]
        x = pr_ref[...]
        prev = _shifted(sh_ref, x, halo_ref[...], pl.program_id(0) == 0, rb)
        xr, xk, xv, xwd, xad, xgd = _split_cols(x + (prev - x) * mu_ref[...])
        bmv = bm_ref[...]
        decay, k2, a_s, b_s, g = _prep_fn(xr, xk, xwd, xad, xgd, w0[...], w2[...], a0[...], a2[...], g2[...],
                                          kk_ref[...], ka_ref[...], bmv)
        vals = (xr, xv, decay, k2, a_s, b_s, decay * xr, _seg(b_s * xr, bmv), _seg(k2 * xr, bmv), g)
        for ref, val in zip(outs, vals):
            ref[...] = val

    row = pl.BlockSpec((rb, RWKV_COLS), lambda i: (i, 0))
    halo = pl.BlockSpec((SUBLANES, RWKV_COLS), lambda i: (jnp.maximum(i * hb - 1, 0), 0))
    orow = pl.BlockSpec((rb, D_R), lambda i: (i, 0))
    return _pc(body, name="rwkv_prep", grid=(t // rb,),
               in_specs=[row, halo, _full((1, RWKV_COLS))] + [_full(p.shape) for p in params] + [_full(bm.shape)],
               out_specs=(orow,) * 10, out_shape=(S((t, D_R), f32),) * 10,
               scratch_shapes=[pltpu.VMEM((rb + SUBLANES, RWKV_COLS), f32)],
               compiler_params=_cparams(("arbitrary",)))(pr, pr, mu, *params, bm)


def _rwkv_prep_bwd(pr, mu, params, bm, cts):
    t = pr.shape[0]
    rb = _row_block8(t)
    hb = rb // SUBLANES
    counts = [len(c) for c in cts]
    flat = [a for c in cts for a in c]

    def body(pr_ref, halo_ref, mu_ref, w0, w2, a0, a2, g2, kk_ref, ka_ref, bm_ref, *rest):
        ct_refs, rest = rest[:len(flat)], rest[len(flat):]
        dx_ref, dmu_ref = rest[0], rest[1]
        dpar_refs, sh_ref = rest[2:9], rest[9]

        @pl.when(pl.program_id(0) == 0)
        def _():
            dmu_ref[...] = jnp.zeros_like(dmu_ref)
            for r in dpar_refs:
                r[...] = jnp.zeros_like(r)
        sums, pos = [], 0
        for c in counts:
            sums.append(sum(r[...] for r in ct_refs[pos:pos + c]))
            pos += c
        x = pr_ref[...]
        prev = _shifted(sh_ref, x, halo_ref[...], pl.program_id(0) == 0, rb)
        xr, xk, xv, xwd, xad, xgd = _split_cols(x + (prev - x) * mu_ref[...])
        bmv = bm_ref[...]
        _, vjp = jax.vjp(lambda *a: _prep_fn(*a, bmv, _seg_linear), xr, xk, xwd, xad, xgd, w0[...], w2[...], a0[...], a2[...],
                         g2[...], kk_ref[...], ka_ref[...])
        grads = vjp(tuple(sums[:5]))
        dxr, dxk, dxwd, dxad, dxgd = grads[:5]
        o1, o2, o3 = 3 * D_R, 3 * D_R + LORA_W, 3 * D_R + LORA_W + LORA_A
        dx_ref[:, 0:D_R] = dxr + sums[5]
        dx_ref[:, D_R:2 * D_R] = dxk
        dx_ref[:, 2 * D_R:o1] = sums[6]
        dx_ref[:, o1:o2] = dxwd
        dx_ref[:, o2:o3] = dxad
        dx_ref[:, o3:RWKV_COLS] = dxgd
        dmu_ref[...] += jnp.sum(dx_ref[...] * (prev - x), axis=0, keepdims=True)
        for r, gval in zip(dpar_refs, grads[5:]):
            r[...] += gval

    row = pl.BlockSpec((rb, RWKV_COLS), lambda i: (i, 0))
    halo = pl.BlockSpec((SUBLANES, RWKV_COLS), lambda i: (jnp.maximum(i * hb - 1, 0), 0))
    crow = pl.BlockSpec((rb, D_R), lambda i: (i, 0))
    return _pc(body, name="rwkv_prep_bwd", grid=(t // rb,),
               in_specs=[row, halo, _full((1, RWKV_COLS))] + [_full(p.shape) for p in params] + [_full(bm.shape)]
               + [crow] * len(flat),
               out_specs=(row, _full((1, RWKV_COLS))) + tuple(_full(p.shape) for p in params),
               out_shape=(S((t, RWKV_COLS), f32), S((1, RWKV_COLS), f32)) + tuple(S(p.shape, f32) for p in params),
               scratch_shapes=[pltpu.VMEM((rb + SUBLANES, RWKV_COLS), f32)],
               compiler_params=_cparams(("arbitrary",)))(pr, pr, mu, *params, bm, *flat)


def _shift_bwd(dxs, mu):
    t = dxs.shape[0]
    rb = _row_block(t)
    hb = rb // SUBLANES
    nblk = t // rb

    def body(dx_ref, halo_ref, mu_ref, o_ref, sh_ref):
        dx = dx_ref[...]
        sh_ref[0:rb, :] = dx
        sh_ref[rb:rb + SUBLANES, :] = jnp.where(pl.program_id(0) == nblk - 1, 0.0, halo_ref[...])
        m = mu_ref[...]
        o_ref[...] = (dx * (1.0 - m) + sh_ref[1:1 + rb, :] * m).astype(bf16)

    row = pl.BlockSpec((rb, RWKV_COLS), lambda i: (i, 0))
    halo = pl.BlockSpec((SUBLANES, RWKV_COLS), lambda i: (jnp.minimum((i + 1) * hb, t // SUBLANES - 1), 0))
    return _pc(body, name="rwkv_shift_bwd", grid=(nblk,), in_specs=[row, halo, _full((1, RWKV_COLS))],
               out_specs=row, out_shape=S((t, RWKV_COLS), bf16),
               scratch_shapes=[pltpu.VMEM((rb + SUBLANES, RWKV_COLS), f32)],
               compiler_params=_cparams(("arbitrary",)))(dxs, dxs, mu)


def _post_fn(y, xr, k2, xv, g, lg, lb, rk, bm, seg=_seg):
    inv_n = 1.0 / HEAD_DIM
    yc = y - seg(y, bm) * inv_n
    var = seg(yc * yc, bm) * inv_n
    yn = yc * lax.rsqrt(var + RWKV_GN_EPS) * lg + lb
    return (yn + seg(xr * k2 * rk, bm) * xv) * g


def _rwkv_post(y, xr, k2, xv, g, lg, lb, rk, bm):
    t = y.shape[0]
    rb = _row_block8(t)

    def body(y_ref, xr_ref, k2_ref, xv_ref, g_ref, lg_ref, lb_ref, rk_ref, bm_ref, o_ref):
        o_ref[...] = _post_fn(y_ref[...], xr_ref[...], k2_ref[...], xv_ref[...], g_ref[...], lg_ref[...], lb_ref[...],
                              rk_ref[...], bm_ref[...])

    row = pl.BlockSpec((rb, D_R), lambda i: (i, 0))
    vec = _full((1, D_R))
    return _pc(body, name="rwkv_post", grid=(t // rb,), in_specs=[row] * 5 + [vec] * 3 + [_full(bm.shape)],
               out_specs=row, out_shape=S((t, D_R), f32),
               compiler_params=_cparams(("arbitrary",)))(y, xr, k2, xv, g, lg, lb, rk, bm)


def _rwkv_post_bwd(dy1, y, xr, k2, xv, g, lg, lb, rk, bm):
    t = y.shape[0]
    rb = _row_block8(t)

    def body(dy_ref, y_ref, xr_ref, k2_ref, xv_ref, g_ref, lg_ref, lb_ref, rk_ref, bm_ref, *outs):
        @pl.when(pl.program_id(0) == 0)
        def _():
            for r in outs[5:]:
                r[...] = jnp.zeros_like(r)
        bmv = bm_ref[...]
        _, vjp = jax.vjp(lambda *a: _post_fn(*a, bmv, _seg_linear), y_ref[...], xr_ref[...], k2_ref[...], xv_ref[...], g_ref[...],
                         lg_ref[...], lb_ref[...], rk_ref[...])
        grads = vjp(dy_ref[...])
        for r, gval in zip(outs[:5], grads[:5]):
            r[...] = gval
        for r, gval in zip(outs[5:], grads[5:]):
            r[...] += gval

    row = pl.BlockSpec((rb, D_R), lambda i: (i, 0))
    vec = _full((1, D_R))
    return _pc(body, name="rwkv_post_bwd", grid=(t // rb,),
               in_specs=[pl.BlockSpec((rb, D_R), lambda i: (i, 1))] + [row] * 5 + [vec] * 3 + [_full(bm.shape)],
               out_specs=(row,) * 5 + (vec,) * 3, out_shape=(S((t, D_R), f32),) * 5 + (S((1, D_R), f32),) * 3,
               compiler_params=_cparams(("arbitrary",)))(dy1, y, xr, k2, xv, g, lg, lb, rk, bm)


def _seg2(x, bb):
    hi = x.astype(bf16)
    lo = (x - hi.astype(f32)).astype(bf16)
    return jnp.dot(jnp.concatenate([hi, lo], axis=1), bb, preferred_element_type=f32)


def _row4(rows, j):
    return jnp.concatenate([jnp.broadcast_to(rows[j:j + 1, p * LANES:(p + 1) * LANES], (HEAD_DIM, LANES))
                            for p in range(4)], axis=0)


def _scan_consts():
    lane_group = jnp.arange(LANES) // HEAD_DIM
    b128 = (lane_group[:, None] == lane_group[None, :]).astype(bf16)
    bb = jnp.concatenate([b128, b128], axis=0)
    qsel = (jnp.arange(PAIR_ROWS)[:, None] % HEAD_DIM == jnp.arange(LANES)[None, :] % HEAD_DIM).astype(f32)
    return bb, qsel


def _store_cols(acc_ref, o_ref, tc):
    for p in range(4):
        blk = acc_ref[p * HEAD_DIM:(p + 1) * HEAD_DIM, :].T
        o_ref[:, (2 * p) * HEAD_DIM:(2 * p + 1) * HEAD_DIM] = blk[0:tc]
        o_ref[:, (2 * p + 1) * HEAD_DIM:(2 * p + 2) * HEAD_DIM] = blk[HEAD_DIM:HEAD_DIM + tc]


PAIR_GROUP = 2 * SUBLANES


def _rwkv_pairs(w, a, b, k, wr, bm):
    t = w.shape[0]
    rb = _row_block8(t)

    def body(w_ref, a_ref, b_ref, k_ref, wr_ref, bm_ref, *outs_sh):
        outs, sh_ref = outs_sh[:-1], outs_sh[-1]

        def second(ref):
            sh_ref[0:rb, :] = ref[...]
            sh_ref[rb:rb + SUBLANES, :] = jnp.zeros((SUBLANES, D_R), f32)
            return sh_ref[1:1 + rb, :]

        w1, b1, k1 = w_ref[...], b_ref[...], k_ref[...]
        w2, a2, wr2 = second(w_ref), second(a_ref), second(wr_ref)
        bmv = bm_ref[...]
        vals = (w1 * a2, w1 * wr2, w1 * w2, b1 * w2, k1 * w2, _seg(b1 * a2, bmv), _seg(k1 * a2, bmv),
                _seg(b1 * wr2, bmv), _seg(k1 * wr2, bmv))
        for ref, val in zip(outs, vals):
            ref[...] = val

    row = pl.BlockSpec((rb, D_R), lambda i: (i, 0))
    return _pc(body, name="rwkv_pairs", grid=(t // rb,), in_specs=[row] * 5 + [_full(bm.shape)],
               out_specs=(row,) * 9, out_shape=(S((t, D_R), f32),) * 9,
               scratch_shapes=[pltpu.VMEM((rb + SUBLANES, D_R), f32)],
               compiler_params=_cparams(("arbitrary",)))(w, a, b, k, wr, bm)


def _wkv_fwd(w, k, v, a, b, wr, br, kr, pairs):
    t = w.shape[0]
    tc = SCAN_CHUNK
    bb, qsel = _scan_consts()

    def body(*refs):
        step_refs, pair_refs = refs[0:8], refs[8:17]
        bb_ref, q_ref, y_ref, st_ref, sa_ref, vb_ref, s_scr, yacc = refs[17:]

        @pl.when(pl.program_id(0) == 0)
        def _():
            s_scr[...] = jnp.zeros_like(s_scr)
        bbv, qv = bb_ref[...], q_ref[...]
        lane64 = lax.broadcasted_iota(jnp.int32, (PAIR_ROWS, LANES), 1) % HEAD_DIM

        def halves(x):
            hi = x.astype(bf16)
            return jnp.concatenate([hi, (x - hi.astype(f32)).astype(bf16)], axis=1)

        def group(gi, s):
            base = pl.multiple_of(gi * PAIR_GROUP, PAIR_GROUP)
            w16, k16, v16, a16, b16, wr16, br16, kr16 = step_refs
            a2p, r2p, w12p, b1wp, k1wp, betap, kappap, bwrp, kwrp = pair_refs

            def rows8(ref, j):
                return ref[pl.ds(base + (j // SUBLANES) * SUBLANES, SUBLANES), :]

            def bcast(rows, j, p):
                return jnp.broadcast_to(rows[j % SUBLANES:j % SUBLANES + 1, p * LANES:(p + 1) * LANES], (HEAD_DIM, LANES))

            step = lambda ref, j, p: bcast(rows8(ref, j), j, p)
            qp = qv[0:HEAD_DIM]
            lane = lane64[0:HEAD_DIM]
            for q in range(SUBLANES):
                j1, j2 = 2 * q, 2 * q + 1
                t1 = base + j1
                nxt = []
                for p in range(4):
                    sl = slice(p * HEAD_DIM, (p + 1) * HEAD_DIM)
                    sp = s[sl]
                    lhs = [halves(jnp.concatenate([sp * step(a16, j1, p), sp * step(a2p, j1, p), sp * step(wr16, j1, p),
                                                   sp * step(r2p, j1, p)], axis=0))]
                    for j in (j1, j2):
                        v8 = rows8(v16, j)
                        vh8 = v8.astype(bf16).astype(f32)
                        lhs.append(jnp.concatenate([(qp * bcast(vh8, j, p)).astype(bf16),
                                                    (qp * bcast(v8 - vh8, j, p)).astype(bf16)], axis=1))
                    r = jnp.dot(jnp.concatenate(lhs, axis=0), bbv, preferred_element_type=f32)
                    sa1, p2, z1, z2, vb1, vb2 = (r[n * HEAD_DIM:(n + 1) * HEAD_DIM] for n in range(6))
                    sa2 = p2 + sa1 * step(betap, j1, p) + vb1 * step(kappap, j1, p)
                    y1 = z1 + sa1 * step(br16, j1, p) + vb1 * step(kr16, j1, p)
                    y2 = (z2 + sa1 * step(bwrp, j1, p) + vb1 * step(kwrp, j1, p)) + (sa2 * step(br16, j2, p)
                                                                                      + vb2 * step(kr16, j2, p))
                    yacc[sl, :] = jnp.where(lane == t1, y1, jnp.where(lane == t1 + 1, y2, yacc[sl, :]))
                    st_ref[base // 2 + q, sl, :] = sp
                    sa_ref[t1, sl, :] = sa1
                    sa_ref[t1 + 1, sl, :] = sa2
                    vb_ref[t1, sl, :] = vb1
                    vb_ref[t1 + 1, sl, :] = vb2
                    nxt.append(((sp * step(w12p, j1, p) + sa1 * step(b1wp, j1, p)) + vb1 * step(k1wp, j1, p))
                               + (sa2 * step(b16, j2, p) + vb2 * step(k16, j2, p)))
                s = jnp.concatenate(nxt, axis=0)
            return s

        s_scr[...] = lax.fori_loop(0, tc // PAIR_GROUP, group, s_scr[...])
        _store_cols(yacc, y_ref, tc)

    row = pl.BlockSpec((tc, D_R), lambda c: (c, 0))
    tiles = pl.BlockSpec((tc, PAIR_ROWS, LANES), lambda c: (c, 0, 0))
    return _pc(body, name="wkv_fwd", grid=(t // tc,),
               in_specs=[row] * 17 + [_full(bb.shape), _full(qsel.shape)],
               out_specs=(row, pl.BlockSpec((tc // 2, PAIR_ROWS, LANES), lambda c: (c, 0, 0)), tiles, tiles),
               out_shape=(S((t, D_R), f32), S((t // 2, PAIR_ROWS, LANES), f32)) + (S((t, PAIR_ROWS, LANES), f32),) * 2,
               scratch_shapes=[pltpu.VMEM((PAIR_ROWS, LANES), f32), pltpu.VMEM((PAIR_ROWS, LANES), f32)],
               compiler_params=_cparams(("arbitrary",)))(w, k, v, a, b, wr, br, kr, *pairs, bb, qsel)


def _wkv_bwd(sprev, sab, vbb, w, k, a, b, r, dy):
    t = w.shape[0]
    tc = SCAN_CHUNK
    nc = t // tc
    bb, qsel = _scan_consts()

    def body(st_ref, sa_ref, vb_ref, w_ref, k_ref, a_ref, b_ref, r_ref, dy_ref, bb_ref, q_ref,
             dr_ref, dw_ref, dk_ref, dv_ref, da_ref, db_ref, g_scr, dvacc, rows_scr):
        @pl.when(pl.program_id(0) == 0)
        def _():
            g_scr[...] = jnp.zeros_like(g_scr)
        bbv, qv = bb_ref[...], q_ref[...]
        lane64 = lax.broadcasted_iota(jnp.int32, (PAIR_ROWS, LANES), 1) % HEAD_DIM
        outs = (dr_ref, dw_ref, db_ref, dk_ref, da_ref)

        def colsums(slot, j, x):
            for p in range(4):
                rows_scr[slot, j:j + 1, p * LANES:(p + 1) * LANES] = jnp.sum(x[p * HEAD_DIM:(p + 1) * HEAD_DIM], axis=0,
                                                                           keepdims=True)

        def group(i, g):
            base = pl.multiple_of((tc // SUBLANES - 1 - i) * SUBLANES, SUBLANES)
            w8, k8, a8, b8, r8, dy8 = (ref[pl.ds(base, SUBLANES), :] for ref in (w_ref, k_ref, a_ref, b_ref, r_ref, dy_ref))

            def after_step(j, sp):
                return sp * _row4(w8, j) + sa_ref[base + j] * _row4(b8, j) + vb_ref[base + j] * _row4(k8, j)

            def back_step(j, sp, s_t, g):
                tt = base + j
                u, vb = sa_ref[tt], vb_ref[tt]
                a4, b4, w4, k4 = _row4(a8, j), _row4(b8, j), _row4(w8, j), _row4(k8, j)
                dyb = _seg2(qv * _row4(dy8, j), bbv)
                g = g + dyb * _row4(r8, j)
                rr2 = _seg2(jnp.concatenate([g * b4, g * k4], axis=0), bbv)
                du, dvb = rr2[0:PAIR_ROWS], rr2[PAIR_ROWS:2 * PAIR_ROWS]
                for slot, val in enumerate((s_t * dyb, g * sp, g * u, g * vb, sp * du)):
                    colsums(slot, j, val)
                dvacc[...] = jnp.where(lane64 == tt, dvb, dvacc[...])
                return g * w4 + du * a4

            for q in reversed(range(SUBLANES // 2)):
                s0 = st_ref[base // 2 + q]
                s1 = after_step(2 * q, s0)
                g = back_step(2 * q + 1, s1, after_step(2 * q + 1, s1), g)
                g = back_step(2 * q, s0, s1, g)
            for slot, ref in enumerate(outs):
                ref[pl.ds(base, SUBLANES), :] = rows_scr[slot]
            return g

        g_scr[...] = lax.fori_loop(0, tc // SUBLANES, group, g_scr[...])
        _store_cols(dvacc, dv_ref, tc)

    row = pl.BlockSpec((tc, D_R), lambda c: (nc - 1 - c, 0))
    tiles = pl.BlockSpec((tc, PAIR_ROWS, LANES), lambda c: (nc - 1 - c, 0, 0))
    states = pl.BlockSpec((tc // 2, PAIR_ROWS, LANES), lambda c: (nc - 1 - c, 0, 0))
    return _pc(body, name="wkv_bwd", grid=(nc,),
               in_specs=[states, tiles, tiles] + [row] * 6 + [_full(bb.shape), _full(qsel.shape)],
               out_specs=(row,) * 6, out_shape=(S((t, D_R), f32),) * 6,
               scratch_shapes=[pltpu.VMEM((PAIR_ROWS, LANES), f32), pltpu.VMEM((PAIR_ROWS, LANES), f32),
                               pltpu.VMEM((5, SUBLANES, D_R), f32)],
               compiler_params=_cparams(("arbitrary",)))(sprev, sab, vbb, w, k, a, b, r, dy, bb, qsel)


def _rope_tables(t):
    half = HEAD_DIM // 2
    inv = ROPE_THETA ** (-jnp.arange(half, dtype=f32) / half)
    ang = jnp.arange(t, dtype=f32)[:, None] * inv[None, :]
    cos, sin = jnp.cos(ang), jnp.sin(ang)
    return jnp.concatenate([cos, cos], axis=1), jnp.concatenate([-sin, sin], axis=1)


def _head_matrix():
    grp = jnp.arange(D_R) // HEAD_DIM
    b = (grp[:, None] == grp[None, :]).astype(bf16)
    return jnp.concatenate([b, b], axis=0)


def _ffn_fwd(h, g, get_w, conv_w, conv_b, i):
    hf = _rms_fwd(h, g, f"ffn{i}_norm")
    w_up_t = get_w(f"ff{i}_up", hf)
    u = _mm(hf, w_up_t, "nt", f"ffn{i}_up")
    z = _ffn_mid(u, conv_w, conv_b, f"ffn{i}_mid")
    w_down = get_w(f"ff{i}_down", z)
    return _mm(z, w_down, "nn", f"ffn{i}_down", res=h), (hf, u, z), w_up_t, w_down


def _ffn_bwd(dh, h, saved, g, w_up_t, conv_w, conv_b, w_down, i, put_g):
    hf, u, z = saved
    dz = _mm(dh, w_down, "nt", f"ffn{i}_dz")
    g_down = _mm(z, dh, "tn", f"ffn{i}_gdown", out_dtype=GRAD_WIRE_DTYPE)
    tok = put_g(f"ff{i}_down", g_down)
    dgate, dval, g_conv, g_convb = _ffn_mid_bwd(dz, u, conv_w, conv_b + tok, f"ffn{i}_mid_bwd")
    g_up_t = _mm(dgate, hf, "tn", f"ffn{i}_gup_gate", out_dtype=GRAD_WIRE_DTYPE, out_rows=2 * D_FF)
    g_up_t = _mm(dval, hf, "tn", f"ffn{i}_gup_val", out_dtype=GRAD_WIRE_DTYPE, out_rows=2 * D_FF, out_row0=D_FF, into=g_up_t)
    tok = put_g(f"ff{i}_up", g_up_t)
    dhf = _mm(dval, w_up_t, "nn", f"ffn{i}_dhf_val", b_row0=D_FF, res=_mm(dgate, w_up_t, "nn", f"ffn{i}_dhf_gate"))
    dh_in, g_norm = _rms_bwd(dhf, h, g + tok, dh, f"ffn{i}_norm_bwd")
    return dh_in, dict(conv=g_conv, conv_b=g_convb, norm=g_norm)


def _local_step(x, target, W, get_w, put_g, put_small, tok0):
    t = N_META + x.shape[0]
    c64, s64 = _rope_tables(t)
    bm = _head_matrix()
    h0 = jnp.concatenate([W["meta_tokens"], x], axis=0)

    ev_w_in_t, ev_w_out = get_w("ev_in", None), get_w("ev_out", None)
    hn0 = _rms_fwd(h0, W["norm_mix"][0] + tok0, "mix0_norm")
    p0 = _mm(hn0, ev_w_in_t, "nt", "ev_in")
    uc = _ev_a_conv(p0, W["ev_conv_a"])
    y0 = _ev_b(p0, W["ev_conv_b"], _ev_a_norm(uc, W["ev_ln_a_g"], W["ev_ln_a_b"]))
    h1 = _mm(y0, ev_w_out, "nn", "ev_out", res=h0)
    h2, ffn0, ff0_up_t, ff0_down = _ffn_fwd(h1, W["norm_ffn"][0], get_w, W["ff_conv"][0], W["ff_conv_b"][0], 0)

    hn1 = _rms_fwd(h2, W["norm_mix"][1], "mix1_norm")
    od_w_in_t = get_w("od_in", hn1)
    w_att, w_rwkv = od_w_in_t[:ATT_COLS], od_w_in_t[ATT_COLS:]
    pr = _mm(hn1, w_rwkv, "nt", "od_in_rwkv")
    qp, kp, vp = _rope_pack(_mm(hn1, w_att, "nt", "od_in_att"), c64, s64)
    op = _attn_fwd(qp, kp, vp, W["od_sinks"])
    prep_params = [W[n] for n in _PREP_PARAMS]
    xr, xv, decay, k2, a_s, b_s, wr, br, kr, gate = _rwkv_prep(pr, W["od_mu"], prep_params, bm)
    pairs = _rwkv_pairs(decay, a_s, b_s, k2, wr, bm)
    ysc, sprev, sab, vbb = _wkv_fwd(decay, k2, xv, a_s, b_s, wr, br, kr, pairs)
    rk = W["od_r_k"].reshape(1, D_R)
    yr = _rwkv_post(ysc, xr, k2, xv, gate, W["od_lnx_g"], W["od_lnx_b"], rk, bm)
    y1 = jnp.concatenate([op[ATT_PAD:], yr.astype(bf16)], axis=1)
    od_w_out = get_w("od_out", y1)
    h3 = _mm(y1, od_w_out, "nn", "od_out", res=h2)
    h4, ffn1, ff1_up_t, ff1_down = _ffn_fwd(h3, W["norm_ffn"][1], get_w, W["ff_conv"][1], W["ff_conv_b"][1], 1)

    tgt = jnp.concatenate([jnp.zeros((N_META, D_MODEL), f32), target], axis=0)
    loss, dh4, g_norm_final = _final_loss(h4, W["norm_final"], tgt)

    dh3, gf1 = _ffn_bwd(dh4, h3, ffn1, W["norm_ffn"][1], ff1_up_t, W["ff_conv"][1], W["ff_conv_b"][1], ff1_down, 1, put_g)
    dy1 = _mm(dh3, od_w_out, "nt", "od_dy")
    g_od_w_out = _mm(y1, dh3, "tn", "od_gout", out_dtype=GRAD_WIRE_DTYPE)
    tok = put_g("od_out", g_od_w_out)
    dysc, dxr_p, dk2_p, dxv_p, dgate, g_lnx_g, g_lnx_b, g_rk = _rwkv_post_bwd(
        dy1, ysc, xr, k2, xv, gate, W["od_lnx_g"], W["od_lnx_b"] + tok, rk, bm)
    dr, dw, dk, dv, da, db = _wkv_bwd(sprev, sab, vbb, decay, k2, a_s, b_s, xr, dysc)
    prep_grads = _rwkv_prep_bwd(pr, W["od_mu"], prep_params, bm,
                                [[dw], [dk, dk2_p], [da], [db], [dgate], [dr, dxr_p], [dv, dxv_p]])
    dxs, g_mu = prep_grads[0], prep_grads[1]
    dpr = _shift_bwd(dxs, W["od_mu"])
    dop = jnp.concatenate([jnp.zeros((ATT_PAD, D_ATT), f32), dy1[:, :D_ATT]], axis=0).astype(bf16)
    dqp, dkp, dvp, dsk = _attn_bwd(qp, kp, vp, W["od_sinks"], dop)
    dpatt = _rope_bwd(dqp, dkp, dvp, c64, s64)
    n_in = ATT_COLS + RWKV_COLS
    g_od_w_in_t = _mm(dpatt, hn1, "tn", "od_gin_att", out_dtype=GRAD_WIRE_DTYPE, out_rows=n_in)
    g_od_w_in_t = _mm(dpr, hn1, "tn", "od_gin_rwkv", out_dtype=GRAD_WIRE_DTYPE, out_rows=n_in, out_row0=ATT_COLS, into=g_od_w_in_t)
    tok = put_g("od_in", g_od_w_in_t)
    dhn1 = _mm(dpr, w_rwkv, "nn", "od_dhn_rwkv", res=_mm(dpatt, w_att, "nn", "od_dhn_att"))
    dh2, g_norm_mix1 = _rms_bwd(dhn1, h2, W["norm_mix"][1] + tok, dh3, "mix1_norm_bwd")

    dh1, gf0 = _ffn_bwd(dh2, h1, ffn0, W["norm_ffn"][0], ff0_up_t, W["ff_conv"][0], W["ff_conv_b"][0], ff0_down, 0, put_g)
    early = dict(
        norm_ffn=jnp.concatenate([gf0["norm"], gf1["norm"]], axis=0), norm_final=g_norm_final.reshape(D_MODEL),
        od_sinks=dsk[:, :N_Q_HEADS], od_mu=g_mu, od_lnx_g=g_lnx_g, od_lnx_b=g_lnx_b, od_r_k=g_rk.reshape(N_Q_HEADS, HEAD_DIM),
        ff_conv=jnp.stack([gf0["conv"], gf1["conv"]]), ff_conv_b=jnp.concatenate([gf0["conv_b"], gf1["conv_b"]], axis=0),
        **dict(zip(_PREP_PARAMS, prep_grads[2:])))
    dy0 = _mm(dh1, ev_w_out, "nt", "ev_dy")
    g_ev_w_out = _mm(y0, dh1, "tn", "ev_gout", out_dtype=GRAD_WIRE_DTYPE)
    tok = put_g("ev_out", g_ev_w_out) + put_small(early)
    duc, g_ln_g, g_ln_b = _ev_a_norm_bwd(dy0, uc, W["ev_ln_a_g"], W["ev_ln_a_b"] + tok)
    dav, dag, g_conv_a = _ev_a_conv_bwd(duc, p0, W["ev_conv_a"])
    dgb, dgc, dxi, g_conv_b = _ev_b_bwd(dy0, p0, W["ev_conv_b"])
    dp0 = jnp.concatenate([dav, dag, dgb, dgc, dxi], axis=1)
    g_ev_w_in_t = _mm(dp0, hn0, "tn", "ev_gin", out_dtype=GRAD_WIRE_DTYPE)
    tok = put_g("ev_in", g_ev_w_in_t)
    dhn0 = _mm(dp0, ev_w_in_t, "nn", "ev_dhn")
    dh0, g_norm_mix0 = _rms_bwd(dhn0, h0, W["norm_mix"][0] + tok, dh1, "mix0_norm_bwd")

    late = dict(meta_tokens=dh0[:N_META], norm_mix=jnp.concatenate([g_norm_mix0, g_norm_mix1], axis=0),
                ev_conv_a=g_conv_a, ev_ln_a_g=g_ln_g, ev_ln_a_b=g_ln_b, ev_conv_b=g_conv_b)
    return loss, dh0[N_META:], late


HBM = pl.BlockSpec(memory_space=pl.ANY)


def _mesh_pos():
    return lax.axis_index("x"), lax.axis_index("y"), lax.axis_index("c")


def _dev(px, py, pc):
    return 4 * px + 2 * py + pc


def _all_gather(xs, name):
    n = len(xs)

    def body(*refs):
        x_refs, o_refs = refs[:n], refs[n:2 * n]
        send_sems, recv_sems, local_sems = refs[2 * n:]
        x, y, c = _mesh_pos()
        me, sibling = (x, y, c), (x, y, 1 - c)
        chips = [(1 - x, y), (x, 1 - y), (1 - x, 1 - y)]

        def copy(i, k, block, to, from_input=False):
            dst = o_refs[i].at[_dev(*block)]
            return pltpu.make_async_remote_copy(src_ref=x_refs[i] if from_input else dst, dst_ref=dst,
                                                send_sem=send_sems.at[i, k], recv_sem=recv_sems.at[i, k],
                                                device_id=to, device_id_type=MESH)

        mine = [pltpu.make_async_copy(x_refs[i], o_refs[i].at[_dev(*me)], local_sems.at[i]) for i in range(n)]
        for cp in mine:
            cp.start()
        first = []
        for i in range(n):
            first.append(copy(i, 0, me, sibling, True))
            first += [copy(i, 1 + j, me, (*chip, c), True) for j, chip in enumerate(chips)]
        for cp in first:
            cp.start()
        passed = []
        for j, chip in enumerate(chips):
            for i in range(n):
                copy(i, 1 + j, (*chip, c), me).wait_recv()
                fwd = copy(i, 4 + j, (*chip, c), sibling)
                fwd.start()
                passed.append(fwd)
        for i in range(n):
            copy(i, 0, sibling, me).wait_recv()
            for j, chip in enumerate(chips):
                copy(i, 4 + j, (*chip, 1 - c), me).wait_recv()
        for cp in first + passed:
            cp.wait_send()
        for cp in mine:
            cp.wait()

    return _pc(body, name=name, in_specs=[HBM] * n, out_specs=tuple([HBM] * n),
               out_shape=tuple(S((N_DEV,) + x.shape, x.dtype) for x in xs),
               scratch_shapes=[pltpu.SemaphoreType.DMA((n, 7)), pltpu.SemaphoreType.DMA((n, 7)),
                               pltpu.SemaphoreType.DMA((n,))])(*xs)


HBM_SPEC = pl.BlockSpec(memory_space=pltpu.HBM)
SEM_SPEC = pl.BlockSpec(memory_space=pltpu.SEMAPHORE)
DATAFLOW = pltpu.SideEffectType.DATAFLOW_SIDE_EFFECTING
_PEER_FLIPS = ((1, 0, 0), (0, 1, 0), (1, 1, 0), (1, 0, 1), (0, 1, 1), (1, 1, 1), (0, 0, 1))
N_PEERS = len(_PEER_FLIPS)


def _peers(x, y, c):
    return [((1 - x) if fx else x, (1 - y) if fy else y, (1 - c) if fc else c) for fx, fy, fc in _PEER_FLIPS]


def _xchg_start(srcs, lands, scatter, name):
    n = len(srcs)

    def body(*refs):
        src_refs, land_refs = refs[:n], refs[n:2 * n]
        send_sems, recv_sems, token = refs[2 * n], refs[2 * n + 1], refs[-1]
        x, y, c = _mesh_pos()
        me = _dev(x, y, c)
        for i in range(n):
            for k, peer in enumerate(_peers(x, y, c)):
                pltpu.make_async_remote_copy(src_ref=src_refs[i].at[_dev(*peer)] if scatter else src_refs[i],
                                             dst_ref=land_refs[i].at[me], send_sem=send_sems.at[i * N_PEERS + k],
                                             recv_sem=recv_sems.at[i * N_PEERS + k], device_id=peer, device_id_type=MESH).start()
        token[...] = jnp.zeros_like(token)

    arrs = list(srcs) + list(lands)
    outs = _pc(body, name=name,
               out_shape=(pltpu.SemaphoreType.DMA((n * N_PEERS,)), pltpu.SemaphoreType.DMA((n * N_PEERS,)),
                          *[pltpu.HBM(a.shape, a.dtype) for a in arrs], S((SUBLANES, LANES), f32)),
               in_specs=[HBM_SPEC] * (2 * n),
               out_specs=(SEM_SPEC, SEM_SPEC, *[HBM_SPEC] * (2 * n), pl.BlockSpec(memory_space=pltpu.VMEM)),
               input_output_aliases={i: 2 + i for i in range(2 * n)},
               compiler_params=pltpu.CompilerParams(has_side_effects=DATAFLOW))(
        *[pltpu.with_memory_space_constraint(a, pltpu.HBM) for a in arrs])
    return (outs[0], outs[1], list(outs[2:2 + n]), list(outs[2 + n:2 + 2 * n]), scatter), outs[-1]


def _xchg_wait(handle, after, name):
    send_sems, recv_sems, srcs, lands, scatter = handle
    n = len(srcs)

    def body(*refs):
        src_refs, land_refs = refs[:n], refs[n:2 * n]
        send, recv = refs[2 * n], refs[2 * n + 1]
        x, y, c = _mesh_pos()
        for i in range(n):
            for k in range(N_PEERS):
                cp = pltpu.make_async_remote_copy(src_ref=src_refs[i].at[0] if scatter else src_refs[i],
                                                  dst_ref=land_refs[i].at[0], send_sem=send.at[i * N_PEERS + k],
                                                  recv_sem=recv.at[i * N_PEERS + k],
                                                  device_id=(x, y, c), device_id_type=MESH)
                cp.wait_send()
                cp.wait_recv()

    arrs = srcs + lands
    outs = _pc(body, name=name, out_shape=tuple(pltpu.HBM(a.shape, a.dtype) for a in arrs),
               in_specs=[HBM_SPEC] * (2 * n) + [SEM_SPEC, SEM_SPEC, pl.BlockSpec(memory_space=pl.ANY)],
               out_specs=tuple([HBM_SPEC] * (2 * n)), input_output_aliases={i: i for i in range(2 * n)},
               compiler_params=pltpu.CompilerParams(has_side_effects=DATAFLOW))(*arrs, send_sems, recv_sems, after)
    return list(outs[:n]), list(outs[n:])


def _rs_sum(g, land, me_vec, name):
    _, r, cols = g.shape
    tr = _divisor_block(r, 16, min(r, 352))

    def body(me_ref, g_ref, *rest):
        o_ref = rest[-1]
        acc = g_ref[0].astype(f32)
        for l_ref in rest[:-1]:
            acc = acc + l_ref[0].astype(f32)
        o_ref[...] = acc

    blk = lambda f: pl.BlockSpec((1, tr, cols), f)
    grid_spec = pltpu.PrefetchScalarGridSpec(
        num_scalar_prefetch=1, grid=(r // tr,),
        in_specs=[blk(lambda i, me_ref: (me_ref[0], i, 0))]
        + [blk(lambda i, me_ref, k=k: ((me_ref[0] + k) % N_DEV, i, 0)) for k in range(1, N_DEV)],
        out_specs=pl.BlockSpec((tr, cols), lambda i, me_ref: (i, 0)))
    return _pc(body, name=name, grid_spec=grid_spec, out_shape=S((r, cols), f32),
               compiler_params=_cparams(("arbitrary",)))(me_vec, g, *([land] * (N_DEV - 1)))


def _sum_devices(a, name):
    def body(a_ref, o_ref):
        acc = a_ref[0]
        for d in range(1, N_DEV):
            acc = acc + a_ref[d]
        o_ref[...] = acc

    return _pc(body, name=name, grid=(1,), in_specs=[_full(a.shape)], out_specs=_full(a.shape[1:]),
               out_shape=S(a.shape[1:], a.dtype), compiler_params=_cparams(("arbitrary",)))(a)


def _adamw(w, m, v, g, name):
    shape = w.shape
    w2, m2, v2, g2 = (a.reshape(-1, shape[-1]) for a in (w, m, v, g))
    rows, cols = w2.shape
    tr = rows if rows % SUBLANES else _divisor_block(rows, SUBLANES, max(SUBLANES, min(rows, ADAMW_BLOCK_ELEMS // cols)))
    c1, c2 = 1.0 - ADAM_B1 ** ADAM_STEP, 1.0 - ADAM_B2 ** ADAM_STEP

    def body(w_ref, m_ref, v_ref, g_ref, d_ref, nm_ref, nv_ref):
        gv = g_ref[...]
        nm = ADAM_B1 * m_ref[...] + (1.0 - ADAM_B1) * gv
        nv = ADAM_B2 * v_ref[...] + (1.0 - ADAM_B2) * (gv * gv)
        d_ref[...] = -ADAM_LR * ((nm / c1) / (jnp.sqrt(nv / c2) + ADAM_EPS) + ADAM_WD * w_ref[...])
        nm_ref[...] = nm
        nv_ref[...] = nv

    blk = pl.BlockSpec((tr, cols), lambda i: (i, 0))
    outs = _pc(body, name=name, grid=(rows // tr,), in_specs=[blk] * 4, out_specs=(blk,) * 3,
               out_shape=(S((rows, cols), f32),) * 3, compiler_params=_cparams(("arbitrary",)))(w2, m2, v2, g2)
    return tuple(o.reshape(shape) for o in outs)


_WEIGHTS = ("meta_tokens", "norm_mix", "norm_ffn", "norm_final", "ev_w_in", "ev_conv_a", "ev_ln_a_g", "ev_ln_a_b",
            "ev_conv_b", "ev_w_out", "od_w_in", "od_sinks", "od_mu", "od_w0", "od_w2", "od_a0", "od_a2", "od_g2",
            "od_k_k", "od_k_a", "od_r_k", "od_lnx_g", "od_lnx_b", "od_w_out", "ff_w_up", "ff_conv", "ff_conv_b", "ff_w_down")
_SMALL_SHARDED = (("meta_tokens", 1), ("ev_conv_a", 2), ("ev_conv_b", 2), ("od_mu", 1), ("od_w0", 1), ("od_w2", 2),
                  ("od_a0", 1), ("od_a2", 2), ("od_g2", 2), ("od_k_k", 1), ("od_k_a", 1), ("od_lnx_g", 1),
                  ("od_lnx_b", 1), ("ff_conv", 2))
_SMALL_REPLICATED = ("norm_mix", "norm_ffn", "norm_final", "ev_ln_a_g", "ev_ln_a_b", "od_sinks", "od_r_k", "ff_conv_b")
SLAB_UNIT = SUBLANES * LANES


def _pack(arrs):
    flat = jnp.concatenate([a.reshape(-1).astype(f32) for a in arrs])
    pad = (-flat.shape[0]) % SLAB_UNIT
    return jnp.pad(flat, (0, pad)).reshape(-1, LANES)


def _unpack(flat, shapes):
    out, off = [], 0
    for shp in shapes:
        size = 1
        for s in shp:
            size *= s
        out.append(flat[..., off:off + size].reshape(flat.shape[:-1] + tuple(shp)))
        off += size
    return out


def _full_shape(shape, axis):
    return tuple(N_DEV * s if i == axis else s for i, s in enumerate(shape))


def kernel(x, meta_tokens, norm_mix, norm_ffn, norm_final, ev_w_in, ev_conv_a, ev_ln_a_g, ev_ln_a_b, ev_conv_b, ev_w_out, od_w_in, od_sinks, od_mu, od_w0, od_w2, od_a0, od_a2, od_g2, od_k_k, od_k_a, od_r_k, od_lnx_g, od_lnx_b, od_w_out, ff_w_up, ff_conv, ff_conv_b, ff_w_down, loss_target, m_meta_tokens, m_norm_mix, m_norm_ffn, m_norm_final, m_ev_w_in, m_ev_conv_a, m_ev_ln_a_g, m_ev_ln_a_b, m_ev_conv_b, m_ev_w_out, m_od_w_in, m_od_sinks, m_od_mu, m_od_w0, m_od_w2, m_od_a0, m_od_a2, m_od_g2, m_od_k_k, m_od_k_a, m_od_r_k, m_od_lnx_g, m_od_lnx_b, m_od_w_out, m_ff_w_up, m_ff_conv, m_ff_conv_b, m_ff_w_down, v_meta_tokens, v_norm_mix, v_norm_ffn, v_norm_final, v_ev_w_in, v_ev_conv_a, v_ev_ln_a_g, v_ev_ln_a_b, v_ev_conv_b, v_ev_w_out, v_od_w_in, v_od_sinks, v_od_mu, v_od_w0, v_od_w2, v_od_a0, v_od_a2, v_od_g2, v_od_k_k, v_od_k_a, v_od_r_k, v_od_lnx_g, v_od_lnx_b, v_od_w_out, v_ff_w_up, v_ff_conv, v_ff_conv_b, v_ff_w_down):
    A = dict(locals())
    px, py, pc = _mesh_pos()
    me = _dev(px, py, pc)
    me_vec = jnp.reshape(me, (1,)).astype(jnp.int32)
    rows = lambda a: a.reshape(N_DEV * a.shape[1], a.shape[2])
    blocks = lambda a: a.reshape(N_DEV, a.shape[0] // N_DEV, a.shape[1])

    shards = dict(ev_in=ev_w_in[0].T, ev_out=ev_w_out[0], ff0_up=ff_w_up[0].T, ff0_down=ff_w_down[0], od_in=od_w_in[0].T,
                  od_out=od_w_out[0], ff1_up=ff_w_up[1].T, ff1_down=ff_w_down[1])
    shards = {n: b.astype(bf16) for n, b in shards.items()}
    small_shapes = [A[n].shape for n, _ in _SMALL_SHARDED]
    gathered = _all_gather([shards["ev_in"], shards["ev_out"], _pack([A[n] for n, _ in _SMALL_SHARDED])], "gather_first")
    gathered, shards = lax.optimization_barrier((gathered, shards))
    fetch, tok0 = {}, jnp.zeros((), f32)
    for n in ("ff0_up", "ff0_down", "od_in", "od_out", "ff1_up", "ff1_down"):
        shard, tok0 = lax.optimization_barrier((shards[n], tok0))
        land = lax.dynamic_update_slice(lax.empty((N_DEV,) + shard.shape, bf16), shard[None], (me, 0, 0))
        fetch[n], token = _xchg_start([shard], [land], False, f"gather_{n}_start")
        tok0 = tok0 + token[0, 0]

    def get_w(n, after):
        if n in ("ev_in", "ev_out"):
            return rows(gathered[("ev_in", "ev_out").index(n)])
        return rows(_xchg_wait(fetch[n], after, f"gather_{n}_wait")[1][0])

    W = {}
    for (n, ax), seg in zip(_SMALL_SHARDED, _unpack(gathered[-1].reshape(N_DEV, -1), small_shapes)):
        W[n] = jnp.moveaxis(seg, 0, ax).reshape(_full_shape(A[n].shape, ax))
    for n in ("ev_conv_a", "ev_conv_b", "od_w2", "od_a2", "od_g2"):
        W[n] = W[n][0]
    for n in _SMALL_REPLICATED:
        W[n] = A[n]
    W["od_r_k"] = od_r_k[0]

    small_shape = {n: _full_shape(A[n].shape, ax) for n, ax in _SMALL_SHARDED}
    small_shape.update({n: A[n].shape for n in _SMALL_REPLICATED})
    sent, small_sent, small_names = {}, {}, {}

    def put_g(n, g):
        g8 = blocks(g)
        sent[n], token = _xchg_start([g8], [lax.empty(g8.shape, g8.dtype)], True, f"reduce_{n}_start")
        return token[0, 0]

    def put_small(gs, stage="early"):
        small_names[stage] = sorted(gs)
        slab = _pack([gs[n] for n in small_names[stage]])
        land = lax.dynamic_update_slice(lax.empty((N_DEV,) + slab.shape, f32), slab[None], (me, 0, 0))
        small_sent[stage], small_tok[stage] = _xchg_start([slab], [land], False, f"gather_{stage}_small_grads_start")
        return small_tok[stage][0, 0]

    small_tok = {}
    loss_tile, grad_x, late = _local_step(x[0], loss_target[0], W, get_w, put_g, put_small, tok0)
    put_small(late, "late")
    late_tok = small_tok["late"]

    gsh, prev = {}, late_tok
    for n in ("ff1_down", "ff1_up", "od_out", "od_in", "ff0_down", "ff0_up", "ev_out", "ev_in"):
        srcs, lands = _xchg_wait(sent[n], prev, f"reduce_{n}_wait")
        gsh[n] = prev = _rs_sum(srcs[0], lands[0], me_vec, f"reduce_{n}_sum")
    grads = dict(ev_w_in=gsh["ev_in"].T[None], ev_w_out=gsh["ev_out"][None], od_w_in=gsh["od_in"].T[None],
                 od_w_out=gsh["od_out"][None], ff_w_up=jnp.stack([gsh["ff0_up"].T, gsh["ff1_up"].T]),
                 ff_w_down=jnp.stack([gsh["ff0_down"], gsh["ff1_down"]]))

    delta, new_m, new_v = {}, {}, {}
    for n in ("ff_w_up", "ff_w_down", "od_w_in", "od_w_out", "ev_w_in", "ev_w_out"):
        delta[n], new_m[n], new_v[n] = _adamw(A[n], A["m_" + n], A["v_" + n], grads[n], "adamw_" + n)
    for stage in ("early", "late"):
        gsm = _xchg_wait(small_sent[stage], delta["ev_w_in"], f"gather_{stage}_small_grads_wait")[1][0]
        summed = _sum_devices(gsm, f"sum_{stage}_small_grads").reshape(-1)
        for n, full in zip(small_names[stage], _unpack(summed, [small_shape[n] for n in small_names[stage]])):
            grads[n] = full
    for n, ax in _SMALL_SHARDED:
        size = A[n].shape[ax]
        grads[n] = lax.dynamic_slice_in_dim(grads[n], me * size, size, axis=ax)
    for n in small_shape:
        delta[n], new_m[n], new_v[n] = _adamw(A[n], A["m_" + n], A["v_" + n], grads[n], "adamw_" + n)

    loss = lax.psum(loss_tile[0, 0], ("x", "y", "c"))
    return (loss, grad_x[None], *[grads[n] for n in _WEIGHTS], *[delta[n] for n in _WEIGHTS],
            *[new_m[n] for n in _WEIGHTS], *[new_v[n] for n in _WEIGHTS])
```

```python
import jax
import jax.numpy as jnp
from jax import lax
from jax.experimental import pallas as pl
from jax.experimental.pallas import tpu as pltpu

f32, bf16 = jnp.float32, jnp.bfloat16

D_MODEL = 1024
N_META = 16
RMS_EPS = 1e-6
LN_EPS = 1e-5
D_A = 512
CONV_A_WIDTH = 31
CONV_B_WIDTH = 3
HEAD_DIM = 64
N_Q_HEADS = 8
N_KV_HEADS = 2
GQA_GROUP = 4
D_ATT = 512
D_KV = 128
BLOCK = 128
ROPE_THETA = 10000.0
D_R = 512
LORA_W, LORA_A, LORA_G = 64, 64, 128
RWKV_GN_EPS = 64e-5
ATT_COLS = D_ATT + 2 * D_KV
RWKV_COLS = 3 * D_R + LORA_W + LORA_A + LORA_G
D_FF = 2816
FF_CONV_WIDTH = 3
FF_BLOCK = 256
NEG_INF = -1e30
ATT_PAD = BLOCK - N_META
ATT_SCALE = HEAD_DIM ** -0.5

ADAM_LR, ADAM_B1, ADAM_B2, ADAM_EPS, ADAM_WD, ADAM_STEP = 0.001, 0.9, 0.999, 1e-08, 0.01, 10

N_DEV = 8
LANES = 128
SUBLANES = 8
SCAN_CHUNK = 48
PAIR_ROWS = 4 * HEAD_DIM
V7X_VMEM_LIMIT = 56 * 1024 * 1024
ADAMW_BLOCK_ELEMS = 400 * 1024
GRAD_WIRE_DTYPE = bf16
MESH = pl.DeviceIdType.MESH
S = jax.ShapeDtypeStruct
HIGHEST = lax.Precision.HIGHEST


def _pc(body, **kw):
    return pl.pallas_call(body, **kw)


def _cparams(sem=None):
    return pltpu.CompilerParams(dimension_semantics=sem, vmem_limit_bytes=V7X_VMEM_LIMIT)


def _divisor_block(t, unit, limit):
    best = unit
    for rb in range(unit, limit + 1, unit):
        if t % rb == 0:
            best = rb
    assert t % best == 0, (t, unit)
    return best


def _row_block(t):
    return _divisor_block(t, 16, 704)


def _row_block8(t):
    return _divisor_block(t, 8, 344)


def _col_tile(n, cap):
    return _divisor_block(n, LANES, min(n, cap)) if n % LANES == 0 else n


def _full(shape):
    nd = len(shape)
    return pl.BlockSpec(shape, lambda *_: (0,) * nd)


def _sigmoid(x):
    return jax.nn.sigmoid(x)


_DIMS = {"nn": (((1,), (0,)), ((), ())), "nt": (((1,), (1,)), ((), ())), "tn": (((0,), (0,)), ((), ()))}
MM_MAX_K = 2816
MM_MAX_TM = 704
MM_MAX_TN = 1408


def _mm(a, b, mode, name, out_dtype=f32, res=None, b_row0=0, out_rows=None, out_row0=0, into=None):
    if mode == "nn":
        (m, k), n, k2 = a.shape, b.shape[1], a.shape[1]
        assert b_row0 % k == 0 and b_row0 + k <= b.shape[0], (a.shape, b.shape, b_row0)
    elif mode == "nt":
        (m, k), (n, k2) = a.shape, b.shape
    else:
        (k, m), (k2, n) = a.shape, b.shape
    assert k == k2, (a.shape, b.shape, mode)
    tm = _row_block(m) if m % LANES else _col_tile(m, MM_MAX_TM)
    tn = _col_tile(n, MM_MAX_TN)
    nk = 1 if (mode == "tn" or k <= MM_MAX_K) else k // MM_MAX_K
    tk = k // nk
    assert tk * nk == k
    dims = _DIMS[mode]

    def body(a_ref, b_ref, *rest):
        part = lax.dot_general(a_ref[...].astype(bf16), b_ref[...].astype(bf16), dims, preferred_element_type=f32)
        if nk == 1:
            o_ref = rest[-1]
            if res is not None:
                part = part + rest[0][...]
            o_ref[...] = part.astype(out_dtype)
            return
        o_ref, acc_ref = rest[-2], rest[-1]
        kk = pl.program_id(2)

        @pl.when(kk == 0)
        def _():
            acc_ref[...] = part

        @pl.when(kk > 0)
        def _():
            acc_ref[...] += part

        @pl.when(kk == nk - 1)
        def _():
            acc = acc_ref[...]
            if res is not None:
                acc = acc + rest[0][...]
            o_ref[...] = acc.astype(out_dtype)

    if mode == "tn":
        a_spec = pl.BlockSpec((k, tm), lambda i, j, kk: (0, i))
    else:
        a_spec = pl.BlockSpec((tm, tk), lambda i, j, kk: (i, kk))
    if mode == "nt":
        b_spec = pl.BlockSpec((tn, tk), lambda i, j, kk: (j, kk))
    else:
        b_spec = pl.BlockSpec((tk, tn), lambda i, j, kk: (kk + b_row0 // tk, j))
    assert out_row0 % tm == 0 and res is None or out_row0 == 0
    o_spec = pl.BlockSpec((tm, tn), lambda i, j, kk: (i + out_row0 // tm, j))
    ins, specs, aliases = [a, b], [a_spec, b_spec], {}
    if res is not None:
        ins.append(res)
        specs.append(o_spec)
    if into is not None:
        assert into.shape == (out_rows, n) and into.dtype == out_dtype
        aliases = {len(ins): 0}
        ins.append(into)
        specs.append(pl.BlockSpec(memory_space=pl.ANY))
    scratch = [pltpu.VMEM((tm, tn), f32)] if nk > 1 else []
    return _pc(body, name=name, grid=(m // tm, n // tn, nk), in_specs=specs, out_specs=o_spec,
               out_shape=S((out_rows or m, n), out_dtype), scratch_shapes=scratch, input_output_aliases=aliases,
               compiler_params=_cparams(("arbitrary", "arbitrary", "arbitrary")))(*ins)


def _rms_fwd(x, g, name):
    t, d = x.shape
    rb = _row_block(t)

    def body(x_ref, g_ref, o_ref):
        xv = x_ref[...]
        rstd = lax.rsqrt(jnp.mean(xv * xv, axis=-1, keepdims=True) + RMS_EPS)
        o_ref[...] = (xv * rstd * g_ref[...]).astype(bf16)

    row = pl.BlockSpec((rb, d), lambda i: (i, 0))
    return _pc(body, name=name, grid=(t // rb,), in_specs=[row, _full((1, d))], out_specs=row,
               out_shape=S((t, d), bf16), compiler_params=_cparams(("arbitrary",)))(x, g.reshape(1, d))


def _rms_bwd(dy, x, g, dres, name):
    t, d = x.shape
    rb = _row_block8(t)

    def body(dy_ref, x_ref, g_ref, dres_ref, dx_ref, dg_ref):
        @pl.when(pl.program_id(0) == 0)
        def _():
            dg_ref[...] = jnp.zeros_like(dg_ref)
        xv, dyv = x_ref[...], dy_ref[...]
        rstd = lax.rsqrt(jnp.mean(xv * xv, axis=-1, keepdims=True) + RMS_EPS)
        xn = xv * rstd
        dg_ref[...] += jnp.sum(dyv * xn, axis=0, keepdims=True)
        dxh = dyv * g_ref[...]
        dx_ref[...] = dres_ref[...] + rstd * (dxh - xn * jnp.mean(dxh * xn, axis=-1, keepdims=True))

    row = pl.BlockSpec((rb, d), lambda i: (i, 0))
    return _pc(body, name=name, grid=(t // rb,), in_specs=[row, row, _full((1, d)), row],
               out_specs=(row, _full((1, d))), out_shape=(S((t, d), f32), S((1, d), f32)),
               compiler_params=_cparams(("arbitrary",)))(dy, x, g.reshape(1, d), dres)


def _final_loss(h, g, target_padded):
    t, d = h.shape
    rb = _row_block8(t)

    def body(x_ref, g_ref, t_ref, loss_ref, dx_ref, dg_ref):
        i = pl.program_id(0)

        @pl.when(i == 0)
        def _():
            dg_ref[...] = jnp.zeros_like(dg_ref)
            loss_ref[...] = jnp.zeros_like(loss_ref)
        xv = x_ref[...]
        rstd = lax.rsqrt(jnp.mean(xv * xv, axis=-1, keepdims=True) + RMS_EPS)
        xn = xv * rstd
        gv = g_ref[...]
        row = i * rb + lax.broadcasted_iota(jnp.int32, (rb, 1), 0)
        diff = jnp.where(row >= N_META, xn * gv - t_ref[...], 0.0)
        loss_ref[...] += 0.5 * jnp.sum(jnp.mean(diff * diff, axis=-1, keepdims=True))
        dout = diff * (1.0 / d)
        dg_ref[...] += jnp.sum(dout * xn, axis=0, keepdims=True)
        dxh = dout * gv
        dx_ref[...] = rstd * (dxh - xn * jnp.mean(dxh * xn, axis=-1, keepdims=True))

    row = pl.BlockSpec((rb, d), lambda i: (i, 0))
    return _pc(body, name="final_loss", grid=(t // rb,), in_specs=[row, _full((1, d)), row],
               out_specs=(_full((SUBLANES, LANES)), row, _full((1, d))),
               out_shape=(S((SUBLANES, LANES), f32), S((t, d), f32), S((1, d), f32)),
               compiler_params=_cparams(("arbitrary",)))(h, g.reshape(1, d), target_padded)


CONV_LEAD = 32


def _fill_front_padded(pad_ref, x, t):
    pad_ref[0:CONV_LEAD, :] = jnp.zeros((CONV_LEAD, x.shape[1]), f32)
    pad_ref[CONV_LEAD:CONV_LEAD + t, :] = x


def _fill_back_padded(pad_ref, x, t):
    pad_ref[0:t, :] = x
    pad_ref[t:t + CONV_LEAD, :] = jnp.zeros((CONV_LEAD, x.shape[1]), f32)


def _conv_rows(pad_ref, w_ref, kw, r0, nr):
    acc = None
    for j in range(kw):
        lo = CONV_LEAD + r0 - (kw - 1) + j
        term = w_ref[j:j + 1, :] * pad_ref[lo:lo + nr, :]
        acc = term if acc is None else acc + term
    return acc


def _conv_t_rows(padb_ref, w_ref, kw, r0, nr):
    acc = None
    for j in range(kw):
        lo = r0 + (kw - 1) - j
        term = w_ref[j:j + 1, :] * padb_ref[lo:lo + nr, :]
        acc = term if acc is None else acc + term
    return acc


def _conv_dw_rows(dy_blk, pad_ref, kw, r0, nr):
    out = []
    for j in range(kw):
        lo = CONV_LEAD + r0 - (kw - 1) + j
        out.append(jnp.sum(dy_blk * pad_ref[lo:lo + nr, :], axis=0, keepdims=True))
    return out


def _acc_list(a, b):
    return b if a is None else [x + y for x, y in zip(a, b)]


def _ev_a_conv(p, conv_a):
    t = p.shape[0]
    cr = _row_block8(t)
    nb = D_A // LANES

    def body(av_ref, ag_ref, w_ref, o_ref, pad_ref):
        _fill_front_padded(pad_ref, av_ref[...] * _sigmoid(ag_ref[...]), t)
        for r in range(t // cr):
            o_ref[r * cr:(r + 1) * cr, :] = _conv_rows(pad_ref, w_ref, CONV_A_WIDTH, r * cr, cr)

    col = lambda off: pl.BlockSpec((t, LANES), lambda j: (0, j + off))
    return _pc(body, name="ev_a_conv", grid=(nb,),
               in_specs=[col(0), col(nb), pl.BlockSpec((CONV_A_WIDTH, LANES), lambda j: (0, j))],
               out_specs=col(0), out_shape=S((t, D_A), f32),
               scratch_shapes=[pltpu.VMEM((t + CONV_LEAD, LANES), f32)],
               compiler_params=_cparams(("arbitrary",)))(p, p, conv_a)


def _ln_silu(uc, g, b):
    mu = jnp.mean(uc, axis=-1, keepdims=True)
    xc = uc - mu
    var = jnp.mean(xc * xc, axis=-1, keepdims=True)
    y = xc * lax.rsqrt(var + LN_EPS) * g + b
    return y * _sigmoid(y)


def _ev_a_norm(uc, g, b):
    t, d = uc.shape
    rb = _row_block(t)

    def body(u_ref, g_ref, b_ref, o_ref):
        o_ref[...] = _ln_silu(u_ref[...], g_ref[...], b_ref[...]).astype(bf16)

    row = pl.BlockSpec((rb, d), lambda i: (i, 0))
    return _pc(body, name="ev_a_norm", grid=(t // rb,), in_specs=[row, _full((1, d)), _full((1, d))],
               out_specs=row, out_shape=S((t, 2 * d), bf16), compiler_params=_cparams(("arbitrary",)))(uc, g, b)


def _ev_a_norm_bwd(dy, uc, g, b):
    t, d = uc.shape
    rb = _row_block8(t)

    def body(dy_ref, u_ref, g_ref, b_ref, du_ref, dg_ref, db_ref):
        @pl.when(pl.program_id(0) == 0)
        def _():
            dg_ref[...] = jnp.zeros_like(dg_ref)
            db_ref[...] = jnp.zeros_like(db_ref)
        _, vjp = jax.vjp(_ln_silu, u_ref[...], g_ref[...], b_ref[...])
        du, dg, db = vjp(dy_ref[...])
        du_ref[...] = du
        dg_ref[...] += dg
        db_ref[...] += db

    row = pl.BlockSpec((rb, d), lambda i: (i, 0))
    return _pc(body, name="ev_a_norm_bwd", grid=(t // rb,), in_specs=[row, row, _full((1, d)), _full((1, d))],
               out_specs=(row, _full((1, d)), _full((1, d))),
               out_shape=(S((t, d), f32), S((1, d), f32), S((1, d), f32)),
               compiler_params=_cparams(("arbitrary",)))(dy, uc, g, b)


def _ev_a_conv_bwd(duc, p, conv_a):
    t = p.shape[0]
    cr = _row_block8(t)
    nb = D_A // LANES

    def body(dy_ref, av_ref, ag_ref, w_ref, dav_ref, dag_ref, dw_ref, pad_ref, padb_ref):
        _fill_front_padded(pad_ref, av_ref[...] * _sigmoid(ag_ref[...]), t)
        _fill_back_padded(padb_ref, dy_ref[...], t)
        dw = None
        for r in range(t // cr):
            rows = slice(r * cr, (r + 1) * cr)
            du = _conv_t_rows(padb_ref, w_ref, CONV_A_WIDTH, r * cr, cr)
            avr = av_ref[rows, :]
            sgr = _sigmoid(ag_ref[rows, :])
            dav_ref[rows, :] = du * sgr
            dag_ref[rows, :] = du * avr * sgr * (1.0 - sgr)
            dw = _acc_list(dw, _conv_dw_rows(dy_ref[rows, :], pad_ref, CONV_A_WIDTH, r * cr, cr))
        for j in range(CONV_A_WIDTH):
            dw_ref[j:j + 1, :] = dw[j]

    col = lambda off: pl.BlockSpec((t, LANES), lambda j: (0, j + off))
    wsp = pl.BlockSpec((CONV_A_WIDTH, LANES), lambda j: (0, j))
    return _pc(body, name="ev_a_conv_bwd", grid=(nb,), in_specs=[col(0), col(0), col(nb), wsp],
               out_specs=(col(0), col(0), wsp),
               out_shape=(S((t, D_A), f32), S((t, D_A), f32), S((CONV_A_WIDTH, D_A), f32)),
               scratch_shapes=[pltpu.VMEM((t + CONV_LEAD, LANES), f32), pltpu.VMEM((t + CONV_LEAD, LANES), f32)],
               compiler_params=_cparams(("arbitrary",)))(duc, p, p, conv_a)


def _ev_b(p, conv_b, y):
    t = p.shape[0]
    cr = _row_block8(t)
    nb = D_A // LANES

    def body(gb_ref, gc_ref, xi_ref, w_ref, y_ref, o_ref, pad_ref, stage_ref):
        _fill_front_padded(pad_ref, gc_ref[...] * xi_ref[...], t)
        for r in range(t // cr):
            rows = slice(r * cr, (r + 1) * cr)
            stage_ref[rows, :] = gb_ref[rows, :] * _conv_rows(pad_ref, w_ref, CONV_B_WIDTH, r * cr, cr)
        o_ref[...] = stage_ref[...].astype(bf16)

    col = lambda off: pl.BlockSpec((t, LANES), lambda j: (0, j + off))
    return _pc(body, name="ev_b", grid=(nb,),
               in_specs=[col(2 * nb), col(3 * nb), col(4 * nb), pl.BlockSpec((CONV_B_WIDTH, LANES), lambda j: (0, j)), HBM],
               out_specs=col(nb), out_shape=S(y.shape, bf16), input_output_aliases={4: 0},
               scratch_shapes=[pltpu.VMEM((t + CONV_LEAD, LANES), f32), pltpu.VMEM((t, LANES), f32)],
               compiler_params=_cparams(("arbitrary",)))(p, p, p, conv_b, y)


def _ev_b_bwd(dy, p, conv_b):
    t = p.shape[0]
    cr = _row_block8(t)
    nb = D_A // LANES

    def body(dy_ref, gb_ref, gc_ref, xi_ref, w_ref, dgb_ref, dgc_ref, dxi_ref, dw_ref, pad_ref, padb_ref):
        _fill_front_padded(pad_ref, gc_ref[...] * xi_ref[...], t)
        _fill_back_padded(padb_ref, dy_ref[...] * gb_ref[...], t)
        dw = None
        for r in range(t // cr):
            rows = slice(r * cr, (r + 1) * cr)
            dgb_ref[rows, :] = dy_ref[rows, :] * _conv_rows(pad_ref, w_ref, CONV_B_WIDTH, r * cr, cr)
            dcx = _conv_t_rows(padb_ref, w_ref, CONV_B_WIDTH, r * cr, cr)
            dgc_ref[rows, :] = dcx * xi_ref[rows, :]
            dxi_ref[rows, :] = dcx * gc_ref[rows, :]
            dw = _acc_list(dw, _conv_dw_rows(padb_ref[rows, :], pad_ref, CONV_B_WIDTH, r * cr, cr))
        for j in range(CONV_B_WIDTH):
            dw_ref[j:j + 1, :] = dw[j]

    col = lambda off: pl.BlockSpec((t, LANES), lambda j: (0, j + off))
    wsp = pl.BlockSpec((CONV_B_WIDTH, LANES), lambda j: (0, j))
    return _pc(body, name="ev_b_bwd", grid=(nb,), in_specs=[col(nb), col(2 * nb), col(3 * nb), col(4 * nb), wsp],
               out_specs=(col(0), col(0), col(0), wsp),
               out_shape=(S((t, D_A), f32), S((t, D_A), f32), S((t, D_A), f32), S((CONV_B_WIDTH, D_A), f32)),
               scratch_shapes=[pltpu.VMEM((t + CONV_LEAD, LANES), f32), pltpu.VMEM((t + CONV_LEAD, LANES), f32)],
               compiler_params=_cparams(("arbitrary",)))(dy, p, p, p, conv_b)


def _ffn_mid(u, conv_w, conv_b, name):
    t = u.shape[0]
    cr = _row_block8(t)
    nb = D_FF // FF_BLOCK

    def one(gt_ref, vl_ref, w_ref, b_ref, o_ref, pad_ref, stage_ref):
        _fill_front_padded(pad_ref, gt_ref[...], t)
        for r in range(t // cr):
            rows = slice(r * cr, (r + 1) * cr)
            gc = _conv_rows(pad_ref, w_ref, FF_CONV_WIDTH, r * cr, cr) + b_ref[...]
            stage_ref[rows, :] = gc * _sigmoid(gc) * vl_ref[rows, :]
        o_ref[...] = stage_ref[...].astype(bf16)

    def body(*refs):
        for h in range(FF_BLOCK // LANES):
            one(*[r.at[:, pl.ds(h * LANES, LANES)] for r in refs[:5]], *refs[5:])

    col = lambda off: pl.BlockSpec((t, FF_BLOCK), lambda j: (0, j + off))
    return _pc(body, name=name, grid=(nb,),
               in_specs=[col(0), col(nb), pl.BlockSpec((FF_CONV_WIDTH, FF_BLOCK), lambda j: (0, j)),
                         pl.BlockSpec((1, FF_BLOCK), lambda j: (0, j))],
               out_specs=col(0), out_shape=S((t, D_FF), bf16),
               scratch_shapes=[pltpu.VMEM((t + CONV_LEAD, LANES), f32), pltpu.VMEM((t, LANES), f32)],
               compiler_params=_cparams(("arbitrary",)))(u, u, conv_w, conv_b.reshape(1, D_FF))


def _ffn_mid_bwd(dz, u, conv_w, conv_b, name):
    t = u.shape[0]
    cr = _row_block8(t)
    nb = D_FF // FF_BLOCK
    nh = FF_BLOCK // LANES

    def body(*refs):
        for h in range(nh):
            one(*[r.at[:, pl.ds(h * LANES, LANES)] for r in refs[:9]], *refs[9:])

    def one(dz_ref, gt_ref, vl_ref, w_ref, b_ref, du_ref, dv_ref, dw_ref, db_ref, pad_ref, padb_ref, stage_ref):
        _fill_front_padded(pad_ref, gt_ref[...], t)
        dw, db = None, None
        for r in range(t // cr):
            rows = slice(r * cr, (r + 1) * cr)
            lo = CONV_LEAD + r * cr - (FF_CONV_WIDTH - 1)
            taps = [pad_ref[lo + j:lo + j + cr, :] for j in range(FF_CONV_WIDTH)]
            gc = sum(w_ref[j:j + 1, :] * taps[j] for j in range(FF_CONV_WIDTH)) + b_ref[...]
            sg = _sigmoid(gc)
            dzr = dz_ref[rows, :]
            stage_ref[rows, :] = dzr * gc * sg
            dgc = dzr * vl_ref[rows, :] * sg * (1.0 + gc * (1.0 - sg))
            padb_ref[rows, :] = dgc
            dw = _acc_list(dw, [jnp.sum(dgc * tap, axis=0, keepdims=True) for tap in taps])
            pb = jnp.sum(dgc, axis=0, keepdims=True)
            db = pb if db is None else db + pb
        padb_ref[t:t + CONV_LEAD, :] = jnp.zeros((CONV_LEAD, LANES), f32)
        for r in range(t // cr):
            pad_ref[r * cr:(r + 1) * cr, :] = _conv_t_rows(padb_ref, w_ref, FF_CONV_WIDTH, r * cr, cr)
        du_ref[...] = pad_ref[0:t, :].astype(du_ref.dtype)
        dv_ref[...] = stage_ref[...].astype(dv_ref.dtype)
        for j in range(FF_CONV_WIDTH):
            dw_ref[j:j + 1, :] = dw[j]
        db_ref[...] = db

    col = lambda off: pl.BlockSpec((t, FF_BLOCK), lambda j: (0, j + off))
    wsp = pl.BlockSpec((FF_CONV_WIDTH, FF_BLOCK), lambda j: (0, j))
    bsp = pl.BlockSpec((1, FF_BLOCK), lambda j: (0, j))
    return _pc(body, name=name, grid=(nb,), in_specs=[col(0), col(0), col(nb), wsp, bsp],
               out_specs=(col(0), col(0), wsp, bsp),
               out_shape=(S((t, D_FF), bf16), S((t, D_FF), bf16), S((FF_CONV_WIDTH, D_FF), f32), S((1, D_FF), f32)),
               scratch_shapes=[pltpu.VMEM((t + CONV_LEAD, LANES), f32), pltpu.VMEM((t + CONV_LEAD, LANES), f32),
                               pltpu.VMEM((t, LANES), f32)],
               compiler_params=_cparams(("arbitrary",)))(dz, u, u, conv_w, conv_b.reshape(1, D_FF))


def _swap_halves(x):
    w = x.shape[1]
    lane = lax.broadcasted_iota(jnp.int32, x.shape, 1) % HEAD_DIM
    return jnp.where(lane < HEAD_DIM // 2, pltpu.roll(x, w - HEAD_DIM // 2, axis=1), pltpu.roll(x, HEAD_DIM // 2, axis=1))


def _rope_pack(patt, c64, s64):
    t = patt.shape[0]
    tp = t + ATT_PAD

    def body(p_ref, c_ref, s_ref, q_ref, k_ref, v_ref):
        c, s = c_ref[...], s_ref[...]

        def rope(x, nh):
            cc = jnp.concatenate([c] * nh, axis=1)
            ss = jnp.concatenate([s] * nh, axis=1)
            return x * cc + _swap_halves(x) * ss

        for ref, val in ((q_ref, rope(p_ref[:, 0:D_ATT], N_Q_HEADS)),
                         (k_ref, rope(p_ref[:, D_ATT:D_ATT + D_KV], N_KV_HEADS)),
                         (v_ref, p_ref[:, D_ATT + D_KV:ATT_COLS])):
            ref[0:ATT_PAD, :] = jnp.zeros((ATT_PAD, val.shape[1]), bf16)
            ref[ATT_PAD:tp, :] = val.astype(bf16)

    return _pc(body, name="rope_pack", in_specs=[_full((t, ATT_COLS)), _full((t, HEAD_DIM)), _full((t, HEAD_DIM))],
               out_specs=(_full((tp, D_ATT)), _full((tp, D_KV)), _full((tp, D_KV))), grid=(1,),
               out_shape=(S((tp, D_ATT), bf16), S((tp, D_KV), bf16), S((tp, D_KV), bf16)),
               compiler_params=_cparams(("arbitrary",)))(patt, c64, s64)


def _rope_bwd(dqp, dkp, dvp, c64, s64):
    tp = dqp.shape[0]
    t = tp - ATT_PAD

    def body(dq_ref, dk_ref, dv_ref, c_ref, s_ref, o_ref):
        c, s = c_ref[...], s_ref[...]

        def unrope(dy, nh):
            cc = jnp.concatenate([c] * nh, axis=1)
            ss = jnp.concatenate([s] * nh, axis=1)
            return dy * cc + _swap_halves(dy * ss)

        o_ref[:, 0:D_ATT] = unrope(dq_ref[ATT_PAD:tp, :], N_Q_HEADS).astype(bf16)
        o_ref[:, D_ATT:D_ATT + D_KV] = unrope(dk_ref[ATT_PAD:tp, :], N_KV_HEADS).astype(bf16)
        o_ref[:, D_ATT + D_KV:ATT_COLS] = dv_ref[ATT_PAD:tp, :].astype(bf16)

    return _pc(body, name="rope_bwd", grid=(1,),
               in_specs=[_full((tp, D_ATT)), _full((tp, D_KV)), _full((tp, D_KV)), _full((t, HEAD_DIM)), _full((t, HEAD_DIM))],
               out_specs=_full((t, ATT_COLS)), out_shape=S((t, ATT_COLS), bf16),
               compiler_params=_cparams(("arbitrary",)))(dqp, dkp, dvp, c64, s64)


def _attn_masks(n):
    rows = GQA_GROUP * BLOCK
    ri = lax.broadcasted_iota(jnp.int32, (rows, BLOCK), 0) % BLOCK
    ci = lax.broadcasted_iota(jnp.int32, (rows, BLOCK), 1)
    m_cur = (ci <= ri) & (ci >= jnp.where(n >= 1, 0, ATT_PAD))
    m_prev = ci > ri + jnp.where(n >= 2, 0, BLOCK)
    m_meta = ci >= jnp.where(n >= 1, ATT_PAD, BLOCK)
    return m_cur, m_prev, m_meta


def _attn_probs(qg, kc, kp, km, masks, skv):
    def scores(k, m):
        s = lax.dot_general(qg, k, _DIMS["nt"], preferred_element_type=f32) * ATT_SCALE
        return jnp.where(m, s, NEG_INF)
    s_c, s_p, s_m = scores(kc, masks[0]), scores(kp, masks[1]), scores(km, masks[2])
    mx = jnp.maximum(jnp.maximum(jnp.max(s_c, axis=-1, keepdims=True), jnp.max(s_p, axis=-1, keepdims=True)),
                     jnp.maximum(jnp.max(s_m, axis=-1, keepdims=True), skv))
    e_c, e_p, e_m, e_s = jnp.exp(s_c - mx), jnp.exp(s_p - mx), jnp.exp(s_m - mx), jnp.exp(skv - mx)
    den = (jnp.sum(e_c, axis=-1, keepdims=True) + jnp.sum(e_p, axis=-1, keepdims=True)
           + jnp.sum(e_m, axis=-1, keepdims=True) + e_s)
    inv = 1.0 / den
    return e_c * inv, e_p * inv, e_m * inv, e_s * inv


def _sink_rows(sk_ref, g):
    hrow = lax.broadcasted_iota(jnp.int32, (GQA_GROUP * BLOCK, 1), 0) // BLOCK
    skv = jnp.zeros((GQA_GROUP * BLOCK, 1), f32)
    for hh in range(GQA_GROUP):
        skv = jnp.where(hrow == hh, sk_ref[0, GQA_GROUP * g + hh], skv)
    return skv, hrow


def _stack_heads(ref, g):
    return jnp.concatenate([ref[:, (GQA_GROUP * g + hh) * HEAD_DIM:(GQA_GROUP * g + hh + 1) * HEAD_DIM]
                            for hh in range(GQA_GROUP)], axis=0)


def _attn_specs():
    blk = lambda w: pl.BlockSpec((BLOCK, w), lambda n: (n, 0))
    prev = pl.BlockSpec((BLOCK, D_KV), lambda n: (jnp.maximum(n - 1, 0), 0))
    meta = pl.BlockSpec((BLOCK, D_KV), lambda n: (0, 0))
    return blk, prev, meta


def _attn_fwd(qp, kp, vp, sinks):
    tp = qp.shape[0]
    blk, prev, meta = _attn_specs()

    def body(sk_ref, q_ref, kc_ref, kp_ref, km_ref, vc_ref, vp_ref, vm_ref, o_ref):
        masks = _attn_masks(pl.program_id(0))
        for g in range(N_KV_HEADS):
            sl = slice(g * HEAD_DIM, (g + 1) * HEAD_DIM)
            skv, _ = _sink_rows(sk_ref, g)
            p_c, p_p, p_m, _ = _attn_probs(_stack_heads(q_ref, g), kc_ref[:, sl], kp_ref[:, sl], km_ref[:, sl], masks, skv)
            o = (jnp.dot(p_c.astype(bf16), vc_ref[:, sl], preferred_element_type=f32)
                 + jnp.dot(p_p.astype(bf16), vp_ref[:, sl], preferred_element_type=f32)
                 + jnp.dot(p_m.astype(bf16), vm_ref[:, sl], preferred_element_type=f32))
            for hh in range(GQA_GROUP):
                h = GQA_GROUP * g + hh
                o_ref[:, h * HEAD_DIM:(h + 1) * HEAD_DIM] = o[hh * BLOCK:(hh + 1) * BLOCK].astype(bf16)

    return _pc(body, name="attn_fwd", grid=(tp // BLOCK,),
               in_specs=[pl.BlockSpec(memory_space=pltpu.SMEM), blk(D_ATT), blk(D_KV), prev, meta, blk(D_KV), prev, meta],
               out_specs=blk(D_ATT), out_shape=S((tp, D_ATT), bf16),
               compiler_params=_cparams(("arbitrary",)))(sinks, qp, kp, kp, kp, vp, vp, vp)


def _attn_bwd(qp, kp, vp, sinks, dop):
    tp = qp.shape[0]
    blk, prev, meta = _attn_specs()

    def body(sk_ref, q_ref, kc_ref, kp_ref, km_ref, vc_ref, vp_ref, vm_ref, do_ref, dq_ref, dk_ref, dv_ref, dsk_ref):
        n = pl.program_id(0)

        @pl.when(n == 0)
        def _():
            dk_ref[...] = jnp.zeros_like(dk_ref)
            dv_ref[...] = jnp.zeros_like(dv_ref)
            dsk_ref[...] = jnp.zeros_like(dsk_ref)
        masks = _attn_masks(n)
        cur = pl.ds(pl.multiple_of(n * BLOCK, BLOCK), BLOCK)
        prv = pl.ds(pl.multiple_of(jnp.maximum(n - 1, 0) * BLOCK, BLOCK), BLOCK)
        lane = lax.broadcasted_iota(jnp.int32, (1, LANES), 1)
        dsk = jnp.zeros((1, LANES), f32)
        for g in range(N_KV_HEADS):
            sl = slice(g * HEAD_DIM, (g + 1) * HEAD_DIM)
            skv, hrow = _sink_rows(sk_ref, g)
            qg = _stack_heads(q_ref, g)
            dog = _stack_heads(do_ref, g)
            ks = (kc_ref[:, sl], kp_ref[:, sl], km_ref[:, sl])
            vs = (vc_ref[:, sl], vp_ref[:, sl], vm_ref[:, sl])
            probs = _attn_probs(qg, ks[0], ks[1], ks[2], masks, skv)
            dps = [lax.dot_general(dog, v, _DIMS["nt"], preferred_element_type=f32) for v in vs]
            delta = sum(jnp.sum(p * dp, axis=-1, keepdims=True) for p, dp in zip(probs[:3], dps))
            dss = [(p * (dp - delta) * ATT_SCALE).astype(bf16) for p, dp in zip(probs[:3], dps)]
            dq = sum(jnp.dot(ds, k, preferred_element_type=f32) for ds, k in zip(dss, ks))
            for hh in range(GQA_GROUP):
                h = GQA_GROUP * g + hh
                dq_ref[:, h * HEAD_DIM:(h + 1) * HEAD_DIM] = dq[hh * BLOCK:(hh + 1) * BLOCK]
                dsk = dsk + jnp.where(lane == h, -jnp.sum(jnp.where(hrow == hh, probs[3] * delta, 0.0)), 0.0)
            for rows, p, ds in zip((cur, prv, slice(0, BLOCK)), probs[:3], dss):
                dv_ref[rows, sl] += lax.dot_general(p.astype(bf16), dog, _DIMS["tn"], preferred_element_type=f32)
                dk_ref[rows, sl] += lax.dot_general(ds, qg, _DIMS["tn"], preferred_element_type=f32)
        dsk_ref[...] += dsk

    return _pc(body, name="attn_bwd", grid=(tp // BLOCK,),
               in_specs=[pl.BlockSpec(memory_space=pltpu.SMEM), blk(D_ATT), blk(D_KV), prev, meta, blk(D_KV), prev, meta,
                         blk(D_ATT)],
               out_specs=(blk(D_ATT), _full((tp, D_KV)), _full((tp, D_KV)), _full((1, LANES))),
               out_shape=(S((tp, D_ATT), f32), S((tp, D_KV), f32), S((tp, D_KV), f32), S((1, LANES), f32)),
               compiler_params=_cparams(("arbitrary",)))(sinks, qp, kp, kp, kp, vp, vp, vp, dop)


def _seg(x, bm):
    hi = x.astype(bf16)
    lo = (x - hi.astype(f32)).astype(bf16)
    return jnp.dot(jnp.concatenate([hi, lo], axis=1), bm, preferred_element_type=f32)


@jax.custom_vjp
def _seg_linear(x, bm):
    return _seg(x, bm)


_seg_linear.defvjp(lambda x, bm: (_seg(x, bm), bm), lambda bm, ct: (_seg(ct, bm), jnp.zeros_like(bm)))


def _softplus(y):
    return jnp.maximum(y, 0.0) + jnp.log(1.0 + jnp.exp(-jnp.abs(y)))


def _prep_fn(xr, xk, xwd, xad, xgd, w0, w2, a0, a2, g2, k_k, k_a, bm, seg=_seg):
    xw = w0 + jnp.dot(jnp.tanh(xwd), w2, preferred_element_type=f32)
    decay = jnp.exp(-jnp.exp(-_softplus(-xw) - 0.5))
    alpha = _sigmoid(a0 + jnp.dot(xad, a2, preferred_element_type=f32))
    g = jnp.dot(_sigmoid(xgd), g2, preferred_element_type=f32)
    kk = xk * k_k
    kkn = kk / jnp.maximum(jnp.sqrt(seg(kk * kk, bm)), 1e-12)
    k2 = xk * (1.0 + (alpha - 1.0) * k_a)
    return decay, k2, -kkn, kkn * alpha, g


def _split_cols(x):
    o1, o2, o3 = 3 * D_R, 3 * D_R + LORA_W, 3 * D_R + LORA_W + LORA_A
    return x[:, 0:D_R], x[:, D_R:2 * D_R], x[:, 2 * D_R:o1], x[:, o1:o2], x[:, o2:o3], x[:, o3:RWKV_COLS]


def _shifted(sh_ref, x, halo, first, rb):
    sh_ref[0:SUBLANES, :] = jnp.where(first, 0.0, halo)
    sh_ref[SUBLANES:SUBLANES + rb, :] = x
    return sh_ref[SUBLANES - 1:SUBLANES - 1 + rb, :]


_PREP_PARAMS = ("od_w0", "od_w2", "od_a0", "od_a2", "od_g2", "od_k_k", "od_k_a")


def _rwkv_prep(pr, mu, params, bm):
    t = pr.shape[0]
    rb = _row_block8(t)
    hb = rb // SUBLANES

    def body(pr_ref, halo_ref, mu_ref, w0, w2, a0, a2, g2, kk_ref, ka_ref, bm_ref, *outs_sh):
        outs, sh_ref = outs_sh[:-1], outs_sh[-1]
        x = pr_ref[...]
        prev = _shifted(sh_ref, x, halo_ref[...], pl.program_id(0) == 0, rb)
        xr, xk, xv, xwd, xad, xgd = _split_cols(x + (prev - x) * mu_ref[...])
        bmv = bm_ref[...]
        decay, k2, a_s, b_s, g = _prep_fn(xr, xk, xwd, xad, xgd, w0[...], w2[...], a0[...], a2[...], g2[...],
                                          kk_ref[...], ka_ref[...], bmv)
        vals = (xr, xv, decay, k2, a_s, b_s, decay * xr, _seg(b_s * xr, bmv), _seg(k2 * xr, bmv), g)
        for ref, val in zip(outs, vals):
            ref[...] = val

    row = pl.BlockSpec((rb, RWKV_COLS), lambda i: (i, 0))
    halo = pl.BlockSpec((SUBLANES, RWKV_COLS), lambda i: (jnp.maximum(i * hb - 1, 0), 0))
    orow = pl.BlockSpec((rb, D_R), lambda i: (i, 0))
    return _pc(body, name="rwkv_prep", grid=(t // rb,),
               in_specs=[row, halo, _full((1, RWKV_COLS))] + [_full(p.shape) for p in params] + [_full(bm.shape)],
               out_specs=(orow,) * 10, out_shape=(S((t, D_R), f32),) * 10,
               scratch_shapes=[pltpu.VMEM((rb + SUBLANES, RWKV_COLS), f32)],
               compiler_params=_cparams(("arbitrary",)))(pr, pr, mu, *params, bm)


def _rwkv_prep_bwd(pr, mu, params, bm, cts):
    t = pr.shape[0]
    rb = _row_block8(t)
    hb = rb // SUBLANES
    counts = [len(c) for c in cts]
    flat = [a for c in cts for a in c]

    def body(pr_ref, halo_ref, mu_ref, w0, w2, a0, a2, g2, kk_ref, ka_ref, bm_ref, *rest):
        ct_refs, rest = rest[:len(flat)], rest[len(flat):]
        dx_ref, dmu_ref = rest[0], rest[1]
        dpar_refs, sh_ref = rest[2:9], rest[9]

        @pl.when(pl.program_id(0) == 0)
        def _():
            dmu_ref[...] = jnp.zeros_like(dmu_ref)
            for r in dpar_refs:
                r[...] = jnp.zeros_like(r)
        sums, pos = [], 0
        for c in counts:
            sums.append(sum(r[...] for r in ct_refs[pos:pos + c]))
            pos += c
        x = pr_ref[...]
        prev = _shifted(sh_ref, x, halo_ref[...], pl.program_id(0) == 0, rb)
        xr, xk, xv, xwd, xad, xgd = _split_cols(x + (prev - x) * mu_ref[...])
        bmv = bm_ref[...]
        _, vjp = jax.vjp(lambda *a: _prep_fn(*a, bmv, _seg_linear), xr, xk, xwd, xad, xgd, w0[...], w2[...], a0[...], a2[...],
                         g2[...], kk_ref[...], ka_ref[...])
        grads = vjp(tuple(sums[:5]))
        dxr, dxk, dxwd, dxad, dxgd = grads[:5]
        o1, o2, o3 = 3 * D_R, 3 * D_R + LORA_W, 3 * D_R + LORA_W + LORA_A
        dx_ref[:, 0:D_R] = dxr + sums[5]
        dx_ref[:, D_R:2 * D_R] = dxk
        dx_ref[:, 2 * D_R:o1] = sums[6]
        dx_ref[:, o1:o2] = dxwd
        dx_ref[:, o2:o3] = dxad
        dx_ref[:, o3:RWKV_COLS] = dxgd
        dmu_ref[...] += jnp.sum(dx_ref[...] * (prev - x), axis=0, keepdims=True)
        for r, gval in zip(dpar_refs, grads[5:]):
            r[...] += gval

    row = pl.BlockSpec((rb, RWKV_COLS), lambda i: (i, 0))
    halo = pl.BlockSpec((SUBLANES, RWKV_COLS), lambda i: (jnp.maximum(i * hb - 1, 0), 0))
    crow = pl.BlockSpec((rb, D_R), lambda i: (i, 0))
    return _pc(body, name="rwkv_prep_bwd", grid=(t // rb,),
               in_specs=[row, halo, _full((1, RWKV_COLS))] + [_full(p.shape) for p in params] + [_full(bm.shape)]
               + [crow] * len(flat),
               out_specs=(row, _full((1, RWKV_COLS))) + tuple(_full(p.shape) for p in params),
               out_shape=(S((t, RWKV_COLS), f32), S((1, RWKV_COLS), f32)) + tuple(S(p.shape, f32) for p in params),
               scratch_shapes=[pltpu.VMEM((rb + SUBLANES, RWKV_COLS), f32)],
               compiler_params=_cparams(("arbitrary",)))(pr, pr, mu, *params, bm, *flat)


def _shift_bwd(dxs, mu):
    t = dxs.shape[0]
    rb = _row_block(t)
    hb = rb // SUBLANES
    nblk = t // rb

    def body(dx_ref, halo_ref, mu_ref, o_ref, sh_ref):
        dx = dx_ref[...]
        sh_ref[0:rb, :] = dx
        sh_ref[rb:rb + SUBLANES, :] = jnp.where(pl.program_id(0) == nblk - 1, 0.0, halo_ref[...])
        m = mu_ref[...]
        o_ref[...] = (dx * (1.0 - m) + sh_ref[1:1 + rb, :] * m).astype(bf16)

    row = pl.BlockSpec((rb, RWKV_COLS), lambda i: (i, 0))
    halo = pl.BlockSpec((SUBLANES, RWKV_COLS), lambda i: (jnp.minimum((i + 1) * hb, t // SUBLANES - 1), 0))
    return _pc(body, name="rwkv_shift_bwd", grid=(nblk,), in_specs=[row, halo, _full((1, RWKV_COLS))],
               out_specs=row, out_shape=S((t, RWKV_COLS), bf16),
               scratch_shapes=[pltpu.VMEM((rb + SUBLANES, RWKV_COLS), f32)],
               compiler_params=_cparams(("arbitrary",)))(dxs, dxs, mu)


def _post_fn(y, xr, k2, xv, g, lg, lb, rk, bm, seg=_seg):
    inv_n = 1.0 / HEAD_DIM
    yc = y - seg(y, bm) * inv_n
    var = seg(yc * yc, bm) * inv_n
    yn = yc * lax.rsqrt(var + RWKV_GN_EPS) * lg + lb
    return (yn + seg(xr * k2 * rk, bm) * xv) * g


def _rwkv_post(y, xr, k2, xv, g, lg, lb, rk, bm):
    t = y.shape[0]
    rb = _row_block8(t)

    def body(y_ref, xr_ref, k2_ref, xv_ref, g_ref, lg_ref, lb_ref, rk_ref, bm_ref, o_ref):
        o_ref[...] = _post_fn(y_ref[...], xr_ref[...], k2_ref[...], xv_ref[...], g_ref[...], lg_ref[...], lb_ref[...],
                              rk_ref[...], bm_ref[...])

    row = pl.BlockSpec((rb, D_R), lambda i: (i, 0))
    vec = _full((1, D_R))
    return _pc(body, name="rwkv_post", grid=(t // rb,), in_specs=[row] * 5 + [vec] * 3 + [_full(bm.shape)],
               out_specs=row, out_shape=S((t, D_R), f32),
               compiler_params=_cparams(("arbitrary",)))(y, xr, k2, xv, g, lg, lb, rk, bm)


def _rwkv_post_bwd(dy1, y, xr, k2, xv, g, lg, lb, rk, bm):
    t = y.shape[0]
    rb = _row_block8(t)

    def body(dy_ref, y_ref, xr_ref, k2_ref, xv_ref, g_ref, lg_ref, lb_ref, rk_ref, bm_ref, *outs):
        @pl.when(pl.program_id(0) == 0)
        def _():
            for r in outs[5:]:
                r[...] = jnp.zeros_like(r)
        bmv = bm_ref[...]
        _, vjp = jax.vjp(lambda *a: _post_fn(*a, bmv, _seg_linear), y_ref[...], xr_ref[...], k2_ref[...], xv_ref[...], g_ref[...],
                         lg_ref[...], lb_ref[...], rk_ref[...])
        grads = vjp(dy_ref[...])
        for r, gval in zip(outs[:5], grads[:5]):
            r[...] = gval
        for r, gval in zip(outs[5:], grads[5:]):
            r[...] += gval

    row = pl.BlockSpec((rb, D_R), lambda i: (i, 0))
    vec = _full((1, D_R))
    return _pc(body, name="rwkv_post_bwd", grid=(t // rb,),
               in_specs=[pl.BlockSpec((rb, D_R), lambda i: (i, 1))] + [row] * 5 + [vec] * 3 + [_full(bm.shape)],
               out_specs=(row,) * 5 + (vec,) * 3, out_shape=(S((t, D_R), f32),) * 5 + (S((1, D_R), f32),) * 3,
               compiler_params=_cparams(("arbitrary",)))(dy1, y, xr, k2, xv, g, lg, lb, rk, bm)


def _seg2(x, bb):
    hi = x.astype(bf16)
    lo = (x - hi.astype(f32)).astype(bf16)
    return jnp.dot(jnp.concatenate([hi, lo], axis=1), bb, preferred_element_type=f32)


def _row4(rows, j):
    return jnp.concatenate([jnp.broadcast_to(rows[j:j + 1, p * LANES:(p + 1) * LANES], (HEAD_DIM, LANES))
                            for p in range(4)], axis=0)


def _scan_consts():
    lane_group = jnp.arange(LANES) // HEAD_DIM
    b128 = (lane_group[:, None] == lane_group[None, :]).astype(bf16)
    bb = jnp.concatenate([b128, b128], axis=0)
    qsel = (jnp.arange(PAIR_ROWS)[:, None] % HEAD_DIM == jnp.arange(LANES)[None, :] % HEAD_DIM).astype(f32)
    return bb, qsel


def _store_cols(acc_ref, o_ref, tc):
    for p in range(4):
        blk = acc_ref[p * HEAD_DIM:(p + 1) * HEAD_DIM, :].T
        o_ref[:, (2 * p) * HEAD_DIM:(2 * p + 1) * HEAD_DIM] = blk[0:tc]
        o_ref[:, (2 * p + 1) * HEAD_DIM:(2 * p + 2) * HEAD_DIM] = blk[HEAD_DIM:HEAD_DIM + tc]


PAIR_GROUP = 2 * SUBLANES


def _rwkv_pairs(w, a, b, k, v, wr, br, kr, bm):
    t = w.shape[0]
    rb = _row_block8(t)

    def body(w_ref, a_ref, b_ref, k_ref, v_ref, wr_ref, br_ref, kr_ref, bm_ref, *outs_sh):
        outs, sh_ref = outs_sh[:-1], outs_sh[-1]

        def second(ref):
            sh_ref[0:rb, :] = ref[...]
            sh_ref[rb:rb + SUBLANES, :] = jnp.zeros((SUBLANES, D_R), f32)
            return sh_ref[1:1 + rb, :]

        w1, a1, b1, k1, v1 = w_ref[...], a_ref[...], b_ref[...], k_ref[...], v_ref[...]
        w2, a2, wr2, br2, kr2, v2 = (second(r) for r in (w_ref, a_ref, wr_ref, br_ref, kr_ref, v_ref))
        bmv = bm_ref[...]
        beta, kappa = _seg(b1 * a2, bmv), _seg(k1 * a2, bmv)
        bwr2, kwr2 = _seg(b1 * wr2, bmv), _seg(k1 * wr2, bmv)
        w1a2 = w1 * a2
        vals = (w1a2 + a1 * beta, v1 * kappa,
                wr_ref[...] + a1 * br_ref[...], v1 * kr_ref[...],
                w1 * wr2 + a1 * (bwr2 + beta * br2) + w1a2 * br2,
                v1 * (kwr2 + kappa * br2) + v2 * kr2,
                w1 * w2, b1 * w2, k1 * w2)
        for ref, val in zip(outs, vals):
            ref[...] = val

    row = pl.BlockSpec((rb, D_R), lambda i: (i, 0))
    return _pc(body, name="rwkv_pairs", grid=(t // rb,), in_specs=[row] * 8 + [_full(bm.shape)],
               out_specs=(row,) * 9, out_shape=(S((t, D_R), f32),) * 9,
               scratch_shapes=[pltpu.VMEM((rb + SUBLANES, D_R), f32)],
               compiler_params=_cparams(("arbitrary",)))(w, a, b, k, v, wr, br, kr, bm)


def _wkv_fwd(k, v, a, b, pairs):
    t = k.shape[0]
    tc = SCAN_CHUNK
    bb, qsel = _scan_consts()

    def body(*refs):
        k16, v16, a16, b16 = refs[0:4]
        ca2p, da2p, c1p, d1p, c2p, d2p, w12p, b1wp, k1wp = refs[4:13]
        bb_ref, q_ref, y_ref, st_ref, sa_ref, vb_ref, s_scr, yacc = refs[13:]

        @pl.when(pl.program_id(0) == 0)
        def _():
            s_scr[...] = jnp.zeros_like(s_scr)
        bbv, qp = bb_ref[...], q_ref[0:HEAD_DIM, :]
        lane = lax.broadcasted_iota(jnp.int32, (HEAD_DIM, LANES), 1) % HEAD_DIM

        def halves(x):
            hi = x.astype(bf16)
            return jnp.concatenate([hi, (x - hi.astype(f32)).astype(bf16)], axis=1)

        def group(gi, s):
            base = pl.multiple_of(gi * PAIR_GROUP, PAIR_GROUP)

            def rows8(ref, j):
                return ref[pl.ds(base + (j // SUBLANES) * SUBLANES, SUBLANES), :]

            def bcast(rows, j, p):
                return jnp.broadcast_to(rows[j % SUBLANES:j % SUBLANES + 1, p * LANES:(p + 1) * LANES], (HEAD_DIM, LANES))

            step = lambda ref, j, p: bcast(rows8(ref, j), j, p)
            for q in range(SUBLANES):
                j1, j2 = 2 * q, 2 * q + 1
                t1 = base + j1
                nxt = []
                for p in range(4):
                    sl = slice(p * HEAD_DIM, (p + 1) * HEAD_DIM)
                    sp = s[sl]
                    lhs = [halves(jnp.concatenate([sp * step(a16, j1, p),
                                                   sp * step(ca2p, j1, p) + qp * step(da2p, j1, p),
                                                   sp * step(c1p, j1, p) + qp * step(d1p, j1, p),
                                                   sp * step(c2p, j1, p) + qp * step(d2p, j1, p)], axis=0))]
                    for j in (j1, j2):
                        v8 = rows8(v16, j)
                        vh8 = v8.astype(bf16).astype(f32)
                        lhs.append(jnp.concatenate([(qp * bcast(vh8, j, p)).astype(bf16),
                                                    (qp * bcast(v8 - vh8, j, p)).astype(bf16)], axis=1))
                    r = jnp.dot(jnp.concatenate(lhs, axis=0), bbv, preferred_element_type=f32)
                    sa1, sa2, y1, y2, vb1, vb2 = (r[n * HEAD_DIM:(n + 1) * HEAD_DIM] for n in range(6))
                    yacc[sl, :] = jnp.where(lane == t1, y1, jnp.where(lane == t1 + 1, y2, yacc[sl, :]))
                    st_ref[base // 2 + q, sl, :] = sp
                    sa_ref[t1, sl, :] = sa1
                    sa_ref[t1 + 1, sl, :] = sa2
                    vb_ref[t1, sl, :] = vb1
                    vb_ref[t1 + 1, sl, :] = vb2
                    nxt.append(((sp * step(w12p, j1, p) + sa1 * step(b1wp, j1, p)) + vb1 * step(k1wp, j1, p))
                               + (sa2 * step(b16, j2, p) + vb2 * step(k16, j2, p)))
                s = jnp.concatenate(nxt, axis=0)
            return s

        s_scr[...] = lax.fori_loop(0, tc // PAIR_GROUP, group, s_scr[...])
        _store_cols(yacc, y_ref, tc)

    row = pl.BlockSpec((tc, D_R), lambda c: (c, 0))
    tiles = pl.BlockSpec((tc, PAIR_ROWS, LANES), lambda c: (c, 0, 0))
    return _pc(body, name="wkv_fwd", grid=(t // tc,),
               in_specs=[row] * 13 + [_full(bb.shape), _full(qsel.shape)],
               out_specs=(row, pl.BlockSpec((tc // 2, PAIR_ROWS, LANES), lambda c: (c, 0, 0)), tiles, tiles),
               out_shape=(S((t, D_R), f32), S((t // 2, PAIR_ROWS, LANES), f32)) + (S((t, PAIR_ROWS, LANES), f32),) * 2,
               scratch_shapes=[pltpu.VMEM((PAIR_ROWS, LANES), f32), pltpu.VMEM((PAIR_ROWS, LANES), f32)],
               compiler_params=_cparams(("arbitrary",)))(k, v, a, b, *pairs, bb, qsel)


def _wkv_bwd(sprev, sab, vbb, w, k, a, b, r, dy):
    t = w.shape[0]
    tc = SCAN_CHUNK
    nc = t // tc
    bb, qsel = _scan_consts()

    def body(st_ref, sa_ref, vb_ref, w_ref, k_ref, a_ref, b_ref, r_ref, dy_ref, bb_ref, q_ref,
             dr_ref, dw_ref, dk_ref, dv_ref, da_ref, db_ref, g_scr, dvacc, rows_scr):
        @pl.when(pl.program_id(0) == 0)
        def _():
            g_scr[...] = jnp.zeros_like(g_scr)
        bbv, qv = bb_ref[...], q_ref[...]
        lane64 = lax.broadcasted_iota(jnp.int32, (PAIR_ROWS, LANES), 1) % HEAD_DIM
        outs = (dr_ref, dw_ref, db_ref, dk_ref, da_ref)

        def colsums(slot, j, x):
            for p in range(4):
                rows_scr[slot, j:j + 1, p * LANES:(p + 1) * LANES] = jnp.sum(x[p * HEAD_DIM:(p + 1) * HEAD_DIM], axis=0,
                                                                           keepdims=True)

        def group(i, g):
            base = pl.multiple_of((tc // SUBLANES - 1 - i) * SUBLANES, SUBLANES)
            w8, k8, a8, b8, r8, dy8 = (ref[pl.ds(base, SUBLANES), :] for ref in (w_ref, k_ref, a_ref, b_ref, r_ref, dy_ref))

            def after_step(j, sp):
                return sp * _row4(w8, j) + sa_ref[base + j] * _row4(b8, j) + vb_ref[base + j] * _row4(k8, j)

            def back_step(j, sp, s_t, g):
                tt = base + j
                u, vb = sa_ref[tt], vb_ref[tt]
                a4, b4, w4, k4 = _row4(a8, j), _row4(b8, j), _row4(w8, j), _row4(k8, j)
                dyb = _seg2(qv * _row4(dy8, j), bbv)
                g = g + dyb * _row4(r8, j)
                rr2 = _seg2(jnp.concatenate([g * b4, g * k4], axis=0), bbv)
                du, dvb = rr2[0:PAIR_ROWS], rr2[PAIR_ROWS:2 * PAIR_ROWS]
                for slot, val in enumerate((s_t * dyb, g * sp, g * u, g * vb, sp * du)):
                    colsums(slot, j, val)
                dvacc[...] = jnp.where(lane64 == tt, dvb, dvacc[...])
                return g * w4 + du * a4

            for q in reversed(range(SUBLANES // 2)):
                s0 = st_ref[base // 2 + q]
                s1 = after_step(2 * q, s0)
                g = back_step(2 * q + 1, s1, after_step(2 * q + 1, s1), g)
                g = back_step(2 * q, s0, s1, g)
            for slot, ref in enumerate(outs):
                ref[pl.ds(base, SUBLANES), :] = rows_scr[slot]
            return g

        g_scr[...] = lax.fori_loop(0, tc // SUBLANES, group, g_scr[...])
        _store_cols(dvacc, dv_ref, tc)

    row = pl.BlockSpec((tc, D_R), lambda c: (nc - 1 - c, 0))
    tiles = pl.BlockSpec((tc, PAIR_ROWS, LANES), lambda c: (nc - 1 - c, 0, 0))
    states = pl.BlockSpec((tc // 2, PAIR_ROWS, LANES), lambda c: (nc - 1 - c, 0, 0))
    return _pc(body, name="wkv_bwd", grid=(nc,),
               in_specs=[states, tiles, tiles] + [row] * 6 + [_full(bb.shape), _full(qsel.shape)],
               out_specs=(row,) * 6, out_shape=(S((t, D_R), f32),) * 6,
               scratch_shapes=[pltpu.VMEM((PAIR_ROWS, LANES), f32), pltpu.VMEM((PAIR_ROWS, LANES), f32),
                               pltpu.VMEM((5, SUBLANES, D_R), f32)],
               compiler_params=_cparams(("arbitrary",)))(sprev, sab, vbb, w, k, a, b, r, dy, bb, qsel)


def _rope_tables(t):
    half = HEAD_DIM // 2
    inv = ROPE_THETA ** (-jnp.arange(half, dtype=f32) / half)
    ang = jnp.arange(t, dtype=f32)[:, None] * inv[None, :]
    cos, sin = jnp.cos(ang), jnp.sin(ang)
    return jnp.concatenate([cos, cos], axis=1), jnp.concatenate([-sin, sin], axis=1)


def _head_matrix():
    grp = jnp.arange(D_R) // HEAD_DIM
    b = (grp[:, None] == grp[None, :]).astype(bf16)
    return jnp.concatenate([b, b], axis=0)


def _ffn_fwd(h, g, get_w, conv_w, conv_b, i):
    hf = _rms_fwd(h, g, f"ffn{i}_norm")
    w_up_t = get_w(f"ff{i}_up", hf)
    u = _mm(hf, w_up_t, "nt", f"ffn{i}_up")
    z = _ffn_mid(u, conv_w, conv_b, f"ffn{i}_mid")
    w_down = get_w(f"ff{i}_down", z)
    return _mm(z, w_down, "nn", f"ffn{i}_down", res=h), (hf, u, z), w_up_t, w_down


def _ffn_bwd(dh, h, saved, g, w_up_t, conv_w, conv_b, w_down, i, put_g):
    hf, u, z = saved
    dz = _mm(dh, w_down, "nt", f"ffn{i}_dz")
    g_down = _mm(z, dh, "tn", f"ffn{i}_gdown", out_dtype=GRAD_WIRE_DTYPE)
    tok = put_g(f"ff{i}_down", g_down)
    dgate, dval, g_conv, g_convb = _ffn_mid_bwd(dz, u, conv_w, conv_b + tok, f"ffn{i}_mid_bwd")
    g_up_t = _mm(dgate, hf, "tn", f"ffn{i}_gup_gate", out_dtype=GRAD_WIRE_DTYPE, out_rows=2 * D_FF)
    g_up_t = _mm(dval, hf, "tn", f"ffn{i}_gup_val", out_dtype=GRAD_WIRE_DTYPE, out_rows=2 * D_FF, out_row0=D_FF, into=g_up_t)
    tok = put_g(f"ff{i}_up", g_up_t)
    dhf = _mm(dval, w_up_t, "nn", f"ffn{i}_dhf_val", b_row0=D_FF, res=_mm(dgate, w_up_t, "nn", f"ffn{i}_dhf_gate"))
    dh_in, g_norm = _rms_bwd(dhf, h, g + tok, dh, f"ffn{i}_norm_bwd")
    return dh_in, dict(conv=g_conv, conv_b=g_convb, norm=g_norm)


def _local_step(x, target, W, get_w, put_g, put_small, tok0):
    t = N_META + x.shape[0]
    c64, s64 = _rope_tables(t)
    bm = _head_matrix()
    h0 = jnp.concatenate([W["meta_tokens"], x], axis=0)

    ev_w_in_t, ev_w_out = get_w("ev_in", None), get_w("ev_out", None)
    hn0 = _rms_fwd(h0, W["norm_mix"][0] + tok0, "mix0_norm")
    p0 = _mm(hn0, ev_w_in_t, "nt", "ev_in")
    uc = _ev_a_conv(p0, W["ev_conv_a"])
    y0 = _ev_b(p0, W["ev_conv_b"], _ev_a_norm(uc, W["ev_ln_a_g"], W["ev_ln_a_b"]))
    h1 = _mm(y0, ev_w_out, "nn", "ev_out", res=h0)
    h2, ffn0, ff0_up_t, ff0_down = _ffn_fwd(h1, W["norm_ffn"][0], get_w, W["ff_conv"][0], W["ff_conv_b"][0], 0)

    hn1 = _rms_fwd(h2, W["norm_mix"][1], "mix1_norm")
    od_w_in_t = get_w("od_in", hn1)
    w_att, w_rwkv = od_w_in_t[:ATT_COLS], od_w_in_t[ATT_COLS:]
    pr = _mm(hn1, w_rwkv, "nt", "od_in_rwkv")
    qp, kp, vp = _rope_pack(_mm(hn1, w_att, "nt", "od_in_att"), c64, s64)
    op = _attn_fwd(qp, kp, vp, W["od_sinks"])
    prep_params = [W[n] for n in _PREP_PARAMS]
    xr, xv, decay, k2, a_s, b_s, wr, br, kr, gate = _rwkv_prep(pr, W["od_mu"], prep_params, bm)
    pairs = _rwkv_pairs(decay, a_s, b_s, k2, xv, wr, br, kr, bm)
    ysc, sprev, sab, vbb = _wkv_fwd(k2, xv, a_s, b_s, pairs)
    rk = W["od_r_k"].reshape(1, D_R)
    yr = _rwkv_post(ysc, xr, k2, xv, gate, W["od_lnx_g"], W["od_lnx_b"], rk, bm)
    y1 = jnp.concatenate([op[ATT_PAD:], yr.astype(bf16)], axis=1)
    od_w_out = get_w("od_out", y1)
    h3 = _mm(y1, od_w_out, "nn", "od_out", res=h2)
    h4, ffn1, ff1_up_t, ff1_down = _ffn_fwd(h3, W["norm_ffn"][1], get_w, W["ff_conv"][1], W["ff_conv_b"][1], 1)

    tgt = jnp.concatenate([jnp.zeros((N_META, D_MODEL), f32), target], axis=0)
    loss, dh4, g_norm_final = _final_loss(h4, W["norm_final"], tgt)

    dh3, gf1 = _ffn_bwd(dh4, h3, ffn1, W["norm_ffn"][1], ff1_up_t, W["ff_conv"][1], W["ff_conv_b"][1], ff1_down, 1, put_g)
    dy1 = _mm(dh3, od_w_out, "nt", "od_dy")
    g_od_w_out = _mm(y1, dh3, "tn", "od_gout", out_dtype=GRAD_WIRE_DTYPE)
    tok = put_g("od_out", g_od_w_out)
    dysc, dxr_p, dk2_p, dxv_p, dgate, g_lnx_g, g_lnx_b, g_rk = _rwkv_post_bwd(
        dy1, ysc, xr, k2, xv, gate, W["od_lnx_g"], W["od_lnx_b"] + tok, rk, bm)
    dr, dw, dk, dv, da, db = _wkv_bwd(sprev, sab, vbb, decay, k2, a_s, b_s, xr, dysc)
    prep_grads = _rwkv_prep_bwd(pr, W["od_mu"], prep_params, bm,
                                [[dw], [dk, dk2_p], [da], [db], [dgate], [dr, dxr_p], [dv, dxv_p]])
    dxs, g_mu = prep_grads[0], prep_grads[1]
    dpr = _shift_bwd(dxs, W["od_mu"])
    dop = jnp.concatenate([jnp.zeros((ATT_PAD, D_ATT), f32), dy1[:, :D_ATT]], axis=0).astype(bf16)
    dqp, dkp, dvp, dsk = _attn_bwd(qp, kp, vp, W["od_sinks"], dop)
    dpatt = _rope_bwd(dqp, dkp, dvp, c64, s64)
    n_in = ATT_COLS + RWKV_COLS
    g_od_w_in_t = _mm(dpatt, hn1, "tn", "od_gin_att", out_dtype=GRAD_WIRE_DTYPE, out_rows=n_in)
    g_od_w_in_t = _mm(dpr, hn1, "tn", "od_gin_rwkv", out_dtype=GRAD_WIRE_DTYPE, out_rows=n_in, out_row0=ATT_COLS, into=g_od_w_in_t)
    tok = put_g("od_in", g_od_w_in_t)
    dhn1 = _mm(dpr, w_rwkv, "nn", "od_dhn_rwkv", res=_mm(dpatt, w_att, "nn", "od_dhn_att"))
    dh2, g_norm_mix1 = _rms_bwd(dhn1, h2, W["norm_mix"][1] + tok, dh3, "mix1_norm_bwd")

    dh1, gf0 = _ffn_bwd(dh2, h1, ffn0, W["norm_ffn"][0], ff0_up_t, W["ff_conv"][0], W["ff_conv_b"][0], ff0_down, 0, put_g)
    early = dict(
        norm_ffn=jnp.concatenate([gf0["norm"], gf1["norm"]], axis=0), norm_final=g_norm_final.reshape(D_MODEL),
        od_sinks=dsk[:, :N_Q_HEADS], od_mu=g_mu, od_lnx_g=g_lnx_g, od_lnx_b=g_lnx_b, od_r_k=g_rk.reshape(N_Q_HEADS, HEAD_DIM),
        ff_conv=jnp.stack([gf0["conv"], gf1["conv"]]), ff_conv_b=jnp.concatenate([gf0["conv_b"], gf1["conv_b"]], axis=0),
        **dict(zip(_PREP_PARAMS, prep_grads[2:])))
    dy0 = _mm(dh1, ev_w_out, "nt", "ev_dy")
    g_ev_w_out = _mm(y0, dh1, "tn", "ev_gout", out_dtype=GRAD_WIRE_DTYPE)
    tok = put_g("ev_out", g_ev_w_out) + put_small(early)
    duc, g_ln_g, g_ln_b = _ev_a_norm_bwd(dy0, uc, W["ev_ln_a_g"], W["ev_ln_a_b"] + tok)
    dav, dag, g_conv_a = _ev_a_conv_bwd(duc, p0, W["ev_conv_a"])
    dgb, dgc, dxi, g_conv_b = _ev_b_bwd(dy0, p0, W["ev_conv_b"])
    dp0 = jnp.concatenate([dav, dag, dgb, dgc, dxi], axis=1)
    g_ev_w_in_t = _mm(dp0, hn0, "tn", "ev_gin", out_dtype=GRAD_WIRE_DTYPE)
    tok = put_g("ev_in", g_ev_w_in_t)
    dhn0 = _mm(dp0, ev_w_in_t, "nn", "ev_dhn")
    dh0, g_norm_mix0 = _rms_bwd(dhn0, h0, W["norm_mix"][0] + tok, dh1, "mix0_norm_bwd")

    late = dict(meta_tokens=dh0[:N_META], norm_mix=jnp.concatenate([g_norm_mix0, g_norm_mix1], axis=0),
                ev_conv_a=g_conv_a, ev_ln_a_g=g_ln_g, ev_ln_a_b=g_ln_b, ev_conv_b=g_conv_b)
    return loss, dh0[N_META:], late


HBM = pl.BlockSpec(memory_space=pl.ANY)


def _mesh_pos():
    return lax.axis_index("x"), lax.axis_index("y"), lax.axis_index("c")


def _dev(px, py, pc):
    return 4 * px + 2 * py + pc


def _all_gather(xs, name):
    n = len(xs)

    def body(*refs):
        x_refs, o_refs = refs[:n], refs[n:2 * n]
        send_sems, recv_sems, local_sems = refs[2 * n:]
        x, y, c = _mesh_pos()
        me, sibling = (x, y, c), (x, y, 1 - c)
        chips = [(1 - x, y), (x, 1 - y), (1 - x, 1 - y)]

        def copy(i, k, block, to, from_input=False):
            dst = o_refs[i].at[_dev(*block)]
            return pltpu.make_async_remote_copy(src_ref=x_refs[i] if from_input else dst, dst_ref=dst,
                                                send_sem=send_sems.at[i, k], recv_sem=recv_sems.at[i, k],
                                                device_id=to, device_id_type=MESH)

        mine = [pltpu.make_async_copy(x_refs[i], o_refs[i].at[_dev(*me)], local_sems.at[i]) for i in range(n)]
        for cp in mine:
            cp.start()
        first = []
        for i in range(n):
            first.append(copy(i, 0, me, sibling, True))
            first += [copy(i, 1 + j, me, (*chip, c), True) for j, chip in enumerate(chips)]
        for cp in first:
            cp.start()
        passed = []
        for j, chip in enumerate(chips):
            for i in range(n):
                copy(i, 1 + j, (*chip, c), me).wait_recv()
                fwd = copy(i, 4 + j, (*chip, c), sibling)
                fwd.start()
                passed.append(fwd)
        for i in range(n):
            copy(i, 0, sibling, me).wait_recv()
            for j, chip in enumerate(chips):
                copy(i, 4 + j, (*chip, 1 - c), me).wait_recv()
        for cp in first + passed:
            cp.wait_send()
        for cp in mine:
            cp.wait()

    return _pc(body, name=name, in_specs=[HBM] * n, out_specs=tuple([HBM] * n),
               out_shape=tuple(S((N_DEV,) + x.shape, x.dtype) for x in xs),
               scratch_shapes=[pltpu.SemaphoreType.DMA((n, 7)), pltpu.SemaphoreType.DMA((n, 7)),
                               pltpu.SemaphoreType.DMA((n,))])(*xs)


HBM_SPEC = pl.BlockSpec(memory_space=pltpu.HBM)
SEM_SPEC = pl.BlockSpec(memory_space=pltpu.SEMAPHORE)
DATAFLOW = pltpu.SideEffectType.DATAFLOW_SIDE_EFFECTING
_PEER_FLIPS = ((1, 0, 0), (0, 1, 0), (1, 1, 0), (1, 0, 1), (0, 1, 1), (1, 1, 1), (0, 0, 1))
N_PEERS = len(_PEER_FLIPS)


def _peers(x, y, c):
    return [((1 - x) if fx else x, (1 - y) if fy else y, (1 - c) if fc else c) for fx, fy, fc in _PEER_FLIPS]


def _xchg_start(srcs, lands, scatter, name):
    n = len(srcs)

    def body(*refs):
        src_refs, land_refs = refs[:n], refs[n:2 * n]
        send_sems, recv_sems, token = refs[2 * n], refs[2 * n + 1], refs[-1]
        x, y, c = _mesh_pos()
        me = _dev(x, y, c)
        for i in range(n):
            for k, peer in enumerate(_peers(x, y, c)):
                pltpu.make_async_remote_copy(src_ref=src_refs[i].at[_dev(*peer)] if scatter else src_refs[i],
                                             dst_ref=land_refs[i].at[me], send_sem=send_sems.at[i * N_PEERS + k],
                                             recv_sem=recv_sems.at[i * N_PEERS + k], device_id=peer, device_id_type=MESH).start()
        token[...] = jnp.zeros_like(token)

    arrs = list(srcs) + list(lands)
    outs = _pc(body, name=name,
               out_shape=(pltpu.SemaphoreType.DMA((n * N_PEERS,)), pltpu.SemaphoreType.DMA((n * N_PEERS,)),
                          *[pltpu.HBM(a.shape, a.dtype) for a in arrs], S((SUBLANES, LANES), f32)),
               in_specs=[HBM_SPEC] * (2 * n),
               out_specs=(SEM_SPEC, SEM_SPEC, *[HBM_SPEC] * (2 * n), pl.BlockSpec(memory_space=pltpu.VMEM)),
               input_output_aliases={i: 2 + i for i in range(2 * n)},
               compiler_params=pltpu.CompilerParams(has_side_effects=DATAFLOW))(
        *[pltpu.with_memory_space_constraint(a, pltpu.HBM) for a in arrs])
    return (outs[0], outs[1], list(outs[2:2 + n]), list(outs[2 + n:2 + 2 * n]), scatter), outs[-1]


def _xchg_wait(handle, after, name):
    send_sems, recv_sems, srcs, lands, scatter = handle
    n = len(srcs)

    def body(*refs):
        src_refs, land_refs = refs[:n], refs[n:2 * n]
        send, recv = refs[2 * n], refs[2 * n + 1]
        x, y, c = _mesh_pos()
        for i in range(n):
            for k in range(N_PEERS):
                cp = pltpu.make_async_remote_copy(src_ref=src_refs[i].at[0] if scatter else src_refs[i],
                                                  dst_ref=land_refs[i].at[0], send_sem=send.at[i * N_PEERS + k],
                                                  recv_sem=recv.at[i * N_PEERS + k],
                                                  device_id=(x, y, c), device_id_type=MESH)
                cp.wait_send()
                cp.wait_recv()

    arrs = srcs + lands
    outs = _pc(body, name=name, out_shape=tuple(pltpu.HBM(a.shape, a.dtype) for a in arrs),
               in_specs=[HBM_SPEC] * (2 * n) + [SEM_SPEC, SEM_SPEC, pl.BlockSpec(memory_space=pl.ANY)],
               out_specs=tuple([HBM_SPEC] * (2 * n)), input_output_aliases={i: i for i in range(2 * n)},
               compiler_params=pltpu.CompilerParams(has_side_effects=DATAFLOW))(*arrs, send_sems, recv_sems, after)
    return list(outs[:n]), list(outs[n:])


def _rs_sum(g, land, me_vec, name):
    _, r, cols = g.shape
    tr = _divisor_block(r, 16, min(r, 352))

    def body(me_ref, g_ref, *rest):
        o_ref = rest[-1]
        acc = g_ref[0].astype(f32)
        for l_ref in rest[:-1]:
            acc = acc + l_ref[0].astype(f32)
        o_ref[...] = acc

    blk = lambda f: pl.BlockSpec((1, tr, cols), f)
    grid_spec = pltpu.PrefetchScalarGridSpec(
        num_scalar_prefetch=1, grid=(r // tr,),
        in_specs=[blk(lambda i, me_ref: (me_ref[0], i, 0))]
        + [blk(lambda i, me_ref, k=k: ((me_ref[0] + k) % N_DEV, i, 0)) for k in range(1, N_DEV)],
        out_specs=pl.BlockSpec((tr, cols), lambda i, me_ref: (i, 0)))
    return _pc(body, name=name, grid_spec=grid_spec, out_shape=S((r, cols), f32),
               compiler_params=_cparams(("arbitrary",)))(me_vec, g, *([land] * (N_DEV - 1)))


def _sum_devices(a, name):
    def body(a_ref, o_ref):
        acc = a_ref[0]
        for d in range(1, N_DEV):
            acc = acc + a_ref[d]
        o_ref[...] = acc

    return _pc(body, name=name, grid=(1,), in_specs=[_full(a.shape)], out_specs=_full(a.shape[1:]),
               out_shape=S(a.shape[1:], a.dtype), compiler_params=_cparams(("arbitrary",)))(a)


def _adamw(w, m, v, g, name):
    shape = w.shape
    w2, m2, v2, g2 = (a.reshape(-1, shape[-1]) for a in (w, m, v, g))
    rows, cols = w2.shape
    tr = rows if rows % SUBLANES else _divisor_block(rows, SUBLANES, max(SUBLANES, min(rows, ADAMW_BLOCK_ELEMS // cols)))
    c1, c2 = 1.0 - ADAM_B1 ** ADAM_STEP, 1.0 - ADAM_B2 ** ADAM_STEP

    def body(w_ref, m_ref, v_ref, g_ref, d_ref, nm_ref, nv_ref):
        gv = g_ref[...]
        nm = ADAM_B1 * m_ref[...] + (1.0 - ADAM_B1) * gv
        nv = ADAM_B2 * v_ref[...] + (1.0 - ADAM_B2) * (gv * gv)
        d_ref[...] = -ADAM_LR * ((nm / c1) / (jnp.sqrt(nv / c2) + ADAM_EPS) + ADAM_WD * w_ref[...])
        nm_ref[...] = nm
        nv_ref[...] = nv

    blk = pl.BlockSpec((tr, cols), lambda i: (i, 0))
    outs = _pc(body, name=name, grid=(rows // tr,), in_specs=[blk] * 4, out_specs=(blk,) * 3,
               out_shape=(S((rows, cols), f32),) * 3, compiler_params=_cparams(("arbitrary",)))(w2, m2, v2, g2)
    return tuple(o.reshape(shape) for o in outs)


_WEIGHTS = ("meta_tokens", "norm_mix", "norm_ffn", "norm_final", "ev_w_in", "ev_conv_a", "ev_ln_a_g", "ev_ln_a_b",
            "ev_conv_b", "ev_w_out", "od_w_in", "od_sinks", "od_mu", "od_w0", "od_w2", "od_a0", "od_a2", "od_g2",
            "od_k_k", "od_k_a", "od_r_k", "od_lnx_g", "od_lnx_b", "od_w_out", "ff_w_up", "ff_conv", "ff_conv_b", "ff_w_down")
_SMALL_SHARDED = (("meta_tokens", 1), ("ev_conv_a", 2), ("ev_conv_b", 2), ("od_mu", 1), ("od_w0", 1), ("od_w2", 2),
                  ("od_a0", 1), ("od_a2", 2), ("od_g2", 2), ("od_k_k", 1), ("od_k_a", 1), ("od_lnx_g", 1),
                  ("od_lnx_b", 1), ("ff_conv", 2))
_SMALL_REPLICATED = ("norm_mix", "norm_ffn", "norm_final", "ev_ln_a_g", "ev_ln_a_b", "od_sinks", "od_r_k", "ff_conv_b")
SLAB_UNIT = SUBLANES * LANES


def _pack(arrs):
    flat = jnp.concatenate([a.reshape(-1).astype(f32) for a in arrs])
    pad = (-flat.shape[0]) % SLAB_UNIT
    return jnp.pad(flat, (0, pad)).reshape(-1, LANES)


def _unpack(flat, shapes):
    out, off = [], 0
    for shp in shapes:
        size = 1
        for s in shp:
            size *= s
        out.append(flat[..., off:off + size].reshape(flat.shape[:-1] + tuple(shp)))
        off += size
    return out


def _full_shape(shape, axis):
    return tuple(N_DEV * s if i == axis else s for i, s in enumerate(shape))


def kernel(x, meta_tokens, norm_mix, norm_ffn, norm_final, ev_w_in, ev_conv_a, ev_ln_a_g, ev_ln_a_b, ev_conv_b, ev_w_out, od_w_in, od_sinks, od_mu, od_w0, od_w2, od_a0, od_a2, od_g2, od_k_k, od_k_a, od_r_k, od_lnx_g, od_lnx_b, od_w_out, ff_w_up, ff_conv, ff_conv_b, ff_w_down, loss_target, m_meta_tokens, m_norm_mix, m_norm_ffn, m_norm_final, m_ev_w_in, m_ev_conv_a, m_ev_ln_a_g, m_ev_ln_a_b, m_ev_conv_b, m_ev_w_out, m_od_w_in, m_od_sinks, m_od_mu, m_od_w0, m_od_w2, m_od_a0, m_od_a2, m_od_g2, m_od_k_k, m_od_k_a, m_od_r_k, m_od_lnx_g, m_od_lnx_b, m_od_w_out, m_ff_w_up, m_ff_conv, m_ff_conv_b, m_ff_w_down, v_meta_tokens, v_norm_mix, v_norm_ffn, v_norm_final, v_ev_w_in, v_ev_conv_a, v_ev_ln_a_g, v_ev_ln_a_b, v_ev_conv_b, v_ev_w_out, v_od_w_in, v_od_sinks, v_od_mu, v_od_w0, v_od_w2, v_od_a0, v_od_a2, v_od_g2, v_od_k_k, v_od_k_a, v_od_r_k, v_od_lnx_g, v_od_lnx_b, v_od_w_out, v_ff_w_up, v_ff_conv, v_ff_conv_b, v_ff_w_down):
    A = dict(locals())
    px, py, pc = _mesh_pos()
    me = _dev(px, py, pc)
    me_vec = jnp.reshape(me, (1,)).astype(jnp.int32)
    rows = lambda a: a.reshape(N_DEV * a.shape[1], a.shape[2])
    blocks = lambda a: a.reshape(N_DEV, a.shape[0] // N_DEV, a.shape[1])

    shards = dict(ev_in=ev_w_in[0].T, ev_out=ev_w_out[0], ff0_up=ff_w_up[0].T, ff0_down=ff_w_down[0], od_in=od_w_in[0].T,
                  od_out=od_w_out[0], ff1_up=ff_w_up[1].T, ff1_down=ff_w_down[1])
    shards = {n: b.astype(bf16) for n, b in shards.items()}
    small_shapes = [A[n].shape for n, _ in _SMALL_SHARDED]
    gathered = _all_gather([shards["ev_in"], shards["ev_out"], _pack([A[n] for n, _ in _SMALL_SHARDED])], "gather_first")
    gathered, shards = lax.optimization_barrier((gathered, shards))
    fetch, tok0 = {}, jnp.zeros((), f32)
    for n in ("ff0_up", "ff0_down", "od_in", "od_out", "ff1_up", "ff1_down"):
        shard, tok0 = lax.optimization_barrier((shards[n], tok0))
        land = lax.dynamic_update_slice(lax.empty((N_DEV,) + shard.shape, bf16), shard[None], (me, 0, 0))
        fetch[n], token = _xchg_start([shard], [land], False, f"gather_{n}_start")
        tok0 = tok0 + token[0, 0]

    def get_w(n, after):
        if n in ("ev_in", "ev_out"):
            return rows(gathered[("ev_in", "ev_out").index(n)])
        return rows(_xchg_wait(fetch[n], after, f"gather_{n}_wait")[1][0])

    W = {}
    for (n, ax), seg in zip(_SMALL_SHARDED, _unpack(gathered[-1].reshape(N_DEV, -1), small_shapes)):
        W[n] = jnp.moveaxis(seg, 0, ax).reshape(_full_shape(A[n].shape, ax))
    for n in ("ev_conv_a", "ev_conv_b", "od_w2", "od_a2", "od_g2"):
        W[n] = W[n][0]
    for n in _SMALL_REPLICATED:
        W[n] = A[n]
    W["od_r_k"] = od_r_k[0]

    small_shape = {n: _full_shape(A[n].shape, ax) for n, ax in _SMALL_SHARDED}
    small_shape.update({n: A[n].shape for n in _SMALL_REPLICATED})
    sent, small_sent, small_names = {}, {}, {}

    def put_g(n, g):
        g8 = blocks(g)
        sent[n], token = _xchg_start([g8], [lax.empty(g8.shape, g8.dtype)], True, f"reduce_{n}_start")
        return token[0, 0]

    def put_small(gs, stage="early"):
        small_names[stage] = sorted(gs)
        slab = _pack([gs[n] for n in small_names[stage]])
        land = lax.dynamic_update_slice(lax.empty((N_DEV,) + slab.shape, f32), slab[None], (me, 0, 0))
        small_sent[stage], small_tok[stage] = _xchg_start([slab], [land], False, f"gather_{stage}_small_grads_start")
        return small_tok[stage][0, 0]

    small_tok = {}
    loss_tile, grad_x, late = _local_step(x[0], loss_target[0], W, get_w, put_g, put_small, tok0)
    put_small(late, "late")
    late_tok = small_tok["late"]

    gsh, prev = {}, late_tok
    for n in ("ff1_down", "ff1_up", "od_out", "od_in", "ff0_down", "ff0_up", "ev_out", "ev_in"):
        srcs, lands = _xchg_wait(sent[n], prev, f"reduce_{n}_wait")
        gsh[n] = prev = _rs_sum(srcs[0], lands[0], me_vec, f"reduce_{n}_sum")
    grads = dict(ev_w_in=gsh["ev_in"].T[None], ev_w_out=gsh["ev_out"][None], od_w_in=gsh["od_in"].T[None],
                 od_w_out=gsh["od_out"][None], ff_w_up=jnp.stack([gsh["ff0_up"].T, gsh["ff1_up"].T]),
                 ff_w_down=jnp.stack([gsh["ff0_down"], gsh["ff1_down"]]))

    delta, new_m, new_v = {}, {}, {}
    for n in ("ff_w_up", "ff_w_down", "od_w_in", "od_w_out", "ev_w_in", "ev_w_out"):
        delta[n], new_m[n], new_v[n] = _adamw(A[n], A["m_" + n], A["v_" + n], grads[n], "adamw_" + n)
    for stage in ("early", "late"):
        gsm = _xchg_wait(small_sent[stage], delta["ev_w_in"], f"gather_{stage}_small_grads_wait")[1][0]
        summed = _sum_devices(gsm, f"sum_{stage}_small_grads").reshape(-1)
        for n, full in zip(small_names[stage], _unpack(summed, [small_shape[n] for n in small_names[stage]])):
            grads[n] = full
    for n, ax in _SMALL_SHARDED:
        size = A[n].shape[ax]
        grads[n] = lax.dynamic_slice_in_dim(grads[n], me * size, size, axis=ax)
    for n in small_shape:
        delta[n], new_m[n], new_v[n] = _adamw(A[n], A["m_" + n], A["v_" + n], grads[n], "adamw_" + n)

    loss = lax.psum(loss_tile[0, 0], ("x", "y", "c"))
    return (loss, grad_x[None], *[grads[n] for n in _WEIGHTS], *[delta[n] for n in _WEIGHTS],
            *[new_m[n] for n in _WEIGHTS], *[new_v[n] for n in _WEIGHTS])
```

```python
import jax
import jax.numpy as jnp
from jax import lax
from jax.experimental import pallas as pl
from jax.experimental.pallas import tpu as pltpu

f32, bf16 = jnp.float32, jnp.bfloat16

D_MODEL = 1024
N_META = 16
RMS_EPS = 1e-6
LN_EPS = 1e-5
D_A = 512
CONV_A_WIDTH = 31
CONV_B_WIDTH = 3
HEAD_DIM = 64
N_Q_HEADS = 8
N_KV_HEADS = 2
GQA_GROUP = 4
D_ATT = 512
D_KV = 128
BLOCK = 128
ROPE_THETA = 10000.0
D_R = 512
LORA_W, LORA_A, LORA_G = 64, 64, 128
RWKV_GN_EPS = 64e-5
ATT_COLS = D_ATT + 2 * D_KV
RWKV_COLS = 3 * D_R + LORA_W + LORA_A + LORA_G
D_FF = 2816
FF_CONV_WIDTH = 3
FF_BLOCK = 256
NEG_INF = -1e30
ATT_PAD = BLOCK - N_META
ATT_SCALE = HEAD_DIM ** -0.5

ADAM_LR, ADAM_B1, ADAM_B2, ADAM_EPS, ADAM_WD, ADAM_STEP = 0.001, 0.9, 0.999, 1e-08, 0.01, 10

N_DEV = 8
LANES = 128
SUBLANES = 8
SCAN_CHUNK = 48
PAIR_ROWS = 4 * HEAD_DIM
V7X_VMEM_LIMIT = 56 * 1024 * 1024
ADAMW_BLOCK_ELEMS = 400 * 1024
GRAD_WIRE_DTYPE = bf16
MESH = pl.DeviceIdType.MESH
S = jax.ShapeDtypeStruct
HIGHEST = lax.Precision.HIGHEST


def _pc(body, **kw):
    return pl.pallas_call(body, **kw)


def _cparams(sem=None):
    return pltpu.CompilerParams(dimension_semantics=sem, vmem_limit_bytes=V7X_VMEM_LIMIT)


def _divisor_block(t, unit, limit):
    best = unit
    for rb in range(unit, limit + 1, unit):
        if t % rb == 0:
            best = rb
    assert t % best == 0, (t, unit)
    return best


def _row_block(t):
    return _divisor_block(t, 16, 704)


def _row_block8(t):
    return _divisor_block(t, 8, 344)


def _col_tile(n, cap):
    return _divisor_block(n, LANES, min(n, cap)) if n % LANES == 0 else n


def _full(shape):
    nd = len(shape)
    return pl.BlockSpec(shape, lambda *_: (0,) * nd)


def _sigmoid(x):
    return jax.nn.sigmoid(x)


_DIMS = {"nn": (((1,), (0,)), ((), ())), "nt": (((1,), (1,)), ((), ())), "tn": (((0,), (0,)), ((), ()))}
MM_MAX_K = 2816
MM_MAX_TM = 704
MM_MAX_TN = 1408


def _mm(a, b, mode, name, out_dtype=f32, res=None, b_row0=0, out_rows=None, out_row0=0, into=None):
    if mode == "nn":
        (m, k), n, k2 = a.shape, b.shape[1], a.shape[1]
        assert b_row0 % k == 0 and b_row0 + k <= b.shape[0], (a.shape, b.shape, b_row0)
    elif mode == "nt":
        (m, k), (n, k2) = a.shape, b.shape
    else:
        (k, m), (k2, n) = a.shape, b.shape
    assert k == k2, (a.shape, b.shape, mode)
    tm = _row_block(m) if m % LANES else _col_tile(m, MM_MAX_TM)
    tn = _col_tile(n, MM_MAX_TN)
    nk = 1 if (mode == "tn" or k <= MM_MAX_K) else k // MM_MAX_K
    tk = k // nk
    assert tk * nk == k
    dims = _DIMS[mode]

    def body(a_ref, b_ref, *rest):
        part = lax.dot_general(a_ref[...].astype(bf16), b_ref[...].astype(bf16), dims, preferred_element_type=f32)
        if nk == 1:
            o_ref = rest[-1]
            if res is not None:
                part = part + rest[0][...]
            o_ref[...] = part.astype(out_dtype)
            return
        o_ref, acc_ref = rest[-2], rest[-1]
        kk = pl.program_id(2)

        @pl.when(kk == 0)
        def _():
            acc_ref[...] = part

        @pl.when(kk > 0)
        def _():
            acc_ref[...] += part

        @pl.when(kk == nk - 1)
        def _():
            acc = acc_ref[...]
            if res is not None:
                acc = acc + rest[0][...]
            o_ref[...] = acc.astype(out_dtype)

    if mode == "tn":
        a_spec = pl.BlockSpec((k, tm), lambda i, j, kk: (0, i))
    else:
        a_spec = pl.BlockSpec((tm, tk), lambda i, j, kk: (i, kk))
    if mode == "nt":
        b_spec = pl.BlockSpec((tn, tk), lambda i, j, kk: (j, kk))
    else:
        b_spec = pl.BlockSpec((tk, tn), lambda i, j, kk: (kk + b_row0 // tk, j))
    assert out_row0 % tm == 0 and res is None or out_row0 == 0
    o_spec = pl.BlockSpec((tm, tn), lambda i, j, kk: (i + out_row0 // tm, j))
    ins, specs, aliases = [a, b], [a_spec, b_spec], {}
    if res is not None:
        ins.append(res)
        specs.append(o_spec)
    if into is not None:
        assert into.shape == (out_rows, n) and into.dtype == out_dtype
        aliases = {len(ins): 0}
        ins.append(into)
        specs.append(pl.BlockSpec(memory_space=pl.ANY))
    scratch = [pltpu.VMEM((tm, tn), f32)] if nk > 1 else []
    return _pc(body, name=name, grid=(m // tm, n // tn, nk), in_specs=specs, out_specs=o_spec,
               out_shape=S((out_rows or m, n), out_dtype), scratch_shapes=scratch, input_output_aliases=aliases,
               compiler_params=_cparams(("arbitrary", "arbitrary", "arbitrary")))(*ins)


def _rms_fwd(x, g, name):
    t, d = x.shape
    rb = _row_block(t)

    def body(x_ref, g_ref, o_ref):
        xv = x_ref[...]
        rstd = lax.rsqrt(jnp.mean(xv * xv, axis=-1, keepdims=True) + RMS_EPS)
        o_ref[...] = (xv * rstd * g_ref[...]).astype(bf16)

    row = pl.BlockSpec((rb, d), lambda i: (i, 0))
    return _pc(body, name=name, grid=(t // rb,), in_specs=[row, _full((1, d))], out_specs=row,
               out_shape=S((t, d), bf16), compiler_params=_cparams(("arbitrary",)))(x, g.reshape(1, d))


def _rms_bwd(dy, x, g, dres, name):
    t, d = x.shape
    rb = _row_block8(t)

    def body(dy_ref, x_ref, g_ref, dres_ref, dx_ref, dg_ref):
        @pl.when(pl.program_id(0) == 0)
        def _():
            dg_ref[...] = jnp.zeros_like(dg_ref)
        xv, dyv = x_ref[...], dy_ref[...]
        rstd = lax.rsqrt(jnp.mean(xv * xv, axis=-1, keepdims=True) + RMS_EPS)
        xn = xv * rstd
        dg_ref[...] += jnp.sum(dyv * xn, axis=0, keepdims=True)
        dxh = dyv * g_ref[...]
        dx_ref[...] = dres_ref[...] + rstd * (dxh - xn * jnp.mean(dxh * xn, axis=-1, keepdims=True))

    row = pl.BlockSpec((rb, d), lambda i: (i, 0))
    return _pc(body, name=name, grid=(t // rb,), in_specs=[row, row, _full((1, d)), row],
               out_specs=(row, _full((1, d))), out_shape=(S((t, d), f32), S((1, d), f32)),
               compiler_params=_cparams(("arbitrary",)))(dy, x, g.reshape(1, d), dres)


def _final_loss(h, g, target_padded):
    t, d = h.shape
    rb = _row_block8(t)

    def body(x_ref, g_ref, t_ref, loss_ref, dx_ref, dg_ref):
        i = pl.program_id(0)

        @pl.when(i == 0)
        def _():
            dg_ref[...] = jnp.zeros_like(dg_ref)
            loss_ref[...] = jnp.zeros_like(loss_ref)
        xv = x_ref[...]
        rstd = lax.rsqrt(jnp.mean(xv * xv, axis=-1, keepdims=True) + RMS_EPS)
        xn = xv * rstd
        gv = g_ref[...]
        row = i * rb + lax.broadcasted_iota(jnp.int32, (rb, 1), 0)
        diff = jnp.where(row >= N_META, xn * gv - t_ref[...], 0.0)
        loss_ref[...] += 0.5 * jnp.sum(jnp.mean(diff * diff, axis=-1, keepdims=True))
        dout = diff * (1.0 / d)
        dg_ref[...] += jnp.sum(dout * xn, axis=0, keepdims=True)
        dxh = dout * gv
        dx_ref[...] = rstd * (dxh - xn * jnp.mean(dxh * xn, axis=-1, keepdims=True))

    row = pl.BlockSpec((rb, d), lambda i: (i, 0))
    return _pc(body, name="final_loss", grid=(t // rb,), in_specs=[row, _full((1, d)), row],
               out_specs=(_full((SUBLANES, LANES)), row, _full((1, d))),
               out_shape=(S((SUBLANES, LANES), f32), S((t, d), f32), S((1, d), f32)),
               compiler_params=_cparams(("arbitrary",)))(h, g.reshape(1, d), target_padded)


CONV_LEAD = 32


def _fill_front_padded(pad_ref, x, t):
    pad_ref[0:CONV_LEAD, :] = jnp.zeros((CONV_LEAD, x.shape[1]), f32)
    pad_ref[CONV_LEAD:CONV_LEAD + t, :] = x


def _fill_back_padded(pad_ref, x, t):
    pad_ref[0:t, :] = x
    pad_ref[t:t + CONV_LEAD, :] = jnp.zeros((CONV_LEAD, x.shape[1]), f32)


def _conv_rows(pad_ref, w_ref, kw, r0, nr):
    acc = None
    for j in range(kw):
        lo = CONV_LEAD + r0 - (kw - 1) + j
        term = w_ref[j:j + 1, :] * pad_ref[lo:lo + nr, :]
        acc = term if acc is None else acc + term
    return acc


def _conv_t_rows(padb_ref, w_ref, kw, r0, nr):
    acc = None
    for j in range(kw):
        lo = r0 + (kw - 1) - j
        term = w_ref[j:j + 1, :] * padb_ref[lo:lo + nr, :]
        acc = term if acc is None else acc + term
    return acc


def _conv_dw_rows(dy_blk, pad_ref, kw, r0, nr):
    out = []
    for j in range(kw):
        lo = CONV_LEAD + r0 - (kw - 1) + j
        out.append(jnp.sum(dy_blk * pad_ref[lo:lo + nr, :], axis=0, keepdims=True))
    return out


def _acc_list(a, b):
    return b if a is None else [x + y for x, y in zip(a, b)]


def _ev_a_conv(p, conv_a):
    t = p.shape[0]
    cr = _row_block8(t)
    nb = D_A // LANES

    def body(av_ref, ag_ref, w_ref, o_ref, pad_ref):
        _fill_front_padded(pad_ref, av_ref[...] * _sigmoid(ag_ref[...]), t)
        for r in range(t // cr):
            o_ref[r * cr:(r + 1) * cr, :] = _conv_rows(pad_ref, w_ref, CONV_A_WIDTH, r * cr, cr)

    col = lambda off: pl.BlockSpec((t, LANES), lambda j: (0, j + off))
    return _pc(body, name="ev_a_conv", grid=(nb,),
               in_specs=[col(0), col(nb), pl.BlockSpec((CONV_A_WIDTH, LANES), lambda j: (0, j))],
               out_specs=col(0), out_shape=S((t, D_A), f32),
               scratch_shapes=[pltpu.VMEM((t + CONV_LEAD, LANES), f32)],
               compiler_params=_cparams(("arbitrary",)))(p, p, conv_a)


def _ln_silu(uc, g, b):
    mu = jnp.mean(uc, axis=-1, keepdims=True)
    xc = uc - mu
    var = jnp.mean(xc * xc, axis=-1, keepdims=True)
    y = xc * lax.rsqrt(var + LN_EPS) * g + b
    return y * _sigmoid(y)


def _ev_a_norm(uc, g, b):
    t, d = uc.shape
    rb = _row_block(t)

    def body(u_ref, g_ref, b_ref, o_ref):
        o_ref[...] = _ln_silu(u_ref[...], g_ref[...], b_ref[...]).astype(bf16)

    row = pl.BlockSpec((rb, d), lambda i: (i, 0))
    return _pc(body, name="ev_a_norm", grid=(t // rb,), in_specs=[row, _full((1, d)), _full((1, d))],
               out_specs=row, out_shape=S((t, 2 * d), bf16), compiler_params=_cparams(("arbitrary",)))(uc, g, b)


def _ev_a_norm_bwd(dy, uc, g, b):
    t, d = uc.shape
    rb = _row_block8(t)

    def body(dy_ref, u_ref, g_ref, b_ref, du_ref, dg_ref, db_ref):
        @pl.when(pl.program_id(0) == 0)
        def _():
            dg_ref[...] = jnp.zeros_like(dg_ref)
            db_ref[...] = jnp.zeros_like(db_ref)
        _, vjp = jax.vjp(_ln_silu, u_ref[...], g_ref[...], b_ref[...])
        du, dg, db = vjp(dy_ref[...])
        du_ref[...] = du
        dg_ref[...] += dg
        db_ref[...] += db

    row = pl.BlockSpec((rb, d), lambda i: (i, 0))
    return _pc(body, name="ev_a_norm_bwd", grid=(t // rb,), in_specs=[row, row, _full((1, d)), _full((1, d))],
               out_specs=(row, _full((1, d)), _full((1, d))),
               out_shape=(S((t, d), f32), S((1, d), f32), S((1, d), f32)),
               compiler_params=_cparams(("arbitrary",)))(dy, uc, g, b)


def _ev_a_conv_bwd(duc, p, conv_a):
    t = p.shape[0]
    cr = _row_block8(t)
    nb = D_A // LANES

    def body(dy_ref, av_ref, ag_ref, w_ref, dav_ref, dag_ref, dw_ref, pad_ref, padb_ref):
        _fill_front_padded(pad_ref, av_ref[...] * _sigmoid(ag_ref[...]), t)
        _fill_back_padded(padb_ref, dy_ref[...], t)
        dw = None
        for r in range(t // cr):
            rows = slice(r * cr, (r + 1) * cr)
            du = _conv_t_rows(padb_ref, w_ref, CONV_A_WIDTH, r * cr, cr)
            avr = av_ref[rows, :]
            sgr = _sigmoid(ag_ref[rows, :])
            dav_ref[rows, :] = du * sgr
            dag_ref[rows, :] = du * avr * sgr * (1.0 - sgr)
            dw = _acc_list(dw, _conv_dw_rows(dy_ref[rows, :], pad_ref, CONV_A_WIDTH, r * cr, cr))
        for j in range(CONV_A_WIDTH):
            dw_ref[j:j + 1, :] = dw[j]

    col = lambda off: pl.BlockSpec((t, LANES), lambda j: (0, j + off))
    wsp = pl.BlockSpec((CONV_A_WIDTH, LANES), lambda j: (0, j))
    return _pc(body, name="ev_a_conv_bwd", grid=(nb,), in_specs=[col(0), col(0), col(nb), wsp],
               out_specs=(col(0), col(0), wsp),
               out_shape=(S((t, D_A), f32), S((t, D_A), f32), S((CONV_A_WIDTH, D_A), f32)),
               scratch_shapes=[pltpu.VMEM((t + CONV_LEAD, LANES), f32), pltpu.VMEM((t + CONV_LEAD, LANES), f32)],
               compiler_params=_cparams(("arbitrary",)))(duc, p, p, conv_a)


def _ev_b(p, conv_b, y):
    t = p.shape[0]
    cr = _row_block8(t)
    nb = D_A // LANES

    def body(gb_ref, gc_ref, xi_ref, w_ref, y_ref, o_ref, pad_ref, stage_ref):
        _fill_front_padded(pad_ref, gc_ref[...] * xi_ref[...], t)
        for r in range(t // cr):
            rows = slice(r * cr, (r + 1) * cr)
            stage_ref[rows, :] = gb_ref[rows, :] * _conv_rows(pad_ref, w_ref, CONV_B_WIDTH, r * cr, cr)
        o_ref[...] = stage_ref[...].astype(bf16)

    col = lambda off: pl.BlockSpec((t, LANES), lambda j: (0, j + off))
    return _pc(body, name="ev_b", grid=(nb,),
               in_specs=[col(2 * nb), col(3 * nb), col(4 * nb), pl.BlockSpec((CONV_B_WIDTH, LANES), lambda j: (0, j)), HBM],
               out_specs=col(nb), out_shape=S(y.shape, bf16), input_output_aliases={4: 0},
               scratch_shapes=[pltpu.VMEM((t + CONV_LEAD, LANES), f32), pltpu.VMEM((t, LANES), f32)],
               compiler_params=_cparams(("arbitrary",)))(p, p, p, conv_b, y)


def _ev_b_bwd(dy, p, conv_b):
    t = p.shape[0]
    cr = _row_block8(t)
    nb = D_A // LANES

    def body(dy_ref, gb_ref, gc_ref, xi_ref, w_ref, dgb_ref, dgc_ref, dxi_ref, dw_ref, pad_ref, padb_ref):
        _fill_front_padded(pad_ref, gc_ref[...] * xi_ref[...], t)
        _fill_back_padded(padb_ref, dy_ref[...] * gb_ref[...], t)
        dw = None
        for r in range(t // cr):
            rows = slice(r * cr, (r + 1) * cr)
            dgb_ref[rows, :] = dy_ref[rows, :] * _conv_rows(pad_ref, w_ref, CONV_B_WIDTH, r * cr, cr)
            dcx = _conv_t_rows(padb_ref, w_ref, CONV_B_WIDTH, r * cr, cr)
            dgc_ref[rows, :] = dcx * xi_ref[rows, :]
            dxi_ref[rows, :] = dcx * gc_ref[rows, :]
            dw = _acc_list(dw, _conv_dw_rows(padb_ref[rows, :], pad_ref, CONV_B_WIDTH, r * cr, cr))
        for j in range(CONV_B_WIDTH):
            dw_ref[j:j + 1, :] = dw[j]

    col = lambda off: pl.BlockSpec((t, LANES), lambda j: (0, j + off))
    wsp = pl.BlockSpec((CONV_B_WIDTH, LANES), lambda j: (0, j))
    return _pc(body, name="ev_b_bwd", grid=(nb,), in_specs=[col(nb), col(2 * nb), col(3 * nb), col(4 * nb), wsp],
               out_specs=(col(0), col(0), col(0), wsp),
               out_shape=(S((t, D_A), f32), S((t, D_A), f32), S((t, D_A), f32), S((CONV_B_WIDTH, D_A), f32)),
               scratch_shapes=[pltpu.VMEM((t + CONV_LEAD, LANES), f32), pltpu.VMEM((t + CONV_LEAD, LANES), f32)],
               compiler_params=_cparams(("arbitrary",)))(dy, p, p, p, conv_b)


def _ffn_mid(u, conv_w, conv_b, name):
    t = u.shape[0]
    cr = _row_block8(t)
    nb = D_FF // FF_BLOCK

    def one(gt_ref, vl_ref, w_ref, b_ref, o_ref, pad_ref, stage_ref):
        _fill_front_padded(pad_ref, gt_ref[...], t)
        for r in range(t // cr):
            rows = slice(r * cr, (r + 1) * cr)
            gc = _conv_rows(pad_ref, w_ref, FF_CONV_WIDTH, r * cr, cr) + b_ref[...]
            stage_ref[rows, :] = gc * _sigmoid(gc) * vl_ref[rows, :]
        o_ref[...] = stage_ref[...].astype(bf16)

    def body(*refs):
        for h in range(FF_BLOCK // LANES):
            one(*[r.at[:, pl.ds(h * LANES, LANES)] for r in refs[:5]], *refs[5:])

    col = lambda off: pl.BlockSpec((t, FF_BLOCK), lambda j: (0, j + off))
    return _pc(body, name=name, grid=(nb,),
               in_specs=[col(0), col(nb), pl.BlockSpec((FF_CONV_WIDTH, FF_BLOCK), lambda j: (0, j)),
                         pl.BlockSpec((1, FF_BLOCK), lambda j: (0, j))],
               out_specs=col(0), out_shape=S((t, D_FF), bf16),
               scratch_shapes=[pltpu.VMEM((t + CONV_LEAD, LANES), f32), pltpu.VMEM((t, LANES), f32)],
               compiler_params=_cparams(("arbitrary",)))(u, u, conv_w, conv_b.reshape(1, D_FF))


def _ffn_mid_bwd(dz, u, conv_w, conv_b, name):
    t = u.shape[0]
    cr = _row_block8(t)
    nb = D_FF // FF_BLOCK
    nh = FF_BLOCK // LANES

    def body(*refs):
        for h in range(nh):
            one(*[r.at[:, pl.ds(h * LANES, LANES)] for r in refs[:9]], *refs[9:])

    def one(dz_ref, gt_ref, vl_ref, w_ref, b_ref, du_ref, dv_ref, dw_ref, db_ref, pad_ref, padb_ref, stage_ref):
        _fill_front_padded(pad_ref, gt_ref[...], t)
        dw, db = None, None
        for r in range(t // cr):
            rows = slice(r * cr, (r + 1) * cr)
            lo = CONV_LEAD + r * cr - (FF_CONV_WIDTH - 1)
            taps = [pad_ref[lo + j:lo + j + cr, :] for j in range(FF_CONV_WIDTH)]
            gc = sum(w_ref[j:j + 1, :] * taps[j] for j in range(FF_CONV_WIDTH)) + b_ref[...]
            sg = _sigmoid(gc)
            dzr = dz_ref[rows, :]
            stage_ref[rows, :] = dzr * gc * sg
            dgc = dzr * vl_ref[rows, :] * sg * (1.0 + gc * (1.0 - sg))
            padb_ref[rows, :] = dgc
            dw = _acc_list(dw, [jnp.sum(dgc * tap, axis=0, keepdims=True) for tap in taps])
            pb = jnp.sum(dgc, axis=0, keepdims=True)
            db = pb if db is None else db + pb
        padb_ref[t:t + CONV_LEAD, :] = jnp.zeros((CONV_LEAD, LANES), f32)
        for r in range(t // cr):
            pad_ref[r * cr:(r + 1) * cr, :] = _conv_t_rows(padb_ref, w_ref, FF_CONV_WIDTH, r * cr, cr)
        du_ref[...] = pad_ref[0:t, :].astype(du_ref.dtype)
        dv_ref[...] = stage_ref[...].astype(dv_ref.dtype)
        for j in range(FF_CONV_WIDTH):
            dw_ref[j:j + 1, :] = dw[j]
        db_ref[...] = db

    col = lambda off: pl.BlockSpec((t, FF_BLOCK), lambda j: (0, j + off))
    wsp = pl.BlockSpec((FF_CONV_WIDTH, FF_BLOCK), lambda j: (0, j))
    bsp = pl.BlockSpec((1, FF_BLOCK), lambda j: (0, j))
    return _pc(body, name=name, grid=(nb,), in_specs=[col(0), col(0), col(nb), wsp, bsp],
               out_specs=(col(0), col(0), wsp, bsp),
               out_shape=(S((t, D_FF), bf16), S((t, D_FF), bf16), S((FF_CONV_WIDTH, D_FF), f32), S((1, D_FF), f32)),
               scratch_shapes=[pltpu.VMEM((t + CONV_LEAD, LANES), f32), pltpu.VMEM((t + CONV_LEAD, LANES), f32),
                               pltpu.VMEM((t, LANES), f32)],
               compiler_params=_cparams(("arbitrary",)))(dz, u, u, conv_w, conv_b.reshape(1, D_FF))


def _swap_halves(x):
    w = x.shape[1]
    lane = lax.broadcasted_iota(jnp.int32, x.shape, 1) % HEAD_DIM
    return jnp.where(lane < HEAD_DIM // 2, pltpu.roll(x, w - HEAD_DIM // 2, axis=1), pltpu.roll(x, HEAD_DIM // 2, axis=1))


def _rope_pack(patt, c64, s64):
    t = patt.shape[0]
    tp = t + ATT_PAD

    def body(p_ref, c_ref, s_ref, q_ref, k_ref, v_ref):
        c, s = c_ref[...], s_ref[...]

        def rope(x, nh):
            cc = jnp.concatenate([c] * nh, axis=1)
            ss = jnp.concatenate([s] * nh, axis=1)
            return x * cc + _swap_halves(x) * ss

        for ref, val in ((q_ref, rope(p_ref[:, 0:D_ATT], N_Q_HEADS)),
                         (k_ref, rope(p_ref[:, D_ATT:D_ATT + D_KV], N_KV_HEADS)),
                         (v_ref, p_ref[:, D_ATT + D_KV:ATT_COLS])):
            ref[0:ATT_PAD, :] = jnp.zeros((ATT_PAD, val.shape[1]), bf16)
            ref[ATT_PAD:tp, :] = val.astype(bf16)

    return _pc(body, name="rope_pack", in_specs=[_full((t, ATT_COLS)), _full((t, HEAD_DIM)), _full((t, HEAD_DIM))],
               out_specs=(_full((tp, D_ATT)), _full((tp, D_KV)), _full((tp, D_KV))), grid=(1,),
               out_shape=(S((tp, D_ATT), bf16), S((tp, D_KV), bf16), S((tp, D_KV), bf16)),
               compiler_params=_cparams(("arbitrary",)))(patt, c64, s64)


def _rope_bwd(dqp, dkp, dvp, c64, s64):
    tp = dqp.shape[0]
    t = tp - ATT_PAD

    def body(dq_ref, dk_ref, dv_ref, c_ref, s_ref, o_ref):
        c, s = c_ref[...], s_ref[...]

        def unrope(dy, nh):
            cc = jnp.concatenate([c] * nh, axis=1)
            ss = jnp.concatenate([s] * nh, axis=1)
            return dy * cc + _swap_halves(dy * ss)

        o_ref[:, 0:D_ATT] = unrope(dq_ref[ATT_PAD:tp, :], N_Q_HEADS).astype(bf16)
        o_ref[:, D_ATT:D_ATT + D_KV] = unrope(dk_ref[ATT_PAD:tp, :], N_KV_HEADS).astype(bf16)
        o_ref[:, D_ATT + D_KV:ATT_COLS] = dv_ref[ATT_PAD:tp, :].astype(bf16)

    return _pc(body, name="rope_bwd", grid=(1,),
               in_specs=[_full((tp, D_ATT)), _full((tp, D_KV)), _full((tp, D_KV)), _full((t, HEAD_DIM)), _full((t, HEAD_DIM))],
               out_specs=_full((t, ATT_COLS)), out_shape=S((t, ATT_COLS), bf16),
               compiler_params=_cparams(("arbitrary",)))(dqp, dkp, dvp, c64, s64)


def _attn_masks(n):
    rows = GQA_GROUP * BLOCK
    ri = lax.broadcasted_iota(jnp.int32, (rows, BLOCK), 0) % BLOCK
    ci = lax.broadcasted_iota(jnp.int32, (rows, BLOCK), 1)
    m_cur = (ci <= ri) & (ci >= jnp.where(n >= 1, 0, ATT_PAD))
    m_prev = ci > ri + jnp.where(n >= 2, 0, BLOCK)
    m_meta = ci >= jnp.where(n >= 1, ATT_PAD, BLOCK)
    return m_cur, m_prev, m_meta


def _attn_probs(qg, kc, kp, km, masks, skv):
    def scores(k, m):
        s = lax.dot_general(qg, k, _DIMS["nt"], preferred_element_type=f32) * ATT_SCALE
        return jnp.where(m, s, NEG_INF)
    s_c, s_p, s_m = scores(kc, masks[0]), scores(kp, masks[1]), scores(km, masks[2])
    mx = jnp.maximum(jnp.maximum(jnp.max(s_c, axis=-1, keepdims=True), jnp.max(s_p, axis=-1, keepdims=True)),
                     jnp.maximum(jnp.max(s_m, axis=-1, keepdims=True), skv))
    e_c, e_p, e_m, e_s = jnp.exp(s_c - mx), jnp.exp(s_p - mx), jnp.exp(s_m - mx), jnp.exp(skv - mx)
    den = (jnp.sum(e_c, axis=-1, keepdims=True) + jnp.sum(e_p, axis=-1, keepdims=True)
           + jnp.sum(e_m, axis=-1, keepdims=True) + e_s)
    inv = 1.0 / den
    return e_c * inv, e_p * inv, e_m * inv, e_s * inv


def _sink_rows(sk_ref, g):
    hrow = lax.broadcasted_iota(jnp.int32, (GQA_GROUP * BLOCK, 1), 0) // BLOCK
    skv = jnp.zeros((GQA_GROUP * BLOCK, 1), f32)
    for hh in range(GQA_GROUP):
        skv = jnp.where(hrow == hh, sk_ref[0, GQA_GROUP * g + hh], skv)
    return skv, hrow


def _stack_heads(ref, g):
    return jnp.concatenate([ref[:, (GQA_GROUP * g + hh) * HEAD_DIM:(GQA_GROUP * g + hh + 1) * HEAD_DIM]
                            for hh in range(GQA_GROUP)], axis=0)


def _attn_specs():
    blk = lambda w: pl.BlockSpec((BLOCK, w), lambda n: (n, 0))
    prev = pl.BlockSpec((BLOCK, D_KV), lambda n: (jnp.maximum(n - 1, 0), 0))
    meta = pl.BlockSpec((BLOCK, D_KV), lambda n: (0, 0))
    return blk, prev, meta


def _attn_fwd(qp, kp, vp, sinks):
    tp = qp.shape[0]
    blk, prev, meta = _attn_specs()

    def body(sk_ref, q_ref, kc_ref, kp_ref, km_ref, vc_ref, vp_ref, vm_ref, o_ref):
        masks = _attn_masks(pl.program_id(0))
        for g in range(N_KV_HEADS):
            sl = slice(g * HEAD_DIM, (g + 1) * HEAD_DIM)
            skv, _ = _sink_rows(sk_ref, g)
            p_c, p_p, p_m, _ = _attn_probs(_stack_heads(q_ref, g), kc_ref[:, sl], kp_ref[:, sl], km_ref[:, sl], masks, skv)
            o = (jnp.dot(p_c.astype(bf16), vc_ref[:, sl], preferred_element_type=f32)
                 + jnp.dot(p_p.astype(bf16), vp_ref[:, sl], preferred_element_type=f32)
                 + jnp.dot(p_m.astype(bf16), vm_ref[:, sl], preferred_element_type=f32))
            for hh in range(GQA_GROUP):
                h = GQA_GROUP * g + hh
                o_ref[:, h * HEAD_DIM:(h + 1) * HEAD_DIM] = o[hh * BLOCK:(hh + 1) * BLOCK].astype(bf16)

    return _pc(body, name="attn_fwd", grid=(tp // BLOCK,),
               in_specs=[pl.BlockSpec(memory_space=pltpu.SMEM), blk(D_ATT), blk(D_KV), prev, meta, blk(D_KV), prev, meta],
               out_specs=blk(D_ATT), out_shape=S((tp, D_ATT), bf16),
               compiler_params=_cparams(("arbitrary",)))(sinks, qp, kp, kp, kp, vp, vp, vp)


def _attn_bwd(qp, kp, vp, sinks, dop):
    tp = qp.shape[0]
    blk, prev, meta = _attn_specs()

    def body(sk_ref, q_ref, kc_ref, kp_ref, km_ref, vc_ref, vp_ref, vm_ref, do_ref, dq_ref, dk_ref, dv_ref, dsk_ref):
        n = pl.program_id(0)

        @pl.when(n == 0)
        def _():
            dk_ref[...] = jnp.zeros_like(dk_ref)
            dv_ref[...] = jnp.zeros_like(dv_ref)
            dsk_ref[...] = jnp.zeros_like(dsk_ref)
        masks = _attn_masks(n)
        cur = pl.ds(pl.multiple_of(n * BLOCK, BLOCK), BLOCK)
        prv = pl.ds(pl.multiple_of(jnp.maximum(n - 1, 0) * BLOCK, BLOCK), BLOCK)
        lane = lax.broadcasted_iota(jnp.int32, (1, LANES), 1)
        dsk = jnp.zeros((1, LANES), f32)
        for g in range(N_KV_HEADS):
            sl = slice(g * HEAD_DIM, (g + 1) * HEAD_DIM)
            skv, hrow = _sink_rows(sk_ref, g)
            qg = _stack_heads(q_ref, g)
            dog = _stack_heads(do_ref, g)
            ks = (kc_ref[:, sl], kp_ref[:, sl], km_ref[:, sl])
            vs = (vc_ref[:, sl], vp_ref[:, sl], vm_ref[:, sl])
            probs = _attn_probs(qg, ks[0], ks[1], ks[2], masks, skv)
            dps = [lax.dot_general(dog, v, _DIMS["nt"], preferred_element_type=f32) for v in vs]
            delta = sum(jnp.sum(p * dp, axis=-1, keepdims=True) for p, dp in zip(probs[:3], dps))
            dss = [(p * (dp - delta) * ATT_SCALE).astype(bf16) for p, dp in zip(probs[:3], dps)]
            dq = sum(jnp.dot(ds, k, preferred_element_type=f32) for ds, k in zip(dss, ks))
            for hh in range(GQA_GROUP):
                h = GQA_GROUP * g + hh
                dq_ref[:, h * HEAD_DIM:(h + 1) * HEAD_DIM] = dq[hh * BLOCK:(hh + 1) * BLOCK]
                dsk = dsk + jnp.where(lane == h, -jnp.sum(jnp.where(hrow == hh, probs[3] * delta, 0.0)), 0.0)
            for rows, p, ds in zip((cur, prv, slice(0, BLOCK)), probs[:3], dss):
                dv_ref[rows, sl] += lax.dot_general(p.astype(bf16), dog, _DIMS["tn"], preferred_element_type=f32)
                dk_ref[rows, sl] += lax.dot_general(ds, qg, _DIMS["tn"], preferred_element_type=f32)
        dsk_ref[...] += dsk

    return _pc(body, name="attn_bwd", grid=(tp // BLOCK,),
               in_specs=[pl.BlockSpec(memory_space=pltpu.SMEM), blk(D_ATT), blk(D_KV), prev, meta, blk(D_KV), prev, meta,
                         blk(D_ATT)],
               out_specs=(blk(D_ATT), _full((tp, D_KV)), _full((tp, D_KV)), _full((1, LANES))),
               out_shape=(S((tp, D_ATT), f32), S((tp, D_KV), f32), S((tp, D_KV), f32), S((1, LANES), f32)),
               compiler_params=_cparams(("arbitrary",)))(sinks, qp, kp, kp, kp, vp, vp, vp, dop)


def _seg(x, bm):
    hi = x.astype(bf16)
    lo = (x - hi.astype(f32)).astype(bf16)
    return jnp.dot(jnp.concatenate([hi, lo], axis=1), bm, preferred_element_type=f32)


@jax.custom_vjp
def _seg_linear(x, bm):
    return _seg(x, bm)


_seg_linear.defvjp(lambda x, bm: (_seg(x, bm), bm), lambda bm, ct: (_seg(ct, bm), jnp.zeros_like(bm)))


def _softplus(y):
    return jnp.maximum(y, 0.0) + jnp.log(1.0 + jnp.exp(-jnp.abs(y)))


def _prep_fn(xr, xk, xwd, xad, xgd, w0, w2, a0, a2, g2, k_k, k_a, bm, seg=_seg):
    xw = w0 + jnp.dot(jnp.tanh(xwd), w2, preferred_element_type=f32)
    decay = jnp.exp(-jnp.exp(-_softplus(-xw) - 0.5))
    alpha = _sigmoid(a0 + jnp.dot(xad, a2, preferred_element_type=f32))
    g = jnp.dot(_sigmoid(xgd), g2, preferred_element_type=f32)
    kk = xk * k_k
    kkn = kk / jnp.maximum(jnp.sqrt(seg(kk * kk, bm)), 1e-12)
    k2 = xk * (1.0 + (alpha - 1.0) * k_a)
    return decay, k2, -kkn, kkn * alpha, g


def _split_cols(x):
    o1, o2, o3 = 3 * D_R, 3 * D_R + LORA_W, 3 * D_R + LORA_W + LORA_A
    return x[:, 0:D_R], x[:, D_R:2 * D_R], x[:, 2 * D_R:o1], x[:, o1:o2], x[:, o2:o3], x[:, o3:RWKV_COLS]


def _shifted(sh_ref, x, halo, first, rb):
    sh_ref[0:SUBLANES, :] = jnp.where(first, 0.0, halo)
    sh_ref[SUBLANES:SUBLANES + rb, :] = x
    return sh_ref[SUBLANES - 1:SUBLANES - 1 + rb, :]


_PREP_PARAMS = ("od_w0", "od_w2", "od_a0", "od_a2", "od_g2", "od_k_k", "od_k_a")


def _rwkv_prep(pr, mu, params, bm):
    t = pr.shape[0]
    rb = _row_block8(t)
    hb = rb // SUBLANES

    def body(pr_ref, halo_ref, mu_ref, w0, w2, a0, a2, g2, kk_ref, ka_ref, bm_ref, *outs_sh):
        outs, sh_ref = outs_sh[:-1], outs_sh[-1]
        x = pr_ref[...]
        prev = _shifted(sh_ref, x, halo_ref[...], pl.program_id(0) == 0, rb)
        xr, xk, xv, xwd, xad, xgd = _split_cols(x + (prev - x) * mu_ref[...])
        bmv = bm_ref[...]
        decay, k2, a_s, b_s, g = _prep_fn(xr, xk, xwd, xad, xgd, w0[...], w2[...], a0[...], a2[...], g2[...],
                                          kk_ref[...], ka_ref[...], bmv)
        vals = (xr, xv, decay, k2, a_s, b_s, decay * xr, _seg(b_s * xr, bmv), _seg(k2 * xr, bmv), g)
        for ref, val in zip(outs, vals):
            ref[...] = val

    row = pl.BlockSpec((rb, RWKV_COLS), lambda i: (i, 0))
    halo = pl.BlockSpec((SUBLANES, RWKV_COLS), lambda i: (jnp.maximum(i * hb - 1, 0), 0))
    orow = pl.BlockSpec((rb, D_R), lambda i: (i, 0))
    return _pc(body, name="rwkv_prep", grid=(t // rb,),
               in_specs=[row, halo, _full((1, RWKV_COLS))] + [_full(p.shape) for p in params] + [_full(bm.shape)],
               out_specs=(orow,) * 10, out_shape=(S((t, D_R), f32),) * 10,
               scratch_shapes=[pltpu.VMEM((rb + SUBLANES, RWKV_COLS), f32)],
               compiler_params=_cparams(("arbitrary",)))(pr, pr, mu, *params, bm)


def _rwkv_prep_bwd(pr, mu, params, bm, cts):
    t = pr.shape[0]
    rb = _row_block8(t)
    hb = rb // SUBLANES
    counts = [len(c) for c in cts]
    flat = [a for c in cts for a in c]

    def body(pr_ref, halo_ref, mu_ref, w0, w2, a0, a2, g2, kk_ref, ka_ref, bm_ref, *rest):
        ct_refs, rest = rest[:len(flat)], rest[len(flat):]
        dx_ref, dmu_ref = rest[0], rest[1]
        dpar_refs, sh_ref = rest[2:9], rest[9]

        @pl.when(pl.program_id(0) == 0)
        def _():
            dmu_ref[...] = jnp.zeros_like(dmu_ref)
            for r in dpar_refs:
                r[...] = jnp.zeros_like(r)
        sums, pos = [], 0
        for c in counts:
            sums.append(sum(r[...] for r in ct_refs[pos:pos + c]))
            pos += c
        x = pr_ref[...]
        prev = _shifted(sh_ref, x, halo_ref[...], pl.program_id(0) == 0, rb)
        xr, xk, xv, xwd, xad, xgd = _split_cols(x + (prev - x) * mu_ref[...])
        bmv = bm_ref[...]
        _, vjp = jax.vjp(lambda *a: _prep_fn(*a, bmv, _seg_linear), xr, xk, xwd, xad, xgd, w0[...], w2[...], a0[...], a2[...],
                         g2[...], kk_ref[...], ka_ref[...])
        grads = vjp(tuple(sums[:5]))
        dxr, dxk, dxwd, dxad, dxgd = grads[:5]
        o1, o2, o3 = 3 * D_R, 3 * D_R + LORA_W, 3 * D_R + LORA_W + LORA_A
        dx_ref[:, 0:D_R] = dxr + sums[5]
        dx_ref[:, D_R:2 * D_R] = dxk
        dx_ref[:, 2 * D_R:o1] = sums[6]
        dx_ref[:, o1:o2] = dxwd
        dx_ref[:, o2:o3] = dxad
        dx_ref[:, o3:RWKV_COLS] = dxgd
        dmu_ref[...] += jnp.sum(dx_ref[...] * (prev - x), axis=0, keepdims=True)
        for r, gval in zip(dpar_refs, grads[5:]):
            r[...] += gval

    row = pl.BlockSpec((rb, RWKV_COLS), lambda i: (i, 0))
    halo = pl.BlockSpec((SUBLANES, RWKV_COLS), lambda i: (jnp.maximum(i * hb - 1, 0), 0))
    crow = pl.BlockSpec((rb, D_R), lambda i: (i, 0))
    return _pc(body, name="rwkv_prep_bwd", grid=(t // rb,),
               in_specs=[row, halo, _full((1, RWKV_COLS))] + [_full(p.shape) for p in params] + [_full(bm.shape)]
               + [crow] * len(flat),
               out_specs=(row, _full((1, RWKV_COLS))) + tuple(_full(p.shape) for p in params),
               out_shape=(S((t, RWKV_COLS), f32), S((1, RWKV_COLS), f32)) + tuple(S(p.shape, f32) for p in params),
               scratch_shapes=[pltpu.VMEM((rb + SUBLANES, RWKV_COLS), f32)],
               compiler_params=_cparams(("arbitrary",)))(pr, pr, mu, *params, bm, *flat)


def _shift_bwd(dxs, mu):
    t = dxs.shape[0]
    rb = _row_block(t)
    hb = rb // SUBLANES
    nblk = t // rb

    def body(dx_ref, halo_ref, mu_ref, o_ref, sh_ref):
        dx = dx_ref[...]
        sh_ref[0:rb, :] = dx
        sh_ref[rb:rb + SUBLANES, :] = jnp.where(pl.program_id(0) == nblk - 1, 0.0, halo_ref[...])
        m = mu_ref[...]
        o_ref[...] = (dx * (1.0 - m) + sh_ref[1:1 + rb, :] * m).astype(bf16)

    row = pl.BlockSpec((rb, RWKV_COLS), lambda i: (i, 0))
    halo = pl.BlockSpec((SUBLANES, RWKV_COLS), lambda i: (jnp.minimum((i + 1) * hb, t // SUBLANES - 1), 0))
    return _pc(body, name="rwkv_shift_bwd", grid=(nblk,), in_specs=[row, halo, _full((1, RWKV_COLS))],
               out_specs=row, out_shape=S((t, RWKV_COLS), bf16),
               scratch_shapes=[pltpu.VMEM((rb + SUBLANES, RWKV_COLS), f32)],
               compiler_params=_cparams(("arbitrary",)))(dxs, dxs, mu)


def _post_fn(y, xr, k2, xv, g, lg, lb, rk, bm, seg=_seg):
    inv_n = 1.0 / HEAD_DIM
    yc = y - seg(y, bm) * inv_n
    var = seg(yc * yc, bm) * inv_n
    yn = yc * lax.rsqrt(var + RWKV_GN_EPS) * lg + lb
    return (yn + seg(xr * k2 * rk, bm) * xv) * g


def _rwkv_post(y, xr, k2, xv, g, lg, lb, rk, bm):
    t = y.shape[0]
    rb = _row_block8(t)

    def body(y_ref, xr_ref, k2_ref, xv_ref, g_ref, lg_ref, lb_ref, rk_ref, bm_ref, o_ref):
        o_ref[...] = _post_fn(y_ref[...], xr_ref[...], k2_ref[...], xv_ref[...], g_ref[...], lg_ref[...], lb_ref[...],
                              rk_ref[...], bm_ref[...])

    row = pl.BlockSpec((rb, D_R), lambda i: (i, 0))
    vec = _full((1, D_R))
    return _pc(body, name="rwkv_post", grid=(t // rb,), in_specs=[row] * 5 + [vec] * 3 + [_full(bm.shape)],
               out_specs=row, out_shape=S((t, D_R), f32),
               compiler_params=_cparams(("arbitrary",)))(y, xr, k2, xv, g, lg, lb, rk, bm)


def _rwkv_post_bwd(dy1, y, xr, k2, xv, g, lg, lb, rk, bm):
    t = y.shape[0]
    rb = _row_block8(t)

    def body(dy_ref, y_ref, xr_ref, k2_ref, xv_ref, g_ref, lg_ref, lb_ref, rk_ref, bm_ref, *outs):
        @pl.when(pl.program_id(0) == 0)
        def _():
            for r in outs[5:]:
                r[...] = jnp.zeros_like(r)
        bmv = bm_ref[...]
        _, vjp = jax.vjp(lambda *a: _post_fn(*a, bmv, _seg_linear), y_ref[...], xr_ref[...], k2_ref[...], xv_ref[...], g_ref[...],
                         lg_ref[...], lb_ref[...], rk_ref[...])
        grads = vjp(dy_ref[...])
        for r, gval in zip(outs[:5], grads[:5]):
            r[...] = gval
        for r, gval in zip(outs[5:], grads[5:]):
            r[...] += gval

    row = pl.BlockSpec((rb, D_R), lambda i: (i, 0))
    vec = _full((1, D_R))
    return _pc(body, name="rwkv_post_bwd", grid=(t // rb,),
               in_specs=[pl.BlockSpec((rb, D_R), lambda i: (i, 1))] + [row] * 5 + [vec] * 3 + [_full(bm.shape)],
               out_specs=(row,) * 5 + (vec,) * 3, out_shape=(S((t, D_R), f32),) * 5 + (S((1, D_R), f32),) * 3,
               compiler_params=_cparams(("arbitrary",)))(dy1, y, xr, k2, xv, g, lg, lb, rk, bm)


def _seg2(x, bb):
    hi = x.astype(bf16)
    lo = (x - hi.astype(f32)).astype(bf16)
    return jnp.dot(jnp.concatenate([hi, lo], axis=1), bb, preferred_element_type=f32)


def _row4(rows, j):
    return jnp.concatenate([jnp.broadcast_to(rows[j:j + 1, p * LANES:(p + 1) * LANES], (HEAD_DIM, LANES))
                            for p in range(4)], axis=0)


def _scan_consts():
    lane_group = jnp.arange(LANES) // HEAD_DIM
    b128 = (lane_group[:, None] == lane_group[None, :]).astype(bf16)
    bb = jnp.concatenate([b128, b128], axis=0)
    qsel = (jnp.arange(PAIR_ROWS)[:, None] % HEAD_DIM == jnp.arange(LANES)[None, :] % HEAD_DIM).astype(f32)
    return bb, qsel


def _store_cols(acc_ref, o_ref, tc):
    for p in range(4):
        blk = acc_ref[p * HEAD_DIM:(p + 1) * HEAD_DIM, :].T
        o_ref[:, (2 * p) * HEAD_DIM:(2 * p + 1) * HEAD_DIM] = blk[0:tc]
        o_ref[:, (2 * p + 1) * HEAD_DIM:(2 * p + 2) * HEAD_DIM] = blk[HEAD_DIM:HEAD_DIM + tc]


PAIR_GROUP = 2 * SUBLANES


def _rwkv_pairs(w, a, b, k, v, wr, br, kr, bm):
    t = w.shape[0]
    rb = _row_block8(t)

    def body(w_ref, a_ref, b_ref, k_ref, v_ref, wr_ref, br_ref, kr_ref, bm_ref, *outs_sh):
        outs, sh_ref = outs_sh[:-1], outs_sh[-1]

        def second(ref):
            sh_ref[0:rb, :] = ref[...]
            sh_ref[rb:rb + SUBLANES, :] = jnp.zeros((SUBLANES, D_R), f32)
            return sh_ref[1:1 + rb, :]

        w1, a1, b1, k1, v1 = w_ref[...], a_ref[...], b_ref[...], k_ref[...], v_ref[...]
        w2, a2, wr2, br2, kr2, v2 = (second(r) for r in (w_ref, a_ref, wr_ref, br_ref, kr_ref, v_ref))
        bmv = bm_ref[...]
        beta, kappa = _seg(b1 * a2, bmv), _seg(k1 * a2, bmv)
        bwr2, kwr2 = _seg(b1 * wr2, bmv), _seg(k1 * wr2, bmv)
        w1a2 = w1 * a2
        vals = (w1a2 + a1 * beta, v1 * kappa,
                wr_ref[...] + a1 * br_ref[...], v1 * kr_ref[...],
                w1 * wr2 + a1 * (bwr2 + beta * br2) + w1a2 * br2,
                v1 * (kwr2 + kappa * br2) + v2 * kr2,
                w1 * w2, b1 * w2, k1 * w2)
        for ref, val in zip(outs, vals):
            ref[...] = val

    row = pl.BlockSpec((rb, D_R), lambda i: (i, 0))
    return _pc(body, name="rwkv_pairs", grid=(t // rb,), in_specs=[row] * 8 + [_full(bm.shape)],
               out_specs=(row,) * 9, out_shape=(S((t, D_R), f32),) * 9,
               scratch_shapes=[pltpu.VMEM((rb + SUBLANES, D_R), f32)],
               compiler_params=_cparams(("arbitrary",)))(w, a, b, k, v, wr, br, kr, bm)


def _wkv_fwd(k, v, a, b, pairs):
    t = k.shape[0]
    tc = SCAN_CHUNK
    bb, qsel = _scan_consts()

    def body(*refs):
        k16, v16, a16, b16 = refs[0:4]
        ca2p, da2p, c1p, d1p, c2p, d2p, w12p, b1wp, k1wp = refs[4:13]
        bb_ref, q_ref, y_ref, st_ref, sa_ref, vb_ref, s_scr, yacc = refs[13:]

        @pl.when(pl.program_id(0) == 0)
        def _():
            s_scr[...] = jnp.zeros_like(s_scr)
        bbv, qp = bb_ref[...], q_ref[0:HEAD_DIM, :]
        lane = lax.broadcasted_iota(jnp.int32, (HEAD_DIM, LANES), 1) % HEAD_DIM

        def halves(x):
            hi = x.astype(bf16)
            return jnp.concatenate([hi, (x - hi.astype(f32)).astype(bf16)], axis=1)

        def group(gi, s):
            base = pl.multiple_of(gi * PAIR_GROUP, PAIR_GROUP)

            def rows8(ref, j):
                return ref[pl.ds(base + (j // SUBLANES) * SUBLANES, SUBLANES), :]

            def bcast(rows, j, p):
                return jnp.broadcast_to(rows[j % SUBLANES:j % SUBLANES + 1, p * LANES:(p + 1) * LANES], (HEAD_DIM, LANES))

            step = lambda ref, j, p: bcast(rows8(ref, j), j, p)
            for q in range(SUBLANES):
                j1, j2 = 2 * q, 2 * q + 1
                t1 = base + j1
                nxt = []
                for p in range(4):
                    sl = slice(p * HEAD_DIM, (p + 1) * HEAD_DIM)
                    sp = s[sl]
                    lhs = [halves(jnp.concatenate([sp * step(a16, j1, p),
                                                   sp * step(ca2p, j1, p) + qp * step(da2p, j1, p),
                                                   sp * step(c1p, j1, p) + qp * step(d1p, j1, p),
                                                   sp * step(c2p, j1, p) + qp * step(d2p, j1, p)], axis=0))]
                    for j in (j1, j2):
                        v8 = rows8(v16, j)
                        vh8 = v8.astype(bf16).astype(f32)
                        lhs.append(jnp.concatenate([(qp * bcast(vh8, j, p)).astype(bf16),
                                                    (qp * bcast(v8 - vh8, j, p)).astype(bf16)], axis=1))
                    r = jnp.dot(jnp.concatenate(lhs, axis=0), bbv, preferred_element_type=f32)
                    sa1, sa2, y1, y2, vb1, vb2 = (r[n * HEAD_DIM:(n + 1) * HEAD_DIM] for n in range(6))
                    yacc[sl, :] = jnp.where(lane == t1, y1, jnp.where(lane == t1 + 1, y2, yacc[sl, :]))
                    st_ref[base // 2 + q, sl, :] = sp
                    sa_ref[t1, sl, :] = sa1
                    sa_ref[t1 + 1, sl, :] = sa2
                    vb_ref[t1, sl, :] = vb1
                    vb_ref[t1 + 1, sl, :] = vb2
                    nxt.append(((sp * step(w12p, j1, p) + sa1 * step(b1wp, j1, p)) + vb1 * step(k1wp, j1, p))
                               + (sa2 * step(b16, j2, p) + vb2 * step(k16, j2, p)))
                s = jnp.concatenate(nxt, axis=0)
            return s

        s_scr[...] = lax.fori_loop(0, tc // PAIR_GROUP, group, s_scr[...])
        _store_cols(yacc, y_ref, tc)

    row = pl.BlockSpec((tc, D_R), lambda c: (c, 0))
    tiles = pl.BlockSpec((tc, PAIR_ROWS, LANES), lambda c: (c, 0, 0))
    return _pc(body, name="wkv_fwd", grid=(t // tc,),
               in_specs=[row] * 13 + [_full(bb.shape), _full(qsel.shape)],
               out_specs=(row, pl.BlockSpec((tc // 2, PAIR_ROWS, LANES), lambda c: (c, 0, 0)), tiles, tiles),
               out_shape=(S((t, D_R), f32), S((t // 2, PAIR_ROWS, LANES), f32)) + (S((t, PAIR_ROWS, LANES), f32),) * 2,
               scratch_shapes=[pltpu.VMEM((PAIR_ROWS, LANES), f32), pltpu.VMEM((PAIR_ROWS, LANES), f32)],
               compiler_params=_cparams(("arbitrary",)))(k, v, a, b, *pairs, bb, qsel)


def _wkv_bwd(sprev, sab, vbb, w, k, a, b, r, dy):
    t = w.shape[0]
    tc = SCAN_CHUNK
    nc = t // tc
    bb, qsel = _scan_consts()

    def body(st_ref, sa_ref, vb_ref, w_ref, k_ref, a_ref, b_ref, r_ref, dy_ref, bb_ref, q_ref,
             dr_ref, dw_ref, dk_ref, dv_ref, da_ref, db_ref, g_scr, dvacc, rows_scr):
        @pl.when(pl.program_id(0) == 0)
        def _():
            g_scr[...] = jnp.zeros_like(g_scr)
        bbv, qv = bb_ref[...], q_ref[...]
        lane64 = lax.broadcasted_iota(jnp.int32, (PAIR_ROWS, LANES), 1) % HEAD_DIM
        outs = (dr_ref, dw_ref, db_ref, dk_ref, da_ref)

        def colsums(slot, j, x):
            for p in range(4):
                rows_scr[slot, j:j + 1, p * LANES:(p + 1) * LANES] = jnp.sum(x[p * HEAD_DIM:(p + 1) * HEAD_DIM], axis=0,
                                                                           keepdims=True)

        def group(i, g):
            base = pl.multiple_of((tc // SUBLANES - 1 - i) * SUBLANES, SUBLANES)
            w8, k8, a8, b8, r8, dy8 = (ref[pl.ds(base, SUBLANES), :] for ref in (w_ref, k_ref, a_ref, b_ref, r_ref, dy_ref))
            dyh8 = dy8.astype(bf16).astype(f32)
            dyl8 = dy8 - dyh8

            def after_step(j, sp):
                return sp * _row4(w8, j) + sa_ref[base + j] * _row4(b8, j) + vb_ref[base + j] * _row4(k8, j)

            def back_step(j, sp, s_t, g):
                tt = base + j
                u, vb = sa_ref[tt], vb_ref[tt]
                a4, b4, w4, k4 = _row4(a8, j), _row4(b8, j), _row4(w8, j), _row4(k8, j)
                dyb = jnp.dot(jnp.concatenate([(qv * _row4(dyh8, j)).astype(bf16), (qv * _row4(dyl8, j)).astype(bf16)], axis=1),
                              bbv, preferred_element_type=f32)
                g = g + dyb * _row4(r8, j)
                rr2 = _seg2(jnp.concatenate([g * b4, g * k4], axis=0), bbv)
                du, dvb = rr2[0:PAIR_ROWS], rr2[PAIR_ROWS:2 * PAIR_ROWS]
                for slot, val in enumerate((s_t * dyb, g * sp, g * u, g * vb, sp * du)):
                    colsums(slot, j, val)
                dvacc[...] = jnp.where(lane64 == tt, dvb, dvacc[...])
                return g * w4 + du * a4

            for q in reversed(range(SUBLANES // 2)):
                s0 = st_ref[base // 2 + q]
                s1 = after_step(2 * q, s0)
                g = back_step(2 * q + 1, s1, after_step(2 * q + 1, s1), g)
                g = back_step(2 * q, s0, s1, g)
            for slot, ref in enumerate(outs):
                ref[pl.ds(base, SUBLANES), :] = rows_scr[slot]
            return g

        g_scr[...] = lax.fori_loop(0, tc // SUBLANES, group, g_scr[...])
        _store_cols(dvacc, dv_ref, tc)

    row = pl.BlockSpec((tc, D_R), lambda c: (nc - 1 - c, 0))
    tiles = pl.BlockSpec((tc, PAIR_ROWS, LANES), lambda c: (nc - 1 - c, 0, 0))
    states = pl.BlockSpec((tc // 2, PAIR_ROWS, LANES), lambda c: (nc - 1 - c, 0, 0))
    return _pc(body, name="wkv_bwd", grid=(nc,),
               in_specs=[states, tiles, tiles] + [row] * 6 + [_full(bb.shape), _full(qsel.shape)],
               out_specs=(row,) * 6, out_shape=(S((t, D_R), f32),) * 6,
               scratch_shapes=[pltpu.VMEM((PAIR_ROWS, LANES), f32), pltpu.VMEM((PAIR_ROWS, LANES), f32),
                               pltpu.VMEM((5, SUBLANES, D_R), f32)],
               compiler_params=_cparams(("arbitrary",)))(sprev, sab, vbb, w, k, a, b, r, dy, bb, qsel)


def _rope_tables(t):
    half = HEAD_DIM // 2
    inv = ROPE_THETA ** (-jnp.arange(half, dtype=f32) / half)
    ang = jnp.arange(t, dtype=f32)[:, None] * inv[None, :]
    cos, sin = jnp.cos(ang), jnp.sin(ang)
    return jnp.concatenate([cos, cos], axis=1), jnp.concatenate([-sin, sin], axis=1)


def _head_matrix():
    grp = jnp.arange(D_R) // HEAD_DIM
    b = (grp[:, None] == grp[None, :]).astype(bf16)
    return jnp.concatenate([b, b], axis=0)


def _ffn_fwd(h, g, get_w, conv_w, conv_b, i):
    hf = _rms_fwd(h, g, f"ffn{i}_norm")
    w_up_t = get_w(f"ff{i}_up", hf)
    u = _mm(hf, w_up_t, "nt", f"ffn{i}_up")
    z = _ffn_mid(u, conv_w, conv_b, f"ffn{i}_mid")
    w_down = get_w(f"ff{i}_down", z)
    return _mm(z, w_down, "nn", f"ffn{i}_down", res=h), (hf, u, z), w_up_t, w_down


def _ffn_bwd(dh, h, saved, g, w_up_t, conv_w, conv_b, w_down, i, put_g):
    hf, u, z = saved
    dz = _mm(dh, w_down, "nt", f"ffn{i}_dz")
    g_down = _mm(z, dh, "tn", f"ffn{i}_gdown", out_dtype=GRAD_WIRE_DTYPE)
    tok = put_g(f"ff{i}_down", g_down)
    dgate, dval, g_conv, g_convb = _ffn_mid_bwd(dz, u, conv_w, conv_b + tok, f"ffn{i}_mid_bwd")
    g_up_t = _mm(dgate, hf, "tn", f"ffn{i}_gup_gate", out_dtype=GRAD_WIRE_DTYPE, out_rows=2 * D_FF)
    g_up_t = _mm(dval, hf, "tn", f"ffn{i}_gup_val", out_dtype=GRAD_WIRE_DTYPE, out_rows=2 * D_FF, out_row0=D_FF, into=g_up_t)
    tok = put_g(f"ff{i}_up", g_up_t)
    dhf = _mm(dval, w_up_t, "nn", f"ffn{i}_dhf_val", b_row0=D_FF, res=_mm(dgate, w_up_t, "nn", f"ffn{i}_dhf_gate"))
    dh_in, g_norm = _rms_bwd(dhf, h, g + tok, dh, f"ffn{i}_norm_bwd")
    return dh_in, dict(conv=g_conv, conv_b=g_convb, norm=g_norm)


def _local_step(x, target, W, get_w, put_g, put_small, tok0):
    t = N_META + x.shape[0]
    c64, s64 = _rope_tables(t)
    bm = _head_matrix()
    h0 = jnp.concatenate([W["meta_tokens"], x], axis=0)

    ev_w_in_t = get_w("ev_in", None)
    hn0 = _rms_fwd(h0, W["norm_mix"][0] + tok0, "mix0_norm")
    p0 = _mm(hn0, ev_w_in_t, "nt", "ev_in")
    uc = _ev_a_conv(p0, W["ev_conv_a"])
    y0 = _ev_b(p0, W["ev_conv_b"], _ev_a_norm(uc, W["ev_ln_a_g"], W["ev_ln_a_b"]))
    ev_w_out = get_w("ev_out", y0)
    h1 = _mm(y0, ev_w_out, "nn", "ev_out", res=h0)
    h2, ffn0, ff0_up_t, ff0_down = _ffn_fwd(h1, W["norm_ffn"][0], get_w, W["ff_conv"][0], W["ff_conv_b"][0], 0)

    hn1 = _rms_fwd(h2, W["norm_mix"][1], "mix1_norm")
    od_w_in_t = get_w("od_in", hn1)
    w_att, w_rwkv = od_w_in_t[:ATT_COLS], od_w_in_t[ATT_COLS:]
    pr = _mm(hn1, w_rwkv, "nt", "od_in_rwkv")
    qp, kp, vp = _rope_pack(_mm(hn1, w_att, "nt", "od_in_att"), c64, s64)
    op = _attn_fwd(qp, kp, vp, W["od_sinks"])
    prep_params = [W[n] for n in _PREP_PARAMS]
    xr, xv, decay, k2, a_s, b_s, wr, br, kr, gate = _rwkv_prep(pr, W["od_mu"], prep_params, bm)
    pairs = _rwkv_pairs(decay, a_s, b_s, k2, xv, wr, br, kr, bm)
    ysc, sprev, sab, vbb = _wkv_fwd(k2, xv, a_s, b_s, pairs)
    rk = W["od_r_k"].reshape(1, D_R)
    yr = _rwkv_post(ysc, xr, k2, xv, gate, W["od_lnx_g"], W["od_lnx_b"], rk, bm)
    y1 = jnp.concatenate([op[ATT_PAD:], yr.astype(bf16)], axis=1)
    od_w_out = get_w("od_out", y1)
    h3 = _mm(y1, od_w_out, "nn", "od_out", res=h2)
    h4, ffn1, ff1_up_t, ff1_down = _ffn_fwd(h3, W["norm_ffn"][1], get_w, W["ff_conv"][1], W["ff_conv_b"][1], 1)

    tgt = jnp.concatenate([jnp.zeros((N_META, D_MODEL), f32), target], axis=0)
    loss, dh4, g_norm_final = _final_loss(h4, W["norm_final"], tgt)

    dh3, gf1 = _ffn_bwd(dh4, h3, ffn1, W["norm_ffn"][1], ff1_up_t, W["ff_conv"][1], W["ff_conv_b"][1], ff1_down, 1, put_g)
    dy1 = _mm(dh3, od_w_out, "nt", "od_dy")
    g_od_w_out = _mm(y1, dh3, "tn", "od_gout", out_dtype=GRAD_WIRE_DTYPE)
    tok = put_g("od_out", g_od_w_out)
    dysc, dxr_p, dk2_p, dxv_p, dgate, g_lnx_g, g_lnx_b, g_rk = _rwkv_post_bwd(
        dy1, ysc, xr, k2, xv, gate, W["od_lnx_g"], W["od_lnx_b"] + tok, rk, bm)
    dr, dw, dk, dv, da, db = _wkv_bwd(sprev, sab, vbb, decay, k2, a_s, b_s, xr, dysc)
    prep_grads = _rwkv_prep_bwd(pr, W["od_mu"], prep_params, bm,
                                [[dw], [dk, dk2_p], [da], [db], [dgate], [dr, dxr_p], [dv, dxv_p]])
    dxs, g_mu = prep_grads[0], prep_grads[1]
    dpr = _shift_bwd(dxs, W["od_mu"])
    dop = jnp.concatenate([jnp.zeros((ATT_PAD, D_ATT), f32), dy1[:, :D_ATT]], axis=0).astype(bf16)
    dqp, dkp, dvp, dsk = _attn_bwd(qp, kp, vp, W["od_sinks"], dop)
    dpatt = _rope_bwd(dqp, dkp, dvp, c64, s64)
    n_in = ATT_COLS + RWKV_COLS
    g_od_w_in_t = _mm(dpatt, hn1, "tn", "od_gin_att", out_dtype=GRAD_WIRE_DTYPE, out_rows=n_in)
    g_od_w_in_t = _mm(dpr, hn1, "tn", "od_gin_rwkv", out_dtype=GRAD_WIRE_DTYPE, out_rows=n_in, out_row0=ATT_COLS, into=g_od_w_in_t)
    tok = put_g("od_in", g_od_w_in_t)
    dhn1 = _mm(dpr, w_rwkv, "nn", "od_dhn_rwkv", res=_mm(dpatt, w_att, "nn", "od_dhn_att"))
    dh2, g_norm_mix1 = _rms_bwd(dhn1, h2, W["norm_mix"][1] + tok, dh3, "mix1_norm_bwd")

    dh1, gf0 = _ffn_bwd(dh2, h1, ffn0, W["norm_ffn"][0], ff0_up_t, W["ff_conv"][0], W["ff_conv_b"][0], ff0_down, 0, put_g)
    early = dict(
        norm_ffn=jnp.concatenate([gf0["norm"], gf1["norm"]], axis=0), norm_final=g_norm_final.reshape(D_MODEL),
        od_sinks=dsk[:, :N_Q_HEADS], od_mu=g_mu, od_lnx_g=g_lnx_g, od_lnx_b=g_lnx_b, od_r_k=g_rk.reshape(N_Q_HEADS, HEAD_DIM),
        ff_conv=jnp.stack([gf0["conv"], gf1["conv"]]), ff_conv_b=jnp.concatenate([gf0["conv_b"], gf1["conv_b"]], axis=0),
        **dict(zip(_PREP_PARAMS, prep_grads[2:])))
    dy0 = _mm(dh1, ev_w_out, "nt", "ev_dy")
    g_ev_w_out = _mm(y0, dh1, "tn", "ev_gout", out_dtype=GRAD_WIRE_DTYPE)
    tok = put_g("ev_out", g_ev_w_out) + put_small(early)
    duc, g_ln_g, g_ln_b = _ev_a_norm_bwd(dy0, uc, W["ev_ln_a_g"], W["ev_ln_a_b"] + tok)
    dav, dag, g_conv_a = _ev_a_conv_bwd(duc, p0, W["ev_conv_a"])
    dgb, dgc, dxi, g_conv_b = _ev_b_bwd(dy0, p0, W["ev_conv_b"])
    dp0 = jnp.concatenate([dav, dag, dgb, dgc, dxi], axis=1)
    g_ev_w_in_t = _mm(dp0, hn0, "tn", "ev_gin", out_dtype=GRAD_WIRE_DTYPE)
    tok = put_g("ev_in", g_ev_w_in_t)
    dhn0 = _mm(dp0, ev_w_in_t, "nn", "ev_dhn")
    dh0, g_norm_mix0 = _rms_bwd(dhn0, h0, W["norm_mix"][0] + tok, dh1, "mix0_norm_bwd")

    late = dict(meta_tokens=dh0[:N_META], norm_mix=jnp.concatenate([g_norm_mix0, g_norm_mix1], axis=0),
                ev_conv_a=g_conv_a, ev_ln_a_g=g_ln_g, ev_ln_a_b=g_ln_b, ev_conv_b=g_conv_b)
    return loss, dh0[N_META:], late


HBM = pl.BlockSpec(memory_space=pl.ANY)


def _mesh_pos():
    return lax.axis_index("x"), lax.axis_index("y"), lax.axis_index("c")


def _dev(px, py, pc):
    return 4 * px + 2 * py + pc


def _all_gather(xs, name):
    n = len(xs)

    def body(*refs):
        x_refs, o_refs = refs[:n], refs[n:2 * n]
        send_sems, recv_sems, local_sems = refs[2 * n:]
        x, y, c = _mesh_pos()
        me, sibling = (x, y, c), (x, y, 1 - c)
        chips = [(1 - x, y), (x, 1 - y), (1 - x, 1 - y)]

        def copy(i, k, block, to, from_input=False):
            dst = o_refs[i].at[_dev(*block)]
            return pltpu.make_async_remote_copy(src_ref=x_refs[i] if from_input else dst, dst_ref=dst,
                                                send_sem=send_sems.at[i, k], recv_sem=recv_sems.at[i, k],
                                                device_id=to, device_id_type=MESH)

        mine = [pltpu.make_async_copy(x_refs[i], o_refs[i].at[_dev(*me)], local_sems.at[i]) for i in range(n)]
        for cp in mine:
            cp.start()
        first = []
        for i in range(n):
            first.append(copy(i, 0, me, sibling, True))
            first += [copy(i, 1 + j, me, (*chip, c), True) for j, chip in enumerate(chips)]
        for cp in first:
            cp.start()
        passed = []
        for j, chip in enumerate(chips):
            for i in range(n):
                copy(i, 1 + j, (*chip, c), me).wait_recv()
                fwd = copy(i, 4 + j, (*chip, c), sibling)
                fwd.start()
                passed.append(fwd)
        for i in range(n):
            copy(i, 0, sibling, me).wait_recv()
            for j, chip in enumerate(chips):
                copy(i, 4 + j, (*chip, 1 - c), me).wait_recv()
        for cp in first + passed:
            cp.wait_send()
        for cp in mine:
            cp.wait()

    return _pc(body, name=name, in_specs=[HBM] * n, out_specs=tuple([HBM] * n),
               out_shape=tuple(S((N_DEV,) + x.shape, x.dtype) for x in xs),
               scratch_shapes=[pltpu.SemaphoreType.DMA((n, 7)), pltpu.SemaphoreType.DMA((n, 7)),
                               pltpu.SemaphoreType.DMA((n,))])(*xs)


HBM_SPEC = pl.BlockSpec(memory_space=pltpu.HBM)
SEM_SPEC = pl.BlockSpec(memory_space=pltpu.SEMAPHORE)
DATAFLOW = pltpu.SideEffectType.DATAFLOW_SIDE_EFFECTING
_PEER_FLIPS = ((1, 0, 0), (0, 1, 0), (1, 1, 0), (1, 0, 1), (0, 1, 1), (1, 1, 1), (0, 0, 1))
N_PEERS = len(_PEER_FLIPS)


def _peers(x, y, c):
    return [((1 - x) if fx else x, (1 - y) if fy else y, (1 - c) if fc else c) for fx, fy, fc in _PEER_FLIPS]


def _xchg_start(srcs, lands, scatter, name):
    n = len(srcs)

    def body(*refs):
        src_refs, land_refs = refs[:n], refs[n:2 * n]
        send_sems, recv_sems, token = refs[2 * n], refs[2 * n + 1], refs[-1]
        x, y, c = _mesh_pos()
        me = _dev(x, y, c)
        for i in range(n):
            for k, peer in enumerate(_peers(x, y, c)):
                pltpu.make_async_remote_copy(src_ref=src_refs[i].at[_dev(*peer)] if scatter else src_refs[i],
                                             dst_ref=land_refs[i].at[me], send_sem=send_sems.at[i * N_PEERS + k],
                                             recv_sem=recv_sems.at[i * N_PEERS + k], device_id=peer, device_id_type=MESH).start()
        token[...] = jnp.zeros_like(token)

    arrs = list(srcs) + list(lands)
    outs = _pc(body, name=name,
               out_shape=(pltpu.SemaphoreType.DMA((n * N_PEERS,)), pltpu.SemaphoreType.DMA((n * N_PEERS,)),
                          *[pltpu.HBM(a.shape, a.dtype) for a in arrs], S((SUBLANES, LANES), f32)),
               in_specs=[HBM_SPEC] * (2 * n),
               out_specs=(SEM_SPEC, SEM_SPEC, *[HBM_SPEC] * (2 * n), pl.BlockSpec(memory_space=pltpu.VMEM)),
               input_output_aliases={i: 2 + i for i in range(2 * n)},
               compiler_params=pltpu.CompilerParams(has_side_effects=DATAFLOW))(
        *[pltpu.with_memory_space_constraint(a, pltpu.HBM) for a in arrs])
    return (outs[0], outs[1], list(outs[2:2 + n]), list(outs[2 + n:2 + 2 * n]), scatter), outs[-1]


def _xchg_wait(handle, after, name):
    send_sems, recv_sems, srcs, lands, scatter = handle
    n = len(srcs)

    def body(*refs):
        src_refs, land_refs = refs[:n], refs[n:2 * n]
        send, recv = refs[2 * n], refs[2 * n + 1]
        x, y, c = _mesh_pos()
        for i in range(n):
            for k in range(N_PEERS):
                cp = pltpu.make_async_remote_copy(src_ref=src_refs[i].at[0] if scatter else src_refs[i],
                                                  dst_ref=land_refs[i].at[0], send_sem=send.at[i * N_PEERS + k],
                                                  recv_sem=recv.at[i * N_PEERS + k],
                                                  device_id=(x, y, c), device_id_type=MESH)
                cp.wait_send()
                cp.wait_recv()

    arrs = srcs + lands
    outs = _pc(body, name=name, out_shape=tuple(pltpu.HBM(a.shape, a.dtype) for a in arrs),
               in_specs=[HBM_SPEC] * (2 * n) + [SEM_SPEC, SEM_SPEC, pl.BlockSpec(memory_space=pl.ANY)],
               out_specs=tuple([HBM_SPEC] * (2 * n)), input_output_aliases={i: i for i in range(2 * n)},
               compiler_params=pltpu.CompilerParams(has_side_effects=DATAFLOW))(*arrs, send_sems, recv_sems, after)
    return list(outs[:n]), list(outs[n:])


def _rs_sum(g, land, me_vec, name):
    _, r, cols = g.shape
    tr = _divisor_block(r, 16, min(r, 352))

    def body(me_ref, g_ref, *rest):
        o_ref = rest[-1]
        acc = g_ref[0].astype(f32)
        for l_ref in rest[:-1]:
            acc = acc + l_ref[0].astype(f32)
        o_ref[...] = acc

    blk = lambda f: pl.BlockSpec((1, tr, cols), f)
    grid_spec = pltpu.PrefetchScalarGridSpec(
        num_scalar_prefetch=1, grid=(r // tr,),
        in_specs=[blk(lambda i, me_ref: (me_ref[0], i, 0))]
        + [blk(lambda i, me_ref, k=k: ((me_ref[0] + k) % N_DEV, i, 0)) for k in range(1, N_DEV)],
        out_specs=pl.BlockSpec((tr, cols), lambda i, me_ref: (i, 0)))
    return _pc(body, name=name, grid_spec=grid_spec, out_shape=S((r, cols), f32),
               compiler_params=_cparams(("arbitrary",)))(me_vec, g, *([land] * (N_DEV - 1)))


def _sum_devices(a, name):
    def body(a_ref, o_ref):
        acc = a_ref[0]
        for d in range(1, N_DEV):
            acc = acc + a_ref[d]
        o_ref[...] = acc

    return _pc(body, name=name, grid=(1,), in_specs=[_full(a.shape)], out_specs=_full(a.shape[1:]),
               out_shape=S(a.shape[1:], a.dtype), compiler_params=_cparams(("arbitrary",)))(a)


def _adamw(w, m, v, g, name):
    shape = w.shape
    w2, m2, v2, g2 = (a.reshape(-1, shape[-1]) for a in (w, m, v, g))
    rows, cols = w2.shape
    tr = rows if rows % SUBLANES else _divisor_block(rows, SUBLANES, max(SUBLANES, min(rows, ADAMW_BLOCK_ELEMS // cols)))
    c1, c2 = 1.0 - ADAM_B1 ** ADAM_STEP, 1.0 - ADAM_B2 ** ADAM_STEP

    def body(w_ref, m_ref, v_ref, g_ref, d_ref, nm_ref, nv_ref):
        gv = g_ref[...]
        nm = ADAM_B1 * m_ref[...] + (1.0 - ADAM_B1) * gv
        nv = ADAM_B2 * v_ref[...] + (1.0 - ADAM_B2) * (gv * gv)
        d_ref[...] = -ADAM_LR * ((nm / c1) / (jnp.sqrt(nv / c2) + ADAM_EPS) + ADAM_WD * w_ref[...])
        nm_ref[...] = nm
        nv_ref[...] = nv

    blk = pl.BlockSpec((tr, cols), lambda i: (i, 0))
    outs = _pc(body, name=name, grid=(rows // tr,), in_specs=[blk] * 4, out_specs=(blk,) * 3,
               out_shape=(S((rows, cols), f32),) * 3, compiler_params=_cparams(("arbitrary",)))(w2, m2, v2, g2)
    return tuple(o.reshape(shape) for o in outs)


_WEIGHTS = ("meta_tokens", "norm_mix", "norm_ffn", "norm_final", "ev_w_in", "ev_conv_a", "ev_ln_a_g", "ev_ln_a_b",
            "ev_conv_b", "ev_w_out", "od_w_in", "od_sinks", "od_mu", "od_w0", "od_w2", "od_a0", "od_a2", "od_g2",
            "od_k_k", "od_k_a", "od_r_k", "od_lnx_g", "od_lnx_b", "od_w_out", "ff_w_up", "ff_conv", "ff_conv_b", "ff_w_down")
_SMALL_SHARDED = (("meta_tokens", 1), ("ev_conv_a", 2), ("ev_conv_b", 2), ("od_mu", 1), ("od_w0", 1), ("od_w2", 2),
                  ("od_a0", 1), ("od_a2", 2), ("od_g2", 2), ("od_k_k", 1), ("od_k_a", 1), ("od_lnx_g", 1),
                  ("od_lnx_b", 1), ("ff_conv", 2))
_SMALL_REPLICATED = ("norm_mix", "norm_ffn", "norm_final", "ev_ln_a_g", "ev_ln_a_b", "od_sinks", "od_r_k", "ff_conv_b")
SLAB_UNIT = SUBLANES * LANES


def _pack(arrs):
    flat = jnp.concatenate([a.reshape(-1).astype(f32) for a in arrs])
    pad = (-flat.shape[0]) % SLAB_UNIT
    return jnp.pad(flat, (0, pad)).reshape(-1, LANES)


def _unpack(flat, shapes):
    out, off = [], 0
    for shp in shapes:
        size = 1
        for s in shp:
            size *= s
        out.append(flat[..., off:off + size].reshape(flat.shape[:-1] + tuple(shp)))
        off += size
    return out


def _full_shape(shape, axis):
    return tuple(N_DEV * s if i == axis else s for i, s in enumerate(shape))


def kernel(x, meta_tokens, norm_mix, norm_ffn, norm_final, ev_w_in, ev_conv_a, ev_ln_a_g, ev_ln_a_b, ev_conv_b, ev_w_out, od_w_in, od_sinks, od_mu, od_w0, od_w2, od_a0, od_a2, od_g2, od_k_k, od_k_a, od_r_k, od_lnx_g, od_lnx_b, od_w_out, ff_w_up, ff_conv, ff_conv_b, ff_w_down, loss_target, m_meta_tokens, m_norm_mix, m_norm_ffn, m_norm_final, m_ev_w_in, m_ev_conv_a, m_ev_ln_a_g, m_ev_ln_a_b, m_ev_conv_b, m_ev_w_out, m_od_w_in, m_od_sinks, m_od_mu, m_od_w0, m_od_w2, m_od_a0, m_od_a2, m_od_g2, m_od_k_k, m_od_k_a, m_od_r_k, m_od_lnx_g, m_od_lnx_b, m_od_w_out, m_ff_w_up, m_ff_conv, m_ff_conv_b, m_ff_w_down, v_meta_tokens, v_norm_mix, v_norm_ffn, v_norm_final, v_ev_w_in, v_ev_conv_a, v_ev_ln_a_g, v_ev_ln_a_b, v_ev_conv_b, v_ev_w_out, v_od_w_in, v_od_sinks, v_od_mu, v_od_w0, v_od_w2, v_od_a0, v_od_a2, v_od_g2, v_od_k_k, v_od_k_a, v_od_r_k, v_od_lnx_g, v_od_lnx_b, v_od_w_out, v_ff_w_up, v_ff_conv, v_ff_conv_b, v_ff_w_down):
    A = dict(locals())
    px, py, pc = _mesh_pos()
    me = _dev(px, py, pc)
    me_vec = jnp.reshape(me, (1,)).astype(jnp.int32)
    rows = lambda a: a.reshape(N_DEV * a.shape[1], a.shape[2])
    blocks = lambda a: a.reshape(N_DEV, a.shape[0] // N_DEV, a.shape[1])

    shards = dict(ev_in=ev_w_in[0].T, ev_out=ev_w_out[0], ff0_up=ff_w_up[0].T, ff0_down=ff_w_down[0], od_in=od_w_in[0].T,
                  od_out=od_w_out[0], ff1_up=ff_w_up[1].T, ff1_down=ff_w_down[1])
    shards = {n: b.astype(bf16) for n, b in shards.items()}
    small_shapes = [A[n].shape for n, _ in _SMALL_SHARDED]
    gathered = _all_gather([shards["ev_in"], _pack([A[n] for n, _ in _SMALL_SHARDED])], "gather_first")
    gathered, shards = lax.optimization_barrier((gathered, shards))
    fetch, tok0 = {}, jnp.zeros((), f32)
    for n in ("ev_out", "ff0_up", "ff0_down", "od_in", "od_out", "ff1_up", "ff1_down"):
        shard, tok0 = lax.optimization_barrier((shards[n], tok0))
        land = lax.dynamic_update_slice(lax.empty((N_DEV,) + shard.shape, bf16), shard[None], (me, 0, 0))
        fetch[n], token = _xchg_start([shard], [land], False, f"gather_{n}_start")
        tok0 = tok0 + token[0, 0]

    def get_w(n, after):
        if n == "ev_in":
            return rows(gathered[0])
        return rows(_xchg_wait(fetch[n], after, f"gather_{n}_wait")[1][0])

    W = {}
    for (n, ax), seg in zip(_SMALL_SHARDED, _unpack(gathered[-1].reshape(N_DEV, -1), small_shapes)):
        W[n] = jnp.moveaxis(seg, 0, ax).reshape(_full_shape(A[n].shape, ax))
    for n in ("ev_conv_a", "ev_conv_b", "od_w2", "od_a2", "od_g2"):
        W[n] = W[n][0]
    for n in _SMALL_REPLICATED:
        W[n] = A[n]
    W["od_r_k"] = od_r_k[0]

    small_shape = {n: _full_shape(A[n].shape, ax) for n, ax in _SMALL_SHARDED}
    small_shape.update({n: A[n].shape for n in _SMALL_REPLICATED})
    sent, small_sent, small_names = {}, {}, {}

    def put_g(n, g):
        g8 = blocks(g)
        sent[n], token = _xchg_start([g8], [lax.empty(g8.shape, g8.dtype)], True, f"reduce_{n}_start")
        return token[0, 0]

    def put_small(gs, stage="early"):
        small_names[stage] = sorted(gs)
        slab = _pack([gs[n] for n in small_names[stage]])
        land = lax.dynamic_update_slice(lax.empty((N_DEV,) + slab.shape, f32), slab[None], (me, 0, 0))
        small_sent[stage], small_tok[stage] = _xchg_start([slab], [land], False, f"gather_{stage}_small_grads_start")
        return small_tok[stage][0, 0]

    small_tok = {}
    loss_tile, grad_x, late = _local_step(x[0], loss_target[0], W, get_w, put_g, put_small, tok0)
    put_small(late, "late")
    late_tok = small_tok["late"]

    gsh, prev = {}, late_tok
    for n in ("ff1_down", "ff1_up", "od_out", "od_in", "ff0_down", "ff0_up", "ev_out", "ev_in"):
        srcs, lands = _xchg_wait(sent[n], prev, f"reduce_{n}_wait")
        gsh[n] = prev = _rs_sum(srcs[0], lands[0], me_vec, f"reduce_{n}_sum")
    grads = dict(ev_w_in=gsh["ev_in"].T[None], ev_w_out=gsh["ev_out"][None], od_w_in=gsh["od_in"].T[None],
                 od_w_out=gsh["od_out"][None], ff_w_up=jnp.stack([gsh["ff0_up"].T, gsh["ff1_up"].T]),
                 ff_w_down=jnp.stack([gsh["ff0_down"], gsh["ff1_down"]]))

    delta, new_m, new_v = {}, {}, {}
    for n in ("ff_w_up", "ff_w_down", "od_w_in", "od_w_out", "ev_w_in", "ev_w_out"):
        delta[n], new_m[n], new_v[n] = _adamw(A[n], A["m_" + n], A["v_" + n], grads[n], "adamw_" + n)
    for stage in ("early", "late"):
        gsm = _xchg_wait(small_sent[stage], delta["ev_w_in"], f"gather_{stage}_small_grads_wait")[1][0]
        summed = _sum_devices(gsm, f"sum_{stage}_small_grads").reshape(-1)
        for n, full in zip(small_names[stage], _unpack(summed, [small_shape[n] for n in small_names[stage]])):
            grads[n] = full
    for n, ax in _SMALL_SHARDED:
        size = A[n].shape[ax]
        grads[n] = lax.dynamic_slice_in_dim(grads[n], me * size, size, axis=ax)
    for n in small_shape:
        delta[n], new_m[n], new_v[n] = _adamw(A[n], A["m_" + n], A["v_" + n], grads[n], "adamw_" + n)

    loss = lax.psum(loss_tile[0, 0], ("x", "y", "c"))
    return (loss, grad_x[None], *[grads[n] for n in _WEIGHTS], *[delta[n] for n in _WEIGHTS],
            *[new_m[n] for n in _WEIGHTS], *[new_v[n] for n in _WEIGHTS])
```

```python
import jax
import jax.numpy as jnp
from jax import lax
from jax.experimental import pallas as pl
from jax.experimental.pallas import tpu as pltpu

f32, bf16 = jnp.float32, jnp.bfloat16

D_MODEL = 1024
N_META = 16
RMS_EPS = 1e-6
LN_EPS = 1e-5
D_A = 512
CONV_A_WIDTH = 31
CONV_B_WIDTH = 3
HEAD_DIM = 64
N_Q_HEADS = 8
N_KV_HEADS = 2
GQA_GROUP = 4
D_ATT = 512
D_KV = 128
BLOCK = 128
ROPE_THETA = 10000.0
D_R = 512
LORA_W, LORA_A, LORA_G = 64, 64, 128
RWKV_GN_EPS = 64e-5
ATT_COLS = D_ATT + 2 * D_KV
RWKV_COLS = 3 * D_R + LORA_W + LORA_A + LORA_G
D_FF = 2816
FF_CONV_WIDTH = 3
FF_BLOCK = 256
NEG_INF = -1e30
ATT_PAD = BLOCK - N_META
ATT_SCALE = HEAD_DIM ** -0.5

ADAM_LR, ADAM_B1, ADAM_B2, ADAM_EPS, ADAM_WD, ADAM_STEP = 0.001, 0.9, 0.999, 1e-08, 0.01, 10

N_DEV = 8
LANES = 128
SUBLANES = 8
SCAN_CHUNK = 48
PAIR_ROWS = 4 * HEAD_DIM
V7X_VMEM_LIMIT = 56 * 1024 * 1024
ADAMW_BLOCK_ELEMS = 400 * 1024
GRAD_WIRE_DTYPE = bf16
MESH = pl.DeviceIdType.MESH
S = jax.ShapeDtypeStruct
HIGHEST = lax.Precision.HIGHEST


def _pc(body, **kw):
    return pl.pallas_call(body, **kw)


def _cparams(sem=None):
    return pltpu.CompilerParams(dimension_semantics=sem, vmem_limit_bytes=V7X_VMEM_LIMIT)


def _divisor_block(t, unit, limit):
    best = unit
    for rb in range(unit, limit + 1, unit):
        if t % rb == 0:
            best = rb
    assert t % best == 0, (t, unit)
    return best


def _row_block(t):
    return _divisor_block(t, 16, 704)


def _row_block8(t):
    return _divisor_block(t, 8, 344)


def _col_tile(n, cap):
    return _divisor_block(n, LANES, min(n, cap)) if n % LANES == 0 else n


def _full(shape):
    nd = len(shape)
    return pl.BlockSpec(shape, lambda *_: (0,) * nd)


def _sigmoid(x):
    return jax.nn.sigmoid(x)


_DIMS = {"nn": (((1,), (0,)), ((), ())), "nt": (((1,), (1,)), ((), ())), "tn": (((0,), (0,)), ((), ()))}
MM_MAX_K = 2816
MM_MAX_TM = 704
MM_MAX_TN = 1408


def _mm(a, b, mode, name, out_dtype=f32, res=None, b_row0=0, out_rows=None, out_row0=0, into=None):
    if mode == "nn":
        (m, k), n, k2 = a.shape, b.shape[1], a.shape[1]
        assert b_row0 % k == 0 and b_row0 + k <= b.shape[0], (a.shape, b.shape, b_row0)
    elif mode == "nt":
        (m, k), (n, k2) = a.shape, b.shape
    else:
        (k, m), (k2, n) = a.shape, b.shape
    assert k == k2, (a.shape, b.shape, mode)
    tm = _row_block(m) if m % LANES else _col_tile(m, MM_MAX_TM)
    tn = _col_tile(n, MM_MAX_TN)
    nk = 1 if (mode == "tn" or k <= MM_MAX_K) else k // MM_MAX_K
    tk = k // nk
    assert tk * nk == k
    dims = _DIMS[mode]

    def body(a_ref, b_ref, *rest):
        part = lax.dot_general(a_ref[...].astype(bf16), b_ref[...].astype(bf16), dims, preferred_element_type=f32)
        if nk == 1:
            o_ref = rest[-1]
            if res is not None:
                part = part + rest[0][...]
            o_ref[...] = part.astype(out_dtype)
            return
        o_ref, acc_ref = rest[-2], rest[-1]
        kk = pl.program_id(2)

        @pl.when(kk == 0)
        def _():
            acc_ref[...] = part

        @pl.when(kk > 0)
        def _():
            acc_ref[...] += part

        @pl.when(kk == nk - 1)
        def _():
            acc = acc_ref[...]
            if res is not None:
                acc = acc + rest[0][...]
            o_ref[...] = acc.astype(out_dtype)

    if mode == "tn":
        a_spec = pl.BlockSpec((k, tm), lambda i, j, kk: (0, i))
    else:
        a_spec = pl.BlockSpec((tm, tk), lambda i, j, kk: (i, kk))
    if mode == "nt":
        b_spec = pl.BlockSpec((tn, tk), lambda i, j, kk: (j, kk))
    else:
        b_spec = pl.BlockSpec((tk, tn), lambda i, j, kk: (kk + b_row0 // tk, j))
    assert out_row0 % tm == 0 and res is None or out_row0 == 0
    o_spec = pl.BlockSpec((tm, tn), lambda i, j, kk: (i + out_row0 // tm, j))
    ins, specs, aliases = [a, b], [a_spec, b_spec], {}
    if res is not None:
        ins.append(res)
        specs.append(o_spec)
    if into is not None:
        assert into.shape == (out_rows, n) and into.dtype == out_dtype
        aliases = {len(ins): 0}
        ins.append(into)
        specs.append(pl.BlockSpec(memory_space=pl.ANY))
    scratch = [pltpu.VMEM((tm, tn), f32)] if nk > 1 else []
    return _pc(body, name=name, grid=(m // tm, n // tn, nk), in_specs=specs, out_specs=o_spec,
               out_shape=S((out_rows or m, n), out_dtype), scratch_shapes=scratch, input_output_aliases=aliases,
               compiler_params=_cparams(("arbitrary", "arbitrary", "arbitrary")))(*ins)


def _rms_fwd(x, g, name):
    t, d = x.shape
    rb = _row_block(t)

    def body(x_ref, g_ref, o_ref):
        xv = x_ref[...]
        rstd = lax.rsqrt(jnp.mean(xv * xv, axis=-1, keepdims=True) + RMS_EPS)
        o_ref[...] = (xv * rstd * g_ref[...]).astype(bf16)

    row = pl.BlockSpec((rb, d), lambda i: (i, 0))
    return _pc(body, name=name, grid=(t // rb,), in_specs=[row, _full((1, d))], out_specs=row,
               out_shape=S((t, d), bf16), compiler_params=_cparams(("arbitrary",)))(x, g.reshape(1, d))


def _mm_rms_bwd(a, b, x, g, dres, name, b_row0=0, res=None):
    (m, k), n = a.shape, b.shape[1]
    assert k <= MM_MAX_K and b_row0 % k == 0 and b_row0 + k <= b.shape[0] and x.shape == (m, n)
    tm = _row_block(m)

    def body(a_ref, b_ref, x_ref, g_ref, dres_ref, *rest):
        dx_ref, dg_ref = rest[-2], rest[-1]

        @pl.when(pl.program_id(0) == 0)
        def _():
            dg_ref[...] = jnp.zeros_like(dg_ref)
        dy = jnp.dot(a_ref[...].astype(bf16), b_ref[...].astype(bf16), preferred_element_type=f32)
        if res is not None:
            dy = dy + rest[0][...]
        xv = x_ref[...]
        rstd = lax.rsqrt(jnp.mean(xv * xv, axis=-1, keepdims=True) + RMS_EPS)
        xn = xv * rstd
        dg_ref[...] += jnp.sum(dy * xn, axis=0, keepdims=True)
        dxh = dy * g_ref[...]
        dx_ref[...] = dres_ref[...] + rstd * (dxh - xn * jnp.mean(dxh * xn, axis=-1, keepdims=True))

    row = pl.BlockSpec((tm, n), lambda i: (i, 0))
    ins = [a, b, x, g.reshape(1, n), dres] + ([res] if res is not None else [])
    specs = [pl.BlockSpec((tm, k), lambda i: (i, 0)), pl.BlockSpec((k, n), lambda i: (b_row0 // k, 0)), row, _full((1, n)), row]
    specs += [row] if res is not None else []
    return _pc(body, name=name, grid=(m // tm,), in_specs=specs, out_specs=(row, _full((1, n))),
               out_shape=(S((m, n), f32), S((1, n), f32)), compiler_params=_cparams(("arbitrary",)))(*ins)


def _final_loss(h, g, target_padded):
    t, d = h.shape
    rb = _row_block8(t)

    def body(x_ref, g_ref, t_ref, loss_ref, dx_ref, dg_ref):
        i = pl.program_id(0)

        @pl.when(i == 0)
        def _():
            dg_ref[...] = jnp.zeros_like(dg_ref)
            loss_ref[...] = jnp.zeros_like(loss_ref)
        xv = x_ref[...]
        rstd = lax.rsqrt(jnp.mean(xv * xv, axis=-1, keepdims=True) + RMS_EPS)
        xn = xv * rstd
        gv = g_ref[...]
        row = i * rb + lax.broadcasted_iota(jnp.int32, (rb, 1), 0)
        diff = jnp.where(row >= N_META, xn * gv - t_ref[...], 0.0)
        loss_ref[...] += 0.5 * jnp.sum(jnp.mean(diff * diff, axis=-1, keepdims=True))
        dout = diff * (1.0 / d)
        dg_ref[...] += jnp.sum(dout * xn, axis=0, keepdims=True)
        dxh = dout * gv
        dx_ref[...] = rstd * (dxh - xn * jnp.mean(dxh * xn, axis=-1, keepdims=True))

    row = pl.BlockSpec((rb, d), lambda i: (i, 0))
    return _pc(body, name="final_loss", grid=(t // rb,), in_specs=[row, _full((1, d)), row],
               out_specs=(_full((SUBLANES, LANES)), row, _full((1, d))),
               out_shape=(S((SUBLANES, LANES), f32), S((t, d), f32), S((1, d), f32)),
               compiler_params=_cparams(("arbitrary",)))(h, g.reshape(1, d), target_padded)


CONV_LEAD = 32


def _fill_front_padded(pad_ref, x, t):
    pad_ref[0:CONV_LEAD, :] = jnp.zeros((CONV_LEAD, x.shape[1]), f32)
    pad_ref[CONV_LEAD:CONV_LEAD + t, :] = x


def _fill_back_padded(pad_ref, x, t):
    pad_ref[0:t, :] = x
    pad_ref[t:t + CONV_LEAD, :] = jnp.zeros((CONV_LEAD, x.shape[1]), f32)


def _conv_rows(pad_ref, w_ref, kw, r0, nr):
    acc = None
    for j in range(kw):
        lo = CONV_LEAD + r0 - (kw - 1) + j
        term = w_ref[j:j + 1, :] * pad_ref[lo:lo + nr, :]
        acc = term if acc is None else acc + term
    return acc


def _conv_t_rows(padb_ref, w_ref, kw, r0, nr):
    acc = None
    for j in range(kw):
        lo = r0 + (kw - 1) - j
        term = w_ref[j:j + 1, :] * padb_ref[lo:lo + nr, :]
        acc = term if acc is None else acc + term
    return acc


def _conv_dw_rows(dy_blk, pad_ref, kw, r0, nr):
    out = []
    for j in range(kw):
        lo = CONV_LEAD + r0 - (kw - 1) + j
        out.append(jnp.sum(dy_blk * pad_ref[lo:lo + nr, :], axis=0, keepdims=True))
    return out


def _acc_list(a, b):
    return b if a is None else [x + y for x, y in zip(a, b)]


def _ev_a_conv(p, conv_a):
    t = p.shape[0]
    cr = _row_block8(t)
    nb = D_A // LANES

    def body(av_ref, ag_ref, w_ref, o_ref, pad_ref):
        _fill_front_padded(pad_ref, av_ref[...] * _sigmoid(ag_ref[...]), t)
        for r in range(t // cr):
            o_ref[r * cr:(r + 1) * cr, :] = _conv_rows(pad_ref, w_ref, CONV_A_WIDTH, r * cr, cr)

    col = lambda off: pl.BlockSpec((t, LANES), lambda j: (0, j + off))
    return _pc(body, name="ev_a_conv", grid=(nb,),
               in_specs=[col(0), col(nb), pl.BlockSpec((CONV_A_WIDTH, LANES), lambda j: (0, j))],
               out_specs=col(0), out_shape=S((t, D_A), f32),
               scratch_shapes=[pltpu.VMEM((t + CONV_LEAD, LANES), f32)],
               compiler_params=_cparams(("arbitrary",)))(p, p, conv_a)


def _ln_silu(uc, g, b):
    mu = jnp.mean(uc, axis=-1, keepdims=True)
    xc = uc - mu
    var = jnp.mean(xc * xc, axis=-1, keepdims=True)
    y = xc * lax.rsqrt(var + LN_EPS) * g + b
    return y * _sigmoid(y)


def _ev_a_norm(uc, g, b):
    t, d = uc.shape
    rb = _row_block(t)

    def body(u_ref, g_ref, b_ref, o_ref):
        o_ref[...] = _ln_silu(u_ref[...], g_ref[...], b_ref[...]).astype(bf16)

    row = pl.BlockSpec((rb, d), lambda i: (i, 0))
    return _pc(body, name="ev_a_norm", grid=(t // rb,), in_specs=[row, _full((1, d)), _full((1, d))],
               out_specs=row, out_shape=S((t, 2 * d), bf16), compiler_params=_cparams(("arbitrary",)))(uc, g, b)


def _ev_a_norm_bwd(dy, uc, g, b):
    t, d = uc.shape
    rb = _row_block8(t)

    def body(dy_ref, u_ref, g_ref, b_ref, du_ref, dg_ref, db_ref):
        @pl.when(pl.program_id(0) == 0)
        def _():
            dg_ref[...] = jnp.zeros_like(dg_ref)
            db_ref[...] = jnp.zeros_like(db_ref)
        _, vjp = jax.vjp(_ln_silu, u_ref[...], g_ref[...], b_ref[...])
        du, dg, db = vjp(dy_ref[...])
        du_ref[...] = du
        dg_ref[...] += dg
        db_ref[...] += db

    row = pl.BlockSpec((rb, d), lambda i: (i, 0))
    return _pc(body, name="ev_a_norm_bwd", grid=(t // rb,), in_specs=[row, row, _full((1, d)), _full((1, d))],
               out_specs=(row, _full((1, d)), _full((1, d))),
               out_shape=(S((t, d), f32), S((1, d), f32), S((1, d), f32)),
               compiler_params=_cparams(("arbitrary",)))(dy, uc, g, b)


def _ev_a_conv_bwd(duc, p, conv_a):
    t = p.shape[0]
    cr = _row_block8(t)
    nb = D_A // LANES

    def body(dy_ref, av_ref, ag_ref, w_ref, dav_ref, dag_ref, dw_ref, pad_ref, padb_ref):
        _fill_front_padded(pad_ref, av_ref[...] * _sigmoid(ag_ref[...]), t)
        _fill_back_padded(padb_ref, dy_ref[...], t)
        dw = None
        for r in range(t // cr):
            rows = slice(r * cr, (r + 1) * cr)
            du = _conv_t_rows(padb_ref, w_ref, CONV_A_WIDTH, r * cr, cr)
            avr = av_ref[rows, :]
            sgr = _sigmoid(ag_ref[rows, :])
            dav_ref[rows, :] = du * sgr
            dag_ref[rows, :] = du * avr * sgr * (1.0 - sgr)
            dw = _acc_list(dw, _conv_dw_rows(dy_ref[rows, :], pad_ref, CONV_A_WIDTH, r * cr, cr))
        for j in range(CONV_A_WIDTH):
            dw_ref[j:j + 1, :] = dw[j]

    col = lambda off: pl.BlockSpec((t, LANES), lambda j: (0, j + off))
    wsp = pl.BlockSpec((CONV_A_WIDTH, LANES), lambda j: (0, j))
    return _pc(body, name="ev_a_conv_bwd", grid=(nb,), in_specs=[col(0), col(0), col(nb), wsp],
               out_specs=(col(0), col(0), wsp),
               out_shape=(S((t, D_A), f32), S((t, D_A), f32), S((CONV_A_WIDTH, D_A), f32)),
               scratch_shapes=[pltpu.VMEM((t + CONV_LEAD, LANES), f32), pltpu.VMEM((t + CONV_LEAD, LANES), f32)],
               compiler_params=_cparams(("arbitrary",)))(duc, p, p, conv_a)


def _ev_b(p, conv_b, y):
    t = p.shape[0]
    cr = _row_block8(t)
    nb = D_A // LANES

    def body(gb_ref, gc_ref, xi_ref, w_ref, y_ref, o_ref, pad_ref, stage_ref):
        _fill_front_padded(pad_ref, gc_ref[...] * xi_ref[...], t)
        for r in range(t // cr):
            rows = slice(r * cr, (r + 1) * cr)
            stage_ref[rows, :] = gb_ref[rows, :] * _conv_rows(pad_ref, w_ref, CONV_B_WIDTH, r * cr, cr)
        o_ref[...] = stage_ref[...].astype(bf16)

    col = lambda off: pl.BlockSpec((t, LANES), lambda j: (0, j + off))
    return _pc(body, name="ev_b", grid=(nb,),
               in_specs=[col(2 * nb), col(3 * nb), col(4 * nb), pl.BlockSpec((CONV_B_WIDTH, LANES), lambda j: (0, j)), HBM],
               out_specs=col(nb), out_shape=S(y.shape, bf16), input_output_aliases={4: 0},
               scratch_shapes=[pltpu.VMEM((t + CONV_LEAD, LANES), f32), pltpu.VMEM((t, LANES), f32)],
               compiler_params=_cparams(("arbitrary",)))(p, p, p, conv_b, y)


def _ev_b_bwd(dy, p, conv_b):
    t = p.shape[0]
    cr = _row_block8(t)
    nb = D_A // LANES

    def body(dy_ref, gb_ref, gc_ref, xi_ref, w_ref, dgb_ref, dgc_ref, dxi_ref, dw_ref, pad_ref, padb_ref):
        _fill_front_padded(pad_ref, gc_ref[...] * xi_ref[...], t)
        _fill_back_padded(padb_ref, dy_ref[...] * gb_ref[...], t)
        dw = None
        for r in range(t // cr):
            rows = slice(r * cr, (r + 1) * cr)
            dgb_ref[rows, :] = dy_ref[rows, :] * _conv_rows(pad_ref, w_ref, CONV_B_WIDTH, r * cr, cr)
            dcx = _conv_t_rows(padb_ref, w_ref, CONV_B_WIDTH, r * cr, cr)
            dgc_ref[rows, :] = dcx * xi_ref[rows, :]
            dxi_ref[rows, :] = dcx * gc_ref[rows, :]
            dw = _acc_list(dw, _conv_dw_rows(padb_ref[rows, :], pad_ref, CONV_B_WIDTH, r * cr, cr))
        for j in range(CONV_B_WIDTH):
            dw_ref[j:j + 1, :] = dw[j]

    col = lambda off: pl.BlockSpec((t, LANES), lambda j: (0, j + off))
    wsp = pl.BlockSpec((CONV_B_WIDTH, LANES), lambda j: (0, j))
    return _pc(body, name="ev_b_bwd", grid=(nb,), in_specs=[col(nb), col(2 * nb), col(3 * nb), col(4 * nb), wsp],
               out_specs=(col(0), col(0), col(0), wsp),
               out_shape=(S((t, D_A), f32), S((t, D_A), f32), S((t, D_A), f32), S((CONV_B_WIDTH, D_A), f32)),
               scratch_shapes=[pltpu.VMEM((t + CONV_LEAD, LANES), f32), pltpu.VMEM((t + CONV_LEAD, LANES), f32)],
               compiler_params=_cparams(("arbitrary",)))(dy, p, p, p, conv_b)


def _ffn_mid(u, conv_w, conv_b, name):
    t = u.shape[0]
    cr = _row_block8(t)
    nb = D_FF // FF_BLOCK

    def one(gt_ref, vl_ref, w_ref, b_ref, o_ref, pad_ref, stage_ref):
        _fill_front_padded(pad_ref, gt_ref[...], t)
        for r in range(t // cr):
            rows = slice(r * cr, (r + 1) * cr)
            gc = _conv_rows(pad_ref, w_ref, FF_CONV_WIDTH, r * cr, cr) + b_ref[...]
            stage_ref[rows, :] = gc * _sigmoid(gc) * vl_ref[rows, :]
        o_ref[...] = stage_ref[...].astype(bf16)

    def body(*refs):
        for h in range(FF_BLOCK // LANES):
            one(*[r.at[:, pl.ds(h * LANES, LANES)] for r in refs[:5]], *refs[5:])

    col = lambda off: pl.BlockSpec((t, FF_BLOCK), lambda j: (0, j + off))
    return _pc(body, name=name, grid=(nb,),
               in_specs=[col(0), col(nb), pl.BlockSpec((FF_CONV_WIDTH, FF_BLOCK), lambda j: (0, j)),
                         pl.BlockSpec((1, FF_BLOCK), lambda j: (0, j))],
               out_specs=col(0), out_shape=S((t, D_FF), bf16),
               scratch_shapes=[pltpu.VMEM((t + CONV_LEAD, LANES), f32), pltpu.VMEM((t, LANES), f32)],
               compiler_params=_cparams(("arbitrary",)))(u, u, conv_w, conv_b.reshape(1, D_FF))


def _ffn_mid_bwd(dz, u, conv_w, conv_b, name):
    t = u.shape[0]
    cr = _row_block8(t)
    nb = D_FF // FF_BLOCK
    nh = FF_BLOCK // LANES

    def body(*refs):
        for h in range(nh):
            one(*[r.at[:, pl.ds(h * LANES, LANES)] for r in refs[:9]], *refs[9:])

    def one(dz_ref, gt_ref, vl_ref, w_ref, b_ref, du_ref, dv_ref, dw_ref, db_ref, pad_ref, padb_ref, stage_ref):
        _fill_front_padded(pad_ref, gt_ref[...], t)
        dw, db = None, None
        for r in range(t // cr):
            rows = slice(r * cr, (r + 1) * cr)
            lo = CONV_LEAD + r * cr - (FF_CONV_WIDTH - 1)
            taps = [pad_ref[lo + j:lo + j + cr, :] for j in range(FF_CONV_WIDTH)]
            gc = sum(w_ref[j:j + 1, :] * taps[j] for j in range(FF_CONV_WIDTH)) + b_ref[...]
            sg = _sigmoid(gc)
            dzr = dz_ref[rows, :]
            stage_ref[rows, :] = dzr * gc * sg
            dgc = dzr * vl_ref[rows, :] * sg * (1.0 + gc * (1.0 - sg))
            padb_ref[rows, :] = dgc
            dw = _acc_list(dw, [jnp.sum(dgc * tap, axis=0, keepdims=True) for tap in taps])
            pb = jnp.sum(dgc, axis=0, keepdims=True)
            db = pb if db is None else db + pb
        padb_ref[t:t + CONV_LEAD, :] = jnp.zeros((CONV_LEAD, LANES), f32)
        for r in range(t // cr):
            pad_ref[r * cr:(r + 1) * cr, :] = _conv_t_rows(padb_ref, w_ref, FF_CONV_WIDTH, r * cr, cr)
        du_ref[...] = pad_ref[0:t, :].astype(du_ref.dtype)
        dv_ref[...] = stage_ref[...].astype(dv_ref.dtype)
        for j in range(FF_CONV_WIDTH):
            dw_ref[j:j + 1, :] = dw[j]
        db_ref[...] = db

    col = lambda off: pl.BlockSpec((t, FF_BLOCK), lambda j: (0, j + off))
    wsp = pl.BlockSpec((FF_CONV_WIDTH, FF_BLOCK), lambda j: (0, j))
    bsp = pl.BlockSpec((1, FF_BLOCK), lambda j: (0, j))
    return _pc(body, name=name, grid=(nb,), in_specs=[col(0), col(0), col(nb), wsp, bsp],
               out_specs=(col(0), col(0), wsp, bsp),
               out_shape=(S((t, D_FF), bf16), S((t, D_FF), bf16), S((FF_CONV_WIDTH, D_FF), f32), S((1, D_FF), f32)),
               scratch_shapes=[pltpu.VMEM((t + CONV_LEAD, LANES), f32), pltpu.VMEM((t + CONV_LEAD, LANES), f32),
                               pltpu.VMEM((t, LANES), f32)],
               compiler_params=_cparams(("arbitrary",)))(dz, u, u, conv_w, conv_b.reshape(1, D_FF))


def _swap_halves(x):
    w = x.shape[1]
    lane = lax.broadcasted_iota(jnp.int32, x.shape, 1) % HEAD_DIM
    return jnp.where(lane < HEAD_DIM // 2, pltpu.roll(x, w - HEAD_DIM // 2, axis=1), pltpu.roll(x, HEAD_DIM // 2, axis=1))


def _rope_pack(patt, c64, s64):
    t = patt.shape[0]
    tp = t + ATT_PAD

    def body(p_ref, c_ref, s_ref, q_ref, k_ref, v_ref):
        c, s = c_ref[...], s_ref[...]

        def rope(x, nh):
            cc = jnp.concatenate([c] * nh, axis=1)
            ss = jnp.concatenate([s] * nh, axis=1)
            return x * cc + _swap_halves(x) * ss

        for ref, val in ((q_ref, rope(p_ref[:, 0:D_ATT], N_Q_HEADS)),
                         (k_ref, rope(p_ref[:, D_ATT:D_ATT + D_KV], N_KV_HEADS)),
                         (v_ref, p_ref[:, D_ATT + D_KV:ATT_COLS])):
            ref[0:ATT_PAD, :] = jnp.zeros((ATT_PAD, val.shape[1]), bf16)
            ref[ATT_PAD:tp, :] = val.astype(bf16)

    return _pc(body, name="rope_pack", in_specs=[_full((t, ATT_COLS)), _full((t, HEAD_DIM)), _full((t, HEAD_DIM))],
               out_specs=(_full((tp, D_ATT)), _full((tp, D_KV)), _full((tp, D_KV))), grid=(1,),
               out_shape=(S((tp, D_ATT), bf16), S((tp, D_KV), bf16), S((tp, D_KV), bf16)),
               compiler_params=_cparams(("arbitrary",)))(patt, c64, s64)


def _rope_bwd(dqp, dkp, dvp, c64, s64):
    tp = dqp.shape[0]
    t = tp - ATT_PAD

    def body(dq_ref, dk_ref, dv_ref, c_ref, s_ref, o_ref):
        c, s = c_ref[...], s_ref[...]

        def unrope(dy, nh):
            cc = jnp.concatenate([c] * nh, axis=1)
            ss = jnp.concatenate([s] * nh, axis=1)
            return dy * cc + _swap_halves(dy * ss)

        o_ref[:, 0:D_ATT] = unrope(dq_ref[ATT_PAD:tp, :], N_Q_HEADS).astype(bf16)
        o_ref[:, D_ATT:D_ATT + D_KV] = unrope(dk_ref[ATT_PAD:tp, :], N_KV_HEADS).astype(bf16)
        o_ref[:, D_ATT + D_KV:ATT_COLS] = dv_ref[ATT_PAD:tp, :].astype(bf16)

    return _pc(body, name="rope_bwd", grid=(1,),
               in_specs=[_full((tp, D_ATT)), _full((tp, D_KV)), _full((tp, D_KV)), _full((t, HEAD_DIM)), _full((t, HEAD_DIM))],
               out_specs=_full((t, ATT_COLS)), out_shape=S((t, ATT_COLS), bf16),
               compiler_params=_cparams(("arbitrary",)))(dqp, dkp, dvp, c64, s64)


def _attn_masks(n):
    rows = GQA_GROUP * BLOCK
    ri = lax.broadcasted_iota(jnp.int32, (rows, BLOCK), 0) % BLOCK
    ci = lax.broadcasted_iota(jnp.int32, (rows, BLOCK), 1)
    m_cur = (ci <= ri) & (ci >= jnp.where(n >= 1, 0, ATT_PAD))
    m_prev = ci > ri + jnp.where(n >= 2, 0, BLOCK)
    m_meta = ci >= jnp.where(n >= 1, ATT_PAD, BLOCK)
    return m_cur, m_prev, m_meta


def _attn_probs(qg, kc, kp, km, masks, skv):
    def scores(k, m):
        s = lax.dot_general(qg, k, _DIMS["nt"], preferred_element_type=f32) * ATT_SCALE
        return jnp.where(m, s, NEG_INF)
    s_c, s_p, s_m = scores(kc, masks[0]), scores(kp, masks[1]), scores(km, masks[2])
    mx = jnp.maximum(jnp.maximum(jnp.max(s_c, axis=-1, keepdims=True), jnp.max(s_p, axis=-1, keepdims=True)),
                     jnp.maximum(jnp.max(s_m, axis=-1, keepdims=True), skv))
    e_c, e_p, e_m, e_s = jnp.exp(s_c - mx), jnp.exp(s_p - mx), jnp.exp(s_m - mx), jnp.exp(skv - mx)
    den = (jnp.sum(e_c, axis=-1, keepdims=True) + jnp.sum(e_p, axis=-1, keepdims=True)
           + jnp.sum(e_m, axis=-1, keepdims=True) + e_s)
    inv = 1.0 / den
    return e_c * inv, e_p * inv, e_m * inv, e_s * inv


def _sink_rows(sk_ref, g):
    hrow = lax.broadcasted_iota(jnp.int32, (GQA_GROUP * BLOCK, 1), 0) // BLOCK
    skv = jnp.zeros((GQA_GROUP * BLOCK, 1), f32)
    for hh in range(GQA_GROUP):
        skv = jnp.where(hrow == hh, sk_ref[0, GQA_GROUP * g + hh], skv)
    return skv, hrow


def _stack_heads(ref, g):
    return jnp.concatenate([ref[:, (GQA_GROUP * g + hh) * HEAD_DIM:(GQA_GROUP * g + hh + 1) * HEAD_DIM]
                            for hh in range(GQA_GROUP)], axis=0)


def _attn_specs():
    blk = lambda w: pl.BlockSpec((BLOCK, w), lambda n: (n, 0))
    prev = pl.BlockSpec((BLOCK, D_KV), lambda n: (jnp.maximum(n - 1, 0), 0))
    meta = pl.BlockSpec((BLOCK, D_KV), lambda n: (0, 0))
    return blk, prev, meta


def _attn_fwd(qp, kp, vp, sinks):
    tp = qp.shape[0]
    blk, prev, meta = _attn_specs()

    def body(sk_ref, q_ref, kc_ref, kp_ref, km_ref, vc_ref, vp_ref, vm_ref, o_ref):
        masks = _attn_masks(pl.program_id(0))
        for g in range(N_KV_HEADS):
            sl = slice(g * HEAD_DIM, (g + 1) * HEAD_DIM)
            skv, _ = _sink_rows(sk_ref, g)
            p_c, p_p, p_m, _ = _attn_probs(_stack_heads(q_ref, g), kc_ref[:, sl], kp_ref[:, sl], km_ref[:, sl], masks, skv)
            o = (jnp.dot(p_c.astype(bf16), vc_ref[:, sl], preferred_element_type=f32)
                 + jnp.dot(p_p.astype(bf16), vp_ref[:, sl], preferred_element_type=f32)
                 + jnp.dot(p_m.astype(bf16), vm_ref[:, sl], preferred_element_type=f32))
            for hh in range(GQA_GROUP):
                h = GQA_GROUP * g + hh
                o_ref[:, h * HEAD_DIM:(h + 1) * HEAD_DIM] = o[hh * BLOCK:(hh + 1) * BLOCK].astype(bf16)

    return _pc(body, name="attn_fwd", grid=(tp // BLOCK,),
               in_specs=[pl.BlockSpec(memory_space=pltpu.SMEM), blk(D_ATT), blk(D_KV), prev, meta, blk(D_KV), prev, meta],
               out_specs=blk(D_ATT), out_shape=S((tp, D_ATT), bf16),
               compiler_params=_cparams(("arbitrary",)))(sinks, qp, kp, kp, kp, vp, vp, vp)


def _attn_bwd(qp, kp, vp, sinks, dop):
    tp = qp.shape[0]
    blk, prev, meta = _attn_specs()

    def body(sk_ref, q_ref, kc_ref, kp_ref, km_ref, vc_ref, vp_ref, vm_ref, do_ref, dq_ref, dk_ref, dv_ref, dsk_ref):
        n = pl.program_id(0)

        @pl.when(n == 0)
        def _():
            dk_ref[...] = jnp.zeros_like(dk_ref)
            dv_ref[...] = jnp.zeros_like(dv_ref)
            dsk_ref[...] = jnp.zeros_like(dsk_ref)
        masks = _attn_masks(n)
        cur = pl.ds(pl.multiple_of(n * BLOCK, BLOCK), BLOCK)
        prv = pl.ds(pl.multiple_of(jnp.maximum(n - 1, 0) * BLOCK, BLOCK), BLOCK)
        lane = lax.broadcasted_iota(jnp.int32, (1, LANES), 1)
        dsk = jnp.zeros((1, LANES), f32)
        for g in range(N_KV_HEADS):
            sl = slice(g * HEAD_DIM, (g + 1) * HEAD_DIM)
            skv, hrow = _sink_rows(sk_ref, g)
            qg = _stack_heads(q_ref, g)
            dog = _stack_heads(do_ref, g)
            ks = (kc_ref[:, sl], kp_ref[:, sl], km_ref[:, sl])
            vs = (vc_ref[:, sl], vp_ref[:, sl], vm_ref[:, sl])
            probs = _attn_probs(qg, ks[0], ks[1], ks[2], masks, skv)
            dps = [lax.dot_general(dog, v, _DIMS["nt"], preferred_element_type=f32) for v in vs]
            delta = sum(jnp.sum(p * dp, axis=-1, keepdims=True) for p, dp in zip(probs[:3], dps))
            dss = [(p * (dp - delta) * ATT_SCALE).astype(bf16) for p, dp in zip(probs[:3], dps)]
            dq = sum(jnp.dot(ds, k, preferred_element_type=f32) for ds, k in zip(dss, ks))
            for hh in range(GQA_GROUP):
                h = GQA_GROUP * g + hh
                dq_ref[:, h * HEAD_DIM:(h + 1) * HEAD_DIM] = dq[hh * BLOCK:(hh + 1) * BLOCK]
                dsk = dsk + jnp.where(lane == h, -jnp.sum(jnp.where(hrow == hh, probs[3] * delta, 0.0)), 0.0)
            for rows, p, ds in zip((cur, prv, slice(0, BLOCK)), probs[:3], dss):
                dv_ref[rows, sl] += lax.dot_general(p.astype(bf16), dog, _DIMS["tn"], preferred_element_type=f32)
                dk_ref[rows, sl] += lax.dot_general(ds, qg, _DIMS["tn"], preferred_element_type=f32)
        dsk_ref[...] += dsk

    return _pc(body, name="attn_bwd", grid=(tp // BLOCK,),
               in_specs=[pl.BlockSpec(memory_space=pltpu.SMEM), blk(D_ATT), blk(D_KV), prev, meta, blk(D_KV), prev, meta,
                         blk(D_ATT)],
               out_specs=(blk(D_ATT), _full((tp, D_KV)), _full((tp, D_KV)), _full((1, LANES))),
               out_shape=(S((tp, D_ATT), f32), S((tp, D_KV), f32), S((tp, D_KV), f32), S((1, LANES), f32)),
               compiler_params=_cparams(("arbitrary",)))(sinks, qp, kp, kp, kp, vp, vp, vp, dop)


def _seg(x, bm):
    hi = x.astype(bf16)
    lo = (x - hi.astype(f32)).astype(bf16)
    return jnp.dot(jnp.concatenate([hi, lo], axis=1), bm, preferred_element_type=f32)


@jax.custom_vjp
def _seg_linear(x, bm):
    return _seg(x, bm)


_seg_linear.defvjp(lambda x, bm: (_seg(x, bm), bm), lambda bm, ct: (_seg(ct, bm), jnp.zeros_like(bm)))


def _softplus(y):
    return jnp.maximum(y, 0.0) + jnp.log(1.0 + jnp.exp(-jnp.abs(y)))


def _prep_fn(xr, xk, xwd, xad, xgd, w0, w2, a0, a2, g2, k_k, k_a, bm, seg=_seg):
    xw = w0 + jnp.dot(jnp.tanh(xwd), w2, preferred_element_type=f32)
    decay = jnp.exp(-jnp.exp(-_softplus(-xw) - 0.5))
    alpha = _sigmoid(a0 + jnp.dot(xad, a2, preferred_element_type=f32))
    g = jnp.dot(_sigmoid(xgd), g2, preferred_element_type=f32)
    kk = xk * k_k
    kkn = kk / jnp.maximum(jnp.sqrt(seg(kk * kk, bm)), 1e-12)
    k2 = xk * (1.0 + (alpha - 1.0) * k_a)
    return decay, k2, -kkn, kkn * alpha, g


def _split_cols(x):
    o1, o2, o3 = 3 * D_R, 3 * D_R + LORA_W, 3 * D_R + LORA_W + LORA_A
    return x[:, 0:D_R], x[:, D_R:2 * D_R], x[:, 2 * D_R:o1], x[:, o1:o2], x[:, o2:o3], x[:, o3:RWKV_COLS]


def _shifted(sh_ref, x, halo, first, rb):
    sh_ref[0:SUBLANES, :] = jnp.where(first, 0.0, halo)
    sh_ref[SUBLANES:SUBLANES + rb, :] = x
    return sh_ref[SUBLANES - 1:SUBLANES - 1 + rb, :]


_PREP_PARAMS = ("od_w0", "od_w2", "od_a0", "od_a2", "od_g2", "od_k_k", "od_k_a")


def _rwkv_prep(pr, mu, params, bm):
    t = pr.shape[0]
    rb = _row_block8(t)
    hb = rb // SUBLANES

    def body(pr_ref, halo_ref, mu_ref, w0, w2, a0, a2, g2, kk_ref, ka_ref, bm_ref, *outs_sh):
        outs, sh_ref = outs_sh[:-1], outs_sh[-1]
        x = pr_ref[...]
        prev = _shifted(sh_ref, x, halo_ref[...], pl.program_id(0) == 0, rb)
        xr, xk, xv, xwd, xad, xgd = _split_cols(x + (prev - x) * mu_ref[...])
        bmv = bm_ref[...]
        decay, k2, a_s, b_s, g = _prep_fn(xr, xk, xwd, xad, xgd, w0[...], w2[...], a0[...], a2[...], g2[...],
                                          kk_ref[...], ka_ref[...], bmv)
        vals = (xr, xv, decay, k2, a_s, b_s, decay * xr, _seg(b_s * xr, bmv), _seg(k2 * xr, bmv), g)
        for ref, val in zip(outs, vals):
            ref[...] = val

    row = pl.BlockSpec((rb, RWKV_COLS), lambda i: (i, 0))
    halo = pl.BlockSpec((SUBLANES, RWKV_COLS), lambda i: (jnp.maximum(i * hb - 1, 0), 0))
    orow = pl.BlockSpec((rb, D_R), lambda i: (i, 0))
    return _pc(body, name="rwkv_prep", grid=(t // rb,),
               in_specs=[row, halo, _full((1, RWKV_COLS))] + [_full(p.shape) for p in params] + [_full(bm.shape)],
               out_specs=(orow,) * 10, out_shape=(S((t, D_R), f32),) * 10,
               scratch_shapes=[pltpu.VMEM((rb + SUBLANES, RWKV_COLS), f32)],
               compiler_params=_cparams(("arbitrary",)))(pr, pr, mu, *params, bm)


def _rwkv_prep_bwd(pr, mu, params, bm, cts):
    t = pr.shape[0]
    rb = _row_block8(t)
    hb = rb // SUBLANES
    counts = [len(c) for c in cts]
    flat = [a for c in cts for a in c]

    def body(pr_ref, halo_ref, mu_ref, w0, w2, a0, a2, g2, kk_ref, ka_ref, bm_ref, *rest):
        ct_refs, rest = rest[:len(flat)], rest[len(flat):]
        dx_ref, dmu_ref = rest[0], rest[1]
        dpar_refs, sh_ref = rest[2:9], rest[9]

        @pl.when(pl.program_id(0) == 0)
        def _():
            dmu_ref[...] = jnp.zeros_like(dmu_ref)
            for r in dpar_refs:
                r[...] = jnp.zeros_like(r)
        sums, pos = [], 0
        for c in counts:
            sums.append(sum(r[...] for r in ct_refs[pos:pos + c]))
            pos += c
        x = pr_ref[...]
        prev = _shifted(sh_ref, x, halo_ref[...], pl.program_id(0) == 0, rb)
        xr, xk, xv, xwd, xad, xgd = _split_cols(x + (prev - x) * mu_ref[...])
        bmv = bm_ref[...]
        _, vjp = jax.vjp(lambda *a: _prep_fn(*a, bmv, _seg_linear), xr, xk, xwd, xad, xgd, w0[...], w2[...], a0[...], a2[...],
                         g2[...], kk_ref[...], ka_ref[...])
        grads = vjp(tuple(sums[:5]))
        dxr, dxk, dxwd, dxad, dxgd = grads[:5]
        o1, o2, o3 = 3 * D_R, 3 * D_R + LORA_W, 3 * D_R + LORA_W + LORA_A
        dx_ref[:, 0:D_R] = dxr + sums[5]
        dx_ref[:, D_R:2 * D_R] = dxk
        dx_ref[:, 2 * D_R:o1] = sums[6]
        dx_ref[:, o1:o2] = dxwd
        dx_ref[:, o2:o3] = dxad
        dx_ref[:, o3:RWKV_COLS] = dxgd
        dmu_ref[...] += jnp.sum(dx_ref[...] * (prev - x), axis=0, keepdims=True)
        for r, gval in zip(dpar_refs, grads[5:]):
            r[...] += gval

    row = pl.BlockSpec((rb, RWKV_COLS), lambda i: (i, 0))
    halo = pl.BlockSpec((SUBLANES, RWKV_COLS), lambda i: (jnp.maximum(i * hb - 1, 0), 0))
    crow = pl.BlockSpec((rb, D_R), lambda i: (i, 0))
    return _pc(body, name="rwkv_prep_bwd", grid=(t // rb,),
               in_specs=[row, halo, _full((1, RWKV_COLS))] + [_full(p.shape) for p in params] + [_full(bm.shape)]
               + [crow] * len(flat),
               out_specs=(row, _full((1, RWKV_COLS))) + tuple(_full(p.shape) for p in params),
               out_shape=(S((t, RWKV_COLS), f32), S((1, RWKV_COLS), f32)) + tuple(S(p.shape, f32) for p in params),
               scratch_shapes=[pltpu.VMEM((rb + SUBLANES, RWKV_COLS), f32)],
               compiler_params=_cparams(("arbitrary",)))(pr, pr, mu, *params, bm, *flat)


def _shift_bwd(dxs, mu):
    t = dxs.shape[0]
    rb = _row_block(t)
    hb = rb // SUBLANES
    nblk = t // rb

    def body(dx_ref, halo_ref, mu_ref, o_ref, sh_ref):
        dx = dx_ref[...]
        sh_ref[0:rb, :] = dx
        sh_ref[rb:rb + SUBLANES, :] = jnp.where(pl.program_id(0) == nblk - 1, 0.0, halo_ref[...])
        m = mu_ref[...]
        o_ref[...] = (dx * (1.0 - m) + sh_ref[1:1 + rb, :] * m).astype(bf16)

    row = pl.BlockSpec((rb, RWKV_COLS), lambda i: (i, 0))
    halo = pl.BlockSpec((SUBLANES, RWKV_COLS), lambda i: (jnp.minimum((i + 1) * hb, t // SUBLANES - 1), 0))
    return _pc(body, name="rwkv_shift_bwd", grid=(nblk,), in_specs=[row, halo, _full((1, RWKV_COLS))],
               out_specs=row, out_shape=S((t, RWKV_COLS), bf16),
               scratch_shapes=[pltpu.VMEM((rb + SUBLANES, RWKV_COLS), f32)],
               compiler_params=_cparams(("arbitrary",)))(dxs, dxs, mu)


def _post_fn(y, xr, k2, xv, g, lg, lb, rk, bm, seg=_seg):
    inv_n = 1.0 / HEAD_DIM
    yc = y - seg(y, bm) * inv_n
    var = seg(yc * yc, bm) * inv_n
    yn = yc * lax.rsqrt(var + RWKV_GN_EPS) * lg + lb
    return (yn + seg(xr * k2 * rk, bm) * xv) * g


def _rwkv_post(y, xr, k2, xv, g, lg, lb, rk, bm):
    t = y.shape[0]
    rb = _row_block8(t)

    def body(y_ref, xr_ref, k2_ref, xv_ref, g_ref, lg_ref, lb_ref, rk_ref, bm_ref, o_ref):
        o_ref[...] = _post_fn(y_ref[...], xr_ref[...], k2_ref[...], xv_ref[...], g_ref[...], lg_ref[...], lb_ref[...],
                              rk_ref[...], bm_ref[...])

    row = pl.BlockSpec((rb, D_R), lambda i: (i, 0))
    vec = _full((1, D_R))
    return _pc(body, name="rwkv_post", grid=(t // rb,), in_specs=[row] * 5 + [vec] * 3 + [_full(bm.shape)],
               out_specs=row, out_shape=S((t, D_R), f32),
               compiler_params=_cparams(("arbitrary",)))(y, xr, k2, xv, g, lg, lb, rk, bm)


def _rwkv_post_bwd(dy1, y, xr, k2, xv, g, lg, lb, rk, bm):
    t = y.shape[0]
    rb = _row_block8(t)

    def body(dy_ref, y_ref, xr_ref, k2_ref, xv_ref, g_ref, lg_ref, lb_ref, rk_ref, bm_ref, *outs):
        @pl.when(pl.program_id(0) == 0)
        def _():
            for r in outs[5:]:
                r[...] = jnp.zeros_like(r)
        bmv = bm_ref[...]
        _, vjp = jax.vjp(lambda *a: _post_fn(*a, bmv, _seg_linear), y_ref[...], xr_ref[...], k2_ref[...], xv_ref[...], g_ref[...],
                         lg_ref[...], lb_ref[...], rk_ref[...])
        grads = vjp(dy_ref[...])
        for r, gval in zip(outs[:5], grads[:5]):
            r[...] = gval
        for r, gval in zip(outs[5:], grads[5:]):
            r[...] += gval

    row = pl.BlockSpec((rb, D_R), lambda i: (i, 0))
    vec = _full((1, D_R))
    return _pc(body, name="rwkv_post_bwd", grid=(t // rb,),
               in_specs=[pl.BlockSpec((rb, D_R), lambda i: (i, 1))] + [row] * 5 + [vec] * 3 + [_full(bm.shape)],
               out_specs=(row,) * 5 + (vec,) * 3, out_shape=(S((t, D_R), f32),) * 5 + (S((1, D_R), f32),) * 3,
               compiler_params=_cparams(("arbitrary",)))(dy1, y, xr, k2, xv, g, lg, lb, rk, bm)


def _seg2(x, bb):
    hi = x.astype(bf16)
    lo = (x - hi.astype(f32)).astype(bf16)
    return jnp.dot(jnp.concatenate([hi, lo], axis=1), bb, preferred_element_type=f32)


def _row4(rows, j):
    return jnp.concatenate([jnp.broadcast_to(rows[j:j + 1, p * LANES:(p + 1) * LANES], (HEAD_DIM, LANES))
                            for p in range(4)], axis=0)


def _scan_consts():
    lane_group = jnp.arange(LANES) // HEAD_DIM
    b128 = (lane_group[:, None] == lane_group[None, :]).astype(bf16)
    bb = jnp.concatenate([b128, b128], axis=0)
    qsel = (jnp.arange(PAIR_ROWS)[:, None] % HEAD_DIM == jnp.arange(LANES)[None, :] % HEAD_DIM).astype(f32)
    return bb, qsel


def _store_cols(acc_ref, o_ref, tc):
    for p in range(4):
        blk = acc_ref[p * HEAD_DIM:(p + 1) * HEAD_DIM, :].T
        o_ref[:, (2 * p) * HEAD_DIM:(2 * p + 1) * HEAD_DIM] = blk[0:tc]
        o_ref[:, (2 * p + 1) * HEAD_DIM:(2 * p + 2) * HEAD_DIM] = blk[HEAD_DIM:HEAD_DIM + tc]


PAIR_GROUP = 2 * SUBLANES


def _rwkv_pairs(w, a, b, k, v, wr, br, kr, bm):
    t = w.shape[0]
    rb = _row_block8(t)

    def body(w_ref, a_ref, b_ref, k_ref, v_ref, wr_ref, br_ref, kr_ref, bm_ref, *outs_sh):
        outs, sh_ref = outs_sh[:-1], outs_sh[-1]

        def second(ref):
            sh_ref[0:rb, :] = ref[...]
            sh_ref[rb:rb + SUBLANES, :] = jnp.zeros((SUBLANES, D_R), f32)
            return sh_ref[1:1 + rb, :]

        w1, a1, b1, k1, v1 = w_ref[...], a_ref[...], b_ref[...], k_ref[...], v_ref[...]
        w2, a2, wr2, br2, kr2, v2 = (second(r) for r in (w_ref, a_ref, wr_ref, br_ref, kr_ref, v_ref))
        bmv = bm_ref[...]
        beta, kappa = _seg(b1 * a2, bmv), _seg(k1 * a2, bmv)
        bwr2, kwr2 = _seg(b1 * wr2, bmv), _seg(k1 * wr2, bmv)
        w1a2 = w1 * a2
        vals = (w1a2 + a1 * beta, v1 * kappa,
                wr_ref[...] + a1 * br_ref[...], v1 * kr_ref[...],
                w1 * wr2 + a1 * (bwr2 + beta * br2) + w1a2 * br2,
                v1 * (kwr2 + kappa * br2) + v2 * kr2,
                w1 * w2, b1 * w2, k1 * w2)
        for ref, val in zip(outs, vals):
            ref[...] = val

    row = pl.BlockSpec((rb, D_R), lambda i: (i, 0))
    return _pc(body, name="rwkv_pairs", grid=(t // rb,), in_specs=[row] * 8 + [_full(bm.shape)],
               out_specs=(row,) * 9, out_shape=(S((t, D_R), f32),) * 9,
               scratch_shapes=[pltpu.VMEM((rb + SUBLANES, D_R), f32)],
               compiler_params=_cparams(("arbitrary",)))(w, a, b, k, v, wr, br, kr, bm)


def _wkv_fwd(k, v, a, b, pairs):
    t = k.shape[0]
    tc = SCAN_CHUNK
    bb, qsel = _scan_consts()

    def body(*refs):
        k16, v16, a16, b16 = refs[0:4]
        ca2p, da2p, c1p, d1p, c2p, d2p, w12p, b1wp, k1wp = refs[4:13]
        bb_ref, q_ref, y_ref, st_ref, sa_ref, vb_ref, s_scr, yacc = refs[13:]

        @pl.when(pl.program_id(0) == 0)
        def _():
            s_scr[...] = jnp.zeros_like(s_scr)
        bbv, qp = bb_ref[...], q_ref[0:HEAD_DIM, :]
        lane = lax.broadcasted_iota(jnp.int32, (HEAD_DIM, LANES), 1) % HEAD_DIM

        def halves(x):
            hi = x.astype(bf16)
            return jnp.concatenate([hi, (x - hi.astype(f32)).astype(bf16)], axis=1)

        def group(gi, s):
            base = pl.multiple_of(gi * PAIR_GROUP, PAIR_GROUP)

            def rows8(ref, j):
                return ref[pl.ds(base + (j // SUBLANES) * SUBLANES, SUBLANES), :]

            def bcast(rows, j, p):
                return jnp.broadcast_to(rows[j % SUBLANES:j % SUBLANES + 1, p * LANES:(p + 1) * LANES], (HEAD_DIM, LANES))

            step = lambda ref, j, p: bcast(rows8(ref, j), j, p)
            for q in range(SUBLANES):
                j1, j2 = 2 * q, 2 * q + 1
                t1 = base + j1
                nxt = []
                for p in range(4):
                    sl = slice(p * HEAD_DIM, (p + 1) * HEAD_DIM)
                    sp = s[sl]
                    lhs = [halves(jnp.concatenate([sp * step(a16, j1, p),
                                                   sp * step(ca2p, j1, p) + qp * step(da2p, j1, p),
                                                   sp * step(c1p, j1, p) + qp * step(d1p, j1, p),
                                                   sp * step(c2p, j1, p) + qp * step(d2p, j1, p)], axis=0))]
                    for j in (j1, j2):
                        v8 = rows8(v16, j)
                        vh8 = v8.astype(bf16).astype(f32)
                        lhs.append(jnp.concatenate([(qp * bcast(vh8, j, p)).astype(bf16),
                                                    (qp * bcast(v8 - vh8, j, p)).astype(bf16)], axis=1))
                    r = jnp.dot(jnp.concatenate(lhs, axis=0), bbv, preferred_element_type=f32)
                    sa1, sa2, y1, y2, vb1, vb2 = (r[n * HEAD_DIM:(n + 1) * HEAD_DIM] for n in range(6))
                    yacc[sl, :] = jnp.where(lane == t1, y1, jnp.where(lane == t1 + 1, y2, yacc[sl, :]))
                    st_ref[base // 2 + q, sl, :] = sp
                    sa_ref[t1, sl, :] = sa1
                    sa_ref[t1 + 1, sl, :] = sa2
                    vb_ref[t1, sl, :] = vb1
                    vb_ref[t1 + 1, sl, :] = vb2
                    nxt.append(((sp * step(w12p, j1, p) + sa1 * step(b1wp, j1, p)) + vb1 * step(k1wp, j1, p))
                               + (sa2 * step(b16, j2, p) + vb2 * step(k16, j2, p)))
                s = jnp.concatenate(nxt, axis=0)
            return s

        s_scr[...] = lax.fori_loop(0, tc // PAIR_GROUP, group, s_scr[...])
        _store_cols(yacc, y_ref, tc)

    row = pl.BlockSpec((tc, D_R), lambda c: (c, 0))
    tiles = pl.BlockSpec((tc, PAIR_ROWS, LANES), lambda c: (c, 0, 0))
    return _pc(body, name="wkv_fwd", grid=(t // tc,),
               in_specs=[row] * 13 + [_full(bb.shape), _full(qsel.shape)],
               out_specs=(row, pl.BlockSpec((tc // 2, PAIR_ROWS, LANES), lambda c: (c, 0, 0)), tiles, tiles),
               out_shape=(S((t, D_R), f32), S((t // 2, PAIR_ROWS, LANES), f32)) + (S((t, PAIR_ROWS, LANES), f32),) * 2,
               scratch_shapes=[pltpu.VMEM((PAIR_ROWS, LANES), f32), pltpu.VMEM((PAIR_ROWS, LANES), f32)],
               compiler_params=_cparams(("arbitrary",)))(k, v, a, b, *pairs, bb, qsel)


def _wkv_bwd(sprev, sab, vbb, w, k, a, b, r, dy):
    t = w.shape[0]
    tc = SCAN_CHUNK
    nc = t // tc
    bb, qsel = _scan_consts()

    def body(st_ref, sa_ref, vb_ref, w_ref, k_ref, a_ref, b_ref, r_ref, dy_ref, bb_ref, q_ref,
             dr_ref, dw_ref, dk_ref, dv_ref, da_ref, db_ref, g_scr, dvacc, rows_scr):
        @pl.when(pl.program_id(0) == 0)
        def _():
            g_scr[...] = jnp.zeros_like(g_scr)
        bbv, qv = bb_ref[...], q_ref[...]
        lane64 = lax.broadcasted_iota(jnp.int32, (PAIR_ROWS, LANES), 1) % HEAD_DIM
        outs = (dr_ref, dw_ref, db_ref, dk_ref, da_ref)

        def colsums(slot, j, x):
            for p in range(4):
                rows_scr[slot, j:j + 1, p * LANES:(p + 1) * LANES] = jnp.sum(x[p * HEAD_DIM:(p + 1) * HEAD_DIM], axis=0,
                                                                           keepdims=True)

        def group(i, g):
            base = pl.multiple_of((tc // SUBLANES - 1 - i) * SUBLANES, SUBLANES)
            w8, k8, a8, b8, r8, dy8 = (ref[pl.ds(base, SUBLANES), :] for ref in (w_ref, k_ref, a_ref, b_ref, r_ref, dy_ref))

            def after_step(j, sp):
                return sp * _row4(w8, j) + sa_ref[base + j] * _row4(b8, j) + vb_ref[base + j] * _row4(k8, j)

            def back_step(j, sp, s_t, g):
                tt = base + j
                u, vb = sa_ref[tt], vb_ref[tt]
                a4, b4, w4, k4 = _row4(a8, j), _row4(b8, j), _row4(w8, j), _row4(k8, j)
                dyb = _seg2(qv * _row4(dy8, j), bbv)
                g = g + dyb * _row4(r8, j)
                rr2 = _seg2(jnp.concatenate([g * b4, g * k4], axis=0), bbv)
                du, dvb = rr2[0:PAIR_ROWS], rr2[PAIR_ROWS:2 * PAIR_ROWS]
                for slot, val in enumerate((s_t * dyb, g * sp, g * u, g * vb, sp * du)):
                    colsums(slot, j, val)
                dvacc[...] = jnp.where(lane64 == tt, dvb, dvacc[...])
                return g * w4 + du * a4

            for q in reversed(range(SUBLANES // 2)):
                s0 = st_ref[base // 2 + q]
                s1 = after_step(2 * q, s0)
                g = back_step(2 * q + 1, s1, after_step(2 * q + 1, s1), g)
                g = back_step(2 * q, s0, s1, g)
            for slot, ref in enumerate(outs):
                ref[pl.ds(base, SUBLANES), :] = rows_scr[slot]
            return g

        g_scr[...] = lax.fori_loop(0, tc // SUBLANES, group, g_scr[...])
        _store_cols(dvacc, dv_ref, tc)

    row = pl.BlockSpec((tc, D_R), lambda c: (nc - 1 - c, 0))
    tiles = pl.BlockSpec((tc, PAIR_ROWS, LANES), lambda c: (nc - 1 - c, 0, 0))
    states = pl.BlockSpec((tc // 2, PAIR_ROWS, LANES), lambda c: (nc - 1 - c, 0, 0))
    return _pc(body, name="wkv_bwd", grid=(nc,),
               in_specs=[states, tiles, tiles] + [row] * 6 + [_full(bb.shape), _full(qsel.shape)],
               out_specs=(row,) * 6, out_shape=(S((t, D_R), f32),) * 6,
               scratch_shapes=[pltpu.VMEM((PAIR_ROWS, LANES), f32), pltpu.VMEM((PAIR_ROWS, LANES), f32),
                               pltpu.VMEM((5, SUBLANES, D_R), f32)],
               compiler_params=_cparams(("arbitrary",)))(sprev, sab, vbb, w, k, a, b, r, dy, bb, qsel)


def _rope_tables(t):
    half = HEAD_DIM // 2
    inv = ROPE_THETA ** (-jnp.arange(half, dtype=f32) / half)
    ang = jnp.arange(t, dtype=f32)[:, None] * inv[None, :]
    cos, sin = jnp.cos(ang), jnp.sin(ang)
    return jnp.concatenate([cos, cos], axis=1), jnp.concatenate([-sin, sin], axis=1)


def _head_matrix():
    grp = jnp.arange(D_R) // HEAD_DIM
    b = (grp[:, None] == grp[None, :]).astype(bf16)
    return jnp.concatenate([b, b], axis=0)


def _ffn_fwd(h, g, get_w, conv_w, conv_b, i):
    hf = _rms_fwd(h, g, f"ffn{i}_norm")
    w_up_t = get_w(f"ff{i}_up", hf)
    u = _mm(hf, w_up_t, "nt", f"ffn{i}_up")
    z = _ffn_mid(u, conv_w, conv_b, f"ffn{i}_mid")
    w_down = get_w(f"ff{i}_down", z)
    return _mm(z, w_down, "nn", f"ffn{i}_down", res=h), (hf, u, z), w_up_t, w_down


def _ffn_bwd(dh, h, saved, g, w_up_t, conv_w, conv_b, w_down, i, put_g):
    hf, u, z = saved
    dz = _mm(dh, w_down, "nt", f"ffn{i}_dz")
    g_down = _mm(z, dh, "tn", f"ffn{i}_gdown", out_dtype=GRAD_WIRE_DTYPE)
    tok = put_g(f"ff{i}_down", g_down)
    dgate, dval, g_conv, g_convb = _ffn_mid_bwd(dz, u, conv_w, conv_b + tok, f"ffn{i}_mid_bwd")
    g_up_t = _mm(dgate, hf, "tn", f"ffn{i}_gup_gate", out_dtype=GRAD_WIRE_DTYPE, out_rows=2 * D_FF)
    g_up_t = _mm(dval, hf, "tn", f"ffn{i}_gup_val", out_dtype=GRAD_WIRE_DTYPE, out_rows=2 * D_FF, out_row0=D_FF, into=g_up_t)
    tok = put_g(f"ff{i}_up", g_up_t)
    dh_in, g_norm = _mm_rms_bwd(dval, w_up_t, h, g + tok, dh, f"ffn{i}_dhf_val_norm_bwd", b_row0=D_FF,
                                res=_mm(dgate, w_up_t, "nn", f"ffn{i}_dhf_gate"))
    return dh_in, dict(conv=g_conv, conv_b=g_convb, norm=g_norm)


def _local_step(x, target, W, get_w, put_g, put_small, tok0):
    t = N_META + x.shape[0]
    c64, s64 = _rope_tables(t)
    bm = _head_matrix()
    h0 = jnp.concatenate([W["meta_tokens"], x], axis=0)

    ev_w_in_t, ev_w_out = get_w("ev_in", None), get_w("ev_out", None)
    hn0 = _rms_fwd(h0, W["norm_mix"][0] + tok0, "mix0_norm")
    p0 = _mm(hn0, ev_w_in_t, "nt", "ev_in")
    uc = _ev_a_conv(p0, W["ev_conv_a"])
    y0 = _ev_b(p0, W["ev_conv_b"], _ev_a_norm(uc, W["ev_ln_a_g"], W["ev_ln_a_b"]))
    h1 = _mm(y0, ev_w_out, "nn", "ev_out", res=h0)
    h2, ffn0, ff0_up_t, ff0_down = _ffn_fwd(h1, W["norm_ffn"][0], get_w, W["ff_conv"][0], W["ff_conv_b"][0], 0)

    hn1 = _rms_fwd(h2, W["norm_mix"][1], "mix1_norm")
    od_w_in_t = get_w("od_in", hn1)
    w_att, w_rwkv = od_w_in_t[:ATT_COLS], od_w_in_t[ATT_COLS:]
    pr = _mm(hn1, w_rwkv, "nt", "od_in_rwkv")
    qp, kp, vp = _rope_pack(_mm(hn1, w_att, "nt", "od_in_att"), c64, s64)
    op = _attn_fwd(qp, kp, vp, W["od_sinks"])
    prep_params = [W[n] for n in _PREP_PARAMS]
    xr, xv, decay, k2, a_s, b_s, wr, br, kr, gate = _rwkv_prep(pr, W["od_mu"], prep_params, bm)
    pairs = _rwkv_pairs(decay, a_s, b_s, k2, xv, wr, br, kr, bm)
    ysc, sprev, sab, vbb = _wkv_fwd(k2, xv, a_s, b_s, pairs)
    rk = W["od_r_k"].reshape(1, D_R)
    yr = _rwkv_post(ysc, xr, k2, xv, gate, W["od_lnx_g"], W["od_lnx_b"], rk, bm)
    y1 = jnp.concatenate([op[ATT_PAD:], yr.astype(bf16)], axis=1)
    od_w_out = get_w("od_out", y1)
    h3 = _mm(y1, od_w_out, "nn", "od_out", res=h2)
    h4, ffn1, ff1_up_t, ff1_down = _ffn_fwd(h3, W["norm_ffn"][1], get_w, W["ff_conv"][1], W["ff_conv_b"][1], 1)

    tgt = jnp.concatenate([jnp.zeros((N_META, D_MODEL), f32), target], axis=0)
    loss, dh4, g_norm_final = _final_loss(h4, W["norm_final"], tgt)

    dh3, gf1 = _ffn_bwd(dh4, h3, ffn1, W["norm_ffn"][1], ff1_up_t, W["ff_conv"][1], W["ff_conv_b"][1], ff1_down, 1, put_g)
    dy1 = _mm(dh3, od_w_out, "nt", "od_dy")
    g_od_w_out = _mm(y1, dh3, "tn", "od_gout", out_dtype=GRAD_WIRE_DTYPE)
    tok = put_g("od_out", g_od_w_out)
    dysc, dxr_p, dk2_p, dxv_p, dgate, g_lnx_g, g_lnx_b, g_rk = _rwkv_post_bwd(
        dy1, ysc, xr, k2, xv, gate, W["od_lnx_g"], W["od_lnx_b"] + tok, rk, bm)
    dr, dw, dk, dv, da, db = _wkv_bwd(sprev, sab, vbb, decay, k2, a_s, b_s, xr, dysc)
    prep_grads = _rwkv_prep_bwd(pr, W["od_mu"], prep_params, bm,
                                [[dw], [dk, dk2_p], [da], [db], [dgate], [dr, dxr_p], [dv, dxv_p]])
    dxs, g_mu = prep_grads[0], prep_grads[1]
    dpr = _shift_bwd(dxs, W["od_mu"])
    dop = jnp.concatenate([jnp.zeros((ATT_PAD, D_ATT), f32), dy1[:, :D_ATT]], axis=0).astype(bf16)
    dqp, dkp, dvp, dsk = _attn_bwd(qp, kp, vp, W["od_sinks"], dop)
    dpatt = _rope_bwd(dqp, dkp, dvp, c64, s64)
    n_in = ATT_COLS + RWKV_COLS
    g_od_w_in_t = _mm(dpatt, hn1, "tn", "od_gin_att", out_dtype=GRAD_WIRE_DTYPE, out_rows=n_in)
    g_od_w_in_t = _mm(dpr, hn1, "tn", "od_gin_rwkv", out_dtype=GRAD_WIRE_DTYPE, out_rows=n_in, out_row0=ATT_COLS, into=g_od_w_in_t)
    tok = put_g("od_in", g_od_w_in_t)
    dh2, g_norm_mix1 = _mm_rms_bwd(dpr, w_rwkv, h2, W["norm_mix"][1] + tok, dh3, "od_dhn_rwkv_norm_bwd",
                                   res=_mm(dpatt, w_att, "nn", "od_dhn_att"))

    dh1, gf0 = _ffn_bwd(dh2, h1, ffn0, W["norm_ffn"][0], ff0_up_t, W["ff_conv"][0], W["ff_conv_b"][0], ff0_down, 0, put_g)
    early = dict(
        norm_ffn=jnp.concatenate([gf0["norm"], gf1["norm"]], axis=0), norm_final=g_norm_final.reshape(D_MODEL),
        od_sinks=dsk[:, :N_Q_HEADS], od_mu=g_mu, od_lnx_g=g_lnx_g, od_lnx_b=g_lnx_b, od_r_k=g_rk.reshape(N_Q_HEADS, HEAD_DIM),
        ff_conv=jnp.stack([gf0["conv"], gf1["conv"]]), ff_conv_b=jnp.concatenate([gf0["conv_b"], gf1["conv_b"]], axis=0),
        **dict(zip(_PREP_PARAMS, prep_grads[2:])))
    dy0 = _mm(dh1, ev_w_out, "nt", "ev_dy")
    g_ev_w_out = _mm(y0, dh1, "tn", "ev_gout", out_dtype=GRAD_WIRE_DTYPE)
    tok = put_g("ev_out", g_ev_w_out) + put_small(early)
    duc, g_ln_g, g_ln_b = _ev_a_norm_bwd(dy0, uc, W["ev_ln_a_g"], W["ev_ln_a_b"] + tok)
    dav, dag, g_conv_a = _ev_a_conv_bwd(duc, p0, W["ev_conv_a"])
    dgb, dgc, dxi, g_conv_b = _ev_b_bwd(dy0, p0, W["ev_conv_b"])
    dp0 = jnp.concatenate([dav, dag, dgb, dgc, dxi], axis=1)
    g_ev_w_in_t = _mm(dp0, hn0, "tn", "ev_gin", out_dtype=GRAD_WIRE_DTYPE)
    tok = put_g("ev_in", g_ev_w_in_t)
    dh0, g_norm_mix0 = _mm_rms_bwd(dp0, ev_w_in_t, h0, W["norm_mix"][0] + tok, dh1, "ev_dhn_norm_bwd")

    late = dict(meta_tokens=dh0[:N_META], norm_mix=jnp.concatenate([g_norm_mix0, g_norm_mix1], axis=0),
                ev_conv_a=g_conv_a, ev_ln_a_g=g_ln_g, ev_ln_a_b=g_ln_b, ev_conv_b=g_conv_b)
    return loss, dh0[N_META:], late


HBM = pl.BlockSpec(memory_space=pl.ANY)


def _mesh_pos():
    return lax.axis_index("x"), lax.axis_index("y"), lax.axis_index("c")


def _dev(px, py, pc):
    return 4 * px + 2 * py + pc


def _all_gather(xs, name):
    n = len(xs)

    def body(*refs):
        x_refs, o_refs = refs[:n], refs[n:2 * n]
        send_sems, recv_sems, local_sems = refs[2 * n:]
        x, y, c = _mesh_pos()
        me, sibling = (x, y, c), (x, y, 1 - c)
        chips = [(1 - x, y), (x, 1 - y), (1 - x, 1 - y)]

        def copy(i, k, block, to, from_input=False):
            dst = o_refs[i].at[_dev(*block)]
            return pltpu.make_async_remote_copy(src_ref=x_refs[i] if from_input else dst, dst_ref=dst,
                                                send_sem=send_sems.at[i, k], recv_sem=recv_sems.at[i, k],
                                                device_id=to, device_id_type=MESH)

        mine = [pltpu.make_async_copy(x_refs[i], o_refs[i].at[_dev(*me)], local_sems.at[i]) for i in range(n)]
        for cp in mine:
            cp.start()
        first = []
        for i in range(n):
            first.append(copy(i, 0, me, sibling, True))
            first += [copy(i, 1 + j, me, (*chip, c), True) for j, chip in enumerate(chips)]
        for cp in first:
            cp.start()
        passed = []
        for j, chip in enumerate(chips):
            for i in range(n):
                copy(i, 1 + j, (*chip, c), me).wait_recv()
                fwd = copy(i, 4 + j, (*chip, c), sibling)
                fwd.start()
                passed.append(fwd)
        for i in range(n):
            copy(i, 0, sibling, me).wait_recv()
            for j, chip in enumerate(chips):
                copy(i, 4 + j, (*chip, 1 - c), me).wait_recv()
        for cp in first + passed:
            cp.wait_send()
        for cp in mine:
            cp.wait()

    return _pc(body, name=name, in_specs=[HBM] * n, out_specs=tuple([HBM] * n),
               out_shape=tuple(S((N_DEV,) + x.shape, x.dtype) for x in xs),
               scratch_shapes=[pltpu.SemaphoreType.DMA((n, 7)), pltpu.SemaphoreType.DMA((n, 7)),
                               pltpu.SemaphoreType.DMA((n,))])(*xs)


HBM_SPEC = pl.BlockSpec(memory_space=pltpu.HBM)
SEM_SPEC = pl.BlockSpec(memory_space=pltpu.SEMAPHORE)
DATAFLOW = pltpu.SideEffectType.DATAFLOW_SIDE_EFFECTING
_PEER_FLIPS = ((1, 0, 0), (0, 1, 0), (1, 1, 0), (1, 0, 1), (0, 1, 1), (1, 1, 1), (0, 0, 1))
N_PEERS = len(_PEER_FLIPS)


def _peers(x, y, c):
    return [((1 - x) if fx else x, (1 - y) if fy else y, (1 - c) if fc else c) for fx, fy, fc in _PEER_FLIPS]


def _xchg_start(srcs, lands, scatter, name):
    n = len(srcs)

    def body(*refs):
        src_refs, land_refs = refs[:n], refs[n:2 * n]
        send_sems, recv_sems, token = refs[2 * n], refs[2 * n + 1], refs[-1]
        x, y, c = _mesh_pos()
        me = _dev(x, y, c)
        for i in range(n):
            for k, peer in enumerate(_peers(x, y, c)):
                pltpu.make_async_remote_copy(src_ref=src_refs[i].at[_dev(*peer)] if scatter else src_refs[i],
                                             dst_ref=land_refs[i].at[me], send_sem=send_sems.at[i * N_PEERS + k],
                                             recv_sem=recv_sems.at[i * N_PEERS + k], device_id=peer, device_id_type=MESH).start()
        token[...] = jnp.zeros_like(token)

    arrs = list(srcs) + list(lands)
    outs = _pc(body, name=name,
               out_shape=(pltpu.SemaphoreType.DMA((n * N_PEERS,)), pltpu.SemaphoreType.DMA((n * N_PEERS,)),
                          *[pltpu.HBM(a.shape, a.dtype) for a in arrs], S((SUBLANES, LANES), f32)),
               in_specs=[HBM_SPEC] * (2 * n),
               out_specs=(SEM_SPEC, SEM_SPEC, *[HBM_SPEC] * (2 * n), pl.BlockSpec(memory_space=pltpu.VMEM)),
               input_output_aliases={i: 2 + i for i in range(2 * n)},
               compiler_params=pltpu.CompilerParams(has_side_effects=DATAFLOW))(
        *[pltpu.with_memory_space_constraint(a, pltpu.HBM) for a in arrs])
    return (outs[0], outs[1], list(outs[2:2 + n]), list(outs[2 + n:2 + 2 * n]), scatter), outs[-1]


def _xchg_wait(handle, after, name):
    send_sems, recv_sems, srcs, lands, scatter = handle
    n = len(srcs)

    def body(*refs):
        src_refs, land_refs = refs[:n], refs[n:2 * n]
        send, recv = refs[2 * n], refs[2 * n + 1]
        x, y, c = _mesh_pos()
        for i in range(n):
            for k in range(N_PEERS):
                cp = pltpu.make_async_remote_copy(src_ref=src_refs[i].at[0] if scatter else src_refs[i],
                                                  dst_ref=land_refs[i].at[0], send_sem=send.at[i * N_PEERS + k],
                                                  recv_sem=recv.at[i * N_PEERS + k],
                                                  device_id=(x, y, c), device_id_type=MESH)
                cp.wait_send()
                cp.wait_recv()

    arrs = srcs + lands
    outs = _pc(body, name=name, out_shape=tuple(pltpu.HBM(a.shape, a.dtype) for a in arrs),
               in_specs=[HBM_SPEC] * (2 * n) + [SEM_SPEC, SEM_SPEC, pl.BlockSpec(memory_space=pl.ANY)],
               out_specs=tuple([HBM_SPEC] * (2 * n)), input_output_aliases={i: i for i in range(2 * n)},
               compiler_params=pltpu.CompilerParams(has_side_effects=DATAFLOW))(*arrs, send_sems, recv_sems, after)
    return list(outs[:n]), list(outs[n:])


def _rs_sum(g, land, me_vec, name):
    _, r, cols = g.shape
    tr = _divisor_block(r, 16, min(r, 352))

    def body(me_ref, g_ref, *rest):
        o_ref = rest[-1]
        acc = g_ref[0].astype(f32)
        for l_ref in rest[:-1]:
            acc = acc + l_ref[0].astype(f32)
        o_ref[...] = acc

    blk = lambda f: pl.BlockSpec((1, tr, cols), f)
    grid_spec = pltpu.PrefetchScalarGridSpec(
        num_scalar_prefetch=1, grid=(r // tr,),
        in_specs=[blk(lambda i, me_ref: (me_ref[0], i, 0))]
        + [blk(lambda i, me_ref, k=k: ((me_ref[0] + k) % N_DEV, i, 0)) for k in range(1, N_DEV)],
        out_specs=pl.BlockSpec((tr, cols), lambda i, me_ref: (i, 0)))
    return _pc(body, name=name, grid_spec=grid_spec, out_shape=S((r, cols), f32),
               compiler_params=_cparams(("arbitrary",)))(me_vec, g, *([land] * (N_DEV - 1)))


def _sum_devices(a, name):
    def body(a_ref, o_ref):
        acc = a_ref[0]
        for d in range(1, N_DEV):
            acc = acc + a_ref[d]
        o_ref[...] = acc

    return _pc(body, name=name, grid=(1,), in_specs=[_full(a.shape)], out_specs=_full(a.shape[1:]),
               out_shape=S(a.shape[1:], a.dtype), compiler_params=_cparams(("arbitrary",)))(a)


def _adamw(w, m, v, g, name):
    shape = w.shape
    w2, m2, v2, g2 = (a.reshape(-1, shape[-1]) for a in (w, m, v, g))
    rows, cols = w2.shape
    tr = rows if rows % SUBLANES else _divisor_block(rows, SUBLANES, max(SUBLANES, min(rows, ADAMW_BLOCK_ELEMS // cols)))
    c1, c2 = 1.0 - ADAM_B1 ** ADAM_STEP, 1.0 - ADAM_B2 ** ADAM_STEP

    def body(w_ref, m_ref, v_ref, g_ref, d_ref, nm_ref, nv_ref):
        gv = g_ref[...]
        nm = ADAM_B1 * m_ref[...] + (1.0 - ADAM_B1) * gv
        nv = ADAM_B2 * v_ref[...] + (1.0 - ADAM_B2) * (gv * gv)
        d_ref[...] = -ADAM_LR * ((nm / c1) / (jnp.sqrt(nv / c2) + ADAM_EPS) + ADAM_WD * w_ref[...])
        nm_ref[...] = nm
        nv_ref[...] = nv

    blk = pl.BlockSpec((tr, cols), lambda i: (i, 0))
    outs = _pc(body, name=name, grid=(rows // tr,), in_specs=[blk] * 4, out_specs=(blk,) * 3,
               out_shape=(S((rows, cols), f32),) * 3, compiler_params=_cparams(("arbitrary",)))(w2, m2, v2, g2)
    return tuple(o.reshape(shape) for o in outs)


_WEIGHTS = ("meta_tokens", "norm_mix", "norm_ffn", "norm_final", "ev_w_in", "ev_conv_a", "ev_ln_a_g", "ev_ln_a_b",
            "ev_conv_b", "ev_w_out", "od_w_in", "od_sinks", "od_mu", "od_w0", "od_w2", "od_a0", "od_a2", "od_g2",
            "od_k_k", "od_k_a", "od_r_k", "od_lnx_g", "od_lnx_b", "od_w_out", "ff_w_up", "ff_conv", "ff_conv_b", "ff_w_down")
_SMALL_SHARDED = (("meta_tokens", 1), ("ev_conv_a", 2), ("ev_conv_b", 2), ("od_mu", 1), ("od_w0", 1), ("od_w2", 2),
                  ("od_a0", 1), ("od_a2", 2), ("od_g2", 2), ("od_k_k", 1), ("od_k_a", 1), ("od_lnx_g", 1),
                  ("od_lnx_b", 1), ("ff_conv", 2))
_SMALL_REPLICATED = ("norm_mix", "norm_ffn", "norm_final", "ev_ln_a_g", "ev_ln_a_b", "od_sinks", "od_r_k", "ff_conv_b")
SLAB_UNIT = SUBLANES * LANES


def _pack(arrs):
    flat = jnp.concatenate([a.reshape(-1).astype(f32) for a in arrs])
    pad = (-flat.shape[0]) % SLAB_UNIT
    return jnp.pad(flat, (0, pad)).reshape(-1, LANES)


def _unpack(flat, shapes):
    out, off = [], 0
    for shp in shapes:
        size = 1
        for s in shp:
            size *= s
        out.append(flat[..., off:off + size].reshape(flat.shape[:-1] + tuple(shp)))
        off += size
    return out


def _full_shape(shape, axis):
    return tuple(N_DEV * s if i == axis else s for i, s in enumerate(shape))


def kernel(x, meta_tokens, norm_mix, norm_ffn, norm_final, ev_w_in, ev_conv_a, ev_ln_a_g, ev_ln_a_b, ev_conv_b, ev_w_out, od_w_in, od_sinks, od_mu, od_w0, od_w2, od_a0, od_a2, od_g2, od_k_k, od_k_a, od_r_k, od_lnx_g, od_lnx_b, od_w_out, ff_w_up, ff_conv, ff_conv_b, ff_w_down, loss_target, m_meta_tokens, m_norm_mix, m_norm_ffn, m_norm_final, m_ev_w_in, m_ev_conv_a, m_ev_ln_a_g, m_ev_ln_a_b, m_ev_conv_b, m_ev_w_out, m_od_w_in, m_od_sinks, m_od_mu, m_od_w0, m_od_w2, m_od_a0, m_od_a2, m_od_g2, m_od_k_k, m_od_k_a, m_od_r_k, m_od_lnx_g, m_od_lnx_b, m_od_w_out, m_ff_w_up, m_ff_conv, m_ff_conv_b, m_ff_w_down, v_meta_tokens, v_norm_mix, v_norm_ffn, v_norm_final, v_ev_w_in, v_ev_conv_a, v_ev_ln_a_g, v_ev_ln_a_b, v_ev_conv_b, v_ev_w_out, v_od_w_in, v_od_sinks, v_od_mu, v_od_w0, v_od_w2, v_od_a0, v_od_a2, v_od_g2, v_od_k_k, v_od_k_a, v_od_r_k, v_od_lnx_g, v_od_lnx_b, v_od_w_out, v_ff_w_up, v_ff_conv, v_ff_conv_b, v_ff_w_down):
    A = dict(locals())
    px, py, pc = _mesh_pos()
    me = _dev(px, py, pc)
    me_vec = jnp.reshape(me, (1,)).astype(jnp.int32)
    rows = lambda a: a.reshape(N_DEV * a.shape[1], a.shape[2])
    blocks = lambda a: a.reshape(N_DEV, a.shape[0] // N_DEV, a.shape[1])

    shards = dict(ev_in=ev_w_in[0].T, ev_out=ev_w_out[0], ff0_up=ff_w_up[0].T, ff0_down=ff_w_down[0], od_in=od_w_in[0].T,
                  od_out=od_w_out[0], ff1_up=ff_w_up[1].T, ff1_down=ff_w_down[1])
    shards = {n: b.astype(bf16) for n, b in shards.items()}
    small_shapes = [A[n].shape for n, _ in _SMALL_SHARDED]
    gathered = _all_gather([shards["ev_in"], shards["ev_out"], _pack([A[n] for n, _ in _SMALL_SHARDED])], "gather_first")
    gathered, shards = lax.optimization_barrier((gathered, shards))
    fetch, tok0 = {}, jnp.zeros((), f32)
    for n in ("ff0_up", "ff0_down", "od_in", "od_out", "ff1_up", "ff1_down"):
        shard, tok0 = lax.optimization_barrier((shards[n], tok0))
        land = lax.dynamic_update_slice(lax.empty((N_DEV,) + shard.shape, bf16), shard[None], (me, 0, 0))
        fetch[n], token = _xchg_start([shard], [land], False, f"gather_{n}_start")
        tok0 = tok0 + token[0, 0]

    def get_w(n, after):
        if n in ("ev_in", "ev_out"):
            return rows(gathered[("ev_in", "ev_out").index(n)])
        return rows(_xchg_wait(fetch[n], after, f"gather_{n}_wait")[1][0])

    W = {}
    for (n, ax), seg in zip(_SMALL_SHARDED, _unpack(gathered[-1].reshape(N_DEV, -1), small_shapes)):
        W[n] = jnp.moveaxis(seg, 0, ax).reshape(_full_shape(A[n].shape, ax))
    for n in ("ev_conv_a", "ev_conv_b", "od_w2", "od_a2", "od_g2"):
        W[n] = W[n][0]
    for n in _SMALL_REPLICATED:
        W[n] = A[n]
    W["od_r_k"] = od_r_k[0]

    small_shape = {n: _full_shape(A[n].shape, ax) for n, ax in _SMALL_SHARDED}
    small_shape.update({n: A[n].shape for n in _SMALL_REPLICATED})
    sent, small_sent, small_names = {}, {}, {}

    def put_g(n, g):
        g8 = blocks(g)
        sent[n], token = _xchg_start([g8], [lax.empty(g8.shape, g8.dtype)], True, f"reduce_{n}_start")
        return token[0, 0]

    def put_small(gs, stage="early"):
        small_names[stage] = sorted(gs)
        slab = _pack([gs[n] for n in small_names[stage]])
        land = lax.dynamic_update_slice(lax.empty((N_DEV,) + slab.shape, f32), slab[None], (me, 0, 0))
        small_sent[stage], small_tok[stage] = _xchg_start([slab], [land], False, f"gather_{stage}_small_grads_start")
        return small_tok[stage][0, 0]

    small_tok = {}
    loss_tile, grad_x, late = _local_step(x[0], loss_target[0], W, get_w, put_g, put_small, tok0)
    put_small(late, "late")
    late_tok = small_tok["late"]

    gsh, prev = {}, late_tok
    for n in ("ff1_down", "ff1_up", "od_out", "od_in", "ff0_down", "ff0_up", "ev_out", "ev_in"):
        srcs, lands = _xchg_wait(sent[n], prev, f"reduce_{n}_wait")
        gsh[n] = prev = _rs_sum(srcs[0], lands[0], me_vec, f"reduce_{n}_sum")
    grads = dict(ev_w_in=gsh["ev_in"].T[None], ev_w_out=gsh["ev_out"][None], od_w_in=gsh["od_in"].T[None],
                 od_w_out=gsh["od_out"][None], ff_w_up=jnp.stack([gsh["ff0_up"].T, gsh["ff1_up"].T]),
                 ff_w_down=jnp.stack([gsh["ff0_down"], gsh["ff1_down"]]))

    delta, new_m, new_v = {}, {}, {}
    for n in ("ff_w_up", "ff_w_down", "od_w_in", "od_w_out", "ev_w_in", "ev_w_out"):
        delta[n], new_m[n], new_v[n] = _adamw(A[n], A["m_" + n], A["v_" + n], grads[n], "adamw_" + n)
    for stage in ("early", "late"):
        gsm = _xchg_wait(small_sent[stage], delta["ev_w_in"], f"gather_{stage}_small_grads_wait")[1][0]
        summed = _sum_devices(gsm, f"sum_{stage}_small_grads").reshape(-1)
        for n, full in zip(small_names[stage], _unpack(summed, [small_shape[n] for n in small_names[stage]])):
            grads[n] = full
    for n, ax in _SMALL_SHARDED:
        size = A[n].shape[ax]
        grads[n] = lax.dynamic_slice_in_dim(grads[n], me * size, size, axis=ax)
    for n in small_shape:
        delta[n], new_m[n], new_v[n] = _adamw(A[n], A["m_" + n], A["v_" + n], grads[n], "adamw_" + n)

    loss = lax.psum(loss_tile[0, 0], ("x", "y", "c"))
    return (loss, grad_x[None], *[grads[n] for n in _WEIGHTS], *[delta[n] for n in _WEIGHTS],
            *[new_m[n] for n in _WEIGHTS], *[new_v[n] for n in _WEIGHTS])
```

```python
import jax
import jax.numpy as jnp
from jax import lax
from jax.experimental import pallas as pl
from jax.experimental.pallas import tpu as pltpu

f32, bf16 = jnp.float32, jnp.bfloat16

D_MODEL = 1024
N_META = 16
RMS_EPS = 1e-6
LN_EPS = 1e-5
D_A = 512
CONV_A_WIDTH = 31
CONV_B_WIDTH = 3
HEAD_DIM = 64
N_Q_HEADS = 8
N_KV_HEADS = 2
GQA_GROUP = 4
D_ATT = 512
D_KV = 128
BLOCK = 128
ROPE_THETA = 10000.0
D_R = 512
LORA_W, LORA_A, LORA_G = 64, 64, 128
RWKV_GN_EPS = 64e-5
ATT_COLS = D_ATT + 2 * D_KV
RWKV_COLS = 3 * D_R + LORA_W + LORA_A + LORA_G
D_FF = 2816
FF_CONV_WIDTH = 3
FF_BLOCK = 256
NEG_INF = -1e30
ATT_PAD = BLOCK - N_META
ATT_SCALE = HEAD_DIM ** -0.5

ADAM_LR, ADAM_B1, ADAM_B2, ADAM_EPS, ADAM_WD, ADAM_STEP = 0.001, 0.9, 0.999, 1e-08, 0.01, 10

N_DEV = 8
LANES = 128
SUBLANES = 8
SCAN_CHUNK = 48
PAIR_ROWS = 4 * HEAD_DIM
V7X_VMEM_LIMIT = 56 * 1024 * 1024
ADAMW_BLOCK_ELEMS = 400 * 1024
GRAD_WIRE_DTYPE = bf16
MESH = pl.DeviceIdType.MESH
S = jax.ShapeDtypeStruct
HIGHEST = lax.Precision.HIGHEST


def _pc(body, **kw):
    return pl.pallas_call(body, **kw)


def _cparams(sem=None):
    return pltpu.CompilerParams(dimension_semantics=sem, vmem_limit_bytes=V7X_VMEM_LIMIT)


def _divisor_block(t, unit, limit):
    best = unit
    for rb in range(unit, limit + 1, unit):
        if t % rb == 0:
            best = rb
    assert t % best == 0, (t, unit)
    return best


def _row_block(t):
    return _divisor_block(t, 16, 704)


def _row_block8(t):
    return _divisor_block(t, 8, 344)


def _col_tile(n, cap):
    return _divisor_block(n, LANES, min(n, cap)) if n % LANES == 0 else n


def _full(shape):
    nd = len(shape)
    return pl.BlockSpec(shape, lambda *_: (0,) * nd)


def _sigmoid(x):
    return jax.nn.sigmoid(x)


_DIMS = {"nn": (((1,), (0,)), ((), ())), "nt": (((1,), (1,)), ((), ())), "tn": (((0,), (0,)), ((), ()))}
MM_MAX_K = 2816
MM_MAX_TM = 704
MM_MAX_TN = 1408


def _mm(a, b, mode, name, out_dtype=f32, res=None, b_row0=0, out_rows=None, out_row0=0, into=None):
    if mode == "nn":
        (m, k), n, k2 = a.shape, b.shape[1], a.shape[1]
        assert b_row0 % k == 0 and b_row0 + k <= b.shape[0], (a.shape, b.shape, b_row0)
    elif mode == "nt":
        (m, k), (n, k2) = a.shape, b.shape
    else:
        (k, m), (k2, n) = a.shape, b.shape
    assert k == k2, (a.shape, b.shape, mode)
    tm = _row_block(m) if m % LANES else _col_tile(m, MM_MAX_TM)
    tn = _col_tile(n, MM_MAX_TN)
    nk = 1 if (mode == "tn" or k <= MM_MAX_K) else k // MM_MAX_K
    tk = k // nk
    assert tk * nk == k
    dims = _DIMS[mode]

    def body(a_ref, b_ref, *rest):
        part = lax.dot_general(a_ref[...].astype(bf16), b_ref[...].astype(bf16), dims, preferred_element_type=f32)
        if nk == 1:
            o_ref = rest[-1]
            if res is not None:
                part = part + rest[0][...]
            o_ref[...] = part.astype(out_dtype)
            return
        o_ref, acc_ref = rest[-2], rest[-1]
        kk = pl.program_id(2)

        @pl.when(kk == 0)
        def _():
            acc_ref[...] = part

        @pl.when(kk > 0)
        def _():
            acc_ref[...] += part

        @pl.when(kk == nk - 1)
        def _():
            acc = acc_ref[...]
            if res is not None:
                acc = acc + rest[0][...]
            o_ref[...] = acc.astype(out_dtype)

    if mode == "tn":
        a_spec = pl.BlockSpec((k, tm), lambda i, j, kk: (0, i))
    else:
        a_spec = pl.BlockSpec((tm, tk), lambda i, j, kk: (i, kk))
    if mode == "nt":
        b_spec = pl.BlockSpec((tn, tk), lambda i, j, kk: (j, kk))
    else:
        b_spec = pl.BlockSpec((tk, tn), lambda i, j, kk: (kk + b_row0 // tk, j))
    assert out_row0 % tm == 0 and res is None or out_row0 == 0
    o_spec = pl.BlockSpec((tm, tn), lambda i, j, kk: (i + out_row0 // tm, j))
    ins, specs, aliases = [a, b], [a_spec, b_spec], {}
    if res is not None:
        ins.append(res)
        specs.append(o_spec)
    if into is not None:
        assert into.shape == (out_rows, n) and into.dtype == out_dtype
        aliases = {len(ins): 0}
        ins.append(into)
        specs.append(pl.BlockSpec(memory_space=pl.ANY))
    scratch = [pltpu.VMEM((tm, tn), f32)] if nk > 1 else []
    return _pc(body, name=name, grid=(m // tm, n // tn, nk), in_specs=specs, out_specs=o_spec,
               out_shape=S((out_rows or m, n), out_dtype), scratch_shapes=scratch, input_output_aliases=aliases,
               compiler_params=_cparams(("arbitrary", "arbitrary", "arbitrary")))(*ins)


def _rms_fwd(x, g, name):
    t, d = x.shape
    rb = _row_block(t)

    def body(x_ref, g_ref, o_ref):
        xv = x_ref[...]
        rstd = lax.rsqrt(jnp.mean(xv * xv, axis=-1, keepdims=True) + RMS_EPS)
        o_ref[...] = (xv * rstd * g_ref[...]).astype(bf16)

    row = pl.BlockSpec((rb, d), lambda i: (i, 0))
    return _pc(body, name=name, grid=(t // rb,), in_specs=[row, _full((1, d))], out_specs=row,
               out_shape=S((t, d), bf16), compiler_params=_cparams(("arbitrary",)))(x, g.reshape(1, d))


def _mm_rms_bwd(a, b, x, g, dres, name, b_row0=0, res=None):
    (m, k), n = a.shape, b.shape[1]
    assert k <= MM_MAX_K and b_row0 % k == 0 and b_row0 + k <= b.shape[0] and x.shape == (m, n)
    tm = _row_block(m)

    def body(a_ref, b_ref, x_ref, g_ref, dres_ref, *rest):
        dx_ref, dg_ref = rest[-2], rest[-1]

        @pl.when(pl.program_id(0) == 0)
        def _():
            dg_ref[...] = jnp.zeros_like(dg_ref)
        dy = jnp.dot(a_ref[...].astype(bf16), b_ref[...].astype(bf16), preferred_element_type=f32)
        if res is not None:
            dy = dy + rest[0][...]
        xv = x_ref[...]
        rstd = lax.rsqrt(jnp.mean(xv * xv, axis=-1, keepdims=True) + RMS_EPS)
        xn = xv * rstd
        dg_ref[...] += jnp.sum(dy * xn, axis=0, keepdims=True)
        dxh = dy * g_ref[...]
        dx_ref[...] = dres_ref[...] + rstd * (dxh - xn * jnp.mean(dxh * xn, axis=-1, keepdims=True))

    row = pl.BlockSpec((tm, n), lambda i: (i, 0))
    ins = [a, b, x, g.reshape(1, n), dres] + ([res] if res is not None else [])
    specs = [pl.BlockSpec((tm, k), lambda i: (i, 0)), pl.BlockSpec((k, n), lambda i: (b_row0 // k, 0)), row, _full((1, n)), row]
    specs += [row] if res is not None else []
    return _pc(body, name=name, grid=(m // tm,), in_specs=specs, out_specs=(row, _full((1, n))),
               out_shape=(S((m, n), f32), S((1, n), f32)), compiler_params=_cparams(("arbitrary",)))(*ins)


def _mm_res_norm(a, b, res, g, name):
    (m, k), n = a.shape, b.shape[1]
    assert k == b.shape[0] and k <= MM_MAX_K and res.shape == (m, n)
    tm = _row_block(m)

    def body(a_ref, b_ref, r_ref, g_ref, h_ref, hn_ref):
        h = jnp.dot(a_ref[...].astype(bf16), b_ref[...].astype(bf16), preferred_element_type=f32) + r_ref[...]
        h_ref[...] = h
        rstd = lax.rsqrt(jnp.mean(h * h, axis=-1, keepdims=True) + RMS_EPS)
        hn_ref[...] = (h * rstd * g_ref[...]).astype(bf16)

    row = pl.BlockSpec((tm, n), lambda i: (i, 0))
    return _pc(body, name=name, grid=(m // tm,),
               in_specs=[pl.BlockSpec((tm, k), lambda i: (i, 0)), _full((k, n)), row, _full((1, n))],
               out_specs=(row, row), out_shape=(S((m, n), f32), S((m, n), bf16)),
               compiler_params=_cparams(("arbitrary",)))(a, b, res, g.reshape(1, n))


def _final_loss(h, g, target_padded):
    t, d = h.shape
    rb = _row_block8(t)

    def body(x_ref, g_ref, t_ref, loss_ref, dx_ref, dg_ref):
        i = pl.program_id(0)

        @pl.when(i == 0)
        def _():
            dg_ref[...] = jnp.zeros_like(dg_ref)
            loss_ref[...] = jnp.zeros_like(loss_ref)
        xv = x_ref[...]
        rstd = lax.rsqrt(jnp.mean(xv * xv, axis=-1, keepdims=True) + RMS_EPS)
        xn = xv * rstd
        gv = g_ref[...]
        row = i * rb + lax.broadcasted_iota(jnp.int32, (rb, 1), 0)
        diff = jnp.where(row >= N_META, xn * gv - t_ref[...], 0.0)
        loss_ref[...] += 0.5 * jnp.sum(jnp.mean(diff * diff, axis=-1, keepdims=True))
        dout = diff * (1.0 / d)
        dg_ref[...] += jnp.sum(dout * xn, axis=0, keepdims=True)
        dxh = dout * gv
        dx_ref[...] = rstd * (dxh - xn * jnp.mean(dxh * xn, axis=-1, keepdims=True))

    row = pl.BlockSpec((rb, d), lambda i: (i, 0))
    return _pc(body, name="final_loss", grid=(t // rb,), in_specs=[row, _full((1, d)), row],
               out_specs=(_full((SUBLANES, LANES)), row, _full((1, d))),
               out_shape=(S((SUBLANES, LANES), f32), S((t, d), f32), S((1, d), f32)),
               compiler_params=_cparams(("arbitrary",)))(h, g.reshape(1, d), target_padded)


CONV_LEAD = 32


def _fill_front_padded(pad_ref, x, t):
    pad_ref[0:CONV_LEAD, :] = jnp.zeros((CONV_LEAD, x.shape[1]), f32)
    pad_ref[CONV_LEAD:CONV_LEAD + t, :] = x


def _fill_back_padded(pad_ref, x, t):
    pad_ref[0:t, :] = x
    pad_ref[t:t + CONV_LEAD, :] = jnp.zeros((CONV_LEAD, x.shape[1]), f32)


def _conv_rows(pad_ref, w_ref, kw, r0, nr):
    acc = None
    for j in range(kw):
        lo = CONV_LEAD + r0 - (kw - 1) + j
        term = w_ref[j:j + 1, :] * pad_ref[lo:lo + nr, :]
        acc = term if acc is None else acc + term
    return acc


def _conv_t_rows(padb_ref, w_ref, kw, r0, nr):
    acc = None
    for j in range(kw):
        lo = r0 + (kw - 1) - j
        term = w_ref[j:j + 1, :] * padb_ref[lo:lo + nr, :]
        acc = term if acc is None else acc + term
    return acc


def _conv_dw_rows(dy_blk, pad_ref, kw, r0, nr):
    out = []
    for j in range(kw):
        lo = CONV_LEAD + r0 - (kw - 1) + j
        out.append(jnp.sum(dy_blk * pad_ref[lo:lo + nr, :], axis=0, keepdims=True))
    return out


def _acc_list(a, b):
    return b if a is None else [x + y for x, y in zip(a, b)]


def _ev_a_conv(p, conv_a):
    t = p.shape[0]
    cr = _row_block8(t)
    nb = D_A // LANES

    def body(av_ref, ag_ref, w_ref, o_ref, pad_ref):
        _fill_front_padded(pad_ref, av_ref[...] * _sigmoid(ag_ref[...]), t)
        for r in range(t // cr):
            o_ref[r * cr:(r + 1) * cr, :] = _conv_rows(pad_ref, w_ref, CONV_A_WIDTH, r * cr, cr)

    col = lambda off: pl.BlockSpec((t, LANES), lambda j: (0, j + off))
    return _pc(body, name="ev_a_conv", grid=(nb,),
               in_specs=[col(0), col(nb), pl.BlockSpec((CONV_A_WIDTH, LANES), lambda j: (0, j))],
               out_specs=col(0), out_shape=S((t, D_A), f32),
               scratch_shapes=[pltpu.VMEM((t + CONV_LEAD, LANES), f32)],
               compiler_params=_cparams(("arbitrary",)))(p, p, conv_a)


def _ln_silu(uc, g, b):
    mu = jnp.mean(uc, axis=-1, keepdims=True)
    xc = uc - mu
    var = jnp.mean(xc * xc, axis=-1, keepdims=True)
    y = xc * lax.rsqrt(var + LN_EPS) * g + b
    return y * _sigmoid(y)


def _ev_a_norm(uc, g, b):
    t, d = uc.shape
    rb = _row_block(t)

    def body(u_ref, g_ref, b_ref, o_ref):
        o_ref[...] = _ln_silu(u_ref[...], g_ref[...], b_ref[...]).astype(bf16)

    row = pl.BlockSpec((rb, d), lambda i: (i, 0))
    return _pc(body, name="ev_a_norm", grid=(t // rb,), in_specs=[row, _full((1, d)), _full((1, d))],
               out_specs=row, out_shape=S((t, 2 * d), bf16), compiler_params=_cparams(("arbitrary",)))(uc, g, b)


def _ev_a_norm_bwd(dy, uc, g, b):
    t, d = uc.shape
    rb = _row_block8(t)

    def body(dy_ref, u_ref, g_ref, b_ref, du_ref, dg_ref, db_ref):
        @pl.when(pl.program_id(0) == 0)
        def _():
            dg_ref[...] = jnp.zeros_like(dg_ref)
            db_ref[...] = jnp.zeros_like(db_ref)
        _, vjp = jax.vjp(_ln_silu, u_ref[...], g_ref[...], b_ref[...])
        du, dg, db = vjp(dy_ref[...])
        du_ref[...] = du
        dg_ref[...] += dg
        db_ref[...] += db

    row = pl.BlockSpec((rb, d), lambda i: (i, 0))
    return _pc(body, name="ev_a_norm_bwd", grid=(t // rb,), in_specs=[row, row, _full((1, d)), _full((1, d))],
               out_specs=(row, _full((1, d)), _full((1, d))),
               out_shape=(S((t, d), f32), S((1, d), f32), S((1, d), f32)),
               compiler_params=_cparams(("arbitrary",)))(dy, uc, g, b)


def _ev_a_conv_bwd(duc, p, conv_a):
    t = p.shape[0]
    cr = _row_block8(t)
    nb = D_A // LANES

    def body(dy_ref, av_ref, ag_ref, w_ref, dav_ref, dag_ref, dw_ref, pad_ref, padb_ref):
        _fill_front_padded(pad_ref, av_ref[...] * _sigmoid(ag_ref[...]), t)
        _fill_back_padded(padb_ref, dy_ref[...], t)
        dw = None
        for r in range(t // cr):
            rows = slice(r * cr, (r + 1) * cr)
            du = _conv_t_rows(padb_ref, w_ref, CONV_A_WIDTH, r * cr, cr)
            avr = av_ref[rows, :]
            sgr = _sigmoid(ag_ref[rows, :])
            dav_ref[rows, :] = du * sgr
            dag_ref[rows, :] = du * avr * sgr * (1.0 - sgr)
            dw = _acc_list(dw, _conv_dw_rows(dy_ref[rows, :], pad_ref, CONV_A_WIDTH, r * cr, cr))
        for j in range(CONV_A_WIDTH):
            dw_ref[j:j + 1, :] = dw[j]

    col = lambda off: pl.BlockSpec((t, LANES), lambda j: (0, j + off))
    wsp = pl.BlockSpec((CONV_A_WIDTH, LANES), lambda j: (0, j))
    return _pc(body, name="ev_a_conv_bwd", grid=(nb,), in_specs=[col(0), col(0), col(nb), wsp],
               out_specs=(col(0), col(0), wsp),
               out_shape=(S((t, D_A), f32), S((t, D_A), f32), S((CONV_A_WIDTH, D_A), f32)),
               scratch_shapes=[pltpu.VMEM((t + CONV_LEAD, LANES), f32), pltpu.VMEM((t + CONV_LEAD, LANES), f32)],
               compiler_params=_cparams(("arbitrary",)))(duc, p, p, conv_a)


def _ev_b(p, conv_b, y):
    t = p.shape[0]
    cr = _row_block8(t)
    nb = D_A // LANES

    def body(gb_ref, gc_ref, xi_ref, w_ref, y_ref, o_ref, pad_ref, stage_ref):
        _fill_front_padded(pad_ref, gc_ref[...] * xi_ref[...], t)
        for r in range(t // cr):
            rows = slice(r * cr, (r + 1) * cr)
            stage_ref[rows, :] = gb_ref[rows, :] * _conv_rows(pad_ref, w_ref, CONV_B_WIDTH, r * cr, cr)
        o_ref[...] = stage_ref[...].astype(bf16)

    col = lambda off: pl.BlockSpec((t, LANES), lambda j: (0, j + off))
    return _pc(body, name="ev_b", grid=(nb,),
               in_specs=[col(2 * nb), col(3 * nb), col(4 * nb), pl.BlockSpec((CONV_B_WIDTH, LANES), lambda j: (0, j)), HBM],
               out_specs=col(nb), out_shape=S(y.shape, bf16), input_output_aliases={4: 0},
               scratch_shapes=[pltpu.VMEM((t + CONV_LEAD, LANES), f32), pltpu.VMEM((t, LANES), f32)],
               compiler_params=_cparams(("arbitrary",)))(p, p, p, conv_b, y)


def _ev_b_bwd(dy, p, conv_b):
    t = p.shape[0]
    cr = _row_block8(t)
    nb = D_A // LANES

    def body(dy_ref, gb_ref, gc_ref, xi_ref, w_ref, dgb_ref, dgc_ref, dxi_ref, dw_ref, pad_ref, padb_ref):
        _fill_front_padded(pad_ref, gc_ref[...] * xi_ref[...], t)
        _fill_back_padded(padb_ref, dy_ref[...] * gb_ref[...], t)
        dw = None
        for r in range(t // cr):
            rows = slice(r * cr, (r + 1) * cr)
            dgb_ref[rows, :] = dy_ref[rows, :] * _conv_rows(pad_ref, w_ref, CONV_B_WIDTH, r * cr, cr)
            dcx = _conv_t_rows(padb_ref, w_ref, CONV_B_WIDTH, r * cr, cr)
            dgc_ref[rows, :] = dcx * xi_ref[rows, :]
            dxi_ref[rows, :] = dcx * gc_ref[rows, :]
            dw = _acc_list(dw, _conv_dw_rows(padb_ref[rows, :], pad_ref, CONV_B_WIDTH, r * cr, cr))
        for j in range(CONV_B_WIDTH):
            dw_ref[j:j + 1, :] = dw[j]

    col = lambda off: pl.BlockSpec((t, LANES), lambda j: (0, j + off))
    wsp = pl.BlockSpec((CONV_B_WIDTH, LANES), lambda j: (0, j))
    return _pc(body, name="ev_b_bwd", grid=(nb,), in_specs=[col(nb), col(2 * nb), col(3 * nb), col(4 * nb), wsp],
               out_specs=(col(0), col(0), col(0), wsp),
               out_shape=(S((t, D_A), f32), S((t, D_A), f32), S((t, D_A), f32), S((CONV_B_WIDTH, D_A), f32)),
               scratch_shapes=[pltpu.VMEM((t + CONV_LEAD, LANES), f32), pltpu.VMEM((t + CONV_LEAD, LANES), f32)],
               compiler_params=_cparams(("arbitrary",)))(dy, p, p, p, conv_b)


def _ffn_mid(u, conv_w, conv_b, name):
    t = u.shape[0]
    cr = _row_block8(t)
    nb = D_FF // FF_BLOCK

    def one(gt_ref, vl_ref, w_ref, b_ref, o_ref, pad_ref, stage_ref):
        _fill_front_padded(pad_ref, gt_ref[...], t)
        for r in range(t // cr):
            rows = slice(r * cr, (r + 1) * cr)
            gc = _conv_rows(pad_ref, w_ref, FF_CONV_WIDTH, r * cr, cr) + b_ref[...]
            stage_ref[rows, :] = gc * _sigmoid(gc) * vl_ref[rows, :]
        o_ref[...] = stage_ref[...].astype(bf16)

    def body(*refs):
        for h in range(FF_BLOCK // LANES):
            one(*[r.at[:, pl.ds(h * LANES, LANES)] for r in refs[:5]], *refs[5:])

    col = lambda off: pl.BlockSpec((t, FF_BLOCK), lambda j: (0, j + off))
    return _pc(body, name=name, grid=(nb,),
               in_specs=[col(0), col(nb), pl.BlockSpec((FF_CONV_WIDTH, FF_BLOCK), lambda j: (0, j)),
                         pl.BlockSpec((1, FF_BLOCK), lambda j: (0, j))],
               out_specs=col(0), out_shape=S((t, D_FF), bf16),
               scratch_shapes=[pltpu.VMEM((t + CONV_LEAD, LANES), f32), pltpu.VMEM((t, LANES), f32)],
               compiler_params=_cparams(("arbitrary",)))(u, u, conv_w, conv_b.reshape(1, D_FF))


def _ffn_mid_bwd(dz, u, conv_w, conv_b, name):
    t = u.shape[0]
    cr = _row_block8(t)
    nb = D_FF // FF_BLOCK
    nh = FF_BLOCK // LANES

    def body(*refs):
        for h in range(nh):
            one(*[r.at[:, pl.ds(h * LANES, LANES)] for r in refs[:9]], *refs[9:])

    def one(dz_ref, gt_ref, vl_ref, w_ref, b_ref, du_ref, dv_ref, dw_ref, db_ref, pad_ref, padb_ref, stage_ref):
        _fill_front_padded(pad_ref, gt_ref[...], t)
        dw, db = None, None
        for r in range(t // cr):
            rows = slice(r * cr, (r + 1) * cr)
            lo = CONV_LEAD + r * cr - (FF_CONV_WIDTH - 1)
            taps = [pad_ref[lo + j:lo + j + cr, :] for j in range(FF_CONV_WIDTH)]
            gc = sum(w_ref[j:j + 1, :] * taps[j] for j in range(FF_CONV_WIDTH)) + b_ref[...]
            sg = _sigmoid(gc)
            dzr = dz_ref[rows, :]
            stage_ref[rows, :] = dzr * gc * sg
            dgc = dzr * vl_ref[rows, :] * sg * (1.0 + gc * (1.0 - sg))
            padb_ref[rows, :] = dgc
            dw = _acc_list(dw, [jnp.sum(dgc * tap, axis=0, keepdims=True) for tap in taps])
            pb = jnp.sum(dgc, axis=0, keepdims=True)
            db = pb if db is None else db + pb
        padb_ref[t:t + CONV_LEAD, :] = jnp.zeros((CONV_LEAD, LANES), f32)
        for r in range(t // cr):
            pad_ref[r * cr:(r + 1) * cr, :] = _conv_t_rows(padb_ref, w_ref, FF_CONV_WIDTH, r * cr, cr)
        du_ref[...] = pad_ref[0:t, :].astype(du_ref.dtype)
        dv_ref[...] = stage_ref[...].astype(dv_ref.dtype)
        for j in range(FF_CONV_WIDTH):
            dw_ref[j:j + 1, :] = dw[j]
        db_ref[...] = db

    col = lambda off: pl.BlockSpec((t, FF_BLOCK), lambda j: (0, j + off))
    wsp = pl.BlockSpec((FF_CONV_WIDTH, FF_BLOCK), lambda j: (0, j))
    bsp = pl.BlockSpec((1, FF_BLOCK), lambda j: (0, j))
    return _pc(body, name=name, grid=(nb,), in_specs=[col(0), col(0), col(nb), wsp, bsp],
               out_specs=(col(0), col(0), wsp, bsp),
               out_shape=(S((t, D_FF), bf16), S((t, D_FF), bf16), S((FF_CONV_WIDTH, D_FF), f32), S((1, D_FF), f32)),
               scratch_shapes=[pltpu.VMEM((t + CONV_LEAD, LANES), f32), pltpu.VMEM((t + CONV_LEAD, LANES), f32),
                               pltpu.VMEM((t, LANES), f32)],
               compiler_params=_cparams(("arbitrary",)))(dz, u, u, conv_w, conv_b.reshape(1, D_FF))


def _swap_halves(x):
    w = x.shape[1]
    lane = lax.broadcasted_iota(jnp.int32, x.shape, 1) % HEAD_DIM
    return jnp.where(lane < HEAD_DIM // 2, pltpu.roll(x, w - HEAD_DIM // 2, axis=1), pltpu.roll(x, HEAD_DIM // 2, axis=1))


def _rope_pack(patt, c64, s64):
    t = patt.shape[0]
    tp = t + ATT_PAD

    def body(p_ref, c_ref, s_ref, q_ref, k_ref, v_ref):
        c, s = c_ref[...], s_ref[...]

        def rope(x, nh):
            cc = jnp.concatenate([c] * nh, axis=1)
            ss = jnp.concatenate([s] * nh, axis=1)
            return x * cc + _swap_halves(x) * ss

        for ref, val in ((q_ref, rope(p_ref[:, 0:D_ATT], N_Q_HEADS)),
                         (k_ref, rope(p_ref[:, D_ATT:D_ATT + D_KV], N_KV_HEADS)),
                         (v_ref, p_ref[:, D_ATT + D_KV:ATT_COLS])):
            ref[0:ATT_PAD, :] = jnp.zeros((ATT_PAD, val.shape[1]), bf16)
            ref[ATT_PAD:tp, :] = val.astype(bf16)

    return _pc(body, name="rope_pack", in_specs=[_full((t, ATT_COLS)), _full((t, HEAD_DIM)), _full((t, HEAD_DIM))],
               out_specs=(_full((tp, D_ATT)), _full((tp, D_KV)), _full((tp, D_KV))), grid=(1,),
               out_shape=(S((tp, D_ATT), bf16), S((tp, D_KV), bf16), S((tp, D_KV), bf16)),
               compiler_params=_cparams(("arbitrary",)))(patt, c64, s64)


def _rope_bwd(dqp, dkp, dvp, c64, s64):
    tp = dqp.shape[0]
    t = tp - ATT_PAD

    def body(dq_ref, dk_ref, dv_ref, c_ref, s_ref, o_ref):
        c, s = c_ref[...], s_ref[...]

        def unrope(dy, nh):
            cc = jnp.concatenate([c] * nh, axis=1)
            ss = jnp.concatenate([s] * nh, axis=1)
            return dy * cc + _swap_halves(dy * ss)

        o_ref[:, 0:D_ATT] = unrope(dq_ref[ATT_PAD:tp, :], N_Q_HEADS).astype(bf16)
        o_ref[:, D_ATT:D_ATT + D_KV] = unrope(dk_ref[ATT_PAD:tp, :], N_KV_HEADS).astype(bf16)
        o_ref[:, D_ATT + D_KV:ATT_COLS] = dv_ref[ATT_PAD:tp, :].astype(bf16)

    return _pc(body, name="rope_bwd", grid=(1,),
               in_specs=[_full((tp, D_ATT)), _full((tp, D_KV)), _full((tp, D_KV)), _full((t, HEAD_DIM)), _full((t, HEAD_DIM))],
               out_specs=_full((t, ATT_COLS)), out_shape=S((t, ATT_COLS), bf16),
               compiler_params=_cparams(("arbitrary",)))(dqp, dkp, dvp, c64, s64)


def _attn_masks(n):
    rows = GQA_GROUP * BLOCK
    ri = lax.broadcasted_iota(jnp.int32, (rows, BLOCK), 0) % BLOCK
    ci = lax.broadcasted_iota(jnp.int32, (rows, BLOCK), 1)
    m_cur = (ci <= ri) & (ci >= jnp.where(n >= 1, 0, ATT_PAD))
    m_prev = ci > ri + jnp.where(n >= 2, 0, BLOCK)
    m_meta = ci >= jnp.where(n >= 1, ATT_PAD, BLOCK)
    return m_cur, m_prev, m_meta


def _attn_probs(qg, kc, kp, km, masks, skv):
    def scores(k, m):
        s = lax.dot_general(qg, k, _DIMS["nt"], preferred_element_type=f32) * ATT_SCALE
        return jnp.where(m, s, NEG_INF)
    s_c, s_p, s_m = scores(kc, masks[0]), scores(kp, masks[1]), scores(km, masks[2])
    mx = jnp.maximum(jnp.maximum(jnp.max(s_c, axis=-1, keepdims=True), jnp.max(s_p, axis=-1, keepdims=True)),
                     jnp.maximum(jnp.max(s_m, axis=-1, keepdims=True), skv))
    e_c, e_p, e_m, e_s = jnp.exp(s_c - mx), jnp.exp(s_p - mx), jnp.exp(s_m - mx), jnp.exp(skv - mx)
    den = (jnp.sum(e_c, axis=-1, keepdims=True) + jnp.sum(e_p, axis=-1, keepdims=True)
           + jnp.sum(e_m, axis=-1, keepdims=True) + e_s)
    inv = 1.0 / den
    return e_c * inv, e_p * inv, e_m * inv, e_s * inv


def _sink_rows(sk_ref, g):
    hrow = lax.broadcasted_iota(jnp.int32, (GQA_GROUP * BLOCK, 1), 0) // BLOCK
    skv = jnp.zeros((GQA_GROUP * BLOCK, 1), f32)
    for hh in range(GQA_GROUP):
        skv = jnp.where(hrow == hh, sk_ref[0, GQA_GROUP * g + hh], skv)
    return skv, hrow


def _stack_heads(ref, g):
    return jnp.concatenate([ref[:, (GQA_GROUP * g + hh) * HEAD_DIM:(GQA_GROUP * g + hh + 1) * HEAD_DIM]
                            for hh in range(GQA_GROUP)], axis=0)


def _attn_specs():
    blk = lambda w: pl.BlockSpec((BLOCK, w), lambda n: (n, 0))
    prev = pl.BlockSpec((BLOCK, D_KV), lambda n: (jnp.maximum(n - 1, 0), 0))
    meta = pl.BlockSpec((BLOCK, D_KV), lambda n: (0, 0))
    return blk, prev, meta


def _attn_fwd(qp, kp, vp, sinks):
    tp = qp.shape[0]
    blk, prev, meta = _attn_specs()

    def body(sk_ref, q_ref, kc_ref, kp_ref, km_ref, vc_ref, vp_ref, vm_ref, o_ref):
        masks = _attn_masks(pl.program_id(0))
        for g in range(N_KV_HEADS):
            sl = slice(g * HEAD_DIM, (g + 1) * HEAD_DIM)
            skv, _ = _sink_rows(sk_ref, g)
            p_c, p_p, p_m, _ = _attn_probs(_stack_heads(q_ref, g), kc_ref[:, sl], kp_ref[:, sl], km_ref[:, sl], masks, skv)
            o = (jnp.dot(p_c.astype(bf16), vc_ref[:, sl], preferred_element_type=f32)
                 + jnp.dot(p_p.astype(bf16), vp_ref[:, sl], preferred_element_type=f32)
                 + jnp.dot(p_m.astype(bf16), vm_ref[:, sl], preferred_element_type=f32))
            for hh in range(GQA_GROUP):
                h = GQA_GROUP * g + hh
                o_ref[:, h * HEAD_DIM:(h + 1) * HEAD_DIM] = o[hh * BLOCK:(hh + 1) * BLOCK].astype(bf16)

    return _pc(body, name="attn_fwd", grid=(tp // BLOCK,),
               in_specs=[pl.BlockSpec(memory_space=pltpu.SMEM), blk(D_ATT), blk(D_KV), prev, meta, blk(D_KV), prev, meta],
               out_specs=blk(D_ATT), out_shape=S((tp, D_ATT), bf16),
               compiler_params=_cparams(("arbitrary",)))(sinks, qp, kp, kp, kp, vp, vp, vp)


def _attn_bwd(qp, kp, vp, sinks, dop):
    tp = qp.shape[0]
    blk, prev, meta = _attn_specs()

    def body(sk_ref, q_ref, kc_ref, kp_ref, km_ref, vc_ref, vp_ref, vm_ref, do_ref, dq_ref, dk_ref, dv_ref, dsk_ref):
        n = pl.program_id(0)

        @pl.when(n == 0)
        def _():
            dk_ref[...] = jnp.zeros_like(dk_ref)
            dv_ref[...] = jnp.zeros_like(dv_ref)
            dsk_ref[...] = jnp.zeros_like(dsk_ref)
        masks = _attn_masks(n)
        cur = pl.ds(pl.multiple_of(n * BLOCK, BLOCK), BLOCK)
        prv = pl.ds(pl.multiple_of(jnp.maximum(n - 1, 0) * BLOCK, BLOCK), BLOCK)
        lane = lax.broadcasted_iota(jnp.int32, (1, LANES), 1)
        dsk = jnp.zeros((1, LANES), f32)
        for g in range(N_KV_HEADS):
            sl = slice(g * HEAD_DIM, (g + 1) * HEAD_DIM)
            skv, hrow = _sink_rows(sk_ref, g)
            qg = _stack_heads(q_ref, g)
            dog = _stack_heads(do_ref, g)
            ks = (kc_ref[:, sl], kp_ref[:, sl], km_ref[:, sl])
            vs = (vc_ref[:, sl], vp_ref[:, sl], vm_ref[:, sl])
            probs = _attn_probs(qg, ks[0], ks[1], ks[2], masks, skv)
            dps = [lax.dot_general(dog, v, _DIMS["nt"], preferred_element_type=f32) for v in vs]
            delta = sum(jnp.sum(p * dp, axis=-1, keepdims=True) for p, dp in zip(probs[:3], dps))
            dss = [(p * (dp - delta) * ATT_SCALE).astype(bf16) for p, dp in zip(probs[:3], dps)]
            dq = sum(jnp.dot(ds, k, preferred_element_type=f32) for ds, k in zip(dss, ks))
            for hh in range(GQA_GROUP):
                h = GQA_GROUP * g + hh
                dq_ref[:, h * HEAD_DIM:(h + 1) * HEAD_DIM] = dq[hh * BLOCK:(hh + 1) * BLOCK]
                dsk = dsk + jnp.where(lane == h, -jnp.sum(jnp.where(hrow == hh, probs[3] * delta, 0.0)), 0.0)
            for rows, p, ds in zip((cur, prv, slice(0, BLOCK)), probs[:3], dss):
                dv_ref[rows, sl] += lax.dot_general(p.astype(bf16), dog, _DIMS["tn"], preferred_element_type=f32)
                dk_ref[rows, sl] += lax.dot_general(ds, qg, _DIMS["tn"], preferred_element_type=f32)
        dsk_ref[...] += dsk

    return _pc(body, name="attn_bwd", grid=(tp // BLOCK,),
               in_specs=[pl.BlockSpec(memory_space=pltpu.SMEM), blk(D_ATT), blk(D_KV), prev, meta, blk(D_KV), prev, meta,
                         blk(D_ATT)],
               out_specs=(blk(D_ATT), _full((tp, D_KV)), _full((tp, D_KV)), _full((1, LANES))),
               out_shape=(S((tp, D_ATT), f32), S((tp, D_KV), f32), S((tp, D_KV), f32), S((1, LANES), f32)),
               compiler_params=_cparams(("arbitrary",)))(sinks, qp, kp, kp, kp, vp, vp, vp, dop)


def _seg(x, bm):
    hi = x.astype(bf16)
    lo = (x - hi.astype(f32)).astype(bf16)
    return jnp.dot(jnp.concatenate([hi, lo], axis=1), bm, preferred_element_type=f32)


@jax.custom_vjp
def _seg_linear(x, bm):
    return _seg(x, bm)


_seg_linear.defvjp(lambda x, bm: (_seg(x, bm), bm), lambda bm, ct: (_seg(ct, bm), jnp.zeros_like(bm)))


def _softplus(y):
    return jnp.maximum(y, 0.0) + jnp.log(1.0 + jnp.exp(-jnp.abs(y)))


def _prep_fn(xr, xk, xwd, xad, xgd, w0, w2, a0, a2, g2, k_k, k_a, bm, seg=_seg):
    xw = w0 + jnp.dot(jnp.tanh(xwd), w2, preferred_element_type=f32)
    decay = jnp.exp(-jnp.exp(-_softplus(-xw) - 0.5))
    alpha = _sigmoid(a0 + jnp.dot(xad, a2, preferred_element_type=f32))
    g = jnp.dot(_sigmoid(xgd), g2, preferred_element_type=f32)
    kk = xk * k_k
    kkn = kk / jnp.maximum(jnp.sqrt(seg(kk * kk, bm)), 1e-12)
    k2 = xk * (1.0 + (alpha - 1.0) * k_a)
    return decay, k2, -kkn, kkn * alpha, g


def _split_cols(x):
    o1, o2, o3 = 3 * D_R, 3 * D_R + LORA_W, 3 * D_R + LORA_W + LORA_A
    return x[:, 0:D_R], x[:, D_R:2 * D_R], x[:, 2 * D_R:o1], x[:, o1:o2], x[:, o2:o3], x[:, o3:RWKV_COLS]


def _shifted(sh_ref, x, halo, first, rb):
    sh_ref[0:SUBLANES, :] = jnp.where(first, 0.0, halo)
    sh_ref[SUBLANES:SUBLANES + rb, :] = x
    return sh_ref[SUBLANES - 1:SUBLANES - 1 + rb, :]


_PREP_PARAMS = ("od_w0", "od_w2", "od_a0", "od_a2", "od_g2", "od_k_k", "od_k_a")


def _rwkv_prep(pr, mu, params, bm):
    t = pr.shape[0]
    rb = _row_block8(t)
    hb = rb // SUBLANES

    def body(pr_ref, halo_ref, mu_ref, w0, w2, a0, a2, g2, kk_ref, ka_ref, bm_ref, *outs_sh):
        outs, sh_ref = outs_sh[:-1], outs_sh[-1]
        x = pr_ref[...]
        prev = _shifted(sh_ref, x, halo_ref[...], pl.program_id(0) == 0, rb)
        xr, xk, xv, xwd, xad, xgd = _split_cols(x + (prev - x) * mu_ref[...])
        bmv = bm_ref[...]
        decay, k2, a_s, b_s, g = _prep_fn(xr, xk, xwd, xad, xgd, w0[...], w2[...], a0[...], a2[...], g2[...],
                                          kk_ref[...], ka_ref[...], bmv)
        vals = (xr, xv, decay, k2, a_s, b_s, decay * xr, _seg(b_s * xr, bmv), _seg(k2 * xr, bmv), g)
        for ref, val in zip(outs, vals):
            ref[...] = val

    row = pl.BlockSpec((rb, RWKV_COLS), lambda i: (i, 0))
    halo = pl.BlockSpec((SUBLANES, RWKV_COLS), lambda i: (jnp.maximum(i * hb - 1, 0), 0))
    orow = pl.BlockSpec((rb, D_R), lambda i: (i, 0))
    return _pc(body, name="rwkv_prep", grid=(t // rb,),
               in_specs=[row, halo, _full((1, RWKV_COLS))] + [_full(p.shape) for p in params] + [_full(bm.shape)],
               out_specs=(orow,) * 10, out_shape=(S((t, D_R), f32),) * 10,
               scratch_shapes=[pltpu.VMEM((rb + SUBLANES, RWKV_COLS), f32)],
               compiler_params=_cparams(("arbitrary",)))(pr, pr, mu, *params, bm)


def _rwkv_prep_bwd(pr, mu, params, bm, cts):
    t = pr.shape[0]
    rb = _row_block8(t)
    hb = rb // SUBLANES
    counts = [len(c) for c in cts]
    flat = [a for c in cts for a in c]

    def body(pr_ref, halo_ref, mu_ref, w0, w2, a0, a2, g2, kk_ref, ka_ref, bm_ref, *rest):
        ct_refs, rest = rest[:len(flat)], rest[len(flat):]
        dx_ref, dmu_ref = rest[0], rest[1]
        dpar_refs, sh_ref = rest[2:9], rest[9]

        @pl.when(pl.program_id(0) == 0)
        def _():
            dmu_ref[...] = jnp.zeros_like(dmu_ref)
            for r in dpar_refs:
                r[...] = jnp.zeros_like(r)
        sums, pos = [], 0
        for c in counts:
            sums.append(sum(r[...] for r in ct_refs[pos:pos + c]))
            pos += c
        x = pr_ref[...]
        prev = _shifted(sh_ref, x, halo_ref[...], pl.program_id(0) == 0, rb)
        xr, xk, xv, xwd, xad, xgd = _split_cols(x + (prev - x) * mu_ref[...])
        bmv = bm_ref[...]
        _, vjp = jax.vjp(lambda *a: _prep_fn(*a, bmv, _seg_linear), xr, xk, xwd, xad, xgd, w0[...], w2[...], a0[...], a2[...],
                         g2[...], kk_ref[...], ka_ref[...])
        grads = vjp(tuple(sums[:5]))
        dxr, dxk, dxwd, dxad, dxgd = grads[:5]
        o1, o2, o3 = 3 * D_R, 3 * D_R + LORA_W, 3 * D_R + LORA_W + LORA_A
        dx_ref[:, 0:D_R] = dxr + sums[5]
        dx_ref[:, D_R:2 * D_R] = dxk
        dx_ref[:, 2 * D_R:o1] = sums[6]
        dx_ref[:, o1:o2] = dxwd
        dx_ref[:, o2:o3] = dxad
        dx_ref[:, o3:RWKV_COLS] = dxgd
        dmu_ref[...] += jnp.sum(dx_ref[...] * (prev - x), axis=0, keepdims=True)
        for r, gval in zip(dpar_refs, grads[5:]):
            r[...] += gval

    row = pl.BlockSpec((rb, RWKV_COLS), lambda i: (i, 0))
    halo = pl.BlockSpec((SUBLANES, RWKV_COLS), lambda i: (jnp.maximum(i * hb - 1, 0), 0))
    crow = pl.BlockSpec((rb, D_R), lambda i: (i, 0))
    return _pc(body, name="rwkv_prep_bwd", grid=(t // rb,),
               in_specs=[row, halo, _full((1, RWKV_COLS))] + [_full(p.shape) for p in params] + [_full(bm.shape)]
               + [crow] * len(flat),
               out_specs=(row, _full((1, RWKV_COLS))) + tuple(_full(p.shape) for p in params),
               out_shape=(S((t, RWKV_COLS), f32), S((1, RWKV_COLS), f32)) + tuple(S(p.shape, f32) for p in params),
               scratch_shapes=[pltpu.VMEM((rb + SUBLANES, RWKV_COLS), f32)],
               compiler_params=_cparams(("arbitrary",)))(pr, pr, mu, *params, bm, *flat)


def _shift_bwd(dxs, mu):
    t = dxs.shape[0]
    rb = _row_block(t)
    hb = rb // SUBLANES
    nblk = t // rb

    def body(dx_ref, halo_ref, mu_ref, o_ref, sh_ref):
        dx = dx_ref[...]
        sh_ref[0:rb, :] = dx
        sh_ref[rb:rb + SUBLANES, :] = jnp.where(pl.program_id(0) == nblk - 1, 0.0, halo_ref[...])
        m = mu_ref[...]
        o_ref[...] = (dx * (1.0 - m) + sh_ref[1:1 + rb, :] * m).astype(bf16)

    row = pl.BlockSpec((rb, RWKV_COLS), lambda i: (i, 0))
    halo = pl.BlockSpec((SUBLANES, RWKV_COLS), lambda i: (jnp.minimum((i + 1) * hb, t // SUBLANES - 1), 0))
    return _pc(body, name="rwkv_shift_bwd", grid=(nblk,), in_specs=[row, halo, _full((1, RWKV_COLS))],
               out_specs=row, out_shape=S((t, RWKV_COLS), bf16),
               scratch_shapes=[pltpu.VMEM((rb + SUBLANES, RWKV_COLS), f32)],
               compiler_params=_cparams(("arbitrary",)))(dxs, dxs, mu)


def _post_fn(y, xr, k2, xv, g, lg, lb, rk, bm, seg=_seg):
    inv_n = 1.0 / HEAD_DIM
    yc = y - seg(y, bm) * inv_n
    var = seg(yc * yc, bm) * inv_n
    yn = yc * lax.rsqrt(var + RWKV_GN_EPS) * lg + lb
    return (yn + seg(xr * k2 * rk, bm) * xv) * g


def _rwkv_post(y, xr, k2, xv, g, lg, lb, rk, bm):
    t = y.shape[0]
    rb = _row_block8(t)

    def body(y_ref, xr_ref, k2_ref, xv_ref, g_ref, lg_ref, lb_ref, rk_ref, bm_ref, o_ref):
        o_ref[...] = _post_fn(y_ref[...], xr_ref[...], k2_ref[...], xv_ref[...], g_ref[...], lg_ref[...], lb_ref[...],
                              rk_ref[...], bm_ref[...])

    row = pl.BlockSpec((rb, D_R), lambda i: (i, 0))
    vec = _full((1, D_R))
    return _pc(body, name="rwkv_post", grid=(t // rb,), in_specs=[row] * 5 + [vec] * 3 + [_full(bm.shape)],
               out_specs=row, out_shape=S((t, D_R), f32),
               compiler_params=_cparams(("arbitrary",)))(y, xr, k2, xv, g, lg, lb, rk, bm)


def _rwkv_post_bwd(dy1, y, xr, k2, xv, g, lg, lb, rk, bm):
    t = y.shape[0]
    rb = _row_block8(t)

    def body(dy_ref, y_ref, xr_ref, k2_ref, xv_ref, g_ref, lg_ref, lb_ref, rk_ref, bm_ref, *outs):
        @pl.when(pl.program_id(0) == 0)
        def _():
            for r in outs[5:]:
                r[...] = jnp.zeros_like(r)
        bmv = bm_ref[...]
        _, vjp = jax.vjp(lambda *a: _post_fn(*a, bmv, _seg_linear), y_ref[...], xr_ref[...], k2_ref[...], xv_ref[...], g_ref[...],
                         lg_ref[...], lb_ref[...], rk_ref[...])
        grads = vjp(dy_ref[...])
        for r, gval in zip(outs[:5], grads[:5]):
            r[...] = gval
        for r, gval in zip(outs[5:], grads[5:]):
            r[...] += gval

    row = pl.BlockSpec((rb, D_R), lambda i: (i, 0))
    vec = _full((1, D_R))
    return _pc(body, name="rwkv_post_bwd", grid=(t // rb,),
               in_specs=[pl.BlockSpec((rb, D_R), lambda i: (i, 1))] + [row] * 5 + [vec] * 3 + [_full(bm.shape)],
               out_specs=(row,) * 5 + (vec,) * 3, out_shape=(S((t, D_R), f32),) * 5 + (S((1, D_R), f32),) * 3,
               compiler_params=_cparams(("arbitrary",)))(dy1, y, xr, k2, xv, g, lg, lb, rk, bm)


def _seg2(x, bb):
    hi = x.astype(bf16)
    lo = (x - hi.astype(f32)).astype(bf16)
    return jnp.dot(jnp.concatenate([hi, lo], axis=1), bb, preferred_element_type=f32)


def _row4(rows, j):
    return jnp.concatenate([jnp.broadcast_to(rows[j:j + 1, p * LANES:(p + 1) * LANES], (HEAD_DIM, LANES))
                            for p in range(4)], axis=0)


def _scan_consts():
    lane_group = jnp.arange(LANES) // HEAD_DIM
    b128 = (lane_group[:, None] == lane_group[None, :]).astype(bf16)
    bb = jnp.concatenate([b128, b128], axis=0)
    qsel = (jnp.arange(PAIR_ROWS)[:, None] % HEAD_DIM == jnp.arange(LANES)[None, :] % HEAD_DIM).astype(f32)
    return bb, qsel


def _store_cols(acc_ref, o_ref, tc):
    for p in range(4):
        blk = acc_ref[p * HEAD_DIM:(p + 1) * HEAD_DIM, :].T
        o_ref[:, (2 * p) * HEAD_DIM:(2 * p + 1) * HEAD_DIM] = blk[0:tc]
        o_ref[:, (2 * p + 1) * HEAD_DIM:(2 * p + 2) * HEAD_DIM] = blk[HEAD_DIM:HEAD_DIM + tc]


PAIR_GROUP = 2 * SUBLANES


def _rwkv_pairs(w, a, b, k, v, wr, br, kr, bm):
    t = w.shape[0]
    rb = _row_block8(t)

    def body(w_ref, a_ref, b_ref, k_ref, v_ref, wr_ref, br_ref, kr_ref, bm_ref, *outs_sh):
        outs, sh_ref = outs_sh[:-1], outs_sh[-1]

        def second(ref):
            sh_ref[0:rb, :] = ref[...]
            sh_ref[rb:rb + SUBLANES, :] = jnp.zeros((SUBLANES, D_R), f32)
            return sh_ref[1:1 + rb, :]

        w1, a1, b1, k1, v1 = w_ref[...], a_ref[...], b_ref[...], k_ref[...], v_ref[...]
        w2, a2, wr2, br2, kr2, v2 = (second(r) for r in (w_ref, a_ref, wr_ref, br_ref, kr_ref, v_ref))
        bmv = bm_ref[...]
        beta, kappa = _seg(b1 * a2, bmv), _seg(k1 * a2, bmv)
        bwr2, kwr2 = _seg(b1 * wr2, bmv), _seg(k1 * wr2, bmv)
        w1a2 = w1 * a2
        vals = (w1a2 + a1 * beta, v1 * kappa,
                wr_ref[...] + a1 * br_ref[...], v1 * kr_ref[...],
                w1 * wr2 + a1 * (bwr2 + beta * br2) + w1a2 * br2,
                v1 * (kwr2 + kappa * br2) + v2 * kr2,
                w1 * w2, b1 * w2, k1 * w2)
        for ref, val in zip(outs, vals):
            ref[...] = val

    row = pl.BlockSpec((rb, D_R), lambda i: (i, 0))
    return _pc(body, name="rwkv_pairs", grid=(t // rb,), in_specs=[row] * 8 + [_full(bm.shape)],
               out_specs=(row,) * 9, out_shape=(S((t, D_R), f32),) * 9,
               scratch_shapes=[pltpu.VMEM((rb + SUBLANES, D_R), f32)],
               compiler_params=_cparams(("arbitrary",)))(w, a, b, k, v, wr, br, kr, bm)


def _wkv_fwd(k, v, a, b, pairs):
    t = k.shape[0]
    tc = SCAN_CHUNK
    bb, qsel = _scan_consts()

    def body(*refs):
        k16, v16, a16, b16 = refs[0:4]
        ca2p, da2p, c1p, d1p, c2p, d2p, w12p, b1wp, k1wp = refs[4:13]
        bb_ref, q_ref, y_ref, st_ref, sa_ref, vb_ref, s_scr, yacc = refs[13:]

        @pl.when(pl.program_id(0) == 0)
        def _():
            s_scr[...] = jnp.zeros_like(s_scr)
        bbv, qp = bb_ref[...], q_ref[0:HEAD_DIM, :]
        lane = lax.broadcasted_iota(jnp.int32, (HEAD_DIM, LANES), 1) % HEAD_DIM

        def halves(x):
            hi = x.astype(bf16)
            return jnp.concatenate([hi, (x - hi.astype(f32)).astype(bf16)], axis=1)

        def group(gi, s):
            base = pl.multiple_of(gi * PAIR_GROUP, PAIR_GROUP)

            def rows8(ref, j):
                return ref[pl.ds(base + (j // SUBLANES) * SUBLANES, SUBLANES), :]

            def bcast(rows, j, p):
                return jnp.broadcast_to(rows[j % SUBLANES:j % SUBLANES + 1, p * LANES:(p + 1) * LANES], (HEAD_DIM, LANES))

            step = lambda ref, j, p: bcast(rows8(ref, j), j, p)
            for q in range(SUBLANES):
                j1, j2 = 2 * q, 2 * q + 1
                t1 = base + j1
                nxt = []
                for p in range(4):
                    sl = slice(p * HEAD_DIM, (p + 1) * HEAD_DIM)
                    sp = s[sl]
                    lhs = [halves(jnp.concatenate([sp * step(a16, j1, p),
                                                   sp * step(ca2p, j1, p) + qp * step(da2p, j1, p),
                                                   sp * step(c1p, j1, p) + qp * step(d1p, j1, p),
                                                   sp * step(c2p, j1, p) + qp * step(d2p, j1, p)], axis=0))]
                    for j in (j1, j2):
                        v8 = rows8(v16, j)
                        vh8 = v8.astype(bf16).astype(f32)
                        lhs.append(jnp.concatenate([(qp * bcast(vh8, j, p)).astype(bf16),
                                                    (qp * bcast(v8 - vh8, j, p)).astype(bf16)], axis=1))
                    r = jnp.dot(jnp.concatenate(lhs, axis=0), bbv, preferred_element_type=f32)
                    sa1, sa2, y1, y2, vb1, vb2 = (r[n * HEAD_DIM:(n + 1) * HEAD_DIM] for n in range(6))
                    yacc[sl, :] = jnp.where(lane == t1, y1, jnp.where(lane == t1 + 1, y2, yacc[sl, :]))
                    st_ref[base // 2 + q, sl, :] = sp
                    sa_ref[t1, sl, :] = sa1
                    sa_ref[t1 + 1, sl, :] = sa2
                    vb_ref[t1, sl, :] = vb1
                    vb_ref[t1 + 1, sl, :] = vb2
                    nxt.append(((sp * step(w12p, j1, p) + sa1 * step(b1wp, j1, p)) + vb1 * step(k1wp, j1, p))
                               + (sa2 * step(b16, j2, p) + vb2 * step(k16, j2, p)))
                s = jnp.concatenate(nxt, axis=0)
            return s

        s_scr[...] = lax.fori_loop(0, tc // PAIR_GROUP, group, s_scr[...])
        _store_cols(yacc, y_ref, tc)

    row = pl.BlockSpec((tc, D_R), lambda c: (c, 0))
    tiles = pl.BlockSpec((tc, PAIR_ROWS, LANES), lambda c: (c, 0, 0))
    return _pc(body, name="wkv_fwd", grid=(t // tc,),
               in_specs=[row] * 13 + [_full(bb.shape), _full(qsel.shape)],
               out_specs=(row, pl.BlockSpec((tc // 2, PAIR_ROWS, LANES), lambda c: (c, 0, 0)), tiles, tiles),
               out_shape=(S((t, D_R), f32), S((t // 2, PAIR_ROWS, LANES), f32)) + (S((t, PAIR_ROWS, LANES), f32),) * 2,
               scratch_shapes=[pltpu.VMEM((PAIR_ROWS, LANES), f32), pltpu.VMEM((PAIR_ROWS, LANES), f32)],
               compiler_params=_cparams(("arbitrary",)))(k, v, a, b, *pairs, bb, qsel)


def _wkv_bwd(sprev, sab, vbb, w, k, a, b, r, dy):
    t = w.shape[0]
    tc = SCAN_CHUNK
    nc = t // tc
    bb, qsel = _scan_consts()

    def body(st_ref, sa_ref, vb_ref, w_ref, k_ref, a_ref, b_ref, r_ref, dy_ref, bb_ref, q_ref,
             dr_ref, dw_ref, dk_ref, dv_ref, da_ref, db_ref, g_scr, dvacc, rows_scr):
        @pl.when(pl.program_id(0) == 0)
        def _():
            g_scr[...] = jnp.zeros_like(g_scr)
        bbv, qv = bb_ref[...], q_ref[...]
        lane64 = lax.broadcasted_iota(jnp.int32, (PAIR_ROWS, LANES), 1) % HEAD_DIM
        outs = (dr_ref, dw_ref, db_ref, dk_ref, da_ref)

        def colsums(slot, j, x):
            for p in range(4):
                rows_scr[slot, j:j + 1, p * LANES:(p + 1) * LANES] = jnp.sum(x[p * HEAD_DIM:(p + 1) * HEAD_DIM], axis=0,
                                                                           keepdims=True)

        def group(i, g):
            base = pl.multiple_of((tc // SUBLANES - 1 - i) * SUBLANES, SUBLANES)
            w8, k8, a8, b8, r8, dy8 = (ref[pl.ds(base, SUBLANES), :] for ref in (w_ref, k_ref, a_ref, b_ref, r_ref, dy_ref))

            def after_step(j, sp):
                return sp * _row4(w8, j) + sa_ref[base + j] * _row4(b8, j) + vb_ref[base + j] * _row4(k8, j)

            def back_step(j, sp, s_t, g):
                tt = base + j
                u, vb = sa_ref[tt], vb_ref[tt]
                a4, b4, w4, k4 = _row4(a8, j), _row4(b8, j), _row4(w8, j), _row4(k8, j)
                dyb = _seg2(qv * _row4(dy8, j), bbv)
                g = g + dyb * _row4(r8, j)
                rr2 = _seg2(jnp.concatenate([g * b4, g * k4], axis=0), bbv)
                du, dvb = rr2[0:PAIR_ROWS], rr2[PAIR_ROWS:2 * PAIR_ROWS]
                for slot, val in enumerate((s_t * dyb, g * sp, g * u, g * vb, sp * du)):
                    colsums(slot, j, val)
                dvacc[...] = jnp.where(lane64 == tt, dvb, dvacc[...])
                return g * w4 + du * a4

            for q in reversed(range(SUBLANES // 2)):
                s0 = st_ref[base // 2 + q]
                s1 = after_step(2 * q, s0)
                g = back_step(2 * q + 1, s1, after_step(2 * q + 1, s1), g)
                g = back_step(2 * q, s0, s1, g)
            for slot, ref in enumerate(outs):
                ref[pl.ds(base, SUBLANES), :] = rows_scr[slot]
            return g

        g_scr[...] = lax.fori_loop(0, tc // SUBLANES, group, g_scr[...])
        _store_cols(dvacc, dv_ref, tc)

    row = pl.BlockSpec((tc, D_R), lambda c: (nc - 1 - c, 0))
    tiles = pl.BlockSpec((tc, PAIR_ROWS, LANES), lambda c: (nc - 1 - c, 0, 0))
    states = pl.BlockSpec((tc // 2, PAIR_ROWS, LANES), lambda c: (nc - 1 - c, 0, 0))
    return _pc(body, name="wkv_bwd", grid=(nc,),
               in_specs=[states, tiles, tiles] + [row] * 6 + [_full(bb.shape), _full(qsel.shape)],
               out_specs=(row,) * 6, out_shape=(S((t, D_R), f32),) * 6,
               scratch_shapes=[pltpu.VMEM((PAIR_ROWS, LANES), f32), pltpu.VMEM((PAIR_ROWS, LANES), f32),
                               pltpu.VMEM((5, SUBLANES, D_R), f32)],
               compiler_params=_cparams(("arbitrary",)))(sprev, sab, vbb, w, k, a, b, r, dy, bb, qsel)


def _rope_tables(t):
    half = HEAD_DIM // 2
    inv = ROPE_THETA ** (-jnp.arange(half, dtype=f32) / half)
    ang = jnp.arange(t, dtype=f32)[:, None] * inv[None, :]
    cos, sin = jnp.cos(ang), jnp.sin(ang)
    return jnp.concatenate([cos, cos], axis=1), jnp.concatenate([-sin, sin], axis=1)


def _head_matrix():
    grp = jnp.arange(D_R) // HEAD_DIM
    b = (grp[:, None] == grp[None, :]).astype(bf16)
    return jnp.concatenate([b, b], axis=0)


def _ffn_fwd(h, hf, get_w, conv_w, conv_b, i, next_gain=None):
    w_up_t = get_w(f"ff{i}_up", hf)
    u = _mm(hf, w_up_t, "nt", f"ffn{i}_up")
    z = _ffn_mid(u, conv_w, conv_b, f"ffn{i}_mid")
    w_down = get_w(f"ff{i}_down", z)
    if next_gain is None:
        h_out, hn_out = _mm(z, w_down, "nn", f"ffn{i}_down", res=h), None
    else:
        h_out, hn_out = _mm_res_norm(z, w_down, h, next_gain, f"ffn{i}_down_norm")
    return h_out, hn_out, (hf, u, z), w_up_t, w_down


def _ffn_bwd(dh, h, saved, g, w_up_t, conv_w, conv_b, w_down, i, put_g):
    hf, u, z = saved
    dz = _mm(dh, w_down, "nt", f"ffn{i}_dz")
    g_down = _mm(z, dh, "tn", f"ffn{i}_gdown", out_dtype=GRAD_WIRE_DTYPE)
    tok = put_g(f"ff{i}_down", g_down)
    dgate, dval, g_conv, g_convb = _ffn_mid_bwd(dz, u, conv_w, conv_b + tok, f"ffn{i}_mid_bwd")
    g_up_t = _mm(dgate, hf, "tn", f"ffn{i}_gup_gate", out_dtype=GRAD_WIRE_DTYPE, out_rows=2 * D_FF)
    g_up_t = _mm(dval, hf, "tn", f"ffn{i}_gup_val", out_dtype=GRAD_WIRE_DTYPE, out_rows=2 * D_FF, out_row0=D_FF, into=g_up_t)
    tok = put_g(f"ff{i}_up", g_up_t)
    dh_in, g_norm = _mm_rms_bwd(dval, w_up_t, h, g + tok, dh, f"ffn{i}_dhf_val_norm_bwd", b_row0=D_FF,
                                res=_mm(dgate, w_up_t, "nn", f"ffn{i}_dhf_gate"))
    return dh_in, dict(conv=g_conv, conv_b=g_convb, norm=g_norm)


def _local_step(x, target, W, get_w, put_g, put_small, tok0):
    t = N_META + x.shape[0]
    c64, s64 = _rope_tables(t)
    bm = _head_matrix()
    h0 = jnp.concatenate([W["meta_tokens"], x], axis=0)

    ev_w_in_t, ev_w_out = get_w("ev_in", None), get_w("ev_out", None)
    hn0 = _rms_fwd(h0, W["norm_mix"][0] + tok0, "mix0_norm")
    p0 = _mm(hn0, ev_w_in_t, "nt", "ev_in")
    uc = _ev_a_conv(p0, W["ev_conv_a"])
    y0 = _ev_b(p0, W["ev_conv_b"], _ev_a_norm(uc, W["ev_ln_a_g"], W["ev_ln_a_b"]))
    h1, hf0 = _mm_res_norm(y0, ev_w_out, h0, W["norm_ffn"][0], "ev_out_norm")
    h2, hn1, ffn0, ff0_up_t, ff0_down = _ffn_fwd(h1, hf0, get_w, W["ff_conv"][0], W["ff_conv_b"][0], 0, W["norm_mix"][1])

    od_w_in_t = get_w("od_in", hn1)
    w_att, w_rwkv = od_w_in_t[:ATT_COLS], od_w_in_t[ATT_COLS:]
    pr = _mm(hn1, w_rwkv, "nt", "od_in_rwkv")
    qp, kp, vp = _rope_pack(_mm(hn1, w_att, "nt", "od_in_att"), c64, s64)
    op = _attn_fwd(qp, kp, vp, W["od_sinks"])
    prep_params = [W[n] for n in _PREP_PARAMS]
    xr, xv, decay, k2, a_s, b_s, wr, br, kr, gate = _rwkv_prep(pr, W["od_mu"], prep_params, bm)
    pairs = _rwkv_pairs(decay, a_s, b_s, k2, xv, wr, br, kr, bm)
    ysc, sprev, sab, vbb = _wkv_fwd(k2, xv, a_s, b_s, pairs)
    rk = W["od_r_k"].reshape(1, D_R)
    yr = _rwkv_post(ysc, xr, k2, xv, gate, W["od_lnx_g"], W["od_lnx_b"], rk, bm)
    y1 = jnp.concatenate([op[ATT_PAD:], yr.astype(bf16)], axis=1)
    od_w_out = get_w("od_out", y1)
    h3, hf1 = _mm_res_norm(y1, od_w_out, h2, W["norm_ffn"][1], "od_out_norm")
    h4, _, ffn1, ff1_up_t, ff1_down = _ffn_fwd(h3, hf1, get_w, W["ff_conv"][1], W["ff_conv_b"][1], 1)

    tgt = jnp.concatenate([jnp.zeros((N_META, D_MODEL), f32), target], axis=0)
    loss, dh4, g_norm_final = _final_loss(h4, W["norm_final"], tgt)

    dh3, gf1 = _ffn_bwd(dh4, h3, ffn1, W["norm_ffn"][1], ff1_up_t, W["ff_conv"][1], W["ff_conv_b"][1], ff1_down, 1, put_g)
    dy1 = _mm(dh3, od_w_out, "nt", "od_dy")
    g_od_w_out = _mm(y1, dh3, "tn", "od_gout", out_dtype=GRAD_WIRE_DTYPE)
    tok = put_g("od_out", g_od_w_out)
    dysc, dxr_p, dk2_p, dxv_p, dgate, g_lnx_g, g_lnx_b, g_rk = _rwkv_post_bwd(
        dy1, ysc, xr, k2, xv, gate, W["od_lnx_g"], W["od_lnx_b"] + tok, rk, bm)
    dr, dw, dk, dv, da, db = _wkv_bwd(sprev, sab, vbb, decay, k2, a_s, b_s, xr, dysc)
    prep_grads = _rwkv_prep_bwd(pr, W["od_mu"], prep_params, bm,
                                [[dw], [dk, dk2_p], [da], [db], [dgate], [dr, dxr_p], [dv, dxv_p]])
    dxs, g_mu = prep_grads[0], prep_grads[1]
    dpr = _shift_bwd(dxs, W["od_mu"])
    dop = jnp.concatenate([jnp.zeros((ATT_PAD, D_ATT), f32), dy1[:, :D_ATT]], axis=0).astype(bf16)
    dqp, dkp, dvp, dsk = _attn_bwd(qp, kp, vp, W["od_sinks"], dop)
    dpatt = _rope_bwd(dqp, dkp, dvp, c64, s64)
    n_in = ATT_COLS + RWKV_COLS
    g_od_w_in_t = _mm(dpatt, hn1, "tn", "od_gin_att", out_dtype=GRAD_WIRE_DTYPE, out_rows=n_in)
    g_od_w_in_t = _mm(dpr, hn1, "tn", "od_gin_rwkv", out_dtype=GRAD_WIRE_DTYPE, out_rows=n_in, out_row0=ATT_COLS, into=g_od_w_in_t)
    tok = put_g("od_in", g_od_w_in_t)
    dh2, g_norm_mix1 = _mm_rms_bwd(dpr, w_rwkv, h2, W["norm_mix"][1] + tok, dh3, "od_dhn_rwkv_norm_bwd",
                                   res=_mm(dpatt, w_att, "nn", "od_dhn_att"))

    dh1, gf0 = _ffn_bwd(dh2, h1, ffn0, W["norm_ffn"][0], ff0_up_t, W["ff_conv"][0], W["ff_conv_b"][0], ff0_down, 0, put_g)
    early = dict(
        norm_ffn=jnp.concatenate([gf0["norm"], gf1["norm"]], axis=0), norm_final=g_norm_final.reshape(D_MODEL),
        od_sinks=dsk[:, :N_Q_HEADS], od_mu=g_mu, od_lnx_g=g_lnx_g, od_lnx_b=g_lnx_b, od_r_k=g_rk.reshape(N_Q_HEADS, HEAD_DIM),
        ff_conv=jnp.stack([gf0["conv"], gf1["conv"]]), ff_conv_b=jnp.concatenate([gf0["conv_b"], gf1["conv_b"]], axis=0),
        **dict(zip(_PREP_PARAMS, prep_grads[2:])))
    dy0 = _mm(dh1, ev_w_out, "nt", "ev_dy")
    g_ev_w_out = _mm(y0, dh1, "tn", "ev_gout", out_dtype=GRAD_WIRE_DTYPE)
    tok = put_g("ev_out", g_ev_w_out) + put_small(early)
    duc, g_ln_g, g_ln_b = _ev_a_norm_bwd(dy0, uc, W["ev_ln_a_g"], W["ev_ln_a_b"] + tok)
    dav, dag, g_conv_a = _ev_a_conv_bwd(duc, p0, W["ev_conv_a"])
    dgb, dgc, dxi, g_conv_b = _ev_b_bwd(dy0, p0, W["ev_conv_b"])
    dp0 = jnp.concatenate([dav, dag, dgb, dgc, dxi], axis=1)
    g_ev_w_in_t = _mm(dp0, hn0, "tn", "ev_gin", out_dtype=GRAD_WIRE_DTYPE)
    tok = put_g("ev_in", g_ev_w_in_t)
    dh0, g_norm_mix0 = _mm_rms_bwd(dp0, ev_w_in_t, h0, W["norm_mix"][0] + tok, dh1, "ev_dhn_norm_bwd")

    late = dict(meta_tokens=dh0[:N_META], norm_mix=jnp.concatenate([g_norm_mix0, g_norm_mix1], axis=0),
                ev_conv_a=g_conv_a, ev_ln_a_g=g_ln_g, ev_ln_a_b=g_ln_b, ev_conv_b=g_conv_b)
    return loss, dh0[N_META:], late


HBM = pl.BlockSpec(memory_space=pl.ANY)


def _mesh_pos():
    return lax.axis_index("x"), lax.axis_index("y"), lax.axis_index("c")


def _dev(px, py, pc):
    return 4 * px + 2 * py + pc


def _all_gather(xs, name):
    n = len(xs)

    def body(*refs):
        x_refs, o_refs = refs[:n], refs[n:2 * n]
        send_sems, recv_sems, local_sems = refs[2 * n:]
        x, y, c = _mesh_pos()
        me, sibling = (x, y, c), (x, y, 1 - c)
        chips = [(1 - x, y), (x, 1 - y), (1 - x, 1 - y)]

        def copy(i, k, block, to, from_input=False):
            dst = o_refs[i].at[_dev(*block)]
            return pltpu.make_async_remote_copy(src_ref=x_refs[i] if from_input else dst, dst_ref=dst,
                                                send_sem=send_sems.at[i, k], recv_sem=recv_sems.at[i, k],
                                                device_id=to, device_id_type=MESH)

        mine = [pltpu.make_async_copy(x_refs[i], o_refs[i].at[_dev(*me)], local_sems.at[i]) for i in range(n)]
        for cp in mine:
            cp.start()
        first = []
        for i in range(n):
            first.append(copy(i, 0, me, sibling, True))
            first += [copy(i, 1 + j, me, (*chip, c), True) for j, chip in enumerate(chips)]
        for cp in first:
            cp.start()
        passed = []
        for j, chip in enumerate(chips):
            for i in range(n):
                copy(i, 1 + j, (*chip, c), me).wait_recv()
                fwd = copy(i, 4 + j, (*chip, c), sibling)
                fwd.start()
                passed.append(fwd)
        for i in range(n):
            copy(i, 0, sibling, me).wait_recv()
            for j, chip in enumerate(chips):
                copy(i, 4 + j, (*chip, 1 - c), me).wait_recv()
        for cp in first + passed:
            cp.wait_send()
        for cp in mine:
            cp.wait()

    return _pc(body, name=name, in_specs=[HBM] * n, out_specs=tuple([HBM] * n),
               out_shape=tuple(S((N_DEV,) + x.shape, x.dtype) for x in xs),
               scratch_shapes=[pltpu.SemaphoreType.DMA((n, 7)), pltpu.SemaphoreType.DMA((n, 7)),
                               pltpu.SemaphoreType.DMA((n,))])(*xs)


HBM_SPEC = pl.BlockSpec(memory_space=pltpu.HBM)
SEM_SPEC = pl.BlockSpec(memory_space=pltpu.SEMAPHORE)
DATAFLOW = pltpu.SideEffectType.DATAFLOW_SIDE_EFFECTING
_PEER_FLIPS = ((1, 0, 0), (0, 1, 0), (1, 1, 0), (1, 0, 1), (0, 1, 1), (1, 1, 1), (0, 0, 1))
N_PEERS = len(_PEER_FLIPS)


def _peers(x, y, c):
    return [((1 - x) if fx else x, (1 - y) if fy else y, (1 - c) if fc else c) for fx, fy, fc in _PEER_FLIPS]


def _xchg_start(srcs, lands, scatter, name):
    n = len(srcs)

    def body(*refs):
        src_refs, land_refs = refs[:n], refs[n:2 * n]
        send_sems, recv_sems, token = refs[2 * n], refs[2 * n + 1], refs[-1]
        x, y, c = _mesh_pos()
        me = _dev(x, y, c)
        for i in range(n):
            for k, peer in enumerate(_peers(x, y, c)):
                pltpu.make_async_remote_copy(src_ref=src_refs[i].at[_dev(*peer)] if scatter else src_refs[i],
                                             dst_ref=land_refs[i].at[me], send_sem=send_sems.at[i * N_PEERS + k],
                                             recv_sem=recv_sems.at[i * N_PEERS + k], device_id=peer, device_id_type=MESH).start()
        token[...] = jnp.zeros_like(token)

    arrs = list(srcs) + list(lands)
    outs = _pc(body, name=name,
               out_shape=(pltpu.SemaphoreType.DMA((n * N_PEERS,)), pltpu.SemaphoreType.DMA((n * N_PEERS,)),
                          *[pltpu.HBM(a.shape, a.dtype) for a in arrs], S((SUBLANES, LANES), f32)),
               in_specs=[HBM_SPEC] * (2 * n),
               out_specs=(SEM_SPEC, SEM_SPEC, *[HBM_SPEC] * (2 * n), pl.BlockSpec(memory_space=pltpu.VMEM)),
               input_output_aliases={i: 2 + i for i in range(2 * n)},
               compiler_params=pltpu.CompilerParams(has_side_effects=DATAFLOW))(
        *[pltpu.with_memory_space_constraint(a, pltpu.HBM) for a in arrs])
    return (outs[0], outs[1], list(outs[2:2 + n]), list(outs[2 + n:2 + 2 * n]), scatter), outs[-1]


def _xchg_wait(handle, after, name):
    send_sems, recv_sems, srcs, lands, scatter = handle
    n = len(srcs)

    def body(*refs):
        src_refs, land_refs = refs[:n], refs[n:2 * n]
        send, recv = refs[2 * n], refs[2 * n + 1]
        x, y, c = _mesh_pos()
        for i in range(n):
            for k in range(N_PEERS):
                cp = pltpu.make_async_remote_copy(src_ref=src_refs[i].at[0] if scatter else src_refs[i],
                                                  dst_ref=land_refs[i].at[0], send_sem=send.at[i * N_PEERS + k],
                                                  recv_sem=recv.at[i * N_PEERS + k],
                                                  device_id=(x, y, c), device_id_type=MESH)
                cp.wait_send()
                cp.wait_recv()

    arrs = srcs + lands
    outs = _pc(body, name=name, out_shape=tuple(pltpu.HBM(a.shape, a.dtype) for a in arrs),
               in_specs=[HBM_SPEC] * (2 * n) + [SEM_SPEC, SEM_SPEC, pl.BlockSpec(memory_space=pl.ANY)],
               out_specs=tuple([HBM_SPEC] * (2 * n)), input_output_aliases={i: i for i in range(2 * n)},
               compiler_params=pltpu.CompilerParams(has_side_effects=DATAFLOW))(*arrs, send_sems, recv_sems, after)
    return list(outs[:n]), list(outs[n:])


def _rs_sum(g, land, me_vec, name):
    _, r, cols = g.shape
    tr = _divisor_block(r, 16, min(r, 352))

    def body(me_ref, g_ref, *rest):
        o_ref = rest[-1]
        acc = g_ref[0].astype(f32)
        for l_ref in rest[:-1]:
            acc = acc + l_ref[0].astype(f32)
        o_ref[...] = acc

    blk = lambda f: pl.BlockSpec((1, tr, cols), f)
    grid_spec = pltpu.PrefetchScalarGridSpec(
        num_scalar_prefetch=1, grid=(r // tr,),
        in_specs=[blk(lambda i, me_ref: (me_ref[0], i, 0))]
        + [blk(lambda i, me_ref, k=k: ((me_ref[0] + k) % N_DEV, i, 0)) for k in range(1, N_DEV)],
        out_specs=pl.BlockSpec((tr, cols), lambda i, me_ref: (i, 0)))
    return _pc(body, name=name, grid_spec=grid_spec, out_shape=S((r, cols), f32),
               compiler_params=_cparams(("arbitrary",)))(me_vec, g, *([land] * (N_DEV - 1)))


def _sum_devices(a, name):
    def body(a_ref, o_ref):
        acc = a_ref[0]
        for d in range(1, N_DEV):
            acc = acc + a_ref[d]
        o_ref[...] = acc

    return _pc(body, name=name, grid=(1,), in_specs=[_full(a.shape)], out_specs=_full(a.shape[1:]),
               out_shape=S(a.shape[1:], a.dtype), compiler_params=_cparams(("arbitrary",)))(a)


def _adamw(w, m, v, g, name):
    shape = w.shape
    w2, m2, v2, g2 = (a.reshape(-1, shape[-1]) for a in (w, m, v, g))
    rows, cols = w2.shape
    tr = rows if rows % SUBLANES else _divisor_block(rows, SUBLANES, max(SUBLANES, min(rows, ADAMW_BLOCK_ELEMS // cols)))
    c1, c2 = 1.0 - ADAM_B1 ** ADAM_STEP, 1.0 - ADAM_B2 ** ADAM_STEP

    def body(w_ref, m_ref, v_ref, g_ref, d_ref, nm_ref, nv_ref):
        gv = g_ref[...]
        nm = ADAM_B1 * m_ref[...] + (1.0 - ADAM_B1) * gv
        nv = ADAM_B2 * v_ref[...] + (1.0 - ADAM_B2) * (gv * gv)
        d_ref[...] = -ADAM_LR * ((nm / c1) / (jnp.sqrt(nv / c2) + ADAM_EPS) + ADAM_WD * w_ref[...])
        nm_ref[...] = nm
        nv_ref[...] = nv

    blk = pl.BlockSpec((tr, cols), lambda i: (i, 0))
    outs = _pc(body, name=name, grid=(rows // tr,), in_specs=[blk] * 4, out_specs=(blk,) * 3,
               out_shape=(S((rows, cols), f32),) * 3, compiler_params=_cparams(("arbitrary",)))(w2, m2, v2, g2)
    return tuple(o.reshape(shape) for o in outs)


_WEIGHTS = ("meta_tokens", "norm_mix", "norm_ffn", "norm_final", "ev_w_in", "ev_conv_a", "ev_ln_a_g", "ev_ln_a_b",
            "ev_conv_b", "ev_w_out", "od_w_in", "od_sinks", "od_mu", "od_w0", "od_w2", "od_a0", "od_a2", "od_g2",
            "od_k_k", "od_k_a", "od_r_k", "od_lnx_g", "od_lnx_b", "od_w_out", "ff_w_up", "ff_conv", "ff_conv_b", "ff_w_down")
_SMALL_SHARDED = (("meta_tokens", 1), ("ev_conv_a", 2), ("ev_conv_b", 2), ("od_mu", 1), ("od_w0", 1), ("od_w2", 2),
                  ("od_a0", 1), ("od_a2", 2), ("od_g2", 2), ("od_k_k", 1), ("od_k_a", 1), ("od_lnx_g", 1),
                  ("od_lnx_b", 1), ("ff_conv", 2))
_SMALL_REPLICATED = ("norm_mix", "norm_ffn", "norm_final", "ev_ln_a_g", "ev_ln_a_b", "od_sinks", "od_r_k", "ff_conv_b")
SLAB_UNIT = SUBLANES * LANES


def _pack(arrs):
    flat = jnp.concatenate([a.reshape(-1).astype(f32) for a in arrs])
    pad = (-flat.shape[0]) % SLAB_UNIT
    return jnp.pad(flat, (0, pad)).reshape(-1, LANES)


def _unpack(flat, shapes):
    out, off = [], 0
    for shp in shapes:
        size = 1
        for s in shp:
            size *= s
        out.append(flat[..., off:off + size].reshape(flat.shape[:-1] + tuple(shp)))
        off += size
    return out


def _full_shape(shape, axis):
    return tuple(N_DEV * s if i == axis else s for i, s in enumerate(shape))


def kernel(x, meta_tokens, norm_mix, norm_ffn, norm_final, ev_w_in, ev_conv_a, ev_ln_a_g, ev_ln_a_b, ev_conv_b, ev_w_out, od_w_in, od_sinks, od_mu, od_w0, od_w2, od_a0, od_a2, od_g2, od_k_k, od_k_a, od_r_k, od_lnx_g, od_lnx_b, od_w_out, ff_w_up, ff_conv, ff_conv_b, ff_w_down, loss_target, m_meta_tokens, m_norm_mix, m_norm_ffn, m_norm_final, m_ev_w_in, m_ev_conv_a, m_ev_ln_a_g, m_ev_ln_a_b, m_ev_conv_b, m_ev_w_out, m_od_w_in, m_od_sinks, m_od_mu, m_od_w0, m_od_w2, m_od_a0, m_od_a2, m_od_g2, m_od_k_k, m_od_k_a, m_od_r_k, m_od_lnx_g, m_od_lnx_b, m_od_w_out, m_ff_w_up, m_ff_conv, m_ff_conv_b, m_ff_w_down, v_meta_tokens, v_norm_mix, v_norm_ffn, v_norm_final, v_ev_w_in, v_ev_conv_a, v_ev_ln_a_g, v_ev_ln_a_b, v_ev_conv_b, v_ev_w_out, v_od_w_in, v_od_sinks, v_od_mu, v_od_w0, v_od_w2, v_od_a0, v_od_a2, v_od_g2, v_od_k_k, v_od_k_a, v_od_r_k, v_od_lnx_g, v_od_lnx_b, v_od_w_out, v_ff_w_up, v_ff_conv, v_ff_conv_b, v_ff_w_down):
    A = dict(locals())
    px, py, pc = _mesh_pos()
    me = _dev(px, py, pc)
    me_vec = jnp.reshape(me, (1,)).astype(jnp.int32)
    rows = lambda a: a.reshape(N_DEV * a.shape[1], a.shape[2])
    blocks = lambda a: a.reshape(N_DEV, a.shape[0] // N_DEV, a.shape[1])

    shards = dict(ev_in=ev_w_in[0].T, ev_out=ev_w_out[0], ff0_up=ff_w_up[0].T, ff0_down=ff_w_down[0], od_in=od_w_in[0].T,
                  od_out=od_w_out[0], ff1_up=ff_w_up[1].T, ff1_down=ff_w_down[1])
    shards = {n: b.astype(bf16) for n, b in shards.items()}
    small_shapes = [A[n].shape for n, _ in _SMALL_SHARDED]
    gathered = _all_gather([shards["ev_in"], shards["ev_out"], _pack([A[n] for n, _ in _SMALL_SHARDED])], "gather_first")
    gathered, shards = lax.optimization_barrier((gathered, shards))
    fetch, tok0 = {}, jnp.zeros((), f32)
    for n in ("ff0_up", "ff0_down", "od_in", "od_out", "ff1_up", "ff1_down"):
        shard, tok0 = lax.optimization_barrier((shards[n], tok0))
        land = lax.dynamic_update_slice(lax.empty((N_DEV,) + shard.shape, bf16), shard[None], (me, 0, 0))
        fetch[n], token = _xchg_start([shard], [land], False, f"gather_{n}_start")
        tok0 = tok0 + token[0, 0]

    def get_w(n, after):
        if n in ("ev_in", "ev_out"):
            return rows(gathered[("ev_in", "ev_out").index(n)])
        return rows(_xchg_wait(fetch[n], after, f"gather_{n}_wait")[1][0])

    W = {}
    for (n, ax), seg in zip(_SMALL_SHARDED, _unpack(gathered[-1].reshape(N_DEV, -1), small_shapes)):
        W[n] = jnp.moveaxis(seg, 0, ax).reshape(_full_shape(A[n].shape, ax))
    for n in ("ev_conv_a", "ev_conv_b", "od_w2", "od_a2", "od_g2"):
        W[n] = W[n][0]
    for n in _SMALL_REPLICATED:
        W[n] = A[n]
    W["od_r_k"] = od_r_k[0]

    small_shape = {n: _full_shape(A[n].shape, ax) for n, ax in _SMALL_SHARDED}
    small_shape.update({n: A[n].shape for n in _SMALL_REPLICATED})
    sent, small_sent, small_names = {}, {}, {}

    def put_g(n, g):
        g8 = blocks(g)
        sent[n], token = _xchg_start([g8], [lax.empty(g8.shape, g8.dtype)], True, f"reduce_{n}_start")
        return token[0, 0]

    def put_small(gs, stage="early"):
        small_names[stage] = sorted(gs)
        slab = _pack([gs[n] for n in small_names[stage]])
        land = lax.dynamic_update_slice(lax.empty((N_DEV,) + slab.shape, f32), slab[None], (me, 0, 0))
        small_sent[stage], small_tok[stage] = _xchg_start([slab], [land], False, f"gather_{stage}_small_grads_start")
        return small_tok[stage][0, 0]

    small_tok = {}
    loss_tile, grad_x, late = _local_step(x[0], loss_target[0], W, get_w, put_g, put_small, tok0)
    put_small(late, "late")
    late_tok = small_tok["late"]

    gsh, prev = {}, late_tok
    for n in ("ff1_down", "ff1_up", "od_out", "od_in", "ff0_down", "ff0_up", "ev_out", "ev_in"):
        srcs, lands = _xchg_wait(sent[n], prev, f"reduce_{n}_wait")
        gsh[n] = prev = _rs_sum(srcs[0], lands[0], me_vec, f"reduce_{n}_sum")
    grads = dict(ev_w_in=gsh["ev_in"].T[None], ev_w_out=gsh["ev_out"][None], od_w_in=gsh["od_in"].T[None],
                 od_w_out=gsh["od_out"][None], ff_w_up=jnp.stack([gsh["ff0_up"].T, gsh["ff1_up"].T]),
                 ff_w_down=jnp.stack([gsh["ff0_down"], gsh["ff1_down"]]))

    delta, new_m, new_v = {}, {}, {}
    for n in ("ff_w_up", "ff_w_down", "od_w_in", "od_w_out", "ev_w_in", "ev_w_out"):
        delta[n], new_m[n], new_v[n] = _adamw(A[n], A["m_" + n], A["v_" + n], grads[n], "adamw_" + n)
    for stage in ("early", "late"):
        gsm = _xchg_wait(small_sent[stage], delta["ev_w_in"], f"gather_{stage}_small_grads_wait")[1][0]
        summed = _sum_devices(gsm, f"sum_{stage}_small_grads").reshape(-1)
        for n, full in zip(small_names[stage], _unpack(summed, [small_shape[n] for n in small_names[stage]])):
            grads[n] = full
    for n, ax in _SMALL_SHARDED:
        size = A[n].shape[ax]
        grads[n] = lax.dynamic_slice_in_dim(grads[n], me * size, size, axis=ax)
    for n in small_shape:
        delta[n], new_m[n], new_v[n] = _adamw(A[n], A["m_" + n], A["v_" + n], grads[n], "adamw_" + n)

    loss = lax.psum(loss_tile[0, 0], ("x", "y", "c"))
    return (loss, grad_x[None], *[grads[n] for n in _WEIGHTS], *[delta[n] for n in _WEIGHTS],
            *[new_m[n] for n in _WEIGHTS], *[new_v[n] for n in _WEIGHTS])
```

```python
import jax
import jax.numpy as jnp
from jax import lax
from jax.experimental import pallas as pl
from jax.experimental.pallas import tpu as pltpu

f32, bf16 = jnp.float32, jnp.bfloat16

D_MODEL = 1024
N_META = 16
RMS_EPS = 1e-6
LN_EPS = 1e-5
D_A = 512
CONV_A_WIDTH = 31
CONV_B_WIDTH = 3
HEAD_DIM = 64
N_Q_HEADS = 8
N_KV_HEADS = 2
GQA_GROUP = 4
D_ATT = 512
D_KV = 128
BLOCK = 128
ROPE_THETA = 10000.0
D_R = 512
LORA_W, LORA_A, LORA_G = 64, 64, 128
RWKV_GN_EPS = 64e-5
ATT_COLS = D_ATT + 2 * D_KV
RWKV_COLS = 3 * D_R + LORA_W + LORA_A + LORA_G
D_FF = 2816
FF_CONV_WIDTH = 3
FF_BLOCK = 256
NEG_INF = -1e30
ATT_PAD = BLOCK - N_META
ATT_SCALE = HEAD_DIM ** -0.5

ADAM_LR, ADAM_B1, ADAM_B2, ADAM_EPS, ADAM_WD, ADAM_STEP = 0.001, 0.9, 0.999, 1e-08, 0.01, 10

N_DEV = 8
LANES = 128
SUBLANES = 8
SCAN_CHUNK = 48
PAIR_ROWS = 4 * HEAD_DIM
V7X_VMEM_LIMIT = 56 * 1024 * 1024
ADAMW_BLOCK_ELEMS = 400 * 1024
GRAD_WIRE_DTYPE = bf16
MESH = pl.DeviceIdType.MESH
S = jax.ShapeDtypeStruct


def _pc(body, **kw):
    return pl.pallas_call(body, **kw)


def _cparams(sem=None):
    return pltpu.CompilerParams(dimension_semantics=sem, vmem_limit_bytes=V7X_VMEM_LIMIT)


def _divisor_block(t, unit, limit):
    best = unit
    for rb in range(unit, limit + 1, unit):
        if t % rb == 0:
            best = rb
    assert t % best == 0, (t, unit)
    return best


def _row_block(t):
    return _divisor_block(t, 16, 704)


def _row_block8(t):
    return _divisor_block(t, 8, 344)


def _col_tile(n, cap):
    return _divisor_block(n, LANES, min(n, cap)) if n % LANES == 0 else n


def _full(shape):
    nd = len(shape)
    return pl.BlockSpec(shape, lambda *_: (0,) * nd)


def _sigmoid(x):
    return jax.nn.sigmoid(x)


_DIMS = {"nn": (((1,), (0,)), ((), ())), "nt": (((1,), (1,)), ((), ())), "tn": (((0,), (0,)), ((), ()))}
MM_MAX_K = 2816
MM_MAX_TM = 704
MM_MAX_TN = 1408


def _mm(a, b, mode, name, out_dtype=f32, res=None, b_row0=0, out_rows=None, out_row0=0, into=None):
    if mode == "nn":
        (m, k), n, k2 = a.shape, b.shape[1], a.shape[1]
        assert b_row0 % k == 0 and b_row0 + k <= b.shape[0], (a.shape, b.shape, b_row0)
    elif mode == "nt":
        (m, k), (n, k2) = a.shape, b.shape
    else:
        (k, m), (k2, n) = a.shape, b.shape
    assert k == k2, (a.shape, b.shape, mode)
    tm = _row_block(m) if m % LANES else _col_tile(m, MM_MAX_TM)
    tn = _col_tile(n, MM_MAX_TN)
    nk = 1 if (mode == "tn" or k <= MM_MAX_K) else k // MM_MAX_K
    tk = k // nk
    assert tk * nk == k
    dims = _DIMS[mode]

    def body(a_ref, b_ref, *rest):
        part = lax.dot_general(a_ref[...].astype(bf16), b_ref[...].astype(bf16), dims, preferred_element_type=f32)
        if nk == 1:
            o_ref = rest[-1]
            if res is not None:
                part = part + rest[0][...]
            o_ref[...] = part.astype(out_dtype)
            return
        o_ref, acc_ref = rest[-2], rest[-1]
        kk = pl.program_id(2)

        @pl.when(kk == 0)
        def _():
            acc_ref[...] = part

        @pl.when(kk > 0)
        def _():
            acc_ref[...] += part

        @pl.when(kk == nk - 1)
        def _():
            acc = acc_ref[...]
            if res is not None:
                acc = acc + rest[0][...]
            o_ref[...] = acc.astype(out_dtype)

    if mode == "tn":
        a_spec = pl.BlockSpec((k, tm), lambda i, j, kk: (0, i))
    else:
        a_spec = pl.BlockSpec((tm, tk), lambda i, j, kk: (i, kk))
    if mode == "nt":
        b_spec = pl.BlockSpec((tn, tk), lambda i, j, kk: (j, kk))
    else:
        b_spec = pl.BlockSpec((tk, tn), lambda i, j, kk: (kk + b_row0 // tk, j))
    assert out_row0 % tm == 0 and res is None or out_row0 == 0
    o_spec = pl.BlockSpec((tm, tn), lambda i, j, kk: (i + out_row0 // tm, j))
    ins, specs, aliases = [a, b], [a_spec, b_spec], {}
    if res is not None:
        ins.append(res)
        specs.append(o_spec)
    if into is not None:
        assert into.shape == (out_rows, n) and into.dtype == out_dtype
        aliases = {len(ins): 0}
        ins.append(into)
        specs.append(pl.BlockSpec(memory_space=pl.ANY))
    scratch = [pltpu.VMEM((tm, tn), f32)] if nk > 1 else []
    return _pc(body, name=name, grid=(m // tm, n // tn, nk), in_specs=specs, out_specs=o_spec,
               out_shape=S((out_rows or m, n), out_dtype), scratch_shapes=scratch, input_output_aliases=aliases,
               compiler_params=_cparams(("arbitrary", "arbitrary", "arbitrary")))(*ins)


def _rms_fwd(x, g, name):
    t, d = x.shape
    rb = _row_block(t)

    def body(x_ref, g_ref, o_ref):
        xv = x_ref[...]
        rstd = lax.rsqrt(jnp.mean(xv * xv, axis=-1, keepdims=True) + RMS_EPS)
        o_ref[...] = (xv * rstd * g_ref[...]).astype(bf16)

    row = pl.BlockSpec((rb, d), lambda i: (i, 0))
    return _pc(body, name=name, grid=(t // rb,), in_specs=[row, _full((1, d))], out_specs=row,
               out_shape=S((t, d), bf16), compiler_params=_cparams(("arbitrary",)))(x, g.reshape(1, d))


def _mm_rms_bwd(a, b, x, g, dres, name, b_row0=0, res=None):
    (m, k), n = a.shape, b.shape[1]
    assert k <= MM_MAX_K and b_row0 % k == 0 and b_row0 + k <= b.shape[0] and x.shape == (m, n)
    tm = _row_block(m)

    def body(a_ref, b_ref, x_ref, g_ref, dres_ref, *rest):
        dx_ref, dg_ref = rest[-2], rest[-1]

        @pl.when(pl.program_id(0) == 0)
        def _():
            dg_ref[...] = jnp.zeros_like(dg_ref)
        dy = jnp.dot(a_ref[...].astype(bf16), b_ref[...].astype(bf16), preferred_element_type=f32)
        if res is not None:
            dy = dy + rest[0][...]
        xv = x_ref[...]
        rstd = lax.rsqrt(jnp.mean(xv * xv, axis=-1, keepdims=True) + RMS_EPS)
        xn = xv * rstd
        dg_ref[...] += jnp.sum(dy * xn, axis=0, keepdims=True)
        dxh = dy * g_ref[...]
        dx_ref[...] = dres_ref[...] + rstd * (dxh - xn * jnp.mean(dxh * xn, axis=-1, keepdims=True))

    row = pl.BlockSpec((tm, n), lambda i: (i, 0))
    ins = [a, b, x, g.reshape(1, n), dres] + ([res] if res is not None else [])
    specs = [pl.BlockSpec((tm, k), lambda i: (i, 0)), pl.BlockSpec((k, n), lambda i: (b_row0 // k, 0)), row, _full((1, n)), row]
    specs += [row] if res is not None else []
    return _pc(body, name=name, grid=(m // tm,), in_specs=specs, out_specs=(row, _full((1, n))),
               out_shape=(S((m, n), f32), S((1, n), f32)), compiler_params=_cparams(("arbitrary",)))(*ins)


def _mm_res_norm(a, b, res, g, name):
    (m, k), n = a.shape, b.shape[1]
    assert k == b.shape[0] and k <= MM_MAX_K and res.shape == (m, n)
    tm = _row_block(m)

    def body(a_ref, b_ref, r_ref, g_ref, h_ref, hn_ref):
        h = jnp.dot(a_ref[...].astype(bf16), b_ref[...].astype(bf16), preferred_element_type=f32) + r_ref[...]
        h_ref[...] = h
        rstd = lax.rsqrt(jnp.mean(h * h, axis=-1, keepdims=True) + RMS_EPS)
        hn_ref[...] = (h * rstd * g_ref[...]).astype(bf16)

    row = pl.BlockSpec((tm, n), lambda i: (i, 0))
    return _pc(body, name=name, grid=(m // tm,),
               in_specs=[pl.BlockSpec((tm, k), lambda i: (i, 0)), _full((k, n)), row, _full((1, n))],
               out_specs=(row, row), out_shape=(S((m, n), f32), S((m, n), bf16)),
               compiler_params=_cparams(("arbitrary",)))(a, b, res, g.reshape(1, n))


def _final_loss(h, g, target_padded):
    t, d = h.shape
    rb = _row_block8(t)

    def body(x_ref, g_ref, t_ref, loss_ref, dx_ref, dg_ref):
        i = pl.program_id(0)

        @pl.when(i == 0)
        def _():
            dg_ref[...] = jnp.zeros_like(dg_ref)
            loss_ref[...] = jnp.zeros_like(loss_ref)
        xv = x_ref[...]
        rstd = lax.rsqrt(jnp.mean(xv * xv, axis=-1, keepdims=True) + RMS_EPS)
        xn = xv * rstd
        gv = g_ref[...]
        row = i * rb + lax.broadcasted_iota(jnp.int32, (rb, 1), 0)
        diff = jnp.where(row >= N_META, xn * gv - t_ref[...], 0.0)
        loss_ref[...] += 0.5 * jnp.sum(jnp.mean(diff * diff, axis=-1, keepdims=True))
        dout = diff * (1.0 / d)
        dg_ref[...] += jnp.sum(dout * xn, axis=0, keepdims=True)
        dxh = dout * gv
        dx_ref[...] = rstd * (dxh - xn * jnp.mean(dxh * xn, axis=-1, keepdims=True))

    row = pl.BlockSpec((rb, d), lambda i: (i, 0))
    return _pc(body, name="final_loss", grid=(t // rb,), in_specs=[row, _full((1, d)), row],
               out_specs=(_full((SUBLANES, LANES)), row, _full((1, d))),
               out_shape=(S((SUBLANES, LANES), f32), S((t, d), f32), S((1, d), f32)),
               compiler_params=_cparams(("arbitrary",)))(h, g.reshape(1, d), target_padded)


CONV_LEAD = 32


def _fill_front_padded(pad_ref, x, t):
    pad_ref[0:CONV_LEAD, :] = jnp.zeros((CONV_LEAD, x.shape[1]), f32)
    pad_ref[CONV_LEAD:CONV_LEAD + t, :] = x


def _fill_back_padded(pad_ref, x, t):
    pad_ref[0:t, :] = x
    pad_ref[t:t + CONV_LEAD, :] = jnp.zeros((CONV_LEAD, x.shape[1]), f32)


def _conv_rows(pad_ref, w_ref, kw, r0, nr):
    acc = None
    for j in range(kw):
        lo = CONV_LEAD + r0 - (kw - 1) + j
        term = w_ref[j:j + 1, :] * pad_ref[lo:lo + nr, :]
        acc = term if acc is None else acc + term
    return acc


def _conv_t_rows(padb_ref, w_ref, kw, r0, nr):
    acc = None
    for j in range(kw):
        lo = r0 + (kw - 1) - j
        term = w_ref[j:j + 1, :] * padb_ref[lo:lo + nr, :]
        acc = term if acc is None else acc + term
    return acc


def _conv_dw_rows(dy_blk, pad_ref, kw, r0, nr):
    out = []
    for j in range(kw):
        lo = CONV_LEAD + r0 - (kw - 1) + j
        out.append(jnp.sum(dy_blk * pad_ref[lo:lo + nr, :], axis=0, keepdims=True))
    return out


def _acc_list(a, b):
    return b if a is None else [x + y for x, y in zip(a, b)]


def _ev_a_conv(p, conv_a):
    t = p.shape[0]
    cr = _row_block8(t)
    nb = D_A // LANES

    def body(av_ref, ag_ref, w_ref, o_ref, pad_ref):
        _fill_front_padded(pad_ref, av_ref[...] * _sigmoid(ag_ref[...]), t)
        for r in range(t // cr):
            o_ref[r * cr:(r + 1) * cr, :] = _conv_rows(pad_ref, w_ref, CONV_A_WIDTH, r * cr, cr)

    col = lambda off: pl.BlockSpec((t, LANES), lambda j: (0, j + off))
    return _pc(body, name="ev_a_conv", grid=(nb,),
               in_specs=[col(0), col(nb), pl.BlockSpec((CONV_A_WIDTH, LANES), lambda j: (0, j))],
               out_specs=col(0), out_shape=S((t, D_A), f32),
               scratch_shapes=[pltpu.VMEM((t + CONV_LEAD, LANES), f32)],
               compiler_params=_cparams(("arbitrary",)))(p, p, conv_a)


def _ln_silu(uc, g, b):
    mu = jnp.mean(uc, axis=-1, keepdims=True)
    xc = uc - mu
    var = jnp.mean(xc * xc, axis=-1, keepdims=True)
    y = xc * lax.rsqrt(var + LN_EPS) * g + b
    return y * _sigmoid(y)


def _ev_a_norm(uc, g, b):
    t, d = uc.shape
    rb = _row_block(t)

    def body(u_ref, g_ref, b_ref, o_ref):
        o_ref[...] = _ln_silu(u_ref[...], g_ref[...], b_ref[...]).astype(bf16)

    row = pl.BlockSpec((rb, d), lambda i: (i, 0))
    return _pc(body, name="ev_a_norm", grid=(t // rb,), in_specs=[row, _full((1, d)), _full((1, d))],
               out_specs=row, out_shape=S((t, 2 * d), bf16), compiler_params=_cparams(("arbitrary",)))(uc, g, b)


def _ev_a_norm_bwd(dy, uc, g, b):
    t, d = uc.shape
    rb = _row_block8(t)

    def body(dy_ref, u_ref, g_ref, b_ref, du_ref, dg_ref, db_ref):
        @pl.when(pl.program_id(0) == 0)
        def _():
            dg_ref[...] = jnp.zeros_like(dg_ref)
            db_ref[...] = jnp.zeros_like(db_ref)
        _, vjp = jax.vjp(_ln_silu, u_ref[...], g_ref[...], b_ref[...])
        du, dg, db = vjp(dy_ref[...])
        du_ref[...] = du
        dg_ref[...] += dg
        db_ref[...] += db

    row = pl.BlockSpec((rb, d), lambda i: (i, 0))
    return _pc(body, name="ev_a_norm_bwd", grid=(t // rb,), in_specs=[row, row, _full((1, d)), _full((1, d))],
               out_specs=(row, _full((1, d)), _full((1, d))),
               out_shape=(S((t, d), f32), S((1, d), f32), S((1, d), f32)),
               compiler_params=_cparams(("arbitrary",)))(dy, uc, g, b)


def _ev_a_conv_bwd(duc, p, conv_a):
    t = p.shape[0]
    cr = _row_block8(t)
    nb = D_A // LANES

    def body(dy_ref, av_ref, ag_ref, w_ref, dav_ref, dag_ref, dw_ref, pad_ref, padb_ref):
        _fill_front_padded(pad_ref, av_ref[...] * _sigmoid(ag_ref[...]), t)
        _fill_back_padded(padb_ref, dy_ref[...], t)
        dw = None
        for r in range(t // cr):
            rows = slice(r * cr, (r + 1) * cr)
            du = _conv_t_rows(padb_ref, w_ref, CONV_A_WIDTH, r * cr, cr)
            avr = av_ref[rows, :]
            sgr = _sigmoid(ag_ref[rows, :])
            dav_ref[rows, :] = du * sgr
            dag_ref[rows, :] = du * avr * sgr * (1.0 - sgr)
            dw = _acc_list(dw, _conv_dw_rows(dy_ref[rows, :], pad_ref, CONV_A_WIDTH, r * cr, cr))
        for j in range(CONV_A_WIDTH):
            dw_ref[j:j + 1, :] = dw[j]

    col = lambda off: pl.BlockSpec((t, LANES), lambda j: (0, j + off))
    wsp = pl.BlockSpec((CONV_A_WIDTH, LANES), lambda j: (0, j))
    return _pc(body, name="ev_a_conv_bwd", grid=(nb,), in_specs=[col(0), col(0), col(nb), wsp],
               out_specs=(col(0), col(0), wsp),
               out_shape=(S((t, D_A), f32), S((t, D_A), f32), S((CONV_A_WIDTH, D_A), f32)),
               scratch_shapes=[pltpu.VMEM((t + CONV_LEAD, LANES), f32), pltpu.VMEM((t + CONV_LEAD, LANES), f32)],
               compiler_params=_cparams(("arbitrary",)))(duc, p, p, conv_a)


def _ev_b(p, conv_b, y):
    t = p.shape[0]
    cr = _row_block8(t)
    nb = D_A // LANES

    def body(gb_ref, gc_ref, xi_ref, w_ref, y_ref, o_ref, pad_ref, stage_ref):
        _fill_front_padded(pad_ref, gc_ref[...] * xi_ref[...], t)
        for r in range(t // cr):
            rows = slice(r * cr, (r + 1) * cr)
            stage_ref[rows, :] = gb_ref[rows, :] * _conv_rows(pad_ref, w_ref, CONV_B_WIDTH, r * cr, cr)
        o_ref[...] = stage_ref[...].astype(bf16)

    col = lambda off: pl.BlockSpec((t, LANES), lambda j: (0, j + off))
    return _pc(body, name="ev_b", grid=(nb,),
               in_specs=[col(2 * nb), col(3 * nb), col(4 * nb), pl.BlockSpec((CONV_B_WIDTH, LANES), lambda j: (0, j)), HBM],
               out_specs=col(nb), out_shape=S(y.shape, bf16), input_output_aliases={4: 0},
               scratch_shapes=[pltpu.VMEM((t + CONV_LEAD, LANES), f32), pltpu.VMEM((t, LANES), f32)],
               compiler_params=_cparams(("arbitrary",)))(p, p, p, conv_b, y)


def _ev_b_bwd(dy, p, conv_b):
    t = p.shape[0]
    cr = _row_block8(t)
    nb = D_A // LANES

    def body(dy_ref, gb_ref, gc_ref, xi_ref, w_ref, dgb_ref, dgc_ref, dxi_ref, dw_ref, pad_ref, padb_ref):
        _fill_front_padded(pad_ref, gc_ref[...] * xi_ref[...], t)
        _fill_back_padded(padb_ref, dy_ref[...] * gb_ref[...], t)
        dw = None
        for r in range(t // cr):
            rows = slice(r * cr, (r + 1) * cr)
            dgb_ref[rows, :] = dy_ref[rows, :] * _conv_rows(pad_ref, w_ref, CONV_B_WIDTH, r * cr, cr)
            dcx = _conv_t_rows(padb_ref, w_ref, CONV_B_WIDTH, r * cr, cr)
            dgc_ref[rows, :] = dcx * xi_ref[rows, :]
            dxi_ref[rows, :] = dcx * gc_ref[rows, :]
            dw = _acc_list(dw, _conv_dw_rows(padb_ref[rows, :], pad_ref, CONV_B_WIDTH, r * cr, cr))
        for j in range(CONV_B_WIDTH):
            dw_ref[j:j + 1, :] = dw[j]

    col = lambda off: pl.BlockSpec((t, LANES), lambda j: (0, j + off))
    wsp = pl.BlockSpec((CONV_B_WIDTH, LANES), lambda j: (0, j))
    return _pc(body, name="ev_b_bwd", grid=(nb,), in_specs=[col(nb), col(2 * nb), col(3 * nb), col(4 * nb), wsp],
               out_specs=(col(0), col(0), col(0), wsp),
               out_shape=(S((t, D_A), f32), S((t, D_A), f32), S((t, D_A), f32), S((CONV_B_WIDTH, D_A), f32)),
               scratch_shapes=[pltpu.VMEM((t + CONV_LEAD, LANES), f32), pltpu.VMEM((t + CONV_LEAD, LANES), f32)],
               compiler_params=_cparams(("arbitrary",)))(dy, p, p, p, conv_b)


def _ffn_mid(u, conv_w, conv_b, name):
    t = u.shape[0]
    cr = _row_block8(t)
    nb = D_FF // FF_BLOCK

    def one(gt_ref, vl_ref, w_ref, b_ref, o_ref, pad_ref, stage_ref):
        _fill_front_padded(pad_ref, gt_ref[...], t)
        for r in range(t // cr):
            rows = slice(r * cr, (r + 1) * cr)
            gc = _conv_rows(pad_ref, w_ref, FF_CONV_WIDTH, r * cr, cr) + b_ref[...]
            stage_ref[rows, :] = gc * _sigmoid(gc) * vl_ref[rows, :]
        o_ref[...] = stage_ref[...].astype(bf16)

    def body(*refs):
        for h in range(FF_BLOCK // LANES):
            one(*[r.at[:, pl.ds(h * LANES, LANES)] for r in refs[:5]], *refs[5:])

    col = lambda off: pl.BlockSpec((t, FF_BLOCK), lambda j: (0, j + off))
    return _pc(body, name=name, grid=(nb,),
               in_specs=[col(0), col(nb), pl.BlockSpec((FF_CONV_WIDTH, FF_BLOCK), lambda j: (0, j)),
                         pl.BlockSpec((1, FF_BLOCK), lambda j: (0, j))],
               out_specs=col(0), out_shape=S((t, D_FF), bf16),
               scratch_shapes=[pltpu.VMEM((t + CONV_LEAD, LANES), f32), pltpu.VMEM((t, LANES), f32)],
               compiler_params=_cparams(("arbitrary",)))(u, u, conv_w, conv_b.reshape(1, D_FF))


def _ffn_mid_bwd(dz, u, conv_w, conv_b, name):
    t = u.shape[0]
    cr = _row_block8(t)
    nb = D_FF // FF_BLOCK
    nh = FF_BLOCK // LANES

    def body(*refs):
        for h in range(nh):
            one(*[r.at[:, pl.ds(h * LANES, LANES)] for r in refs[:9]], *refs[9:])

    def one(dz_ref, gt_ref, vl_ref, w_ref, b_ref, du_ref, dv_ref, dw_ref, db_ref, pad_ref, padb_ref, stage_ref):
        _fill_front_padded(pad_ref, gt_ref[...], t)
        dw, db = None, None
        for r in range(t // cr):
            rows = slice(r * cr, (r + 1) * cr)
            lo = CONV_LEAD + r * cr - (FF_CONV_WIDTH - 1)
            taps = [pad_ref[lo + j:lo + j + cr, :] for j in range(FF_CONV_WIDTH)]
            gc = sum(w_ref[j:j + 1, :] * taps[j] for j in range(FF_CONV_WIDTH)) + b_ref[...]
            sg = _sigmoid(gc)
            dzr = dz_ref[rows, :]
            stage_ref[rows, :] = dzr * gc * sg
            dgc = dzr * vl_ref[rows, :] * sg * (1.0 + gc * (1.0 - sg))
            padb_ref[rows, :] = dgc
            dw = _acc_list(dw, [jnp.sum(dgc * tap, axis=0, keepdims=True) for tap in taps])
            pb = jnp.sum(dgc, axis=0, keepdims=True)
            db = pb if db is None else db + pb
        padb_ref[t:t + CONV_LEAD, :] = jnp.zeros((CONV_LEAD, LANES), f32)
        for r in range(t // cr):
            pad_ref[r * cr:(r + 1) * cr, :] = _conv_t_rows(padb_ref, w_ref, FF_CONV_WIDTH, r * cr, cr)
        du_ref[...] = pad_ref[0:t, :].astype(du_ref.dtype)
        dv_ref[...] = stage_ref[...].astype(dv_ref.dtype)
        for j in range(FF_CONV_WIDTH):
            dw_ref[j:j + 1, :] = dw[j]
        db_ref[...] = db

    col = lambda off: pl.BlockSpec((t, FF_BLOCK), lambda j: (0, j + off))
    wsp = pl.BlockSpec((FF_CONV_WIDTH, FF_BLOCK), lambda j: (0, j))
    bsp = pl.BlockSpec((1, FF_BLOCK), lambda j: (0, j))
    return _pc(body, name=name, grid=(nb,), in_specs=[col(0), col(0), col(nb), wsp, bsp],
               out_specs=(col(0), col(0), wsp, bsp),
               out_shape=(S((t, D_FF), bf16), S((t, D_FF), bf16), S((FF_CONV_WIDTH, D_FF), f32), S((1, D_FF), f32)),
               scratch_shapes=[pltpu.VMEM((t + CONV_LEAD, LANES), f32), pltpu.VMEM((t + CONV_LEAD, LANES), f32),
                               pltpu.VMEM((t, LANES), f32)],
               compiler_params=_cparams(("arbitrary",)))(dz, u, u, conv_w, conv_b.reshape(1, D_FF))


def _swap_halves(x):
    w = x.shape[1]
    lane = lax.broadcasted_iota(jnp.int32, x.shape, 1) % HEAD_DIM
    return jnp.where(lane < HEAD_DIM // 2, pltpu.roll(x, w - HEAD_DIM // 2, axis=1), pltpu.roll(x, HEAD_DIM // 2, axis=1))


def _rope_pack(patt, c64, s64):
    t = patt.shape[0]
    tp = t + ATT_PAD

    def body(p_ref, c_ref, s_ref, q_ref, k_ref, v_ref):
        c, s = c_ref[...], s_ref[...]

        def rope(x, nh):
            cc = jnp.concatenate([c] * nh, axis=1)
            ss = jnp.concatenate([s] * nh, axis=1)
            return x * cc + _swap_halves(x) * ss

        for ref, val in ((q_ref, rope(p_ref[:, 0:D_ATT], N_Q_HEADS)),
                         (k_ref, rope(p_ref[:, D_ATT:D_ATT + D_KV], N_KV_HEADS)),
                         (v_ref, p_ref[:, D_ATT + D_KV:ATT_COLS])):
            ref[0:ATT_PAD, :] = jnp.zeros((ATT_PAD, val.shape[1]), bf16)
            ref[ATT_PAD:tp, :] = val.astype(bf16)

    return _pc(body, name="rope_pack", in_specs=[_full((t, ATT_COLS)), _full((t, HEAD_DIM)), _full((t, HEAD_DIM))],
               out_specs=(_full((tp, D_ATT)), _full((tp, D_KV)), _full((tp, D_KV))), grid=(1,),
               out_shape=(S((tp, D_ATT), bf16), S((tp, D_KV), bf16), S((tp, D_KV), bf16)),
               compiler_params=_cparams(("arbitrary",)))(patt, c64, s64)


def _rope_bwd(dqp, dkp, dvp, c64, s64):
    tp = dqp.shape[0]
    t = tp - ATT_PAD

    def body(dq_ref, dk_ref, dv_ref, c_ref, s_ref, o_ref):
        c, s = c_ref[...], s_ref[...]

        def unrope(dy, nh):
            cc = jnp.concatenate([c] * nh, axis=1)
            ss = jnp.concatenate([s] * nh, axis=1)
            return dy * cc + _swap_halves(dy * ss)

        o_ref[:, 0:D_ATT] = unrope(dq_ref[ATT_PAD:tp, :], N_Q_HEADS).astype(bf16)
        o_ref[:, D_ATT:D_ATT + D_KV] = unrope(dk_ref[ATT_PAD:tp, :], N_KV_HEADS).astype(bf16)
        o_ref[:, D_ATT + D_KV:ATT_COLS] = dv_ref[ATT_PAD:tp, :].astype(bf16)

    return _pc(body, name="rope_bwd", grid=(1,),
               in_specs=[_full((tp, D_ATT)), _full((tp, D_KV)), _full((tp, D_KV)), _full((t, HEAD_DIM)), _full((t, HEAD_DIM))],
               out_specs=_full((t, ATT_COLS)), out_shape=S((t, ATT_COLS), bf16),
               compiler_params=_cparams(("arbitrary",)))(dqp, dkp, dvp, c64, s64)


def _attn_masks(n):
    rows = GQA_GROUP * BLOCK
    ri = lax.broadcasted_iota(jnp.int32, (rows, BLOCK), 0) % BLOCK
    ci = lax.broadcasted_iota(jnp.int32, (rows, BLOCK), 1)
    m_cur = (ci <= ri) & (ci >= jnp.where(n >= 1, 0, ATT_PAD))
    m_prev = ci > ri + jnp.where(n >= 2, 0, BLOCK)
    m_meta = ci >= jnp.where(n >= 1, ATT_PAD, BLOCK)
    return m_cur, m_prev, m_meta


def _attn_probs(qg, kc, kp, km, masks, skv):
    def scores(k, m):
        s = lax.dot_general(qg, k, _DIMS["nt"], preferred_element_type=f32) * ATT_SCALE
        return jnp.where(m, s, NEG_INF)
    s_c, s_p, s_m = scores(kc, masks[0]), scores(kp, masks[1]), scores(km, masks[2])
    mx = jnp.maximum(jnp.maximum(jnp.max(s_c, axis=-1, keepdims=True), jnp.max(s_p, axis=-1, keepdims=True)),
                     jnp.maximum(jnp.max(s_m, axis=-1, keepdims=True), skv))
    e_c, e_p, e_m, e_s = jnp.exp(s_c - mx), jnp.exp(s_p - mx), jnp.exp(s_m - mx), jnp.exp(skv - mx)
    den = (jnp.sum(e_c, axis=-1, keepdims=True) + jnp.sum(e_p, axis=-1, keepdims=True)
           + jnp.sum(e_m, axis=-1, keepdims=True) + e_s)
    inv = 1.0 / den
    return e_c * inv, e_p * inv, e_m * inv, e_s * inv


def _sink_rows(sk_ref, g):
    hrow = lax.broadcasted_iota(jnp.int32, (GQA_GROUP * BLOCK, 1), 0) // BLOCK
    skv = jnp.zeros((GQA_GROUP * BLOCK, 1), f32)
    for hh in range(GQA_GROUP):
        skv = jnp.where(hrow == hh, sk_ref[0, GQA_GROUP * g + hh], skv)
    return skv, hrow


def _stack_heads(ref, g):
    return jnp.concatenate([ref[:, (GQA_GROUP * g + hh) * HEAD_DIM:(GQA_GROUP * g + hh + 1) * HEAD_DIM]
                            for hh in range(GQA_GROUP)], axis=0)


def _attn_specs():
    blk = lambda w: pl.BlockSpec((BLOCK, w), lambda n: (n, 0))
    prev = pl.BlockSpec((BLOCK, D_KV), lambda n: (jnp.maximum(n - 1, 0), 0))
    meta = pl.BlockSpec((BLOCK, D_KV), lambda n: (0, 0))
    return blk, prev, meta


def _attn_fwd(qp, kp, vp, sinks):
    tp = qp.shape[0]
    blk, prev, meta = _attn_specs()

    def body(sk_ref, q_ref, kc_ref, kp_ref, km_ref, vc_ref, vp_ref, vm_ref, o_ref):
        masks = _attn_masks(pl.program_id(0))
        for g in range(N_KV_HEADS):
            sl = slice(g * HEAD_DIM, (g + 1) * HEAD_DIM)
            skv, _ = _sink_rows(sk_ref, g)
            p_c, p_p, p_m, _ = _attn_probs(_stack_heads(q_ref, g), kc_ref[:, sl], kp_ref[:, sl], km_ref[:, sl], masks, skv)
            o = (jnp.dot(p_c.astype(bf16), vc_ref[:, sl], preferred_element_type=f32)
                 + jnp.dot(p_p.astype(bf16), vp_ref[:, sl], preferred_element_type=f32)
                 + jnp.dot(p_m.astype(bf16), vm_ref[:, sl], preferred_element_type=f32))
            for hh in range(GQA_GROUP):
                h = GQA_GROUP * g + hh
                o_ref[:, h * HEAD_DIM:(h + 1) * HEAD_DIM] = o[hh * BLOCK:(hh + 1) * BLOCK].astype(bf16)

    return _pc(body, name="attn_fwd", grid=(tp // BLOCK,),
               in_specs=[pl.BlockSpec(memory_space=pltpu.SMEM), blk(D_ATT), blk(D_KV), prev, meta, blk(D_KV), prev, meta],
               out_specs=blk(D_ATT), out_shape=S((tp, D_ATT), bf16),
               compiler_params=_cparams(("arbitrary",)))(sinks, qp, kp, kp, kp, vp, vp, vp)


def _attn_bwd(qp, kp, vp, sinks, dop):
    tp = qp.shape[0]
    blk, prev, meta = _attn_specs()

    def body(sk_ref, q_ref, kc_ref, kp_ref, km_ref, vc_ref, vp_ref, vm_ref, do_ref, dq_ref, dk_ref, dv_ref, dsk_ref):
        n = pl.program_id(0)

        @pl.when(n == 0)
        def _():
            dk_ref[...] = jnp.zeros_like(dk_ref)
            dv_ref[...] = jnp.zeros_like(dv_ref)
            dsk_ref[...] = jnp.zeros_like(dsk_ref)
        masks = _attn_masks(n)
        cur = pl.ds(pl.multiple_of(n * BLOCK, BLOCK), BLOCK)
        prv = pl.ds(pl.multiple_of(jnp.maximum(n - 1, 0) * BLOCK, BLOCK), BLOCK)
        lane = lax.broadcasted_iota(jnp.int32, (1, LANES), 1)
        dsk = jnp.zeros((1, LANES), f32)
        for g in range(N_KV_HEADS):
            sl = slice(g * HEAD_DIM, (g + 1) * HEAD_DIM)
            skv, hrow = _sink_rows(sk_ref, g)
            qg = _stack_heads(q_ref, g)
            dog = _stack_heads(do_ref, g)
            ks = (kc_ref[:, sl], kp_ref[:, sl], km_ref[:, sl])
            vs = (vc_ref[:, sl], vp_ref[:, sl], vm_ref[:, sl])
            probs = _attn_probs(qg, ks[0], ks[1], ks[2], masks, skv)
            dps = [lax.dot_general(dog, v, _DIMS["nt"], preferred_element_type=f32) for v in vs]
            delta = sum(jnp.sum(p * dp, axis=-1, keepdims=True) for p, dp in zip(probs[:3], dps))
            dss = [(p * (dp - delta) * ATT_SCALE).astype(bf16) for p, dp in zip(probs[:3], dps)]
            dq = sum(jnp.dot(ds, k, preferred_element_type=f32) for ds, k in zip(dss, ks))
            for hh in range(GQA_GROUP):
                h = GQA_GROUP * g + hh
                dq_ref[:, h * HEAD_DIM:(h + 1) * HEAD_DIM] = dq[hh * BLOCK:(hh + 1) * BLOCK]
                dsk = dsk + jnp.where(lane == h, -jnp.sum(jnp.where(hrow == hh, probs[3] * delta, 0.0)), 0.0)
            for rows, p, ds in zip((cur, prv, slice(0, BLOCK)), probs[:3], dss):
                dv_ref[rows, sl] += lax.dot_general(p.astype(bf16), dog, _DIMS["tn"], preferred_element_type=f32)
                dk_ref[rows, sl] += lax.dot_general(ds, qg, _DIMS["tn"], preferred_element_type=f32)
        dsk_ref[...] += dsk

    return _pc(body, name="attn_bwd", grid=(tp // BLOCK,),
               in_specs=[pl.BlockSpec(memory_space=pltpu.SMEM), blk(D_ATT), blk(D_KV), prev, meta, blk(D_KV), prev, meta,
                         blk(D_ATT)],
               out_specs=(blk(D_ATT), _full((tp, D_KV)), _full((tp, D_KV)), _full((1, LANES))),
               out_shape=(S((tp, D_ATT), f32), S((tp, D_KV), f32), S((tp, D_KV), f32), S((1, LANES), f32)),
               compiler_params=_cparams(("arbitrary",)))(sinks, qp, kp, kp, kp, vp, vp, vp, dop)


def _seg(x, bm):
    hi = x.astype(bf16)
    lo = (x - hi.astype(f32)).astype(bf16)
    return jnp.dot(jnp.concatenate([hi, lo], axis=1), bm, preferred_element_type=f32)


@jax.custom_vjp
def _seg_linear(x, bm):
    return _seg(x, bm)


_seg_linear.defvjp(lambda x, bm: (_seg(x, bm), bm), lambda bm, ct: (_seg(ct, bm), jnp.zeros_like(bm)))


def _softplus(y):
    return jnp.maximum(y, 0.0) + jnp.log(1.0 + jnp.exp(-jnp.abs(y)))


def _prep_fn(xr, xk, xwd, xad, xgd, w0, w2, a0, a2, g2, k_k, k_a, bm, seg=_seg):
    xw = w0 + jnp.dot(jnp.tanh(xwd), w2, preferred_element_type=f32)
    decay = jnp.exp(-jnp.exp(-_softplus(-xw) - 0.5))
    alpha = _sigmoid(a0 + jnp.dot(xad, a2, preferred_element_type=f32))
    g = jnp.dot(_sigmoid(xgd), g2, preferred_element_type=f32)
    kk = xk * k_k
    kkn = kk / jnp.maximum(jnp.sqrt(seg(kk * kk, bm)), 1e-12)
    k2 = xk * (1.0 + (alpha - 1.0) * k_a)
    return decay, k2, -kkn, kkn * alpha, g


def _split_cols(x):
    o1, o2, o3 = 3 * D_R, 3 * D_R + LORA_W, 3 * D_R + LORA_W + LORA_A
    return x[:, 0:D_R], x[:, D_R:2 * D_R], x[:, 2 * D_R:o1], x[:, o1:o2], x[:, o2:o3], x[:, o3:RWKV_COLS]


def _shifted(sh_ref, x, halo, first, rb):
    sh_ref[0:SUBLANES, :] = jnp.where(first, 0.0, halo)
    sh_ref[SUBLANES:SUBLANES + rb, :] = x
    return sh_ref[SUBLANES - 1:SUBLANES - 1 + rb, :]


_PREP_PARAMS = ("od_w0", "od_w2", "od_a0", "od_a2", "od_g2", "od_k_k", "od_k_a")


def _rwkv_prep(pr, mu, params, bm):
    t = pr.shape[0]
    rb = _row_block8(t)
    hb = rb // SUBLANES

    def body(pr_ref, halo_ref, mu_ref, w0, w2, a0, a2, g2, kk_ref, ka_ref, bm_ref, *outs_sh):
        outs, sh_ref = outs_sh[:-1], outs_sh[-1]
        x = pr_ref[...]
        prev = _shifted(sh_ref, x, halo_ref[...], pl.program_id(0) == 0, rb)
        xr, xk, xv, xwd, xad, xgd = _split_cols(x + (prev - x) * mu_ref[...])
        bmv = bm_ref[...]
        decay, k2, a_s, b_s, g = _prep_fn(xr, xk, xwd, xad, xgd, w0[...], w2[...], a0[...], a2[...], g2[...],
                                          kk_ref[...], ka_ref[...], bmv)
        vals = (xr, xv, decay, k2, a_s, b_s, decay * xr, _seg(b_s * xr, bmv), _seg(k2 * xr, bmv), g)
        for ref, val in zip(outs, vals):
            ref[...] = val

    row = pl.BlockSpec((rb, RWKV_COLS), lambda i: (i, 0))
    halo = pl.BlockSpec((SUBLANES, RWKV_COLS), lambda i: (jnp.maximum(i * hb - 1, 0), 0))
    orow = pl.BlockSpec((rb, D_R), lambda i: (i, 0))
    return _pc(body, name="rwkv_prep", grid=(t // rb,),
               in_specs=[row, halo, _full((1, RWKV_COLS))] + [_full(p.shape) for p in params] + [_full(bm.shape)],
               out_specs=(orow,) * 10, out_shape=(S((t, D_R), f32),) * 10,
               scratch_shapes=[pltpu.VMEM((rb + SUBLANES, RWKV_COLS), f32)],
               compiler_params=_cparams(("arbitrary",)))(pr, pr, mu, *params, bm)


def _rwkv_prep_bwd(pr, mu, params, bm, cts):
    t = pr.shape[0]
    rb = _row_block8(t)
    hb = rb // SUBLANES
    counts = [len(c) for c in cts]
    flat = [a for c in cts for a in c]

    def body(pr_ref, halo_ref, mu_ref, w0, w2, a0, a2, g2, kk_ref, ka_ref, bm_ref, *rest):
        ct_refs, rest = rest[:len(flat)], rest[len(flat):]
        dx_ref, dmu_ref = rest[0], rest[1]
        dpar_refs, sh_ref = rest[2:9], rest[9]

        @pl.when(pl.program_id(0) == 0)
        def _():
            dmu_ref[...] = jnp.zeros_like(dmu_ref)
            for r in dpar_refs:
                r[...] = jnp.zeros_like(r)
        sums, pos = [], 0
        for c in counts:
            sums.append(sum(r[...] for r in ct_refs[pos:pos + c]))
            pos += c
        x = pr_ref[...]
        prev = _shifted(sh_ref, x, halo_ref[...], pl.program_id(0) == 0, rb)
        xr, xk, xv, xwd, xad, xgd = _split_cols(x + (prev - x) * mu_ref[...])
        bmv = bm_ref[...]
        _, vjp = jax.vjp(lambda *a: _prep_fn(*a, bmv, _seg_linear), xr, xk, xwd, xad, xgd, w0[...], w2[...], a0[...], a2[...],
                         g2[...], kk_ref[...], ka_ref[...])
        grads = vjp(tuple(sums[:5]))
        dxr, dxk, dxwd, dxad, dxgd = grads[:5]
        o1, o2, o3 = 3 * D_R, 3 * D_R + LORA_W, 3 * D_R + LORA_W + LORA_A
        dx_ref[:, 0:D_R] = dxr + sums[5]
        dx_ref[:, D_R:2 * D_R] = dxk
        dx_ref[:, 2 * D_R:o1] = sums[6]
        dx_ref[:, o1:o2] = dxwd
        dx_ref[:, o2:o3] = dxad
        dx_ref[:, o3:RWKV_COLS] = dxgd
        dmu_ref[...] += jnp.sum(dx_ref[...] * (prev - x), axis=0, keepdims=True)
        for r, gval in zip(dpar_refs, grads[5:]):
            r[...] += gval

    row = pl.BlockSpec((rb, RWKV_COLS), lambda i: (i, 0))
    halo = pl.BlockSpec((SUBLANES, RWKV_COLS), lambda i: (jnp.maximum(i * hb - 1, 0), 0))
    crow = pl.BlockSpec((rb, D_R), lambda i: (i, 0))
    return _pc(body, name="rwkv_prep_bwd", grid=(t // rb,),
               in_specs=[row, halo, _full((1, RWKV_COLS))] + [_full(p.shape) for p in params] + [_full(bm.shape)]
               + [crow] * len(flat),
               out_specs=(row, _full((1, RWKV_COLS))) + tuple(_full(p.shape) for p in params),
               out_shape=(S((t, RWKV_COLS), f32), S((1, RWKV_COLS), f32)) + tuple(S(p.shape, f32) for p in params),
               scratch_shapes=[pltpu.VMEM((rb + SUBLANES, RWKV_COLS), f32)],
               compiler_params=_cparams(("arbitrary",)))(pr, pr, mu, *params, bm, *flat)


def _shift_bwd(dxs, mu):
    t = dxs.shape[0]
    rb = _row_block(t)
    hb = rb // SUBLANES
    nblk = t // rb

    def body(dx_ref, halo_ref, mu_ref, o_ref, sh_ref):
        dx = dx_ref[...]
        sh_ref[0:rb, :] = dx
        sh_ref[rb:rb + SUBLANES, :] = jnp.where(pl.program_id(0) == nblk - 1, 0.0, halo_ref[...])
        m = mu_ref[...]
        o_ref[...] = (dx * (1.0 - m) + sh_ref[1:1 + rb, :] * m).astype(bf16)

    row = pl.BlockSpec((rb, RWKV_COLS), lambda i: (i, 0))
    halo = pl.BlockSpec((SUBLANES, RWKV_COLS), lambda i: (jnp.minimum((i + 1) * hb, t // SUBLANES - 1), 0))
    return _pc(body, name="rwkv_shift_bwd", grid=(nblk,), in_specs=[row, halo, _full((1, RWKV_COLS))],
               out_specs=row, out_shape=S((t, RWKV_COLS), bf16),
               scratch_shapes=[pltpu.VMEM((rb + SUBLANES, RWKV_COLS), f32)],
               compiler_params=_cparams(("arbitrary",)))(dxs, dxs, mu)


def _post_fn(y, xr, k2, xv, g, lg, lb, rk, bm, seg=_seg):
    inv_n = 1.0 / HEAD_DIM
    yc = y - seg(y, bm) * inv_n
    var = seg(yc * yc, bm) * inv_n
    yn = yc * lax.rsqrt(var + RWKV_GN_EPS) * lg + lb
    return (yn + seg(xr * k2 * rk, bm) * xv) * g


def _rwkv_post(y, xr, k2, xv, g, lg, lb, rk, bm):
    t = y.shape[0]
    rb = _row_block8(t)

    def body(y_ref, xr_ref, k2_ref, xv_ref, g_ref, lg_ref, lb_ref, rk_ref, bm_ref, o_ref):
        o_ref[...] = _post_fn(y_ref[...], xr_ref[...], k2_ref[...], xv_ref[...], g_ref[...], lg_ref[...], lb_ref[...],
                              rk_ref[...], bm_ref[...])

    row = pl.BlockSpec((rb, D_R), lambda i: (i, 0))
    vec = _full((1, D_R))
    return _pc(body, name="rwkv_post", grid=(t // rb,), in_specs=[row] * 5 + [vec] * 3 + [_full(bm.shape)],
               out_specs=row, out_shape=S((t, D_R), f32),
               compiler_params=_cparams(("arbitrary",)))(y, xr, k2, xv, g, lg, lb, rk, bm)


def _rwkv_post_bwd(dy1, y, xr, k2, xv, g, lg, lb, rk, bm):
    t = y.shape[0]
    rb = _row_block8(t)

    def body(dy_ref, y_ref, xr_ref, k2_ref, xv_ref, g_ref, lg_ref, lb_ref, rk_ref, bm_ref, *outs):
        @pl.when(pl.program_id(0) == 0)
        def _():
            for r in outs[5:]:
                r[...] = jnp.zeros_like(r)
        bmv = bm_ref[...]
        _, vjp = jax.vjp(lambda *a: _post_fn(*a, bmv, _seg_linear), y_ref[...], xr_ref[...], k2_ref[...], xv_ref[...], g_ref[...],
                         lg_ref[...], lb_ref[...], rk_ref[...])
        grads = vjp(dy_ref[...])
        for r, gval in zip(outs[:5], grads[:5]):
            r[...] = gval
        for r, gval in zip(outs[5:], grads[5:]):
            r[...] += gval

    row = pl.BlockSpec((rb, D_R), lambda i: (i, 0))
    vec = _full((1, D_R))
    return _pc(body, name="rwkv_post_bwd", grid=(t // rb,),
               in_specs=[pl.BlockSpec((rb, D_R), lambda i: (i, 1))] + [row] * 5 + [vec] * 3 + [_full(bm.shape)],
               out_specs=(row,) * 5 + (vec,) * 3, out_shape=(S((t, D_R), f32),) * 5 + (S((1, D_R), f32),) * 3,
               compiler_params=_cparams(("arbitrary",)))(dy1, y, xr, k2, xv, g, lg, lb, rk, bm)


def _row4(rows, j):
    return jnp.concatenate([jnp.broadcast_to(rows[j:j + 1, p * LANES:(p + 1) * LANES], (HEAD_DIM, LANES))
                            for p in range(4)], axis=0)


def _scan_consts():
    lane_group = jnp.arange(LANES) // HEAD_DIM
    b128 = (lane_group[:, None] == lane_group[None, :]).astype(bf16)
    bb = jnp.concatenate([b128, b128], axis=0)
    qsel = (jnp.arange(PAIR_ROWS)[:, None] % HEAD_DIM == jnp.arange(LANES)[None, :] % HEAD_DIM).astype(f32)
    return bb, qsel


def _store_cols(acc_ref, o_ref, tc):
    for p in range(4):
        blk = acc_ref[p * HEAD_DIM:(p + 1) * HEAD_DIM, :].T
        o_ref[:, (2 * p) * HEAD_DIM:(2 * p + 1) * HEAD_DIM] = blk[0:tc]
        o_ref[:, (2 * p + 1) * HEAD_DIM:(2 * p + 2) * HEAD_DIM] = blk[HEAD_DIM:HEAD_DIM + tc]


PAIR_GROUP = 2 * SUBLANES


def _rwkv_pairs(w, a, b, k, v, wr, br, kr, bm):
    t = w.shape[0]
    rb = _row_block8(t)

    def body(w_ref, a_ref, b_ref, k_ref, v_ref, wr_ref, br_ref, kr_ref, bm_ref, *outs_sh):
        outs, sh_ref = outs_sh[:-1], outs_sh[-1]

        def second(ref):
            sh_ref[0:rb, :] = ref[...]
            sh_ref[rb:rb + SUBLANES, :] = jnp.zeros((SUBLANES, D_R), f32)
            return sh_ref[1:1 + rb, :]

        w1, a1, b1, k1, v1 = w_ref[...], a_ref[...], b_ref[...], k_ref[...], v_ref[...]
        w2, a2, wr2, br2, kr2, v2 = (second(r) for r in (w_ref, a_ref, wr_ref, br_ref, kr_ref, v_ref))
        bmv = bm_ref[...]
        beta, kappa = _seg(b1 * a2, bmv), _seg(k1 * a2, bmv)
        bwr2, kwr2 = _seg(b1 * wr2, bmv), _seg(k1 * wr2, bmv)
        w1a2 = w1 * a2
        vals = (w1a2 + a1 * beta, v1 * kappa,
                wr_ref[...] + a1 * br_ref[...], v1 * kr_ref[...],
                w1 * wr2 + a1 * (bwr2 + beta * br2) + w1a2 * br2,
                v1 * (kwr2 + kappa * br2) + v2 * kr2,
                w1 * w2, b1 * w2, k1 * w2)
        for ref, val in zip(outs, vals):
            ref[...] = val

    row = pl.BlockSpec((rb, D_R), lambda i: (i, 0))
    return _pc(body, name="rwkv_pairs", grid=(t // rb,), in_specs=[row] * 8 + [_full(bm.shape)],
               out_specs=(row,) * 9, out_shape=(S((t, D_R), f32),) * 9,
               scratch_shapes=[pltpu.VMEM((rb + SUBLANES, D_R), f32)],
               compiler_params=_cparams(("arbitrary",)))(w, a, b, k, v, wr, br, kr, bm)


def _wkv_fwd(k, v, a, b, pairs):
    t = k.shape[0]
    tc = SCAN_CHUNK
    bb, qsel = _scan_consts()

    def body(*refs):
        k16, v16, a16, b16 = refs[0:4]
        ca2p, da2p, c1p, d1p, c2p, d2p, w12p, b1wp, k1wp = refs[4:13]
        bb_ref, q_ref, y_ref, st_ref, sa_ref, vb_ref, s_scr, yacc = refs[13:]

        @pl.when(pl.program_id(0) == 0)
        def _():
            s_scr[...] = jnp.zeros_like(s_scr)
        bbv, qp = bb_ref[...], q_ref[0:HEAD_DIM, :]
        lane = lax.broadcasted_iota(jnp.int32, (HEAD_DIM, LANES), 1) % HEAD_DIM

        def halves(x):
            hi = x.astype(bf16)
            return jnp.concatenate([hi, (x - hi.astype(f32)).astype(bf16)], axis=1)

        def group(gi, s):
            base = pl.multiple_of(gi * PAIR_GROUP, PAIR_GROUP)

            def rows8(ref, j):
                return ref[pl.ds(base + (j // SUBLANES) * SUBLANES, SUBLANES), :]

            def bcast(rows, j, p):
                return jnp.broadcast_to(rows[j % SUBLANES:j % SUBLANES + 1, p * LANES:(p + 1) * LANES], (HEAD_DIM, LANES))

            step = lambda ref, j, p: bcast(rows8(ref, j), j, p)
            for q in range(SUBLANES):
                j1, j2 = 2 * q, 2 * q + 1
                t1 = base + j1
                nxt = []
                for p in range(4):
                    sl = slice(p * HEAD_DIM, (p + 1) * HEAD_DIM)
                    sp = s[sl]
                    lhs = [halves(jnp.concatenate([sp * step(a16, j1, p),
                                                   sp * step(ca2p, j1, p) + qp * step(da2p, j1, p),
                                                   sp * step(c1p, j1, p) + qp * step(d1p, j1, p),
                                                   sp * step(c2p, j1, p) + qp * step(d2p, j1, p)], axis=0))]
                    for j in (j1, j2):
                        v8 = rows8(v16, j)
                        vh8 = v8.astype(bf16).astype(f32)
                        lhs.append(jnp.concatenate([(qp * bcast(vh8, j, p)).astype(bf16),
                                                    (qp * bcast(v8 - vh8, j, p)).astype(bf16)], axis=1))
                    r = jnp.dot(jnp.concatenate(lhs, axis=0), bbv, preferred_element_type=f32)
                    sa1, sa2, y1, y2, vb1, vb2 = (r[n * HEAD_DIM:(n + 1) * HEAD_DIM] for n in range(6))
                    yacc[sl, :] = jnp.where(lane == t1, y1, jnp.where(lane == t1 + 1, y2, yacc[sl, :]))
                    st_ref[base // 2 + q, sl, :] = sp
                    sa_ref[t1, sl, :] = sa1
                    sa_ref[t1 + 1, sl, :] = sa2
                    vb_ref[t1, sl, :] = vb1
                    vb_ref[t1 + 1, sl, :] = vb2
                    nxt.append(((sp * step(w12p, j1, p) + sa1 * step(b1wp, j1, p)) + vb1 * step(k1wp, j1, p))
                               + (sa2 * step(b16, j2, p) + vb2 * step(k16, j2, p)))
                s = jnp.concatenate(nxt, axis=0)
            return s

        s_scr[...] = lax.fori_loop(0, tc // PAIR_GROUP, group, s_scr[...])
        _store_cols(yacc, y_ref, tc)

    row = pl.BlockSpec((tc, D_R), lambda c: (c, 0))
    tiles = pl.BlockSpec((tc, PAIR_ROWS, LANES), lambda c: (c, 0, 0))
    return _pc(body, name="wkv_fwd", grid=(t // tc,),
               in_specs=[row] * 13 + [_full(bb.shape), _full(qsel.shape)],
               out_specs=(row, pl.BlockSpec((tc // 2, PAIR_ROWS, LANES), lambda c: (c, 0, 0)), tiles, tiles),
               out_shape=(S((t, D_R), f32), S((t // 2, PAIR_ROWS, LANES), f32)) + (S((t, PAIR_ROWS, LANES), f32),) * 2,
               scratch_shapes=[pltpu.VMEM((PAIR_ROWS, LANES), f32), pltpu.VMEM((PAIR_ROWS, LANES), f32)],
               compiler_params=_cparams(("arbitrary",)))(k, v, a, b, *pairs, bb, qsel)


def _wkv_bwd(sprev, sab, vbb, w, k, a, b, r, dy):
    t = w.shape[0]
    tc = SCAN_CHUNK
    nc = t // tc
    bb, qsel = _scan_consts()

    def body(st_ref, sa_ref, vb_ref, w_ref, k_ref, a_ref, b_ref, r_ref, dy_ref, bb_ref, q_ref,
             dr_ref, dw_ref, dk_ref, dv_ref, da_ref, db_ref, g_scr, dvacc, rows_scr):
        @pl.when(pl.program_id(0) == 0)
        def _():
            g_scr[...] = jnp.zeros_like(g_scr)
        bbv, qv = bb_ref[...], q_ref[...]
        lane64 = lax.broadcasted_iota(jnp.int32, (PAIR_ROWS, LANES), 1) % HEAD_DIM
        outs = (dr_ref, dw_ref, db_ref, dk_ref, da_ref)

        def colsums(slot, j, x):
            for p in range(4):
                rows_scr[slot, j:j + 1, p * LANES:(p + 1) * LANES] = jnp.sum(x[p * HEAD_DIM:(p + 1) * HEAD_DIM], axis=0,
                                                                           keepdims=True)

        def group(i, g):
            base = pl.multiple_of((tc // SUBLANES - 1 - i) * SUBLANES, SUBLANES)
            w8, k8, a8, b8, r8, dy8 = (ref[pl.ds(base, SUBLANES), :] for ref in (w_ref, k_ref, a_ref, b_ref, r_ref, dy_ref))

            def after_step(j, sp):
                return sp * _row4(w8, j) + sa_ref[base + j] * _row4(b8, j) + vb_ref[base + j] * _row4(k8, j)

            def back_step(j, sp, s_t, g):
                tt = base + j
                u, vb = sa_ref[tt], vb_ref[tt]
                a4, b4, w4, k4 = _row4(a8, j), _row4(b8, j), _row4(w8, j), _row4(k8, j)
                dyb = _seg(qv * _row4(dy8, j), bbv)
                g = g + dyb * _row4(r8, j)
                rr2 = _seg(jnp.concatenate([g * b4, g * k4], axis=0), bbv)
                du, dvb = rr2[0:PAIR_ROWS], rr2[PAIR_ROWS:2 * PAIR_ROWS]
                for slot, val in enumerate((s_t * dyb, g * sp, g * u, g * vb, sp * du)):
                    colsums(slot, j, val)
                dvacc[...] = jnp.where(lane64 == tt, dvb, dvacc[...])
                return g * w4 + du * a4

            for q in reversed(range(SUBLANES // 2)):
                s0 = st_ref[base // 2 + q]
                s1 = after_step(2 * q, s0)
                g = back_step(2 * q + 1, s1, after_step(2 * q + 1, s1), g)
                g = back_step(2 * q, s0, s1, g)
            for slot, ref in enumerate(outs):
                ref[pl.ds(base, SUBLANES), :] = rows_scr[slot]
            return g

        g_scr[...] = lax.fori_loop(0, tc // SUBLANES, group, g_scr[...])
        _store_cols(dvacc, dv_ref, tc)

    row = pl.BlockSpec((tc, D_R), lambda c: (nc - 1 - c, 0))
    tiles = pl.BlockSpec((tc, PAIR_ROWS, LANES), lambda c: (nc - 1 - c, 0, 0))
    states = pl.BlockSpec((tc // 2, PAIR_ROWS, LANES), lambda c: (nc - 1 - c, 0, 0))
    return _pc(body, name="wkv_bwd", grid=(nc,),
               in_specs=[states, tiles, tiles] + [row] * 6 + [_full(bb.shape), _full(qsel.shape)],
               out_specs=(row,) * 6, out_shape=(S((t, D_R), f32),) * 6,
               scratch_shapes=[pltpu.VMEM((PAIR_ROWS, LANES), f32), pltpu.VMEM((PAIR_ROWS, LANES), f32),
                               pltpu.VMEM((5, SUBLANES, D_R), f32)],
               compiler_params=_cparams(("arbitrary",)))(sprev, sab, vbb, w, k, a, b, r, dy, bb, qsel)


def _rope_tables(t):
    half = HEAD_DIM // 2
    inv = ROPE_THETA ** (-jnp.arange(half, dtype=f32) / half)
    ang = jnp.arange(t, dtype=f32)[:, None] * inv[None, :]
    cos, sin = jnp.cos(ang), jnp.sin(ang)
    return jnp.concatenate([cos, cos], axis=1), jnp.concatenate([-sin, sin], axis=1)


def _head_matrix():
    grp = jnp.arange(D_R) // HEAD_DIM
    b = (grp[:, None] == grp[None, :]).astype(bf16)
    return jnp.concatenate([b, b], axis=0)


def _ffn_fwd(h, hf, get_w, conv_w, conv_b, i, next_gain=None):
    w_up_t = get_w(f"ff{i}_up", hf)
    u = _mm(hf, w_up_t, "nt", f"ffn{i}_up")
    z = _ffn_mid(u, conv_w, conv_b, f"ffn{i}_mid")
    w_down = get_w(f"ff{i}_down", z)
    if next_gain is None:
        h_out, hn_out = _mm(z, w_down, "nn", f"ffn{i}_down", res=h), None
    else:
        h_out, hn_out = _mm_res_norm(z, w_down, h, next_gain, f"ffn{i}_down_norm")
    return h_out, hn_out, (hf, u, z), w_up_t, w_down


def _ffn_bwd(dh, h, saved, g, w_up_t, conv_w, conv_b, w_down, i, put_g):
    hf, u, z = saved
    dz = _mm(dh, w_down, "nt", f"ffn{i}_dz")
    g_down = _mm(z, dh, "tn", f"ffn{i}_gdown", out_dtype=GRAD_WIRE_DTYPE)
    tok = put_g(f"ff{i}_down", g_down)
    dgate, dval, g_conv, g_convb = _ffn_mid_bwd(dz, u, conv_w, conv_b + tok, f"ffn{i}_mid_bwd")
    g_up_t = _mm(dgate, hf, "tn", f"ffn{i}_gup_gate", out_dtype=GRAD_WIRE_DTYPE, out_rows=2 * D_FF)
    g_up_t = _mm(dval, hf, "tn", f"ffn{i}_gup_val", out_dtype=GRAD_WIRE_DTYPE, out_rows=2 * D_FF, out_row0=D_FF, into=g_up_t)
    tok = put_g(f"ff{i}_up", g_up_t)
    dh_in, g_norm = _mm_rms_bwd(dval, w_up_t, h, g + tok, dh, f"ffn{i}_dhf_val_norm_bwd", b_row0=D_FF,
                                res=_mm(dgate, w_up_t, "nn", f"ffn{i}_dhf_gate"))
    return dh_in, dict(conv=g_conv, conv_b=g_convb, norm=g_norm)


def _local_step(x, target, W, get_w, put_g, put_small, tok0):
    t = N_META + x.shape[0]
    c64, s64 = _rope_tables(t)
    bm = _head_matrix()
    h0 = jnp.concatenate([W["meta_tokens"], x], axis=0)

    ev_w_in_t, ev_w_out = get_w("ev_in", None), get_w("ev_out", None)
    hn0 = _rms_fwd(h0, W["norm_mix"][0] + tok0, "mix0_norm")
    p0 = _mm(hn0, ev_w_in_t, "nt", "ev_in")
    uc = _ev_a_conv(p0, W["ev_conv_a"])
    y0 = _ev_b(p0, W["ev_conv_b"], _ev_a_norm(uc, W["ev_ln_a_g"], W["ev_ln_a_b"]))
    h1, hf0 = _mm_res_norm(y0, ev_w_out, h0, W["norm_ffn"][0], "ev_out_norm")
    h2, hn1, ffn0, ff0_up_t, ff0_down = _ffn_fwd(h1, hf0, get_w, W["ff_conv"][0], W["ff_conv_b"][0], 0, W["norm_mix"][1])

    od_w_in_t = get_w("od_in", hn1)
    w_att, w_rwkv = od_w_in_t[:ATT_COLS], od_w_in_t[ATT_COLS:]
    pr = _mm(hn1, w_rwkv, "nt", "od_in_rwkv")
    qp, kp, vp = _rope_pack(_mm(hn1, w_att, "nt", "od_in_att"), c64, s64)
    op = _attn_fwd(qp, kp, vp, W["od_sinks"])
    prep_params = [W[n] for n in _PREP_PARAMS]
    xr, xv, decay, k2, a_s, b_s, wr, br, kr, gate = _rwkv_prep(pr, W["od_mu"], prep_params, bm)
    pairs = _rwkv_pairs(decay, a_s, b_s, k2, xv, wr, br, kr, bm)
    ysc, sprev, sab, vbb = _wkv_fwd(k2, xv, a_s, b_s, pairs)
    rk = W["od_r_k"].reshape(1, D_R)
    yr = _rwkv_post(ysc, xr, k2, xv, gate, W["od_lnx_g"], W["od_lnx_b"], rk, bm)
    y1 = jnp.concatenate([op[ATT_PAD:], yr.astype(bf16)], axis=1)
    od_w_out = get_w("od_out", y1)
    h3, hf1 = _mm_res_norm(y1, od_w_out, h2, W["norm_ffn"][1], "od_out_norm")
    h4, _, ffn1, ff1_up_t, ff1_down = _ffn_fwd(h3, hf1, get_w, W["ff_conv"][1], W["ff_conv_b"][1], 1)

    tgt = jnp.concatenate([jnp.zeros((N_META, D_MODEL), f32), target], axis=0)
    loss, dh4, g_norm_final = _final_loss(h4, W["norm_final"], tgt)

    dh3, gf1 = _ffn_bwd(dh4, h3, ffn1, W["norm_ffn"][1], ff1_up_t, W["ff_conv"][1], W["ff_conv_b"][1], ff1_down, 1, put_g)
    dy1 = _mm(dh3, od_w_out, "nt", "od_dy")
    g_od_w_out = _mm(y1, dh3, "tn", "od_gout", out_dtype=GRAD_WIRE_DTYPE)
    tok = put_g("od_out", g_od_w_out)
    dysc, dxr_p, dk2_p, dxv_p, dgate, g_lnx_g, g_lnx_b, g_rk = _rwkv_post_bwd(
        dy1, ysc, xr, k2, xv, gate, W["od_lnx_g"], W["od_lnx_b"] + tok, rk, bm)
    dr, dw, dk, dv, da, db = _wkv_bwd(sprev, sab, vbb, decay, k2, a_s, b_s, xr, dysc)
    prep_grads = _rwkv_prep_bwd(pr, W["od_mu"], prep_params, bm,
                                [[dw], [dk, dk2_p], [da], [db], [dgate], [dr, dxr_p], [dv, dxv_p]])
    dxs, g_mu = prep_grads[0], prep_grads[1]
    dpr = _shift_bwd(dxs, W["od_mu"])
    dop = jnp.concatenate([jnp.zeros((ATT_PAD, D_ATT), f32), dy1[:, :D_ATT]], axis=0).astype(bf16)
    dqp, dkp, dvp, dsk = _attn_bwd(qp, kp, vp, W["od_sinks"], dop)
    dpatt = _rope_bwd(dqp, dkp, dvp, c64, s64)
    n_in = ATT_COLS + RWKV_COLS
    g_od_w_in_t = _mm(dpatt, hn1, "tn", "od_gin_att", out_dtype=GRAD_WIRE_DTYPE, out_rows=n_in)
    g_od_w_in_t = _mm(dpr, hn1, "tn", "od_gin_rwkv", out_dtype=GRAD_WIRE_DTYPE, out_rows=n_in, out_row0=ATT_COLS, into=g_od_w_in_t)
    tok = put_g("od_in", g_od_w_in_t)
    dh2, g_norm_mix1 = _mm_rms_bwd(dpr, w_rwkv, h2, W["norm_mix"][1] + tok, dh3, "od_dhn_rwkv_norm_bwd",
                                   res=_mm(dpatt, w_att, "nn", "od_dhn_att"))

    dh1, gf0 = _ffn_bwd(dh2, h1, ffn0, W["norm_ffn"][0], ff0_up_t, W["ff_conv"][0], W["ff_conv_b"][0], ff0_down, 0, put_g)
    early = dict(
        norm_ffn=jnp.concatenate([gf0["norm"], gf1["norm"]], axis=0), norm_final=g_norm_final.reshape(D_MODEL),
        od_sinks=dsk[:, :N_Q_HEADS], od_mu=g_mu, od_lnx_g=g_lnx_g, od_lnx_b=g_lnx_b, od_r_k=g_rk.reshape(N_Q_HEADS, HEAD_DIM),
        ff_conv=jnp.stack([gf0["conv"], gf1["conv"]]), ff_conv_b=jnp.concatenate([gf0["conv_b"], gf1["conv_b"]], axis=0),
        **dict(zip(_PREP_PARAMS, prep_grads[2:])))
    dy0 = _mm(dh1, ev_w_out, "nt", "ev_dy")
    g_ev_w_out = _mm(y0, dh1, "tn", "ev_gout", out_dtype=GRAD_WIRE_DTYPE)
    tok = put_g("ev_out", g_ev_w_out) + put_small(early)
    duc, g_ln_g, g_ln_b = _ev_a_norm_bwd(dy0, uc, W["ev_ln_a_g"], W["ev_ln_a_b"] + tok)
    dav, dag, g_conv_a = _ev_a_conv_bwd(duc, p0, W["ev_conv_a"])
    dgb, dgc, dxi, g_conv_b = _ev_b_bwd(dy0, p0, W["ev_conv_b"])
    dp0 = jnp.concatenate([dav, dag, dgb, dgc, dxi], axis=1)
    g_ev_w_in_t = _mm(dp0, hn0, "tn", "ev_gin", out_dtype=GRAD_WIRE_DTYPE)
    tok = put_g("ev_in", g_ev_w_in_t)
    dh0, g_norm_mix0 = _mm_rms_bwd(dp0, ev_w_in_t, h0, W["norm_mix"][0] + tok, dh1, "ev_dhn_norm_bwd")

    late = dict(meta_tokens=dh0[:N_META], norm_mix=jnp.concatenate([g_norm_mix0, g_norm_mix1], axis=0),
                ev_conv_a=g_conv_a, ev_ln_a_g=g_ln_g, ev_ln_a_b=g_ln_b, ev_conv_b=g_conv_b)
    return loss, dh0[N_META:], late


HBM = pl.BlockSpec(memory_space=pl.ANY)


def _mesh_pos():
    return lax.axis_index("x"), lax.axis_index("y"), lax.axis_index("c")


def _dev(px, py, pc):
    return 4 * px + 2 * py + pc


def _all_gather(xs, name):
    n = len(xs)

    def body(*refs):
        x_refs, o_refs = refs[:n], refs[n:2 * n]
        send_sems, recv_sems, local_sems = refs[2 * n:]
        x, y, c = _mesh_pos()
        me, sibling = (x, y, c), (x, y, 1 - c)
        chips = [(1 - x, y), (x, 1 - y), (1 - x, 1 - y)]

        def copy(i, k, block, to, from_input=False):
            dst = o_refs[i].at[_dev(*block)]
            return pltpu.make_async_remote_copy(src_ref=x_refs[i] if from_input else dst, dst_ref=dst,
                                                send_sem=send_sems.at[i, k], recv_sem=recv_sems.at[i, k],
                                                device_id=to, device_id_type=MESH)

        mine = [pltpu.make_async_copy(x_refs[i], o_refs[i].at[_dev(*me)], local_sems.at[i]) for i in range(n)]
        for cp in mine:
            cp.start()
        first = []
        for i in range(n):
            first.append(copy(i, 0, me, sibling, True))
            first += [copy(i, 1 + j, me, (*chip, c), True) for j, chip in enumerate(chips)]
        for cp in first:
            cp.start()
        passed = []
        for j, chip in enumerate(chips):
            for i in range(n):
                copy(i, 1 + j, (*chip, c), me).wait_recv()
                fwd = copy(i, 4 + j, (*chip, c), sibling)
                fwd.start()
                passed.append(fwd)
        for i in range(n):
            copy(i, 0, sibling, me).wait_recv()
            for j, chip in enumerate(chips):
                copy(i, 4 + j, (*chip, 1 - c), me).wait_recv()
        for cp in first + passed:
            cp.wait_send()
        for cp in mine:
            cp.wait()

    return _pc(body, name=name, in_specs=[HBM] * n, out_specs=tuple([HBM] * n),
               out_shape=tuple(S((N_DEV,) + x.shape, x.dtype) for x in xs),
               scratch_shapes=[pltpu.SemaphoreType.DMA((n, 7)), pltpu.SemaphoreType.DMA((n, 7)),
                               pltpu.SemaphoreType.DMA((n,))])(*xs)


HBM_SPEC = pl.BlockSpec(memory_space=pltpu.HBM)
SEM_SPEC = pl.BlockSpec(memory_space=pltpu.SEMAPHORE)
DATAFLOW = pltpu.SideEffectType.DATAFLOW_SIDE_EFFECTING
_PEER_FLIPS = ((1, 0, 0), (0, 1, 0), (1, 1, 0), (1, 0, 1), (0, 1, 1), (1, 1, 1), (0, 0, 1))
N_PEERS = len(_PEER_FLIPS)


def _peers(x, y, c):
    return [((1 - x) if fx else x, (1 - y) if fy else y, (1 - c) if fc else c) for fx, fy, fc in _PEER_FLIPS]


def _xchg_start(srcs, lands, scatter, name):
    n = len(srcs)

    def body(*refs):
        src_refs, land_refs = refs[:n], refs[n:2 * n]
        send_sems, recv_sems, token = refs[2 * n], refs[2 * n + 1], refs[-1]
        x, y, c = _mesh_pos()
        me = _dev(x, y, c)
        for i in range(n):
            for k, peer in enumerate(_peers(x, y, c)):
                pltpu.make_async_remote_copy(src_ref=src_refs[i].at[_dev(*peer)] if scatter else src_refs[i],
                                             dst_ref=land_refs[i].at[me], send_sem=send_sems.at[i * N_PEERS + k],
                                             recv_sem=recv_sems.at[i * N_PEERS + k], device_id=peer, device_id_type=MESH).start()
        token[...] = jnp.zeros_like(token)

    arrs = list(srcs) + list(lands)
    outs = _pc(body, name=name,
               out_shape=(pltpu.SemaphoreType.DMA((n * N_PEERS,)), pltpu.SemaphoreType.DMA((n * N_PEERS,)),
                          *[pltpu.HBM(a.shape, a.dtype) for a in arrs], S((SUBLANES, LANES), f32)),
               in_specs=[HBM_SPEC] * (2 * n),
               out_specs=(SEM_SPEC, SEM_SPEC, *[HBM_SPEC] * (2 * n), pl.BlockSpec(memory_space=pltpu.VMEM)),
               input_output_aliases={i: 2 + i for i in range(2 * n)},
               compiler_params=pltpu.CompilerParams(has_side_effects=DATAFLOW))(
        *[pltpu.with_memory_space_constraint(a, pltpu.HBM) for a in arrs])
    return (outs[0], outs[1], list(outs[2:2 + n]), list(outs[2 + n:2 + 2 * n]), scatter), outs[-1]


def _xchg_wait(handle, after, name):
    send_sems, recv_sems, srcs, lands, scatter = handle
    n = len(srcs)

    def body(*refs):
        src_refs, land_refs = refs[:n], refs[n:2 * n]
        send, recv = refs[2 * n], refs[2 * n + 1]
        x, y, c = _mesh_pos()
        for i in range(n):
            for k in range(N_PEERS):
                cp = pltpu.make_async_remote_copy(src_ref=src_refs[i].at[0] if scatter else src_refs[i],
                                                  dst_ref=land_refs[i].at[0], send_sem=send.at[i * N_PEERS + k],
                                                  recv_sem=recv.at[i * N_PEERS + k],
                                                  device_id=(x, y, c), device_id_type=MESH)
                cp.wait_send()
                cp.wait_recv()

    arrs = srcs + lands
    outs = _pc(body, name=name, out_shape=tuple(pltpu.HBM(a.shape, a.dtype) for a in arrs),
               in_specs=[HBM_SPEC] * (2 * n) + [SEM_SPEC, SEM_SPEC, pl.BlockSpec(memory_space=pl.ANY)],
               out_specs=tuple([HBM_SPEC] * (2 * n)), input_output_aliases={i: i for i in range(2 * n)},
               compiler_params=pltpu.CompilerParams(has_side_effects=DATAFLOW))(*arrs, send_sems, recv_sems, after)
    return list(outs[:n]), list(outs[n:])


def _rs_sum(g, land, me_vec, name):
    _, r, cols = g.shape
    tr = _divisor_block(r, 16, min(r, 352))

    def body(me_ref, g_ref, *rest):
        o_ref = rest[-1]
        acc = g_ref[0].astype(f32)
        for l_ref in rest[:-1]:
            acc = acc + l_ref[0].astype(f32)
        o_ref[...] = acc

    blk = lambda f: pl.BlockSpec((1, tr, cols), f)
    grid_spec = pltpu.PrefetchScalarGridSpec(
        num_scalar_prefetch=1, grid=(r // tr,),
        in_specs=[blk(lambda i, me_ref: (me_ref[0], i, 0))]
        + [blk(lambda i, me_ref, k=k: ((me_ref[0] + k) % N_DEV, i, 0)) for k in range(1, N_DEV)],
        out_specs=pl.BlockSpec((tr, cols), lambda i, me_ref: (i, 0)))
    return _pc(body, name=name, grid_spec=grid_spec, out_shape=S((r, cols), f32),
               compiler_params=_cparams(("arbitrary",)))(me_vec, g, *([land] * (N_DEV - 1)))


def _rs_sum_adamw(g, land, me_vec, w, m, v, layer, into, name):
    _, r, cols = g.shape
    nl = w.shape[0]
    tr = _divisor_block(r, 16, min(r, 176))
    c1, c2 = 1.0 - ADAM_B1 ** ADAM_STEP, 1.0 - ADAM_B2 ** ADAM_STEP

    def body(me_ref, g_ref, *rest):
        lands, (w_ref, m_ref, v_ref) = rest[:N_DEV - 1], rest[N_DEV - 1:N_DEV + 2]
        go_ref, d_ref, nm_ref, nv_ref = rest[-4:]
        gv = g_ref[0].astype(f32)
        for l_ref in lands:
            gv = gv + l_ref[0].astype(f32)
        nm = ADAM_B1 * m_ref[0] + (1.0 - ADAM_B1) * gv
        nv = ADAM_B2 * v_ref[0] + (1.0 - ADAM_B2) * (gv * gv)
        d_ref[0] = -ADAM_LR * ((nm / c1) / (jnp.sqrt(nv / c2) + ADAM_EPS) + ADAM_WD * w_ref[0])
        go_ref[0] = gv
        nm_ref[0] = nm
        nv_ref[0] = nv

    blk = lambda f: pl.BlockSpec((1, tr, cols), f)
    lay = blk(lambda i, me_ref: (layer, i, 0))
    n_in = N_DEV + 3
    extra = [] if into is None else list(into)
    grid_spec = pltpu.PrefetchScalarGridSpec(
        num_scalar_prefetch=1, grid=(r // tr,),
        in_specs=[blk(lambda i, me_ref: (me_ref[0], i, 0))]
        + [blk(lambda i, me_ref, k=k: ((me_ref[0] + k) % N_DEV, i, 0)) for k in range(1, N_DEV)]
        + [lay] * 3 + [pl.BlockSpec(memory_space=pl.ANY)] * len(extra),
        out_specs=(lay,) * 4)
    return _pc(body if into is None else (lambda *refs: body(*refs[:n_in + 1], *refs[n_in + 1 + 4:])),
               name=name, grid_spec=grid_spec, out_shape=(S((nl, r, cols), f32),) * 4,
               input_output_aliases={} if into is None else {n_in + 1 + j: j for j in range(4)},
               compiler_params=_cparams(("arbitrary",)))(me_vec, g, *([land] * (N_DEV - 1)), w, m, v, *extra)


def _sum_devices(a, name):
    def body(a_ref, o_ref):
        acc = a_ref[0]
        for d in range(1, N_DEV):
            acc = acc + a_ref[d]
        o_ref[...] = acc

    return _pc(body, name=name, grid=(1,), in_specs=[_full(a.shape)], out_specs=_full(a.shape[1:]),
               out_shape=S(a.shape[1:], a.dtype), compiler_params=_cparams(("arbitrary",)))(a)


def _adamw(w, m, v, g, name):
    shape = w.shape
    w2, m2, v2, g2 = (a.reshape(-1, shape[-1]) for a in (w, m, v, g))
    rows, cols = w2.shape
    tr = rows if rows % SUBLANES else _divisor_block(rows, SUBLANES, max(SUBLANES, min(rows, ADAMW_BLOCK_ELEMS // cols)))
    c1, c2 = 1.0 - ADAM_B1 ** ADAM_STEP, 1.0 - ADAM_B2 ** ADAM_STEP

    def body(w_ref, m_ref, v_ref, g_ref, d_ref, nm_ref, nv_ref):
        gv = g_ref[...]
        nm = ADAM_B1 * m_ref[...] + (1.0 - ADAM_B1) * gv
        nv = ADAM_B2 * v_ref[...] + (1.0 - ADAM_B2) * (gv * gv)
        d_ref[...] = -ADAM_LR * ((nm / c1) / (jnp.sqrt(nv / c2) + ADAM_EPS) + ADAM_WD * w_ref[...])
        nm_ref[...] = nm
        nv_ref[...] = nv

    blk = pl.BlockSpec((tr, cols), lambda i: (i, 0))
    outs = _pc(body, name=name, grid=(rows // tr,), in_specs=[blk] * 4, out_specs=(blk,) * 3,
               out_shape=(S((rows, cols), f32),) * 3, compiler_params=_cparams(("arbitrary",)))(w2, m2, v2, g2)
    return tuple(o.reshape(shape) for o in outs)


_WEIGHTS = ("meta_tokens", "norm_mix", "norm_ffn", "norm_final", "ev_w_in", "ev_conv_a", "ev_ln_a_g", "ev_ln_a_b",
            "ev_conv_b", "ev_w_out", "od_w_in", "od_sinks", "od_mu", "od_w0", "od_w2", "od_a0", "od_a2", "od_g2",
            "od_k_k", "od_k_a", "od_r_k", "od_lnx_g", "od_lnx_b", "od_w_out", "ff_w_up", "ff_conv", "ff_conv_b", "ff_w_down")
_SMALL_SHARDED = (("meta_tokens", 1), ("ev_conv_a", 2), ("ev_conv_b", 2), ("od_mu", 1), ("od_w0", 1), ("od_w2", 2),
                  ("od_a0", 1), ("od_a2", 2), ("od_g2", 2), ("od_k_k", 1), ("od_k_a", 1), ("od_lnx_g", 1),
                  ("od_lnx_b", 1), ("ff_conv", 2))
_SMALL_REPLICATED = ("norm_mix", "norm_ffn", "norm_final", "ev_ln_a_g", "ev_ln_a_b", "od_sinks", "od_r_k", "ff_conv_b")
SLAB_UNIT = SUBLANES * LANES


def _pack(arrs):
    flat = jnp.concatenate([a.reshape(-1).astype(f32) for a in arrs])
    pad = (-flat.shape[0]) % SLAB_UNIT
    return jnp.pad(flat, (0, pad)).reshape(-1, LANES)


def _unpack(flat, shapes):
    out, off = [], 0
    for shp in shapes:
        size = 1
        for s in shp:
            size *= s
        out.append(flat[..., off:off + size].reshape(flat.shape[:-1] + tuple(shp)))
        off += size
    return out


def _full_shape(shape, axis):
    return tuple(N_DEV * s if i == axis else s for i, s in enumerate(shape))


def kernel(x, meta_tokens, norm_mix, norm_ffn, norm_final, ev_w_in, ev_conv_a, ev_ln_a_g, ev_ln_a_b, ev_conv_b, ev_w_out, od_w_in, od_sinks, od_mu, od_w0, od_w2, od_a0, od_a2, od_g2, od_k_k, od_k_a, od_r_k, od_lnx_g, od_lnx_b, od_w_out, ff_w_up, ff_conv, ff_conv_b, ff_w_down, loss_target, m_meta_tokens, m_norm_mix, m_norm_ffn, m_norm_final, m_ev_w_in, m_ev_conv_a, m_ev_ln_a_g, m_ev_ln_a_b, m_ev_conv_b, m_ev_w_out, m_od_w_in, m_od_sinks, m_od_mu, m_od_w0, m_od_w2, m_od_a0, m_od_a2, m_od_g2, m_od_k_k, m_od_k_a, m_od_r_k, m_od_lnx_g, m_od_lnx_b, m_od_w_out, m_ff_w_up, m_ff_conv, m_ff_conv_b, m_ff_w_down, v_meta_tokens, v_norm_mix, v_norm_ffn, v_norm_final, v_ev_w_in, v_ev_conv_a, v_ev_ln_a_g, v_ev_ln_a_b, v_ev_conv_b, v_ev_w_out, v_od_w_in, v_od_sinks, v_od_mu, v_od_w0, v_od_w2, v_od_a0, v_od_a2, v_od_g2, v_od_k_k, v_od_k_a, v_od_r_k, v_od_lnx_g, v_od_lnx_b, v_od_w_out, v_ff_w_up, v_ff_conv, v_ff_conv_b, v_ff_w_down):
    A = dict(locals())
    px, py, pc = _mesh_pos()
    me = _dev(px, py, pc)
    me_vec = jnp.reshape(me, (1,)).astype(jnp.int32)
    rows = lambda a: a.reshape(N_DEV * a.shape[1], a.shape[2])
    blocks = lambda a: a.reshape(N_DEV, a.shape[0] // N_DEV, a.shape[1])

    shards = dict(ev_in=ev_w_in[0].T, ev_out=ev_w_out[0], ff0_up=ff_w_up[0].T, ff0_down=ff_w_down[0], od_in=od_w_in[0].T,
                  od_out=od_w_out[0], ff1_up=ff_w_up[1].T, ff1_down=ff_w_down[1])
    shards = {n: b.astype(bf16) for n, b in shards.items()}
    small_shapes = [A[n].shape for n, _ in _SMALL_SHARDED]
    gathered = _all_gather([shards["ev_in"], shards["ev_out"], _pack([A[n] for n, _ in _SMALL_SHARDED])], "gather_first")
    gathered, shards = lax.optimization_barrier((gathered, shards))
    fetch, tok0 = {}, jnp.zeros((), f32)
    for n in ("ff0_up", "ff0_down", "od_in", "od_out", "ff1_up", "ff1_down"):
        shard, tok0 = lax.optimization_barrier((shards[n], tok0))
        land = lax.dynamic_update_slice(lax.empty((N_DEV,) + shard.shape, bf16), shard[None], (me, 0, 0))
        fetch[n], token = _xchg_start([shard], [land], False, f"gather_{n}_start")
        tok0 = tok0 + token[0, 0]

    def get_w(n, after):
        if n in ("ev_in", "ev_out"):
            return rows(gathered[("ev_in", "ev_out").index(n)])
        return rows(_xchg_wait(fetch[n], after, f"gather_{n}_wait")[1][0])

    W = {}
    for (n, ax), seg in zip(_SMALL_SHARDED, _unpack(gathered[-1].reshape(N_DEV, -1), small_shapes)):
        W[n] = jnp.moveaxis(seg, 0, ax).reshape(_full_shape(A[n].shape, ax))
    for n in ("ev_conv_a", "ev_conv_b", "od_w2", "od_a2", "od_g2"):
        W[n] = W[n][0]
    for n in _SMALL_REPLICATED:
        W[n] = A[n]
    W["od_r_k"] = od_r_k[0]

    small_shape = {n: _full_shape(A[n].shape, ax) for n, ax in _SMALL_SHARDED}
    small_shape.update({n: A[n].shape for n in _SMALL_REPLICATED})
    sent, small_sent, small_names = {}, {}, {}

    def put_g(n, g):
        g8 = blocks(g)
        sent[n], token = _xchg_start([g8], [lax.empty(g8.shape, g8.dtype)], True, f"reduce_{n}_start")
        return token[0, 0]

    def put_small(gs, stage="early"):
        small_names[stage] = sorted(gs)
        slab = _pack([gs[n] for n in small_names[stage]])
        land = lax.dynamic_update_slice(lax.empty((N_DEV,) + slab.shape, f32), slab[None], (me, 0, 0))
        small_sent[stage], small_tok[stage] = _xchg_start([slab], [land], False, f"gather_{stage}_small_grads_start")
        return small_tok[stage][0, 0]

    small_tok = {}
    loss_tile, grad_x, late = _local_step(x[0], loss_target[0], W, get_w, put_g, put_small, tok0)
    put_small(late, "late")
    late_tok = small_tok["late"]

    gsh, prev, fused = {}, late_tok, {}
    delta, new_m, new_v = {}, {}, {}
    row_sharded = dict(ff1_down=("ff_w_down", 1), od_out=("od_w_out", 0), ff0_down=("ff_w_down", 0), ev_out=("ev_w_out", 0))
    for n in ("ff1_down", "ff1_up", "od_out", "od_in", "ff0_down", "ff0_up", "ev_out", "ev_in"):
        srcs, lands = _xchg_wait(sent[n], prev, f"reduce_{n}_wait")
        if n in row_sharded:
            wn, layer = row_sharded[n]
            fused[wn] = _rs_sum_adamw(srcs[0], lands[0], me_vec, A[wn], A["m_" + wn], A["v_" + wn], layer, fused.get(wn),
                                      f"reduce_{n}_sum_adamw")
            prev = fused[wn][0]
        else:
            gsh[n] = prev = _rs_sum(srcs[0], lands[0], me_vec, f"reduce_{n}_sum")
    grads = dict(ev_w_in=gsh["ev_in"].T[None], od_w_in=gsh["od_in"].T[None],
                 ff_w_up=jnp.stack([gsh["ff0_up"].T, gsh["ff1_up"].T]))
    for wn, (g_sum, d, nm, nv) in fused.items():
        grads[wn], delta[wn], new_m[wn], new_v[wn] = g_sum, d, nm, nv

    for n in ("ff_w_up", "od_w_in", "ev_w_in"):
        delta[n], new_m[n], new_v[n] = _adamw(A[n], A["m_" + n], A["v_" + n], grads[n], "adamw_" + n)
    for stage in ("early", "late"):
        gsm = _xchg_wait(small_sent[stage], delta["ev_w_in"], f"gather_{stage}_small_grads_wait")[1][0]
        summed = _sum_devices(gsm, f"sum_{stage}_small_grads").reshape(-1)
        for n, full in zip(small_names[stage], _unpack(summed, [small_shape[n] for n in small_names[stage]])):
            grads[n] = full
    for n, ax in _SMALL_SHARDED:
        size = A[n].shape[ax]
        grads[n] = lax.dynamic_slice_in_dim(grads[n], me * size, size, axis=ax)
    for n in small_shape:
        delta[n], new_m[n], new_v[n] = _adamw(A[n], A["m_" + n], A["v_" + n], grads[n], "adamw_" + n)

    loss = lax.psum(loss_tile[0, 0], ("x", "y", "c"))
    return (loss, grad_x[None], *[grads[n] for n in _WEIGHTS], *[delta[n] for n in _WEIGHTS],
            *[new_m[n] for n in _WEIGHTS], *[new_v[n] for n in _WEIGHTS])
```

```python
import jax
import jax.numpy as jnp
from jax import lax
from jax.experimental import pallas as pl
from jax.experimental.pallas import tpu as pltpu

f32, bf16 = jnp.float32, jnp.bfloat16

D_MODEL = 1024
N_META = 16
RMS_EPS = 1e-6
LN_EPS = 1e-5
D_A = 512
CONV_A_WIDTH = 31
CONV_B_WIDTH = 3
HEAD_DIM = 64
N_Q_HEADS = 8
N_KV_HEADS = 2
GQA_GROUP = 4
D_ATT = 512
D_KV = 128
BLOCK = 128
ROPE_THETA = 10000.0
D_R = 512
LORA_W, LORA_A, LORA_G = 64, 64, 128
RWKV_GN_EPS = 64e-5
ATT_COLS = D_ATT + 2 * D_KV
RWKV_COLS = 3 * D_R + LORA_W + LORA_A + LORA_G
D_FF = 2816
FF_CONV_WIDTH = 3
FF_BLOCK = 256
NEG_INF = -1e30
ATT_PAD = BLOCK - N_META
ATT_SCALE = HEAD_DIM ** -0.5

ADAM_LR, ADAM_B1, ADAM_B2, ADAM_EPS, ADAM_WD, ADAM_STEP = 0.001, 0.9, 0.999, 1e-08, 0.01, 10

N_DEV = 8
LANES = 128
SUBLANES = 8
SCAN_CHUNK = 48
PAIR_ROWS = 4 * HEAD_DIM
V7X_VMEM_LIMIT = 56 * 1024 * 1024
ADAMW_BLOCK_ELEMS = 400 * 1024
GRAD_WIRE_DTYPE = bf16
MESH = pl.DeviceIdType.MESH
S = jax.ShapeDtypeStruct


def _pc(body, **kw):
    return pl.pallas_call(body, **kw)


def _cparams(sem=None):
    return pltpu.CompilerParams(dimension_semantics=sem, vmem_limit_bytes=V7X_VMEM_LIMIT)


def _divisor_block(t, unit, limit):
    best = unit
    for rb in range(unit, limit + 1, unit):
        if t % rb == 0:
            best = rb
    assert t % best == 0, (t, unit)
    return best


def _row_block(t):
    return _divisor_block(t, 16, 704)


def _row_block8(t):
    return _divisor_block(t, 8, 344)


def _col_tile(n, cap):
    return _divisor_block(n, LANES, min(n, cap)) if n % LANES == 0 else n


def _full(shape):
    nd = len(shape)
    return pl.BlockSpec(shape, lambda *_: (0,) * nd)


def _sigmoid(x):
    return jax.nn.sigmoid(x)


_DIMS = {"nn": (((1,), (0,)), ((), ())), "nt": (((1,), (1,)), ((), ())), "tn": (((0,), (0,)), ((), ()))}
MM_MAX_K = 2816
MM_MAX_TM = 704
MM_MAX_TN = 1408


def _mm(a, b, mode, name, out_dtype=f32, res=None, b_row0=0, out_rows=None, out_row0=0, into=None):
    if mode == "nn":
        (m, k), n, k2 = a.shape, b.shape[1], a.shape[1]
        assert b_row0 % k == 0 and b_row0 + k <= b.shape[0], (a.shape, b.shape, b_row0)
    elif mode == "nt":
        (m, k), (n, k2) = a.shape, b.shape
    else:
        (k, m), (k2, n) = a.shape, b.shape
    assert k == k2, (a.shape, b.shape, mode)
    tm = _row_block(m) if m % LANES else _col_tile(m, MM_MAX_TM)
    tn = _col_tile(n, MM_MAX_TN)
    nk = 1 if (mode == "tn" or k <= MM_MAX_K) else k // MM_MAX_K
    tk = k // nk
    assert tk * nk == k
    dims = _DIMS[mode]

    def body(a_ref, b_ref, *rest):
        part = lax.dot_general(a_ref[...].astype(bf16), b_ref[...].astype(bf16), dims, preferred_element_type=f32)
        if nk == 1:
            o_ref = rest[-1]
            if res is not None:
                part = part + rest[0][...]
            o_ref[...] = part.astype(out_dtype)
            return
        o_ref, acc_ref = rest[-2], rest[-1]
        kk = pl.program_id(2)

        @pl.when(kk == 0)
        def _():
            acc_ref[...] = part

        @pl.when(kk > 0)
        def _():
            acc_ref[...] += part

        @pl.when(kk == nk - 1)
        def _():
            acc = acc_ref[...]
            if res is not None:
                acc = acc + rest[0][...]
            o_ref[...] = acc.astype(out_dtype)

    if mode == "tn":
        a_spec = pl.BlockSpec((k, tm), lambda i, j, kk: (0, i))
    else:
        a_spec = pl.BlockSpec((tm, tk), lambda i, j, kk: (i, kk))
    if mode == "nt":
        b_spec = pl.BlockSpec((tn, tk), lambda i, j, kk: (j, kk))
    else:
        b_spec = pl.BlockSpec((tk, tn), lambda i, j, kk: (kk + b_row0 // tk, j))
    assert out_row0 % tm == 0 and res is None or out_row0 == 0
    o_spec = pl.BlockSpec((tm, tn), lambda i, j, kk: (i + out_row0 // tm, j))
    ins, specs, aliases = [a, b], [a_spec, b_spec], {}
    if res is not None:
        ins.append(res)
        specs.append(o_spec)
    if into is not None:
        assert into.shape == (out_rows, n) and into.dtype == out_dtype
        aliases = {len(ins): 0}
        ins.append(into)
        specs.append(pl.BlockSpec(memory_space=pl.ANY))
    scratch = [pltpu.VMEM((tm, tn), f32)] if nk > 1 else []
    return _pc(body, name=name, grid=(m // tm, n // tn, nk), in_specs=specs, out_specs=o_spec,
               out_shape=S((out_rows or m, n), out_dtype), scratch_shapes=scratch, input_output_aliases=aliases,
               compiler_params=_cparams(("arbitrary", "arbitrary", "arbitrary")))(*ins)


def _rms_fwd(x, g, name):
    t, d = x.shape
    rb = _row_block(t)

    def body(x_ref, g_ref, o_ref):
        xv = x_ref[...]
        rstd = lax.rsqrt(jnp.mean(xv * xv, axis=-1, keepdims=True) + RMS_EPS)
        o_ref[...] = (xv * rstd * g_ref[...]).astype(bf16)

    row = pl.BlockSpec((rb, d), lambda i: (i, 0))
    return _pc(body, name=name, grid=(t // rb,), in_specs=[row, _full((1, d))], out_specs=row,
               out_shape=S((t, d), bf16), compiler_params=_cparams(("arbitrary",)))(x, g.reshape(1, d))


def _mm_rms_bwd(a, b, x, g, dres, name, b_row0=0, res=None):
    (m, k), n = a.shape, b.shape[1]
    assert k <= MM_MAX_K and b_row0 % k == 0 and b_row0 + k <= b.shape[0] and x.shape == (m, n)
    tm = _row_block(m)

    def body(a_ref, b_ref, x_ref, g_ref, dres_ref, *rest):
        dx_ref, dg_ref = rest[-2], rest[-1]

        @pl.when(pl.program_id(0) == 0)
        def _():
            dg_ref[...] = jnp.zeros_like(dg_ref)
        dy = jnp.dot(a_ref[...].astype(bf16), b_ref[...].astype(bf16), preferred_element_type=f32)
        if res is not None:
            dy = dy + rest[0][...]
        xv = x_ref[...]
        rstd = lax.rsqrt(jnp.mean(xv * xv, axis=-1, keepdims=True) + RMS_EPS)
        xn = xv * rstd
        dg_ref[...] += jnp.sum(dy * xn, axis=0, keepdims=True)
        dxh = dy * g_ref[...]
        dx_ref[...] = dres_ref[...] + rstd * (dxh - xn * jnp.mean(dxh * xn, axis=-1, keepdims=True))

    row = pl.BlockSpec((tm, n), lambda i: (i, 0))
    ins = [a, b, x, g.reshape(1, n), dres] + ([res] if res is not None else [])
    specs = [pl.BlockSpec((tm, k), lambda i: (i, 0)), pl.BlockSpec((k, n), lambda i: (b_row0 // k, 0)), row, _full((1, n)), row]
    specs += [row] if res is not None else []
    return _pc(body, name=name, grid=(m // tm,), in_specs=specs, out_specs=(row, _full((1, n))),
               out_shape=(S((m, n), f32), S((1, n), f32)), compiler_params=_cparams(("arbitrary",)))(*ins)


def _mm_res_norm(a, b, res, g, name):
    (m, k), n = a.shape, b.shape[1]
    assert k == b.shape[0] and k <= MM_MAX_K and res.shape == (m, n)
    tm = _row_block(m)

    def body(a_ref, b_ref, r_ref, g_ref, h_ref, hn_ref):
        h = jnp.dot(a_ref[...].astype(bf16), b_ref[...].astype(bf16), preferred_element_type=f32) + r_ref[...]
        h_ref[...] = h
        rstd = lax.rsqrt(jnp.mean(h * h, axis=-1, keepdims=True) + RMS_EPS)
        hn_ref[...] = (h * rstd * g_ref[...]).astype(bf16)

    row = pl.BlockSpec((tm, n), lambda i: (i, 0))
    return _pc(body, name=name, grid=(m // tm,),
               in_specs=[pl.BlockSpec((tm, k), lambda i: (i, 0)), _full((k, n)), row, _full((1, n))],
               out_specs=(row, row), out_shape=(S((m, n), f32), S((m, n), bf16)),
               compiler_params=_cparams(("arbitrary",)))(a, b, res, g.reshape(1, n))


def _final_loss(h, g, target_padded):
    t, d = h.shape
    rb = _row_block8(t)

    def body(x_ref, g_ref, t_ref, loss_ref, dx_ref, dg_ref):
        i = pl.program_id(0)

        @pl.when(i == 0)
        def _():
            dg_ref[...] = jnp.zeros_like(dg_ref)
            loss_ref[...] = jnp.zeros_like(loss_ref)
        xv = x_ref[...]
        rstd = lax.rsqrt(jnp.mean(xv * xv, axis=-1, keepdims=True) + RMS_EPS)
        xn = xv * rstd
        gv = g_ref[...]
        row = i * rb + lax.broadcasted_iota(jnp.int32, (rb, 1), 0)
        diff = jnp.where(row >= N_META, xn * gv - t_ref[...], 0.0)
        loss_ref[...] += 0.5 * jnp.sum(jnp.mean(diff * diff, axis=-1, keepdims=True))
        dout = diff * (1.0 / d)
        dg_ref[...] += jnp.sum(dout * xn, axis=0, keepdims=True)
        dxh = dout * gv
        dx_ref[...] = rstd * (dxh - xn * jnp.mean(dxh * xn, axis=-1, keepdims=True))

    row = pl.BlockSpec((rb, d), lambda i: (i, 0))
    return _pc(body, name="final_loss", grid=(t // rb,), in_specs=[row, _full((1, d)), row],
               out_specs=(_full((SUBLANES, LANES)), row, _full((1, d))),
               out_shape=(S((SUBLANES, LANES), f32), S((t, d), f32), S((1, d), f32)),
               compiler_params=_cparams(("arbitrary",)))(h, g.reshape(1, d), target_padded)


CONV_LEAD = 32


def _fill_front_padded(pad_ref, x, t):
    pad_ref[0:CONV_LEAD, :] = jnp.zeros((CONV_LEAD, x.shape[1]), f32)
    pad_ref[CONV_LEAD:CONV_LEAD + t, :] = x


def _fill_back_padded(pad_ref, x, t):
    pad_ref[0:t, :] = x
    pad_ref[t:t + CONV_LEAD, :] = jnp.zeros((CONV_LEAD, x.shape[1]), f32)


def _conv_rows(pad_ref, w_ref, kw, r0, nr):
    acc = None
    for j in range(kw):
        lo = CONV_LEAD + r0 - (kw - 1) + j
        term = w_ref[j:j + 1, :] * pad_ref[lo:lo + nr, :]
        acc = term if acc is None else acc + term
    return acc


def _conv_t_rows(padb_ref, w_ref, kw, r0, nr):
    acc = None
    for j in range(kw):
        lo = r0 + (kw - 1) - j
        term = w_ref[j:j + 1, :] * padb_ref[lo:lo + nr, :]
        acc = term if acc is None else acc + term
    return acc


def _conv_dw_rows(dy_blk, pad_ref, kw, r0, nr):
    out = []
    for j in range(kw):
        lo = CONV_LEAD + r0 - (kw - 1) + j
        out.append(jnp.sum(dy_blk * pad_ref[lo:lo + nr, :], axis=0, keepdims=True))
    return out


def _acc_list(a, b):
    return b if a is None else [x + y for x, y in zip(a, b)]


def _ev_a_conv(p, conv_a):
    t = p.shape[0]
    cr = _row_block8(t)
    nb = D_A // LANES

    def body(av_ref, ag_ref, w_ref, o_ref, pad_ref):
        _fill_front_padded(pad_ref, av_ref[...] * _sigmoid(ag_ref[...]), t)
        for r in range(t // cr):
            o_ref[r * cr:(r + 1) * cr, :] = _conv_rows(pad_ref, w_ref, CONV_A_WIDTH, r * cr, cr)

    col = lambda off: pl.BlockSpec((t, LANES), lambda j: (0, j + off))
    return _pc(body, name="ev_a_conv", grid=(nb,),
               in_specs=[col(0), col(nb), pl.BlockSpec((CONV_A_WIDTH, LANES), lambda j: (0, j))],
               out_specs=col(0), out_shape=S((t, D_A), f32),
               scratch_shapes=[pltpu.VMEM((t + CONV_LEAD, LANES), f32)],
               compiler_params=_cparams(("arbitrary",)))(p, p, conv_a)


def _ln_silu(uc, g, b):
    mu = jnp.mean(uc, axis=-1, keepdims=True)
    xc = uc - mu
    var = jnp.mean(xc * xc, axis=-1, keepdims=True)
    y = xc * lax.rsqrt(var + LN_EPS) * g + b
    return y * _sigmoid(y)


def _ev_a_norm(uc, g, b):
    t, d = uc.shape
    rb = _row_block(t)

    def body(u_ref, g_ref, b_ref, o_ref):
        o_ref[...] = _ln_silu(u_ref[...], g_ref[...], b_ref[...]).astype(bf16)

    row = pl.BlockSpec((rb, d), lambda i: (i, 0))
    return _pc(body, name="ev_a_norm", grid=(t // rb,), in_specs=[row, _full((1, d)), _full((1, d))],
               out_specs=row, out_shape=S((t, 2 * d), bf16), compiler_params=_cparams(("arbitrary",)))(uc, g, b)


def _ev_a_norm_bwd(dy, uc, g, b):
    t, d = uc.shape
    rb = _row_block8(t)

    def body(dy_ref, u_ref, g_ref, b_ref, du_ref, dg_ref, db_ref):
        @pl.when(pl.program_id(0) == 0)
        def _():
            dg_ref[...] = jnp.zeros_like(dg_ref)
            db_ref[...] = jnp.zeros_like(db_ref)
        _, vjp = jax.vjp(_ln_silu, u_ref[...], g_ref[...], b_ref[...])
        du, dg, db = vjp(dy_ref[...])
        du_ref[...] = du
        dg_ref[...] += dg
        db_ref[...] += db

    row = pl.BlockSpec((rb, d), lambda i: (i, 0))
    return _pc(body, name="ev_a_norm_bwd", grid=(t // rb,), in_specs=[row, row, _full((1, d)), _full((1, d))],
               out_specs=(row, _full((1, d)), _full((1, d))),
               out_shape=(S((t, d), f32), S((1, d), f32), S((1, d), f32)),
               compiler_params=_cparams(("arbitrary",)))(dy, uc, g, b)


def _ev_a_conv_bwd(duc, p, conv_a):
    t = p.shape[0]
    cr = _row_block8(t)
    nb = D_A // LANES

    def body(dy_ref, av_ref, ag_ref, w_ref, dav_ref, dag_ref, dw_ref, pad_ref, padb_ref):
        _fill_front_padded(pad_ref, av_ref[...] * _sigmoid(ag_ref[...]), t)
        _fill_back_padded(padb_ref, dy_ref[...], t)
        dw = None
        for r in range(t // cr):
            rows = slice(r * cr, (r + 1) * cr)
            du = _conv_t_rows(padb_ref, w_ref, CONV_A_WIDTH, r * cr, cr)
            avr = av_ref[rows, :]
            sgr = _sigmoid(ag_ref[rows, :])
            dav_ref[rows, :] = du * sgr
            dag_ref[rows, :] = du * avr * sgr * (1.0 - sgr)
            dw = _acc_list(dw, _conv_dw_rows(dy_ref[rows, :], pad_ref, CONV_A_WIDTH, r * cr, cr))
        for j in range(CONV_A_WIDTH):
            dw_ref[j:j + 1, :] = dw[j]

    col = lambda off: pl.BlockSpec((t, LANES), lambda j: (0, j + off))
    wsp = pl.BlockSpec((CONV_A_WIDTH, LANES), lambda j: (0, j))
    return _pc(body, name="ev_a_conv_bwd", grid=(nb,), in_specs=[col(0), col(0), col(nb), wsp],
               out_specs=(col(0), col(0), wsp),
               out_shape=(S((t, D_A), f32), S((t, D_A), f32), S((CONV_A_WIDTH, D_A), f32)),
               scratch_shapes=[pltpu.VMEM((t + CONV_LEAD, LANES), f32), pltpu.VMEM((t + CONV_LEAD, LANES), f32)],
               compiler_params=_cparams(("arbitrary",)))(duc, p, p, conv_a)


def _ev_b(p, conv_b, y):
    t = p.shape[0]
    cr = _row_block8(t)
    nb = D_A // LANES

    def body(gb_ref, gc_ref, xi_ref, w_ref, y_ref, o_ref, pad_ref, stage_ref):
        _fill_front_padded(pad_ref, gc_ref[...] * xi_ref[...], t)
        for r in range(t // cr):
            rows = slice(r * cr, (r + 1) * cr)
            stage_ref[rows, :] = gb_ref[rows, :] * _conv_rows(pad_ref, w_ref, CONV_B_WIDTH, r * cr, cr)
        o_ref[...] = stage_ref[...].astype(bf16)

    col = lambda off: pl.BlockSpec((t, LANES), lambda j: (0, j + off))
    return _pc(body, name="ev_b", grid=(nb,),
               in_specs=[col(2 * nb), col(3 * nb), col(4 * nb), pl.BlockSpec((CONV_B_WIDTH, LANES), lambda j: (0, j)), HBM],
               out_specs=col(nb), out_shape=S(y.shape, bf16), input_output_aliases={4: 0},
               scratch_shapes=[pltpu.VMEM((t + CONV_LEAD, LANES), f32), pltpu.VMEM((t, LANES), f32)],
               compiler_params=_cparams(("arbitrary",)))(p, p, p, conv_b, y)


def _ev_b_bwd(dy, p, conv_b):
    t = p.shape[0]
    cr = _row_block8(t)
    nb = D_A // LANES

    def body(dy_ref, gb_ref, gc_ref, xi_ref, w_ref, dgb_ref, dgc_ref, dxi_ref, dw_ref, pad_ref, padb_ref):
        _fill_front_padded(pad_ref, gc_ref[...] * xi_ref[...], t)
        _fill_back_padded(padb_ref, dy_ref[...] * gb_ref[...], t)
        dw = None
        for r in range(t // cr):
            rows = slice(r * cr, (r + 1) * cr)
            dgb_ref[rows, :] = dy_ref[rows, :] * _conv_rows(pad_ref, w_ref, CONV_B_WIDTH, r * cr, cr)
            dcx = _conv_t_rows(padb_ref, w_ref, CONV_B_WIDTH, r * cr, cr)
            dgc_ref[rows, :] = dcx * xi_ref[rows, :]
            dxi_ref[rows, :] = dcx * gc_ref[rows, :]
            dw = _acc_list(dw, _conv_dw_rows(padb_ref[rows, :], pad_ref, CONV_B_WIDTH, r * cr, cr))
        for j in range(CONV_B_WIDTH):
            dw_ref[j:j + 1, :] = dw[j]

    col = lambda off: pl.BlockSpec((t, LANES), lambda j: (0, j + off))
    wsp = pl.BlockSpec((CONV_B_WIDTH, LANES), lambda j: (0, j))
    return _pc(body, name="ev_b_bwd", grid=(nb,), in_specs=[col(nb), col(2 * nb), col(3 * nb), col(4 * nb), wsp],
               out_specs=(col(0), col(0), col(0), wsp),
               out_shape=(S((t, D_A), f32), S((t, D_A), f32), S((t, D_A), f32), S((CONV_B_WIDTH, D_A), f32)),
               scratch_shapes=[pltpu.VMEM((t + CONV_LEAD, LANES), f32), pltpu.VMEM((t + CONV_LEAD, LANES), f32)],
               compiler_params=_cparams(("arbitrary",)))(dy, p, p, p, conv_b)


def _ffn_mid(u, conv_w, conv_b, name):
    t = u.shape[0]
    cr = _row_block8(t)
    nb = D_FF // FF_BLOCK

    def one(gt_ref, vl_ref, w_ref, b_ref, o_ref, pad_ref, stage_ref):
        _fill_front_padded(pad_ref, gt_ref[...], t)
        for r in range(t // cr):
            rows = slice(r * cr, (r + 1) * cr)
            gc = _conv_rows(pad_ref, w_ref, FF_CONV_WIDTH, r * cr, cr) + b_ref[...]
            stage_ref[rows, :] = gc * _sigmoid(gc) * vl_ref[rows, :]
        o_ref[...] = stage_ref[...].astype(bf16)

    def body(*refs):
        for h in range(FF_BLOCK // LANES):
            one(*[r.at[:, pl.ds(h * LANES, LANES)] for r in refs[:5]], *refs[5:])

    col = lambda off: pl.BlockSpec((t, FF_BLOCK), lambda j: (0, j + off))
    return _pc(body, name=name, grid=(nb,),
               in_specs=[col(0), col(nb), pl.BlockSpec((FF_CONV_WIDTH, FF_BLOCK), lambda j: (0, j)),
                         pl.BlockSpec((1, FF_BLOCK), lambda j: (0, j))],
               out_specs=col(0), out_shape=S((t, D_FF), bf16),
               scratch_shapes=[pltpu.VMEM((t + CONV_LEAD, LANES), f32), pltpu.VMEM((t, LANES), f32)],
               compiler_params=_cparams(("arbitrary",)))(u, u, conv_w, conv_b.reshape(1, D_FF))


def _ffn_mid_bwd(dz, u, conv_w, conv_b, name):
    t = u.shape[0]
    cr = _row_block8(t)
    nb = D_FF // FF_BLOCK
    nh = FF_BLOCK // LANES

    def body(*refs):
        for h in range(nh):
            one(*[r.at[:, pl.ds(h * LANES, LANES)] for r in refs[:9]], *refs[9:])

    def one(dz_ref, gt_ref, vl_ref, w_ref, b_ref, du_ref, dv_ref, dw_ref, db_ref, pad_ref, padb_ref, stage_ref):
        _fill_front_padded(pad_ref, gt_ref[...], t)
        dw, db = None, None
        for r in range(t // cr):
            rows = slice(r * cr, (r + 1) * cr)
            lo = CONV_LEAD + r * cr - (FF_CONV_WIDTH - 1)
            taps = [pad_ref[lo + j:lo + j + cr, :] for j in range(FF_CONV_WIDTH)]
            gc = sum(w_ref[j:j + 1, :] * taps[j] for j in range(FF_CONV_WIDTH)) + b_ref[...]
            sg = _sigmoid(gc)
            dzr = dz_ref[rows, :]
            stage_ref[rows, :] = dzr * gc * sg
            dgc = dzr * vl_ref[rows, :] * sg * (1.0 + gc * (1.0 - sg))
            padb_ref[rows, :] = dgc
            dw = _acc_list(dw, [jnp.sum(dgc * tap, axis=0, keepdims=True) for tap in taps])
            pb = jnp.sum(dgc, axis=0, keepdims=True)
            db = pb if db is None else db + pb
        padb_ref[t:t + CONV_LEAD, :] = jnp.zeros((CONV_LEAD, LANES), f32)
        for r in range(t // cr):
            pad_ref[r * cr:(r + 1) * cr, :] = _conv_t_rows(padb_ref, w_ref, FF_CONV_WIDTH, r * cr, cr)
        du_ref[...] = pad_ref[0:t, :].astype(du_ref.dtype)
        dv_ref[...] = stage_ref[...].astype(dv_ref.dtype)
        for j in range(FF_CONV_WIDTH):
            dw_ref[j:j + 1, :] = dw[j]
        db_ref[...] = db

    col = lambda off: pl.BlockSpec((t, FF_BLOCK), lambda j: (0, j + off))
    wsp = pl.BlockSpec((FF_CONV_WIDTH, FF_BLOCK), lambda j: (0, j))
    bsp = pl.BlockSpec((1, FF_BLOCK), lambda j: (0, j))
    return _pc(body, name=name, grid=(nb,), in_specs=[col(0), col(0), col(nb), wsp, bsp],
               out_specs=(col(0), col(0), wsp, bsp),
               out_shape=(S((t, D_FF), bf16), S((t, D_FF), bf16), S((FF_CONV_WIDTH, D_FF), f32), S((1, D_FF), f32)),
               scratch_shapes=[pltpu.VMEM((t + CONV_LEAD, LANES), f32), pltpu.VMEM((t + CONV_LEAD, LANES), f32),
                               pltpu.VMEM((t, LANES), f32)],
               compiler_params=_cparams(("arbitrary",)))(dz, u, u, conv_w, conv_b.reshape(1, D_FF))


def _swap_halves(x):
    w = x.shape[1]
    lane = lax.broadcasted_iota(jnp.int32, x.shape, 1) % HEAD_DIM
    return jnp.where(lane < HEAD_DIM // 2, pltpu.roll(x, w - HEAD_DIM // 2, axis=1), pltpu.roll(x, HEAD_DIM // 2, axis=1))


def _rope_pack(patt, c64, s64):
    t = patt.shape[0]
    tp = t + ATT_PAD

    def body(p_ref, c_ref, s_ref, q_ref, k_ref, v_ref):
        c, s = c_ref[...], s_ref[...]

        def rope(x, nh):
            cc = jnp.concatenate([c] * nh, axis=1)
            ss = jnp.concatenate([s] * nh, axis=1)
            return x * cc + _swap_halves(x) * ss

        for ref, val in ((q_ref, rope(p_ref[:, 0:D_ATT], N_Q_HEADS)),
                         (k_ref, rope(p_ref[:, D_ATT:D_ATT + D_KV], N_KV_HEADS)),
                         (v_ref, p_ref[:, D_ATT + D_KV:ATT_COLS])):
            ref[0:ATT_PAD, :] = jnp.zeros((ATT_PAD, val.shape[1]), bf16)
            ref[ATT_PAD:tp, :] = val.astype(bf16)

    return _pc(body, name="rope_pack", in_specs=[_full((t, ATT_COLS)), _full((t, HEAD_DIM)), _full((t, HEAD_DIM))],
               out_specs=(_full((tp, D_ATT)), _full((tp, D_KV)), _full((tp, D_KV))), grid=(1,),
               out_shape=(S((tp, D_ATT), bf16), S((tp, D_KV), bf16), S((tp, D_KV), bf16)),
               compiler_params=_cparams(("arbitrary",)))(patt, c64, s64)


def _rope_bwd(dqp, dkp, dvp, c64, s64):
    tp = dqp.shape[0]
    t = tp - ATT_PAD

    def body(dq_ref, dk_ref, dv_ref, c_ref, s_ref, o_ref):
        c, s = c_ref[...], s_ref[...]

        def unrope(dy, nh):
            cc = jnp.concatenate([c] * nh, axis=1)
            ss = jnp.concatenate([s] * nh, axis=1)
            return dy * cc + _swap_halves(dy * ss)

        o_ref[:, 0:D_ATT] = unrope(dq_ref[ATT_PAD:tp, :], N_Q_HEADS).astype(bf16)
        o_ref[:, D_ATT:D_ATT + D_KV] = unrope(dk_ref[ATT_PAD:tp, :], N_KV_HEADS).astype(bf16)
        o_ref[:, D_ATT + D_KV:ATT_COLS] = dv_ref[ATT_PAD:tp, :].astype(bf16)

    return _pc(body, name="rope_bwd", grid=(1,),
               in_specs=[_full((tp, D_ATT)), _full((tp, D_KV)), _full((tp, D_KV)), _full((t, HEAD_DIM)), _full((t, HEAD_DIM))],
               out_specs=_full((t, ATT_COLS)), out_shape=S((t, ATT_COLS), bf16),
               compiler_params=_cparams(("arbitrary",)))(dqp, dkp, dvp, c64, s64)


def _attn_masks(n):
    rows = GQA_GROUP * BLOCK
    ri = lax.broadcasted_iota(jnp.int32, (rows, BLOCK), 0) % BLOCK
    ci = lax.broadcasted_iota(jnp.int32, (rows, BLOCK), 1)
    m_cur = (ci <= ri) & (ci >= jnp.where(n >= 1, 0, ATT_PAD))
    m_prev = ci > ri + jnp.where(n >= 2, 0, BLOCK)
    m_meta = ci >= jnp.where(n >= 1, ATT_PAD, BLOCK)
    return m_cur, m_prev, m_meta


def _attn_probs(qg, kc, kp, km, masks, skv):
    def scores(k, m):
        s = lax.dot_general(qg, k, _DIMS["nt"], preferred_element_type=f32) * ATT_SCALE
        return jnp.where(m, s, NEG_INF)
    s_c, s_p, s_m = scores(kc, masks[0]), scores(kp, masks[1]), scores(km, masks[2])
    mx = jnp.maximum(jnp.maximum(jnp.max(s_c, axis=-1, keepdims=True), jnp.max(s_p, axis=-1, keepdims=True)),
                     jnp.maximum(jnp.max(s_m, axis=-1, keepdims=True), skv))
    e_c, e_p, e_m, e_s = jnp.exp(s_c - mx), jnp.exp(s_p - mx), jnp.exp(s_m - mx), jnp.exp(skv - mx)
    den = (jnp.sum(e_c, axis=-1, keepdims=True) + jnp.sum(e_p, axis=-1, keepdims=True)
           + jnp.sum(e_m, axis=-1, keepdims=True) + e_s)
    inv = 1.0 / den
    return e_c * inv, e_p * inv, e_m * inv, e_s * inv


def _sink_rows(sk_ref, g):
    hrow = lax.broadcasted_iota(jnp.int32, (GQA_GROUP * BLOCK, 1), 0) // BLOCK
    skv = jnp.zeros((GQA_GROUP * BLOCK, 1), f32)
    for hh in range(GQA_GROUP):
        skv = jnp.where(hrow == hh, sk_ref[0, GQA_GROUP * g + hh], skv)
    return skv, hrow


def _stack_heads(ref, g):
    return jnp.concatenate([ref[:, (GQA_GROUP * g + hh) * HEAD_DIM:(GQA_GROUP * g + hh + 1) * HEAD_DIM]
                            for hh in range(GQA_GROUP)], axis=0)


def _attn_specs():
    blk = lambda w: pl.BlockSpec((BLOCK, w), lambda n: (n, 0))
    prev = pl.BlockSpec((BLOCK, D_KV), lambda n: (jnp.maximum(n - 1, 0), 0))
    meta = pl.BlockSpec((BLOCK, D_KV), lambda n: (0, 0))
    return blk, prev, meta


def _attn_fwd(qp, kp, vp, sinks):
    tp = qp.shape[0]
    blk, prev, meta = _attn_specs()

    def body(sk_ref, q_ref, kc_ref, kp_ref, km_ref, vc_ref, vp_ref, vm_ref, o_ref):
        masks = _attn_masks(pl.program_id(0))
        for g in range(N_KV_HEADS):
            sl = slice(g * HEAD_DIM, (g + 1) * HEAD_DIM)
            skv, _ = _sink_rows(sk_ref, g)
            p_c, p_p, p_m, _ = _attn_probs(_stack_heads(q_ref, g), kc_ref[:, sl], kp_ref[:, sl], km_ref[:, sl], masks, skv)
            o = (jnp.dot(p_c.astype(bf16), vc_ref[:, sl], preferred_element_type=f32)
                 + jnp.dot(p_p.astype(bf16), vp_ref[:, sl], preferred_element_type=f32)
                 + jnp.dot(p_m.astype(bf16), vm_ref[:, sl], preferred_element_type=f32))
            for hh in range(GQA_GROUP):
                h = GQA_GROUP * g + hh
                o_ref[:, h * HEAD_DIM:(h + 1) * HEAD_DIM] = o[hh * BLOCK:(hh + 1) * BLOCK].astype(bf16)

    return _pc(body, name="attn_fwd", grid=(tp // BLOCK,),
               in_specs=[pl.BlockSpec(memory_space=pltpu.SMEM), blk(D_ATT), blk(D_KV), prev, meta, blk(D_KV), prev, meta],
               out_specs=blk(D_ATT), out_shape=S((tp, D_ATT), bf16),
               compiler_params=_cparams(("arbitrary",)))(sinks, qp, kp, kp, kp, vp, vp, vp)


def _attn_bwd(qp, kp, vp, sinks, dop):
    tp = qp.shape[0]
    blk, prev, meta = _attn_specs()

    def body(sk_ref, q_ref, kc_ref, kp_ref, km_ref, vc_ref, vp_ref, vm_ref, do_ref, dq_ref, dk_ref, dv_ref, dsk_ref):
        n = pl.program_id(0)

        @pl.when(n == 0)
        def _():
            dk_ref[...] = jnp.zeros_like(dk_ref)
            dv_ref[...] = jnp.zeros_like(dv_ref)
            dsk_ref[...] = jnp.zeros_like(dsk_ref)
        masks = _attn_masks(n)
        cur = pl.ds(pl.multiple_of(n * BLOCK, BLOCK), BLOCK)
        prv = pl.ds(pl.multiple_of(jnp.maximum(n - 1, 0) * BLOCK, BLOCK), BLOCK)
        lane = lax.broadcasted_iota(jnp.int32, (1, LANES), 1)
        dsk = jnp.zeros((1, LANES), f32)
        for g in range(N_KV_HEADS):
            sl = slice(g * HEAD_DIM, (g + 1) * HEAD_DIM)
            skv, hrow = _sink_rows(sk_ref, g)
            qg = _stack_heads(q_ref, g)
            dog = _stack_heads(do_ref, g)
            ks = (kc_ref[:, sl], kp_ref[:, sl], km_ref[:, sl])
            vs = (vc_ref[:, sl], vp_ref[:, sl], vm_ref[:, sl])
            probs = _attn_probs(qg, ks[0], ks[1], ks[2], masks, skv)
            dps = [lax.dot_general(dog, v, _DIMS["nt"], preferred_element_type=f32) for v in vs]
            delta = sum(jnp.sum(p * dp, axis=-1, keepdims=True) for p, dp in zip(probs[:3], dps))
            dss = [(p * (dp - delta) * ATT_SCALE).astype(bf16) for p, dp in zip(probs[:3], dps)]
            dq = sum(jnp.dot(ds, k, preferred_element_type=f32) for ds, k in zip(dss, ks))
            for hh in range(GQA_GROUP):
                h = GQA_GROUP * g + hh
                dq_ref[:, h * HEAD_DIM:(h + 1) * HEAD_DIM] = dq[hh * BLOCK:(hh + 1) * BLOCK]
                dsk = dsk + jnp.where(lane == h, -jnp.sum(jnp.where(hrow == hh, probs[3] * delta, 0.0)), 0.0)
            for rows, p, ds in zip((cur, prv, slice(0, BLOCK)), probs[:3], dss):
                dv_ref[rows, sl] += lax.dot_general(p.astype(bf16), dog, _DIMS["tn"], preferred_element_type=f32)
                dk_ref[rows, sl] += lax.dot_general(ds, qg, _DIMS["tn"], preferred_element_type=f32)
        dsk_ref[...] += dsk

    return _pc(body, name="attn_bwd", grid=(tp // BLOCK,),
               in_specs=[pl.BlockSpec(memory_space=pltpu.SMEM), blk(D_ATT), blk(D_KV), prev, meta, blk(D_KV), prev, meta,
                         blk(D_ATT)],
               out_specs=(blk(D_ATT), _full((tp, D_KV)), _full((tp, D_KV)), _full((1, LANES))),
               out_shape=(S((tp, D_ATT), f32), S((tp, D_KV), f32), S((tp, D_KV), f32), S((1, LANES), f32)),
               compiler_params=_cparams(("arbitrary",)))(sinks, qp, kp, kp, kp, vp, vp, vp, dop)


def _seg(x, bm):
    hi = x.astype(bf16)
    lo = (x - hi.astype(f32)).astype(bf16)
    return jnp.dot(jnp.concatenate([hi, lo], axis=1), bm, preferred_element_type=f32)


@jax.custom_vjp
def _seg_linear(x, bm):
    return _seg(x, bm)


_seg_linear.defvjp(lambda x, bm: (_seg(x, bm), bm), lambda bm, ct: (_seg(ct, bm), jnp.zeros_like(bm)))


def _softplus(y):
    return jnp.maximum(y, 0.0) + jnp.log(1.0 + jnp.exp(-jnp.abs(y)))


def _prep_fn(xr, xk, xwd, xad, xgd, w0, w2, a0, a2, g2, k_k, k_a, bm, seg=_seg):
    xw = w0 + jnp.dot(jnp.tanh(xwd), w2, preferred_element_type=f32)
    decay = jnp.exp(-jnp.exp(-_softplus(-xw) - 0.5))
    alpha = _sigmoid(a0 + jnp.dot(xad, a2, preferred_element_type=f32))
    g = jnp.dot(_sigmoid(xgd), g2, preferred_element_type=f32)
    kk = xk * k_k
    kkn = kk / jnp.maximum(jnp.sqrt(seg(kk * kk, bm)), 1e-12)
    k2 = xk * (1.0 + (alpha - 1.0) * k_a)
    return decay, k2, -kkn, kkn * alpha, g


def _split_cols(x):
    o1, o2, o3 = 3 * D_R, 3 * D_R + LORA_W, 3 * D_R + LORA_W + LORA_A
    return x[:, 0:D_R], x[:, D_R:2 * D_R], x[:, 2 * D_R:o1], x[:, o1:o2], x[:, o2:o3], x[:, o3:RWKV_COLS]


def _shifted(sh_ref, x, halo, first, rb):
    sh_ref[0:SUBLANES, :] = jnp.where(first, 0.0, halo)
    sh_ref[SUBLANES:SUBLANES + rb, :] = x
    return sh_ref[SUBLANES - 1:SUBLANES - 1 + rb, :]


_PREP_PARAMS = ("od_w0", "od_w2", "od_a0", "od_a2", "od_g2", "od_k_k", "od_k_a")


def _rwkv_prep(pr, mu, params, bm):
    t = pr.shape[0]
    rb = _row_block8(t)
    hb = rb // SUBLANES

    def body(pr_ref, halo_ref, mu_ref, w0, w2, a0, a2, g2, kk_ref, ka_ref, bm_ref, *outs_sh):
        outs, sh_ref = outs_sh[:-1], outs_sh[-1]
        x = pr_ref[...]
        prev = _shifted(sh_ref, x, halo_ref[...], pl.program_id(0) == 0, rb)
        xr, xk, xv, xwd, xad, xgd = _split_cols(x + (prev - x) * mu_ref[...])
        bmv = bm_ref[...]
        decay, k2, a_s, b_s, g = _prep_fn(xr, xk, xwd, xad, xgd, w0[...], w2[...], a0[...], a2[...], g2[...],
                                          kk_ref[...], ka_ref[...], bmv)
        vals = (xr, xv, decay, k2, a_s, b_s, decay * xr, _seg(b_s * xr, bmv), _seg(k2 * xr, bmv), g)
        for ref, val in zip(outs, vals):
            ref[...] = val

    row = pl.BlockSpec((rb, RWKV_COLS), lambda i: (i, 0))
    halo = pl.BlockSpec((SUBLANES, RWKV_COLS), lambda i: (jnp.maximum(i * hb - 1, 0), 0))
    orow = pl.BlockSpec((rb, D_R), lambda i: (i, 0))
    return _pc(body, name="rwkv_prep", grid=(t // rb,),
               in_specs=[row, halo, _full((1, RWKV_COLS))] + [_full(p.shape) for p in params] + [_full(bm.shape)],
               out_specs=(orow,) * 10, out_shape=(S((t, D_R), f32),) * 10,
               scratch_shapes=[pltpu.VMEM((rb + SUBLANES, RWKV_COLS), f32)],
               compiler_params=_cparams(("arbitrary",)))(pr, pr, mu, *params, bm)


def _rwkv_prep_bwd(pr, mu, params, bm, cts):
    t = pr.shape[0]
    rb = _row_block8(t)
    hb = rb // SUBLANES
    counts = [len(c) for c in cts]
    flat = [a for c in cts for a in c]

    def body(pr_ref, halo_ref, mu_ref, w0, w2, a0, a2, g2, kk_ref, ka_ref, bm_ref, *rest):
        ct_refs, rest = rest[:len(flat)], rest[len(flat):]
        dx_ref, dmu_ref = rest[0], rest[1]
        dpar_refs, sh_ref = rest[2:9], rest[9]

        @pl.when(pl.program_id(0) == 0)
        def _():
            dmu_ref[...] = jnp.zeros_like(dmu_ref)
            for r in dpar_refs:
                r[...] = jnp.zeros_like(r)
        sums, pos = [], 0
        for c in counts:
            sums.append(sum(r[...] for r in ct_refs[pos:pos + c]))
            pos += c
        x = pr_ref[...]
        prev = _shifted(sh_ref, x, halo_ref[...], pl.program_id(0) == 0, rb)
        xr, xk, xv, xwd, xad, xgd = _split_cols(x + (prev - x) * mu_ref[...])
        bmv = bm_ref[...]
        _, vjp = jax.vjp(lambda *a: _prep_fn(*a, bmv, _seg_linear), xr, xk, xwd, xad, xgd, w0[...], w2[...], a0[...], a2[...],
                         g2[...], kk_ref[...], ka_ref[...])
        grads = vjp(tuple(sums[:5]))
        dxr, dxk, dxwd, dxad, dxgd = grads[:5]
        o1, o2, o3 = 3 * D_R, 3 * D_R + LORA_W, 3 * D_R + LORA_W + LORA_A
        dx_ref[:, 0:D_R] = dxr + sums[5]
        dx_ref[:, D_R:2 * D_R] = dxk
        dx_ref[:, 2 * D_R:o1] = sums[6]
        dx_ref[:, o1:o2] = dxwd
        dx_ref[:, o2:o3] = dxad
        dx_ref[:, o3:RWKV_COLS] = dxgd
        dmu_ref[...] += jnp.sum(dx_ref[...] * (prev - x), axis=0, keepdims=True)
        for r, gval in zip(dpar_refs, grads[5:]):
            r[...] += gval

    row = pl.BlockSpec((rb, RWKV_COLS), lambda i: (i, 0))
    halo = pl.BlockSpec((SUBLANES, RWKV_COLS), lambda i: (jnp.maximum(i * hb - 1, 0), 0))
    crow = pl.BlockSpec((rb, D_R), lambda i: (i, 0))
    return _pc(body, name="rwkv_prep_bwd", grid=(t // rb,),
               in_specs=[row, halo, _full((1, RWKV_COLS))] + [_full(p.shape) for p in params] + [_full(bm.shape)]
               + [crow] * len(flat),
               out_specs=(row, _full((1, RWKV_COLS))) + tuple(_full(p.shape) for p in params),
               out_shape=(S((t, RWKV_COLS), f32), S((1, RWKV_COLS), f32)) + tuple(S(p.shape, f32) for p in params),
               scratch_shapes=[pltpu.VMEM((rb + SUBLANES, RWKV_COLS), f32)],
               compiler_params=_cparams(("arbitrary",)))(pr, pr, mu, *params, bm, *flat)


def _shift_bwd(dxs, mu):
    t = dxs.shape[0]
    rb = _row_block(t)
    hb = rb // SUBLANES
    nblk = t // rb

    def body(dx_ref, halo_ref, mu_ref, o_ref, sh_ref):
        dx = dx_ref[...]
        sh_ref[0:rb, :] = dx
        sh_ref[rb:rb + SUBLANES, :] = jnp.where(pl.program_id(0) == nblk - 1, 0.0, halo_ref[...])
        m = mu_ref[...]
        o_ref[...] = (dx * (1.0 - m) + sh_ref[1:1 + rb, :] * m).astype(bf16)

    row = pl.BlockSpec((rb, RWKV_COLS), lambda i: (i, 0))
    halo = pl.BlockSpec((SUBLANES, RWKV_COLS), lambda i: (jnp.minimum((i + 1) * hb, t // SUBLANES - 1), 0))
    return _pc(body, name="rwkv_shift_bwd", grid=(nblk,), in_specs=[row, halo, _full((1, RWKV_COLS))],
               out_specs=row, out_shape=S((t, RWKV_COLS), bf16),
               scratch_shapes=[pltpu.VMEM((rb + SUBLANES, RWKV_COLS), f32)],
               compiler_params=_cparams(("arbitrary",)))(dxs, dxs, mu)


def _post_fn(y, xr, k2, xv, g, lg, lb, rk, bm, seg=_seg):
    inv_n = 1.0 / HEAD_DIM
    yc = y - seg(y, bm) * inv_n
    var = seg(yc * yc, bm) * inv_n
    yn = yc * lax.rsqrt(var + RWKV_GN_EPS) * lg + lb
    return (yn + seg(xr * k2 * rk, bm) * xv) * g


def _rwkv_post(y, xr, k2, xv, g, lg, lb, rk, bm):
    t = y.shape[0]
    rb = _row_block8(t)

    def body(y_ref, xr_ref, k2_ref, xv_ref, g_ref, lg_ref, lb_ref, rk_ref, bm_ref, o_ref):
        o_ref[...] = _post_fn(y_ref[...], xr_ref[...], k2_ref[...], xv_ref[...], g_ref[...], lg_ref[...], lb_ref[...],
                              rk_ref[...], bm_ref[...])

    row = pl.BlockSpec((rb, D_R), lambda i: (i, 0))
    vec = _full((1, D_R))
    return _pc(body, name="rwkv_post", grid=(t // rb,), in_specs=[row] * 5 + [vec] * 3 + [_full(bm.shape)],
               out_specs=row, out_shape=S((t, D_R), f32),
               compiler_params=_cparams(("arbitrary",)))(y, xr, k2, xv, g, lg, lb, rk, bm)


def _rwkv_post_bwd(dy1, y, xr, k2, xv, g, lg, lb, rk, bm):
    t = y.shape[0]
    rb = _row_block8(t)

    def body(dy_ref, y_ref, xr_ref, k2_ref, xv_ref, g_ref, lg_ref, lb_ref, rk_ref, bm_ref, *outs):
        @pl.when(pl.program_id(0) == 0)
        def _():
            for r in outs[5:]:
                r[...] = jnp.zeros_like(r)
        bmv = bm_ref[...]
        _, vjp = jax.vjp(lambda *a: _post_fn(*a, bmv, _seg_linear), y_ref[...], xr_ref[...], k2_ref[...], xv_ref[...], g_ref[...],
                         lg_ref[...], lb_ref[...], rk_ref[...])
        grads = vjp(dy_ref[...])
        for r, gval in zip(outs[:5], grads[:5]):
            r[...] = gval
        for r, gval in zip(outs[5:], grads[5:]):
            r[...] += gval

    row = pl.BlockSpec((rb, D_R), lambda i: (i, 0))
    vec = _full((1, D_R))
    return _pc(body, name="rwkv_post_bwd", grid=(t // rb,),
               in_specs=[pl.BlockSpec((rb, D_R), lambda i: (i, 1))] + [row] * 5 + [vec] * 3 + [_full(bm.shape)],
               out_specs=(row,) * 5 + (vec,) * 3, out_shape=(S((t, D_R), f32),) * 5 + (S((1, D_R), f32),) * 3,
               compiler_params=_cparams(("arbitrary",)))(dy1, y, xr, k2, xv, g, lg, lb, rk, bm)


def _row4(rows, j):
    return jnp.concatenate([jnp.broadcast_to(rows[j:j + 1, p * LANES:(p + 1) * LANES], (HEAD_DIM, LANES))
                            for p in range(4)], axis=0)


def _scan_consts():
    lane_group = jnp.arange(LANES) // HEAD_DIM
    b128 = (lane_group[:, None] == lane_group[None, :]).astype(bf16)
    bb = jnp.concatenate([b128, b128], axis=0)
    qsel = (jnp.arange(PAIR_ROWS)[:, None] % HEAD_DIM == jnp.arange(LANES)[None, :] % HEAD_DIM).astype(f32)
    return bb, qsel


def _store_cols(acc_ref, o_ref, tc):
    for p in range(4):
        blk = acc_ref[p * HEAD_DIM:(p + 1) * HEAD_DIM, :].T
        o_ref[:, (2 * p) * HEAD_DIM:(2 * p + 1) * HEAD_DIM] = blk[0:tc]
        o_ref[:, (2 * p + 1) * HEAD_DIM:(2 * p + 2) * HEAD_DIM] = blk[HEAD_DIM:HEAD_DIM + tc]


PAIR_GROUP = 2 * SUBLANES


def _rwkv_pairs(w, a, b, k, v, wr, br, kr, bm):
    t = w.shape[0]
    rb = _row_block8(t)

    def body(w_ref, a_ref, b_ref, k_ref, v_ref, wr_ref, br_ref, kr_ref, bm_ref, *outs_sh):
        outs, sh_ref = outs_sh[:-1], outs_sh[-1]

        def second(ref):
            sh_ref[0:rb, :] = ref[...]
            sh_ref[rb:rb + SUBLANES, :] = jnp.zeros((SUBLANES, D_R), f32)
            return sh_ref[1:1 + rb, :]

        w1, a1, b1, k1, v1 = w_ref[...], a_ref[...], b_ref[...], k_ref[...], v_ref[...]
        w2, a2, wr2, br2, kr2, v2 = (second(r) for r in (w_ref, a_ref, wr_ref, br_ref, kr_ref, v_ref))
        bmv = bm_ref[...]
        beta, kappa = _seg(b1 * a2, bmv), _seg(k1 * a2, bmv)
        bwr2, kwr2 = _seg(b1 * wr2, bmv), _seg(k1 * wr2, bmv)
        w1a2 = w1 * a2
        vals = (w1a2 + a1 * beta, v1 * kappa,
                wr_ref[...] + a1 * br_ref[...], v1 * kr_ref[...],
                w1 * wr2 + a1 * (bwr2 + beta * br2) + w1a2 * br2,
                v1 * (kwr2 + kappa * br2) + v2 * kr2,
                w1 * w2, b1 * w2, k1 * w2)
        for ref, val in zip(outs, vals):
            ref[...] = val

    row = pl.BlockSpec((rb, D_R), lambda i: (i, 0))
    return _pc(body, name="rwkv_pairs", grid=(t // rb,), in_specs=[row] * 8 + [_full(bm.shape)],
               out_specs=(row,) * 9, out_shape=(S((t, D_R), f32),) * 9,
               scratch_shapes=[pltpu.VMEM((rb + SUBLANES, D_R), f32)],
               compiler_params=_cparams(("arbitrary",)))(w, a, b, k, v, wr, br, kr, bm)


def _wkv_fwd(k, v, a, b, pairs):
    t = k.shape[0]
    tc = SCAN_CHUNK
    bb, qsel = _scan_consts()

    def body(*refs):
        k16, v16, a16, b16 = refs[0:4]
        ca2p, da2p, c1p, d1p, c2p, d2p, w12p, b1wp, k1wp = refs[4:13]
        bb_ref, q_ref, y_ref, st_ref, sa_ref, vb_ref, s_scr, yacc = refs[13:]

        @pl.when(pl.program_id(0) == 0)
        def _():
            s_scr[...] = jnp.zeros_like(s_scr)
        bbv, qp = bb_ref[...], q_ref[0:HEAD_DIM, :]
        lane = lax.broadcasted_iota(jnp.int32, (HEAD_DIM, LANES), 1) % HEAD_DIM

        def halves(x):
            hi = x.astype(bf16)
            return jnp.concatenate([hi, (x - hi.astype(f32)).astype(bf16)], axis=1)

        def group(gi, s):
            base = pl.multiple_of(gi * PAIR_GROUP, PAIR_GROUP)

            def rows8(ref, j):
                return ref[pl.ds(base + (j // SUBLANES) * SUBLANES, SUBLANES), :]

            def bcast(rows, j, p):
                return jnp.broadcast_to(rows[j % SUBLANES:j % SUBLANES + 1, p * LANES:(p + 1) * LANES], (HEAD_DIM, LANES))

            step = lambda ref, j, p: bcast(rows8(ref, j), j, p)
            for q in range(SUBLANES):
                j1, j2 = 2 * q, 2 * q + 1
                t1 = base + j1
                nxt = []
                for p in range(4):
                    sl = slice(p * HEAD_DIM, (p + 1) * HEAD_DIM)
                    sp = s[sl]
                    lhs = [halves(jnp.concatenate([sp * step(a16, j1, p),
                                                   sp * step(ca2p, j1, p) + qp * step(da2p, j1, p),
                                                   sp * step(c1p, j1, p) + qp * step(d1p, j1, p),
                                                   sp * step(c2p, j1, p) + qp * step(d2p, j1, p)], axis=0))]
                    for j in (j1, j2):
                        v8 = rows8(v16, j)
                        vh8 = v8.astype(bf16).astype(f32)
                        lhs.append(jnp.concatenate([(qp * bcast(vh8, j, p)).astype(bf16),
                                                    (qp * bcast(v8 - vh8, j, p)).astype(bf16)], axis=1))
                    r = jnp.dot(jnp.concatenate(lhs, axis=0), bbv, preferred_element_type=f32)
                    sa1, sa2, y1, y2, vb1, vb2 = (r[n * HEAD_DIM:(n + 1) * HEAD_DIM] for n in range(6))
                    yacc[sl, :] = jnp.where(lane == t1, y1, jnp.where(lane == t1 + 1, y2, yacc[sl, :]))
                    st_ref[base // 2 + q, sl, :] = sp
                    sa_ref[t1, sl, :] = sa1
                    sa_ref[t1 + 1, sl, :] = sa2
                    vb_ref[t1, sl, :] = vb1
                    vb_ref[t1 + 1, sl, :] = vb2
                    nxt.append(((sp * step(w12p, j1, p) + sa1 * step(b1wp, j1, p)) + vb1 * step(k1wp, j1, p))
                               + (sa2 * step(b16, j2, p) + vb2 * step(k16, j2, p)))
                s = jnp.concatenate(nxt, axis=0)
            return s

        s_scr[...] = lax.fori_loop(0, tc // PAIR_GROUP, group, s_scr[...])
        _store_cols(yacc, y_ref, tc)

    row = pl.BlockSpec((tc, D_R), lambda c: (c, 0))
    tiles = pl.BlockSpec((tc, PAIR_ROWS, LANES), lambda c: (c, 0, 0))
    return _pc(body, name="wkv_fwd", grid=(t // tc,),
               in_specs=[row] * 13 + [_full(bb.shape), _full(qsel.shape)],
               out_specs=(row, pl.BlockSpec((tc // 2, PAIR_ROWS, LANES), lambda c: (c, 0, 0)), tiles, tiles),
               out_shape=(S((t, D_R), f32), S((t // 2, PAIR_ROWS, LANES), f32)) + (S((t, PAIR_ROWS, LANES), f32),) * 2,
               scratch_shapes=[pltpu.VMEM((PAIR_ROWS, LANES), f32), pltpu.VMEM((PAIR_ROWS, LANES), f32)],
               compiler_params=_cparams(("arbitrary",)))(k, v, a, b, *pairs, bb, qsel)


def _wkv_bwd(sprev, sab, vbb, w, k, a, b, r, dy):
    t = w.shape[0]
    tc = SCAN_CHUNK
    nc = t // tc
    bb, qsel = _scan_consts()

    def body(st_ref, sa_ref, vb_ref, w_ref, k_ref, a_ref, b_ref, r_ref, dy_ref, bb_ref, q_ref,
             dr_ref, dw_ref, dk_ref, dv_ref, da_ref, db_ref, g_scr, dvacc, rows_scr):
        @pl.when(pl.program_id(0) == 0)
        def _():
            g_scr[...] = jnp.zeros_like(g_scr)
        bbv, qv = bb_ref[...], q_ref[...]
        lane64 = lax.broadcasted_iota(jnp.int32, (PAIR_ROWS, LANES), 1) % HEAD_DIM
        outs = (dr_ref, dw_ref, db_ref, dk_ref, da_ref)

        def colsums(slot, j, x):
            for p in range(4):
                rows_scr[slot, j:j + 1, p * LANES:(p + 1) * LANES] = jnp.sum(x[p * HEAD_DIM:(p + 1) * HEAD_DIM], axis=0,
                                                                           keepdims=True)

        def group(i, g):
            base = pl.multiple_of((tc // SUBLANES - 1 - i) * SUBLANES, SUBLANES)
            w8, k8, a8, b8, r8, dy8 = (ref[pl.ds(base, SUBLANES), :] for ref in (w_ref, k_ref, a_ref, b_ref, r_ref, dy_ref))

            def after_step(j, sp):
                return sp * _row4(w8, j) + sa_ref[base + j] * _row4(b8, j) + vb_ref[base + j] * _row4(k8, j)

            def back_step(j, sp, s_t, g):
                tt = base + j
                u, vb = sa_ref[tt], vb_ref[tt]
                a4, b4, w4, k4 = _row4(a8, j), _row4(b8, j), _row4(w8, j), _row4(k8, j)
                dyb = _seg(qv * _row4(dy8, j), bbv)
                g = g + dyb * _row4(r8, j)
                rr2 = _seg(jnp.concatenate([g * b4, g * k4], axis=0), bbv)
                du, dvb = rr2[0:PAIR_ROWS], rr2[PAIR_ROWS:2 * PAIR_ROWS]
                for slot, val in enumerate((s_t * dyb, g * sp, g * u, g * vb, sp * du)):
                    colsums(slot, j, val)
                dvacc[...] = jnp.where(lane64 == tt, dvb, dvacc[...])
                return g * w4 + du * a4

            for q in reversed(range(SUBLANES // 2)):
                s0 = st_ref[base // 2 + q]
                s1 = after_step(2 * q, s0)
                g = back_step(2 * q + 1, s1, after_step(2 * q + 1, s1), g)
                g = back_step(2 * q, s0, s1, g)
            for slot, ref in enumerate(outs):
                ref[pl.ds(base, SUBLANES), :] = rows_scr[slot]
            return g

        g_scr[...] = lax.fori_loop(0, tc // SUBLANES, group, g_scr[...])
        _store_cols(dvacc, dv_ref, tc)

    row = pl.BlockSpec((tc, D_R), lambda c: (nc - 1 - c, 0))
    tiles = pl.BlockSpec((tc, PAIR_ROWS, LANES), lambda c: (nc - 1 - c, 0, 0))
    states = pl.BlockSpec((tc // 2, PAIR_ROWS, LANES), lambda c: (nc - 1 - c, 0, 0))
    return _pc(body, name="wkv_bwd", grid=(nc,),
               in_specs=[states, tiles, tiles] + [row] * 6 + [_full(bb.shape), _full(qsel.shape)],
               out_specs=(row,) * 6, out_shape=(S((t, D_R), f32),) * 6,
               scratch_shapes=[pltpu.VMEM((PAIR_ROWS, LANES), f32), pltpu.VMEM((PAIR_ROWS, LANES), f32),
                               pltpu.VMEM((5, SUBLANES, D_R), f32)],
               compiler_params=_cparams(("arbitrary",)))(sprev, sab, vbb, w, k, a, b, r, dy, bb, qsel)


def _rope_tables(t):
    half = HEAD_DIM // 2
    inv = ROPE_THETA ** (-jnp.arange(half, dtype=f32) / half)
    ang = jnp.arange(t, dtype=f32)[:, None] * inv[None, :]
    cos, sin = jnp.cos(ang), jnp.sin(ang)
    return jnp.concatenate([cos, cos], axis=1), jnp.concatenate([-sin, sin], axis=1)


def _head_matrix():
    grp = jnp.arange(D_R) // HEAD_DIM
    b = (grp[:, None] == grp[None, :]).astype(bf16)
    return jnp.concatenate([b, b], axis=0)


def _ffn_fwd(h, hf, get_w, conv_w, conv_b, i, next_gain=None):
    w_up_t = get_w(f"ff{i}_up", hf)
    u = _mm(hf, w_up_t, "nt", f"ffn{i}_up")
    z = _ffn_mid(u, conv_w, conv_b, f"ffn{i}_mid")
    w_down = get_w(f"ff{i}_down", z)
    if next_gain is None:
        h_out, hn_out = _mm(z, w_down, "nn", f"ffn{i}_down", res=h), None
    else:
        h_out, hn_out = _mm_res_norm(z, w_down, h, next_gain, f"ffn{i}_down_norm")
    return h_out, hn_out, (hf, u, z), w_up_t, w_down


def _ffn_bwd(dh, h, saved, g, w_up_t, conv_w, conv_b, w_down, i, put_g):
    hf, u, z = saved
    dz = _mm(dh, w_down, "nt", f"ffn{i}_dz")
    g_down = _mm(z, dh, "tn", f"ffn{i}_gdown", out_dtype=GRAD_WIRE_DTYPE)
    tok = put_g(f"ff{i}_down", g_down)
    dgate, dval, g_conv, g_convb = _ffn_mid_bwd(dz, u, conv_w, conv_b + tok, f"ffn{i}_mid_bwd")
    g_up_t = _mm(dgate, hf, "tn", f"ffn{i}_gup_gate", out_dtype=GRAD_WIRE_DTYPE, out_rows=2 * D_FF)
    g_up_t = _mm(dval, hf, "tn", f"ffn{i}_gup_val", out_dtype=GRAD_WIRE_DTYPE, out_rows=2 * D_FF, out_row0=D_FF, into=g_up_t)
    tok = put_g(f"ff{i}_up", g_up_t)
    dh_in, g_norm = _mm_rms_bwd(dval, w_up_t, h, g + tok, dh, f"ffn{i}_dhf_val_norm_bwd", b_row0=D_FF,
                                res=_mm(dgate, w_up_t, "nn", f"ffn{i}_dhf_gate"))
    return dh_in, dict(conv=g_conv, conv_b=g_convb, norm=g_norm)


def _local_step(x, target, W, get_w, put_g, put_small, tok0):
    t = N_META + x.shape[0]
    c64, s64 = _rope_tables(t)
    bm = _head_matrix()
    h0 = jnp.concatenate([W["meta_tokens"], x], axis=0)

    ev_w_in_t, ev_w_out = get_w("ev_in", None), get_w("ev_out", None)
    hn0 = _rms_fwd(h0, W["norm_mix"][0] + tok0, "mix0_norm")
    p0 = _mm(hn0, ev_w_in_t, "nt", "ev_in")
    uc = _ev_a_conv(p0, W["ev_conv_a"])
    y0 = _ev_b(p0, W["ev_conv_b"], _ev_a_norm(uc, W["ev_ln_a_g"], W["ev_ln_a_b"]))
    h1, hf0 = _mm_res_norm(y0, ev_w_out, h0, W["norm_ffn"][0], "ev_out_norm")
    h2, hn1, ffn0, ff0_up_t, ff0_down = _ffn_fwd(h1, hf0, get_w, W["ff_conv"][0], W["ff_conv_b"][0], 0, W["norm_mix"][1])

    od_w_in_t = get_w("od_in", hn1)
    w_att, w_rwkv = od_w_in_t[:ATT_COLS], od_w_in_t[ATT_COLS:]
    pr = _mm(hn1, w_rwkv, "nt", "od_in_rwkv")
    qp, kp, vp = _rope_pack(_mm(hn1, w_att, "nt", "od_in_att"), c64, s64)
    op = _attn_fwd(qp, kp, vp, W["od_sinks"])
    prep_params = [W[n] for n in _PREP_PARAMS]
    xr, xv, decay, k2, a_s, b_s, wr, br, kr, gate = _rwkv_prep(pr, W["od_mu"], prep_params, bm)
    pairs = _rwkv_pairs(decay, a_s, b_s, k2, xv, wr, br, kr, bm)
    ysc, sprev, sab, vbb = _wkv_fwd(k2, xv, a_s, b_s, pairs)
    rk = W["od_r_k"].reshape(1, D_R)
    yr = _rwkv_post(ysc, xr, k2, xv, gate, W["od_lnx_g"], W["od_lnx_b"], rk, bm)
    y1 = jnp.concatenate([op[ATT_PAD:], yr.astype(bf16)], axis=1)
    od_w_out = get_w("od_out", y1)
    h3, hf1 = _mm_res_norm(y1, od_w_out, h2, W["norm_ffn"][1], "od_out_norm")
    h4, _, ffn1, ff1_up_t, ff1_down = _ffn_fwd(h3, hf1, get_w, W["ff_conv"][1], W["ff_conv_b"][1], 1)

    tgt = jnp.concatenate([jnp.zeros((N_META, D_MODEL), f32), target], axis=0)
    loss, dh4, g_norm_final = _final_loss(h4, W["norm_final"], tgt)

    dh3, gf1 = _ffn_bwd(dh4, h3, ffn1, W["norm_ffn"][1], ff1_up_t, W["ff_conv"][1], W["ff_conv_b"][1], ff1_down, 1, put_g)
    dy1 = _mm(dh3, od_w_out, "nt", "od_dy")
    g_od_w_out = _mm(y1, dh3, "tn", "od_gout", out_dtype=GRAD_WIRE_DTYPE)
    tok = put_g("od_out", g_od_w_out)
    dysc, dxr_p, dk2_p, dxv_p, dgate, g_lnx_g, g_lnx_b, g_rk = _rwkv_post_bwd(
        dy1, ysc, xr, k2, xv, gate, W["od_lnx_g"], W["od_lnx_b"] + tok, rk, bm)
    dr, dw, dk, dv, da, db = _wkv_bwd(sprev, sab, vbb, decay, k2, a_s, b_s, xr, dysc)
    prep_grads = _rwkv_prep_bwd(pr, W["od_mu"], prep_params, bm,
                                [[dw], [dk, dk2_p], [da], [db], [dgate], [dr, dxr_p], [dv, dxv_p]])
    dxs, g_mu = prep_grads[0], prep_grads[1]
    dpr = _shift_bwd(dxs, W["od_mu"])
    dop = jnp.concatenate([jnp.zeros((ATT_PAD, D_ATT), f32), dy1[:, :D_ATT]], axis=0).astype(bf16)
    dqp, dkp, dvp, dsk = _attn_bwd(qp, kp, vp, W["od_sinks"], dop)
    dpatt = _rope_bwd(dqp, dkp, dvp, c64, s64)
    n_in = ATT_COLS + RWKV_COLS
    g_od_w_in_t = _mm(dpatt, hn1, "tn", "od_gin_att", out_dtype=GRAD_WIRE_DTYPE, out_rows=n_in)
    g_od_w_in_t = _mm(dpr, hn1, "tn", "od_gin_rwkv", out_dtype=GRAD_WIRE_DTYPE, out_rows=n_in, out_row0=ATT_COLS, into=g_od_w_in_t)
    tok = put_g("od_in", g_od_w_in_t)
    dh2, g_norm_mix1 = _mm_rms_bwd(dpr, w_rwkv, h2, W["norm_mix"][1] + tok, dh3, "od_dhn_rwkv_norm_bwd",
                                   res=_mm(dpatt, w_att, "nn", "od_dhn_att"))

    dh1, gf0 = _ffn_bwd(dh2, h1, ffn0, W["norm_ffn"][0], ff0_up_t, W["ff_conv"][0], W["ff_conv_b"][0], ff0_down, 0, put_g)
    early = dict(
        norm_ffn=jnp.concatenate([gf0["norm"], gf1["norm"]], axis=0), norm_final=g_norm_final.reshape(D_MODEL),
        od_sinks=dsk[:, :N_Q_HEADS], od_mu=g_mu, od_lnx_g=g_lnx_g, od_lnx_b=g_lnx_b, od_r_k=g_rk.reshape(N_Q_HEADS, HEAD_DIM),
        ff_conv=jnp.stack([gf0["conv"], gf1["conv"]]), ff_conv_b=jnp.concatenate([gf0["conv_b"], gf1["conv_b"]], axis=0),
        **dict(zip(_PREP_PARAMS, prep_grads[2:])))
    dy0 = _mm(dh1, ev_w_out, "nt", "ev_dy")
    g_ev_w_out = _mm(y0, dh1, "tn", "ev_gout", out_dtype=GRAD_WIRE_DTYPE)
    tok = put_g("ev_out", g_ev_w_out) + put_small(early)
    duc, g_ln_g, g_ln_b = _ev_a_norm_bwd(dy0, uc, W["ev_ln_a_g"], W["ev_ln_a_b"] + tok)
    dav, dag, g_conv_a = _ev_a_conv_bwd(duc, p0, W["ev_conv_a"])
    dgb, dgc, dxi, g_conv_b = _ev_b_bwd(dy0, p0, W["ev_conv_b"])
    dp0 = jnp.concatenate([dav, dag, dgb, dgc, dxi], axis=1)
    g_ev_w_in_t = _mm(dp0, hn0, "tn", "ev_gin", out_dtype=GRAD_WIRE_DTYPE)
    tok = put_g("ev_in", g_ev_w_in_t)
    dh0, g_norm_mix0 = _mm_rms_bwd(dp0, ev_w_in_t, h0, W["norm_mix"][0] + tok, dh1, "ev_dhn_norm_bwd")

    late = dict(meta_tokens=dh0[:N_META], norm_mix=jnp.concatenate([g_norm_mix0, g_norm_mix1], axis=0),
                ev_conv_a=g_conv_a, ev_ln_a_g=g_ln_g, ev_ln_a_b=g_ln_b, ev_conv_b=g_conv_b)
    return loss, dh0[N_META:], late


HBM = pl.BlockSpec(memory_space=pl.ANY)


def _mesh_pos():
    return lax.axis_index("x"), lax.axis_index("y"), lax.axis_index("c")


def _dev(px, py, pc):
    return 4 * px + 2 * py + pc


def _all_gather(xs, name):
    n = len(xs)

    def body(*refs):
        x_refs, o_refs = refs[:n], refs[n:2 * n]
        send_sems, recv_sems, local_sems = refs[2 * n:]
        x, y, c = _mesh_pos()
        me, sibling = (x, y, c), (x, y, 1 - c)
        chips = [(1 - x, y), (x, 1 - y), (1 - x, 1 - y)]

        def copy(i, k, block, to, from_input=False):
            dst = o_refs[i].at[_dev(*block)]
            return pltpu.make_async_remote_copy(src_ref=x_refs[i] if from_input else dst, dst_ref=dst,
                                                send_sem=send_sems.at[i, k], recv_sem=recv_sems.at[i, k],
                                                device_id=to, device_id_type=MESH)

        mine = [pltpu.make_async_copy(x_refs[i], o_refs[i].at[_dev(*me)], local_sems.at[i]) for i in range(n)]
        for cp in mine:
            cp.start()
        first = []
        for i in range(n):
            first.append(copy(i, 0, me, sibling, True))
            first += [copy(i, 1 + j, me, (*chip, c), True) for j, chip in enumerate(chips)]
        for cp in first:
            cp.start()
        passed = []
        for j, chip in enumerate(chips):
            for i in range(n):
                copy(i, 1 + j, (*chip, c), me).wait_recv()
                fwd = copy(i, 4 + j, (*chip, c), sibling)
                fwd.start()
                passed.append(fwd)
        for i in range(n):
            copy(i, 0, sibling, me).wait_recv()
            for j, chip in enumerate(chips):
                copy(i, 4 + j, (*chip, 1 - c), me).wait_recv()
        for cp in first + passed:
            cp.wait_send()
        for cp in mine:
            cp.wait()

    return _pc(body, name=name, in_specs=[HBM] * n, out_specs=tuple([HBM] * n),
               out_shape=tuple(S((N_DEV,) + x.shape, x.dtype) for x in xs),
               scratch_shapes=[pltpu.SemaphoreType.DMA((n, 7)), pltpu.SemaphoreType.DMA((n, 7)),
                               pltpu.SemaphoreType.DMA((n,))])(*xs)


HBM_SPEC = pl.BlockSpec(memory_space=pltpu.HBM)
SEM_SPEC = pl.BlockSpec(memory_space=pltpu.SEMAPHORE)
DATAFLOW = pltpu.SideEffectType.DATAFLOW_SIDE_EFFECTING
_PEER_FLIPS = ((1, 0, 0), (0, 1, 0), (1, 1, 0), (1, 0, 1), (0, 1, 1), (1, 1, 1), (0, 0, 1))
N_PEERS = len(_PEER_FLIPS)


def _peers(x, y, c):
    return [((1 - x) if fx else x, (1 - y) if fy else y, (1 - c) if fc else c) for fx, fy, fc in _PEER_FLIPS]


def _xchg_start(srcs, lands, scatter, name):
    n = len(srcs)

    def body(*refs):
        src_refs, land_refs = refs[:n], refs[n:2 * n]
        send_sems, recv_sems, token = refs[2 * n], refs[2 * n + 1], refs[-1]
        x, y, c = _mesh_pos()
        me = _dev(x, y, c)
        for i in range(n):
            for k, peer in enumerate(_peers(x, y, c)):
                pltpu.make_async_remote_copy(src_ref=src_refs[i].at[_dev(*peer)] if scatter else src_refs[i],
                                             dst_ref=land_refs[i].at[me], send_sem=send_sems.at[i * N_PEERS + k],
                                             recv_sem=recv_sems.at[i * N_PEERS + k], device_id=peer, device_id_type=MESH).start()
        token[...] = jnp.zeros_like(token)

    arrs = list(srcs) + list(lands)
    outs = _pc(body, name=name,
               out_shape=(pltpu.SemaphoreType.DMA((n * N_PEERS,)), pltpu.SemaphoreType.DMA((n * N_PEERS,)),
                          *[pltpu.HBM(a.shape, a.dtype) for a in arrs], S((SUBLANES, LANES), f32)),
               in_specs=[HBM_SPEC] * (2 * n),
               out_specs=(SEM_SPEC, SEM_SPEC, *[HBM_SPEC] * (2 * n), pl.BlockSpec(memory_space=pltpu.VMEM)),
               input_output_aliases={i: 2 + i for i in range(2 * n)},
               compiler_params=pltpu.CompilerParams(has_side_effects=DATAFLOW))(
        *[pltpu.with_memory_space_constraint(a, pltpu.HBM) for a in arrs])
    return (outs[0], outs[1], list(outs[2:2 + n]), list(outs[2 + n:2 + 2 * n]), scatter), outs[-1]


def _xchg_wait(handle, after, name):
    send_sems, recv_sems, srcs, lands, scatter = handle
    n = len(srcs)

    def body(*refs):
        src_refs, land_refs = refs[:n], refs[n:2 * n]
        send, recv = refs[2 * n], refs[2 * n + 1]
        x, y, c = _mesh_pos()
        for i in range(n):
            for k in range(N_PEERS):
                cp = pltpu.make_async_remote_copy(src_ref=src_refs[i].at[0] if scatter else src_refs[i],
                                                  dst_ref=land_refs[i].at[0], send_sem=send.at[i * N_PEERS + k],
                                                  recv_sem=recv.at[i * N_PEERS + k],
                                                  device_id=(x, y, c), device_id_type=MESH)
                cp.wait_send()
                cp.wait_recv()

    arrs = srcs + lands
    outs = _pc(body, name=name, out_shape=tuple(pltpu.HBM(a.shape, a.dtype) for a in arrs),
               in_specs=[HBM_SPEC] * (2 * n) + [SEM_SPEC, SEM_SPEC, pl.BlockSpec(memory_space=pl.ANY)],
               out_specs=tuple([HBM_SPEC] * (2 * n)), input_output_aliases={i: i for i in range(2 * n)},
               compiler_params=pltpu.CompilerParams(has_side_effects=DATAFLOW))(*arrs, send_sems, recv_sems, after)
    return list(outs[:n]), list(outs[n:])


def _rs_sum(g, land, me_vec, name):
    _, r, cols = g.shape
    tr = _divisor_block(r, 16, min(r, 352))

    def body(me_ref, g_ref, *rest):
        o_ref = rest[-1]
        acc = g_ref[0].astype(f32)
        for l_ref in rest[:-1]:
            acc = acc + l_ref[0].astype(f32)
        o_ref[...] = acc

    blk = lambda f: pl.BlockSpec((1, tr, cols), f)
    grid_spec = pltpu.PrefetchScalarGridSpec(
        num_scalar_prefetch=1, grid=(r // tr,),
        in_specs=[blk(lambda i, me_ref: (me_ref[0], i, 0))]
        + [blk(lambda i, me_ref, k=k: ((me_ref[0] + k) % N_DEV, i, 0)) for k in range(1, N_DEV)],
        out_specs=pl.BlockSpec((tr, cols), lambda i, me_ref: (i, 0)))
    return _pc(body, name=name, grid_spec=grid_spec, out_shape=S((r, cols), f32),
               compiler_params=_cparams(("arbitrary",)))(me_vec, g, *([land] * (N_DEV - 1)))


def _rs_sum_adamw(g, land, me_vec, w, m, v, layer, into, name):
    _, r, cols = g.shape
    nl = w.shape[0]
    tr = _divisor_block(r, 16, min(r, 176))
    c1, c2 = 1.0 - ADAM_B1 ** ADAM_STEP, 1.0 - ADAM_B2 ** ADAM_STEP

    def body(me_ref, g_ref, *rest):
        lands, (w_ref, m_ref, v_ref) = rest[:N_DEV - 1], rest[N_DEV - 1:N_DEV + 2]
        go_ref, d_ref, nm_ref, nv_ref = rest[-4:]
        gv = g_ref[0].astype(f32)
        for l_ref in lands:
            gv = gv + l_ref[0].astype(f32)
        nm = ADAM_B1 * m_ref[0] + (1.0 - ADAM_B1) * gv
        nv = ADAM_B2 * v_ref[0] + (1.0 - ADAM_B2) * (gv * gv)
        d_ref[0] = -ADAM_LR * ((nm / c1) / (jnp.sqrt(nv / c2) + ADAM_EPS) + ADAM_WD * w_ref[0])
        go_ref[0] = gv
        nm_ref[0] = nm
        nv_ref[0] = nv

    blk = lambda f: pl.BlockSpec((1, tr, cols), f)
    lay = blk(lambda i, me_ref: (layer, i, 0))
    n_in = N_DEV + 3
    extra = [] if into is None else list(into)
    grid_spec = pltpu.PrefetchScalarGridSpec(
        num_scalar_prefetch=1, grid=(r // tr,),
        in_specs=[blk(lambda i, me_ref: (me_ref[0], i, 0))]
        + [blk(lambda i, me_ref, k=k: ((me_ref[0] + k) % N_DEV, i, 0)) for k in range(1, N_DEV)]
        + [lay] * 3 + [pl.BlockSpec(memory_space=pl.ANY)] * len(extra),
        out_specs=(lay,) * 4)
    return _pc(body if into is None else (lambda *refs: body(*refs[:n_in + 1], *refs[n_in + 1 + 4:])),
               name=name, grid_spec=grid_spec, out_shape=(S((nl, r, cols), f32),) * 4,
               input_output_aliases={} if into is None else {n_in + 1 + j: j for j in range(4)},
               compiler_params=_cparams(("arbitrary",)))(me_vec, g, *([land] * (N_DEV - 1)), w, m, v, *extra)


def _sum_devices(a, name):
    def body(a_ref, o_ref):
        acc = a_ref[0]
        for d in range(1, N_DEV):
            acc = acc + a_ref[d]
        o_ref[...] = acc

    return _pc(body, name=name, grid=(1,), in_specs=[_full(a.shape)], out_specs=_full(a.shape[1:]),
               out_shape=S(a.shape[1:], a.dtype), compiler_params=_cparams(("arbitrary",)))(a)


def _adamw(w, m, v, g, name):
    shape = w.shape
    w2, m2, v2, g2 = (a.reshape(-1, shape[-1]) for a in (w, m, v, g))
    rows, cols = w2.shape
    tr = rows if rows % SUBLANES else _divisor_block(rows, SUBLANES, max(SUBLANES, min(rows, ADAMW_BLOCK_ELEMS // cols)))
    c1, c2 = 1.0 - ADAM_B1 ** ADAM_STEP, 1.0 - ADAM_B2 ** ADAM_STEP

    def body(w_ref, m_ref, v_ref, g_ref, d_ref, nm_ref, nv_ref):
        gv = g_ref[...]
        nm = ADAM_B1 * m_ref[...] + (1.0 - ADAM_B1) * gv
        nv = ADAM_B2 * v_ref[...] + (1.0 - ADAM_B2) * (gv * gv)
        d_ref[...] = -ADAM_LR * ((nm / c1) / (jnp.sqrt(nv / c2) + ADAM_EPS) + ADAM_WD * w_ref[...])
        nm_ref[...] = nm
        nv_ref[...] = nv

    blk = pl.BlockSpec((tr, cols), lambda i: (i, 0))
    outs = _pc(body, name=name, grid=(rows // tr,), in_specs=[blk] * 4, out_specs=(blk,) * 3,
               out_shape=(S((rows, cols), f32),) * 3, compiler_params=_cparams(("arbitrary",)))(w2, m2, v2, g2)
    return tuple(o.reshape(shape) for o in outs)


_WEIGHTS = ("meta_tokens", "norm_mix", "norm_ffn", "norm_final", "ev_w_in", "ev_conv_a", "ev_ln_a_g", "ev_ln_a_b",
            "ev_conv_b", "ev_w_out", "od_w_in", "od_sinks", "od_mu", "od_w0", "od_w2", "od_a0", "od_a2", "od_g2",
            "od_k_k", "od_k_a", "od_r_k", "od_lnx_g", "od_lnx_b", "od_w_out", "ff_w_up", "ff_conv", "ff_conv_b", "ff_w_down")
_SMALL_SHARDED = (("meta_tokens", 1), ("ev_conv_a", 2), ("ev_conv_b", 2), ("od_mu", 1), ("od_w0", 1), ("od_w2", 2),
                  ("od_a0", 1), ("od_a2", 2), ("od_g2", 2), ("od_k_k", 1), ("od_k_a", 1), ("od_lnx_g", 1),
                  ("od_lnx_b", 1), ("ff_conv", 2))
_SMALL_REPLICATED = ("norm_mix", "norm_ffn", "norm_final", "ev_ln_a_g", "ev_ln_a_b", "od_sinks", "od_r_k", "ff_conv_b")
SLAB_UNIT = SUBLANES * LANES


def _pack(arrs):
    flat = jnp.concatenate([a.reshape(-1).astype(f32) for a in arrs])
    pad = (-flat.shape[0]) % SLAB_UNIT
    return jnp.pad(flat, (0, pad)).reshape(-1, LANES)


def _unpack(flat, shapes):
    out, off = [], 0
    for shp in shapes:
        size = 1
        for s in shp:
            size *= s
        out.append(flat[..., off:off + size].reshape(flat.shape[:-1] + tuple(shp)))
        off += size
    return out


def _full_shape(shape, axis):
    return tuple(N_DEV * s if i == axis else s for i, s in enumerate(shape))


def kernel(x, meta_tokens, norm_mix, norm_ffn, norm_final, ev_w_in, ev_conv_a, ev_ln_a_g, ev_ln_a_b, ev_conv_b, ev_w_out, od_w_in, od_sinks, od_mu, od_w0, od_w2, od_a0, od_a2, od_g2, od_k_k, od_k_a, od_r_k, od_lnx_g, od_lnx_b, od_w_out, ff_w_up, ff_conv, ff_conv_b, ff_w_down, loss_target, m_meta_tokens, m_norm_mix, m_norm_ffn, m_norm_final, m_ev_w_in, m_ev_conv_a, m_ev_ln_a_g, m_ev_ln_a_b, m_ev_conv_b, m_ev_w_out, m_od_w_in, m_od_sinks, m_od_mu, m_od_w0, m_od_w2, m_od_a0, m_od_a2, m_od_g2, m_od_k_k, m_od_k_a, m_od_r_k, m_od_lnx_g, m_od_lnx_b, m_od_w_out, m_ff_w_up, m_ff_conv, m_ff_conv_b, m_ff_w_down, v_meta_tokens, v_norm_mix, v_norm_ffn, v_norm_final, v_ev_w_in, v_ev_conv_a, v_ev_ln_a_g, v_ev_ln_a_b, v_ev_conv_b, v_ev_w_out, v_od_w_in, v_od_sinks, v_od_mu, v_od_w0, v_od_w2, v_od_a0, v_od_a2, v_od_g2, v_od_k_k, v_od_k_a, v_od_r_k, v_od_lnx_g, v_od_lnx_b, v_od_w_out, v_ff_w_up, v_ff_conv, v_ff_conv_b, v_ff_w_down):
    A = dict(locals())
    px, py, pc = _mesh_pos()
    me = _dev(px, py, pc)
    me_vec = jnp.reshape(me, (1,)).astype(jnp.int32)
    rows = lambda a: a.reshape(N_DEV * a.shape[1], a.shape[2])
    blocks = lambda a: a.reshape(N_DEV, a.shape[0] // N_DEV, a.shape[1])

    shards = dict(ev_in=ev_w_in[0].T, ev_out=ev_w_out[0], ff0_up=ff_w_up[0].T, ff0_down=ff_w_down[0], od_in=od_w_in[0].T,
                  od_out=od_w_out[0], ff1_up=ff_w_up[1].T, ff1_down=ff_w_down[1])
    shards = {n: b.astype(bf16) for n, b in shards.items()}
    small_shapes = [A[n].shape for n, _ in _SMALL_SHARDED]
    gathered = _all_gather([shards["ev_in"], shards["ev_out"], _pack([A[n] for n, _ in _SMALL_SHARDED])], "gather_first")
    gathered, shards = lax.optimization_barrier((gathered, shards))
    fetch, tok0 = {}, jnp.zeros((), f32)
    for n in ("ff0_up", "ff0_down", "od_in", "od_out", "ff1_up", "ff1_down"):
        shard, tok0 = lax.optimization_barrier((shards[n], tok0))
        land = lax.dynamic_update_slice(lax.empty((N_DEV,) + shard.shape, bf16), shard[None], (me, 0, 0))
        fetch[n], token = _xchg_start([shard], [land], False, f"gather_{n}_start")
        tok0 = tok0 + token[0, 0]

    def get_w(n, after):
        if n in ("ev_in", "ev_out"):
            return rows(gathered[("ev_in", "ev_out").index(n)])
        return rows(_xchg_wait(fetch[n], after, f"gather_{n}_wait")[1][0])

    W = {}
    for (n, ax), seg in zip(_SMALL_SHARDED, _unpack(gathered[-1].reshape(N_DEV, -1), small_shapes)):
        W[n] = jnp.moveaxis(seg, 0, ax).reshape(_full_shape(A[n].shape, ax))
    for n in ("ev_conv_a", "ev_conv_b", "od_w2", "od_a2", "od_g2"):
        W[n] = W[n][0]
    for n in _SMALL_REPLICATED:
        W[n] = A[n]
    W["od_r_k"] = od_r_k[0]

    small_shape = {n: _full_shape(A[n].shape, ax) for n, ax in _SMALL_SHARDED}
    small_shape.update({n: A[n].shape for n in _SMALL_REPLICATED})
    sent, small_sent, small_names = {}, {}, {}

    def put_g(n, g):
        g8 = blocks(g)
        sent[n], token = _xchg_start([g8], [lax.empty(g8.shape, g8.dtype)], True, f"reduce_{n}_start")
        return token[0, 0]

    def put_small(gs, stage="early"):
        small_names[stage] = sorted(gs)
        slab = _pack([gs[n] for n in small_names[stage]])
        land = lax.dynamic_update_slice(lax.empty((N_DEV,) + slab.shape, f32), slab[None], (me, 0, 0))
        small_sent[stage], small_tok[stage] = _xchg_start([slab], [land], False, f"gather_{stage}_small_grads_start")
        return small_tok[stage][0, 0]

    small_tok = {}
    loss_tile, grad_x, late = _local_step(x[0], loss_target[0], W, get_w, put_g, put_small, tok0)
    put_small(late, "late")
    late_tok = small_tok["late"]

    gsh, prev, fused = {}, late_tok, {}
    delta, new_m, new_v = {}, {}, {}
    row_sharded = dict(ff1_down=("ff_w_down", 1), od_out=("od_w_out", 0), ff0_down=("ff_w_down", 0), ev_out=("ev_w_out", 0))
    for n in ("ff1_down", "ff1_up", "od_out", "od_in", "ff0_down", "ff0_up", "ev_out", "ev_in"):
        srcs, lands = _xchg_wait(sent[n], prev, f"reduce_{n}_wait")
        if n in row_sharded:
            wn, layer = row_sharded[n]
            fused[wn] = _rs_sum_adamw(srcs[0], lands[0], me_vec, A[wn], A["m_" + wn], A["v_" + wn], layer, fused.get(wn),
                                      f"reduce_{n}_sum_adamw")
            prev = fused[wn][0]
        else:
            gsh[n] = prev = _rs_sum(srcs[0], lands[0], me_vec, f"reduce_{n}_sum")
    grads = dict(ev_w_in=gsh["ev_in"].T[None], od_w_in=gsh["od_in"].T[None],
                 ff_w_up=jnp.stack([gsh["ff0_up"].T, gsh["ff1_up"].T]))
    for wn, (g_sum, d, nm, nv) in fused.items():
        grads[wn], delta[wn], new_m[wn], new_v[wn] = g_sum, d, nm, nv

    for n in ("ff_w_up", "od_w_in", "ev_w_in"):
        delta[n], new_m[n], new_v[n] = _adamw(A[n], A["m_" + n], A["v_" + n], grads[n], "adamw_" + n)
    for stage in ("early", "late"):
        gsm = _xchg_wait(small_sent[stage], delta["ev_w_in"], f"gather_{stage}_small_grads_wait")[1][0]
        summed = _sum_devices(gsm, f"sum_{stage}_small_grads").reshape(-1)
        for n, full in zip(small_names[stage], _unpack(summed, [small_shape[n] for n in small_names[stage]])):
            grads[n] = full
    for n, ax in _SMALL_SHARDED:
        size = A[n].shape[ax]
        grads[n] = lax.dynamic_slice_in_dim(grads[n], me * size, size, axis=ax)
    small = list(small_shape)
    slab = lambda pre: _pack([A[pre + n] for n in small])
    upd = _adamw(slab(""), slab("m_"), slab("v_"), _pack([grads[n] for n in small]), "adamw_small")
    for arr, dst in zip(upd, (delta, new_m, new_v)):
        for n, seg in zip(small, _unpack(arr.reshape(-1), [A[n].shape for n in small])):
            dst[n] = seg

    loss = lax.psum(loss_tile[0, 0], ("x", "y", "c"))
    return (loss, grad_x[None], *[grads[n] for n in _WEIGHTS], *[delta[n] for n in _WEIGHTS],
            *[new_m[n] for n in _WEIGHTS], *[new_v[n] for n in _WEIGHTS])
```
